```python
import math
import jax, jax.numpy as jnp
from jax import lax
import numpy as np

D_MODEL = 1024
BATCH = 32
SEQ = 2048
DEPTH = 1

SSD_HEADS = 16
SSD_HEAD_DIM = 64
SSD_WIDTH = SSD_HEADS * SSD_HEAD_DIM
SSD_GROUPS = 2
SSD_STATE = 128
CONV_WIDTH = 4
CHUNK = 128
CONV_CH = SSD_WIDTH + 2 * SSD_GROUPS * SSD_STATE
ATT_HEADS = 16
ATT_HEAD_DIM = 64
ATT_WIDTH = ATT_HEADS * ATT_HEAD_DIM
Q_BLOCK = 128
MIX_WIDTH = SSD_WIDTH + ATT_WIDTH
IN_SIZES = (SSD_WIDTH, CONV_CH, SSD_HEADS, ATT_WIDTH, ATT_WIDTH, ATT_WIDTH, ATT_HEADS)
IN_WIDTH = sum(IN_SIZES)
D_FF = 4 * D_MODEL
EPS = 1e-5

kernel_name = "hymba_ssd_fox_sqrelu_block"


def rmsnorm(x, w):
    xf = x.astype(jnp.float32)
    y = xf * lax.rsqrt(jnp.mean(xf * xf, axis=-1, keepdims=True) + EPS)
    return (y * w.astype(jnp.float32)).astype(x.dtype)


def causal_dwconv(u, w, b):
    out = lax.conv_general_dilated(
        u, w[:, None, :].astype(u.dtype), window_strides=(1,),
        padding=[(CONV_WIDTH - 1, 0)], dimension_numbers=("NWC", "WIO", "NWC"),
        feature_group_count=u.shape[-1])
    return out + b


def segsum(a):
    cum = jnp.cumsum(a, axis=-1)
    diff = cum[..., :, None] - cum[..., None, :]
    l = a.shape[-1]
    mask = jnp.tril(jnp.ones((l, l), dtype=bool))
    return jnp.where(mask, diff, -jnp.inf)


def ssd_chunked(xh, a, bmat, cmat):
    bsz, T = xh.shape[:2]
    nc = T // CHUNK
    r = SSD_HEADS // SSD_GROUPS
    f32 = jnp.float32
    x = xh.astype(f32).reshape(bsz, nc, CHUNK, SSD_GROUPS, r, SSD_HEAD_DIM)
    a = a.astype(f32).reshape(bsz, nc, CHUNK, SSD_GROUPS, r).transpose(0, 3, 4, 1, 2)
    B = bmat.astype(f32).reshape(bsz, nc, CHUNK, SSD_GROUPS, SSD_STATE)
    C = cmat.astype(f32).reshape(bsz, nc, CHUNK, SSD_GROUPS, SSD_STATE)
    a_cum = jnp.cumsum(a, axis=-1)
    ldec = jnp.exp(segsum(a))
    cb = jnp.einsum("bclgn,bcsgn->bcgls", C, B)
    y_diag = jnp.einsum("bcgls,bgrcls,bcsgrp->bclgrp", cb, ldec, x)
    decay_states = jnp.exp(a_cum[..., -1:] - a_cum)
    states = jnp.einsum("bclgn,bgrcl,bclgrp->bcgrpn", B, decay_states, x)
    chunk_decay = jnp.exp(a_cum[..., -1])

    def step(h, inp):
        s_c, d_c = inp
        return h * d_c[..., None, None] + s_c, h

    h0 = jnp.zeros((bsz, SSD_GROUPS, r, SSD_HEAD_DIM, SSD_STATE), f32)
    _, prev = lax.scan(step, h0, (jnp.moveaxis(states, 1, 0), jnp.moveaxis(chunk_decay, -1, 0)))
    y_off = jnp.einsum("bclgn,cbgrpn,bgrcl->bclgrp", C, prev, jnp.exp(a_cum))
    return (y_diag + y_off).reshape(bsz, T, SSD_HEADS, SSD_HEAD_DIM)


def forgetting_attention(q, k, v, log_f):
    T = q.shape[2]
    scale = 1.0 / math.sqrt(ATT_HEAD_DIM)
    c = jnp.cumsum(log_f, axis=-1)
    outs = []
    for i in range(T // Q_BLOCK):
        qs, qe = i * Q_BLOCK, (i + 1) * Q_BLOCK
        s = jnp.einsum("bhqd,bhkd->bhqk", q[:, :, qs:qe], k[:, :, :qe]).astype(jnp.float32) * scale
        s = s + (c[:, :, qs:qe, None] - c[:, :, None, :qe])
        mask = (qs + jnp.arange(Q_BLOCK))[:, None] >= jnp.arange(qe)[None, :]
        p = jax.nn.softmax(jnp.where(mask, s, -jnp.inf), axis=-1)
        outs.append(jnp.einsum("bhqk,bhkd->bhqd", p.astype(v.dtype), v[:, :, :qe]))
    return jnp.concatenate(outs, axis=2)


def hybrid_mixer(h, w_in, conv_w, conv_b, dt_bias, a_log, d_skip, ssd_norm_w, f_bias, w_out):
    bsz, T, _ = h.shape
    proj = jnp.einsum("btd,de->bte", h, w_in)
    idx = np.cumsum(IN_SIZES)[:-1].tolist()
    z, xbc, dt_raw, q, k, v, f_raw = jnp.split(proj, idx, axis=-1)
    xbc = jax.nn.silu(causal_dwconv(xbc, conv_w, conv_b))
    xs, bm, cm = jnp.split(xbc, [SSD_WIDTH, SSD_WIDTH + SSD_GROUPS * SSD_STATE], axis=-1)
    xs = xs.reshape(bsz, T, SSD_HEADS, SSD_HEAD_DIM)
    bm = bm.reshape(bsz, T, SSD_GROUPS, SSD_STATE)
    cm = cm.reshape(bsz, T, SSD_GROUPS, SSD_STATE)
    dt = jax.nn.softplus(dt_raw.astype(jnp.float32) + dt_bias.astype(jnp.float32))
    A = -jnp.exp(a_log.astype(jnp.float32))
    y = ssd_chunked(xs.astype(jnp.float32) * dt[..., None], A * dt, bm, cm)
    y = y + d_skip.astype(jnp.float32)[:, None] * xs.astype(jnp.float32)
    y = y.reshape(bsz, T, SSD_WIDTH) * jax.nn.silu(z.astype(jnp.float32))
    yg = y.reshape(bsz, T, SSD_GROUPS, SSD_WIDTH // SSD_GROUPS)
    yg = yg * lax.rsqrt(jnp.mean(yg * yg, axis=-1, keepdims=True) + EPS)
    y_ssd = (yg.reshape(bsz, T, SSD_WIDTH) * ssd_norm_w.astype(jnp.float32)).astype(h.dtype)
    heads = lambda t: t.reshape(bsz, T, ATT_HEADS, ATT_HEAD_DIM).transpose(0, 2, 1, 3)
    log_f = jax.nn.log_sigmoid(f_raw.astype(jnp.float32) + f_bias.astype(jnp.float32)).transpose(0, 2, 1)
    o = forgetting_attention(heads(q), heads(k), heads(v), log_f)
    y_att = o.transpose(0, 2, 1, 3).reshape(bsz, T, ATT_WIDTH).astype(h.dtype)
    return jnp.einsum("bte,ed->btd", jnp.concatenate([y_ssd, y_att], axis=-1), w_out)


def _fwd_setup_inputs(seed: int = 0) -> dict:
    key = jax.random.key(seed)
    ks = jax.random.split(key, 16)
    f32 = jnp.float32
    L = DEPTH
    x = jax.random.normal(ks[0], (BATCH, SEQ, D_MODEL), f32)
    norm_mix_w = 1.0 + 0.01 * jax.random.normal(ks[1], (L, D_MODEL), f32)
    w_in = jax.random.normal(ks[2], (L, D_MODEL, IN_WIDTH), f32) * D_MODEL ** -0.5
    conv_w = jax.random.uniform(ks[3], (L, CONV_WIDTH, CONV_CH), f32, -1.0, 1.0) * CONV_WIDTH ** -0.5
    conv_b = 0.01 * jax.random.normal(ks[4], (L, CONV_CH), f32)
    dt0 = jnp.exp(jax.random.uniform(ks[5], (L, SSD_HEADS), f32, math.log(1e-3), math.log(1e-1)))
    dt_bias = dt0 + jnp.log(-jnp.expm1(-dt0))
    a_log = jnp.log(jax.random.uniform(ks[6], (L, SSD_HEADS), f32, 1.0, 16.0))
    d_skip = 1.0 + 0.01 * jax.random.normal(ks[7], (L, SSD_HEADS), f32)
    ssd_norm_w = 1.0 + 0.01 * jax.random.normal(ks[8], (L, SSD_WIDTH), f32)
    f_bias = jax.random.uniform(ks[9], (L, ATT_HEADS), f32, 1.0, 4.0)
    w_out = jax.random.normal(ks[10], (L, MIX_WIDTH, D_MODEL), f32) * MIX_WIDTH ** -0.5
    norm_mlp_w = 1.0 + 0.01 * jax.random.normal(ks[11], (L, D_MODEL), f32)
    w_up = jax.random.normal(ks[12], (L, D_MODEL, D_FF), f32) * D_MODEL ** -0.5
    w_down = jax.random.normal(ks[13], (L, D_FF, D_MODEL), f32) * D_FF ** -0.5
    norm_final_w = 1.0 + 0.01 * jax.random.normal(ks[14], (D_MODEL,), f32)
    return {"x": x, "norm_mix_w": norm_mix_w, "w_in": w_in, "conv_w": conv_w, "conv_b": conv_b,
            "dt_bias": dt_bias, "a_log": a_log, "d_skip": d_skip, "ssd_norm_w": ssd_norm_w,
            "f_bias": f_bias, "w_out": w_out, "norm_mlp_w": norm_mlp_w, "w_up": w_up,
            "w_down": w_down, "norm_final_w": norm_final_w}


def _fwd_reference(x, norm_mix_w, w_in, conv_w, conv_b, dt_bias, a_log, d_skip, ssd_norm_w,
              f_bias, w_out, norm_mlp_w, w_up, w_down, norm_final_w):
    h = x
    for l in range(DEPTH):
        h = h + hybrid_mixer(rmsnorm(h, norm_mix_w[l]), w_in[l], conv_w[l], conv_b[l], dt_bias[l],
                             a_log[l], d_skip[l], ssd_norm_w[l], f_bias[l], w_out[l])
        u = jnp.square(jax.nn.relu(jnp.einsum("btd,df->btf", rmsnorm(h, norm_mlp_w[l]), w_up[l])))
        h = h + jnp.einsum("btf,fd->btd", u, w_down[l])
    return rmsnorm(h, norm_final_w)


import jax as _jax
import jax.numpy as _jnp

TWIN_FORMAT = 'train_step'
FWD_PARAMS = ['x', 'norm_mix_w', 'w_in', 'conv_w', 'conv_b', 'dt_bias', 'a_log', 'd_skip', 'ssd_norm_w', 'f_bias', 'w_out', 'norm_mlp_w', 'w_up', 'w_down', 'norm_final_w']
TWIN_WEIGHTS = ['norm_mix_w', 'w_in', 'conv_w', 'conv_b', 'dt_bias', 'a_log', 'd_skip', 'ssd_norm_w', 'f_bias', 'w_out', 'norm_mlp_w', 'w_up', 'w_down', 'norm_final_w']
TWIN_DIFF_INPUT = 'x'
TWIN_INPUTS = ['x', 'norm_mix_w', 'w_in', 'conv_w', 'conv_b', 'dt_bias', 'a_log', 'd_skip', 'ssd_norm_w', 'f_bias', 'w_out', 'norm_mlp_w', 'w_up', 'w_down', 'norm_final_w', 'loss_target', 'm_norm_mix_w', 'm_w_in', 'm_conv_w', 'm_conv_b', 'm_dt_bias', 'm_a_log', 'm_d_skip', 'm_ssd_norm_w', 'm_f_bias', 'm_w_out', 'm_norm_mlp_w', 'm_w_up', 'm_w_down', 'm_norm_final_w', 'v_norm_mix_w', 'v_w_in', 'v_conv_w', 'v_conv_b', 'v_dt_bias', 'v_a_log', 'v_d_skip', 'v_ssd_norm_w', 'v_f_bias', 'v_w_out', 'v_norm_mlp_w', 'v_w_up', 'v_w_down', 'v_norm_final_w']
TWIN_OUTPUTS = ['loss', 'grad_x', 'grad_norm_mix_w', 'grad_w_in', 'grad_conv_w', 'grad_conv_b', 'grad_dt_bias', 'grad_a_log', 'grad_d_skip', 'grad_ssd_norm_w', 'grad_f_bias', 'grad_w_out', 'grad_norm_mlp_w', 'grad_w_up', 'grad_w_down', 'grad_norm_final_w', 'delta_norm_mix_w', 'delta_w_in', 'delta_conv_w', 'delta_conv_b', 'delta_dt_bias', 'delta_a_log', 'delta_d_skip', 'delta_ssd_norm_w', 'delta_f_bias', 'delta_w_out', 'delta_norm_mlp_w', 'delta_w_up', 'delta_w_down', 'delta_norm_final_w', 'new_m_norm_mix_w', 'new_m_w_in', 'new_m_conv_w', 'new_m_conv_b', 'new_m_dt_bias', 'new_m_a_log', 'new_m_d_skip', 'new_m_ssd_norm_w', 'new_m_f_bias', 'new_m_w_out', 'new_m_norm_mlp_w', 'new_m_w_up', 'new_m_w_down', 'new_m_norm_final_w', 'new_v_norm_mix_w', 'new_v_w_in', 'new_v_conv_w', 'new_v_conv_b', 'new_v_dt_bias', 'new_v_a_log', 'new_v_d_skip', 'new_v_ssd_norm_w', 'new_v_f_bias', 'new_v_w_out', 'new_v_norm_mlp_w', 'new_v_w_up', 'new_v_w_down', 'new_v_norm_final_w']
TWIN_LEAF_KINDS = {'loss': 'loss', 'grad_x': 'grad_x', 'grad_norm_mix_w': 'grad_w', 'grad_w_in': 'grad_w', 'grad_conv_w': 'grad_w', 'grad_conv_b': 'grad_w', 'grad_dt_bias': 'grad_w', 'grad_a_log': 'grad_w', 'grad_d_skip': 'grad_w', 'grad_ssd_norm_w': 'grad_w', 'grad_f_bias': 'grad_w', 'grad_w_out': 'grad_w', 'grad_norm_mlp_w': 'grad_w', 'grad_w_up': 'grad_w', 'grad_w_down': 'grad_w', 'grad_norm_final_w': 'grad_w', 'delta_norm_mix_w': 'delta_w', 'delta_w_in': 'delta_w', 'delta_conv_w': 'delta_w', 'delta_conv_b': 'delta_w', 'delta_dt_bias': 'delta_w', 'delta_a_log': 'delta_w', 'delta_d_skip': 'delta_w', 'delta_ssd_norm_w': 'delta_w', 'delta_f_bias': 'delta_w', 'delta_w_out': 'delta_w', 'delta_norm_mlp_w': 'delta_w', 'delta_w_up': 'delta_w', 'delta_w_down': 'delta_w', 'delta_norm_final_w': 'delta_w', 'new_m_norm_mix_w': 'new_m', 'new_m_w_in': 'new_m', 'new_m_conv_w': 'new_m', 'new_m_conv_b': 'new_m', 'new_m_dt_bias': 'new_m', 'new_m_a_log': 'new_m', 'new_m_d_skip': 'new_m', 'new_m_ssd_norm_w': 'new_m', 'new_m_f_bias': 'new_m', 'new_m_w_out': 'new_m', 'new_m_norm_mlp_w': 'new_m', 'new_m_w_up': 'new_m', 'new_m_w_down': 'new_m', 'new_m_norm_final_w': 'new_m', 'new_v_norm_mix_w': 'new_v', 'new_v_w_in': 'new_v', 'new_v_conv_w': 'new_v', 'new_v_conv_b': 'new_v', 'new_v_dt_bias': 'new_v', 'new_v_a_log': 'new_v', 'new_v_d_skip': 'new_v', 'new_v_ssd_norm_w': 'new_v', 'new_v_f_bias': 'new_v', 'new_v_w_out': 'new_v', 'new_v_norm_mlp_w': 'new_v', 'new_v_w_up': 'new_v', 'new_v_w_down': 'new_v', 'new_v_norm_final_w': 'new_v'}


def _forward(args):
    return _fwd_reference(*[args[k] for k in FWD_PARAMS])


def _output_shape():
    out = _jax.eval_shape(lambda: _forward(_fwd_setup_inputs(0)))
    return out.shape, out.dtype

N_MICROBATCH = 1
ADAM_LR = 0.001
ADAM_B1 = 0.9
ADAM_B2 = 0.999
ADAM_EPS = 1e-08
ADAM_WD = 0.01
ADAM_STEP = 10
PER_EXAMPLE_BATCH_AXIS = {'x': 0, 'loss_target': 0}
SHARED_INPUTS = []
_WEIGHT_DTYPES = {'norm_mix_w': _jnp.float32, 'w_in': _jnp.float32, 'conv_w': _jnp.float32, 'conv_b': _jnp.float32, 'dt_bias': _jnp.float32, 'a_log': _jnp.float32, 'd_skip': _jnp.float32, 'ssd_norm_w': _jnp.float32, 'f_bias': _jnp.float32, 'w_out': _jnp.float32, 'norm_mlp_w': _jnp.float32, 'w_up': _jnp.float32, 'w_down': _jnp.float32, 'norm_final_w': _jnp.float32}
MOMENT_SCALE = {'norm_mix_w': 2.724085e-01, 'w_in': 1.025434e-01, 'conv_w': 2.294656e-01, 'conv_b': 2.374268e-01, 'dt_bias': 3.459330e-01, 'a_log': 1.380660e-01, 'd_skip': 9.956820e-01, 'ssd_norm_w': 1.572307e-01, 'f_bias': 2.195374e-01, 'w_out': 1.628823e-01, 'norm_mlp_w': 2.057585e-01, 'w_up': 1.017188e-01, 'w_down': 1.799646e-01, 'norm_final_w': 6.440163e+01}


def _to_microbatches(a, axis):
    t = _jnp.moveaxis(a, axis, 0)
    t = t.reshape((N_MICROBATCH, t.shape[0] // N_MICROBATCH) + t.shape[1:])
    return _jnp.moveaxis(t, 1, axis + 1)


def setup_inputs(seed: int = 0) -> dict:
    inp = _fwd_setup_inputs(seed)
    key = _jax.random.fold_in(_jax.random.key(seed), 7919)
    shape, _ = _output_shape()
    out = dict(inp)
    out["loss_target"] = _jax.random.normal(_jax.random.fold_in(key, 0), shape, _jnp.float32)
    for i, name in enumerate(TWIN_WEIGHTS):
        w = inp[name].astype(_jnp.float32)
        if MOMENT_SCALE is None:
            s = _jnp.sqrt(_jnp.mean(_jnp.square(w)) + 1e-30)
        else:
            s = MOMENT_SCALE[name]
        km, kv = _jax.random.split(_jax.random.fold_in(key, i + 1))
        out[name] = w
        out["m_" + name] = s * _jax.random.normal(km, w.shape, _jnp.float32)
        out["v_" + name] = (s * s) * _jax.random.uniform(kv, w.shape, _jnp.float32, 0.5, 1.5)
    if N_MICROBATCH > 1:
        for name, axis in PER_EXAMPLE_BATCH_AXIS.items():
            out[name] = _to_microbatches(out[name], axis)
    return {'x': out['x'], 'norm_mix_w': out['norm_mix_w'], 'w_in': out['w_in'], 'conv_w': out['conv_w'], 'conv_b': out['conv_b'], 'dt_bias': out['dt_bias'], 'a_log': out['a_log'], 'd_skip': out['d_skip'], 'ssd_norm_w': out['ssd_norm_w'], 'f_bias': out['f_bias'], 'w_out': out['w_out'], 'norm_mlp_w': out['norm_mlp_w'], 'w_up': out['w_up'], 'w_down': out['w_down'], 'norm_final_w': out['norm_final_w'], 'loss_target': out['loss_target'], 'm_norm_mix_w': out['m_norm_mix_w'], 'm_w_in': out['m_w_in'], 'm_conv_w': out['m_conv_w'], 'm_conv_b': out['m_conv_b'], 'm_dt_bias': out['m_dt_bias'], 'm_a_log': out['m_a_log'], 'm_d_skip': out['m_d_skip'], 'm_ssd_norm_w': out['m_ssd_norm_w'], 'm_f_bias': out['m_f_bias'], 'm_w_out': out['m_w_out'], 'm_norm_mlp_w': out['m_norm_mlp_w'], 'm_w_up': out['m_w_up'], 'm_w_down': out['m_w_down'], 'm_norm_final_w': out['m_norm_final_w'], 'v_norm_mix_w': out['v_norm_mix_w'], 'v_w_in': out['v_w_in'], 'v_conv_w': out['v_conv_w'], 'v_conv_b': out['v_conv_b'], 'v_dt_bias': out['v_dt_bias'], 'v_a_log': out['v_a_log'], 'v_d_skip': out['v_d_skip'], 'v_ssd_norm_w': out['v_ssd_norm_w'], 'v_f_bias': out['v_f_bias'], 'v_w_out': out['v_w_out'], 'v_norm_mlp_w': out['v_norm_mlp_w'], 'v_w_up': out['v_w_up'], 'v_w_down': out['v_w_down'], 'v_norm_final_w': out['v_norm_final_w']}


def _loss(weights, diff, rest, loss_target):
    with _jax.named_scope("forward"):
        args = {**rest, TWIN_DIFF_INPUT: diff, **{k: w.astype(_WEIGHT_DTYPES[k]) for k, w in weights.items()}}
        y = _forward(args)
    with _jax.named_scope("loss_head"):
        err = _jnp.square(y.astype(_jnp.float32) - loss_target)
        return 0.5 * _jnp.sum(_jnp.mean(err, axis=-1)) if err.ndim else 0.5 * err


def _adamw(w, g, m, v):
    m = ADAM_B1 * m + (1.0 - ADAM_B1) * g
    v = ADAM_B2 * v + (1.0 - ADAM_B2) * _jnp.square(g)
    m_hat = m / (1.0 - ADAM_B1 ** ADAM_STEP)
    v_hat = v / (1.0 - ADAM_B2 ** ADAM_STEP)
    delta = -ADAM_LR * (m_hat / (_jnp.sqrt(v_hat) + ADAM_EPS) + ADAM_WD * w)
    return delta, m, v


def reference(x, norm_mix_w, w_in, conv_w, conv_b, dt_bias, a_log, d_skip, ssd_norm_w, f_bias, w_out, norm_mlp_w, w_up, w_down, norm_final_w, loss_target, m_norm_mix_w, m_w_in, m_conv_w, m_conv_b, m_dt_bias, m_a_log, m_d_skip, m_ssd_norm_w, m_f_bias, m_w_out, m_norm_mlp_w, m_w_up, m_w_down, m_norm_final_w, v_norm_mix_w, v_w_in, v_conv_w, v_conv_b, v_dt_bias, v_a_log, v_d_skip, v_ssd_norm_w, v_f_bias, v_w_out, v_norm_mlp_w, v_w_up, v_w_down, v_norm_final_w):
    given = dict(x=x, norm_mix_w=norm_mix_w, w_in=w_in, conv_w=conv_w, conv_b=conv_b, dt_bias=dt_bias, a_log=a_log, d_skip=d_skip, ssd_norm_w=ssd_norm_w, f_bias=f_bias, w_out=w_out, norm_mlp_w=norm_mlp_w, w_up=w_up, w_down=w_down, norm_final_w=norm_final_w, loss_target=loss_target, m_norm_mix_w=m_norm_mix_w, m_w_in=m_w_in, m_conv_w=m_conv_w, m_conv_b=m_conv_b, m_dt_bias=m_dt_bias, m_a_log=m_a_log, m_d_skip=m_d_skip, m_ssd_norm_w=m_ssd_norm_w, m_f_bias=m_f_bias, m_w_out=m_w_out, m_norm_mlp_w=m_norm_mlp_w, m_w_up=m_w_up, m_w_down=m_w_down, m_norm_final_w=m_norm_final_w, v_norm_mix_w=v_norm_mix_w, v_w_in=v_w_in, v_conv_w=v_conv_w, v_conv_b=v_conv_b, v_dt_bias=v_dt_bias, v_a_log=v_a_log, v_d_skip=v_d_skip, v_ssd_norm_w=v_ssd_norm_w, v_f_bias=v_f_bias, v_w_out=v_w_out, v_norm_mlp_w=v_norm_mlp_w, v_w_up=v_w_up, v_w_down=v_w_down, v_norm_final_w=v_norm_final_w)
    weights = {n: given[n] for n in TWIN_WEIGHTS}
    shared = {n: given[n] for n in SHARED_INPUTS}
    per_example = {n: given[n] for n in ['x']}
    grad_fn = _jax.value_and_grad(_loss, argnums=(0, 1))

    def one_microbatch(ex, loss_target):
        ex = dict(ex)
        diff = ex.pop(TWIN_DIFF_INPUT)
        return grad_fn(weights, diff, {**shared, **ex}, loss_target)

    if N_MICROBATCH == 1:
        loss, (grad_w, grad_x) = one_microbatch(per_example, given["loss_target"])
    else:
        def body(carry, xs):
            loss_sum, grad_sum = carry
            l_k, (gw_k, gx_k) = one_microbatch(xs[0], xs[1])
            with _jax.named_scope("update"):
                return (loss_sum + l_k, _jax.tree.map(_jnp.add, grad_sum, gw_k)), gx_k

        init = (_jnp.zeros((), _jnp.float32), _jax.tree.map(_jnp.zeros_like, weights))
        (loss, grad_w), grad_x = _jax.lax.scan(body, init, (per_example, given["loss_target"]))
    with _jax.named_scope("update"):
        delta_w, new_m, new_v = {}, {}, {}
        for n in TWIN_WEIGHTS:
            delta_w[n], new_m[n], new_v[n] = _adamw(weights[n], grad_w[n], given["m_" + n], given["v_" + n])
    return (loss, grad_x, *[grad_w[n] for n in TWIN_WEIGHTS], *[delta_w[n] for n in TWIN_WEIGHTS],
            *[new_m[n] for n in TWIN_WEIGHTS], *[new_v[n] for n in TWIN_WEIGHTS])
```

```python
import functools
import math

import jax
import jax.numpy as jnp
from jax import lax
from jax.experimental import pallas as pl
from jax.experimental.pallas import tpu as pltpu

F32 = jnp.float32
BF16 = jnp.bfloat16
HIGHEST = lax.Precision.HIGHEST
MESH = pl.DeviceIdType.MESH

D_MODEL = 1024
SSD_HEADS = 16
HEAD_DIM = 64
SSD_WIDTH = 1024
SSD_STATE = 128
CONV_CH = 1536
CHUNK = 128
ATT_WIDTH = 1024
EPS = 1e-5
IN_WIDTH = 5664
PA_WIDTH = 2688
QKV_WIDTH = 3072
D_FF = 4096
ATT_BLOCK = 256
NEG = -1e30
VMEM_LIMIT = 48 * 1024 * 1024

ADAM_LR = 0.001
ADAM_B1 = 0.9
ADAM_B2 = 0.999
ADAM_EPS = 1e-08
ADAM_WD = 0.01
ADAM_STEP = 10

N_CHIPS = 4
BLOB_ROWS = 4096
HALF_ROWS = BLOB_ROWS // 2
SMALL_ROWS = 96


def _cparams(sem):
    return pltpu.CompilerParams(dimension_semantics=sem, vmem_limit_bytes=VMEM_LIMIT)


def _pick(n, cands):
    for c in cands:
        if n % c == 0:
            return c
    return n


def _mm(a, b, *, name, ta=False, tb=False, out_dtype=F32, res=None, a_act=None, epi_up=None):
    if ta:
        K, M = a.shape
    else:
        M, K = a.shape
    if tb:
        N, K2 = b.shape
    else:
        K2, N = b.shape
    assert K == K2, (a.shape, b.shape)
    tm = _pick(M, (512, 256, 128))
    tn = _pick(N, (512, 384, 256, 128))
    tk = _pick(K, (1024, 896, 512, 384, 256, 128))
    nk = K // tk
    dn = (((0 if ta else 1,), (1 if tb else 0,)), ((), ()))
    has_res = res is not None
    has_up = epi_up is not None

    def body(*refs):
        a_ref, b_ref = refs[0], refs[1]
        i = 2
        res_ref = up_ref = None
        if has_res:
            res_ref = refs[i]
            i += 1
        if has_up:
            up_ref = refs[i]
            i += 1
        o_ref, acc_ref = refs[i], refs[i + 1]
        k = pl.program_id(2)

        @pl.when(k == 0)
        def _():
            acc_ref[...] = jnp.zeros_like(acc_ref)

        av = a_ref[...]
        if a_act == "relu2":
            r = jnp.maximum(av.astype(F32), 0.0)
            av = r * r
        acc_ref[...] += lax.dot_general(av.astype(BF16), b_ref[...].astype(BF16), dn,
                                        preferred_element_type=F32)

        @pl.when(k == nk - 1)
        def _():
            out = acc_ref[...]
            if has_res:
                out = out + res_ref[...].astype(F32)
            if has_up:
                out = out * (2.0 * jnp.maximum(up_ref[...].astype(F32), 0.0))
            o_ref[...] = out.astype(out_dtype)

    a_spec = pl.BlockSpec((tk, tm), lambda i, j, k: (k, i)) if ta else pl.BlockSpec((tm, tk), lambda i, j, k: (i, k))
    b_spec = pl.BlockSpec((tn, tk), lambda i, j, k: (j, k)) if tb else pl.BlockSpec((tk, tn), lambda i, j, k: (k, j))
    o_spec = pl.BlockSpec((tm, tn), lambda i, j, k: (i, j))
    ins, specs = [a, b], [a_spec, b_spec]
    if has_res:
        ins.append(res)
        specs.append(o_spec)
    if has_up:
        ins.append(epi_up)
        specs.append(o_spec)
    return pl.pallas_call(
        body, name=name, grid=(M // tm, N // tn, nk),
        in_specs=specs, out_specs=o_spec,
        out_shape=jax.ShapeDtypeStruct((M, N), out_dtype),
        scratch_shapes=[pltpu.VMEM((tm, tn), F32)],
        compiler_params=_cparams(("parallel", "parallel", "arbitrary")),
    )(*ins)


def _rmsnorm_fwd(x, w, *, name):
    n, d = x.shape
    tm = _pick(n, (512, 256, 128))

    def body(x_ref, w_ref, y_ref, r_ref):
        xv = x_ref[...]
        rstd = lax.rsqrt(jnp.mean(xv * xv, axis=1, keepdims=True) + EPS)
        y_ref[...] = (xv * rstd * w_ref[...]).astype(BF16)
        r_ref[...] = rstd

    return pl.pallas_call(
        body, name=name, grid=(n // tm,),
        in_specs=[pl.BlockSpec((tm, d), lambda i: (i, 0)), pl.BlockSpec((1, d), lambda i: (0, 0))],
        out_specs=[pl.BlockSpec((tm, d), lambda i: (i, 0)), pl.BlockSpec((tm, 1), lambda i: (i, 0))],
        out_shape=[jax.ShapeDtypeStruct((n, d), BF16), jax.ShapeDtypeStruct((n, 1), F32)],
        compiler_params=_cparams(("parallel",)),
    )(x, w)


def _rmsnorm_bwd(dyn, x, rstd, w, dres, *, name):
    n, d = x.shape
    tm = _pick(n, (512, 256, 128))

    def body(g_ref, x_ref, r_ref, w_ref, d_ref, dx_ref, dw_ref):
        @pl.when(pl.program_id(0) == 0)
        def _():
            dw_ref[...] = jnp.zeros_like(dw_ref)

        g = g_ref[...]
        r = r_ref[...]
        xhat = x_ref[...] * r
        gw = g * w_ref[...]
        dx_ref[...] = d_ref[...] + r * (gw - xhat * jnp.mean(gw * xhat, axis=1, keepdims=True))
        dw_ref[...] += jnp.sum(g * xhat, axis=0, keepdims=True)

    row = pl.BlockSpec((tm, d), lambda i: (i, 0))
    vec = pl.BlockSpec((1, d), lambda i: (0, 0))
    return pl.pallas_call(
        body, name=name, grid=(n // tm,),
        in_specs=[row, row, pl.BlockSpec((tm, 1), lambda i: (i, 0)), vec, row],
        out_specs=[row, vec],
        out_shape=[jax.ShapeDtypeStruct((n, d), F32), jax.ShapeDtypeStruct((1, d), F32)],
        compiler_params=_cparams(("arbitrary",)),
    )(dyn, x, rstd, w, dres)


def _final(h2, w, target):
    n, d = h2.shape
    tm = _pick(n, (512, 256, 128))

    def body(h_ref, w_ref, t_ref, dh_ref, loss_ref, dw_ref):
        @pl.when(pl.program_id(0) == 0)
        def _():
            loss_ref[...] = jnp.zeros_like(loss_ref)
            dw_ref[...] = jnp.zeros_like(dw_ref)

        hv = h_ref[...]
        wv = w_ref[...]
        rstd = lax.rsqrt(jnp.mean(hv * hv, axis=1, keepdims=True) + EPS)
        xhat = hv * rstd
        err = xhat * wv - t_ref[...]
        part = jnp.sum(jnp.mean(err * err, axis=1, keepdims=True), axis=0, keepdims=True)
        loss_ref[...] += 0.5 * part
        dy = err * (1.0 / d)
        gw = dy * wv
        dh_ref[...] = rstd * (gw - xhat * jnp.mean(gw * xhat, axis=1, keepdims=True))
        dw_ref[...] += jnp.sum(dy * xhat, axis=0, keepdims=True)

    row = pl.BlockSpec((tm, d), lambda i: (i, 0))
    vec = pl.BlockSpec((1, d), lambda i: (0, 0))
    return pl.pallas_call(
        body, name="final_norm_loss", grid=(n // tm,),
        in_specs=[row, vec, row],
        out_specs=[row, pl.BlockSpec((1, 128), lambda i: (0, 0)), vec],
        out_shape=[jax.ShapeDtypeStruct((n, d), F32), jax.ShapeDtypeStruct((1, 128), F32),
                   jax.ShapeDtypeStruct((1, d), F32)],
        compiler_params=_cparams(("arbitrary",)),
    )(h2, w, target)


def _softplus(x):
    return jnp.maximum(x, 0.0) + jnp.log(1.0 + jnp.exp(-jnp.abs(x)))


def _prep(proj_a, bias128, alog128, bl, t):
    n = bl * t
    nch = t // CHUNK
    col0 = (SSD_WIDTH + CONV_CH) // 128

    def body(p_ref, b_ref, al_ref, dt_ref, sg_ref, ac_ref, c_ref, sf_ref, carry):
        @pl.when(pl.program_id(1) == 0)
        def _():
            carry[...] = jnp.zeros_like(carry)

        xv = p_ref[...] + b_ref[...]
        sp = _softplus(xv)
        a = -jnp.exp(al_ref[...]) * sp
        logf = -_softplus(-xv)
        row = lax.broadcasted_iota(jnp.int32, (CHUNK, CHUNK), 0)
        col = lax.broadcasted_iota(jnp.int32, (CHUNK, CHUNK), 1)
        tril = (row >= col).astype(F32)
        acum = jnp.dot(tril, a, precision=HIGHEST, preferred_element_type=F32)
        c = jnp.dot(tril, logf, precision=HIGHEST, preferred_element_type=F32) + carry[...]
        carry[...] = c[CHUNK - 1:CHUNK, :]
        dt_ref[...] = sp[:, 0:16]
        sg_ref[...] = jax.nn.sigmoid(xv)[:, 0:16]
        ac_ref[...] = acum[:, 0:16]
        c_ref[...] = c[:, 16:32]
        sf_ref[...] = jax.nn.sigmoid(-xv)[:, 16:32]

    o16 = pl.BlockSpec((CHUNK, 16), lambda b, c: (b * nch + c, 0))
    v128 = pl.BlockSpec((1, 128), lambda b, c: (0, 0))
    return pl.pallas_call(
        body, name="head_scalars", grid=(bl, nch),
        in_specs=[pl.BlockSpec((CHUNK, 128), lambda b, c: (b * nch + c, col0)), v128, v128],
        out_specs=[o16] * 5,
        out_shape=[jax.ShapeDtypeStruct((n, 16), F32)] * 5,
        scratch_shapes=[pltpu.VMEM((1, 128), F32)],
        compiler_params=_cparams(("parallel", "arbitrary")),
    )(proj_a, bias128, alog128)


def _fpost(dck, dcq, sigf, bl, t):
    n = bl * t
    nch = t // CHUNK

    def body(dc_ref, dq_ref, sf_ref, df_ref, db_ref, carry):
        @pl.when(pl.program_id(1) == 0)
        def _():
            carry[...] = jnp.zeros_like(carry)

        @pl.when((pl.program_id(0) == 0) & (pl.program_id(1) == 0))
        def _():
            db_ref[...] = jnp.zeros_like(db_ref)

        row = lax.broadcasted_iota(jnp.int32, (CHUNK, CHUNK), 0)
        col = lax.broadcasted_iota(jnp.int32, (CHUNK, CHUNK), 1)
        triu = (row <= col).astype(F32)
        dlf = jnp.dot(triu, dc_ref[...] + dq_ref[...], precision=HIGHEST, preferred_element_type=F32) + carry[...]
        carry[...] = dlf[0:1, :]
        df = dlf[:, 0:16] * sf_ref[...]
        df_ref[...] = df
        db_ref[...] += jnp.sum(df, axis=0, keepdims=True)

    rev = lambda b, c: (b * nch + nch - 1 - c, 0)
    blk = pl.BlockSpec((CHUNK, 16), rev)
    return pl.pallas_call(
        body, name="forget_gate_bwd", grid=(bl, nch),
        in_specs=[pl.BlockSpec((CHUNK, 128), rev), pl.BlockSpec((CHUNK, 128), rev), blk],
        out_specs=[blk, pl.BlockSpec((1, 16), lambda b, c: (0, 0))],
        out_shape=[jax.ShapeDtypeStruct((n, 16), F32), jax.ShapeDtypeStruct((1, 16), F32)],
        scratch_shapes=[pltpu.VMEM((1, 128), F32)],
        compiler_params=_cparams(("arbitrary", "arbitrary")),
    )(dck, dcq, sigf)


CONV_TILE = 256
CONV_ROWS = 256


def _conv_taps(u_ref, i, w, bias):
    r0 = pl.multiple_of(i * CONV_ROWS, CONV_ROWS)
    cur = u_ref[pl.ds(r0, CONV_ROWS), :]
    p0 = pl.multiple_of(jnp.maximum(r0 - 8, 0), 8)
    prev = jnp.where(i > 0, u_ref[pl.ds(p0, 8), :], 0.0)
    cat = jnp.concatenate([prev, cur], axis=0)
    pre = bias + w[3:4, :] * cur
    taps = [cur]
    for s in (1, 2, 3):
        sh = pltpu.roll(cat, s, 0)[8:, :]
        taps.append(sh)
        pre = pre + w[3 - s:4 - s, :] * sh
    return r0, pre, taps


def _conv_fwd(proj_a, conv_w, conv_b, bl, t):
    n = bl * t
    nct = CONV_CH // CONV_TILE
    c0 = SSD_WIDTH // CONV_TILE

    def body(u_ref, w_ref, b_ref, o_ref):
        w = w_ref[...]
        bias = b_ref[...]

        def chunk(i, carry):
            r0, pre, _ = _conv_taps(u_ref, i, w, bias)
            o_ref[pl.ds(r0, CONV_ROWS), :] = pre * jax.nn.sigmoid(pre)
            return carry

        lax.fori_loop(0, t // CONV_ROWS, chunk, 0)

    return pl.pallas_call(
        body, name="conv_silu_fwd", grid=(bl, nct),
        in_specs=[pl.BlockSpec((t, CONV_TILE), lambda b, c: (b, c0 + c)),
                  pl.BlockSpec((4, CONV_TILE), lambda b, c: (0, c)),
                  pl.BlockSpec((1, CONV_TILE), lambda b, c: (0, c))],
        out_specs=pl.BlockSpec((t, CONV_TILE), lambda b, c: (b, c)),
        out_shape=jax.ShapeDtypeStruct((n, CONV_CH), F32),
        compiler_params=_cparams(("parallel", "parallel")),
    )(proj_a, conv_w, conv_b)


def _conv_bwd(dxc, proj_a, conv_w, conv_b, bl, t):
    n = bl * t
    nct = CONV_CH // CONV_TILE
    c0 = SSD_WIDTH // CONV_TILE
    nrc = t // CONV_ROWS

    def body(g_ref, u_ref, w_ref, b_ref, du_ref, dw_ref, db_ref, dp_scr):
        @pl.when(pl.program_id(1) == 0)
        def _():
            dw_ref[...] = jnp.zeros_like(dw_ref)
            db_ref[...] = jnp.zeros_like(db_ref)

        w = w_ref[...]
        bias = b_ref[...]
        dp_scr[pl.ds(t, 8), :] = jnp.zeros((8, CONV_TILE), F32)

        def chunk1(i, carry):
            dw0, dw1, dw2, dw3, db = carry
            r0, pre, taps = _conv_taps(u_ref, i, w, bias)
            sg = jax.nn.sigmoid(pre)
            dpre = g_ref[pl.ds(r0, CONV_ROWS), :] * (sg * (1.0 + pre * (1.0 - sg)))
            dp_scr[pl.ds(r0, CONV_ROWS), :] = dpre
            dw3 = dw3 + jnp.sum(dpre * taps[0], axis=0, keepdims=True)
            dw2 = dw2 + jnp.sum(dpre * taps[1], axis=0, keepdims=True)
            dw1 = dw1 + jnp.sum(dpre * taps[2], axis=0, keepdims=True)
            dw0 = dw0 + jnp.sum(dpre * taps[3], axis=0, keepdims=True)
            db = db + jnp.sum(dpre, axis=0, keepdims=True)
            return dw0, dw1, dw2, dw3, db

        z = jnp.zeros((1, CONV_TILE), F32)
        dw0, dw1, dw2, dw3, db = lax.fori_loop(0, nrc, chunk1, (z, z, z, z, z))
        dw_ref[...] += jnp.concatenate([dw0, dw1, dw2, dw3], axis=0)
        db_ref[...] += db

        def chunk2(i, carry):
            r0 = pl.multiple_of(i * CONV_ROWS, CONV_ROWS)
            cat = dp_scr[pl.ds(r0, CONV_ROWS + 8), :]
            du = w[3:4, :] * cat[:CONV_ROWS, :]
            for s in (1, 2, 3):
                du = du + w[3 - s:4 - s, :] * pltpu.roll(cat, CONV_ROWS + 8 - s, 0)[:CONV_ROWS, :]
            du_ref[pl.ds(r0, CONV_ROWS), :] = du.astype(BF16)
            return carry

        lax.fori_loop(0, nrc, chunk2, 0)

    return pl.pallas_call(
        body, name="conv_silu_bwd", grid=(nct, bl),
        in_specs=[pl.BlockSpec((t, CONV_TILE), lambda c, b: (b, c)),
                  pl.BlockSpec((t, CONV_TILE), lambda c, b: (b, c0 + c)),
                  pl.BlockSpec((4, CONV_TILE), lambda c, b: (0, c)),
                  pl.BlockSpec((1, CONV_TILE), lambda c, b: (0, c))],
        out_specs=[pl.BlockSpec((t, CONV_TILE), lambda c, b: (b, c)),
                   pl.BlockSpec((4, CONV_TILE), lambda c, b: (0, c)),
                   pl.BlockSpec((1, CONV_TILE), lambda c, b: (0, c))],
        out_shape=[jax.ShapeDtypeStruct((n, CONV_CH), BF16), jax.ShapeDtypeStruct((4, CONV_CH), F32),
                   jax.ShapeDtypeStruct((1, CONV_CH), F32)],
        scratch_shapes=[pltpu.VMEM((t + 8, CONV_TILE), F32)],
        compiler_params=_cparams(("parallel", "arbitrary")),
    )(dxc, proj_a, conv_w, conv_b)


NT_DIMS = (((1,), (1,)), ((), ()))
TN_DIMS = (((0,), (0,)), ((), ()))


def _dot(a, b, dims=None):
    if dims is None:
        return jnp.dot(a, b, preferred_element_type=F32)
    return lax.dot_general(a, b, dims, preferred_element_type=F32)


def _ssd_fwd(xc, proj_a, dt, acum, acum_t, dskip_e, norm_w, bl, t):
    n = bl * t
    nch = t // CHUNK
    L = CHUNK

    def body(xc_ref, z_ref, dt_ref, ac_ref, act_ref, dsk_ref, nw_ref, ys_ref, yp_ref, hp_ref, h_scr, y_scr):
        @pl.when(pl.program_id(1) == 0)
        def _():
            h_scr[...] = jnp.zeros_like(h_scr)

        row = lax.broadcasted_iota(jnp.int32, (L, L), 0)
        col = lax.broadcasted_iota(jnp.int32, (L, L), 1)
        causal = row >= col
        dt_all = dt_ref[...]
        ac_all = ac_ref[...]
        act_all = act_ref[...]
        for g in range(2):
            bg = xc_ref[:, SSD_WIDTH + g * 128:SSD_WIDTH + (g + 1) * 128].astype(BF16)
            cg = xc_ref[:, SSD_WIDTH + 256 + g * 128:SSD_WIDTH + 256 + (g + 1) * 128].astype(BF16)
            gmat = _dot(cg, bg, NT_DIMS)
            for r in range(8):
                h = g * 8 + r
                sl = slice(h * HEAD_DIM, (h + 1) * HEAD_DIM)
                xs = xc_ref[:, sl]
                xdt = xs * dt_all[:, h:h + 1]
                ac = ac_all[:, h:h + 1]
                ar = act_all[h:h + 1, :]
                ldec = jnp.exp(jnp.where(causal, ac - ar, NEG))
                m = (gmat * ldec).astype(BF16)
                hp = h_scr[h]
                hp_ref[h] = hp
                yd = _dot(m, xdt.astype(BF16))
                yo = _dot(cg, hp.astype(BF16), NT_DIMS) * jnp.exp(ac)
                y_scr[:, sl] = yd + yo + dsk_ref[:, sl] * xs
                alast = ac_all[L - 1:L, h:h + 1]
                xd = (xdt * jnp.exp(alast - ac)).astype(BF16)
                h_scr[h] = jnp.exp(alast) * hp + _dot(xd, bg, TN_DIMS)
        y = y_scr[...]
        yp_ref[...] = y
        zv = z_ref[...]
        yg = y * (zv * jax.nn.sigmoid(zv))
        for g in range(2):
            gs = slice(g * 512, (g + 1) * 512)
            grp = yg[:, gs]
            rstd = lax.rsqrt(jnp.mean(grp * grp, axis=1, keepdims=True) + EPS)
            ys_ref[:, gs] = (grp * rstd * nw_ref[:, gs]).astype(BF16)

    rb = lambda b, c: (b * nch + c, 0)
    v1k = pl.BlockSpec((1, SSD_WIDTH), lambda b, c: (0, 0))
    return pl.pallas_call(
        body, name="ssd_fwd", grid=(bl, nch),
        in_specs=[pl.BlockSpec((L, CONV_CH), rb), pl.BlockSpec((L, SSD_WIDTH), rb),
                  pl.BlockSpec((L, 16), rb), pl.BlockSpec((L, 16), rb),
                  pl.BlockSpec((16, L), lambda b, c: (0, b * nch + c)), v1k, v1k],
        out_specs=[pl.BlockSpec((L, SSD_WIDTH), rb), pl.BlockSpec((L, SSD_WIDTH), rb),
                   pl.BlockSpec((None, 16, HEAD_DIM, SSD_STATE), lambda b, c: (b * nch + c, 0, 0, 0))],
        out_shape=[jax.ShapeDtypeStruct((n, SSD_WIDTH), BF16), jax.ShapeDtypeStruct((n, SSD_WIDTH), F32),
                   jax.ShapeDtypeStruct((bl * nch, 16, HEAD_DIM, SSD_STATE), F32)],
        scratch_shapes=[pltpu.VMEM((16, HEAD_DIM, SSD_STATE), F32), pltpu.VMEM((L, SSD_WIDTH), F32)],
        compiler_params=_cparams(("parallel", "arbitrary")),
    )(xc, proj_a, dt, acum, acum_t, dskip_e, norm_w)


def _ssd_bwd(dys, xc, proj_a, ypre, hprev, dt, sig, acum, acum_t, a_log, dskip_e, norm_w, bl, t):
    n = bl * t
    nch = t // CHUNK
    L = CHUNK

    def body(dys_ref, xc_ref, z_ref, yp_ref, hp_ref, dt_ref, sg_ref, ac_ref, act_ref, al_ref, dsk_ref, nw_ref,
             dxc_ref, dz_ref, ddt_ref, dnw_ref, dsk16_ref, da16_ref, db16_ref, dh_scr, dy_scr):
        first = (pl.program_id(0) == 0) & (pl.program_id(1) == 0)

        @pl.when(first)
        def _():
            dnw_ref[...] = jnp.zeros_like(dnw_ref)
            dsk16_ref[...] = jnp.zeros_like(dsk16_ref)
            da16_ref[...] = jnp.zeros_like(da16_ref)
            db16_ref[...] = jnp.zeros_like(db16_ref)

        @pl.when(pl.program_id(1) == 0)
        def _():
            dh_scr[...] = jnp.zeros_like(dh_scr)

        y = yp_ref[...]
        zv = z_ref[...]
        sz = jax.nn.sigmoid(zv)
        gate = zv * sz
        yg = y * gate
        dout = dys_ref[...]
        nw = nw_ref[...]
        for g in range(2):
            gs = slice(g * 512, (g + 1) * 512)
            grp = yg[:, gs]
            rstd = lax.rsqrt(jnp.mean(grp * grp, axis=1, keepdims=True) + EPS)
            ghat = grp * rstd
            dnw_ref[:, gs] += jnp.sum(dout[:, gs] * ghat, axis=0, keepdims=True)
            gw = dout[:, gs] * nw[:, gs]
            dyg = rstd * (gw - ghat * jnp.mean(gw * ghat, axis=1, keepdims=True))
            dy_scr[:, gs] = dyg * gate[:, gs]
            dz_ref[:, gs] = (dyg * y[:, gs] * (sz[:, gs] * (1.0 + zv[:, gs] * (1.0 - sz[:, gs])))).astype(BF16)

        row = lax.broadcasted_iota(jnp.int32, (L, L), 0)
        col = lax.broadcasted_iota(jnp.int32, (L, L), 1)
        causal = row >= col
        lane16 = lax.broadcasted_iota(jnp.int32, (1, 16), 1)
        lane128 = lax.broadcasted_iota(jnp.int32, (1, L), 1)
        last_row = lax.broadcasted_iota(jnp.int32, (L, 1), 0) == (L - 1)
        dt_all = dt_ref[...]
        ac_all = ac_ref[...]
        act_all = act_ref[...]
        dac_col = jnp.zeros((L, L), F32)
        dac_row = jnp.zeros((L, L), F32)
        ddt_x = jnp.zeros((L, L), F32)
        dsk16 = jnp.zeros((1, 16), F32)
        rows16 = lax.broadcasted_iota(jnp.int32, (L, 1), 0)
        for g in range(2):
            bsl = slice(SSD_WIDTH + g * 128, SSD_WIDTH + (g + 1) * 128)
            csl = slice(SSD_WIDTH + 256 + g * 128, SSD_WIDTH + 256 + (g + 1) * 128)
            bg = xc_ref[:, bsl].astype(BF16)
            cg = xc_ref[:, csl].astype(BF16)
            gmat = _dot(cg, bg, NT_DIMS)
            dg_sum = jnp.zeros((L, L), F32)
            dc_acc = jnp.zeros((L, SSD_STATE), F32)
            db_acc = jnp.zeros((L, SSD_STATE), F32)
            for r in range(8):
                h = g * 8 + r
                sl = slice(h * HEAD_DIM, (h + 1) * HEAD_DIM)
                onehot = lane16 == h
                onehot_w = lane128 == h
                xs = xc_ref[:, sl]
                dth = dt_all[:, h:h + 1]
                xdt = xs * dth
                xb = xdt.astype(BF16)
                ac = ac_all[:, h:h + 1]
                ar = act_all[h:h + 1, :]
                alast = ac_all[L - 1:L, h:h + 1]
                ldec = jnp.exp(jnp.where(causal, ac - ar, NEG))
                mf = gmat * ldec
                e_in = jnp.exp(ac)
                dec = jnp.exp(alast - ac)
                elast = jnp.exp(alast)
                hp = hp_ref[h]
                hpb = hp.astype(BF16)
                dyh = dy_scr[:, sl]
                dyb = dyh.astype(BF16)
                dsk16 = dsk16 + jnp.where(onehot, jnp.sum(jnp.sum(dyh * xs, axis=1, keepdims=True), axis=0, keepdims=True), 0.0)
                dm = _dot(dyb, xb, NT_DIMS)
                dx = _dot(mf.astype(BF16), dyb, TN_DIMS)
                dg_sum = dg_sum + dm * ldec
                wmat = dm * mf
                dac_h = jnp.sum(wmat, axis=1, keepdims=True)
                dac_row = dac_row + jnp.where(rows16 == h, -jnp.sum(wmat, axis=0, keepdims=True), 0.0)
                ch = _dot(cg, hpb, NT_DIMS)
                dye = dyh * e_in
                dyeb = dye.astype(BF16)
                dc_acc = dc_acc + _dot(dyeb, hpb)
                dhp = _dot(dyeb, cg, TN_DIMS)
                dac_h = dac_h + jnp.sum(dye * ch, axis=1, keepdims=True)
                ds = dh_scr[h]
                dsb = ds.astype(BF16)
                dxd = _dot(bg, dsb, NT_DIMS)
                db_acc = db_acc + _dot((xdt * dec).astype(BF16), dsb)
                dx = dx + dxd * dec
                ddec = jnp.sum(dxd * xdt, axis=1, keepdims=True) * dec
                extra = (jnp.sum(ddec, axis=0, keepdims=True)
                         + elast * jnp.sum(jnp.sum(hp * ds, axis=1, keepdims=True), axis=0, keepdims=True))
                dac_h = dac_h - ddec + jnp.where(last_row, extra, 0.0)
                dh_scr[h] = elast * ds + dhp
                dac_col = dac_col + jnp.where(onehot_w, dac_h, 0.0)
                ddt_x = ddt_x + jnp.where(onehot_w, jnp.sum(dx * xs, axis=1, keepdims=True), 0.0)
                dxc_ref[:, sl] = dx * dth + dsk_ref[:, sl] * dyh
            dgb = dg_sum.astype(BF16)
            dxc_ref[:, csl] = dc_acc + _dot(dgb, bg)
            dxc_ref[:, bsl] = db_acc + _dot(dgb, cg, TN_DIMS)
        dac = dac_col + jnp.transpose(dac_row)
        triu = (row <= col).astype(F32)
        da = jnp.dot(triu, dac, precision=HIGHEST, preferred_element_type=F32)[:, 0:16]
        a_row = -jnp.exp(al_ref[...])
        ddt = (ddt_x[:, 0:16] + da * a_row) * sg_ref[...]
        ddt_ref[...] = ddt
        dsk16_ref[...] += dsk16
        da16_ref[...] += jnp.sum(da * dt_all, axis=0, keepdims=True) * a_row
        db16_ref[...] += jnp.sum(ddt, axis=0, keepdims=True)

    rb = lambda b, c: (b * nch + nch - 1 - c, 0)
    v1k = pl.BlockSpec((1, SSD_WIDTH), lambda b, c: (0, 0))
    v16 = pl.BlockSpec((1, 16), lambda b, c: (0, 0))
    wide = pl.BlockSpec((L, SSD_WIDTH), rb)
    s16 = pl.BlockSpec((L, 16), rb)
    return pl.pallas_call(
        body, name="ssd_bwd", grid=(bl, nch),
        in_specs=[wide, pl.BlockSpec((L, CONV_CH), rb), wide, wide,
                  pl.BlockSpec((None, 16, HEAD_DIM, SSD_STATE), lambda b, c: (b * nch + nch - 1 - c, 0, 0, 0)),
                  s16, s16, s16, pl.BlockSpec((16, L), lambda b, c: (0, b * nch + nch - 1 - c)), v16, v1k, v1k],
        out_specs=[pl.BlockSpec((L, CONV_CH), rb), wide, s16, v1k, v16, v16, v16],
        out_shape=[jax.ShapeDtypeStruct((n, CONV_CH), F32), jax.ShapeDtypeStruct((n, SSD_WIDTH), BF16),
                   jax.ShapeDtypeStruct((n, 16), F32), jax.ShapeDtypeStruct((1, SSD_WIDTH), F32),
                   jax.ShapeDtypeStruct((1, 16), F32), jax.ShapeDtypeStruct((1, 16), F32),
                   jax.ShapeDtypeStruct((1, 16), F32)],
        scratch_shapes=[pltpu.VMEM((16, HEAD_DIM, SSD_STATE), F32), pltpu.VMEM((L, SSD_WIDTH), F32)],
        compiler_params=_cparams(("arbitrary", "arbitrary")),
    )(dys, xc, proj_a, ypre, hprev, dt, sig, acum, acum_t, a_log, dskip_e, norm_w)


def _attn_fwd(qkv, negc, bl, t):
    n = bl * t
    tb_ = ATT_BLOCK
    nb = t // tb_
    scale = 1.0 / math.sqrt(HEAD_DIM)

    def body(q_ref, k_ref, v_ref, c_ref, o_ref, lse_ref):
        row = lax.broadcasted_iota(jnp.int32, (tb_, tb_), 0)
        col = lax.broadcasted_iota(jnp.int32, (tb_, tb_), 1)
        causal = row >= col

        def qloop(qi, carry0):
            r = pl.multiple_of(qi * tb_, tb_)
            for j in range(2):
                sl = slice(j * HEAD_DIM, (j + 1) * HEAD_DIM)
                qj = q_ref[pl.ds(r, tb_), sl]

                def step(kj, carry, masked):
                    m, l, acc = carry
                    rk = pl.multiple_of(kj * tb_, tb_)
                    s = _dot(qj, k_ref[pl.ds(rk, tb_), sl], NT_DIMS) * scale + c_ref[kj][j:j + 1, :]
                    if masked:
                        s = jnp.where(causal, s, NEG)
                    mn = jnp.maximum(m, jnp.max(s, axis=1, keepdims=True))
                    p = jnp.exp(s - mn)
                    alpha = jnp.exp(m - mn)
                    l = alpha * l + jnp.sum(p, axis=1, keepdims=True)
                    acc = alpha * acc + _dot(p.astype(BF16), v_ref[pl.ds(rk, tb_), sl])
                    return mn, l, acc

                init = (jnp.full((tb_, 1), NEG, F32), jnp.zeros((tb_, 1), F32), jnp.zeros((tb_, HEAD_DIM), F32))
                carry = lax.fori_loop(0, qi, functools.partial(step, masked=False), init)
                m, l, acc = step(qi, carry, True)
                o_ref[pl.ds(r, tb_), sl] = (acc / l).astype(BF16)
                lse_ref[pl.ds(r, tb_), sl] = jnp.broadcast_to(m + jnp.log(l), (tb_, HEAD_DIM))
            return carry0

        lax.fori_loop(0, nb, qloop, 0)

    blk = lambda off: pl.BlockSpec((t, 128), lambda b, hp: (b, off + hp))
    return pl.pallas_call(
        body, name="fox_attn_fwd", grid=(bl, 8),
        in_specs=[blk(0), blk(8), blk(16),
                  pl.BlockSpec((None, None, nb, 8, tb_), lambda b, hp: (b, hp, 0, 0, 0))],
        out_specs=[blk(0), blk(0)],
        out_shape=[jax.ShapeDtypeStruct((n, ATT_WIDTH), BF16), jax.ShapeDtypeStruct((n, ATT_WIDTH), F32)],
        compiler_params=_cparams(("parallel", "parallel")),
    )(qkv, qkv, qkv, negc)


def _attn_bwd(qkv, do, o, lse, negc, bl, t):
    n = bl * t
    tb_ = ATT_BLOCK
    nb = t // tb_
    scale = 1.0 / math.sqrt(HEAD_DIM)

    def body(q_ref, k_ref, v_ref, do_ref, o_ref, lse_ref, c_ref, dq_ref, dk_ref, dv_ref, dc_ref, dr_ref, dq_scr):
        row = lax.broadcasted_iota(jnp.int32, (tb_, tb_), 0)
        col = lax.broadcasted_iota(jnp.int32, (tb_, tb_), 1)
        causal = row >= col
        dq_scr[...] = jnp.zeros_like(dq_scr)
        dc_ref[...] = jnp.zeros_like(dc_ref)
        dr_ref[...] = jnp.zeros_like(dr_ref)

        def kvloop(kj, carry0):
            rk = pl.multiple_of(kj * tb_, tb_)
            for j in range(2):
                sl = slice(j * HEAD_DIM, (j + 1) * HEAD_DIM)
                kb = k_ref[pl.ds(rk, tb_), sl]
                vb = v_ref[pl.ds(rk, tb_), sl]
                negc_row = c_ref[kj][j:j + 1, :]

                def step(qi, carry, masked):
                    dk, dv, dcr = carry
                    r = pl.multiple_of(qi * tb_, tb_)
                    qj = q_ref[pl.ds(r, tb_), sl]
                    doj = do_ref[pl.ds(r, tb_), sl]
                    delta = jnp.sum(doj.astype(F32) * o_ref[pl.ds(r, tb_), sl].astype(F32), axis=1, keepdims=True)
                    s = _dot(qj, kb, NT_DIMS) * scale + negc_row
                    if masked:
                        s = jnp.where(causal, s, NEG)
                    p = jnp.exp(s - lse_ref[pl.ds(r, tb_), j * HEAD_DIM:j * HEAD_DIM + 1])
                    dp = _dot(doj, vb, NT_DIMS)
                    ds = p * (dp - delta)
                    dsb = ds.astype(BF16)
                    dv = dv + _dot(p.astype(BF16), doj, TN_DIMS)
                    dk = dk + _dot(dsb, qj, TN_DIMS)
                    dq_scr[pl.ds(r, tb_), sl] += _dot(dsb, kb)
                    dr_ref[pl.ds(r, tb_), sl] += jnp.broadcast_to(jnp.sum(ds, axis=1, keepdims=True), (tb_, HEAD_DIM))
                    dcr = dcr - jnp.sum(ds, axis=0, keepdims=True)
                    return dk, dv, dcr

                init = (jnp.zeros((tb_, HEAD_DIM), F32), jnp.zeros((tb_, HEAD_DIM), F32), jnp.zeros((1, tb_), F32))
                carry = step(kj, init, True)
                dk, dv, dcr = lax.fori_loop(kj + 1, nb, functools.partial(step, masked=False), carry)
                dk_ref[pl.ds(rk, tb_), sl] = (dk * scale).astype(BF16)
                dv_ref[pl.ds(rk, tb_), sl] = dv.astype(BF16)
                dc_ref[kj, j:j + 1, :] = dcr
            return carry0

        lax.fori_loop(0, nb, kvloop, 0)
        dq_ref[...] = (dq_scr[...] * scale).astype(BF16)

    blk = lambda off: pl.BlockSpec((t, 128), lambda b, hp: (b, off + hp))
    cblk = pl.BlockSpec((None, None, nb, 8, tb_), lambda b, hp: (b, hp, 0, 0, 0))
    return pl.pallas_call(
        body, name="fox_attn_bwd", grid=(bl, 8),
        in_specs=[blk(0), blk(8), blk(16), blk(0), blk(0), blk(0), cblk],
        out_specs=[blk(0), blk(0), blk(0), cblk, blk(0)],
        out_shape=[jax.ShapeDtypeStruct((n, ATT_WIDTH), BF16)] * 3
        + [jax.ShapeDtypeStruct((bl, 8, nb, 8, tb_), F32), jax.ShapeDtypeStruct((n, ATT_WIDTH), F32)],
        scratch_shapes=[pltpu.VMEM((t, 128), F32)],
        compiler_params=_cparams(("parallel", "parallel")),
    )(qkv, qkv, qkv, do, o, lse, negc)


def _adamw(w, g, m, v, *, name):
    r, c = w.shape
    tr = _pick(r, (256, 128, 64, 32, 16, 8))
    bc1 = 1.0 - ADAM_B1 ** ADAM_STEP
    bc2 = 1.0 - ADAM_B2 ** ADAM_STEP

    def body(w_ref, g_ref, m_ref, v_ref, d_ref, nm_ref, nv_ref):
        gv = g_ref[...]
        mn = ADAM_B1 * m_ref[...] + (1.0 - ADAM_B1) * gv
        vn = ADAM_B2 * v_ref[...] + (1.0 - ADAM_B2) * (gv * gv)
        m_hat = mn / bc1
        v_hat = vn / bc2
        d_ref[...] = -ADAM_LR * (m_hat / (jnp.sqrt(v_hat) + ADAM_EPS) + ADAM_WD * w_ref[...])
        nm_ref[...] = mn
        nv_ref[...] = vn

    blk = pl.BlockSpec((tr, c), lambda i: (i, 0))
    return pl.pallas_call(
        body, name=name, grid=(r // tr,), in_specs=[blk] * 4, out_specs=[blk] * 3,
        out_shape=[jax.ShapeDtypeStruct((r, c), F32)] * 3,
        compiler_params=_cparams(("parallel",)),
    )(w, g, m, v)


def _sum_leading(parts, *, name, out_dtype=F32):
    k, r, c = parts.shape
    tr = _pick(r, (512, 256, 128, 96, 64, 32, 16, 8))

    def body(p_ref, o_ref):
        acc = p_ref[0].astype(F32)
        for i in range(1, k):
            acc = acc + p_ref[i].astype(F32)
        o_ref[...] = acc.astype(out_dtype)

    return pl.pallas_call(
        body, name=name, grid=(r // tr,),
        in_specs=[pl.BlockSpec((k, tr, c), lambda i: (0, i, 0))],
        out_specs=pl.BlockSpec((tr, c), lambda i: (i, 0)),
        out_shape=jax.ShapeDtypeStruct((r, c), out_dtype),
        compiler_params=_cparams(("parallel",)),
    )(parts)


def _add_pair(a, b, *, name):
    k, r, c = a.shape
    tr = _pick(r, (512, 256, 128))

    def body(a_ref, b_ref, o_ref):
        o_ref[...] = (a_ref[...].astype(F32) + b_ref[...].astype(F32)).astype(BF16)

    blk = pl.BlockSpec((None, tr, c), lambda j, i: (j, i, 0))
    return pl.pallas_call(
        body, name=name, grid=(k, r // tr), in_specs=[blk, blk], out_specs=blk,
        out_shape=jax.ShapeDtypeStruct((k, r, c), BF16),
        compiler_params=_cparams(("parallel", "parallel")),
    )(a, b)


ANY = pl.BlockSpec(memory_space=pl.ANY)


def _chip_peers(x, y):
    return [(1 - x, y, 2 * (1 - x) + y), (x, 1 - y, 2 * x + 1 - y), (1 - x, 1 - y, 2 * (1 - x) + 1 - y)]


def _gather_weights(blob):
    def body(b_ref, o_ref, send_sems, recv_sems, local_sem):
        x, y, c = lax.axis_index("x"), lax.axis_index("y"), lax.axis_index("c")
        me = 2 * x + y
        sibling = (x, y, 1 - c)
        peers = _chip_peers(x, y)

        def half(chip, hc):
            return o_ref.at[chip, pl.ds(hc * HALF_ROWS, HALF_ROWS), :]

        def copy(k, src, chip, hc, to):
            return pltpu.make_async_remote_copy(src_ref=src, dst_ref=half(chip, hc), send_sem=send_sems.at[k],
                                                recv_sem=recv_sems.at[k], device_id=to, device_id_type=MESH)

        mine = pltpu.make_async_copy(b_ref, o_ref.at[me], local_sem)
        mine.start()
        my_half = b_ref.at[pl.ds(c * HALF_ROWS, HALF_ROWS), :]
        first = [copy(k, my_half, me, c, (px, py, c)) for k, (px, py, _) in enumerate(peers)]
        for cp in first:
            cp.start()
        passed = [copy(3 + k, half(pc, c), pc, c, sibling) for k, (_, _, pc) in enumerate(peers)]
        for k, (px, py, pc) in enumerate(peers):
            copy(k, my_half, pc, c, (px, py, c)).wait_recv()
            passed[k].start()
        for k, (_, _, pc) in enumerate(peers):
            copy(3 + k, half(pc, 1 - c), pc, 1 - c, sibling).wait_recv()
        for cp in first + passed:
            cp.wait_send()
        mine.wait()

    return pl.pallas_call(
        body, name="gather_weights", in_specs=[ANY], out_specs=ANY,
        out_shape=jax.ShapeDtypeStruct((N_CHIPS, BLOB_ROWS, 1024), BF16),
        scratch_shapes=[pltpu.SemaphoreType.DMA((6,)), pltpu.SemaphoreType.DMA((6,)), pltpu.SemaphoreType.DMA],
    )(blob)


def _swap_halves(g):
    def body(g_ref, o_ref, send_sem, recv_sem):
        x, y, c = lax.axis_index("x"), lax.axis_index("y"), lax.axis_index("c")
        cp = pltpu.make_async_remote_copy(
            src_ref=g_ref.at[:, pl.ds((1 - c) * HALF_ROWS, HALF_ROWS), :], dst_ref=o_ref,
            send_sem=send_sem, recv_sem=recv_sem, device_id=(x, y, 1 - c), device_id_type=MESH)
        cp.start()
        cp.wait()

    return pl.pallas_call(
        body, name="grad_swap_halves", in_specs=[ANY], out_specs=ANY,
        out_shape=jax.ShapeDtypeStruct((N_CHIPS, HALF_ROWS, 1024), BF16),
        scratch_shapes=[pltpu.SemaphoreType.DMA, pltpu.SemaphoreType.DMA],
    )(g)


def _exchange_chips(p):
    def body(p_ref, o_ref, send_sems, recv_sems, local_sem):
        x, y, c = lax.axis_index("x"), lax.axis_index("y"), lax.axis_index("c")
        me = 2 * x + y
        peers = _chip_peers(x, y)
        mine = pltpu.make_async_copy(p_ref.at[me], o_ref.at[me], local_sem)
        mine.start()
        cps = [pltpu.make_async_remote_copy(src_ref=p_ref.at[pc], dst_ref=o_ref.at[me], send_sem=send_sems.at[k],
                                            recv_sem=recv_sems.at[k], device_id=(px, py, c), device_id_type=MESH)
               for k, (px, py, pc) in enumerate(peers)]
        for cp in cps:
            cp.start()
        for k, (px, py, pc) in enumerate(peers):
            pltpu.make_async_remote_copy(src_ref=p_ref.at[pc], dst_ref=o_ref.at[pc], send_sem=send_sems.at[k],
                                         recv_sem=recv_sems.at[k], device_id=(px, py, c),
                                         device_id_type=MESH).wait_recv()
        for cp in cps:
            cp.wait_send()
        mine.wait()

    return pl.pallas_call(
        body, name="grad_exchange_chips", in_specs=[ANY], out_specs=ANY,
        out_shape=jax.ShapeDtypeStruct((N_CHIPS, HALF_ROWS, 1024), BF16),
        scratch_shapes=[pltpu.SemaphoreType.DMA((3,)), pltpu.SemaphoreType.DMA((3,)), pltpu.SemaphoreType.DMA],
    )(p)


def _join_halves(gh):
    def body(g_ref, o_ref, send_sem, recv_sem, local_sem):
        x, y, c = lax.axis_index("x"), lax.axis_index("y"), lax.axis_index("c")
        mine = pltpu.make_async_copy(g_ref, o_ref.at[pl.ds(c * HALF_ROWS, HALF_ROWS), :], local_sem)
        mine.start()
        cp = pltpu.make_async_remote_copy(
            src_ref=g_ref, dst_ref=o_ref.at[pl.ds(c * HALF_ROWS, HALF_ROWS), :],
            send_sem=send_sem, recv_sem=recv_sem, device_id=(x, y, 1 - c), device_id_type=MESH)
        cp.start()
        pltpu.make_async_remote_copy(
            src_ref=g_ref, dst_ref=o_ref.at[pl.ds((1 - c) * HALF_ROWS, HALF_ROWS), :],
            send_sem=send_sem, recv_sem=recv_sem, device_id=(x, y, 1 - c), device_id_type=MESH).wait_recv()
        cp.wait_send()
        mine.wait()

    return pl.pallas_call(
        body, name="grad_join_halves", in_specs=[ANY], out_specs=ANY,
        out_shape=jax.ShapeDtypeStruct((BLOB_ROWS, 1024), F32),
        scratch_shapes=[pltpu.SemaphoreType.DMA, pltpu.SemaphoreType.DMA, pltpu.SemaphoreType.DMA],
    )(gh)


def _gather_small(s, *, name):
    rows = s.shape[0]

    def body(s_ref, o_ref, send_sems, recv_sems, local_sem):
        x, y, c = lax.axis_index("x"), lax.axis_index("y"), lax.axis_index("c")
        me = 4 * x + 2 * y + c
        mine = pltpu.make_async_copy(s_ref, o_ref.at[me], local_sem)
        mine.start()
        peers = []
        for k in range(1, 8):
            peers.append((1 - x if k & 4 else x, 1 - y if k & 2 else y, 1 - c if k & 1 else c))
        cps = [pltpu.make_async_remote_copy(src_ref=s_ref, dst_ref=o_ref.at[me], send_sem=send_sems.at[k],
                                            recv_sem=recv_sems.at[k], device_id=p, device_id_type=MESH)
               for k, p in enumerate(peers)]
        for cp in cps:
            cp.start()
        for k, (px, py, pc) in enumerate(peers):
            pltpu.make_async_remote_copy(src_ref=s_ref, dst_ref=o_ref.at[4 * px + 2 * py + pc],
                                         send_sem=send_sems.at[k], recv_sem=recv_sems.at[k],
                                         device_id=(px, py, pc), device_id_type=MESH).wait_recv()
        for cp in cps:
            cp.wait_send()
        mine.wait()

    return pl.pallas_call(
        body, name=name, in_specs=[ANY], out_specs=ANY,
        out_shape=jax.ShapeDtypeStruct((8, rows, 128), F32),
        scratch_shapes=[pltpu.SemaphoreType.DMA((7,)), pltpu.SemaphoreType.DMA((7,)), pltpu.SemaphoreType.DMA],
    )(s)


SHARD_ROWS = (IN_WIDTH // N_CHIPS, 512, 1024, 1024)


def _pack_shard(w_in_s, w_out_s, w_up_s, w_down_s, dtype):
    parts = [w_in_s.reshape(-1, 1024), w_out_s.reshape(-1, 1024), w_up_s.reshape(-1, 1024),
             w_down_s.reshape(-1, 1024)]
    used = sum(SHARD_ROWS)
    parts.append(jnp.zeros((BLOB_ROWS - used, 1024), parts[0].dtype))
    return jnp.concatenate(parts, axis=0).astype(dtype)


def _unpack_shard(blob):
    r0 = 0
    out = []
    for rows, shape in zip(SHARD_ROWS, ((D_MODEL, IN_WIDTH // N_CHIPS), (512, D_MODEL), (D_MODEL, D_FF // N_CHIPS),
                                        (D_FF // N_CHIPS, D_MODEL))):
        out.append(blob[r0:r0 + rows].reshape(shape))
        r0 += rows
    return out


def _full_weights(gathered):
    per = [_unpack_shard(gathered[j]) for j in range(N_CHIPS)]
    w_in = jnp.concatenate([p[0] for p in per], axis=1)
    w_out = jnp.concatenate([p[1] for p in per], axis=0)
    w_up = jnp.concatenate([p[2] for p in per], axis=1)
    w_down = jnp.concatenate([p[3] for p in per], axis=0)
    return w_in, w_out, w_up, w_down


def _split_w_in(w_in):
    z_xbc = w_in[:, 0:2560]
    dt = w_in[:, 2560:2576]
    qkv = w_in[:, 2576:5648]
    f = w_in[:, 5648:5664]
    pad = jnp.zeros((w_in.shape[0], PA_WIDTH - 2592), w_in.dtype)
    return jnp.concatenate([z_xbc, dt, f, pad], axis=1), qkv


def _merge_w_in(d_a, d_qkv):
    return jnp.concatenate([d_a[:, 0:2560], d_a[:, 2560:2576], d_qkv, d_a[:, 2576:2592]], axis=1)


def _local_step(x3, target3, w_in, w_out, w_up, w_down, norm_mix_w, conv_w, conv_b, dt_bias, a_log, d_skip,
                ssd_norm_w, f_bias, norm_mlp_w, norm_final_w):
    bl, t, d = x3.shape
    n = bl * t
    x = x3.reshape(n, d)
    target = target3.reshape(n, d)
    w_a, w_qkv = _split_w_in(w_in)
    wo_s, wo_a = w_out[:SSD_WIDTH], w_out[SSD_WIDTH:]
    nfw = norm_final_w.reshape(1, d)
    dskip_e = jnp.repeat(d_skip, HEAD_DIM, axis=1)
    nb = t // ATT_BLOCK

    h0, rstd0 = _rmsnorm_fwd(x, norm_mix_w, name="norm_mix_fwd")
    proj_a = _mm(h0, w_a, name="proj_a")
    qkv = _mm(h0, w_qkv, name="proj_qkv", out_dtype=BF16)
    bias128 = jnp.concatenate([dt_bias, f_bias, jnp.zeros((1, 96), F32)], axis=1)
    alog128 = jnp.concatenate([a_log, jnp.zeros((1, 112), F32)], axis=1)
    dt, sig, acum, ccum, sigf = _prep(proj_a, bias128, alog128, bl, t)
    acum_t = acum.T
    negc = -ccum.reshape(bl, nb, ATT_BLOCK, 8, 2).transpose(0, 3, 1, 4, 2)
    negc = jnp.pad(negc, ((0, 0), (0, 0), (0, 0), (0, 6), (0, 0)))
    xc = _conv_fwd(proj_a, conv_w, conv_b, bl, t)
    y_ssd, y_pre, hprev = _ssd_fwd(xc, proj_a, dt, acum, acum_t, dskip_e, ssd_norm_w, bl, t)
    y_att, lse = _attn_fwd(qkv, negc, bl, t)
    t1 = _mm(y_ssd, wo_s, name="out_proj_ssd", res=x)
    h1 = _mm(y_att, wo_a, name="out_proj_att", res=t1)
    h1n, rstd1 = _rmsnorm_fwd(h1, norm_mlp_w, name="norm_mlp_fwd")
    up = _mm(h1n, w_up, name="mlp_up")
    h2 = _mm(up, w_down, name="mlp_down", a_act="relu2", res=h1)
    dh2, loss, d_nfw = _final(h2, nfw, target)

    dup = _mm(dh2, w_down, name="mlp_down_bwd_act", tb=True, epi_up=up, out_dtype=BF16)
    d_w_down = _mm(up, dh2, name="mlp_down_bwd_w", ta=True, a_act="relu2")
    dh1n = _mm(dup, w_up, name="mlp_up_bwd_act", tb=True)
    d_w_up = _mm(h1n, dup, name="mlp_up_bwd_w", ta=True)
    dh1, d_nmlp = _rmsnorm_bwd(dh1n, h1, rstd1, norm_mlp_w, dh2, name="norm_mlp_bwd")
    dys = _mm(dh1, wo_s, name="out_proj_bwd_ssd", tb=True)
    do = _mm(dh1, wo_a, name="out_proj_bwd_att", tb=True, out_dtype=BF16)
    d_w_out = jnp.concatenate([_mm(y_ssd, dh1, name="out_proj_bwd_w_ssd", ta=True),
                               _mm(y_att, dh1, name="out_proj_bwd_w_att", ta=True)], axis=0)
    dq, dk, dv, dcb, drow = _attn_bwd(qkv, do, y_att, lse, negc, bl, t)
    dc_keys = jnp.pad(dcb[:, :, :, 0:2, :].transpose(0, 2, 4, 1, 3).reshape(n, 16), ((0, 0), (0, 112)))
    dc_queries = jnp.pad(drow.reshape(n, 16, HEAD_DIM)[:, :, 0], ((0, 0), (0, 112)))
    df_raw, d_fb = _fpost(dc_keys, dc_queries, sigf, bl, t)
    dxc, dz, ddt_raw, d_snw, d_dsk, d_alog, d_dtb = _ssd_bwd(dys, xc, proj_a, y_pre, hprev, dt, sig, acum, acum_t,
                                                            a_log, dskip_e, ssd_norm_w, bl, t)
    dxbc, d_conv_w, d_conv_b = _conv_bwd(dxc, proj_a, conv_w, conv_b, bl, t)
    dproj_a = jnp.concatenate([dz, dxbc, ddt_raw.astype(BF16), df_raw.astype(BF16),
                               jnp.zeros((n, PA_WIDTH - 2592), BF16)], axis=1)
    dqkv = jnp.concatenate([dq, dk, dv], axis=1)
    d_w_a = _mm(h0, dproj_a, name="proj_a_bwd_w", ta=True)
    d_w_qkv = _mm(h0, dqkv, name="proj_qkv_bwd_w", ta=True)
    t2 = _mm(dproj_a, w_a, name="proj_a_bwd_act", tb=True)
    dh0 = _mm(dqkv, w_qkv, name="proj_qkv_bwd_act", tb=True, res=t2)
    dx, d_nmix = _rmsnorm_bwd(dh0, x, rstd0, norm_mix_w, dh1, name="norm_mix_bwd")

    grads = dict(norm_mix_w=d_nmix, w_in=_merge_w_in(d_w_a, d_w_qkv), conv_w=d_conv_w, conv_b=d_conv_b,
                 dt_bias=d_dtb, a_log=d_alog, d_skip=d_dsk, ssd_norm_w=d_snw, f_bias=d_fb, w_out=d_w_out,
                 norm_mlp_w=d_nmlp, w_up=d_w_up, w_down=d_w_down, norm_final_w=d_nfw)
    return dx.reshape(bl, t, d), loss, grads


SMALL_ORDER = ("norm_mix_w", "conv_w", "conv_b", "dt_bias", "a_log", "d_skip", "ssd_norm_w", "f_bias",
               "norm_mlp_w", "norm_final_w")
SMALL_SIZES = (1024, 4 * CONV_CH, CONV_CH, 16, 16, 16, 1024, 16, 1024, 1024)


def _pack_small(vals, rows):
    flat = jnp.concatenate([v.reshape(-1).astype(F32) for v in vals])
    return jnp.pad(flat, (0, rows * 128 - flat.shape[0])).reshape(rows, 128)


def _unpack_small(packed, sizes):
    flat = packed.reshape(-1)
    out, o = [], 0
    for s in sizes:
        out.append(flat[o:o + s])
        o += s
    return out


def kernel(x, norm_mix_w, w_in, conv_w, conv_b, dt_bias, a_log, d_skip, ssd_norm_w, f_bias, w_out, norm_mlp_w, w_up, w_down, norm_final_w, loss_target, m_norm_mix_w, m_w_in, m_conv_w, m_conv_b, m_dt_bias, m_a_log, m_d_skip, m_ssd_norm_w, m_f_bias, m_w_out, m_norm_mlp_w, m_w_up, m_w_down, m_norm_final_w, v_norm_mix_w, v_w_in, v_conv_w, v_conv_b, v_dt_bias, v_a_log, v_d_skip, v_ssd_norm_w, v_f_bias, v_w_out, v_norm_mlp_w, v_w_up, v_w_down, v_norm_final_w):
    chip = 2 * lax.axis_index("x") + lax.axis_index("y")
    cw = CONV_CH // N_CHIPS

    blob = _pack_shard(w_in[0], w_out[0], w_up[0], w_down[0], BF16)
    gathered = _gather_weights(blob)
    w_in_f, w_out_f, w_up_f, w_down_f = _full_weights(gathered)
    small_all = _gather_small(_pack_small([conv_w[0]], 16), name="gather_conv_w")
    conv_w_f = jnp.concatenate([small_all[2 * j].reshape(-1)[:4 * cw].reshape(4, cw) for j in range(N_CHIPS)], axis=1)

    dx, loss_part, g = _local_step(x, loss_target, w_in_f, w_out_f, w_up_f, w_down_f, norm_mix_w, conv_w_f,
                                   conv_b, dt_bias, a_log, d_skip, ssd_norm_w, f_bias, norm_mlp_w, norm_final_w)

    gblob = jnp.stack([_pack_shard(g["w_in"][:, j * (IN_WIDTH // N_CHIPS):(j + 1) * (IN_WIDTH // N_CHIPS)],
                                   g["w_out"][j * 512:(j + 1) * 512],
                                   g["w_up"][:, j * 1024:(j + 1) * 1024],
                                   g["w_down"][j * 1024:(j + 1) * 1024], BF16) for j in range(N_CHIPS)])
    c = lax.axis_index("c")
    from_sibling = _swap_halves(gblob)
    my_half = lax.dynamic_slice_in_dim(gblob, c * HALF_ROWS, HALF_ROWS, axis=1)
    chip_part = _add_pair(my_half, from_sibling, name="grad_add_sibling")
    parts = _exchange_chips(chip_part)
    g_half = _sum_leading(parts, name="grad_sum_chips")
    g_shard = _join_halves(g_half)
    g_w_in, g_w_out, g_w_up, g_w_down = _unpack_shard(g_shard)

    small_vals = [g[k] for k in SMALL_ORDER] + [loss_part[:, 0:1]]
    small_sum = _sum_leading(_gather_small(_pack_small(small_vals, SMALL_ROWS), name="gather_small_grads"), name="small_sum")
    sg = dict(zip(SMALL_ORDER + ("loss",), _unpack_small(small_sum, SMALL_SIZES + (1,))))
    loss = sg["loss"].reshape(())
    g_conv_full = sg["conv_w"].reshape(4, CONV_CH)
    g_conv = lax.dynamic_slice_in_dim(g_conv_full, chip * cw, cw, axis=1)

    grads = dict(norm_mix_w=sg["norm_mix_w"].reshape(1, -1), w_in=g_w_in[None], conv_w=g_conv[None],
                 conv_b=sg["conv_b"].reshape(1, -1), dt_bias=sg["dt_bias"].reshape(1, -1),
                 a_log=sg["a_log"].reshape(1, -1), d_skip=sg["d_skip"].reshape(1, -1),
                 ssd_norm_w=sg["ssd_norm_w"].reshape(1, -1), f_bias=sg["f_bias"].reshape(1, -1), w_out=g_w_out[None],
                 norm_mlp_w=sg["norm_mlp_w"].reshape(1, -1), w_up=g_w_up[None], w_down=g_w_down[None],
                 norm_final_w=sg["norm_final_w"])
    weights = dict(norm_mix_w=norm_mix_w, w_in=w_in, conv_w=conv_w, conv_b=conv_b, dt_bias=dt_bias, a_log=a_log,
                   d_skip=d_skip, ssd_norm_w=ssd_norm_w, f_bias=f_bias, w_out=w_out, norm_mlp_w=norm_mlp_w,
                   w_up=w_up, w_down=w_down, norm_final_w=norm_final_w)
    ms = dict(norm_mix_w=m_norm_mix_w, w_in=m_w_in, conv_w=m_conv_w, conv_b=m_conv_b, dt_bias=m_dt_bias,
              a_log=m_a_log, d_skip=m_d_skip, ssd_norm_w=m_ssd_norm_w, f_bias=m_f_bias, w_out=m_w_out,
              norm_mlp_w=m_norm_mlp_w, w_up=m_w_up, w_down=m_w_down, norm_final_w=m_norm_final_w)
    vs = dict(norm_mix_w=v_norm_mix_w, w_in=v_w_in, conv_w=v_conv_w, conv_b=v_conv_b, dt_bias=v_dt_bias,
              a_log=v_a_log, d_skip=v_d_skip, ssd_norm_w=v_ssd_norm_w, f_bias=v_f_bias, w_out=v_w_out,
              norm_mlp_w=v_norm_mlp_w, w_up=v_w_up, w_down=v_w_down, norm_final_w=v_norm_final_w)
    names = list(weights)
    big = ("w_in", "w_out", "w_up", "w_down")
    delta, new_m, new_v = {}, {}, {}
    for k in big:
        shp = weights[k].shape
        two_d = lambda a: a.reshape(shp[-2], shp[-1])
        d_, m_, v_ = _adamw(two_d(weights[k]), two_d(grads[k]), two_d(ms[k]), two_d(vs[k]), name="adamw_" + k)
        delta[k], new_m[k], new_v[k] = d_.reshape(shp), m_.reshape(shp), v_.reshape(shp)
    smalls = [k for k in names if k not in big]
    sizes = [math.prod(weights[k].shape) for k in smalls]
    rows = -(-sum(sizes) // 1024) * 8
    packs = [_pack_small([d[k] for k in smalls], rows) for d in (weights, grads, ms, vs)]
    outs = _adamw(*packs, name="adamw_small")
    for o, dst in zip(outs, (delta, new_m, new_v)):
        for k, val in zip(smalls, _unpack_small(o, sizes)):
            dst[k] = val.reshape(weights[k].shape)
    return (loss, dx, *[grads[k] for k in names], *[delta[k] for k in names], *[new_m[k] for k in names],
            *[new_v[k] for k in names])
```

```python
import functools
import math

import jax
import jax.numpy as jnp
from jax import lax
from jax.experimental import pallas as pl
from jax.experimental.pallas import tpu as pltpu

F32 = jnp.float32
BF16 = jnp.bfloat16
HIGHEST = lax.Precision.HIGHEST
MESH = pl.DeviceIdType.MESH

D_MODEL = 1024
SSD_HEADS = 16
HEAD_DIM = 64
SSD_WIDTH = 1024
SSD_STATE = 128
CONV_CH = 1536
CHUNK = 128
ATT_WIDTH = 1024
EPS = 1e-5
IN_WIDTH = 5664
PA_WIDTH = 2688
QKV_WIDTH = 3072
D_FF = 4096
ATT_BLOCK = 256
NEG = -1e30
VMEM_LIMIT = 48 * 1024 * 1024

ADAM_LR = 0.001
ADAM_B1 = 0.9
ADAM_B2 = 0.999
ADAM_EPS = 1e-08
ADAM_WD = 0.01
ADAM_STEP = 10

N_CHIPS = 4
BLOB_ROWS = 4096
HALF_ROWS = BLOB_ROWS // 2
SMALL_ROWS = 96


def _cparams(sem):
    return pltpu.CompilerParams(dimension_semantics=sem, vmem_limit_bytes=VMEM_LIMIT)


def _pick(n, cands):
    for c in cands:
        if n % c == 0:
            return c
    return n


def _mm(a, b, *, name, ta=False, tb=False, out_dtype=F32, res=None, a_act=None, epi_up=None):
    if ta:
        K, M = a.shape
    else:
        M, K = a.shape
    if tb:
        N, K2 = b.shape
    else:
        K2, N = b.shape
    assert K == K2, (a.shape, b.shape)
    tm = _pick(M, (512, 256, 128))
    tn = _pick(N, (512, 384, 256, 128))
    tk = _pick(K, (1024, 896, 512, 384, 256, 128))
    nk = K // tk
    dn = (((0 if ta else 1,), (1 if tb else 0,)), ((), ()))
    has_res = res is not None
    has_up = epi_up is not None

    def body(*refs):
        a_ref, b_ref = refs[0], refs[1]
        i = 2
        res_ref = up_ref = None
        if has_res:
            res_ref = refs[i]
            i += 1
        if has_up:
            up_ref = refs[i]
            i += 1
        o_ref, acc_ref = refs[i], refs[i + 1]
        k = pl.program_id(2)

        @pl.when(k == 0)
        def _():
            acc_ref[...] = jnp.zeros_like(acc_ref)

        av = a_ref[...]
        if a_act == "relu2":
            r = jnp.maximum(av.astype(F32), 0.0)
            av = r * r
        acc_ref[...] += lax.dot_general(av.astype(BF16), b_ref[...].astype(BF16), dn,
                                        preferred_element_type=F32)

        @pl.when(k == nk - 1)
        def _():
            out = acc_ref[...]
            if has_res:
                out = out + res_ref[...].astype(F32)
            if has_up:
                out = out * (2.0 * jnp.maximum(up_ref[...].astype(F32), 0.0))
            o_ref[...] = out.astype(out_dtype)

    a_spec = pl.BlockSpec((tk, tm), lambda i, j, k: (k, i)) if ta else pl.BlockSpec((tm, tk), lambda i, j, k: (i, k))
    b_spec = pl.BlockSpec((tn, tk), lambda i, j, k: (j, k)) if tb else pl.BlockSpec((tk, tn), lambda i, j, k: (k, j))
    o_spec = pl.BlockSpec((tm, tn), lambda i, j, k: (i, j))
    ins, specs = [a, b], [a_spec, b_spec]
    if has_res:
        ins.append(res)
        specs.append(o_spec)
    if has_up:
        ins.append(epi_up)
        specs.append(o_spec)
    return pl.pallas_call(
        body, name=name, grid=(M // tm, N // tn, nk),
        in_specs=specs, out_specs=o_spec,
        out_shape=jax.ShapeDtypeStruct((M, N), out_dtype),
        scratch_shapes=[pltpu.VMEM((tm, tn), F32)],
        compiler_params=_cparams(("parallel", "parallel", "arbitrary")),
    )(*ins)


def _rmsnorm_fwd(x, w, *, name):
    n, d = x.shape
    tm = _pick(n, (512, 256, 128))

    def body(x_ref, w_ref, y_ref, r_ref):
        xv = x_ref[...]
        rstd = lax.rsqrt(jnp.mean(xv * xv, axis=1, keepdims=True) + EPS)
        y_ref[...] = (xv * rstd * w_ref[...]).astype(BF16)
        r_ref[...] = rstd

    return pl.pallas_call(
        body, name=name, grid=(n // tm,),
        in_specs=[pl.BlockSpec((tm, d), lambda i: (i, 0)), pl.BlockSpec((1, d), lambda i: (0, 0))],
        out_specs=[pl.BlockSpec((tm, d), lambda i: (i, 0)), pl.BlockSpec((tm, 1), lambda i: (i, 0))],
        out_shape=[jax.ShapeDtypeStruct((n, d), BF16), jax.ShapeDtypeStruct((n, 1), F32)],
        compiler_params=_cparams(("parallel",)),
    )(x, w)


def _rmsnorm_bwd(dyn, x, rstd, w, dres, *, name):
    n, d = x.shape
    tm = _pick(n, (512, 256, 128))

    def body(g_ref, x_ref, r_ref, w_ref, d_ref, dx_ref, dw_ref):
        @pl.when(pl.program_id(0) == 0)
        def _():
            dw_ref[...] = jnp.zeros_like(dw_ref)

        g = g_ref[...]
        r = r_ref[...]
        xhat = x_ref[...] * r
        gw = g * w_ref[...]
        dx_ref[...] = d_ref[...] + r * (gw - xhat * jnp.mean(gw * xhat, axis=1, keepdims=True))
        dw_ref[...] += jnp.sum(g * xhat, axis=0, keepdims=True)

    row = pl.BlockSpec((tm, d), lambda i: (i, 0))
    vec = pl.BlockSpec((1, d), lambda i: (0, 0))
    return pl.pallas_call(
        body, name=name, grid=(n // tm,),
        in_specs=[row, row, pl.BlockSpec((tm, 1), lambda i: (i, 0)), vec, row],
        out_specs=[row, vec],
        out_shape=[jax.ShapeDtypeStruct((n, d), F32), jax.ShapeDtypeStruct((1, d), F32)],
        compiler_params=_cparams(("arbitrary",)),
    )(dyn, x, rstd, w, dres)


def _final(h2, w, target):
    n, d = h2.shape
    tm = _pick(n, (512, 256, 128))

    def body(h_ref, w_ref, t_ref, dh_ref, loss_ref, dw_ref):
        @pl.when(pl.program_id(0) == 0)
        def _():
            loss_ref[...] = jnp.zeros_like(loss_ref)
            dw_ref[...] = jnp.zeros_like(dw_ref)

        hv = h_ref[...]
        wv = w_ref[...]
        rstd = lax.rsqrt(jnp.mean(hv * hv, axis=1, keepdims=True) + EPS)
        xhat = hv * rstd
        err = xhat * wv - t_ref[...]
        part = jnp.sum(jnp.mean(err * err, axis=1, keepdims=True), axis=0, keepdims=True)
        loss_ref[...] += 0.5 * part
        dy = err * (1.0 / d)
        gw = dy * wv
        dh_ref[...] = rstd * (gw - xhat * jnp.mean(gw * xhat, axis=1, keepdims=True))
        dw_ref[...] += jnp.sum(dy * xhat, axis=0, keepdims=True)

    row = pl.BlockSpec((tm, d), lambda i: (i, 0))
    vec = pl.BlockSpec((1, d), lambda i: (0, 0))
    return pl.pallas_call(
        body, name="final_norm_loss", grid=(n // tm,),
        in_specs=[row, vec, row],
        out_specs=[row, pl.BlockSpec((1, 128), lambda i: (0, 0)), vec],
        out_shape=[jax.ShapeDtypeStruct((n, d), F32), jax.ShapeDtypeStruct((1, 128), F32),
                   jax.ShapeDtypeStruct((1, d), F32)],
        compiler_params=_cparams(("arbitrary",)),
    )(h2, w, target)


def _softplus(x):
    return jnp.maximum(x, 0.0) + jnp.log(1.0 + jnp.exp(-jnp.abs(x)))


def _prep(proj_a, bias128, alog128, bl, t):
    n = bl * t
    nch = t // CHUNK
    col0 = (SSD_WIDTH + CONV_CH) // 128

    def body(p_ref, b_ref, al_ref, dt_ref, sg_ref, ac_ref, c_ref, sf_ref, carry):
        @pl.when(pl.program_id(1) == 0)
        def _():
            carry[...] = jnp.zeros_like(carry)

        xv = p_ref[...] + b_ref[...]
        sp = _softplus(xv)
        a = -jnp.exp(al_ref[...]) * sp
        logf = -_softplus(-xv)
        row = lax.broadcasted_iota(jnp.int32, (CHUNK, CHUNK), 0)
        col = lax.broadcasted_iota(jnp.int32, (CHUNK, CHUNK), 1)
        tril = (row >= col).astype(F32)
        acum = jnp.dot(tril, a, precision=HIGHEST, preferred_element_type=F32)
        c = jnp.dot(tril, logf, precision=HIGHEST, preferred_element_type=F32) + carry[...]
        carry[...] = c[CHUNK - 1:CHUNK, :]
        dt_ref[...] = sp[:, 0:16]
        sg_ref[...] = jax.nn.sigmoid(xv)[:, 0:16]
        ac_ref[...] = acum[:, 0:16]
        c_ref[...] = c[:, 16:32]
        sf_ref[...] = jax.nn.sigmoid(-xv)[:, 16:32]

    o16 = pl.BlockSpec((CHUNK, 16), lambda b, c: (b * nch + c, 0))
    v128 = pl.BlockSpec((1, 128), lambda b, c: (0, 0))
    return pl.pallas_call(
        body, name="head_scalars", grid=(bl, nch),
        in_specs=[pl.BlockSpec((CHUNK, 128), lambda b, c: (b * nch + c, col0)), v128, v128],
        out_specs=[o16] * 5,
        out_shape=[jax.ShapeDtypeStruct((n, 16), F32)] * 5,
        scratch_shapes=[pltpu.VMEM((1, 128), F32)],
        compiler_params=_cparams(("parallel", "arbitrary")),
    )(proj_a, bias128, alog128)


def _fpost(dck, dcq, sigf, bl, t):
    n = bl * t
    nch = t // CHUNK

    def body(dc_ref, dq_ref, sf_ref, df_ref, db_ref, carry):
        @pl.when(pl.program_id(1) == 0)
        def _():
            carry[...] = jnp.zeros_like(carry)

        @pl.when((pl.program_id(0) == 0) & (pl.program_id(1) == 0))
        def _():
            db_ref[...] = jnp.zeros_like(db_ref)

        row = lax.broadcasted_iota(jnp.int32, (CHUNK, CHUNK), 0)
        col = lax.broadcasted_iota(jnp.int32, (CHUNK, CHUNK), 1)
        triu = (row <= col).astype(F32)
        dlf = jnp.dot(triu, dc_ref[...] + dq_ref[...], precision=HIGHEST, preferred_element_type=F32) + carry[...]
        carry[...] = dlf[0:1, :]
        df = dlf[:, 0:16] * sf_ref[...]
        df_ref[...] = df
        db_ref[...] += jnp.sum(df, axis=0, keepdims=True)

    rev = lambda b, c: (b * nch + nch - 1 - c, 0)
    blk = pl.BlockSpec((CHUNK, 16), rev)
    return pl.pallas_call(
        body, name="forget_gate_bwd", grid=(bl, nch),
        in_specs=[pl.BlockSpec((CHUNK, 128), rev), pl.BlockSpec((CHUNK, 128), rev), blk],
        out_specs=[blk, pl.BlockSpec((1, 16), lambda b, c: (0, 0))],
        out_shape=[jax.ShapeDtypeStruct((n, 16), F32), jax.ShapeDtypeStruct((1, 16), F32)],
        scratch_shapes=[pltpu.VMEM((1, 128), F32)],
        compiler_params=_cparams(("arbitrary", "arbitrary")),
    )(dck, dcq, sigf)


CONV_TILE = 256
CONV_ROWS = 256


def _conv_taps(u_ref, i, w, bias):
    r0 = pl.multiple_of(i * CONV_ROWS, CONV_ROWS)
    cur = u_ref[pl.ds(r0, CONV_ROWS), :]
    p0 = pl.multiple_of(jnp.maximum(r0 - 8, 0), 8)
    prev = jnp.where(i > 0, u_ref[pl.ds(p0, 8), :], 0.0)
    cat = jnp.concatenate([prev, cur], axis=0)
    pre = bias + w[3:4, :] * cur
    taps = [cur]
    for s in (1, 2, 3):
        sh = pltpu.roll(cat, s, 0)[8:, :]
        taps.append(sh)
        pre = pre + w[3 - s:4 - s, :] * sh
    return r0, pre, taps


def _conv_fwd(proj_a, conv_w, conv_b, bl, t):
    n = bl * t
    nct = CONV_CH // CONV_TILE
    c0 = SSD_WIDTH // CONV_TILE

    def body(u_ref, w_ref, b_ref, o_ref):
        w = w_ref[...]
        bias = b_ref[...]

        def chunk(i, carry):
            r0, pre, _ = _conv_taps(u_ref, i, w, bias)
            o_ref[pl.ds(r0, CONV_ROWS), :] = pre * jax.nn.sigmoid(pre)
            return carry

        lax.fori_loop(0, t // CONV_ROWS, chunk, 0)

    return pl.pallas_call(
        body, name="conv_silu_fwd", grid=(bl, nct),
        in_specs=[pl.BlockSpec((t, CONV_TILE), lambda b, c: (b, c0 + c)),
                  pl.BlockSpec((4, CONV_TILE), lambda b, c: (0, c)),
                  pl.BlockSpec((1, CONV_TILE), lambda b, c: (0, c))],
        out_specs=pl.BlockSpec((t, CONV_TILE), lambda b, c: (b, c)),
        out_shape=jax.ShapeDtypeStruct((n, CONV_CH), F32),
        compiler_params=_cparams(("parallel", "parallel")),
    )(proj_a, conv_w, conv_b)


def _conv_bwd(dxc, proj_a, conv_w, conv_b, bl, t):
    n = bl * t
    nct = CONV_CH // CONV_TILE
    c0 = SSD_WIDTH // CONV_TILE
    nrc = t // CONV_ROWS

    def body(g_ref, u_ref, w_ref, b_ref, du_ref, dw_ref, db_ref, dp_scr):
        @pl.when(pl.program_id(1) == 0)
        def _():
            dw_ref[...] = jnp.zeros_like(dw_ref)
            db_ref[...] = jnp.zeros_like(db_ref)

        w = w_ref[...]
        bias = b_ref[...]
        dp_scr[pl.ds(t, 8), :] = jnp.zeros((8, CONV_TILE), F32)

        def chunk1(i, carry):
            dw0, dw1, dw2, dw3, db = carry
            r0, pre, taps = _conv_taps(u_ref, i, w, bias)
            sg = jax.nn.sigmoid(pre)
            dpre = g_ref[pl.ds(r0, CONV_ROWS), :] * (sg * (1.0 + pre * (1.0 - sg)))
            dp_scr[pl.ds(r0, CONV_ROWS), :] = dpre
            dw3 = dw3 + jnp.sum(dpre * taps[0], axis=0, keepdims=True)
            dw2 = dw2 + jnp.sum(dpre * taps[1], axis=0, keepdims=True)
            dw1 = dw1 + jnp.sum(dpre * taps[2], axis=0, keepdims=True)
            dw0 = dw0 + jnp.sum(dpre * taps[3], axis=0, keepdims=True)
            db = db + jnp.sum(dpre, axis=0, keepdims=True)
            return dw0, dw1, dw2, dw3, db

        z = jnp.zeros((1, CONV_TILE), F32)
        dw0, dw1, dw2, dw3, db = lax.fori_loop(0, nrc, chunk1, (z, z, z, z, z))
        dw_ref[...] += jnp.concatenate([dw0, dw1, dw2, dw3], axis=0)
        db_ref[...] += db

        def chunk2(i, carry):
            r0 = pl.multiple_of(i * CONV_ROWS, CONV_ROWS)
            cat = dp_scr[pl.ds(r0, CONV_ROWS + 8), :]
            du = w[3:4, :] * cat[:CONV_ROWS, :]
            for s in (1, 2, 3):
                du = du + w[3 - s:4 - s, :] * pltpu.roll(cat, CONV_ROWS + 8 - s, 0)[:CONV_ROWS, :]
            du_ref[pl.ds(r0, CONV_ROWS), :] = du.astype(BF16)
            return carry

        lax.fori_loop(0, nrc, chunk2, 0)

    return pl.pallas_call(
        body, name="conv_silu_bwd", grid=(nct, bl),
        in_specs=[pl.BlockSpec((t, CONV_TILE), lambda c, b: (b, c)),
                  pl.BlockSpec((t, CONV_TILE), lambda c, b: (b, c0 + c)),
                  pl.BlockSpec((4, CONV_TILE), lambda c, b: (0, c)),
                  pl.BlockSpec((1, CONV_TILE), lambda c, b: (0, c))],
        out_specs=[pl.BlockSpec((t, CONV_TILE), lambda c, b: (b, c)),
                   pl.BlockSpec((4, CONV_TILE), lambda c, b: (0, c)),
                   pl.BlockSpec((1, CONV_TILE), lambda c, b: (0, c))],
        out_shape=[jax.ShapeDtypeStruct((n, CONV_CH), BF16), jax.ShapeDtypeStruct((4, CONV_CH), F32),
                   jax.ShapeDtypeStruct((1, CONV_CH), F32)],
        scratch_shapes=[pltpu.VMEM((t + 8, CONV_TILE), F32)],
        compiler_params=_cparams(("parallel", "arbitrary")),
    )(dxc, proj_a, conv_w, conv_b)


NT_DIMS = (((1,), (1,)), ((), ()))
TN_DIMS = (((0,), (0,)), ((), ()))


def _dot(a, b, dims=None):
    if dims is None:
        return jnp.dot(a, b, preferred_element_type=F32)
    return lax.dot_general(a, b, dims, preferred_element_type=F32)


def _ssd_fwd(xc, proj_a, dt, acum, acum_t, dskip_e, norm_w, bl, t):
    n = bl * t
    nch = t // CHUNK
    L = CHUNK

    def body(xc_ref, z_ref, dt_ref, ac_ref, act_ref, dsk_ref, nw_ref, ys_ref, yp_ref, hp_ref, h_scr, y_scr):
        @pl.when(pl.program_id(1) == 0)
        def _():
            h_scr[...] = jnp.zeros_like(h_scr)

        row = lax.broadcasted_iota(jnp.int32, (L, L), 0)
        col = lax.broadcasted_iota(jnp.int32, (L, L), 1)
        causal = row >= col
        dt_all = dt_ref[...]
        ac_all = ac_ref[...]
        act_all = act_ref[...]
        for g in range(2):
            bg = xc_ref[:, SSD_WIDTH + g * 128:SSD_WIDTH + (g + 1) * 128].astype(BF16)
            cg = xc_ref[:, SSD_WIDTH + 256 + g * 128:SSD_WIDTH + 256 + (g + 1) * 128].astype(BF16)
            gmat = _dot(cg, bg, NT_DIMS)
            for r in range(8):
                h = g * 8 + r
                sl = slice(h * HEAD_DIM, (h + 1) * HEAD_DIM)
                xs = xc_ref[:, sl]
                xdt = xs * dt_all[:, h:h + 1]
                ac = ac_all[:, h:h + 1]
                ar = act_all[h:h + 1, :]
                ldec = jnp.exp(jnp.where(causal, ac - ar, NEG))
                m = (gmat * ldec).astype(BF16)
                hp = h_scr[h]
                hp_ref[h] = hp
                yd = _dot(m, xdt.astype(BF16))
                yo = _dot(cg, hp.astype(BF16), NT_DIMS) * jnp.exp(ac)
                y_scr[:, sl] = yd + yo + dsk_ref[:, sl] * xs
                alast = ac_all[L - 1:L, h:h + 1]
                xd = (xdt * jnp.exp(alast - ac)).astype(BF16)
                h_scr[h] = jnp.exp(alast) * hp + _dot(xd, bg, TN_DIMS)
        y = y_scr[...]
        yp_ref[...] = y
        zv = z_ref[...]
        yg = y * (zv * jax.nn.sigmoid(zv))
        for g in range(2):
            gs = slice(g * 512, (g + 1) * 512)
            grp = yg[:, gs]
            rstd = lax.rsqrt(jnp.mean(grp * grp, axis=1, keepdims=True) + EPS)
            ys_ref[:, gs] = (grp * rstd * nw_ref[:, gs]).astype(BF16)

    rb = lambda b, c: (b * nch + c, 0)
    v1k = pl.BlockSpec((1, SSD_WIDTH), lambda b, c: (0, 0))
    return pl.pallas_call(
        body, name="ssd_fwd", grid=(bl, nch),
        in_specs=[pl.BlockSpec((L, CONV_CH), rb), pl.BlockSpec((L, SSD_WIDTH), rb),
                  pl.BlockSpec((L, 16), rb), pl.BlockSpec((L, 16), rb),
                  pl.BlockSpec((16, L), lambda b, c: (0, b * nch + c)), v1k, v1k],
        out_specs=[pl.BlockSpec((L, SSD_WIDTH), rb), pl.BlockSpec((L, SSD_WIDTH), rb),
                   pl.BlockSpec((None, 16, HEAD_DIM, SSD_STATE), lambda b, c: (b * nch + c, 0, 0, 0))],
        out_shape=[jax.ShapeDtypeStruct((n, SSD_WIDTH), BF16), jax.ShapeDtypeStruct((n, SSD_WIDTH), F32),
                   jax.ShapeDtypeStruct((bl * nch, 16, HEAD_DIM, SSD_STATE), F32)],
        scratch_shapes=[pltpu.VMEM((16, HEAD_DIM, SSD_STATE), F32), pltpu.VMEM((L, SSD_WIDTH), F32)],
        compiler_params=_cparams(("parallel", "arbitrary")),
    )(xc, proj_a, dt, acum, acum_t, dskip_e, norm_w)


def _ssd_bwd(dys, xc, proj_a, ypre, hprev, dt, sig, acum, acum_t, a_log, dskip_e, norm_w, bl, t):
    n = bl * t
    nch = t // CHUNK
    L = CHUNK

    def body(dys_ref, xc_ref, z_ref, yp_ref, hp_ref, dt_ref, sg_ref, ac_ref, act_ref, al_ref, dsk_ref, nw_ref,
             dxc_ref, dz_ref, ddt_ref, dnw_ref, dsk16_ref, da16_ref, db16_ref, dh_scr, dy_scr):
        first = (pl.program_id(0) == 0) & (pl.program_id(1) == 0)

        @pl.when(first)
        def _():
            dnw_ref[...] = jnp.zeros_like(dnw_ref)
            dsk16_ref[...] = jnp.zeros_like(dsk16_ref)
            da16_ref[...] = jnp.zeros_like(da16_ref)
            db16_ref[...] = jnp.zeros_like(db16_ref)

        @pl.when(pl.program_id(1) == 0)
        def _():
            dh_scr[...] = jnp.zeros_like(dh_scr)

        y = yp_ref[...]
        zv = z_ref[...]
        sz = jax.nn.sigmoid(zv)
        gate = zv * sz
        yg = y * gate
        dout = dys_ref[...]
        nw = nw_ref[...]
        for g in range(2):
            gs = slice(g * 512, (g + 1) * 512)
            grp = yg[:, gs]
            rstd = lax.rsqrt(jnp.mean(grp * grp, axis=1, keepdims=True) + EPS)
            ghat = grp * rstd
            dnw_ref[:, gs] += jnp.sum(dout[:, gs] * ghat, axis=0, keepdims=True)
            gw = dout[:, gs] * nw[:, gs]
            dyg = rstd * (gw - ghat * jnp.mean(gw * ghat, axis=1, keepdims=True))
            dy_scr[:, gs] = dyg * gate[:, gs]
            dz_ref[:, gs] = (dyg * y[:, gs] * (sz[:, gs] * (1.0 + zv[:, gs] * (1.0 - sz[:, gs])))).astype(BF16)

        row = lax.broadcasted_iota(jnp.int32, (L, L), 0)
        col = lax.broadcasted_iota(jnp.int32, (L, L), 1)
        causal = row >= col
        lane16 = lax.broadcasted_iota(jnp.int32, (1, 16), 1)
        lane128 = lax.broadcasted_iota(jnp.int32, (1, L), 1)
        last_row = lax.broadcasted_iota(jnp.int32, (L, 1), 0) == (L - 1)
        dt_all = dt_ref[...]
        ac_all = ac_ref[...]
        act_all = act_ref[...]
        dac_col = jnp.zeros((L, L), F32)
        dac_row = jnp.zeros((L, L), F32)
        ddt_x = jnp.zeros((L, L), F32)
        dsk16 = jnp.zeros((1, 16), F32)
        rows16 = lax.broadcasted_iota(jnp.int32, (L, 1), 0)
        for g in range(2):
            bsl = slice(SSD_WIDTH + g * 128, SSD_WIDTH + (g + 1) * 128)
            csl = slice(SSD_WIDTH + 256 + g * 128, SSD_WIDTH + 256 + (g + 1) * 128)
            bg = xc_ref[:, bsl].astype(BF16)
            cg = xc_ref[:, csl].astype(BF16)
            gmat = _dot(cg, bg, NT_DIMS)
            dg_sum = jnp.zeros((L, L), F32)
            dc_acc = jnp.zeros((L, SSD_STATE), F32)
            db_acc = jnp.zeros((L, SSD_STATE), F32)
            for r in range(8):
                h = g * 8 + r
                sl = slice(h * HEAD_DIM, (h + 1) * HEAD_DIM)
                onehot = lane16 == h
                onehot_w = lane128 == h
                xs = xc_ref[:, sl]
                dth = dt_all[:, h:h + 1]
                xdt = xs * dth
                xb = xdt.astype(BF16)
                ac = ac_all[:, h:h + 1]
                ar = act_all[h:h + 1, :]
                alast = ac_all[L - 1:L, h:h + 1]
                ldec = jnp.exp(jnp.where(causal, ac - ar, NEG))
                mf = gmat * ldec
                e_in = jnp.exp(ac)
                dec = jnp.exp(alast - ac)
                elast = jnp.exp(alast)
                hp = hp_ref[h]
                hpb = hp.astype(BF16)
                dyh = dy_scr[:, sl]
                dyb = dyh.astype(BF16)
                dsk16 = dsk16 + jnp.where(onehot, jnp.sum(jnp.sum(dyh * xs, axis=1, keepdims=True), axis=0, keepdims=True), 0.0)
                dm = _dot(dyb, xb, NT_DIMS)
                dx = _dot(mf.astype(BF16), dyb, TN_DIMS)
                dg_sum = dg_sum + dm * ldec
                wmat = dm * mf
                dac_h = jnp.sum(wmat, axis=1, keepdims=True)
                dac_row = dac_row + jnp.where(rows16 == h, -jnp.sum(wmat, axis=0, keepdims=True), 0.0)
                ch = _dot(cg, hpb, NT_DIMS)
                dye = dyh * e_in
                dyeb = dye.astype(BF16)
                dc_acc = dc_acc + _dot(dyeb, hpb)
                dhp = _dot(dyeb, cg, TN_DIMS)
                dac_h = dac_h + jnp.sum(dye * ch, axis=1, keepdims=True)
                ds = dh_scr[h]
                dsb = ds.astype(BF16)
                dxd = _dot(bg, dsb, NT_DIMS)
                db_acc = db_acc + _dot((xdt * dec).astype(BF16), dsb)
                dx = dx + dxd * dec
                ddec = jnp.sum(dxd * xdt, axis=1, keepdims=True) * dec
                extra = (jnp.sum(ddec, axis=0, keepdims=True)
                         + elast * jnp.sum(jnp.sum(hp * ds, axis=1, keepdims=True), axis=0, keepdims=True))
                dac_h = dac_h - ddec + jnp.where(last_row, extra, 0.0)
                dh_scr[h] = elast * ds + dhp
                dac_col = dac_col + jnp.where(onehot_w, dac_h, 0.0)
                ddt_x = ddt_x + jnp.where(onehot_w, jnp.sum(dx * xs, axis=1, keepdims=True), 0.0)
                dxc_ref[:, sl] = dx * dth + dsk_ref[:, sl] * dyh
            dgb = dg_sum.astype(BF16)
            dxc_ref[:, csl] = dc_acc + _dot(dgb, bg)
            dxc_ref[:, bsl] = db_acc + _dot(dgb, cg, TN_DIMS)
        dac = dac_col + jnp.transpose(dac_row)
        triu = (row <= col).astype(F32)
        da = jnp.dot(triu, dac, precision=HIGHEST, preferred_element_type=F32)[:, 0:16]
        a_row = -jnp.exp(al_ref[...])
        ddt = (ddt_x[:, 0:16] + da * a_row) * sg_ref[...]
        ddt_ref[...] = ddt
        dsk16_ref[...] += dsk16
        da16_ref[...] += jnp.sum(da * dt_all, axis=0, keepdims=True) * a_row
        db16_ref[...] += jnp.sum(ddt, axis=0, keepdims=True)

    rb = lambda b, c: (b * nch + nch - 1 - c, 0)
    v1k = pl.BlockSpec((1, SSD_WIDTH), lambda b, c: (0, 0))
    v16 = pl.BlockSpec((1, 16), lambda b, c: (0, 0))
    wide = pl.BlockSpec((L, SSD_WIDTH), rb)
    s16 = pl.BlockSpec((L, 16), rb)
    return pl.pallas_call(
        body, name="ssd_bwd", grid=(bl, nch),
        in_specs=[wide, pl.BlockSpec((L, CONV_CH), rb), wide, wide,
                  pl.BlockSpec((None, 16, HEAD_DIM, SSD_STATE), lambda b, c: (b * nch + nch - 1 - c, 0, 0, 0)),
                  s16, s16, s16, pl.BlockSpec((16, L), lambda b, c: (0, b * nch + nch - 1 - c)), v16, v1k, v1k],
        out_specs=[pl.BlockSpec((L, CONV_CH), rb), wide, s16, v1k, v16, v16, v16],
        out_shape=[jax.ShapeDtypeStruct((n, CONV_CH), F32), jax.ShapeDtypeStruct((n, SSD_WIDTH), BF16),
                   jax.ShapeDtypeStruct((n, 16), F32), jax.ShapeDtypeStruct((1, SSD_WIDTH), F32),
                   jax.ShapeDtypeStruct((1, 16), F32), jax.ShapeDtypeStruct((1, 16), F32),
                   jax.ShapeDtypeStruct((1, 16), F32)],
        scratch_shapes=[pltpu.VMEM((16, HEAD_DIM, SSD_STATE), F32), pltpu.VMEM((L, SSD_WIDTH), F32)],
        compiler_params=_cparams(("arbitrary", "arbitrary")),
    )(dys, xc, proj_a, ypre, hprev, dt, sig, acum, acum_t, a_log, dskip_e, norm_w)


def _attn_fwd(qkv, negc, bl, t):
    n = bl * t
    tb_ = ATT_BLOCK
    nb = t // tb_
    scale = 1.0 / math.sqrt(HEAD_DIM)

    def body(q_ref, k_ref, v_ref, c_ref, o_ref, lse_ref):
        row = lax.broadcasted_iota(jnp.int32, (tb_, tb_), 0)
        col = lax.broadcasted_iota(jnp.int32, (tb_, tb_), 1)
        causal = row >= col
        for qi in range(nb):
            r0, lk = qi * tb_, (qi + 1) * tb_
            for j in range(2):
                sl = slice(j * HEAD_DIM, (j + 1) * HEAD_DIM)
                s = _dot(q_ref[r0:lk, sl], k_ref[0:lk, sl], NT_DIMS) * scale + c_ref[j:j + 1, 0:lk]
                tail = jnp.where(causal, s[:, r0:lk], NEG)
                s = tail if qi == 0 else jnp.concatenate([s[:, 0:r0], tail], axis=1)
                m = jnp.max(s, axis=1, keepdims=True)
                p = jnp.exp(s - m)
                l = jnp.sum(p, axis=1, keepdims=True)
                acc = _dot(p.astype(BF16), v_ref[0:lk, sl])
                o_ref[r0:lk, sl] = (acc / l).astype(BF16)
                lse_ref[r0:lk, sl] = jnp.broadcast_to(m + jnp.log(l), (tb_, HEAD_DIM))

    blk = lambda off: pl.BlockSpec((t, 128), lambda b, hp: (b, off + hp))
    return pl.pallas_call(
        body, name="fox_attn_fwd", grid=(bl, 8),
        in_specs=[blk(0), blk(8), blk(16), pl.BlockSpec((None, None, 8, t), lambda b, hp: (b, hp, 0, 0))],
        out_specs=[blk(0), blk(0)],
        out_shape=[jax.ShapeDtypeStruct((n, ATT_WIDTH), BF16), jax.ShapeDtypeStruct((n, ATT_WIDTH), F32)],
        compiler_params=_cparams(("parallel", "parallel")),
    )(qkv, qkv, qkv, negc)


def _attn_bwd(qkv, do, o, lse, negc, bl, t):
    n = bl * t
    tb_ = ATT_BLOCK
    nb = t // tb_
    scale = 1.0 / math.sqrt(HEAD_DIM)

    def body(q_ref, k_ref, v_ref, do_ref, o_ref, lse_ref, c_ref, dq_ref, dk_ref, dv_ref, dc_ref, dr_ref,
             dq_scr, delta_scr):
        row = lax.broadcasted_iota(jnp.int32, (tb_, tb_), 0)
        col = lax.broadcasted_iota(jnp.int32, (tb_, tb_), 1)
        causal = row >= col
        dq_scr[...] = jnp.zeros_like(dq_scr)
        dr_ref[...] = jnp.zeros_like(dr_ref)
        dc_ref[...] = jnp.zeros_like(dc_ref)
        prod = do_ref[...].astype(F32) * o_ref[...].astype(F32)
        for j in range(2):
            sl = slice(j * HEAD_DIM, (j + 1) * HEAD_DIM)
            delta_scr[:, sl] = jnp.broadcast_to(jnp.sum(prod[:, sl], axis=1, keepdims=True), (t, HEAD_DIM))
        for kj in range(nb):
            r0, r1 = kj * tb_, (kj + 1) * tb_
            for j in range(2):
                sl = slice(j * HEAD_DIM, (j + 1) * HEAD_DIM)
                one = slice(j * HEAD_DIM, j * HEAD_DIM + 1)
                kb = k_ref[r0:r1, sl]
                qs = q_ref[r0:t, sl]
                dos = do_ref[r0:t, sl]
                s = _dot(qs, kb, NT_DIMS) * scale + c_ref[j:j + 1, r0:r1]
                head = jnp.where(causal, s[0:tb_, :], NEG)
                s = head if kj == nb - 1 else jnp.concatenate([head, s[tb_:, :]], axis=0)
                p = jnp.exp(s - lse_ref[r0:t, one])
                dp = _dot(dos, v_ref[r0:r1, sl], NT_DIMS)
                ds = p * (dp - delta_scr[r0:t, one])
                dsb = ds.astype(BF16)
                dv_ref[r0:r1, sl] = _dot(p.astype(BF16), dos, TN_DIMS).astype(BF16)
                dk_ref[r0:r1, sl] = (_dot(dsb, qs, TN_DIMS) * scale).astype(BF16)
                dq_scr[r0:t, sl] += _dot(dsb, kb)
                dr_ref[r0:t, sl] += jnp.broadcast_to(jnp.sum(ds, axis=1, keepdims=True), (t - r0, HEAD_DIM))
                dc_ref[j:j + 1, r0:r1] = -jnp.sum(ds, axis=0, keepdims=True)
        dq_ref[...] = (dq_scr[...] * scale).astype(BF16)

    blk = lambda off: pl.BlockSpec((t, 128), lambda b, hp: (b, off + hp))
    cblk = pl.BlockSpec((None, None, 8, t), lambda b, hp: (b, hp, 0, 0))
    return pl.pallas_call(
        body, name="fox_attn_bwd", grid=(bl, 8),
        in_specs=[blk(0), blk(8), blk(16), blk(0), blk(0), blk(0), cblk],
        out_specs=[blk(0), blk(0), blk(0), cblk, blk(0)],
        out_shape=[jax.ShapeDtypeStruct((n, ATT_WIDTH), BF16)] * 3
        + [jax.ShapeDtypeStruct((bl, 8, 8, t), F32), jax.ShapeDtypeStruct((n, ATT_WIDTH), F32)],
        scratch_shapes=[pltpu.VMEM((t, 128), F32), pltpu.VMEM((t, 128), F32)],
        compiler_params=_cparams(("parallel", "parallel")),
    )(qkv, qkv, qkv, do, o, lse, negc)


def _adamw(w, g, m, v, *, name):
    r, c = w.shape
    tr = _pick(r, (256, 128, 64, 32, 16, 8))
    bc1 = 1.0 - ADAM_B1 ** ADAM_STEP
    bc2 = 1.0 - ADAM_B2 ** ADAM_STEP

    def body(w_ref, g_ref, m_ref, v_ref, d_ref, nm_ref, nv_ref):
        gv = g_ref[...]
        mn = ADAM_B1 * m_ref[...] + (1.0 - ADAM_B1) * gv
        vn = ADAM_B2 * v_ref[...] + (1.0 - ADAM_B2) * (gv * gv)
        m_hat = mn / bc1
        v_hat = vn / bc2
        d_ref[...] = -ADAM_LR * (m_hat / (jnp.sqrt(v_hat) + ADAM_EPS) + ADAM_WD * w_ref[...])
        nm_ref[...] = mn
        nv_ref[...] = vn

    blk = pl.BlockSpec((tr, c), lambda i: (i, 0))
    return pl.pallas_call(
        body, name=name, grid=(r // tr,), in_specs=[blk] * 4, out_specs=[blk] * 3,
        out_shape=[jax.ShapeDtypeStruct((r, c), F32)] * 3,
        compiler_params=_cparams(("parallel",)),
    )(w, g, m, v)


def _sum_leading(parts, *, name, out_dtype=F32):
    k, r, c = parts.shape
    tr = _pick(r, (512, 256, 128, 96, 64, 32, 16, 8))

    def body(p_ref, o_ref):
        acc = p_ref[0].astype(F32)
        for i in range(1, k):
            acc = acc + p_ref[i].astype(F32)
        o_ref[...] = acc.astype(out_dtype)

    return pl.pallas_call(
        body, name=name, grid=(r // tr,),
        in_specs=[pl.BlockSpec((k, tr, c), lambda i: (0, i, 0))],
        out_specs=pl.BlockSpec((tr, c), lambda i: (i, 0)),
        out_shape=jax.ShapeDtypeStruct((r, c), out_dtype),
        compiler_params=_cparams(("parallel",)),
    )(parts)


def _add_pair(a, b, *, name):
    k, r, c = a.shape
    tr = _pick(r, (512, 256, 128))

    def body(a_ref, b_ref, o_ref):
        o_ref[...] = (a_ref[...].astype(F32) + b_ref[...].astype(F32)).astype(BF16)

    blk = pl.BlockSpec((None, tr, c), lambda j, i: (j, i, 0))
    return pl.pallas_call(
        body, name=name, grid=(k, r // tr), in_specs=[blk, blk], out_specs=blk,
        out_shape=jax.ShapeDtypeStruct((k, r, c), BF16),
        compiler_params=_cparams(("parallel", "parallel")),
    )(a, b)


ANY = pl.BlockSpec(memory_space=pl.ANY)


def _chip_peers(x, y):
    return [(1 - x, y, 2 * (1 - x) + y), (x, 1 - y, 2 * x + 1 - y), (1 - x, 1 - y, 2 * (1 - x) + 1 - y)]


def _gather_weights(blob):
    def body(b_ref, o_ref, send_sems, recv_sems, local_sem):
        x, y, c = lax.axis_index("x"), lax.axis_index("y"), lax.axis_index("c")
        me = 2 * x + y
        sibling = (x, y, 1 - c)
        peers = _chip_peers(x, y)

        def half(chip, hc):
            return o_ref.at[chip, pl.ds(hc * HALF_ROWS, HALF_ROWS), :]

        def copy(k, src, chip, hc, to):
            return pltpu.make_async_remote_copy(src_ref=src, dst_ref=half(chip, hc), send_sem=send_sems.at[k],
                                                recv_sem=recv_sems.at[k], device_id=to, device_id_type=MESH)

        mine = pltpu.make_async_copy(b_ref, o_ref.at[me], local_sem)
        mine.start()
        my_half = b_ref.at[pl.ds(c * HALF_ROWS, HALF_ROWS), :]
        first = [copy(k, my_half, me, c, (px, py, c)) for k, (px, py, _) in enumerate(peers)]
        for cp in first:
            cp.start()
        passed = [copy(3 + k, half(pc, c), pc, c, sibling) for k, (_, _, pc) in enumerate(peers)]
        for k, (px, py, pc) in enumerate(peers):
            copy(k, my_half, pc, c, (px, py, c)).wait_recv()
            passed[k].start()
        for k, (_, _, pc) in enumerate(peers):
            copy(3 + k, half(pc, 1 - c), pc, 1 - c, sibling).wait_recv()
        for cp in first + passed:
            cp.wait_send()
        mine.wait()

    return pl.pallas_call(
        body, name="gather_weights", in_specs=[ANY], out_specs=ANY,
        out_shape=jax.ShapeDtypeStruct((N_CHIPS, BLOB_ROWS, 1024), BF16),
        scratch_shapes=[pltpu.SemaphoreType.DMA((6,)), pltpu.SemaphoreType.DMA((6,)), pltpu.SemaphoreType.DMA],
    )(blob)


def _swap_halves(g):
    def body(g_ref, o_ref, send_sem, recv_sem):
        x, y, c = lax.axis_index("x"), lax.axis_index("y"), lax.axis_index("c")
        cp = pltpu.make_async_remote_copy(
            src_ref=g_ref.at[:, pl.ds((1 - c) * HALF_ROWS, HALF_ROWS), :], dst_ref=o_ref,
            send_sem=send_sem, recv_sem=recv_sem, device_id=(x, y, 1 - c), device_id_type=MESH)
        cp.start()
        cp.wait()

    return pl.pallas_call(
        body, name="grad_swap_halves", in_specs=[ANY], out_specs=ANY,
        out_shape=jax.ShapeDtypeStruct((N_CHIPS, HALF_ROWS, 1024), BF16),
        scratch_shapes=[pltpu.SemaphoreType.DMA, pltpu.SemaphoreType.DMA],
    )(g)


def _exchange_chips(p):
    def body(p_ref, o_ref, send_sems, recv_sems, local_sem):
        x, y, c = lax.axis_index("x"), lax.axis_index("y"), lax.axis_index("c")
        me = 2 * x + y
        peers = _chip_peers(x, y)
        mine = pltpu.make_async_copy(p_ref.at[me], o_ref.at[me], local_sem)
        mine.start()
        cps = [pltpu.make_async_remote_copy(src_ref=p_ref.at[pc], dst_ref=o_ref.at[me], send_sem=send_sems.at[k],
                                            recv_sem=recv_sems.at[k], device_id=(px, py, c), device_id_type=MESH)
               for k, (px, py, pc) in enumerate(peers)]
        for cp in cps:
            cp.start()
        for k, (px, py, pc) in enumerate(peers):
            pltpu.make_async_remote_copy(src_ref=p_ref.at[pc], dst_ref=o_ref.at[pc], send_sem=send_sems.at[k],
                                         recv_sem=recv_sems.at[k], device_id=(px, py, c),
                                         device_id_type=MESH).wait_recv()
        for cp in cps:
            cp.wait_send()
        mine.wait()

    return pl.pallas_call(
        body, name="grad_exchange_chips", in_specs=[ANY], out_specs=ANY,
        out_shape=jax.ShapeDtypeStruct((N_CHIPS, HALF_ROWS, 1024), BF16),
        scratch_shapes=[pltpu.SemaphoreType.DMA((3,)), pltpu.SemaphoreType.DMA((3,)), pltpu.SemaphoreType.DMA],
    )(p)


def _join_halves(gh):
    def body(g_ref, o_ref, send_sem, recv_sem, local_sem):
        x, y, c = lax.axis_index("x"), lax.axis_index("y"), lax.axis_index("c")
        mine = pltpu.make_async_copy(g_ref, o_ref.at[pl.ds(c * HALF_ROWS, HALF_ROWS), :], local_sem)
        mine.start()
        cp = pltpu.make_async_remote_copy(
            src_ref=g_ref, dst_ref=o_ref.at[pl.ds(c * HALF_ROWS, HALF_ROWS), :],
            send_sem=send_sem, recv_sem=recv_sem, device_id=(x, y, 1 - c), device_id_type=MESH)
        cp.start()
        pltpu.make_async_remote_copy(
            src_ref=g_ref, dst_ref=o_ref.at[pl.ds((1 - c) * HALF_ROWS, HALF_ROWS), :],
            send_sem=send_sem, recv_sem=recv_sem, device_id=(x, y, 1 - c), device_id_type=MESH).wait_recv()
        cp.wait_send()
        mine.wait()

    return pl.pallas_call(
        body, name="grad_join_halves", in_specs=[ANY], out_specs=ANY,
        out_shape=jax.ShapeDtypeStruct((BLOB_ROWS, 1024), F32),
        scratch_shapes=[pltpu.SemaphoreType.DMA, pltpu.SemaphoreType.DMA, pltpu.SemaphoreType.DMA],
    )(gh)


def _gather_small(s, *, name):
    rows = s.shape[0]

    def body(s_ref, o_ref, send_sems, recv_sems, local_sem):
        x, y, c = lax.axis_index("x"), lax.axis_index("y"), lax.axis_index("c")
        me = 4 * x + 2 * y + c
        mine = pltpu.make_async_copy(s_ref, o_ref.at[me], local_sem)
        mine.start()
        peers = []
        for k in range(1, 8):
            peers.append((1 - x if k & 4 else x, 1 - y if k & 2 else y, 1 - c if k & 1 else c))
        cps = [pltpu.make_async_remote_copy(src_ref=s_ref, dst_ref=o_ref.at[me], send_sem=send_sems.at[k],
                                            recv_sem=recv_sems.at[k], device_id=p, device_id_type=MESH)
               for k, p in enumerate(peers)]
        for cp in cps:
            cp.start()
        for k, (px, py, pc) in enumerate(peers):
            pltpu.make_async_remote_copy(src_ref=s_ref, dst_ref=o_ref.at[4 * px + 2 * py + pc],
                                         send_sem=send_sems.at[k], recv_sem=recv_sems.at[k],
                                         device_id=(px, py, pc), device_id_type=MESH).wait_recv()
        for cp in cps:
            cp.wait_send()
        mine.wait()

    return pl.pallas_call(
        body, name=name, in_specs=[ANY], out_specs=ANY,
        out_shape=jax.ShapeDtypeStruct((8, rows, 128), F32),
        scratch_shapes=[pltpu.SemaphoreType.DMA((7,)), pltpu.SemaphoreType.DMA((7,)), pltpu.SemaphoreType.DMA],
    )(s)


SHARD_ROWS = (IN_WIDTH // N_CHIPS, 512, 1024, 1024)


def _pack_shard(w_in_s, w_out_s, w_up_s, w_down_s, dtype):
    parts = [w_in_s.reshape(-1, 1024), w_out_s.reshape(-1, 1024), w_up_s.reshape(-1, 1024),
             w_down_s.reshape(-1, 1024)]
    used = sum(SHARD_ROWS)
    parts.append(jnp.zeros((BLOB_ROWS - used, 1024), parts[0].dtype))
    return jnp.concatenate(parts, axis=0).astype(dtype)


def _unpack_shard(blob):
    r0 = 0
    out = []
    for rows, shape in zip(SHARD_ROWS, ((D_MODEL, IN_WIDTH // N_CHIPS), (512, D_MODEL), (D_MODEL, D_FF // N_CHIPS),
                                        (D_FF // N_CHIPS, D_MODEL))):
        out.append(blob[r0:r0 + rows].reshape(shape))
        r0 += rows
    return out


def _full_weights(gathered):
    per = [_unpack_shard(gathered[j]) for j in range(N_CHIPS)]
    w_in = jnp.concatenate([p[0] for p in per], axis=1)
    w_out = jnp.concatenate([p[1] for p in per], axis=0)
    w_up = jnp.concatenate([p[2] for p in per], axis=1)
    w_down = jnp.concatenate([p[3] for p in per], axis=0)
    return w_in, w_out, w_up, w_down


def _split_w_in(w_in):
    z_xbc = w_in[:, 0:2560]
    dt = w_in[:, 2560:2576]
    qkv = w_in[:, 2576:5648]
    f = w_in[:, 5648:5664]
    pad = jnp.zeros((w_in.shape[0], PA_WIDTH - 2592), w_in.dtype)
    return jnp.concatenate([z_xbc, dt, f, pad], axis=1), qkv


def _merge_w_in(d_a, d_qkv):
    return jnp.concatenate([d_a[:, 0:2560], d_a[:, 2560:2576], d_qkv, d_a[:, 2576:2592]], axis=1)


def _local_step(x3, target3, w_in, w_out, w_up, w_down, norm_mix_w, conv_w, conv_b, dt_bias, a_log, d_skip,
                ssd_norm_w, f_bias, norm_mlp_w, norm_final_w):
    bl, t, d = x3.shape
    n = bl * t
    x = x3.reshape(n, d)
    target = target3.reshape(n, d)
    w_a, w_qkv = _split_w_in(w_in)
    wo_s, wo_a = w_out[:SSD_WIDTH], w_out[SSD_WIDTH:]
    nfw = norm_final_w.reshape(1, d)
    dskip_e = jnp.repeat(d_skip, HEAD_DIM, axis=1)
    nb = t // ATT_BLOCK

    h0, rstd0 = _rmsnorm_fwd(x, norm_mix_w, name="norm_mix_fwd")
    proj_a = _mm(h0, w_a, name="proj_a")
    qkv = _mm(h0, w_qkv, name="proj_qkv", out_dtype=BF16)
    bias128 = jnp.concatenate([dt_bias, f_bias, jnp.zeros((1, 96), F32)], axis=1)
    alog128 = jnp.concatenate([a_log, jnp.zeros((1, 112), F32)], axis=1)
    dt, sig, acum, ccum, sigf = _prep(proj_a, bias128, alog128, bl, t)
    acum_t = acum.T
    negc = jnp.pad(-ccum.reshape(bl, t, 8, 2).transpose(0, 2, 3, 1), ((0, 0), (0, 0), (0, 6), (0, 0)))
    xc = _conv_fwd(proj_a, conv_w, conv_b, bl, t)
    y_ssd, y_pre, hprev = _ssd_fwd(xc, proj_a, dt, acum, acum_t, dskip_e, ssd_norm_w, bl, t)
    y_att, lse = _attn_fwd(qkv, negc, bl, t)
    t1 = _mm(y_ssd, wo_s, name="out_proj_ssd", res=x)
    h1 = _mm(y_att, wo_a, name="out_proj_att", res=t1)
    h1n, rstd1 = _rmsnorm_fwd(h1, norm_mlp_w, name="norm_mlp_fwd")
    up = _mm(h1n, w_up, name="mlp_up")
    h2 = _mm(up, w_down, name="mlp_down", a_act="relu2", res=h1)
    dh2, loss, d_nfw = _final(h2, nfw, target)

    dup = _mm(dh2, w_down, name="mlp_down_bwd_act", tb=True, epi_up=up, out_dtype=BF16)
    d_w_down = _mm(up, dh2, name="mlp_down_bwd_w", ta=True, a_act="relu2")
    dh1n = _mm(dup, w_up, name="mlp_up_bwd_act", tb=True)
    d_w_up = _mm(h1n, dup, name="mlp_up_bwd_w", ta=True)
    dh1, d_nmlp = _rmsnorm_bwd(dh1n, h1, rstd1, norm_mlp_w, dh2, name="norm_mlp_bwd")
    dys = _mm(dh1, wo_s, name="out_proj_bwd_ssd", tb=True)
    do = _mm(dh1, wo_a, name="out_proj_bwd_att", tb=True, out_dtype=BF16)
    d_w_out = jnp.concatenate([_mm(y_ssd, dh1, name="out_proj_bwd_w_ssd", ta=True),
                               _mm(y_att, dh1, name="out_proj_bwd_w_att", ta=True)], axis=0)
    dq, dk, dv, dcb, drow = _attn_bwd(qkv, do, y_att, lse, negc, bl, t)
    dc_keys = jnp.pad(dcb[:, :, 0:2, :].transpose(0, 3, 1, 2).reshape(n, 16), ((0, 0), (0, 112)))
    dc_queries = jnp.pad(drow.reshape(n, 16, HEAD_DIM)[:, :, 0], ((0, 0), (0, 112)))
    df_raw, d_fb = _fpost(dc_keys, dc_queries, sigf, bl, t)
    dxc, dz, ddt_raw, d_snw, d_dsk, d_alog, d_dtb = _ssd_bwd(dys, xc, proj_a, y_pre, hprev, dt, sig, acum, acum_t,
                                                            a_log, dskip_e, ssd_norm_w, bl, t)
    dxbc, d_conv_w, d_conv_b = _conv_bwd(dxc, proj_a, conv_w, conv_b, bl, t)
    dproj_a = jnp.concatenate([dz, dxbc, ddt_raw.astype(BF16), df_raw.astype(BF16),
                               jnp.zeros((n, PA_WIDTH - 2592), BF16)], axis=1)
    dqkv = jnp.concatenate([dq, dk, dv], axis=1)
    d_w_a = _mm(h0, dproj_a, name="proj_a_bwd_w", ta=True)
    d_w_qkv = _mm(h0, dqkv, name="proj_qkv_bwd_w", ta=True)
    t2 = _mm(dproj_a, w_a, name="proj_a_bwd_act", tb=True)
    dh0 = _mm(dqkv, w_qkv, name="proj_qkv_bwd_act", tb=True, res=t2)
    dx, d_nmix = _rmsnorm_bwd(dh0, x, rstd0, norm_mix_w, dh1, name="norm_mix_bwd")

    grads = dict(norm_mix_w=d_nmix, w_in=_merge_w_in(d_w_a, d_w_qkv), conv_w=d_conv_w, conv_b=d_conv_b,
                 dt_bias=d_dtb, a_log=d_alog, d_skip=d_dsk, ssd_norm_w=d_snw, f_bias=d_fb, w_out=d_w_out,
                 norm_mlp_w=d_nmlp, w_up=d_w_up, w_down=d_w_down, norm_final_w=d_nfw)
    return dx.reshape(bl, t, d), loss, grads


SMALL_ORDER = ("norm_mix_w", "conv_w", "conv_b", "dt_bias", "a_log", "d_skip", "ssd_norm_w", "f_bias",
               "norm_mlp_w", "norm_final_w")
SMALL_SIZES = (1024, 4 * CONV_CH, CONV_CH, 16, 16, 16, 1024, 16, 1024, 1024)


def _pack_small(vals, rows):
    flat = jnp.concatenate([v.reshape(-1).astype(F32) for v in vals])
    return jnp.pad(flat, (0, rows * 128 - flat.shape[0])).reshape(rows, 128)


def _unpack_small(packed, sizes):
    flat = packed.reshape(-1)
    out, o = [], 0
    for s in sizes:
        out.append(flat[o:o + s])
        o += s
    return out


def kernel(x, norm_mix_w, w_in, conv_w, conv_b, dt_bias, a_log, d_skip, ssd_norm_w, f_bias, w_out, norm_mlp_w, w_up, w_down, norm_final_w, loss_target, m_norm_mix_w, m_w_in, m_conv_w, m_conv_b, m_dt_bias, m_a_log, m_d_skip, m_ssd_norm_w, m_f_bias, m_w_out, m_norm_mlp_w, m_w_up, m_w_down, m_norm_final_w, v_norm_mix_w, v_w_in, v_conv_w, v_conv_b, v_dt_bias, v_a_log, v_d_skip, v_ssd_norm_w, v_f_bias, v_w_out, v_norm_mlp_w, v_w_up, v_w_down, v_norm_final_w):
    chip = 2 * lax.axis_index("x") + lax.axis_index("y")
    cw = CONV_CH // N_CHIPS

    blob = _pack_shard(w_in[0], w_out[0], w_up[0], w_down[0], BF16)
    gathered = _gather_weights(blob)
    w_in_f, w_out_f, w_up_f, w_down_f = _full_weights(gathered)
    small_all = _gather_small(_pack_small([conv_w[0]], 16), name="gather_conv_w")
    conv_w_f = jnp.concatenate([small_all[2 * j].reshape(-1)[:4 * cw].reshape(4, cw) for j in range(N_CHIPS)], axis=1)

    dx, loss_part, g = _local_step(x, loss_target, w_in_f, w_out_f, w_up_f, w_down_f, norm_mix_w, conv_w_f,
                                   conv_b, dt_bias, a_log, d_skip, ssd_norm_w, f_bias, norm_mlp_w, norm_final_w)

    gblob = jnp.stack([_pack_shard(g["w_in"][:, j * (IN_WIDTH // N_CHIPS):(j + 1) * (IN_WIDTH // N_CHIPS)],
                                   g["w_out"][j * 512:(j + 1) * 512],
                                   g["w_up"][:, j * 1024:(j + 1) * 1024],
                                   g["w_down"][j * 1024:(j + 1) * 1024], BF16) for j in range(N_CHIPS)])
    c = lax.axis_index("c")
    from_sibling = _swap_halves(gblob)
    my_half = lax.dynamic_slice_in_dim(gblob, c * HALF_ROWS, HALF_ROWS, axis=1)
    chip_part = _add_pair(my_half, from_sibling, name="grad_add_sibling")
    parts = _exchange_chips(chip_part)
    g_half = _sum_leading(parts, name="grad_sum_chips")
    g_shard = _join_halves(g_half)
    g_w_in, g_w_out, g_w_up, g_w_down = _unpack_shard(g_shard)

    small_vals = [g[k] for k in SMALL_ORDER] + [loss_part[:, 0:1]]
    small_sum = _sum_leading(_gather_small(_pack_small(small_vals, SMALL_ROWS), name="gather_small_grads"), name="small_sum")
    sg = dict(zip(SMALL_ORDER + ("loss",), _unpack_small(small_sum, SMALL_SIZES + (1,))))
    loss = sg["loss"].reshape(())
    g_conv_full = sg["conv_w"].reshape(4, CONV_CH)
    g_conv = lax.dynamic_slice_in_dim(g_conv_full, chip * cw, cw, axis=1)

    grads = dict(norm_mix_w=sg["norm_mix_w"].reshape(1, -1), w_in=g_w_in[None], conv_w=g_conv[None],
                 conv_b=sg["conv_b"].reshape(1, -1), dt_bias=sg["dt_bias"].reshape(1, -1),
                 a_log=sg["a_log"].reshape(1, -1), d_skip=sg["d_skip"].reshape(1, -1),
                 ssd_norm_w=sg["ssd_norm_w"].reshape(1, -1), f_bias=sg["f_bias"].reshape(1, -1), w_out=g_w_out[None],
                 norm_mlp_w=sg["norm_mlp_w"].reshape(1, -1), w_up=g_w_up[None], w_down=g_w_down[None],
                 norm_final_w=sg["norm_final_w"])
    weights = dict(norm_mix_w=norm_mix_w, w_in=w_in, conv_w=conv_w, conv_b=conv_b, dt_bias=dt_bias, a_log=a_log,
                   d_skip=d_skip, ssd_norm_w=ssd_norm_w, f_bias=f_bias, w_out=w_out, norm_mlp_w=norm_mlp_w,
                   w_up=w_up, w_down=w_down, norm_final_w=norm_final_w)
    ms = dict(norm_mix_w=m_norm_mix_w, w_in=m_w_in, conv_w=m_conv_w, conv_b=m_conv_b, dt_bias=m_dt_bias,
              a_log=m_a_log, d_skip=m_d_skip, ssd_norm_w=m_ssd_norm_w, f_bias=m_f_bias, w_out=m_w_out,
              norm_mlp_w=m_norm_mlp_w, w_up=m_w_up, w_down=m_w_down, norm_final_w=m_norm_final_w)
    vs = dict(norm_mix_w=v_norm_mix_w, w_in=v_w_in, conv_w=v_conv_w, conv_b=v_conv_b, dt_bias=v_dt_bias,
              a_log=v_a_log, d_skip=v_d_skip, ssd_norm_w=v_ssd_norm_w, f_bias=v_f_bias, w_out=v_w_out,
              norm_mlp_w=v_norm_mlp_w, w_up=v_w_up, w_down=v_w_down, norm_final_w=v_norm_final_w)
    names = list(weights)
    big = ("w_in", "w_out", "w_up", "w_down")
    delta, new_m, new_v = {}, {}, {}
    for k in big:
        shp = weights[k].shape
        two_d = lambda a: a.reshape(shp[-2], shp[-1])
        d_, m_, v_ = _adamw(two_d(weights[k]), two_d(grads[k]), two_d(ms[k]), two_d(vs[k]), name="adamw_" + k)
        delta[k], new_m[k], new_v[k] = d_.reshape(shp), m_.reshape(shp), v_.reshape(shp)
    smalls = [k for k in names if k not in big]
    sizes = [math.prod(weights[k].shape) for k in smalls]
    rows = -(-sum(sizes) // 1024) * 8
    packs = [_pack_small([d[k] for k in smalls], rows) for d in (weights, grads, ms, vs)]
    outs = _adamw(*packs, name="adamw_small")
    for o, dst in zip(outs, (delta, new_m, new_v)):
        for k, val in zip(smalls, _unpack_small(o, sizes)):
            dst[k] = val.reshape(weights[k].shape)
    return (loss, dx, *[grads[k] for k in names], *[delta[k] for k in names], *[new_m[k] for k in names],
            *[new_v[k] for k in names])
```

```python
import functools
import math

import jax
import jax.numpy as jnp
from jax import lax
from jax.experimental import pallas as pl
from jax.experimental.pallas import tpu as pltpu

F32 = jnp.float32
BF16 = jnp.bfloat16
HIGHEST = lax.Precision.HIGHEST
MESH = pl.DeviceIdType.MESH

D_MODEL = 1024
SSD_HEADS = 16
HEAD_DIM = 64
SSD_WIDTH = 1024
SSD_STATE = 128
CONV_CH = 1536
CHUNK = 128
ATT_WIDTH = 1024
EPS = 1e-5
IN_WIDTH = 5664
PA_WIDTH = 2688
QKV_WIDTH = 3072
D_FF = 4096
ATT_BLOCK = 256
NEG = -1e30
VMEM_LIMIT = 48 * 1024 * 1024

ADAM_LR = 0.001
ADAM_B1 = 0.9
ADAM_B2 = 0.999
ADAM_EPS = 1e-08
ADAM_WD = 0.01
ADAM_STEP = 10

N_CHIPS = 4
BLOB_ROWS = 4096
HALF_ROWS = BLOB_ROWS // 2
SMALL_ROWS = 96


def _cparams(sem):
    return pltpu.CompilerParams(dimension_semantics=sem, vmem_limit_bytes=VMEM_LIMIT)


def _pick(n, cands):
    for c in cands:
        if n % c == 0:
            return c
    return n


def _mm(a, b, *, name, ta=False, tb=False, out_dtype=F32, res=None, a_act=None, epi_up=None):
    if ta:
        K, M = a.shape
    else:
        M, K = a.shape
    if tb:
        N, K2 = b.shape
    else:
        K2, N = b.shape
    assert K == K2, (a.shape, b.shape)
    tm = _pick(M, (512, 256, 128))
    tn = _pick(N, (512, 384, 256, 128))
    tk = _pick(K, (1024, 896, 512, 384, 256, 128))
    nk = K // tk
    dn = (((0 if ta else 1,), (1 if tb else 0,)), ((), ()))
    has_res = res is not None
    has_up = epi_up is not None

    def body(*refs):
        a_ref, b_ref = refs[0], refs[1]
        i = 2
        res_ref = up_ref = None
        if has_res:
            res_ref = refs[i]
            i += 1
        if has_up:
            up_ref = refs[i]
            i += 1
        o_ref, acc_ref = refs[i], refs[i + 1]
        k = pl.program_id(2)

        @pl.when(k == 0)
        def _():
            acc_ref[...] = jnp.zeros_like(acc_ref)

        av = a_ref[...]
        if a_act == "relu2":
            r = jnp.maximum(av.astype(F32), 0.0)
            av = r * r
        acc_ref[...] += lax.dot_general(av.astype(BF16), b_ref[...].astype(BF16), dn,
                                        preferred_element_type=F32)

        @pl.when(k == nk - 1)
        def _():
            out = acc_ref[...]
            if has_res:
                out = out + res_ref[...].astype(F32)
            if has_up:
                out = out * (2.0 * jnp.maximum(up_ref[...].astype(F32), 0.0))
            o_ref[...] = out.astype(out_dtype)

    a_spec = pl.BlockSpec((tk, tm), lambda i, j, k: (k, i)) if ta else pl.BlockSpec((tm, tk), lambda i, j, k: (i, k))
    b_spec = pl.BlockSpec((tn, tk), lambda i, j, k: (j, k)) if tb else pl.BlockSpec((tk, tn), lambda i, j, k: (k, j))
    o_spec = pl.BlockSpec((tm, tn), lambda i, j, k: (i, j))
    ins, specs = [a, b], [a_spec, b_spec]
    if has_res:
        ins.append(res)
        specs.append(o_spec)
    if has_up:
        ins.append(epi_up)
        specs.append(o_spec)
    return pl.pallas_call(
        body, name=name, grid=(M // tm, N // tn, nk),
        in_specs=specs, out_specs=o_spec,
        out_shape=jax.ShapeDtypeStruct((M, N), out_dtype),
        scratch_shapes=[pltpu.VMEM((tm, tn), F32)],
        compiler_params=_cparams(("parallel", "parallel", "arbitrary")),
    )(*ins)


def _rmsnorm_fwd(x, w, *, name):
    n, d = x.shape
    tm = _pick(n, (512, 256, 128))

    def body(x_ref, w_ref, y_ref, r_ref):
        xv = x_ref[...]
        rstd = lax.rsqrt(jnp.mean(xv * xv, axis=1, keepdims=True) + EPS)
        y_ref[...] = (xv * rstd * w_ref[...]).astype(BF16)
        r_ref[...] = rstd

    return pl.pallas_call(
        body, name=name, grid=(n // tm,),
        in_specs=[pl.BlockSpec((tm, d), lambda i: (i, 0)), pl.BlockSpec((1, d), lambda i: (0, 0))],
        out_specs=[pl.BlockSpec((tm, d), lambda i: (i, 0)), pl.BlockSpec((tm, 1), lambda i: (i, 0))],
        out_shape=[jax.ShapeDtypeStruct((n, d), BF16), jax.ShapeDtypeStruct((n, 1), F32)],
        compiler_params=_cparams(("parallel",)),
    )(x, w)


def _rmsnorm_bwd(dyn, x, rstd, w, dres, *, name):
    n, d = x.shape
    tm = _pick(n, (512, 256, 128))

    def body(g_ref, x_ref, r_ref, w_ref, d_ref, dx_ref, dw_ref):
        @pl.when(pl.program_id(0) == 0)
        def _():
            dw_ref[...] = jnp.zeros_like(dw_ref)

        g = g_ref[...]
        r = r_ref[...]
        xhat = x_ref[...] * r
        gw = g * w_ref[...]
        dx_ref[...] = d_ref[...] + r * (gw - xhat * jnp.mean(gw * xhat, axis=1, keepdims=True))
        dw_ref[...] += jnp.sum(g * xhat, axis=0, keepdims=True)

    row = pl.BlockSpec((tm, d), lambda i: (i, 0))
    vec = pl.BlockSpec((1, d), lambda i: (0, 0))
    return pl.pallas_call(
        body, name=name, grid=(n // tm,),
        in_specs=[row, row, pl.BlockSpec((tm, 1), lambda i: (i, 0)), vec, row],
        out_specs=[row, vec],
        out_shape=[jax.ShapeDtypeStruct((n, d), F32), jax.ShapeDtypeStruct((1, d), F32)],
        compiler_params=_cparams(("arbitrary",)),
    )(dyn, x, rstd, w, dres)


def _final(h2, w, target):
    n, d = h2.shape
    tm = _pick(n, (512, 256, 128))

    def body(h_ref, w_ref, t_ref, dh_ref, loss_ref, dw_ref):
        @pl.when(pl.program_id(0) == 0)
        def _():
            loss_ref[...] = jnp.zeros_like(loss_ref)
            dw_ref[...] = jnp.zeros_like(dw_ref)

        hv = h_ref[...]
        wv = w_ref[...]
        rstd = lax.rsqrt(jnp.mean(hv * hv, axis=1, keepdims=True) + EPS)
        xhat = hv * rstd
        err = xhat * wv - t_ref[...]
        part = jnp.sum(jnp.mean(err * err, axis=1, keepdims=True), axis=0, keepdims=True)
        loss_ref[...] += 0.5 * part
        dy = err * (1.0 / d)
        gw = dy * wv
        dh_ref[...] = rstd * (gw - xhat * jnp.mean(gw * xhat, axis=1, keepdims=True))
        dw_ref[...] += jnp.sum(dy * xhat, axis=0, keepdims=True)

    row = pl.BlockSpec((tm, d), lambda i: (i, 0))
    vec = pl.BlockSpec((1, d), lambda i: (0, 0))
    return pl.pallas_call(
        body, name="final_norm_loss", grid=(n // tm,),
        in_specs=[row, vec, row],
        out_specs=[row, pl.BlockSpec((1, 128), lambda i: (0, 0)), vec],
        out_shape=[jax.ShapeDtypeStruct((n, d), F32), jax.ShapeDtypeStruct((1, 128), F32),
                   jax.ShapeDtypeStruct((1, d), F32)],
        compiler_params=_cparams(("arbitrary",)),
    )(h2, w, target)


def _softplus(x):
    return jnp.maximum(x, 0.0) + jnp.log(1.0 + jnp.exp(-jnp.abs(x)))


def _prep(proj_a, bias128, alog128, bl, t):
    n = bl * t
    nch = t // CHUNK
    col0 = (SSD_WIDTH + CONV_CH) // 128

    def body(p_ref, b_ref, al_ref, dt_ref, sg_ref, ac_ref, c_ref, sf_ref, carry):
        @pl.when(pl.program_id(1) == 0)
        def _():
            carry[...] = jnp.zeros_like(carry)

        xv = p_ref[...] + b_ref[...]
        sp = _softplus(xv)
        a = -jnp.exp(al_ref[...]) * sp
        logf = -_softplus(-xv)
        row = lax.broadcasted_iota(jnp.int32, (CHUNK, CHUNK), 0)
        col = lax.broadcasted_iota(jnp.int32, (CHUNK, CHUNK), 1)
        tril = (row >= col).astype(F32)
        acum = jnp.dot(tril, a, precision=HIGHEST, preferred_element_type=F32)
        c = jnp.dot(tril, logf, precision=HIGHEST, preferred_element_type=F32) + carry[...]
        carry[...] = c[CHUNK - 1:CHUNK, :]
        head_lanes = lax.broadcasted_iota(jnp.int32, (1, 128), 1) < 16
        dt_ref[...] = jnp.where(head_lanes, sp, 0.0)
        sg_ref[...] = jax.nn.sigmoid(xv)[:, 0:16]
        ac_ref[...] = jnp.where(head_lanes, acum, 0.0)
        c_ref[...] = c[:, 16:32]
        sf_ref[...] = jax.nn.sigmoid(-xv)[:, 16:32]

    o16 = pl.BlockSpec((CHUNK, 16), lambda b, c: (b * nch + c, 0))
    o128 = pl.BlockSpec((CHUNK, 128), lambda b, c: (b * nch + c, 0))
    v128 = pl.BlockSpec((1, 128), lambda b, c: (0, 0))
    w16 = jax.ShapeDtypeStruct((n, 16), F32)
    w128 = jax.ShapeDtypeStruct((n, 128), F32)
    return pl.pallas_call(
        body, name="head_scalars", grid=(bl, nch),
        in_specs=[pl.BlockSpec((CHUNK, 128), lambda b, c: (b * nch + c, col0)), v128, v128],
        out_specs=[o128, o16, o128, o16, o16],
        out_shape=[w128, w16, w128, w16, w16],
        scratch_shapes=[pltpu.VMEM((1, 128), F32)],
        compiler_params=_cparams(("parallel", "arbitrary")),
    )(proj_a, bias128, alog128)


def _fpost(dck, dcq, sigf, bl, t):
    n = bl * t
    nch = t // CHUNK

    def body(dc_ref, dq_ref, sf_ref, df_ref, db_ref, carry):
        @pl.when(pl.program_id(1) == 0)
        def _():
            carry[...] = jnp.zeros_like(carry)

        @pl.when((pl.program_id(0) == 0) & (pl.program_id(1) == 0))
        def _():
            db_ref[...] = jnp.zeros_like(db_ref)

        row = lax.broadcasted_iota(jnp.int32, (CHUNK, CHUNK), 0)
        col = lax.broadcasted_iota(jnp.int32, (CHUNK, CHUNK), 1)
        triu = (row <= col).astype(F32)
        dlf = jnp.dot(triu, dc_ref[...] + dq_ref[...], precision=HIGHEST, preferred_element_type=F32) + carry[...]
        carry[...] = dlf[0:1, :]
        df = dlf[:, 0:16] * sf_ref[...]
        df_ref[...] = df
        db_ref[...] += jnp.sum(df, axis=0, keepdims=True)

    rev = lambda b, c: (b * nch + nch - 1 - c, 0)
    blk = pl.BlockSpec((CHUNK, 16), rev)
    return pl.pallas_call(
        body, name="forget_gate_bwd", grid=(bl, nch),
        in_specs=[pl.BlockSpec((CHUNK, 128), rev), pl.BlockSpec((CHUNK, 128), rev), blk],
        out_specs=[blk, pl.BlockSpec((1, 16), lambda b, c: (0, 0))],
        out_shape=[jax.ShapeDtypeStruct((n, 16), F32), jax.ShapeDtypeStruct((1, 16), F32)],
        scratch_shapes=[pltpu.VMEM((1, 128), F32)],
        compiler_params=_cparams(("arbitrary", "arbitrary")),
    )(dck, dcq, sigf)


CONV_TILE = 256
CONV_ROWS = 256


def _conv_taps(u_ref, i, w, bias):
    r0 = pl.multiple_of(i * CONV_ROWS, CONV_ROWS)
    cur = u_ref[pl.ds(r0, CONV_ROWS), :]
    p0 = pl.multiple_of(jnp.maximum(r0 - 8, 0), 8)
    prev = jnp.where(i > 0, u_ref[pl.ds(p0, 8), :], 0.0)
    cat = jnp.concatenate([prev, cur], axis=0)
    pre = bias + w[3:4, :] * cur
    taps = [cur]
    for s in (1, 2, 3):
        sh = pltpu.roll(cat, s, 0)[8:, :]
        taps.append(sh)
        pre = pre + w[3 - s:4 - s, :] * sh
    return r0, pre, taps


def _conv_fwd(proj_a, conv_w, conv_b, bl, t):
    n = bl * t
    nct = CONV_CH // CONV_TILE
    c0 = SSD_WIDTH // CONV_TILE

    def body(u_ref, w_ref, b_ref, o_ref):
        w = w_ref[...]
        bias = b_ref[...]

        def chunk(i, carry):
            r0, pre, _ = _conv_taps(u_ref, i, w, bias)
            o_ref[pl.ds(r0, CONV_ROWS), :] = pre * jax.nn.sigmoid(pre)
            return carry

        lax.fori_loop(0, t // CONV_ROWS, chunk, 0)

    return pl.pallas_call(
        body, name="conv_silu_fwd", grid=(bl, nct),
        in_specs=[pl.BlockSpec((t, CONV_TILE), lambda b, c: (b, c0 + c)),
                  pl.BlockSpec((4, CONV_TILE), lambda b, c: (0, c)),
                  pl.BlockSpec((1, CONV_TILE), lambda b, c: (0, c))],
        out_specs=pl.BlockSpec((t, CONV_TILE), lambda b, c: (b, c)),
        out_shape=jax.ShapeDtypeStruct((n, CONV_CH), F32),
        compiler_params=_cparams(("parallel", "parallel")),
    )(proj_a, conv_w, conv_b)


def _conv_bwd(dxc, proj_a, conv_w, conv_b, bl, t):
    n = bl * t
    nct = CONV_CH // CONV_TILE
    c0 = SSD_WIDTH // CONV_TILE
    nrc = t // CONV_ROWS

    def body(g_ref, u_ref, w_ref, b_ref, du_ref, dw_ref, db_ref, dp_scr):
        @pl.when(pl.program_id(1) == 0)
        def _():
            dw_ref[...] = jnp.zeros_like(dw_ref)
            db_ref[...] = jnp.zeros_like(db_ref)

        w = w_ref[...]
        bias = b_ref[...]
        dp_scr[pl.ds(t, 8), :] = jnp.zeros((8, CONV_TILE), F32)

        def chunk1(i, carry):
            dw0, dw1, dw2, dw3, db = carry
            r0, pre, taps = _conv_taps(u_ref, i, w, bias)
            sg = jax.nn.sigmoid(pre)
            dpre = g_ref[pl.ds(r0, CONV_ROWS), :] * (sg * (1.0 + pre * (1.0 - sg)))
            dp_scr[pl.ds(r0, CONV_ROWS), :] = dpre
            dw3 = dw3 + jnp.sum(dpre * taps[0], axis=0, keepdims=True)
            dw2 = dw2 + jnp.sum(dpre * taps[1], axis=0, keepdims=True)
            dw1 = dw1 + jnp.sum(dpre * taps[2], axis=0, keepdims=True)
            dw0 = dw0 + jnp.sum(dpre * taps[3], axis=0, keepdims=True)
            db = db + jnp.sum(dpre, axis=0, keepdims=True)
            return dw0, dw1, dw2, dw3, db

        z = jnp.zeros((1, CONV_TILE), F32)
        dw0, dw1, dw2, dw3, db = lax.fori_loop(0, nrc, chunk1, (z, z, z, z, z))
        dw_ref[...] += jnp.concatenate([dw0, dw1, dw2, dw3], axis=0)
        db_ref[...] += db

        def chunk2(i, carry):
            r0 = pl.multiple_of(i * CONV_ROWS, CONV_ROWS)
            cat = dp_scr[pl.ds(r0, CONV_ROWS + 8), :]
            du = w[3:4, :] * cat[:CONV_ROWS, :]
            for s in (1, 2, 3):
                du = du + w[3 - s:4 - s, :] * pltpu.roll(cat, CONV_ROWS + 8 - s, 0)[:CONV_ROWS, :]
            du_ref[pl.ds(r0, CONV_ROWS), :] = du.astype(BF16)
            return carry

        lax.fori_loop(0, nrc, chunk2, 0)

    return pl.pallas_call(
        body, name="conv_silu_bwd", grid=(nct, bl),
        in_specs=[pl.BlockSpec((t, CONV_TILE), lambda c, b: (b, c)),
                  pl.BlockSpec((t, CONV_TILE), lambda c, b: (b, c0 + c)),
                  pl.BlockSpec((4, CONV_TILE), lambda c, b: (0, c)),
                  pl.BlockSpec((1, CONV_TILE), lambda c, b: (0, c))],
        out_specs=[pl.BlockSpec((t, CONV_TILE), lambda c, b: (b, c)),
                   pl.BlockSpec((4, CONV_TILE), lambda c, b: (0, c)),
                   pl.BlockSpec((1, CONV_TILE), lambda c, b: (0, c))],
        out_shape=[jax.ShapeDtypeStruct((n, CONV_CH), BF16), jax.ShapeDtypeStruct((4, CONV_CH), F32),
                   jax.ShapeDtypeStruct((1, CONV_CH), F32)],
        scratch_shapes=[pltpu.VMEM((t + 8, CONV_TILE), F32)],
        compiler_params=_cparams(("parallel", "arbitrary")),
    )(dxc, proj_a, conv_w, conv_b)


NT_DIMS = (((1,), (1,)), ((), ()))
TN_DIMS = (((0,), (0,)), ((), ()))


def _dot(a, b, dims=None):
    if dims is None:
        return jnp.dot(a, b, preferred_element_type=F32)
    return lax.dot_general(a, b, dims, preferred_element_type=F32)


def _ssd_fwd(xc, proj_a, dt, acum, acum_t, dskip_e, norm_w, bl, t):
    n = bl * t
    nch = t // CHUNK
    L = CHUNK

    def body(xc_ref, z_ref, dt_ref, ac_ref, act_ref, dsk_ref, nw_ref, ys_ref, yp_ref, hp_ref, h_scr, y_scr):
        @pl.when(pl.program_id(1) == 0)
        def _():
            h_scr[...] = jnp.zeros_like(h_scr)

        row = lax.broadcasted_iota(jnp.int32, (L, L), 0)
        col = lax.broadcasted_iota(jnp.int32, (L, L), 1)
        causal = row >= col
        dt_all = dt_ref[...]
        ac_all = ac_ref[...]
        act_all = act_ref[...]
        for g in range(2):
            bg = xc_ref[:, SSD_WIDTH + g * 128:SSD_WIDTH + (g + 1) * 128].astype(BF16)
            cg = xc_ref[:, SSD_WIDTH + 256 + g * 128:SSD_WIDTH + 256 + (g + 1) * 128].astype(BF16)
            gmat = _dot(cg, bg, NT_DIMS)
            for r in range(8):
                h = g * 8 + r
                sl = slice(h * HEAD_DIM, (h + 1) * HEAD_DIM)
                xs = xc_ref[:, sl]
                xdt = xs * dt_all[:, h:h + 1]
                ac = ac_all[:, h:h + 1]
                ar = act_all[h:h + 1, :]
                ldec = jnp.exp(jnp.where(causal, ac - ar, NEG))
                m = (gmat * ldec).astype(BF16)
                hp = h_scr[h]
                hp_ref[h] = hp
                yd = _dot(m, xdt.astype(BF16))
                yo = _dot(cg, hp.astype(BF16), NT_DIMS) * jnp.exp(ac)
                y_scr[:, sl] = yd + yo + dsk_ref[:, sl] * xs
                alast = ac_all[L - 1:L, h:h + 1]
                xd = (xdt * jnp.exp(alast - ac)).astype(BF16)
                h_scr[h] = jnp.exp(alast) * hp + _dot(xd, bg, TN_DIMS)
        y = y_scr[...]
        yp_ref[...] = y
        zv = z_ref[...]
        yg = y * (zv * jax.nn.sigmoid(zv))
        for g in range(2):
            gs = slice(g * 512, (g + 1) * 512)
            grp = yg[:, gs]
            rstd = lax.rsqrt(jnp.mean(grp * grp, axis=1, keepdims=True) + EPS)
            ys_ref[:, gs] = (grp * rstd * nw_ref[:, gs]).astype(BF16)

    rb = lambda b, c: (b * nch + c, 0)
    v1k = pl.BlockSpec((1, SSD_WIDTH), lambda b, c: (0, 0))
    return pl.pallas_call(
        body, name="ssd_fwd", grid=(bl, nch),
        in_specs=[pl.BlockSpec((L, CONV_CH), rb), pl.BlockSpec((L, SSD_WIDTH), rb),
                  pl.BlockSpec((L, 16), rb), pl.BlockSpec((L, 16), rb),
                  pl.BlockSpec((16, L), lambda b, c: (0, b * nch + c)), v1k, v1k],
        out_specs=[pl.BlockSpec((L, SSD_WIDTH), rb), pl.BlockSpec((L, SSD_WIDTH), rb),
                   pl.BlockSpec((None, 16, HEAD_DIM, SSD_STATE), lambda b, c: (b * nch + c, 0, 0, 0))],
        out_shape=[jax.ShapeDtypeStruct((n, SSD_WIDTH), BF16), jax.ShapeDtypeStruct((n, SSD_WIDTH), F32),
                   jax.ShapeDtypeStruct((bl * nch, 16, HEAD_DIM, SSD_STATE), F32)],
        scratch_shapes=[pltpu.VMEM((16, HEAD_DIM, SSD_STATE), F32), pltpu.VMEM((L, SSD_WIDTH), F32)],
        compiler_params=_cparams(("parallel", "arbitrary")),
    )(xc, proj_a, dt, acum, acum_t, dskip_e, norm_w)


def _ssd_bwd(dys, xc, proj_a, ypre, hprev, dt, sig, acum, acum_t, a_log, dskip_e, norm_w, bl, t):
    n = bl * t
    nch = t // CHUNK
    L = CHUNK

    def body(dys_ref, xc_ref, z_ref, yp_ref, hp_ref, dt_ref, sg_ref, ac_ref, act_ref, al_ref, dsk_ref, nw_ref,
             dxc_ref, dz_ref, ddt_ref, dnw_ref, dsk16_ref, da16_ref, db16_ref, dh_scr, dy_scr):
        first = (pl.program_id(0) == 0) & (pl.program_id(1) == 0)

        @pl.when(first)
        def _():
            dnw_ref[...] = jnp.zeros_like(dnw_ref)
            dsk16_ref[...] = jnp.zeros_like(dsk16_ref)
            da16_ref[...] = jnp.zeros_like(da16_ref)
            db16_ref[...] = jnp.zeros_like(db16_ref)

        @pl.when(pl.program_id(1) == 0)
        def _():
            dh_scr[...] = jnp.zeros_like(dh_scr)

        y = yp_ref[...]
        zv = z_ref[...]
        sz = jax.nn.sigmoid(zv)
        gate = zv * sz
        yg = y * gate
        dout = dys_ref[...]
        nw = nw_ref[...]
        for g in range(2):
            gs = slice(g * 512, (g + 1) * 512)
            grp = yg[:, gs]
            rstd = lax.rsqrt(jnp.mean(grp * grp, axis=1, keepdims=True) + EPS)
            ghat = grp * rstd
            dnw_ref[:, gs] += jnp.sum(dout[:, gs] * ghat, axis=0, keepdims=True)
            gw = dout[:, gs] * nw[:, gs]
            dyg = rstd * (gw - ghat * jnp.mean(gw * ghat, axis=1, keepdims=True))
            dy_scr[:, gs] = dyg * gate[:, gs]
            dz_ref[:, gs] = (dyg * y[:, gs] * (sz[:, gs] * (1.0 + zv[:, gs] * (1.0 - sz[:, gs])))).astype(BF16)

        row = lax.broadcasted_iota(jnp.int32, (L, L), 0)
        col = lax.broadcasted_iota(jnp.int32, (L, L), 1)
        causal = row >= col
        lane16 = lax.broadcasted_iota(jnp.int32, (1, 16), 1)
        lane128 = lax.broadcasted_iota(jnp.int32, (1, L), 1)
        last_row = lax.broadcasted_iota(jnp.int32, (L, 1), 0) == (L - 1)
        dt_all = dt_ref[...]
        ac_all = ac_ref[...]
        act_all = act_ref[...]
        dac_col = jnp.zeros((L, L), F32)
        dac_row = jnp.zeros((L, L), F32)
        ddt_x = jnp.zeros((L, L), F32)
        dsk16 = jnp.zeros((1, 16), F32)
        rows16 = lax.broadcasted_iota(jnp.int32, (L, 1), 0)
        for g in range(2):
            bsl = slice(SSD_WIDTH + g * 128, SSD_WIDTH + (g + 1) * 128)
            csl = slice(SSD_WIDTH + 256 + g * 128, SSD_WIDTH + 256 + (g + 1) * 128)
            bg = xc_ref[:, bsl].astype(BF16)
            cg = xc_ref[:, csl].astype(BF16)
            gmat = _dot(cg, bg, NT_DIMS)
            dg_sum = jnp.zeros((L, L), F32)
            dc_acc = jnp.zeros((L, SSD_STATE), F32)
            db_acc = jnp.zeros((L, SSD_STATE), F32)
            for r in range(8):
                h = g * 8 + r
                sl = slice(h * HEAD_DIM, (h + 1) * HEAD_DIM)
                onehot = lane16 == h
                onehot_w = lane128 == h
                xs = xc_ref[:, sl]
                dth = dt_all[:, h:h + 1]
                xdt = xs * dth
                xb = xdt.astype(BF16)
                ac = ac_all[:, h:h + 1]
                ar = act_all[h:h + 1, :]
                alast = ac_all[L - 1:L, h:h + 1]
                ldec = jnp.exp(jnp.where(causal, ac - ar, NEG))
                mf = gmat * ldec
                e_in = jnp.exp(ac)
                dec = jnp.exp(alast - ac)
                elast = jnp.exp(alast)
                hp = hp_ref[h]
                hpb = hp.astype(BF16)
                dyh = dy_scr[:, sl]
                dyb = dyh.astype(BF16)
                dsk16 = dsk16 + jnp.where(onehot, jnp.sum(jnp.sum(dyh * xs, axis=1, keepdims=True), axis=0, keepdims=True), 0.0)
                dm = _dot(dyb, xb, NT_DIMS)
                dx = _dot(mf.astype(BF16), dyb, TN_DIMS)
                dg_sum = dg_sum + dm * ldec
                wmat = dm * mf
                dac_h = jnp.sum(wmat, axis=1, keepdims=True)
                dac_row = dac_row + jnp.where(rows16 == h, -jnp.sum(wmat, axis=0, keepdims=True), 0.0)
                ch = _dot(cg, hpb, NT_DIMS)
                dye = dyh * e_in
                dyeb = dye.astype(BF16)
                dc_acc = dc_acc + _dot(dyeb, hpb)
                dhp = _dot(dyeb, cg, TN_DIMS)
                dac_h = dac_h + jnp.sum(dye * ch, axis=1, keepdims=True)
                ds = dh_scr[h]
                dsb = ds.astype(BF16)
                dxd = _dot(bg, dsb, NT_DIMS)
                db_acc = db_acc + _dot((xdt * dec).astype(BF16), dsb)
                dx = dx + dxd * dec
                ddec = jnp.sum(dxd * xdt, axis=1, keepdims=True) * dec
                extra = (jnp.sum(ddec, axis=0, keepdims=True)
                         + elast * jnp.sum(jnp.sum(hp * ds, axis=1, keepdims=True), axis=0, keepdims=True))
                dac_h = dac_h - ddec + jnp.where(last_row, extra, 0.0)
                dh_scr[h] = elast * ds + dhp
                dac_col = dac_col + jnp.where(onehot_w, dac_h, 0.0)
                ddt_x = ddt_x + jnp.where(onehot_w, jnp.sum(dx * xs, axis=1, keepdims=True), 0.0)
                dxc_ref[:, sl] = dx * dth + dsk_ref[:, sl] * dyh
            dgb = dg_sum.astype(BF16)
            dxc_ref[:, csl] = dc_acc + _dot(dgb, bg)
            dxc_ref[:, bsl] = db_acc + _dot(dgb, cg, TN_DIMS)
        dac = dac_col + jnp.transpose(dac_row)
        triu = (row <= col).astype(F32)
        da = jnp.dot(triu, dac, precision=HIGHEST, preferred_element_type=F32)[:, 0:16]
        a_row = -jnp.exp(al_ref[...])
        ddt = (ddt_x[:, 0:16] + da * a_row) * sg_ref[...]
        ddt_ref[...] = ddt
        dsk16_ref[...] += dsk16
        da16_ref[...] += jnp.sum(da * dt_all, axis=0, keepdims=True) * a_row
        db16_ref[...] += jnp.sum(ddt, axis=0, keepdims=True)

    rb = lambda b, c: (b * nch + nch - 1 - c, 0)
    v1k = pl.BlockSpec((1, SSD_WIDTH), lambda b, c: (0, 0))
    v16 = pl.BlockSpec((1, 16), lambda b, c: (0, 0))
    wide = pl.BlockSpec((L, SSD_WIDTH), rb)
    s16 = pl.BlockSpec((L, 16), rb)
    return pl.pallas_call(
        body, name="ssd_bwd", grid=(bl, nch),
        in_specs=[wide, pl.BlockSpec((L, CONV_CH), rb), wide, wide,
                  pl.BlockSpec((None, 16, HEAD_DIM, SSD_STATE), lambda b, c: (b * nch + nch - 1 - c, 0, 0, 0)),
                  s16, s16, s16, pl.BlockSpec((16, L), lambda b, c: (0, b * nch + nch - 1 - c)), v16, v1k, v1k],
        out_specs=[pl.BlockSpec((L, CONV_CH), rb), wide, s16, v1k, v16, v16, v16],
        out_shape=[jax.ShapeDtypeStruct((n, CONV_CH), F32), jax.ShapeDtypeStruct((n, SSD_WIDTH), BF16),
                   jax.ShapeDtypeStruct((n, 16), F32), jax.ShapeDtypeStruct((1, SSD_WIDTH), F32),
                   jax.ShapeDtypeStruct((1, 16), F32), jax.ShapeDtypeStruct((1, 16), F32),
                   jax.ShapeDtypeStruct((1, 16), F32)],
        scratch_shapes=[pltpu.VMEM((16, HEAD_DIM, SSD_STATE), F32), pltpu.VMEM((L, SSD_WIDTH), F32)],
        compiler_params=_cparams(("arbitrary", "arbitrary")),
    )(dys, xc, proj_a, ypre, hprev, dt, sig, acum, acum_t, a_log, dskip_e, norm_w)


def _head_expander():
    r = lax.broadcasted_iota(jnp.int32, (128, SSD_WIDTH), 0)
    c = lax.broadcasted_iota(jnp.int32, (128, SSD_WIDTH), 1)
    return (c // HEAD_DIM == r).astype(F32)


def _spread(v128, expander):
    return jnp.dot(v128, expander, precision=HIGHEST, preferred_element_type=F32)


def _head_sums(v1024, expander):
    return lax.dot_general(v1024, expander, NT_DIMS, precision=HIGHEST, preferred_element_type=F32)


def _ssd_fwd2(xc, proj_a, dt, acum, acum_t, dskip_e, norm_w, bl, t):
    n = bl * t
    nch = t // CHUNK
    L = CHUNK

    def body(xc_ref, z_ref, dt_ref, ac_ref, act_ref, dsk_ref, nw_ref, ys_ref, yp_ref, hp_ref, h_scr, y_scr, x_scr):
        @pl.when(pl.program_id(1) == 0)
        def _():
            h_scr[...] = jnp.zeros_like(h_scr)

        row = lax.broadcasted_iota(jnp.int32, (L, L), 0)
        col = lax.broadcasted_iota(jnp.int32, (L, L), 1)
        causal = row >= col
        expander = _head_expander()
        ac_all = ac_ref[...]
        act_all = act_ref[...]
        ac_e = _spread(ac_all, expander)
        e_in = jnp.exp(ac_e)
        dec = jnp.exp(ac_e[L - 1:L, :] - ac_e)
        xs_all = xc_ref[:, 0:SSD_WIDTH]
        x_all = xs_all * _spread(dt_ref[...], expander)
        x_scr[...] = x_all.astype(BF16)
        hp_all = h_scr[...]
        hp_ref[...] = hp_all
        for g in range(2):
            gs = slice(g * 512, (g + 1) * 512)
            bg = xc_ref[:, SSD_WIDTH + g * 128:SSD_WIDTH + (g + 1) * 128].astype(BF16)
            cg = xc_ref[:, SSD_WIDTH + 256 + g * 128:SSD_WIDTH + 256 + (g + 1) * 128].astype(BF16)
            gmat = _dot(cg, bg, NT_DIMS)
            y_scr[:, gs] = (_dot(cg, hp_all[gs, :].astype(BF16), NT_DIMS) * e_in[:, gs]
                            + dsk_ref[:, gs] * xs_all[:, gs])
            s_new = _dot((x_all[:, gs] * dec[:, gs]).astype(BF16), bg, TN_DIMS)
            for r in range(8):
                h = g * 8 + r
                sl = slice(h * HEAD_DIM, (h + 1) * HEAD_DIM)
                ldec = jnp.exp(jnp.where(causal, ac_all[:, h:h + 1] - act_all[h:h + 1, :], NEG))
                y_scr[:, sl] += _dot((gmat * ldec).astype(BF16), x_scr[:, sl])
                elast = jnp.exp(ac_all[L - 1:L, h:h + 1])
                h_scr[sl, :] = elast * hp_all[sl, :] + s_new[r * HEAD_DIM:(r + 1) * HEAD_DIM, :]
        y = y_scr[...]
        yp_ref[...] = y
        zv = z_ref[...]
        yg = y * (zv * jax.nn.sigmoid(zv))
        for g in range(2):
            gs = slice(g * 512, (g + 1) * 512)
            grp = yg[:, gs]
            rstd = lax.rsqrt(jnp.mean(grp * grp, axis=1, keepdims=True) + EPS)
            ys_ref[:, gs] = (grp * rstd * nw_ref[:, gs]).astype(BF16)

    rb = lambda b, c: (b * nch + c, 0)
    v1k = pl.BlockSpec((1, SSD_WIDTH), lambda b, c: (0, 0))
    return pl.pallas_call(
        body, name="ssd_fwd", grid=(bl, nch),
        in_specs=[pl.BlockSpec((L, CONV_CH), rb), pl.BlockSpec((L, SSD_WIDTH), rb),
                  pl.BlockSpec((L, 128), rb), pl.BlockSpec((L, 128), rb),
                  pl.BlockSpec((16, L), lambda b, c: (0, b * nch + c)), v1k, v1k],
        out_specs=[pl.BlockSpec((L, SSD_WIDTH), rb), pl.BlockSpec((L, SSD_WIDTH), rb),
                   pl.BlockSpec((None, SSD_WIDTH, SSD_STATE), lambda b, c: (b * nch + c, 0, 0))],
        out_shape=[jax.ShapeDtypeStruct((n, SSD_WIDTH), BF16), jax.ShapeDtypeStruct((n, SSD_WIDTH), F32),
                   jax.ShapeDtypeStruct((bl * nch, SSD_WIDTH, SSD_STATE), F32)],
        scratch_shapes=[pltpu.VMEM((SSD_WIDTH, SSD_STATE), F32), pltpu.VMEM((L, SSD_WIDTH), F32),
                        pltpu.VMEM((L, SSD_WIDTH), BF16)],
        compiler_params=_cparams(("parallel", "arbitrary")),
    )(xc, proj_a, dt, acum, acum_t, dskip_e, norm_w)


def _ssd_bwd2(dys, xc, proj_a, ypre, hprev, dt, sig, acum, acum_t, a_log, dskip_e, norm_w, bl, t):
    n = bl * t
    nch = t // CHUNK
    L = CHUNK

    def body(dys_ref, xc_ref, z_ref, yp_ref, hp_ref, dt_ref, sg_ref, ac_ref, act_ref, al_ref, dsk_ref, nw_ref,
             dxc_ref, dz_ref, ddt_ref, dnw_ref, dsk16_ref, da16_ref, db16_ref,
             dh_scr, dy_scr, x_scr, dx_scr, red_scr):
        first = (pl.program_id(0) == 0) & (pl.program_id(1) == 0)

        @pl.when(first)
        def _():
            dnw_ref[...] = jnp.zeros_like(dnw_ref)
            dsk16_ref[...] = jnp.zeros_like(dsk16_ref)
            da16_ref[...] = jnp.zeros_like(da16_ref)
            db16_ref[...] = jnp.zeros_like(db16_ref)

        @pl.when(pl.program_id(1) == 0)
        def _():
            dh_scr[...] = jnp.zeros_like(dh_scr)

        y = yp_ref[...]
        zv = z_ref[...]
        sz = jax.nn.sigmoid(zv)
        gate = zv * sz
        yg = y * gate
        dout = dys_ref[...]
        nw = nw_ref[...]
        for g in range(2):
            gs = slice(g * 512, (g + 1) * 512)
            grp = yg[:, gs]
            rstd = lax.rsqrt(jnp.mean(grp * grp, axis=1, keepdims=True) + EPS)
            ghat = grp * rstd
            dnw_ref[:, gs] += jnp.sum(dout[:, gs] * ghat, axis=0, keepdims=True)
            gw = dout[:, gs] * nw[:, gs]
            dyg = rstd * (gw - ghat * jnp.mean(gw * ghat, axis=1, keepdims=True))
            dy_scr[:, gs] = dyg * gate[:, gs]
            dz_ref[:, gs] = (dyg * y[:, gs] * (sz[:, gs] * (1.0 + zv[:, gs] * (1.0 - sz[:, gs])))).astype(BF16)

        row = lax.broadcasted_iota(jnp.int32, (L, L), 0)
        col = lax.broadcasted_iota(jnp.int32, (L, L), 1)
        causal = row >= col
        lane128 = lax.broadcasted_iota(jnp.int32, (1, L), 1)
        rows128 = lax.broadcasted_iota(jnp.int32, (L, 1), 0)
        last_row = rows128 == (L - 1)
        expander = _head_expander()
        ac_all = ac_ref[...]
        act_all = act_ref[...]
        dt_all = dt_ref[...]
        dt_e = _spread(dt_all, expander)
        ac_e = _spread(ac_all, expander)
        e_in = jnp.exp(ac_e)
        dec = jnp.exp(ac_e[L - 1:L, :] - ac_e)
        xs_all = xc_ref[:, 0:SSD_WIDTH]
        x_all = xs_all * dt_e
        x_scr[...] = x_all.astype(BF16)
        dy_all = dy_scr[...]
        hp_all = hp_ref[...]
        ds_all = dh_scr[...]
        dsk_cols = jnp.sum(dy_all * xs_all, axis=0, keepdims=True)
        dac = jnp.zeros((L, L), F32)
        dac_row = jnp.zeros((L, L), F32)
        ddec_cols = []
        for g in range(2):
            gs = slice(g * 512, (g + 1) * 512)
            bsl = slice(SSD_WIDTH + g * 128, SSD_WIDTH + (g + 1) * 128)
            csl = slice(SSD_WIDTH + 256 + g * 128, SSD_WIDTH + 256 + (g + 1) * 128)
            bg = xc_ref[:, bsl].astype(BF16)
            cg = xc_ref[:, csl].astype(BF16)
            gmat = _dot(cg, bg, NT_DIMS)
            hpb = hp_all[gs, :].astype(BF16)
            dsb = ds_all[gs, :].astype(BF16)
            ch = _dot(cg, hpb, NT_DIMS)
            dye = dy_all[:, gs] * e_in[:, gs]
            dyeb = dye.astype(BF16)
            dc_acc = _dot(dyeb, hpb)
            dhp = _dot(dyeb, cg, TN_DIMS)
            dxd = _dot(bg, dsb, NT_DIMS)
            db_acc = _dot((x_all[:, gs] * dec[:, gs]).astype(BF16), dsb)
            ddec = dxd * x_all[:, gs] * dec[:, gs]
            ddec_cols.append(jnp.sum(ddec, axis=0, keepdims=True))
            dx_scr[:, gs] = dxd * dec[:, gs]
            red_scr[:, gs] = dye * ch - ddec
            dg_sum = jnp.zeros((L, L), F32)
            for r in range(8):
                h = g * 8 + r
                sl = slice(h * HEAD_DIM, (h + 1) * HEAD_DIM)
                onehot_w = lane128 == h
                ldec = jnp.exp(jnp.where(causal, ac_all[:, h:h + 1] - act_all[h:h + 1, :], NEG))
                mf = gmat * ldec
                dyb = dy_scr[:, sl].astype(BF16)
                dm = _dot(dyb, x_scr[:, sl], NT_DIMS)
                dx_scr[:, sl] += _dot(mf.astype(BF16), dyb, TN_DIMS)
                dg_sum = dg_sum + dm * ldec
                wmat = dm * mf
                elast = jnp.exp(ac_all[L - 1:L, h:h + 1])
                hp_h = hp_all[sl, :]
                ds_h = ds_all[sl, :]
                extra = elast * jnp.sum(jnp.sum(hp_h * ds_h, axis=1, keepdims=True), axis=0, keepdims=True)
                dac = dac + jnp.where(onehot_w, jnp.sum(wmat, axis=1, keepdims=True) + jnp.where(last_row, extra, 0.0),
                                      0.0)
                dac_row = dac_row + jnp.where(rows128 == h, -jnp.sum(wmat, axis=0, keepdims=True), 0.0)
                dh_scr[sl, :] = elast * ds_h + dhp[r * HEAD_DIM:(r + 1) * HEAD_DIM, :]
            dgb = dg_sum.astype(BF16)
            dxc_ref[:, csl] = dc_acc + _dot(dgb, bg)
            dxc_ref[:, bsl] = db_acc + _dot(dgb, cg, TN_DIMS)
        dx_all = dx_scr[...]
        dxc_ref[:, 0:SSD_WIDTH] = dx_all * dt_e + dsk_ref[...] * dy_all
        red = red_scr[...]
        dac_slab = _head_sums(red, expander)
        ddec_tot = _head_sums(jnp.broadcast_to(jnp.concatenate(ddec_cols, axis=1), (8, SSD_WIDTH)), expander)
        ddt_x = _head_sums(dx_all * xs_all, expander)
        dsk16_ref[...] += _head_sums(jnp.broadcast_to(dsk_cols, (8, SSD_WIDTH)), expander)[0:1, 0:16]
        dac = dac + dac_slab + jnp.transpose(dac_row) + jnp.where(last_row, ddec_tot[0:1, :], 0.0)
        triu = (row <= col).astype(F32)
        da = jnp.dot(triu, dac, precision=HIGHEST, preferred_element_type=F32)[:, 0:16]
        a_row = -jnp.exp(al_ref[...])
        dt16 = dt_all[:, 0:16]
        ddt = (ddt_x[:, 0:16] + da * a_row) * sg_ref[...]
        ddt_ref[...] = ddt
        da16_ref[...] += jnp.sum(da * dt16, axis=0, keepdims=True) * a_row
        db16_ref[...] += jnp.sum(ddt, axis=0, keepdims=True)

    rb = lambda b, c: (b * nch + nch - 1 - c, 0)
    v1k = pl.BlockSpec((1, SSD_WIDTH), lambda b, c: (0, 0))
    v16 = pl.BlockSpec((1, 16), lambda b, c: (0, 0))
    wide = pl.BlockSpec((L, SSD_WIDTH), rb)
    s16 = pl.BlockSpec((L, 16), rb)
    s128 = pl.BlockSpec((L, 128), rb)
    return pl.pallas_call(
        body, name="ssd_bwd", grid=(bl, nch),
        in_specs=[wide, pl.BlockSpec((L, CONV_CH), rb), wide, wide,
                  pl.BlockSpec((None, SSD_WIDTH, SSD_STATE), lambda b, c: (b * nch + nch - 1 - c, 0, 0)),
                  s128, s16, s128, pl.BlockSpec((16, L), lambda b, c: (0, b * nch + nch - 1 - c)), v16, v1k, v1k],
        out_specs=[pl.BlockSpec((L, CONV_CH), rb), wide, s16, v1k, v16, v16, v16],
        out_shape=[jax.ShapeDtypeStruct((n, CONV_CH), F32), jax.ShapeDtypeStruct((n, SSD_WIDTH), BF16),
                   jax.ShapeDtypeStruct((n, 16), F32), jax.ShapeDtypeStruct((1, SSD_WIDTH), F32),
                   jax.ShapeDtypeStruct((1, 16), F32), jax.ShapeDtypeStruct((1, 16), F32),
                   jax.ShapeDtypeStruct((1, 16), F32)],
        scratch_shapes=[pltpu.VMEM((SSD_WIDTH, SSD_STATE), F32), pltpu.VMEM((L, SSD_WIDTH), F32),
                        pltpu.VMEM((L, SSD_WIDTH), BF16), pltpu.VMEM((L, SSD_WIDTH), F32),
                        pltpu.VMEM((L, SSD_WIDTH), F32)],
        compiler_params=_cparams(("arbitrary", "arbitrary")),
    )(dys, xc, proj_a, ypre, hprev, dt, sig, acum, acum_t, a_log, dskip_e, norm_w)


def _attn_fwd(qkv, negc, bl, t):
    n = bl * t
    tb_ = ATT_BLOCK
    nb = t // tb_
    scale = 1.0 / math.sqrt(HEAD_DIM)

    def body(q_ref, k_ref, v_ref, c_ref, o_ref, lse_ref):
        row = lax.broadcasted_iota(jnp.int32, (tb_, tb_), 0)
        col = lax.broadcasted_iota(jnp.int32, (tb_, tb_), 1)
        causal = row >= col
        for qi in range(nb):
            r0, lk = qi * tb_, (qi + 1) * tb_
            for j in range(2):
                sl = slice(j * HEAD_DIM, (j + 1) * HEAD_DIM)
                s = _dot(q_ref[r0:lk, sl], k_ref[0:lk, sl], NT_DIMS) * scale + c_ref[j:j + 1, 0:lk]
                tail = jnp.where(causal, s[:, r0:lk], NEG)
                s = tail if qi == 0 else jnp.concatenate([s[:, 0:r0], tail], axis=1)
                m = jnp.max(s, axis=1, keepdims=True)
                p = jnp.exp(s - m)
                l = jnp.sum(p, axis=1, keepdims=True)
                acc = _dot(p.astype(BF16), v_ref[0:lk, sl])
                o_ref[r0:lk, sl] = (acc / l).astype(BF16)
                lse_ref[r0:lk, sl] = jnp.broadcast_to(m + jnp.log(l), (tb_, HEAD_DIM))

    blk = lambda off: pl.BlockSpec((t, 128), lambda b, hp: (b, off + hp))
    return pl.pallas_call(
        body, name="fox_attn_fwd", grid=(bl, 8),
        in_specs=[blk(0), blk(8), blk(16), pl.BlockSpec((None, None, 8, t), lambda b, hp: (b, hp, 0, 0))],
        out_specs=[blk(0), blk(0)],
        out_shape=[jax.ShapeDtypeStruct((n, ATT_WIDTH), BF16), jax.ShapeDtypeStruct((n, ATT_WIDTH), F32)],
        compiler_params=_cparams(("parallel", "parallel")),
    )(qkv, qkv, qkv, negc)


def _attn_bwd(qkv, do, o, lse, negc, bl, t):
    n = bl * t
    tb_ = ATT_BLOCK
    nb = t // tb_
    scale = 1.0 / math.sqrt(HEAD_DIM)

    def body(q_ref, k_ref, v_ref, do_ref, o_ref, lse_ref, c_ref, dq_ref, dk_ref, dv_ref, dc_ref, dr_ref,
             dq_scr, delta_scr):
        row = lax.broadcasted_iota(jnp.int32, (tb_, tb_), 0)
        col = lax.broadcasted_iota(jnp.int32, (tb_, tb_), 1)
        causal = row >= col
        dq_scr[...] = jnp.zeros_like(dq_scr)
        dr_ref[...] = jnp.zeros_like(dr_ref)
        dc_ref[...] = jnp.zeros_like(dc_ref)
        prod = do_ref[...].astype(F32) * o_ref[...].astype(F32)
        for j in range(2):
            sl = slice(j * HEAD_DIM, (j + 1) * HEAD_DIM)
            delta_scr[:, sl] = jnp.broadcast_to(jnp.sum(prod[:, sl], axis=1, keepdims=True), (t, HEAD_DIM))
        for kj in range(nb):
            r0, r1 = kj * tb_, (kj + 1) * tb_
            for j in range(2):
                sl = slice(j * HEAD_DIM, (j + 1) * HEAD_DIM)
                one = slice(j * HEAD_DIM, j * HEAD_DIM + 1)
                kb = k_ref[r0:r1, sl]
                qs = q_ref[r0:t, sl]
                dos = do_ref[r0:t, sl]
                s = _dot(qs, kb, NT_DIMS) * scale + c_ref[j:j + 1, r0:r1]
                head = jnp.where(causal, s[0:tb_, :], NEG)
                s = head if kj == nb - 1 else jnp.concatenate([head, s[tb_:, :]], axis=0)
                p = jnp.exp(s - lse_ref[r0:t, one])
                dp = _dot(dos, v_ref[r0:r1, sl], NT_DIMS)
                ds = p * (dp - delta_scr[r0:t, one])
                dsb = ds.astype(BF16)
                dv_ref[r0:r1, sl] = _dot(p.astype(BF16), dos, TN_DIMS).astype(BF16)
                dk_ref[r0:r1, sl] = (_dot(dsb, qs, TN_DIMS) * scale).astype(BF16)
                dq_scr[r0:t, sl] += _dot(dsb, kb)
                dr_ref[r0:t, sl] += jnp.broadcast_to(jnp.sum(ds, axis=1, keepdims=True), (t - r0, HEAD_DIM))
                dc_ref[j:j + 1, r0:r1] = -jnp.sum(ds, axis=0, keepdims=True)
        dq_ref[...] = (dq_scr[...] * scale).astype(BF16)

    blk = lambda off: pl.BlockSpec((t, 128), lambda b, hp: (b, off + hp))
    cblk = pl.BlockSpec((None, None, 8, t), lambda b, hp: (b, hp, 0, 0))
    return pl.pallas_call(
        body, name="fox_attn_bwd", grid=(bl, 8),
        in_specs=[blk(0), blk(8), blk(16), blk(0), blk(0), blk(0), cblk],
        out_specs=[blk(0), blk(0), blk(0), cblk, blk(0)],
        out_shape=[jax.ShapeDtypeStruct((n, ATT_WIDTH), BF16)] * 3
        + [jax.ShapeDtypeStruct((bl, 8, 8, t), F32), jax.ShapeDtypeStruct((n, ATT_WIDTH), F32)],
        scratch_shapes=[pltpu.VMEM((t, 128), F32), pltpu.VMEM((t, 128), F32)],
        compiler_params=_cparams(("parallel", "parallel")),
    )(qkv, qkv, qkv, do, o, lse, negc)


def _adamw(w, g, m, v, *, name):
    r, c = w.shape
    tr = _pick(r, (256, 128, 64, 32, 16, 8))
    bc1 = 1.0 - ADAM_B1 ** ADAM_STEP
    bc2 = 1.0 - ADAM_B2 ** ADAM_STEP

    def body(w_ref, g_ref, m_ref, v_ref, d_ref, nm_ref, nv_ref):
        gv = g_ref[...]
        mn = ADAM_B1 * m_ref[...] + (1.0 - ADAM_B1) * gv
        vn = ADAM_B2 * v_ref[...] + (1.0 - ADAM_B2) * (gv * gv)
        m_hat = mn / bc1
        v_hat = vn / bc2
        d_ref[...] = -ADAM_LR * (m_hat / (jnp.sqrt(v_hat) + ADAM_EPS) + ADAM_WD * w_ref[...])
        nm_ref[...] = mn
        nv_ref[...] = vn

    blk = pl.BlockSpec((tr, c), lambda i: (i, 0))
    return pl.pallas_call(
        body, name=name, grid=(r // tr,), in_specs=[blk] * 4, out_specs=[blk] * 3,
        out_shape=[jax.ShapeDtypeStruct((r, c), F32)] * 3,
        compiler_params=_cparams(("parallel",)),
    )(w, g, m, v)


def _sum_leading(parts, *, name, out_dtype=F32):
    k, r, c = parts.shape
    tr = _pick(r, (512, 256, 128, 96, 64, 32, 16, 8))

    def body(p_ref, o_ref):
        acc = p_ref[0].astype(F32)
        for i in range(1, k):
            acc = acc + p_ref[i].astype(F32)
        o_ref[...] = acc.astype(out_dtype)

    return pl.pallas_call(
        body, name=name, grid=(r // tr,),
        in_specs=[pl.BlockSpec((k, tr, c), lambda i: (0, i, 0))],
        out_specs=pl.BlockSpec((tr, c), lambda i: (i, 0)),
        out_shape=jax.ShapeDtypeStruct((r, c), out_dtype),
        compiler_params=_cparams(("parallel",)),
    )(parts)


def _add_pair(a, b, *, name):
    k, r, c = a.shape
    tr = _pick(r, (512, 256, 128))

    def body(a_ref, b_ref, o_ref):
        o_ref[...] = (a_ref[...].astype(F32) + b_ref[...].astype(F32)).astype(BF16)

    blk = pl.BlockSpec((None, tr, c), lambda j, i: (j, i, 0))
    return pl.pallas_call(
        body, name=name, grid=(k, r // tr), in_specs=[blk, blk], out_specs=blk,
        out_shape=jax.ShapeDtypeStruct((k, r, c), BF16),
        compiler_params=_cparams(("parallel", "parallel")),
    )(a, b)


ANY = pl.BlockSpec(memory_space=pl.ANY)


def _chip_peers(x, y):
    return [(1 - x, y, 2 * (1 - x) + y), (x, 1 - y, 2 * x + 1 - y), (1 - x, 1 - y, 2 * (1 - x) + 1 - y)]


def _gather_weights(blob):
    def body(b_ref, o_ref, send_sems, recv_sems, local_sem):
        x, y, c = lax.axis_index("x"), lax.axis_index("y"), lax.axis_index("c")
        me = 2 * x + y
        sibling = (x, y, 1 - c)
        peers = _chip_peers(x, y)

        def half(chip, hc):
            return o_ref.at[chip, pl.ds(hc * HALF_ROWS, HALF_ROWS), :]

        def copy(k, src, chip, hc, to):
            return pltpu.make_async_remote_copy(src_ref=src, dst_ref=half(chip, hc), send_sem=send_sems.at[k],
                                                recv_sem=recv_sems.at[k], device_id=to, device_id_type=MESH)

        mine = pltpu.make_async_copy(b_ref, o_ref.at[me], local_sem)
        mine.start()
        my_half = b_ref.at[pl.ds(c * HALF_ROWS, HALF_ROWS), :]
        first = [copy(k, my_half, me, c, (px, py, c)) for k, (px, py, _) in enumerate(peers)]
        for cp in first:
            cp.start()
        passed = [copy(3 + k, half(pc, c), pc, c, sibling) for k, (_, _, pc) in enumerate(peers)]
        for k, (px, py, pc) in enumerate(peers):
            copy(k, my_half, pc, c, (px, py, c)).wait_recv()
            passed[k].start()
        for k, (_, _, pc) in enumerate(peers):
            copy(3 + k, half(pc, 1 - c), pc, 1 - c, sibling).wait_recv()
        for cp in first + passed:
            cp.wait_send()
        mine.wait()

    return pl.pallas_call(
        body, name="gather_weights", in_specs=[ANY], out_specs=ANY,
        out_shape=jax.ShapeDtypeStruct((N_CHIPS, BLOB_ROWS, 1024), BF16),
        scratch_shapes=[pltpu.SemaphoreType.DMA((6,)), pltpu.SemaphoreType.DMA((6,)), pltpu.SemaphoreType.DMA],
    )(blob)


def _swap_halves(g):
    def body(g_ref, o_ref, send_sem, recv_sem):
        x, y, c = lax.axis_index("x"), lax.axis_index("y"), lax.axis_index("c")
        cp = pltpu.make_async_remote_copy(
            src_ref=g_ref.at[:, pl.ds((1 - c) * HALF_ROWS, HALF_ROWS), :], dst_ref=o_ref,
            send_sem=send_sem, recv_sem=recv_sem, device_id=(x, y, 1 - c), device_id_type=MESH)
        cp.start()
        cp.wait()

    return pl.pallas_call(
        body, name="grad_swap_halves", in_specs=[ANY], out_specs=ANY,
        out_shape=jax.ShapeDtypeStruct((N_CHIPS, HALF_ROWS, 1024), BF16),
        scratch_shapes=[pltpu.SemaphoreType.DMA, pltpu.SemaphoreType.DMA],
    )(g)


def _exchange_chips(p):
    def body(p_ref, o_ref, send_sems, recv_sems, local_sem):
        x, y, c = lax.axis_index("x"), lax.axis_index("y"), lax.axis_index("c")
        me = 2 * x + y
        peers = _chip_peers(x, y)
        mine = pltpu.make_async_copy(p_ref.at[me], o_ref.at[me], local_sem)
        mine.start()
        cps = [pltpu.make_async_remote_copy(src_ref=p_ref.at[pc], dst_ref=o_ref.at[me], send_sem=send_sems.at[k],
                                            recv_sem=recv_sems.at[k], device_id=(px, py, c), device_id_type=MESH)
               for k, (px, py, pc) in enumerate(peers)]
        for cp in cps:
            cp.start()
        for k, (px, py, pc) in enumerate(peers):
            pltpu.make_async_remote_copy(src_ref=p_ref.at[pc], dst_ref=o_ref.at[pc], send_sem=send_sems.at[k],
                                         recv_sem=recv_sems.at[k], device_id=(px, py, c),
                                         device_id_type=MESH).wait_recv()
        for cp in cps:
            cp.wait_send()
        mine.wait()

    return pl.pallas_call(
        body, name="grad_exchange_chips", in_specs=[ANY], out_specs=ANY,
        out_shape=jax.ShapeDtypeStruct((N_CHIPS, HALF_ROWS, 1024), BF16),
        scratch_shapes=[pltpu.SemaphoreType.DMA((3,)), pltpu.SemaphoreType.DMA((3,)), pltpu.SemaphoreType.DMA],
    )(p)


def _join_halves(gh):
    def body(g_ref, o_ref, send_sem, recv_sem, local_sem):
        x, y, c = lax.axis_index("x"), lax.axis_index("y"), lax.axis_index("c")
        mine = pltpu.make_async_copy(g_ref, o_ref.at[pl.ds(c * HALF_ROWS, HALF_ROWS), :], local_sem)
        mine.start()
        cp = pltpu.make_async_remote_copy(
            src_ref=g_ref, dst_ref=o_ref.at[pl.ds(c * HALF_ROWS, HALF_ROWS), :],
            send_sem=send_sem, recv_sem=recv_sem, device_id=(x, y, 1 - c), device_id_type=MESH)
        cp.start()
        pltpu.make_async_remote_copy(
            src_ref=g_ref, dst_ref=o_ref.at[pl.ds((1 - c) * HALF_ROWS, HALF_ROWS), :],
            send_sem=send_sem, recv_sem=recv_sem, device_id=(x, y, 1 - c), device_id_type=MESH).wait_recv()
        cp.wait_send()
        mine.wait()

    return pl.pallas_call(
        body, name="grad_join_halves", in_specs=[ANY], out_specs=ANY,
        out_shape=jax.ShapeDtypeStruct((BLOB_ROWS, 1024), F32),
        scratch_shapes=[pltpu.SemaphoreType.DMA, pltpu.SemaphoreType.DMA, pltpu.SemaphoreType.DMA],
    )(gh)


def _gather_small(s, *, name):
    rows = s.shape[0]

    def body(s_ref, o_ref, send_sems, recv_sems, local_sem):
        x, y, c = lax.axis_index("x"), lax.axis_index("y"), lax.axis_index("c")
        me = 4 * x + 2 * y + c
        mine = pltpu.make_async_copy(s_ref, o_ref.at[me], local_sem)
        mine.start()
        peers = []
        for k in range(1, 8):
            peers.append((1 - x if k & 4 else x, 1 - y if k & 2 else y, 1 - c if k & 1 else c))
        cps = [pltpu.make_async_remote_copy(src_ref=s_ref, dst_ref=o_ref.at[me], send_sem=send_sems.at[k],
                                            recv_sem=recv_sems.at[k], device_id=p, device_id_type=MESH)
               for k, p in enumerate(peers)]
        for cp in cps:
            cp.start()
        for k, (px, py, pc) in enumerate(peers):
            pltpu.make_async_remote_copy(src_ref=s_ref, dst_ref=o_ref.at[4 * px + 2 * py + pc],
                                         send_sem=send_sems.at[k], recv_sem=recv_sems.at[k],
                                         device_id=(px, py, pc), device_id_type=MESH).wait_recv()
        for cp in cps:
            cp.wait_send()
        mine.wait()

    return pl.pallas_call(
        body, name=name, in_specs=[ANY], out_specs=ANY,
        out_shape=jax.ShapeDtypeStruct((8, rows, 128), F32),
        scratch_shapes=[pltpu.SemaphoreType.DMA((7,)), pltpu.SemaphoreType.DMA((7,)), pltpu.SemaphoreType.DMA],
    )(s)


SHARD_ROWS = (IN_WIDTH // N_CHIPS, 512, 1024, 1024)


def _pack_shard(w_in_s, w_out_s, w_up_s, w_down_s, dtype):
    parts = [w_in_s.reshape(-1, 1024), w_out_s.reshape(-1, 1024), w_up_s.reshape(-1, 1024),
             w_down_s.reshape(-1, 1024)]
    used = sum(SHARD_ROWS)
    parts.append(jnp.zeros((BLOB_ROWS - used, 1024), parts[0].dtype))
    return jnp.concatenate(parts, axis=0).astype(dtype)


def _unpack_shard(blob):
    r0 = 0
    out = []
    for rows, shape in zip(SHARD_ROWS, ((D_MODEL, IN_WIDTH // N_CHIPS), (512, D_MODEL), (D_MODEL, D_FF // N_CHIPS),
                                        (D_FF // N_CHIPS, D_MODEL))):
        out.append(blob[r0:r0 + rows].reshape(shape))
        r0 += rows
    return out


def _full_weights(gathered):
    per = [_unpack_shard(gathered[j]) for j in range(N_CHIPS)]
    w_in = jnp.concatenate([p[0] for p in per], axis=1)
    w_out = jnp.concatenate([p[1] for p in per], axis=0)
    w_up = jnp.concatenate([p[2] for p in per], axis=1)
    w_down = jnp.concatenate([p[3] for p in per], axis=0)
    return w_in, w_out, w_up, w_down


def _split_w_in(w_in):
    z_xbc = w_in[:, 0:2560]
    dt = w_in[:, 2560:2576]
    qkv = w_in[:, 2576:5648]
    f = w_in[:, 5648:5664]
    pad = jnp.zeros((w_in.shape[0], PA_WIDTH - 2592), w_in.dtype)
    return jnp.concatenate([z_xbc, dt, f, pad], axis=1), qkv


def _merge_w_in(d_a, d_qkv):
    return jnp.concatenate([d_a[:, 0:2560], d_a[:, 2560:2576], d_qkv, d_a[:, 2576:2592]], axis=1)


def _local_step(x3, target3, w_in, w_out, w_up, w_down, norm_mix_w, conv_w, conv_b, dt_bias, a_log, d_skip,
                ssd_norm_w, f_bias, norm_mlp_w, norm_final_w):
    bl, t, d = x3.shape
    n = bl * t
    x = x3.reshape(n, d)
    target = target3.reshape(n, d)
    w_a, w_qkv = _split_w_in(w_in)
    wo_s, wo_a = w_out[:SSD_WIDTH], w_out[SSD_WIDTH:]
    nfw = norm_final_w.reshape(1, d)
    dskip_e = jnp.repeat(d_skip, HEAD_DIM, axis=1)
    nb = t // ATT_BLOCK

    h0, rstd0 = _rmsnorm_fwd(x, norm_mix_w, name="norm_mix_fwd")
    proj_a = _mm(h0, w_a, name="proj_a")
    qkv = _mm(h0, w_qkv, name="proj_qkv", out_dtype=BF16)
    bias128 = jnp.concatenate([dt_bias, f_bias, jnp.zeros((1, 96), F32)], axis=1)
    alog128 = jnp.concatenate([a_log, jnp.zeros((1, 112), F32)], axis=1)
    dt, sig, acum, ccum, sigf = _prep(proj_a, bias128, alog128, bl, t)
    acum_t = acum[:, 0:16].T
    negc = jnp.pad(-ccum.reshape(bl, t, 8, 2).transpose(0, 2, 3, 1), ((0, 0), (0, 0), (0, 6), (0, 0)))
    xc = _conv_fwd(proj_a, conv_w, conv_b, bl, t)
    y_ssd, y_pre, hprev = _ssd_fwd2(xc, proj_a, dt, acum, acum_t, dskip_e, ssd_norm_w, bl, t)
    y_att, lse = _attn_fwd(qkv, negc, bl, t)
    t1 = _mm(y_ssd, wo_s, name="out_proj_ssd", res=x)
    h1 = _mm(y_att, wo_a, name="out_proj_att", res=t1)
    h1n, rstd1 = _rmsnorm_fwd(h1, norm_mlp_w, name="norm_mlp_fwd")
    up = _mm(h1n, w_up, name="mlp_up")
    h2 = _mm(up, w_down, name="mlp_down", a_act="relu2", res=h1)
    dh2, loss, d_nfw = _final(h2, nfw, target)

    dup = _mm(dh2, w_down, name="mlp_down_bwd_act", tb=True, epi_up=up, out_dtype=BF16)
    d_w_down = _mm(up, dh2, name="mlp_down_bwd_w", ta=True, a_act="relu2")
    dh1n = _mm(dup, w_up, name="mlp_up_bwd_act", tb=True)
    d_w_up = _mm(h1n, dup, name="mlp_up_bwd_w", ta=True)
    dh1, d_nmlp = _rmsnorm_bwd(dh1n, h1, rstd1, norm_mlp_w, dh2, name="norm_mlp_bwd")
    dys = _mm(dh1, wo_s, name="out_proj_bwd_ssd", tb=True)
    do = _mm(dh1, wo_a, name="out_proj_bwd_att", tb=True, out_dtype=BF16)
    d_w_out = jnp.concatenate([_mm(y_ssd, dh1, name="out_proj_bwd_w_ssd", ta=True),
                               _mm(y_att, dh1, name="out_proj_bwd_w_att", ta=True)], axis=0)
    dq, dk, dv, dcb, drow = _attn_bwd(qkv, do, y_att, lse, negc, bl, t)
    dc_keys = jnp.pad(dcb[:, :, 0:2, :].transpose(0, 3, 1, 2).reshape(n, 16), ((0, 0), (0, 112)))
    dc_queries = jnp.pad(drow.reshape(n, 16, HEAD_DIM)[:, :, 0], ((0, 0), (0, 112)))
    df_raw, d_fb = _fpost(dc_keys, dc_queries, sigf, bl, t)
    dxc, dz, ddt_raw, d_snw, d_dsk, d_alog, d_dtb = _ssd_bwd2(dys, xc, proj_a, y_pre, hprev, dt, sig, acum, acum_t,
                                                            a_log, dskip_e, ssd_norm_w, bl, t)
    dxbc, d_conv_w, d_conv_b = _conv_bwd(dxc, proj_a, conv_w, conv_b, bl, t)
    dproj_a = jnp.concatenate([dz, dxbc, ddt_raw.astype(BF16), df_raw.astype(BF16),
                               jnp.zeros((n, PA_WIDTH - 2592), BF16)], axis=1)
    dqkv = jnp.concatenate([dq, dk, dv], axis=1)
    d_w_a = _mm(h0, dproj_a, name="proj_a_bwd_w", ta=True)
    d_w_qkv = _mm(h0, dqkv, name="proj_qkv_bwd_w", ta=True)
    t2 = _mm(dproj_a, w_a, name="proj_a_bwd_act", tb=True)
    dh0 = _mm(dqkv, w_qkv, name="proj_qkv_bwd_act", tb=True, res=t2)
    dx, d_nmix = _rmsnorm_bwd(dh0, x, rstd0, norm_mix_w, dh1, name="norm_mix_bwd")

    grads = dict(norm_mix_w=d_nmix, w_in=_merge_w_in(d_w_a, d_w_qkv), conv_w=d_conv_w, conv_b=d_conv_b,
                 dt_bias=d_dtb, a_log=d_alog, d_skip=d_dsk, ssd_norm_w=d_snw, f_bias=d_fb, w_out=d_w_out,
                 norm_mlp_w=d_nmlp, w_up=d_w_up, w_down=d_w_down, norm_final_w=d_nfw)
    return dx.reshape(bl, t, d), loss, grads


SMALL_ORDER = ("norm_mix_w", "conv_w", "conv_b", "dt_bias", "a_log", "d_skip", "ssd_norm_w", "f_bias",
               "norm_mlp_w", "norm_final_w")
SMALL_SIZES = (1024, 4 * CONV_CH, CONV_CH, 16, 16, 16, 1024, 16, 1024, 1024)


def _pack_small(vals, rows):
    flat = jnp.concatenate([v.reshape(-1).astype(F32) for v in vals])
    return jnp.pad(flat, (0, rows * 128 - flat.shape[0])).reshape(rows, 128)


def _unpack_small(packed, sizes):
    flat = packed.reshape(-1)
    out, o = [], 0
    for s in sizes:
        out.append(flat[o:o + s])
        o += s
    return out


def kernel(x, norm_mix_w, w_in, conv_w, conv_b, dt_bias, a_log, d_skip, ssd_norm_w, f_bias, w_out, norm_mlp_w, w_up, w_down, norm_final_w, loss_target, m_norm_mix_w, m_w_in, m_conv_w, m_conv_b, m_dt_bias, m_a_log, m_d_skip, m_ssd_norm_w, m_f_bias, m_w_out, m_norm_mlp_w, m_w_up, m_w_down, m_norm_final_w, v_norm_mix_w, v_w_in, v_conv_w, v_conv_b, v_dt_bias, v_a_log, v_d_skip, v_ssd_norm_w, v_f_bias, v_w_out, v_norm_mlp_w, v_w_up, v_w_down, v_norm_final_w):
    chip = 2 * lax.axis_index("x") + lax.axis_index("y")
    cw = CONV_CH // N_CHIPS

    blob = _pack_shard(w_in[0], w_out[0], w_up[0], w_down[0], BF16)
    gathered = _gather_weights(blob)
    w_in_f, w_out_f, w_up_f, w_down_f = _full_weights(gathered)
    small_all = _gather_small(_pack_small([conv_w[0]], 16), name="gather_conv_w")
    conv_w_f = jnp.concatenate([small_all[2 * j].reshape(-1)[:4 * cw].reshape(4, cw) for j in range(N_CHIPS)], axis=1)

    dx, loss_part, g = _local_step(x, loss_target, w_in_f, w_out_f, w_up_f, w_down_f, norm_mix_w, conv_w_f,
                                   conv_b, dt_bias, a_log, d_skip, ssd_norm_w, f_bias, norm_mlp_w, norm_final_w)

    gblob = jnp.stack([_pack_shard(g["w_in"][:, j * (IN_WIDTH // N_CHIPS):(j + 1) * (IN_WIDTH // N_CHIPS)],
                                   g["w_out"][j * 512:(j + 1) * 512],
                                   g["w_up"][:, j * 1024:(j + 1) * 1024],
                                   g["w_down"][j * 1024:(j + 1) * 1024], BF16) for j in range(N_CHIPS)])
    c = lax.axis_index("c")
    from_sibling = _swap_halves(gblob)
    my_half = lax.dynamic_slice_in_dim(gblob, c * HALF_ROWS, HALF_ROWS, axis=1)
    chip_part = _add_pair(my_half, from_sibling, name="grad_add_sibling")
    parts = _exchange_chips(chip_part)
    g_half = _sum_leading(parts, name="grad_sum_chips")
    g_shard = _join_halves(g_half)
    g_w_in, g_w_out, g_w_up, g_w_down = _unpack_shard(g_shard)

    small_vals = [g[k] for k in SMALL_ORDER] + [loss_part[:, 0:1]]
    small_sum = _sum_leading(_gather_small(_pack_small(small_vals, SMALL_ROWS), name="gather_small_grads"), name="small_sum")
    sg = dict(zip(SMALL_ORDER + ("loss",), _unpack_small(small_sum, SMALL_SIZES + (1,))))
    loss = sg["loss"].reshape(())
    g_conv_full = sg["conv_w"].reshape(4, CONV_CH)
    g_conv = lax.dynamic_slice_in_dim(g_conv_full, chip * cw, cw, axis=1)

    grads = dict(norm_mix_w=sg["norm_mix_w"].reshape(1, -1), w_in=g_w_in[None], conv_w=g_conv[None],
                 conv_b=sg["conv_b"].reshape(1, -1), dt_bias=sg["dt_bias"].reshape(1, -1),
                 a_log=sg["a_log"].reshape(1, -1), d_skip=sg["d_skip"].reshape(1, -1),
                 ssd_norm_w=sg["ssd_norm_w"].reshape(1, -1), f_bias=sg["f_bias"].reshape(1, -1), w_out=g_w_out[None],
                 norm_mlp_w=sg["norm_mlp_w"].reshape(1, -1), w_up=g_w_up[None], w_down=g_w_down[None],
                 norm_final_w=sg["norm_final_w"])
    weights = dict(norm_mix_w=norm_mix_w, w_in=w_in, conv_w=conv_w, conv_b=conv_b, dt_bias=dt_bias, a_log=a_log,
                   d_skip=d_skip, ssd_norm_w=ssd_norm_w, f_bias=f_bias, w_out=w_out, norm_mlp_w=norm_mlp_w,
                   w_up=w_up, w_down=w_down, norm_final_w=norm_final_w)
    ms = dict(norm_mix_w=m_norm_mix_w, w_in=m_w_in, conv_w=m_conv_w, conv_b=m_conv_b, dt_bias=m_dt_bias,
              a_log=m_a_log, d_skip=m_d_skip, ssd_norm_w=m_ssd_norm_w, f_bias=m_f_bias, w_out=m_w_out,
              norm_mlp_w=m_norm_mlp_w, w_up=m_w_up, w_down=m_w_down, norm_final_w=m_norm_final_w)
    vs = dict(norm_mix_w=v_norm_mix_w, w_in=v_w_in, conv_w=v_conv_w, conv_b=v_conv_b, dt_bias=v_dt_bias,
              a_log=v_a_log, d_skip=v_d_skip, ssd_norm_w=v_ssd_norm_w, f_bias=v_f_bias, w_out=v_w_out,
              norm_mlp_w=v_norm_mlp_w, w_up=v_w_up, w_down=v_w_down, norm_final_w=v_norm_final_w)
    names = list(weights)
    big = ("w_in", "w_out", "w_up", "w_down")
    delta, new_m, new_v = {}, {}, {}
    for k in big:
        shp = weights[k].shape
        two_d = lambda a: a.reshape(shp[-2], shp[-1])
        d_, m_, v_ = _adamw(two_d(weights[k]), two_d(grads[k]), two_d(ms[k]), two_d(vs[k]), name="adamw_" + k)
        delta[k], new_m[k], new_v[k] = d_.reshape(shp), m_.reshape(shp), v_.reshape(shp)
    smalls = [k for k in names if k not in big]
    sizes = [math.prod(weights[k].shape) for k in smalls]
    rows = -(-sum(sizes) // 1024) * 8
    packs = [_pack_small([d[k] for k in smalls], rows) for d in (weights, grads, ms, vs)]
    outs = _adamw(*packs, name="adamw_small")
    for o, dst in zip(outs, (delta, new_m, new_v)):
        for k, val in zip(smalls, _unpack_small(o, sizes)):
            dst[k] = val.reshape(weights[k].shape)
    return (loss, dx, *[grads[k] for k in names], *[delta[k] for k in names], *[new_m[k] for k in names],
            *[new_v[k] for k in names])
```

```python
import functools
import math

import jax
import jax.numpy as jnp
from jax import lax
from jax.experimental import pallas as pl
from jax.experimental.pallas import tpu as pltpu

F32 = jnp.float32
BF16 = jnp.bfloat16
HIGHEST = lax.Precision.HIGHEST
MESH = pl.DeviceIdType.MESH

D_MODEL = 1024
SSD_HEADS = 16
HEAD_DIM = 64
SSD_WIDTH = 1024
SSD_STATE = 128
CONV_CH = 1536
CHUNK = 128
ATT_WIDTH = 1024
EPS = 1e-5
IN_WIDTH = 5664
PA_WIDTH = 2688
QKV_WIDTH = 3072
D_FF = 4096
ATT_BLOCK = 256
NEG = -1e30
VMEM_LIMIT = 48 * 1024 * 1024

ADAM_LR = 0.001
ADAM_B1 = 0.9
ADAM_B2 = 0.999
ADAM_EPS = 1e-08
ADAM_WD = 0.01
ADAM_STEP = 10

N_CHIPS = 4
BLOB_ROWS = 4096
HALF_ROWS = BLOB_ROWS // 2
SMALL_ROWS = 96


def _cparams(sem):
    return pltpu.CompilerParams(dimension_semantics=sem, vmem_limit_bytes=VMEM_LIMIT)


def _pick(n, cands):
    for c in cands:
        if n % c == 0:
            return c
    return n


MM_CHUNK = 512


def _mm(a, b, *, name, tiles, ta=False, tb=False, out_dtype=F32, res=None, a_act=None, epi_up=None):
    if ta:
        K, M = a.shape
    else:
        M, K = a.shape
    if tb:
        N, K2 = b.shape
    else:
        K2, N = b.shape
    assert K == K2, (a.shape, b.shape)
    tm, tn, tk = tiles
    assert M % tm == 0 and N % tn == 0 and K % tk == 0, (name, M, N, K, tiles)
    nk = K // tk
    dn = (((0 if ta else 1,), (1 if tb else 0,)), ((), ()))
    has_res = res is not None
    has_up = epi_up is not None
    cn = _pick(tn, (MM_CHUNK, 384, 256, 128))

    def prologue(av):
        if a_act == "relu2":
            r = jnp.maximum(av.astype(F32), 0.0)
            av = r * r
        return av.astype(BF16)

    def epilogue(out, res_v, up_v):
        if has_res:
            out = out + res_v.astype(F32)
        if has_up:
            out = out * (2.0 * jnp.maximum(up_v.astype(F32), 0.0))
        return out.astype(out_dtype)

    def body(*refs):
        a_ref, b_ref = refs[0], refs[1]
        i = 2
        res_ref = up_ref = None
        if has_res:
            res_ref = refs[i]
            i += 1
        if has_up:
            up_ref = refs[i]
            i += 1
        o_ref = refs[i]
        if nk == 1:
            av = prologue(a_ref[...])
            for c in range(tn // cn):
                cs = slice(c * cn, (c + 1) * cn)
                bv = (b_ref[cs, :] if tb else b_ref[:, cs]).astype(BF16)
                out = lax.dot_general(av, bv, dn, preferred_element_type=F32)
                o_ref[:, cs] = epilogue(out, res_ref[:, cs] if has_res else None, up_ref[:, cs] if has_up else None)
            return
        acc_ref = refs[i + 1]
        k = pl.program_id(2)

        @pl.when(k == 0)
        def _():
            acc_ref[...] = jnp.zeros_like(acc_ref)

        acc_ref[...] += lax.dot_general(prologue(a_ref[...]), b_ref[...].astype(BF16), dn,
                                        preferred_element_type=F32)

        @pl.when(k == nk - 1)
        def _():
            o_ref[...] = epilogue(acc_ref[...], res_ref[...] if has_res else None, up_ref[...] if has_up else None)

    a_spec = pl.BlockSpec((tk, tm), lambda i, j, k: (k, i)) if ta else pl.BlockSpec((tm, tk), lambda i, j, k: (i, k))
    b_spec = pl.BlockSpec((tn, tk), lambda i, j, k: (j, k)) if tb else pl.BlockSpec((tk, tn), lambda i, j, k: (k, j))
    o_spec = pl.BlockSpec((tm, tn), lambda i, j, k: (i, j))
    ins, specs = [a, b], [a_spec, b_spec]
    if has_res:
        ins.append(res)
        specs.append(o_spec)
    if has_up:
        ins.append(epi_up)
        specs.append(o_spec)
    return pl.pallas_call(
        body, name=name, grid=(M // tm, N // tn, nk),
        in_specs=specs, out_specs=o_spec,
        out_shape=jax.ShapeDtypeStruct((M, N), out_dtype),
        scratch_shapes=[] if nk == 1 else [pltpu.VMEM((tm, tn), F32)],
        compiler_params=_cparams(("parallel", "parallel", "arbitrary")),
    )(*ins)


def _rmsnorm_fwd(x, w, *, name):
    n, d = x.shape
    tm = _pick(n, (512, 256, 128))

    def body(x_ref, w_ref, y_ref, r_ref):
        xv = x_ref[...]
        rstd = lax.rsqrt(jnp.mean(xv * xv, axis=1, keepdims=True) + EPS)
        y_ref[...] = (xv * rstd * w_ref[...]).astype(BF16)
        r_ref[...] = rstd

    return pl.pallas_call(
        body, name=name, grid=(n // tm,),
        in_specs=[pl.BlockSpec((tm, d), lambda i: (i, 0)), pl.BlockSpec((1, d), lambda i: (0, 0))],
        out_specs=[pl.BlockSpec((tm, d), lambda i: (i, 0)), pl.BlockSpec((tm, 1), lambda i: (i, 0))],
        out_shape=[jax.ShapeDtypeStruct((n, d), BF16), jax.ShapeDtypeStruct((n, 1), F32)],
        compiler_params=_cparams(("parallel",)),
    )(x, w)


def _rmsnorm_bwd(dyn, x, rstd, w, dres, *, name):
    n, d = x.shape
    tm = _pick(n, (512, 256, 128))

    def body(g_ref, x_ref, r_ref, w_ref, d_ref, dx_ref, dxb_ref, dw_ref):
        @pl.when(pl.program_id(0) == 0)
        def _():
            dw_ref[...] = jnp.zeros_like(dw_ref)

        g = g_ref[...]
        r = r_ref[...]
        xhat = x_ref[...] * r
        gw = g * w_ref[...]
        dx = d_ref[...] + r * (gw - xhat * jnp.mean(gw * xhat, axis=1, keepdims=True))
        dx_ref[...] = dx
        dxb_ref[...] = dx.astype(BF16)
        dw_ref[...] += jnp.sum(g * xhat, axis=0, keepdims=True)

    row = pl.BlockSpec((tm, d), lambda i: (i, 0))
    vec = pl.BlockSpec((1, d), lambda i: (0, 0))
    return pl.pallas_call(
        body, name=name, grid=(n // tm,),
        in_specs=[row, row, pl.BlockSpec((tm, 1), lambda i: (i, 0)), vec, row],
        out_specs=[row, row, vec],
        out_shape=[jax.ShapeDtypeStruct((n, d), F32), jax.ShapeDtypeStruct((n, d), BF16),
                   jax.ShapeDtypeStruct((1, d), F32)],
        compiler_params=_cparams(("arbitrary",)),
    )(dyn, x, rstd, w, dres)


def _final(h2, w, target):
    n, d = h2.shape
    tm = _pick(n, (512, 256, 128))

    def body(h_ref, w_ref, t_ref, dh_ref, dhb_ref, loss_ref, dw_ref):
        @pl.when(pl.program_id(0) == 0)
        def _():
            loss_ref[...] = jnp.zeros_like(loss_ref)
            dw_ref[...] = jnp.zeros_like(dw_ref)

        hv = h_ref[...]
        wv = w_ref[...]
        rstd = lax.rsqrt(jnp.mean(hv * hv, axis=1, keepdims=True) + EPS)
        xhat = hv * rstd
        err = xhat * wv - t_ref[...]
        part = jnp.sum(jnp.mean(err * err, axis=1, keepdims=True), axis=0, keepdims=True)
        loss_ref[...] += 0.5 * part
        dy = err * (1.0 / d)
        gw = dy * wv
        dh = rstd * (gw - xhat * jnp.mean(gw * xhat, axis=1, keepdims=True))
        dh_ref[...] = dh
        dhb_ref[...] = dh.astype(BF16)
        dw_ref[...] += jnp.sum(dy * xhat, axis=0, keepdims=True)

    row = pl.BlockSpec((tm, d), lambda i: (i, 0))
    vec = pl.BlockSpec((1, d), lambda i: (0, 0))
    return pl.pallas_call(
        body, name="final_norm_loss", grid=(n // tm,),
        in_specs=[row, vec, row],
        out_specs=[row, row, pl.BlockSpec((1, 128), lambda i: (0, 0)), vec],
        out_shape=[jax.ShapeDtypeStruct((n, d), F32), jax.ShapeDtypeStruct((n, d), BF16),
                   jax.ShapeDtypeStruct((1, 128), F32), jax.ShapeDtypeStruct((1, d), F32)],
        compiler_params=_cparams(("arbitrary",)),
    )(h2, w, target)


def _softplus(x):
    return jnp.maximum(x, 0.0) + jnp.log(1.0 + jnp.exp(-jnp.abs(x)))


def _prep(proj_a, bias128, alog128, bl, t):
    n = bl * t
    nch = t // CHUNK
    col0 = (SSD_WIDTH + CONV_CH) // 128

    def body(p_ref, b_ref, al_ref, dt_ref, sg_ref, ac_ref, c_ref, sf_ref, carry):
        @pl.when(pl.program_id(1) == 0)
        def _():
            carry[...] = jnp.zeros_like(carry)

        xv = p_ref[...] + b_ref[...]
        sp = _softplus(xv)
        a = -jnp.exp(al_ref[...]) * sp
        logf = -_softplus(-xv)
        row = lax.broadcasted_iota(jnp.int32, (CHUNK, CHUNK), 0)
        col = lax.broadcasted_iota(jnp.int32, (CHUNK, CHUNK), 1)
        tril = (row >= col).astype(F32)
        acum = jnp.dot(tril, a, precision=HIGHEST, preferred_element_type=F32)
        c = jnp.dot(tril, logf, precision=HIGHEST, preferred_element_type=F32) + carry[...]
        carry[...] = c[CHUNK - 1:CHUNK, :]
        head_lanes = lax.broadcasted_iota(jnp.int32, (1, 128), 1) < 16
        dt_ref[...] = jnp.where(head_lanes, sp, 0.0)
        sg_ref[...] = jax.nn.sigmoid(xv)[:, 0:16]
        ac_ref[...] = jnp.where(head_lanes, acum, 0.0)
        c_ref[...] = c[:, 16:32]
        sf_ref[...] = jax.nn.sigmoid(-xv)[:, 16:32]

    o16 = pl.BlockSpec((CHUNK, 16), lambda b, c: (b * nch + c, 0))
    o128 = pl.BlockSpec((CHUNK, 128), lambda b, c: (b * nch + c, 0))
    v128 = pl.BlockSpec((1, 128), lambda b, c: (0, 0))
    w16 = jax.ShapeDtypeStruct((n, 16), F32)
    w128 = jax.ShapeDtypeStruct((n, 128), F32)
    return pl.pallas_call(
        body, name="head_scalars", grid=(bl, nch),
        in_specs=[pl.BlockSpec((CHUNK, 128), lambda b, c: (b * nch + c, col0)), v128, v128],
        out_specs=[o128, o16, o128, o16, o16],
        out_shape=[w128, w16, w128, w16, w16],
        scratch_shapes=[pltpu.VMEM((1, 128), F32)],
        compiler_params=_cparams(("parallel", "arbitrary")),
    )(proj_a, bias128, alog128)


def _fpost(dck, dcq, sigf, bl, t):
    n = bl * t
    nch = t // CHUNK

    def body(dc_ref, dq_ref, sf_ref, df_ref, db_ref, carry):
        @pl.when(pl.program_id(1) == 0)
        def _():
            carry[...] = jnp.zeros_like(carry)

        @pl.when((pl.program_id(0) == 0) & (pl.program_id(1) == 0))
        def _():
            db_ref[...] = jnp.zeros_like(db_ref)

        row = lax.broadcasted_iota(jnp.int32, (CHUNK, CHUNK), 0)
        col = lax.broadcasted_iota(jnp.int32, (CHUNK, CHUNK), 1)
        triu = (row <= col).astype(F32)
        dlf = jnp.dot(triu, dc_ref[...] + dq_ref[...], precision=HIGHEST, preferred_element_type=F32) + carry[...]
        carry[...] = dlf[0:1, :]
        df = dlf[:, 0:16] * sf_ref[...]
        df_ref[...] = df
        db_ref[...] += jnp.sum(df, axis=0, keepdims=True)

    rev = lambda b, c: (b * nch + nch - 1 - c, 0)
    blk = pl.BlockSpec((CHUNK, 16), rev)
    return pl.pallas_call(
        body, name="forget_gate_bwd", grid=(bl, nch),
        in_specs=[pl.BlockSpec((CHUNK, 128), rev), pl.BlockSpec((CHUNK, 128), rev), blk],
        out_specs=[blk, pl.BlockSpec((1, 16), lambda b, c: (0, 0))],
        out_shape=[jax.ShapeDtypeStruct((n, 16), F32), jax.ShapeDtypeStruct((1, 16), F32)],
        scratch_shapes=[pltpu.VMEM((1, 128), F32)],
        compiler_params=_cparams(("arbitrary", "arbitrary")),
    )(dck, dcq, sigf)


CONV_TILE = 256
CONV_ROWS = 256


def _conv_taps(u_ref, i, w, bias):
    r0 = pl.multiple_of(i * CONV_ROWS, CONV_ROWS)
    cur = u_ref[pl.ds(r0, CONV_ROWS), :]
    p0 = pl.multiple_of(jnp.maximum(r0 - 8, 0), 8)
    prev = jnp.where(i > 0, u_ref[pl.ds(p0, 8), :], 0.0)
    cat = jnp.concatenate([prev, cur], axis=0)
    pre = bias + w[3:4, :] * cur
    taps = [cur]
    for s in (1, 2, 3):
        sh = pltpu.roll(cat, s, 0)[8:, :]
        taps.append(sh)
        pre = pre + w[3 - s:4 - s, :] * sh
    return r0, pre, taps


def _conv_fwd(proj_a, conv_w, conv_b, bl, t):
    n = bl * t
    nct = CONV_CH // CONV_TILE
    c0 = SSD_WIDTH // CONV_TILE

    def body(u_ref, w_ref, b_ref, o_ref):
        w = w_ref[...]
        bias = b_ref[...]

        def chunk(i, carry):
            r0, pre, _ = _conv_taps(u_ref, i, w, bias)
            o_ref[pl.ds(r0, CONV_ROWS), :] = pre * jax.nn.sigmoid(pre)
            return carry

        lax.fori_loop(0, t // CONV_ROWS, chunk, 0)

    return pl.pallas_call(
        body, name="conv_silu_fwd", grid=(bl, nct),
        in_specs=[pl.BlockSpec((t, CONV_TILE), lambda b, c: (b, c0 + c)),
                  pl.BlockSpec((4, CONV_TILE), lambda b, c: (0, c)),
                  pl.BlockSpec((1, CONV_TILE), lambda b, c: (0, c))],
        out_specs=pl.BlockSpec((t, CONV_TILE), lambda b, c: (b, c)),
        out_shape=jax.ShapeDtypeStruct((n, CONV_CH), F32),
        compiler_params=_cparams(("parallel", "parallel")),
    )(proj_a, conv_w, conv_b)


def _conv_bwd(dxc, proj_a, conv_w, conv_b, bl, t):
    n = bl * t
    nct = CONV_CH // CONV_TILE
    c0 = SSD_WIDTH // CONV_TILE
    nrc = t // CONV_ROWS

    def body(g_ref, u_ref, w_ref, b_ref, du_ref, dw_ref, db_ref, dp_scr):
        @pl.when(pl.program_id(1) == 0)
        def _():
            dw_ref[...] = jnp.zeros_like(dw_ref)
            db_ref[...] = jnp.zeros_like(db_ref)

        w = w_ref[...]
        bias = b_ref[...]
        dp_scr[pl.ds(t, 8), :] = jnp.zeros((8, CONV_TILE), F32)

        def chunk1(i, carry):
            dw0, dw1, dw2, dw3, db = carry
            r0, pre, taps = _conv_taps(u_ref, i, w, bias)
            sg = jax.nn.sigmoid(pre)
            dpre = g_ref[pl.ds(r0, CONV_ROWS), :] * (sg * (1.0 + pre * (1.0 - sg)))
            dp_scr[pl.ds(r0, CONV_ROWS), :] = dpre
            dw3 = dw3 + jnp.sum(dpre * taps[0], axis=0, keepdims=True)
            dw2 = dw2 + jnp.sum(dpre * taps[1], axis=0, keepdims=True)
            dw1 = dw1 + jnp.sum(dpre * taps[2], axis=0, keepdims=True)
            dw0 = dw0 + jnp.sum(dpre * taps[3], axis=0, keepdims=True)
            db = db + jnp.sum(dpre, axis=0, keepdims=True)
            return dw0, dw1, dw2, dw3, db

        z = jnp.zeros((1, CONV_TILE), F32)
        dw0, dw1, dw2, dw3, db = lax.fori_loop(0, nrc, chunk1, (z, z, z, z, z))
        dw_ref[...] += jnp.concatenate([dw0, dw1, dw2, dw3], axis=0)
        db_ref[...] += db

        def chunk2(i, carry):
            r0 = pl.multiple_of(i * CONV_ROWS, CONV_ROWS)
            cat = dp_scr[pl.ds(r0, CONV_ROWS + 8), :]
            du = w[3:4, :] * cat[:CONV_ROWS, :]
            for s in (1, 2, 3):
                du = du + w[3 - s:4 - s, :] * pltpu.roll(cat, CONV_ROWS + 8 - s, 0)[:CONV_ROWS, :]
            du_ref[pl.ds(r0, CONV_ROWS), :] = du.astype(BF16)
            return carry

        lax.fori_loop(0, nrc, chunk2, 0)

    return pl.pallas_call(
        body, name="conv_silu_bwd", grid=(nct, bl),
        in_specs=[pl.BlockSpec((t, CONV_TILE), lambda c, b: (b, c)),
                  pl.BlockSpec((t, CONV_TILE), lambda c, b: (b, c0 + c)),
                  pl.BlockSpec((4, CONV_TILE), lambda c, b: (0, c)),
                  pl.BlockSpec((1, CONV_TILE), lambda c, b: (0, c))],
        out_specs=[pl.BlockSpec((t, CONV_TILE), lambda c, b: (b, c)),
                   pl.BlockSpec((4, CONV_TILE), lambda c, b: (0, c)),
                   pl.BlockSpec((1, CONV_TILE), lambda c, b: (0, c))],
        out_shape=[jax.ShapeDtypeStruct((n, CONV_CH), BF16), jax.ShapeDtypeStruct((4, CONV_CH), F32),
                   jax.ShapeDtypeStruct((1, CONV_CH), F32)],
        scratch_shapes=[pltpu.VMEM((t + 8, CONV_TILE), F32)],
        compiler_params=_cparams(("parallel", "arbitrary")),
    )(dxc, proj_a, conv_w, conv_b)


NT_DIMS = (((1,), (1,)), ((), ()))
TN_DIMS = (((0,), (0,)), ((), ()))


def _dot(a, b, dims=None):
    if dims is None:
        return jnp.dot(a, b, preferred_element_type=F32)
    return lax.dot_general(a, b, dims, preferred_element_type=F32)


def _ssd_fwd(xc, proj_a, dt, acum, acum_t, dskip_e, norm_w, bl, t):
    n = bl * t
    nch = t // CHUNK
    L = CHUNK

    def body(xc_ref, z_ref, dt_ref, ac_ref, act_ref, dsk_ref, nw_ref, ys_ref, yp_ref, hp_ref, h_scr, y_scr):
        @pl.when(pl.program_id(1) == 0)
        def _():
            h_scr[...] = jnp.zeros_like(h_scr)

        row = lax.broadcasted_iota(jnp.int32, (L, L), 0)
        col = lax.broadcasted_iota(jnp.int32, (L, L), 1)
        causal = row >= col
        dt_all = dt_ref[...]
        ac_all = ac_ref[...]
        act_all = act_ref[...]
        for g in range(2):
            bg = xc_ref[:, SSD_WIDTH + g * 128:SSD_WIDTH + (g + 1) * 128].astype(BF16)
            cg = xc_ref[:, SSD_WIDTH + 256 + g * 128:SSD_WIDTH + 256 + (g + 1) * 128].astype(BF16)
            gmat = _dot(cg, bg, NT_DIMS)
            for r in range(8):
                h = g * 8 + r
                sl = slice(h * HEAD_DIM, (h + 1) * HEAD_DIM)
                xs = xc_ref[:, sl]
                xdt = xs * dt_all[:, h:h + 1]
                ac = ac_all[:, h:h + 1]
                ar = act_all[h:h + 1, :]
                ldec = jnp.exp(jnp.where(causal, ac - ar, NEG))
                m = (gmat * ldec).astype(BF16)
                hp = h_scr[h]
                hp_ref[h] = hp
                yd = _dot(m, xdt.astype(BF16))
                yo = _dot(cg, hp.astype(BF16), NT_DIMS) * jnp.exp(ac)
                y_scr[:, sl] = yd + yo + dsk_ref[:, sl] * xs
                alast = ac_all[L - 1:L, h:h + 1]
                xd = (xdt * jnp.exp(alast - ac)).astype(BF16)
                h_scr[h] = jnp.exp(alast) * hp + _dot(xd, bg, TN_DIMS)
        y = y_scr[...]
        yp_ref[...] = y
        zv = z_ref[...]
        yg = y * (zv * jax.nn.sigmoid(zv))
        for g in range(2):
            gs = slice(g * 512, (g + 1) * 512)
            grp = yg[:, gs]
            rstd = lax.rsqrt(jnp.mean(grp * grp, axis=1, keepdims=True) + EPS)
            ys_ref[:, gs] = (grp * rstd * nw_ref[:, gs]).astype(BF16)

    rb = lambda b, c: (b * nch + c, 0)
    v1k = pl.BlockSpec((1, SSD_WIDTH), lambda b, c: (0, 0))
    return pl.pallas_call(
        body, name="ssd_fwd", grid=(bl, nch),
        in_specs=[pl.BlockSpec((L, CONV_CH), rb), pl.BlockSpec((L, SSD_WIDTH), rb),
                  pl.BlockSpec((L, 16), rb), pl.BlockSpec((L, 16), rb),
                  pl.BlockSpec((16, L), lambda b, c: (0, b * nch + c)), v1k, v1k],
        out_specs=[pl.BlockSpec((L, SSD_WIDTH), rb), pl.BlockSpec((L, SSD_WIDTH), rb),
                   pl.BlockSpec((None, 16, HEAD_DIM, SSD_STATE), lambda b, c: (b * nch + c, 0, 0, 0))],
        out_shape=[jax.ShapeDtypeStruct((n, SSD_WIDTH), BF16), jax.ShapeDtypeStruct((n, SSD_WIDTH), F32),
                   jax.ShapeDtypeStruct((bl * nch, 16, HEAD_DIM, SSD_STATE), F32)],
        scratch_shapes=[pltpu.VMEM((16, HEAD_DIM, SSD_STATE), F32), pltpu.VMEM((L, SSD_WIDTH), F32)],
        compiler_params=_cparams(("parallel", "arbitrary")),
    )(xc, proj_a, dt, acum, acum_t, dskip_e, norm_w)


def _ssd_bwd(dys, xc, proj_a, ypre, hprev, dt, sig, acum, acum_t, a_log, dskip_e, norm_w, bl, t):
    n = bl * t
    nch = t // CHUNK
    L = CHUNK

    def body(dys_ref, xc_ref, z_ref, yp_ref, hp_ref, dt_ref, sg_ref, ac_ref, act_ref, al_ref, dsk_ref, nw_ref,
             dxc_ref, dz_ref, ddt_ref, dnw_ref, dsk16_ref, da16_ref, db16_ref, dh_scr, dy_scr):
        first = (pl.program_id(0) == 0) & (pl.program_id(1) == 0)

        @pl.when(first)
        def _():
            dnw_ref[...] = jnp.zeros_like(dnw_ref)
            dsk16_ref[...] = jnp.zeros_like(dsk16_ref)
            da16_ref[...] = jnp.zeros_like(da16_ref)
            db16_ref[...] = jnp.zeros_like(db16_ref)

        @pl.when(pl.program_id(1) == 0)
        def _():
            dh_scr[...] = jnp.zeros_like(dh_scr)

        y = yp_ref[...]
        zv = z_ref[...]
        sz = jax.nn.sigmoid(zv)
        gate = zv * sz
        yg = y * gate
        dout = dys_ref[...]
        nw = nw_ref[...]
        for g in range(2):
            gs = slice(g * 512, (g + 1) * 512)
            grp = yg[:, gs]
            rstd = lax.rsqrt(jnp.mean(grp * grp, axis=1, keepdims=True) + EPS)
            ghat = grp * rstd
            dnw_ref[:, gs] += jnp.sum(dout[:, gs] * ghat, axis=0, keepdims=True)
            gw = dout[:, gs] * nw[:, gs]
            dyg = rstd * (gw - ghat * jnp.mean(gw * ghat, axis=1, keepdims=True))
            dy_scr[:, gs] = dyg * gate[:, gs]
            dz_ref[:, gs] = (dyg * y[:, gs] * (sz[:, gs] * (1.0 + zv[:, gs] * (1.0 - sz[:, gs])))).astype(BF16)

        row = lax.broadcasted_iota(jnp.int32, (L, L), 0)
        col = lax.broadcasted_iota(jnp.int32, (L, L), 1)
        causal = row >= col
        lane16 = lax.broadcasted_iota(jnp.int32, (1, 16), 1)
        lane128 = lax.broadcasted_iota(jnp.int32, (1, L), 1)
        last_row = lax.broadcasted_iota(jnp.int32, (L, 1), 0) == (L - 1)
        dt_all = dt_ref[...]
        ac_all = ac_ref[...]
        act_all = act_ref[...]
        dac_col = jnp.zeros((L, L), F32)
        dac_row = jnp.zeros((L, L), F32)
        ddt_x = jnp.zeros((L, L), F32)
        dsk16 = jnp.zeros((1, 16), F32)
        rows16 = lax.broadcasted_iota(jnp.int32, (L, 1), 0)
        for g in range(2):
            bsl = slice(SSD_WIDTH + g * 128, SSD_WIDTH + (g + 1) * 128)
            csl = slice(SSD_WIDTH + 256 + g * 128, SSD_WIDTH + 256 + (g + 1) * 128)
            bg = xc_ref[:, bsl].astype(BF16)
            cg = xc_ref[:, csl].astype(BF16)
            gmat = _dot(cg, bg, NT_DIMS)
            dg_sum = jnp.zeros((L, L), F32)
            dc_acc = jnp.zeros((L, SSD_STATE), F32)
            db_acc = jnp.zeros((L, SSD_STATE), F32)
            for r in range(8):
                h = g * 8 + r
                sl = slice(h * HEAD_DIM, (h + 1) * HEAD_DIM)
                onehot = lane16 == h
                onehot_w = lane128 == h
                xs = xc_ref[:, sl]
                dth = dt_all[:, h:h + 1]
                xdt = xs * dth
                xb = xdt.astype(BF16)
                ac = ac_all[:, h:h + 1]
                ar = act_all[h:h + 1, :]
                alast = ac_all[L - 1:L, h:h + 1]
                ldec = jnp.exp(jnp.where(causal, ac - ar, NEG))
                mf = gmat * ldec
                e_in = jnp.exp(ac)
                dec = jnp.exp(alast - ac)
                elast = jnp.exp(alast)
                hp = hp_ref[h]
                hpb = hp.astype(BF16)
                dyh = dy_scr[:, sl]
                dyb = dyh.astype(BF16)
                dsk16 = dsk16 + jnp.where(onehot, jnp.sum(jnp.sum(dyh * xs, axis=1, keepdims=True), axis=0, keepdims=True), 0.0)
                dm = _dot(dyb, xb, NT_DIMS)
                dx = _dot(mf.astype(BF16), dyb, TN_DIMS)
                dg_sum = dg_sum + dm * ldec
                wmat = dm * mf
                dac_h = jnp.sum(wmat, axis=1, keepdims=True)
                dac_row = dac_row + jnp.where(rows16 == h, -jnp.sum(wmat, axis=0, keepdims=True), 0.0)
                ch = _dot(cg, hpb, NT_DIMS)
                dye = dyh * e_in
                dyeb = dye.astype(BF16)
                dc_acc = dc_acc + _dot(dyeb, hpb)
                dhp = _dot(dyeb, cg, TN_DIMS)
                dac_h = dac_h + jnp.sum(dye * ch, axis=1, keepdims=True)
                ds = dh_scr[h]
                dsb = ds.astype(BF16)
                dxd = _dot(bg, dsb, NT_DIMS)
                db_acc = db_acc + _dot((xdt * dec).astype(BF16), dsb)
                dx = dx + dxd * dec
                ddec = jnp.sum(dxd * xdt, axis=1, keepdims=True) * dec
                extra = (jnp.sum(ddec, axis=0, keepdims=True)
                         + elast * jnp.sum(jnp.sum(hp * ds, axis=1, keepdims=True), axis=0, keepdims=True))
                dac_h = dac_h - ddec + jnp.where(last_row, extra, 0.0)
                dh_scr[h] = elast * ds + dhp
                dac_col = dac_col + jnp.where(onehot_w, dac_h, 0.0)
                ddt_x = ddt_x + jnp.where(onehot_w, jnp.sum(dx * xs, axis=1, keepdims=True), 0.0)
                dxc_ref[:, sl] = dx * dth + dsk_ref[:, sl] * dyh
            dgb = dg_sum.astype(BF16)
            dxc_ref[:, csl] = dc_acc + _dot(dgb, bg)
            dxc_ref[:, bsl] = db_acc + _dot(dgb, cg, TN_DIMS)
        dac = dac_col + jnp.transpose(dac_row)
        triu = (row <= col).astype(F32)
        da = jnp.dot(triu, dac, precision=HIGHEST, preferred_element_type=F32)[:, 0:16]
        a_row = -jnp.exp(al_ref[...])
        ddt = (ddt_x[:, 0:16] + da * a_row) * sg_ref[...]
        ddt_ref[...] = ddt
        dsk16_ref[...] += dsk16
        da16_ref[...] += jnp.sum(da * dt_all, axis=0, keepdims=True) * a_row
        db16_ref[...] += jnp.sum(ddt, axis=0, keepdims=True)

    rb = lambda b, c: (b * nch + nch - 1 - c, 0)
    v1k = pl.BlockSpec((1, SSD_WIDTH), lambda b, c: (0, 0))
    v16 = pl.BlockSpec((1, 16), lambda b, c: (0, 0))
    wide = pl.BlockSpec((L, SSD_WIDTH), rb)
    s16 = pl.BlockSpec((L, 16), rb)
    return pl.pallas_call(
        body, name="ssd_bwd", grid=(bl, nch),
        in_specs=[wide, pl.BlockSpec((L, CONV_CH), rb), wide, wide,
                  pl.BlockSpec((None, 16, HEAD_DIM, SSD_STATE), lambda b, c: (b * nch + nch - 1 - c, 0, 0, 0)),
                  s16, s16, s16, pl.BlockSpec((16, L), lambda b, c: (0, b * nch + nch - 1 - c)), v16, v1k, v1k],
        out_specs=[pl.BlockSpec((L, CONV_CH), rb), wide, s16, v1k, v16, v16, v16],
        out_shape=[jax.ShapeDtypeStruct((n, CONV_CH), F32), jax.ShapeDtypeStruct((n, SSD_WIDTH), BF16),
                   jax.ShapeDtypeStruct((n, 16), F32), jax.ShapeDtypeStruct((1, SSD_WIDTH), F32),
                   jax.ShapeDtypeStruct((1, 16), F32), jax.ShapeDtypeStruct((1, 16), F32),
                   jax.ShapeDtypeStruct((1, 16), F32)],
        scratch_shapes=[pltpu.VMEM((16, HEAD_DIM, SSD_STATE), F32), pltpu.VMEM((L, SSD_WIDTH), F32)],
        compiler_params=_cparams(("arbitrary", "arbitrary")),
    )(dys, xc, proj_a, ypre, hprev, dt, sig, acum, acum_t, a_log, dskip_e, norm_w)


def _head_expander():
    r = lax.broadcasted_iota(jnp.int32, (128, SSD_WIDTH), 0)
    c = lax.broadcasted_iota(jnp.int32, (128, SSD_WIDTH), 1)
    return (c // HEAD_DIM == r).astype(F32)


def _spread(v128, expander):
    return jnp.dot(v128, expander, precision=HIGHEST, preferred_element_type=F32)


def _head_sums(v1024, expander):
    return lax.dot_general(v1024, expander, NT_DIMS, precision=HIGHEST, preferred_element_type=F32)


def _ssd_fwd2(xc, proj_a, dt, acum, acum_t, dskip_e, norm_w, bl, t):
    n = bl * t
    nch = t // CHUNK
    L = CHUNK

    def body(xc_ref, z_ref, dt_ref, ac_ref, act_ref, dsk_ref, nw_ref, ys_ref, yp_ref, hp_ref, h_scr, y_scr, x_scr):
        @pl.when(pl.program_id(1) == 0)
        def _():
            h_scr[...] = jnp.zeros_like(h_scr)

        row = lax.broadcasted_iota(jnp.int32, (L, L), 0)
        col = lax.broadcasted_iota(jnp.int32, (L, L), 1)
        causal = row >= col
        expander = _head_expander()
        ac_all = ac_ref[...]
        act_all = act_ref[...]
        ac_e = _spread(ac_all, expander)
        e_in = jnp.exp(ac_e)
        dec = jnp.exp(ac_e[L - 1:L, :] - ac_e)
        xs_all = xc_ref[:, 0:SSD_WIDTH]
        x_all = xs_all * _spread(dt_ref[...], expander)
        x_scr[...] = x_all.astype(BF16)
        hp_all = h_scr[...]
        hp_ref[...] = hp_all
        for g in range(2):
            gs = slice(g * 512, (g + 1) * 512)
            bg = xc_ref[:, SSD_WIDTH + g * 128:SSD_WIDTH + (g + 1) * 128].astype(BF16)
            cg = xc_ref[:, SSD_WIDTH + 256 + g * 128:SSD_WIDTH + 256 + (g + 1) * 128].astype(BF16)
            gmat = _dot(cg, bg, NT_DIMS)
            y_scr[:, gs] = (_dot(cg, hp_all[gs, :].astype(BF16), NT_DIMS) * e_in[:, gs]
                            + dsk_ref[:, gs] * xs_all[:, gs])
            s_new = _dot((x_all[:, gs] * dec[:, gs]).astype(BF16), bg, TN_DIMS)
            for r in range(8):
                h = g * 8 + r
                sl = slice(h * HEAD_DIM, (h + 1) * HEAD_DIM)
                ldec = jnp.exp(jnp.where(causal, ac_all[:, h:h + 1] - act_all[h:h + 1, :], NEG))
                y_scr[:, sl] += _dot((gmat * ldec).astype(BF16), x_scr[:, sl])
                elast = jnp.exp(ac_all[L - 1:L, h:h + 1])
                h_scr[sl, :] = elast * hp_all[sl, :] + s_new[r * HEAD_DIM:(r + 1) * HEAD_DIM, :]
        y = y_scr[...]
        yp_ref[...] = y
        zv = z_ref[...]
        yg = y * (zv * jax.nn.sigmoid(zv))
        for g in range(2):
            gs = slice(g * 512, (g + 1) * 512)
            grp = yg[:, gs]
            rstd = lax.rsqrt(jnp.mean(grp * grp, axis=1, keepdims=True) + EPS)
            ys_ref[:, gs] = (grp * rstd * nw_ref[:, gs]).astype(BF16)

    rb = lambda b, c: (b * nch + c, 0)
    v1k = pl.BlockSpec((1, SSD_WIDTH), lambda b, c: (0, 0))
    return pl.pallas_call(
        body, name="ssd_fwd", grid=(bl, nch),
        in_specs=[pl.BlockSpec((L, CONV_CH), rb), pl.BlockSpec((L, SSD_WIDTH), rb),
                  pl.BlockSpec((L, 128), rb), pl.BlockSpec((L, 128), rb),
                  pl.BlockSpec((16, L), lambda b, c: (0, b * nch + c)), v1k, v1k],
        out_specs=[pl.BlockSpec((L, SSD_WIDTH), rb), pl.BlockSpec((L, SSD_WIDTH), rb),
                   pl.BlockSpec((None, SSD_WIDTH, SSD_STATE), lambda b, c: (b * nch + c, 0, 0))],
        out_shape=[jax.ShapeDtypeStruct((n, SSD_WIDTH), BF16), jax.ShapeDtypeStruct((n, SSD_WIDTH), F32),
                   jax.ShapeDtypeStruct((bl * nch, SSD_WIDTH, SSD_STATE), F32)],
        scratch_shapes=[pltpu.VMEM((SSD_WIDTH, SSD_STATE), F32), pltpu.VMEM((L, SSD_WIDTH), F32),
                        pltpu.VMEM((L, SSD_WIDTH), BF16)],
        compiler_params=_cparams(("parallel", "arbitrary")),
    )(xc, proj_a, dt, acum, acum_t, dskip_e, norm_w)


def _ssd_bwd2(dys, xc, proj_a, ypre, hprev, dt, sig, acum, acum_t, a_log, dskip_e, norm_w, bl, t):
    n = bl * t
    nch = t // CHUNK
    L = CHUNK

    def body(dys_ref, xc_ref, z_ref, yp_ref, hp_ref, dt_ref, sg_ref, ac_ref, act_ref, al_ref, dsk_ref, nw_ref,
             dxc_ref, dz_ref, ddt_ref, dnw_ref, dsk16_ref, da16_ref, db16_ref,
             dh_scr, dy_scr, x_scr, dx_scr, red_scr):
        first = (pl.program_id(0) == 0) & (pl.program_id(1) == 0)

        @pl.when(first)
        def _():
            dnw_ref[...] = jnp.zeros_like(dnw_ref)
            dsk16_ref[...] = jnp.zeros_like(dsk16_ref)
            da16_ref[...] = jnp.zeros_like(da16_ref)
            db16_ref[...] = jnp.zeros_like(db16_ref)

        @pl.when(pl.program_id(1) == 0)
        def _():
            dh_scr[...] = jnp.zeros_like(dh_scr)

        y = yp_ref[...]
        zv = z_ref[...]
        sz = jax.nn.sigmoid(zv)
        gate = zv * sz
        yg = y * gate
        dout = dys_ref[...]
        nw = nw_ref[...]
        for g in range(2):
            gs = slice(g * 512, (g + 1) * 512)
            grp = yg[:, gs]
            rstd = lax.rsqrt(jnp.mean(grp * grp, axis=1, keepdims=True) + EPS)
            ghat = grp * rstd
            dnw_ref[:, gs] += jnp.sum(dout[:, gs] * ghat, axis=0, keepdims=True)
            gw = dout[:, gs] * nw[:, gs]
            dyg = rstd * (gw - ghat * jnp.mean(gw * ghat, axis=1, keepdims=True))
            dy_scr[:, gs] = dyg * gate[:, gs]
            dz_ref[:, gs] = (dyg * y[:, gs] * (sz[:, gs] * (1.0 + zv[:, gs] * (1.0 - sz[:, gs])))).astype(BF16)

        row = lax.broadcasted_iota(jnp.int32, (L, L), 0)
        col = lax.broadcasted_iota(jnp.int32, (L, L), 1)
        causal = row >= col
        lane128 = lax.broadcasted_iota(jnp.int32, (1, L), 1)
        rows128 = lax.broadcasted_iota(jnp.int32, (L, 1), 0)
        last_row = rows128 == (L - 1)
        expander = _head_expander()
        ac_all = ac_ref[...]
        act_all = act_ref[...]
        dt_all = dt_ref[...]
        dt_e = _spread(dt_all, expander)
        ac_e = _spread(ac_all, expander)
        e_in = jnp.exp(ac_e)
        dec = jnp.exp(ac_e[L - 1:L, :] - ac_e)
        xs_all = xc_ref[:, 0:SSD_WIDTH]
        x_all = xs_all * dt_e
        x_scr[...] = x_all.astype(BF16)
        dy_all = dy_scr[...]
        hp_all = hp_ref[...]
        ds_all = dh_scr[...]
        dsk_cols = jnp.sum(dy_all * xs_all, axis=0, keepdims=True)
        dac = jnp.zeros((L, L), F32)
        dac_row = jnp.zeros((L, L), F32)
        ddec_cols = []
        for g in range(2):
            gs = slice(g * 512, (g + 1) * 512)
            bsl = slice(SSD_WIDTH + g * 128, SSD_WIDTH + (g + 1) * 128)
            csl = slice(SSD_WIDTH + 256 + g * 128, SSD_WIDTH + 256 + (g + 1) * 128)
            bg = xc_ref[:, bsl].astype(BF16)
            cg = xc_ref[:, csl].astype(BF16)
            gmat = _dot(cg, bg, NT_DIMS)
            hpb = hp_all[gs, :].astype(BF16)
            dsb = ds_all[gs, :].astype(BF16)
            ch = _dot(cg, hpb, NT_DIMS)
            dye = dy_all[:, gs] * e_in[:, gs]
            dyeb = dye.astype(BF16)
            dc_acc = _dot(dyeb, hpb)
            dhp = _dot(dyeb, cg, TN_DIMS)
            dxd = _dot(bg, dsb, NT_DIMS)
            db_acc = _dot((x_all[:, gs] * dec[:, gs]).astype(BF16), dsb)
            ddec = dxd * x_all[:, gs] * dec[:, gs]
            ddec_cols.append(jnp.sum(ddec, axis=0, keepdims=True))
            dx_scr[:, gs] = dxd * dec[:, gs]
            red_scr[:, gs] = dye * ch - ddec
            dg_sum = jnp.zeros((L, L), F32)
            for r in range(8):
                h = g * 8 + r
                sl = slice(h * HEAD_DIM, (h + 1) * HEAD_DIM)
                onehot_w = lane128 == h
                ldec = jnp.exp(jnp.where(causal, ac_all[:, h:h + 1] - act_all[h:h + 1, :], NEG))
                mf = gmat * ldec
                dyb = dy_scr[:, sl].astype(BF16)
                dm = _dot(dyb, x_scr[:, sl], NT_DIMS)
                dx_scr[:, sl] += _dot(mf.astype(BF16), dyb, TN_DIMS)
                dg_sum = dg_sum + dm * ldec
                wmat = dm * mf
                elast = jnp.exp(ac_all[L - 1:L, h:h + 1])
                hp_h = hp_all[sl, :]
                ds_h = ds_all[sl, :]
                extra = elast * jnp.sum(jnp.sum(hp_h * ds_h, axis=1, keepdims=True), axis=0, keepdims=True)
                dac = dac + jnp.where(onehot_w, jnp.sum(wmat, axis=1, keepdims=True) + jnp.where(last_row, extra, 0.0),
                                      0.0)
                dac_row = dac_row + jnp.where(rows128 == h, -jnp.sum(wmat, axis=0, keepdims=True), 0.0)
                dh_scr[sl, :] = elast * ds_h + dhp[r * HEAD_DIM:(r + 1) * HEAD_DIM, :]
            dgb = dg_sum.astype(BF16)
            dxc_ref[:, csl] = dc_acc + _dot(dgb, bg)
            dxc_ref[:, bsl] = db_acc + _dot(dgb, cg, TN_DIMS)
        dx_all = dx_scr[...]
        dxc_ref[:, 0:SSD_WIDTH] = dx_all * dt_e + dsk_ref[...] * dy_all
        red = red_scr[...]
        dac_slab = _head_sums(red, expander)
        ddec_tot = _head_sums(jnp.broadcast_to(jnp.concatenate(ddec_cols, axis=1), (8, SSD_WIDTH)), expander)
        ddt_x = _head_sums(dx_all * xs_all, expander)
        dsk16_ref[...] += _head_sums(jnp.broadcast_to(dsk_cols, (8, SSD_WIDTH)), expander)[0:1, 0:16]
        dac = dac + dac_slab + jnp.transpose(dac_row) + jnp.where(last_row, ddec_tot[0:1, :], 0.0)
        triu = (row <= col).astype(F32)
        da = jnp.dot(triu, dac, precision=HIGHEST, preferred_element_type=F32)[:, 0:16]
        a_row = -jnp.exp(al_ref[...])
        dt16 = dt_all[:, 0:16]
        ddt = (ddt_x[:, 0:16] + da * a_row) * sg_ref[...]
        ddt_ref[...] = ddt
        da16_ref[...] += jnp.sum(da * dt16, axis=0, keepdims=True) * a_row
        db16_ref[...] += jnp.sum(ddt, axis=0, keepdims=True)

    rb = lambda b, c: (b * nch + nch - 1 - c, 0)
    v1k = pl.BlockSpec((1, SSD_WIDTH), lambda b, c: (0, 0))
    v16 = pl.BlockSpec((1, 16), lambda b, c: (0, 0))
    wide = pl.BlockSpec((L, SSD_WIDTH), rb)
    s16 = pl.BlockSpec((L, 16), rb)
    s128 = pl.BlockSpec((L, 128), rb)
    return pl.pallas_call(
        body, name="ssd_bwd", grid=(bl, nch),
        in_specs=[wide, pl.BlockSpec((L, CONV_CH), rb), wide, wide,
                  pl.BlockSpec((None, SSD_WIDTH, SSD_STATE), lambda b, c: (b * nch + nch - 1 - c, 0, 0)),
                  s128, s16, s128, pl.BlockSpec((16, L), lambda b, c: (0, b * nch + nch - 1 - c)), v16, v1k, v1k],
        out_specs=[pl.BlockSpec((L, CONV_CH), rb), wide, s16, v1k, v16, v16, v16],
        out_shape=[jax.ShapeDtypeStruct((n, CONV_CH), F32), jax.ShapeDtypeStruct((n, SSD_WIDTH), BF16),
                   jax.ShapeDtypeStruct((n, 16), F32), jax.ShapeDtypeStruct((1, SSD_WIDTH), F32),
                   jax.ShapeDtypeStruct((1, 16), F32), jax.ShapeDtypeStruct((1, 16), F32),
                   jax.ShapeDtypeStruct((1, 16), F32)],
        scratch_shapes=[pltpu.VMEM((SSD_WIDTH, SSD_STATE), F32), pltpu.VMEM((L, SSD_WIDTH), F32),
                        pltpu.VMEM((L, SSD_WIDTH), BF16), pltpu.VMEM((L, SSD_WIDTH), F32),
                        pltpu.VMEM((L, SSD_WIDTH), F32)],
        compiler_params=_cparams(("arbitrary", "arbitrary")),
    )(dys, xc, proj_a, ypre, hprev, dt, sig, acum, acum_t, a_log, dskip_e, norm_w)


def _attn_fwd(qkv, negc, bl, t):
    n = bl * t
    tb_ = ATT_BLOCK
    nb = t // tb_
    scale = 1.0 / math.sqrt(HEAD_DIM)

    def body(q_ref, k_ref, v_ref, c_ref, o_ref, lse_ref):
        row = lax.broadcasted_iota(jnp.int32, (tb_, tb_), 0)
        col = lax.broadcasted_iota(jnp.int32, (tb_, tb_), 1)
        causal = row >= col
        for qi in range(nb):
            r0, lk = qi * tb_, (qi + 1) * tb_
            for j in range(2):
                sl = slice(j * HEAD_DIM, (j + 1) * HEAD_DIM)
                s = _dot(q_ref[r0:lk, sl], k_ref[0:lk, sl], NT_DIMS) * scale + c_ref[j:j + 1, 0:lk]
                tail = jnp.where(causal, s[:, r0:lk], NEG)
                s = tail if qi == 0 else jnp.concatenate([s[:, 0:r0], tail], axis=1)
                m = jnp.max(s, axis=1, keepdims=True)
                p = jnp.exp(s - m)
                l = jnp.sum(p, axis=1, keepdims=True)
                acc = _dot(p.astype(BF16), v_ref[0:lk, sl])
                o_ref[r0:lk, sl] = (acc / l).astype(BF16)
                lse_ref[r0:lk, sl] = jnp.broadcast_to(m + jnp.log(l), (tb_, HEAD_DIM))

    blk = lambda off: pl.BlockSpec((t, 128), lambda b, hp: (b, off + hp))
    return pl.pallas_call(
        body, name="fox_attn_fwd", grid=(bl, 8),
        in_specs=[blk(0), blk(8), blk(16), pl.BlockSpec((None, None, 8, t), lambda b, hp: (b, hp, 0, 0))],
        out_specs=[blk(0), blk(0)],
        out_shape=[jax.ShapeDtypeStruct((n, ATT_WIDTH), BF16), jax.ShapeDtypeStruct((n, ATT_WIDTH), F32)],
        compiler_params=_cparams(("parallel", "parallel")),
    )(qkv, qkv, qkv, negc)


def _attn_bwd(qkv, do, o, lse, negc, bl, t):
    n = bl * t
    tb_ = ATT_BLOCK
    nb = t // tb_
    scale = 1.0 / math.sqrt(HEAD_DIM)

    def body(q_ref, k_ref, v_ref, do_ref, o_ref, lse_ref, c_ref, dq_ref, dk_ref, dv_ref, dc_ref, dr_ref,
             dq_scr, delta_scr):
        row = lax.broadcasted_iota(jnp.int32, (tb_, tb_), 0)
        col = lax.broadcasted_iota(jnp.int32, (tb_, tb_), 1)
        causal = row >= col
        dq_scr[...] = jnp.zeros_like(dq_scr)
        dr_ref[...] = jnp.zeros_like(dr_ref)
        dc_ref[...] = jnp.zeros_like(dc_ref)
        prod = do_ref[...].astype(F32) * o_ref[...].astype(F32)
        for j in range(2):
            sl = slice(j * HEAD_DIM, (j + 1) * HEAD_DIM)
            delta_scr[:, sl] = jnp.broadcast_to(jnp.sum(prod[:, sl], axis=1, keepdims=True), (t, HEAD_DIM))
        for kj in range(nb):
            r0, r1 = kj * tb_, (kj + 1) * tb_
            for j in range(2):
                sl = slice(j * HEAD_DIM, (j + 1) * HEAD_DIM)
                one = slice(j * HEAD_DIM, j * HEAD_DIM + 1)
                kb = k_ref[r0:r1, sl]
                qs = q_ref[r0:t, sl]
                dos = do_ref[r0:t, sl]
                s = _dot(qs, kb, NT_DIMS) * scale + c_ref[j:j + 1, r0:r1]
                head = jnp.where(causal, s[0:tb_, :], NEG)
                s = head if kj == nb - 1 else jnp.concatenate([head, s[tb_:, :]], axis=0)
                p = jnp.exp(s - lse_ref[r0:t, one])
                dp = _dot(dos, v_ref[r0:r1, sl], NT_DIMS)
                ds = p * (dp - delta_scr[r0:t, one])
                dsb = ds.astype(BF16)
                dv_ref[r0:r1, sl] = _dot(p.astype(BF16), dos, TN_DIMS).astype(BF16)
                dk_ref[r0:r1, sl] = (_dot(dsb, qs, TN_DIMS) * scale).astype(BF16)
                dq_scr[r0:t, sl] += _dot(dsb, kb)
                dr_ref[r0:t, sl] += jnp.broadcast_to(jnp.sum(ds, axis=1, keepdims=True), (t - r0, HEAD_DIM))
                dc_ref[j:j + 1, r0:r1] = -jnp.sum(ds, axis=0, keepdims=True)
        dq_ref[...] = (dq_scr[...] * scale).astype(BF16)

    blk = lambda off: pl.BlockSpec((t, 128), lambda b, hp: (b, off + hp))
    cblk = pl.BlockSpec((None, None, 8, t), lambda b, hp: (b, hp, 0, 0))
    return pl.pallas_call(
        body, name="fox_attn_bwd", grid=(bl, 8),
        in_specs=[blk(0), blk(8), blk(16), blk(0), blk(0), blk(0), cblk],
        out_specs=[blk(0), blk(0), blk(0), cblk, blk(0)],
        out_shape=[jax.ShapeDtypeStruct((n, ATT_WIDTH), BF16)] * 3
        + [jax.ShapeDtypeStruct((bl, 8, 8, t), F32), jax.ShapeDtypeStruct((n, ATT_WIDTH), F32)],
        scratch_shapes=[pltpu.VMEM((t, 128), F32), pltpu.VMEM((t, 128), F32)],
        compiler_params=_cparams(("parallel", "parallel")),
    )(qkv, qkv, qkv, do, o, lse, negc)


def _adamw(w, g, m, v, *, name):
    r, c = w.shape
    tr = _pick(r, (256, 128, 64, 32, 16, 8))
    bc1 = 1.0 - ADAM_B1 ** ADAM_STEP
    bc2 = 1.0 - ADAM_B2 ** ADAM_STEP

    def body(w_ref, g_ref, m_ref, v_ref, d_ref, nm_ref, nv_ref):
        gv = g_ref[...]
        mn = ADAM_B1 * m_ref[...] + (1.0 - ADAM_B1) * gv
        vn = ADAM_B2 * v_ref[...] + (1.0 - ADAM_B2) * (gv * gv)
        m_hat = mn / bc1
        v_hat = vn / bc2
        d_ref[...] = -ADAM_LR * (m_hat / (jnp.sqrt(v_hat) + ADAM_EPS) + ADAM_WD * w_ref[...])
        nm_ref[...] = mn
        nv_ref[...] = vn

    blk = pl.BlockSpec((tr, c), lambda i: (i, 0))
    return pl.pallas_call(
        body, name=name, grid=(r // tr,), in_specs=[blk] * 4, out_specs=[blk] * 3,
        out_shape=[jax.ShapeDtypeStruct((r, c), F32)] * 3,
        compiler_params=_cparams(("parallel",)),
    )(w, g, m, v)


def _sum_leading(parts, *, name, out_dtype=F32):
    k, r, c = parts.shape
    tr = _pick(r, (512, 256, 128, 96, 64, 32, 16, 8))

    def body(p_ref, o_ref):
        acc = p_ref[0].astype(F32)
        for i in range(1, k):
            acc = acc + p_ref[i].astype(F32)
        o_ref[...] = acc.astype(out_dtype)

    return pl.pallas_call(
        body, name=name, grid=(r // tr,),
        in_specs=[pl.BlockSpec((k, tr, c), lambda i: (0, i, 0))],
        out_specs=pl.BlockSpec((tr, c), lambda i: (i, 0)),
        out_shape=jax.ShapeDtypeStruct((r, c), out_dtype),
        compiler_params=_cparams(("parallel",)),
    )(parts)


def _add_pair(a, b, *, name):
    k, r, c = a.shape
    tr = _pick(r, (512, 256, 128))

    def body(a_ref, b_ref, o_ref):
        o_ref[...] = (a_ref[...].astype(F32) + b_ref[...].astype(F32)).astype(BF16)

    blk = pl.BlockSpec((None, tr, c), lambda j, i: (j, i, 0))
    return pl.pallas_call(
        body, name=name, grid=(k, r // tr), in_specs=[blk, blk], out_specs=blk,
        out_shape=jax.ShapeDtypeStruct((k, r, c), BF16),
        compiler_params=_cparams(("parallel", "parallel")),
    )(a, b)


ANY = pl.BlockSpec(memory_space=pl.ANY)


def _chip_peers(x, y):
    return [(1 - x, y, 2 * (1 - x) + y), (x, 1 - y, 2 * x + 1 - y), (1 - x, 1 - y, 2 * (1 - x) + 1 - y)]


def _gather_weights(blob):
    def body(b_ref, o_ref, send_sems, recv_sems):
        x, y, c = lax.axis_index("x"), lax.axis_index("y"), lax.axis_index("c")
        me = 2 * x + y
        sibling = (x, y, 1 - c)
        peers = _chip_peers(x, y)

        def half(chip, hc):
            return o_ref.at[chip, pl.ds(hc * HALF_ROWS, HALF_ROWS), :]

        def copy(k, src, chip, hc, to):
            return pltpu.make_async_remote_copy(src_ref=src, dst_ref=half(chip, hc), send_sem=send_sems.at[k],
                                                recv_sem=recv_sems.at[k], device_id=to, device_id_type=MESH)

        my_half = b_ref.at[pl.ds(c * HALF_ROWS, HALF_ROWS), :]
        first = [copy(k, my_half, me, c, (px, py, c)) for k, (px, py, _) in enumerate(peers)]
        for cp in first:
            cp.start()
        passed = [copy(3 + k, half(pc, c), pc, c, sibling) for k, (_, _, pc) in enumerate(peers)]
        for k, (px, py, pc) in enumerate(peers):
            copy(k, my_half, pc, c, (px, py, c)).wait_recv()
            passed[k].start()
        for k, (_, _, pc) in enumerate(peers):
            copy(3 + k, half(pc, 1 - c), pc, 1 - c, sibling).wait_recv()
        for cp in first + passed:
            cp.wait_send()

    got = pl.pallas_call(
        body, name="gather_weights", in_specs=[ANY], out_specs=ANY,
        out_shape=jax.ShapeDtypeStruct((N_CHIPS, BLOB_ROWS, 1024), BF16),
        scratch_shapes=[pltpu.SemaphoreType.DMA((6,)), pltpu.SemaphoreType.DMA((6,))],
    )(blob)
    me = 2 * lax.axis_index("x") + lax.axis_index("y")
    return lax.dynamic_update_slice(got, blob[None], (me, 0, 0))


def _swap_halves(g):
    def body(g_ref, o_ref, send_sem, recv_sem):
        x, y, c = lax.axis_index("x"), lax.axis_index("y"), lax.axis_index("c")
        cp = pltpu.make_async_remote_copy(
            src_ref=g_ref.at[:, pl.ds((1 - c) * HALF_ROWS, HALF_ROWS), :], dst_ref=o_ref,
            send_sem=send_sem, recv_sem=recv_sem, device_id=(x, y, 1 - c), device_id_type=MESH)
        cp.start()
        cp.wait()

    return pl.pallas_call(
        body, name="grad_swap_halves", in_specs=[ANY], out_specs=ANY,
        out_shape=jax.ShapeDtypeStruct((N_CHIPS, HALF_ROWS, 1024), BF16),
        scratch_shapes=[pltpu.SemaphoreType.DMA, pltpu.SemaphoreType.DMA],
    )(g)


def _exchange_chips(p):
    def body(p_ref, o_ref, send_sems, recv_sems):
        x, y, c = lax.axis_index("x"), lax.axis_index("y"), lax.axis_index("c")
        me = 2 * x + y
        peers = _chip_peers(x, y)
        cps = [pltpu.make_async_remote_copy(src_ref=p_ref.at[pc], dst_ref=o_ref.at[me], send_sem=send_sems.at[k],
                                            recv_sem=recv_sems.at[k], device_id=(px, py, c), device_id_type=MESH)
               for k, (px, py, pc) in enumerate(peers)]
        for cp in cps:
            cp.start()
        for k, (px, py, pc) in enumerate(peers):
            pltpu.make_async_remote_copy(src_ref=p_ref.at[pc], dst_ref=o_ref.at[pc], send_sem=send_sems.at[k],
                                         recv_sem=recv_sems.at[k], device_id=(px, py, c),
                                         device_id_type=MESH).wait_recv()
        for cp in cps:
            cp.wait_send()

    got = pl.pallas_call(
        body, name="grad_exchange_chips", in_specs=[ANY], out_specs=ANY,
        out_shape=jax.ShapeDtypeStruct((N_CHIPS, HALF_ROWS, 1024), BF16),
        scratch_shapes=[pltpu.SemaphoreType.DMA((3,)), pltpu.SemaphoreType.DMA((3,))],
    )(p)
    me = 2 * lax.axis_index("x") + lax.axis_index("y")
    return lax.dynamic_update_slice(got, lax.dynamic_slice_in_dim(p, me, 1, axis=0), (me, 0, 0))


def _join_halves(gh):
    def body(g_ref, o_ref, send_sem, recv_sem):
        x, y, c = lax.axis_index("x"), lax.axis_index("y"), lax.axis_index("c")
        cp = pltpu.make_async_remote_copy(src_ref=g_ref, dst_ref=o_ref, send_sem=send_sem, recv_sem=recv_sem,
                                          device_id=(x, y, 1 - c), device_id_type=MESH)
        cp.start()
        cp.wait()

    other = pl.pallas_call(
        body, name="grad_join_halves", in_specs=[ANY], out_specs=ANY,
        out_shape=jax.ShapeDtypeStruct((HALF_ROWS, 1024), F32),
        scratch_shapes=[pltpu.SemaphoreType.DMA, pltpu.SemaphoreType.DMA],
    )(gh)
    south = lax.axis_index("c") == 0
    return jnp.concatenate([jnp.where(south, gh, other), jnp.where(south, other, gh)], axis=0)


def _gather_small(s, *, name):
    rows = s.shape[0]

    def body(s_ref, o_ref, send_sems, recv_sems, local_sem):
        x, y, c = lax.axis_index("x"), lax.axis_index("y"), lax.axis_index("c")
        me = 4 * x + 2 * y + c
        mine = pltpu.make_async_copy(s_ref, o_ref.at[me], local_sem)
        mine.start()
        peers = []
        for k in range(1, 8):
            peers.append((1 - x if k & 4 else x, 1 - y if k & 2 else y, 1 - c if k & 1 else c))
        cps = [pltpu.make_async_remote_copy(src_ref=s_ref, dst_ref=o_ref.at[me], send_sem=send_sems.at[k],
                                            recv_sem=recv_sems.at[k], device_id=p, device_id_type=MESH)
               for k, p in enumerate(peers)]
        for cp in cps:
            cp.start()
        for k, (px, py, pc) in enumerate(peers):
            pltpu.make_async_remote_copy(src_ref=s_ref, dst_ref=o_ref.at[4 * px + 2 * py + pc],
                                         send_sem=send_sems.at[k], recv_sem=recv_sems.at[k],
                                         device_id=(px, py, pc), device_id_type=MESH).wait_recv()
        for cp in cps:
            cp.wait_send()
        mine.wait()

    return pl.pallas_call(
        body, name=name, in_specs=[ANY], out_specs=ANY,
        out_shape=jax.ShapeDtypeStruct((8, rows, 128), F32),
        scratch_shapes=[pltpu.SemaphoreType.DMA((7,)), pltpu.SemaphoreType.DMA((7,)), pltpu.SemaphoreType.DMA],
    )(s)


SHARD_ROWS = (IN_WIDTH // N_CHIPS, 512, 1024, 1024)


def _pack_shard(w_in_s, w_out_s, w_up_s, w_down_s, dtype):
    parts = [w_in_s.reshape(-1, 1024), w_out_s.reshape(-1, 1024), w_up_s.reshape(-1, 1024),
             w_down_s.reshape(-1, 1024)]
    used = sum(SHARD_ROWS)
    parts.append(jnp.zeros((BLOB_ROWS - used, 1024), parts[0].dtype))
    return jnp.concatenate(parts, axis=0).astype(dtype)


def _unpack_shard(blob):
    r0 = 0
    out = []
    for rows, shape in zip(SHARD_ROWS, ((D_MODEL, IN_WIDTH // N_CHIPS), (512, D_MODEL), (D_MODEL, D_FF // N_CHIPS),
                                        (D_FF // N_CHIPS, D_MODEL))):
        out.append(blob[r0:r0 + rows].reshape(shape))
        r0 += rows
    return out


def _full_weights(gathered):
    per = [_unpack_shard(gathered[j]) for j in range(N_CHIPS)]
    w_in = jnp.concatenate([p[0] for p in per], axis=1)
    w_out = jnp.concatenate([p[1] for p in per], axis=0)
    w_up = jnp.concatenate([p[2] for p in per], axis=1)
    w_down = jnp.concatenate([p[3] for p in per], axis=0)
    return w_in, w_out, w_up, w_down


def _split_w_in(w_in):
    z_xbc = w_in[:, 0:2560]
    dt = w_in[:, 2560:2576]
    qkv = w_in[:, 2576:5648]
    f = w_in[:, 5648:5664]
    pad = jnp.zeros((w_in.shape[0], PA_WIDTH - 2592), w_in.dtype)
    return jnp.concatenate([z_xbc, dt, f, pad], axis=1), qkv


def _merge_w_in(d_a, d_qkv):
    return jnp.concatenate([d_a[:, 0:2560], d_a[:, 2560:2576], d_qkv, d_a[:, 2576:2592]], axis=1)


def _local_step(x3, target3, w_in, w_out, w_up, w_down, norm_mix_w, conv_w, conv_b, dt_bias, a_log, d_skip,
                ssd_norm_w, f_bias, norm_mlp_w, norm_final_w):
    bl, t, d = x3.shape
    n = bl * t
    x = x3.reshape(n, d)
    target = target3.reshape(n, d)
    w_a, w_qkv = _split_w_in(w_in)
    wo_s, wo_a = w_out[:SSD_WIDTH], w_out[SSD_WIDTH:]
    nfw = norm_final_w.reshape(1, d)
    dskip_e = jnp.repeat(d_skip, HEAD_DIM, axis=1)
    nb = t // ATT_BLOCK

    h0, rstd0 = _rmsnorm_fwd(x, norm_mix_w, name="norm_mix_fwd")
    r1, r2, r4, kt = min(n, 1024), min(n, 512), min(n, 256), min(n, 512)
    proj_a = _mm(h0, w_a, name="proj_a", tiles=(r2, PA_WIDTH, D_MODEL))
    qkv = _mm(h0, w_qkv, name="proj_qkv", tiles=(r2, QKV_WIDTH, D_MODEL), out_dtype=BF16)
    bias128 = jnp.concatenate([dt_bias, f_bias, jnp.zeros((1, 96), F32)], axis=1)
    alog128 = jnp.concatenate([a_log, jnp.zeros((1, 112), F32)], axis=1)
    dt, sig, acum, ccum, sigf = _prep(proj_a, bias128, alog128, bl, t)
    acum_t = acum[:, 0:16].T
    negc = jnp.pad(-ccum.reshape(bl, t, 8, 2).transpose(0, 2, 3, 1), ((0, 0), (0, 0), (0, 6), (0, 0)))
    xc = _conv_fwd(proj_a, conv_w, conv_b, bl, t)
    y_ssd, y_pre, hprev = _ssd_fwd2(xc, proj_a, dt, acum, acum_t, dskip_e, ssd_norm_w, bl, t)
    y_att, lse = _attn_fwd(qkv, negc, bl, t)
    t1 = _mm(y_ssd, wo_s, name="out_proj_ssd", tiles=(r1, D_MODEL, SSD_WIDTH), res=x)
    h1 = _mm(y_att, wo_a, name="out_proj_att", tiles=(r1, D_MODEL, ATT_WIDTH), res=t1)
    h1n, rstd1 = _rmsnorm_fwd(h1, norm_mlp_w, name="norm_mlp_fwd")
    up = _mm(h1n, w_up, name="mlp_up", tiles=(r2, D_FF, D_MODEL))
    h2 = _mm(up, w_down, name="mlp_down", tiles=(r4, D_MODEL, D_FF), a_act="relu2", res=h1)
    dh2, dh2b, loss, d_nfw = _final(h2, nfw, target)

    dup = _mm(dh2b, w_down, name="mlp_down_bwd_act", tiles=(r4, D_FF, D_MODEL), tb=True, epi_up=up, out_dtype=BF16)
    d_w_down = _mm(up, dh2b, name="mlp_down_bwd_w", tiles=(1024, 1024, kt), ta=True, a_act="relu2")
    dh1n = _mm(dup, w_up, name="mlp_up_bwd_act", tiles=(r2, D_MODEL, D_FF), tb=True)
    d_w_up = _mm(h1n, dup, name="mlp_up_bwd_w", tiles=(1024, 1024, kt), ta=True)
    dh1, dh1b, d_nmlp = _rmsnorm_bwd(dh1n, h1, rstd1, norm_mlp_w, dh2, name="norm_mlp_bwd")
    dys = _mm(dh1b, wo_s, name="out_proj_bwd_ssd", tiles=(r1, SSD_WIDTH, D_MODEL), tb=True)
    do = _mm(dh1b, wo_a, name="out_proj_bwd_att", tiles=(r1, ATT_WIDTH, D_MODEL), tb=True, out_dtype=BF16)
    d_w_out = jnp.concatenate([_mm(y_ssd, dh1b, name="out_proj_bwd_w_ssd", tiles=(1024, 1024, kt), ta=True),
                               _mm(y_att, dh1b, name="out_proj_bwd_w_att", tiles=(1024, 1024, kt), ta=True)], axis=0)
    dq, dk, dv, dcb, drow = _attn_bwd(qkv, do, y_att, lse, negc, bl, t)
    dc_keys = jnp.pad(dcb[:, :, 0:2, :].transpose(0, 3, 1, 2).reshape(n, 16), ((0, 0), (0, 112)))
    dc_queries = jnp.pad(drow.reshape(n, 16, HEAD_DIM)[:, :, 0], ((0, 0), (0, 112)))
    df_raw, d_fb = _fpost(dc_keys, dc_queries, sigf, bl, t)
    dxc, dz, ddt_raw, d_snw, d_dsk, d_alog, d_dtb = _ssd_bwd2(dys, xc, proj_a, y_pre, hprev, dt, sig, acum, acum_t,
                                                            a_log, dskip_e, ssd_norm_w, bl, t)
    dxbc, d_conv_w, d_conv_b = _conv_bwd(dxc, proj_a, conv_w, conv_b, bl, t)
    dproj_a = jnp.concatenate([dz, dxbc, ddt_raw.astype(BF16), df_raw.astype(BF16),
                               jnp.zeros((n, PA_WIDTH - 2592), BF16)], axis=1)
    dqkv = jnp.concatenate([dq, dk, dv], axis=1)
    d_w_a = _mm(h0, dproj_a, name="proj_a_bwd_w", tiles=(1024, 896, kt), ta=True)
    d_w_qkv = _mm(h0, dqkv, name="proj_qkv_bwd_w", tiles=(1024, 1024, kt), ta=True)
    t2 = _mm(dproj_a, w_a, name="proj_a_bwd_act", tiles=(r1, D_MODEL, PA_WIDTH), tb=True)
    dh0 = _mm(dqkv, w_qkv, name="proj_qkv_bwd_act", tiles=(r1, D_MODEL, QKV_WIDTH), tb=True, res=t2)
    dx, _, d_nmix = _rmsnorm_bwd(dh0, x, rstd0, norm_mix_w, dh1, name="norm_mix_bwd")

    grads = dict(norm_mix_w=d_nmix, w_in=_merge_w_in(d_w_a, d_w_qkv), conv_w=d_conv_w, conv_b=d_conv_b,
                 dt_bias=d_dtb, a_log=d_alog, d_skip=d_dsk, ssd_norm_w=d_snw, f_bias=d_fb, w_out=d_w_out,
                 norm_mlp_w=d_nmlp, w_up=d_w_up, w_down=d_w_down, norm_final_w=d_nfw)
    return dx.reshape(bl, t, d), loss, grads


SMALL_ORDER = ("norm_mix_w", "conv_w", "conv_b", "dt_bias", "a_log", "d_skip", "ssd_norm_w", "f_bias",
               "norm_mlp_w", "norm_final_w")
SMALL_SIZES = (1024, 4 * CONV_CH, CONV_CH, 16, 16, 16, 1024, 16, 1024, 1024)


def _pack_small(vals, rows):
    flat = jnp.concatenate([v.reshape(-1).astype(F32) for v in vals])
    return jnp.pad(flat, (0, rows * 128 - flat.shape[0])).reshape(rows, 128)


def _unpack_small(packed, sizes):
    flat = packed.reshape(-1)
    out, o = [], 0
    for s in sizes:
        out.append(flat[o:o + s])
        o += s
    return out


def kernel(x, norm_mix_w, w_in, conv_w, conv_b, dt_bias, a_log, d_skip, ssd_norm_w, f_bias, w_out, norm_mlp_w, w_up, w_down, norm_final_w, loss_target, m_norm_mix_w, m_w_in, m_conv_w, m_conv_b, m_dt_bias, m_a_log, m_d_skip, m_ssd_norm_w, m_f_bias, m_w_out, m_norm_mlp_w, m_w_up, m_w_down, m_norm_final_w, v_norm_mix_w, v_w_in, v_conv_w, v_conv_b, v_dt_bias, v_a_log, v_d_skip, v_ssd_norm_w, v_f_bias, v_w_out, v_norm_mlp_w, v_w_up, v_w_down, v_norm_final_w):
    chip = 2 * lax.axis_index("x") + lax.axis_index("y")
    cw = CONV_CH // N_CHIPS

    blob = _pack_shard(w_in[0], w_out[0], w_up[0], w_down[0], BF16)
    gathered = _gather_weights(blob)
    w_in_f, w_out_f, w_up_f, w_down_f = _full_weights(gathered)
    small_all = _gather_small(_pack_small([conv_w[0]], 16), name="gather_conv_w")
    conv_w_f = jnp.concatenate([small_all[2 * j].reshape(-1)[:4 * cw].reshape(4, cw) for j in range(N_CHIPS)], axis=1)

    dx, loss_part, g = _local_step(x, loss_target, w_in_f, w_out_f, w_up_f, w_down_f, norm_mix_w, conv_w_f,
                                   conv_b, dt_bias, a_log, d_skip, ssd_norm_w, f_bias, norm_mlp_w, norm_final_w)

    gblob = jnp.stack([_pack_shard(g["w_in"][:, j * (IN_WIDTH // N_CHIPS):(j + 1) * (IN_WIDTH // N_CHIPS)],
                                   g["w_out"][j * 512:(j + 1) * 512],
                                   g["w_up"][:, j * 1024:(j + 1) * 1024],
                                   g["w_down"][j * 1024:(j + 1) * 1024], BF16) for j in range(N_CHIPS)])
    c = lax.axis_index("c")
    from_sibling = _swap_halves(gblob)
    my_half = lax.dynamic_slice_in_dim(gblob, c * HALF_ROWS, HALF_ROWS, axis=1)
    chip_part = _add_pair(my_half, from_sibling, name="grad_add_sibling")
    parts = _exchange_chips(chip_part)
    g_half = _sum_leading(parts, name="grad_sum_chips")
    g_shard = _join_halves(g_half)
    g_w_in, g_w_out, g_w_up, g_w_down = _unpack_shard(g_shard)

    small_vals = [g[k] for k in SMALL_ORDER] + [loss_part[:, 0:1]]
    small_sum = _sum_leading(_gather_small(_pack_small(small_vals, SMALL_ROWS), name="gather_small_grads"), name="small_sum")
    sg = dict(zip(SMALL_ORDER + ("loss",), _unpack_small(small_sum, SMALL_SIZES + (1,))))
    loss = sg["loss"].reshape(())
    g_conv_full = sg["conv_w"].reshape(4, CONV_CH)
    g_conv = lax.dynamic_slice_in_dim(g_conv_full, chip * cw, cw, axis=1)

    grads = dict(norm_mix_w=sg["norm_mix_w"].reshape(1, -1), w_in=g_w_in[None], conv_w=g_conv[None],
                 conv_b=sg["conv_b"].reshape(1, -1), dt_bias=sg["dt_bias"].reshape(1, -1),
                 a_log=sg["a_log"].reshape(1, -1), d_skip=sg["d_skip"].reshape(1, -1),
                 ssd_norm_w=sg["ssd_norm_w"].reshape(1, -1), f_bias=sg["f_bias"].reshape(1, -1), w_out=g_w_out[None],
                 norm_mlp_w=sg["norm_mlp_w"].reshape(1, -1), w_up=g_w_up[None], w_down=g_w_down[None],
                 norm_final_w=sg["norm_final_w"])
    weights = dict(norm_mix_w=norm_mix_w, w_in=w_in, conv_w=conv_w, conv_b=conv_b, dt_bias=dt_bias, a_log=a_log,
                   d_skip=d_skip, ssd_norm_w=ssd_norm_w, f_bias=f_bias, w_out=w_out, norm_mlp_w=norm_mlp_w,
                   w_up=w_up, w_down=w_down, norm_final_w=norm_final_w)
    ms = dict(norm_mix_w=m_norm_mix_w, w_in=m_w_in, conv_w=m_conv_w, conv_b=m_conv_b, dt_bias=m_dt_bias,
              a_log=m_a_log, d_skip=m_d_skip, ssd_norm_w=m_ssd_norm_w, f_bias=m_f_bias, w_out=m_w_out,
              norm_mlp_w=m_norm_mlp_w, w_up=m_w_up, w_down=m_w_down, norm_final_w=m_norm_final_w)
    vs = dict(norm_mix_w=v_norm_mix_w, w_in=v_w_in, conv_w=v_conv_w, conv_b=v_conv_b, dt_bias=v_dt_bias,
              a_log=v_a_log, d_skip=v_d_skip, ssd_norm_w=v_ssd_norm_w, f_bias=v_f_bias, w_out=v_w_out,
              norm_mlp_w=v_norm_mlp_w, w_up=v_w_up, w_down=v_w_down, norm_final_w=v_norm_final_w)
    names = list(weights)
    big = ("w_in", "w_out", "w_up", "w_down")
    delta, new_m, new_v = {}, {}, {}
    for k in big:
        shp = weights[k].shape
        two_d = lambda a: a.reshape(shp[-2], shp[-1])
        d_, m_, v_ = _adamw(two_d(weights[k]), two_d(grads[k]), two_d(ms[k]), two_d(vs[k]), name="adamw_" + k)
        delta[k], new_m[k], new_v[k] = d_.reshape(shp), m_.reshape(shp), v_.reshape(shp)
    smalls = [k for k in names if k not in big]
    sizes = [math.prod(weights[k].shape) for k in smalls]
    rows = -(-sum(sizes) // 1024) * 8
    packs = [_pack_small([d[k] for k in smalls], rows) for d in (weights, grads, ms, vs)]
    outs = _adamw(*packs, name="adamw_small")
    for o, dst in zip(outs, (delta, new_m, new_v)):
        for k, val in zip(smalls, _unpack_small(o, sizes)):
            dst[k] = val.reshape(weights[k].shape)
    return (loss, dx, *[grads[k] for k in names], *[delta[k] for k in names], *[new_m[k] for k in names],
            *[new_v[k] for k in names])
```

```python
import functools
import math

import jax
import jax.numpy as jnp
from jax import lax
from jax.experimental import pallas as pl
from jax.experimental.pallas import tpu as pltpu

F32 = jnp.float32
BF16 = jnp.bfloat16
HIGHEST = lax.Precision.HIGHEST
MESH = pl.DeviceIdType.MESH

D_MODEL = 1024
SSD_HEADS = 16
HEAD_DIM = 64
SSD_WIDTH = 1024
SSD_STATE = 128
CONV_CH = 1536
CHUNK = 128
ATT_WIDTH = 1024
EPS = 1e-5
IN_WIDTH = 5664
PA_WIDTH = 2688
QKV_WIDTH = 3072
D_FF = 4096
ATT_BLOCK = 256
NEG = -1e30
VMEM_LIMIT = 48 * 1024 * 1024

ADAM_LR = 0.001
ADAM_B1 = 0.9
ADAM_B2 = 0.999
ADAM_EPS = 1e-08
ADAM_WD = 0.01
ADAM_STEP = 10

N_CHIPS = 4
BLOB_ROWS = 4096
HALF_ROWS = BLOB_ROWS // 2
SMALL_ROWS = 96


def _cparams(sem):
    return pltpu.CompilerParams(dimension_semantics=sem, vmem_limit_bytes=VMEM_LIMIT)


def _pick(n, cands):
    for c in cands:
        if n % c == 0:
            return c
    return n


MM_CHUNK = 512


def _mm(a, b, *, name, tiles, ta=False, tb=False, out_dtype=F32, res=None, a_act=None, epi_up=None):
    if ta:
        K, M = a.shape
    else:
        M, K = a.shape
    if tb:
        N, K2 = b.shape
    else:
        K2, N = b.shape
    assert K == K2, (a.shape, b.shape)
    tm, tn, tk = tiles
    assert M % tm == 0 and N % tn == 0 and K % tk == 0, (name, M, N, K, tiles)
    nk = K // tk
    dn = (((0 if ta else 1,), (1 if tb else 0,)), ((), ()))
    has_res = res is not None
    has_up = epi_up is not None
    cn = _pick(tn, (MM_CHUNK, 384, 256, 128))

    def prologue(av):
        if a_act == "relu2":
            r = jnp.maximum(av.astype(F32), 0.0)
            av = r * r
        return av.astype(BF16)

    def epilogue(out, res_v, up_v):
        if has_res:
            out = out + res_v.astype(F32)
        if has_up:
            out = out * (2.0 * jnp.maximum(up_v.astype(F32), 0.0))
        return out.astype(out_dtype)

    def body(*refs):
        a_ref, b_ref = refs[0], refs[1]
        i = 2
        res_ref = up_ref = None
        if has_res:
            res_ref = refs[i]
            i += 1
        if has_up:
            up_ref = refs[i]
            i += 1
        o_ref = refs[i]
        if nk == 1:
            av = prologue(a_ref[...])
            for c in range(tn // cn):
                cs = slice(c * cn, (c + 1) * cn)
                bv = (b_ref[cs, :] if tb else b_ref[:, cs]).astype(BF16)
                out = lax.dot_general(av, bv, dn, preferred_element_type=F32)
                o_ref[:, cs] = epilogue(out, res_ref[:, cs] if has_res else None, up_ref[:, cs] if has_up else None)
            return
        acc_ref = refs[i + 1]
        k = pl.program_id(2)

        @pl.when(k == 0)
        def _():
            acc_ref[...] = jnp.zeros_like(acc_ref)

        acc_ref[...] += lax.dot_general(prologue(a_ref[...]), b_ref[...].astype(BF16), dn,
                                        preferred_element_type=F32)

        @pl.when(k == nk - 1)
        def _():
            o_ref[...] = epilogue(acc_ref[...], res_ref[...] if has_res else None, up_ref[...] if has_up else None)

    a_spec = pl.BlockSpec((tk, tm), lambda i, j, k: (k, i)) if ta else pl.BlockSpec((tm, tk), lambda i, j, k: (i, k))
    b_spec = pl.BlockSpec((tn, tk), lambda i, j, k: (j, k)) if tb else pl.BlockSpec((tk, tn), lambda i, j, k: (k, j))
    o_spec = pl.BlockSpec((tm, tn), lambda i, j, k: (i, j))
    ins, specs = [a, b], [a_spec, b_spec]
    if has_res:
        ins.append(res)
        specs.append(o_spec)
    if has_up:
        ins.append(epi_up)
        specs.append(o_spec)
    return pl.pallas_call(
        body, name=name, grid=(M // tm, N // tn, nk),
        in_specs=specs, out_specs=o_spec,
        out_shape=jax.ShapeDtypeStruct((M, N), out_dtype),
        scratch_shapes=[] if nk == 1 else [pltpu.VMEM((tm, tn), F32)],
        compiler_params=_cparams(("parallel", "parallel", "arbitrary")),
    )(*ins)


def _rmsnorm_fwd(x, w, *, name):
    n, d = x.shape
    tm = _pick(n, (512, 256, 128))

    def body(x_ref, w_ref, y_ref, r_ref):
        xv = x_ref[...]
        rstd = lax.rsqrt(jnp.mean(xv * xv, axis=1, keepdims=True) + EPS)
        y_ref[...] = (xv * rstd * w_ref[...]).astype(BF16)
        r_ref[...] = rstd

    return pl.pallas_call(
        body, name=name, grid=(n // tm,),
        in_specs=[pl.BlockSpec((tm, d), lambda i: (i, 0)), pl.BlockSpec((1, d), lambda i: (0, 0))],
        out_specs=[pl.BlockSpec((tm, d), lambda i: (i, 0)), pl.BlockSpec((tm, 1), lambda i: (i, 0))],
        out_shape=[jax.ShapeDtypeStruct((n, d), BF16), jax.ShapeDtypeStruct((n, 1), F32)],
        compiler_params=_cparams(("parallel",)),
    )(x, w)


def _rmsnorm_bwd(dyn, x, rstd, w, dres, *, name):
    n, d = x.shape
    tm = _pick(n, (512, 256, 128))

    def body(g_ref, x_ref, r_ref, w_ref, d_ref, dx_ref, dxb_ref, dw_ref):
        @pl.when(pl.program_id(0) == 0)
        def _():
            dw_ref[...] = jnp.zeros_like(dw_ref)

        g = g_ref[...]
        r = r_ref[...]
        xhat = x_ref[...] * r
        gw = g * w_ref[...]
        dx = d_ref[...] + r * (gw - xhat * jnp.mean(gw * xhat, axis=1, keepdims=True))
        dx_ref[...] = dx
        dxb_ref[...] = dx.astype(BF16)
        dw_ref[...] += jnp.sum(g * xhat, axis=0, keepdims=True)

    row = pl.BlockSpec((tm, d), lambda i: (i, 0))
    vec = pl.BlockSpec((1, d), lambda i: (0, 0))
    return pl.pallas_call(
        body, name=name, grid=(n // tm,),
        in_specs=[row, row, pl.BlockSpec((tm, 1), lambda i: (i, 0)), vec, row],
        out_specs=[row, row, vec],
        out_shape=[jax.ShapeDtypeStruct((n, d), F32), jax.ShapeDtypeStruct((n, d), BF16),
                   jax.ShapeDtypeStruct((1, d), F32)],
        compiler_params=_cparams(("arbitrary",)),
    )(dyn, x, rstd, w, dres)


def _final(h2, w, target):
    n, d = h2.shape
    tm = _pick(n, (512, 256, 128))

    def body(h_ref, w_ref, t_ref, dh_ref, dhb_ref, loss_ref, dw_ref):
        @pl.when(pl.program_id(0) == 0)
        def _():
            loss_ref[...] = jnp.zeros_like(loss_ref)
            dw_ref[...] = jnp.zeros_like(dw_ref)

        hv = h_ref[...]
        wv = w_ref[...]
        rstd = lax.rsqrt(jnp.mean(hv * hv, axis=1, keepdims=True) + EPS)
        xhat = hv * rstd
        err = xhat * wv - t_ref[...]
        part = jnp.sum(jnp.mean(err * err, axis=1, keepdims=True), axis=0, keepdims=True)
        loss_ref[...] += 0.5 * part
        dy = err * (1.0 / d)
        gw = dy * wv
        dh = rstd * (gw - xhat * jnp.mean(gw * xhat, axis=1, keepdims=True))
        dh_ref[...] = dh
        dhb_ref[...] = dh.astype(BF16)
        dw_ref[...] += jnp.sum(dy * xhat, axis=0, keepdims=True)

    row = pl.BlockSpec((tm, d), lambda i: (i, 0))
    vec = pl.BlockSpec((1, d), lambda i: (0, 0))
    return pl.pallas_call(
        body, name="final_norm_loss", grid=(n // tm,),
        in_specs=[row, vec, row],
        out_specs=[row, row, pl.BlockSpec((1, 128), lambda i: (0, 0)), vec],
        out_shape=[jax.ShapeDtypeStruct((n, d), F32), jax.ShapeDtypeStruct((n, d), BF16),
                   jax.ShapeDtypeStruct((1, 128), F32), jax.ShapeDtypeStruct((1, d), F32)],
        compiler_params=_cparams(("arbitrary",)),
    )(h2, w, target)


def _softplus(x):
    return jnp.maximum(x, 0.0) + jnp.log(1.0 + jnp.exp(-jnp.abs(x)))


def _prep(proj_a, bias128, alog128, bl, t):
    n = bl * t
    nch = t // CHUNK
    col0 = (SSD_WIDTH + CONV_CH) // 128

    def body(p_ref, b_ref, al_ref, dt_ref, sg_ref, ac_ref, c_ref, sf_ref, carry):
        @pl.when(pl.program_id(1) == 0)
        def _():
            carry[...] = jnp.zeros_like(carry)

        xv = p_ref[...] + b_ref[...]
        sp = _softplus(xv)
        a = -jnp.exp(al_ref[...]) * sp
        logf = -_softplus(-xv)
        row = lax.broadcasted_iota(jnp.int32, (CHUNK, CHUNK), 0)
        col = lax.broadcasted_iota(jnp.int32, (CHUNK, CHUNK), 1)
        tril = (row >= col).astype(F32)
        acum = jnp.dot(tril, a, precision=HIGHEST, preferred_element_type=F32)
        c = jnp.dot(tril, logf, precision=HIGHEST, preferred_element_type=F32) + carry[...]
        carry[...] = c[CHUNK - 1:CHUNK, :]
        head_lanes = lax.broadcasted_iota(jnp.int32, (1, 128), 1) < 16
        dt_ref[...] = jnp.where(head_lanes, sp, 0.0)
        sg_ref[...] = jax.nn.sigmoid(xv)[:, 0:16]
        ac_ref[...] = jnp.where(head_lanes, acum, 0.0)
        c_ref[...] = c[:, 16:32]
        sf_ref[...] = jax.nn.sigmoid(-xv)[:, 16:32]

    o16 = pl.BlockSpec((CHUNK, 16), lambda b, c: (b * nch + c, 0))
    o128 = pl.BlockSpec((CHUNK, 128), lambda b, c: (b * nch + c, 0))
    v128 = pl.BlockSpec((1, 128), lambda b, c: (0, 0))
    w16 = jax.ShapeDtypeStruct((n, 16), F32)
    w128 = jax.ShapeDtypeStruct((n, 128), F32)
    return pl.pallas_call(
        body, name="head_scalars", grid=(bl, nch),
        in_specs=[pl.BlockSpec((CHUNK, 128), lambda b, c: (b * nch + c, col0)), v128, v128],
        out_specs=[o128, o16, o128, o16, o16],
        out_shape=[w128, w16, w128, w16, w16],
        scratch_shapes=[pltpu.VMEM((1, 128), F32)],
        compiler_params=_cparams(("parallel", "arbitrary")),
    )(proj_a, bias128, alog128)


def _fpost(dc, sigf, bl, t):
    n = bl * t
    nch = t // CHUNK

    def body(dc_ref, sf_ref, df_ref, db_ref, carry):
        @pl.when(pl.program_id(1) == 0)
        def _():
            carry[...] = jnp.zeros_like(carry)

        @pl.when((pl.program_id(0) == 0) & (pl.program_id(1) == 0))
        def _():
            db_ref[...] = jnp.zeros_like(db_ref)

        row = lax.broadcasted_iota(jnp.int32, (CHUNK, CHUNK), 0)
        col = lax.broadcasted_iota(jnp.int32, (CHUNK, CHUNK), 1)
        triu = (row <= col).astype(F32)
        dlf = jnp.dot(triu, dc_ref[...], precision=HIGHEST, preferred_element_type=F32) + carry[...]
        carry[...] = dlf[0:1, :]
        df = dlf[:, 0:16] * sf_ref[...]
        df_ref[...] = df
        db_ref[...] += jnp.sum(df, axis=0, keepdims=True)

    rev = lambda b, c: (b * nch + nch - 1 - c, 0)
    blk = pl.BlockSpec((CHUNK, 16), rev)
    return pl.pallas_call(
        body, name="forget_gate_bwd", grid=(bl, nch),
        in_specs=[pl.BlockSpec((CHUNK, 128), rev), blk],
        out_specs=[blk, pl.BlockSpec((1, 16), lambda b, c: (0, 0))],
        out_shape=[jax.ShapeDtypeStruct((n, 16), F32), jax.ShapeDtypeStruct((1, 16), F32)],
        scratch_shapes=[pltpu.VMEM((1, 128), F32)],
        compiler_params=_cparams(("arbitrary", "arbitrary")),
    )(dc, sigf)


CONV_TILE = 256
CONV_ROWS = 256


def _conv_taps(u_ref, i, w, bias):
    r0 = pl.multiple_of(i * CONV_ROWS, CONV_ROWS)
    cur = u_ref[pl.ds(r0, CONV_ROWS), :]
    p0 = pl.multiple_of(jnp.maximum(r0 - 8, 0), 8)
    prev = jnp.where(i > 0, u_ref[pl.ds(p0, 8), :], 0.0)
    cat = jnp.concatenate([prev, cur], axis=0)
    pre = bias + w[3:4, :] * cur
    taps = [cur]
    for s in (1, 2, 3):
        sh = pltpu.roll(cat, s, 0)[8:, :]
        taps.append(sh)
        pre = pre + w[3 - s:4 - s, :] * sh
    return r0, pre, taps


def _conv_fwd(proj_a, conv_w, conv_b, bl, t):
    n = bl * t
    nct = CONV_CH // CONV_TILE
    c0 = SSD_WIDTH // CONV_TILE

    def body(u_ref, w_ref, b_ref, o_ref):
        w = w_ref[...]
        bias = b_ref[...]

        def chunk(i, carry):
            r0, pre, _ = _conv_taps(u_ref, i, w, bias)
            o_ref[pl.ds(r0, CONV_ROWS), :] = pre * jax.nn.sigmoid(pre)
            return carry

        lax.fori_loop(0, t // CONV_ROWS, chunk, 0)

    return pl.pallas_call(
        body, name="conv_silu_fwd", grid=(bl, nct),
        in_specs=[pl.BlockSpec((t, CONV_TILE), lambda b, c: (b, c0 + c)),
                  pl.BlockSpec((4, CONV_TILE), lambda b, c: (0, c)),
                  pl.BlockSpec((1, CONV_TILE), lambda b, c: (0, c))],
        out_specs=pl.BlockSpec((t, CONV_TILE), lambda b, c: (b, c)),
        out_shape=jax.ShapeDtypeStruct((n, CONV_CH), F32),
        compiler_params=_cparams(("parallel", "parallel")),
    )(proj_a, conv_w, conv_b)


def _conv_bwd(dxc, proj_a, conv_w, conv_b, bl, t):
    n = bl * t
    nct = CONV_CH // CONV_TILE
    c0 = SSD_WIDTH // CONV_TILE
    nrc = t // CONV_ROWS

    def body(g_ref, u_ref, w_ref, b_ref, du_ref, dw_ref, db_ref, dp_scr):
        @pl.when(pl.program_id(1) == 0)
        def _():
            dw_ref[...] = jnp.zeros_like(dw_ref)
            db_ref[...] = jnp.zeros_like(db_ref)

        w = w_ref[...]
        bias = b_ref[...]
        dp_scr[pl.ds(t, 8), :] = jnp.zeros((8, CONV_TILE), F32)

        def chunk1(i, carry):
            dw0, dw1, dw2, dw3, db = carry
            r0, pre, taps = _conv_taps(u_ref, i, w, bias)
            sg = jax.nn.sigmoid(pre)
            dpre = g_ref[pl.ds(r0, CONV_ROWS), :] * (sg * (1.0 + pre * (1.0 - sg)))
            dp_scr[pl.ds(r0, CONV_ROWS), :] = dpre
            dw3 = dw3 + jnp.sum(dpre * taps[0], axis=0, keepdims=True)
            dw2 = dw2 + jnp.sum(dpre * taps[1], axis=0, keepdims=True)
            dw1 = dw1 + jnp.sum(dpre * taps[2], axis=0, keepdims=True)
            dw0 = dw0 + jnp.sum(dpre * taps[3], axis=0, keepdims=True)
            db = db + jnp.sum(dpre, axis=0, keepdims=True)
            return dw0, dw1, dw2, dw3, db

        z = jnp.zeros((1, CONV_TILE), F32)
        dw0, dw1, dw2, dw3, db = lax.fori_loop(0, nrc, chunk1, (z, z, z, z, z))
        dw_ref[...] += jnp.concatenate([dw0, dw1, dw2, dw3], axis=0)
        db_ref[...] += db

        def chunk2(i, carry):
            r0 = pl.multiple_of(i * CONV_ROWS, CONV_ROWS)
            cat = dp_scr[pl.ds(r0, CONV_ROWS + 8), :]
            du = w[3:4, :] * cat[:CONV_ROWS, :]
            for s in (1, 2, 3):
                du = du + w[3 - s:4 - s, :] * pltpu.roll(cat, CONV_ROWS + 8 - s, 0)[:CONV_ROWS, :]
            du_ref[pl.ds(r0, CONV_ROWS), :] = du.astype(BF16)
            return carry

        lax.fori_loop(0, nrc, chunk2, 0)

    return pl.pallas_call(
        body, name="conv_silu_bwd", grid=(nct, bl),
        in_specs=[pl.BlockSpec((t, CONV_TILE), lambda c, b: (b, c)),
                  pl.BlockSpec((t, CONV_TILE), lambda c, b: (b, c0 + c)),
                  pl.BlockSpec((4, CONV_TILE), lambda c, b: (0, c)),
                  pl.BlockSpec((1, CONV_TILE), lambda c, b: (0, c))],
        out_specs=[pl.BlockSpec((t, CONV_TILE), lambda c, b: (b, c)),
                   pl.BlockSpec((4, CONV_TILE), lambda c, b: (0, c)),
                   pl.BlockSpec((1, CONV_TILE), lambda c, b: (0, c))],
        out_shape=[jax.ShapeDtypeStruct((n, CONV_CH), BF16), jax.ShapeDtypeStruct((4, CONV_CH), F32),
                   jax.ShapeDtypeStruct((1, CONV_CH), F32)],
        scratch_shapes=[pltpu.VMEM((t + 8, CONV_TILE), F32)],
        compiler_params=_cparams(("parallel", "arbitrary")),
    )(dxc, proj_a, conv_w, conv_b)


NT_DIMS = (((1,), (1,)), ((), ()))
TN_DIMS = (((0,), (0,)), ((), ()))


def _dot(a, b, dims=None):
    if dims is None:
        return jnp.dot(a, b, preferred_element_type=F32)
    return lax.dot_general(a, b, dims, preferred_element_type=F32)


def _ssd_fwd(xc, proj_a, dt, acum, acum_t, dskip_e, norm_w, bl, t):
    n = bl * t
    nch = t // CHUNK
    L = CHUNK

    def body(xc_ref, z_ref, dt_ref, ac_ref, act_ref, dsk_ref, nw_ref, ys_ref, yp_ref, hp_ref, h_scr, y_scr):
        @pl.when(pl.program_id(1) == 0)
        def _():
            h_scr[...] = jnp.zeros_like(h_scr)

        row = lax.broadcasted_iota(jnp.int32, (L, L), 0)
        col = lax.broadcasted_iota(jnp.int32, (L, L), 1)
        causal = row >= col
        dt_all = dt_ref[...]
        ac_all = ac_ref[...]
        act_all = act_ref[...]
        for g in range(2):
            bg = xc_ref[:, SSD_WIDTH + g * 128:SSD_WIDTH + (g + 1) * 128].astype(BF16)
            cg = xc_ref[:, SSD_WIDTH + 256 + g * 128:SSD_WIDTH + 256 + (g + 1) * 128].astype(BF16)
            gmat = _dot(cg, bg, NT_DIMS)
            for r in range(8):
                h = g * 8 + r
                sl = slice(h * HEAD_DIM, (h + 1) * HEAD_DIM)
                xs = xc_ref[:, sl]
                xdt = xs * dt_all[:, h:h + 1]
                ac = ac_all[:, h:h + 1]
                ar = act_all[h:h + 1, :]
                ldec = jnp.exp(jnp.where(causal, ac - ar, NEG))
                m = (gmat * ldec).astype(BF16)
                hp = h_scr[h]
                hp_ref[h] = hp
                yd = _dot(m, xdt.astype(BF16))
                yo = _dot(cg, hp.astype(BF16), NT_DIMS) * jnp.exp(ac)
                y_scr[:, sl] = yd + yo + dsk_ref[:, sl] * xs
                alast = ac_all[L - 1:L, h:h + 1]
                xd = (xdt * jnp.exp(alast - ac)).astype(BF16)
                h_scr[h] = jnp.exp(alast) * hp + _dot(xd, bg, TN_DIMS)
        y = y_scr[...]
        yp_ref[...] = y
        zv = z_ref[...]
        yg = y * (zv * jax.nn.sigmoid(zv))
        for g in range(2):
            gs = slice(g * 512, (g + 1) * 512)
            grp = yg[:, gs]
            rstd = lax.rsqrt(jnp.mean(grp * grp, axis=1, keepdims=True) + EPS)
            ys_ref[:, gs] = (grp * rstd * nw_ref[:, gs]).astype(BF16)

    rb = lambda b, c: (b * nch + c, 0)
    v1k = pl.BlockSpec((1, SSD_WIDTH), lambda b, c: (0, 0))
    return pl.pallas_call(
        body, name="ssd_fwd", grid=(bl, nch),
        in_specs=[pl.BlockSpec((L, CONV_CH), rb), pl.BlockSpec((L, SSD_WIDTH), rb),
                  pl.BlockSpec((L, 16), rb), pl.BlockSpec((L, 16), rb),
                  pl.BlockSpec((16, L), lambda b, c: (0, b * nch + c)), v1k, v1k],
        out_specs=[pl.BlockSpec((L, SSD_WIDTH), rb), pl.BlockSpec((L, SSD_WIDTH), rb),
                   pl.BlockSpec((None, 16, HEAD_DIM, SSD_STATE), lambda b, c: (b * nch + c, 0, 0, 0))],
        out_shape=[jax.ShapeDtypeStruct((n, SSD_WIDTH), BF16), jax.ShapeDtypeStruct((n, SSD_WIDTH), F32),
                   jax.ShapeDtypeStruct((bl * nch, 16, HEAD_DIM, SSD_STATE), F32)],
        scratch_shapes=[pltpu.VMEM((16, HEAD_DIM, SSD_STATE), F32), pltpu.VMEM((L, SSD_WIDTH), F32)],
        compiler_params=_cparams(("parallel", "arbitrary")),
    )(xc, proj_a, dt, acum, acum_t, dskip_e, norm_w)


def _ssd_bwd(dys, xc, proj_a, ypre, hprev, dt, sig, acum, acum_t, a_log, dskip_e, norm_w, bl, t):
    n = bl * t
    nch = t // CHUNK
    L = CHUNK

    def body(dys_ref, xc_ref, z_ref, yp_ref, hp_ref, dt_ref, sg_ref, ac_ref, act_ref, al_ref, dsk_ref, nw_ref,
             dxc_ref, dz_ref, ddt_ref, dnw_ref, dsk16_ref, da16_ref, db16_ref, dh_scr, dy_scr):
        first = (pl.program_id(0) == 0) & (pl.program_id(1) == 0)

        @pl.when(first)
        def _():
            dnw_ref[...] = jnp.zeros_like(dnw_ref)
            dsk16_ref[...] = jnp.zeros_like(dsk16_ref)
            da16_ref[...] = jnp.zeros_like(da16_ref)
            db16_ref[...] = jnp.zeros_like(db16_ref)

        @pl.when(pl.program_id(1) == 0)
        def _():
            dh_scr[...] = jnp.zeros_like(dh_scr)

        y = yp_ref[...]
        zv = z_ref[...]
        sz = jax.nn.sigmoid(zv)
        gate = zv * sz
        yg = y * gate
        dout = dys_ref[...]
        nw = nw_ref[...]
        for g in range(2):
            gs = slice(g * 512, (g + 1) * 512)
            grp = yg[:, gs]
            rstd = lax.rsqrt(jnp.mean(grp * grp, axis=1, keepdims=True) + EPS)
            ghat = grp * rstd
            dnw_ref[:, gs] += jnp.sum(dout[:, gs] * ghat, axis=0, keepdims=True)
            gw = dout[:, gs] * nw[:, gs]
            dyg = rstd * (gw - ghat * jnp.mean(gw * ghat, axis=1, keepdims=True))
            dy_scr[:, gs] = dyg * gate[:, gs]
            dz_ref[:, gs] = (dyg * y[:, gs] * (sz[:, gs] * (1.0 + zv[:, gs] * (1.0 - sz[:, gs])))).astype(BF16)

        row = lax.broadcasted_iota(jnp.int32, (L, L), 0)
        col = lax.broadcasted_iota(jnp.int32, (L, L), 1)
        causal = row >= col
        lane16 = lax.broadcasted_iota(jnp.int32, (1, 16), 1)
        lane128 = lax.broadcasted_iota(jnp.int32, (1, L), 1)
        last_row = lax.broadcasted_iota(jnp.int32, (L, 1), 0) == (L - 1)
        dt_all = dt_ref[...]
        ac_all = ac_ref[...]
        act_all = act_ref[...]
        dac_col = jnp.zeros((L, L), F32)
        dac_row = jnp.zeros((L, L), F32)
        ddt_x = jnp.zeros((L, L), F32)
        dsk16 = jnp.zeros((1, 16), F32)
        rows16 = lax.broadcasted_iota(jnp.int32, (L, 1), 0)
        for g in range(2):
            bsl = slice(SSD_WIDTH + g * 128, SSD_WIDTH + (g + 1) * 128)
            csl = slice(SSD_WIDTH + 256 + g * 128, SSD_WIDTH + 256 + (g + 1) * 128)
            bg = xc_ref[:, bsl].astype(BF16)
            cg = xc_ref[:, csl].astype(BF16)
            gmat = _dot(cg, bg, NT_DIMS)
            dg_sum = jnp.zeros((L, L), F32)
            dc_acc = jnp.zeros((L, SSD_STATE), F32)
            db_acc = jnp.zeros((L, SSD_STATE), F32)
            for r in range(8):
                h = g * 8 + r
                sl = slice(h * HEAD_DIM, (h + 1) * HEAD_DIM)
                onehot = lane16 == h
                onehot_w = lane128 == h
                xs = xc_ref[:, sl]
                dth = dt_all[:, h:h + 1]
                xdt = xs * dth
                xb = xdt.astype(BF16)
                ac = ac_all[:, h:h + 1]
                ar = act_all[h:h + 1, :]
                alast = ac_all[L - 1:L, h:h + 1]
                ldec = jnp.exp(jnp.where(causal, ac - ar, NEG))
                mf = gmat * ldec
                e_in = jnp.exp(ac)
                dec = jnp.exp(alast - ac)
                elast = jnp.exp(alast)
                hp = hp_ref[h]
                hpb = hp.astype(BF16)
                dyh = dy_scr[:, sl]
                dyb = dyh.astype(BF16)
                dsk16 = dsk16 + jnp.where(onehot, jnp.sum(jnp.sum(dyh * xs, axis=1, keepdims=True), axis=0, keepdims=True), 0.0)
                dm = _dot(dyb, xb, NT_DIMS)
                dx = _dot(mf.astype(BF16), dyb, TN_DIMS)
                dg_sum = dg_sum + dm * ldec
                wmat = dm * mf
                dac_h = jnp.sum(wmat, axis=1, keepdims=True)
                dac_row = dac_row + jnp.where(rows16 == h, -jnp.sum(wmat, axis=0, keepdims=True), 0.0)
                ch = _dot(cg, hpb, NT_DIMS)
                dye = dyh * e_in
                dyeb = dye.astype(BF16)
                dc_acc = dc_acc + _dot(dyeb, hpb)
                dhp = _dot(dyeb, cg, TN_DIMS)
                dac_h = dac_h + jnp.sum(dye * ch, axis=1, keepdims=True)
                ds = dh_scr[h]
                dsb = ds.astype(BF16)
                dxd = _dot(bg, dsb, NT_DIMS)
                db_acc = db_acc + _dot((xdt * dec).astype(BF16), dsb)
                dx = dx + dxd * dec
                ddec = jnp.sum(dxd * xdt, axis=1, keepdims=True) * dec
                extra = (jnp.sum(ddec, axis=0, keepdims=True)
                         + elast * jnp.sum(jnp.sum(hp * ds, axis=1, keepdims=True), axis=0, keepdims=True))
                dac_h = dac_h - ddec + jnp.where(last_row, extra, 0.0)
                dh_scr[h] = elast * ds + dhp
                dac_col = dac_col + jnp.where(onehot_w, dac_h, 0.0)
                ddt_x = ddt_x + jnp.where(onehot_w, jnp.sum(dx * xs, axis=1, keepdims=True), 0.0)
                dxc_ref[:, sl] = dx * dth + dsk_ref[:, sl] * dyh
            dgb = dg_sum.astype(BF16)
            dxc_ref[:, csl] = dc_acc + _dot(dgb, bg)
            dxc_ref[:, bsl] = db_acc + _dot(dgb, cg, TN_DIMS)
        dac = dac_col + jnp.transpose(dac_row)
        triu = (row <= col).astype(F32)
        da = jnp.dot(triu, dac, precision=HIGHEST, preferred_element_type=F32)[:, 0:16]
        a_row = -jnp.exp(al_ref[...])
        ddt = (ddt_x[:, 0:16] + da * a_row) * sg_ref[...]
        ddt_ref[...] = ddt
        dsk16_ref[...] += dsk16
        da16_ref[...] += jnp.sum(da * dt_all, axis=0, keepdims=True) * a_row
        db16_ref[...] += jnp.sum(ddt, axis=0, keepdims=True)

    rb = lambda b, c: (b * nch + nch - 1 - c, 0)
    v1k = pl.BlockSpec((1, SSD_WIDTH), lambda b, c: (0, 0))
    v16 = pl.BlockSpec((1, 16), lambda b, c: (0, 0))
    wide = pl.BlockSpec((L, SSD_WIDTH), rb)
    s16 = pl.BlockSpec((L, 16), rb)
    return pl.pallas_call(
        body, name="ssd_bwd", grid=(bl, nch),
        in_specs=[wide, pl.BlockSpec((L, CONV_CH), rb), wide, wide,
                  pl.BlockSpec((None, 16, HEAD_DIM, SSD_STATE), lambda b, c: (b * nch + nch - 1 - c, 0, 0, 0)),
                  s16, s16, s16, pl.BlockSpec((16, L), lambda b, c: (0, b * nch + nch - 1 - c)), v16, v1k, v1k],
        out_specs=[pl.BlockSpec((L, CONV_CH), rb), wide, s16, v1k, v16, v16, v16],
        out_shape=[jax.ShapeDtypeStruct((n, CONV_CH), F32), jax.ShapeDtypeStruct((n, SSD_WIDTH), BF16),
                   jax.ShapeDtypeStruct((n, 16), F32), jax.ShapeDtypeStruct((1, SSD_WIDTH), F32),
                   jax.ShapeDtypeStruct((1, 16), F32), jax.ShapeDtypeStruct((1, 16), F32),
                   jax.ShapeDtypeStruct((1, 16), F32)],
        scratch_shapes=[pltpu.VMEM((16, HEAD_DIM, SSD_STATE), F32), pltpu.VMEM((L, SSD_WIDTH), F32)],
        compiler_params=_cparams(("arbitrary", "arbitrary")),
    )(dys, xc, proj_a, ypre, hprev, dt, sig, acum, acum_t, a_log, dskip_e, norm_w)


def _head_expander():
    r = lax.broadcasted_iota(jnp.int32, (128, SSD_WIDTH), 0)
    c = lax.broadcasted_iota(jnp.int32, (128, SSD_WIDTH), 1)
    return (c // HEAD_DIM == r).astype(F32)


def _spread(v128, expander):
    return jnp.dot(v128, expander, precision=HIGHEST, preferred_element_type=F32)


def _head_sums(v1024, expander):
    return lax.dot_general(v1024, expander, NT_DIMS, precision=HIGHEST, preferred_element_type=F32)


def _ssd_fwd2(xc, proj_a, dt, acum, acum_t, dskip_e, norm_w, bl, t):
    n = bl * t
    nch = t // CHUNK
    L = CHUNK

    def body(xc_ref, z_ref, dt_ref, ac_ref, act_ref, dsk_ref, nw_ref, ys_ref, yp_ref, hp_ref, h_scr, y_scr, x_scr):
        @pl.when(pl.program_id(1) == 0)
        def _():
            h_scr[...] = jnp.zeros_like(h_scr)

        row = lax.broadcasted_iota(jnp.int32, (L, L), 0)
        col = lax.broadcasted_iota(jnp.int32, (L, L), 1)
        causal = row >= col
        expander = _head_expander()
        ac_all = ac_ref[...]
        act_all = act_ref[...]
        ac_e = _spread(ac_all, expander)
        e_in = jnp.exp(ac_e)
        dec = jnp.exp(ac_e[L - 1:L, :] - ac_e)
        xs_all = xc_ref[:, 0:SSD_WIDTH]
        x_all = xs_all * _spread(dt_ref[...], expander)
        x_scr[...] = x_all.astype(BF16)
        hp_all = h_scr[...]
        hp_ref[...] = hp_all
        for g in range(2):
            gs = slice(g * 512, (g + 1) * 512)
            bg = xc_ref[:, SSD_WIDTH + g * 128:SSD_WIDTH + (g + 1) * 128].astype(BF16)
            cg = xc_ref[:, SSD_WIDTH + 256 + g * 128:SSD_WIDTH + 256 + (g + 1) * 128].astype(BF16)
            gmat = _dot(cg, bg, NT_DIMS)
            y_scr[:, gs] = (_dot(cg, hp_all[gs, :].astype(BF16), NT_DIMS) * e_in[:, gs]
                            + dsk_ref[:, gs] * xs_all[:, gs])
            s_new = _dot((x_all[:, gs] * dec[:, gs]).astype(BF16), bg, TN_DIMS)
            for r in range(8):
                h = g * 8 + r
                sl = slice(h * HEAD_DIM, (h + 1) * HEAD_DIM)
                ldec = jnp.exp(jnp.where(causal, ac_all[:, h:h + 1] - act_all[h:h + 1, :], NEG))
                y_scr[:, sl] += _dot((gmat * ldec).astype(BF16), x_scr[:, sl])
                elast = jnp.exp(ac_all[L - 1:L, h:h + 1])
                h_scr[sl, :] = elast * hp_all[sl, :] + s_new[r * HEAD_DIM:(r + 1) * HEAD_DIM, :]
        y = y_scr[...]
        yp_ref[...] = y
        zv = z_ref[...]
        yg = y * (zv * jax.nn.sigmoid(zv))
        for g in range(2):
            gs = slice(g * 512, (g + 1) * 512)
            grp = yg[:, gs]
            rstd = lax.rsqrt(jnp.mean(grp * grp, axis=1, keepdims=True) + EPS)
            ys_ref[:, gs] = (grp * rstd * nw_ref[:, gs]).astype(BF16)

    rb = lambda b, c: (b * nch + c, 0)
    v1k = pl.BlockSpec((1, SSD_WIDTH), lambda b, c: (0, 0))
    return pl.pallas_call(
        body, name="ssd_fwd", grid=(bl, nch),
        in_specs=[pl.BlockSpec((L, CONV_CH), rb), pl.BlockSpec((L, SSD_WIDTH), rb),
                  pl.BlockSpec((L, 128), rb), pl.BlockSpec((L, 128), rb),
                  pl.BlockSpec((16, L), lambda b, c: (0, b * nch + c)), v1k, v1k],
        out_specs=[pl.BlockSpec((L, SSD_WIDTH), rb), pl.BlockSpec((L, SSD_WIDTH), rb),
                   pl.BlockSpec((None, SSD_WIDTH, SSD_STATE), lambda b, c: (b * nch + c, 0, 0))],
        out_shape=[jax.ShapeDtypeStruct((n, SSD_WIDTH), BF16), jax.ShapeDtypeStruct((n, SSD_WIDTH), F32),
                   jax.ShapeDtypeStruct((bl * nch, SSD_WIDTH, SSD_STATE), F32)],
        scratch_shapes=[pltpu.VMEM((SSD_WIDTH, SSD_STATE), F32), pltpu.VMEM((L, SSD_WIDTH), F32),
                        pltpu.VMEM((L, SSD_WIDTH), BF16)],
        compiler_params=_cparams(("parallel", "arbitrary")),
    )(xc, proj_a, dt, acum, acum_t, dskip_e, norm_w)


def _ssd_bwd2(dys, xc, proj_a, ypre, hprev, dt, sig, acum, acum_t, a_log, dskip_e, norm_w, bl, t):
    n = bl * t
    nch = t // CHUNK
    L = CHUNK

    def body(dys_ref, xc_ref, z_ref, yp_ref, hp_ref, dt_ref, sg_ref, ac_ref, act_ref, al_ref, dsk_ref, nw_ref,
             dxc_ref, dz_ref, ddt_ref, dnw_ref, dsk16_ref, da16_ref, db16_ref,
             dh_scr, dy_scr, x_scr, dx_scr, red_scr):
        first = (pl.program_id(0) == 0) & (pl.program_id(1) == 0)

        @pl.when(first)
        def _():
            dnw_ref[...] = jnp.zeros_like(dnw_ref)
            dsk16_ref[...] = jnp.zeros_like(dsk16_ref)
            da16_ref[...] = jnp.zeros_like(da16_ref)
            db16_ref[...] = jnp.zeros_like(db16_ref)

        @pl.when(pl.program_id(1) == 0)
        def _():
            dh_scr[...] = jnp.zeros_like(dh_scr)

        y = yp_ref[...]
        zv = z_ref[...]
        sz = jax.nn.sigmoid(zv)
        gate = zv * sz
        yg = y * gate
        dout = dys_ref[...]
        nw = nw_ref[...]
        for g in range(2):
            gs = slice(g * 512, (g + 1) * 512)
            grp = yg[:, gs]
            rstd = lax.rsqrt(jnp.mean(grp * grp, axis=1, keepdims=True) + EPS)
            ghat = grp * rstd
            dnw_ref[:, gs] += jnp.sum(dout[:, gs] * ghat, axis=0, keepdims=True)
            gw = dout[:, gs] * nw[:, gs]
            dyg = rstd * (gw - ghat * jnp.mean(gw * ghat, axis=1, keepdims=True))
            dy_scr[:, gs] = dyg * gate[:, gs]
            dz_ref[:, gs] = (dyg * y[:, gs] * (sz[:, gs] * (1.0 + zv[:, gs] * (1.0 - sz[:, gs])))).astype(BF16)

        row = lax.broadcasted_iota(jnp.int32, (L, L), 0)
        col = lax.broadcasted_iota(jnp.int32, (L, L), 1)
        causal = row >= col
        lane128 = lax.broadcasted_iota(jnp.int32, (1, L), 1)
        rows128 = lax.broadcasted_iota(jnp.int32, (L, 1), 0)
        last_row = rows128 == (L - 1)
        expander = _head_expander()
        ac_all = ac_ref[...]
        act_all = act_ref[...]
        dt_all = dt_ref[...]
        dt_e = _spread(dt_all, expander)
        ac_e = _spread(ac_all, expander)
        e_in = jnp.exp(ac_e)
        dec = jnp.exp(ac_e[L - 1:L, :] - ac_e)
        xs_all = xc_ref[:, 0:SSD_WIDTH]
        x_all = xs_all * dt_e
        x_scr[...] = x_all.astype(BF16)
        dy_all = dy_scr[...]
        hp_all = hp_ref[...]
        ds_all = dh_scr[...]
        dsk_cols = jnp.sum(dy_all * xs_all, axis=0, keepdims=True)
        dac = jnp.zeros((L, L), F32)
        dac_row = jnp.zeros((L, L), F32)
        ddec_cols = []
        for g in range(2):
            gs = slice(g * 512, (g + 1) * 512)
            bsl = slice(SSD_WIDTH + g * 128, SSD_WIDTH + (g + 1) * 128)
            csl = slice(SSD_WIDTH + 256 + g * 128, SSD_WIDTH + 256 + (g + 1) * 128)
            bg = xc_ref[:, bsl].astype(BF16)
            cg = xc_ref[:, csl].astype(BF16)
            gmat = _dot(cg, bg, NT_DIMS)
            hpb = hp_all[gs, :].astype(BF16)
            dsb = ds_all[gs, :].astype(BF16)
            ch = _dot(cg, hpb, NT_DIMS)
            dye = dy_all[:, gs] * e_in[:, gs]
            dyeb = dye.astype(BF16)
            dc_acc = _dot(dyeb, hpb)
            dhp = _dot(dyeb, cg, TN_DIMS)
            dxd = _dot(bg, dsb, NT_DIMS)
            db_acc = _dot((x_all[:, gs] * dec[:, gs]).astype(BF16), dsb)
            ddec = dxd * x_all[:, gs] * dec[:, gs]
            ddec_cols.append(jnp.sum(ddec, axis=0, keepdims=True))
            dx_scr[:, gs] = dxd * dec[:, gs]
            red_scr[:, gs] = dye * ch - ddec
            dg_sum = jnp.zeros((L, L), F32)
            for r in range(8):
                h = g * 8 + r
                sl = slice(h * HEAD_DIM, (h + 1) * HEAD_DIM)
                onehot_w = lane128 == h
                ldec = jnp.exp(jnp.where(causal, ac_all[:, h:h + 1] - act_all[h:h + 1, :], NEG))
                mf = gmat * ldec
                dyb = dy_scr[:, sl].astype(BF16)
                dm = _dot(dyb, x_scr[:, sl], NT_DIMS)
                dx_scr[:, sl] += _dot(mf.astype(BF16), dyb, TN_DIMS)
                dg_sum = dg_sum + dm * ldec
                wmat = dm * mf
                elast = jnp.exp(ac_all[L - 1:L, h:h + 1])
                hp_h = hp_all[sl, :]
                ds_h = ds_all[sl, :]
                extra = elast * jnp.sum(jnp.sum(hp_h * ds_h, axis=1, keepdims=True), axis=0, keepdims=True)
                dac = dac + jnp.where(onehot_w, jnp.sum(wmat, axis=1, keepdims=True) + jnp.where(last_row, extra, 0.0),
                                      0.0)
                dac_row = dac_row + jnp.where(rows128 == h, -jnp.sum(wmat, axis=0, keepdims=True), 0.0)
                dh_scr[sl, :] = elast * ds_h + dhp[r * HEAD_DIM:(r + 1) * HEAD_DIM, :]
            dgb = dg_sum.astype(BF16)
            dxc_ref[:, csl] = dc_acc + _dot(dgb, bg)
            dxc_ref[:, bsl] = db_acc + _dot(dgb, cg, TN_DIMS)
        dx_all = dx_scr[...]
        dxc_ref[:, 0:SSD_WIDTH] = dx_all * dt_e + dsk_ref[...] * dy_all
        red = red_scr[...]
        dac_slab = _head_sums(red, expander)
        ddec_tot = _head_sums(jnp.broadcast_to(jnp.concatenate(ddec_cols, axis=1), (8, SSD_WIDTH)), expander)
        ddt_x = _head_sums(dx_all * xs_all, expander)
        dsk16_ref[...] += _head_sums(jnp.broadcast_to(dsk_cols, (8, SSD_WIDTH)), expander)[0:1, 0:16]
        dac = dac + dac_slab + jnp.transpose(dac_row) + jnp.where(last_row, ddec_tot[0:1, :], 0.0)
        triu = (row <= col).astype(F32)
        da = jnp.dot(triu, dac, precision=HIGHEST, preferred_element_type=F32)[:, 0:16]
        a_row = -jnp.exp(al_ref[...])
        dt16 = dt_all[:, 0:16]
        ddt = (ddt_x[:, 0:16] + da * a_row) * sg_ref[...]
        ddt_ref[...] = ddt
        da16_ref[...] += jnp.sum(da * dt16, axis=0, keepdims=True) * a_row
        db16_ref[...] += jnp.sum(ddt, axis=0, keepdims=True)

    rb = lambda b, c: (b * nch + nch - 1 - c, 0)
    v1k = pl.BlockSpec((1, SSD_WIDTH), lambda b, c: (0, 0))
    v16 = pl.BlockSpec((1, 16), lambda b, c: (0, 0))
    wide = pl.BlockSpec((L, SSD_WIDTH), rb)
    s16 = pl.BlockSpec((L, 16), rb)
    s128 = pl.BlockSpec((L, 128), rb)
    return pl.pallas_call(
        body, name="ssd_bwd", grid=(bl, nch),
        in_specs=[wide, pl.BlockSpec((L, CONV_CH), rb), wide, wide,
                  pl.BlockSpec((None, SSD_WIDTH, SSD_STATE), lambda b, c: (b * nch + nch - 1 - c, 0, 0)),
                  s128, s16, s128, pl.BlockSpec((16, L), lambda b, c: (0, b * nch + nch - 1 - c)), v16, v1k, v1k],
        out_specs=[pl.BlockSpec((L, CONV_CH), rb), wide, s16, v1k, v16, v16, v16],
        out_shape=[jax.ShapeDtypeStruct((n, CONV_CH), F32), jax.ShapeDtypeStruct((n, SSD_WIDTH), BF16),
                   jax.ShapeDtypeStruct((n, 16), F32), jax.ShapeDtypeStruct((1, SSD_WIDTH), F32),
                   jax.ShapeDtypeStruct((1, 16), F32), jax.ShapeDtypeStruct((1, 16), F32),
                   jax.ShapeDtypeStruct((1, 16), F32)],
        scratch_shapes=[pltpu.VMEM((SSD_WIDTH, SSD_STATE), F32), pltpu.VMEM((L, SSD_WIDTH), F32),
                        pltpu.VMEM((L, SSD_WIDTH), BF16), pltpu.VMEM((L, SSD_WIDTH), F32),
                        pltpu.VMEM((L, SSD_WIDTH), F32)],
        compiler_params=_cparams(("arbitrary", "arbitrary")),
    )(dys, xc, proj_a, ypre, hprev, dt, sig, acum, acum_t, a_log, dskip_e, norm_w)


def _attn_fwd(qkv, negc, bl, t):
    n = bl * t
    tb_ = ATT_BLOCK
    nb = t // tb_
    scale = 1.0 / math.sqrt(HEAD_DIM)

    def body(q_ref, k_ref, v_ref, c_ref, o_ref, lse_ref):
        row = lax.broadcasted_iota(jnp.int32, (tb_, tb_), 0)
        col = lax.broadcasted_iota(jnp.int32, (tb_, tb_), 1)
        causal = row >= col
        for qi in range(nb):
            r0, lk = qi * tb_, (qi + 1) * tb_
            for j in range(2):
                sl = slice(j * HEAD_DIM, (j + 1) * HEAD_DIM)
                s = _dot(q_ref[r0:lk, sl], k_ref[0:lk, sl], NT_DIMS) * scale + c_ref[j:j + 1, 0:lk]
                tail = jnp.where(causal, s[:, r0:lk], NEG)
                s = tail if qi == 0 else jnp.concatenate([s[:, 0:r0], tail], axis=1)
                m = jnp.max(s, axis=1, keepdims=True)
                p = jnp.exp(s - m)
                l = jnp.sum(p, axis=1, keepdims=True)
                acc = _dot(p.astype(BF16), v_ref[0:lk, sl])
                o_ref[r0:lk, sl] = (acc / l).astype(BF16)
                lse_ref[r0:lk, sl] = jnp.broadcast_to(m + jnp.log(l), (tb_, HEAD_DIM))

    blk = lambda off: pl.BlockSpec((t, 128), lambda b, hp: (b, off + hp))
    return pl.pallas_call(
        body, name="fox_attn_fwd", grid=(bl, 8),
        in_specs=[blk(0), blk(8), blk(16), pl.BlockSpec((None, None, 8, t), lambda b, hp: (b, hp, 0, 0))],
        out_specs=[blk(0), blk(0)],
        out_shape=[jax.ShapeDtypeStruct((n, ATT_WIDTH), BF16), jax.ShapeDtypeStruct((n, ATT_WIDTH), F32)],
        compiler_params=_cparams(("parallel", "parallel")),
    )(qkv, qkv, qkv, negc)


def _attn_bwd(qkv, do, o, lse, negc, bl, t):
    n = bl * t
    tb_ = ATT_BLOCK
    nb = t // tb_
    scale = 1.0 / math.sqrt(HEAD_DIM)

    def body(q_ref, k_ref, v_ref, do_ref, o_ref, lse_ref, c_ref, dq_ref, dk_ref, dv_ref, dc_ref,
             dq_scr, delta_scr, dr_scr):
        row = lax.broadcasted_iota(jnp.int32, (tb_, tb_), 0)
        col = lax.broadcasted_iota(jnp.int32, (tb_, tb_), 1)
        causal = row >= col
        dq_scr[...] = jnp.zeros_like(dq_scr)
        dr_scr[...] = jnp.zeros_like(dr_scr)
        dc_ref[...] = jnp.zeros_like(dc_ref)
        prod = do_ref[...].astype(F32) * o_ref[...].astype(F32)
        for j in range(2):
            sl = slice(j * HEAD_DIM, (j + 1) * HEAD_DIM)
            delta_scr[:, sl] = jnp.broadcast_to(jnp.sum(prod[:, sl], axis=1, keepdims=True), (t, HEAD_DIM))
        for kj in range(nb):
            r0, r1 = kj * tb_, (kj + 1) * tb_
            for j in range(2):
                sl = slice(j * HEAD_DIM, (j + 1) * HEAD_DIM)
                one = slice(j * HEAD_DIM, j * HEAD_DIM + 1)
                kb = k_ref[r0:r1, sl]
                qs = q_ref[r0:t, sl]
                dos = do_ref[r0:t, sl]
                s = _dot(qs, kb, NT_DIMS) * scale + c_ref[j:j + 1, r0:r1]
                head = jnp.where(causal, s[0:tb_, :], NEG)
                s = head if kj == nb - 1 else jnp.concatenate([head, s[tb_:, :]], axis=0)
                p = jnp.exp(s - lse_ref[r0:t, one])
                dp = _dot(dos, v_ref[r0:r1, sl], NT_DIMS)
                ds = p * (dp - delta_scr[r0:t, one])
                dsb = ds.astype(BF16)
                dv_ref[r0:r1, sl] = _dot(p.astype(BF16), dos, TN_DIMS).astype(BF16)
                dk_ref[r0:r1, sl] = (_dot(dsb, qs, TN_DIMS) * scale).astype(BF16)
                dq_scr[r0:t, sl] += _dot(dsb, kb)
                dr_scr[r0:t, sl] += jnp.broadcast_to(jnp.sum(ds, axis=1, keepdims=True), (t - r0, HEAD_DIM))
                dc_ref[j:j + 1, r0:r1] = -jnp.sum(ds, axis=0, keepdims=True)
        dq_ref[...] = (dq_scr[...] * scale).astype(BF16)
        dr_t = jnp.transpose(dr_scr[...])
        for j in range(2):
            dc_ref[j:j + 1, :] += dr_t[j * HEAD_DIM:j * HEAD_DIM + 1, :]

    blk = lambda off: pl.BlockSpec((t, 128), lambda b, hp: (b, off + hp))
    cblk = pl.BlockSpec((None, None, 8, t), lambda b, hp: (b, hp, 0, 0))
    return pl.pallas_call(
        body, name="fox_attn_bwd", grid=(bl, 8),
        in_specs=[blk(0), blk(8), blk(16), blk(0), blk(0), blk(0), cblk],
        out_specs=[blk(0), blk(0), blk(0), cblk],
        out_shape=[jax.ShapeDtypeStruct((n, ATT_WIDTH), BF16)] * 3 + [jax.ShapeDtypeStruct((bl, 8, 8, t), F32)],
        scratch_shapes=[pltpu.VMEM((t, 128), F32), pltpu.VMEM((t, 128), F32), pltpu.VMEM((t, 128), F32)],
        compiler_params=_cparams(("parallel", "parallel")),
    )(qkv, qkv, qkv, do, o, lse, negc)


def _adamw(w, g, m, v, *, name):
    r, c = w.shape
    tr = _pick(r, (256, 128, 64, 32, 16, 8))
    bc1 = 1.0 - ADAM_B1 ** ADAM_STEP
    bc2 = 1.0 - ADAM_B2 ** ADAM_STEP

    def body(w_ref, g_ref, m_ref, v_ref, d_ref, nm_ref, nv_ref):
        gv = g_ref[...]
        mn = ADAM_B1 * m_ref[...] + (1.0 - ADAM_B1) * gv
        vn = ADAM_B2 * v_ref[...] + (1.0 - ADAM_B2) * (gv * gv)
        m_hat = mn / bc1
        v_hat = vn / bc2
        d_ref[...] = -ADAM_LR * (m_hat / (jnp.sqrt(v_hat) + ADAM_EPS) + ADAM_WD * w_ref[...])
        nm_ref[...] = mn
        nv_ref[...] = vn

    blk = pl.BlockSpec((tr, c), lambda i: (i, 0))
    return pl.pallas_call(
        body, name=name, grid=(r // tr,), in_specs=[blk] * 4, out_specs=[blk] * 3,
        out_shape=[jax.ShapeDtypeStruct((r, c), F32)] * 3,
        compiler_params=_cparams(("parallel",)),
    )(w, g, m, v)


def _sum_leading(parts, *, name, out_dtype=F32):
    k, r, c = parts.shape
    tr = _pick(r, (512, 256, 128, 96, 64, 32, 16, 8))

    def body(p_ref, o_ref):
        acc = p_ref[0].astype(F32)
        for i in range(1, k):
            acc = acc + p_ref[i].astype(F32)
        o_ref[...] = acc.astype(out_dtype)

    return pl.pallas_call(
        body, name=name, grid=(r // tr,),
        in_specs=[pl.BlockSpec((k, tr, c), lambda i: (0, i, 0))],
        out_specs=pl.BlockSpec((tr, c), lambda i: (i, 0)),
        out_shape=jax.ShapeDtypeStruct((r, c), out_dtype),
        compiler_params=_cparams(("parallel",)),
    )(parts)


def _add_pair(a, b, *, name):
    k, r, c = a.shape
    tr = _pick(r, (512, 256, 128))

    def body(a_ref, b_ref, o_ref):
        o_ref[...] = (a_ref[...].astype(F32) + b_ref[...].astype(F32)).astype(BF16)

    blk = pl.BlockSpec((None, tr, c), lambda j, i: (j, i, 0))
    return pl.pallas_call(
        body, name=name, grid=(k, r // tr), in_specs=[blk, blk], out_specs=blk,
        out_shape=jax.ShapeDtypeStruct((k, r, c), BF16),
        compiler_params=_cparams(("parallel", "parallel")),
    )(a, b)


ANY = pl.BlockSpec(memory_space=pl.ANY)


def _chip_peers(x, y):
    return [(1 - x, y, 2 * (1 - x) + y), (x, 1 - y, 2 * x + 1 - y), (1 - x, 1 - y, 2 * (1 - x) + 1 - y)]


def _gather_weights(blob, *, name):
    rows, cols = blob.shape
    half_rows = rows // 2

    def body(b_ref, o_ref, send_sems, recv_sems):
        x, y, c = lax.axis_index("x"), lax.axis_index("y"), lax.axis_index("c")
        me = 2 * x + y
        sibling = (x, y, 1 - c)
        peers = _chip_peers(x, y)

        def half(chip, hc):
            return o_ref.at[chip, pl.ds(hc * half_rows, half_rows), :]

        def copy(k, src, chip, hc, to):
            return pltpu.make_async_remote_copy(src_ref=src, dst_ref=half(chip, hc), send_sem=send_sems.at[k],
                                                recv_sem=recv_sems.at[k], device_id=to, device_id_type=MESH)

        my_half = b_ref.at[pl.ds(c * half_rows, half_rows), :]
        first = [copy(k, my_half, me, c, (px, py, c)) for k, (px, py, _) in enumerate(peers)]
        for cp in first:
            cp.start()
        passed = [copy(3 + k, half(pc, c), pc, c, sibling) for k, (_, _, pc) in enumerate(peers)]
        for k, (px, py, pc) in enumerate(peers):
            copy(k, my_half, pc, c, (px, py, c)).wait_recv()
            passed[k].start()
        for k, (_, _, pc) in enumerate(peers):
            copy(3 + k, half(pc, 1 - c), pc, 1 - c, sibling).wait_recv()
        for cp in first + passed:
            cp.wait_send()

    got = pl.pallas_call(
        body, name=name, in_specs=[ANY], out_specs=ANY,
        out_shape=jax.ShapeDtypeStruct((N_CHIPS, rows, cols), BF16),
        scratch_shapes=[pltpu.SemaphoreType.DMA((6,)), pltpu.SemaphoreType.DMA((6,))],
    )(blob)
    me = 2 * lax.axis_index("x") + lax.axis_index("y")
    return lax.dynamic_update_slice(got, blob[None], (me, 0, 0))


def _swap_halves(g, *, name):
    _, rows, cols = g.shape
    half_rows = rows // 2

    def body(g_ref, o_ref, send_sem, recv_sem):
        x, y, c = lax.axis_index("x"), lax.axis_index("y"), lax.axis_index("c")
        cp = pltpu.make_async_remote_copy(
            src_ref=g_ref.at[:, pl.ds((1 - c) * half_rows, half_rows), :], dst_ref=o_ref,
            send_sem=send_sem, recv_sem=recv_sem, device_id=(x, y, 1 - c), device_id_type=MESH)
        cp.start()
        cp.wait()

    return pl.pallas_call(
        body, name=name, in_specs=[ANY], out_specs=ANY,
        out_shape=jax.ShapeDtypeStruct((N_CHIPS, half_rows, cols), BF16),
        scratch_shapes=[pltpu.SemaphoreType.DMA, pltpu.SemaphoreType.DMA],
    )(g)


def _exchange_chips(p, *, name):
    def body(p_ref, o_ref, send_sems, recv_sems):
        x, y, c = lax.axis_index("x"), lax.axis_index("y"), lax.axis_index("c")
        me = 2 * x + y
        peers = _chip_peers(x, y)
        cps = [pltpu.make_async_remote_copy(src_ref=p_ref.at[pc], dst_ref=o_ref.at[me], send_sem=send_sems.at[k],
                                            recv_sem=recv_sems.at[k], device_id=(px, py, c), device_id_type=MESH)
               for k, (px, py, pc) in enumerate(peers)]
        for cp in cps:
            cp.start()
        for k, (px, py, pc) in enumerate(peers):
            pltpu.make_async_remote_copy(src_ref=p_ref.at[pc], dst_ref=o_ref.at[pc], send_sem=send_sems.at[k],
                                         recv_sem=recv_sems.at[k], device_id=(px, py, c),
                                         device_id_type=MESH).wait_recv()
        for cp in cps:
            cp.wait_send()

    got = pl.pallas_call(
        body, name=name, in_specs=[ANY], out_specs=ANY,
        out_shape=jax.ShapeDtypeStruct(p.shape, BF16),
        scratch_shapes=[pltpu.SemaphoreType.DMA((3,)), pltpu.SemaphoreType.DMA((3,))],
    )(p)
    me = 2 * lax.axis_index("x") + lax.axis_index("y")
    return lax.dynamic_update_slice(got, lax.dynamic_slice_in_dim(p, me, 1, axis=0), (me, 0, 0))


def _join_halves(gh, *, name):
    def body(g_ref, o_ref, send_sem, recv_sem):
        x, y, c = lax.axis_index("x"), lax.axis_index("y"), lax.axis_index("c")
        cp = pltpu.make_async_remote_copy(src_ref=g_ref, dst_ref=o_ref, send_sem=send_sem, recv_sem=recv_sem,
                                          device_id=(x, y, 1 - c), device_id_type=MESH)
        cp.start()
        cp.wait()

    other = pl.pallas_call(
        body, name=name, in_specs=[ANY], out_specs=ANY,
        out_shape=jax.ShapeDtypeStruct(gh.shape, F32),
        scratch_shapes=[pltpu.SemaphoreType.DMA, pltpu.SemaphoreType.DMA],
    )(gh)
    south = lax.axis_index("c") == 0
    return jnp.concatenate([jnp.where(south, gh, other), jnp.where(south, other, gh)], axis=0)


def _gather_small(s, *, name):
    rows = s.shape[0]

    def body(s_ref, o_ref, send_sems, recv_sems, local_sem):
        x, y, c = lax.axis_index("x"), lax.axis_index("y"), lax.axis_index("c")
        me = 4 * x + 2 * y + c
        mine = pltpu.make_async_copy(s_ref, o_ref.at[me], local_sem)
        mine.start()
        peers = []
        for k in range(1, 8):
            peers.append((1 - x if k & 4 else x, 1 - y if k & 2 else y, 1 - c if k & 1 else c))
        cps = [pltpu.make_async_remote_copy(src_ref=s_ref, dst_ref=o_ref.at[me], send_sem=send_sems.at[k],
                                            recv_sem=recv_sems.at[k], device_id=p, device_id_type=MESH)
               for k, p in enumerate(peers)]
        for cp in cps:
            cp.start()
        for k, (px, py, pc) in enumerate(peers):
            pltpu.make_async_remote_copy(src_ref=s_ref, dst_ref=o_ref.at[4 * px + 2 * py + pc],
                                         send_sem=send_sems.at[k], recv_sem=recv_sems.at[k],
                                         device_id=(px, py, pc), device_id_type=MESH).wait_recv()
        for cp in cps:
            cp.wait_send()
        mine.wait()

    return pl.pallas_call(
        body, name=name, in_specs=[ANY], out_specs=ANY,
        out_shape=jax.ShapeDtypeStruct((8, rows, 128), F32),
        scratch_shapes=[pltpu.SemaphoreType.DMA((7,)), pltpu.SemaphoreType.DMA((7,)), pltpu.SemaphoreType.DMA],
    )(s)


IN_SHARD = IN_WIDTH // N_CHIPS
IN_SHARD_PAD = 1536
OUT_ROWS, UP_ROWS, DOWN_ROWS = 512, 1024, 1024


def _pack_in(w_in_s):
    return jnp.pad(w_in_s, ((0, 0), (0, IN_SHARD_PAD - IN_SHARD))).astype(BF16)


def _pack_rest(w_out_s, w_up_s, w_down_s):
    return jnp.concatenate([w_out_s, w_up_s, w_down_s], axis=0).astype(BF16)


def _unpack_rest(blob):
    return (blob[0:OUT_ROWS], blob[OUT_ROWS:OUT_ROWS + UP_ROWS], blob[OUT_ROWS + UP_ROWS:])


def _full_weights(g_in, g_rest):
    w_in = jnp.concatenate([g_in[j, :, :IN_SHARD] for j in range(N_CHIPS)], axis=1)
    w_out = g_rest[:, 0:OUT_ROWS].reshape(N_CHIPS * OUT_ROWS, D_MODEL)
    w_up = g_rest[:, OUT_ROWS:OUT_ROWS + UP_ROWS].transpose(1, 0, 2).reshape(D_MODEL, D_FF)
    w_down = g_rest[:, OUT_ROWS + UP_ROWS:].reshape(D_FF, D_MODEL)
    return w_in, w_out, w_up, w_down


def _split_w_in(w_in):
    z_xbc = w_in[:, 0:2560]
    dt = w_in[:, 2560:2576]
    qkv = w_in[:, 2576:5648]
    f = w_in[:, 5648:5664]
    pad = jnp.zeros((w_in.shape[0], PA_WIDTH - 2592), w_in.dtype)
    return jnp.concatenate([z_xbc, dt, f, pad], axis=1), qkv


def _merge_w_in(d_a, d_qkv):
    return jnp.concatenate([d_a[:, 0:2560], d_a[:, 2560:2576], d_qkv, d_a[:, 2576:2592]], axis=1)


def _local_step(x3, target3, w_in, w_out, w_up, w_down, norm_mix_w, conv_w, conv_b, dt_bias, a_log, d_skip,
                ssd_norm_w, f_bias, norm_mlp_w, norm_final_w):
    bl, t, d = x3.shape
    n = bl * t
    x = x3.reshape(n, d)
    target = target3.reshape(n, d)
    w_a, w_qkv = _split_w_in(w_in)
    wo_s, wo_a = w_out[:SSD_WIDTH], w_out[SSD_WIDTH:]
    nfw = norm_final_w.reshape(1, d)
    dskip_e = jnp.repeat(d_skip, HEAD_DIM, axis=1)
    nb = t // ATT_BLOCK

    h0, rstd0 = _rmsnorm_fwd(x, norm_mix_w, name="norm_mix_fwd")
    r1, r2, r4, kt = min(n, 1024), min(n, 512), min(n, 256), min(n, 512)
    proj_a = _mm(h0, w_a, name="proj_a", tiles=(r2, PA_WIDTH, D_MODEL))
    qkv = _mm(h0, w_qkv, name="proj_qkv", tiles=(r2, QKV_WIDTH, D_MODEL), out_dtype=BF16)
    bias128 = jnp.concatenate([dt_bias, f_bias, jnp.zeros((1, 96), F32)], axis=1)
    alog128 = jnp.concatenate([a_log, jnp.zeros((1, 112), F32)], axis=1)
    dt, sig, acum, ccum, sigf = _prep(proj_a, bias128, alog128, bl, t)
    acum_t = acum[:, 0:16].T
    negc = jnp.pad(-ccum.reshape(bl, t, 8, 2).transpose(0, 2, 3, 1), ((0, 0), (0, 0), (0, 6), (0, 0)))
    xc = _conv_fwd(proj_a, conv_w, conv_b, bl, t)
    y_ssd, y_pre, hprev = _ssd_fwd2(xc, proj_a, dt, acum, acum_t, dskip_e, ssd_norm_w, bl, t)
    y_att, lse = _attn_fwd(qkv, negc, bl, t)
    t1 = _mm(y_ssd, wo_s, name="out_proj_ssd", tiles=(r1, D_MODEL, SSD_WIDTH), res=x)
    h1 = _mm(y_att, wo_a, name="out_proj_att", tiles=(r1, D_MODEL, ATT_WIDTH), res=t1)
    h1n, rstd1 = _rmsnorm_fwd(h1, norm_mlp_w, name="norm_mlp_fwd")
    up = _mm(h1n, w_up, name="mlp_up", tiles=(r2, D_FF, D_MODEL))
    h2 = _mm(up, w_down, name="mlp_down", tiles=(r4, D_MODEL, D_FF), a_act="relu2", res=h1)
    dh2, dh2b, loss, d_nfw = _final(h2, nfw, target)

    dup = _mm(dh2b, w_down, name="mlp_down_bwd_act", tiles=(r4, D_FF, D_MODEL), tb=True, epi_up=up, out_dtype=BF16)
    d_w_down = _mm(up, dh2b, name="mlp_down_bwd_w", tiles=(1024, 1024, kt), ta=True, a_act="relu2")
    dh1n = _mm(dup, w_up, name="mlp_up_bwd_act", tiles=(r2, D_MODEL, D_FF), tb=True)
    d_w_up = _mm(h1n, dup, name="mlp_up_bwd_w", tiles=(1024, 1024, kt), ta=True)
    dh1, dh1b, d_nmlp = _rmsnorm_bwd(dh1n, h1, rstd1, norm_mlp_w, dh2, name="norm_mlp_bwd")
    dys = _mm(dh1b, wo_s, name="out_proj_bwd_ssd", tiles=(r1, SSD_WIDTH, D_MODEL), tb=True)
    do = _mm(dh1b, wo_a, name="out_proj_bwd_att", tiles=(r1, ATT_WIDTH, D_MODEL), tb=True, out_dtype=BF16)
    d_w_out = jnp.concatenate([_mm(y_ssd, dh1b, name="out_proj_bwd_w_ssd", tiles=(1024, 1024, kt), ta=True),
                               _mm(y_att, dh1b, name="out_proj_bwd_w_att", tiles=(1024, 1024, kt), ta=True)], axis=0)
    dq, dk, dv, dcb = _attn_bwd(qkv, do, y_att, lse, negc, bl, t)
    dc = jnp.pad(dcb[:, :, 0:2, :].transpose(0, 3, 1, 2).reshape(n, 16), ((0, 0), (0, 112)))
    df_raw, d_fb = _fpost(dc, sigf, bl, t)
    dxc, dz, ddt_raw, d_snw, d_dsk, d_alog, d_dtb = _ssd_bwd2(dys, xc, proj_a, y_pre, hprev, dt, sig, acum, acum_t,
                                                            a_log, dskip_e, ssd_norm_w, bl, t)
    dxbc, d_conv_w, d_conv_b = _conv_bwd(dxc, proj_a, conv_w, conv_b, bl, t)
    dproj_a = jnp.concatenate([dz, dxbc, ddt_raw.astype(BF16), df_raw.astype(BF16),
                               jnp.zeros((n, PA_WIDTH - 2592), BF16)], axis=1)
    dqkv = jnp.concatenate([dq, dk, dv], axis=1)
    d_w_a = _mm(h0, dproj_a, name="proj_a_bwd_w", tiles=(1024, 896, kt), ta=True)
    d_w_qkv = _mm(h0, dqkv, name="proj_qkv_bwd_w", tiles=(1024, 1024, kt), ta=True)
    t2 = _mm(dproj_a, w_a, name="proj_a_bwd_act", tiles=(r1, D_MODEL, PA_WIDTH), tb=True)
    dh0 = _mm(dqkv, w_qkv, name="proj_qkv_bwd_act", tiles=(r1, D_MODEL, QKV_WIDTH), tb=True, res=t2)
    dx, _, d_nmix = _rmsnorm_bwd(dh0, x, rstd0, norm_mix_w, dh1, name="norm_mix_bwd")

    grads = dict(norm_mix_w=d_nmix, w_in=_merge_w_in(d_w_a, d_w_qkv), conv_w=d_conv_w, conv_b=d_conv_b,
                 dt_bias=d_dtb, a_log=d_alog, d_skip=d_dsk, ssd_norm_w=d_snw, f_bias=d_fb, w_out=d_w_out,
                 norm_mlp_w=d_nmlp, w_up=d_w_up, w_down=d_w_down, norm_final_w=d_nfw)
    return dx.reshape(bl, t, d), loss, grads


SMALL_ORDER = ("norm_mix_w", "conv_w", "conv_b", "dt_bias", "a_log", "d_skip", "ssd_norm_w", "f_bias",
               "norm_mlp_w", "norm_final_w")
SMALL_SIZES = (1024, 4 * CONV_CH, CONV_CH, 16, 16, 16, 1024, 16, 1024, 1024)


def _pack_small(vals, rows):
    flat = jnp.concatenate([v.reshape(-1).astype(F32) for v in vals])
    return jnp.pad(flat, (0, rows * 128 - flat.shape[0])).reshape(rows, 128)


def _unpack_small(packed, sizes):
    flat = packed.reshape(-1)
    out, o = [], 0
    for s in sizes:
        out.append(flat[o:o + s])
        o += s
    return out


def kernel(x, norm_mix_w, w_in, conv_w, conv_b, dt_bias, a_log, d_skip, ssd_norm_w, f_bias, w_out, norm_mlp_w, w_up, w_down, norm_final_w, loss_target, m_norm_mix_w, m_w_in, m_conv_w, m_conv_b, m_dt_bias, m_a_log, m_d_skip, m_ssd_norm_w, m_f_bias, m_w_out, m_norm_mlp_w, m_w_up, m_w_down, m_norm_final_w, v_norm_mix_w, v_w_in, v_conv_w, v_conv_b, v_dt_bias, v_a_log, v_d_skip, v_ssd_norm_w, v_f_bias, v_w_out, v_norm_mlp_w, v_w_up, v_w_down, v_norm_final_w):
    chip = 2 * lax.axis_index("x") + lax.axis_index("y")
    cw = CONV_CH // N_CHIPS

    g_in = _gather_weights(_pack_in(w_in[0]), name="gather_w_in")
    g_rest = _gather_weights(_pack_rest(w_out[0], w_up[0], w_down[0]), name="gather_w_rest")
    w_in_f, w_out_f, w_up_f, w_down_f = _full_weights(g_in, g_rest)
    small_all = _gather_small(_pack_small([conv_w[0]], 16), name="gather_conv_w")
    conv_w_f = jnp.concatenate([small_all[2 * j].reshape(-1)[:4 * cw].reshape(4, cw) for j in range(N_CHIPS)], axis=1)

    dx, loss_part, g = _local_step(x, loss_target, w_in_f, w_out_f, w_up_f, w_down_f, norm_mix_w, conv_w_f,
                                   conv_b, dt_bias, a_log, d_skip, ssd_norm_w, f_bias, norm_mlp_w, norm_final_w)

    c = lax.axis_index("c")

    def reduce_scatter(gb, tag):
        half_rows = gb.shape[1] // 2
        from_sibling = _swap_halves(gb, name="grad_swap_halves_" + tag)
        my_half = lax.dynamic_slice_in_dim(gb, c * half_rows, half_rows, axis=1)
        chip_part = _add_pair(my_half, from_sibling, name="grad_add_sibling_" + tag)
        parts = _exchange_chips(chip_part, name="grad_exchange_chips_" + tag)
        g_half = _sum_leading(parts, name="grad_sum_chips_" + tag)
        return _join_halves(g_half, name="grad_join_halves_" + tag)

    gb_in = jnp.stack([_pack_in(g["w_in"][:, j * IN_SHARD:(j + 1) * IN_SHARD]) for j in range(N_CHIPS)])
    gb_rest = jnp.stack([_pack_rest(g["w_out"][j * OUT_ROWS:(j + 1) * OUT_ROWS],
                                    g["w_up"][:, j * UP_ROWS:(j + 1) * UP_ROWS],
                                    g["w_down"][j * DOWN_ROWS:(j + 1) * DOWN_ROWS]) for j in range(N_CHIPS)])
    g_w_out, g_w_up, g_w_down = _unpack_rest(reduce_scatter(gb_rest, "rest"))
    g_w_in = reduce_scatter(gb_in, "in")[:, :IN_SHARD]

    small_vals = [g[k] for k in SMALL_ORDER] + [loss_part[:, 0:1]]
    small_sum = _sum_leading(_gather_small(_pack_small(small_vals, SMALL_ROWS), name="gather_small_grads"), name="small_sum")
    sg = dict(zip(SMALL_ORDER + ("loss",), _unpack_small(small_sum, SMALL_SIZES + (1,))))
    loss = sg["loss"].reshape(())
    g_conv_full = sg["conv_w"].reshape(4, CONV_CH)
    g_conv = lax.dynamic_slice_in_dim(g_conv_full, chip * cw, cw, axis=1)

    grads = dict(norm_mix_w=sg["norm_mix_w"].reshape(1, -1), w_in=g_w_in[None], conv_w=g_conv[None],
                 conv_b=sg["conv_b"].reshape(1, -1), dt_bias=sg["dt_bias"].reshape(1, -1),
                 a_log=sg["a_log"].reshape(1, -1), d_skip=sg["d_skip"].reshape(1, -1),
                 ssd_norm_w=sg["ssd_norm_w"].reshape(1, -1), f_bias=sg["f_bias"].reshape(1, -1), w_out=g_w_out[None],
                 norm_mlp_w=sg["norm_mlp_w"].reshape(1, -1), w_up=g_w_up[None], w_down=g_w_down[None],
                 norm_final_w=sg["norm_final_w"])
    weights = dict(norm_mix_w=norm_mix_w, w_in=w_in, conv_w=conv_w, conv_b=conv_b, dt_bias=dt_bias, a_log=a_log,
                   d_skip=d_skip, ssd_norm_w=ssd_norm_w, f_bias=f_bias, w_out=w_out, norm_mlp_w=norm_mlp_w,
                   w_up=w_up, w_down=w_down, norm_final_w=norm_final_w)
    ms = dict(norm_mix_w=m_norm_mix_w, w_in=m_w_in, conv_w=m_conv_w, conv_b=m_conv_b, dt_bias=m_dt_bias,
              a_log=m_a_log, d_skip=m_d_skip, ssd_norm_w=m_ssd_norm_w, f_bias=m_f_bias, w_out=m_w_out,
              norm_mlp_w=m_norm_mlp_w, w_up=m_w_up, w_down=m_w_down, norm_final_w=m_norm_final_w)
    vs = dict(norm_mix_w=v_norm_mix_w, w_in=v_w_in, conv_w=v_conv_w, conv_b=v_conv_b, dt_bias=v_dt_bias,
              a_log=v_a_log, d_skip=v_d_skip, ssd_norm_w=v_ssd_norm_w, f_bias=v_f_bias, w_out=v_w_out,
              norm_mlp_w=v_norm_mlp_w, w_up=v_w_up, w_down=v_w_down, norm_final_w=v_norm_final_w)
    names = list(weights)
    big = ("w_in", "w_out", "w_up", "w_down")
    delta, new_m, new_v = {}, {}, {}
    for k in big:
        shp = weights[k].shape
        two_d = lambda a: a.reshape(shp[-2], shp[-1])
        d_, m_, v_ = _adamw(two_d(weights[k]), two_d(grads[k]), two_d(ms[k]), two_d(vs[k]), name="adamw_" + k)
        delta[k], new_m[k], new_v[k] = d_.reshape(shp), m_.reshape(shp), v_.reshape(shp)
    smalls = [k for k in names if k not in big]
    sizes = [math.prod(weights[k].shape) for k in smalls]
    rows = -(-sum(sizes) // 1024) * 8
    packs = [_pack_small([d[k] for k in smalls], rows) for d in (weights, grads, ms, vs)]
    outs = _adamw(*packs, name="adamw_small")
    for o, dst in zip(outs, (delta, new_m, new_v)):
        for k, val in zip(smalls, _unpack_small(o, sizes)):
            dst[k] = val.reshape(weights[k].shape)
    return (loss, dx, *[grads[k] for k in names], *[delta[k] for k in names], *[new_m[k] for k in names],
            *[new_v[k] for k in names])
```

```python
import functools
import math

import jax
import jax.numpy as jnp
from jax import lax
from jax.experimental import pallas as pl
from jax.experimental.pallas import tpu as pltpu

F32 = jnp.float32
BF16 = jnp.bfloat16
HIGHEST = lax.Precision.HIGHEST
MESH = pl.DeviceIdType.MESH

D_MODEL = 1024
SSD_HEADS = 16
HEAD_DIM = 64
SSD_WIDTH = 1024
SSD_STATE = 128
CONV_CH = 1536
CHUNK = 128
ATT_WIDTH = 1024
EPS = 1e-5
IN_WIDTH = 5664
PA_WIDTH = 2688
QKV_WIDTH = 3072
D_FF = 4096
ATT_BLOCK = 256
NEG = -1e30
LOG2E = 1.4426950408889634
VMEM_LIMIT = 48 * 1024 * 1024

ADAM_LR = 0.001
ADAM_B1 = 0.9
ADAM_B2 = 0.999
ADAM_EPS = 1e-08
ADAM_WD = 0.01
ADAM_STEP = 10

N_CHIPS = 4
BLOB_ROWS = 4096
HALF_ROWS = BLOB_ROWS // 2
SMALL_ROWS = 96


def _cparams(sem):
    return pltpu.CompilerParams(dimension_semantics=sem, vmem_limit_bytes=VMEM_LIMIT)


def _pick(n, cands):
    for c in cands:
        if n % c == 0:
            return c
    return n


MM_CHUNK = 512


def _mm(a, b, *, name, tiles, ta=False, tb=False, out_dtype=F32, res=None, a_act=None, epi_up=None):
    if ta:
        K, M = a.shape
    else:
        M, K = a.shape
    if tb:
        N, K2 = b.shape
    else:
        K2, N = b.shape
    assert K == K2, (a.shape, b.shape)
    tm, tn, tk = tiles
    assert M % tm == 0 and N % tn == 0 and K % tk == 0, (name, M, N, K, tiles)
    nk = K // tk
    dn = (((0 if ta else 1,), (1 if tb else 0,)), ((), ()))
    has_res = res is not None
    has_up = epi_up is not None
    cn = _pick(tn, (MM_CHUNK, 384, 256, 128))

    def prologue(av):
        if a_act == "relu2":
            r = jnp.maximum(av.astype(F32), 0.0)
            av = r * r
        return av.astype(BF16)

    def epilogue(out, res_v, up_v):
        if has_res:
            out = out + res_v.astype(F32)
        if has_up:
            out = out * (2.0 * jnp.maximum(up_v.astype(F32), 0.0))
        return out.astype(out_dtype)

    def body(*refs):
        a_ref, b_ref = refs[0], refs[1]
        i = 2
        res_ref = up_ref = None
        if has_res:
            res_ref = refs[i]
            i += 1
        if has_up:
            up_ref = refs[i]
            i += 1
        o_ref = refs[i]
        if nk == 1:
            av = prologue(a_ref[...])
            for c in range(tn // cn):
                cs = slice(c * cn, (c + 1) * cn)
                bv = (b_ref[cs, :] if tb else b_ref[:, cs]).astype(BF16)
                out = lax.dot_general(av, bv, dn, preferred_element_type=F32)
                o_ref[:, cs] = epilogue(out, res_ref[:, cs] if has_res else None, up_ref[:, cs] if has_up else None)
            return
        acc_ref = refs[i + 1]
        k = pl.program_id(2)

        @pl.when(k == 0)
        def _():
            acc_ref[...] = jnp.zeros_like(acc_ref)

        acc_ref[...] += lax.dot_general(prologue(a_ref[...]), b_ref[...].astype(BF16), dn,
                                        preferred_element_type=F32)

        @pl.when(k == nk - 1)
        def _():
            o_ref[...] = epilogue(acc_ref[...], res_ref[...] if has_res else None, up_ref[...] if has_up else None)

    a_spec = pl.BlockSpec((tk, tm), lambda i, j, k: (k, i)) if ta else pl.BlockSpec((tm, tk), lambda i, j, k: (i, k))
    b_spec = pl.BlockSpec((tn, tk), lambda i, j, k: (j, k)) if tb else pl.BlockSpec((tk, tn), lambda i, j, k: (k, j))
    o_spec = pl.BlockSpec((tm, tn), lambda i, j, k: (i, j))
    ins, specs = [a, b], [a_spec, b_spec]
    if has_res:
        ins.append(res)
        specs.append(o_spec)
    if has_up:
        ins.append(epi_up)
        specs.append(o_spec)
    return pl.pallas_call(
        body, name=name, grid=(M // tm, N // tn, nk),
        in_specs=specs, out_specs=o_spec,
        out_shape=jax.ShapeDtypeStruct((M, N), out_dtype),
        scratch_shapes=[] if nk == 1 else [pltpu.VMEM((tm, tn), F32)],
        compiler_params=_cparams(("parallel", "parallel", "arbitrary")),
    )(*ins)


def _rmsnorm_fwd(x, w, *, name):
    n, d = x.shape
    tm = _pick(n, (512, 256, 128))

    def body(x_ref, w_ref, y_ref, r_ref):
        xv = x_ref[...]
        rstd = lax.rsqrt(jnp.mean(xv * xv, axis=1, keepdims=True) + EPS)
        y_ref[...] = (xv * rstd * w_ref[...]).astype(BF16)
        r_ref[...] = rstd

    return pl.pallas_call(
        body, name=name, grid=(n // tm,),
        in_specs=[pl.BlockSpec((tm, d), lambda i: (i, 0)), pl.BlockSpec((1, d), lambda i: (0, 0))],
        out_specs=[pl.BlockSpec((tm, d), lambda i: (i, 0)), pl.BlockSpec((tm, 1), lambda i: (i, 0))],
        out_shape=[jax.ShapeDtypeStruct((n, d), BF16), jax.ShapeDtypeStruct((n, 1), F32)],
        compiler_params=_cparams(("parallel",)),
    )(x, w)


def _rmsnorm_bwd(dyn, x, rstd, w, dres, *, name):
    n, d = x.shape
    tm = _pick(n, (512, 256, 128))

    def body(g_ref, x_ref, r_ref, w_ref, d_ref, dx_ref, dxb_ref, dw_ref):
        @pl.when(pl.program_id(0) == 0)
        def _():
            dw_ref[...] = jnp.zeros_like(dw_ref)

        g = g_ref[...]
        r = r_ref[...]
        xhat = x_ref[...] * r
        gw = g * w_ref[...]
        dx = d_ref[...] + r * (gw - xhat * jnp.mean(gw * xhat, axis=1, keepdims=True))
        dx_ref[...] = dx
        dxb_ref[...] = dx.astype(BF16)
        dw_ref[...] += jnp.sum(g * xhat, axis=0, keepdims=True)

    row = pl.BlockSpec((tm, d), lambda i: (i, 0))
    vec = pl.BlockSpec((1, d), lambda i: (0, 0))
    return pl.pallas_call(
        body, name=name, grid=(n // tm,),
        in_specs=[row, row, pl.BlockSpec((tm, 1), lambda i: (i, 0)), vec, row],
        out_specs=[row, row, vec],
        out_shape=[jax.ShapeDtypeStruct((n, d), F32), jax.ShapeDtypeStruct((n, d), BF16),
                   jax.ShapeDtypeStruct((1, d), F32)],
        compiler_params=_cparams(("arbitrary",)),
    )(dyn, x, rstd, w, dres)


def _final(h2, w, target):
    n, d = h2.shape
    tm = _pick(n, (512, 256, 128))

    def body(h_ref, w_ref, t_ref, dh_ref, dhb_ref, loss_ref, dw_ref):
        @pl.when(pl.program_id(0) == 0)
        def _():
            loss_ref[...] = jnp.zeros_like(loss_ref)
            dw_ref[...] = jnp.zeros_like(dw_ref)

        hv = h_ref[...]
        wv = w_ref[...]
        rstd = lax.rsqrt(jnp.mean(hv * hv, axis=1, keepdims=True) + EPS)
        xhat = hv * rstd
        err = xhat * wv - t_ref[...]
        part = jnp.sum(jnp.mean(err * err, axis=1, keepdims=True), axis=0, keepdims=True)
        loss_ref[...] += 0.5 * part
        dy = err * (1.0 / d)
        gw = dy * wv
        dh = rstd * (gw - xhat * jnp.mean(gw * xhat, axis=1, keepdims=True))
        dh_ref[...] = dh
        dhb_ref[...] = dh.astype(BF16)
        dw_ref[...] += jnp.sum(dy * xhat, axis=0, keepdims=True)

    row = pl.BlockSpec((tm, d), lambda i: (i, 0))
    vec = pl.BlockSpec((1, d), lambda i: (0, 0))
    return pl.pallas_call(
        body, name="final_norm_loss", grid=(n // tm,),
        in_specs=[row, vec, row],
        out_specs=[row, row, pl.BlockSpec((1, 128), lambda i: (0, 0)), vec],
        out_shape=[jax.ShapeDtypeStruct((n, d), F32), jax.ShapeDtypeStruct((n, d), BF16),
                   jax.ShapeDtypeStruct((1, 128), F32), jax.ShapeDtypeStruct((1, d), F32)],
        compiler_params=_cparams(("arbitrary",)),
    )(h2, w, target)


def _softplus(x):
    return jnp.maximum(x, 0.0) + jnp.log(1.0 + jnp.exp(-jnp.abs(x)))


def _prep(proj_a, bias128, alog128, bl, t):
    n = bl * t
    nch = t // CHUNK
    col0 = (SSD_WIDTH + CONV_CH) // 128

    def body(p_ref, b_ref, al_ref, dt_ref, sg_ref, ac_ref, c_ref, sf_ref, carry):
        @pl.when(pl.program_id(1) == 0)
        def _():
            carry[...] = jnp.zeros_like(carry)

        xv = p_ref[...] + b_ref[...]
        sp = _softplus(xv)
        a = -jnp.exp(al_ref[...]) * sp
        logf = -_softplus(-xv)
        row = lax.broadcasted_iota(jnp.int32, (CHUNK, CHUNK), 0)
        col = lax.broadcasted_iota(jnp.int32, (CHUNK, CHUNK), 1)
        tril = (row >= col).astype(F32)
        acum = jnp.dot(tril, a, precision=HIGHEST, preferred_element_type=F32)
        c = jnp.dot(tril, logf, precision=HIGHEST, preferred_element_type=F32) + carry[...]
        carry[...] = c[CHUNK - 1:CHUNK, :]
        head_lanes = lax.broadcasted_iota(jnp.int32, (1, 128), 1) < 16
        dt_ref[...] = jnp.where(head_lanes, sp, 0.0)
        sg_ref[...] = jax.nn.sigmoid(xv)[:, 0:16]
        ac_ref[...] = jnp.where(head_lanes, acum, 0.0)
        c_ref[...] = c[:, 16:32]
        sf_ref[...] = jax.nn.sigmoid(-xv)[:, 16:32]

    o16 = pl.BlockSpec((CHUNK, 16), lambda b, c: (b * nch + c, 0))
    o128 = pl.BlockSpec((CHUNK, 128), lambda b, c: (b * nch + c, 0))
    v128 = pl.BlockSpec((1, 128), lambda b, c: (0, 0))
    w16 = jax.ShapeDtypeStruct((n, 16), F32)
    w128 = jax.ShapeDtypeStruct((n, 128), F32)
    return pl.pallas_call(
        body, name="head_scalars", grid=(bl, nch),
        in_specs=[pl.BlockSpec((CHUNK, 128), lambda b, c: (b * nch + c, col0)), v128, v128],
        out_specs=[o128, o16, o128, o16, o16],
        out_shape=[w128, w16, w128, w16, w16],
        scratch_shapes=[pltpu.VMEM((1, 128), F32)],
        compiler_params=_cparams(("parallel", "arbitrary")),
    )(proj_a, bias128, alog128)


def _fpost(dc, sigf, bl, t):
    n = bl * t
    nch = t // CHUNK

    def body(dc_ref, sf_ref, df_ref, db_ref, carry):
        @pl.when(pl.program_id(1) == 0)
        def _():
            carry[...] = jnp.zeros_like(carry)

        @pl.when((pl.program_id(0) == 0) & (pl.program_id(1) == 0))
        def _():
            db_ref[...] = jnp.zeros_like(db_ref)

        row = lax.broadcasted_iota(jnp.int32, (CHUNK, CHUNK), 0)
        col = lax.broadcasted_iota(jnp.int32, (CHUNK, CHUNK), 1)
        triu = (row <= col).astype(F32)
        dlf = jnp.dot(triu, dc_ref[...], precision=HIGHEST, preferred_element_type=F32) + carry[...]
        carry[...] = dlf[0:1, :]
        df = dlf[:, 0:16] * sf_ref[...]
        df_ref[...] = df
        db_ref[...] += jnp.sum(df, axis=0, keepdims=True)

    rev = lambda b, c: (b * nch + nch - 1 - c, 0)
    blk = pl.BlockSpec((CHUNK, 16), rev)
    return pl.pallas_call(
        body, name="forget_gate_bwd", grid=(bl, nch),
        in_specs=[pl.BlockSpec((CHUNK, 128), rev), blk],
        out_specs=[blk, pl.BlockSpec((1, 16), lambda b, c: (0, 0))],
        out_shape=[jax.ShapeDtypeStruct((n, 16), F32), jax.ShapeDtypeStruct((1, 16), F32)],
        scratch_shapes=[pltpu.VMEM((1, 128), F32)],
        compiler_params=_cparams(("arbitrary", "arbitrary")),
    )(dc, sigf)


CONV_TILE = 256
CONV_ROWS = 256


def _conv_taps(u_ref, i, w, bias):
    r0 = pl.multiple_of(i * CONV_ROWS, CONV_ROWS)
    cur = u_ref[pl.ds(r0, CONV_ROWS), :]
    p0 = pl.multiple_of(jnp.maximum(r0 - 8, 0), 8)
    prev = jnp.where(i > 0, u_ref[pl.ds(p0, 8), :], 0.0)
    cat = jnp.concatenate([prev, cur], axis=0)
    pre = bias + w[3:4, :] * cur
    taps = [cur]
    for s in (1, 2, 3):
        sh = pltpu.roll(cat, s, 0)[8:, :]
        taps.append(sh)
        pre = pre + w[3 - s:4 - s, :] * sh
    return r0, pre, taps


def _conv_fwd(proj_a, conv_w, conv_b, bl, t):
    n = bl * t
    nct = CONV_CH // CONV_TILE
    c0 = SSD_WIDTH // CONV_TILE

    def body(u_ref, w_ref, b_ref, o_ref):
        w = w_ref[...]
        bias = b_ref[...]

        def chunk(i, carry):
            r0, pre, _ = _conv_taps(u_ref, i, w, bias)
            o_ref[pl.ds(r0, CONV_ROWS), :] = pre * jax.nn.sigmoid(pre)
            return carry

        lax.fori_loop(0, t // CONV_ROWS, chunk, 0)

    return pl.pallas_call(
        body, name="conv_silu_fwd", grid=(bl, nct),
        in_specs=[pl.BlockSpec((t, CONV_TILE), lambda b, c: (b, c0 + c)),
                  pl.BlockSpec((4, CONV_TILE), lambda b, c: (0, c)),
                  pl.BlockSpec((1, CONV_TILE), lambda b, c: (0, c))],
        out_specs=pl.BlockSpec((t, CONV_TILE), lambda b, c: (b, c)),
        out_shape=jax.ShapeDtypeStruct((n, CONV_CH), F32),
        compiler_params=_cparams(("parallel", "parallel")),
    )(proj_a, conv_w, conv_b)


def _conv_bwd(dxc, proj_a, conv_w, conv_b, bl, t):
    n = bl * t
    nct = CONV_CH // CONV_TILE
    c0 = SSD_WIDTH // CONV_TILE
    nrc = t // CONV_ROWS

    def body(g_ref, u_ref, w_ref, b_ref, du_ref, dw_ref, db_ref, dp_scr):
        @pl.when(pl.program_id(1) == 0)
        def _():
            dw_ref[...] = jnp.zeros_like(dw_ref)
            db_ref[...] = jnp.zeros_like(db_ref)

        w = w_ref[...]
        bias = b_ref[...]
        dp_scr[pl.ds(t, 8), :] = jnp.zeros((8, CONV_TILE), F32)

        def chunk1(i, carry):
            dw0, dw1, dw2, dw3, db = carry
            r0, pre, taps = _conv_taps(u_ref, i, w, bias)
            sg = jax.nn.sigmoid(pre)
            dpre = g_ref[pl.ds(r0, CONV_ROWS), :] * (sg * (1.0 + pre * (1.0 - sg)))
            dp_scr[pl.ds(r0, CONV_ROWS), :] = dpre
            dw3 = dw3 + jnp.sum(dpre * taps[0], axis=0, keepdims=True)
            dw2 = dw2 + jnp.sum(dpre * taps[1], axis=0, keepdims=True)
            dw1 = dw1 + jnp.sum(dpre * taps[2], axis=0, keepdims=True)
            dw0 = dw0 + jnp.sum(dpre * taps[3], axis=0, keepdims=True)
            db = db + jnp.sum(dpre, axis=0, keepdims=True)
            return dw0, dw1, dw2, dw3, db

        z = jnp.zeros((1, CONV_TILE), F32)
        dw0, dw1, dw2, dw3, db = lax.fori_loop(0, nrc, chunk1, (z, z, z, z, z))
        dw_ref[...] += jnp.concatenate([dw0, dw1, dw2, dw3], axis=0)
        db_ref[...] += db

        def chunk2(i, carry):
            r0 = pl.multiple_of(i * CONV_ROWS, CONV_ROWS)
            cat = dp_scr[pl.ds(r0, CONV_ROWS + 8), :]
            du = w[3:4, :] * cat[:CONV_ROWS, :]
            for s in (1, 2, 3):
                du = du + w[3 - s:4 - s, :] * pltpu.roll(cat, CONV_ROWS + 8 - s, 0)[:CONV_ROWS, :]
            du_ref[pl.ds(r0, CONV_ROWS), :] = du.astype(BF16)
            return carry

        lax.fori_loop(0, nrc, chunk2, 0)

    return pl.pallas_call(
        body, name="conv_silu_bwd", grid=(nct, bl),
        in_specs=[pl.BlockSpec((t, CONV_TILE), lambda c, b: (b, c)),
                  pl.BlockSpec((t, CONV_TILE), lambda c, b: (b, c0 + c)),
                  pl.BlockSpec((4, CONV_TILE), lambda c, b: (0, c)),
                  pl.BlockSpec((1, CONV_TILE), lambda c, b: (0, c))],
        out_specs=[pl.BlockSpec((t, CONV_TILE), lambda c, b: (b, c)),
                   pl.BlockSpec((4, CONV_TILE), lambda c, b: (0, c)),
                   pl.BlockSpec((1, CONV_TILE), lambda c, b: (0, c))],
        out_shape=[jax.ShapeDtypeStruct((n, CONV_CH), BF16), jax.ShapeDtypeStruct((4, CONV_CH), F32),
                   jax.ShapeDtypeStruct((1, CONV_CH), F32)],
        scratch_shapes=[pltpu.VMEM((t + 8, CONV_TILE), F32)],
        compiler_params=_cparams(("parallel", "arbitrary")),
    )(dxc, proj_a, conv_w, conv_b)


NT_DIMS = (((1,), (1,)), ((), ()))
TN_DIMS = (((0,), (0,)), ((), ()))


def _dot(a, b, dims=None):
    if dims is None:
        return jnp.dot(a, b, preferred_element_type=F32)
    return lax.dot_general(a, b, dims, preferred_element_type=F32)


def _ssd_fwd(xc, proj_a, dt, acum, acum_t, dskip_e, norm_w, bl, t):
    n = bl * t
    nch = t // CHUNK
    L = CHUNK

    def body(xc_ref, z_ref, dt_ref, ac_ref, act_ref, dsk_ref, nw_ref, ys_ref, yp_ref, hp_ref, h_scr, y_scr):
        @pl.when(pl.program_id(1) == 0)
        def _():
            h_scr[...] = jnp.zeros_like(h_scr)

        row = lax.broadcasted_iota(jnp.int32, (L, L), 0)
        col = lax.broadcasted_iota(jnp.int32, (L, L), 1)
        causal = row >= col
        dt_all = dt_ref[...]
        ac_all = ac_ref[...]
        act_all = act_ref[...]
        for g in range(2):
            bg = xc_ref[:, SSD_WIDTH + g * 128:SSD_WIDTH + (g + 1) * 128].astype(BF16)
            cg = xc_ref[:, SSD_WIDTH + 256 + g * 128:SSD_WIDTH + 256 + (g + 1) * 128].astype(BF16)
            gmat = _dot(cg, bg, NT_DIMS)
            for r in range(8):
                h = g * 8 + r
                sl = slice(h * HEAD_DIM, (h + 1) * HEAD_DIM)
                xs = xc_ref[:, sl]
                xdt = xs * dt_all[:, h:h + 1]
                ac = ac_all[:, h:h + 1]
                ar = act_all[h:h + 1, :]
                ldec = jnp.exp(jnp.where(causal, ac - ar, NEG))
                m = (gmat * ldec).astype(BF16)
                hp = h_scr[h]
                hp_ref[h] = hp
                yd = _dot(m, xdt.astype(BF16))
                yo = _dot(cg, hp.astype(BF16), NT_DIMS) * jnp.exp(ac)
                y_scr[:, sl] = yd + yo + dsk_ref[:, sl] * xs
                alast = ac_all[L - 1:L, h:h + 1]
                xd = (xdt * jnp.exp(alast - ac)).astype(BF16)
                h_scr[h] = jnp.exp(alast) * hp + _dot(xd, bg, TN_DIMS)
        y = y_scr[...]
        yp_ref[...] = y
        zv = z_ref[...]
        yg = y * (zv * jax.nn.sigmoid(zv))
        for g in range(2):
            gs = slice(g * 512, (g + 1) * 512)
            grp = yg[:, gs]
            rstd = lax.rsqrt(jnp.mean(grp * grp, axis=1, keepdims=True) + EPS)
            ys_ref[:, gs] = (grp * rstd * nw_ref[:, gs]).astype(BF16)

    rb = lambda b, c: (b * nch + c, 0)
    v1k = pl.BlockSpec((1, SSD_WIDTH), lambda b, c: (0, 0))
    return pl.pallas_call(
        body, name="ssd_fwd", grid=(bl, nch),
        in_specs=[pl.BlockSpec((L, CONV_CH), rb), pl.BlockSpec((L, SSD_WIDTH), rb),
                  pl.BlockSpec((L, 16), rb), pl.BlockSpec((L, 16), rb),
                  pl.BlockSpec((16, L), lambda b, c: (0, b * nch + c)), v1k, v1k],
        out_specs=[pl.BlockSpec((L, SSD_WIDTH), rb), pl.BlockSpec((L, SSD_WIDTH), rb),
                   pl.BlockSpec((None, 16, HEAD_DIM, SSD_STATE), lambda b, c: (b * nch + c, 0, 0, 0))],
        out_shape=[jax.ShapeDtypeStruct((n, SSD_WIDTH), BF16), jax.ShapeDtypeStruct((n, SSD_WIDTH), F32),
                   jax.ShapeDtypeStruct((bl * nch, 16, HEAD_DIM, SSD_STATE), F32)],
        scratch_shapes=[pltpu.VMEM((16, HEAD_DIM, SSD_STATE), F32), pltpu.VMEM((L, SSD_WIDTH), F32)],
        compiler_params=_cparams(("parallel", "arbitrary")),
    )(xc, proj_a, dt, acum, acum_t, dskip_e, norm_w)


def _ssd_bwd(dys, xc, proj_a, ypre, hprev, dt, sig, acum, acum_t, a_log, dskip_e, norm_w, bl, t):
    n = bl * t
    nch = t // CHUNK
    L = CHUNK

    def body(dys_ref, xc_ref, z_ref, yp_ref, hp_ref, dt_ref, sg_ref, ac_ref, act_ref, al_ref, dsk_ref, nw_ref,
             dxc_ref, dz_ref, ddt_ref, dnw_ref, dsk16_ref, da16_ref, db16_ref, dh_scr, dy_scr):
        first = (pl.program_id(0) == 0) & (pl.program_id(1) == 0)

        @pl.when(first)
        def _():
            dnw_ref[...] = jnp.zeros_like(dnw_ref)
            dsk16_ref[...] = jnp.zeros_like(dsk16_ref)
            da16_ref[...] = jnp.zeros_like(da16_ref)
            db16_ref[...] = jnp.zeros_like(db16_ref)

        @pl.when(pl.program_id(1) == 0)
        def _():
            dh_scr[...] = jnp.zeros_like(dh_scr)

        y = yp_ref[...]
        zv = z_ref[...]
        sz = jax.nn.sigmoid(zv)
        gate = zv * sz
        yg = y * gate
        dout = dys_ref[...]
        nw = nw_ref[...]
        for g in range(2):
            gs = slice(g * 512, (g + 1) * 512)
            grp = yg[:, gs]
            rstd = lax.rsqrt(jnp.mean(grp * grp, axis=1, keepdims=True) + EPS)
            ghat = grp * rstd
            dnw_ref[:, gs] += jnp.sum(dout[:, gs] * ghat, axis=0, keepdims=True)
            gw = dout[:, gs] * nw[:, gs]
            dyg = rstd * (gw - ghat * jnp.mean(gw * ghat, axis=1, keepdims=True))
            dy_scr[:, gs] = dyg * gate[:, gs]
            dz_ref[:, gs] = (dyg * y[:, gs] * (sz[:, gs] * (1.0 + zv[:, gs] * (1.0 - sz[:, gs])))).astype(BF16)

        row = lax.broadcasted_iota(jnp.int32, (L, L), 0)
        col = lax.broadcasted_iota(jnp.int32, (L, L), 1)
        causal = row >= col
        lane16 = lax.broadcasted_iota(jnp.int32, (1, 16), 1)
        lane128 = lax.broadcasted_iota(jnp.int32, (1, L), 1)
        last_row = lax.broadcasted_iota(jnp.int32, (L, 1), 0) == (L - 1)
        dt_all = dt_ref[...]
        ac_all = ac_ref[...]
        act_all = act_ref[...]
        dac_col = jnp.zeros((L, L), F32)
        dac_row = jnp.zeros((L, L), F32)
        ddt_x = jnp.zeros((L, L), F32)
        dsk16 = jnp.zeros((1, 16), F32)
        rows16 = lax.broadcasted_iota(jnp.int32, (L, 1), 0)
        for g in range(2):
            bsl = slice(SSD_WIDTH + g * 128, SSD_WIDTH + (g + 1) * 128)
            csl = slice(SSD_WIDTH + 256 + g * 128, SSD_WIDTH + 256 + (g + 1) * 128)
            bg = xc_ref[:, bsl].astype(BF16)
            cg = xc_ref[:, csl].astype(BF16)
            gmat = _dot(cg, bg, NT_DIMS)
            dg_sum = jnp.zeros((L, L), F32)
            dc_acc = jnp.zeros((L, SSD_STATE), F32)
            db_acc = jnp.zeros((L, SSD_STATE), F32)
            for r in range(8):
                h = g * 8 + r
                sl = slice(h * HEAD_DIM, (h + 1) * HEAD_DIM)
                onehot = lane16 == h
                onehot_w = lane128 == h
                xs = xc_ref[:, sl]
                dth = dt_all[:, h:h + 1]
                xdt = xs * dth
                xb = xdt.astype(BF16)
                ac = ac_all[:, h:h + 1]
                ar = act_all[h:h + 1, :]
                alast = ac_all[L - 1:L, h:h + 1]
                ldec = jnp.exp(jnp.where(causal, ac - ar, NEG))
                mf = gmat * ldec
                e_in = jnp.exp(ac)
                dec = jnp.exp(alast - ac)
                elast = jnp.exp(alast)
                hp = hp_ref[h]
                hpb = hp.astype(BF16)
                dyh = dy_scr[:, sl]
                dyb = dyh.astype(BF16)
                dsk16 = dsk16 + jnp.where(onehot, jnp.sum(jnp.sum(dyh * xs, axis=1, keepdims=True), axis=0, keepdims=True), 0.0)
                dm = _dot(dyb, xb, NT_DIMS)
                dx = _dot(mf.astype(BF16), dyb, TN_DIMS)
                dg_sum = dg_sum + dm * ldec
                wmat = dm * mf
                dac_h = jnp.sum(wmat, axis=1, keepdims=True)
                dac_row = dac_row + jnp.where(rows16 == h, -jnp.sum(wmat, axis=0, keepdims=True), 0.0)
                ch = _dot(cg, hpb, NT_DIMS)
                dye = dyh * e_in
                dyeb = dye.astype(BF16)
                dc_acc = dc_acc + _dot(dyeb, hpb)
                dhp = _dot(dyeb, cg, TN_DIMS)
                dac_h = dac_h + jnp.sum(dye * ch, axis=1, keepdims=True)
                ds = dh_scr[h]
                dsb = ds.astype(BF16)
                dxd = _dot(bg, dsb, NT_DIMS)
                db_acc = db_acc + _dot((xdt * dec).astype(BF16), dsb)
                dx = dx + dxd * dec
                ddec = jnp.sum(dxd * xdt, axis=1, keepdims=True) * dec
                extra = (jnp.sum(ddec, axis=0, keepdims=True)
                         + elast * jnp.sum(jnp.sum(hp * ds, axis=1, keepdims=True), axis=0, keepdims=True))
                dac_h = dac_h - ddec + jnp.where(last_row, extra, 0.0)
                dh_scr[h] = elast * ds + dhp
                dac_col = dac_col + jnp.where(onehot_w, dac_h, 0.0)
                ddt_x = ddt_x + jnp.where(onehot_w, jnp.sum(dx * xs, axis=1, keepdims=True), 0.0)
                dxc_ref[:, sl] = dx * dth + dsk_ref[:, sl] * dyh
            dgb = dg_sum.astype(BF16)
            dxc_ref[:, csl] = dc_acc + _dot(dgb, bg)
            dxc_ref[:, bsl] = db_acc + _dot(dgb, cg, TN_DIMS)
        dac = dac_col + jnp.transpose(dac_row)
        triu = (row <= col).astype(F32)
        da = jnp.dot(triu, dac, precision=HIGHEST, preferred_element_type=F32)[:, 0:16]
        a_row = -jnp.exp(al_ref[...])
        ddt = (ddt_x[:, 0:16] + da * a_row) * sg_ref[...]
        ddt_ref[...] = ddt
        dsk16_ref[...] += dsk16
        da16_ref[...] += jnp.sum(da * dt_all, axis=0, keepdims=True) * a_row
        db16_ref[...] += jnp.sum(ddt, axis=0, keepdims=True)

    rb = lambda b, c: (b * nch + nch - 1 - c, 0)
    v1k = pl.BlockSpec((1, SSD_WIDTH), lambda b, c: (0, 0))
    v16 = pl.BlockSpec((1, 16), lambda b, c: (0, 0))
    wide = pl.BlockSpec((L, SSD_WIDTH), rb)
    s16 = pl.BlockSpec((L, 16), rb)
    return pl.pallas_call(
        body, name="ssd_bwd", grid=(bl, nch),
        in_specs=[wide, pl.BlockSpec((L, CONV_CH), rb), wide, wide,
                  pl.BlockSpec((None, 16, HEAD_DIM, SSD_STATE), lambda b, c: (b * nch + nch - 1 - c, 0, 0, 0)),
                  s16, s16, s16, pl.BlockSpec((16, L), lambda b, c: (0, b * nch + nch - 1 - c)), v16, v1k, v1k],
        out_specs=[pl.BlockSpec((L, CONV_CH), rb), wide, s16, v1k, v16, v16, v16],
        out_shape=[jax.ShapeDtypeStruct((n, CONV_CH), F32), jax.ShapeDtypeStruct((n, SSD_WIDTH), BF16),
                   jax.ShapeDtypeStruct((n, 16), F32), jax.ShapeDtypeStruct((1, SSD_WIDTH), F32),
                   jax.ShapeDtypeStruct((1, 16), F32), jax.ShapeDtypeStruct((1, 16), F32),
                   jax.ShapeDtypeStruct((1, 16), F32)],
        scratch_shapes=[pltpu.VMEM((16, HEAD_DIM, SSD_STATE), F32), pltpu.VMEM((L, SSD_WIDTH), F32)],
        compiler_params=_cparams(("arbitrary", "arbitrary")),
    )(dys, xc, proj_a, ypre, hprev, dt, sig, acum, acum_t, a_log, dskip_e, norm_w)


def _head_expander():
    r = lax.broadcasted_iota(jnp.int32, (128, SSD_WIDTH), 0)
    c = lax.broadcasted_iota(jnp.int32, (128, SSD_WIDTH), 1)
    return (c // HEAD_DIM == r).astype(F32)


def _spread(v128, expander):
    return jnp.dot(v128, expander, precision=HIGHEST, preferred_element_type=F32)


def _head_sums(v1024, expander):
    return lax.dot_general(v1024, expander, NT_DIMS, precision=HIGHEST, preferred_element_type=F32)


def _ssd_fwd2(xc, proj_a, dt, acum, acum_t, dskip_e, norm_w, bl, t):
    n = bl * t
    nch = t // CHUNK
    L = CHUNK

    def body(xc_ref, z_ref, dt_ref, ac_ref, act_ref, dsk_ref, nw_ref, ys_ref, yp_ref, hp_ref, h_scr, y_scr, x_scr):
        @pl.when(pl.program_id(1) == 0)
        def _():
            h_scr[...] = jnp.zeros_like(h_scr)

        row = lax.broadcasted_iota(jnp.int32, (L, L), 0)
        col = lax.broadcasted_iota(jnp.int32, (L, L), 1)
        causal = row >= col
        expander = _head_expander()
        ac_all = ac_ref[...]
        act_all = act_ref[...]
        ac_e = _spread(ac_all, expander)
        e_in = jnp.exp(ac_e)
        dec = jnp.exp(ac_e[L - 1:L, :] - ac_e)
        xs_all = xc_ref[:, 0:SSD_WIDTH]
        x_all = xs_all * _spread(dt_ref[...], expander)
        x_scr[...] = x_all.astype(BF16)
        hp_all = h_scr[...]
        hp_ref[...] = hp_all
        for g in range(2):
            gs = slice(g * 512, (g + 1) * 512)
            bg = xc_ref[:, SSD_WIDTH + g * 128:SSD_WIDTH + (g + 1) * 128].astype(BF16)
            cg = xc_ref[:, SSD_WIDTH + 256 + g * 128:SSD_WIDTH + 256 + (g + 1) * 128].astype(BF16)
            gmat = _dot(cg, bg, NT_DIMS)
            y_scr[:, gs] = (_dot(cg, hp_all[gs, :].astype(BF16), NT_DIMS) * e_in[:, gs]
                            + dsk_ref[:, gs] * xs_all[:, gs])
            s_new = _dot((x_all[:, gs] * dec[:, gs]).astype(BF16), bg, TN_DIMS)
            for r in range(8):
                h = g * 8 + r
                sl = slice(h * HEAD_DIM, (h + 1) * HEAD_DIM)
                ldec = jnp.exp(jnp.where(causal, ac_all[:, h:h + 1] - act_all[h:h + 1, :], NEG))
                y_scr[:, sl] += _dot((gmat * ldec).astype(BF16), x_scr[:, sl])
                elast = jnp.exp(ac_all[L - 1:L, h:h + 1])
                h_scr[sl, :] = elast * hp_all[sl, :] + s_new[r * HEAD_DIM:(r + 1) * HEAD_DIM, :]
        y = y_scr[...]
        yp_ref[...] = y
        zv = z_ref[...]
        yg = y * (zv * jax.nn.sigmoid(zv))
        for g in range(2):
            gs = slice(g * 512, (g + 1) * 512)
            grp = yg[:, gs]
            rstd = lax.rsqrt(jnp.mean(grp * grp, axis=1, keepdims=True) + EPS)
            ys_ref[:, gs] = (grp * rstd * nw_ref[:, gs]).astype(BF16)

    rb = lambda b, c: (b * nch + c, 0)
    v1k = pl.BlockSpec((1, SSD_WIDTH), lambda b, c: (0, 0))
    return pl.pallas_call(
        body, name="ssd_fwd", grid=(bl, nch),
        in_specs=[pl.BlockSpec((L, CONV_CH), rb), pl.BlockSpec((L, SSD_WIDTH), rb),
                  pl.BlockSpec((L, 128), rb), pl.BlockSpec((L, 128), rb),
                  pl.BlockSpec((16, L), lambda b, c: (0, b * nch + c)), v1k, v1k],
        out_specs=[pl.BlockSpec((L, SSD_WIDTH), rb), pl.BlockSpec((L, SSD_WIDTH), rb),
                   pl.BlockSpec((None, SSD_WIDTH, SSD_STATE), lambda b, c: (b * nch + c, 0, 0))],
        out_shape=[jax.ShapeDtypeStruct((n, SSD_WIDTH), BF16), jax.ShapeDtypeStruct((n, SSD_WIDTH), F32),
                   jax.ShapeDtypeStruct((bl * nch, SSD_WIDTH, SSD_STATE), F32)],
        scratch_shapes=[pltpu.VMEM((SSD_WIDTH, SSD_STATE), F32), pltpu.VMEM((L, SSD_WIDTH), F32),
                        pltpu.VMEM((L, SSD_WIDTH), BF16)],
        compiler_params=_cparams(("parallel", "arbitrary")),
    )(xc, proj_a, dt, acum, acum_t, dskip_e, norm_w)


def _ssd_bwd2(dys, xc, proj_a, ypre, hprev, dt, sig, acum, acum_t, a_log, dskip_e, norm_w, bl, t):
    n = bl * t
    nch = t // CHUNK
    L = CHUNK

    def body(dys_ref, xc_ref, z_ref, yp_ref, hp_ref, dt_ref, sg_ref, ac_ref, act_ref, al_ref, dsk_ref, nw_ref,
             dxc_ref, dz_ref, ddt_ref, dnw_ref, dsk16_ref, da16_ref, db16_ref,
             dh_scr, dy_scr, x_scr, dx_scr, red_scr):
        first = (pl.program_id(0) == 0) & (pl.program_id(1) == 0)

        @pl.when(first)
        def _():
            dnw_ref[...] = jnp.zeros_like(dnw_ref)
            dsk16_ref[...] = jnp.zeros_like(dsk16_ref)
            da16_ref[...] = jnp.zeros_like(da16_ref)
            db16_ref[...] = jnp.zeros_like(db16_ref)

        @pl.when(pl.program_id(1) == 0)
        def _():
            dh_scr[...] = jnp.zeros_like(dh_scr)

        y = yp_ref[...]
        zv = z_ref[...]
        sz = jax.nn.sigmoid(zv)
        gate = zv * sz
        yg = y * gate
        dout = dys_ref[...]
        nw = nw_ref[...]
        for g in range(2):
            gs = slice(g * 512, (g + 1) * 512)
            grp = yg[:, gs]
            rstd = lax.rsqrt(jnp.mean(grp * grp, axis=1, keepdims=True) + EPS)
            ghat = grp * rstd
            dnw_ref[:, gs] += jnp.sum(dout[:, gs] * ghat, axis=0, keepdims=True)
            gw = dout[:, gs] * nw[:, gs]
            dyg = rstd * (gw - ghat * jnp.mean(gw * ghat, axis=1, keepdims=True))
            dy_scr[:, gs] = dyg * gate[:, gs]
            dz_ref[:, gs] = (dyg * y[:, gs] * (sz[:, gs] * (1.0 + zv[:, gs] * (1.0 - sz[:, gs])))).astype(BF16)

        row = lax.broadcasted_iota(jnp.int32, (L, L), 0)
        col = lax.broadcasted_iota(jnp.int32, (L, L), 1)
        causal = row >= col
        lane128 = lax.broadcasted_iota(jnp.int32, (1, L), 1)
        rows128 = lax.broadcasted_iota(jnp.int32, (L, 1), 0)
        last_row = rows128 == (L - 1)
        expander = _head_expander()
        ac_all = ac_ref[...]
        act_all = act_ref[...]
        dt_all = dt_ref[...]
        dt_e = _spread(dt_all, expander)
        ac_e = _spread(ac_all, expander)
        e_in = jnp.exp(ac_e)
        dec = jnp.exp(ac_e[L - 1:L, :] - ac_e)
        xs_all = xc_ref[:, 0:SSD_WIDTH]
        x_all = xs_all * dt_e
        x_scr[...] = x_all.astype(BF16)
        dy_all = dy_scr[...]
        hp_all = hp_ref[...]
        ds_all = dh_scr[...]
        dsk_cols = jnp.sum(dy_all * xs_all, axis=0, keepdims=True)
        dac = jnp.zeros((L, L), F32)
        dac_row = jnp.zeros((L, L), F32)
        ddec_cols = []
        for g in range(2):
            gs = slice(g * 512, (g + 1) * 512)
            bsl = slice(SSD_WIDTH + g * 128, SSD_WIDTH + (g + 1) * 128)
            csl = slice(SSD_WIDTH + 256 + g * 128, SSD_WIDTH + 256 + (g + 1) * 128)
            bg = xc_ref[:, bsl].astype(BF16)
            cg = xc_ref[:, csl].astype(BF16)
            gmat = _dot(cg, bg, NT_DIMS)
            hpb = hp_all[gs, :].astype(BF16)
            dsb = ds_all[gs, :].astype(BF16)
            ch = _dot(cg, hpb, NT_DIMS)
            dye = dy_all[:, gs] * e_in[:, gs]
            dyeb = dye.astype(BF16)
            dc_acc = _dot(dyeb, hpb)
            dhp = _dot(dyeb, cg, TN_DIMS)
            dxd = _dot(bg, dsb, NT_DIMS)
            db_acc = _dot((x_all[:, gs] * dec[:, gs]).astype(BF16), dsb)
            ddec = dxd * x_all[:, gs] * dec[:, gs]
            ddec_cols.append(jnp.sum(ddec, axis=0, keepdims=True))
            dx_scr[:, gs] = dxd * dec[:, gs]
            red_scr[:, gs] = dye * ch - ddec
            dg_sum = jnp.zeros((L, L), F32)
            for r in range(8):
                h = g * 8 + r
                sl = slice(h * HEAD_DIM, (h + 1) * HEAD_DIM)
                onehot_w = lane128 == h
                ldec = jnp.exp(jnp.where(causal, ac_all[:, h:h + 1] - act_all[h:h + 1, :], NEG))
                mf = gmat * ldec
                dyb = dy_scr[:, sl].astype(BF16)
                dm = _dot(dyb, x_scr[:, sl], NT_DIMS)
                dx_scr[:, sl] += _dot(mf.astype(BF16), dyb, TN_DIMS)
                dg_sum = dg_sum + dm * ldec
                wmat = dm * mf
                elast = jnp.exp(ac_all[L - 1:L, h:h + 1])
                hp_h = hp_all[sl, :]
                ds_h = ds_all[sl, :]
                extra = elast * jnp.sum(jnp.sum(hp_h * ds_h, axis=1, keepdims=True), axis=0, keepdims=True)
                dac = dac + jnp.where(onehot_w, jnp.sum(wmat, axis=1, keepdims=True) + jnp.where(last_row, extra, 0.0),
                                      0.0)
                dac_row = dac_row + jnp.where(rows128 == h, -jnp.sum(wmat, axis=0, keepdims=True), 0.0)
                dh_scr[sl, :] = elast * ds_h + dhp[r * HEAD_DIM:(r + 1) * HEAD_DIM, :]
            dgb = dg_sum.astype(BF16)
            dxc_ref[:, csl] = dc_acc + _dot(dgb, bg)
            dxc_ref[:, bsl] = db_acc + _dot(dgb, cg, TN_DIMS)
        dx_all = dx_scr[...]
        dxc_ref[:, 0:SSD_WIDTH] = dx_all * dt_e + dsk_ref[...] * dy_all
        red = red_scr[...]
        dac_slab = _head_sums(red, expander)
        ddec_tot = _head_sums(jnp.broadcast_to(jnp.concatenate(ddec_cols, axis=1), (8, SSD_WIDTH)), expander)
        ddt_x = _head_sums(dx_all * xs_all, expander)
        dsk16_ref[...] += _head_sums(jnp.broadcast_to(dsk_cols, (8, SSD_WIDTH)), expander)[0:1, 0:16]
        dac = dac + dac_slab + jnp.transpose(dac_row) + jnp.where(last_row, ddec_tot[0:1, :], 0.0)
        triu = (row <= col).astype(F32)
        da = jnp.dot(triu, dac, precision=HIGHEST, preferred_element_type=F32)[:, 0:16]
        a_row = -jnp.exp(al_ref[...])
        dt16 = dt_all[:, 0:16]
        ddt = (ddt_x[:, 0:16] + da * a_row) * sg_ref[...]
        ddt_ref[...] = ddt
        da16_ref[...] += jnp.sum(da * dt16, axis=0, keepdims=True) * a_row
        db16_ref[...] += jnp.sum(ddt, axis=0, keepdims=True)

    rb = lambda b, c: (b * nch + nch - 1 - c, 0)
    v1k = pl.BlockSpec((1, SSD_WIDTH), lambda b, c: (0, 0))
    v16 = pl.BlockSpec((1, 16), lambda b, c: (0, 0))
    wide = pl.BlockSpec((L, SSD_WIDTH), rb)
    s16 = pl.BlockSpec((L, 16), rb)
    s128 = pl.BlockSpec((L, 128), rb)
    return pl.pallas_call(
        body, name="ssd_bwd", grid=(bl, nch),
        in_specs=[wide, pl.BlockSpec((L, CONV_CH), rb), wide, wide,
                  pl.BlockSpec((None, SSD_WIDTH, SSD_STATE), lambda b, c: (b * nch + nch - 1 - c, 0, 0)),
                  s128, s16, s128, pl.BlockSpec((16, L), lambda b, c: (0, b * nch + nch - 1 - c)), v16, v1k, v1k],
        out_specs=[pl.BlockSpec((L, CONV_CH), rb), wide, s16, v1k, v16, v16, v16],
        out_shape=[jax.ShapeDtypeStruct((n, CONV_CH), F32), jax.ShapeDtypeStruct((n, SSD_WIDTH), BF16),
                   jax.ShapeDtypeStruct((n, 16), F32), jax.ShapeDtypeStruct((1, SSD_WIDTH), F32),
                   jax.ShapeDtypeStruct((1, 16), F32), jax.ShapeDtypeStruct((1, 16), F32),
                   jax.ShapeDtypeStruct((1, 16), F32)],
        scratch_shapes=[pltpu.VMEM((SSD_WIDTH, SSD_STATE), F32), pltpu.VMEM((L, SSD_WIDTH), F32),
                        pltpu.VMEM((L, SSD_WIDTH), BF16), pltpu.VMEM((L, SSD_WIDTH), F32),
                        pltpu.VMEM((L, SSD_WIDTH), F32)],
        compiler_params=_cparams(("arbitrary", "arbitrary")),
    )(dys, xc, proj_a, ypre, hprev, dt, sig, acum, acum_t, a_log, dskip_e, norm_w)


def _attn_fwd(qkv, negc, bl, t):
    n = bl * t
    tb_ = ATT_BLOCK
    nb = t // tb_
    scale2 = LOG2E / math.sqrt(HEAD_DIM)

    def body(q_ref, k_ref, v_ref, c_ref, o_ref, lse_ref):
        row = lax.broadcasted_iota(jnp.int32, (tb_, tb_), 0)
        col = lax.broadcasted_iota(jnp.int32, (tb_, tb_), 1)
        causal = row >= col
        for qi in range(nb):
            r0, lk = qi * tb_, (qi + 1) * tb_
            for j in range(2):
                sl = slice(j * HEAD_DIM, (j + 1) * HEAD_DIM)
                s = _dot(q_ref[r0:lk, sl], k_ref[0:lk, sl], NT_DIMS) * scale2 + c_ref[j:j + 1, 0:lk] * LOG2E
                tail = jnp.where(causal, s[:, r0:lk], NEG)
                s = tail if qi == 0 else jnp.concatenate([s[:, 0:r0], tail], axis=1)
                m = jnp.max(s, axis=1, keepdims=True)
                p = jnp.exp2(s - m)
                l = jnp.sum(p, axis=1, keepdims=True)
                acc = _dot(p.astype(BF16), v_ref[0:lk, sl])
                o_ref[r0:lk, sl] = (acc / l).astype(BF16)
                lse_ref[r0:lk, sl] = jnp.broadcast_to(m + jnp.log(l) * LOG2E, (tb_, HEAD_DIM))

    blk = lambda off: pl.BlockSpec((t, 128), lambda b, hp: (b, off + hp))
    return pl.pallas_call(
        body, name="fox_attn_fwd", grid=(bl, 8),
        in_specs=[blk(0), blk(8), blk(16), pl.BlockSpec((None, None, 8, t), lambda b, hp: (b, hp, 0, 0))],
        out_specs=[blk(0), blk(0)],
        out_shape=[jax.ShapeDtypeStruct((n, ATT_WIDTH), BF16), jax.ShapeDtypeStruct((n, ATT_WIDTH), F32)],
        compiler_params=_cparams(("parallel", "parallel")),
    )(qkv, qkv, qkv, negc)


def _attn_bwd(qkv, do, o, lse, negc, bl, t):
    n = bl * t
    tb_ = ATT_BLOCK
    nb = t // tb_
    scale = 1.0 / math.sqrt(HEAD_DIM)
    scale2 = LOG2E * scale

    def body(q_ref, k_ref, v_ref, do_ref, o_ref, lse_ref, c_ref, dq_ref, dk_ref, dv_ref, dc_ref,
             dq_scr, delta_scr, dr_scr, qt_scr, dot_scr, dkt_scr, dvt_scr):
        row = lax.broadcasted_iota(jnp.int32, (tb_, tb_), 0)
        col = lax.broadcasted_iota(jnp.int32, (tb_, tb_), 1)
        causal = row >= col
        dq_scr[...] = jnp.zeros_like(dq_scr)
        dr_scr[...] = jnp.zeros_like(dr_scr)
        dc_ref[...] = jnp.zeros_like(dc_ref)
        qt_scr[...] = jnp.transpose(q_ref[...].astype(F32)).astype(BF16)
        dot_scr[...] = jnp.transpose(do_ref[...].astype(F32)).astype(BF16)
        prod = do_ref[...].astype(F32) * o_ref[...].astype(F32)
        for j in range(2):
            sl = slice(j * HEAD_DIM, (j + 1) * HEAD_DIM)
            delta_scr[:, sl] = jnp.broadcast_to(jnp.sum(prod[:, sl], axis=1, keepdims=True), (t, HEAD_DIM))
        for kj in range(nb):
            r0, r1 = kj * tb_, (kj + 1) * tb_
            for j in range(2):
                sl = slice(j * HEAD_DIM, (j + 1) * HEAD_DIM)
                one = slice(j * HEAD_DIM, j * HEAD_DIM + 1)
                kb = k_ref[r0:r1, sl]
                qs = q_ref[r0:t, sl]
                dos = do_ref[r0:t, sl]
                s = _dot(qs, kb, NT_DIMS) * scale2 + c_ref[j:j + 1, r0:r1] * LOG2E
                head = jnp.where(causal, s[0:tb_, :], NEG)
                s = head if kj == nb - 1 else jnp.concatenate([head, s[tb_:, :]], axis=0)
                p = jnp.exp2(s - lse_ref[r0:t, one])
                dp = _dot(dos, v_ref[r0:r1, sl], NT_DIMS)
                ds = p * (dp - delta_scr[r0:t, one])
                dsb = ds.astype(BF16)
                dvt_scr[sl, r0:r1] = _dot(dot_scr[sl, r0:t], p.astype(BF16))
                dkt_scr[sl, r0:r1] = _dot(qt_scr[sl, r0:t], dsb)
                dq_scr[r0:t, sl] += _dot(dsb, kb)
                dr_scr[r0:t, sl] += jnp.broadcast_to(jnp.sum(ds, axis=1, keepdims=True), (t - r0, HEAD_DIM))
                dc_ref[j:j + 1, r0:r1] = -jnp.sum(ds, axis=0, keepdims=True)
        dq_ref[...] = (dq_scr[...] * scale).astype(BF16)
        dk_ref[...] = (jnp.transpose(dkt_scr[...]) * scale).astype(BF16)
        dv_ref[...] = jnp.transpose(dvt_scr[...]).astype(BF16)
        dr_t = jnp.transpose(dr_scr[...])
        for j in range(2):
            dc_ref[j:j + 1, :] += dr_t[j * HEAD_DIM:j * HEAD_DIM + 1, :]

    blk = lambda off: pl.BlockSpec((t, 128), lambda b, hp: (b, off + hp))
    cblk = pl.BlockSpec((None, None, 8, t), lambda b, hp: (b, hp, 0, 0))
    return pl.pallas_call(
        body, name="fox_attn_bwd", grid=(bl, 8),
        in_specs=[blk(0), blk(8), blk(16), blk(0), blk(0), blk(0), cblk],
        out_specs=[blk(0), blk(0), blk(0), cblk],
        out_shape=[jax.ShapeDtypeStruct((n, ATT_WIDTH), BF16)] * 3 + [jax.ShapeDtypeStruct((bl, 8, 8, t), F32)],
        scratch_shapes=[pltpu.VMEM((t, 128), F32), pltpu.VMEM((t, 128), F32), pltpu.VMEM((t, 128), F32),
                        pltpu.VMEM((128, t), BF16), pltpu.VMEM((128, t), BF16),
                        pltpu.VMEM((128, t), F32), pltpu.VMEM((128, t), F32)],
        compiler_params=_cparams(("parallel", "parallel")),
    )(qkv, qkv, qkv, do, o, lse, negc)


def _adamw(w, g, m, v, *, name):
    r, c = w.shape
    tr = _pick(r, (256, 128, 64, 32, 16, 8))
    bc1 = 1.0 - ADAM_B1 ** ADAM_STEP
    bc2 = 1.0 - ADAM_B2 ** ADAM_STEP

    def body(w_ref, g_ref, m_ref, v_ref, d_ref, nm_ref, nv_ref):
        gv = g_ref[...]
        mn = ADAM_B1 * m_ref[...] + (1.0 - ADAM_B1) * gv
        vn = ADAM_B2 * v_ref[...] + (1.0 - ADAM_B2) * (gv * gv)
        m_hat = mn / bc1
        v_hat = vn / bc2
        d_ref[...] = -ADAM_LR * (m_hat / (jnp.sqrt(v_hat) + ADAM_EPS) + ADAM_WD * w_ref[...])
        nm_ref[...] = mn
        nv_ref[...] = vn

    blk = pl.BlockSpec((tr, c), lambda i: (i, 0))
    return pl.pallas_call(
        body, name=name, grid=(r // tr,), in_specs=[blk] * 4, out_specs=[blk] * 3,
        out_shape=[jax.ShapeDtypeStruct((r, c), F32)] * 3,
        compiler_params=_cparams(("parallel",)),
    )(w, g, m, v)


def _sum_leading(parts, *, name, out_dtype=F32):
    k, r, c = parts.shape
    tr = _pick(r, (512, 256, 128, 96, 64, 32, 16, 8))

    def body(p_ref, o_ref):
        acc = p_ref[0].astype(F32)
        for i in range(1, k):
            acc = acc + p_ref[i].astype(F32)
        o_ref[...] = acc.astype(out_dtype)

    return pl.pallas_call(
        body, name=name, grid=(r // tr,),
        in_specs=[pl.BlockSpec((k, tr, c), lambda i: (0, i, 0))],
        out_specs=pl.BlockSpec((tr, c), lambda i: (i, 0)),
        out_shape=jax.ShapeDtypeStruct((r, c), out_dtype),
        compiler_params=_cparams(("parallel",)),
    )(parts)


def _add_pair(a, b, *, name):
    k, r, c = a.shape
    tr = _pick(r, (512, 256, 128))

    def body(a_ref, b_ref, o_ref):
        o_ref[...] = (a_ref[...].astype(F32) + b_ref[...].astype(F32)).astype(BF16)

    blk = pl.BlockSpec((None, tr, c), lambda j, i: (j, i, 0))
    return pl.pallas_call(
        body, name=name, grid=(k, r // tr), in_specs=[blk, blk], out_specs=blk,
        out_shape=jax.ShapeDtypeStruct((k, r, c), BF16),
        compiler_params=_cparams(("parallel", "parallel")),
    )(a, b)


ANY = pl.BlockSpec(memory_space=pl.ANY)


def _chip_peers(x, y):
    return [(1 - x, y, 2 * (1 - x) + y), (x, 1 - y, 2 * x + 1 - y), (1 - x, 1 - y, 2 * (1 - x) + 1 - y)]


def _gather_weights(blob, *, name):
    rows, cols = blob.shape
    half_rows = rows // 2

    def body(b_ref, o_ref, send_sems, recv_sems):
        x, y, c = lax.axis_index("x"), lax.axis_index("y"), lax.axis_index("c")
        me = 2 * x + y
        sibling = (x, y, 1 - c)
        peers = _chip_peers(x, y)

        def half(chip, hc):
            return o_ref.at[chip, pl.ds(hc * half_rows, half_rows), :]

        def copy(k, src, chip, hc, to):
            return pltpu.make_async_remote_copy(src_ref=src, dst_ref=half(chip, hc), send_sem=send_sems.at[k],
                                                recv_sem=recv_sems.at[k], device_id=to, device_id_type=MESH)

        my_half = b_ref.at[pl.ds(c * half_rows, half_rows), :]
        first = [copy(k, my_half, me, c, (px, py, c)) for k, (px, py, _) in enumerate(peers)]
        for cp in first:
            cp.start()
        passed = [copy(3 + k, half(pc, c), pc, c, sibling) for k, (_, _, pc) in enumerate(peers)]
        for k, (px, py, pc) in enumerate(peers):
            copy(k, my_half, pc, c, (px, py, c)).wait_recv()
            passed[k].start()
        for k, (_, _, pc) in enumerate(peers):
            copy(3 + k, half(pc, 1 - c), pc, 1 - c, sibling).wait_recv()
        for cp in first + passed:
            cp.wait_send()

    got = pl.pallas_call(
        body, name=name, in_specs=[ANY], out_specs=ANY,
        out_shape=jax.ShapeDtypeStruct((N_CHIPS, rows, cols), BF16),
        scratch_shapes=[pltpu.SemaphoreType.DMA((6,)), pltpu.SemaphoreType.DMA((6,))],
    )(blob)
    me = 2 * lax.axis_index("x") + lax.axis_index("y")
    return lax.dynamic_update_slice(got, blob[None], (me, 0, 0))


def _swap_halves(g, *, name):
    _, rows, cols = g.shape
    half_rows = rows // 2

    def body(g_ref, o_ref, send_sem, recv_sem):
        x, y, c = lax.axis_index("x"), lax.axis_index("y"), lax.axis_index("c")
        cp = pltpu.make_async_remote_copy(
            src_ref=g_ref.at[:, pl.ds((1 - c) * half_rows, half_rows), :], dst_ref=o_ref,
            send_sem=send_sem, recv_sem=recv_sem, device_id=(x, y, 1 - c), device_id_type=MESH)
        cp.start()
        cp.wait()

    return pl.pallas_call(
        body, name=name, in_specs=[ANY], out_specs=ANY,
        out_shape=jax.ShapeDtypeStruct((N_CHIPS, half_rows, cols), BF16),
        scratch_shapes=[pltpu.SemaphoreType.DMA, pltpu.SemaphoreType.DMA],
    )(g)


def _exchange_chips(p, *, name):
    def body(p_ref, o_ref, send_sems, recv_sems):
        x, y, c = lax.axis_index("x"), lax.axis_index("y"), lax.axis_index("c")
        me = 2 * x + y
        peers = _chip_peers(x, y)
        cps = [pltpu.make_async_remote_copy(src_ref=p_ref.at[pc], dst_ref=o_ref.at[me], send_sem=send_sems.at[k],
                                            recv_sem=recv_sems.at[k], device_id=(px, py, c), device_id_type=MESH)
               for k, (px, py, pc) in enumerate(peers)]
        for cp in cps:
            cp.start()
        for k, (px, py, pc) in enumerate(peers):
            pltpu.make_async_remote_copy(src_ref=p_ref.at[pc], dst_ref=o_ref.at[pc], send_sem=send_sems.at[k],
                                         recv_sem=recv_sems.at[k], device_id=(px, py, c),
                                         device_id_type=MESH).wait_recv()
        for cp in cps:
            cp.wait_send()

    got = pl.pallas_call(
        body, name=name, in_specs=[ANY], out_specs=ANY,
        out_shape=jax.ShapeDtypeStruct(p.shape, BF16),
        scratch_shapes=[pltpu.SemaphoreType.DMA((3,)), pltpu.SemaphoreType.DMA((3,))],
    )(p)
    me = 2 * lax.axis_index("x") + lax.axis_index("y")
    return lax.dynamic_update_slice(got, lax.dynamic_slice_in_dim(p, me, 1, axis=0), (me, 0, 0))


def _join_halves(gh, *, name):
    def body(g_ref, o_ref, send_sem, recv_sem):
        x, y, c = lax.axis_index("x"), lax.axis_index("y"), lax.axis_index("c")
        cp = pltpu.make_async_remote_copy(src_ref=g_ref, dst_ref=o_ref, send_sem=send_sem, recv_sem=recv_sem,
                                          device_id=(x, y, 1 - c), device_id_type=MESH)
        cp.start()
        cp.wait()

    other = pl.pallas_call(
        body, name=name, in_specs=[ANY], out_specs=ANY,
        out_shape=jax.ShapeDtypeStruct(gh.shape, F32),
        scratch_shapes=[pltpu.SemaphoreType.DMA, pltpu.SemaphoreType.DMA],
    )(gh)
    south = lax.axis_index("c") == 0
    return jnp.concatenate([jnp.where(south, gh, other), jnp.where(south, other, gh)], axis=0)


def _gather_small(s, *, name):
    rows = s.shape[0]

    def body(s_ref, o_ref, send_sems, recv_sems, local_sem):
        x, y, c = lax.axis_index("x"), lax.axis_index("y"), lax.axis_index("c")
        me = 4 * x + 2 * y + c
        mine = pltpu.make_async_copy(s_ref, o_ref.at[me], local_sem)
        mine.start()
        peers = []
        for k in range(1, 8):
            peers.append((1 - x if k & 4 else x, 1 - y if k & 2 else y, 1 - c if k & 1 else c))
        cps = [pltpu.make_async_remote_copy(src_ref=s_ref, dst_ref=o_ref.at[me], send_sem=send_sems.at[k],
                                            recv_sem=recv_sems.at[k], device_id=p, device_id_type=MESH)
               for k, p in enumerate(peers)]
        for cp in cps:
            cp.start()
        for k, (px, py, pc) in enumerate(peers):
            pltpu.make_async_remote_copy(src_ref=s_ref, dst_ref=o_ref.at[4 * px + 2 * py + pc],
                                         send_sem=send_sems.at[k], recv_sem=recv_sems.at[k],
                                         device_id=(px, py, pc), device_id_type=MESH).wait_recv()
        for cp in cps:
            cp.wait_send()
        mine.wait()

    return pl.pallas_call(
        body, name=name, in_specs=[ANY], out_specs=ANY,
        out_shape=jax.ShapeDtypeStruct((8, rows, 128), F32),
        scratch_shapes=[pltpu.SemaphoreType.DMA((7,)), pltpu.SemaphoreType.DMA((7,)), pltpu.SemaphoreType.DMA],
    )(s)


IN_SHARD = IN_WIDTH // N_CHIPS
IN_SHARD_PAD = 1536
OUT_ROWS, UP_ROWS, DOWN_ROWS = 512, 1024, 1024


def _pack_in(w_in_s):
    return jnp.pad(w_in_s, ((0, 0), (0, IN_SHARD_PAD - IN_SHARD))).astype(BF16)


def _pack_rest(w_out_s, w_up_s, w_down_s):
    return jnp.concatenate([w_out_s, w_up_s, w_down_s], axis=0).astype(BF16)


def _unpack_rest(blob):
    return (blob[0:OUT_ROWS], blob[OUT_ROWS:OUT_ROWS + UP_ROWS], blob[OUT_ROWS + UP_ROWS:])


def _full_weights(g_in, g_rest):
    w_in = jnp.concatenate([g_in[j, :, :IN_SHARD] for j in range(N_CHIPS)], axis=1)
    w_out = g_rest[:, 0:OUT_ROWS].reshape(N_CHIPS * OUT_ROWS, D_MODEL)
    w_up = g_rest[:, OUT_ROWS:OUT_ROWS + UP_ROWS].transpose(1, 0, 2).reshape(D_MODEL, D_FF)
    w_down = g_rest[:, OUT_ROWS + UP_ROWS:].reshape(D_FF, D_MODEL)
    return w_in, w_out, w_up, w_down


def _split_w_in(w_in):
    z_xbc = w_in[:, 0:2560]
    dt = w_in[:, 2560:2576]
    qkv = w_in[:, 2576:5648]
    f = w_in[:, 5648:5664]
    pad = jnp.zeros((w_in.shape[0], PA_WIDTH - 2592), w_in.dtype)
    return jnp.concatenate([z_xbc, dt, f, pad], axis=1), qkv


def _merge_w_in(d_a, d_qkv):
    return jnp.concatenate([d_a[:, 0:2560], d_a[:, 2560:2576], d_qkv, d_a[:, 2576:2592]], axis=1)


def _local_step(x3, target3, w_in, w_out, w_up, w_down, norm_mix_w, conv_w, conv_b, dt_bias, a_log, d_skip,
                ssd_norm_w, f_bias, norm_mlp_w, norm_final_w):
    bl, t, d = x3.shape
    n = bl * t
    x = x3.reshape(n, d)
    target = target3.reshape(n, d)
    w_a, w_qkv = _split_w_in(w_in)
    wo_s, wo_a = w_out[:SSD_WIDTH], w_out[SSD_WIDTH:]
    nfw = norm_final_w.reshape(1, d)
    dskip_e = jnp.repeat(d_skip, HEAD_DIM, axis=1)
    nb = t // ATT_BLOCK

    h0, rstd0 = _rmsnorm_fwd(x, norm_mix_w, name="norm_mix_fwd")
    r1, r2, r4, kt = min(n, 1024), min(n, 512), min(n, 256), min(n, 512)
    proj_a = _mm(h0, w_a, name="proj_a", tiles=(r2, PA_WIDTH, D_MODEL))
    qkv = _mm(h0, w_qkv, name="proj_qkv", tiles=(r2, QKV_WIDTH, D_MODEL), out_dtype=BF16)
    bias128 = jnp.concatenate([dt_bias, f_bias, jnp.zeros((1, 96), F32)], axis=1)
    alog128 = jnp.concatenate([a_log, jnp.zeros((1, 112), F32)], axis=1)
    dt, sig, acum, ccum, sigf = _prep(proj_a, bias128, alog128, bl, t)
    acum_t = acum[:, 0:16].T
    negc = jnp.pad(-ccum.reshape(bl, t, 8, 2).transpose(0, 2, 3, 1), ((0, 0), (0, 0), (0, 6), (0, 0)))
    xc = _conv_fwd(proj_a, conv_w, conv_b, bl, t)
    y_ssd, y_pre, hprev = _ssd_fwd2(xc, proj_a, dt, acum, acum_t, dskip_e, ssd_norm_w, bl, t)
    y_att, lse = _attn_fwd(qkv, negc, bl, t)
    t1 = _mm(y_ssd, wo_s, name="out_proj_ssd", tiles=(r1, D_MODEL, SSD_WIDTH), res=x)
    h1 = _mm(y_att, wo_a, name="out_proj_att", tiles=(r1, D_MODEL, ATT_WIDTH), res=t1)
    h1n, rstd1 = _rmsnorm_fwd(h1, norm_mlp_w, name="norm_mlp_fwd")
    up = _mm(h1n, w_up, name="mlp_up", tiles=(r2, D_FF, D_MODEL))
    h2 = _mm(up, w_down, name="mlp_down", tiles=(r4, D_MODEL, D_FF), a_act="relu2", res=h1)
    dh2, dh2b, loss, d_nfw = _final(h2, nfw, target)

    dup = _mm(dh2b, w_down, name="mlp_down_bwd_act", tiles=(r4, D_FF, D_MODEL), tb=True, epi_up=up, out_dtype=BF16)
    d_w_down = _mm(up, dh2b, name="mlp_down_bwd_w", tiles=(1024, 1024, kt), ta=True, a_act="relu2")
    dh1n = _mm(dup, w_up, name="mlp_up_bwd_act", tiles=(r2, D_MODEL, D_FF), tb=True)
    d_w_up = _mm(h1n, dup, name="mlp_up_bwd_w", tiles=(1024, 1024, kt), ta=True)
    dh1, dh1b, d_nmlp = _rmsnorm_bwd(dh1n, h1, rstd1, norm_mlp_w, dh2, name="norm_mlp_bwd")
    dys = _mm(dh1b, wo_s, name="out_proj_bwd_ssd", tiles=(r1, SSD_WIDTH, D_MODEL), tb=True)
    do = _mm(dh1b, wo_a, name="out_proj_bwd_att", tiles=(r1, ATT_WIDTH, D_MODEL), tb=True, out_dtype=BF16)
    d_w_out = jnp.concatenate([_mm(y_ssd, dh1b, name="out_proj_bwd_w_ssd", tiles=(1024, 1024, kt), ta=True),
                               _mm(y_att, dh1b, name="out_proj_bwd_w_att", tiles=(1024, 1024, kt), ta=True)], axis=0)
    dq, dk, dv, dcb = _attn_bwd(qkv, do, y_att, lse, negc, bl, t)
    dc = jnp.pad(dcb[:, :, 0:2, :].transpose(0, 3, 1, 2).reshape(n, 16), ((0, 0), (0, 112)))
    df_raw, d_fb = _fpost(dc, sigf, bl, t)
    dxc, dz, ddt_raw, d_snw, d_dsk, d_alog, d_dtb = _ssd_bwd2(dys, xc, proj_a, y_pre, hprev, dt, sig, acum, acum_t,
                                                            a_log, dskip_e, ssd_norm_w, bl, t)
    dxbc, d_conv_w, d_conv_b = _conv_bwd(dxc, proj_a, conv_w, conv_b, bl, t)
    dproj_a = jnp.concatenate([dz, dxbc, ddt_raw.astype(BF16), df_raw.astype(BF16),
                               jnp.zeros((n, PA_WIDTH - 2592), BF16)], axis=1)
    dqkv = jnp.concatenate([dq, dk, dv], axis=1)
    d_w_a = _mm(h0, dproj_a, name="proj_a_bwd_w", tiles=(1024, 896, kt), ta=True)
    d_w_qkv = _mm(h0, dqkv, name="proj_qkv_bwd_w", tiles=(1024, 1024, kt), ta=True)
    t2 = _mm(dproj_a, w_a, name="proj_a_bwd_act", tiles=(r1, D_MODEL, PA_WIDTH), tb=True)
    dh0 = _mm(dqkv, w_qkv, name="proj_qkv_bwd_act", tiles=(r1, D_MODEL, QKV_WIDTH), tb=True, res=t2)
    dx, _, d_nmix = _rmsnorm_bwd(dh0, x, rstd0, norm_mix_w, dh1, name="norm_mix_bwd")

    grads = dict(norm_mix_w=d_nmix, w_in=_merge_w_in(d_w_a, d_w_qkv), conv_w=d_conv_w, conv_b=d_conv_b,
                 dt_bias=d_dtb, a_log=d_alog, d_skip=d_dsk, ssd_norm_w=d_snw, f_bias=d_fb, w_out=d_w_out,
                 norm_mlp_w=d_nmlp, w_up=d_w_up, w_down=d_w_down, norm_final_w=d_nfw)
    return dx.reshape(bl, t, d), loss, grads


SMALL_ORDER = ("norm_mix_w", "conv_w", "conv_b", "dt_bias", "a_log", "d_skip", "ssd_norm_w", "f_bias",
               "norm_mlp_w", "norm_final_w")
SMALL_SIZES = (1024, 4 * CONV_CH, CONV_CH, 16, 16, 16, 1024, 16, 1024, 1024)


def _pack_small(vals, rows):
    flat = jnp.concatenate([v.reshape(-1).astype(F32) for v in vals])
    return jnp.pad(flat, (0, rows * 128 - flat.shape[0])).reshape(rows, 128)


def _unpack_small(packed, sizes):
    flat = packed.reshape(-1)
    out, o = [], 0
    for s in sizes:
        out.append(flat[o:o + s])
        o += s
    return out


def kernel(x, norm_mix_w, w_in, conv_w, conv_b, dt_bias, a_log, d_skip, ssd_norm_w, f_bias, w_out, norm_mlp_w, w_up, w_down, norm_final_w, loss_target, m_norm_mix_w, m_w_in, m_conv_w, m_conv_b, m_dt_bias, m_a_log, m_d_skip, m_ssd_norm_w, m_f_bias, m_w_out, m_norm_mlp_w, m_w_up, m_w_down, m_norm_final_w, v_norm_mix_w, v_w_in, v_conv_w, v_conv_b, v_dt_bias, v_a_log, v_d_skip, v_ssd_norm_w, v_f_bias, v_w_out, v_norm_mlp_w, v_w_up, v_w_down, v_norm_final_w):
    chip = 2 * lax.axis_index("x") + lax.axis_index("y")
    cw = CONV_CH // N_CHIPS

    g_in = _gather_weights(_pack_in(w_in[0]), name="gather_w_in")
    g_rest = _gather_weights(_pack_rest(w_out[0], w_up[0], w_down[0]), name="gather_w_rest")
    w_in_f, w_out_f, w_up_f, w_down_f = _full_weights(g_in, g_rest)
    small_all = _gather_small(_pack_small([conv_w[0]], 16), name="gather_conv_w")
    conv_w_f = jnp.concatenate([small_all[2 * j].reshape(-1)[:4 * cw].reshape(4, cw) for j in range(N_CHIPS)], axis=1)

    dx, loss_part, g = _local_step(x, loss_target, w_in_f, w_out_f, w_up_f, w_down_f, norm_mix_w, conv_w_f,
                                   conv_b, dt_bias, a_log, d_skip, ssd_norm_w, f_bias, norm_mlp_w, norm_final_w)

    c = lax.axis_index("c")

    def reduce_scatter(gb, tag):
        half_rows = gb.shape[1] // 2
        from_sibling = _swap_halves(gb, name="grad_swap_halves_" + tag)
        my_half = lax.dynamic_slice_in_dim(gb, c * half_rows, half_rows, axis=1)
        chip_part = _add_pair(my_half, from_sibling, name="grad_add_sibling_" + tag)
        parts = _exchange_chips(chip_part, name="grad_exchange_chips_" + tag)
        g_half = _sum_leading(parts, name="grad_sum_chips_" + tag)
        return _join_halves(g_half, name="grad_join_halves_" + tag)

    gb_in = jnp.stack([_pack_in(g["w_in"][:, j * IN_SHARD:(j + 1) * IN_SHARD]) for j in range(N_CHIPS)])
    gb_rest = jnp.stack([_pack_rest(g["w_out"][j * OUT_ROWS:(j + 1) * OUT_ROWS],
                                    g["w_up"][:, j * UP_ROWS:(j + 1) * UP_ROWS],
                                    g["w_down"][j * DOWN_ROWS:(j + 1) * DOWN_ROWS]) for j in range(N_CHIPS)])
    g_w_out, g_w_up, g_w_down = _unpack_rest(reduce_scatter(gb_rest, "rest"))
    g_w_in = reduce_scatter(gb_in, "in")[:, :IN_SHARD]

    small_vals = [g[k] for k in SMALL_ORDER] + [loss_part[:, 0:1]]
    small_sum = _sum_leading(_gather_small(_pack_small(small_vals, SMALL_ROWS), name="gather_small_grads"), name="small_sum")
    sg = dict(zip(SMALL_ORDER + ("loss",), _unpack_small(small_sum, SMALL_SIZES + (1,))))
    loss = sg["loss"].reshape(())
    g_conv_full = sg["conv_w"].reshape(4, CONV_CH)
    g_conv = lax.dynamic_slice_in_dim(g_conv_full, chip * cw, cw, axis=1)

    grads = dict(norm_mix_w=sg["norm_mix_w"].reshape(1, -1), w_in=g_w_in[None], conv_w=g_conv[None],
                 conv_b=sg["conv_b"].reshape(1, -1), dt_bias=sg["dt_bias"].reshape(1, -1),
                 a_log=sg["a_log"].reshape(1, -1), d_skip=sg["d_skip"].reshape(1, -1),
                 ssd_norm_w=sg["ssd_norm_w"].reshape(1, -1), f_bias=sg["f_bias"].reshape(1, -1), w_out=g_w_out[None],
                 norm_mlp_w=sg["norm_mlp_w"].reshape(1, -1), w_up=g_w_up[None], w_down=g_w_down[None],
                 norm_final_w=sg["norm_final_w"])
    weights = dict(norm_mix_w=norm_mix_w, w_in=w_in, conv_w=conv_w, conv_b=conv_b, dt_bias=dt_bias, a_log=a_log,
                   d_skip=d_skip, ssd_norm_w=ssd_norm_w, f_bias=f_bias, w_out=w_out, norm_mlp_w=norm_mlp_w,
                   w_up=w_up, w_down=w_down, norm_final_w=norm_final_w)
    ms = dict(norm_mix_w=m_norm_mix_w, w_in=m_w_in, conv_w=m_conv_w, conv_b=m_conv_b, dt_bias=m_dt_bias,
              a_log=m_a_log, d_skip=m_d_skip, ssd_norm_w=m_ssd_norm_w, f_bias=m_f_bias, w_out=m_w_out,
              norm_mlp_w=m_norm_mlp_w, w_up=m_w_up, w_down=m_w_down, norm_final_w=m_norm_final_w)
    vs = dict(norm_mix_w=v_norm_mix_w, w_in=v_w_in, conv_w=v_conv_w, conv_b=v_conv_b, dt_bias=v_dt_bias,
              a_log=v_a_log, d_skip=v_d_skip, ssd_norm_w=v_ssd_norm_w, f_bias=v_f_bias, w_out=v_w_out,
              norm_mlp_w=v_norm_mlp_w, w_up=v_w_up, w_down=v_w_down, norm_final_w=v_norm_final_w)
    names = list(weights)
    big = ("w_in", "w_out", "w_up", "w_down")
    delta, new_m, new_v = {}, {}, {}
    for k in big:
        shp = weights[k].shape
        two_d = lambda a: a.reshape(shp[-2], shp[-1])
        d_, m_, v_ = _adamw(two_d(weights[k]), two_d(grads[k]), two_d(ms[k]), two_d(vs[k]), name="adamw_" + k)
        delta[k], new_m[k], new_v[k] = d_.reshape(shp), m_.reshape(shp), v_.reshape(shp)
    smalls = [k for k in names if k not in big]
    sizes = [math.prod(weights[k].shape) for k in smalls]
    rows = -(-sum(sizes) // 1024) * 8
    packs = [_pack_small([d[k] for k in smalls], rows) for d in (weights, grads, ms, vs)]
    outs = _adamw(*packs, name="adamw_small")
    for o, dst in zip(outs, (delta, new_m, new_v)):
        for k, val in zip(smalls, _unpack_small(o, sizes)):
            dst[k] = val.reshape(weights[k].shape)
    return (loss, dx, *[grads[k] for k in names], *[delta[k] for k in names], *[new_m[k] for k in names],
            *[new_v[k] for k in names])
```

```python
import functools
import math

import jax
import jax.numpy as jnp
from jax import lax
from jax.experimental import pallas as pl
from jax.experimental.pallas import tpu as pltpu

F32 = jnp.float32
BF16 = jnp.bfloat16
HIGHEST = lax.Precision.HIGHEST
MESH = pl.DeviceIdType.MESH

D_MODEL = 1024
SSD_HEADS = 16
HEAD_DIM = 64
SSD_WIDTH = 1024
SSD_STATE = 128
CONV_CH = 1536
CHUNK = 128
ATT_WIDTH = 1024
EPS = 1e-5
IN_WIDTH = 5664
PA_WIDTH = 2688
QKV_WIDTH = 3072
D_FF = 4096
ATT_BLOCK = 256
NEG = -1e30
LOG2E = 1.4426950408889634
VMEM_LIMIT = 48 * 1024 * 1024

ADAM_LR = 0.001
ADAM_B1 = 0.9
ADAM_B2 = 0.999
ADAM_EPS = 1e-08
ADAM_WD = 0.01
ADAM_STEP = 10

N_CHIPS = 4
BLOB_ROWS = 4096
HALF_ROWS = BLOB_ROWS // 2
SMALL_ROWS = 96


def _cparams(sem):
    return pltpu.CompilerParams(dimension_semantics=sem, vmem_limit_bytes=VMEM_LIMIT)


def _pick(n, cands):
    for c in cands:
        if n % c == 0:
            return c
    return n


MM_CHUNK = 512


def _mm(a, b, *, name, tiles, ta=False, tb=False, out_dtype=F32, res=None, a_act=None, epi_up=None):
    if ta:
        K, M = a.shape
    else:
        M, K = a.shape
    if tb:
        N, K2 = b.shape
    else:
        K2, N = b.shape
    assert K == K2, (a.shape, b.shape)
    tm, tn, tk = tiles
    assert M % tm == 0 and N % tn == 0 and K % tk == 0, (name, M, N, K, tiles)
    nk = K // tk
    dn = (((0 if ta else 1,), (1 if tb else 0,)), ((), ()))
    has_res = res is not None
    has_up = epi_up is not None
    cn = _pick(tn, (MM_CHUNK, 384, 256, 128))

    def prologue(av):
        if a_act == "relu2":
            r = jnp.maximum(av.astype(F32), 0.0)
            av = r * r
        return av.astype(BF16)

    def epilogue(out, res_v, up_v):
        if has_res:
            out = out + res_v.astype(F32)
        if has_up:
            out = out * (2.0 * jnp.maximum(up_v.astype(F32), 0.0))
        return out.astype(out_dtype)

    def body(*refs):
        a_ref, b_ref = refs[0], refs[1]
        i = 2
        res_ref = up_ref = None
        if has_res:
            res_ref = refs[i]
            i += 1
        if has_up:
            up_ref = refs[i]
            i += 1
        o_ref = refs[i]
        if nk == 1:
            av = prologue(a_ref[...])
            for c in range(tn // cn):
                cs = slice(c * cn, (c + 1) * cn)
                bv = (b_ref[cs, :] if tb else b_ref[:, cs]).astype(BF16)
                out = lax.dot_general(av, bv, dn, preferred_element_type=F32)
                o_ref[:, cs] = epilogue(out, res_ref[:, cs] if has_res else None, up_ref[:, cs] if has_up else None)
            return
        acc_ref = refs[i + 1]
        k = pl.program_id(2)

        @pl.when(k == 0)
        def _():
            acc_ref[...] = jnp.zeros_like(acc_ref)

        acc_ref[...] += lax.dot_general(prologue(a_ref[...]), b_ref[...].astype(BF16), dn,
                                        preferred_element_type=F32)

        @pl.when(k == nk - 1)
        def _():
            o_ref[...] = epilogue(acc_ref[...], res_ref[...] if has_res else None, up_ref[...] if has_up else None)

    a_spec = pl.BlockSpec((tk, tm), lambda i, j, k: (k, i)) if ta else pl.BlockSpec((tm, tk), lambda i, j, k: (i, k))
    b_spec = pl.BlockSpec((tn, tk), lambda i, j, k: (j, k)) if tb else pl.BlockSpec((tk, tn), lambda i, j, k: (k, j))
    o_spec = pl.BlockSpec((tm, tn), lambda i, j, k: (i, j))
    ins, specs = [a, b], [a_spec, b_spec]
    if has_res:
        ins.append(res)
        specs.append(o_spec)
    if has_up:
        ins.append(epi_up)
        specs.append(o_spec)
    return pl.pallas_call(
        body, name=name, grid=(M // tm, N // tn, nk),
        in_specs=specs, out_specs=o_spec,
        out_shape=jax.ShapeDtypeStruct((M, N), out_dtype),
        scratch_shapes=[] if nk == 1 else [pltpu.VMEM((tm, tn), F32)],
        compiler_params=_cparams(("parallel", "parallel", "arbitrary")),
    )(*ins)


def _rmsnorm_fwd(x, w, *, name):
    n, d = x.shape
    tm = _pick(n, (512, 256, 128))

    def body(x_ref, w_ref, y_ref, r_ref):
        xv = x_ref[...]
        rstd = lax.rsqrt(jnp.mean(xv * xv, axis=1, keepdims=True) + EPS)
        y_ref[...] = (xv * rstd * w_ref[...]).astype(BF16)
        r_ref[...] = rstd

    return pl.pallas_call(
        body, name=name, grid=(n // tm,),
        in_specs=[pl.BlockSpec((tm, d), lambda i: (i, 0)), pl.BlockSpec((1, d), lambda i: (0, 0))],
        out_specs=[pl.BlockSpec((tm, d), lambda i: (i, 0)), pl.BlockSpec((tm, 1), lambda i: (i, 0))],
        out_shape=[jax.ShapeDtypeStruct((n, d), BF16), jax.ShapeDtypeStruct((n, 1), F32)],
        compiler_params=_cparams(("parallel",)),
    )(x, w)


def _rmsnorm_bwd(dyn, x, rstd, w, dres, *, name):
    n, d = x.shape
    tm = _pick(n, (512, 256, 128))

    def body(g_ref, x_ref, r_ref, w_ref, d_ref, dx_ref, dxb_ref, dw_ref):
        @pl.when(pl.program_id(0) == 0)
        def _():
            dw_ref[...] = jnp.zeros_like(dw_ref)

        g = g_ref[...]
        r = r_ref[...]
        xhat = x_ref[...] * r
        gw = g * w_ref[...]
        dx = d_ref[...] + r * (gw - xhat * jnp.mean(gw * xhat, axis=1, keepdims=True))
        dx_ref[...] = dx
        dxb_ref[...] = dx.astype(BF16)
        dw_ref[...] += jnp.sum(g * xhat, axis=0, keepdims=True)

    row = pl.BlockSpec((tm, d), lambda i: (i, 0))
    vec = pl.BlockSpec((1, d), lambda i: (0, 0))
    return pl.pallas_call(
        body, name=name, grid=(n // tm,),
        in_specs=[row, row, pl.BlockSpec((tm, 1), lambda i: (i, 0)), vec, row],
        out_specs=[row, row, vec],
        out_shape=[jax.ShapeDtypeStruct((n, d), F32), jax.ShapeDtypeStruct((n, d), BF16),
                   jax.ShapeDtypeStruct((1, d), F32)],
        compiler_params=_cparams(("arbitrary",)),
    )(dyn, x, rstd, w, dres)


def _final(h2, w, target):
    n, d = h2.shape
    tm = _pick(n, (512, 256, 128))

    def body(h_ref, w_ref, t_ref, dh_ref, dhb_ref, loss_ref, dw_ref):
        @pl.when(pl.program_id(0) == 0)
        def _():
            loss_ref[...] = jnp.zeros_like(loss_ref)
            dw_ref[...] = jnp.zeros_like(dw_ref)

        hv = h_ref[...]
        wv = w_ref[...]
        rstd = lax.rsqrt(jnp.mean(hv * hv, axis=1, keepdims=True) + EPS)
        xhat = hv * rstd
        err = xhat * wv - t_ref[...]
        part = jnp.sum(jnp.mean(err * err, axis=1, keepdims=True), axis=0, keepdims=True)
        loss_ref[...] += 0.5 * part
        dy = err * (1.0 / d)
        gw = dy * wv
        dh = rstd * (gw - xhat * jnp.mean(gw * xhat, axis=1, keepdims=True))
        dh_ref[...] = dh
        dhb_ref[...] = dh.astype(BF16)
        dw_ref[...] += jnp.sum(dy * xhat, axis=0, keepdims=True)

    row = pl.BlockSpec((tm, d), lambda i: (i, 0))
    vec = pl.BlockSpec((1, d), lambda i: (0, 0))
    return pl.pallas_call(
        body, name="final_norm_loss", grid=(n // tm,),
        in_specs=[row, vec, row],
        out_specs=[row, row, pl.BlockSpec((1, 128), lambda i: (0, 0)), vec],
        out_shape=[jax.ShapeDtypeStruct((n, d), F32), jax.ShapeDtypeStruct((n, d), BF16),
                   jax.ShapeDtypeStruct((1, 128), F32), jax.ShapeDtypeStruct((1, d), F32)],
        compiler_params=_cparams(("arbitrary",)),
    )(h2, w, target)


def _softplus(x):
    return jnp.maximum(x, 0.0) + jnp.log(1.0 + jnp.exp(-jnp.abs(x)))


def _prep(proj_a, bias128, alog128, bl, t):
    n = bl * t
    nch = t // CHUNK
    col0 = (SSD_WIDTH + CONV_CH) // 128

    def body(p_ref, b_ref, al_ref, dt_ref, sg_ref, ac_ref, c_ref, sf_ref, carry):
        @pl.when(pl.program_id(1) == 0)
        def _():
            carry[...] = jnp.zeros_like(carry)

        xv = p_ref[...] + b_ref[...]
        sp = _softplus(xv)
        a = -jnp.exp(al_ref[...]) * sp
        logf = -_softplus(-xv)
        row = lax.broadcasted_iota(jnp.int32, (CHUNK, CHUNK), 0)
        col = lax.broadcasted_iota(jnp.int32, (CHUNK, CHUNK), 1)
        tril = (row >= col).astype(F32)
        acum = jnp.dot(tril, a, precision=HIGHEST, preferred_element_type=F32)
        c = jnp.dot(tril, logf, precision=HIGHEST, preferred_element_type=F32) + carry[...]
        carry[...] = c[CHUNK - 1:CHUNK, :]
        head_lanes = lax.broadcasted_iota(jnp.int32, (1, 128), 1) < 16
        dt_ref[...] = jnp.where(head_lanes, sp, 0.0)
        sg_ref[...] = jax.nn.sigmoid(xv)[:, 0:16]
        ac_ref[...] = jnp.where(head_lanes, acum, 0.0)
        c_ref[...] = c[:, 16:32]
        sf_ref[...] = jax.nn.sigmoid(-xv)[:, 16:32]

    o16 = pl.BlockSpec((CHUNK, 16), lambda b, c: (b * nch + c, 0))
    o128 = pl.BlockSpec((CHUNK, 128), lambda b, c: (b * nch + c, 0))
    v128 = pl.BlockSpec((1, 128), lambda b, c: (0, 0))
    w16 = jax.ShapeDtypeStruct((n, 16), F32)
    w128 = jax.ShapeDtypeStruct((n, 128), F32)
    return pl.pallas_call(
        body, name="head_scalars", grid=(bl, nch),
        in_specs=[pl.BlockSpec((CHUNK, 128), lambda b, c: (b * nch + c, col0)), v128, v128],
        out_specs=[o128, o16, o128, o16, o16],
        out_shape=[w128, w16, w128, w16, w16],
        scratch_shapes=[pltpu.VMEM((1, 128), F32)],
        compiler_params=_cparams(("parallel", "arbitrary")),
    )(proj_a, bias128, alog128)


def _fpost(dc, sigf, bl, t):
    n = bl * t
    nch = t // CHUNK

    def body(dc_ref, sf_ref, df_ref, db_ref, carry):
        @pl.when(pl.program_id(1) == 0)
        def _():
            carry[...] = jnp.zeros_like(carry)

        @pl.when((pl.program_id(0) == 0) & (pl.program_id(1) == 0))
        def _():
            db_ref[...] = jnp.zeros_like(db_ref)

        row = lax.broadcasted_iota(jnp.int32, (CHUNK, CHUNK), 0)
        col = lax.broadcasted_iota(jnp.int32, (CHUNK, CHUNK), 1)
        triu = (row <= col).astype(F32)
        dlf = jnp.dot(triu, dc_ref[...], precision=HIGHEST, preferred_element_type=F32) + carry[...]
        carry[...] = dlf[0:1, :]
        df = dlf[:, 0:16] * sf_ref[...]
        df_ref[...] = df
        db_ref[...] += jnp.sum(df, axis=0, keepdims=True)

    rev = lambda b, c: (b * nch + nch - 1 - c, 0)
    blk = pl.BlockSpec((CHUNK, 16), rev)
    return pl.pallas_call(
        body, name="forget_gate_bwd", grid=(bl, nch),
        in_specs=[pl.BlockSpec((CHUNK, 128), rev), blk],
        out_specs=[blk, pl.BlockSpec((1, 16), lambda b, c: (0, 0))],
        out_shape=[jax.ShapeDtypeStruct((n, 16), F32), jax.ShapeDtypeStruct((1, 16), F32)],
        scratch_shapes=[pltpu.VMEM((1, 128), F32)],
        compiler_params=_cparams(("arbitrary", "arbitrary")),
    )(dc, sigf)


CONV_TILE = 256
CONV_ROWS = 256


def _conv_taps(u_ref, i, w, bias):
    r0 = pl.multiple_of(i * CONV_ROWS, CONV_ROWS)
    cur = u_ref[pl.ds(r0, CONV_ROWS), :]
    p0 = pl.multiple_of(jnp.maximum(r0 - 8, 0), 8)
    prev = jnp.where(i > 0, u_ref[pl.ds(p0, 8), :], 0.0)
    cat = jnp.concatenate([prev, cur], axis=0)
    pre = bias + w[3:4, :] * cur
    taps = [cur]
    for s in (1, 2, 3):
        sh = pltpu.roll(cat, s, 0)[8:, :]
        taps.append(sh)
        pre = pre + w[3 - s:4 - s, :] * sh
    return r0, pre, taps


def _conv_fwd(proj_a, conv_w, conv_b, bl, t):
    n = bl * t
    nct = CONV_CH // CONV_TILE
    c0 = SSD_WIDTH // CONV_TILE

    def body(u_ref, w_ref, b_ref, o_ref):
        w = w_ref[...]
        bias = b_ref[...]

        def chunk(i, carry):
            r0, pre, _ = _conv_taps(u_ref, i, w, bias)
            o_ref[pl.ds(r0, CONV_ROWS), :] = pre * jax.nn.sigmoid(pre)
            return carry

        lax.fori_loop(0, t // CONV_ROWS, chunk, 0)

    return pl.pallas_call(
        body, name="conv_silu_fwd", grid=(bl, nct),
        in_specs=[pl.BlockSpec((t, CONV_TILE), lambda b, c: (b, c0 + c)),
                  pl.BlockSpec((4, CONV_TILE), lambda b, c: (0, c)),
                  pl.BlockSpec((1, CONV_TILE), lambda b, c: (0, c))],
        out_specs=pl.BlockSpec((t, CONV_TILE), lambda b, c: (b, c)),
        out_shape=jax.ShapeDtypeStruct((n, CONV_CH), F32),
        compiler_params=_cparams(("parallel", "parallel")),
    )(proj_a, conv_w, conv_b)


def _conv_bwd(dxc, proj_a, conv_w, conv_b, bl, t):
    n = bl * t
    nct = CONV_CH // CONV_TILE
    c0 = SSD_WIDTH // CONV_TILE
    nrc = t // CONV_ROWS

    def body(g_ref, u_ref, w_ref, b_ref, du_ref, dw_ref, db_ref, dp_scr):
        @pl.when(pl.program_id(1) == 0)
        def _():
            dw_ref[...] = jnp.zeros_like(dw_ref)
            db_ref[...] = jnp.zeros_like(db_ref)

        w = w_ref[...]
        bias = b_ref[...]
        dp_scr[pl.ds(t, 8), :] = jnp.zeros((8, CONV_TILE), F32)

        def chunk1(i, carry):
            dw0, dw1, dw2, dw3, db = carry
            r0, pre, taps = _conv_taps(u_ref, i, w, bias)
            sg = jax.nn.sigmoid(pre)
            dpre = g_ref[pl.ds(r0, CONV_ROWS), :] * (sg * (1.0 + pre * (1.0 - sg)))
            dp_scr[pl.ds(r0, CONV_ROWS), :] = dpre
            dw3 = dw3 + jnp.sum(dpre * taps[0], axis=0, keepdims=True)
            dw2 = dw2 + jnp.sum(dpre * taps[1], axis=0, keepdims=True)
            dw1 = dw1 + jnp.sum(dpre * taps[2], axis=0, keepdims=True)
            dw0 = dw0 + jnp.sum(dpre * taps[3], axis=0, keepdims=True)
            db = db + jnp.sum(dpre, axis=0, keepdims=True)
            return dw0, dw1, dw2, dw3, db

        z = jnp.zeros((1, CONV_TILE), F32)
        dw0, dw1, dw2, dw3, db = lax.fori_loop(0, nrc, chunk1, (z, z, z, z, z))
        dw_ref[...] += jnp.concatenate([dw0, dw1, dw2, dw3], axis=0)
        db_ref[...] += db

        def chunk2(i, carry):
            r0 = pl.multiple_of(i * CONV_ROWS, CONV_ROWS)
            cat = dp_scr[pl.ds(r0, CONV_ROWS + 8), :]
            du = w[3:4, :] * cat[:CONV_ROWS, :]
            for s in (1, 2, 3):
                du = du + w[3 - s:4 - s, :] * pltpu.roll(cat, CONV_ROWS + 8 - s, 0)[:CONV_ROWS, :]
            du_ref[pl.ds(r0, CONV_ROWS), :] = du.astype(BF16)
            return carry

        lax.fori_loop(0, nrc, chunk2, 0)

    return pl.pallas_call(
        body, name="conv_silu_bwd", grid=(nct, bl),
        in_specs=[pl.BlockSpec((t, CONV_TILE), lambda c, b: (b, c)),
                  pl.BlockSpec((t, CONV_TILE), lambda c, b: (b, c0 + c)),
                  pl.BlockSpec((4, CONV_TILE), lambda c, b: (0, c)),
                  pl.BlockSpec((1, CONV_TILE), lambda c, b: (0, c))],
        out_specs=[pl.BlockSpec((t, CONV_TILE), lambda c, b: (b, c)),
                   pl.BlockSpec((4, CONV_TILE), lambda c, b: (0, c)),
                   pl.BlockSpec((1, CONV_TILE), lambda c, b: (0, c))],
        out_shape=[jax.ShapeDtypeStruct((n, CONV_CH), BF16), jax.ShapeDtypeStruct((4, CONV_CH), F32),
                   jax.ShapeDtypeStruct((1, CONV_CH), F32)],
        scratch_shapes=[pltpu.VMEM((t + 8, CONV_TILE), F32)],
        compiler_params=_cparams(("parallel", "arbitrary")),
    )(dxc, proj_a, conv_w, conv_b)


NT_DIMS = (((1,), (1,)), ((), ()))
TN_DIMS = (((0,), (0,)), ((), ()))


def _dot(a, b, dims=None):
    if dims is None:
        return jnp.dot(a, b, preferred_element_type=F32)
    return lax.dot_general(a, b, dims, preferred_element_type=F32)


def _ssd_fwd(xc, proj_a, dt, acum, acum_t, dskip_e, norm_w, bl, t):
    n = bl * t
    nch = t // CHUNK
    L = CHUNK

    def body(xc_ref, z_ref, dt_ref, ac_ref, act_ref, dsk_ref, nw_ref, ys_ref, yp_ref, hp_ref, h_scr, y_scr):
        @pl.when(pl.program_id(1) == 0)
        def _():
            h_scr[...] = jnp.zeros_like(h_scr)

        row = lax.broadcasted_iota(jnp.int32, (L, L), 0)
        col = lax.broadcasted_iota(jnp.int32, (L, L), 1)
        causal = row >= col
        dt_all = dt_ref[...]
        ac_all = ac_ref[...]
        act_all = act_ref[...]
        for g in range(2):
            bg = xc_ref[:, SSD_WIDTH + g * 128:SSD_WIDTH + (g + 1) * 128].astype(BF16)
            cg = xc_ref[:, SSD_WIDTH + 256 + g * 128:SSD_WIDTH + 256 + (g + 1) * 128].astype(BF16)
            gmat = _dot(cg, bg, NT_DIMS)
            for r in range(8):
                h = g * 8 + r
                sl = slice(h * HEAD_DIM, (h + 1) * HEAD_DIM)
                xs = xc_ref[:, sl]
                xdt = xs * dt_all[:, h:h + 1]
                ac = ac_all[:, h:h + 1]
                ar = act_all[h:h + 1, :]
                ldec = jnp.exp(jnp.where(causal, ac - ar, NEG))
                m = (gmat * ldec).astype(BF16)
                hp = h_scr[h]
                hp_ref[h] = hp
                yd = _dot(m, xdt.astype(BF16))
                yo = _dot(cg, hp.astype(BF16), NT_DIMS) * jnp.exp(ac)
                y_scr[:, sl] = yd + yo + dsk_ref[:, sl] * xs
                alast = ac_all[L - 1:L, h:h + 1]
                xd = (xdt * jnp.exp(alast - ac)).astype(BF16)
                h_scr[h] = jnp.exp(alast) * hp + _dot(xd, bg, TN_DIMS)
        y = y_scr[...]
        yp_ref[...] = y
        zv = z_ref[...]
        yg = y * (zv * jax.nn.sigmoid(zv))
        for g in range(2):
            gs = slice(g * 512, (g + 1) * 512)
            grp = yg[:, gs]
            rstd = lax.rsqrt(jnp.mean(grp * grp, axis=1, keepdims=True) + EPS)
            ys_ref[:, gs] = (grp * rstd * nw_ref[:, gs]).astype(BF16)

    rb = lambda b, c: (b * nch + c, 0)
    v1k = pl.BlockSpec((1, SSD_WIDTH), lambda b, c: (0, 0))
    return pl.pallas_call(
        body, name="ssd_fwd", grid=(bl, nch),
        in_specs=[pl.BlockSpec((L, CONV_CH), rb), pl.BlockSpec((L, SSD_WIDTH), rb),
                  pl.BlockSpec((L, 16), rb), pl.BlockSpec((L, 16), rb),
                  pl.BlockSpec((16, L), lambda b, c: (0, b * nch + c)), v1k, v1k],
        out_specs=[pl.BlockSpec((L, SSD_WIDTH), rb), pl.BlockSpec((L, SSD_WIDTH), rb),
                   pl.BlockSpec((None, 16, HEAD_DIM, SSD_STATE), lambda b, c: (b * nch + c, 0, 0, 0))],
        out_shape=[jax.ShapeDtypeStruct((n, SSD_WIDTH), BF16), jax.ShapeDtypeStruct((n, SSD_WIDTH), F32),
                   jax.ShapeDtypeStruct((bl * nch, 16, HEAD_DIM, SSD_STATE), F32)],
        scratch_shapes=[pltpu.VMEM((16, HEAD_DIM, SSD_STATE), F32), pltpu.VMEM((L, SSD_WIDTH), F32)],
        compiler_params=_cparams(("parallel", "arbitrary")),
    )(xc, proj_a, dt, acum, acum_t, dskip_e, norm_w)


def _ssd_bwd(dys, xc, proj_a, ypre, hprev, dt, sig, acum, acum_t, a_log, dskip_e, norm_w, bl, t):
    n = bl * t
    nch = t // CHUNK
    L = CHUNK

    def body(dys_ref, xc_ref, z_ref, yp_ref, hp_ref, dt_ref, sg_ref, ac_ref, act_ref, al_ref, dsk_ref, nw_ref,
             dxc_ref, dz_ref, ddt_ref, dnw_ref, dsk16_ref, da16_ref, db16_ref, dh_scr, dy_scr):
        first = (pl.program_id(0) == 0) & (pl.program_id(1) == 0)

        @pl.when(first)
        def _():
            dnw_ref[...] = jnp.zeros_like(dnw_ref)
            dsk16_ref[...] = jnp.zeros_like(dsk16_ref)
            da16_ref[...] = jnp.zeros_like(da16_ref)
            db16_ref[...] = jnp.zeros_like(db16_ref)

        @pl.when(pl.program_id(1) == 0)
        def _():
            dh_scr[...] = jnp.zeros_like(dh_scr)

        y = yp_ref[...]
        zv = z_ref[...]
        sz = jax.nn.sigmoid(zv)
        gate = zv * sz
        yg = y * gate
        dout = dys_ref[...]
        nw = nw_ref[...]
        for g in range(2):
            gs = slice(g * 512, (g + 1) * 512)
            grp = yg[:, gs]
            rstd = lax.rsqrt(jnp.mean(grp * grp, axis=1, keepdims=True) + EPS)
            ghat = grp * rstd
            dnw_ref[:, gs] += jnp.sum(dout[:, gs] * ghat, axis=0, keepdims=True)
            gw = dout[:, gs] * nw[:, gs]
            dyg = rstd * (gw - ghat * jnp.mean(gw * ghat, axis=1, keepdims=True))
            dy_scr[:, gs] = dyg * gate[:, gs]
            dz_ref[:, gs] = (dyg * y[:, gs] * (sz[:, gs] * (1.0 + zv[:, gs] * (1.0 - sz[:, gs])))).astype(BF16)

        row = lax.broadcasted_iota(jnp.int32, (L, L), 0)
        col = lax.broadcasted_iota(jnp.int32, (L, L), 1)
        causal = row >= col
        lane16 = lax.broadcasted_iota(jnp.int32, (1, 16), 1)
        lane128 = lax.broadcasted_iota(jnp.int32, (1, L), 1)
        last_row = lax.broadcasted_iota(jnp.int32, (L, 1), 0) == (L - 1)
        dt_all = dt_ref[...]
        ac_all = ac_ref[...]
        act_all = act_ref[...]
        dac_col = jnp.zeros((L, L), F32)
        dac_row = jnp.zeros((L, L), F32)
        ddt_x = jnp.zeros((L, L), F32)
        dsk16 = jnp.zeros((1, 16), F32)
        rows16 = lax.broadcasted_iota(jnp.int32, (L, 1), 0)
        for g in range(2):
            bsl = slice(SSD_WIDTH + g * 128, SSD_WIDTH + (g + 1) * 128)
            csl = slice(SSD_WIDTH + 256 + g * 128, SSD_WIDTH + 256 + (g + 1) * 128)
            bg = xc_ref[:, bsl].astype(BF16)
            cg = xc_ref[:, csl].astype(BF16)
            gmat = _dot(cg, bg, NT_DIMS)
            dg_sum = jnp.zeros((L, L), F32)
            dc_acc = jnp.zeros((L, SSD_STATE), F32)
            db_acc = jnp.zeros((L, SSD_STATE), F32)
            for r in range(8):
                h = g * 8 + r
                sl = slice(h * HEAD_DIM, (h + 1) * HEAD_DIM)
                onehot = lane16 == h
                onehot_w = lane128 == h
                xs = xc_ref[:, sl]
                dth = dt_all[:, h:h + 1]
                xdt = xs * dth
                xb = xdt.astype(BF16)
                ac = ac_all[:, h:h + 1]
                ar = act_all[h:h + 1, :]
                alast = ac_all[L - 1:L, h:h + 1]
                ldec = jnp.exp(jnp.where(causal, ac - ar, NEG))
                mf = gmat * ldec
                e_in = jnp.exp(ac)
                dec = jnp.exp(alast - ac)
                elast = jnp.exp(alast)
                hp = hp_ref[h]
                hpb = hp.astype(BF16)
                dyh = dy_scr[:, sl]
                dyb = dyh.astype(BF16)
                dsk16 = dsk16 + jnp.where(onehot, jnp.sum(jnp.sum(dyh * xs, axis=1, keepdims=True), axis=0, keepdims=True), 0.0)
                dm = _dot(dyb, xb, NT_DIMS)
                dx = _dot(mf.astype(BF16), dyb, TN_DIMS)
                dg_sum = dg_sum + dm * ldec
                wmat = dm * mf
                dac_h = jnp.sum(wmat, axis=1, keepdims=True)
                dac_row = dac_row + jnp.where(rows16 == h, -jnp.sum(wmat, axis=0, keepdims=True), 0.0)
                ch = _dot(cg, hpb, NT_DIMS)
                dye = dyh * e_in
                dyeb = dye.astype(BF16)
                dc_acc = dc_acc + _dot(dyeb, hpb)
                dhp = _dot(dyeb, cg, TN_DIMS)
                dac_h = dac_h + jnp.sum(dye * ch, axis=1, keepdims=True)
                ds = dh_scr[h]
                dsb = ds.astype(BF16)
                dxd = _dot(bg, dsb, NT_DIMS)
                db_acc = db_acc + _dot((xdt * dec).astype(BF16), dsb)
                dx = dx + dxd * dec
                ddec = jnp.sum(dxd * xdt, axis=1, keepdims=True) * dec
                extra = (jnp.sum(ddec, axis=0, keepdims=True)
                         + elast * jnp.sum(jnp.sum(hp * ds, axis=1, keepdims=True), axis=0, keepdims=True))
                dac_h = dac_h - ddec + jnp.where(last_row, extra, 0.0)
                dh_scr[h] = elast * ds + dhp
                dac_col = dac_col + jnp.where(onehot_w, dac_h, 0.0)
                ddt_x = ddt_x + jnp.where(onehot_w, jnp.sum(dx * xs, axis=1, keepdims=True), 0.0)
                dxc_ref[:, sl] = dx * dth + dsk_ref[:, sl] * dyh
            dgb = dg_sum.astype(BF16)
            dxc_ref[:, csl] = dc_acc + _dot(dgb, bg)
            dxc_ref[:, bsl] = db_acc + _dot(dgb, cg, TN_DIMS)
        dac = dac_col + jnp.transpose(dac_row)
        triu = (row <= col).astype(F32)
        da = jnp.dot(triu, dac, precision=HIGHEST, preferred_element_type=F32)[:, 0:16]
        a_row = -jnp.exp(al_ref[...])
        ddt = (ddt_x[:, 0:16] + da * a_row) * sg_ref[...]
        ddt_ref[...] = ddt
        dsk16_ref[...] += dsk16
        da16_ref[...] += jnp.sum(da * dt_all, axis=0, keepdims=True) * a_row
        db16_ref[...] += jnp.sum(ddt, axis=0, keepdims=True)

    rb = lambda b, c: (b * nch + nch - 1 - c, 0)
    v1k = pl.BlockSpec((1, SSD_WIDTH), lambda b, c: (0, 0))
    v16 = pl.BlockSpec((1, 16), lambda b, c: (0, 0))
    wide = pl.BlockSpec((L, SSD_WIDTH), rb)
    s16 = pl.BlockSpec((L, 16), rb)
    return pl.pallas_call(
        body, name="ssd_bwd", grid=(bl, nch),
        in_specs=[wide, pl.BlockSpec((L, CONV_CH), rb), wide, wide,
                  pl.BlockSpec((None, 16, HEAD_DIM, SSD_STATE), lambda b, c: (b * nch + nch - 1 - c, 0, 0, 0)),
                  s16, s16, s16, pl.BlockSpec((16, L), lambda b, c: (0, b * nch + nch - 1 - c)), v16, v1k, v1k],
        out_specs=[pl.BlockSpec((L, CONV_CH), rb), wide, s16, v1k, v16, v16, v16],
        out_shape=[jax.ShapeDtypeStruct((n, CONV_CH), F32), jax.ShapeDtypeStruct((n, SSD_WIDTH), BF16),
                   jax.ShapeDtypeStruct((n, 16), F32), jax.ShapeDtypeStruct((1, SSD_WIDTH), F32),
                   jax.ShapeDtypeStruct((1, 16), F32), jax.ShapeDtypeStruct((1, 16), F32),
                   jax.ShapeDtypeStruct((1, 16), F32)],
        scratch_shapes=[pltpu.VMEM((16, HEAD_DIM, SSD_STATE), F32), pltpu.VMEM((L, SSD_WIDTH), F32)],
        compiler_params=_cparams(("arbitrary", "arbitrary")),
    )(dys, xc, proj_a, ypre, hprev, dt, sig, acum, acum_t, a_log, dskip_e, norm_w)


def _head_expander():
    r = lax.broadcasted_iota(jnp.int32, (128, SSD_WIDTH), 0)
    c = lax.broadcasted_iota(jnp.int32, (128, SSD_WIDTH), 1)
    return (c // HEAD_DIM == r).astype(F32)


def _spread(v128, expander):
    return jnp.dot(v128, expander, precision=HIGHEST, preferred_element_type=F32)


def _head_sums(v1024, expander):
    return lax.dot_general(v1024, expander, NT_DIMS, precision=HIGHEST, preferred_element_type=F32)


def _ssd_fwd2(xc, proj_a, dt, acum, acum_t, dskip_e, norm_w, bl, t):
    n = bl * t
    nch = t // CHUNK
    L = CHUNK

    def body(xc_ref, z_ref, dt_ref, ac_ref, act_ref, dsk_ref, nw_ref, ys_ref, yp_ref, hp_ref, h_scr, y_scr, x_scr):
        @pl.when(pl.program_id(1) == 0)
        def _():
            h_scr[...] = jnp.zeros_like(h_scr)

        row = lax.broadcasted_iota(jnp.int32, (L, L), 0)
        col = lax.broadcasted_iota(jnp.int32, (L, L), 1)
        causal = row >= col
        expander = _head_expander()
        ac_all = ac_ref[...]
        act_all = act_ref[...]
        ac_e = _spread(ac_all, expander)
        e_in = jnp.exp(ac_e)
        dec = jnp.exp(ac_e[L - 1:L, :] - ac_e)
        xs_all = xc_ref[:, 0:SSD_WIDTH]
        x_all = xs_all * _spread(dt_ref[...], expander)
        x_scr[...] = x_all.astype(BF16)
        hp_all = h_scr[...]
        hp_ref[...] = hp_all
        for g in range(2):
            gs = slice(g * 512, (g + 1) * 512)
            bg = xc_ref[:, SSD_WIDTH + g * 128:SSD_WIDTH + (g + 1) * 128].astype(BF16)
            cg = xc_ref[:, SSD_WIDTH + 256 + g * 128:SSD_WIDTH + 256 + (g + 1) * 128].astype(BF16)
            gmat = _dot(cg, bg, NT_DIMS)
            y_scr[:, gs] = (_dot(cg, hp_all[gs, :].astype(BF16), NT_DIMS) * e_in[:, gs]
                            + dsk_ref[:, gs] * xs_all[:, gs])
            s_new = _dot((x_all[:, gs] * dec[:, gs]).astype(BF16), bg, TN_DIMS)
            for r in range(8):
                h = g * 8 + r
                sl = slice(h * HEAD_DIM, (h + 1) * HEAD_DIM)
                ldec = jnp.exp(jnp.where(causal, ac_all[:, h:h + 1] - act_all[h:h + 1, :], NEG))
                y_scr[:, sl] += _dot((gmat * ldec).astype(BF16), x_scr[:, sl])
                elast = jnp.exp(ac_all[L - 1:L, h:h + 1])
                h_scr[sl, :] = elast * hp_all[sl, :] + s_new[r * HEAD_DIM:(r + 1) * HEAD_DIM, :]
        y = y_scr[...]
        yp_ref[...] = y
        zv = z_ref[...]
        yg = y * (zv * jax.nn.sigmoid(zv))
        for g in range(2):
            gs = slice(g * 512, (g + 1) * 512)
            grp = yg[:, gs]
            rstd = lax.rsqrt(jnp.mean(grp * grp, axis=1, keepdims=True) + EPS)
            ys_ref[:, gs] = (grp * rstd * nw_ref[:, gs]).astype(BF16)

    rb = lambda b, c: (b * nch + c, 0)
    v1k = pl.BlockSpec((1, SSD_WIDTH), lambda b, c: (0, 0))
    return pl.pallas_call(
        body, name="ssd_fwd", grid=(bl, nch),
        in_specs=[pl.BlockSpec((L, CONV_CH), rb), pl.BlockSpec((L, SSD_WIDTH), rb),
                  pl.BlockSpec((L, 128), rb), pl.BlockSpec((L, 128), rb),
                  pl.BlockSpec((16, L), lambda b, c: (0, b * nch + c)), v1k, v1k],
        out_specs=[pl.BlockSpec((L, SSD_WIDTH), rb), pl.BlockSpec((L, SSD_WIDTH), rb),
                   pl.BlockSpec((None, SSD_WIDTH, SSD_STATE), lambda b, c: (b * nch + c, 0, 0))],
        out_shape=[jax.ShapeDtypeStruct((n, SSD_WIDTH), BF16), jax.ShapeDtypeStruct((n, SSD_WIDTH), F32),
                   jax.ShapeDtypeStruct((bl * nch, SSD_WIDTH, SSD_STATE), F32)],
        scratch_shapes=[pltpu.VMEM((SSD_WIDTH, SSD_STATE), F32), pltpu.VMEM((L, SSD_WIDTH), F32),
                        pltpu.VMEM((L, SSD_WIDTH), BF16)],
        compiler_params=_cparams(("parallel", "arbitrary")),
    )(xc, proj_a, dt, acum, acum_t, dskip_e, norm_w)


def _ssd_bwd2(dys, xc, proj_a, ypre, hprev, dt, sig, acum, acum_t, a_log, dskip_e, norm_w, bl, t):
    n = bl * t
    nch = t // CHUNK
    L = CHUNK

    def body(dys_ref, xc_ref, z_ref, yp_ref, hp_ref, dt_ref, sg_ref, ac_ref, act_ref, al_ref, dsk_ref, nw_ref,
             dxc_ref, dz_ref, ddt_ref, dnw_ref, dsk16_ref, da16_ref, db16_ref,
             dh_scr, dy_scr, x_scr, dx_scr, red_scr):
        first = (pl.program_id(0) == 0) & (pl.program_id(1) == 0)

        @pl.when(first)
        def _():
            dnw_ref[...] = jnp.zeros_like(dnw_ref)
            dsk16_ref[...] = jnp.zeros_like(dsk16_ref)
            da16_ref[...] = jnp.zeros_like(da16_ref)
            db16_ref[...] = jnp.zeros_like(db16_ref)

        @pl.when(pl.program_id(1) == 0)
        def _():
            dh_scr[...] = jnp.zeros_like(dh_scr)

        y = yp_ref[...]
        zv = z_ref[...]
        sz = jax.nn.sigmoid(zv)
        gate = zv * sz
        yg = y * gate
        dout = dys_ref[...]
        nw = nw_ref[...]
        for g in range(2):
            gs = slice(g * 512, (g + 1) * 512)
            grp = yg[:, gs]
            rstd = lax.rsqrt(jnp.mean(grp * grp, axis=1, keepdims=True) + EPS)
            ghat = grp * rstd
            dnw_ref[:, gs] += jnp.sum(dout[:, gs] * ghat, axis=0, keepdims=True)
            gw = dout[:, gs] * nw[:, gs]
            dyg = rstd * (gw - ghat * jnp.mean(gw * ghat, axis=1, keepdims=True))
            dy_scr[:, gs] = dyg * gate[:, gs]
            dz_ref[:, gs] = (dyg * y[:, gs] * (sz[:, gs] * (1.0 + zv[:, gs] * (1.0 - sz[:, gs])))).astype(BF16)

        row = lax.broadcasted_iota(jnp.int32, (L, L), 0)
        col = lax.broadcasted_iota(jnp.int32, (L, L), 1)
        causal = row >= col
        lane128 = lax.broadcasted_iota(jnp.int32, (1, L), 1)
        rows128 = lax.broadcasted_iota(jnp.int32, (L, 1), 0)
        last_row = rows128 == (L - 1)
        expander = _head_expander()
        ac_all = ac_ref[...]
        act_all = act_ref[...]
        dt_all = dt_ref[...]
        dt_e = _spread(dt_all, expander)
        ac_e = _spread(ac_all, expander)
        e_in = jnp.exp(ac_e)
        dec = jnp.exp(ac_e[L - 1:L, :] - ac_e)
        xs_all = xc_ref[:, 0:SSD_WIDTH]
        x_all = xs_all * dt_e
        x_scr[...] = x_all.astype(BF16)
        dy_all = dy_scr[...]
        hp_all = hp_ref[...]
        ds_all = dh_scr[...]
        dsk_cols = jnp.sum(dy_all * xs_all, axis=0, keepdims=True)
        dac = jnp.zeros((L, L), F32)
        dac_row = jnp.zeros((L, L), F32)
        ddec_cols = []
        for g in range(2):
            gs = slice(g * 512, (g + 1) * 512)
            bsl = slice(SSD_WIDTH + g * 128, SSD_WIDTH + (g + 1) * 128)
            csl = slice(SSD_WIDTH + 256 + g * 128, SSD_WIDTH + 256 + (g + 1) * 128)
            bg = xc_ref[:, bsl].astype(BF16)
            cg = xc_ref[:, csl].astype(BF16)
            gmat = _dot(cg, bg, NT_DIMS)
            hpb = hp_all[gs, :].astype(BF16)
            dsb = ds_all[gs, :].astype(BF16)
            ch = _dot(cg, hpb, NT_DIMS)
            dye = dy_all[:, gs] * e_in[:, gs]
            dyeb = dye.astype(BF16)
            dc_acc = _dot(dyeb, hpb)
            dhp = _dot(dyeb, cg, TN_DIMS)
            dxd = _dot(bg, dsb, NT_DIMS)
            db_acc = _dot((x_all[:, gs] * dec[:, gs]).astype(BF16), dsb)
            ddec = dxd * x_all[:, gs] * dec[:, gs]
            ddec_cols.append(jnp.sum(ddec, axis=0, keepdims=True))
            dx_scr[:, gs] = dxd * dec[:, gs]
            red_scr[:, gs] = dye * ch - ddec
            dg_sum = jnp.zeros((L, L), F32)
            for r in range(8):
                h = g * 8 + r
                sl = slice(h * HEAD_DIM, (h + 1) * HEAD_DIM)
                onehot_w = lane128 == h
                ldec = jnp.exp(jnp.where(causal, ac_all[:, h:h + 1] - act_all[h:h + 1, :], NEG))
                mf = gmat * ldec
                dyb = dy_scr[:, sl].astype(BF16)
                dm = _dot(dyb, x_scr[:, sl], NT_DIMS)
                dx_scr[:, sl] += _dot(mf.astype(BF16), dyb, TN_DIMS)
                dg_sum = dg_sum + dm * ldec
                wmat = dm * mf
                elast = jnp.exp(ac_all[L - 1:L, h:h + 1])
                hp_h = hp_all[sl, :]
                ds_h = ds_all[sl, :]
                extra = elast * jnp.sum(jnp.sum(hp_h * ds_h, axis=1, keepdims=True), axis=0, keepdims=True)
                dac = dac + jnp.where(onehot_w, jnp.sum(wmat, axis=1, keepdims=True) + jnp.where(last_row, extra, 0.0),
                                      0.0)
                dac_row = dac_row + jnp.where(rows128 == h, -jnp.sum(wmat, axis=0, keepdims=True), 0.0)
                dh_scr[sl, :] = elast * ds_h + dhp[r * HEAD_DIM:(r + 1) * HEAD_DIM, :]
            dgb = dg_sum.astype(BF16)
            dxc_ref[:, csl] = dc_acc + _dot(dgb, bg)
            dxc_ref[:, bsl] = db_acc + _dot(dgb, cg, TN_DIMS)
        dx_all = dx_scr[...]
        dxc_ref[:, 0:SSD_WIDTH] = dx_all * dt_e + dsk_ref[...] * dy_all
        red = red_scr[...]
        dac_slab = _head_sums(red, expander)
        ddec_tot = _head_sums(jnp.broadcast_to(jnp.concatenate(ddec_cols, axis=1), (8, SSD_WIDTH)), expander)
        ddt_x = _head_sums(dx_all * xs_all, expander)
        dsk16_ref[...] += _head_sums(jnp.broadcast_to(dsk_cols, (8, SSD_WIDTH)), expander)[0:1, 0:16]
        dac = dac + dac_slab + jnp.transpose(dac_row) + jnp.where(last_row, ddec_tot[0:1, :], 0.0)
        triu = (row <= col).astype(F32)
        da = jnp.dot(triu, dac, precision=HIGHEST, preferred_element_type=F32)[:, 0:16]
        a_row = -jnp.exp(al_ref[...])
        dt16 = dt_all[:, 0:16]
        ddt = (ddt_x[:, 0:16] + da * a_row) * sg_ref[...]
        ddt_ref[...] = ddt
        da16_ref[...] += jnp.sum(da * dt16, axis=0, keepdims=True) * a_row
        db16_ref[...] += jnp.sum(ddt, axis=0, keepdims=True)

    rb = lambda b, c: (b * nch + nch - 1 - c, 0)
    v1k = pl.BlockSpec((1, SSD_WIDTH), lambda b, c: (0, 0))
    v16 = pl.BlockSpec((1, 16), lambda b, c: (0, 0))
    wide = pl.BlockSpec((L, SSD_WIDTH), rb)
    s16 = pl.BlockSpec((L, 16), rb)
    s128 = pl.BlockSpec((L, 128), rb)
    return pl.pallas_call(
        body, name="ssd_bwd", grid=(bl, nch),
        in_specs=[wide, pl.BlockSpec((L, CONV_CH), rb), wide, wide,
                  pl.BlockSpec((None, SSD_WIDTH, SSD_STATE), lambda b, c: (b * nch + nch - 1 - c, 0, 0)),
                  s128, s16, s128, pl.BlockSpec((16, L), lambda b, c: (0, b * nch + nch - 1 - c)), v16, v1k, v1k],
        out_specs=[pl.BlockSpec((L, CONV_CH), rb), wide, s16, v1k, v16, v16, v16],
        out_shape=[jax.ShapeDtypeStruct((n, CONV_CH), F32), jax.ShapeDtypeStruct((n, SSD_WIDTH), BF16),
                   jax.ShapeDtypeStruct((n, 16), F32), jax.ShapeDtypeStruct((1, SSD_WIDTH), F32),
                   jax.ShapeDtypeStruct((1, 16), F32), jax.ShapeDtypeStruct((1, 16), F32),
                   jax.ShapeDtypeStruct((1, 16), F32)],
        scratch_shapes=[pltpu.VMEM((SSD_WIDTH, SSD_STATE), F32), pltpu.VMEM((L, SSD_WIDTH), F32),
                        pltpu.VMEM((L, SSD_WIDTH), BF16), pltpu.VMEM((L, SSD_WIDTH), F32),
                        pltpu.VMEM((L, SSD_WIDTH), F32)],
        compiler_params=_cparams(("arbitrary", "arbitrary")),
    )(dys, xc, proj_a, ypre, hprev, dt, sig, acum, acum_t, a_log, dskip_e, norm_w)


def _attn_fwd(qkv, negc, bl, t):
    n = bl * t
    tb_ = ATT_BLOCK
    nb = t // tb_
    scale2 = LOG2E / math.sqrt(HEAD_DIM)

    def body(q_ref, k_ref, v_ref, c_ref, o_ref, lse_ref):
        row = lax.broadcasted_iota(jnp.int32, (tb_, tb_), 0)
        col = lax.broadcasted_iota(jnp.int32, (tb_, tb_), 1)
        causal = row >= col
        for qi in range(nb):
            r0, lk = qi * tb_, (qi + 1) * tb_
            for j in range(2):
                sl = slice(j * HEAD_DIM, (j + 1) * HEAD_DIM)
                s = _dot(q_ref[r0:lk, sl], k_ref[0:lk, sl], NT_DIMS) * scale2 + c_ref[j:j + 1, 0:lk] * LOG2E
                tail = jnp.where(causal, s[:, r0:lk], NEG)
                s = tail if qi == 0 else jnp.concatenate([s[:, 0:r0], tail], axis=1)
                m = jnp.max(s, axis=1, keepdims=True)
                p = jnp.exp2(s - m)
                l = jnp.sum(p, axis=1, keepdims=True)
                acc = _dot(p.astype(BF16), v_ref[0:lk, sl])
                o_ref[r0:lk, sl] = (acc / l).astype(BF16)
                lse_ref[r0:lk, sl] = jnp.broadcast_to(m + jnp.log(l) * LOG2E, (tb_, HEAD_DIM))

    blk = lambda off: pl.BlockSpec((t, 128), lambda b, hp: (b, off + hp))
    return pl.pallas_call(
        body, name="fox_attn_fwd", grid=(bl, 8),
        in_specs=[blk(0), blk(8), blk(16), pl.BlockSpec((None, None, 8, t), lambda b, hp: (b, hp, 0, 0))],
        out_specs=[blk(0), blk(0)],
        out_shape=[jax.ShapeDtypeStruct((n, ATT_WIDTH), BF16), jax.ShapeDtypeStruct((n, ATT_WIDTH), F32)],
        compiler_params=_cparams(("parallel", "parallel")),
    )(qkv, qkv, qkv, negc)


def _attn_bwd(qkv, do, o, lse, negc, after, bl, t):
    n = bl * t
    tb_ = ATT_BLOCK
    nb = t // tb_
    scale = 1.0 / math.sqrt(HEAD_DIM)
    scale2 = LOG2E * scale

    def body(q_ref, k_ref, v_ref, do_ref, o_ref, lse_ref, c_ref, after_ref, dq_ref, dk_ref, dv_ref, dc_ref,
             dq_scr, delta_scr, dr_scr, qt_scr, dot_scr, dkt_scr, dvt_scr):
        row = lax.broadcasted_iota(jnp.int32, (tb_, tb_), 0)
        col = lax.broadcasted_iota(jnp.int32, (tb_, tb_), 1)
        causal = row >= col
        dq_scr[...] = jnp.zeros_like(dq_scr)
        dr_scr[...] = jnp.zeros_like(dr_scr)
        dc_ref[...] = jnp.zeros_like(dc_ref)
        qt_scr[...] = jnp.transpose(q_ref[...].astype(F32)).astype(BF16)
        dot_scr[...] = jnp.transpose(do_ref[...].astype(F32)).astype(BF16)
        prod = do_ref[...].astype(F32) * o_ref[...].astype(F32)
        for j in range(2):
            sl = slice(j * HEAD_DIM, (j + 1) * HEAD_DIM)
            delta_scr[:, sl] = jnp.broadcast_to(jnp.sum(prod[:, sl], axis=1, keepdims=True), (t, HEAD_DIM))
        for kj in range(nb):
            r0, r1 = kj * tb_, (kj + 1) * tb_
            for j in range(2):
                sl = slice(j * HEAD_DIM, (j + 1) * HEAD_DIM)
                one = slice(j * HEAD_DIM, j * HEAD_DIM + 1)
                kb = k_ref[r0:r1, sl]
                qs = q_ref[r0:t, sl]
                dos = do_ref[r0:t, sl]
                s = _dot(qs, kb, NT_DIMS) * scale2 + c_ref[j:j + 1, r0:r1] * LOG2E
                head = jnp.where(causal, s[0:tb_, :], NEG)
                s = head if kj == nb - 1 else jnp.concatenate([head, s[tb_:, :]], axis=0)
                p = jnp.exp2(s - lse_ref[r0:t, one])
                dp = _dot(dos, v_ref[r0:r1, sl], NT_DIMS)
                ds = p * (dp - delta_scr[r0:t, one])
                dsb = ds.astype(BF16)
                dvt_scr[sl, r0:r1] = _dot(dot_scr[sl, r0:t], p.astype(BF16))
                dkt_scr[sl, r0:r1] = _dot(qt_scr[sl, r0:t], dsb)
                dq_scr[r0:t, sl] += _dot(dsb, kb)
                dr_scr[r0:t, sl] += jnp.broadcast_to(jnp.sum(ds, axis=1, keepdims=True), (t - r0, HEAD_DIM))
                dc_ref[j:j + 1, r0:r1] = -jnp.sum(ds, axis=0, keepdims=True)
        dq_ref[...] = (dq_scr[...] * scale).astype(BF16)
        dk_ref[...] = (jnp.transpose(dkt_scr[...]) * scale).astype(BF16)
        dv_ref[...] = jnp.transpose(dvt_scr[...]).astype(BF16)
        dr_t = jnp.transpose(dr_scr[...])
        for j in range(2):
            dc_ref[j:j + 1, :] += dr_t[j * HEAD_DIM:j * HEAD_DIM + 1, :]

    blk = lambda off: pl.BlockSpec((t, 128), lambda b, hp: (b, off + hp))
    cblk = pl.BlockSpec((None, None, 8, t), lambda b, hp: (b, hp, 0, 0))
    return pl.pallas_call(
        body, name="fox_attn_bwd", grid=(bl, 8),
        in_specs=[blk(0), blk(8), blk(16), blk(0), blk(0), blk(0), cblk, ANY],
        out_specs=[blk(0), blk(0), blk(0), cblk],
        out_shape=[jax.ShapeDtypeStruct((n, ATT_WIDTH), BF16)] * 3 + [jax.ShapeDtypeStruct((bl, 8, 8, t), F32)],
        scratch_shapes=[pltpu.VMEM((t, 128), F32), pltpu.VMEM((t, 128), F32), pltpu.VMEM((t, 128), F32),
                        pltpu.VMEM((128, t), BF16), pltpu.VMEM((128, t), BF16),
                        pltpu.VMEM((128, t), F32), pltpu.VMEM((128, t), F32)],
        compiler_params=_cparams(("parallel", "parallel")),
    )(qkv, qkv, qkv, do, o, lse, negc, after)


def _adamw(w, g, m, v, *, name):
    r, c = w.shape
    tr = _pick(r, (256, 128, 64, 32, 16, 8))
    bc1 = 1.0 - ADAM_B1 ** ADAM_STEP
    bc2 = 1.0 - ADAM_B2 ** ADAM_STEP

    def body(w_ref, g_ref, m_ref, v_ref, d_ref, nm_ref, nv_ref):
        gv = g_ref[...]
        mn = ADAM_B1 * m_ref[...] + (1.0 - ADAM_B1) * gv
        vn = ADAM_B2 * v_ref[...] + (1.0 - ADAM_B2) * (gv * gv)
        m_hat = mn / bc1
        v_hat = vn / bc2
        d_ref[...] = -ADAM_LR * (m_hat / (jnp.sqrt(v_hat) + ADAM_EPS) + ADAM_WD * w_ref[...])
        nm_ref[...] = mn
        nv_ref[...] = vn

    blk = pl.BlockSpec((tr, c), lambda i: (i, 0))
    return pl.pallas_call(
        body, name=name, grid=(r // tr,), in_specs=[blk] * 4, out_specs=[blk] * 3,
        out_shape=[jax.ShapeDtypeStruct((r, c), F32)] * 3,
        compiler_params=_cparams(("parallel",)),
    )(w, g, m, v)


def _sum_leading(parts, *, name, out_dtype=F32):
    k, r, c = parts.shape
    tr = _pick(r, (512, 256, 128, 96, 64, 32, 16, 8))

    def body(p_ref, o_ref):
        acc = p_ref[0].astype(F32)
        for i in range(1, k):
            acc = acc + p_ref[i].astype(F32)
        o_ref[...] = acc.astype(out_dtype)

    return pl.pallas_call(
        body, name=name, grid=(r // tr,),
        in_specs=[pl.BlockSpec((k, tr, c), lambda i: (0, i, 0))],
        out_specs=pl.BlockSpec((tr, c), lambda i: (i, 0)),
        out_shape=jax.ShapeDtypeStruct((r, c), out_dtype),
        compiler_params=_cparams(("parallel",)),
    )(parts)


def _add_pair(a, b, *, name):
    k, r, c = a.shape
    tr = _pick(r, (512, 256, 128))

    def body(a_ref, b_ref, o_ref):
        o_ref[...] = (a_ref[...].astype(F32) + b_ref[...].astype(F32)).astype(BF16)

    blk = pl.BlockSpec((None, tr, c), lambda j, i: (j, i, 0))
    return pl.pallas_call(
        body, name=name, grid=(k, r // tr), in_specs=[blk, blk], out_specs=blk,
        out_shape=jax.ShapeDtypeStruct((k, r, c), BF16),
        compiler_params=_cparams(("parallel", "parallel")),
    )(a, b)


ANY = pl.BlockSpec(memory_space=pl.ANY)


def _chip_peers(x, y):
    return [(1 - x, y, 2 * (1 - x) + y), (x, 1 - y, 2 * x + 1 - y), (1 - x, 1 - y, 2 * (1 - x) + 1 - y)]


def _gather_weights(blob, *, name):
    rows, cols = blob.shape
    half_rows = rows // 2

    def body(b_ref, o_ref, send_sems, recv_sems):
        x, y, c = lax.axis_index("x"), lax.axis_index("y"), lax.axis_index("c")
        me = 2 * x + y
        sibling = (x, y, 1 - c)
        peers = _chip_peers(x, y)

        def half(chip, hc):
            return o_ref.at[chip, pl.ds(hc * half_rows, half_rows), :]

        def copy(k, src, chip, hc, to):
            return pltpu.make_async_remote_copy(src_ref=src, dst_ref=half(chip, hc), send_sem=send_sems.at[k],
                                                recv_sem=recv_sems.at[k], device_id=to, device_id_type=MESH)

        my_half = b_ref.at[pl.ds(c * half_rows, half_rows), :]
        first = [copy(k, my_half, me, c, (px, py, c)) for k, (px, py, _) in enumerate(peers)]
        for cp in first:
            cp.start()
        passed = [copy(3 + k, half(pc, c), pc, c, sibling) for k, (_, _, pc) in enumerate(peers)]
        for k, (px, py, pc) in enumerate(peers):
            copy(k, my_half, pc, c, (px, py, c)).wait_recv()
            passed[k].start()
        for k, (_, _, pc) in enumerate(peers):
            copy(3 + k, half(pc, 1 - c), pc, 1 - c, sibling).wait_recv()
        for cp in first + passed:
            cp.wait_send()

    got = pl.pallas_call(
        body, name=name, in_specs=[ANY], out_specs=ANY,
        out_shape=jax.ShapeDtypeStruct((N_CHIPS, rows, cols), BF16),
        scratch_shapes=[pltpu.SemaphoreType.DMA((6,)), pltpu.SemaphoreType.DMA((6,))],
    )(blob)
    me = 2 * lax.axis_index("x") + lax.axis_index("y")
    return lax.dynamic_update_slice(got, blob[None], (me, 0, 0))


def _swap_halves(g, *, name):
    _, rows, cols = g.shape
    half_rows = rows // 2

    def body(g_ref, o_ref, send_sem, recv_sem):
        x, y, c = lax.axis_index("x"), lax.axis_index("y"), lax.axis_index("c")
        cp = pltpu.make_async_remote_copy(
            src_ref=g_ref.at[:, pl.ds((1 - c) * half_rows, half_rows), :], dst_ref=o_ref,
            send_sem=send_sem, recv_sem=recv_sem, device_id=(x, y, 1 - c), device_id_type=MESH)
        cp.start()
        cp.wait()

    return pl.pallas_call(
        body, name=name, in_specs=[ANY], out_specs=ANY,
        out_shape=jax.ShapeDtypeStruct((N_CHIPS, half_rows, cols), BF16),
        scratch_shapes=[pltpu.SemaphoreType.DMA, pltpu.SemaphoreType.DMA],
    )(g)


def _exchange_chips(p, *, name):
    def body(p_ref, o_ref, send_sems, recv_sems):
        x, y, c = lax.axis_index("x"), lax.axis_index("y"), lax.axis_index("c")
        me = 2 * x + y
        peers = _chip_peers(x, y)
        cps = [pltpu.make_async_remote_copy(src_ref=p_ref.at[pc], dst_ref=o_ref.at[me], send_sem=send_sems.at[k],
                                            recv_sem=recv_sems.at[k], device_id=(px, py, c), device_id_type=MESH)
               for k, (px, py, pc) in enumerate(peers)]
        for cp in cps:
            cp.start()
        for k, (px, py, pc) in enumerate(peers):
            pltpu.make_async_remote_copy(src_ref=p_ref.at[pc], dst_ref=o_ref.at[pc], send_sem=send_sems.at[k],
                                         recv_sem=recv_sems.at[k], device_id=(px, py, c),
                                         device_id_type=MESH).wait_recv()
        for cp in cps:
            cp.wait_send()

    got = pl.pallas_call(
        body, name=name, in_specs=[ANY], out_specs=ANY,
        out_shape=jax.ShapeDtypeStruct(p.shape, BF16),
        scratch_shapes=[pltpu.SemaphoreType.DMA((3,)), pltpu.SemaphoreType.DMA((3,))],
    )(p)
    me = 2 * lax.axis_index("x") + lax.axis_index("y")
    return lax.dynamic_update_slice(got, lax.dynamic_slice_in_dim(p, me, 1, axis=0), (me, 0, 0))


HBM_SPEC = pl.BlockSpec(memory_space=pltpu.HBM)
SEM_SPEC = pl.BlockSpec(memory_space=pltpu.SEMAPHORE)
SPLIT_EFFECT = pltpu.SideEffectType.DATAFLOW_SIDE_EFFECTING


def _exchange_peers_copies(p_ref, land_ref, send_sems, recv_sems, sending):
    x, y, c = lax.axis_index("x"), lax.axis_index("y"), lax.axis_index("c")
    me = 2 * x + y
    return [pltpu.make_async_remote_copy(src_ref=p_ref.at[pc], dst_ref=land_ref.at[me if sending else pc],
                                         send_sem=send_sems.at[k], recv_sem=recv_sems.at[k],
                                         device_id=(px, py, c), device_id_type=MESH)
            for k, (px, py, pc) in enumerate(_chip_peers(x, y))]


def _exchange_start(p, *, name):
    def body(p_ref, land_ref, send_sems, recv_sems, p_thru, land_thru, token):
        for cp in _exchange_peers_copies(p_ref, land_ref, send_sems, recv_sems, True):
            cp.start()
        token[...] = jnp.zeros_like(token)

    return pl.pallas_call(
        body, name=name,
        out_shape=(pltpu.SemaphoreType.DMA((3,)), pltpu.SemaphoreType.DMA((3,)), pltpu.HBM(p.shape, p.dtype),
                   pltpu.HBM(p.shape, p.dtype), jax.ShapeDtypeStruct((8, 128), F32)),
        in_specs=(HBM_SPEC, HBM_SPEC),
        out_specs=(SEM_SPEC, SEM_SPEC, HBM_SPEC, HBM_SPEC, pl.BlockSpec(memory_space=pltpu.VMEM)),
        input_output_aliases={0: 2, 1: 3},
        compiler_params=pltpu.CompilerParams(has_side_effects=SPLIT_EFFECT),
    )(pltpu.with_memory_space_constraint(p, pltpu.HBM),
      pltpu.with_memory_space_constraint(lax.empty(p.shape, p.dtype), pltpu.HBM))


def _exchange_wait(send_sems, recv_sems, p_thru, land_thru, after, *, name):
    def body(p_ref, land_ref, send_sems, recv_sems, after_ref, p_dead, got_ref):
        for cp in _exchange_peers_copies(p_ref, land_ref, send_sems, recv_sems, False):
            cp.wait_send()
            cp.wait_recv()

    return pl.pallas_call(
        body, name=name,
        out_shape=(pltpu.HBM(p_thru.shape, p_thru.dtype), pltpu.HBM(p_thru.shape, p_thru.dtype)),
        in_specs=(HBM_SPEC, HBM_SPEC, SEM_SPEC, SEM_SPEC, ANY), out_specs=(HBM_SPEC, HBM_SPEC),
        input_output_aliases={0: 0, 1: 1},
        compiler_params=pltpu.CompilerParams(has_side_effects=SPLIT_EFFECT),
    )(p_thru, land_thru, send_sems, recv_sems, after)


def _sum_parts(parts, own, *, name):
    k, r, c = parts.shape
    tr = _pick(r, (512, 256, 128))

    def body(p_ref, own_ref, o_ref):
        me = 2 * lax.axis_index("x") + lax.axis_index("y")
        acc = jnp.zeros((tr, c), F32)
        for i in range(k):
            acc = acc + jnp.where(me == i, own_ref[i], p_ref[i]).astype(F32)
        o_ref[...] = acc

    blk = pl.BlockSpec((k, tr, c), lambda i: (0, i, 0))
    return pl.pallas_call(
        body, name=name, grid=(r // tr,), in_specs=[blk, blk],
        out_specs=pl.BlockSpec((tr, c), lambda i: (i, 0)),
        out_shape=jax.ShapeDtypeStruct((r, c), F32),
        compiler_params=_cparams(("parallel",)),
    )(parts, own)


def _join_halves(gh, *, name):
    def body(g_ref, o_ref, send_sem, recv_sem):
        x, y, c = lax.axis_index("x"), lax.axis_index("y"), lax.axis_index("c")
        cp = pltpu.make_async_remote_copy(src_ref=g_ref, dst_ref=o_ref, send_sem=send_sem, recv_sem=recv_sem,
                                          device_id=(x, y, 1 - c), device_id_type=MESH)
        cp.start()
        cp.wait()

    other = pl.pallas_call(
        body, name=name, in_specs=[ANY], out_specs=ANY,
        out_shape=jax.ShapeDtypeStruct(gh.shape, F32),
        scratch_shapes=[pltpu.SemaphoreType.DMA, pltpu.SemaphoreType.DMA],
    )(gh)
    south = lax.axis_index("c") == 0
    return jnp.concatenate([jnp.where(south, gh, other), jnp.where(south, other, gh)], axis=0)


def _gather_small(s, *, name):
    rows = s.shape[0]

    def body(s_ref, o_ref, send_sems, recv_sems, local_sem):
        x, y, c = lax.axis_index("x"), lax.axis_index("y"), lax.axis_index("c")
        me = 4 * x + 2 * y + c
        mine = pltpu.make_async_copy(s_ref, o_ref.at[me], local_sem)
        mine.start()
        peers = []
        for k in range(1, 8):
            peers.append((1 - x if k & 4 else x, 1 - y if k & 2 else y, 1 - c if k & 1 else c))
        cps = [pltpu.make_async_remote_copy(src_ref=s_ref, dst_ref=o_ref.at[me], send_sem=send_sems.at[k],
                                            recv_sem=recv_sems.at[k], device_id=p, device_id_type=MESH)
               for k, p in enumerate(peers)]
        for cp in cps:
            cp.start()
        for k, (px, py, pc) in enumerate(peers):
            pltpu.make_async_remote_copy(src_ref=s_ref, dst_ref=o_ref.at[4 * px + 2 * py + pc],
                                         send_sem=send_sems.at[k], recv_sem=recv_sems.at[k],
                                         device_id=(px, py, pc), device_id_type=MESH).wait_recv()
        for cp in cps:
            cp.wait_send()
        mine.wait()

    return pl.pallas_call(
        body, name=name, in_specs=[ANY], out_specs=ANY,
        out_shape=jax.ShapeDtypeStruct((8, rows, 128), F32),
        scratch_shapes=[pltpu.SemaphoreType.DMA((7,)), pltpu.SemaphoreType.DMA((7,)), pltpu.SemaphoreType.DMA],
    )(s)


IN_SHARD = IN_WIDTH // N_CHIPS
IN_SHARD_PAD = 1536
OUT_ROWS, UP_ROWS, DOWN_ROWS = 512, 1024, 1024


def _pack_in(w_in_s):
    return jnp.pad(w_in_s, ((0, 0), (0, IN_SHARD_PAD - IN_SHARD))).astype(BF16)


def _pack_rest(w_out_s, w_up_s, w_down_s):
    return jnp.concatenate([w_out_s, w_up_s, w_down_s], axis=0).astype(BF16)


def _unpack_rest(blob):
    return (blob[0:OUT_ROWS], blob[OUT_ROWS:OUT_ROWS + UP_ROWS], blob[OUT_ROWS + UP_ROWS:])


def _full_weights(g_in, g_rest):
    w_in = jnp.concatenate([g_in[j, :, :IN_SHARD] for j in range(N_CHIPS)], axis=1)
    w_out = g_rest[:, 0:OUT_ROWS].reshape(N_CHIPS * OUT_ROWS, D_MODEL)
    w_up = g_rest[:, OUT_ROWS:OUT_ROWS + UP_ROWS].transpose(1, 0, 2).reshape(D_MODEL, D_FF)
    w_down = g_rest[:, OUT_ROWS + UP_ROWS:].reshape(D_FF, D_MODEL)
    return w_in, w_out, w_up, w_down


def _split_w_in(w_in):
    z_xbc = w_in[:, 0:2560]
    dt = w_in[:, 2560:2576]
    qkv = w_in[:, 2576:5648]
    f = w_in[:, 5648:5664]
    pad = jnp.zeros((w_in.shape[0], PA_WIDTH - 2592), w_in.dtype)
    return jnp.concatenate([z_xbc, dt, f, pad], axis=1), qkv


def _merge_w_in(d_a, d_qkv):
    return jnp.concatenate([d_a[:, 0:2560], d_a[:, 2560:2576], d_qkv, d_a[:, 2576:2592]], axis=1)


def _local_step(x3, target3, w_in, w_out, w_up, w_down, norm_mix_w, conv_w, conv_b, dt_bias, a_log, d_skip,
                ssd_norm_w, f_bias, norm_mlp_w, norm_final_w, early_grads=None):
    bl, t, d = x3.shape
    n = bl * t
    x = x3.reshape(n, d)
    target = target3.reshape(n, d)
    w_a, w_qkv = _split_w_in(w_in)
    wo_s, wo_a = w_out[:SSD_WIDTH], w_out[SSD_WIDTH:]
    nfw = norm_final_w.reshape(1, d)
    dskip_e = jnp.repeat(d_skip, HEAD_DIM, axis=1)
    nb = t // ATT_BLOCK

    h0, rstd0 = _rmsnorm_fwd(x, norm_mix_w, name="norm_mix_fwd")
    r1, r2, r4, kt = min(n, 1024), min(n, 512), min(n, 256), min(n, 512)
    proj_a = _mm(h0, w_a, name="proj_a", tiles=(r2, PA_WIDTH, D_MODEL))
    qkv = _mm(h0, w_qkv, name="proj_qkv", tiles=(r2, QKV_WIDTH, D_MODEL), out_dtype=BF16)
    bias128 = jnp.concatenate([dt_bias, f_bias, jnp.zeros((1, 96), F32)], axis=1)
    alog128 = jnp.concatenate([a_log, jnp.zeros((1, 112), F32)], axis=1)
    dt, sig, acum, ccum, sigf = _prep(proj_a, bias128, alog128, bl, t)
    acum_t = acum[:, 0:16].T
    negc = jnp.pad(-ccum.reshape(bl, t, 8, 2).transpose(0, 2, 3, 1), ((0, 0), (0, 0), (0, 6), (0, 0)))
    xc = _conv_fwd(proj_a, conv_w, conv_b, bl, t)
    y_ssd, y_pre, hprev = _ssd_fwd2(xc, proj_a, dt, acum, acum_t, dskip_e, ssd_norm_w, bl, t)
    y_att, lse = _attn_fwd(qkv, negc, bl, t)
    t1 = _mm(y_ssd, wo_s, name="out_proj_ssd", tiles=(r1, D_MODEL, SSD_WIDTH), res=x)
    h1 = _mm(y_att, wo_a, name="out_proj_att", tiles=(r1, D_MODEL, ATT_WIDTH), res=t1)
    h1n, rstd1 = _rmsnorm_fwd(h1, norm_mlp_w, name="norm_mlp_fwd")
    up = _mm(h1n, w_up, name="mlp_up", tiles=(r2, D_FF, D_MODEL))
    h2 = _mm(up, w_down, name="mlp_down", tiles=(r4, D_MODEL, D_FF), a_act="relu2", res=h1)
    dh2, dh2b, loss, d_nfw = _final(h2, nfw, target)

    dup = _mm(dh2b, w_down, name="mlp_down_bwd_act", tiles=(r4, D_FF, D_MODEL), tb=True, epi_up=up, out_dtype=BF16)
    d_w_down = _mm(up, dh2b, name="mlp_down_bwd_w", tiles=(1024, 1024, kt), ta=True, a_act="relu2")
    dh1n = _mm(dup, w_up, name="mlp_up_bwd_act", tiles=(r2, D_MODEL, D_FF), tb=True)
    d_w_up = _mm(h1n, dup, name="mlp_up_bwd_w", tiles=(1024, 1024, kt), ta=True)
    dh1, dh1b, d_nmlp = _rmsnorm_bwd(dh1n, h1, rstd1, norm_mlp_w, dh2, name="norm_mlp_bwd")
    dys = _mm(dh1b, wo_s, name="out_proj_bwd_ssd", tiles=(r1, SSD_WIDTH, D_MODEL), tb=True)
    do = _mm(dh1b, wo_a, name="out_proj_bwd_att", tiles=(r1, ATT_WIDTH, D_MODEL), tb=True, out_dtype=BF16)
    d_w_out = jnp.concatenate([_mm(y_ssd, dh1b, name="out_proj_bwd_w_ssd", tiles=(1024, 1024, kt), ta=True),
                               _mm(y_att, dh1b, name="out_proj_bwd_w_att", tiles=(1024, 1024, kt), ta=True)], axis=0)
    token = jnp.zeros((8, 128), F32) if early_grads is None else early_grads(d_w_out, d_w_up, d_w_down)
    dq, dk, dv, dcb = _attn_bwd(qkv, do, y_att, lse, negc, token, bl, t)
    dc = jnp.pad(dcb[:, :, 0:2, :].transpose(0, 3, 1, 2).reshape(n, 16), ((0, 0), (0, 112)))
    df_raw, d_fb = _fpost(dc, sigf, bl, t)
    dxc, dz, ddt_raw, d_snw, d_dsk, d_alog, d_dtb = _ssd_bwd2(dys, xc, proj_a, y_pre, hprev, dt, sig, acum, acum_t,
                                                            a_log, dskip_e, ssd_norm_w, bl, t)
    dxbc, d_conv_w, d_conv_b = _conv_bwd(dxc, proj_a, conv_w, conv_b, bl, t)
    dproj_a = jnp.concatenate([dz, dxbc, ddt_raw.astype(BF16), df_raw.astype(BF16),
                               jnp.zeros((n, PA_WIDTH - 2592), BF16)], axis=1)
    dqkv = jnp.concatenate([dq, dk, dv], axis=1)
    d_w_a = _mm(h0, dproj_a, name="proj_a_bwd_w", tiles=(1024, 896, kt), ta=True)
    d_w_qkv = _mm(h0, dqkv, name="proj_qkv_bwd_w", tiles=(1024, 1024, kt), ta=True)
    t2 = _mm(dproj_a, w_a, name="proj_a_bwd_act", tiles=(r1, D_MODEL, PA_WIDTH), tb=True)
    dh0 = _mm(dqkv, w_qkv, name="proj_qkv_bwd_act", tiles=(r1, D_MODEL, QKV_WIDTH), tb=True, res=t2)
    dx, _, d_nmix = _rmsnorm_bwd(dh0, x, rstd0, norm_mix_w, dh1, name="norm_mix_bwd")

    grads = dict(norm_mix_w=d_nmix, w_in=_merge_w_in(d_w_a, d_w_qkv), conv_w=d_conv_w, conv_b=d_conv_b,
                 dt_bias=d_dtb, a_log=d_alog, d_skip=d_dsk, ssd_norm_w=d_snw, f_bias=d_fb, w_out=d_w_out,
                 norm_mlp_w=d_nmlp, w_up=d_w_up, w_down=d_w_down, norm_final_w=d_nfw)
    return dx.reshape(bl, t, d), loss, grads


SMALL_ORDER = ("norm_mix_w", "conv_w", "conv_b", "dt_bias", "a_log", "d_skip", "ssd_norm_w", "f_bias",
               "norm_mlp_w", "norm_final_w")
SMALL_SIZES = (1024, 4 * CONV_CH, CONV_CH, 16, 16, 16, 1024, 16, 1024, 1024)


def _pack_small(vals, rows):
    flat = jnp.concatenate([v.reshape(-1).astype(F32) for v in vals])
    return jnp.pad(flat, (0, rows * 128 - flat.shape[0])).reshape(rows, 128)


def _unpack_small(packed, sizes):
    flat = packed.reshape(-1)
    out, o = [], 0
    for s in sizes:
        out.append(flat[o:o + s])
        o += s
    return out


def kernel(x, norm_mix_w, w_in, conv_w, conv_b, dt_bias, a_log, d_skip, ssd_norm_w, f_bias, w_out, norm_mlp_w, w_up, w_down, norm_final_w, loss_target, m_norm_mix_w, m_w_in, m_conv_w, m_conv_b, m_dt_bias, m_a_log, m_d_skip, m_ssd_norm_w, m_f_bias, m_w_out, m_norm_mlp_w, m_w_up, m_w_down, m_norm_final_w, v_norm_mix_w, v_w_in, v_conv_w, v_conv_b, v_dt_bias, v_a_log, v_d_skip, v_ssd_norm_w, v_f_bias, v_w_out, v_norm_mlp_w, v_w_up, v_w_down, v_norm_final_w):
    chip = 2 * lax.axis_index("x") + lax.axis_index("y")
    cw = CONV_CH // N_CHIPS

    g_in = _gather_weights(_pack_in(w_in[0]), name="gather_w_in")
    g_rest = _gather_weights(_pack_rest(w_out[0], w_up[0], w_down[0]), name="gather_w_rest")
    w_in_f, w_out_f, w_up_f, w_down_f = _full_weights(g_in, g_rest)
    small_all = _gather_small(_pack_small([conv_w[0]], 16), name="gather_conv_w")
    conv_w_f = jnp.concatenate([small_all[2 * j].reshape(-1)[:4 * cw].reshape(4, cw) for j in range(N_CHIPS)], axis=1)

    c = lax.axis_index("c")

    def chip_partial(gb, tag):
        half_rows = gb.shape[1] // 2
        from_sibling = _swap_halves(gb, name="grad_swap_halves_" + tag)
        my_half = lax.dynamic_slice_in_dim(gb, c * half_rows, half_rows, axis=1)
        return _add_pair(my_half, from_sibling, name="grad_add_sibling_" + tag)

    in_flight = {}

    def early_grads(d_w_out, d_w_up, d_w_down):
        gb_rest = jnp.stack([_pack_rest(d_w_out[j * OUT_ROWS:(j + 1) * OUT_ROWS],
                                        d_w_up[:, j * UP_ROWS:(j + 1) * UP_ROWS],
                                        d_w_down[j * DOWN_ROWS:(j + 1) * DOWN_ROWS]) for j in range(N_CHIPS)])
        part = chip_partial(gb_rest, "rest")
        *handles, token = _exchange_start(part, name="grad_exchange_start_rest")
        in_flight["rest"] = handles
        return token

    dx, loss_part, g = _local_step(x, loss_target, w_in_f, w_out_f, w_up_f, w_down_f, norm_mix_w, conv_w_f,
                                   conv_b, dt_bias, a_log, d_skip, ssd_norm_w, f_bias, norm_mlp_w, norm_final_w,
                                   early_grads=early_grads)

    send_sems, recv_sems, part_rest, land_rest = in_flight["rest"]
    part_rest, parts_rest = _exchange_wait(send_sems, recv_sems, part_rest, land_rest, dx,
                                           name="grad_exchange_wait_rest")
    g_rest_half = _sum_parts(parts_rest, part_rest, name="grad_sum_chips_rest")
    g_w_out, g_w_up, g_w_down = _unpack_rest(_join_halves(g_rest_half, name="grad_join_halves_rest"))

    gb_in = jnp.stack([_pack_in(g["w_in"][:, j * IN_SHARD:(j + 1) * IN_SHARD]) for j in range(N_CHIPS)])
    parts_in = _exchange_chips(chip_partial(gb_in, "in"), name="grad_exchange_chips_in")
    g_in_half = _sum_leading(parts_in, name="grad_sum_chips_in")
    g_w_in = _join_halves(g_in_half, name="grad_join_halves_in")[:, :IN_SHARD]

    small_vals = [g[k] for k in SMALL_ORDER] + [loss_part[:, 0:1]]
    small_sum = _sum_leading(_gather_small(_pack_small(small_vals, SMALL_ROWS), name="gather_small_grads"), name="small_sum")
    sg = dict(zip(SMALL_ORDER + ("loss",), _unpack_small(small_sum, SMALL_SIZES + (1,))))
    loss = sg["loss"].reshape(())
    g_conv_full = sg["conv_w"].reshape(4, CONV_CH)
    g_conv = lax.dynamic_slice_in_dim(g_conv_full, chip * cw, cw, axis=1)

    grads = dict(norm_mix_w=sg["norm_mix_w"].reshape(1, -1), w_in=g_w_in[None], conv_w=g_conv[None],
                 conv_b=sg["conv_b"].reshape(1, -1), dt_bias=sg["dt_bias"].reshape(1, -1),
                 a_log=sg["a_log"].reshape(1, -1), d_skip=sg["d_skip"].reshape(1, -1),
                 ssd_norm_w=sg["ssd_norm_w"].reshape(1, -1), f_bias=sg["f_bias"].reshape(1, -1), w_out=g_w_out[None],
                 norm_mlp_w=sg["norm_mlp_w"].reshape(1, -1), w_up=g_w_up[None], w_down=g_w_down[None],
                 norm_final_w=sg["norm_final_w"])
    weights = dict(norm_mix_w=norm_mix_w, w_in=w_in, conv_w=conv_w, conv_b=conv_b, dt_bias=dt_bias, a_log=a_log,
                   d_skip=d_skip, ssd_norm_w=ssd_norm_w, f_bias=f_bias, w_out=w_out, norm_mlp_w=norm_mlp_w,
                   w_up=w_up, w_down=w_down, norm_final_w=norm_final_w)
    ms = dict(norm_mix_w=m_norm_mix_w, w_in=m_w_in, conv_w=m_conv_w, conv_b=m_conv_b, dt_bias=m_dt_bias,
              a_log=m_a_log, d_skip=m_d_skip, ssd_norm_w=m_ssd_norm_w, f_bias=m_f_bias, w_out=m_w_out,
              norm_mlp_w=m_norm_mlp_w, w_up=m_w_up, w_down=m_w_down, norm_final_w=m_norm_final_w)
    vs = dict(norm_mix_w=v_norm_mix_w, w_in=v_w_in, conv_w=v_conv_w, conv_b=v_conv_b, dt_bias=v_dt_bias,
              a_log=v_a_log, d_skip=v_d_skip, ssd_norm_w=v_ssd_norm_w, f_bias=v_f_bias, w_out=v_w_out,
              norm_mlp_w=v_norm_mlp_w, w_up=v_w_up, w_down=v_w_down, norm_final_w=v_norm_final_w)
    names = list(weights)
    big = ("w_in", "w_out", "w_up", "w_down")
    delta, new_m, new_v = {}, {}, {}
    for k in big:
        shp = weights[k].shape
        two_d = lambda a: a.reshape(shp[-2], shp[-1])
        d_, m_, v_ = _adamw(two_d(weights[k]), two_d(grads[k]), two_d(ms[k]), two_d(vs[k]), name="adamw_" + k)
        delta[k], new_m[k], new_v[k] = d_.reshape(shp), m_.reshape(shp), v_.reshape(shp)
    smalls = [k for k in names if k not in big]
    sizes = [math.prod(weights[k].shape) for k in smalls]
    rows = -(-sum(sizes) // 1024) * 8
    packs = [_pack_small([d[k] for k in smalls], rows) for d in (weights, grads, ms, vs)]
    outs = _adamw(*packs, name="adamw_small")
    for o, dst in zip(outs, (delta, new_m, new_v)):
        for k, val in zip(smalls, _unpack_small(o, sizes)):
            dst[k] = val.reshape(weights[k].shape)
    return (loss, dx, *[grads[k] for k in names], *[delta[k] for k in names], *[new_m[k] for k in names],
            *[new_v[k] for k in names])
```

```python
import functools
import math

import jax
import jax.numpy as jnp
from jax import lax
from jax.experimental import pallas as pl
from jax.experimental.pallas import tpu as pltpu

F32 = jnp.float32
BF16 = jnp.bfloat16
HIGHEST = lax.Precision.HIGHEST
MESH = pl.DeviceIdType.MESH

D_MODEL = 1024
SSD_HEADS = 16
HEAD_DIM = 64
SSD_WIDTH = 1024
SSD_STATE = 128
CONV_CH = 1536
CHUNK = 128
ATT_WIDTH = 1024
EPS = 1e-5
IN_WIDTH = 5664
PA_WIDTH = 2688
QKV_WIDTH = 3072
D_FF = 4096
ATT_BLOCK = 256
NEG = -1e30
LOG2E = 1.4426950408889634
VMEM_LIMIT = 48 * 1024 * 1024

ADAM_LR = 0.001
ADAM_B1 = 0.9
ADAM_B2 = 0.999
ADAM_EPS = 1e-08
ADAM_WD = 0.01
ADAM_STEP = 10

N_CHIPS = 4
BLOB_ROWS = 4096
HALF_ROWS = BLOB_ROWS // 2
SMALL_ROWS = 96


def _cparams(sem):
    return pltpu.CompilerParams(dimension_semantics=sem, vmem_limit_bytes=VMEM_LIMIT)


def _pick(n, cands):
    for c in cands:
        if n % c == 0:
            return c
    return n


MM_CHUNK = 512


def _mm(a, b, *, name, tiles, ta=False, tb=False, out_dtype=F32, res=None, a_act=None, epi_up=None, after=None):
    if ta:
        K, M = a.shape
    else:
        M, K = a.shape
    if tb:
        N, K2 = b.shape
    else:
        K2, N = b.shape
    assert K == K2, (a.shape, b.shape)
    tm, tn, tk = tiles
    assert M % tm == 0 and N % tn == 0 and K % tk == 0, (name, M, N, K, tiles)
    nk = K // tk
    dn = (((0 if ta else 1,), (1 if tb else 0,)), ((), ()))
    has_res = res is not None
    has_up = epi_up is not None
    cn = _pick(tn, (MM_CHUNK, 384, 256, 128))

    def prologue(av):
        if a_act == "relu2":
            r = jnp.maximum(av.astype(F32), 0.0)
            av = r * r
        return av.astype(BF16)

    def epilogue(out, res_v, up_v):
        if has_res:
            out = out + res_v.astype(F32)
        if has_up:
            out = out * (2.0 * jnp.maximum(up_v.astype(F32), 0.0))
        return out.astype(out_dtype)

    def body(*refs):
        a_ref, b_ref = refs[0], refs[1]
        i = 2
        res_ref = up_ref = None
        if has_res:
            res_ref = refs[i]
            i += 1
        if has_up:
            up_ref = refs[i]
            i += 1
        if after is not None:
            i += 1
        o_ref = refs[i]
        if nk == 1:
            av = prologue(a_ref[...])
            for c in range(tn // cn):
                cs = slice(c * cn, (c + 1) * cn)
                bv = (b_ref[cs, :] if tb else b_ref[:, cs]).astype(BF16)
                out = lax.dot_general(av, bv, dn, preferred_element_type=F32)
                o_ref[:, cs] = epilogue(out, res_ref[:, cs] if has_res else None, up_ref[:, cs] if has_up else None)
            return
        acc_ref = refs[i + 1]
        k = pl.program_id(2)

        @pl.when(k == 0)
        def _():
            acc_ref[...] = jnp.zeros_like(acc_ref)

        acc_ref[...] += lax.dot_general(prologue(a_ref[...]), b_ref[...].astype(BF16), dn,
                                        preferred_element_type=F32)

        @pl.when(k == nk - 1)
        def _():
            o_ref[...] = epilogue(acc_ref[...], res_ref[...] if has_res else None, up_ref[...] if has_up else None)

    a_spec = pl.BlockSpec((tk, tm), lambda i, j, k: (k, i)) if ta else pl.BlockSpec((tm, tk), lambda i, j, k: (i, k))
    b_spec = pl.BlockSpec((tn, tk), lambda i, j, k: (j, k)) if tb else pl.BlockSpec((tk, tn), lambda i, j, k: (k, j))
    o_spec = pl.BlockSpec((tm, tn), lambda i, j, k: (i, j))
    ins, specs = [a, b], [a_spec, b_spec]
    if has_res:
        ins.append(res)
        specs.append(o_spec)
    if has_up:
        ins.append(epi_up)
        specs.append(o_spec)
    if after is not None:
        ins.append(after)
        specs.append(pl.BlockSpec(memory_space=pl.ANY))
    return pl.pallas_call(
        body, name=name, grid=(M // tm, N // tn, nk),
        in_specs=specs, out_specs=o_spec,
        out_shape=jax.ShapeDtypeStruct((M, N), out_dtype),
        scratch_shapes=[] if nk == 1 else [pltpu.VMEM((tm, tn), F32)],
        compiler_params=_cparams(("parallel", "parallel", "arbitrary")),
    )(*ins)


def _rmsnorm_fwd(x, w, *, name):
    n, d = x.shape
    tm = _pick(n, (512, 256, 128))

    def body(x_ref, w_ref, y_ref, r_ref):
        xv = x_ref[...]
        rstd = lax.rsqrt(jnp.mean(xv * xv, axis=1, keepdims=True) + EPS)
        y_ref[...] = (xv * rstd * w_ref[...]).astype(BF16)
        r_ref[...] = rstd

    return pl.pallas_call(
        body, name=name, grid=(n // tm,),
        in_specs=[pl.BlockSpec((tm, d), lambda i: (i, 0)), pl.BlockSpec((1, d), lambda i: (0, 0))],
        out_specs=[pl.BlockSpec((tm, d), lambda i: (i, 0)), pl.BlockSpec((tm, 1), lambda i: (i, 0))],
        out_shape=[jax.ShapeDtypeStruct((n, d), BF16), jax.ShapeDtypeStruct((n, 1), F32)],
        compiler_params=_cparams(("parallel",)),
    )(x, w)


def _rmsnorm_bwd(dyn, x, rstd, w, dres, *, name):
    n, d = x.shape
    tm = _pick(n, (512, 256, 128))

    def body(g_ref, x_ref, r_ref, w_ref, d_ref, dx_ref, dxb_ref, dw_ref):
        @pl.when(pl.program_id(0) == 0)
        def _():
            dw_ref[...] = jnp.zeros_like(dw_ref)

        g = g_ref[...]
        r = r_ref[...]
        xhat = x_ref[...] * r
        gw = g * w_ref[...]
        dx = d_ref[...] + r * (gw - xhat * jnp.mean(gw * xhat, axis=1, keepdims=True))
        dx_ref[...] = dx
        dxb_ref[...] = dx.astype(BF16)
        dw_ref[...] += jnp.sum(g * xhat, axis=0, keepdims=True)

    row = pl.BlockSpec((tm, d), lambda i: (i, 0))
    vec = pl.BlockSpec((1, d), lambda i: (0, 0))
    return pl.pallas_call(
        body, name=name, grid=(n // tm,),
        in_specs=[row, row, pl.BlockSpec((tm, 1), lambda i: (i, 0)), vec, row],
        out_specs=[row, row, vec],
        out_shape=[jax.ShapeDtypeStruct((n, d), F32), jax.ShapeDtypeStruct((n, d), BF16),
                   jax.ShapeDtypeStruct((1, d), F32)],
        compiler_params=_cparams(("arbitrary",)),
    )(dyn, x, rstd, w, dres)


def _final(h2, w, target):
    n, d = h2.shape
    tm = _pick(n, (512, 256, 128))

    def body(h_ref, w_ref, t_ref, dh_ref, dhb_ref, loss_ref, dw_ref):
        @pl.when(pl.program_id(0) == 0)
        def _():
            loss_ref[...] = jnp.zeros_like(loss_ref)
            dw_ref[...] = jnp.zeros_like(dw_ref)

        hv = h_ref[...]
        wv = w_ref[...]
        rstd = lax.rsqrt(jnp.mean(hv * hv, axis=1, keepdims=True) + EPS)
        xhat = hv * rstd
        err = xhat * wv - t_ref[...]
        part = jnp.sum(jnp.mean(err * err, axis=1, keepdims=True), axis=0, keepdims=True)
        loss_ref[...] += 0.5 * part
        dy = err * (1.0 / d)
        gw = dy * wv
        dh = rstd * (gw - xhat * jnp.mean(gw * xhat, axis=1, keepdims=True))
        dh_ref[...] = dh
        dhb_ref[...] = dh.astype(BF16)
        dw_ref[...] += jnp.sum(dy * xhat, axis=0, keepdims=True)

    row = pl.BlockSpec((tm, d), lambda i: (i, 0))
    vec = pl.BlockSpec((1, d), lambda i: (0, 0))
    return pl.pallas_call(
        body, name="final_norm_loss", grid=(n // tm,),
        in_specs=[row, vec, row],
        out_specs=[row, row, pl.BlockSpec((1, 128), lambda i: (0, 0)), vec],
        out_shape=[jax.ShapeDtypeStruct((n, d), F32), jax.ShapeDtypeStruct((n, d), BF16),
                   jax.ShapeDtypeStruct((1, 128), F32), jax.ShapeDtypeStruct((1, d), F32)],
        compiler_params=_cparams(("arbitrary",)),
    )(h2, w, target)


def _softplus(x):
    return jnp.maximum(x, 0.0) + jnp.log(1.0 + jnp.exp(-jnp.abs(x)))


def _prep(proj_a, bias128, alog128, bl, t):
    n = bl * t
    nch = t // CHUNK
    col0 = (SSD_WIDTH + CONV_CH) // 128

    def body(p_ref, b_ref, al_ref, dt_ref, sg_ref, ac_ref, c_ref, sf_ref, carry):
        @pl.when(pl.program_id(1) == 0)
        def _():
            carry[...] = jnp.zeros_like(carry)

        xv = p_ref[...] + b_ref[...]
        sp = _softplus(xv)
        a = -jnp.exp(al_ref[...]) * sp
        logf = -_softplus(-xv)
        row = lax.broadcasted_iota(jnp.int32, (CHUNK, CHUNK), 0)
        col = lax.broadcasted_iota(jnp.int32, (CHUNK, CHUNK), 1)
        tril = (row >= col).astype(F32)
        acum = jnp.dot(tril, a, precision=HIGHEST, preferred_element_type=F32)
        c = jnp.dot(tril, logf, precision=HIGHEST, preferred_element_type=F32) + carry[...]
        carry[...] = c[CHUNK - 1:CHUNK, :]
        head_lanes = lax.broadcasted_iota(jnp.int32, (1, 128), 1) < 16
        dt_ref[...] = jnp.where(head_lanes, sp, 0.0)
        sg_ref[...] = jax.nn.sigmoid(xv)[:, 0:16]
        ac_ref[...] = jnp.where(head_lanes, acum, 0.0)
        c_ref[...] = c[:, 16:32]
        sf_ref[...] = jax.nn.sigmoid(-xv)[:, 16:32]

    o16 = pl.BlockSpec((CHUNK, 16), lambda b, c: (b * nch + c, 0))
    o128 = pl.BlockSpec((CHUNK, 128), lambda b, c: (b * nch + c, 0))
    v128 = pl.BlockSpec((1, 128), lambda b, c: (0, 0))
    w16 = jax.ShapeDtypeStruct((n, 16), F32)
    w128 = jax.ShapeDtypeStruct((n, 128), F32)
    return pl.pallas_call(
        body, name="head_scalars", grid=(bl, nch),
        in_specs=[pl.BlockSpec((CHUNK, 128), lambda b, c: (b * nch + c, col0)), v128, v128],
        out_specs=[o128, o16, o128, o16, o16],
        out_shape=[w128, w16, w128, w16, w16],
        scratch_shapes=[pltpu.VMEM((1, 128), F32)],
        compiler_params=_cparams(("parallel", "arbitrary")),
    )(proj_a, bias128, alog128)


def _fpost(dc, sigf, bl, t):
    n = bl * t
    nch = t // CHUNK

    def body(dc_ref, sf_ref, df_ref, db_ref, carry):
        @pl.when(pl.program_id(1) == 0)
        def _():
            carry[...] = jnp.zeros_like(carry)

        @pl.when((pl.program_id(0) == 0) & (pl.program_id(1) == 0))
        def _():
            db_ref[...] = jnp.zeros_like(db_ref)

        row = lax.broadcasted_iota(jnp.int32, (CHUNK, CHUNK), 0)
        col = lax.broadcasted_iota(jnp.int32, (CHUNK, CHUNK), 1)
        triu = (row <= col).astype(F32)
        dlf = jnp.dot(triu, dc_ref[...], precision=HIGHEST, preferred_element_type=F32) + carry[...]
        carry[...] = dlf[0:1, :]
        df = dlf[:, 0:16] * sf_ref[...]
        df_ref[...] = df
        db_ref[...] += jnp.sum(df, axis=0, keepdims=True)

    rev = lambda b, c: (b * nch + nch - 1 - c, 0)
    blk = pl.BlockSpec((CHUNK, 16), rev)
    return pl.pallas_call(
        body, name="forget_gate_bwd", grid=(bl, nch),
        in_specs=[pl.BlockSpec((CHUNK, 128), rev), blk],
        out_specs=[blk, pl.BlockSpec((1, 16), lambda b, c: (0, 0))],
        out_shape=[jax.ShapeDtypeStruct((n, 16), F32), jax.ShapeDtypeStruct((1, 16), F32)],
        scratch_shapes=[pltpu.VMEM((1, 128), F32)],
        compiler_params=_cparams(("arbitrary", "arbitrary")),
    )(dc, sigf)


CONV_TILE = 256
CONV_ROWS = 256


def _conv_taps(u_ref, i, w, bias):
    r0 = pl.multiple_of(i * CONV_ROWS, CONV_ROWS)
    cur = u_ref[pl.ds(r0, CONV_ROWS), :]
    p0 = pl.multiple_of(jnp.maximum(r0 - 8, 0), 8)
    prev = jnp.where(i > 0, u_ref[pl.ds(p0, 8), :], 0.0)
    cat = jnp.concatenate([prev, cur], axis=0)
    pre = bias + w[3:4, :] * cur
    taps = [cur]
    for s in (1, 2, 3):
        sh = pltpu.roll(cat, s, 0)[8:, :]
        taps.append(sh)
        pre = pre + w[3 - s:4 - s, :] * sh
    return r0, pre, taps


def _conv_fwd(proj_a, conv_w, conv_b, bl, t):
    n = bl * t
    nct = CONV_CH // CONV_TILE
    c0 = SSD_WIDTH // CONV_TILE

    def body(u_ref, w_ref, b_ref, o_ref):
        w = w_ref[...]
        bias = b_ref[...]

        def chunk(i, carry):
            r0, pre, _ = _conv_taps(u_ref, i, w, bias)
            o_ref[pl.ds(r0, CONV_ROWS), :] = pre * jax.nn.sigmoid(pre)
            return carry

        lax.fori_loop(0, t // CONV_ROWS, chunk, 0)

    return pl.pallas_call(
        body, name="conv_silu_fwd", grid=(bl, nct),
        in_specs=[pl.BlockSpec((t, CONV_TILE), lambda b, c: (b, c0 + c)),
                  pl.BlockSpec((4, CONV_TILE), lambda b, c: (0, c)),
                  pl.BlockSpec((1, CONV_TILE), lambda b, c: (0, c))],
        out_specs=pl.BlockSpec((t, CONV_TILE), lambda b, c: (b, c)),
        out_shape=jax.ShapeDtypeStruct((n, CONV_CH), F32),
        compiler_params=_cparams(("parallel", "parallel")),
    )(proj_a, conv_w, conv_b)


def _conv_bwd(dxc, proj_a, conv_w, conv_b, bl, t):
    n = bl * t
    nct = CONV_CH // CONV_TILE
    c0 = SSD_WIDTH // CONV_TILE
    nrc = t // CONV_ROWS

    def body(g_ref, u_ref, w_ref, b_ref, du_ref, dw_ref, db_ref, dp_scr):
        @pl.when(pl.program_id(1) == 0)
        def _():
            dw_ref[...] = jnp.zeros_like(dw_ref)
            db_ref[...] = jnp.zeros_like(db_ref)

        w = w_ref[...]
        bias = b_ref[...]
        dp_scr[pl.ds(t, 8), :] = jnp.zeros((8, CONV_TILE), F32)

        def chunk1(i, carry):
            dw0, dw1, dw2, dw3, db = carry
            r0, pre, taps = _conv_taps(u_ref, i, w, bias)
            sg = jax.nn.sigmoid(pre)
            dpre = g_ref[pl.ds(r0, CONV_ROWS), :] * (sg * (1.0 + pre * (1.0 - sg)))
            dp_scr[pl.ds(r0, CONV_ROWS), :] = dpre
            dw3 = dw3 + jnp.sum(dpre * taps[0], axis=0, keepdims=True)
            dw2 = dw2 + jnp.sum(dpre * taps[1], axis=0, keepdims=True)
            dw1 = dw1 + jnp.sum(dpre * taps[2], axis=0, keepdims=True)
            dw0 = dw0 + jnp.sum(dpre * taps[3], axis=0, keepdims=True)
            db = db + jnp.sum(dpre, axis=0, keepdims=True)
            return dw0, dw1, dw2, dw3, db

        z = jnp.zeros((1, CONV_TILE), F32)
        dw0, dw1, dw2, dw3, db = lax.fori_loop(0, nrc, chunk1, (z, z, z, z, z))
        dw_ref[...] += jnp.concatenate([dw0, dw1, dw2, dw3], axis=0)
        db_ref[...] += db

        def chunk2(i, carry):
            r0 = pl.multiple_of(i * CONV_ROWS, CONV_ROWS)
            cat = dp_scr[pl.ds(r0, CONV_ROWS + 8), :]
            du = w[3:4, :] * cat[:CONV_ROWS, :]
            for s in (1, 2, 3):
                du = du + w[3 - s:4 - s, :] * pltpu.roll(cat, CONV_ROWS + 8 - s, 0)[:CONV_ROWS, :]
            du_ref[pl.ds(r0, CONV_ROWS), :] = du.astype(BF16)
            return carry

        lax.fori_loop(0, nrc, chunk2, 0)

    return pl.pallas_call(
        body, name="conv_silu_bwd", grid=(nct, bl),
        in_specs=[pl.BlockSpec((t, CONV_TILE), lambda c, b: (b, c)),
                  pl.BlockSpec((t, CONV_TILE), lambda c, b: (b, c0 + c)),
                  pl.BlockSpec((4, CONV_TILE), lambda c, b: (0, c)),
                  pl.BlockSpec((1, CONV_TILE), lambda c, b: (0, c))],
        out_specs=[pl.BlockSpec((t, CONV_TILE), lambda c, b: (b, c)),
                   pl.BlockSpec((4, CONV_TILE), lambda c, b: (0, c)),
                   pl.BlockSpec((1, CONV_TILE), lambda c, b: (0, c))],
        out_shape=[jax.ShapeDtypeStruct((n, CONV_CH), BF16), jax.ShapeDtypeStruct((4, CONV_CH), F32),
                   jax.ShapeDtypeStruct((1, CONV_CH), F32)],
        scratch_shapes=[pltpu.VMEM((t + 8, CONV_TILE), F32)],
        compiler_params=_cparams(("parallel", "arbitrary")),
    )(dxc, proj_a, conv_w, conv_b)


NT_DIMS = (((1,), (1,)), ((), ()))
TN_DIMS = (((0,), (0,)), ((), ()))


def _dot(a, b, dims=None):
    if dims is None:
        return jnp.dot(a, b, preferred_element_type=F32)
    return lax.dot_general(a, b, dims, preferred_element_type=F32)


def _ssd_fwd(xc, proj_a, dt, acum, acum_t, dskip_e, norm_w, bl, t):
    n = bl * t
    nch = t // CHUNK
    L = CHUNK

    def body(xc_ref, z_ref, dt_ref, ac_ref, act_ref, dsk_ref, nw_ref, ys_ref, yp_ref, hp_ref, h_scr, y_scr):
        @pl.when(pl.program_id(1) == 0)
        def _():
            h_scr[...] = jnp.zeros_like(h_scr)

        row = lax.broadcasted_iota(jnp.int32, (L, L), 0)
        col = lax.broadcasted_iota(jnp.int32, (L, L), 1)
        causal = row >= col
        dt_all = dt_ref[...]
        ac_all = ac_ref[...]
        act_all = act_ref[...]
        for g in range(2):
            bg = xc_ref[:, SSD_WIDTH + g * 128:SSD_WIDTH + (g + 1) * 128].astype(BF16)
            cg = xc_ref[:, SSD_WIDTH + 256 + g * 128:SSD_WIDTH + 256 + (g + 1) * 128].astype(BF16)
            gmat = _dot(cg, bg, NT_DIMS)
            for r in range(8):
                h = g * 8 + r
                sl = slice(h * HEAD_DIM, (h + 1) * HEAD_DIM)
                xs = xc_ref[:, sl]
                xdt = xs * dt_all[:, h:h + 1]
                ac = ac_all[:, h:h + 1]
                ar = act_all[h:h + 1, :]
                ldec = jnp.exp(jnp.where(causal, ac - ar, NEG))
                m = (gmat * ldec).astype(BF16)
                hp = h_scr[h]
                hp_ref[h] = hp
                yd = _dot(m, xdt.astype(BF16))
                yo = _dot(cg, hp.astype(BF16), NT_DIMS) * jnp.exp(ac)
                y_scr[:, sl] = yd + yo + dsk_ref[:, sl] * xs
                alast = ac_all[L - 1:L, h:h + 1]
                xd = (xdt * jnp.exp(alast - ac)).astype(BF16)
                h_scr[h] = jnp.exp(alast) * hp + _dot(xd, bg, TN_DIMS)
        y = y_scr[...]
        yp_ref[...] = y
        zv = z_ref[...]
        yg = y * (zv * jax.nn.sigmoid(zv))
        for g in range(2):
            gs = slice(g * 512, (g + 1) * 512)
            grp = yg[:, gs]
            rstd = lax.rsqrt(jnp.mean(grp * grp, axis=1, keepdims=True) + EPS)
            ys_ref[:, gs] = (grp * rstd * nw_ref[:, gs]).astype(BF16)

    rb = lambda b, c: (b * nch + c, 0)
    v1k = pl.BlockSpec((1, SSD_WIDTH), lambda b, c: (0, 0))
    return pl.pallas_call(
        body, name="ssd_fwd", grid=(bl, nch),
        in_specs=[pl.BlockSpec((L, CONV_CH), rb), pl.BlockSpec((L, SSD_WIDTH), rb),
                  pl.BlockSpec((L, 16), rb), pl.BlockSpec((L, 16), rb),
                  pl.BlockSpec((16, L), lambda b, c: (0, b * nch + c)), v1k, v1k],
        out_specs=[pl.BlockSpec((L, SSD_WIDTH), rb), pl.BlockSpec((L, SSD_WIDTH), rb),
                   pl.BlockSpec((None, 16, HEAD_DIM, SSD_STATE), lambda b, c: (b * nch + c, 0, 0, 0))],
        out_shape=[jax.ShapeDtypeStruct((n, SSD_WIDTH), BF16), jax.ShapeDtypeStruct((n, SSD_WIDTH), F32),
                   jax.ShapeDtypeStruct((bl * nch, 16, HEAD_DIM, SSD_STATE), F32)],
        scratch_shapes=[pltpu.VMEM((16, HEAD_DIM, SSD_STATE), F32), pltpu.VMEM((L, SSD_WIDTH), F32)],
        compiler_params=_cparams(("parallel", "arbitrary")),
    )(xc, proj_a, dt, acum, acum_t, dskip_e, norm_w)


def _ssd_bwd(dys, xc, proj_a, ypre, hprev, dt, sig, acum, acum_t, a_log, dskip_e, norm_w, bl, t):
    n = bl * t
    nch = t // CHUNK
    L = CHUNK

    def body(dys_ref, xc_ref, z_ref, yp_ref, hp_ref, dt_ref, sg_ref, ac_ref, act_ref, al_ref, dsk_ref, nw_ref,
             dxc_ref, dz_ref, ddt_ref, dnw_ref, dsk16_ref, da16_ref, db16_ref, dh_scr, dy_scr):
        first = (pl.program_id(0) == 0) & (pl.program_id(1) == 0)

        @pl.when(first)
        def _():
            dnw_ref[...] = jnp.zeros_like(dnw_ref)
            dsk16_ref[...] = jnp.zeros_like(dsk16_ref)
            da16_ref[...] = jnp.zeros_like(da16_ref)
            db16_ref[...] = jnp.zeros_like(db16_ref)

        @pl.when(pl.program_id(1) == 0)
        def _():
            dh_scr[...] = jnp.zeros_like(dh_scr)

        y = yp_ref[...]
        zv = z_ref[...]
        sz = jax.nn.sigmoid(zv)
        gate = zv * sz
        yg = y * gate
        dout = dys_ref[...]
        nw = nw_ref[...]
        for g in range(2):
            gs = slice(g * 512, (g + 1) * 512)
            grp = yg[:, gs]
            rstd = lax.rsqrt(jnp.mean(grp * grp, axis=1, keepdims=True) + EPS)
            ghat = grp * rstd
            dnw_ref[:, gs] += jnp.sum(dout[:, gs] * ghat, axis=0, keepdims=True)
            gw = dout[:, gs] * nw[:, gs]
            dyg = rstd * (gw - ghat * jnp.mean(gw * ghat, axis=1, keepdims=True))
            dy_scr[:, gs] = dyg * gate[:, gs]
            dz_ref[:, gs] = (dyg * y[:, gs] * (sz[:, gs] * (1.0 + zv[:, gs] * (1.0 - sz[:, gs])))).astype(BF16)

        row = lax.broadcasted_iota(jnp.int32, (L, L), 0)
        col = lax.broadcasted_iota(jnp.int32, (L, L), 1)
        causal = row >= col
        lane16 = lax.broadcasted_iota(jnp.int32, (1, 16), 1)
        lane128 = lax.broadcasted_iota(jnp.int32, (1, L), 1)
        last_row = lax.broadcasted_iota(jnp.int32, (L, 1), 0) == (L - 1)
        dt_all = dt_ref[...]
        ac_all = ac_ref[...]
        act_all = act_ref[...]
        dac_col = jnp.zeros((L, L), F32)
        dac_row = jnp.zeros((L, L), F32)
        ddt_x = jnp.zeros((L, L), F32)
        dsk16 = jnp.zeros((1, 16), F32)
        rows16 = lax.broadcasted_iota(jnp.int32, (L, 1), 0)
        for g in range(2):
            bsl = slice(SSD_WIDTH + g * 128, SSD_WIDTH + (g + 1) * 128)
            csl = slice(SSD_WIDTH + 256 + g * 128, SSD_WIDTH + 256 + (g + 1) * 128)
            bg = xc_ref[:, bsl].astype(BF16)
            cg = xc_ref[:, csl].astype(BF16)
            gmat = _dot(cg, bg, NT_DIMS)
            dg_sum = jnp.zeros((L, L), F32)
            dc_acc = jnp.zeros((L, SSD_STATE), F32)
            db_acc = jnp.zeros((L, SSD_STATE), F32)
            for r in range(8):
                h = g * 8 + r
                sl = slice(h * HEAD_DIM, (h + 1) * HEAD_DIM)
                onehot = lane16 == h
                onehot_w = lane128 == h
                xs = xc_ref[:, sl]
                dth = dt_all[:, h:h + 1]
                xdt = xs * dth
                xb = xdt.astype(BF16)
                ac = ac_all[:, h:h + 1]
                ar = act_all[h:h + 1, :]
                alast = ac_all[L - 1:L, h:h + 1]
                ldec = jnp.exp(jnp.where(causal, ac - ar, NEG))
                mf = gmat * ldec
                e_in = jnp.exp(ac)
                dec = jnp.exp(alast - ac)
                elast = jnp.exp(alast)
                hp = hp_ref[h]
                hpb = hp.astype(BF16)
                dyh = dy_scr[:, sl]
                dyb = dyh.astype(BF16)
                dsk16 = dsk16 + jnp.where(onehot, jnp.sum(jnp.sum(dyh * xs, axis=1, keepdims=True), axis=0, keepdims=True), 0.0)
                dm = _dot(dyb, xb, NT_DIMS)
                dx = _dot(mf.astype(BF16), dyb, TN_DIMS)
                dg_sum = dg_sum + dm * ldec
                wmat = dm * mf
                dac_h = jnp.sum(wmat, axis=1, keepdims=True)
                dac_row = dac_row + jnp.where(rows16 == h, -jnp.sum(wmat, axis=0, keepdims=True), 0.0)
                ch = _dot(cg, hpb, NT_DIMS)
                dye = dyh * e_in
                dyeb = dye.astype(BF16)
                dc_acc = dc_acc + _dot(dyeb, hpb)
                dhp = _dot(dyeb, cg, TN_DIMS)
                dac_h = dac_h + jnp.sum(dye * ch, axis=1, keepdims=True)
                ds = dh_scr[h]
                dsb = ds.astype(BF16)
                dxd = _dot(bg, dsb, NT_DIMS)
                db_acc = db_acc + _dot((xdt * dec).astype(BF16), dsb)
                dx = dx + dxd * dec
                ddec = jnp.sum(dxd * xdt, axis=1, keepdims=True) * dec
                extra = (jnp.sum(ddec, axis=0, keepdims=True)
                         + elast * jnp.sum(jnp.sum(hp * ds, axis=1, keepdims=True), axis=0, keepdims=True))
                dac_h = dac_h - ddec + jnp.where(last_row, extra, 0.0)
                dh_scr[h] = elast * ds + dhp
                dac_col = dac_col + jnp.where(onehot_w, dac_h, 0.0)
                ddt_x = ddt_x + jnp.where(onehot_w, jnp.sum(dx * xs, axis=1, keepdims=True), 0.0)
                dxc_ref[:, sl] = dx * dth + dsk_ref[:, sl] * dyh
            dgb = dg_sum.astype(BF16)
            dxc_ref[:, csl] = dc_acc + _dot(dgb, bg)
            dxc_ref[:, bsl] = db_acc + _dot(dgb, cg, TN_DIMS)
        dac = dac_col + jnp.transpose(dac_row)
        triu = (row <= col).astype(F32)
        da = jnp.dot(triu, dac, precision=HIGHEST, preferred_element_type=F32)[:, 0:16]
        a_row = -jnp.exp(al_ref[...])
        ddt = (ddt_x[:, 0:16] + da * a_row) * sg_ref[...]
        ddt_ref[...] = ddt
        dsk16_ref[...] += dsk16
        da16_ref[...] += jnp.sum(da * dt_all, axis=0, keepdims=True) * a_row
        db16_ref[...] += jnp.sum(ddt, axis=0, keepdims=True)

    rb = lambda b, c: (b * nch + nch - 1 - c, 0)
    v1k = pl.BlockSpec((1, SSD_WIDTH), lambda b, c: (0, 0))
    v16 = pl.BlockSpec((1, 16), lambda b, c: (0, 0))
    wide = pl.BlockSpec((L, SSD_WIDTH), rb)
    s16 = pl.BlockSpec((L, 16), rb)
    return pl.pallas_call(
        body, name="ssd_bwd", grid=(bl, nch),
        in_specs=[wide, pl.BlockSpec((L, CONV_CH), rb), wide, wide,
                  pl.BlockSpec((None, 16, HEAD_DIM, SSD_STATE), lambda b, c: (b * nch + nch - 1 - c, 0, 0, 0)),
                  s16, s16, s16, pl.BlockSpec((16, L), lambda b, c: (0, b * nch + nch - 1 - c)), v16, v1k, v1k],
        out_specs=[pl.BlockSpec((L, CONV_CH), rb), wide, s16, v1k, v16, v16, v16],
        out_shape=[jax.ShapeDtypeStruct((n, CONV_CH), F32), jax.ShapeDtypeStruct((n, SSD_WIDTH), BF16),
                   jax.ShapeDtypeStruct((n, 16), F32), jax.ShapeDtypeStruct((1, SSD_WIDTH), F32),
                   jax.ShapeDtypeStruct((1, 16), F32), jax.ShapeDtypeStruct((1, 16), F32),
                   jax.ShapeDtypeStruct((1, 16), F32)],
        scratch_shapes=[pltpu.VMEM((16, HEAD_DIM, SSD_STATE), F32), pltpu.VMEM((L, SSD_WIDTH), F32)],
        compiler_params=_cparams(("arbitrary", "arbitrary")),
    )(dys, xc, proj_a, ypre, hprev, dt, sig, acum, acum_t, a_log, dskip_e, norm_w)


def _head_expander():
    r = lax.broadcasted_iota(jnp.int32, (128, SSD_WIDTH), 0)
    c = lax.broadcasted_iota(jnp.int32, (128, SSD_WIDTH), 1)
    return (c // HEAD_DIM == r).astype(F32)


def _spread(v128, expander):
    return jnp.dot(v128, expander, precision=HIGHEST, preferred_element_type=F32)


def _head_sums(v1024, expander):
    return lax.dot_general(v1024, expander, NT_DIMS, precision=HIGHEST, preferred_element_type=F32)


def _ssd_fwd2(xc, proj_a, dt, acum, acum_t, dskip_e, norm_w, bl, t):
    n = bl * t
    nch = t // CHUNK
    L = CHUNK

    def body(xc_ref, z_ref, dt_ref, ac_ref, act_ref, dsk_ref, nw_ref, ys_ref, yp_ref, hp_ref, h_scr, y_scr, x_scr):
        @pl.when(pl.program_id(1) == 0)
        def _():
            h_scr[...] = jnp.zeros_like(h_scr)

        row = lax.broadcasted_iota(jnp.int32, (L, L), 0)
        col = lax.broadcasted_iota(jnp.int32, (L, L), 1)
        causal = row >= col
        expander = _head_expander()
        ac_all = ac_ref[...]
        act_all = act_ref[...]
        ac_e = _spread(ac_all, expander)
        e_in = jnp.exp(ac_e)
        dec = jnp.exp(ac_e[L - 1:L, :] - ac_e)
        xs_all = xc_ref[:, 0:SSD_WIDTH]
        x_all = xs_all * _spread(dt_ref[...], expander)
        x_scr[...] = x_all.astype(BF16)
        hp_all = h_scr[...]
        hp_ref[...] = hp_all
        for g in range(2):
            gs = slice(g * 512, (g + 1) * 512)
            bg = xc_ref[:, SSD_WIDTH + g * 128:SSD_WIDTH + (g + 1) * 128].astype(BF16)
            cg = xc_ref[:, SSD_WIDTH + 256 + g * 128:SSD_WIDTH + 256 + (g + 1) * 128].astype(BF16)
            gmat = _dot(cg, bg, NT_DIMS)
            y_scr[:, gs] = (_dot(cg, hp_all[gs, :].astype(BF16), NT_DIMS) * e_in[:, gs]
                            + dsk_ref[:, gs] * xs_all[:, gs])
            s_new = _dot((x_all[:, gs] * dec[:, gs]).astype(BF16), bg, TN_DIMS)
            for r in range(8):
                h = g * 8 + r
                sl = slice(h * HEAD_DIM, (h + 1) * HEAD_DIM)
                ldec = jnp.exp(jnp.where(causal, ac_all[:, h:h + 1] - act_all[h:h + 1, :], NEG))
                y_scr[:, sl] += _dot((gmat * ldec).astype(BF16), x_scr[:, sl])
                elast = jnp.exp(ac_all[L - 1:L, h:h + 1])
                h_scr[sl, :] = elast * hp_all[sl, :] + s_new[r * HEAD_DIM:(r + 1) * HEAD_DIM, :]
        y = y_scr[...]
        yp_ref[...] = y
        zv = z_ref[...]
        yg = y * (zv * jax.nn.sigmoid(zv))
        for g in range(2):
            gs = slice(g * 512, (g + 1) * 512)
            grp = yg[:, gs]
            rstd = lax.rsqrt(jnp.mean(grp * grp, axis=1, keepdims=True) + EPS)
            ys_ref[:, gs] = (grp * rstd * nw_ref[:, gs]).astype(BF16)

    rb = lambda b, c: (b * nch + c, 0)
    v1k = pl.BlockSpec((1, SSD_WIDTH), lambda b, c: (0, 0))
    return pl.pallas_call(
        body, name="ssd_fwd", grid=(bl, nch),
        in_specs=[pl.BlockSpec((L, CONV_CH), rb), pl.BlockSpec((L, SSD_WIDTH), rb),
                  pl.BlockSpec((L, 128), rb), pl.BlockSpec((L, 128), rb),
                  pl.BlockSpec((16, L), lambda b, c: (0, b * nch + c)), v1k, v1k],
        out_specs=[pl.BlockSpec((L, SSD_WIDTH), rb), pl.BlockSpec((L, SSD_WIDTH), rb),
                   pl.BlockSpec((None, SSD_WIDTH, SSD_STATE), lambda b, c: (b * nch + c, 0, 0))],
        out_shape=[jax.ShapeDtypeStruct((n, SSD_WIDTH), BF16), jax.ShapeDtypeStruct((n, SSD_WIDTH), F32),
                   jax.ShapeDtypeStruct((bl * nch, SSD_WIDTH, SSD_STATE), F32)],
        scratch_shapes=[pltpu.VMEM((SSD_WIDTH, SSD_STATE), F32), pltpu.VMEM((L, SSD_WIDTH), F32),
                        pltpu.VMEM((L, SSD_WIDTH), BF16)],
        compiler_params=_cparams(("parallel", "arbitrary")),
    )(xc, proj_a, dt, acum, acum_t, dskip_e, norm_w)


def _ssd_bwd2(dys, xc, proj_a, ypre, hprev, dt, sig, acum, acum_t, a_log, dskip_e, norm_w, bl, t):
    n = bl * t
    nch = t // CHUNK
    L = CHUNK

    def body(dys_ref, xc_ref, z_ref, yp_ref, hp_ref, dt_ref, sg_ref, ac_ref, act_ref, al_ref, dsk_ref, nw_ref,
             dxc_ref, dz_ref, ddt_ref, dnw_ref, dsk16_ref, da16_ref, db16_ref,
             dh_scr, dy_scr, x_scr, dx_scr, red_scr):
        first = (pl.program_id(0) == 0) & (pl.program_id(1) == 0)

        @pl.when(first)
        def _():
            dnw_ref[...] = jnp.zeros_like(dnw_ref)
            dsk16_ref[...] = jnp.zeros_like(dsk16_ref)
            da16_ref[...] = jnp.zeros_like(da16_ref)
            db16_ref[...] = jnp.zeros_like(db16_ref)

        @pl.when(pl.program_id(1) == 0)
        def _():
            dh_scr[...] = jnp.zeros_like(dh_scr)

        y = yp_ref[...]
        zv = z_ref[...]
        sz = jax.nn.sigmoid(zv)
        gate = zv * sz
        yg = y * gate
        dout = dys_ref[...]
        nw = nw_ref[...]
        for g in range(2):
            gs = slice(g * 512, (g + 1) * 512)
            grp = yg[:, gs]
            rstd = lax.rsqrt(jnp.mean(grp * grp, axis=1, keepdims=True) + EPS)
            ghat = grp * rstd
            dnw_ref[:, gs] += jnp.sum(dout[:, gs] * ghat, axis=0, keepdims=True)
            gw = dout[:, gs] * nw[:, gs]
            dyg = rstd * (gw - ghat * jnp.mean(gw * ghat, axis=1, keepdims=True))
            dy_scr[:, gs] = dyg * gate[:, gs]
            dz_ref[:, gs] = (dyg * y[:, gs] * (sz[:, gs] * (1.0 + zv[:, gs] * (1.0 - sz[:, gs])))).astype(BF16)

        row = lax.broadcasted_iota(jnp.int32, (L, L), 0)
        col = lax.broadcasted_iota(jnp.int32, (L, L), 1)
        causal = row >= col
        lane128 = lax.broadcasted_iota(jnp.int32, (1, L), 1)
        rows128 = lax.broadcasted_iota(jnp.int32, (L, 1), 0)
        last_row = rows128 == (L - 1)
        expander = _head_expander()
        ac_all = ac_ref[...]
        act_all = act_ref[...]
        dt_all = dt_ref[...]
        dt_e = _spread(dt_all, expander)
        ac_e = _spread(ac_all, expander)
        e_in = jnp.exp(ac_e)
        dec = jnp.exp(ac_e[L - 1:L, :] - ac_e)
        xs_all = xc_ref[:, 0:SSD_WIDTH]
        x_all = xs_all * dt_e
        x_scr[...] = x_all.astype(BF16)
        dy_all = dy_scr[...]
        hp_all = hp_ref[...]
        ds_all = dh_scr[...]
        dsk_cols = jnp.sum(dy_all * xs_all, axis=0, keepdims=True)
        dac = jnp.zeros((L, L), F32)
        dac_row = jnp.zeros((L, L), F32)
        ddec_cols = []
        for g in range(2):
            gs = slice(g * 512, (g + 1) * 512)
            bsl = slice(SSD_WIDTH + g * 128, SSD_WIDTH + (g + 1) * 128)
            csl = slice(SSD_WIDTH + 256 + g * 128, SSD_WIDTH + 256 + (g + 1) * 128)
            bg = xc_ref[:, bsl].astype(BF16)
            cg = xc_ref[:, csl].astype(BF16)
            gmat = _dot(cg, bg, NT_DIMS)
            hpb = hp_all[gs, :].astype(BF16)
            dsb = ds_all[gs, :].astype(BF16)
            ch = _dot(cg, hpb, NT_DIMS)
            dye = dy_all[:, gs] * e_in[:, gs]
            dyeb = dye.astype(BF16)
            dc_acc = _dot(dyeb, hpb)
            dhp = _dot(dyeb, cg, TN_DIMS)
            dxd = _dot(bg, dsb, NT_DIMS)
            db_acc = _dot((x_all[:, gs] * dec[:, gs]).astype(BF16), dsb)
            ddec = dxd * x_all[:, gs] * dec[:, gs]
            ddec_cols.append(jnp.sum(ddec, axis=0, keepdims=True))
            dx_scr[:, gs] = dxd * dec[:, gs]
            red_scr[:, gs] = dye * ch - ddec
            dg_sum = jnp.zeros((L, L), F32)
            for r in range(8):
                h = g * 8 + r
                sl = slice(h * HEAD_DIM, (h + 1) * HEAD_DIM)
                onehot_w = lane128 == h
                ldec = jnp.exp(jnp.where(causal, ac_all[:, h:h + 1] - act_all[h:h + 1, :], NEG))
                mf = gmat * ldec
                dyb = dy_scr[:, sl].astype(BF16)
                dm = _dot(dyb, x_scr[:, sl], NT_DIMS)
                dx_scr[:, sl] += _dot(mf.astype(BF16), dyb, TN_DIMS)
                dg_sum = dg_sum + dm * ldec
                wmat = dm * mf
                elast = jnp.exp(ac_all[L - 1:L, h:h + 1])
                hp_h = hp_all[sl, :]
                ds_h = ds_all[sl, :]
                extra = elast * jnp.sum(jnp.sum(hp_h * ds_h, axis=1, keepdims=True), axis=0, keepdims=True)
                dac = dac + jnp.where(onehot_w, jnp.sum(wmat, axis=1, keepdims=True) + jnp.where(last_row, extra, 0.0),
                                      0.0)
                dac_row = dac_row + jnp.where(rows128 == h, -jnp.sum(wmat, axis=0, keepdims=True), 0.0)
                dh_scr[sl, :] = elast * ds_h + dhp[r * HEAD_DIM:(r + 1) * HEAD_DIM, :]
            dgb = dg_sum.astype(BF16)
            dxc_ref[:, csl] = dc_acc + _dot(dgb, bg)
            dxc_ref[:, bsl] = db_acc + _dot(dgb, cg, TN_DIMS)
        dx_all = dx_scr[...]
        dxc_ref[:, 0:SSD_WIDTH] = dx_all * dt_e + dsk_ref[...] * dy_all
        red = red_scr[...]
        dac_slab = _head_sums(red, expander)
        ddec_tot = _head_sums(jnp.broadcast_to(jnp.concatenate(ddec_cols, axis=1), (8, SSD_WIDTH)), expander)
        ddt_x = _head_sums(dx_all * xs_all, expander)
        dsk16_ref[...] += _head_sums(jnp.broadcast_to(dsk_cols, (8, SSD_WIDTH)), expander)[0:1, 0:16]
        dac = dac + dac_slab + jnp.transpose(dac_row) + jnp.where(last_row, ddec_tot[0:1, :], 0.0)
        triu = (row <= col).astype(F32)
        da = jnp.dot(triu, dac, precision=HIGHEST, preferred_element_type=F32)[:, 0:16]
        a_row = -jnp.exp(al_ref[...])
        dt16 = dt_all[:, 0:16]
        ddt = (ddt_x[:, 0:16] + da * a_row) * sg_ref[...]
        ddt_ref[...] = ddt
        da16_ref[...] += jnp.sum(da * dt16, axis=0, keepdims=True) * a_row
        db16_ref[...] += jnp.sum(ddt, axis=0, keepdims=True)

    rb = lambda b, c: (b * nch + nch - 1 - c, 0)
    v1k = pl.BlockSpec((1, SSD_WIDTH), lambda b, c: (0, 0))
    v16 = pl.BlockSpec((1, 16), lambda b, c: (0, 0))
    wide = pl.BlockSpec((L, SSD_WIDTH), rb)
    s16 = pl.BlockSpec((L, 16), rb)
    s128 = pl.BlockSpec((L, 128), rb)
    return pl.pallas_call(
        body, name="ssd_bwd", grid=(bl, nch),
        in_specs=[wide, pl.BlockSpec((L, CONV_CH), rb), wide, wide,
                  pl.BlockSpec((None, SSD_WIDTH, SSD_STATE), lambda b, c: (b * nch + nch - 1 - c, 0, 0)),
                  s128, s16, s128, pl.BlockSpec((16, L), lambda b, c: (0, b * nch + nch - 1 - c)), v16, v1k, v1k],
        out_specs=[pl.BlockSpec((L, CONV_CH), rb), wide, s16, v1k, v16, v16, v16],
        out_shape=[jax.ShapeDtypeStruct((n, CONV_CH), F32), jax.ShapeDtypeStruct((n, SSD_WIDTH), BF16),
                   jax.ShapeDtypeStruct((n, 16), F32), jax.ShapeDtypeStruct((1, SSD_WIDTH), F32),
                   jax.ShapeDtypeStruct((1, 16), F32), jax.ShapeDtypeStruct((1, 16), F32),
                   jax.ShapeDtypeStruct((1, 16), F32)],
        scratch_shapes=[pltpu.VMEM((SSD_WIDTH, SSD_STATE), F32), pltpu.VMEM((L, SSD_WIDTH), F32),
                        pltpu.VMEM((L, SSD_WIDTH), BF16), pltpu.VMEM((L, SSD_WIDTH), F32),
                        pltpu.VMEM((L, SSD_WIDTH), F32)],
        compiler_params=_cparams(("arbitrary", "arbitrary")),
    )(dys, xc, proj_a, ypre, hprev, dt, sig, acum, acum_t, a_log, dskip_e, norm_w)


def _attn_fwd(qkv, negc, bl, t):
    n = bl * t
    tb_ = ATT_BLOCK
    nb = t // tb_
    scale2 = LOG2E / math.sqrt(HEAD_DIM)

    def body(q_ref, k_ref, v_ref, c_ref, o_ref, lse_ref):
        row = lax.broadcasted_iota(jnp.int32, (tb_, tb_), 0)
        col = lax.broadcasted_iota(jnp.int32, (tb_, tb_), 1)
        causal = row >= col
        for qi in range(nb):
            r0, lk = qi * tb_, (qi + 1) * tb_
            for j in range(2):
                sl = slice(j * HEAD_DIM, (j + 1) * HEAD_DIM)
                s = _dot(q_ref[r0:lk, sl], k_ref[0:lk, sl], NT_DIMS) * scale2 + c_ref[j:j + 1, 0:lk] * LOG2E
                tail = jnp.where(causal, s[:, r0:lk], NEG)
                s = tail if qi == 0 else jnp.concatenate([s[:, 0:r0], tail], axis=1)
                m = jnp.max(s, axis=1, keepdims=True)
                p = jnp.exp2(s - m)
                l = jnp.sum(p, axis=1, keepdims=True)
                acc = _dot(p.astype(BF16), v_ref[0:lk, sl])
                o_ref[r0:lk, sl] = (acc / l).astype(BF16)
                lse_ref[r0:lk, sl] = jnp.broadcast_to(m + jnp.log(l) * LOG2E, (tb_, HEAD_DIM))

    blk = lambda off: pl.BlockSpec((t, 128), lambda b, hp: (b, off + hp))
    return pl.pallas_call(
        body, name="fox_attn_fwd", grid=(bl, 8),
        in_specs=[blk(0), blk(8), blk(16), pl.BlockSpec((None, None, 8, t), lambda b, hp: (b, hp, 0, 0))],
        out_specs=[blk(0), blk(0)],
        out_shape=[jax.ShapeDtypeStruct((n, ATT_WIDTH), BF16), jax.ShapeDtypeStruct((n, ATT_WIDTH), F32)],
        compiler_params=_cparams(("parallel", "parallel")),
    )(qkv, qkv, qkv, negc)


def _attn_bwd(qkv, do, o, lse, negc, after, bl, t):
    n = bl * t
    tb_ = ATT_BLOCK
    nb = t // tb_
    scale = 1.0 / math.sqrt(HEAD_DIM)
    scale2 = LOG2E * scale

    def body(q_ref, k_ref, v_ref, do_ref, o_ref, lse_ref, c_ref, after_ref, dq_ref, dk_ref, dv_ref, dc_ref,
             dq_scr, delta_scr, dr_scr, qt_scr, dot_scr, dkt_scr, dvt_scr):
        row = lax.broadcasted_iota(jnp.int32, (tb_, tb_), 0)
        col = lax.broadcasted_iota(jnp.int32, (tb_, tb_), 1)
        causal = row >= col
        dq_scr[...] = jnp.zeros_like(dq_scr)
        dr_scr[...] = jnp.zeros_like(dr_scr)
        dc_ref[...] = jnp.zeros_like(dc_ref)
        qt_scr[...] = jnp.transpose(q_ref[...].astype(F32)).astype(BF16)
        dot_scr[...] = jnp.transpose(do_ref[...].astype(F32)).astype(BF16)
        prod = do_ref[...].astype(F32) * o_ref[...].astype(F32)
        for j in range(2):
            sl = slice(j * HEAD_DIM, (j + 1) * HEAD_DIM)
            delta_scr[:, sl] = jnp.broadcast_to(jnp.sum(prod[:, sl], axis=1, keepdims=True), (t, HEAD_DIM))
        for kj in range(nb):
            r0, r1 = kj * tb_, (kj + 1) * tb_
            for j in range(2):
                sl = slice(j * HEAD_DIM, (j + 1) * HEAD_DIM)
                one = slice(j * HEAD_DIM, j * HEAD_DIM + 1)
                kb = k_ref[r0:r1, sl]
                qs = q_ref[r0:t, sl]
                dos = do_ref[r0:t, sl]
                s = _dot(qs, kb, NT_DIMS) * scale2 + c_ref[j:j + 1, r0:r1] * LOG2E
                head = jnp.where(causal, s[0:tb_, :], NEG)
                s = head if kj == nb - 1 else jnp.concatenate([head, s[tb_:, :]], axis=0)
                p = jnp.exp2(s - lse_ref[r0:t, one])
                dp = _dot(dos, v_ref[r0:r1, sl], NT_DIMS)
                ds = p * (dp - delta_scr[r0:t, one])
                dsb = ds.astype(BF16)
                dvt_scr[sl, r0:r1] = _dot(dot_scr[sl, r0:t], p.astype(BF16))
                dkt_scr[sl, r0:r1] = _dot(qt_scr[sl, r0:t], dsb)
                dq_scr[r0:t, sl] += _dot(dsb, kb)
                dr_scr[r0:t, sl] += jnp.broadcast_to(jnp.sum(ds, axis=1, keepdims=True), (t - r0, HEAD_DIM))
                dc_ref[j:j + 1, r0:r1] = -jnp.sum(ds, axis=0, keepdims=True)
        dq_ref[...] = (dq_scr[...] * scale).astype(BF16)
        dk_ref[...] = (jnp.transpose(dkt_scr[...]) * scale).astype(BF16)
        dv_ref[...] = jnp.transpose(dvt_scr[...]).astype(BF16)
        dr_t = jnp.transpose(dr_scr[...])
        for j in range(2):
            dc_ref[j:j + 1, :] += dr_t[j * HEAD_DIM:j * HEAD_DIM + 1, :]

    blk = lambda off: pl.BlockSpec((t, 128), lambda b, hp: (b, off + hp))
    cblk = pl.BlockSpec((None, None, 8, t), lambda b, hp: (b, hp, 0, 0))
    return pl.pallas_call(
        body, name="fox_attn_bwd", grid=(bl, 8),
        in_specs=[blk(0), blk(8), blk(16), blk(0), blk(0), blk(0), cblk, ANY],
        out_specs=[blk(0), blk(0), blk(0), cblk],
        out_shape=[jax.ShapeDtypeStruct((n, ATT_WIDTH), BF16)] * 3 + [jax.ShapeDtypeStruct((bl, 8, 8, t), F32)],
        scratch_shapes=[pltpu.VMEM((t, 128), F32), pltpu.VMEM((t, 128), F32), pltpu.VMEM((t, 128), F32),
                        pltpu.VMEM((128, t), BF16), pltpu.VMEM((128, t), BF16),
                        pltpu.VMEM((128, t), F32), pltpu.VMEM((128, t), F32)],
        compiler_params=_cparams(("parallel", "parallel")),
    )(qkv, qkv, qkv, do, o, lse, negc, after)


def _adamw(w, g, m, v, *, name):
    r, c = w.shape
    tr = _pick(r, (256, 128, 64, 32, 16, 8))
    bc1 = 1.0 - ADAM_B1 ** ADAM_STEP
    bc2 = 1.0 - ADAM_B2 ** ADAM_STEP

    def body(w_ref, g_ref, m_ref, v_ref, d_ref, nm_ref, nv_ref):
        gv = g_ref[...]
        mn = ADAM_B1 * m_ref[...] + (1.0 - ADAM_B1) * gv
        vn = ADAM_B2 * v_ref[...] + (1.0 - ADAM_B2) * (gv * gv)
        m_hat = mn / bc1
        v_hat = vn / bc2
        d_ref[...] = -ADAM_LR * (m_hat / (jnp.sqrt(v_hat) + ADAM_EPS) + ADAM_WD * w_ref[...])
        nm_ref[...] = mn
        nv_ref[...] = vn

    blk = pl.BlockSpec((tr, c), lambda i: (i, 0))
    return pl.pallas_call(
        body, name=name, grid=(r // tr,), in_specs=[blk] * 4, out_specs=[blk] * 3,
        out_shape=[jax.ShapeDtypeStruct((r, c), F32)] * 3,
        compiler_params=_cparams(("parallel",)),
    )(w, g, m, v)


def _sum_leading(parts, *, name, out_dtype=F32):
    k, r, c = parts.shape
    tr = _pick(r, (512, 256, 128, 96, 64, 32, 16, 8))

    def body(p_ref, o_ref):
        acc = p_ref[0].astype(F32)
        for i in range(1, k):
            acc = acc + p_ref[i].astype(F32)
        o_ref[...] = acc.astype(out_dtype)

    return pl.pallas_call(
        body, name=name, grid=(r // tr,),
        in_specs=[pl.BlockSpec((k, tr, c), lambda i: (0, i, 0))],
        out_specs=pl.BlockSpec((tr, c), lambda i: (i, 0)),
        out_shape=jax.ShapeDtypeStruct((r, c), out_dtype),
        compiler_params=_cparams(("parallel",)),
    )(parts)


def _add_pair(a, b, *, name):
    k, r, c = a.shape
    tr = _pick(r, (512, 256, 128))

    def body(a_ref, b_ref, o_ref):
        o_ref[...] = (a_ref[...].astype(F32) + b_ref[...].astype(F32)).astype(BF16)

    blk = pl.BlockSpec((None, tr, c), lambda j, i: (j, i, 0))
    return pl.pallas_call(
        body, name=name, grid=(k, r // tr), in_specs=[blk, blk], out_specs=blk,
        out_shape=jax.ShapeDtypeStruct((k, r, c), BF16),
        compiler_params=_cparams(("parallel", "parallel")),
    )(a, b)


ANY = pl.BlockSpec(memory_space=pl.ANY)


def _chip_peers(x, y):
    return [(1 - x, y, 2 * (1 - x) + y), (x, 1 - y, 2 * x + 1 - y), (1 - x, 1 - y, 2 * (1 - x) + 1 - y)]


def _gather_weights(blob, *, name):
    rows, cols = blob.shape
    half_rows = rows // 2

    def body(b_ref, o_ref, send_sems, recv_sems):
        x, y, c = lax.axis_index("x"), lax.axis_index("y"), lax.axis_index("c")
        me = 2 * x + y
        sibling = (x, y, 1 - c)
        peers = _chip_peers(x, y)

        def half(chip, hc):
            return o_ref.at[chip, pl.ds(hc * half_rows, half_rows), :]

        def copy(k, src, chip, hc, to):
            return pltpu.make_async_remote_copy(src_ref=src, dst_ref=half(chip, hc), send_sem=send_sems.at[k],
                                                recv_sem=recv_sems.at[k], device_id=to, device_id_type=MESH)

        my_half = b_ref.at[pl.ds(c * half_rows, half_rows), :]
        first = [copy(k, my_half, me, c, (px, py, c)) for k, (px, py, _) in enumerate(peers)]
        for cp in first:
            cp.start()
        passed = [copy(3 + k, half(pc, c), pc, c, sibling) for k, (_, _, pc) in enumerate(peers)]
        for k, (px, py, pc) in enumerate(peers):
            copy(k, my_half, pc, c, (px, py, c)).wait_recv()
            passed[k].start()
        for k, (_, _, pc) in enumerate(peers):
            copy(3 + k, half(pc, 1 - c), pc, 1 - c, sibling).wait_recv()
        for cp in first + passed:
            cp.wait_send()

    return pl.pallas_call(
        body, name=name, in_specs=[ANY], out_specs=ANY,
        out_shape=jax.ShapeDtypeStruct((N_CHIPS, rows, cols), BF16),
        scratch_shapes=[pltpu.SemaphoreType.DMA((6,)), pltpu.SemaphoreType.DMA((6,))],
    )(blob)


def _swap_halves(g, *, name):
    _, rows, cols = g.shape
    half_rows = rows // 2

    def body(g_ref, o_ref, send_sem, recv_sem):
        x, y, c = lax.axis_index("x"), lax.axis_index("y"), lax.axis_index("c")
        cp = pltpu.make_async_remote_copy(
            src_ref=g_ref.at[:, pl.ds((1 - c) * half_rows, half_rows), :], dst_ref=o_ref,
            send_sem=send_sem, recv_sem=recv_sem, device_id=(x, y, 1 - c), device_id_type=MESH)
        cp.start()
        cp.wait()

    return pl.pallas_call(
        body, name=name, in_specs=[ANY], out_specs=ANY,
        out_shape=jax.ShapeDtypeStruct((N_CHIPS, half_rows, cols), BF16),
        scratch_shapes=[pltpu.SemaphoreType.DMA, pltpu.SemaphoreType.DMA],
    )(g)


def _exchange_chips(p, *, name):
    def body(p_ref, o_ref, send_sems, recv_sems):
        x, y, c = lax.axis_index("x"), lax.axis_index("y"), lax.axis_index("c")
        me = 2 * x + y
        peers = _chip_peers(x, y)
        cps = [pltpu.make_async_remote_copy(src_ref=p_ref.at[pc], dst_ref=o_ref.at[me], send_sem=send_sems.at[k],
                                            recv_sem=recv_sems.at[k], device_id=(px, py, c), device_id_type=MESH)
               for k, (px, py, pc) in enumerate(peers)]
        for cp in cps:
            cp.start()
        for k, (px, py, pc) in enumerate(peers):
            pltpu.make_async_remote_copy(src_ref=p_ref.at[pc], dst_ref=o_ref.at[pc], send_sem=send_sems.at[k],
                                         recv_sem=recv_sems.at[k], device_id=(px, py, c),
                                         device_id_type=MESH).wait_recv()
        for cp in cps:
            cp.wait_send()

    got = pl.pallas_call(
        body, name=name, in_specs=[ANY], out_specs=ANY,
        out_shape=jax.ShapeDtypeStruct(p.shape, BF16),
        scratch_shapes=[pltpu.SemaphoreType.DMA((3,)), pltpu.SemaphoreType.DMA((3,))],
    )(p)
    me = 2 * lax.axis_index("x") + lax.axis_index("y")
    return lax.dynamic_update_slice(got, lax.dynamic_slice_in_dim(p, me, 1, axis=0), (me, 0, 0))


HBM_SPEC = pl.BlockSpec(memory_space=pltpu.HBM)
SEM_SPEC = pl.BlockSpec(memory_space=pltpu.SEMAPHORE)
SPLIT_EFFECT = pltpu.SideEffectType.DATAFLOW_SIDE_EFFECTING


def _gather_peers_copies(b_ref, land_ref, send_sems, recv_sems, sending):
    x, y, c = lax.axis_index("x"), lax.axis_index("y"), lax.axis_index("c")
    me = 2 * x + y
    half_rows = b_ref.shape[0] // 2
    src = b_ref.at[pl.ds(c * half_rows, half_rows), :]
    return [pltpu.make_async_remote_copy(
        src_ref=src, dst_ref=land_ref.at[me if sending else pc, pl.ds(c * half_rows, half_rows), :],
        send_sem=send_sems.at[k], recv_sem=recv_sems.at[k], device_id=(px, py, c), device_id_type=MESH)
        for k, (px, py, pc) in enumerate(_chip_peers(x, y))]


def _gather_start(blob, after, *, name):
    shape = (N_CHIPS,) + blob.shape

    def body(b_ref, land_ref, after_ref, send_sems, recv_sems, b_thru, land_thru, token):
        for cp in _gather_peers_copies(b_ref, land_ref, send_sems, recv_sems, True):
            cp.start()
        token[...] = jnp.zeros_like(token)

    return pl.pallas_call(
        body, name=name,
        out_shape=(pltpu.SemaphoreType.DMA((3,)), pltpu.SemaphoreType.DMA((3,)), pltpu.HBM(blob.shape, blob.dtype),
                   pltpu.HBM(shape, blob.dtype), jax.ShapeDtypeStruct((8, 128), F32)),
        in_specs=(HBM_SPEC, HBM_SPEC, ANY),
        out_specs=(SEM_SPEC, SEM_SPEC, HBM_SPEC, HBM_SPEC, pl.BlockSpec(memory_space=pltpu.VMEM)),
        input_output_aliases={0: 2, 1: 3},
        compiler_params=pltpu.CompilerParams(has_side_effects=SPLIT_EFFECT),
    )(pltpu.with_memory_space_constraint(blob, pltpu.HBM),
      pltpu.with_memory_space_constraint(lax.empty(shape, blob.dtype), pltpu.HBM), after)


def _gather_wait(send_sems, recv_sems, b_thru, land_thru, after, *, name):
    def body(b_ref, land_ref, send_sems, recv_sems, after_ref, b_dead, got_ref):
        for cp in _gather_peers_copies(b_ref, land_ref, send_sems, recv_sems, False):
            cp.wait_send()
            cp.wait_recv()

    return pl.pallas_call(
        body, name=name,
        out_shape=(pltpu.HBM(b_thru.shape, b_thru.dtype), pltpu.HBM(land_thru.shape, land_thru.dtype)),
        in_specs=(HBM_SPEC, HBM_SPEC, SEM_SPEC, SEM_SPEC, ANY), out_specs=(HBM_SPEC, HBM_SPEC),
        input_output_aliases={0: 0, 1: 1},
        compiler_params=pltpu.CompilerParams(has_side_effects=SPLIT_EFFECT),
    )(b_thru, land_thru, send_sems, recv_sems, after)


def _gather_forward(land, *, name):
    half_rows = land.shape[1] // 2

    def body(l_ref, o_ref, send_sems, recv_sems):
        x, y, c = lax.axis_index("x"), lax.axis_index("y"), lax.axis_index("c")
        cps = []
        for k, (_, _, pc) in enumerate(_chip_peers(x, y)):
            mine = pl.ds(c * half_rows, half_rows)
            cps.append(pltpu.make_async_remote_copy(
                src_ref=l_ref.at[pc, mine, :], dst_ref=o_ref.at[pc, mine, :], send_sem=send_sems.at[k],
                recv_sem=recv_sems.at[k], device_id=(x, y, 1 - c), device_id_type=MESH))
        for cp in cps:
            cp.start()
        for k, (_, _, pc) in enumerate(_chip_peers(x, y)):
            theirs = pl.ds((1 - c) * half_rows, half_rows)
            pltpu.make_async_remote_copy(
                src_ref=l_ref.at[pc, theirs, :], dst_ref=o_ref.at[pc, theirs, :], send_sem=send_sems.at[k],
                recv_sem=recv_sems.at[k], device_id=(x, y, 1 - c), device_id_type=MESH).wait_recv()
        for cp in cps:
            cp.wait_send()

    return pl.pallas_call(
        body, name=name, in_specs=[ANY], out_specs=ANY, input_output_aliases={0: 0},
        out_shape=jax.ShapeDtypeStruct(land.shape, land.dtype),
        scratch_shapes=[pltpu.SemaphoreType.DMA((3,)), pltpu.SemaphoreType.DMA((3,))],
    )(land)


def _exchange_peers_copies(p_ref, land_ref, send_sems, recv_sems, sending):
    x, y, c = lax.axis_index("x"), lax.axis_index("y"), lax.axis_index("c")
    me = 2 * x + y
    return [pltpu.make_async_remote_copy(src_ref=p_ref.at[pc], dst_ref=land_ref.at[me if sending else pc],
                                         send_sem=send_sems.at[k], recv_sem=recv_sems.at[k],
                                         device_id=(px, py, c), device_id_type=MESH)
            for k, (px, py, pc) in enumerate(_chip_peers(x, y))]


def _exchange_start(p, *, name):
    def body(p_ref, land_ref, send_sems, recv_sems, p_thru, land_thru, token):
        for cp in _exchange_peers_copies(p_ref, land_ref, send_sems, recv_sems, True):
            cp.start()
        token[...] = jnp.zeros_like(token)

    return pl.pallas_call(
        body, name=name,
        out_shape=(pltpu.SemaphoreType.DMA((3,)), pltpu.SemaphoreType.DMA((3,)), pltpu.HBM(p.shape, p.dtype),
                   pltpu.HBM(p.shape, p.dtype), jax.ShapeDtypeStruct((8, 128), F32)),
        in_specs=(HBM_SPEC, HBM_SPEC),
        out_specs=(SEM_SPEC, SEM_SPEC, HBM_SPEC, HBM_SPEC, pl.BlockSpec(memory_space=pltpu.VMEM)),
        input_output_aliases={0: 2, 1: 3},
        compiler_params=pltpu.CompilerParams(has_side_effects=SPLIT_EFFECT),
    )(pltpu.with_memory_space_constraint(p, pltpu.HBM),
      pltpu.with_memory_space_constraint(lax.empty(p.shape, p.dtype), pltpu.HBM))


def _exchange_wait(send_sems, recv_sems, p_thru, land_thru, after, *, name):
    def body(p_ref, land_ref, send_sems, recv_sems, after_ref, p_dead, got_ref):
        for cp in _exchange_peers_copies(p_ref, land_ref, send_sems, recv_sems, False):
            cp.wait_send()
            cp.wait_recv()

    return pl.pallas_call(
        body, name=name,
        out_shape=(pltpu.HBM(p_thru.shape, p_thru.dtype), pltpu.HBM(p_thru.shape, p_thru.dtype)),
        in_specs=(HBM_SPEC, HBM_SPEC, SEM_SPEC, SEM_SPEC, ANY), out_specs=(HBM_SPEC, HBM_SPEC),
        input_output_aliases={0: 0, 1: 1},
        compiler_params=pltpu.CompilerParams(has_side_effects=SPLIT_EFFECT),
    )(p_thru, land_thru, send_sems, recv_sems, after)


def _sum_parts(parts, own, *, name):
    k, r, c = parts.shape
    tr = _pick(r, (512, 256, 128))

    def body(p_ref, own_ref, o_ref):
        me = 2 * lax.axis_index("x") + lax.axis_index("y")
        acc = jnp.zeros((tr, c), F32)
        for i in range(k):
            acc = acc + jnp.where(me == i, own_ref[i], p_ref[i]).astype(F32)
        o_ref[...] = acc

    blk = pl.BlockSpec((k, tr, c), lambda i: (0, i, 0))
    return pl.pallas_call(
        body, name=name, grid=(r // tr,), in_specs=[blk, blk],
        out_specs=pl.BlockSpec((tr, c), lambda i: (i, 0)),
        out_shape=jax.ShapeDtypeStruct((r, c), F32),
        compiler_params=_cparams(("parallel",)),
    )(parts, own)


def _join_halves(gh, *, name):
    def body(g_ref, o_ref, send_sem, recv_sem):
        x, y, c = lax.axis_index("x"), lax.axis_index("y"), lax.axis_index("c")
        cp = pltpu.make_async_remote_copy(src_ref=g_ref, dst_ref=o_ref, send_sem=send_sem, recv_sem=recv_sem,
                                          device_id=(x, y, 1 - c), device_id_type=MESH)
        cp.start()
        cp.wait()

    other = pl.pallas_call(
        body, name=name, in_specs=[ANY], out_specs=ANY,
        out_shape=jax.ShapeDtypeStruct(gh.shape, F32),
        scratch_shapes=[pltpu.SemaphoreType.DMA, pltpu.SemaphoreType.DMA],
    )(gh)
    south = lax.axis_index("c") == 0
    return jnp.concatenate([jnp.where(south, gh, other), jnp.where(south, other, gh)], axis=0)


def _gather_small(s, *, name):
    rows = s.shape[0]

    def body(s_ref, o_ref, send_sems, recv_sems, local_sem):
        x, y, c = lax.axis_index("x"), lax.axis_index("y"), lax.axis_index("c")
        me = 4 * x + 2 * y + c
        mine = pltpu.make_async_copy(s_ref, o_ref.at[me], local_sem)
        mine.start()
        peers = []
        for k in range(1, 8):
            peers.append((1 - x if k & 4 else x, 1 - y if k & 2 else y, 1 - c if k & 1 else c))
        cps = [pltpu.make_async_remote_copy(src_ref=s_ref, dst_ref=o_ref.at[me], send_sem=send_sems.at[k],
                                            recv_sem=recv_sems.at[k], device_id=p, device_id_type=MESH)
               for k, p in enumerate(peers)]
        for cp in cps:
            cp.start()
        for k, (px, py, pc) in enumerate(peers):
            pltpu.make_async_remote_copy(src_ref=s_ref, dst_ref=o_ref.at[4 * px + 2 * py + pc],
                                         send_sem=send_sems.at[k], recv_sem=recv_sems.at[k],
                                         device_id=(px, py, pc), device_id_type=MESH).wait_recv()
        for cp in cps:
            cp.wait_send()
        mine.wait()

    return pl.pallas_call(
        body, name=name, in_specs=[ANY], out_specs=ANY,
        out_shape=jax.ShapeDtypeStruct((8, rows, 128), F32),
        scratch_shapes=[pltpu.SemaphoreType.DMA((7,)), pltpu.SemaphoreType.DMA((7,)), pltpu.SemaphoreType.DMA],
    )(s)


IN_SHARD = IN_WIDTH // N_CHIPS
IN_SHARD_PAD = 1536
OUT_ROWS, UP_ROWS, DOWN_ROWS = 512, 1024, 1024


def _pack_in(w_in_s):
    return jnp.pad(w_in_s, ((0, 0), (0, IN_SHARD_PAD - IN_SHARD))).astype(BF16)


def _pack_rest(w_out_s, w_up_s, w_down_s):
    return jnp.concatenate([w_out_s, w_up_s, w_down_s], axis=0).astype(BF16)


def _unpack_rest(blob):
    return (blob[0:OUT_ROWS], blob[OUT_ROWS:OUT_ROWS + UP_ROWS], blob[OUT_ROWS + UP_ROWS:])


def _with_own(gathered, own):
    me = 2 * lax.axis_index("x") + lax.axis_index("y")
    return [jnp.where(me == j, own, gathered[j]) for j in range(N_CHIPS)]


def _full_w_in(g_in, own):
    return jnp.concatenate([s[:, :IN_SHARD] for s in _with_own(g_in, own)], axis=1)


def _full_rest(g_rest, own):
    shards = _with_own(g_rest, own)
    w_out = jnp.concatenate([s[0:OUT_ROWS] for s in shards], axis=0)
    w_up = jnp.concatenate([s[OUT_ROWS:OUT_ROWS + UP_ROWS] for s in shards], axis=1)
    w_down = jnp.concatenate([s[OUT_ROWS + UP_ROWS:] for s in shards], axis=0)
    return w_out, w_up, w_down


def _split_w_in(w_in):
    z_xbc = w_in[:, 0:2560]
    dt = w_in[:, 2560:2576]
    qkv = w_in[:, 2576:5648]
    f = w_in[:, 5648:5664]
    pad = jnp.zeros((w_in.shape[0], PA_WIDTH - 2592), w_in.dtype)
    return jnp.concatenate([z_xbc, dt, f, pad], axis=1), qkv


def _merge_w_in(d_a, d_qkv):
    return jnp.concatenate([d_a[:, 0:2560], d_a[:, 2560:2576], d_qkv, d_a[:, 2576:2592]], axis=1)


def _local_step(x3, target3, w_in, rest_weights, norm_mix_w, conv_w, conv_b, dt_bias, a_log, d_skip,
                ssd_norm_w, f_bias, norm_mlp_w, norm_final_w, first_after=None, early_grads=None, late_grads=None):
    bl, t, d = x3.shape
    n = bl * t
    x = x3.reshape(n, d)
    target = target3.reshape(n, d)
    w_a, w_qkv = _split_w_in(w_in)
    nfw = norm_final_w.reshape(1, d)
    dskip_e = jnp.repeat(d_skip, HEAD_DIM, axis=1)
    nb = t // ATT_BLOCK

    h0, rstd0 = _rmsnorm_fwd(x, norm_mix_w, name="norm_mix_fwd")
    r1, r2, r4, kt = min(n, 1024), min(n, 512), min(n, 256), min(n, 512)
    proj_a = _mm(h0, w_a, name="proj_a", tiles=(r2, PA_WIDTH, D_MODEL), after=first_after)
    qkv = _mm(h0, w_qkv, name="proj_qkv", tiles=(r2, QKV_WIDTH, D_MODEL), out_dtype=BF16)
    bias128 = jnp.concatenate([dt_bias, f_bias, jnp.zeros((1, 96), F32)], axis=1)
    alog128 = jnp.concatenate([a_log, jnp.zeros((1, 112), F32)], axis=1)
    dt, sig, acum, ccum, sigf = _prep(proj_a, bias128, alog128, bl, t)
    acum_t = acum[:, 0:16].T
    negc = jnp.pad(-ccum.reshape(bl, t, 8, 2).transpose(0, 2, 3, 1), ((0, 0), (0, 0), (0, 6), (0, 0)))
    xc = _conv_fwd(proj_a, conv_w, conv_b, bl, t)
    y_ssd, y_pre, hprev = _ssd_fwd2(xc, proj_a, dt, acum, acum_t, dskip_e, ssd_norm_w, bl, t)
    y_att, lse = _attn_fwd(qkv, negc, bl, t)
    w_out, w_up, w_down = rest_weights(y_att)
    wo_s, wo_a = w_out[:SSD_WIDTH], w_out[SSD_WIDTH:]
    t1 = _mm(y_ssd, wo_s, name="out_proj_ssd", tiles=(r1, D_MODEL, SSD_WIDTH), res=x)
    h1 = _mm(y_att, wo_a, name="out_proj_att", tiles=(r1, D_MODEL, ATT_WIDTH), res=t1)
    h1n, rstd1 = _rmsnorm_fwd(h1, norm_mlp_w, name="norm_mlp_fwd")
    up = _mm(h1n, w_up, name="mlp_up", tiles=(r2, D_FF, D_MODEL))
    h2 = _mm(up, w_down, name="mlp_down", tiles=(r4, D_MODEL, D_FF), a_act="relu2", res=h1)
    dh2, dh2b, loss, d_nfw = _final(h2, nfw, target)

    dup = _mm(dh2b, w_down, name="mlp_down_bwd_act", tiles=(r4, D_FF, D_MODEL), tb=True, epi_up=up, out_dtype=BF16)
    d_w_down = _mm(up, dh2b, name="mlp_down_bwd_w", tiles=(1024, 1024, kt), ta=True, a_act="relu2")
    dh1n = _mm(dup, w_up, name="mlp_up_bwd_act", tiles=(r2, D_MODEL, D_FF), tb=True)
    d_w_up = _mm(h1n, dup, name="mlp_up_bwd_w", tiles=(1024, 1024, kt), ta=True)
    dh1, dh1b, d_nmlp = _rmsnorm_bwd(dh1n, h1, rstd1, norm_mlp_w, dh2, name="norm_mlp_bwd")
    dys = _mm(dh1b, wo_s, name="out_proj_bwd_ssd", tiles=(r1, SSD_WIDTH, D_MODEL), tb=True)
    do = _mm(dh1b, wo_a, name="out_proj_bwd_att", tiles=(r1, ATT_WIDTH, D_MODEL), tb=True, out_dtype=BF16)
    d_w_out = jnp.concatenate([_mm(y_ssd, dh1b, name="out_proj_bwd_w_ssd", tiles=(1024, 1024, kt), ta=True),
                               _mm(y_att, dh1b, name="out_proj_bwd_w_att", tiles=(1024, 1024, kt), ta=True)], axis=0)
    token = jnp.zeros((8, 128), F32) if early_grads is None else early_grads(d_w_out, d_w_up, d_w_down)
    dq, dk, dv, dcb = _attn_bwd(qkv, do, y_att, lse, negc, token, bl, t)
    dc = jnp.pad(dcb[:, :, 0:2, :].transpose(0, 3, 1, 2).reshape(n, 16), ((0, 0), (0, 112)))
    df_raw, d_fb = _fpost(dc, sigf, bl, t)
    dxc, dz, ddt_raw, d_snw, d_dsk, d_alog, d_dtb = _ssd_bwd2(dys, xc, proj_a, y_pre, hprev, dt, sig, acum, acum_t,
                                                            a_log, dskip_e, ssd_norm_w, bl, t)
    dxbc, d_conv_w, d_conv_b = _conv_bwd(dxc, proj_a, conv_w, conv_b, bl, t)
    dproj_a = jnp.concatenate([dz, dxbc, ddt_raw.astype(BF16), df_raw.astype(BF16),
                               jnp.zeros((n, PA_WIDTH - 2592), BF16)], axis=1)
    dqkv = jnp.concatenate([dq, dk, dv], axis=1)
    d_w_a = _mm(h0, dproj_a, name="proj_a_bwd_w", tiles=(1024, 896, kt), ta=True)
    d_w_qkv = _mm(h0, dqkv, name="proj_qkv_bwd_w", tiles=(1024, 1024, kt), ta=True)
    d_w_in = _merge_w_in(d_w_a, d_w_qkv)
    late_token = None if late_grads is None else late_grads(d_w_in)
    t2 = _mm(dproj_a, w_a, name="proj_a_bwd_act", tiles=(r1, D_MODEL, PA_WIDTH), tb=True, after=late_token)
    dh0 = _mm(dqkv, w_qkv, name="proj_qkv_bwd_act", tiles=(r1, D_MODEL, QKV_WIDTH), tb=True, res=t2)
    dx, _, d_nmix = _rmsnorm_bwd(dh0, x, rstd0, norm_mix_w, dh1, name="norm_mix_bwd")

    grads = dict(norm_mix_w=d_nmix, w_in=d_w_in, conv_w=d_conv_w, conv_b=d_conv_b,
                 dt_bias=d_dtb, a_log=d_alog, d_skip=d_dsk, ssd_norm_w=d_snw, f_bias=d_fb, w_out=d_w_out,
                 norm_mlp_w=d_nmlp, w_up=d_w_up, w_down=d_w_down, norm_final_w=d_nfw)
    return dx.reshape(bl, t, d), loss, grads


SMALL_ORDER = ("norm_mix_w", "conv_w", "conv_b", "dt_bias", "a_log", "d_skip", "ssd_norm_w", "f_bias",
               "norm_mlp_w", "norm_final_w")
SMALL_SIZES = (1024, 4 * CONV_CH, CONV_CH, 16, 16, 16, 1024, 16, 1024, 1024)


def _pack_small(vals, rows):
    flat = jnp.concatenate([v.reshape(-1).astype(F32) for v in vals])
    return jnp.pad(flat, (0, rows * 128 - flat.shape[0])).reshape(rows, 128)


def _unpack_small(packed, sizes):
    flat = packed.reshape(-1)
    out, o = [], 0
    for s in sizes:
        out.append(flat[o:o + s])
        o += s
    return out


def kernel(x, norm_mix_w, w_in, conv_w, conv_b, dt_bias, a_log, d_skip, ssd_norm_w, f_bias, w_out, norm_mlp_w, w_up, w_down, norm_final_w, loss_target, m_norm_mix_w, m_w_in, m_conv_w, m_conv_b, m_dt_bias, m_a_log, m_d_skip, m_ssd_norm_w, m_f_bias, m_w_out, m_norm_mlp_w, m_w_up, m_w_down, m_norm_final_w, v_norm_mix_w, v_w_in, v_conv_w, v_conv_b, v_dt_bias, v_a_log, v_d_skip, v_ssd_norm_w, v_f_bias, v_w_out, v_norm_mlp_w, v_w_up, v_w_down, v_norm_final_w):
    chip = 2 * lax.axis_index("x") + lax.axis_index("y")
    cw = CONV_CH // N_CHIPS

    own_in = _pack_in(w_in[0])
    own_rest = _pack_rest(w_out[0], w_up[0], w_down[0])
    g_in = _gather_weights(own_in, name="gather_w_in")
    w_in_f = _full_w_in(g_in, own_in)
    *rest_handles, rest_token = _gather_start(own_rest, g_in, name="gather_start_rest")

    def rest_weights(after):
        _, landed = _gather_wait(*rest_handles, after, name="gather_wait_rest")
        return _full_rest(_gather_forward(landed, name="gather_forward_rest"), own_rest)
    small_all = _gather_small(_pack_small([conv_w[0]], 16), name="gather_conv_w")
    conv_w_f = jnp.concatenate([small_all[2 * j].reshape(-1)[:4 * cw].reshape(4, cw) for j in range(N_CHIPS)], axis=1)

    c = lax.axis_index("c")

    def chip_partial(gb, tag):
        half_rows = gb.shape[1] // 2
        from_sibling = _swap_halves(gb, name="grad_swap_halves_" + tag)
        my_half = lax.dynamic_slice_in_dim(gb, c * half_rows, half_rows, axis=1)
        return _add_pair(my_half, from_sibling, name="grad_add_sibling_" + tag)

    in_flight = {}

    def early_grads(d_w_out, d_w_up, d_w_down):
        gb_rest = jnp.stack([_pack_rest(d_w_out[j * OUT_ROWS:(j + 1) * OUT_ROWS],
                                        d_w_up[:, j * UP_ROWS:(j + 1) * UP_ROWS],
                                        d_w_down[j * DOWN_ROWS:(j + 1) * DOWN_ROWS]) for j in range(N_CHIPS)])
        part = chip_partial(gb_rest, "rest")
        *handles, token = _exchange_start(part, name="grad_exchange_start_rest")
        in_flight["rest"] = handles
        return token

    def late_grads(d_w_in):
        gb_in = jnp.stack([_pack_in(d_w_in[:, j * IN_SHARD:(j + 1) * IN_SHARD]) for j in range(N_CHIPS)])
        *handles, token = _exchange_start(chip_partial(gb_in, "in"), name="grad_exchange_start_in")
        in_flight["in"] = handles
        return token

    dx, loss_part, g = _local_step(x, loss_target, w_in_f, rest_weights, norm_mix_w, conv_w_f,
                                   conv_b, dt_bias, a_log, d_skip, ssd_norm_w, f_bias, norm_mlp_w, norm_final_w,
                                   first_after=rest_token, early_grads=early_grads, late_grads=late_grads)

    send_sems, recv_sems, part_rest, land_rest = in_flight["rest"]
    part_rest, parts_rest = _exchange_wait(send_sems, recv_sems, part_rest, land_rest, dx,
                                           name="grad_exchange_wait_rest")
    g_rest_half = _sum_parts(parts_rest, part_rest, name="grad_sum_chips_rest")
    g_w_out, g_w_up, g_w_down = _unpack_rest(_join_halves(g_rest_half, name="grad_join_halves_rest"))

    part_in, parts_in = _exchange_wait(*in_flight["in"], dx, name="grad_exchange_wait_in")
    g_in_half = _sum_parts(parts_in, part_in, name="grad_sum_chips_in")
    g_w_in = _join_halves(g_in_half, name="grad_join_halves_in")[:, :IN_SHARD]

    small_vals = [g[k] for k in SMALL_ORDER] + [loss_part[:, 0:1]]
    small_sum = _sum_leading(_gather_small(_pack_small(small_vals, SMALL_ROWS), name="gather_small_grads"), name="small_sum")
    sg = dict(zip(SMALL_ORDER + ("loss",), _unpack_small(small_sum, SMALL_SIZES + (1,))))
    loss = sg["loss"].reshape(())
    g_conv_full = sg["conv_w"].reshape(4, CONV_CH)
    g_conv = lax.dynamic_slice_in_dim(g_conv_full, chip * cw, cw, axis=1)

    grads = dict(norm_mix_w=sg["norm_mix_w"].reshape(1, -1), w_in=g_w_in[None], conv_w=g_conv[None],
                 conv_b=sg["conv_b"].reshape(1, -1), dt_bias=sg["dt_bias"].reshape(1, -1),
                 a_log=sg["a_log"].reshape(1, -1), d_skip=sg["d_skip"].reshape(1, -1),
                 ssd_norm_w=sg["ssd_norm_w"].reshape(1, -1), f_bias=sg["f_bias"].reshape(1, -1), w_out=g_w_out[None],
                 norm_mlp_w=sg["norm_mlp_w"].reshape(1, -1), w_up=g_w_up[None], w_down=g_w_down[None],
                 norm_final_w=sg["norm_final_w"])
    weights = dict(norm_mix_w=norm_mix_w, w_in=w_in, conv_w=conv_w, conv_b=conv_b, dt_bias=dt_bias, a_log=a_log,
                   d_skip=d_skip, ssd_norm_w=ssd_norm_w, f_bias=f_bias, w_out=w_out, norm_mlp_w=norm_mlp_w,
                   w_up=w_up, w_down=w_down, norm_final_w=norm_final_w)
    ms = dict(norm_mix_w=m_norm_mix_w, w_in=m_w_in, conv_w=m_conv_w, conv_b=m_conv_b, dt_bias=m_dt_bias,
              a_log=m_a_log, d_skip=m_d_skip, ssd_norm_w=m_ssd_norm_w, f_bias=m_f_bias, w_out=m_w_out,
              norm_mlp_w=m_norm_mlp_w, w_up=m_w_up, w_down=m_w_down, norm_final_w=m_norm_final_w)
    vs = dict(norm_mix_w=v_norm_mix_w, w_in=v_w_in, conv_w=v_conv_w, conv_b=v_conv_b, dt_bias=v_dt_bias,
              a_log=v_a_log, d_skip=v_d_skip, ssd_norm_w=v_ssd_norm_w, f_bias=v_f_bias, w_out=v_w_out,
              norm_mlp_w=v_norm_mlp_w, w_up=v_w_up, w_down=v_w_down, norm_final_w=v_norm_final_w)
    names = list(weights)
    big = ("w_in", "w_out", "w_up", "w_down")
    delta, new_m, new_v = {}, {}, {}
    for k in big:
        shp = weights[k].shape
        two_d = lambda a: a.reshape(shp[-2], shp[-1])
        d_, m_, v_ = _adamw(two_d(weights[k]), two_d(grads[k]), two_d(ms[k]), two_d(vs[k]), name="adamw_" + k)
        delta[k], new_m[k], new_v[k] = d_.reshape(shp), m_.reshape(shp), v_.reshape(shp)
    smalls = [k for k in names if k not in big]
    sizes = [math.prod(weights[k].shape) for k in smalls]
    rows = -(-sum(sizes) // 1024) * 8
    packs = [_pack_small([d[k] for k in smalls], rows) for d in (weights, grads, ms, vs)]
    outs = _adamw(*packs, name="adamw_small")
    for o, dst in zip(outs, (delta, new_m, new_v)):
        for k, val in zip(smalls, _unpack_small(o, sizes)):
            dst[k] = val.reshape(weights[k].shape)
    return (loss, dx, *[grads[k] for k in names], *[delta[k] for k in names], *[new_m[k] for k in names],
            *[new_v[k] for k in names])
```

```python
import functools
import math

import jax
import jax.numpy as jnp
from jax import lax
from jax.experimental import pallas as pl
from jax.experimental.pallas import tpu as pltpu

F32 = jnp.float32
BF16 = jnp.bfloat16
HIGHEST = lax.Precision.HIGHEST
MESH = pl.DeviceIdType.MESH

D_MODEL = 1024
SSD_HEADS = 16
HEAD_DIM = 64
SSD_WIDTH = 1024
SSD_STATE = 128
CONV_CH = 1536
CHUNK = 128
ATT_WIDTH = 1024
EPS = 1e-5
IN_WIDTH = 5664
PA_WIDTH = 2688
QKV_WIDTH = 3072
D_FF = 4096
ATT_BLOCK = 256
NEG = -1e30
LOG2E = 1.4426950408889634
VMEM_LIMIT = 48 * 1024 * 1024

ADAM_LR = 0.001
ADAM_B1 = 0.9
ADAM_B2 = 0.999
ADAM_EPS = 1e-08
ADAM_WD = 0.01
ADAM_STEP = 10

N_CHIPS = 4
BLOB_ROWS = 4096
HALF_ROWS = BLOB_ROWS // 2
SMALL_ROWS = 96


def _cparams(sem):
    return pltpu.CompilerParams(dimension_semantics=sem, vmem_limit_bytes=VMEM_LIMIT)


def _pick(n, cands):
    for c in cands:
        if n % c == 0:
            return c
    return n


MM_CHUNK = 512


def _mm(a, b, *, name, tiles, ta=False, tb=False, out_dtype=F32, res=None, a_act=None, epi_up=None, after=None):
    if ta:
        K, M = a.shape
    else:
        M, K = a.shape
    if tb:
        N, K2 = b.shape
    else:
        K2, N = b.shape
    assert K == K2, (a.shape, b.shape)
    tm, tn, tk = tiles
    assert M % tm == 0 and N % tn == 0 and K % tk == 0, (name, M, N, K, tiles)
    nk = K // tk
    dn = (((0 if ta else 1,), (1 if tb else 0,)), ((), ()))
    has_res = res is not None
    has_up = epi_up is not None
    cn = _pick(tn, (MM_CHUNK, 384, 256, 128))

    def prologue(av):
        if a_act == "relu2":
            r = jnp.maximum(av.astype(F32), 0.0)
            av = r * r
        return av.astype(BF16)

    def epilogue(out, res_v, up_v):
        if has_res:
            out = out + res_v.astype(F32)
        if has_up:
            out = out * (2.0 * jnp.maximum(up_v.astype(F32), 0.0))
        return out.astype(out_dtype)

    def body(*refs):
        a_ref, b_ref = refs[0], refs[1]
        i = 2
        res_ref = up_ref = None
        if has_res:
            res_ref = refs[i]
            i += 1
        if has_up:
            up_ref = refs[i]
            i += 1
        if after is not None:
            i += 1
        o_ref = refs[i]
        if nk == 1:
            av = prologue(a_ref[...])
            for c in range(tn // cn):
                cs = slice(c * cn, (c + 1) * cn)
                bv = (b_ref[cs, :] if tb else b_ref[:, cs]).astype(BF16)
                out = lax.dot_general(av, bv, dn, preferred_element_type=F32)
                o_ref[:, cs] = epilogue(out, res_ref[:, cs] if has_res else None, up_ref[:, cs] if has_up else None)
            return
        acc_ref = refs[i + 1]
        k = pl.program_id(2)

        @pl.when(k == 0)
        def _():
            acc_ref[...] = jnp.zeros_like(acc_ref)

        acc_ref[...] += lax.dot_general(prologue(a_ref[...]), b_ref[...].astype(BF16), dn,
                                        preferred_element_type=F32)

        @pl.when(k == nk - 1)
        def _():
            o_ref[...] = epilogue(acc_ref[...], res_ref[...] if has_res else None, up_ref[...] if has_up else None)

    a_spec = pl.BlockSpec((tk, tm), lambda i, j, k: (k, i)) if ta else pl.BlockSpec((tm, tk), lambda i, j, k: (i, k))
    b_spec = pl.BlockSpec((tn, tk), lambda i, j, k: (j, k)) if tb else pl.BlockSpec((tk, tn), lambda i, j, k: (k, j))
    o_spec = pl.BlockSpec((tm, tn), lambda i, j, k: (i, j))
    ins, specs = [a, b], [a_spec, b_spec]
    if has_res:
        ins.append(res)
        specs.append(o_spec)
    if has_up:
        ins.append(epi_up)
        specs.append(o_spec)
    if after is not None:
        ins.append(after)
        specs.append(pl.BlockSpec(memory_space=pl.ANY))
    return pl.pallas_call(
        body, name=name, grid=(M // tm, N // tn, nk),
        in_specs=specs, out_specs=o_spec,
        out_shape=jax.ShapeDtypeStruct((M, N), out_dtype),
        scratch_shapes=[] if nk == 1 else [pltpu.VMEM((tm, tn), F32)],
        compiler_params=_cparams(("parallel", "parallel", "arbitrary")),
    )(*ins)


def _rmsnorm_fwd(x, w, *, name):
    n, d = x.shape
    tm = _pick(n, (512, 256, 128))

    def body(x_ref, w_ref, y_ref, r_ref):
        xv = x_ref[...]
        rstd = lax.rsqrt(jnp.mean(xv * xv, axis=1, keepdims=True) + EPS)
        y_ref[...] = (xv * rstd * w_ref[...]).astype(BF16)
        r_ref[...] = rstd

    return pl.pallas_call(
        body, name=name, grid=(n // tm,),
        in_specs=[pl.BlockSpec((tm, d), lambda i: (i, 0)), pl.BlockSpec((1, d), lambda i: (0, 0))],
        out_specs=[pl.BlockSpec((tm, d), lambda i: (i, 0)), pl.BlockSpec((tm, 1), lambda i: (i, 0))],
        out_shape=[jax.ShapeDtypeStruct((n, d), BF16), jax.ShapeDtypeStruct((n, 1), F32)],
        compiler_params=_cparams(("parallel",)),
    )(x, w)


def _rmsnorm_bwd(dyn, x, rstd, w, dres, *, name):
    n, d = x.shape
    tm = _pick(n, (512, 256, 128))

    def body(g_ref, x_ref, r_ref, w_ref, d_ref, dx_ref, dxb_ref, dw_ref):
        @pl.when(pl.program_id(0) == 0)
        def _():
            dw_ref[...] = jnp.zeros_like(dw_ref)

        g = g_ref[...]
        r = r_ref[...]
        xhat = x_ref[...] * r
        gw = g * w_ref[...]
        dx = d_ref[...] + r * (gw - xhat * jnp.mean(gw * xhat, axis=1, keepdims=True))
        dx_ref[...] = dx
        dxb_ref[...] = dx.astype(BF16)
        dw_ref[...] += jnp.sum(g * xhat, axis=0, keepdims=True)

    row = pl.BlockSpec((tm, d), lambda i: (i, 0))
    vec = pl.BlockSpec((1, d), lambda i: (0, 0))
    return pl.pallas_call(
        body, name=name, grid=(n // tm,),
        in_specs=[row, row, pl.BlockSpec((tm, 1), lambda i: (i, 0)), vec, row],
        out_specs=[row, row, vec],
        out_shape=[jax.ShapeDtypeStruct((n, d), F32), jax.ShapeDtypeStruct((n, d), BF16),
                   jax.ShapeDtypeStruct((1, d), F32)],
        compiler_params=_cparams(("arbitrary",)),
    )(dyn, x, rstd, w, dres)


def _final(h2, w, target):
    n, d = h2.shape
    tm = _pick(n, (512, 256, 128))

    def body(h_ref, w_ref, t_ref, dh_ref, dhb_ref, loss_ref, dw_ref):
        @pl.when(pl.program_id(0) == 0)
        def _():
            loss_ref[...] = jnp.zeros_like(loss_ref)
            dw_ref[...] = jnp.zeros_like(dw_ref)

        hv = h_ref[...]
        wv = w_ref[...]
        rstd = lax.rsqrt(jnp.mean(hv * hv, axis=1, keepdims=True) + EPS)
        xhat = hv * rstd
        err = xhat * wv - t_ref[...]
        part = jnp.sum(jnp.mean(err * err, axis=1, keepdims=True), axis=0, keepdims=True)
        loss_ref[...] += 0.5 * part
        dy = err * (1.0 / d)
        gw = dy * wv
        dh = rstd * (gw - xhat * jnp.mean(gw * xhat, axis=1, keepdims=True))
        dh_ref[...] = dh
        dhb_ref[...] = dh.astype(BF16)
        dw_ref[...] += jnp.sum(dy * xhat, axis=0, keepdims=True)

    row = pl.BlockSpec((tm, d), lambda i: (i, 0))
    vec = pl.BlockSpec((1, d), lambda i: (0, 0))
    return pl.pallas_call(
        body, name="final_norm_loss", grid=(n // tm,),
        in_specs=[row, vec, row],
        out_specs=[row, row, pl.BlockSpec((1, 128), lambda i: (0, 0)), vec],
        out_shape=[jax.ShapeDtypeStruct((n, d), F32), jax.ShapeDtypeStruct((n, d), BF16),
                   jax.ShapeDtypeStruct((1, 128), F32), jax.ShapeDtypeStruct((1, d), F32)],
        compiler_params=_cparams(("arbitrary",)),
    )(h2, w, target)


def _rows_product(a_ref, b_ref, tb, a_act):
    av = a_ref[...]
    if a_act == "relu2":
        r = jnp.maximum(av.astype(F32), 0.0)
        av = r * r
    dn = (((1,), (1 if tb else 0,)), ((), ()))
    return lax.dot_general(av.astype(BF16), b_ref[...].astype(BF16), dn, preferred_element_type=F32)


def _mm_norm_fwd(a, b, res, w, *, name, tm):
    m, k = a.shape
    d = b.shape[1]

    def body(a_ref, b_ref, res_ref, w_ref, h_ref, y_ref, r_ref):
        hv = _rows_product(a_ref, b_ref, False, None) + res_ref[...]
        rstd = lax.rsqrt(jnp.mean(hv * hv, axis=1, keepdims=True) + EPS)
        h_ref[...] = hv
        y_ref[...] = (hv * rstd * w_ref[...]).astype(BF16)
        r_ref[...] = rstd

    row = pl.BlockSpec((tm, d), lambda i: (i, 0))
    return pl.pallas_call(
        body, name=name, grid=(m // tm,),
        in_specs=[pl.BlockSpec((tm, k), lambda i: (i, 0)), pl.BlockSpec((k, d), lambda i: (0, 0)), row,
                  pl.BlockSpec((1, d), lambda i: (0, 0))],
        out_specs=[row, row, pl.BlockSpec((tm, 1), lambda i: (i, 0))],
        out_shape=[jax.ShapeDtypeStruct((m, d), F32), jax.ShapeDtypeStruct((m, d), BF16),
                   jax.ShapeDtypeStruct((m, 1), F32)],
        compiler_params=_cparams(("parallel",)),
    )(a, b, res, w)


def _mm_final(a, b, res, w, target, *, name, tm, a_act):
    m, k = a.shape
    d = b.shape[1]

    def body(a_ref, b_ref, res_ref, w_ref, t_ref, dh_ref, dhb_ref, loss_ref, dw_ref):
        @pl.when(pl.program_id(0) == 0)
        def _():
            loss_ref[...] = jnp.zeros_like(loss_ref)
            dw_ref[...] = jnp.zeros_like(dw_ref)

        hv = _rows_product(a_ref, b_ref, False, a_act) + res_ref[...]
        wv = w_ref[...]
        rstd = lax.rsqrt(jnp.mean(hv * hv, axis=1, keepdims=True) + EPS)
        xhat = hv * rstd
        err = xhat * wv - t_ref[...]
        loss_ref[...] += 0.5 * jnp.sum(jnp.mean(err * err, axis=1, keepdims=True), axis=0, keepdims=True)
        dy = err * (1.0 / d)
        gw = dy * wv
        dh = rstd * (gw - xhat * jnp.mean(gw * xhat, axis=1, keepdims=True))
        dh_ref[...] = dh
        dhb_ref[...] = dh.astype(BF16)
        dw_ref[...] += jnp.sum(dy * xhat, axis=0, keepdims=True)

    row = pl.BlockSpec((tm, d), lambda i: (i, 0))
    vec = pl.BlockSpec((1, d), lambda i: (0, 0))
    return pl.pallas_call(
        body, name=name, grid=(m // tm,),
        in_specs=[pl.BlockSpec((tm, k), lambda i: (i, 0)), pl.BlockSpec((k, d), lambda i: (0, 0)), row, vec, row],
        out_specs=[row, row, pl.BlockSpec((1, 128), lambda i: (0, 0)), vec],
        out_shape=[jax.ShapeDtypeStruct((m, d), F32), jax.ShapeDtypeStruct((m, d), BF16),
                   jax.ShapeDtypeStruct((1, 128), F32), jax.ShapeDtypeStruct((1, d), F32)],
        compiler_params=_cparams(("arbitrary",)),
    )(a, b, res, w, target)


def _mm_norm_bwd(a, b, res, x, rstd, w, dres, *, name, tm):
    m, k = a.shape
    d = b.shape[0]
    has_res = res is not None

    def body(*refs):
        a_ref, b_ref = refs[0], refs[1]
        i = 2
        res_ref = None
        if has_res:
            res_ref = refs[i]
            i += 1
        x_ref, r_ref, w_ref, d_ref, dx_ref, dxb_ref, dw_ref = refs[i:i + 7]

        @pl.when(pl.program_id(0) == 0)
        def _():
            dw_ref[...] = jnp.zeros_like(dw_ref)

        g = _rows_product(a_ref, b_ref, True, None)
        if has_res:
            g = g + res_ref[...]
        r = r_ref[...]
        xhat = x_ref[...] * r
        gw = g * w_ref[...]
        dx = d_ref[...] + r * (gw - xhat * jnp.mean(gw * xhat, axis=1, keepdims=True))
        dx_ref[...] = dx
        dxb_ref[...] = dx.astype(BF16)
        dw_ref[...] += jnp.sum(g * xhat, axis=0, keepdims=True)

    row = pl.BlockSpec((tm, d), lambda i: (i, 0))
    vec = pl.BlockSpec((1, d), lambda i: (0, 0))
    ins = [a, b] + ([res] if has_res else []) + [x, rstd, w, dres]
    specs = ([pl.BlockSpec((tm, k), lambda i: (i, 0)), pl.BlockSpec((d, k), lambda i: (0, 0))]
             + ([row] if has_res else []) + [row, pl.BlockSpec((tm, 1), lambda i: (i, 0)), vec, row])
    return pl.pallas_call(
        body, name=name, grid=(m // tm,), in_specs=specs, out_specs=[row, row, vec],
        out_shape=[jax.ShapeDtypeStruct((m, d), F32), jax.ShapeDtypeStruct((m, d), BF16),
                   jax.ShapeDtypeStruct((1, d), F32)],
        compiler_params=_cparams(("arbitrary",)),
    )(*ins)


def _softplus(x):
    return jnp.maximum(x, 0.0) + jnp.log(1.0 + jnp.exp(-jnp.abs(x)))


def _prep(proj_a, bias128, alog128, bl, t):
    n = bl * t
    nch = t // CHUNK
    col0 = (SSD_WIDTH + CONV_CH) // 128

    def body(p_ref, b_ref, al_ref, dt_ref, sg_ref, ac_ref, c_ref, sf_ref, carry):
        @pl.when(pl.program_id(1) == 0)
        def _():
            carry[...] = jnp.zeros_like(carry)

        xv = p_ref[...] + b_ref[...]
        sp = _softplus(xv)
        a = -jnp.exp(al_ref[...]) * sp
        logf = -_softplus(-xv)
        row = lax.broadcasted_iota(jnp.int32, (CHUNK, CHUNK), 0)
        col = lax.broadcasted_iota(jnp.int32, (CHUNK, CHUNK), 1)
        tril = (row >= col).astype(F32)
        acum = jnp.dot(tril, a, precision=HIGHEST, preferred_element_type=F32)
        c = jnp.dot(tril, logf, precision=HIGHEST, preferred_element_type=F32) + carry[...]
        carry[...] = c[CHUNK - 1:CHUNK, :]
        head_lanes = lax.broadcasted_iota(jnp.int32, (1, 128), 1) < 16
        dt_ref[...] = jnp.where(head_lanes, sp, 0.0)
        sg_ref[...] = jax.nn.sigmoid(xv)[:, 0:16]
        ac_ref[...] = jnp.where(head_lanes, acum, 0.0)
        c_ref[...] = c[:, 16:32]
        sf_ref[...] = jax.nn.sigmoid(-xv)[:, 16:32]

    o16 = pl.BlockSpec((CHUNK, 16), lambda b, c: (b * nch + c, 0))
    o128 = pl.BlockSpec((CHUNK, 128), lambda b, c: (b * nch + c, 0))
    v128 = pl.BlockSpec((1, 128), lambda b, c: (0, 0))
    w16 = jax.ShapeDtypeStruct((n, 16), F32)
    w128 = jax.ShapeDtypeStruct((n, 128), F32)
    return pl.pallas_call(
        body, name="head_scalars", grid=(bl, nch),
        in_specs=[pl.BlockSpec((CHUNK, 128), lambda b, c: (b * nch + c, col0)), v128, v128],
        out_specs=[o128, o16, o128, o16, o16],
        out_shape=[w128, w16, w128, w16, w16],
        scratch_shapes=[pltpu.VMEM((1, 128), F32)],
        compiler_params=_cparams(("parallel", "arbitrary")),
    )(proj_a, bias128, alog128)


def _fpost(dc, sigf, bl, t):
    n = bl * t
    nch = t // CHUNK

    def body(dc_ref, sf_ref, df_ref, db_ref, carry):
        @pl.when(pl.program_id(1) == 0)
        def _():
            carry[...] = jnp.zeros_like(carry)

        @pl.when((pl.program_id(0) == 0) & (pl.program_id(1) == 0))
        def _():
            db_ref[...] = jnp.zeros_like(db_ref)

        row = lax.broadcasted_iota(jnp.int32, (CHUNK, CHUNK), 0)
        col = lax.broadcasted_iota(jnp.int32, (CHUNK, CHUNK), 1)
        triu = (row <= col).astype(F32)
        dlf = jnp.dot(triu, dc_ref[...], precision=HIGHEST, preferred_element_type=F32) + carry[...]
        carry[...] = dlf[0:1, :]
        df = dlf[:, 0:16] * sf_ref[...]
        df_ref[...] = df
        db_ref[...] += jnp.sum(df, axis=0, keepdims=True)

    rev = lambda b, c: (b * nch + nch - 1 - c, 0)
    blk = pl.BlockSpec((CHUNK, 16), rev)
    return pl.pallas_call(
        body, name="forget_gate_bwd", grid=(bl, nch),
        in_specs=[pl.BlockSpec((CHUNK, 128), rev), blk],
        out_specs=[blk, pl.BlockSpec((1, 16), lambda b, c: (0, 0))],
        out_shape=[jax.ShapeDtypeStruct((n, 16), F32), jax.ShapeDtypeStruct((1, 16), F32)],
        scratch_shapes=[pltpu.VMEM((1, 128), F32)],
        compiler_params=_cparams(("arbitrary", "arbitrary")),
    )(dc, sigf)


CONV_TILE = 256
CONV_ROWS = 256


def _conv_taps(u_ref, i, w, bias):
    r0 = pl.multiple_of(i * CONV_ROWS, CONV_ROWS)
    cur = u_ref[pl.ds(r0, CONV_ROWS), :]
    p0 = pl.multiple_of(jnp.maximum(r0 - 8, 0), 8)
    prev = jnp.where(i > 0, u_ref[pl.ds(p0, 8), :], 0.0)
    cat = jnp.concatenate([prev, cur], axis=0)
    pre = bias + w[3:4, :] * cur
    taps = [cur]
    for s in (1, 2, 3):
        sh = pltpu.roll(cat, s, 0)[8:, :]
        taps.append(sh)
        pre = pre + w[3 - s:4 - s, :] * sh
    return r0, pre, taps


def _conv_fwd(proj_a, conv_w, conv_b, bl, t):
    n = bl * t
    nct = CONV_CH // CONV_TILE
    c0 = SSD_WIDTH // CONV_TILE

    def body(u_ref, w_ref, b_ref, o_ref):
        w = w_ref[...]
        bias = b_ref[...]

        def chunk(i, carry):
            r0, pre, _ = _conv_taps(u_ref, i, w, bias)
            o_ref[pl.ds(r0, CONV_ROWS), :] = pre * jax.nn.sigmoid(pre)
            return carry

        lax.fori_loop(0, t // CONV_ROWS, chunk, 0)

    return pl.pallas_call(
        body, name="conv_silu_fwd", grid=(bl, nct),
        in_specs=[pl.BlockSpec((t, CONV_TILE), lambda b, c: (b, c0 + c)),
                  pl.BlockSpec((4, CONV_TILE), lambda b, c: (0, c)),
                  pl.BlockSpec((1, CONV_TILE), lambda b, c: (0, c))],
        out_specs=pl.BlockSpec((t, CONV_TILE), lambda b, c: (b, c)),
        out_shape=jax.ShapeDtypeStruct((n, CONV_CH), F32),
        compiler_params=_cparams(("parallel", "parallel")),
    )(proj_a, conv_w, conv_b)


def _conv_bwd(dxc, proj_a, conv_w, conv_b, bl, t):
    n = bl * t
    nct = CONV_CH // CONV_TILE
    c0 = SSD_WIDTH // CONV_TILE
    nrc = t // CONV_ROWS

    def body(g_ref, u_ref, w_ref, b_ref, du_ref, dw_ref, db_ref, dp_scr):
        @pl.when(pl.program_id(1) == 0)
        def _():
            dw_ref[...] = jnp.zeros_like(dw_ref)
            db_ref[...] = jnp.zeros_like(db_ref)

        w = w_ref[...]
        bias = b_ref[...]
        dp_scr[pl.ds(t, 8), :] = jnp.zeros((8, CONV_TILE), F32)

        def chunk1(i, carry):
            dw0, dw1, dw2, dw3, db = carry
            r0, pre, taps = _conv_taps(u_ref, i, w, bias)
            sg = jax.nn.sigmoid(pre)
            dpre = g_ref[pl.ds(r0, CONV_ROWS), :] * (sg * (1.0 + pre * (1.0 - sg)))
            dp_scr[pl.ds(r0, CONV_ROWS), :] = dpre
            dw3 = dw3 + jnp.sum(dpre * taps[0], axis=0, keepdims=True)
            dw2 = dw2 + jnp.sum(dpre * taps[1], axis=0, keepdims=True)
            dw1 = dw1 + jnp.sum(dpre * taps[2], axis=0, keepdims=True)
            dw0 = dw0 + jnp.sum(dpre * taps[3], axis=0, keepdims=True)
            db = db + jnp.sum(dpre, axis=0, keepdims=True)
            return dw0, dw1, dw2, dw3, db

        z = jnp.zeros((1, CONV_TILE), F32)
        dw0, dw1, dw2, dw3, db = lax.fori_loop(0, nrc, chunk1, (z, z, z, z, z))
        dw_ref[...] += jnp.concatenate([dw0, dw1, dw2, dw3], axis=0)
        db_ref[...] += db

        def chunk2(i, carry):
            r0 = pl.multiple_of(i * CONV_ROWS, CONV_ROWS)
            cat = dp_scr[pl.ds(r0, CONV_ROWS + 8), :]
            du = w[3:4, :] * cat[:CONV_ROWS, :]
            for s in (1, 2, 3):
                du = du + w[3 - s:4 - s, :] * pltpu.roll(cat, CONV_ROWS + 8 - s, 0)[:CONV_ROWS, :]
            du_ref[pl.ds(r0, CONV_ROWS), :] = du.astype(BF16)
            return carry

        lax.fori_loop(0, nrc, chunk2, 0)

    return pl.pallas_call(
        body, name="conv_silu_bwd", grid=(nct, bl),
        in_specs=[pl.BlockSpec((t, CONV_TILE), lambda c, b: (b, c)),
                  pl.BlockSpec((t, CONV_TILE), lambda c, b: (b, c0 + c)),
                  pl.BlockSpec((4, CONV_TILE), lambda c, b: (0, c)),
                  pl.BlockSpec((1, CONV_TILE), lambda c, b: (0, c))],
        out_specs=[pl.BlockSpec((t, CONV_TILE), lambda c, b: (b, c)),
                   pl.BlockSpec((4, CONV_TILE), lambda c, b: (0, c)),
                   pl.BlockSpec((1, CONV_TILE), lambda c, b: (0, c))],
        out_shape=[jax.ShapeDtypeStruct((n, CONV_CH), BF16), jax.ShapeDtypeStruct((4, CONV_CH), F32),
                   jax.ShapeDtypeStruct((1, CONV_CH), F32)],
        scratch_shapes=[pltpu.VMEM((t + 8, CONV_TILE), F32)],
        compiler_params=_cparams(("parallel", "arbitrary")),
    )(dxc, proj_a, conv_w, conv_b)


NT_DIMS = (((1,), (1,)), ((), ()))
TN_DIMS = (((0,), (0,)), ((), ()))


def _dot(a, b, dims=None):
    if dims is None:
        return jnp.dot(a, b, preferred_element_type=F32)
    return lax.dot_general(a, b, dims, preferred_element_type=F32)


def _ssd_fwd(xc, proj_a, dt, acum, acum_t, dskip_e, norm_w, bl, t):
    n = bl * t
    nch = t // CHUNK
    L = CHUNK

    def body(xc_ref, z_ref, dt_ref, ac_ref, act_ref, dsk_ref, nw_ref, ys_ref, yp_ref, hp_ref, h_scr, y_scr):
        @pl.when(pl.program_id(1) == 0)
        def _():
            h_scr[...] = jnp.zeros_like(h_scr)

        row = lax.broadcasted_iota(jnp.int32, (L, L), 0)
        col = lax.broadcasted_iota(jnp.int32, (L, L), 1)
        causal = row >= col
        dt_all = dt_ref[...]
        ac_all = ac_ref[...]
        act_all = act_ref[...]
        for g in range(2):
            bg = xc_ref[:, SSD_WIDTH + g * 128:SSD_WIDTH + (g + 1) * 128].astype(BF16)
            cg = xc_ref[:, SSD_WIDTH + 256 + g * 128:SSD_WIDTH + 256 + (g + 1) * 128].astype(BF16)
            gmat = _dot(cg, bg, NT_DIMS)
            for r in range(8):
                h = g * 8 + r
                sl = slice(h * HEAD_DIM, (h + 1) * HEAD_DIM)
                xs = xc_ref[:, sl]
                xdt = xs * dt_all[:, h:h + 1]
                ac = ac_all[:, h:h + 1]
                ar = act_all[h:h + 1, :]
                ldec = jnp.exp(jnp.where(causal, ac - ar, NEG))
                m = (gmat * ldec).astype(BF16)
                hp = h_scr[h]
                hp_ref[h] = hp
                yd = _dot(m, xdt.astype(BF16))
                yo = _dot(cg, hp.astype(BF16), NT_DIMS) * jnp.exp(ac)
                y_scr[:, sl] = yd + yo + dsk_ref[:, sl] * xs
                alast = ac_all[L - 1:L, h:h + 1]
                xd = (xdt * jnp.exp(alast - ac)).astype(BF16)
                h_scr[h] = jnp.exp(alast) * hp + _dot(xd, bg, TN_DIMS)
        y = y_scr[...]
        yp_ref[...] = y
        zv = z_ref[...]
        yg = y * (zv * jax.nn.sigmoid(zv))
        for g in range(2):
            gs = slice(g * 512, (g + 1) * 512)
            grp = yg[:, gs]
            rstd = lax.rsqrt(jnp.mean(grp * grp, axis=1, keepdims=True) + EPS)
            ys_ref[:, gs] = (grp * rstd * nw_ref[:, gs]).astype(BF16)

    rb = lambda b, c: (b * nch + c, 0)
    v1k = pl.BlockSpec((1, SSD_WIDTH), lambda b, c: (0, 0))
    return pl.pallas_call(
        body, name="ssd_fwd", grid=(bl, nch),
        in_specs=[pl.BlockSpec((L, CONV_CH), rb), pl.BlockSpec((L, SSD_WIDTH), rb),
                  pl.BlockSpec((L, 16), rb), pl.BlockSpec((L, 16), rb),
                  pl.BlockSpec((16, L), lambda b, c: (0, b * nch + c)), v1k, v1k],
        out_specs=[pl.BlockSpec((L, SSD_WIDTH), rb), pl.BlockSpec((L, SSD_WIDTH), rb),
                   pl.BlockSpec((None, 16, HEAD_DIM, SSD_STATE), lambda b, c: (b * nch + c, 0, 0, 0))],
        out_shape=[jax.ShapeDtypeStruct((n, SSD_WIDTH), BF16), jax.ShapeDtypeStruct((n, SSD_WIDTH), F32),
                   jax.ShapeDtypeStruct((bl * nch, 16, HEAD_DIM, SSD_STATE), F32)],
        scratch_shapes=[pltpu.VMEM((16, HEAD_DIM, SSD_STATE), F32), pltpu.VMEM((L, SSD_WIDTH), F32)],
        compiler_params=_cparams(("parallel", "arbitrary")),
    )(xc, proj_a, dt, acum, acum_t, dskip_e, norm_w)


def _ssd_bwd(dys, xc, proj_a, ypre, hprev, dt, sig, acum, acum_t, a_log, dskip_e, norm_w, bl, t):
    n = bl * t
    nch = t // CHUNK
    L = CHUNK

    def body(dys_ref, xc_ref, z_ref, yp_ref, hp_ref, dt_ref, sg_ref, ac_ref, act_ref, al_ref, dsk_ref, nw_ref,
             dxc_ref, dz_ref, ddt_ref, dnw_ref, dsk16_ref, da16_ref, db16_ref, dh_scr, dy_scr):
        first = (pl.program_id(0) == 0) & (pl.program_id(1) == 0)

        @pl.when(first)
        def _():
            dnw_ref[...] = jnp.zeros_like(dnw_ref)
            dsk16_ref[...] = jnp.zeros_like(dsk16_ref)
            da16_ref[...] = jnp.zeros_like(da16_ref)
            db16_ref[...] = jnp.zeros_like(db16_ref)

        @pl.when(pl.program_id(1) == 0)
        def _():
            dh_scr[...] = jnp.zeros_like(dh_scr)

        y = yp_ref[...]
        zv = z_ref[...]
        sz = jax.nn.sigmoid(zv)
        gate = zv * sz
        yg = y * gate
        dout = dys_ref[...]
        nw = nw_ref[...]
        for g in range(2):
            gs = slice(g * 512, (g + 1) * 512)
            grp = yg[:, gs]
            rstd = lax.rsqrt(jnp.mean(grp * grp, axis=1, keepdims=True) + EPS)
            ghat = grp * rstd
            dnw_ref[:, gs] += jnp.sum(dout[:, gs] * ghat, axis=0, keepdims=True)
            gw = dout[:, gs] * nw[:, gs]
            dyg = rstd * (gw - ghat * jnp.mean(gw * ghat, axis=1, keepdims=True))
            dy_scr[:, gs] = dyg * gate[:, gs]
            dz_ref[:, gs] = (dyg * y[:, gs] * (sz[:, gs] * (1.0 + zv[:, gs] * (1.0 - sz[:, gs])))).astype(BF16)

        row = lax.broadcasted_iota(jnp.int32, (L, L), 0)
        col = lax.broadcasted_iota(jnp.int32, (L, L), 1)
        causal = row >= col
        lane16 = lax.broadcasted_iota(jnp.int32, (1, 16), 1)
        lane128 = lax.broadcasted_iota(jnp.int32, (1, L), 1)
        last_row = lax.broadcasted_iota(jnp.int32, (L, 1), 0) == (L - 1)
        dt_all = dt_ref[...]
        ac_all = ac_ref[...]
        act_all = act_ref[...]
        dac_col = jnp.zeros((L, L), F32)
        dac_row = jnp.zeros((L, L), F32)
        ddt_x = jnp.zeros((L, L), F32)
        dsk16 = jnp.zeros((1, 16), F32)
        rows16 = lax.broadcasted_iota(jnp.int32, (L, 1), 0)
        for g in range(2):
            bsl = slice(SSD_WIDTH + g * 128, SSD_WIDTH + (g + 1) * 128)
            csl = slice(SSD_WIDTH + 256 + g * 128, SSD_WIDTH + 256 + (g + 1) * 128)
            bg = xc_ref[:, bsl].astype(BF16)
            cg = xc_ref[:, csl].astype(BF16)
            gmat = _dot(cg, bg, NT_DIMS)
            dg_sum = jnp.zeros((L, L), F32)
            dc_acc = jnp.zeros((L, SSD_STATE), F32)
            db_acc = jnp.zeros((L, SSD_STATE), F32)
            for r in range(8):
                h = g * 8 + r
                sl = slice(h * HEAD_DIM, (h + 1) * HEAD_DIM)
                onehot = lane16 == h
                onehot_w = lane128 == h
                xs = xc_ref[:, sl]
                dth = dt_all[:, h:h + 1]
                xdt = xs * dth
                xb = xdt.astype(BF16)
                ac = ac_all[:, h:h + 1]
                ar = act_all[h:h + 1, :]
                alast = ac_all[L - 1:L, h:h + 1]
                ldec = jnp.exp(jnp.where(causal, ac - ar, NEG))
                mf = gmat * ldec
                e_in = jnp.exp(ac)
                dec = jnp.exp(alast - ac)
                elast = jnp.exp(alast)
                hp = hp_ref[h]
                hpb = hp.astype(BF16)
                dyh = dy_scr[:, sl]
                dyb = dyh.astype(BF16)
                dsk16 = dsk16 + jnp.where(onehot, jnp.sum(jnp.sum(dyh * xs, axis=1, keepdims=True), axis=0, keepdims=True), 0.0)
                dm = _dot(dyb, xb, NT_DIMS)
                dx = _dot(mf.astype(BF16), dyb, TN_DIMS)
                dg_sum = dg_sum + dm * ldec
                wmat = dm * mf
                dac_h = jnp.sum(wmat, axis=1, keepdims=True)
                dac_row = dac_row + jnp.where(rows16 == h, -jnp.sum(wmat, axis=0, keepdims=True), 0.0)
                ch = _dot(cg, hpb, NT_DIMS)
                dye = dyh * e_in
                dyeb = dye.astype(BF16)
                dc_acc = dc_acc + _dot(dyeb, hpb)
                dhp = _dot(dyeb, cg, TN_DIMS)
                dac_h = dac_h + jnp.sum(dye * ch, axis=1, keepdims=True)
                ds = dh_scr[h]
                dsb = ds.astype(BF16)
                dxd = _dot(bg, dsb, NT_DIMS)
                db_acc = db_acc + _dot((xdt * dec).astype(BF16), dsb)
                dx = dx + dxd * dec
                ddec = jnp.sum(dxd * xdt, axis=1, keepdims=True) * dec
                extra = (jnp.sum(ddec, axis=0, keepdims=True)
                         + elast * jnp.sum(jnp.sum(hp * ds, axis=1, keepdims=True), axis=0, keepdims=True))
                dac_h = dac_h - ddec + jnp.where(last_row, extra, 0.0)
                dh_scr[h] = elast * ds + dhp
                dac_col = dac_col + jnp.where(onehot_w, dac_h, 0.0)
                ddt_x = ddt_x + jnp.where(onehot_w, jnp.sum(dx * xs, axis=1, keepdims=True), 0.0)
                dxc_ref[:, sl] = dx * dth + dsk_ref[:, sl] * dyh
            dgb = dg_sum.astype(BF16)
            dxc_ref[:, csl] = dc_acc + _dot(dgb, bg)
            dxc_ref[:, bsl] = db_acc + _dot(dgb, cg, TN_DIMS)
        dac = dac_col + jnp.transpose(dac_row)
        triu = (row <= col).astype(F32)
        da = jnp.dot(triu, dac, precision=HIGHEST, preferred_element_type=F32)[:, 0:16]
        a_row = -jnp.exp(al_ref[...])
        ddt = (ddt_x[:, 0:16] + da * a_row) * sg_ref[...]
        ddt_ref[...] = ddt
        dsk16_ref[...] += dsk16
        da16_ref[...] += jnp.sum(da * dt_all, axis=0, keepdims=True) * a_row
        db16_ref[...] += jnp.sum(ddt, axis=0, keepdims=True)

    rb = lambda b, c: (b * nch + nch - 1 - c, 0)
    v1k = pl.BlockSpec((1, SSD_WIDTH), lambda b, c: (0, 0))
    v16 = pl.BlockSpec((1, 16), lambda b, c: (0, 0))
    wide = pl.BlockSpec((L, SSD_WIDTH), rb)
    s16 = pl.BlockSpec((L, 16), rb)
    return pl.pallas_call(
        body, name="ssd_bwd", grid=(bl, nch),
        in_specs=[wide, pl.BlockSpec((L, CONV_CH), rb), wide, wide,
                  pl.BlockSpec((None, 16, HEAD_DIM, SSD_STATE), lambda b, c: (b * nch + nch - 1 - c, 0, 0, 0)),
                  s16, s16, s16, pl.BlockSpec((16, L), lambda b, c: (0, b * nch + nch - 1 - c)), v16, v1k, v1k],
        out_specs=[pl.BlockSpec((L, CONV_CH), rb), wide, s16, v1k, v16, v16, v16],
        out_shape=[jax.ShapeDtypeStruct((n, CONV_CH), F32), jax.ShapeDtypeStruct((n, SSD_WIDTH), BF16),
                   jax.ShapeDtypeStruct((n, 16), F32), jax.ShapeDtypeStruct((1, SSD_WIDTH), F32),
                   jax.ShapeDtypeStruct((1, 16), F32), jax.ShapeDtypeStruct((1, 16), F32),
                   jax.ShapeDtypeStruct((1, 16), F32)],
        scratch_shapes=[pltpu.VMEM((16, HEAD_DIM, SSD_STATE), F32), pltpu.VMEM((L, SSD_WIDTH), F32)],
        compiler_params=_cparams(("arbitrary", "arbitrary")),
    )(dys, xc, proj_a, ypre, hprev, dt, sig, acum, acum_t, a_log, dskip_e, norm_w)


def _head_expander():
    r = lax.broadcasted_iota(jnp.int32, (128, SSD_WIDTH), 0)
    c = lax.broadcasted_iota(jnp.int32, (128, SSD_WIDTH), 1)
    return ((c // HEAD_DIM == r % 16) & (r < 48)).astype(BF16)


def _spread(v128, expander):
    hi = v128.astype(BF16).astype(F32)
    r1 = v128 - hi
    mid = r1.astype(BF16).astype(F32)
    lo = (r1 - mid).astype(BF16).astype(F32)
    packed = (hi + pltpu.roll(mid, 16, 1) + pltpu.roll(lo, 32, 1)).astype(BF16)
    return jnp.dot(packed, expander, preferred_element_type=F32)


def _head_sums(v1024, expander):
    hi = v1024.astype(BF16)
    lo = (v1024 - hi.astype(F32)).astype(BF16)
    heads = jnp.where(lax.broadcasted_iota(jnp.int32, (128, SSD_WIDTH), 0) < 16, expander, jnp.zeros_like(expander))
    return _dot(hi, heads, NT_DIMS) + _dot(lo, heads, NT_DIMS)


def _ssd_fwd2(xc, proj_a, dt, acum, acum_t, dskip_e, norm_w, bl, t):
    n = bl * t
    nch = t // CHUNK
    L = CHUNK

    def body(xc_ref, z_ref, dt_ref, ac_ref, act_ref, dsk_ref, nw_ref, ys_ref, yp_ref, hp_ref, h_scr, y_scr, x_scr):
        @pl.when(pl.program_id(1) == 0)
        def _():
            h_scr[...] = jnp.zeros_like(h_scr)

        row = lax.broadcasted_iota(jnp.int32, (L, L), 0)
        col = lax.broadcasted_iota(jnp.int32, (L, L), 1)
        causal = row >= col
        expander = _head_expander()
        ac_all = ac_ref[...]
        act_all = act_ref[...]
        ac_e = _spread(ac_all, expander)
        e_in = jnp.exp(ac_e)
        dec = jnp.exp(ac_e[L - 1:L, :] - ac_e)
        xs_all = xc_ref[:, 0:SSD_WIDTH]
        x_all = xs_all * _spread(dt_ref[...], expander)
        x_scr[...] = x_all.astype(BF16)
        hp_all = h_scr[...]
        hp_ref[...] = hp_all
        for g in range(2):
            gs = slice(g * 512, (g + 1) * 512)
            bg = xc_ref[:, SSD_WIDTH + g * 128:SSD_WIDTH + (g + 1) * 128].astype(BF16)
            cg = xc_ref[:, SSD_WIDTH + 256 + g * 128:SSD_WIDTH + 256 + (g + 1) * 128].astype(BF16)
            gmat = _dot(cg, bg, NT_DIMS)
            y_scr[:, gs] = (_dot(cg, hp_all[gs, :].astype(BF16), NT_DIMS) * e_in[:, gs]
                            + dsk_ref[:, gs] * xs_all[:, gs])
            s_new = _dot((x_all[:, gs] * dec[:, gs]).astype(BF16), bg, TN_DIMS)
            for r in range(8):
                h = g * 8 + r
                sl = slice(h * HEAD_DIM, (h + 1) * HEAD_DIM)
                ldec = jnp.exp(jnp.where(causal, ac_all[:, h:h + 1] - act_all[h:h + 1, :], NEG))
                y_scr[:, sl] += _dot((gmat * ldec).astype(BF16), x_scr[:, sl])
                elast = jnp.exp(ac_all[L - 1:L, h:h + 1])
                h_scr[sl, :] = elast * hp_all[sl, :] + s_new[r * HEAD_DIM:(r + 1) * HEAD_DIM, :]
        y = y_scr[...]
        yp_ref[...] = y
        zv = z_ref[...]
        yg = y * (zv * jax.nn.sigmoid(zv))
        for g in range(2):
            gs = slice(g * 512, (g + 1) * 512)
            grp = yg[:, gs]
            rstd = lax.rsqrt(jnp.mean(grp * grp, axis=1, keepdims=True) + EPS)
            ys_ref[:, gs] = (grp * rstd * nw_ref[:, gs]).astype(BF16)

    rb = lambda b, c: (b * nch + c, 0)
    v1k = pl.BlockSpec((1, SSD_WIDTH), lambda b, c: (0, 0))
    return pl.pallas_call(
        body, name="ssd_fwd", grid=(bl, nch),
        in_specs=[pl.BlockSpec((L, CONV_CH), rb), pl.BlockSpec((L, SSD_WIDTH), rb),
                  pl.BlockSpec((L, 128), rb), pl.BlockSpec((L, 128), rb),
                  pl.BlockSpec((16, L), lambda b, c: (0, b * nch + c)), v1k, v1k],
        out_specs=[pl.BlockSpec((L, SSD_WIDTH), rb), pl.BlockSpec((L, SSD_WIDTH), rb),
                   pl.BlockSpec((None, SSD_WIDTH, SSD_STATE), lambda b, c: (b * nch + c, 0, 0))],
        out_shape=[jax.ShapeDtypeStruct((n, SSD_WIDTH), BF16), jax.ShapeDtypeStruct((n, SSD_WIDTH), F32),
                   jax.ShapeDtypeStruct((bl * nch, SSD_WIDTH, SSD_STATE), F32)],
        scratch_shapes=[pltpu.VMEM((SSD_WIDTH, SSD_STATE), F32), pltpu.VMEM((L, SSD_WIDTH), F32),
                        pltpu.VMEM((L, SSD_WIDTH), BF16)],
        compiler_params=_cparams(("parallel", "arbitrary")),
    )(xc, proj_a, dt, acum, acum_t, dskip_e, norm_w)


def _ssd_bwd2(dys, xc, proj_a, ypre, hprev, dt, sig, acum, acum_t, a_log, dskip_e, norm_w, bl, t):
    n = bl * t
    nch = t // CHUNK
    L = CHUNK

    def body(dys_ref, xc_ref, z_ref, yp_ref, hp_ref, dt_ref, sg_ref, ac_ref, act_ref, al_ref, dsk_ref, nw_ref,
             dxc_ref, dz_ref, ddt_ref, dnw_ref, dsk16_ref, da16_ref, db16_ref,
             dh_scr, dy_scr, x_scr, dx_scr, red_scr):
        first = (pl.program_id(0) == 0) & (pl.program_id(1) == 0)

        @pl.when(first)
        def _():
            dnw_ref[...] = jnp.zeros_like(dnw_ref)
            dsk16_ref[...] = jnp.zeros_like(dsk16_ref)
            da16_ref[...] = jnp.zeros_like(da16_ref)
            db16_ref[...] = jnp.zeros_like(db16_ref)

        @pl.when(pl.program_id(1) == 0)
        def _():
            dh_scr[...] = jnp.zeros_like(dh_scr)

        y = yp_ref[...]
        zv = z_ref[...]
        sz = jax.nn.sigmoid(zv)
        gate = zv * sz
        yg = y * gate
        dout = dys_ref[...]
        nw = nw_ref[...]
        for g in range(2):
            gs = slice(g * 512, (g + 1) * 512)
            grp = yg[:, gs]
            rstd = lax.rsqrt(jnp.mean(grp * grp, axis=1, keepdims=True) + EPS)
            ghat = grp * rstd
            dnw_ref[:, gs] += jnp.sum(dout[:, gs] * ghat, axis=0, keepdims=True)
            gw = dout[:, gs] * nw[:, gs]
            dyg = rstd * (gw - ghat * jnp.mean(gw * ghat, axis=1, keepdims=True))
            dy_scr[:, gs] = dyg * gate[:, gs]
            dz_ref[:, gs] = (dyg * y[:, gs] * (sz[:, gs] * (1.0 + zv[:, gs] * (1.0 - sz[:, gs])))).astype(BF16)

        row = lax.broadcasted_iota(jnp.int32, (L, L), 0)
        col = lax.broadcasted_iota(jnp.int32, (L, L), 1)
        causal = row >= col
        lane128 = lax.broadcasted_iota(jnp.int32, (1, L), 1)
        rows128 = lax.broadcasted_iota(jnp.int32, (L, 1), 0)
        last_row = rows128 == (L - 1)
        expander = _head_expander()
        ac_all = ac_ref[...]
        act_all = act_ref[...]
        dt_all = dt_ref[...]
        dt_e = _spread(dt_all, expander)
        ac_e = _spread(ac_all, expander)
        e_in = jnp.exp(ac_e)
        dec = jnp.exp(ac_e[L - 1:L, :] - ac_e)
        xs_all = xc_ref[:, 0:SSD_WIDTH]
        x_all = xs_all * dt_e
        x_scr[...] = x_all.astype(BF16)
        dy_all = dy_scr[...]
        hp_all = hp_ref[...]
        ds_all = dh_scr[...]
        dsk_cols = jnp.sum(dy_all * xs_all, axis=0, keepdims=True)
        dac = jnp.zeros((L, L), F32)
        dac_row = jnp.zeros((L, L), F32)
        ddec_cols = []
        for g in range(2):
            gs = slice(g * 512, (g + 1) * 512)
            bsl = slice(SSD_WIDTH + g * 128, SSD_WIDTH + (g + 1) * 128)
            csl = slice(SSD_WIDTH + 256 + g * 128, SSD_WIDTH + 256 + (g + 1) * 128)
            bg = xc_ref[:, bsl].astype(BF16)
            cg = xc_ref[:, csl].astype(BF16)
            gmat = _dot(cg, bg, NT_DIMS)
            hpb = hp_all[gs, :].astype(BF16)
            dsb = ds_all[gs, :].astype(BF16)
            ch = _dot(cg, hpb, NT_DIMS)
            dye = dy_all[:, gs] * e_in[:, gs]
            dyeb = dye.astype(BF16)
            dc_acc = _dot(dyeb, hpb)
            dhp = _dot(dyeb, cg, TN_DIMS)
            dxd = _dot(bg, dsb, NT_DIMS)
            db_acc = _dot((x_all[:, gs] * dec[:, gs]).astype(BF16), dsb)
            ddec = dxd * x_all[:, gs] * dec[:, gs]
            ddec_cols.append(jnp.sum(ddec, axis=0, keepdims=True))
            dx_scr[:, gs] = dxd * dec[:, gs]
            red_scr[:, gs] = dye * ch - ddec
            dg_sum = jnp.zeros((L, L), F32)
            for r in range(8):
                h = g * 8 + r
                sl = slice(h * HEAD_DIM, (h + 1) * HEAD_DIM)
                onehot_w = lane128 == h
                ldec = jnp.exp(jnp.where(causal, ac_all[:, h:h + 1] - act_all[h:h + 1, :], NEG))
                mf = gmat * ldec
                dyb = dy_scr[:, sl].astype(BF16)
                dm = _dot(dyb, x_scr[:, sl], NT_DIMS)
                dx_scr[:, sl] += _dot(mf.astype(BF16), dyb, TN_DIMS)
                dg_sum = dg_sum + dm * ldec
                wmat = dm * mf
                elast = jnp.exp(ac_all[L - 1:L, h:h + 1])
                hp_h = hp_all[sl, :]
                ds_h = ds_all[sl, :]
                extra = elast * jnp.sum(jnp.sum(hp_h * ds_h, axis=1, keepdims=True), axis=0, keepdims=True)
                dac = dac + jnp.where(onehot_w, jnp.sum(wmat, axis=1, keepdims=True) + jnp.where(last_row, extra, 0.0),
                                      0.0)
                dac_row = dac_row + jnp.where(rows128 == h, -jnp.sum(wmat, axis=0, keepdims=True), 0.0)
                dh_scr[sl, :] = elast * ds_h + dhp[r * HEAD_DIM:(r + 1) * HEAD_DIM, :]
            dgb = dg_sum.astype(BF16)
            dxc_ref[:, csl] = dc_acc + _dot(dgb, bg)
            dxc_ref[:, bsl] = db_acc + _dot(dgb, cg, TN_DIMS)
        dx_all = dx_scr[...]
        dxc_ref[:, 0:SSD_WIDTH] = dx_all * dt_e + dsk_ref[...] * dy_all
        red = red_scr[...]
        dac_slab = _head_sums(red, expander)
        ddec_tot = _head_sums(jnp.broadcast_to(jnp.concatenate(ddec_cols, axis=1), (8, SSD_WIDTH)), expander)
        ddt_x = _head_sums(dx_all * xs_all, expander)
        dsk16_ref[...] += _head_sums(jnp.broadcast_to(dsk_cols, (8, SSD_WIDTH)), expander)[0:1, 0:16]
        dac = dac + dac_slab + jnp.transpose(dac_row) + jnp.where(last_row, ddec_tot[0:1, :], 0.0)
        triu = (row <= col).astype(F32)
        da = jnp.dot(triu, dac, precision=HIGHEST, preferred_element_type=F32)[:, 0:16]
        a_row = -jnp.exp(al_ref[...])
        dt16 = dt_all[:, 0:16]
        ddt = (ddt_x[:, 0:16] + da * a_row) * sg_ref[...]
        ddt_ref[...] = ddt
        da16_ref[...] += jnp.sum(da * dt16, axis=0, keepdims=True) * a_row
        db16_ref[...] += jnp.sum(ddt, axis=0, keepdims=True)

    rb = lambda b, c: (b * nch + nch - 1 - c, 0)
    v1k = pl.BlockSpec((1, SSD_WIDTH), lambda b, c: (0, 0))
    v16 = pl.BlockSpec((1, 16), lambda b, c: (0, 0))
    wide = pl.BlockSpec((L, SSD_WIDTH), rb)
    s16 = pl.BlockSpec((L, 16), rb)
    s128 = pl.BlockSpec((L, 128), rb)
    return pl.pallas_call(
        body, name="ssd_bwd", grid=(bl, nch),
        in_specs=[wide, pl.BlockSpec((L, CONV_CH), rb), wide, wide,
                  pl.BlockSpec((None, SSD_WIDTH, SSD_STATE), lambda b, c: (b * nch + nch - 1 - c, 0, 0)),
                  s128, s16, s128, pl.BlockSpec((16, L), lambda b, c: (0, b * nch + nch - 1 - c)), v16, v1k, v1k],
        out_specs=[pl.BlockSpec((L, CONV_CH), rb), wide, s16, v1k, v16, v16, v16],
        out_shape=[jax.ShapeDtypeStruct((n, CONV_CH), F32), jax.ShapeDtypeStruct((n, SSD_WIDTH), BF16),
                   jax.ShapeDtypeStruct((n, 16), F32), jax.ShapeDtypeStruct((1, SSD_WIDTH), F32),
                   jax.ShapeDtypeStruct((1, 16), F32), jax.ShapeDtypeStruct((1, 16), F32),
                   jax.ShapeDtypeStruct((1, 16), F32)],
        scratch_shapes=[pltpu.VMEM((SSD_WIDTH, SSD_STATE), F32), pltpu.VMEM((L, SSD_WIDTH), F32),
                        pltpu.VMEM((L, SSD_WIDTH), BF16), pltpu.VMEM((L, SSD_WIDTH), F32),
                        pltpu.VMEM((L, SSD_WIDTH), F32)],
        compiler_params=_cparams(("arbitrary", "arbitrary")),
    )(dys, xc, proj_a, ypre, hprev, dt, sig, acum, acum_t, a_log, dskip_e, norm_w)


def _attn_fwd(qkv, negc, bl, t):
    n = bl * t
    tb_ = ATT_BLOCK
    nb = t // tb_
    scale2 = LOG2E / math.sqrt(HEAD_DIM)

    def body(q_ref, k_ref, v_ref, c_ref, o_ref, lse_ref):
        row = lax.broadcasted_iota(jnp.int32, (tb_, tb_), 0)
        col = lax.broadcasted_iota(jnp.int32, (tb_, tb_), 1)
        causal = row >= col
        for qi in range(nb):
            r0, lk = qi * tb_, (qi + 1) * tb_
            for j in range(2):
                sl = slice(j * HEAD_DIM, (j + 1) * HEAD_DIM)
                s = _dot(q_ref[r0:lk, sl], k_ref[0:lk, sl], NT_DIMS) * scale2 + c_ref[j:j + 1, 0:lk] * LOG2E
                tail = jnp.where(causal, s[:, r0:lk], NEG)
                s = tail if qi == 0 else jnp.concatenate([s[:, 0:r0], tail], axis=1)
                m = jnp.max(s, axis=1, keepdims=True)
                p = jnp.exp2(s - m)
                l = jnp.sum(p, axis=1, keepdims=True)
                acc = _dot(p.astype(BF16), v_ref[0:lk, sl])
                o_ref[r0:lk, sl] = (acc / l).astype(BF16)
                lse_ref[r0:lk, sl] = jnp.broadcast_to(m + jnp.log(l) * LOG2E, (tb_, HEAD_DIM))

    blk = lambda off: pl.BlockSpec((t, 128), lambda b, hp: (b, off + hp))
    return pl.pallas_call(
        body, name="fox_attn_fwd", grid=(bl, 8),
        in_specs=[blk(0), blk(8), blk(16), pl.BlockSpec((None, None, 8, t), lambda b, hp: (b, hp, 0, 0))],
        out_specs=[blk(0), blk(0)],
        out_shape=[jax.ShapeDtypeStruct((n, ATT_WIDTH), BF16), jax.ShapeDtypeStruct((n, ATT_WIDTH), F32)],
        compiler_params=_cparams(("parallel", "parallel")),
    )(qkv, qkv, qkv, negc)


def _attn_bwd(qkv, do, o, lse, negc, after, bl, t):
    n = bl * t
    tb_ = ATT_BLOCK
    nb = t // tb_
    scale = 1.0 / math.sqrt(HEAD_DIM)
    scale2 = LOG2E * scale

    def body(q_ref, k_ref, v_ref, do_ref, o_ref, lse_ref, c_ref, after_ref, dq_ref, dk_ref, dv_ref, dc_ref,
             dq_scr, delta_scr, dr_scr, qt_scr, dot_scr, dkt_scr, dvt_scr):
        row = lax.broadcasted_iota(jnp.int32, (tb_, tb_), 0)
        col = lax.broadcasted_iota(jnp.int32, (tb_, tb_), 1)
        causal = row >= col
        dq_scr[...] = jnp.zeros_like(dq_scr)
        dr_scr[...] = jnp.zeros_like(dr_scr)
        dc_ref[...] = jnp.zeros_like(dc_ref)
        qt_scr[...] = jnp.transpose(q_ref[...].astype(F32)).astype(BF16)
        dot_scr[...] = jnp.transpose(do_ref[...].astype(F32)).astype(BF16)
        prod = do_ref[...].astype(F32) * o_ref[...].astype(F32)
        for j in range(2):
            sl = slice(j * HEAD_DIM, (j + 1) * HEAD_DIM)
            delta_scr[:, sl] = jnp.broadcast_to(jnp.sum(prod[:, sl], axis=1, keepdims=True), (t, HEAD_DIM))
        for kj in range(nb):
            r0, r1 = kj * tb_, (kj + 1) * tb_
            for j in range(2):
                sl = slice(j * HEAD_DIM, (j + 1) * HEAD_DIM)
                one = slice(j * HEAD_DIM, j * HEAD_DIM + 1)
                kb = k_ref[r0:r1, sl]
                qs = q_ref[r0:t, sl]
                dos = do_ref[r0:t, sl]
                s = _dot(qs, kb, NT_DIMS) * scale2 + c_ref[j:j + 1, r0:r1] * LOG2E
                head = jnp.where(causal, s[0:tb_, :], NEG)
                s = head if kj == nb - 1 else jnp.concatenate([head, s[tb_:, :]], axis=0)
                p = jnp.exp2(s - lse_ref[r0:t, one])
                dp = _dot(dos, v_ref[r0:r1, sl], NT_DIMS)
                ds = p * (dp - delta_scr[r0:t, one])
                dsb = ds.astype(BF16)
                dvt_scr[sl, r0:r1] = _dot(dot_scr[sl, r0:t], p.astype(BF16))
                dkt_scr[sl, r0:r1] = _dot(qt_scr[sl, r0:t], dsb)
                dq_scr[r0:t, sl] += _dot(dsb, kb)
                dr_scr[r0:t, sl] += jnp.broadcast_to(jnp.sum(ds, axis=1, keepdims=True), (t - r0, HEAD_DIM))
                dc_ref[j:j + 1, r0:r1] = -jnp.sum(ds, axis=0, keepdims=True)
        dq_ref[...] = (dq_scr[...] * scale).astype(BF16)
        dk_ref[...] = (jnp.transpose(dkt_scr[...]) * scale).astype(BF16)
        dv_ref[...] = jnp.transpose(dvt_scr[...]).astype(BF16)
        dr_t = jnp.transpose(dr_scr[...])
        for j in range(2):
            dc_ref[j:j + 1, :] += dr_t[j * HEAD_DIM:j * HEAD_DIM + 1, :]

    blk = lambda off: pl.BlockSpec((t, 128), lambda b, hp: (b, off + hp))
    cblk = pl.BlockSpec((None, None, 8, t), lambda b, hp: (b, hp, 0, 0))
    return pl.pallas_call(
        body, name="fox_attn_bwd", grid=(bl, 8),
        in_specs=[blk(0), blk(8), blk(16), blk(0), blk(0), blk(0), cblk, ANY],
        out_specs=[blk(0), blk(0), blk(0), cblk],
        out_shape=[jax.ShapeDtypeStruct((n, ATT_WIDTH), BF16)] * 3 + [jax.ShapeDtypeStruct((bl, 8, 8, t), F32)],
        scratch_shapes=[pltpu.VMEM((t, 128), F32), pltpu.VMEM((t, 128), F32), pltpu.VMEM((t, 128), F32),
                        pltpu.VMEM((128, t), BF16), pltpu.VMEM((128, t), BF16),
                        pltpu.VMEM((128, t), F32), pltpu.VMEM((128, t), F32)],
        compiler_params=_cparams(("parallel", "parallel")),
    )(qkv, qkv, qkv, do, o, lse, negc, after)


def _adamw(w, g, m, v, *, name):
    r, c = w.shape
    tr = _pick(r, (256, 128, 64, 32, 16, 8))
    bc1 = 1.0 - ADAM_B1 ** ADAM_STEP
    bc2 = 1.0 - ADAM_B2 ** ADAM_STEP

    def body(w_ref, g_ref, m_ref, v_ref, d_ref, nm_ref, nv_ref):
        gv = g_ref[...]
        mn = ADAM_B1 * m_ref[...] + (1.0 - ADAM_B1) * gv
        vn = ADAM_B2 * v_ref[...] + (1.0 - ADAM_B2) * (gv * gv)
        m_hat = mn / bc1
        v_hat = vn / bc2
        d_ref[...] = -ADAM_LR * (m_hat / (jnp.sqrt(v_hat) + ADAM_EPS) + ADAM_WD * w_ref[...])
        nm_ref[...] = mn
        nv_ref[...] = vn

    blk = pl.BlockSpec((tr, c), lambda i: (i, 0))
    return pl.pallas_call(
        body, name=name, grid=(r // tr,), in_specs=[blk] * 4, out_specs=[blk] * 3,
        out_shape=[jax.ShapeDtypeStruct((r, c), F32)] * 3,
        compiler_params=_cparams(("parallel",)),
    )(w, g, m, v)


def _sum_leading(parts, *, name, out_dtype=F32):
    k, r, c = parts.shape
    tr = _pick(r, (512, 256, 128, 96, 64, 32, 16, 8))

    def body(p_ref, o_ref):
        acc = p_ref[0].astype(F32)
        for i in range(1, k):
            acc = acc + p_ref[i].astype(F32)
        o_ref[...] = acc.astype(out_dtype)

    return pl.pallas_call(
        body, name=name, grid=(r // tr,),
        in_specs=[pl.BlockSpec((k, tr, c), lambda i: (0, i, 0))],
        out_specs=pl.BlockSpec((tr, c), lambda i: (i, 0)),
        out_shape=jax.ShapeDtypeStruct((r, c), out_dtype),
        compiler_params=_cparams(("parallel",)),
    )(parts)


def _add_pair(a, b, *, name):
    k, r, c = a.shape
    tr = _pick(r, (512, 256, 128))

    def body(a_ref, b_ref, o_ref):
        o_ref[...] = (a_ref[...].astype(F32) + b_ref[...].astype(F32)).astype(BF16)

    blk = pl.BlockSpec((None, tr, c), lambda j, i: (j, i, 0))
    return pl.pallas_call(
        body, name=name, grid=(k, r // tr), in_specs=[blk, blk], out_specs=blk,
        out_shape=jax.ShapeDtypeStruct((k, r, c), BF16),
        compiler_params=_cparams(("parallel", "parallel")),
    )(a, b)


ANY = pl.BlockSpec(memory_space=pl.ANY)


def _chip_peers(x, y):
    return [(1 - x, y, 2 * (1 - x) + y), (x, 1 - y, 2 * x + 1 - y), (1 - x, 1 - y, 2 * (1 - x) + 1 - y)]


def _gather_weights(blob, *, name):
    rows, cols = blob.shape
    half_rows = rows // 2

    def body(b_ref, o_ref, send_sems, recv_sems):
        x, y, c = lax.axis_index("x"), lax.axis_index("y"), lax.axis_index("c")
        me = 2 * x + y
        sibling = (x, y, 1 - c)
        peers = _chip_peers(x, y)

        def half(chip, hc):
            return o_ref.at[chip, pl.ds(hc * half_rows, half_rows), :]

        def copy(k, src, chip, hc, to):
            return pltpu.make_async_remote_copy(src_ref=src, dst_ref=half(chip, hc), send_sem=send_sems.at[k],
                                                recv_sem=recv_sems.at[k], device_id=to, device_id_type=MESH)

        my_half = b_ref.at[pl.ds(c * half_rows, half_rows), :]
        first = [copy(k, my_half, me, c, (px, py, c)) for k, (px, py, _) in enumerate(peers)]
        for cp in first:
            cp.start()
        passed = [copy(3 + k, half(pc, c), pc, c, sibling) for k, (_, _, pc) in enumerate(peers)]
        for k, (px, py, pc) in enumerate(peers):
            copy(k, my_half, pc, c, (px, py, c)).wait_recv()
            passed[k].start()
        for k, (_, _, pc) in enumerate(peers):
            copy(3 + k, half(pc, 1 - c), pc, 1 - c, sibling).wait_recv()
        for cp in first + passed:
            cp.wait_send()

    return pl.pallas_call(
        body, name=name, in_specs=[ANY], out_specs=ANY,
        out_shape=jax.ShapeDtypeStruct((N_CHIPS, rows, cols), BF16),
        scratch_shapes=[pltpu.SemaphoreType.DMA((6,)), pltpu.SemaphoreType.DMA((6,))],
    )(blob)


def _swap_halves(g, *, name):
    _, rows, cols = g.shape
    half_rows = rows // 2

    def body(g_ref, o_ref, send_sem, recv_sem):
        x, y, c = lax.axis_index("x"), lax.axis_index("y"), lax.axis_index("c")
        cp = pltpu.make_async_remote_copy(
            src_ref=g_ref.at[:, pl.ds((1 - c) * half_rows, half_rows), :], dst_ref=o_ref,
            send_sem=send_sem, recv_sem=recv_sem, device_id=(x, y, 1 - c), device_id_type=MESH)
        cp.start()
        cp.wait()

    return pl.pallas_call(
        body, name=name, in_specs=[ANY], out_specs=ANY,
        out_shape=jax.ShapeDtypeStruct((N_CHIPS, half_rows, cols), BF16),
        scratch_shapes=[pltpu.SemaphoreType.DMA, pltpu.SemaphoreType.DMA],
    )(g)


def _exchange_chips(p, *, name):
    def body(p_ref, o_ref, send_sems, recv_sems):
        x, y, c = lax.axis_index("x"), lax.axis_index("y"), lax.axis_index("c")
        me = 2 * x + y
        peers = _chip_peers(x, y)
        cps = [pltpu.make_async_remote_copy(src_ref=p_ref.at[pc], dst_ref=o_ref.at[me], send_sem=send_sems.at[k],
                                            recv_sem=recv_sems.at[k], device_id=(px, py, c), device_id_type=MESH)
               for k, (px, py, pc) in enumerate(peers)]
        for cp in cps:
            cp.start()
        for k, (px, py, pc) in enumerate(peers):
            pltpu.make_async_remote_copy(src_ref=p_ref.at[pc], dst_ref=o_ref.at[pc], send_sem=send_sems.at[k],
                                         recv_sem=recv_sems.at[k], device_id=(px, py, c),
                                         device_id_type=MESH).wait_recv()
        for cp in cps:
            cp.wait_send()

    got = pl.pallas_call(
        body, name=name, in_specs=[ANY], out_specs=ANY,
        out_shape=jax.ShapeDtypeStruct(p.shape, BF16),
        scratch_shapes=[pltpu.SemaphoreType.DMA((3,)), pltpu.SemaphoreType.DMA((3,))],
    )(p)
    me = 2 * lax.axis_index("x") + lax.axis_index("y")
    return lax.dynamic_update_slice(got, lax.dynamic_slice_in_dim(p, me, 1, axis=0), (me, 0, 0))


HBM_SPEC = pl.BlockSpec(memory_space=pltpu.HBM)
SEM_SPEC = pl.BlockSpec(memory_space=pltpu.SEMAPHORE)
SPLIT_EFFECT = pltpu.SideEffectType.DATAFLOW_SIDE_EFFECTING


def _gather_peers_copies(b_ref, land_ref, send_sems, recv_sems, sending):
    x, y, c = lax.axis_index("x"), lax.axis_index("y"), lax.axis_index("c")
    me = 2 * x + y
    half_rows = b_ref.shape[0] // 2
    src = b_ref.at[pl.ds(c * half_rows, half_rows), :]
    return [pltpu.make_async_remote_copy(
        src_ref=src, dst_ref=land_ref.at[me if sending else pc, pl.ds(c * half_rows, half_rows), :],
        send_sem=send_sems.at[k], recv_sem=recv_sems.at[k], device_id=(px, py, c), device_id_type=MESH)
        for k, (px, py, pc) in enumerate(_chip_peers(x, y))]


def _gather_start(blob, after, *, name):
    shape = (N_CHIPS,) + blob.shape

    def body(b_ref, land_ref, after_ref, send_sems, recv_sems, b_thru, land_thru, token):
        for cp in _gather_peers_copies(b_ref, land_ref, send_sems, recv_sems, True):
            cp.start()
        token[...] = jnp.zeros_like(token)

    return pl.pallas_call(
        body, name=name,
        out_shape=(pltpu.SemaphoreType.DMA((3,)), pltpu.SemaphoreType.DMA((3,)), pltpu.HBM(blob.shape, blob.dtype),
                   pltpu.HBM(shape, blob.dtype), jax.ShapeDtypeStruct((8, 128), F32)),
        in_specs=(HBM_SPEC, HBM_SPEC, ANY),
        out_specs=(SEM_SPEC, SEM_SPEC, HBM_SPEC, HBM_SPEC, pl.BlockSpec(memory_space=pltpu.VMEM)),
        input_output_aliases={0: 2, 1: 3},
        compiler_params=pltpu.CompilerParams(has_side_effects=SPLIT_EFFECT),
    )(pltpu.with_memory_space_constraint(blob, pltpu.HBM),
      pltpu.with_memory_space_constraint(lax.empty(shape, blob.dtype), pltpu.HBM), after)


def _gather_wait(send_sems, recv_sems, b_thru, land_thru, after, *, name):
    def body(b_ref, land_ref, send_sems, recv_sems, after_ref, b_dead, got_ref):
        for cp in _gather_peers_copies(b_ref, land_ref, send_sems, recv_sems, False):
            cp.wait_send()
            cp.wait_recv()

    return pl.pallas_call(
        body, name=name,
        out_shape=(pltpu.HBM(b_thru.shape, b_thru.dtype), pltpu.HBM(land_thru.shape, land_thru.dtype)),
        in_specs=(HBM_SPEC, HBM_SPEC, SEM_SPEC, SEM_SPEC, ANY), out_specs=(HBM_SPEC, HBM_SPEC),
        input_output_aliases={0: 0, 1: 1},
        compiler_params=pltpu.CompilerParams(has_side_effects=SPLIT_EFFECT),
    )(b_thru, land_thru, send_sems, recv_sems, after)


def _gather_forward(land, *, name):
    half_rows = land.shape[1] // 2

    def body(l_ref, o_ref, send_sems, recv_sems):
        x, y, c = lax.axis_index("x"), lax.axis_index("y"), lax.axis_index("c")
        cps = []
        for k, (_, _, pc) in enumerate(_chip_peers(x, y)):
            mine = pl.ds(c * half_rows, half_rows)
            cps.append(pltpu.make_async_remote_copy(
                src_ref=l_ref.at[pc, mine, :], dst_ref=o_ref.at[pc, mine, :], send_sem=send_sems.at[k],
                recv_sem=recv_sems.at[k], device_id=(x, y, 1 - c), device_id_type=MESH))
        for cp in cps:
            cp.start()
        for k, (_, _, pc) in enumerate(_chip_peers(x, y)):
            theirs = pl.ds((1 - c) * half_rows, half_rows)
            pltpu.make_async_remote_copy(
                src_ref=l_ref.at[pc, theirs, :], dst_ref=o_ref.at[pc, theirs, :], send_sem=send_sems.at[k],
                recv_sem=recv_sems.at[k], device_id=(x, y, 1 - c), device_id_type=MESH).wait_recv()
        for cp in cps:
            cp.wait_send()

    return pl.pallas_call(
        body, name=name, in_specs=[ANY], out_specs=ANY, input_output_aliases={0: 0},
        out_shape=jax.ShapeDtypeStruct(land.shape, land.dtype),
        scratch_shapes=[pltpu.SemaphoreType.DMA((3,)), pltpu.SemaphoreType.DMA((3,))],
    )(land)


def _exchange_peers_copies(p_ref, land_ref, send_sems, recv_sems, sending):
    x, y, c = lax.axis_index("x"), lax.axis_index("y"), lax.axis_index("c")
    me = 2 * x + y
    return [pltpu.make_async_remote_copy(src_ref=p_ref.at[pc], dst_ref=land_ref.at[me if sending else pc],
                                         send_sem=send_sems.at[k], recv_sem=recv_sems.at[k],
                                         device_id=(px, py, c), device_id_type=MESH)
            for k, (px, py, pc) in enumerate(_chip_peers(x, y))]


def _exchange_start(p, *, name):
    def body(p_ref, land_ref, send_sems, recv_sems, p_thru, land_thru, token):
        for cp in _exchange_peers_copies(p_ref, land_ref, send_sems, recv_sems, True):
            cp.start()
        token[...] = jnp.zeros_like(token)

    return pl.pallas_call(
        body, name=name,
        out_shape=(pltpu.SemaphoreType.DMA((3,)), pltpu.SemaphoreType.DMA((3,)), pltpu.HBM(p.shape, p.dtype),
                   pltpu.HBM(p.shape, p.dtype), jax.ShapeDtypeStruct((8, 128), F32)),
        in_specs=(HBM_SPEC, HBM_SPEC),
        out_specs=(SEM_SPEC, SEM_SPEC, HBM_SPEC, HBM_SPEC, pl.BlockSpec(memory_space=pltpu.VMEM)),
        input_output_aliases={0: 2, 1: 3},
        compiler_params=pltpu.CompilerParams(has_side_effects=SPLIT_EFFECT),
    )(pltpu.with_memory_space_constraint(p, pltpu.HBM),
      pltpu.with_memory_space_constraint(lax.empty(p.shape, p.dtype), pltpu.HBM))


def _exchange_wait(send_sems, recv_sems, p_thru, land_thru, after, *, name):
    def body(p_ref, land_ref, send_sems, recv_sems, after_ref, p_dead, got_ref):
        for cp in _exchange_peers_copies(p_ref, land_ref, send_sems, recv_sems, False):
            cp.wait_send()
            cp.wait_recv()

    return pl.pallas_call(
        body, name=name,
        out_shape=(pltpu.HBM(p_thru.shape, p_thru.dtype), pltpu.HBM(p_thru.shape, p_thru.dtype)),
        in_specs=(HBM_SPEC, HBM_SPEC, SEM_SPEC, SEM_SPEC, ANY), out_specs=(HBM_SPEC, HBM_SPEC),
        input_output_aliases={0: 0, 1: 1},
        compiler_params=pltpu.CompilerParams(has_side_effects=SPLIT_EFFECT),
    )(p_thru, land_thru, send_sems, recv_sems, after)


def _sum_parts(parts, own, *, name):
    k, r, c = parts.shape
    tr = _pick(r, (512, 256, 128))

    def body(p_ref, own_ref, o_ref):
        me = 2 * lax.axis_index("x") + lax.axis_index("y")
        acc = jnp.zeros((tr, c), F32)
        for i in range(k):
            acc = acc + jnp.where(me == i, own_ref[i], p_ref[i]).astype(F32)
        o_ref[...] = acc

    blk = pl.BlockSpec((k, tr, c), lambda i: (0, i, 0))
    return pl.pallas_call(
        body, name=name, grid=(r // tr,), in_specs=[blk, blk],
        out_specs=pl.BlockSpec((tr, c), lambda i: (i, 0)),
        out_shape=jax.ShapeDtypeStruct((r, c), F32),
        compiler_params=_cparams(("parallel",)),
    )(parts, own)


def _join_halves(gh, *, name):
    def body(g_ref, o_ref, send_sem, recv_sem):
        x, y, c = lax.axis_index("x"), lax.axis_index("y"), lax.axis_index("c")
        cp = pltpu.make_async_remote_copy(src_ref=g_ref, dst_ref=o_ref, send_sem=send_sem, recv_sem=recv_sem,
                                          device_id=(x, y, 1 - c), device_id_type=MESH)
        cp.start()
        cp.wait()

    other = pl.pallas_call(
        body, name=name, in_specs=[ANY], out_specs=ANY,
        out_shape=jax.ShapeDtypeStruct(gh.shape, F32),
        scratch_shapes=[pltpu.SemaphoreType.DMA, pltpu.SemaphoreType.DMA],
    )(gh)
    south = lax.axis_index("c") == 0
    return jnp.concatenate([jnp.where(south, gh, other), jnp.where(south, other, gh)], axis=0)


def _gather_small(s, *, name):
    rows = s.shape[0]

    def body(s_ref, o_ref, send_sems, recv_sems, local_sem):
        x, y, c = lax.axis_index("x"), lax.axis_index("y"), lax.axis_index("c")
        me = 4 * x + 2 * y + c
        mine = pltpu.make_async_copy(s_ref, o_ref.at[me], local_sem)
        mine.start()
        peers = []
        for k in range(1, 8):
            peers.append((1 - x if k & 4 else x, 1 - y if k & 2 else y, 1 - c if k & 1 else c))
        cps = [pltpu.make_async_remote_copy(src_ref=s_ref, dst_ref=o_ref.at[me], send_sem=send_sems.at[k],
                                            recv_sem=recv_sems.at[k], device_id=p, device_id_type=MESH)
               for k, p in enumerate(peers)]
        for cp in cps:
            cp.start()
        for k, (px, py, pc) in enumerate(peers):
            pltpu.make_async_remote_copy(src_ref=s_ref, dst_ref=o_ref.at[4 * px + 2 * py + pc],
                                         send_sem=send_sems.at[k], recv_sem=recv_sems.at[k],
                                         device_id=(px, py, pc), device_id_type=MESH).wait_recv()
        for cp in cps:
            cp.wait_send()
        mine.wait()

    return pl.pallas_call(
        body, name=name, in_specs=[ANY], out_specs=ANY,
        out_shape=jax.ShapeDtypeStruct((8, rows, 128), F32),
        scratch_shapes=[pltpu.SemaphoreType.DMA((7,)), pltpu.SemaphoreType.DMA((7,)), pltpu.SemaphoreType.DMA],
    )(s)


IN_SHARD = IN_WIDTH // N_CHIPS
IN_SHARD_PAD = 1536
OUT_ROWS, UP_ROWS, DOWN_ROWS = 512, 1024, 1024


def _pack_in(w_in_s):
    return jnp.pad(w_in_s, ((0, 0), (0, IN_SHARD_PAD - IN_SHARD))).astype(BF16)


def _pack_rest(w_out_s, w_up_s, w_down_s):
    return jnp.concatenate([w_out_s, w_up_s, w_down_s], axis=0).astype(BF16)


def _unpack_rest(blob):
    return (blob[0:OUT_ROWS], blob[OUT_ROWS:OUT_ROWS + UP_ROWS], blob[OUT_ROWS + UP_ROWS:])


def _with_own(gathered, own):
    me = 2 * lax.axis_index("x") + lax.axis_index("y")
    return [jnp.where(me == j, own, gathered[j]) for j in range(N_CHIPS)]


def _full_w_in(g_in, own):
    return jnp.concatenate([s[:, :IN_SHARD] for s in _with_own(g_in, own)], axis=1)


def _full_rest(g_rest, own):
    shards = _with_own(g_rest, own)
    w_out = jnp.concatenate([s[0:OUT_ROWS] for s in shards], axis=0)
    w_up = jnp.concatenate([s[OUT_ROWS:OUT_ROWS + UP_ROWS] for s in shards], axis=1)
    w_down = jnp.concatenate([s[OUT_ROWS + UP_ROWS:] for s in shards], axis=0)
    return w_out, w_up, w_down


def _split_w_in(w_in):
    z_xbc = w_in[:, 0:2560]
    dt = w_in[:, 2560:2576]
    qkv = w_in[:, 2576:5648]
    f = w_in[:, 5648:5664]
    pad = jnp.zeros((w_in.shape[0], PA_WIDTH - 2592), w_in.dtype)
    return jnp.concatenate([z_xbc, dt, f, pad], axis=1), qkv


def _merge_w_in(d_a, d_qkv):
    return jnp.concatenate([d_a[:, 0:2560], d_a[:, 2560:2576], d_qkv, d_a[:, 2576:2592]], axis=1)


def _local_step(x3, target3, w_in, rest_weights, norm_mix_w, conv_w, conv_b, dt_bias, a_log, d_skip,
                ssd_norm_w, f_bias, norm_mlp_w, norm_final_w, first_after=None, early_grads=None, late_grads=None):
    bl, t, d = x3.shape
    n = bl * t
    x = x3.reshape(n, d)
    target = target3.reshape(n, d)
    w_a, w_qkv = _split_w_in(w_in)
    nfw = norm_final_w.reshape(1, d)
    dskip_e = jnp.repeat(d_skip, HEAD_DIM, axis=1)
    nb = t // ATT_BLOCK

    h0, rstd0 = _rmsnorm_fwd(x, norm_mix_w, name="norm_mix_fwd")
    r1, r2, r4, kt = min(n, 1024), min(n, 512), min(n, 256), min(n, 512)
    proj_a = _mm(h0, w_a, name="proj_a", tiles=(r2, PA_WIDTH, D_MODEL), after=first_after)
    qkv = _mm(h0, w_qkv, name="proj_qkv", tiles=(r2, QKV_WIDTH, D_MODEL), out_dtype=BF16)
    bias128 = jnp.concatenate([dt_bias, f_bias, jnp.zeros((1, 96), F32)], axis=1)
    alog128 = jnp.concatenate([a_log, jnp.zeros((1, 112), F32)], axis=1)
    dt, sig, acum, ccum, sigf = _prep(proj_a, bias128, alog128, bl, t)
    acum_t = acum[:, 0:16].T
    negc = jnp.pad(-ccum.reshape(bl, t, 8, 2).transpose(0, 2, 3, 1), ((0, 0), (0, 0), (0, 6), (0, 0)))
    xc = _conv_fwd(proj_a, conv_w, conv_b, bl, t)
    y_ssd, y_pre, hprev = _ssd_fwd2(xc, proj_a, dt, acum, acum_t, dskip_e, ssd_norm_w, bl, t)
    y_att, lse = _attn_fwd(qkv, negc, bl, t)
    w_out, w_up, w_down = rest_weights(y_att)
    wo_s, wo_a = w_out[:SSD_WIDTH], w_out[SSD_WIDTH:]
    t1 = _mm(y_ssd, wo_s, name="out_proj_ssd", tiles=(r1, D_MODEL, SSD_WIDTH), res=x)
    h1, h1n, rstd1 = _mm_norm_fwd(y_att, wo_a, t1, norm_mlp_w, name="out_proj_att_norm_mlp", tm=r2)
    up = _mm(h1n, w_up, name="mlp_up", tiles=(r2, D_FF, D_MODEL), out_dtype=BF16)
    dh2, dh2b, loss, d_nfw = _mm_final(up, w_down, h1, nfw, target, name="mlp_down_final_norm_loss", tm=r2,
                                       a_act="relu2")

    dup = _mm(dh2b, w_down, name="mlp_down_bwd_act", tiles=(r2, D_FF, D_MODEL), tb=True, epi_up=up, out_dtype=BF16)
    d_w_down = _mm(up, dh2b, name="mlp_down_bwd_w", tiles=(1024, 1024, kt), ta=True, a_act="relu2")
    dh1, dh1b, d_nmlp = _mm_norm_bwd(dup, w_up, None, h1, rstd1, norm_mlp_w, dh2, name="mlp_up_bwd_act_norm_mlp",
                                     tm=r2)
    d_w_up = _mm(h1n, dup, name="mlp_up_bwd_w", tiles=(1024, 1024, kt), ta=True)
    dys = _mm(dh1b, wo_s, name="out_proj_bwd_ssd", tiles=(r1, SSD_WIDTH, D_MODEL), tb=True)
    do = _mm(dh1b, wo_a, name="out_proj_bwd_att", tiles=(r1, ATT_WIDTH, D_MODEL), tb=True, out_dtype=BF16)
    d_w_out = jnp.concatenate([_mm(y_ssd, dh1b, name="out_proj_bwd_w_ssd", tiles=(1024, 1024, kt), ta=True),
                               _mm(y_att, dh1b, name="out_proj_bwd_w_att", tiles=(1024, 1024, kt), ta=True)], axis=0)
    token = jnp.zeros((8, 128), F32) if early_grads is None else early_grads(d_w_out, d_w_up, d_w_down)
    dq, dk, dv, dcb = _attn_bwd(qkv, do, y_att, lse, negc, token, bl, t)
    dc = jnp.pad(dcb[:, :, 0:2, :].transpose(0, 3, 1, 2).reshape(n, 16), ((0, 0), (0, 112)))
    df_raw, d_fb = _fpost(dc, sigf, bl, t)
    dxc, dz, ddt_raw, d_snw, d_dsk, d_alog, d_dtb = _ssd_bwd2(dys, xc, proj_a, y_pre, hprev, dt, sig, acum, acum_t,
                                                            a_log, dskip_e, ssd_norm_w, bl, t)
    dxbc, d_conv_w, d_conv_b = _conv_bwd(dxc, proj_a, conv_w, conv_b, bl, t)
    dproj_a = jnp.concatenate([dz, dxbc, ddt_raw.astype(BF16), df_raw.astype(BF16),
                               jnp.zeros((n, PA_WIDTH - 2592), BF16)], axis=1)
    dqkv = jnp.concatenate([dq, dk, dv], axis=1)
    d_w_a = _mm(h0, dproj_a, name="proj_a_bwd_w", tiles=(1024, 896, kt), ta=True)
    d_w_qkv = _mm(h0, dqkv, name="proj_qkv_bwd_w", tiles=(1024, 1024, kt), ta=True)
    d_w_in = _merge_w_in(d_w_a, d_w_qkv)
    late_token = None if late_grads is None else late_grads(d_w_in)
    t2 = _mm(dproj_a, w_a, name="proj_a_bwd_act", tiles=(r1, D_MODEL, PA_WIDTH), tb=True, after=late_token)
    dx, _, d_nmix = _mm_norm_bwd(dqkv, w_qkv, t2, x, rstd0, norm_mix_w, dh1, name="proj_qkv_bwd_act_norm_mix",
                                 tm=r2)

    grads = dict(norm_mix_w=d_nmix, w_in=d_w_in, conv_w=d_conv_w, conv_b=d_conv_b,
                 dt_bias=d_dtb, a_log=d_alog, d_skip=d_dsk, ssd_norm_w=d_snw, f_bias=d_fb, w_out=d_w_out,
                 norm_mlp_w=d_nmlp, w_up=d_w_up, w_down=d_w_down, norm_final_w=d_nfw)
    return dx.reshape(bl, t, d), loss, grads


SMALL_ORDER = ("norm_mix_w", "conv_w", "conv_b", "dt_bias", "a_log", "d_skip", "ssd_norm_w", "f_bias",
               "norm_mlp_w", "norm_final_w")
SMALL_SIZES = (1024, 4 * CONV_CH, CONV_CH, 16, 16, 16, 1024, 16, 1024, 1024)


def _pack_small(vals, rows):
    flat = jnp.concatenate([v.reshape(-1).astype(F32) for v in vals])
    return jnp.pad(flat, (0, rows * 128 - flat.shape[0])).reshape(rows, 128)


def _unpack_small(packed, sizes):
    flat = packed.reshape(-1)
    out, o = [], 0
    for s in sizes:
        out.append(flat[o:o + s])
        o += s
    return out


def kernel(x, norm_mix_w, w_in, conv_w, conv_b, dt_bias, a_log, d_skip, ssd_norm_w, f_bias, w_out, norm_mlp_w, w_up, w_down, norm_final_w, loss_target, m_norm_mix_w, m_w_in, m_conv_w, m_conv_b, m_dt_bias, m_a_log, m_d_skip, m_ssd_norm_w, m_f_bias, m_w_out, m_norm_mlp_w, m_w_up, m_w_down, m_norm_final_w, v_norm_mix_w, v_w_in, v_conv_w, v_conv_b, v_dt_bias, v_a_log, v_d_skip, v_ssd_norm_w, v_f_bias, v_w_out, v_norm_mlp_w, v_w_up, v_w_down, v_norm_final_w):
    chip = 2 * lax.axis_index("x") + lax.axis_index("y")
    cw = CONV_CH // N_CHIPS

    own_in = _pack_in(w_in[0])
    own_rest = _pack_rest(w_out[0], w_up[0], w_down[0])
    g_in = _gather_weights(own_in, name="gather_w_in")
    w_in_f = _full_w_in(g_in, own_in)
    *rest_handles, rest_token = _gather_start(own_rest, g_in, name="gather_start_rest")

    def rest_weights(after):
        _, landed = _gather_wait(*rest_handles, after, name="gather_wait_rest")
        return _full_rest(_gather_forward(landed, name="gather_forward_rest"), own_rest)
    small_all = _gather_small(_pack_small([conv_w[0]], 16), name="gather_conv_w")
    conv_w_f = jnp.concatenate([small_all[2 * j].reshape(-1)[:4 * cw].reshape(4, cw) for j in range(N_CHIPS)], axis=1)

    c = lax.axis_index("c")

    def chip_partial(gb, tag):
        half_rows = gb.shape[1] // 2
        from_sibling = _swap_halves(gb, name="grad_swap_halves_" + tag)
        my_half = lax.dynamic_slice_in_dim(gb, c * half_rows, half_rows, axis=1)
        return _add_pair(my_half, from_sibling, name="grad_add_sibling_" + tag)

    in_flight = {}

    def early_grads(d_w_out, d_w_up, d_w_down):
        gb_rest = jnp.stack([_pack_rest(d_w_out[j * OUT_ROWS:(j + 1) * OUT_ROWS],
                                        d_w_up[:, j * UP_ROWS:(j + 1) * UP_ROWS],
                                        d_w_down[j * DOWN_ROWS:(j + 1) * DOWN_ROWS]) for j in range(N_CHIPS)])
        part = chip_partial(gb_rest, "rest")
        *handles, token = _exchange_start(part, name="grad_exchange_start_rest")
        in_flight["rest"] = handles
        return token

    def late_grads(d_w_in):
        gb_in = jnp.stack([_pack_in(d_w_in[:, j * IN_SHARD:(j + 1) * IN_SHARD]) for j in range(N_CHIPS)])
        *handles, token = _exchange_start(chip_partial(gb_in, "in"), name="grad_exchange_start_in")
        in_flight["in"] = handles
        return token

    dx, loss_part, g = _local_step(x, loss_target, w_in_f, rest_weights, norm_mix_w, conv_w_f,
                                   conv_b, dt_bias, a_log, d_skip, ssd_norm_w, f_bias, norm_mlp_w, norm_final_w,
                                   first_after=rest_token, early_grads=early_grads, late_grads=late_grads)

    send_sems, recv_sems, part_rest, land_rest = in_flight["rest"]
    part_rest, parts_rest = _exchange_wait(send_sems, recv_sems, part_rest, land_rest, dx,
                                           name="grad_exchange_wait_rest")
    g_rest_half = _sum_parts(parts_rest, part_rest, name="grad_sum_chips_rest")
    g_w_out, g_w_up, g_w_down = _unpack_rest(_join_halves(g_rest_half, name="grad_join_halves_rest"))

    part_in, parts_in = _exchange_wait(*in_flight["in"], dx, name="grad_exchange_wait_in")
    g_in_half = _sum_parts(parts_in, part_in, name="grad_sum_chips_in")
    g_w_in = _join_halves(g_in_half, name="grad_join_halves_in")[:, :IN_SHARD]

    small_vals = [g[k] for k in SMALL_ORDER] + [loss_part[:, 0:1]]
    small_sum = _sum_leading(_gather_small(_pack_small(small_vals, SMALL_ROWS), name="gather_small_grads"), name="small_sum")
    sg = dict(zip(SMALL_ORDER + ("loss",), _unpack_small(small_sum, SMALL_SIZES + (1,))))
    loss = sg["loss"].reshape(())
    g_conv_full = sg["conv_w"].reshape(4, CONV_CH)
    g_conv = lax.dynamic_slice_in_dim(g_conv_full, chip * cw, cw, axis=1)

    grads = dict(norm_mix_w=sg["norm_mix_w"].reshape(1, -1), w_in=g_w_in[None], conv_w=g_conv[None],
                 conv_b=sg["conv_b"].reshape(1, -1), dt_bias=sg["dt_bias"].reshape(1, -1),
                 a_log=sg["a_log"].reshape(1, -1), d_skip=sg["d_skip"].reshape(1, -1),
                 ssd_norm_w=sg["ssd_norm_w"].reshape(1, -1), f_bias=sg["f_bias"].reshape(1, -1), w_out=g_w_out[None],
                 norm_mlp_w=sg["norm_mlp_w"].reshape(1, -1), w_up=g_w_up[None], w_down=g_w_down[None],
                 norm_final_w=sg["norm_final_w"])
    weights = dict(norm_mix_w=norm_mix_w, w_in=w_in, conv_w=conv_w, conv_b=conv_b, dt_bias=dt_bias, a_log=a_log,
                   d_skip=d_skip, ssd_norm_w=ssd_norm_w, f_bias=f_bias, w_out=w_out, norm_mlp_w=norm_mlp_w,
                   w_up=w_up, w_down=w_down, norm_final_w=norm_final_w)
    ms = dict(norm_mix_w=m_norm_mix_w, w_in=m_w_in, conv_w=m_conv_w, conv_b=m_conv_b, dt_bias=m_dt_bias,
              a_log=m_a_log, d_skip=m_d_skip, ssd_norm_w=m_ssd_norm_w, f_bias=m_f_bias, w_out=m_w_out,
              norm_mlp_w=m_norm_mlp_w, w_up=m_w_up, w_down=m_w_down, norm_final_w=m_norm_final_w)
    vs = dict(norm_mix_w=v_norm_mix_w, w_in=v_w_in, conv_w=v_conv_w, conv_b=v_conv_b, dt_bias=v_dt_bias,
              a_log=v_a_log, d_skip=v_d_skip, ssd_norm_w=v_ssd_norm_w, f_bias=v_f_bias, w_out=v_w_out,
              norm_mlp_w=v_norm_mlp_w, w_up=v_w_up, w_down=v_w_down, norm_final_w=v_norm_final_w)
    names = list(weights)
    big = ("w_in", "w_out", "w_up", "w_down")
    delta, new_m, new_v = {}, {}, {}
    for k in big:
        shp = weights[k].shape
        two_d = lambda a: a.reshape(shp[-2], shp[-1])
        d_, m_, v_ = _adamw(two_d(weights[k]), two_d(grads[k]), two_d(ms[k]), two_d(vs[k]), name="adamw_" + k)
        delta[k], new_m[k], new_v[k] = d_.reshape(shp), m_.reshape(shp), v_.reshape(shp)
    smalls = [k for k in names if k not in big]
    sizes = [math.prod(weights[k].shape) for k in smalls]
    rows = -(-sum(sizes) // 1024) * 8
    packs = [_pack_small([d[k] for k in smalls], rows) for d in (weights, grads, ms, vs)]
    outs = _adamw(*packs, name="adamw_small")
    for o, dst in zip(outs, (delta, new_m, new_v)):
        for k, val in zip(smalls, _unpack_small(o, sizes)):
            dst[k] = val.reshape(weights[k].shape)
    return (loss, dx, *[grads[k] for k in names], *[delta[k] for k in names], *[new_m[k] for k in names],
            *[new_v[k] for k in names])
```

```python
import functools
import math

import jax
import jax.numpy as jnp
from jax import lax
from jax.experimental import pallas as pl
from jax.experimental.pallas import tpu as pltpu

F32 = jnp.float32
BF16 = jnp.bfloat16
HIGHEST = lax.Precision.HIGHEST
MESH = pl.DeviceIdType.MESH

D_MODEL = 1024
SSD_HEADS = 16
HEAD_DIM = 64
SSD_WIDTH = 1024
SSD_STATE = 128
CONV_CH = 1536
CHUNK = 128
ATT_WIDTH = 1024
EPS = 1e-5
IN_WIDTH = 5664
PA_WIDTH = 2688
QKV_WIDTH = 3072
D_FF = 4096
ATT_BLOCK = 256
NEG = -1e30
LOG2E = 1.4426950408889634
VMEM_LIMIT = 48 * 1024 * 1024

ADAM_LR = 0.001
ADAM_B1 = 0.9
ADAM_B2 = 0.999
ADAM_EPS = 1e-08
ADAM_WD = 0.01
ADAM_STEP = 10

N_CHIPS = 4
BLOB_ROWS = 4096
HALF_ROWS = BLOB_ROWS // 2
SMALL_ROWS = 96


def _cparams(sem):
    return pltpu.CompilerParams(dimension_semantics=sem, vmem_limit_bytes=VMEM_LIMIT)


def _pick(n, cands):
    for c in cands:
        if n % c == 0:
            return c
    return n


MM_CHUNK = 512


def _mm(a, b, *, name, tiles, ta=False, tb=False, out_dtype=F32, res=None, a_act=None, epi_up=None, after=None,
        into=None):
    n_unread = (after is not None) + (into is not None and into[3] is not None)
    if ta:
        K, M = a.shape
    else:
        M, K = a.shape
    if tb:
        N, K2 = b.shape
    else:
        K2, N = b.shape
    assert K == K2, (a.shape, b.shape)
    tm, tn, tk = tiles
    assert M % tm == 0 and N % tn == 0 and K % tk == 0, (name, M, N, K, tiles)
    nk = K // tk
    dn = (((0 if ta else 1,), (1 if tb else 0,)), ((), ()))
    has_res = res is not None
    has_up = epi_up is not None
    cn = _pick(tn, (MM_CHUNK, 384, 256, 128))

    def prologue(av):
        if a_act == "relu2":
            r = jnp.maximum(av.astype(F32), 0.0)
            av = r * r
        return av.astype(BF16)

    def epilogue(out, res_v, up_v):
        if has_res:
            out = out + res_v.astype(F32)
        if has_up:
            out = out * (2.0 * jnp.maximum(up_v.astype(F32), 0.0))
        return out.astype(out_dtype)

    def body(*refs):
        a_ref, b_ref = refs[0], refs[1]
        i = 2
        res_ref = up_ref = None
        if has_res:
            res_ref = refs[i]
            i += 1
        if has_up:
            up_ref = refs[i]
            i += 1
        i += n_unread
        o_ref = refs[i]
        if nk == 1:
            av = prologue(a_ref[...])
            for c in range(tn // cn):
                cs = slice(c * cn, (c + 1) * cn)
                bv = (b_ref[cs, :] if tb else b_ref[:, cs]).astype(BF16)
                out = lax.dot_general(av, bv, dn, preferred_element_type=F32)
                o_ref[:, cs] = epilogue(out, res_ref[:, cs] if has_res else None, up_ref[:, cs] if has_up else None)
            return
        acc_ref = refs[i + 1]
        k = pl.program_id(2)

        @pl.when(k == 0)
        def _():
            acc_ref[...] = jnp.zeros_like(acc_ref)

        acc_ref[...] += lax.dot_general(prologue(a_ref[...]), b_ref[...].astype(BF16), dn,
                                        preferred_element_type=F32)

        @pl.when(k == nk - 1)
        def _():
            o_ref[...] = epilogue(acc_ref[...], res_ref[...] if has_res else None, up_ref[...] if has_up else None)

    a_spec = pl.BlockSpec((tk, tm), lambda i, j, k: (k, i)) if ta else pl.BlockSpec((tm, tk), lambda i, j, k: (i, k))
    b_spec = pl.BlockSpec((tn, tk), lambda i, j, k: (j, k)) if tb else pl.BlockSpec((tk, tn), lambda i, j, k: (k, j))
    o_spec = pl.BlockSpec((tm, tn), lambda i, j, k: (i, j))
    ins, specs = [a, b], [a_spec, b_spec]
    if has_res:
        ins.append(res)
        specs.append(o_spec)
    if has_up:
        ins.append(epi_up)
        specs.append(o_spec)
    if after is not None:
        ins.append(after)
        specs.append(pl.BlockSpec(memory_space=pl.ANY))
    out_shape, out_spec, aliases = jax.ShapeDtypeStruct((M, N), out_dtype), o_spec, {}
    if into is not None:
        shape, block, index, buf = into
        out_shape, out_spec = jax.ShapeDtypeStruct(shape, out_dtype), pl.BlockSpec(block, index)
        if buf is not None:
            aliases = {len(ins): 0}
            ins.append(buf)
            specs.append(pl.BlockSpec(memory_space=pl.ANY))
    return pl.pallas_call(
        body, name=name, grid=(M // tm, N // tn, nk),
        in_specs=specs, out_specs=out_spec, out_shape=out_shape, input_output_aliases=aliases,
        scratch_shapes=[] if nk == 1 else [pltpu.VMEM((tm, tn), F32)],
        compiler_params=_cparams(("parallel", "parallel", "arbitrary")),
    )(*ins)


def _rmsnorm_fwd(x, w, *, name):
    n, d = x.shape
    tm = _pick(n, (512, 256, 128))

    def body(x_ref, w_ref, y_ref, r_ref):
        xv = x_ref[...]
        rstd = lax.rsqrt(jnp.mean(xv * xv, axis=1, keepdims=True) + EPS)
        y_ref[...] = (xv * rstd * w_ref[...]).astype(BF16)
        r_ref[...] = rstd

    return pl.pallas_call(
        body, name=name, grid=(n // tm,),
        in_specs=[pl.BlockSpec((tm, d), lambda i: (i, 0)), pl.BlockSpec((1, d), lambda i: (0, 0))],
        out_specs=[pl.BlockSpec((tm, d), lambda i: (i, 0)), pl.BlockSpec((tm, 1), lambda i: (i, 0))],
        out_shape=[jax.ShapeDtypeStruct((n, d), BF16), jax.ShapeDtypeStruct((n, 1), F32)],
        compiler_params=_cparams(("parallel",)),
    )(x, w)


def _rmsnorm_bwd(dyn, x, rstd, w, dres, *, name):
    n, d = x.shape
    tm = _pick(n, (512, 256, 128))

    def body(g_ref, x_ref, r_ref, w_ref, d_ref, dx_ref, dxb_ref, dw_ref):
        @pl.when(pl.program_id(0) == 0)
        def _():
            dw_ref[...] = jnp.zeros_like(dw_ref)

        g = g_ref[...]
        r = r_ref[...]
        xhat = x_ref[...] * r
        gw = g * w_ref[...]
        dx = d_ref[...] + r * (gw - xhat * jnp.mean(gw * xhat, axis=1, keepdims=True))
        dx_ref[...] = dx
        dxb_ref[...] = dx.astype(BF16)
        dw_ref[...] += jnp.sum(g * xhat, axis=0, keepdims=True)

    row = pl.BlockSpec((tm, d), lambda i: (i, 0))
    vec = pl.BlockSpec((1, d), lambda i: (0, 0))
    return pl.pallas_call(
        body, name=name, grid=(n // tm,),
        in_specs=[row, row, pl.BlockSpec((tm, 1), lambda i: (i, 0)), vec, row],
        out_specs=[row, row, vec],
        out_shape=[jax.ShapeDtypeStruct((n, d), F32), jax.ShapeDtypeStruct((n, d), BF16),
                   jax.ShapeDtypeStruct((1, d), F32)],
        compiler_params=_cparams(("arbitrary",)),
    )(dyn, x, rstd, w, dres)


def _final(h2, w, target):
    n, d = h2.shape
    tm = _pick(n, (512, 256, 128))

    def body(h_ref, w_ref, t_ref, dh_ref, dhb_ref, loss_ref, dw_ref):
        @pl.when(pl.program_id(0) == 0)
        def _():
            loss_ref[...] = jnp.zeros_like(loss_ref)
            dw_ref[...] = jnp.zeros_like(dw_ref)

        hv = h_ref[...]
        wv = w_ref[...]
        rstd = lax.rsqrt(jnp.mean(hv * hv, axis=1, keepdims=True) + EPS)
        xhat = hv * rstd
        err = xhat * wv - t_ref[...]
        part = jnp.sum(jnp.mean(err * err, axis=1, keepdims=True), axis=0, keepdims=True)
        loss_ref[...] += 0.5 * part
        dy = err * (1.0 / d)
        gw = dy * wv
        dh = rstd * (gw - xhat * jnp.mean(gw * xhat, axis=1, keepdims=True))
        dh_ref[...] = dh
        dhb_ref[...] = dh.astype(BF16)
        dw_ref[...] += jnp.sum(dy * xhat, axis=0, keepdims=True)

    row = pl.BlockSpec((tm, d), lambda i: (i, 0))
    vec = pl.BlockSpec((1, d), lambda i: (0, 0))
    return pl.pallas_call(
        body, name="final_norm_loss", grid=(n // tm,),
        in_specs=[row, vec, row],
        out_specs=[row, row, pl.BlockSpec((1, 128), lambda i: (0, 0)), vec],
        out_shape=[jax.ShapeDtypeStruct((n, d), F32), jax.ShapeDtypeStruct((n, d), BF16),
                   jax.ShapeDtypeStruct((1, 128), F32), jax.ShapeDtypeStruct((1, d), F32)],
        compiler_params=_cparams(("arbitrary",)),
    )(h2, w, target)


def _rows_product(a_ref, b_ref, tb, a_act):
    av = a_ref[...]
    if a_act == "relu2":
        r = jnp.maximum(av.astype(F32), 0.0)
        av = r * r
    dn = (((1,), (1 if tb else 0,)), ((), ()))
    return lax.dot_general(av.astype(BF16), b_ref[...].astype(BF16), dn, preferred_element_type=F32)


def _mm_norm_fwd(a, b, res, w, *, name, tm):
    m, k = a.shape
    d = b.shape[1]

    def body(a_ref, b_ref, res_ref, w_ref, h_ref, y_ref, r_ref):
        hv = _rows_product(a_ref, b_ref, False, None) + res_ref[...]
        rstd = lax.rsqrt(jnp.mean(hv * hv, axis=1, keepdims=True) + EPS)
        h_ref[...] = hv
        y_ref[...] = (hv * rstd * w_ref[...]).astype(BF16)
        r_ref[...] = rstd

    row = pl.BlockSpec((tm, d), lambda i: (i, 0))
    return pl.pallas_call(
        body, name=name, grid=(m // tm,),
        in_specs=[pl.BlockSpec((tm, k), lambda i: (i, 0)), pl.BlockSpec((k, d), lambda i: (0, 0)), row,
                  pl.BlockSpec((1, d), lambda i: (0, 0))],
        out_specs=[row, row, pl.BlockSpec((tm, 1), lambda i: (i, 0))],
        out_shape=[jax.ShapeDtypeStruct((m, d), F32), jax.ShapeDtypeStruct((m, d), BF16),
                   jax.ShapeDtypeStruct((m, 1), F32)],
        compiler_params=_cparams(("parallel",)),
    )(a, b, res, w)


def _mm_final(a, b, res, w, target, *, name, tm, a_act):
    m, k = a.shape
    d = b.shape[1]

    def body(a_ref, b_ref, res_ref, w_ref, t_ref, dh_ref, dhb_ref, loss_ref, dw_ref):
        @pl.when(pl.program_id(0) == 0)
        def _():
            loss_ref[...] = jnp.zeros_like(loss_ref)
            dw_ref[...] = jnp.zeros_like(dw_ref)

        hv = _rows_product(a_ref, b_ref, False, a_act) + res_ref[...]
        wv = w_ref[...]
        rstd = lax.rsqrt(jnp.mean(hv * hv, axis=1, keepdims=True) + EPS)
        xhat = hv * rstd
        err = xhat * wv - t_ref[...]
        loss_ref[...] += 0.5 * jnp.sum(jnp.mean(err * err, axis=1, keepdims=True), axis=0, keepdims=True)
        dy = err * (1.0 / d)
        gw = dy * wv
        dh = rstd * (gw - xhat * jnp.mean(gw * xhat, axis=1, keepdims=True))
        dh_ref[...] = dh
        dhb_ref[...] = dh.astype(BF16)
        dw_ref[...] += jnp.sum(dy * xhat, axis=0, keepdims=True)

    row = pl.BlockSpec((tm, d), lambda i: (i, 0))
    vec = pl.BlockSpec((1, d), lambda i: (0, 0))
    return pl.pallas_call(
        body, name=name, grid=(m // tm,),
        in_specs=[pl.BlockSpec((tm, k), lambda i: (i, 0)), pl.BlockSpec((k, d), lambda i: (0, 0)), row, vec, row],
        out_specs=[row, row, pl.BlockSpec((1, 128), lambda i: (0, 0)), vec],
        out_shape=[jax.ShapeDtypeStruct((m, d), F32), jax.ShapeDtypeStruct((m, d), BF16),
                   jax.ShapeDtypeStruct((1, 128), F32), jax.ShapeDtypeStruct((1, d), F32)],
        compiler_params=_cparams(("arbitrary",)),
    )(a, b, res, w, target)


def _mm_norm_bwd(a, b, res, x, rstd, w, dres, *, name, tm):
    m, k = a.shape
    d = b.shape[0]
    has_res = res is not None

    def body(*refs):
        a_ref, b_ref = refs[0], refs[1]
        i = 2
        res_ref = None
        if has_res:
            res_ref = refs[i]
            i += 1
        x_ref, r_ref, w_ref, d_ref, dx_ref, dxb_ref, dw_ref = refs[i:i + 7]

        @pl.when(pl.program_id(0) == 0)
        def _():
            dw_ref[...] = jnp.zeros_like(dw_ref)

        g = _rows_product(a_ref, b_ref, True, None)
        if has_res:
            g = g + res_ref[...]
        r = r_ref[...]
        xhat = x_ref[...] * r
        gw = g * w_ref[...]
        dx = d_ref[...] + r * (gw - xhat * jnp.mean(gw * xhat, axis=1, keepdims=True))
        dx_ref[...] = dx
        dxb_ref[...] = dx.astype(BF16)
        dw_ref[...] += jnp.sum(g * xhat, axis=0, keepdims=True)

    row = pl.BlockSpec((tm, d), lambda i: (i, 0))
    vec = pl.BlockSpec((1, d), lambda i: (0, 0))
    ins = [a, b] + ([res] if has_res else []) + [x, rstd, w, dres]
    specs = ([pl.BlockSpec((tm, k), lambda i: (i, 0)), pl.BlockSpec((d, k), lambda i: (0, 0))]
             + ([row] if has_res else []) + [row, pl.BlockSpec((tm, 1), lambda i: (i, 0)), vec, row])
    return pl.pallas_call(
        body, name=name, grid=(m // tm,), in_specs=specs, out_specs=[row, row, vec],
        out_shape=[jax.ShapeDtypeStruct((m, d), F32), jax.ShapeDtypeStruct((m, d), BF16),
                   jax.ShapeDtypeStruct((1, d), F32)],
        compiler_params=_cparams(("arbitrary",)),
    )(*ins)


def _softplus(x):
    return jnp.maximum(x, 0.0) + jnp.log(1.0 + jnp.exp(-jnp.abs(x)))


def _prep(proj_a, bias128, alog128, bl, t):
    n = bl * t
    nch = t // CHUNK
    col0 = (SSD_WIDTH + CONV_CH) // 128

    def body(p_ref, b_ref, al_ref, dt_ref, gd_ref, ac_ref, c_ref, carry):
        @pl.when(pl.program_id(1) == 0)
        def _():
            carry[...] = jnp.zeros_like(carry)

        xv = p_ref[...] + b_ref[...]
        sp = _softplus(xv)
        a = -jnp.exp(al_ref[...]) * sp
        logf = -_softplus(-xv)
        row = lax.broadcasted_iota(jnp.int32, (CHUNK, CHUNK), 0)
        col = lax.broadcasted_iota(jnp.int32, (CHUNK, CHUNK), 1)
        tril = (row >= col).astype(F32)
        acum = jnp.dot(tril, a, precision=HIGHEST, preferred_element_type=F32)
        c = jnp.dot(tril, logf, precision=HIGHEST, preferred_element_type=F32) + carry[...]
        carry[...] = c[CHUNK - 1:CHUNK, :]
        lane = lax.broadcasted_iota(jnp.int32, (1, 128), 1)
        head_lanes = lane < 16
        dt_ref[...] = jnp.where(head_lanes, sp, 0.0)
        gd_ref[...] = jnp.where(head_lanes, jax.nn.sigmoid(xv), jnp.where(lane < 32, jax.nn.sigmoid(-xv), 0.0))
        ac_ref[...] = jnp.where(head_lanes, acum, 0.0)
        c_ref[...] = c[:, 16:32]

    o16 = pl.BlockSpec((CHUNK, 16), lambda b, c: (b * nch + c, 0))
    o128 = pl.BlockSpec((CHUNK, 128), lambda b, c: (b * nch + c, 0))
    v128 = pl.BlockSpec((1, 128), lambda b, c: (0, 0))
    w128 = jax.ShapeDtypeStruct((n, 128), F32)
    return pl.pallas_call(
        body, name="head_scalars", grid=(bl, nch),
        in_specs=[pl.BlockSpec((CHUNK, 128), lambda b, c: (b * nch + c, col0)), v128, v128],
        out_specs=[o128, o128, o128, o16],
        out_shape=[w128, w128, w128, jax.ShapeDtypeStruct((n, 16), F32)],
        scratch_shapes=[pltpu.VMEM((1, 128), F32)],
        compiler_params=_cparams(("parallel", "arbitrary")),
    )(proj_a, bias128, alog128)


def _fpost(dc, gate_d, ddt, dpa, bl, t):
    n = bl * t
    nch = t // CHUNK
    col0 = (SSD_WIDTH + CONV_CH) // 128

    def body(dc_ref, gd_ref, ddt_ref, dpa_in, out_ref, db_ref, carry):
        @pl.when(pl.program_id(1) == 0)
        def _():
            carry[...] = jnp.zeros_like(carry)

        @pl.when((pl.program_id(0) == 0) & (pl.program_id(1) == 0))
        def _():
            db_ref[...] = jnp.zeros_like(db_ref)

        row = lax.broadcasted_iota(jnp.int32, (CHUNK, CHUNK), 0)
        col = lax.broadcasted_iota(jnp.int32, (CHUNK, CHUNK), 1)
        triu = (row <= col).astype(F32)
        dlf = jnp.dot(triu, dc_ref[...], precision=HIGHEST, preferred_element_type=F32) + carry[...]
        carry[...] = dlf[0:1, :]
        lane = lax.broadcasted_iota(jnp.int32, (1, 128), 1)
        df = jnp.where((lane >= 16) & (lane < 32), dlf * gd_ref[...], 0.0)
        out_ref[...] = (ddt_ref[...] + df).astype(BF16)
        db_ref[...] += jnp.sum(df, axis=0, keepdims=True)[:, 16:32]

    rev = lambda b, c: (b * nch + nch - 1 - c, 0)
    blk = pl.BlockSpec((CHUNK, 128), rev)
    return pl.pallas_call(
        body, name="forget_gate_bwd", grid=(bl, nch),
        in_specs=[blk, blk, blk, ANY],
        out_specs=[pl.BlockSpec((CHUNK, 128), lambda b, c: (b * nch + nch - 1 - c, col0)),
                   pl.BlockSpec((1, 16), lambda b, c: (0, 0))],
        out_shape=[jax.ShapeDtypeStruct(dpa.shape, dpa.dtype), jax.ShapeDtypeStruct((1, 16), F32)],
        input_output_aliases={3: 0},
        scratch_shapes=[pltpu.VMEM((1, 128), F32)],
        compiler_params=_cparams(("arbitrary", "arbitrary")),
    )(dc, gate_d, ddt, dpa)


CONV_TILE = 256
CONV_ROWS = 256


def _conv_taps(u_ref, i, w, bias):
    r0 = pl.multiple_of(i * CONV_ROWS, CONV_ROWS)
    cur = u_ref[pl.ds(r0, CONV_ROWS), :]
    p0 = pl.multiple_of(jnp.maximum(r0 - 8, 0), 8)
    prev = jnp.where(i > 0, u_ref[pl.ds(p0, 8), :], 0.0)
    cat = jnp.concatenate([prev, cur], axis=0)
    pre = bias + w[3:4, :] * cur
    taps = [cur]
    for s in (1, 2, 3):
        sh = pltpu.roll(cat, s, 0)[8:, :]
        taps.append(sh)
        pre = pre + w[3 - s:4 - s, :] * sh
    return r0, pre, taps


def _conv_fwd(proj_a, conv_w, conv_b, bl, t):
    n = bl * t
    nct = CONV_CH // CONV_TILE
    c0 = SSD_WIDTH // CONV_TILE

    def body(u_ref, w_ref, b_ref, o_ref):
        w = w_ref[...]
        bias = b_ref[...]

        def chunk(i, carry):
            r0, pre, _ = _conv_taps(u_ref, i, w, bias)
            o_ref[pl.ds(r0, CONV_ROWS), :] = pre * jax.nn.sigmoid(pre)
            return carry

        lax.fori_loop(0, t // CONV_ROWS, chunk, 0)

    return pl.pallas_call(
        body, name="conv_silu_fwd", grid=(bl, nct),
        in_specs=[pl.BlockSpec((t, CONV_TILE), lambda b, c: (b, c0 + c)),
                  pl.BlockSpec((4, CONV_TILE), lambda b, c: (0, c)),
                  pl.BlockSpec((1, CONV_TILE), lambda b, c: (0, c))],
        out_specs=pl.BlockSpec((t, CONV_TILE), lambda b, c: (b, c)),
        out_shape=jax.ShapeDtypeStruct((n, CONV_CH), F32),
        compiler_params=_cparams(("parallel", "parallel")),
    )(proj_a, conv_w, conv_b)


def _conv_bwd(dxc, proj_a, conv_w, conv_b, dpa, bl, t):
    nct = CONV_CH // CONV_TILE
    c0 = SSD_WIDTH // CONV_TILE
    nrc = t // CONV_ROWS

    def body(g_ref, u_ref, w_ref, b_ref, dpa_in, du_ref, dw_ref, db_ref, dp_scr):
        @pl.when(pl.program_id(1) == 0)
        def _():
            dw_ref[...] = jnp.zeros_like(dw_ref)
            db_ref[...] = jnp.zeros_like(db_ref)

        w = w_ref[...]
        bias = b_ref[...]
        dp_scr[pl.ds(t, 8), :] = jnp.zeros((8, CONV_TILE), F32)

        def chunk1(i, carry):
            dw0, dw1, dw2, dw3, db = carry
            r0, pre, taps = _conv_taps(u_ref, i, w, bias)
            sg = jax.nn.sigmoid(pre)
            dpre = g_ref[pl.ds(r0, CONV_ROWS), :] * (sg * (1.0 + pre * (1.0 - sg)))
            dp_scr[pl.ds(r0, CONV_ROWS), :] = dpre
            dw3 = dw3 + jnp.sum(dpre * taps[0], axis=0, keepdims=True)
            dw2 = dw2 + jnp.sum(dpre * taps[1], axis=0, keepdims=True)
            dw1 = dw1 + jnp.sum(dpre * taps[2], axis=0, keepdims=True)
            dw0 = dw0 + jnp.sum(dpre * taps[3], axis=0, keepdims=True)
            db = db + jnp.sum(dpre, axis=0, keepdims=True)
            return dw0, dw1, dw2, dw3, db

        z = jnp.zeros((1, CONV_TILE), F32)
        dw0, dw1, dw2, dw3, db = lax.fori_loop(0, nrc, chunk1, (z, z, z, z, z))
        dw_ref[...] += jnp.concatenate([dw0, dw1, dw2, dw3], axis=0)
        db_ref[...] += db

        def chunk2(i, carry):
            r0 = pl.multiple_of(i * CONV_ROWS, CONV_ROWS)
            cat = dp_scr[pl.ds(r0, CONV_ROWS + 8), :]
            du = w[3:4, :] * cat[:CONV_ROWS, :]
            for s in (1, 2, 3):
                du = du + w[3 - s:4 - s, :] * pltpu.roll(cat, CONV_ROWS + 8 - s, 0)[:CONV_ROWS, :]
            du_ref[pl.ds(r0, CONV_ROWS), :] = du.astype(BF16)
            return carry

        lax.fori_loop(0, nrc, chunk2, 0)

    return pl.pallas_call(
        body, name="conv_silu_bwd", grid=(nct, bl),
        in_specs=[pl.BlockSpec((t, CONV_TILE), lambda c, b: (b, c)),
                  pl.BlockSpec((t, CONV_TILE), lambda c, b: (b, c0 + c)),
                  pl.BlockSpec((4, CONV_TILE), lambda c, b: (0, c)),
                  pl.BlockSpec((1, CONV_TILE), lambda c, b: (0, c)), ANY],
        out_specs=[pl.BlockSpec((t, CONV_TILE), lambda c, b: (b, c0 + c)),
                   pl.BlockSpec((4, CONV_TILE), lambda c, b: (0, c)),
                   pl.BlockSpec((1, CONV_TILE), lambda c, b: (0, c))],
        out_shape=[jax.ShapeDtypeStruct(dpa.shape, dpa.dtype), jax.ShapeDtypeStruct((4, CONV_CH), F32),
                   jax.ShapeDtypeStruct((1, CONV_CH), F32)],
        input_output_aliases={4: 0},
        scratch_shapes=[pltpu.VMEM((t + 8, CONV_TILE), F32)],
        compiler_params=_cparams(("parallel", "arbitrary")),
    )(dxc, proj_a, conv_w, conv_b, dpa)


NT_DIMS = (((1,), (1,)), ((), ()))
TN_DIMS = (((0,), (0,)), ((), ()))


def _dot(a, b, dims=None):
    if dims is None:
        return jnp.dot(a, b, preferred_element_type=F32)
    return lax.dot_general(a, b, dims, preferred_element_type=F32)


def _ssd_fwd(xc, proj_a, dt, acum, acum_t, dskip_e, norm_w, bl, t):
    n = bl * t
    nch = t // CHUNK
    L = CHUNK

    def body(xc_ref, z_ref, dt_ref, ac_ref, act_ref, dsk_ref, nw_ref, ys_ref, yp_ref, hp_ref, h_scr, y_scr):
        @pl.when(pl.program_id(1) == 0)
        def _():
            h_scr[...] = jnp.zeros_like(h_scr)

        row = lax.broadcasted_iota(jnp.int32, (L, L), 0)
        col = lax.broadcasted_iota(jnp.int32, (L, L), 1)
        causal = row >= col
        dt_all = dt_ref[...]
        ac_all = ac_ref[...]
        act_all = act_ref[...]
        for g in range(2):
            bg = xc_ref[:, SSD_WIDTH + g * 128:SSD_WIDTH + (g + 1) * 128].astype(BF16)
            cg = xc_ref[:, SSD_WIDTH + 256 + g * 128:SSD_WIDTH + 256 + (g + 1) * 128].astype(BF16)
            gmat = _dot(cg, bg, NT_DIMS)
            for r in range(8):
                h = g * 8 + r
                sl = slice(h * HEAD_DIM, (h + 1) * HEAD_DIM)
                xs = xc_ref[:, sl]
                xdt = xs * dt_all[:, h:h + 1]
                ac = ac_all[:, h:h + 1]
                ar = act_all[h:h + 1, :]
                ldec = jnp.exp(jnp.where(causal, ac - ar, NEG))
                m = (gmat * ldec).astype(BF16)
                hp = h_scr[h]
                hp_ref[h] = hp
                yd = _dot(m, xdt.astype(BF16))
                yo = _dot(cg, hp.astype(BF16), NT_DIMS) * jnp.exp(ac)
                y_scr[:, sl] = yd + yo + dsk_ref[:, sl] * xs
                alast = ac_all[L - 1:L, h:h + 1]
                xd = (xdt * jnp.exp(alast - ac)).astype(BF16)
                h_scr[h] = jnp.exp(alast) * hp + _dot(xd, bg, TN_DIMS)
        y = y_scr[...]
        yp_ref[...] = y
        zv = z_ref[...]
        yg = y * (zv * jax.nn.sigmoid(zv))
        for g in range(2):
            gs = slice(g * 512, (g + 1) * 512)
            grp = yg[:, gs]
            rstd = lax.rsqrt(jnp.mean(grp * grp, axis=1, keepdims=True) + EPS)
            ys_ref[:, gs] = (grp * rstd * nw_ref[:, gs]).astype(BF16)

    rb = lambda b, c: (b * nch + c, 0)
    v1k = pl.BlockSpec((1, SSD_WIDTH), lambda b, c: (0, 0))
    return pl.pallas_call(
        body, name="ssd_fwd", grid=(bl, nch),
        in_specs=[pl.BlockSpec((L, CONV_CH), rb), pl.BlockSpec((L, SSD_WIDTH), rb),
                  pl.BlockSpec((L, 16), rb), pl.BlockSpec((L, 16), rb),
                  pl.BlockSpec((16, L), lambda b, c: (0, b * nch + c)), v1k, v1k],
        out_specs=[pl.BlockSpec((L, SSD_WIDTH), rb), pl.BlockSpec((L, SSD_WIDTH), rb),
                   pl.BlockSpec((None, 16, HEAD_DIM, SSD_STATE), lambda b, c: (b * nch + c, 0, 0, 0))],
        out_shape=[jax.ShapeDtypeStruct((n, SSD_WIDTH), BF16), jax.ShapeDtypeStruct((n, SSD_WIDTH), F32),
                   jax.ShapeDtypeStruct((bl * nch, 16, HEAD_DIM, SSD_STATE), F32)],
        scratch_shapes=[pltpu.VMEM((16, HEAD_DIM, SSD_STATE), F32), pltpu.VMEM((L, SSD_WIDTH), F32)],
        compiler_params=_cparams(("parallel", "arbitrary")),
    )(xc, proj_a, dt, acum, acum_t, dskip_e, norm_w)


def _ssd_bwd(dys, xc, proj_a, ypre, hprev, dt, sig, acum, acum_t, a_log, dskip_e, norm_w, bl, t):
    n = bl * t
    nch = t // CHUNK
    L = CHUNK

    def body(dys_ref, xc_ref, z_ref, yp_ref, hp_ref, dt_ref, sg_ref, ac_ref, act_ref, al_ref, dsk_ref, nw_ref,
             dxc_ref, dz_ref, ddt_ref, dnw_ref, dsk16_ref, da16_ref, db16_ref, dh_scr, dy_scr):
        first = (pl.program_id(0) == 0) & (pl.program_id(1) == 0)

        @pl.when(first)
        def _():
            dnw_ref[...] = jnp.zeros_like(dnw_ref)
            dsk16_ref[...] = jnp.zeros_like(dsk16_ref)
            da16_ref[...] = jnp.zeros_like(da16_ref)
            db16_ref[...] = jnp.zeros_like(db16_ref)

        @pl.when(pl.program_id(1) == 0)
        def _():
            dh_scr[...] = jnp.zeros_like(dh_scr)

        y = yp_ref[...]
        zv = z_ref[...]
        sz = jax.nn.sigmoid(zv)
        gate = zv * sz
        yg = y * gate
        dout = dys_ref[...]
        nw = nw_ref[...]
        for g in range(2):
            gs = slice(g * 512, (g + 1) * 512)
            grp = yg[:, gs]
            rstd = lax.rsqrt(jnp.mean(grp * grp, axis=1, keepdims=True) + EPS)
            ghat = grp * rstd
            dnw_ref[:, gs] += jnp.sum(dout[:, gs] * ghat, axis=0, keepdims=True)
            gw = dout[:, gs] * nw[:, gs]
            dyg = rstd * (gw - ghat * jnp.mean(gw * ghat, axis=1, keepdims=True))
            dy_scr[:, gs] = dyg * gate[:, gs]
            dz_ref[:, gs] = (dyg * y[:, gs] * (sz[:, gs] * (1.0 + zv[:, gs] * (1.0 - sz[:, gs])))).astype(BF16)

        row = lax.broadcasted_iota(jnp.int32, (L, L), 0)
        col = lax.broadcasted_iota(jnp.int32, (L, L), 1)
        causal = row >= col
        lane16 = lax.broadcasted_iota(jnp.int32, (1, 16), 1)
        lane128 = lax.broadcasted_iota(jnp.int32, (1, L), 1)
        last_row = lax.broadcasted_iota(jnp.int32, (L, 1), 0) == (L - 1)
        dt_all = dt_ref[...]
        ac_all = ac_ref[...]
        act_all = act_ref[...]
        dac_col = jnp.zeros((L, L), F32)
        dac_row = jnp.zeros((L, L), F32)
        ddt_x = jnp.zeros((L, L), F32)
        dsk16 = jnp.zeros((1, 16), F32)
        rows16 = lax.broadcasted_iota(jnp.int32, (L, 1), 0)
        for g in range(2):
            bsl = slice(SSD_WIDTH + g * 128, SSD_WIDTH + (g + 1) * 128)
            csl = slice(SSD_WIDTH + 256 + g * 128, SSD_WIDTH + 256 + (g + 1) * 128)
            bg = xc_ref[:, bsl].astype(BF16)
            cg = xc_ref[:, csl].astype(BF16)
            gmat = _dot(cg, bg, NT_DIMS)
            dg_sum = jnp.zeros((L, L), F32)
            dc_acc = jnp.zeros((L, SSD_STATE), F32)
            db_acc = jnp.zeros((L, SSD_STATE), F32)
            for r in range(8):
                h = g * 8 + r
                sl = slice(h * HEAD_DIM, (h + 1) * HEAD_DIM)
                onehot = lane16 == h
                onehot_w = lane128 == h
                xs = xc_ref[:, sl]
                dth = dt_all[:, h:h + 1]
                xdt = xs * dth
                xb = xdt.astype(BF16)
                ac = ac_all[:, h:h + 1]
                ar = act_all[h:h + 1, :]
                alast = ac_all[L - 1:L, h:h + 1]
                ldec = jnp.exp(jnp.where(causal, ac - ar, NEG))
                mf = gmat * ldec
                e_in = jnp.exp(ac)
                dec = jnp.exp(alast - ac)
                elast = jnp.exp(alast)
                hp = hp_ref[h]
                hpb = hp.astype(BF16)
                dyh = dy_scr[:, sl]
                dyb = dyh.astype(BF16)
                dsk16 = dsk16 + jnp.where(onehot, jnp.sum(jnp.sum(dyh * xs, axis=1, keepdims=True), axis=0, keepdims=True), 0.0)
                dm = _dot(dyb, xb, NT_DIMS)
                dx = _dot(mf.astype(BF16), dyb, TN_DIMS)
                dg_sum = dg_sum + dm * ldec
                wmat = dm * mf
                dac_h = jnp.sum(wmat, axis=1, keepdims=True)
                dac_row = dac_row + jnp.where(rows16 == h, -jnp.sum(wmat, axis=0, keepdims=True), 0.0)
                ch = _dot(cg, hpb, NT_DIMS)
                dye = dyh * e_in
                dyeb = dye.astype(BF16)
                dc_acc = dc_acc + _dot(dyeb, hpb)
                dhp = _dot(dyeb, cg, TN_DIMS)
                dac_h = dac_h + jnp.sum(dye * ch, axis=1, keepdims=True)
                ds = dh_scr[h]
                dsb = ds.astype(BF16)
                dxd = _dot(bg, dsb, NT_DIMS)
                db_acc = db_acc + _dot((xdt * dec).astype(BF16), dsb)
                dx = dx + dxd * dec
                ddec = jnp.sum(dxd * xdt, axis=1, keepdims=True) * dec
                extra = (jnp.sum(ddec, axis=0, keepdims=True)
                         + elast * jnp.sum(jnp.sum(hp * ds, axis=1, keepdims=True), axis=0, keepdims=True))
                dac_h = dac_h - ddec + jnp.where(last_row, extra, 0.0)
                dh_scr[h] = elast * ds + dhp
                dac_col = dac_col + jnp.where(onehot_w, dac_h, 0.0)
                ddt_x = ddt_x + jnp.where(onehot_w, jnp.sum(dx * xs, axis=1, keepdims=True), 0.0)
                dxc_ref[:, sl] = dx * dth + dsk_ref[:, sl] * dyh
            dgb = dg_sum.astype(BF16)
            dxc_ref[:, csl] = dc_acc + _dot(dgb, bg)
            dxc_ref[:, bsl] = db_acc + _dot(dgb, cg, TN_DIMS)
        dac = dac_col + jnp.transpose(dac_row)
        triu = (row <= col).astype(F32)
        da = jnp.dot(triu, dac, precision=HIGHEST, preferred_element_type=F32)[:, 0:16]
        a_row = -jnp.exp(al_ref[...])
        ddt = (ddt_x[:, 0:16] + da * a_row) * sg_ref[...]
        ddt_ref[...] = ddt
        dsk16_ref[...] += dsk16
        da16_ref[...] += jnp.sum(da * dt_all, axis=0, keepdims=True) * a_row
        db16_ref[...] += jnp.sum(ddt, axis=0, keepdims=True)

    rb = lambda b, c: (b * nch + nch - 1 - c, 0)
    v1k = pl.BlockSpec((1, SSD_WIDTH), lambda b, c: (0, 0))
    v16 = pl.BlockSpec((1, 16), lambda b, c: (0, 0))
    wide = pl.BlockSpec((L, SSD_WIDTH), rb)
    s16 = pl.BlockSpec((L, 16), rb)
    return pl.pallas_call(
        body, name="ssd_bwd", grid=(bl, nch),
        in_specs=[wide, pl.BlockSpec((L, CONV_CH), rb), wide, wide,
                  pl.BlockSpec((None, 16, HEAD_DIM, SSD_STATE), lambda b, c: (b * nch + nch - 1 - c, 0, 0, 0)),
                  s16, s16, s16, pl.BlockSpec((16, L), lambda b, c: (0, b * nch + nch - 1 - c)), v16, v1k, v1k],
        out_specs=[pl.BlockSpec((L, CONV_CH), rb), wide, s16, v1k, v16, v16, v16],
        out_shape=[jax.ShapeDtypeStruct((n, CONV_CH), F32), jax.ShapeDtypeStruct((n, SSD_WIDTH), BF16),
                   jax.ShapeDtypeStruct((n, 16), F32), jax.ShapeDtypeStruct((1, SSD_WIDTH), F32),
                   jax.ShapeDtypeStruct((1, 16), F32), jax.ShapeDtypeStruct((1, 16), F32),
                   jax.ShapeDtypeStruct((1, 16), F32)],
        scratch_shapes=[pltpu.VMEM((16, HEAD_DIM, SSD_STATE), F32), pltpu.VMEM((L, SSD_WIDTH), F32)],
        compiler_params=_cparams(("arbitrary", "arbitrary")),
    )(dys, xc, proj_a, ypre, hprev, dt, sig, acum, acum_t, a_log, dskip_e, norm_w)


def _head_expander():
    r = lax.broadcasted_iota(jnp.int32, (128, SSD_WIDTH), 0)
    c = lax.broadcasted_iota(jnp.int32, (128, SSD_WIDTH), 1)
    return ((c // HEAD_DIM == r % 16) & (r < 48)).astype(BF16)


def _spread(v128, expander):
    hi = v128.astype(BF16).astype(F32)
    r1 = v128 - hi
    mid = r1.astype(BF16).astype(F32)
    lo = (r1 - mid).astype(BF16).astype(F32)
    packed = (hi + pltpu.roll(mid, 16, 1) + pltpu.roll(lo, 32, 1)).astype(BF16)
    return jnp.dot(packed, expander, preferred_element_type=F32)


def _head_sums(v1024, expander):
    hi = v1024.astype(BF16)
    lo = (v1024 - hi.astype(F32)).astype(BF16)
    heads = jnp.where(lax.broadcasted_iota(jnp.int32, (128, SSD_WIDTH), 0) < 16, expander, jnp.zeros_like(expander))
    return _dot(hi, heads, NT_DIMS) + _dot(lo, heads, NT_DIMS)


def _ssd_fwd2(xc, proj_a, dt, acum, acum_t, dskip_e, norm_w, bl, t):
    n = bl * t
    nch = t // CHUNK
    L = CHUNK

    def body(xc_ref, z_ref, dt_ref, ac_ref, act_ref, dsk_ref, nw_ref, ys_ref, yp_ref, hp_ref, h_scr, y_scr, x_scr):
        @pl.when(pl.program_id(1) == 0)
        def _():
            h_scr[...] = jnp.zeros_like(h_scr)

        row = lax.broadcasted_iota(jnp.int32, (L, L), 0)
        col = lax.broadcasted_iota(jnp.int32, (L, L), 1)
        causal = row >= col
        expander = _head_expander()
        ac_all = ac_ref[...]
        act_all = act_ref[...]
        ac_e = _spread(ac_all, expander)
        e_in = jnp.exp(ac_e)
        dec = jnp.exp(ac_e[L - 1:L, :] - ac_e)
        xs_all = xc_ref[:, 0:SSD_WIDTH]
        x_all = xs_all * _spread(dt_ref[...], expander)
        x_scr[...] = x_all.astype(BF16)
        hp_all = h_scr[...]
        hp_ref[...] = hp_all
        for g in range(2):
            gs = slice(g * 512, (g + 1) * 512)
            bg = xc_ref[:, SSD_WIDTH + g * 128:SSD_WIDTH + (g + 1) * 128].astype(BF16)
            cg = xc_ref[:, SSD_WIDTH + 256 + g * 128:SSD_WIDTH + 256 + (g + 1) * 128].astype(BF16)
            gmat = _dot(cg, bg, NT_DIMS)
            y_scr[:, gs] = (_dot(cg, hp_all[gs, :].astype(BF16), NT_DIMS) * e_in[:, gs]
                            + dsk_ref[:, gs] * xs_all[:, gs])
            s_new = _dot((x_all[:, gs] * dec[:, gs]).astype(BF16), bg, TN_DIMS)
            for r in range(8):
                h = g * 8 + r
                sl = slice(h * HEAD_DIM, (h + 1) * HEAD_DIM)
                ldec = jnp.exp(jnp.where(causal, ac_all[:, h:h + 1] - act_all[h:h + 1, :], NEG))
                y_scr[:, sl] += _dot((gmat * ldec).astype(BF16), x_scr[:, sl])
                elast = jnp.exp(ac_all[L - 1:L, h:h + 1])
                h_scr[sl, :] = elast * hp_all[sl, :] + s_new[r * HEAD_DIM:(r + 1) * HEAD_DIM, :]
        y = y_scr[...]
        yp_ref[...] = y
        zv = z_ref[...]
        yg = y * (zv * jax.nn.sigmoid(zv))
        for g in range(2):
            gs = slice(g * 512, (g + 1) * 512)
            grp = yg[:, gs]
            rstd = lax.rsqrt(jnp.mean(grp * grp, axis=1, keepdims=True) + EPS)
            ys_ref[:, gs] = (grp * rstd * nw_ref[:, gs]).astype(BF16)

    rb = lambda b, c: (b * nch + c, 0)
    v1k = pl.BlockSpec((1, SSD_WIDTH), lambda b, c: (0, 0))
    return pl.pallas_call(
        body, name="ssd_fwd", grid=(bl, nch),
        in_specs=[pl.BlockSpec((L, CONV_CH), rb), pl.BlockSpec((L, SSD_WIDTH), rb),
                  pl.BlockSpec((L, 128), rb), pl.BlockSpec((L, 128), rb),
                  pl.BlockSpec((16, L), lambda b, c: (0, b * nch + c)), v1k, v1k],
        out_specs=[pl.BlockSpec((L, SSD_WIDTH), rb), pl.BlockSpec((L, SSD_WIDTH), rb),
                   pl.BlockSpec((None, SSD_WIDTH, SSD_STATE), lambda b, c: (b * nch + c, 0, 0))],
        out_shape=[jax.ShapeDtypeStruct((n, SSD_WIDTH), BF16), jax.ShapeDtypeStruct((n, SSD_WIDTH), F32),
                   jax.ShapeDtypeStruct((bl * nch, SSD_WIDTH, SSD_STATE), F32)],
        scratch_shapes=[pltpu.VMEM((SSD_WIDTH, SSD_STATE), F32), pltpu.VMEM((L, SSD_WIDTH), F32),
                        pltpu.VMEM((L, SSD_WIDTH), BF16)],
        compiler_params=_cparams(("parallel", "arbitrary")),
    )(xc, proj_a, dt, acum, acum_t, dskip_e, norm_w)


def _ssd_bwd2(dys, xc, proj_a, ypre, hprev, dt, gate_d, acum, acum_t, alog128, dskip_e, norm_w, bl, t):
    n = bl * t
    nch = t // CHUNK
    L = CHUNK

    def body(dys_ref, xc_ref, z_ref, yp_ref, hp_ref, dt_ref, gd_ref, ac_ref, act_ref, al_ref, dsk_ref, nw_ref,
             dxc_ref, dz_ref, ddt_ref, dnw_ref, dsk16_ref, da16_ref, db16_ref,
             dh_scr, dy_scr, x_scr, dx_scr, red_scr):
        first = (pl.program_id(0) == 0) & (pl.program_id(1) == 0)

        @pl.when(first)
        def _():
            dnw_ref[...] = jnp.zeros_like(dnw_ref)
            dsk16_ref[...] = jnp.zeros_like(dsk16_ref)
            da16_ref[...] = jnp.zeros_like(da16_ref)
            db16_ref[...] = jnp.zeros_like(db16_ref)

        @pl.when(pl.program_id(1) == 0)
        def _():
            dh_scr[...] = jnp.zeros_like(dh_scr)

        y = yp_ref[...]
        zv = z_ref[...]
        sz = jax.nn.sigmoid(zv)
        gate = zv * sz
        yg = y * gate
        dout = dys_ref[...]
        nw = nw_ref[...]
        for g in range(2):
            gs = slice(g * 512, (g + 1) * 512)
            grp = yg[:, gs]
            rstd = lax.rsqrt(jnp.mean(grp * grp, axis=1, keepdims=True) + EPS)
            ghat = grp * rstd
            dnw_ref[:, gs] += jnp.sum(dout[:, gs] * ghat, axis=0, keepdims=True)
            gw = dout[:, gs] * nw[:, gs]
            dyg = rstd * (gw - ghat * jnp.mean(gw * ghat, axis=1, keepdims=True))
            dy_scr[:, gs] = dyg * gate[:, gs]
            dz_ref[:, gs] = (dyg * y[:, gs] * (sz[:, gs] * (1.0 + zv[:, gs] * (1.0 - sz[:, gs])))).astype(BF16)

        row = lax.broadcasted_iota(jnp.int32, (L, L), 0)
        col = lax.broadcasted_iota(jnp.int32, (L, L), 1)
        causal = row >= col
        lane128 = lax.broadcasted_iota(jnp.int32, (1, L), 1)
        rows128 = lax.broadcasted_iota(jnp.int32, (L, 1), 0)
        last_row = rows128 == (L - 1)
        expander = _head_expander()
        ac_all = ac_ref[...]
        act_all = act_ref[...]
        dt_all = dt_ref[...]
        dt_e = _spread(dt_all, expander)
        ac_e = _spread(ac_all, expander)
        e_in = jnp.exp(ac_e)
        dec = jnp.exp(ac_e[L - 1:L, :] - ac_e)
        xs_all = xc_ref[:, 0:SSD_WIDTH]
        x_all = xs_all * dt_e
        x_scr[...] = x_all.astype(BF16)
        dy_all = dy_scr[...]
        hp_all = hp_ref[...]
        ds_all = dh_scr[...]
        dsk_cols = jnp.sum(dy_all * xs_all, axis=0, keepdims=True)
        dac = jnp.zeros((L, L), F32)
        dac_row = jnp.zeros((L, L), F32)
        ddec_cols = []
        for g in range(2):
            gs = slice(g * 512, (g + 1) * 512)
            bsl = slice(SSD_WIDTH + g * 128, SSD_WIDTH + (g + 1) * 128)
            csl = slice(SSD_WIDTH + 256 + g * 128, SSD_WIDTH + 256 + (g + 1) * 128)
            bg = xc_ref[:, bsl].astype(BF16)
            cg = xc_ref[:, csl].astype(BF16)
            gmat = _dot(cg, bg, NT_DIMS)
            hpb = hp_all[gs, :].astype(BF16)
            dsb = ds_all[gs, :].astype(BF16)
            ch = _dot(cg, hpb, NT_DIMS)
            dye = dy_all[:, gs] * e_in[:, gs]
            dyeb = dye.astype(BF16)
            dc_acc = _dot(dyeb, hpb)
            dhp = _dot(dyeb, cg, TN_DIMS)
            dxd = _dot(bg, dsb, NT_DIMS)
            db_acc = _dot((x_all[:, gs] * dec[:, gs]).astype(BF16), dsb)
            ddec = dxd * x_all[:, gs] * dec[:, gs]
            ddec_cols.append(jnp.sum(ddec, axis=0, keepdims=True))
            dx_scr[:, gs] = dxd * dec[:, gs]
            red_scr[:, gs] = dye * ch - ddec
            dg_sum = jnp.zeros((L, L), F32)
            for r in range(8):
                h = g * 8 + r
                sl = slice(h * HEAD_DIM, (h + 1) * HEAD_DIM)
                onehot_w = lane128 == h
                ldec = jnp.exp(jnp.where(causal, ac_all[:, h:h + 1] - act_all[h:h + 1, :], NEG))
                mf = gmat * ldec
                dyb = dy_scr[:, sl].astype(BF16)
                dm = _dot(dyb, x_scr[:, sl], NT_DIMS)
                dx_scr[:, sl] += _dot(mf.astype(BF16), dyb, TN_DIMS)
                dg_sum = dg_sum + dm * ldec
                wmat = dm * mf
                elast = jnp.exp(ac_all[L - 1:L, h:h + 1])
                hp_h = hp_all[sl, :]
                ds_h = ds_all[sl, :]
                extra = elast * jnp.sum(jnp.sum(hp_h * ds_h, axis=1, keepdims=True), axis=0, keepdims=True)
                dac = dac + jnp.where(onehot_w, jnp.sum(wmat, axis=1, keepdims=True) + jnp.where(last_row, extra, 0.0),
                                      0.0)
                dac_row = dac_row + jnp.where(rows128 == h, -jnp.sum(wmat, axis=0, keepdims=True), 0.0)
                dh_scr[sl, :] = elast * ds_h + dhp[r * HEAD_DIM:(r + 1) * HEAD_DIM, :]
            dgb = dg_sum.astype(BF16)
            dxc_ref[:, csl] = dc_acc + _dot(dgb, bg)
            dxc_ref[:, bsl] = db_acc + _dot(dgb, cg, TN_DIMS)
        dx_all = dx_scr[...]
        dxc_ref[:, 0:SSD_WIDTH] = dx_all * dt_e + dsk_ref[...] * dy_all
        red = red_scr[...]
        dac_slab = _head_sums(red, expander)
        ddec_tot = _head_sums(jnp.broadcast_to(jnp.concatenate(ddec_cols, axis=1), (8, SSD_WIDTH)), expander)
        ddt_x = _head_sums(dx_all * xs_all, expander)
        dsk16_ref[...] += _head_sums(jnp.broadcast_to(dsk_cols, (8, SSD_WIDTH)), expander)[0:1, 0:16]
        dac = dac + dac_slab + jnp.transpose(dac_row) + jnp.where(last_row, ddec_tot[0:1, :], 0.0)
        triu = (row <= col).astype(F32)
        da = jnp.dot(triu, dac, precision=HIGHEST, preferred_element_type=F32)
        a_row = -jnp.exp(al_ref[...])
        ddt = jnp.where(lane128 < 16, (ddt_x + da * a_row) * gd_ref[...], 0.0)
        ddt_ref[...] = ddt
        da16_ref[...] += (jnp.sum(da * dt_all, axis=0, keepdims=True) * a_row)[:, 0:16]
        db16_ref[...] += jnp.sum(ddt, axis=0, keepdims=True)[:, 0:16]

    rb = lambda b, c: (b * nch + nch - 1 - c, 0)
    v1k = pl.BlockSpec((1, SSD_WIDTH), lambda b, c: (0, 0))
    v16 = pl.BlockSpec((1, 16), lambda b, c: (0, 0))
    v128 = pl.BlockSpec((1, 128), lambda b, c: (0, 0))
    wide = pl.BlockSpec((L, SSD_WIDTH), rb)
    s128 = pl.BlockSpec((L, 128), rb)
    return pl.pallas_call(
        body, name="ssd_bwd", grid=(bl, nch),
        in_specs=[wide, pl.BlockSpec((L, CONV_CH), rb), wide, wide,
                  pl.BlockSpec((None, SSD_WIDTH, SSD_STATE), lambda b, c: (b * nch + nch - 1 - c, 0, 0)),
                  s128, s128, s128, pl.BlockSpec((16, L), lambda b, c: (0, b * nch + nch - 1 - c)), v128, v1k, v1k],
        out_specs=[pl.BlockSpec((L, CONV_CH), rb), wide, s128, v1k, v16, v16, v16],
        out_shape=[jax.ShapeDtypeStruct((n, CONV_CH), F32), jax.ShapeDtypeStruct((n, PA_WIDTH), BF16),
                   jax.ShapeDtypeStruct((n, 128), F32), jax.ShapeDtypeStruct((1, SSD_WIDTH), F32),
                   jax.ShapeDtypeStruct((1, 16), F32), jax.ShapeDtypeStruct((1, 16), F32),
                   jax.ShapeDtypeStruct((1, 16), F32)],
        scratch_shapes=[pltpu.VMEM((SSD_WIDTH, SSD_STATE), F32), pltpu.VMEM((L, SSD_WIDTH), F32),
                        pltpu.VMEM((L, SSD_WIDTH), BF16), pltpu.VMEM((L, SSD_WIDTH), F32),
                        pltpu.VMEM((L, SSD_WIDTH), F32)],
        compiler_params=_cparams(("arbitrary", "arbitrary")),
    )(dys, xc, proj_a, ypre, hprev, dt, gate_d, acum, acum_t, alog128, dskip_e, norm_w)


def _attn_fwd(qkv, negc, bl, t):
    n = bl * t
    tb_ = ATT_BLOCK
    nb = t // tb_
    scale2 = LOG2E / math.sqrt(HEAD_DIM)

    def body(q_ref, k_ref, v_ref, c_ref, o_ref, lse_ref):
        row = lax.broadcasted_iota(jnp.int32, (tb_, tb_), 0)
        col = lax.broadcasted_iota(jnp.int32, (tb_, tb_), 1)
        causal = row >= col
        for qi in range(nb):
            r0, lk = qi * tb_, (qi + 1) * tb_
            for j in range(2):
                sl = slice(j * HEAD_DIM, (j + 1) * HEAD_DIM)
                s = _dot(q_ref[r0:lk, sl], k_ref[0:lk, sl], NT_DIMS) * scale2 + c_ref[j:j + 1, 0:lk] * LOG2E
                tail = jnp.where(causal, s[:, r0:lk], NEG)
                s = tail if qi == 0 else jnp.concatenate([s[:, 0:r0], tail], axis=1)
                m = jnp.max(s, axis=1, keepdims=True)
                p = jnp.exp2(s - m)
                l = jnp.sum(p, axis=1, keepdims=True)
                acc = _dot(p.astype(BF16), v_ref[0:lk, sl])
                o_ref[r0:lk, sl] = (acc / l).astype(BF16)
                lse_ref[r0:lk, sl] = jnp.broadcast_to(m + jnp.log(l) * LOG2E, (tb_, HEAD_DIM))

    blk = lambda off: pl.BlockSpec((t, 128), lambda b, hp: (b, off + hp))
    return pl.pallas_call(
        body, name="fox_attn_fwd", grid=(bl, 8),
        in_specs=[blk(0), blk(8), blk(16), pl.BlockSpec((None, None, 8, t), lambda b, hp: (b, hp, 0, 0))],
        out_specs=[blk(0), blk(0)],
        out_shape=[jax.ShapeDtypeStruct((n, ATT_WIDTH), BF16), jax.ShapeDtypeStruct((n, ATT_WIDTH), F32)],
        compiler_params=_cparams(("parallel", "parallel")),
    )(qkv, qkv, qkv, negc)


def _attn_bwd(qkv, do, o, lse, negc, after, bl, t):
    n = bl * t
    tb_ = ATT_BLOCK
    nb = t // tb_
    scale = 1.0 / math.sqrt(HEAD_DIM)
    scale2 = LOG2E * scale

    def body(q_ref, k_ref, v_ref, do_ref, o_ref, lse_ref, c_ref, after_ref, dq_ref, dk_ref, dv_ref, dc_ref,
             dq_scr, delta_scr, dr_scr, qt_scr, dot_scr, dkt_scr, dvt_scr):
        row = lax.broadcasted_iota(jnp.int32, (tb_, tb_), 0)
        col = lax.broadcasted_iota(jnp.int32, (tb_, tb_), 1)
        causal = row >= col
        dq_scr[...] = jnp.zeros_like(dq_scr)
        dr_scr[...] = jnp.zeros_like(dr_scr)
        dc_ref[...] = jnp.zeros_like(dc_ref)
        qt_scr[...] = jnp.transpose(q_ref[...].astype(F32)).astype(BF16)
        dot_scr[...] = jnp.transpose(do_ref[...].astype(F32)).astype(BF16)
        prod = do_ref[...].astype(F32) * o_ref[...].astype(F32)
        for j in range(2):
            sl = slice(j * HEAD_DIM, (j + 1) * HEAD_DIM)
            delta_scr[:, sl] = jnp.broadcast_to(jnp.sum(prod[:, sl], axis=1, keepdims=True), (t, HEAD_DIM))
        for kj in range(nb):
            r0, r1 = kj * tb_, (kj + 1) * tb_
            for j in range(2):
                sl = slice(j * HEAD_DIM, (j + 1) * HEAD_DIM)
                one = slice(j * HEAD_DIM, j * HEAD_DIM + 1)
                kb = k_ref[r0:r1, sl]
                qs = q_ref[r0:t, sl]
                dos = do_ref[r0:t, sl]
                s = _dot(qs, kb, NT_DIMS) * scale2 + c_ref[j:j + 1, r0:r1] * LOG2E
                head = jnp.where(causal, s[0:tb_, :], NEG)
                s = head if kj == nb - 1 else jnp.concatenate([head, s[tb_:, :]], axis=0)
                p = jnp.exp2(s - lse_ref[r0:t, one])
                dp = _dot(dos, v_ref[r0:r1, sl], NT_DIMS)
                ds = p * (dp - delta_scr[r0:t, one])
                dsb = ds.astype(BF16)
                dvt_scr[sl, r0:r1] = _dot(dot_scr[sl, r0:t], p.astype(BF16))
                dkt_scr[sl, r0:r1] = _dot(qt_scr[sl, r0:t], dsb)
                dq_scr[r0:t, sl] += _dot(dsb, kb)
                dr_scr[r0:t, sl] += jnp.broadcast_to(jnp.sum(ds, axis=1, keepdims=True), (t - r0, HEAD_DIM))
                dc_ref[j:j + 1, r0:r1] = -jnp.sum(ds, axis=0, keepdims=True)
        dq_ref[...] = (dq_scr[...] * scale).astype(BF16)
        dk_ref[...] = (jnp.transpose(dkt_scr[...]) * scale).astype(BF16)
        dv_ref[...] = jnp.transpose(dvt_scr[...]).astype(BF16)
        dr_t = jnp.transpose(dr_scr[...])
        for j in range(2):
            dc_ref[j:j + 1, :] += dr_t[j * HEAD_DIM:j * HEAD_DIM + 1, :]

    blk = lambda off: pl.BlockSpec((t, 128), lambda b, hp: (b, off + hp))
    cblk = pl.BlockSpec((None, None, 8, t), lambda b, hp: (b, hp, 0, 0))
    return pl.pallas_call(
        body, name="fox_attn_bwd", grid=(bl, 8),
        in_specs=[blk(0), blk(8), blk(16), blk(0), blk(0), blk(0), cblk, ANY],
        out_specs=[blk(0), blk(0), blk(0), cblk],
        out_shape=[jax.ShapeDtypeStruct((n, ATT_WIDTH), BF16)] * 3 + [jax.ShapeDtypeStruct((bl, 8, 8, t), F32)],
        scratch_shapes=[pltpu.VMEM((t, 128), F32), pltpu.VMEM((t, 128), F32), pltpu.VMEM((t, 128), F32),
                        pltpu.VMEM((128, t), BF16), pltpu.VMEM((128, t), BF16),
                        pltpu.VMEM((128, t), F32), pltpu.VMEM((128, t), F32)],
        compiler_params=_cparams(("parallel", "parallel")),
    )(qkv, qkv, qkv, do, o, lse, negc, after)


def _adamw(w, g, m, v, *, name):
    lead = w.ndim == 3
    r, c = w.shape[-2:]
    tr = _pick(r, (256, 128, 64, 32, 16, 8))
    bc1 = 1.0 - ADAM_B1 ** ADAM_STEP
    bc2 = 1.0 - ADAM_B2 ** ADAM_STEP

    def body(w_ref, g_ref, m_ref, v_ref, d_ref, nm_ref, nv_ref):
        gv = g_ref[...]
        mn = ADAM_B1 * m_ref[...] + (1.0 - ADAM_B1) * gv
        vn = ADAM_B2 * v_ref[...] + (1.0 - ADAM_B2) * (gv * gv)
        m_hat = mn / bc1
        v_hat = vn / bc2
        d_ref[...] = -ADAM_LR * (m_hat / (jnp.sqrt(v_hat) + ADAM_EPS) + ADAM_WD * w_ref[...])
        nm_ref[...] = mn
        nv_ref[...] = vn

    flat = pl.BlockSpec((tr, c), lambda i: (i, 0))
    blk = pl.BlockSpec((None, tr, c), lambda i: (0, i, 0)) if lead else flat
    return pl.pallas_call(
        body, name=name, grid=(r // tr,), in_specs=[blk, flat, blk, blk], out_specs=[blk] * 3,
        out_shape=[jax.ShapeDtypeStruct(w.shape, F32)] * 3,
        compiler_params=_cparams(("parallel",)),
    )(w, g, m, v)


def _sum_leading(parts, *, name, out_dtype=F32):
    k, r, c = parts.shape
    tr = _pick(r, (512, 256, 128, 96, 64, 32, 16, 8))

    def body(p_ref, o_ref):
        acc = p_ref[0].astype(F32)
        for i in range(1, k):
            acc = acc + p_ref[i].astype(F32)
        o_ref[...] = acc.astype(out_dtype)

    return pl.pallas_call(
        body, name=name, grid=(r // tr,),
        in_specs=[pl.BlockSpec((k, tr, c), lambda i: (0, i, 0))],
        out_specs=pl.BlockSpec((tr, c), lambda i: (i, 0)),
        out_shape=jax.ShapeDtypeStruct((r, c), out_dtype),
        compiler_params=_cparams(("parallel",)),
    )(parts)


def _add_pair(a, b, *, name):
    k, r, c = a.shape
    tr = _pick(r, (512, 256, 128))

    def body(a_ref, b_ref, o_ref):
        o_ref[...] = (a_ref[...].astype(F32) + b_ref[...].astype(F32)).astype(BF16)

    blk = pl.BlockSpec((None, tr, c), lambda j, i: (j, i, 0))
    return pl.pallas_call(
        body, name=name, grid=(k, r // tr), in_specs=[blk, blk], out_specs=blk,
        out_shape=jax.ShapeDtypeStruct((k, r, c), BF16),
        compiler_params=_cparams(("parallel", "parallel")),
    )(a, b)


ANY = pl.BlockSpec(memory_space=pl.ANY)


def _chip_peers(x, y):
    return [(1 - x, y, 2 * (1 - x) + y), (x, 1 - y, 2 * x + 1 - y), (1 - x, 1 - y, 2 * (1 - x) + 1 - y)]


def _gather_weights(blob, *, name):
    rows, cols = blob.shape
    half_rows = rows // 2

    def body(b_ref, o_ref, send_sems, recv_sems):
        x, y, c = lax.axis_index("x"), lax.axis_index("y"), lax.axis_index("c")
        me = 2 * x + y
        sibling = (x, y, 1 - c)
        peers = _chip_peers(x, y)

        def half(chip, hc):
            return o_ref.at[chip, pl.ds(hc * half_rows, half_rows), :]

        def copy(k, src, chip, hc, to):
            return pltpu.make_async_remote_copy(src_ref=src, dst_ref=half(chip, hc), send_sem=send_sems.at[k],
                                                recv_sem=recv_sems.at[k], device_id=to, device_id_type=MESH)

        my_half = b_ref.at[pl.ds(c * half_rows, half_rows), :]
        first = [copy(k, my_half, me, c, (px, py, c)) for k, (px, py, _) in enumerate(peers)]
        for cp in first:
            cp.start()
        passed = [copy(3 + k, half(pc, c), pc, c, sibling) for k, (_, _, pc) in enumerate(peers)]
        for k, (px, py, pc) in enumerate(peers):
            copy(k, my_half, pc, c, (px, py, c)).wait_recv()
            passed[k].start()
        for k, (_, _, pc) in enumerate(peers):
            copy(3 + k, half(pc, 1 - c), pc, 1 - c, sibling).wait_recv()
        for cp in first + passed:
            cp.wait_send()

    return pl.pallas_call(
        body, name=name, in_specs=[ANY], out_specs=ANY,
        out_shape=jax.ShapeDtypeStruct((N_CHIPS, rows, cols), BF16),
        scratch_shapes=[pltpu.SemaphoreType.DMA((6,)), pltpu.SemaphoreType.DMA((6,))],
    )(blob)


def _swap_halves(g, *, name):
    _, rows, cols = g.shape
    half_rows = rows // 2

    def body(g_ref, o_ref, send_sem, recv_sem):
        x, y, c = lax.axis_index("x"), lax.axis_index("y"), lax.axis_index("c")
        cp = pltpu.make_async_remote_copy(
            src_ref=g_ref.at[:, pl.ds((1 - c) * half_rows, half_rows), :], dst_ref=o_ref,
            send_sem=send_sem, recv_sem=recv_sem, device_id=(x, y, 1 - c), device_id_type=MESH)
        cp.start()
        cp.wait()

    return pl.pallas_call(
        body, name=name, in_specs=[ANY], out_specs=ANY,
        out_shape=jax.ShapeDtypeStruct((N_CHIPS, half_rows, cols), BF16),
        scratch_shapes=[pltpu.SemaphoreType.DMA, pltpu.SemaphoreType.DMA],
    )(g)


def _exchange_chips(p, *, name):
    def body(p_ref, o_ref, send_sems, recv_sems):
        x, y, c = lax.axis_index("x"), lax.axis_index("y"), lax.axis_index("c")
        me = 2 * x + y
        peers = _chip_peers(x, y)
        cps = [pltpu.make_async_remote_copy(src_ref=p_ref.at[pc], dst_ref=o_ref.at[me], send_sem=send_sems.at[k],
                                            recv_sem=recv_sems.at[k], device_id=(px, py, c), device_id_type=MESH)
               for k, (px, py, pc) in enumerate(peers)]
        for cp in cps:
            cp.start()
        for k, (px, py, pc) in enumerate(peers):
            pltpu.make_async_remote_copy(src_ref=p_ref.at[pc], dst_ref=o_ref.at[pc], send_sem=send_sems.at[k],
                                         recv_sem=recv_sems.at[k], device_id=(px, py, c),
                                         device_id_type=MESH).wait_recv()
        for cp in cps:
            cp.wait_send()

    got = pl.pallas_call(
        body, name=name, in_specs=[ANY], out_specs=ANY,
        out_shape=jax.ShapeDtypeStruct(p.shape, BF16),
        scratch_shapes=[pltpu.SemaphoreType.DMA((3,)), pltpu.SemaphoreType.DMA((3,))],
    )(p)
    me = 2 * lax.axis_index("x") + lax.axis_index("y")
    return lax.dynamic_update_slice(got, lax.dynamic_slice_in_dim(p, me, 1, axis=0), (me, 0, 0))


HBM_SPEC = pl.BlockSpec(memory_space=pltpu.HBM)
SEM_SPEC = pl.BlockSpec(memory_space=pltpu.SEMAPHORE)
SPLIT_EFFECT = pltpu.SideEffectType.DATAFLOW_SIDE_EFFECTING


def _gather_peers_copies(b_ref, land_ref, send_sems, recv_sems, sending):
    x, y, c = lax.axis_index("x"), lax.axis_index("y"), lax.axis_index("c")
    me = 2 * x + y
    half_rows = b_ref.shape[0] // 2
    src = b_ref.at[pl.ds(c * half_rows, half_rows), :]
    return [pltpu.make_async_remote_copy(
        src_ref=src, dst_ref=land_ref.at[me if sending else pc, pl.ds(c * half_rows, half_rows), :],
        send_sem=send_sems.at[k], recv_sem=recv_sems.at[k], device_id=(px, py, c), device_id_type=MESH)
        for k, (px, py, pc) in enumerate(_chip_peers(x, y))]


def _gather_start(blob, after, *, name):
    shape = (N_CHIPS,) + blob.shape

    def body(b_ref, land_ref, after_ref, send_sems, recv_sems, b_thru, land_thru, token):
        for cp in _gather_peers_copies(b_ref, land_ref, send_sems, recv_sems, True):
            cp.start()
        token[...] = jnp.zeros_like(token)

    return pl.pallas_call(
        body, name=name,
        out_shape=(pltpu.SemaphoreType.DMA((3,)), pltpu.SemaphoreType.DMA((3,)), pltpu.HBM(blob.shape, blob.dtype),
                   pltpu.HBM(shape, blob.dtype), jax.ShapeDtypeStruct((8, 128), F32)),
        in_specs=(HBM_SPEC, HBM_SPEC, ANY),
        out_specs=(SEM_SPEC, SEM_SPEC, HBM_SPEC, HBM_SPEC, pl.BlockSpec(memory_space=pltpu.VMEM)),
        input_output_aliases={0: 2, 1: 3},
        compiler_params=pltpu.CompilerParams(has_side_effects=SPLIT_EFFECT),
    )(pltpu.with_memory_space_constraint(blob, pltpu.HBM),
      pltpu.with_memory_space_constraint(lax.empty(shape, blob.dtype), pltpu.HBM), after)


def _gather_wait(send_sems, recv_sems, b_thru, land_thru, after, *, name):
    def body(b_ref, land_ref, send_sems, recv_sems, after_ref, b_dead, got_ref):
        for cp in _gather_peers_copies(b_ref, land_ref, send_sems, recv_sems, False):
            cp.wait_send()
            cp.wait_recv()

    return pl.pallas_call(
        body, name=name,
        out_shape=(pltpu.HBM(b_thru.shape, b_thru.dtype), pltpu.HBM(land_thru.shape, land_thru.dtype)),
        in_specs=(HBM_SPEC, HBM_SPEC, SEM_SPEC, SEM_SPEC, ANY), out_specs=(HBM_SPEC, HBM_SPEC),
        input_output_aliases={0: 0, 1: 1},
        compiler_params=pltpu.CompilerParams(has_side_effects=SPLIT_EFFECT),
    )(b_thru, land_thru, send_sems, recv_sems, after)


def _gather_forward(land, *, name):
    half_rows = land.shape[1] // 2

    def body(l_ref, o_ref, send_sems, recv_sems):
        x, y, c = lax.axis_index("x"), lax.axis_index("y"), lax.axis_index("c")
        cps = []
        for k, (_, _, pc) in enumerate(_chip_peers(x, y)):
            mine = pl.ds(c * half_rows, half_rows)
            cps.append(pltpu.make_async_remote_copy(
                src_ref=l_ref.at[pc, mine, :], dst_ref=o_ref.at[pc, mine, :], send_sem=send_sems.at[k],
                recv_sem=recv_sems.at[k], device_id=(x, y, 1 - c), device_id_type=MESH))
        for cp in cps:
            cp.start()
        for k, (_, _, pc) in enumerate(_chip_peers(x, y)):
            theirs = pl.ds((1 - c) * half_rows, half_rows)
            pltpu.make_async_remote_copy(
                src_ref=l_ref.at[pc, theirs, :], dst_ref=o_ref.at[pc, theirs, :], send_sem=send_sems.at[k],
                recv_sem=recv_sems.at[k], device_id=(x, y, 1 - c), device_id_type=MESH).wait_recv()
        for cp in cps:
            cp.wait_send()

    return pl.pallas_call(
        body, name=name, in_specs=[ANY], out_specs=ANY, input_output_aliases={0: 0},
        out_shape=jax.ShapeDtypeStruct(land.shape, land.dtype),
        scratch_shapes=[pltpu.SemaphoreType.DMA((3,)), pltpu.SemaphoreType.DMA((3,))],
    )(land)


def _exchange_peers_copies(p_ref, land_ref, send_sems, recv_sems, sending):
    x, y, c = lax.axis_index("x"), lax.axis_index("y"), lax.axis_index("c")
    me = 2 * x + y
    return [pltpu.make_async_remote_copy(src_ref=p_ref.at[pc], dst_ref=land_ref.at[me if sending else pc],
                                         send_sem=send_sems.at[k], recv_sem=recv_sems.at[k],
                                         device_id=(px, py, c), device_id_type=MESH)
            for k, (px, py, pc) in enumerate(_chip_peers(x, y))]


def _exchange_start(p, *, name):
    def body(p_ref, land_ref, send_sems, recv_sems, p_thru, land_thru, token):
        for cp in _exchange_peers_copies(p_ref, land_ref, send_sems, recv_sems, True):
            cp.start()
        token[...] = jnp.zeros_like(token)

    return pl.pallas_call(
        body, name=name,
        out_shape=(pltpu.SemaphoreType.DMA((3,)), pltpu.SemaphoreType.DMA((3,)), pltpu.HBM(p.shape, p.dtype),
                   pltpu.HBM(p.shape, p.dtype), jax.ShapeDtypeStruct((8, 128), F32)),
        in_specs=(HBM_SPEC, HBM_SPEC),
        out_specs=(SEM_SPEC, SEM_SPEC, HBM_SPEC, HBM_SPEC, pl.BlockSpec(memory_space=pltpu.VMEM)),
        input_output_aliases={0: 2, 1: 3},
        compiler_params=pltpu.CompilerParams(has_side_effects=SPLIT_EFFECT),
    )(pltpu.with_memory_space_constraint(p, pltpu.HBM),
      pltpu.with_memory_space_constraint(lax.empty(p.shape, p.dtype), pltpu.HBM))


def _exchange_wait(send_sems, recv_sems, p_thru, land_thru, after, *, name):
    def body(p_ref, land_ref, send_sems, recv_sems, after_ref, p_dead, got_ref):
        for cp in _exchange_peers_copies(p_ref, land_ref, send_sems, recv_sems, False):
            cp.wait_send()
            cp.wait_recv()

    return pl.pallas_call(
        body, name=name,
        out_shape=(pltpu.HBM(p_thru.shape, p_thru.dtype), pltpu.HBM(p_thru.shape, p_thru.dtype)),
        in_specs=(HBM_SPEC, HBM_SPEC, SEM_SPEC, SEM_SPEC, ANY), out_specs=(HBM_SPEC, HBM_SPEC),
        input_output_aliases={0: 0, 1: 1},
        compiler_params=pltpu.CompilerParams(has_side_effects=SPLIT_EFFECT),
    )(p_thru, land_thru, send_sems, recv_sems, after)


def _sum_parts(parts, own, *, name):
    k, r, c = parts.shape
    tr = _pick(r, (512, 256, 128))

    def body(p_ref, own_ref, o_ref):
        me = 2 * lax.axis_index("x") + lax.axis_index("y")
        acc = jnp.zeros((tr, c), F32)
        for i in range(k):
            acc = acc + jnp.where(me == i, own_ref[i], p_ref[i]).astype(F32)
        o_ref[...] = acc

    blk = pl.BlockSpec((k, tr, c), lambda i: (0, i, 0))
    return pl.pallas_call(
        body, name=name, grid=(r // tr,), in_specs=[blk, blk],
        out_specs=pl.BlockSpec((tr, c), lambda i: (i, 0)),
        out_shape=jax.ShapeDtypeStruct((r, c), F32),
        compiler_params=_cparams(("parallel",)),
    )(parts, own)


def _join_halves(gh, *, name):
    def body(g_ref, o_ref, send_sem, recv_sem):
        x, y, c = lax.axis_index("x"), lax.axis_index("y"), lax.axis_index("c")
        cp = pltpu.make_async_remote_copy(src_ref=g_ref, dst_ref=o_ref, send_sem=send_sem, recv_sem=recv_sem,
                                          device_id=(x, y, 1 - c), device_id_type=MESH)
        cp.start()
        cp.wait()

    other = pl.pallas_call(
        body, name=name, in_specs=[ANY], out_specs=ANY,
        out_shape=jax.ShapeDtypeStruct(gh.shape, F32),
        scratch_shapes=[pltpu.SemaphoreType.DMA, pltpu.SemaphoreType.DMA],
    )(gh)
    south = lax.axis_index("c") == 0
    return jnp.concatenate([jnp.where(south, gh, other), jnp.where(south, other, gh)], axis=0)


def _gather_small(s, *, name):
    rows = s.shape[0]

    def body(s_ref, o_ref, send_sems, recv_sems, local_sem):
        x, y, c = lax.axis_index("x"), lax.axis_index("y"), lax.axis_index("c")
        me = 4 * x + 2 * y + c
        mine = pltpu.make_async_copy(s_ref, o_ref.at[me], local_sem)
        mine.start()
        peers = []
        for k in range(1, 8):
            peers.append((1 - x if k & 4 else x, 1 - y if k & 2 else y, 1 - c if k & 1 else c))
        cps = [pltpu.make_async_remote_copy(src_ref=s_ref, dst_ref=o_ref.at[me], send_sem=send_sems.at[k],
                                            recv_sem=recv_sems.at[k], device_id=p, device_id_type=MESH)
               for k, p in enumerate(peers)]
        for cp in cps:
            cp.start()
        for k, (px, py, pc) in enumerate(peers):
            pltpu.make_async_remote_copy(src_ref=s_ref, dst_ref=o_ref.at[4 * px + 2 * py + pc],
                                         send_sem=send_sems.at[k], recv_sem=recv_sems.at[k],
                                         device_id=(px, py, pc), device_id_type=MESH).wait_recv()
        for cp in cps:
            cp.wait_send()
        mine.wait()

    return pl.pallas_call(
        body, name=name, in_specs=[ANY], out_specs=ANY,
        out_shape=jax.ShapeDtypeStruct((8, rows, 128), F32),
        scratch_shapes=[pltpu.SemaphoreType.DMA((7,)), pltpu.SemaphoreType.DMA((7,)), pltpu.SemaphoreType.DMA],
    )(s)


IN_SHARD = IN_WIDTH // N_CHIPS
IN_SHARD_PAD = 1536
UP_ROWS, DOWN_ROWS, OUT_ROWS = 1024, 1024, 512
REST_ROWS = UP_ROWS + DOWN_ROWS + OUT_ROWS


def _pack_in(w_in_s):
    return jnp.pad(w_in_s, ((0, 0), (0, IN_SHARD_PAD - IN_SHARD))).astype(BF16)


def _pack_rest(w_out_s, w_up_s, w_down_s):
    return jnp.concatenate([w_up_s, w_down_s, w_out_s], axis=0).astype(BF16)


def _unpack_rest(blob):
    return (blob[UP_ROWS + DOWN_ROWS:], blob[0:UP_ROWS], blob[UP_ROWS:UP_ROWS + DOWN_ROWS])


def _with_own(gathered, own):
    me = 2 * lax.axis_index("x") + lax.axis_index("y")
    return [jnp.where(me == j, own, gathered[j]) for j in range(N_CHIPS)]


def _full_w_in(g_in, own):
    return jnp.concatenate([s[:, :IN_SHARD] for s in _with_own(g_in, own)], axis=1)


def _full_rest(g_rest, own):
    parts = [_unpack_rest(s) for s in _with_own(g_rest, own)]
    w_out = jnp.concatenate([p[0] for p in parts], axis=0)
    w_up = jnp.concatenate([p[1] for p in parts], axis=1)
    w_down = jnp.concatenate([p[2] for p in parts], axis=0)
    return w_out, w_up, w_down


def _split_w_in(w_in):
    z_xbc = w_in[:, 0:2560]
    dt = w_in[:, 2560:2576]
    qkv = w_in[:, 2576:5648]
    f = w_in[:, 5648:5664]
    pad = jnp.zeros((w_in.shape[0], PA_WIDTH - 2592), w_in.dtype)
    return jnp.concatenate([z_xbc, dt, f, pad], axis=1), qkv


def _merge_w_in(d_a, d_qkv):
    return jnp.concatenate([d_a[:, 0:2560], d_a[:, 2560:2576], d_qkv, d_a[:, 2576:2592]], axis=1)


def _local_step(x3, target3, w_in, rest_weights, norm_mix_w, conv_w, conv_b, dt_bias, a_log, d_skip,
                ssd_norm_w, f_bias, norm_mlp_w, norm_final_w, first_after=None, early_grads=None, late_grads=None):
    bl, t, d = x3.shape
    n = bl * t
    x = x3.reshape(n, d)
    target = target3.reshape(n, d)
    w_a, w_qkv = _split_w_in(w_in)
    nfw = norm_final_w.reshape(1, d)
    dskip_e = jnp.repeat(d_skip, HEAD_DIM, axis=1)
    nb = t // ATT_BLOCK

    h0, rstd0 = _rmsnorm_fwd(x, norm_mix_w, name="norm_mix_fwd")
    r1, r2, r4, kt = min(n, 1024), min(n, 512), min(n, 256), min(n, 512)
    proj_a = _mm(h0, w_a, name="proj_a", tiles=(r2, PA_WIDTH, D_MODEL), after=first_after)
    qkv = _mm(h0, w_qkv, name="proj_qkv", tiles=(r2, QKV_WIDTH, D_MODEL), out_dtype=BF16)
    bias128 = jnp.concatenate([dt_bias, f_bias, jnp.zeros((1, 96), F32)], axis=1)
    alog128 = jnp.concatenate([a_log, jnp.zeros((1, 112), F32)], axis=1)
    dt, gate_d, acum, ccum = _prep(proj_a, bias128, alog128, bl, t)
    acum_t = acum[:, 0:16].T
    negc = jnp.pad(-ccum.reshape(bl, t, 8, 2).transpose(0, 2, 3, 1), ((0, 0), (0, 0), (0, 6), (0, 0)))
    xc = _conv_fwd(proj_a, conv_w, conv_b, bl, t)
    y_ssd, y_pre, hprev = _ssd_fwd2(xc, proj_a, dt, acum, acum_t, dskip_e, ssd_norm_w, bl, t)
    y_att, lse = _attn_fwd(qkv, negc, bl, t)
    w_out, w_up, w_down = rest_weights(y_att)
    wo_s, wo_a = w_out[:SSD_WIDTH], w_out[SSD_WIDTH:]
    t1 = _mm(y_ssd, wo_s, name="out_proj_ssd", tiles=(r1, D_MODEL, SSD_WIDTH), res=x)
    h1, h1n, rstd1 = _mm_norm_fwd(y_att, wo_a, t1, norm_mlp_w, name="out_proj_att_norm_mlp", tm=r2)
    up = _mm(h1n, w_up, name="mlp_up", tiles=(r2, D_FF, D_MODEL), out_dtype=BF16)
    dh2, dh2b, loss, d_nfw = _mm_final(up, w_down, h1, nfw, target, name="mlp_down_final_norm_loss", tm=r2,
                                       a_act="relu2")

    dup = _mm(dh2b, w_down, name="mlp_down_bwd_act", tiles=(r2, D_FF, D_MODEL), tb=True, epi_up=up, out_dtype=BF16)
    rest_shape = (N_CHIPS, REST_ROWS, D_MODEL)
    gb_rest = _mm(up, dh2b, name="mlp_down_bwd_w", tiles=(DOWN_ROWS, D_MODEL, kt), ta=True, a_act="relu2",
                  out_dtype=BF16, into=(rest_shape, (None, DOWN_ROWS, D_MODEL), lambda i, j, k: (i, 1, 0), None))
    dh1, dh1b, d_nmlp = _mm_norm_bwd(dup, w_up, None, h1, rstd1, norm_mlp_w, dh2, name="mlp_up_bwd_act_norm_mlp",
                                     tm=r2)
    gb_rest = _mm(h1n, dup, name="mlp_up_bwd_w", tiles=(D_MODEL, UP_ROWS, kt), ta=True, out_dtype=BF16,
                  into=(rest_shape, (None, D_MODEL, UP_ROWS), lambda i, j, k: (j, 0, 0), gb_rest))
    dys = _mm(dh1b, wo_s, name="out_proj_bwd_ssd", tiles=(r1, SSD_WIDTH, D_MODEL), tb=True)
    do = _mm(dh1b, wo_a, name="out_proj_bwd_att", tiles=(r1, ATT_WIDTH, D_MODEL), tb=True, out_dtype=BF16)
    out_block = (UP_ROWS + DOWN_ROWS) // OUT_ROWS
    for half, (y_half, tag) in enumerate(((y_ssd, "ssd"), (y_att, "att"))):
        gb_rest = _mm(y_half, dh1b, name="out_proj_bwd_w_" + tag, tiles=(OUT_ROWS, D_MODEL, kt), ta=True,
                      out_dtype=BF16, into=(rest_shape, (None, OUT_ROWS, D_MODEL),
                                            functools.partial(lambda i, j, k, h: (2 * h + i, out_block, 0), h=half),
                                            gb_rest))
    token = jnp.zeros((8, 128), F32) if early_grads is None else early_grads(gb_rest)
    dq, dk, dv, dcb = _attn_bwd(qkv, do, y_att, lse, negc, token, bl, t)
    dc = jnp.pad(dcb[:, :, 0:2, :].transpose(0, 3, 1, 2).reshape(n, 16), ((0, 0), (16, 96)))
    dxc, dpa, ddt_raw, d_snw, d_dsk, d_alog, d_dtb = _ssd_bwd2(dys, xc, proj_a, y_pre, hprev, dt, gate_d, acum,
                                                             acum_t, alog128, dskip_e, ssd_norm_w, bl, t)
    dpa, d_conv_w, d_conv_b = _conv_bwd(dxc, proj_a, conv_w, conv_b, dpa, bl, t)
    dproj_a, d_fb = _fpost(dc, gate_d, ddt_raw, dpa, bl, t)
    dqkv = jnp.concatenate([dq, dk, dv], axis=1)
    d_w_a = _mm(h0, dproj_a, name="proj_a_bwd_w", tiles=(1024, 896, kt), ta=True, out_dtype=BF16)
    d_w_qkv = _mm(h0, dqkv, name="proj_qkv_bwd_w", tiles=(1024, 1024, kt), ta=True, out_dtype=BF16)
    d_w_in = _merge_w_in(d_w_a, d_w_qkv)
    late_token = None if late_grads is None else late_grads(d_w_in)
    t2 = _mm(dproj_a, w_a, name="proj_a_bwd_act", tiles=(r1, D_MODEL, PA_WIDTH), tb=True, after=late_token)
    dx, _, d_nmix = _mm_norm_bwd(dqkv, w_qkv, t2, x, rstd0, norm_mix_w, dh1, name="proj_qkv_bwd_act_norm_mix",
                                 tm=r2)

    grads = dict(norm_mix_w=d_nmix, w_in=d_w_in, conv_w=d_conv_w, conv_b=d_conv_b,
                 dt_bias=d_dtb, a_log=d_alog, d_skip=d_dsk, ssd_norm_w=d_snw, f_bias=d_fb, rest=gb_rest,
                 norm_mlp_w=d_nmlp, norm_final_w=d_nfw)
    return dx.reshape(bl, t, d), loss, grads


SMALL_ORDER = ("norm_mix_w", "conv_w", "conv_b", "dt_bias", "a_log", "d_skip", "ssd_norm_w", "f_bias",
               "norm_mlp_w", "norm_final_w")
SMALL_SIZES = (1024, 4 * CONV_CH, CONV_CH, 16, 16, 16, 1024, 16, 1024, 1024)


def _pack_small(vals, rows):
    flat = jnp.concatenate([v.reshape(-1).astype(F32) for v in vals])
    return jnp.pad(flat, (0, rows * 128 - flat.shape[0])).reshape(rows, 128)


def _unpack_small(packed, sizes):
    flat = packed.reshape(-1)
    out, o = [], 0
    for s in sizes:
        out.append(flat[o:o + s])
        o += s
    return out


def kernel(x, norm_mix_w, w_in, conv_w, conv_b, dt_bias, a_log, d_skip, ssd_norm_w, f_bias, w_out, norm_mlp_w, w_up, w_down, norm_final_w, loss_target, m_norm_mix_w, m_w_in, m_conv_w, m_conv_b, m_dt_bias, m_a_log, m_d_skip, m_ssd_norm_w, m_f_bias, m_w_out, m_norm_mlp_w, m_w_up, m_w_down, m_norm_final_w, v_norm_mix_w, v_w_in, v_conv_w, v_conv_b, v_dt_bias, v_a_log, v_d_skip, v_ssd_norm_w, v_f_bias, v_w_out, v_norm_mlp_w, v_w_up, v_w_down, v_norm_final_w):
    chip = 2 * lax.axis_index("x") + lax.axis_index("y")
    cw = CONV_CH // N_CHIPS

    own_in = _pack_in(w_in[0])
    own_rest = _pack_rest(w_out[0], w_up[0], w_down[0])
    g_in = _gather_weights(own_in, name="gather_w_in")
    w_in_f = _full_w_in(g_in, own_in)
    *rest_handles, rest_token = _gather_start(own_rest, g_in, name="gather_start_rest")

    def rest_weights(after):
        _, landed = _gather_wait(*rest_handles, after, name="gather_wait_rest")
        return _full_rest(_gather_forward(landed, name="gather_forward_rest"), own_rest)
    small_all = _gather_small(_pack_small([conv_w[0]], 16), name="gather_conv_w")
    conv_w_f = jnp.concatenate([small_all[2 * j].reshape(-1)[:4 * cw].reshape(4, cw) for j in range(N_CHIPS)], axis=1)

    c = lax.axis_index("c")

    def chip_partial(gb, tag):
        half_rows = gb.shape[1] // 2
        from_sibling = _swap_halves(gb, name="grad_swap_halves_" + tag)
        my_half = lax.dynamic_slice_in_dim(gb, c * half_rows, half_rows, axis=1)
        return _add_pair(my_half, from_sibling, name="grad_add_sibling_" + tag)

    in_flight = {}

    def early_grads(gb_rest):
        part = chip_partial(gb_rest, "rest")
        *handles, token = _exchange_start(part, name="grad_exchange_start_rest")
        in_flight["rest"] = handles
        return token

    def late_grads(d_w_in):
        gb_in = jnp.stack([_pack_in(d_w_in[:, j * IN_SHARD:(j + 1) * IN_SHARD]) for j in range(N_CHIPS)])
        *handles, token = _exchange_start(chip_partial(gb_in, "in"), name="grad_exchange_start_in")
        in_flight["in"] = handles
        return token

    dx, loss_part, g = _local_step(x, loss_target, w_in_f, rest_weights, norm_mix_w, conv_w_f,
                                   conv_b, dt_bias, a_log, d_skip, ssd_norm_w, f_bias, norm_mlp_w, norm_final_w,
                                   first_after=rest_token, early_grads=early_grads, late_grads=late_grads)

    send_sems, recv_sems, part_rest, land_rest = in_flight["rest"]
    part_rest, parts_rest = _exchange_wait(send_sems, recv_sems, part_rest, land_rest, dx,
                                           name="grad_exchange_wait_rest")
    g_rest_half = _sum_parts(parts_rest, part_rest, name="grad_sum_chips_rest")
    g_w_out, g_w_up, g_w_down = _unpack_rest(_join_halves(g_rest_half, name="grad_join_halves_rest"))

    part_in, parts_in = _exchange_wait(*in_flight["in"], dx, name="grad_exchange_wait_in")
    g_in_half = _sum_parts(parts_in, part_in, name="grad_sum_chips_in")
    g_w_in = _join_halves(g_in_half, name="grad_join_halves_in")[:, :IN_SHARD]

    small_vals = [g[k] for k in SMALL_ORDER] + [loss_part[:, 0:1]]
    small_sum = _sum_leading(_gather_small(_pack_small(small_vals, SMALL_ROWS), name="gather_small_grads"), name="small_sum")
    sg = dict(zip(SMALL_ORDER + ("loss",), _unpack_small(small_sum, SMALL_SIZES + (1,))))
    loss = sg["loss"].reshape(())
    g_conv_full = sg["conv_w"].reshape(4, CONV_CH)
    g_conv = lax.dynamic_slice_in_dim(g_conv_full, chip * cw, cw, axis=1)

    grads = dict(norm_mix_w=sg["norm_mix_w"].reshape(1, -1), w_in=g_w_in[None], conv_w=g_conv[None],
                 conv_b=sg["conv_b"].reshape(1, -1), dt_bias=sg["dt_bias"].reshape(1, -1),
                 a_log=sg["a_log"].reshape(1, -1), d_skip=sg["d_skip"].reshape(1, -1),
                 ssd_norm_w=sg["ssd_norm_w"].reshape(1, -1), f_bias=sg["f_bias"].reshape(1, -1), w_out=g_w_out[None],
                 norm_mlp_w=sg["norm_mlp_w"].reshape(1, -1), w_up=g_w_up[None], w_down=g_w_down[None],
                 norm_final_w=sg["norm_final_w"])
    weights = dict(norm_mix_w=norm_mix_w, w_in=w_in, conv_w=conv_w, conv_b=conv_b, dt_bias=dt_bias, a_log=a_log,
                   d_skip=d_skip, ssd_norm_w=ssd_norm_w, f_bias=f_bias, w_out=w_out, norm_mlp_w=norm_mlp_w,
                   w_up=w_up, w_down=w_down, norm_final_w=norm_final_w)
    ms = dict(norm_mix_w=m_norm_mix_w, w_in=m_w_in, conv_w=m_conv_w, conv_b=m_conv_b, dt_bias=m_dt_bias,
              a_log=m_a_log, d_skip=m_d_skip, ssd_norm_w=m_ssd_norm_w, f_bias=m_f_bias, w_out=m_w_out,
              norm_mlp_w=m_norm_mlp_w, w_up=m_w_up, w_down=m_w_down, norm_final_w=m_norm_final_w)
    vs = dict(norm_mix_w=v_norm_mix_w, w_in=v_w_in, conv_w=v_conv_w, conv_b=v_conv_b, dt_bias=v_dt_bias,
              a_log=v_a_log, d_skip=v_d_skip, ssd_norm_w=v_ssd_norm_w, f_bias=v_f_bias, w_out=v_w_out,
              norm_mlp_w=v_norm_mlp_w, w_up=v_w_up, w_down=v_w_down, norm_final_w=v_norm_final_w)
    names = list(weights)
    big = ("w_in", "w_out", "w_up", "w_down")
    delta, new_m, new_v = {}, {}, {}
    for k, g2 in zip(big, (g_w_in, g_w_out, g_w_up, g_w_down)):
        delta[k], new_m[k], new_v[k] = _adamw(weights[k], g2, ms[k], vs[k], name="adamw_" + k)
    smalls = [k for k in names if k not in big]
    sizes = [math.prod(weights[k].shape) for k in smalls]
    rows = -(-sum(sizes) // 1024) * 8
    packs = [_pack_small([d[k] for k in smalls], rows) for d in (weights, grads, ms, vs)]
    outs = _adamw(*packs, name="adamw_small")
    for o, dst in zip(outs, (delta, new_m, new_v)):
        for k, val in zip(smalls, _unpack_small(o, sizes)):
            dst[k] = val.reshape(weights[k].shape)
    return (loss, dx, *[grads[k] for k in names], *[delta[k] for k in names], *[new_m[k] for k in names],
            *[new_v[k] for k in names])
```

```python
import functools
import math

import jax
import jax.numpy as jnp
from jax import lax
from jax.experimental import pallas as pl
from jax.experimental.pallas import tpu as pltpu

F32 = jnp.float32
BF16 = jnp.bfloat16
HIGHEST = lax.Precision.HIGHEST
MESH = pl.DeviceIdType.MESH

D_MODEL = 1024
SSD_HEADS = 16
HEAD_DIM = 64
SSD_WIDTH = 1024
SSD_STATE = 128
CONV_CH = 1536
CHUNK = 128
ATT_WIDTH = 1024
EPS = 1e-5
IN_WIDTH = 5664
PA_WIDTH = 2688
QKV_WIDTH = 3072
D_FF = 4096
ATT_BLOCK = 256
NEG = -1e30
LOG2E = 1.4426950408889634
VMEM_LIMIT = 48 * 1024 * 1024

ADAM_LR = 0.001
ADAM_B1 = 0.9
ADAM_B2 = 0.999
ADAM_EPS = 1e-08
ADAM_WD = 0.01
ADAM_STEP = 10

N_CHIPS = 4
BLOB_ROWS = 4096
HALF_ROWS = BLOB_ROWS // 2
SMALL_ROWS = 96


def _cparams(sem):
    return pltpu.CompilerParams(dimension_semantics=sem, vmem_limit_bytes=VMEM_LIMIT)


def _pick(n, cands):
    for c in cands:
        if n % c == 0:
            return c
    return n


MM_CHUNK = 512


def _mm(a, b, *, name, tiles, ta=False, tb=False, out_dtype=F32, res=None, a_act=None, epi_up=None, after=None,
        into=None):
    n_unread = (after is not None) + (into is not None and into[3] is not None)
    if ta:
        K, M = a.shape
    else:
        M, K = a.shape
    if tb:
        N, K2 = b.shape
    else:
        K2, N = b.shape
    assert K == K2, (a.shape, b.shape)
    tm, tn, tk = tiles
    assert M % tm == 0 and N % tn == 0 and K % tk == 0, (name, M, N, K, tiles)
    nk = K // tk
    dn = (((0 if ta else 1,), (1 if tb else 0,)), ((), ()))
    has_res = res is not None
    has_up = epi_up is not None
    cn = _pick(tn, (MM_CHUNK, 384, 256, 128))

    def prologue(av):
        if a_act == "relu2":
            r = jnp.maximum(av.astype(F32), 0.0)
            av = r * r
        return av.astype(BF16)

    def epilogue(out, res_v, up_v):
        if has_res:
            out = out + res_v.astype(F32)
        if has_up:
            out = out * (2.0 * jnp.maximum(up_v.astype(F32), 0.0))
        return out.astype(out_dtype)

    def body(*refs):
        a_ref, b_ref = refs[0], refs[1]
        i = 2
        res_ref = up_ref = None
        if has_res:
            res_ref = refs[i]
            i += 1
        if has_up:
            up_ref = refs[i]
            i += 1
        i += n_unread
        o_ref = refs[i]
        if nk == 1:
            av = prologue(a_ref[...])
            for c in range(tn // cn):
                cs = slice(c * cn, (c + 1) * cn)
                bv = (b_ref[cs, :] if tb else b_ref[:, cs]).astype(BF16)
                out = lax.dot_general(av, bv, dn, preferred_element_type=F32)
                o_ref[:, cs] = epilogue(out, res_ref[:, cs] if has_res else None, up_ref[:, cs] if has_up else None)
            return
        acc_ref = refs[i + 1]
        k = pl.program_id(2)

        @pl.when(k == 0)
        def _():
            acc_ref[...] = jnp.zeros_like(acc_ref)

        acc_ref[...] += lax.dot_general(prologue(a_ref[...]), b_ref[...].astype(BF16), dn,
                                        preferred_element_type=F32)

        @pl.when(k == nk - 1)
        def _():
            o_ref[...] = epilogue(acc_ref[...], res_ref[...] if has_res else None, up_ref[...] if has_up else None)

    a_spec = pl.BlockSpec((tk, tm), lambda i, j, k: (k, i)) if ta else pl.BlockSpec((tm, tk), lambda i, j, k: (i, k))
    b_spec = pl.BlockSpec((tn, tk), lambda i, j, k: (j, k)) if tb else pl.BlockSpec((tk, tn), lambda i, j, k: (k, j))
    o_spec = pl.BlockSpec((tm, tn), lambda i, j, k: (i, j))
    ins, specs = [a, b], [a_spec, b_spec]
    if has_res:
        ins.append(res)
        specs.append(o_spec)
    if has_up:
        ins.append(epi_up)
        specs.append(o_spec)
    if after is not None:
        ins.append(after)
        specs.append(pl.BlockSpec(memory_space=pl.ANY))
    out_shape, out_spec, aliases = jax.ShapeDtypeStruct((M, N), out_dtype), o_spec, {}
    if into is not None:
        shape, block, index, buf = into
        out_shape, out_spec = jax.ShapeDtypeStruct(shape, out_dtype), pl.BlockSpec(block, index)
        if buf is not None:
            aliases = {len(ins): 0}
            ins.append(buf)
            specs.append(pl.BlockSpec(memory_space=pl.ANY))
    return pl.pallas_call(
        body, name=name, grid=(M // tm, N // tn, nk),
        in_specs=specs, out_specs=out_spec, out_shape=out_shape, input_output_aliases=aliases,
        scratch_shapes=[] if nk == 1 else [pltpu.VMEM((tm, tn), F32)],
        compiler_params=_cparams(("parallel", "parallel", "arbitrary")),
    )(*ins)


def _rmsnorm_fwd(x, w, *, name):
    n, d = x.shape
    tm = _pick(n, (512, 256, 128))

    def body(x_ref, w_ref, y_ref, r_ref):
        xv = x_ref[...]
        rstd = lax.rsqrt(jnp.mean(xv * xv, axis=1, keepdims=True) + EPS)
        y_ref[...] = (xv * rstd * w_ref[...]).astype(BF16)
        r_ref[...] = rstd

    return pl.pallas_call(
        body, name=name, grid=(n // tm,),
        in_specs=[pl.BlockSpec((tm, d), lambda i: (i, 0)), pl.BlockSpec((1, d), lambda i: (0, 0))],
        out_specs=[pl.BlockSpec((tm, d), lambda i: (i, 0)), pl.BlockSpec((tm, 1), lambda i: (i, 0))],
        out_shape=[jax.ShapeDtypeStruct((n, d), BF16), jax.ShapeDtypeStruct((n, 1), F32)],
        compiler_params=_cparams(("parallel",)),
    )(x, w)


def _rmsnorm_bwd(dyn, x, rstd, w, dres, *, name):
    n, d = x.shape
    tm = _pick(n, (512, 256, 128))

    def body(g_ref, x_ref, r_ref, w_ref, d_ref, dx_ref, dxb_ref, dw_ref):
        @pl.when(pl.program_id(0) == 0)
        def _():
            dw_ref[...] = jnp.zeros_like(dw_ref)

        g = g_ref[...]
        r = r_ref[...]
        xhat = x_ref[...] * r
        gw = g * w_ref[...]
        dx = d_ref[...] + r * (gw - xhat * jnp.mean(gw * xhat, axis=1, keepdims=True))
        dx_ref[...] = dx
        dxb_ref[...] = dx.astype(BF16)
        dw_ref[...] += jnp.sum(g * xhat, axis=0, keepdims=True)

    row = pl.BlockSpec((tm, d), lambda i: (i, 0))
    vec = pl.BlockSpec((1, d), lambda i: (0, 0))
    return pl.pallas_call(
        body, name=name, grid=(n // tm,),
        in_specs=[row, row, pl.BlockSpec((tm, 1), lambda i: (i, 0)), vec, row],
        out_specs=[row, row, vec],
        out_shape=[jax.ShapeDtypeStruct((n, d), F32), jax.ShapeDtypeStruct((n, d), BF16),
                   jax.ShapeDtypeStruct((1, d), F32)],
        compiler_params=_cparams(("arbitrary",)),
    )(dyn, x, rstd, w, dres)


def _final(h2, w, target):
    n, d = h2.shape
    tm = _pick(n, (512, 256, 128))

    def body(h_ref, w_ref, t_ref, dh_ref, dhb_ref, loss_ref, dw_ref):
        @pl.when(pl.program_id(0) == 0)
        def _():
            loss_ref[...] = jnp.zeros_like(loss_ref)
            dw_ref[...] = jnp.zeros_like(dw_ref)

        hv = h_ref[...]
        wv = w_ref[...]
        rstd = lax.rsqrt(jnp.mean(hv * hv, axis=1, keepdims=True) + EPS)
        xhat = hv * rstd
        err = xhat * wv - t_ref[...]
        part = jnp.sum(jnp.mean(err * err, axis=1, keepdims=True), axis=0, keepdims=True)
        loss_ref[...] += 0.5 * part
        dy = err * (1.0 / d)
        gw = dy * wv
        dh = rstd * (gw - xhat * jnp.mean(gw * xhat, axis=1, keepdims=True))
        dh_ref[...] = dh
        dhb_ref[...] = dh.astype(BF16)
        dw_ref[...] += jnp.sum(dy * xhat, axis=0, keepdims=True)

    row = pl.BlockSpec((tm, d), lambda i: (i, 0))
    vec = pl.BlockSpec((1, d), lambda i: (0, 0))
    return pl.pallas_call(
        body, name="final_norm_loss", grid=(n // tm,),
        in_specs=[row, vec, row],
        out_specs=[row, row, pl.BlockSpec((1, 128), lambda i: (0, 0)), vec],
        out_shape=[jax.ShapeDtypeStruct((n, d), F32), jax.ShapeDtypeStruct((n, d), BF16),
                   jax.ShapeDtypeStruct((1, 128), F32), jax.ShapeDtypeStruct((1, d), F32)],
        compiler_params=_cparams(("arbitrary",)),
    )(h2, w, target)


def _rows_product(a_ref, b_ref, tb, a_act):
    av = a_ref[...]
    if a_act == "relu2":
        r = jnp.maximum(av.astype(F32), 0.0)
        av = r * r
    dn = (((1,), (1 if tb else 0,)), ((), ()))
    return lax.dot_general(av.astype(BF16), b_ref[...].astype(BF16), dn, preferred_element_type=F32)


def _mm_norm_fwd(a, b, res, w, *, name, tm):
    m, k = a.shape
    d = b.shape[1]

    def body(a_ref, b_ref, res_ref, w_ref, h_ref, y_ref, r_ref):
        hv = _rows_product(a_ref, b_ref, False, None) + res_ref[...]
        rstd = lax.rsqrt(jnp.mean(hv * hv, axis=1, keepdims=True) + EPS)
        h_ref[...] = hv
        y_ref[...] = (hv * rstd * w_ref[...]).astype(BF16)
        r_ref[...] = rstd

    row = pl.BlockSpec((tm, d), lambda i: (i, 0))
    return pl.pallas_call(
        body, name=name, grid=(m // tm,),
        in_specs=[pl.BlockSpec((tm, k), lambda i: (i, 0)), pl.BlockSpec((k, d), lambda i: (0, 0)), row,
                  pl.BlockSpec((1, d), lambda i: (0, 0))],
        out_specs=[row, row, pl.BlockSpec((tm, 1), lambda i: (i, 0))],
        out_shape=[jax.ShapeDtypeStruct((m, d), F32), jax.ShapeDtypeStruct((m, d), BF16),
                   jax.ShapeDtypeStruct((m, 1), F32)],
        compiler_params=_cparams(("parallel",)),
    )(a, b, res, w)


def _mm_final(a, b, res, w, target, *, name, tm, a_act):
    m, k = a.shape
    d = b.shape[1]

    def body(a_ref, b_ref, res_ref, w_ref, t_ref, dh_ref, dhb_ref, loss_ref, dw_ref):
        @pl.when(pl.program_id(0) == 0)
        def _():
            loss_ref[...] = jnp.zeros_like(loss_ref)
            dw_ref[...] = jnp.zeros_like(dw_ref)

        hv = _rows_product(a_ref, b_ref, False, a_act) + res_ref[...]
        wv = w_ref[...]
        rstd = lax.rsqrt(jnp.mean(hv * hv, axis=1, keepdims=True) + EPS)
        xhat = hv * rstd
        err = xhat * wv - t_ref[...]
        loss_ref[...] += 0.5 * jnp.sum(jnp.mean(err * err, axis=1, keepdims=True), axis=0, keepdims=True)
        dy = err * (1.0 / d)
        gw = dy * wv
        dh = rstd * (gw - xhat * jnp.mean(gw * xhat, axis=1, keepdims=True))
        dh_ref[...] = dh
        dhb_ref[...] = dh.astype(BF16)
        dw_ref[...] += jnp.sum(dy * xhat, axis=0, keepdims=True)

    row = pl.BlockSpec((tm, d), lambda i: (i, 0))
    vec = pl.BlockSpec((1, d), lambda i: (0, 0))
    return pl.pallas_call(
        body, name=name, grid=(m // tm,),
        in_specs=[pl.BlockSpec((tm, k), lambda i: (i, 0)), pl.BlockSpec((k, d), lambda i: (0, 0)), row, vec, row],
        out_specs=[row, row, pl.BlockSpec((1, 128), lambda i: (0, 0)), vec],
        out_shape=[jax.ShapeDtypeStruct((m, d), F32), jax.ShapeDtypeStruct((m, d), BF16),
                   jax.ShapeDtypeStruct((1, 128), F32), jax.ShapeDtypeStruct((1, d), F32)],
        compiler_params=_cparams(("arbitrary",)),
    )(a, b, res, w, target)


def _mm_norm_bwd(a, b, res, x, rstd, w, dres, *, name, tm):
    m, k = a.shape
    d = b.shape[0]
    has_res = res is not None

    def body(*refs):
        a_ref, b_ref = refs[0], refs[1]
        i = 2
        res_ref = None
        if has_res:
            res_ref = refs[i]
            i += 1
        x_ref, r_ref, w_ref, d_ref, dx_ref, dxb_ref, dw_ref = refs[i:i + 7]

        @pl.when(pl.program_id(0) == 0)
        def _():
            dw_ref[...] = jnp.zeros_like(dw_ref)

        g = _rows_product(a_ref, b_ref, True, None)
        if has_res:
            g = g + res_ref[...]
        r = r_ref[...]
        xhat = x_ref[...] * r
        gw = g * w_ref[...]
        dx = d_ref[...] + r * (gw - xhat * jnp.mean(gw * xhat, axis=1, keepdims=True))
        dx_ref[...] = dx
        dxb_ref[...] = dx.astype(BF16)
        dw_ref[...] += jnp.sum(g * xhat, axis=0, keepdims=True)

    row = pl.BlockSpec((tm, d), lambda i: (i, 0))
    vec = pl.BlockSpec((1, d), lambda i: (0, 0))
    ins = [a, b] + ([res] if has_res else []) + [x, rstd, w, dres]
    specs = ([pl.BlockSpec((tm, k), lambda i: (i, 0)), pl.BlockSpec((d, k), lambda i: (0, 0))]
             + ([row] if has_res else []) + [row, pl.BlockSpec((tm, 1), lambda i: (i, 0)), vec, row])
    return pl.pallas_call(
        body, name=name, grid=(m // tm,), in_specs=specs, out_specs=[row, row, vec],
        out_shape=[jax.ShapeDtypeStruct((m, d), F32), jax.ShapeDtypeStruct((m, d), BF16),
                   jax.ShapeDtypeStruct((1, d), F32)],
        compiler_params=_cparams(("arbitrary",)),
    )(*ins)


def _softplus(x):
    return jnp.maximum(x, 0.0) + jnp.log(1.0 + jnp.exp(-jnp.abs(x)))


def _prep(proj_a, bias128, alog128, bl, t):
    n = bl * t
    nch = t // CHUNK
    col0 = (SSD_WIDTH + CONV_CH) // 128

    def body(p_ref, b_ref, al_ref, dt_ref, gd_ref, ac_ref, c_ref, carry):
        @pl.when(pl.program_id(1) == 0)
        def _():
            carry[...] = jnp.zeros_like(carry)

        xv = p_ref[...] + b_ref[...]
        sp = _softplus(xv)
        a = -jnp.exp(al_ref[...]) * sp
        logf = -_softplus(-xv)
        row = lax.broadcasted_iota(jnp.int32, (CHUNK, CHUNK), 0)
        col = lax.broadcasted_iota(jnp.int32, (CHUNK, CHUNK), 1)
        tril = (row >= col).astype(F32)
        acum = jnp.dot(tril, a, precision=HIGHEST, preferred_element_type=F32)
        c = jnp.dot(tril, logf, precision=HIGHEST, preferred_element_type=F32) + carry[...]
        carry[...] = c[CHUNK - 1:CHUNK, :]
        lane = lax.broadcasted_iota(jnp.int32, (1, 128), 1)
        head_lanes = lane < 16
        dt_ref[...] = jnp.where(head_lanes, sp, 0.0)
        gd_ref[...] = jnp.where(head_lanes, jax.nn.sigmoid(xv), jnp.where(lane < 32, jax.nn.sigmoid(-xv), 0.0))
        ac_ref[...] = jnp.where(head_lanes, acum, 0.0)
        c_ref[...] = c[:, 16:32]

    o16 = pl.BlockSpec((CHUNK, 16), lambda b, c: (b * nch + c, 0))
    o128 = pl.BlockSpec((CHUNK, 128), lambda b, c: (b * nch + c, 0))
    v128 = pl.BlockSpec((1, 128), lambda b, c: (0, 0))
    w128 = jax.ShapeDtypeStruct((n, 128), F32)
    return pl.pallas_call(
        body, name="head_scalars", grid=(bl, nch),
        in_specs=[pl.BlockSpec((CHUNK, 128), lambda b, c: (b * nch + c, col0)), v128, v128],
        out_specs=[o128, o128, o128, o16],
        out_shape=[w128, w128, w128, jax.ShapeDtypeStruct((n, 16), F32)],
        scratch_shapes=[pltpu.VMEM((1, 128), F32)],
        compiler_params=_cparams(("parallel", "arbitrary")),
    )(proj_a, bias128, alog128)


def _fpost(dc, gate_d, ddt, dpa, bl, t):
    n = bl * t
    nch = t // CHUNK
    col0 = (SSD_WIDTH + CONV_CH) // 128

    def body(dc_ref, gd_ref, ddt_ref, dpa_in, out_ref, db_ref, carry):
        @pl.when(pl.program_id(1) == 0)
        def _():
            carry[...] = jnp.zeros_like(carry)

        @pl.when((pl.program_id(0) == 0) & (pl.program_id(1) == 0))
        def _():
            db_ref[...] = jnp.zeros_like(db_ref)

        row = lax.broadcasted_iota(jnp.int32, (CHUNK, CHUNK), 0)
        col = lax.broadcasted_iota(jnp.int32, (CHUNK, CHUNK), 1)
        triu = (row <= col).astype(F32)
        dlf = jnp.dot(triu, dc_ref[...], precision=HIGHEST, preferred_element_type=F32) + carry[...]
        carry[...] = dlf[0:1, :]
        lane = lax.broadcasted_iota(jnp.int32, (1, 128), 1)
        df = jnp.where((lane >= 16) & (lane < 32), dlf * gd_ref[...], 0.0)
        out_ref[...] = (ddt_ref[...] + df).astype(BF16)
        db_ref[...] += jnp.sum(df, axis=0, keepdims=True)[:, 16:32]

    rev = lambda b, c: (b * nch + nch - 1 - c, 0)
    blk = pl.BlockSpec((CHUNK, 128), rev)
    return pl.pallas_call(
        body, name="forget_gate_bwd", grid=(bl, nch),
        in_specs=[blk, blk, blk, ANY],
        out_specs=[pl.BlockSpec((CHUNK, 128), lambda b, c: (b * nch + nch - 1 - c, col0)),
                   pl.BlockSpec((1, 16), lambda b, c: (0, 0))],
        out_shape=[jax.ShapeDtypeStruct(dpa.shape, dpa.dtype), jax.ShapeDtypeStruct((1, 16), F32)],
        input_output_aliases={3: 0},
        scratch_shapes=[pltpu.VMEM((1, 128), F32)],
        compiler_params=_cparams(("arbitrary", "arbitrary")),
    )(dc, gate_d, ddt, dpa)


CONV_TILE = 256
CONV_ROWS = 256


def _conv_taps(u_ref, i, w, bias):
    r0 = pl.multiple_of(i * CONV_ROWS, CONV_ROWS)
    cur = u_ref[pl.ds(r0, CONV_ROWS), :]
    p0 = pl.multiple_of(jnp.maximum(r0 - 8, 0), 8)
    prev = jnp.where(i > 0, u_ref[pl.ds(p0, 8), :], 0.0)
    cat = jnp.concatenate([prev, cur], axis=0)
    pre = bias + w[3:4, :] * cur
    taps = [cur]
    for s in (1, 2, 3):
        sh = pltpu.roll(cat, s, 0)[8:, :]
        taps.append(sh)
        pre = pre + w[3 - s:4 - s, :] * sh
    return r0, pre, taps


def _conv_fwd(proj_a, conv_w, conv_b, bl, t):
    n = bl * t
    nct = CONV_CH // CONV_TILE
    c0 = SSD_WIDTH // CONV_TILE

    def body(u_ref, w_ref, b_ref, o_ref):
        w = w_ref[...]
        bias = b_ref[...]

        def chunk(i, carry):
            r0, pre, _ = _conv_taps(u_ref, i, w, bias)
            o_ref[pl.ds(r0, CONV_ROWS), :] = pre * jax.nn.sigmoid(pre)
            return carry

        lax.fori_loop(0, t // CONV_ROWS, chunk, 0)

    return pl.pallas_call(
        body, name="conv_silu_fwd", grid=(bl, nct),
        in_specs=[pl.BlockSpec((t, CONV_TILE), lambda b, c: (b, c0 + c)),
                  pl.BlockSpec((4, CONV_TILE), lambda b, c: (0, c)),
                  pl.BlockSpec((1, CONV_TILE), lambda b, c: (0, c))],
        out_specs=pl.BlockSpec((t, CONV_TILE), lambda b, c: (b, c)),
        out_shape=jax.ShapeDtypeStruct((n, CONV_CH), F32),
        compiler_params=_cparams(("parallel", "parallel")),
    )(proj_a, conv_w, conv_b)


def _conv_bwd(dxc, proj_a, conv_w, conv_b, dpa, bl, t):
    nct = CONV_CH // CONV_TILE
    c0 = SSD_WIDTH // CONV_TILE
    nrc = t // CONV_ROWS

    def body(g_ref, u_ref, w_ref, b_ref, dpa_in, du_ref, dw_ref, db_ref, dp_scr):
        @pl.when(pl.program_id(1) == 0)
        def _():
            dw_ref[...] = jnp.zeros_like(dw_ref)
            db_ref[...] = jnp.zeros_like(db_ref)

        w = w_ref[...]
        bias = b_ref[...]
        dp_scr[pl.ds(t, 8), :] = jnp.zeros((8, CONV_TILE), F32)

        def chunk1(i, carry):
            dw0, dw1, dw2, dw3, db = carry
            r0, pre, taps = _conv_taps(u_ref, i, w, bias)
            sg = jax.nn.sigmoid(pre)
            dpre = g_ref[pl.ds(r0, CONV_ROWS), :] * (sg * (1.0 + pre * (1.0 - sg)))
            dp_scr[pl.ds(r0, CONV_ROWS), :] = dpre
            dw3 = dw3 + jnp.sum(dpre * taps[0], axis=0, keepdims=True)
            dw2 = dw2 + jnp.sum(dpre * taps[1], axis=0, keepdims=True)
            dw1 = dw1 + jnp.sum(dpre * taps[2], axis=0, keepdims=True)
            dw0 = dw0 + jnp.sum(dpre * taps[3], axis=0, keepdims=True)
            db = db + jnp.sum(dpre, axis=0, keepdims=True)
            return dw0, dw1, dw2, dw3, db

        z = jnp.zeros((1, CONV_TILE), F32)
        dw0, dw1, dw2, dw3, db = lax.fori_loop(0, nrc, chunk1, (z, z, z, z, z))
        dw_ref[...] += jnp.concatenate([dw0, dw1, dw2, dw3], axis=0)
        db_ref[...] += db

        def chunk2(i, carry):
            r0 = pl.multiple_of(i * CONV_ROWS, CONV_ROWS)
            cat = dp_scr[pl.ds(r0, CONV_ROWS + 8), :]
            du = w[3:4, :] * cat[:CONV_ROWS, :]
            for s in (1, 2, 3):
                du = du + w[3 - s:4 - s, :] * pltpu.roll(cat, CONV_ROWS + 8 - s, 0)[:CONV_ROWS, :]
            du_ref[pl.ds(r0, CONV_ROWS), :] = du.astype(BF16)
            return carry

        lax.fori_loop(0, nrc, chunk2, 0)

    return pl.pallas_call(
        body, name="conv_silu_bwd", grid=(nct, bl),
        in_specs=[pl.BlockSpec((t, CONV_TILE), lambda c, b: (b, c)),
                  pl.BlockSpec((t, CONV_TILE), lambda c, b: (b, c0 + c)),
                  pl.BlockSpec((4, CONV_TILE), lambda c, b: (0, c)),
                  pl.BlockSpec((1, CONV_TILE), lambda c, b: (0, c)), ANY],
        out_specs=[pl.BlockSpec((t, CONV_TILE), lambda c, b: (b, c0 + c)),
                   pl.BlockSpec((4, CONV_TILE), lambda c, b: (0, c)),
                   pl.BlockSpec((1, CONV_TILE), lambda c, b: (0, c))],
        out_shape=[jax.ShapeDtypeStruct(dpa.shape, dpa.dtype), jax.ShapeDtypeStruct((4, CONV_CH), F32),
                   jax.ShapeDtypeStruct((1, CONV_CH), F32)],
        input_output_aliases={4: 0},
        scratch_shapes=[pltpu.VMEM((t + 8, CONV_TILE), F32)],
        compiler_params=_cparams(("parallel", "arbitrary")),
    )(dxc, proj_a, conv_w, conv_b, dpa)


NT_DIMS = (((1,), (1,)), ((), ()))
TN_DIMS = (((0,), (0,)), ((), ()))


def _dot(a, b, dims=None):
    if dims is None:
        return jnp.dot(a, b, preferred_element_type=F32)
    return lax.dot_general(a, b, dims, preferred_element_type=F32)


def _ssd_fwd(xc, proj_a, dt, acum, acum_t, dskip_e, norm_w, bl, t):
    n = bl * t
    nch = t // CHUNK
    L = CHUNK

    def body(xc_ref, z_ref, dt_ref, ac_ref, act_ref, dsk_ref, nw_ref, ys_ref, yp_ref, hp_ref, h_scr, y_scr):
        @pl.when(pl.program_id(1) == 0)
        def _():
            h_scr[...] = jnp.zeros_like(h_scr)

        row = lax.broadcasted_iota(jnp.int32, (L, L), 0)
        col = lax.broadcasted_iota(jnp.int32, (L, L), 1)
        causal = row >= col
        dt_all = dt_ref[...]
        ac_all = ac_ref[...]
        act_all = act_ref[...]
        for g in range(2):
            bg = xc_ref[:, SSD_WIDTH + g * 128:SSD_WIDTH + (g + 1) * 128].astype(BF16)
            cg = xc_ref[:, SSD_WIDTH + 256 + g * 128:SSD_WIDTH + 256 + (g + 1) * 128].astype(BF16)
            gmat = _dot(cg, bg, NT_DIMS)
            for r in range(8):
                h = g * 8 + r
                sl = slice(h * HEAD_DIM, (h + 1) * HEAD_DIM)
                xs = xc_ref[:, sl]
                xdt = xs * dt_all[:, h:h + 1]
                ac = ac_all[:, h:h + 1]
                ar = act_all[h:h + 1, :]
                ldec = jnp.exp(jnp.where(causal, ac - ar, NEG))
                m = (gmat * ldec).astype(BF16)
                hp = h_scr[h]
                hp_ref[h] = hp
                yd = _dot(m, xdt.astype(BF16))
                yo = _dot(cg, hp.astype(BF16), NT_DIMS) * jnp.exp(ac)
                y_scr[:, sl] = yd + yo + dsk_ref[:, sl] * xs
                alast = ac_all[L - 1:L, h:h + 1]
                xd = (xdt * jnp.exp(alast - ac)).astype(BF16)
                h_scr[h] = jnp.exp(alast) * hp + _dot(xd, bg, TN_DIMS)
        y = y_scr[...]
        yp_ref[...] = y
        zv = z_ref[...]
        yg = y * (zv * jax.nn.sigmoid(zv))
        for g in range(2):
            gs = slice(g * 512, (g + 1) * 512)
            grp = yg[:, gs]
            rstd = lax.rsqrt(jnp.mean(grp * grp, axis=1, keepdims=True) + EPS)
            ys_ref[:, gs] = (grp * rstd * nw_ref[:, gs]).astype(BF16)

    rb = lambda b, c: (b * nch + c, 0)
    v1k = pl.BlockSpec((1, SSD_WIDTH), lambda b, c: (0, 0))
    return pl.pallas_call(
        body, name="ssd_fwd", grid=(bl, nch),
        in_specs=[pl.BlockSpec((L, CONV_CH), rb), pl.BlockSpec((L, SSD_WIDTH), rb),
                  pl.BlockSpec((L, 16), rb), pl.BlockSpec((L, 16), rb),
                  pl.BlockSpec((16, L), lambda b, c: (0, b * nch + c)), v1k, v1k],
        out_specs=[pl.BlockSpec((L, SSD_WIDTH), rb), pl.BlockSpec((L, SSD_WIDTH), rb),
                   pl.BlockSpec((None, 16, HEAD_DIM, SSD_STATE), lambda b, c: (b * nch + c, 0, 0, 0))],
        out_shape=[jax.ShapeDtypeStruct((n, SSD_WIDTH), BF16), jax.ShapeDtypeStruct((n, SSD_WIDTH), F32),
                   jax.ShapeDtypeStruct((bl * nch, 16, HEAD_DIM, SSD_STATE), F32)],
        scratch_shapes=[pltpu.VMEM((16, HEAD_DIM, SSD_STATE), F32), pltpu.VMEM((L, SSD_WIDTH), F32)],
        compiler_params=_cparams(("parallel", "arbitrary")),
    )(xc, proj_a, dt, acum, acum_t, dskip_e, norm_w)


def _ssd_bwd(dys, xc, proj_a, ypre, hprev, dt, sig, acum, acum_t, a_log, dskip_e, norm_w, bl, t):
    n = bl * t
    nch = t // CHUNK
    L = CHUNK

    def body(dys_ref, xc_ref, z_ref, yp_ref, hp_ref, dt_ref, sg_ref, ac_ref, act_ref, al_ref, dsk_ref, nw_ref,
             dxc_ref, dz_ref, ddt_ref, dnw_ref, dsk16_ref, da16_ref, db16_ref, dh_scr, dy_scr):
        first = (pl.program_id(0) == 0) & (pl.program_id(1) == 0)

        @pl.when(first)
        def _():
            dnw_ref[...] = jnp.zeros_like(dnw_ref)
            dsk16_ref[...] = jnp.zeros_like(dsk16_ref)
            da16_ref[...] = jnp.zeros_like(da16_ref)
            db16_ref[...] = jnp.zeros_like(db16_ref)

        @pl.when(pl.program_id(1) == 0)
        def _():
            dh_scr[...] = jnp.zeros_like(dh_scr)

        y = yp_ref[...]
        zv = z_ref[...]
        sz = jax.nn.sigmoid(zv)
        gate = zv * sz
        yg = y * gate
        dout = dys_ref[...]
        nw = nw_ref[...]
        for g in range(2):
            gs = slice(g * 512, (g + 1) * 512)
            grp = yg[:, gs]
            rstd = lax.rsqrt(jnp.mean(grp * grp, axis=1, keepdims=True) + EPS)
            ghat = grp * rstd
            dnw_ref[:, gs] += jnp.sum(dout[:, gs] * ghat, axis=0, keepdims=True)
            gw = dout[:, gs] * nw[:, gs]
            dyg = rstd * (gw - ghat * jnp.mean(gw * ghat, axis=1, keepdims=True))
            dy_scr[:, gs] = dyg * gate[:, gs]
            dz_ref[:, gs] = (dyg * y[:, gs] * (sz[:, gs] * (1.0 + zv[:, gs] * (1.0 - sz[:, gs])))).astype(BF16)

        row = lax.broadcasted_iota(jnp.int32, (L, L), 0)
        col = lax.broadcasted_iota(jnp.int32, (L, L), 1)
        causal = row >= col
        lane16 = lax.broadcasted_iota(jnp.int32, (1, 16), 1)
        lane128 = lax.broadcasted_iota(jnp.int32, (1, L), 1)
        last_row = lax.broadcasted_iota(jnp.int32, (L, 1), 0) == (L - 1)
        dt_all = dt_ref[...]
        ac_all = ac_ref[...]
        act_all = act_ref[...]
        dac_col = jnp.zeros((L, L), F32)
        dac_row = jnp.zeros((L, L), F32)
        ddt_x = jnp.zeros((L, L), F32)
        dsk16 = jnp.zeros((1, 16), F32)
        rows16 = lax.broadcasted_iota(jnp.int32, (L, 1), 0)
        for g in range(2):
            bsl = slice(SSD_WIDTH + g * 128, SSD_WIDTH + (g + 1) * 128)
            csl = slice(SSD_WIDTH + 256 + g * 128, SSD_WIDTH + 256 + (g + 1) * 128)
            bg = xc_ref[:, bsl].astype(BF16)
            cg = xc_ref[:, csl].astype(BF16)
            gmat = _dot(cg, bg, NT_DIMS)
            dg_sum = jnp.zeros((L, L), F32)
            dc_acc = jnp.zeros((L, SSD_STATE), F32)
            db_acc = jnp.zeros((L, SSD_STATE), F32)
            for r in range(8):
                h = g * 8 + r
                sl = slice(h * HEAD_DIM, (h + 1) * HEAD_DIM)
                onehot = lane16 == h
                onehot_w = lane128 == h
                xs = xc_ref[:, sl]
                dth = dt_all[:, h:h + 1]
                xdt = xs * dth
                xb = xdt.astype(BF16)
                ac = ac_all[:, h:h + 1]
                ar = act_all[h:h + 1, :]
                alast = ac_all[L - 1:L, h:h + 1]
                ldec = jnp.exp(jnp.where(causal, ac - ar, NEG))
                mf = gmat * ldec
                e_in = jnp.exp(ac)
                dec = jnp.exp(alast - ac)
                elast = jnp.exp(alast)
                hp = hp_ref[h]
                hpb = hp.astype(BF16)
                dyh = dy_scr[:, sl]
                dyb = dyh.astype(BF16)
                dsk16 = dsk16 + jnp.where(onehot, jnp.sum(jnp.sum(dyh * xs, axis=1, keepdims=True), axis=0, keepdims=True), 0.0)
                dm = _dot(dyb, xb, NT_DIMS)
                dx = _dot(mf.astype(BF16), dyb, TN_DIMS)
                dg_sum = dg_sum + dm * ldec
                wmat = dm * mf
                dac_h = jnp.sum(wmat, axis=1, keepdims=True)
                dac_row = dac_row + jnp.where(rows16 == h, -jnp.sum(wmat, axis=0, keepdims=True), 0.0)
                ch = _dot(cg, hpb, NT_DIMS)
                dye = dyh * e_in
                dyeb = dye.astype(BF16)
                dc_acc = dc_acc + _dot(dyeb, hpb)
                dhp = _dot(dyeb, cg, TN_DIMS)
                dac_h = dac_h + jnp.sum(dye * ch, axis=1, keepdims=True)
                ds = dh_scr[h]
                dsb = ds.astype(BF16)
                dxd = _dot(bg, dsb, NT_DIMS)
                db_acc = db_acc + _dot((xdt * dec).astype(BF16), dsb)
                dx = dx + dxd * dec
                ddec = jnp.sum(dxd * xdt, axis=1, keepdims=True) * dec
                extra = (jnp.sum(ddec, axis=0, keepdims=True)
                         + elast * jnp.sum(jnp.sum(hp * ds, axis=1, keepdims=True), axis=0, keepdims=True))
                dac_h = dac_h - ddec + jnp.where(last_row, extra, 0.0)
                dh_scr[h] = elast * ds + dhp
                dac_col = dac_col + jnp.where(onehot_w, dac_h, 0.0)
                ddt_x = ddt_x + jnp.where(onehot_w, jnp.sum(dx * xs, axis=1, keepdims=True), 0.0)
                dxc_ref[:, sl] = dx * dth + dsk_ref[:, sl] * dyh
            dgb = dg_sum.astype(BF16)
            dxc_ref[:, csl] = dc_acc + _dot(dgb, bg)
            dxc_ref[:, bsl] = db_acc + _dot(dgb, cg, TN_DIMS)
        dac = dac_col + jnp.transpose(dac_row)
        triu = (row <= col).astype(F32)
        da = jnp.dot(triu, dac, precision=HIGHEST, preferred_element_type=F32)[:, 0:16]
        a_row = -jnp.exp(al_ref[...])
        ddt = (ddt_x[:, 0:16] + da * a_row) * sg_ref[...]
        ddt_ref[...] = ddt
        dsk16_ref[...] += dsk16
        da16_ref[...] += jnp.sum(da * dt_all, axis=0, keepdims=True) * a_row
        db16_ref[...] += jnp.sum(ddt, axis=0, keepdims=True)

    rb = lambda b, c: (b * nch + nch - 1 - c, 0)
    v1k = pl.BlockSpec((1, SSD_WIDTH), lambda b, c: (0, 0))
    v16 = pl.BlockSpec((1, 16), lambda b, c: (0, 0))
    wide = pl.BlockSpec((L, SSD_WIDTH), rb)
    s16 = pl.BlockSpec((L, 16), rb)
    return pl.pallas_call(
        body, name="ssd_bwd", grid=(bl, nch),
        in_specs=[wide, pl.BlockSpec((L, CONV_CH), rb), wide, wide,
                  pl.BlockSpec((None, 16, HEAD_DIM, SSD_STATE), lambda b, c: (b * nch + nch - 1 - c, 0, 0, 0)),
                  s16, s16, s16, pl.BlockSpec((16, L), lambda b, c: (0, b * nch + nch - 1 - c)), v16, v1k, v1k],
        out_specs=[pl.BlockSpec((L, CONV_CH), rb), wide, s16, v1k, v16, v16, v16],
        out_shape=[jax.ShapeDtypeStruct((n, CONV_CH), F32), jax.ShapeDtypeStruct((n, SSD_WIDTH), BF16),
                   jax.ShapeDtypeStruct((n, 16), F32), jax.ShapeDtypeStruct((1, SSD_WIDTH), F32),
                   jax.ShapeDtypeStruct((1, 16), F32), jax.ShapeDtypeStruct((1, 16), F32),
                   jax.ShapeDtypeStruct((1, 16), F32)],
        scratch_shapes=[pltpu.VMEM((16, HEAD_DIM, SSD_STATE), F32), pltpu.VMEM((L, SSD_WIDTH), F32)],
        compiler_params=_cparams(("arbitrary", "arbitrary")),
    )(dys, xc, proj_a, ypre, hprev, dt, sig, acum, acum_t, a_log, dskip_e, norm_w)


def _head_expander():
    r = lax.broadcasted_iota(jnp.int32, (128, SSD_WIDTH), 0)
    c = lax.broadcasted_iota(jnp.int32, (128, SSD_WIDTH), 1)
    return ((c // HEAD_DIM == r % 16) & (r < 48)).astype(BF16)


def _spread(v128, expander):
    hi = v128.astype(BF16).astype(F32)
    r1 = v128 - hi
    mid = r1.astype(BF16).astype(F32)
    lo = (r1 - mid).astype(BF16).astype(F32)
    packed = (hi + pltpu.roll(mid, 16, 1) + pltpu.roll(lo, 32, 1)).astype(BF16)
    return jnp.dot(packed, expander, preferred_element_type=F32)


def _head_sums(v1024, expander):
    hi = v1024.astype(BF16)
    lo = (v1024 - hi.astype(F32)).astype(BF16)
    heads = jnp.where(lax.broadcasted_iota(jnp.int32, (128, SSD_WIDTH), 0) < 16, expander, jnp.zeros_like(expander))
    return _dot(hi, heads, NT_DIMS) + _dot(lo, heads, NT_DIMS)


def _ssd_fwd2(xc, proj_a, dt, acum, acum_t, dskip_e, norm_w, bl, t):
    n = bl * t
    nch = t // CHUNK
    L = CHUNK

    def body(xc_ref, z_ref, dt_ref, ac_ref, act_ref, dsk_ref, nw_ref, ys_ref, yp_ref, hp_ref, h_scr, y_scr, x_scr):
        @pl.when(pl.program_id(1) == 0)
        def _():
            h_scr[...] = jnp.zeros_like(h_scr)

        row = lax.broadcasted_iota(jnp.int32, (L, L), 0)
        col = lax.broadcasted_iota(jnp.int32, (L, L), 1)
        causal = row >= col
        expander = _head_expander()
        ac_all = ac_ref[...]
        act_all = act_ref[...]
        ac_e = _spread(ac_all, expander)
        e_in = jnp.exp(ac_e)
        dec = jnp.exp(ac_e[L - 1:L, :] - ac_e)
        xs_all = xc_ref[:, 0:SSD_WIDTH]
        x_all = xs_all * _spread(dt_ref[...], expander)
        x_scr[...] = x_all.astype(BF16)
        hp_all = h_scr[...]
        hp_ref[...] = hp_all
        for g in range(2):
            gs = slice(g * 512, (g + 1) * 512)
            bg = xc_ref[:, SSD_WIDTH + g * 128:SSD_WIDTH + (g + 1) * 128].astype(BF16)
            cg = xc_ref[:, SSD_WIDTH + 256 + g * 128:SSD_WIDTH + 256 + (g + 1) * 128].astype(BF16)
            gmat = _dot(cg, bg, NT_DIMS)
            y_scr[:, gs] = (_dot(cg, hp_all[gs, :].astype(BF16), NT_DIMS) * e_in[:, gs]
                            + dsk_ref[:, gs] * xs_all[:, gs])
            s_new = _dot((x_all[:, gs] * dec[:, gs]).astype(BF16), bg, TN_DIMS)
            for r in range(8):
                h = g * 8 + r
                sl = slice(h * HEAD_DIM, (h + 1) * HEAD_DIM)
                ldec = jnp.exp(jnp.where(causal, ac_all[:, h:h + 1] - act_all[h:h + 1, :], NEG))
                y_scr[:, sl] += _dot((gmat * ldec).astype(BF16), x_scr[:, sl])
                elast = jnp.exp(ac_all[L - 1:L, h:h + 1])
                h_scr[sl, :] = elast * hp_all[sl, :] + s_new[r * HEAD_DIM:(r + 1) * HEAD_DIM, :]
        y = y_scr[...]
        yp_ref[...] = y
        zv = z_ref[...]
        yg = y * (zv * jax.nn.sigmoid(zv))
        for g in range(2):
            gs = slice(g * 512, (g + 1) * 512)
            grp = yg[:, gs]
            rstd = lax.rsqrt(jnp.mean(grp * grp, axis=1, keepdims=True) + EPS)
            ys_ref[:, gs] = (grp * rstd * nw_ref[:, gs]).astype(BF16)

    rb = lambda b, c: (b * nch + c, 0)
    v1k = pl.BlockSpec((1, SSD_WIDTH), lambda b, c: (0, 0))
    return pl.pallas_call(
        body, name="ssd_fwd", grid=(bl, nch),
        in_specs=[pl.BlockSpec((L, CONV_CH), rb), pl.BlockSpec((L, SSD_WIDTH), rb),
                  pl.BlockSpec((L, 128), rb), pl.BlockSpec((L, 128), rb),
                  pl.BlockSpec((16, L), lambda b, c: (0, b * nch + c)), v1k, v1k],
        out_specs=[pl.BlockSpec((L, SSD_WIDTH), rb), pl.BlockSpec((L, SSD_WIDTH), rb),
                   pl.BlockSpec((None, SSD_WIDTH, SSD_STATE), lambda b, c: (b * nch + c, 0, 0))],
        out_shape=[jax.ShapeDtypeStruct((n, SSD_WIDTH), BF16), jax.ShapeDtypeStruct((n, SSD_WIDTH), F32),
                   jax.ShapeDtypeStruct((bl * nch, SSD_WIDTH, SSD_STATE), F32)],
        scratch_shapes=[pltpu.VMEM((SSD_WIDTH, SSD_STATE), F32), pltpu.VMEM((L, SSD_WIDTH), F32),
                        pltpu.VMEM((L, SSD_WIDTH), BF16)],
        compiler_params=_cparams(("parallel", "arbitrary")),
    )(xc, proj_a, dt, acum, acum_t, dskip_e, norm_w)


def _ssd_bwd2(dys, xc, proj_a, ypre, hprev, dt, gate_d, acum, acum_t, alog128, dskip_e, norm_w, bl, t):
    n = bl * t
    nch = t // CHUNK
    L = CHUNK

    def body(dys_ref, xc_ref, z_ref, yp_ref, hp_ref, dt_ref, gd_ref, ac_ref, act_ref, al_ref, dsk_ref, nw_ref,
             dxc_ref, dz_ref, ddt_ref, dnw_ref, dsk16_ref, da16_ref, db16_ref,
             dh_scr, dy_scr, x_scr, dx_scr, red_scr):
        first = (pl.program_id(0) == 0) & (pl.program_id(1) == 0)

        @pl.when(first)
        def _():
            dnw_ref[...] = jnp.zeros_like(dnw_ref)
            dsk16_ref[...] = jnp.zeros_like(dsk16_ref)
            da16_ref[...] = jnp.zeros_like(da16_ref)
            db16_ref[...] = jnp.zeros_like(db16_ref)

        @pl.when(pl.program_id(1) == 0)
        def _():
            dh_scr[...] = jnp.zeros_like(dh_scr)

        y = yp_ref[...]
        zv = z_ref[...]
        sz = jax.nn.sigmoid(zv)
        gate = zv * sz
        yg = y * gate
        dout = dys_ref[...]
        nw = nw_ref[...]
        for g in range(2):
            gs = slice(g * 512, (g + 1) * 512)
            grp = yg[:, gs]
            rstd = lax.rsqrt(jnp.mean(grp * grp, axis=1, keepdims=True) + EPS)
            ghat = grp * rstd
            dnw_ref[:, gs] += jnp.sum(dout[:, gs] * ghat, axis=0, keepdims=True)
            gw = dout[:, gs] * nw[:, gs]
            dyg = rstd * (gw - ghat * jnp.mean(gw * ghat, axis=1, keepdims=True))
            dy_scr[:, gs] = dyg * gate[:, gs]
            dz_ref[:, gs] = (dyg * y[:, gs] * (sz[:, gs] * (1.0 + zv[:, gs] * (1.0 - sz[:, gs])))).astype(BF16)

        row = lax.broadcasted_iota(jnp.int32, (L, L), 0)
        col = lax.broadcasted_iota(jnp.int32, (L, L), 1)
        causal = row >= col
        lane128 = lax.broadcasted_iota(jnp.int32, (1, L), 1)
        rows128 = lax.broadcasted_iota(jnp.int32, (L, 1), 0)
        last_row = rows128 == (L - 1)
        expander = _head_expander()
        ac_all = ac_ref[...]
        act_all = act_ref[...]
        dt_all = dt_ref[...]
        dt_e = _spread(dt_all, expander)
        ac_e = _spread(ac_all, expander)
        e_in = jnp.exp(ac_e)
        dec = jnp.exp(ac_e[L - 1:L, :] - ac_e)
        xs_all = xc_ref[:, 0:SSD_WIDTH]
        x_all = xs_all * dt_e
        x_scr[...] = x_all.astype(BF16)
        dy_all = dy_scr[...]
        hp_all = hp_ref[...]
        ds_all = dh_scr[...]
        dsk_cols = jnp.sum(dy_all * xs_all, axis=0, keepdims=True)
        dac = jnp.zeros((L, L), F32)
        dac_row = jnp.zeros((L, L), F32)
        ddec_cols = []
        for g in range(2):
            gs = slice(g * 512, (g + 1) * 512)
            bsl = slice(SSD_WIDTH + g * 128, SSD_WIDTH + (g + 1) * 128)
            csl = slice(SSD_WIDTH + 256 + g * 128, SSD_WIDTH + 256 + (g + 1) * 128)
            bg = xc_ref[:, bsl].astype(BF16)
            cg = xc_ref[:, csl].astype(BF16)
            gmat = _dot(cg, bg, NT_DIMS)
            hpb = hp_all[gs, :].astype(BF16)
            dsb = ds_all[gs, :].astype(BF16)
            ch = _dot(cg, hpb, NT_DIMS)
            dye = dy_all[:, gs] * e_in[:, gs]
            dyeb = dye.astype(BF16)
            dc_acc = _dot(dyeb, hpb)
            dhp = _dot(dyeb, cg, TN_DIMS)
            dxd = _dot(bg, dsb, NT_DIMS)
            db_acc = _dot((x_all[:, gs] * dec[:, gs]).astype(BF16), dsb)
            ddec = dxd * x_all[:, gs] * dec[:, gs]
            ddec_cols.append(jnp.sum(ddec, axis=0, keepdims=True))
            dx_scr[:, gs] = dxd * dec[:, gs]
            red_scr[:, gs] = dye * ch - ddec
            dg_sum = jnp.zeros((L, L), F32)
            for r in range(8):
                h = g * 8 + r
                sl = slice(h * HEAD_DIM, (h + 1) * HEAD_DIM)
                onehot_w = lane128 == h
                ldec = jnp.exp(jnp.where(causal, ac_all[:, h:h + 1] - act_all[h:h + 1, :], NEG))
                mf = gmat * ldec
                dyb = dy_scr[:, sl].astype(BF16)
                dm = _dot(dyb, x_scr[:, sl], NT_DIMS)
                dx_scr[:, sl] += _dot(mf.astype(BF16), dyb, TN_DIMS)
                dg_sum = dg_sum + dm * ldec
                wmat = dm * mf
                elast = jnp.exp(ac_all[L - 1:L, h:h + 1])
                hp_h = hp_all[sl, :]
                ds_h = ds_all[sl, :]
                extra = elast * jnp.sum(jnp.sum(hp_h * ds_h, axis=1, keepdims=True), axis=0, keepdims=True)
                dac = dac + jnp.where(onehot_w, jnp.sum(wmat, axis=1, keepdims=True) + jnp.where(last_row, extra, 0.0),
                                      0.0)
                dac_row = dac_row + jnp.where(rows128 == h, -jnp.sum(wmat, axis=0, keepdims=True), 0.0)
                dh_scr[sl, :] = elast * ds_h + dhp[r * HEAD_DIM:(r + 1) * HEAD_DIM, :]
            dgb = dg_sum.astype(BF16)
            dxc_ref[:, csl] = dc_acc + _dot(dgb, bg)
            dxc_ref[:, bsl] = db_acc + _dot(dgb, cg, TN_DIMS)
        dx_all = dx_scr[...]
        dxc_ref[:, 0:SSD_WIDTH] = dx_all * dt_e + dsk_ref[...] * dy_all
        red = red_scr[...]
        dac_slab = _head_sums(red, expander)
        ddec_tot = _head_sums(jnp.broadcast_to(jnp.concatenate(ddec_cols, axis=1), (8, SSD_WIDTH)), expander)
        ddt_x = _head_sums(dx_all * xs_all, expander)
        dsk16_ref[...] += _head_sums(jnp.broadcast_to(dsk_cols, (8, SSD_WIDTH)), expander)[0:1, 0:16]
        dac = dac + dac_slab + jnp.transpose(dac_row) + jnp.where(last_row, ddec_tot[0:1, :], 0.0)
        triu = (row <= col).astype(F32)
        da = jnp.dot(triu, dac, precision=HIGHEST, preferred_element_type=F32)
        a_row = -jnp.exp(al_ref[...])
        ddt = jnp.where(lane128 < 16, (ddt_x + da * a_row) * gd_ref[...], 0.0)
        ddt_ref[...] = ddt
        da16_ref[...] += (jnp.sum(da * dt_all, axis=0, keepdims=True) * a_row)[:, 0:16]
        db16_ref[...] += jnp.sum(ddt, axis=0, keepdims=True)[:, 0:16]

    rb = lambda b, c: (b * nch + nch - 1 - c, 0)
    v1k = pl.BlockSpec((1, SSD_WIDTH), lambda b, c: (0, 0))
    v16 = pl.BlockSpec((1, 16), lambda b, c: (0, 0))
    v128 = pl.BlockSpec((1, 128), lambda b, c: (0, 0))
    wide = pl.BlockSpec((L, SSD_WIDTH), rb)
    s128 = pl.BlockSpec((L, 128), rb)
    return pl.pallas_call(
        body, name="ssd_bwd", grid=(bl, nch),
        in_specs=[wide, pl.BlockSpec((L, CONV_CH), rb), wide, wide,
                  pl.BlockSpec((None, SSD_WIDTH, SSD_STATE), lambda b, c: (b * nch + nch - 1 - c, 0, 0)),
                  s128, s128, s128, pl.BlockSpec((16, L), lambda b, c: (0, b * nch + nch - 1 - c)), v128, v1k, v1k],
        out_specs=[pl.BlockSpec((L, CONV_CH), rb), wide, s128, v1k, v16, v16, v16],
        out_shape=[jax.ShapeDtypeStruct((n, CONV_CH), F32), jax.ShapeDtypeStruct((n, PA_WIDTH), BF16),
                   jax.ShapeDtypeStruct((n, 128), F32), jax.ShapeDtypeStruct((1, SSD_WIDTH), F32),
                   jax.ShapeDtypeStruct((1, 16), F32), jax.ShapeDtypeStruct((1, 16), F32),
                   jax.ShapeDtypeStruct((1, 16), F32)],
        scratch_shapes=[pltpu.VMEM((SSD_WIDTH, SSD_STATE), F32), pltpu.VMEM((L, SSD_WIDTH), F32),
                        pltpu.VMEM((L, SSD_WIDTH), BF16), pltpu.VMEM((L, SSD_WIDTH), F32),
                        pltpu.VMEM((L, SSD_WIDTH), F32)],
        compiler_params=_cparams(("arbitrary", "arbitrary")),
    )(dys, xc, proj_a, ypre, hprev, dt, gate_d, acum, acum_t, alog128, dskip_e, norm_w)


def _attn_fwd(qkv, negc, bl, t):
    n = bl * t
    tb_ = ATT_BLOCK
    nb = t // tb_
    scale2 = LOG2E / math.sqrt(HEAD_DIM)

    def body(q_ref, k_ref, v_ref, c_ref, o_ref, lse_ref):
        row = lax.broadcasted_iota(jnp.int32, (tb_, tb_), 0)
        col = lax.broadcasted_iota(jnp.int32, (tb_, tb_), 1)
        causal = row >= col
        for qi in range(nb):
            r0, lk = qi * tb_, (qi + 1) * tb_
            for j in range(2):
                sl = slice(j * HEAD_DIM, (j + 1) * HEAD_DIM)
                s = _dot(q_ref[r0:lk, sl], k_ref[0:lk, sl], NT_DIMS) * scale2 + c_ref[j:j + 1, 0:lk] * LOG2E
                tail = jnp.where(causal, s[:, r0:lk], NEG)
                s = tail if qi == 0 else jnp.concatenate([s[:, 0:r0], tail], axis=1)
                m = jnp.max(s, axis=1, keepdims=True)
                p = jnp.exp2(s - m)
                l = jnp.sum(p, axis=1, keepdims=True)
                acc = _dot(p.astype(BF16), v_ref[0:lk, sl])
                o_ref[r0:lk, sl] = (acc / l).astype(BF16)
                lse_ref[r0:lk, sl] = jnp.broadcast_to(m + jnp.log(l) * LOG2E, (tb_, HEAD_DIM))

    blk = lambda off: pl.BlockSpec((t, 128), lambda b, hp: (b, off + hp))
    return pl.pallas_call(
        body, name="fox_attn_fwd", grid=(bl, 8),
        in_specs=[blk(0), blk(8), blk(16), pl.BlockSpec((None, None, 8, t), lambda b, hp: (b, hp, 0, 0))],
        out_specs=[blk(0), blk(0)],
        out_shape=[jax.ShapeDtypeStruct((n, ATT_WIDTH), BF16), jax.ShapeDtypeStruct((n, ATT_WIDTH), F32)],
        compiler_params=_cparams(("parallel", "parallel")),
    )(qkv, qkv, qkv, negc)


def _attn_bwd(qkv, do, o, lse, negc, after, bl, t):
    n = bl * t
    tb_ = ATT_BLOCK
    nb = t // tb_
    scale = 1.0 / math.sqrt(HEAD_DIM)
    scale2 = LOG2E * scale

    def body(q_ref, k_ref, v_ref, do_ref, o_ref, lse_ref, c_ref, after_ref, dq_ref, dk_ref, dv_ref, dc_ref,
             dq_scr, delta_scr, dr_scr, qt_scr, dot_scr, dkt_scr, dvt_scr):
        row = lax.broadcasted_iota(jnp.int32, (tb_, tb_), 0)
        col = lax.broadcasted_iota(jnp.int32, (tb_, tb_), 1)
        causal = row >= col
        dq_scr[...] = jnp.zeros_like(dq_scr)
        dr_scr[...] = jnp.zeros_like(dr_scr)
        dc_ref[...] = jnp.zeros_like(dc_ref)
        qt_scr[...] = jnp.transpose(q_ref[...].astype(F32)).astype(BF16)
        dot_scr[...] = jnp.transpose(do_ref[...].astype(F32)).astype(BF16)
        prod = do_ref[...].astype(F32) * o_ref[...].astype(F32)
        for j in range(2):
            sl = slice(j * HEAD_DIM, (j + 1) * HEAD_DIM)
            delta_scr[:, sl] = jnp.broadcast_to(jnp.sum(prod[:, sl], axis=1, keepdims=True), (t, HEAD_DIM))
        for kj in range(nb):
            r0, r1 = kj * tb_, (kj + 1) * tb_
            for j in range(2):
                sl = slice(j * HEAD_DIM, (j + 1) * HEAD_DIM)
                one = slice(j * HEAD_DIM, j * HEAD_DIM + 1)
                kb = k_ref[r0:r1, sl]
                qs = q_ref[r0:t, sl]
                dos = do_ref[r0:t, sl]
                s = _dot(qs, kb, NT_DIMS) * scale2 + c_ref[j:j + 1, r0:r1] * LOG2E
                head = jnp.where(causal, s[0:tb_, :], NEG)
                s = head if kj == nb - 1 else jnp.concatenate([head, s[tb_:, :]], axis=0)
                p = jnp.exp2(s - lse_ref[r0:t, one])
                dp = _dot(dos, v_ref[r0:r1, sl], NT_DIMS)
                ds = p * (dp - delta_scr[r0:t, one])
                dsb = ds.astype(BF16)
                dvt_scr[sl, r0:r1] = _dot(dot_scr[sl, r0:t], p.astype(BF16))
                dkt_scr[sl, r0:r1] = _dot(qt_scr[sl, r0:t], dsb)
                dq_scr[r0:t, sl] += _dot(dsb, kb)
                dr_scr[r0:t, sl] += jnp.broadcast_to(jnp.sum(ds, axis=1, keepdims=True), (t - r0, HEAD_DIM))
                dc_ref[j:j + 1, r0:r1] = -jnp.sum(ds, axis=0, keepdims=True)
        dq_ref[...] = (dq_scr[...] * scale).astype(BF16)
        dk_ref[...] = (jnp.transpose(dkt_scr[...]) * scale).astype(BF16)
        dv_ref[...] = jnp.transpose(dvt_scr[...]).astype(BF16)
        dr_t = jnp.transpose(dr_scr[...])
        for j in range(2):
            dc_ref[j:j + 1, :] += dr_t[j * HEAD_DIM:j * HEAD_DIM + 1, :]

    blk = lambda off: pl.BlockSpec((t, 128), lambda b, hp: (b, off + hp))
    cblk = pl.BlockSpec((None, None, 8, t), lambda b, hp: (b, hp, 0, 0))
    return pl.pallas_call(
        body, name="fox_attn_bwd", grid=(bl, 8),
        in_specs=[blk(0), blk(8), blk(16), blk(0), blk(0), blk(0), cblk, ANY],
        out_specs=[blk(0), blk(0), blk(0), cblk],
        out_shape=[jax.ShapeDtypeStruct((n, ATT_WIDTH), BF16)] * 3 + [jax.ShapeDtypeStruct((bl, 8, 8, t), F32)],
        scratch_shapes=[pltpu.VMEM((t, 128), F32), pltpu.VMEM((t, 128), F32), pltpu.VMEM((t, 128), F32),
                        pltpu.VMEM((128, t), BF16), pltpu.VMEM((128, t), BF16),
                        pltpu.VMEM((128, t), F32), pltpu.VMEM((128, t), F32)],
        compiler_params=_cparams(("parallel", "parallel")),
    )(qkv, qkv, qkv, do, o, lse, negc, after)


def _adamw(w, g, m, v, *, name):
    lead = w.ndim == 3
    r, c = w.shape[-2:]
    tr = _pick(r, (256, IN_SHARD // 3, 128, 64, 32, 16, 8))
    bc1 = 1.0 - ADAM_B1 ** ADAM_STEP
    bc2 = 1.0 - ADAM_B2 ** ADAM_STEP

    def body(w_ref, g_ref, m_ref, v_ref, d_ref, nm_ref, nv_ref):
        gv = g_ref[...]
        mn = ADAM_B1 * m_ref[...] + (1.0 - ADAM_B1) * gv
        vn = ADAM_B2 * v_ref[...] + (1.0 - ADAM_B2) * (gv * gv)
        m_hat = mn / bc1
        v_hat = vn / bc2
        d_ref[...] = -ADAM_LR * (m_hat / (jnp.sqrt(v_hat) + ADAM_EPS) + ADAM_WD * w_ref[...])
        nm_ref[...] = mn
        nv_ref[...] = vn

    flat = pl.BlockSpec((tr, c), lambda i: (i, 0))
    blk = pl.BlockSpec((None, tr, c), lambda i: (0, i, 0)) if lead else flat
    return pl.pallas_call(
        body, name=name, grid=(r // tr,), in_specs=[blk, flat, blk, blk], out_specs=[blk] * 3,
        out_shape=[jax.ShapeDtypeStruct(w.shape, F32)] * 3,
        compiler_params=_cparams(("parallel",)),
    )(w, g, m, v)


def _sum_leading(parts, *, name, out_dtype=F32):
    k, r, c = parts.shape
    tr = _pick(r, (512, 256, 128, 96, 64, 32, 16, 8))

    def body(p_ref, o_ref):
        acc = p_ref[0].astype(F32)
        for i in range(1, k):
            acc = acc + p_ref[i].astype(F32)
        o_ref[...] = acc.astype(out_dtype)

    return pl.pallas_call(
        body, name=name, grid=(r // tr,),
        in_specs=[pl.BlockSpec((k, tr, c), lambda i: (0, i, 0))],
        out_specs=pl.BlockSpec((tr, c), lambda i: (i, 0)),
        out_shape=jax.ShapeDtypeStruct((r, c), out_dtype),
        compiler_params=_cparams(("parallel",)),
    )(parts)


def _add_pair(a, b, *, name):
    k, r, c = a.shape
    tr = _pick(r, (512, 256, 128))

    def body(a_ref, b_ref, o_ref):
        o_ref[...] = (a_ref[...].astype(F32) + b_ref[...].astype(F32)).astype(BF16)

    blk = pl.BlockSpec((None, tr, c), lambda j, i: (j, i, 0))
    return pl.pallas_call(
        body, name=name, grid=(k, r // tr), in_specs=[blk, blk], out_specs=blk,
        out_shape=jax.ShapeDtypeStruct((k, r, c), BF16),
        compiler_params=_cparams(("parallel", "parallel")),
    )(a, b)


ANY = pl.BlockSpec(memory_space=pl.ANY)


def _chip_peers(x, y):
    return [(1 - x, y, 2 * (1 - x) + y), (x, 1 - y, 2 * x + 1 - y), (1 - x, 1 - y, 2 * (1 - x) + 1 - y)]


def _gather_weights(blob, *, name):
    rows, cols = blob.shape
    half_rows = rows // 2

    def body(b_ref, o_ref, send_sems, recv_sems):
        x, y, c = lax.axis_index("x"), lax.axis_index("y"), lax.axis_index("c")
        me = 2 * x + y
        sibling = (x, y, 1 - c)
        peers = _chip_peers(x, y)

        def half(chip, hc):
            return o_ref.at[chip, pl.ds(hc * half_rows, half_rows), :]

        def copy(k, src, chip, hc, to):
            return pltpu.make_async_remote_copy(src_ref=src, dst_ref=half(chip, hc), send_sem=send_sems.at[k],
                                                recv_sem=recv_sems.at[k], device_id=to, device_id_type=MESH)

        my_half = b_ref.at[pl.ds(c * half_rows, half_rows), :]
        first = [copy(k, my_half, me, c, (px, py, c)) for k, (px, py, _) in enumerate(peers)]
        for cp in first:
            cp.start()
        passed = [copy(3 + k, half(pc, c), pc, c, sibling) for k, (_, _, pc) in enumerate(peers)]
        for k, (px, py, pc) in enumerate(peers):
            copy(k, my_half, pc, c, (px, py, c)).wait_recv()
            passed[k].start()
        for k, (_, _, pc) in enumerate(peers):
            copy(3 + k, half(pc, 1 - c), pc, 1 - c, sibling).wait_recv()
        for cp in first + passed:
            cp.wait_send()

    return pl.pallas_call(
        body, name=name, in_specs=[ANY], out_specs=ANY,
        out_shape=jax.ShapeDtypeStruct((N_CHIPS, rows, cols), BF16),
        scratch_shapes=[pltpu.SemaphoreType.DMA((6,)), pltpu.SemaphoreType.DMA((6,))],
    )(blob)


def _swap_halves(g, *, name):
    _, rows, cols = g.shape
    half_rows = rows // 2

    def body(g_ref, o_ref, send_sem, recv_sem):
        x, y, c = lax.axis_index("x"), lax.axis_index("y"), lax.axis_index("c")
        cp = pltpu.make_async_remote_copy(
            src_ref=g_ref.at[:, pl.ds((1 - c) * half_rows, half_rows), :], dst_ref=o_ref,
            send_sem=send_sem, recv_sem=recv_sem, device_id=(x, y, 1 - c), device_id_type=MESH)
        cp.start()
        cp.wait()

    return pl.pallas_call(
        body, name=name, in_specs=[ANY], out_specs=ANY,
        out_shape=jax.ShapeDtypeStruct((N_CHIPS, half_rows, cols), BF16),
        scratch_shapes=[pltpu.SemaphoreType.DMA, pltpu.SemaphoreType.DMA],
    )(g)


def _exchange_chips(p, *, name):
    def body(p_ref, o_ref, send_sems, recv_sems):
        x, y, c = lax.axis_index("x"), lax.axis_index("y"), lax.axis_index("c")
        me = 2 * x + y
        peers = _chip_peers(x, y)
        cps = [pltpu.make_async_remote_copy(src_ref=p_ref.at[pc], dst_ref=o_ref.at[me], send_sem=send_sems.at[k],
                                            recv_sem=recv_sems.at[k], device_id=(px, py, c), device_id_type=MESH)
               for k, (px, py, pc) in enumerate(peers)]
        for cp in cps:
            cp.start()
        for k, (px, py, pc) in enumerate(peers):
            pltpu.make_async_remote_copy(src_ref=p_ref.at[pc], dst_ref=o_ref.at[pc], send_sem=send_sems.at[k],
                                         recv_sem=recv_sems.at[k], device_id=(px, py, c),
                                         device_id_type=MESH).wait_recv()
        for cp in cps:
            cp.wait_send()

    got = pl.pallas_call(
        body, name=name, in_specs=[ANY], out_specs=ANY,
        out_shape=jax.ShapeDtypeStruct(p.shape, BF16),
        scratch_shapes=[pltpu.SemaphoreType.DMA((3,)), pltpu.SemaphoreType.DMA((3,))],
    )(p)
    me = 2 * lax.axis_index("x") + lax.axis_index("y")
    return lax.dynamic_update_slice(got, lax.dynamic_slice_in_dim(p, me, 1, axis=0), (me, 0, 0))


HBM_SPEC = pl.BlockSpec(memory_space=pltpu.HBM)
SEM_SPEC = pl.BlockSpec(memory_space=pltpu.SEMAPHORE)
SPLIT_EFFECT = pltpu.SideEffectType.DATAFLOW_SIDE_EFFECTING


def _gather_peers_copies(b_ref, land_ref, send_sems, recv_sems, sending):
    x, y, c = lax.axis_index("x"), lax.axis_index("y"), lax.axis_index("c")
    me = 2 * x + y
    half_rows = b_ref.shape[0] // 2
    src = b_ref.at[pl.ds(c * half_rows, half_rows), :]
    return [pltpu.make_async_remote_copy(
        src_ref=src, dst_ref=land_ref.at[me if sending else pc, pl.ds(c * half_rows, half_rows), :],
        send_sem=send_sems.at[k], recv_sem=recv_sems.at[k], device_id=(px, py, c), device_id_type=MESH)
        for k, (px, py, pc) in enumerate(_chip_peers(x, y))]


def _gather_start(blob, after, *, name):
    shape = (N_CHIPS,) + blob.shape

    def body(b_ref, land_ref, after_ref, send_sems, recv_sems, b_thru, land_thru, token):
        for cp in _gather_peers_copies(b_ref, land_ref, send_sems, recv_sems, True):
            cp.start()
        token[...] = jnp.zeros_like(token)

    return pl.pallas_call(
        body, name=name,
        out_shape=(pltpu.SemaphoreType.DMA((3,)), pltpu.SemaphoreType.DMA((3,)), pltpu.HBM(blob.shape, blob.dtype),
                   pltpu.HBM(shape, blob.dtype), jax.ShapeDtypeStruct((8, 128), F32)),
        in_specs=(HBM_SPEC, HBM_SPEC, ANY),
        out_specs=(SEM_SPEC, SEM_SPEC, HBM_SPEC, HBM_SPEC, pl.BlockSpec(memory_space=pltpu.VMEM)),
        input_output_aliases={0: 2, 1: 3},
        compiler_params=pltpu.CompilerParams(has_side_effects=SPLIT_EFFECT),
    )(pltpu.with_memory_space_constraint(blob, pltpu.HBM),
      pltpu.with_memory_space_constraint(lax.empty(shape, blob.dtype), pltpu.HBM), after)


def _gather_wait(send_sems, recv_sems, b_thru, land_thru, after, *, name):
    def body(b_ref, land_ref, send_sems, recv_sems, after_ref, b_dead, got_ref):
        for cp in _gather_peers_copies(b_ref, land_ref, send_sems, recv_sems, False):
            cp.wait_send()
            cp.wait_recv()

    return pl.pallas_call(
        body, name=name,
        out_shape=(pltpu.HBM(b_thru.shape, b_thru.dtype), pltpu.HBM(land_thru.shape, land_thru.dtype)),
        in_specs=(HBM_SPEC, HBM_SPEC, SEM_SPEC, SEM_SPEC, ANY), out_specs=(HBM_SPEC, HBM_SPEC),
        input_output_aliases={0: 0, 1: 1},
        compiler_params=pltpu.CompilerParams(has_side_effects=SPLIT_EFFECT),
    )(b_thru, land_thru, send_sems, recv_sems, after)


def _gather_forward(land, *, name):
    half_rows = land.shape[1] // 2

    def body(l_ref, o_ref, send_sems, recv_sems):
        x, y, c = lax.axis_index("x"), lax.axis_index("y"), lax.axis_index("c")
        cps = []
        for k, (_, _, pc) in enumerate(_chip_peers(x, y)):
            mine = pl.ds(c * half_rows, half_rows)
            cps.append(pltpu.make_async_remote_copy(
                src_ref=l_ref.at[pc, mine, :], dst_ref=o_ref.at[pc, mine, :], send_sem=send_sems.at[k],
                recv_sem=recv_sems.at[k], device_id=(x, y, 1 - c), device_id_type=MESH))
        for cp in cps:
            cp.start()
        for k, (_, _, pc) in enumerate(_chip_peers(x, y)):
            theirs = pl.ds((1 - c) * half_rows, half_rows)
            pltpu.make_async_remote_copy(
                src_ref=l_ref.at[pc, theirs, :], dst_ref=o_ref.at[pc, theirs, :], send_sem=send_sems.at[k],
                recv_sem=recv_sems.at[k], device_id=(x, y, 1 - c), device_id_type=MESH).wait_recv()
        for cp in cps:
            cp.wait_send()

    return pl.pallas_call(
        body, name=name, in_specs=[ANY], out_specs=ANY, input_output_aliases={0: 0},
        out_shape=jax.ShapeDtypeStruct(land.shape, land.dtype),
        scratch_shapes=[pltpu.SemaphoreType.DMA((3,)), pltpu.SemaphoreType.DMA((3,))],
    )(land)


def _exchange_peers_copies(p_ref, land_ref, send_sems, recv_sems, sending):
    x, y, c = lax.axis_index("x"), lax.axis_index("y"), lax.axis_index("c")
    me = 2 * x + y
    return [pltpu.make_async_remote_copy(src_ref=p_ref.at[pc], dst_ref=land_ref.at[me if sending else pc],
                                         send_sem=send_sems.at[k], recv_sem=recv_sems.at[k],
                                         device_id=(px, py, c), device_id_type=MESH)
            for k, (px, py, pc) in enumerate(_chip_peers(x, y))]


def _exchange_start(p, *, name):
    def body(p_ref, land_ref, send_sems, recv_sems, p_thru, land_thru, token):
        for cp in _exchange_peers_copies(p_ref, land_ref, send_sems, recv_sems, True):
            cp.start()
        token[...] = jnp.zeros_like(token)

    return pl.pallas_call(
        body, name=name,
        out_shape=(pltpu.SemaphoreType.DMA((3,)), pltpu.SemaphoreType.DMA((3,)), pltpu.HBM(p.shape, p.dtype),
                   pltpu.HBM(p.shape, p.dtype), jax.ShapeDtypeStruct((8, 128), F32)),
        in_specs=(HBM_SPEC, HBM_SPEC),
        out_specs=(SEM_SPEC, SEM_SPEC, HBM_SPEC, HBM_SPEC, pl.BlockSpec(memory_space=pltpu.VMEM)),
        input_output_aliases={0: 2, 1: 3},
        compiler_params=pltpu.CompilerParams(has_side_effects=SPLIT_EFFECT),
    )(pltpu.with_memory_space_constraint(p, pltpu.HBM),
      pltpu.with_memory_space_constraint(lax.empty(p.shape, p.dtype), pltpu.HBM))


def _exchange_wait(send_sems, recv_sems, p_thru, land_thru, after, *, name):
    def body(p_ref, land_ref, send_sems, recv_sems, after_ref, p_dead, got_ref):
        for cp in _exchange_peers_copies(p_ref, land_ref, send_sems, recv_sems, False):
            cp.wait_send()
            cp.wait_recv()

    return pl.pallas_call(
        body, name=name,
        out_shape=(pltpu.HBM(p_thru.shape, p_thru.dtype), pltpu.HBM(p_thru.shape, p_thru.dtype)),
        in_specs=(HBM_SPEC, HBM_SPEC, SEM_SPEC, SEM_SPEC, ANY), out_specs=(HBM_SPEC, HBM_SPEC),
        input_output_aliases={0: 0, 1: 1},
        compiler_params=pltpu.CompilerParams(has_side_effects=SPLIT_EFFECT),
    )(p_thru, land_thru, send_sems, recv_sems, after)


def _sum_parts(parts, own, *, name):
    k, r, c = parts.shape
    tr = _pick(r, (512, 256, 128))

    def body(p_ref, own_ref, o_ref):
        me = 2 * lax.axis_index("x") + lax.axis_index("y")
        acc = jnp.zeros((tr, c), F32)
        for i in range(k):
            acc = acc + jnp.where(me == i, own_ref[i], p_ref[i]).astype(F32)
        o_ref[...] = acc

    blk = pl.BlockSpec((k, tr, c), lambda i: (0, i, 0))
    return pl.pallas_call(
        body, name=name, grid=(r // tr,), in_specs=[blk, blk],
        out_specs=pl.BlockSpec((tr, c), lambda i: (i, 0)),
        out_shape=jax.ShapeDtypeStruct((r, c), F32),
        compiler_params=_cparams(("parallel",)),
    )(parts, own)


def _join_halves(gh, *, name):
    def body(g_ref, o_ref, send_sem, recv_sem):
        x, y, c = lax.axis_index("x"), lax.axis_index("y"), lax.axis_index("c")
        cp = pltpu.make_async_remote_copy(src_ref=g_ref, dst_ref=o_ref, send_sem=send_sem, recv_sem=recv_sem,
                                          device_id=(x, y, 1 - c), device_id_type=MESH)
        cp.start()
        cp.wait()

    other = pl.pallas_call(
        body, name=name, in_specs=[ANY], out_specs=ANY,
        out_shape=jax.ShapeDtypeStruct(gh.shape, F32),
        scratch_shapes=[pltpu.SemaphoreType.DMA, pltpu.SemaphoreType.DMA],
    )(gh)
    south = lax.axis_index("c") == 0
    return jnp.concatenate([jnp.where(south, gh, other), jnp.where(south, other, gh)], axis=0)


def _gather_small(s, *, name):
    rows = s.shape[0]

    def body(s_ref, o_ref, send_sems, recv_sems, local_sem):
        x, y, c = lax.axis_index("x"), lax.axis_index("y"), lax.axis_index("c")
        me = 4 * x + 2 * y + c
        mine = pltpu.make_async_copy(s_ref, o_ref.at[me], local_sem)
        mine.start()
        peers = []
        for k in range(1, 8):
            peers.append((1 - x if k & 4 else x, 1 - y if k & 2 else y, 1 - c if k & 1 else c))
        cps = [pltpu.make_async_remote_copy(src_ref=s_ref, dst_ref=o_ref.at[me], send_sem=send_sems.at[k],
                                            recv_sem=recv_sems.at[k], device_id=p, device_id_type=MESH)
               for k, p in enumerate(peers)]
        for cp in cps:
            cp.start()
        for k, (px, py, pc) in enumerate(peers):
            pltpu.make_async_remote_copy(src_ref=s_ref, dst_ref=o_ref.at[4 * px + 2 * py + pc],
                                         send_sem=send_sems.at[k], recv_sem=recv_sems.at[k],
                                         device_id=(px, py, pc), device_id_type=MESH).wait_recv()
        for cp in cps:
            cp.wait_send()
        mine.wait()

    return pl.pallas_call(
        body, name=name, in_specs=[ANY], out_specs=ANY,
        out_shape=jax.ShapeDtypeStruct((8, rows, 128), F32),
        scratch_shapes=[pltpu.SemaphoreType.DMA((7,)), pltpu.SemaphoreType.DMA((7,)), pltpu.SemaphoreType.DMA],
    )(s)


IN_SHARD = IN_WIDTH // N_CHIPS
IN_SHARD_PAD = 1536
UP_ROWS, DOWN_ROWS, OUT_ROWS = 1024, 1024, 512
REST_ROWS = UP_ROWS + DOWN_ROWS + OUT_ROWS


def _pack_in(w_in_s):
    return jnp.pad(w_in_s, ((0, 0), (0, IN_SHARD_PAD - IN_SHARD))).astype(BF16)


def _pack_rest(w_out_s, w_up_s, w_down_s):
    return jnp.concatenate([w_up_s, w_down_s, w_out_s], axis=0).astype(BF16)


def _unpack_rest(blob):
    return (blob[UP_ROWS + DOWN_ROWS:], blob[0:UP_ROWS], blob[UP_ROWS:UP_ROWS + DOWN_ROWS])


def _with_own(gathered, own):
    me = 2 * lax.axis_index("x") + lax.axis_index("y")
    return [jnp.where(me == j, own, gathered[j]) for j in range(N_CHIPS)]


def _full_w_in(g_in, own):
    return jnp.concatenate([s[:, :IN_SHARD] for s in _with_own(g_in, own)], axis=1)


def _full_rest(g_rest, own):
    parts = [_unpack_rest(s) for s in _with_own(g_rest, own)]
    w_out = jnp.concatenate([p[0] for p in parts], axis=0)
    w_up = jnp.concatenate([p[1] for p in parts], axis=1)
    w_down = jnp.concatenate([p[2] for p in parts], axis=0)
    return w_out, w_up, w_down


def _split_w_in(w_in):
    z_xbc = w_in[:, 0:2560]
    dt = w_in[:, 2560:2576]
    qkv = w_in[:, 2576:5648]
    f = w_in[:, 5648:5664]
    pad = jnp.zeros((w_in.shape[0], PA_WIDTH - 2592), w_in.dtype)
    return jnp.concatenate([z_xbc, dt, f, pad], axis=1), qkv


def _merge_w_in(d_a, d_qkv):
    return jnp.concatenate([d_a[:, 0:2560], d_a[:, 2560:2576], d_qkv, d_a[:, 2576:2592]], axis=1)


def _local_step(x3, target3, w_in, rest_weights, norm_mix_w, conv_w, conv_b, dt_bias, a_log, d_skip,
                ssd_norm_w, f_bias, norm_mlp_w, norm_final_w, first_after=None, early_grads=None, late_grads=None):
    bl, t, d = x3.shape
    n = bl * t
    x = x3.reshape(n, d)
    target = target3.reshape(n, d)
    w_a, w_qkv = _split_w_in(w_in)
    nfw = norm_final_w.reshape(1, d)
    dskip_e = jnp.repeat(d_skip, HEAD_DIM, axis=1)
    nb = t // ATT_BLOCK

    h0, rstd0 = _rmsnorm_fwd(x, norm_mix_w, name="norm_mix_fwd")
    r1, r2, r4, kt = min(n, 1024), min(n, 512), min(n, 256), min(n, 512)
    proj_a = _mm(h0, w_a, name="proj_a", tiles=(r2, PA_WIDTH, D_MODEL), after=first_after)
    qkv = _mm(h0, w_qkv, name="proj_qkv", tiles=(r2, QKV_WIDTH, D_MODEL), out_dtype=BF16)
    bias128 = jnp.concatenate([dt_bias, f_bias, jnp.zeros((1, 96), F32)], axis=1)
    alog128 = jnp.concatenate([a_log, jnp.zeros((1, 112), F32)], axis=1)
    dt, gate_d, acum, ccum = _prep(proj_a, bias128, alog128, bl, t)
    acum_t = acum[:, 0:16].T
    negc = jnp.pad(-ccum.reshape(bl, t, 8, 2).transpose(0, 2, 3, 1), ((0, 0), (0, 0), (0, 6), (0, 0)))
    xc = _conv_fwd(proj_a, conv_w, conv_b, bl, t)
    y_ssd, y_pre, hprev = _ssd_fwd2(xc, proj_a, dt, acum, acum_t, dskip_e, ssd_norm_w, bl, t)
    y_att, lse = _attn_fwd(qkv, negc, bl, t)
    w_out, w_up, w_down = rest_weights(y_att)
    wo_s, wo_a = w_out[:SSD_WIDTH], w_out[SSD_WIDTH:]
    t1 = _mm(y_ssd, wo_s, name="out_proj_ssd", tiles=(r1, D_MODEL, SSD_WIDTH), res=x)
    h1, h1n, rstd1 = _mm_norm_fwd(y_att, wo_a, t1, norm_mlp_w, name="out_proj_att_norm_mlp", tm=r2)
    up = _mm(h1n, w_up, name="mlp_up", tiles=(r2, D_FF, D_MODEL), out_dtype=BF16)
    dh2, dh2b, loss, d_nfw = _mm_final(up, w_down, h1, nfw, target, name="mlp_down_final_norm_loss", tm=r2,
                                       a_act="relu2")

    dup = _mm(dh2b, w_down, name="mlp_down_bwd_act", tiles=(r2, D_FF, D_MODEL), tb=True, epi_up=up, out_dtype=BF16)
    rest_shape = (N_CHIPS, REST_ROWS, D_MODEL)
    gb_rest = _mm(up, dh2b, name="mlp_down_bwd_w", tiles=(DOWN_ROWS, D_MODEL, kt), ta=True, a_act="relu2",
                  out_dtype=BF16, into=(rest_shape, (None, DOWN_ROWS, D_MODEL), lambda i, j, k: (i, 1, 0), None))
    dh1, dh1b, d_nmlp = _mm_norm_bwd(dup, w_up, None, h1, rstd1, norm_mlp_w, dh2, name="mlp_up_bwd_act_norm_mlp",
                                     tm=r2)
    gb_rest = _mm(h1n, dup, name="mlp_up_bwd_w", tiles=(D_MODEL, UP_ROWS, kt), ta=True, out_dtype=BF16,
                  into=(rest_shape, (None, D_MODEL, UP_ROWS), lambda i, j, k: (j, 0, 0), gb_rest))
    dys = _mm(dh1b, wo_s, name="out_proj_bwd_ssd", tiles=(r1, SSD_WIDTH, D_MODEL), tb=True)
    do = _mm(dh1b, wo_a, name="out_proj_bwd_att", tiles=(r1, ATT_WIDTH, D_MODEL), tb=True, out_dtype=BF16)
    out_block = (UP_ROWS + DOWN_ROWS) // OUT_ROWS
    for half, (y_half, tag) in enumerate(((y_ssd, "ssd"), (y_att, "att"))):
        gb_rest = _mm(y_half, dh1b, name="out_proj_bwd_w_" + tag, tiles=(OUT_ROWS, D_MODEL, kt), ta=True,
                      out_dtype=BF16, into=(rest_shape, (None, OUT_ROWS, D_MODEL),
                                            functools.partial(lambda i, j, k, h: (2 * h + i, out_block, 0), h=half),
                                            gb_rest))
    token = jnp.zeros((8, 128), F32) if early_grads is None else early_grads(gb_rest)
    dq, dk, dv, dcb = _attn_bwd(qkv, do, y_att, lse, negc, token, bl, t)
    dc = jnp.pad(dcb[:, :, 0:2, :].transpose(0, 3, 1, 2).reshape(n, 16), ((0, 0), (16, 96)))
    dxc, dpa, ddt_raw, d_snw, d_dsk, d_alog, d_dtb = _ssd_bwd2(dys, xc, proj_a, y_pre, hprev, dt, gate_d, acum,
                                                             acum_t, alog128, dskip_e, ssd_norm_w, bl, t)
    dpa, d_conv_w, d_conv_b = _conv_bwd(dxc, proj_a, conv_w, conv_b, dpa, bl, t)
    dproj_a, d_fb = _fpost(dc, gate_d, ddt_raw, dpa, bl, t)
    dqkv = jnp.concatenate([dq, dk, dv], axis=1)
    d_w_a = _mm(h0, dproj_a, name="proj_a_bwd_w", tiles=(1024, 896, kt), ta=True, out_dtype=BF16)
    d_w_qkv = _mm(h0, dqkv, name="proj_qkv_bwd_w", tiles=(1024, 1024, kt), ta=True, out_dtype=BF16)
    d_w_in = _merge_w_in(d_w_a, d_w_qkv)
    late_token = None if late_grads is None else late_grads(d_w_in)
    t2 = _mm(dproj_a, w_a, name="proj_a_bwd_act", tiles=(r1, D_MODEL, PA_WIDTH), tb=True, after=late_token)
    dx, _, d_nmix = _mm_norm_bwd(dqkv, w_qkv, t2, x, rstd0, norm_mix_w, dh1, name="proj_qkv_bwd_act_norm_mix",
                                 tm=r2)

    grads = dict(norm_mix_w=d_nmix, w_in=d_w_in, conv_w=d_conv_w, conv_b=d_conv_b,
                 dt_bias=d_dtb, a_log=d_alog, d_skip=d_dsk, ssd_norm_w=d_snw, f_bias=d_fb, rest=gb_rest,
                 norm_mlp_w=d_nmlp, norm_final_w=d_nfw)
    return dx.reshape(bl, t, d), loss, grads


SMALL_ORDER = ("norm_mix_w", "conv_w", "conv_b", "dt_bias", "a_log", "d_skip", "ssd_norm_w", "f_bias",
               "norm_mlp_w", "norm_final_w")
SMALL_SIZES = (1024, 4 * CONV_CH, CONV_CH, 16, 16, 16, 1024, 16, 1024, 1024)


def _pack_small(vals, rows):
    flat = jnp.concatenate([v.reshape(-1).astype(F32) for v in vals])
    return jnp.pad(flat, (0, rows * 128 - flat.shape[0])).reshape(rows, 128)


def _unpack_small(packed, sizes):
    flat = packed.reshape(-1)
    out, o = [], 0
    for s in sizes:
        out.append(flat[o:o + s])
        o += s
    return out


def kernel(x, norm_mix_w, w_in, conv_w, conv_b, dt_bias, a_log, d_skip, ssd_norm_w, f_bias, w_out, norm_mlp_w, w_up, w_down, norm_final_w, loss_target, m_norm_mix_w, m_w_in, m_conv_w, m_conv_b, m_dt_bias, m_a_log, m_d_skip, m_ssd_norm_w, m_f_bias, m_w_out, m_norm_mlp_w, m_w_up, m_w_down, m_norm_final_w, v_norm_mix_w, v_w_in, v_conv_w, v_conv_b, v_dt_bias, v_a_log, v_d_skip, v_ssd_norm_w, v_f_bias, v_w_out, v_norm_mlp_w, v_w_up, v_w_down, v_norm_final_w):
    chip = 2 * lax.axis_index("x") + lax.axis_index("y")
    cw = CONV_CH // N_CHIPS

    own_in = _pack_in(w_in[0])
    own_rest = _pack_rest(w_out[0], w_up[0], w_down[0])
    g_in = _gather_weights(own_in, name="gather_w_in")
    w_in_f = _full_w_in(g_in, own_in)
    *rest_handles, rest_token = _gather_start(own_rest, g_in, name="gather_start_rest")

    def rest_weights(after):
        _, landed = _gather_wait(*rest_handles, after, name="gather_wait_rest")
        return _full_rest(_gather_forward(landed, name="gather_forward_rest"), own_rest)
    small_all = _gather_small(_pack_small([conv_w[0]], 16), name="gather_conv_w")
    conv_w_f = jnp.concatenate([small_all[2 * j].reshape(-1)[:4 * cw].reshape(4, cw) for j in range(N_CHIPS)], axis=1)

    c = lax.axis_index("c")

    def chip_partial(gb, tag):
        half_rows = gb.shape[1] // 2
        from_sibling = _swap_halves(gb, name="grad_swap_halves_" + tag)
        my_half = lax.dynamic_slice_in_dim(gb, c * half_rows, half_rows, axis=1)
        return _add_pair(my_half, from_sibling, name="grad_add_sibling_" + tag)

    in_flight = {}

    def early_grads(gb_rest):
        part = chip_partial(gb_rest, "rest")
        *handles, token = _exchange_start(part, name="grad_exchange_start_rest")
        in_flight["rest"] = handles
        return token

    def late_grads(d_w_in):
        gb_in = jnp.stack([_pack_in(d_w_in[:, j * IN_SHARD:(j + 1) * IN_SHARD]) for j in range(N_CHIPS)])
        *handles, token = _exchange_start(chip_partial(gb_in, "in"), name="grad_exchange_start_in")
        in_flight["in"] = handles
        return token

    dx, loss_part, g = _local_step(x, loss_target, w_in_f, rest_weights, norm_mix_w, conv_w_f,
                                   conv_b, dt_bias, a_log, d_skip, ssd_norm_w, f_bias, norm_mlp_w, norm_final_w,
                                   first_after=rest_token, early_grads=early_grads, late_grads=late_grads)

    send_sems, recv_sems, part_rest, land_rest = in_flight["rest"]
    part_rest, parts_rest = _exchange_wait(send_sems, recv_sems, part_rest, land_rest, dx,
                                           name="grad_exchange_wait_rest")
    g_rest_half = _sum_parts(parts_rest, part_rest, name="grad_sum_chips_rest")
    g_w_out, g_w_up, g_w_down = _unpack_rest(_join_halves(g_rest_half, name="grad_join_halves_rest"))

    part_in, parts_in = _exchange_wait(*in_flight["in"], dx, name="grad_exchange_wait_in")
    g_in_half = _sum_parts(parts_in, part_in, name="grad_sum_chips_in")
    g_w_in = _join_halves(g_in_half, name="grad_join_halves_in")[:, :IN_SHARD]

    small_vals = [g[k] for k in SMALL_ORDER] + [loss_part[:, 0:1]]
    small_sum = _sum_leading(_gather_small(_pack_small(small_vals, SMALL_ROWS), name="gather_small_grads"), name="small_sum")
    sg = dict(zip(SMALL_ORDER + ("loss",), _unpack_small(small_sum, SMALL_SIZES + (1,))))
    loss = sg["loss"].reshape(())
    g_conv_full = sg["conv_w"].reshape(4, CONV_CH)
    g_conv = lax.dynamic_slice_in_dim(g_conv_full, chip * cw, cw, axis=1)

    grads = dict(norm_mix_w=sg["norm_mix_w"].reshape(1, -1), w_in=g_w_in[None], conv_w=g_conv[None],
                 conv_b=sg["conv_b"].reshape(1, -1), dt_bias=sg["dt_bias"].reshape(1, -1),
                 a_log=sg["a_log"].reshape(1, -1), d_skip=sg["d_skip"].reshape(1, -1),
                 ssd_norm_w=sg["ssd_norm_w"].reshape(1, -1), f_bias=sg["f_bias"].reshape(1, -1), w_out=g_w_out[None],
                 norm_mlp_w=sg["norm_mlp_w"].reshape(1, -1), w_up=g_w_up[None], w_down=g_w_down[None],
                 norm_final_w=sg["norm_final_w"])
    weights = dict(norm_mix_w=norm_mix_w, w_in=w_in, conv_w=conv_w, conv_b=conv_b, dt_bias=dt_bias, a_log=a_log,
                   d_skip=d_skip, ssd_norm_w=ssd_norm_w, f_bias=f_bias, w_out=w_out, norm_mlp_w=norm_mlp_w,
                   w_up=w_up, w_down=w_down, norm_final_w=norm_final_w)
    ms = dict(norm_mix_w=m_norm_mix_w, w_in=m_w_in, conv_w=m_conv_w, conv_b=m_conv_b, dt_bias=m_dt_bias,
              a_log=m_a_log, d_skip=m_d_skip, ssd_norm_w=m_ssd_norm_w, f_bias=m_f_bias, w_out=m_w_out,
              norm_mlp_w=m_norm_mlp_w, w_up=m_w_up, w_down=m_w_down, norm_final_w=m_norm_final_w)
    vs = dict(norm_mix_w=v_norm_mix_w, w_in=v_w_in, conv_w=v_conv_w, conv_b=v_conv_b, dt_bias=v_dt_bias,
              a_log=v_a_log, d_skip=v_d_skip, ssd_norm_w=v_ssd_norm_w, f_bias=v_f_bias, w_out=v_w_out,
              norm_mlp_w=v_norm_mlp_w, w_up=v_w_up, w_down=v_w_down, norm_final_w=v_norm_final_w)
    names = list(weights)
    big = ("w_in", "w_out", "w_up", "w_down")
    delta, new_m, new_v = {}, {}, {}
    for k, g2 in zip(big[1:], (g_w_out, g_w_up, g_w_down)):
        delta[k], new_m[k], new_v[k] = _adamw(weights[k], g2, ms[k], vs[k], name="adamw_" + k)
    g_in_t = g_w_in.T
    outs_t = _adamw(w_in[0].T, g_in_t, m_w_in[0].T, v_w_in[0].T, name="adamw_w_in")
    delta["w_in"], new_m["w_in"], new_v["w_in"] = [o.T[None] for o in outs_t]
    grads["w_in"] = g_in_t.T[None]
    smalls = [k for k in names if k not in big]
    sizes = [math.prod(weights[k].shape) for k in smalls]
    rows = -(-sum(sizes) // 1024) * 8
    packs = [_pack_small([d[k] for k in smalls], rows) for d in (weights, grads, ms, vs)]
    outs = _adamw(*packs, name="adamw_small")
    for o, dst in zip(outs, (delta, new_m, new_v)):
        for k, val in zip(smalls, _unpack_small(o, sizes)):
            dst[k] = val.reshape(weights[k].shape)
    return (loss, dx, *[grads[k] for k in names], *[delta[k] for k in names], *[new_m[k] for k in names],
            *[new_v[k] for k in names])
```

```python
import functools
import math

import jax
import jax.numpy as jnp
from jax import lax
from jax.experimental import pallas as pl
from jax.experimental.pallas import tpu as pltpu

F32 = jnp.float32
BF16 = jnp.bfloat16
HIGHEST = lax.Precision.HIGHEST
MESH = pl.DeviceIdType.MESH

D_MODEL = 1024
SSD_HEADS = 16
HEAD_DIM = 64
SSD_WIDTH = 1024
SSD_STATE = 128
CONV_CH = 1536
CHUNK = 128
ATT_WIDTH = 1024
EPS = 1e-5
IN_WIDTH = 5664
PA_WIDTH = 2688
QKV_WIDTH = 3072
D_FF = 4096
ATT_BLOCK = 256
NEG = -1e30
LOG2E = 1.4426950408889634
VMEM_LIMIT = 48 * 1024 * 1024

ADAM_LR = 0.001
ADAM_B1 = 0.9
ADAM_B2 = 0.999
ADAM_EPS = 1e-08
ADAM_WD = 0.01
ADAM_STEP = 10

N_CHIPS = 4
SMALL_ROWS = 96


def _cparams(sem):
    return pltpu.CompilerParams(dimension_semantics=sem, vmem_limit_bytes=VMEM_LIMIT)


def _pick(n, cands):
    for c in cands:
        if n % c == 0:
            return c
    return n


MM_CHUNK = 512


def _mm(a, b, *, name, tiles, ta=False, tb=False, out_dtype=F32, res=None, a_act=None, epi_up=None, after=None,
        into=None):
    n_unread = (after is not None) + (into is not None and into[3] is not None)
    if ta:
        K, M = a.shape
    else:
        M, K = a.shape
    if tb:
        N, K2 = b.shape
    else:
        K2, N = b.shape
    assert K == K2, (a.shape, b.shape)
    tm, tn, tk = tiles
    assert M % tm == 0 and N % tn == 0 and K % tk == 0, (name, M, N, K, tiles)
    nk = K // tk
    dn = (((0 if ta else 1,), (1 if tb else 0,)), ((), ()))
    has_res = res is not None
    has_up = epi_up is not None
    cn = _pick(tn, (MM_CHUNK, 384, 256, 128))

    def prologue(av):
        if a_act == "relu2":
            r = jnp.maximum(av.astype(F32), 0.0)
            av = r * r
        return av.astype(BF16)

    def epilogue(out, res_v, up_v):
        if has_res:
            out = out + res_v.astype(F32)
        if has_up:
            out = out * (2.0 * jnp.maximum(up_v.astype(F32), 0.0))
        return out.astype(out_dtype)

    def body(*refs):
        a_ref, b_ref = refs[0], refs[1]
        i = 2
        res_ref = up_ref = None
        if has_res:
            res_ref = refs[i]
            i += 1
        if has_up:
            up_ref = refs[i]
            i += 1
        i += n_unread
        o_ref = refs[i]
        if nk == 1:
            av = prologue(a_ref[...])
            for c in range(tn // cn):
                cs = slice(c * cn, (c + 1) * cn)
                bv = (b_ref[cs, :] if tb else b_ref[:, cs]).astype(BF16)
                out = lax.dot_general(av, bv, dn, preferred_element_type=F32)
                o_ref[:, cs] = epilogue(out, res_ref[:, cs] if has_res else None, up_ref[:, cs] if has_up else None)
            return
        acc_ref = refs[i + 1]
        k = pl.program_id(2)

        @pl.when(k == 0)
        def _():
            acc_ref[...] = jnp.zeros_like(acc_ref)

        acc_ref[...] += lax.dot_general(prologue(a_ref[...]), b_ref[...].astype(BF16), dn,
                                        preferred_element_type=F32)

        @pl.when(k == nk - 1)
        def _():
            out = epilogue(acc_ref[...], res_ref[...] if has_res else None, up_ref[...] if has_up else None)
            o_ref[...] = out.reshape(o_ref.shape)

    a_spec = pl.BlockSpec((tk, tm), lambda i, j, k: (k, i)) if ta else pl.BlockSpec((tm, tk), lambda i, j, k: (i, k))
    b_spec = pl.BlockSpec((tn, tk), lambda i, j, k: (j, k)) if tb else pl.BlockSpec((tk, tn), lambda i, j, k: (k, j))
    o_spec = pl.BlockSpec((tm, tn), lambda i, j, k: (i, j))
    ins, specs = [a, b], [a_spec, b_spec]
    if has_res:
        ins.append(res)
        specs.append(o_spec)
    if has_up:
        ins.append(epi_up)
        specs.append(o_spec)
    if after is not None:
        ins.append(after)
        specs.append(pl.BlockSpec(memory_space=pl.ANY))
    out_shape, out_spec, aliases = jax.ShapeDtypeStruct((M, N), out_dtype), o_spec, {}
    if into is not None:
        shape, block, index, buf = into
        out_shape, out_spec = jax.ShapeDtypeStruct(shape, out_dtype), pl.BlockSpec(block, index)
        if buf is not None:
            aliases = {len(ins): 0}
            ins.append(buf)
            specs.append(pl.BlockSpec(memory_space=pl.ANY))
    return pl.pallas_call(
        body, name=name, grid=(M // tm, N // tn, nk),
        in_specs=specs, out_specs=out_spec, out_shape=out_shape, input_output_aliases=aliases,
        scratch_shapes=[] if nk == 1 else [pltpu.VMEM((tm, tn), F32)],
        compiler_params=_cparams(("parallel", "parallel", "arbitrary")),
    )(*ins)


def _rmsnorm_fwd(x, w, *, name):
    n, d = x.shape
    tm = _pick(n, (512, 256, 128))

    def body(x_ref, w_ref, y_ref, r_ref):
        xv = x_ref[...]
        rstd = lax.rsqrt(jnp.mean(xv * xv, axis=1, keepdims=True) + EPS)
        y_ref[...] = (xv * rstd * w_ref[...]).astype(BF16)
        r_ref[...] = rstd

    return pl.pallas_call(
        body, name=name, grid=(n // tm,),
        in_specs=[pl.BlockSpec((tm, d), lambda i: (i, 0)), pl.BlockSpec((1, d), lambda i: (0, 0))],
        out_specs=[pl.BlockSpec((tm, d), lambda i: (i, 0)), pl.BlockSpec((tm, 1), lambda i: (i, 0))],
        out_shape=[jax.ShapeDtypeStruct((n, d), BF16), jax.ShapeDtypeStruct((n, 1), F32)],
        compiler_params=_cparams(("parallel",)),
    )(x, w)


def _rows_product(a_ref, b_ref, tb, a_act):
    av = a_ref[...]
    if a_act == "relu2":
        r = jnp.maximum(av.astype(F32), 0.0)
        av = r * r
    dn = (((1,), (1 if tb else 0,)), ((), ()))
    return lax.dot_general(av.astype(BF16), b_ref[...].astype(BF16), dn, preferred_element_type=F32)


def _mm_norm_fwd(a, b, res, w, *, name, tm):
    m, k = a.shape
    d = b.shape[1]

    def body(a_ref, b_ref, res_ref, w_ref, h_ref, y_ref, r_ref):
        hv = _rows_product(a_ref, b_ref, False, None) + res_ref[...]
        rstd = lax.rsqrt(jnp.mean(hv * hv, axis=1, keepdims=True) + EPS)
        h_ref[...] = hv
        y_ref[...] = (hv * rstd * w_ref[...]).astype(BF16)
        r_ref[...] = rstd

    row = pl.BlockSpec((tm, d), lambda i: (i, 0))
    return pl.pallas_call(
        body, name=name, grid=(m // tm,),
        in_specs=[pl.BlockSpec((tm, k), lambda i: (i, 0)), pl.BlockSpec((k, d), lambda i: (0, 0)), row,
                  pl.BlockSpec((1, d), lambda i: (0, 0))],
        out_specs=[row, row, pl.BlockSpec((tm, 1), lambda i: (i, 0))],
        out_shape=[jax.ShapeDtypeStruct((m, d), F32), jax.ShapeDtypeStruct((m, d), BF16),
                   jax.ShapeDtypeStruct((m, 1), F32)],
        compiler_params=_cparams(("parallel",)),
    )(a, b, res, w)


def _mm_final(a, b, res, w, target, *, name, tm, a_act):
    m, k = a.shape
    d = b.shape[1]

    def body(a_ref, b_ref, res_ref, w_ref, t_ref, dh_ref, dhb_ref, loss_ref, dw_ref):
        @pl.when(pl.program_id(0) == 0)
        def _():
            loss_ref[...] = jnp.zeros_like(loss_ref)
            dw_ref[...] = jnp.zeros_like(dw_ref)

        hv = _rows_product(a_ref, b_ref, False, a_act) + res_ref[...]
        wv = w_ref[...]
        rstd = lax.rsqrt(jnp.mean(hv * hv, axis=1, keepdims=True) + EPS)
        xhat = hv * rstd
        err = xhat * wv - t_ref[...]
        loss_ref[...] += 0.5 * jnp.sum(jnp.mean(err * err, axis=1, keepdims=True), axis=0, keepdims=True)
        dy = err * (1.0 / d)
        gw = dy * wv
        dh = rstd * (gw - xhat * jnp.mean(gw * xhat, axis=1, keepdims=True))
        dh_ref[...] = dh
        dhb_ref[...] = dh.astype(BF16)
        dw_ref[...] += jnp.sum(dy * xhat, axis=0, keepdims=True)

    row = pl.BlockSpec((tm, d), lambda i: (i, 0))
    vec = pl.BlockSpec((1, d), lambda i: (0, 0))
    return pl.pallas_call(
        body, name=name, grid=(m // tm,),
        in_specs=[pl.BlockSpec((tm, k), lambda i: (i, 0)), pl.BlockSpec((k, d), lambda i: (0, 0)), row, vec, row],
        out_specs=[row, row, pl.BlockSpec((1, 128), lambda i: (0, 0)), vec],
        out_shape=[jax.ShapeDtypeStruct((m, d), F32), jax.ShapeDtypeStruct((m, d), BF16),
                   jax.ShapeDtypeStruct((1, 128), F32), jax.ShapeDtypeStruct((1, d), F32)],
        compiler_params=_cparams(("arbitrary",)),
    )(a, b, res, w, target)


def _mm_norm_bwd(a, b, res, x, rstd, w, dres, *, name, tm):
    m, k = a.shape
    d = b.shape[0]
    has_res = res is not None

    def body(*refs):
        a_ref, b_ref = refs[0], refs[1]
        i = 2
        res_ref = None
        if has_res:
            res_ref = refs[i]
            i += 1
        x_ref, r_ref, w_ref, d_ref, dx_ref, dxb_ref, dw_ref = refs[i:i + 7]

        @pl.when(pl.program_id(0) == 0)
        def _():
            dw_ref[...] = jnp.zeros_like(dw_ref)

        g = _rows_product(a_ref, b_ref, True, None)
        if has_res:
            g = g + res_ref[...]
        r = r_ref[...]
        xhat = x_ref[...] * r
        gw = g * w_ref[...]
        dx = d_ref[...] + r * (gw - xhat * jnp.mean(gw * xhat, axis=1, keepdims=True))
        dx_ref[...] = dx
        dxb_ref[...] = dx.astype(BF16)
        dw_ref[...] += jnp.sum(g * xhat, axis=0, keepdims=True)

    row = pl.BlockSpec((tm, d), lambda i: (i, 0))
    vec = pl.BlockSpec((1, d), lambda i: (0, 0))
    ins = [a, b] + ([res] if has_res else []) + [x, rstd, w, dres]
    specs = ([pl.BlockSpec((tm, k), lambda i: (i, 0)), pl.BlockSpec((d, k), lambda i: (0, 0))]
             + ([row] if has_res else []) + [row, pl.BlockSpec((tm, 1), lambda i: (i, 0)), vec, row])
    return pl.pallas_call(
        body, name=name, grid=(m // tm,), in_specs=specs, out_specs=[row, row, vec],
        out_shape=[jax.ShapeDtypeStruct((m, d), F32), jax.ShapeDtypeStruct((m, d), BF16),
                   jax.ShapeDtypeStruct((1, d), F32)],
        compiler_params=_cparams(("arbitrary",)),
    )(*ins)


def _softplus(x):
    return jnp.maximum(x, 0.0) + jnp.log(1.0 + jnp.exp(-jnp.abs(x)))


def _prep(proj_a, bias128, alog128, bl, t):
    n = bl * t
    nch = t // CHUNK
    col0 = (SSD_WIDTH + CONV_CH) // 128

    def body(p_ref, b_ref, al_ref, dt_ref, gd_ref, ac_ref, c_ref, carry):
        @pl.when(pl.program_id(1) == 0)
        def _():
            carry[...] = jnp.zeros_like(carry)

        xv = p_ref[...] + b_ref[...]
        sp = _softplus(xv)
        a = -jnp.exp(al_ref[...]) * sp
        logf = -_softplus(-xv)
        row = lax.broadcasted_iota(jnp.int32, (CHUNK, CHUNK), 0)
        col = lax.broadcasted_iota(jnp.int32, (CHUNK, CHUNK), 1)
        tril = (row >= col).astype(F32)
        acum = jnp.dot(tril, a, precision=HIGHEST, preferred_element_type=F32)
        c = jnp.dot(tril, logf, precision=HIGHEST, preferred_element_type=F32) + carry[...]
        carry[...] = c[CHUNK - 1:CHUNK, :]
        lane = lax.broadcasted_iota(jnp.int32, (1, 128), 1)
        head_lanes = lane < 16
        dt_ref[...] = jnp.where(head_lanes, sp, 0.0)
        gd_ref[...] = jnp.where(head_lanes, jax.nn.sigmoid(xv), jnp.where(lane < 32, jax.nn.sigmoid(-xv), 0.0))
        ac_ref[...] = jnp.where(head_lanes, acum, 0.0)
        c_ref[...] = c[:, 16:32]

    o16 = pl.BlockSpec((CHUNK, 16), lambda b, c: (b * nch + c, 0))
    o128 = pl.BlockSpec((CHUNK, 128), lambda b, c: (b * nch + c, 0))
    v128 = pl.BlockSpec((1, 128), lambda b, c: (0, 0))
    w128 = jax.ShapeDtypeStruct((n, 128), F32)
    return pl.pallas_call(
        body, name="head_scalars", grid=(bl, nch),
        in_specs=[pl.BlockSpec((CHUNK, 128), lambda b, c: (b * nch + c, col0)), v128, v128],
        out_specs=[o128, o128, o128, o16],
        out_shape=[w128, w128, w128, jax.ShapeDtypeStruct((n, 16), F32)],
        scratch_shapes=[pltpu.VMEM((1, 128), F32)],
        compiler_params=_cparams(("parallel", "arbitrary")),
    )(proj_a, bias128, alog128)


def _fpost(dc, gate_d, ddt, dpa, bl, t):
    n = bl * t
    nch = t // CHUNK
    col0 = (SSD_WIDTH + CONV_CH) // 128

    def body(dc_ref, gd_ref, ddt_ref, dpa_in, out_ref, db_ref, carry):
        @pl.when(pl.program_id(1) == 0)
        def _():
            carry[...] = jnp.zeros_like(carry)

        @pl.when((pl.program_id(0) == 0) & (pl.program_id(1) == 0))
        def _():
            db_ref[...] = jnp.zeros_like(db_ref)

        row = lax.broadcasted_iota(jnp.int32, (CHUNK, CHUNK), 0)
        col = lax.broadcasted_iota(jnp.int32, (CHUNK, CHUNK), 1)
        triu = (row <= col).astype(F32)
        dlf = jnp.dot(triu, dc_ref[...], precision=HIGHEST, preferred_element_type=F32) + carry[...]
        carry[...] = dlf[0:1, :]
        lane = lax.broadcasted_iota(jnp.int32, (1, 128), 1)
        df = jnp.where((lane >= 16) & (lane < 32), dlf * gd_ref[...], 0.0)
        out_ref[...] = (ddt_ref[...] + df).astype(BF16)
        db_ref[...] += jnp.sum(df, axis=0, keepdims=True)[:, 16:32]

    rev = lambda b, c: (b * nch + nch - 1 - c, 0)
    blk = pl.BlockSpec((CHUNK, 128), rev)
    return pl.pallas_call(
        body, name="forget_gate_bwd", grid=(bl, nch),
        in_specs=[blk, blk, blk, ANY],
        out_specs=[pl.BlockSpec((CHUNK, 128), lambda b, c: (b * nch + nch - 1 - c, col0)),
                   pl.BlockSpec((1, 16), lambda b, c: (0, 0))],
        out_shape=[jax.ShapeDtypeStruct(dpa.shape, dpa.dtype), jax.ShapeDtypeStruct((1, 16), F32)],
        input_output_aliases={3: 0},
        scratch_shapes=[pltpu.VMEM((1, 128), F32)],
        compiler_params=_cparams(("arbitrary", "arbitrary")),
    )(dc, gate_d, ddt, dpa)


CONV_TILE = 256
CONV_ROWS = 256


def _conv_taps(u_ref, i, w, bias):
    r0 = pl.multiple_of(i * CONV_ROWS, CONV_ROWS)
    cur = u_ref[pl.ds(r0, CONV_ROWS), :]
    p0 = pl.multiple_of(jnp.maximum(r0 - 8, 0), 8)
    prev = jnp.where(i > 0, u_ref[pl.ds(p0, 8), :], 0.0)
    cat = jnp.concatenate([prev, cur], axis=0)
    pre = bias + w[3:4, :] * cur
    taps = [cur]
    for s in (1, 2, 3):
        sh = pltpu.roll(cat, s, 0)[8:, :]
        taps.append(sh)
        pre = pre + w[3 - s:4 - s, :] * sh
    return r0, pre, taps


def _conv_fwd(proj_a, conv_w, conv_b, bl, t):
    n = bl * t
    nct = CONV_CH // CONV_TILE
    c0 = SSD_WIDTH // CONV_TILE

    def body(u_ref, w_ref, b_ref, o_ref):
        w = w_ref[...]
        bias = b_ref[...]

        def chunk(i, carry):
            r0, pre, _ = _conv_taps(u_ref, i, w, bias)
            o_ref[pl.ds(r0, CONV_ROWS), :] = pre * jax.nn.sigmoid(pre)
            return carry

        lax.fori_loop(0, t // CONV_ROWS, chunk, 0)

    return pl.pallas_call(
        body, name="conv_silu_fwd", grid=(bl, nct),
        in_specs=[pl.BlockSpec((t, CONV_TILE), lambda b, c: (b, c0 + c)),
                  pl.BlockSpec((4, CONV_TILE), lambda b, c: (0, c)),
                  pl.BlockSpec((1, CONV_TILE), lambda b, c: (0, c))],
        out_specs=pl.BlockSpec((t, CONV_TILE), lambda b, c: (b, c)),
        out_shape=jax.ShapeDtypeStruct((n, CONV_CH), F32),
        compiler_params=_cparams(("parallel", "parallel")),
    )(proj_a, conv_w, conv_b)


def _conv_bwd(dxc, proj_a, conv_w, conv_b, dpa, bl, t):
    nct = CONV_CH // CONV_TILE
    c0 = SSD_WIDTH // CONV_TILE
    nrc = t // CONV_ROWS

    def body(g_ref, u_ref, w_ref, b_ref, dpa_in, du_ref, dw_ref, db_ref, dp_scr):
        @pl.when(pl.program_id(1) == 0)
        def _():
            dw_ref[...] = jnp.zeros_like(dw_ref)
            db_ref[...] = jnp.zeros_like(db_ref)

        w = w_ref[...]
        bias = b_ref[...]
        dp_scr[pl.ds(t, 8), :] = jnp.zeros((8, CONV_TILE), F32)

        def chunk1(i, carry):
            dw0, dw1, dw2, dw3, db = carry
            r0, pre, taps = _conv_taps(u_ref, i, w, bias)
            sg = jax.nn.sigmoid(pre)
            dpre = g_ref[pl.ds(r0, CONV_ROWS), :] * (sg * (1.0 + pre * (1.0 - sg)))
            dp_scr[pl.ds(r0, CONV_ROWS), :] = dpre
            dw3 = dw3 + jnp.sum(dpre * taps[0], axis=0, keepdims=True)
            dw2 = dw2 + jnp.sum(dpre * taps[1], axis=0, keepdims=True)
            dw1 = dw1 + jnp.sum(dpre * taps[2], axis=0, keepdims=True)
            dw0 = dw0 + jnp.sum(dpre * taps[3], axis=0, keepdims=True)
            db = db + jnp.sum(dpre, axis=0, keepdims=True)
            return dw0, dw1, dw2, dw3, db

        z = jnp.zeros((1, CONV_TILE), F32)
        dw0, dw1, dw2, dw3, db = lax.fori_loop(0, nrc, chunk1, (z, z, z, z, z))
        dw_ref[...] += jnp.concatenate([dw0, dw1, dw2, dw3], axis=0)
        db_ref[...] += db

        def chunk2(i, carry):
            r0 = pl.multiple_of(i * CONV_ROWS, CONV_ROWS)
            cat = dp_scr[pl.ds(r0, CONV_ROWS + 8), :]
            du = w[3:4, :] * cat[:CONV_ROWS, :]
            for s in (1, 2, 3):
                du = du + w[3 - s:4 - s, :] * pltpu.roll(cat, CONV_ROWS + 8 - s, 0)[:CONV_ROWS, :]
            du_ref[pl.ds(r0, CONV_ROWS), :] = du.astype(BF16)
            return carry

        lax.fori_loop(0, nrc, chunk2, 0)

    return pl.pallas_call(
        body, name="conv_silu_bwd", grid=(nct, bl),
        in_specs=[pl.BlockSpec((t, CONV_TILE), lambda c, b: (b, c)),
                  pl.BlockSpec((t, CONV_TILE), lambda c, b: (b, c0 + c)),
                  pl.BlockSpec((4, CONV_TILE), lambda c, b: (0, c)),
                  pl.BlockSpec((1, CONV_TILE), lambda c, b: (0, c)), ANY],
        out_specs=[pl.BlockSpec((t, CONV_TILE), lambda c, b: (b, c0 + c)),
                   pl.BlockSpec((4, CONV_TILE), lambda c, b: (0, c)),
                   pl.BlockSpec((1, CONV_TILE), lambda c, b: (0, c))],
        out_shape=[jax.ShapeDtypeStruct(dpa.shape, dpa.dtype), jax.ShapeDtypeStruct((4, CONV_CH), F32),
                   jax.ShapeDtypeStruct((1, CONV_CH), F32)],
        input_output_aliases={4: 0},
        scratch_shapes=[pltpu.VMEM((t + 8, CONV_TILE), F32)],
        compiler_params=_cparams(("parallel", "arbitrary")),
    )(dxc, proj_a, conv_w, conv_b, dpa)


NT_DIMS = (((1,), (1,)), ((), ()))
TN_DIMS = (((0,), (0,)), ((), ()))


def _dot(a, b, dims=None):
    if dims is None:
        return jnp.dot(a, b, preferred_element_type=F32)
    return lax.dot_general(a, b, dims, preferred_element_type=F32)


def _head_expander():
    r = lax.broadcasted_iota(jnp.int32, (128, SSD_WIDTH), 0)
    c = lax.broadcasted_iota(jnp.int32, (128, SSD_WIDTH), 1)
    return ((c // HEAD_DIM == r % 16) & (r < 48)).astype(BF16)


def _spread(v128, expander):
    hi = v128.astype(BF16).astype(F32)
    r1 = v128 - hi
    mid = r1.astype(BF16).astype(F32)
    lo = (r1 - mid).astype(BF16).astype(F32)
    packed = (hi + pltpu.roll(mid, 16, 1) + pltpu.roll(lo, 32, 1)).astype(BF16)
    return jnp.dot(packed, expander, preferred_element_type=F32)


def _head_sums(v1024, expander):
    hi = v1024.astype(BF16)
    lo = (v1024 - hi.astype(F32)).astype(BF16)
    heads = jnp.where(lax.broadcasted_iota(jnp.int32, (128, SSD_WIDTH), 0) < 16, expander, jnp.zeros_like(expander))
    return _dot(hi, heads, NT_DIMS) + _dot(lo, heads, NT_DIMS)


def _ssd_fwd(xc, proj_a, dt, acum, acum_t, dskip_e, norm_w, bl, t):
    n = bl * t
    nch = t // CHUNK
    L = CHUNK

    def body(xc_ref, z_ref, dt_ref, ac_ref, act_ref, dsk_ref, nw_ref, ys_ref, yp_ref, hp_ref, h_scr, y_scr, x_scr):
        @pl.when(pl.program_id(1) == 0)
        def _():
            h_scr[...] = jnp.zeros_like(h_scr)

        row = lax.broadcasted_iota(jnp.int32, (L, L), 0)
        col = lax.broadcasted_iota(jnp.int32, (L, L), 1)
        causal = row >= col
        expander = _head_expander()
        ac_all = ac_ref[...]
        act_all = act_ref[...]
        ac_e = _spread(ac_all, expander)
        e_in = jnp.exp(ac_e)
        dec = jnp.exp(ac_e[L - 1:L, :] - ac_e)
        xs_all = xc_ref[:, 0:SSD_WIDTH]
        x_all = xs_all * _spread(dt_ref[...], expander)
        x_scr[...] = x_all.astype(BF16)
        hp_all = h_scr[...]
        hp_ref[...] = hp_all
        for g in range(2):
            gs = slice(g * 512, (g + 1) * 512)
            bg = xc_ref[:, SSD_WIDTH + g * 128:SSD_WIDTH + (g + 1) * 128].astype(BF16)
            cg = xc_ref[:, SSD_WIDTH + 256 + g * 128:SSD_WIDTH + 256 + (g + 1) * 128].astype(BF16)
            gmat = _dot(cg, bg, NT_DIMS)
            y_scr[:, gs] = (_dot(cg, hp_all[gs, :].astype(BF16), NT_DIMS) * e_in[:, gs]
                            + dsk_ref[:, gs] * xs_all[:, gs])
            s_new = _dot((x_all[:, gs] * dec[:, gs]).astype(BF16), bg, TN_DIMS)
            for r in range(8):
                h = g * 8 + r
                sl = slice(h * HEAD_DIM, (h + 1) * HEAD_DIM)
                ldec = jnp.exp(jnp.where(causal, ac_all[:, h:h + 1] - act_all[h:h + 1, :], NEG))
                y_scr[:, sl] += _dot((gmat * ldec).astype(BF16), x_scr[:, sl])
                elast = jnp.exp(ac_all[L - 1:L, h:h + 1])
                h_scr[sl, :] = elast * hp_all[sl, :] + s_new[r * HEAD_DIM:(r + 1) * HEAD_DIM, :]
        y = y_scr[...]
        yp_ref[...] = y
        zv = z_ref[...]
        yg = y * (zv * jax.nn.sigmoid(zv))
        for g in range(2):
            gs = slice(g * 512, (g + 1) * 512)
            grp = yg[:, gs]
            rstd = lax.rsqrt(jnp.mean(grp * grp, axis=1, keepdims=True) + EPS)
            ys_ref[:, gs] = (grp * rstd * nw_ref[:, gs]).astype(BF16)

    rb = lambda b, c: (b * nch + c, 0)
    v1k = pl.BlockSpec((1, SSD_WIDTH), lambda b, c: (0, 0))
    return pl.pallas_call(
        body, name="ssd_fwd", grid=(bl, nch),
        in_specs=[pl.BlockSpec((L, CONV_CH), rb), pl.BlockSpec((L, SSD_WIDTH), rb),
                  pl.BlockSpec((L, 128), rb), pl.BlockSpec((L, 128), rb),
                  pl.BlockSpec((16, L), lambda b, c: (0, b * nch + c)), v1k, v1k],
        out_specs=[pl.BlockSpec((L, SSD_WIDTH), rb), pl.BlockSpec((L, SSD_WIDTH), rb),
                   pl.BlockSpec((None, SSD_WIDTH, SSD_STATE), lambda b, c: (b * nch + c, 0, 0))],
        out_shape=[jax.ShapeDtypeStruct((n, SSD_WIDTH), BF16), jax.ShapeDtypeStruct((n, SSD_WIDTH), F32),
                   jax.ShapeDtypeStruct((bl * nch, SSD_WIDTH, SSD_STATE), F32)],
        scratch_shapes=[pltpu.VMEM((SSD_WIDTH, SSD_STATE), F32), pltpu.VMEM((L, SSD_WIDTH), F32),
                        pltpu.VMEM((L, SSD_WIDTH), BF16)],
        compiler_params=_cparams(("parallel", "arbitrary")),
    )(xc, proj_a, dt, acum, acum_t, dskip_e, norm_w)


def _ssd_bwd(dys, xc, proj_a, ypre, hprev, dt, gate_d, acum, acum_t, alog128, dskip_e, norm_w, bl, t):
    n = bl * t
    nch = t // CHUNK
    L = CHUNK

    def body(dys_ref, xc_ref, z_ref, yp_ref, hp_ref, dt_ref, gd_ref, ac_ref, act_ref, al_ref, dsk_ref, nw_ref,
             dxc_ref, dz_ref, ddt_ref, dnw_ref, dsk16_ref, da16_ref, db16_ref,
             dh_scr, dy_scr, x_scr, dx_scr, red_scr):
        first = (pl.program_id(0) == 0) & (pl.program_id(1) == 0)

        @pl.when(first)
        def _():
            dnw_ref[...] = jnp.zeros_like(dnw_ref)
            dsk16_ref[...] = jnp.zeros_like(dsk16_ref)
            da16_ref[...] = jnp.zeros_like(da16_ref)
            db16_ref[...] = jnp.zeros_like(db16_ref)

        @pl.when(pl.program_id(1) == 0)
        def _():
            dh_scr[...] = jnp.zeros_like(dh_scr)

        y = yp_ref[...]
        zv = z_ref[...]
        sz = jax.nn.sigmoid(zv)
        gate = zv * sz
        yg = y * gate
        dout = dys_ref[...]
        nw = nw_ref[...]
        for g in range(2):
            gs = slice(g * 512, (g + 1) * 512)
            grp = yg[:, gs]
            rstd = lax.rsqrt(jnp.mean(grp * grp, axis=1, keepdims=True) + EPS)
            ghat = grp * rstd
            dnw_ref[:, gs] += jnp.sum(dout[:, gs] * ghat, axis=0, keepdims=True)
            gw = dout[:, gs] * nw[:, gs]
            dyg = rstd * (gw - ghat * jnp.mean(gw * ghat, axis=1, keepdims=True))
            dy_scr[:, gs] = dyg * gate[:, gs]
            dz_ref[:, gs] = (dyg * y[:, gs] * (sz[:, gs] * (1.0 + zv[:, gs] * (1.0 - sz[:, gs])))).astype(BF16)

        row = lax.broadcasted_iota(jnp.int32, (L, L), 0)
        col = lax.broadcasted_iota(jnp.int32, (L, L), 1)
        causal = row >= col
        lane128 = lax.broadcasted_iota(jnp.int32, (1, L), 1)
        rows128 = lax.broadcasted_iota(jnp.int32, (L, 1), 0)
        last_row = rows128 == (L - 1)
        expander = _head_expander()
        ac_all = ac_ref[...]
        act_all = act_ref[...]
        dt_all = dt_ref[...]
        dt_e = _spread(dt_all, expander)
        ac_e = _spread(ac_all, expander)
        e_in = jnp.exp(ac_e)
        dec = jnp.exp(ac_e[L - 1:L, :] - ac_e)
        xs_all = xc_ref[:, 0:SSD_WIDTH]
        x_all = xs_all * dt_e
        x_scr[...] = x_all.astype(BF16)
        dy_all = dy_scr[...]
        hp_all = hp_ref[...]
        ds_all = dh_scr[...]
        dsk_cols = jnp.sum(dy_all * xs_all, axis=0, keepdims=True)
        dac = jnp.zeros((L, L), F32)
        dac_row = jnp.zeros((L, L), F32)
        ddec_cols = []
        for g in range(2):
            gs = slice(g * 512, (g + 1) * 512)
            bsl = slice(SSD_WIDTH + g * 128, SSD_WIDTH + (g + 1) * 128)
            csl = slice(SSD_WIDTH + 256 + g * 128, SSD_WIDTH + 256 + (g + 1) * 128)
            bg = xc_ref[:, bsl].astype(BF16)
            cg = xc_ref[:, csl].astype(BF16)
            gmat = _dot(cg, bg, NT_DIMS)
            hpb = hp_all[gs, :].astype(BF16)
            dsb = ds_all[gs, :].astype(BF16)
            ch = _dot(cg, hpb, NT_DIMS)
            dye = dy_all[:, gs] * e_in[:, gs]
            dyeb = dye.astype(BF16)
            dc_acc = _dot(dyeb, hpb)
            dhp = _dot(dyeb, cg, TN_DIMS)
            dxd = _dot(bg, dsb, NT_DIMS)
            db_acc = _dot((x_all[:, gs] * dec[:, gs]).astype(BF16), dsb)
            ddec = dxd * x_all[:, gs] * dec[:, gs]
            ddec_cols.append(jnp.sum(ddec, axis=0, keepdims=True))
            dx_scr[:, gs] = dxd * dec[:, gs]
            red_scr[:, gs] = dye * ch - ddec
            dg_sum = jnp.zeros((L, L), F32)
            for r in range(8):
                h = g * 8 + r
                sl = slice(h * HEAD_DIM, (h + 1) * HEAD_DIM)
                onehot_w = lane128 == h
                ldec = jnp.exp(jnp.where(causal, ac_all[:, h:h + 1] - act_all[h:h + 1, :], NEG))
                mf = gmat * ldec
                dyb = dy_scr[:, sl].astype(BF16)
                dm = _dot(dyb, x_scr[:, sl], NT_DIMS)
                dx_scr[:, sl] += _dot(mf.astype(BF16), dyb, TN_DIMS)
                dg_sum = dg_sum + dm * ldec
                wmat = dm * mf
                elast = jnp.exp(ac_all[L - 1:L, h:h + 1])
                hp_h = hp_all[sl, :]
                ds_h = ds_all[sl, :]
                extra = elast * jnp.sum(jnp.sum(hp_h * ds_h, axis=1, keepdims=True), axis=0, keepdims=True)
                dac = dac + jnp.where(onehot_w, jnp.sum(wmat, axis=1, keepdims=True) + jnp.where(last_row, extra, 0.0),
                                      0.0)
                dac_row = dac_row + jnp.where(rows128 == h, -jnp.sum(wmat, axis=0, keepdims=True), 0.0)
                dh_scr[sl, :] = elast * ds_h + dhp[r * HEAD_DIM:(r + 1) * HEAD_DIM, :]
            dgb = dg_sum.astype(BF16)
            dxc_ref[:, csl] = dc_acc + _dot(dgb, bg)
            dxc_ref[:, bsl] = db_acc + _dot(dgb, cg, TN_DIMS)
        dx_all = dx_scr[...]
        dxc_ref[:, 0:SSD_WIDTH] = dx_all * dt_e + dsk_ref[...] * dy_all
        red = red_scr[...]
        dac_slab = _head_sums(red, expander)
        ddec_tot = _head_sums(jnp.broadcast_to(jnp.concatenate(ddec_cols, axis=1), (8, SSD_WIDTH)), expander)
        ddt_x = _head_sums(dx_all * xs_all, expander)
        dsk16_ref[...] += _head_sums(jnp.broadcast_to(dsk_cols, (8, SSD_WIDTH)), expander)[0:1, 0:16]
        dac = dac + dac_slab + jnp.transpose(dac_row) + jnp.where(last_row, ddec_tot[0:1, :], 0.0)
        triu = (row <= col).astype(F32)
        da = jnp.dot(triu, dac, precision=HIGHEST, preferred_element_type=F32)
        a_row = -jnp.exp(al_ref[...])
        ddt = jnp.where(lane128 < 16, (ddt_x + da * a_row) * gd_ref[...], 0.0)
        ddt_ref[...] = ddt
        da16_ref[...] += (jnp.sum(da * dt_all, axis=0, keepdims=True) * a_row)[:, 0:16]
        db16_ref[...] += jnp.sum(ddt, axis=0, keepdims=True)[:, 0:16]

    rb = lambda b, c: (b * nch + nch - 1 - c, 0)
    v1k = pl.BlockSpec((1, SSD_WIDTH), lambda b, c: (0, 0))
    v16 = pl.BlockSpec((1, 16), lambda b, c: (0, 0))
    v128 = pl.BlockSpec((1, 128), lambda b, c: (0, 0))
    wide = pl.BlockSpec((L, SSD_WIDTH), rb)
    s128 = pl.BlockSpec((L, 128), rb)
    return pl.pallas_call(
        body, name="ssd_bwd", grid=(bl, nch),
        in_specs=[wide, pl.BlockSpec((L, CONV_CH), rb), wide, wide,
                  pl.BlockSpec((None, SSD_WIDTH, SSD_STATE), lambda b, c: (b * nch + nch - 1 - c, 0, 0)),
                  s128, s128, s128, pl.BlockSpec((16, L), lambda b, c: (0, b * nch + nch - 1 - c)), v128, v1k, v1k],
        out_specs=[pl.BlockSpec((L, CONV_CH), rb), wide, s128, v1k, v16, v16, v16],
        out_shape=[jax.ShapeDtypeStruct((n, CONV_CH), F32), jax.ShapeDtypeStruct((n, PA_WIDTH), BF16),
                   jax.ShapeDtypeStruct((n, 128), F32), jax.ShapeDtypeStruct((1, SSD_WIDTH), F32),
                   jax.ShapeDtypeStruct((1, 16), F32), jax.ShapeDtypeStruct((1, 16), F32),
                   jax.ShapeDtypeStruct((1, 16), F32)],
        scratch_shapes=[pltpu.VMEM((SSD_WIDTH, SSD_STATE), F32), pltpu.VMEM((L, SSD_WIDTH), F32),
                        pltpu.VMEM((L, SSD_WIDTH), BF16), pltpu.VMEM((L, SSD_WIDTH), F32),
                        pltpu.VMEM((L, SSD_WIDTH), F32)],
        compiler_params=_cparams(("arbitrary", "arbitrary")),
    )(dys, xc, proj_a, ypre, hprev, dt, gate_d, acum, acum_t, alog128, dskip_e, norm_w)


def _attn_fwd(qkv, negc, bl, t):
    n = bl * t
    tb_ = ATT_BLOCK
    nb = t // tb_
    scale2 = LOG2E / math.sqrt(HEAD_DIM)

    def body(q_ref, k_ref, v_ref, c_ref, o_ref, lse_ref):
        row = lax.broadcasted_iota(jnp.int32, (tb_, tb_), 0)
        col = lax.broadcasted_iota(jnp.int32, (tb_, tb_), 1)
        causal = row >= col
        for qi in range(nb):
            r0, lk = qi * tb_, (qi + 1) * tb_
            for j in range(2):
                sl = slice(j * HEAD_DIM, (j + 1) * HEAD_DIM)
                s = _dot(q_ref[r0:lk, sl], k_ref[0:lk, sl], NT_DIMS) * scale2 + c_ref[j:j + 1, 0:lk] * LOG2E
                tail = jnp.where(causal, s[:, r0:lk], NEG)
                s = tail if qi == 0 else jnp.concatenate([s[:, 0:r0], tail], axis=1)
                m = jnp.max(s, axis=1, keepdims=True)
                p = jnp.exp2(s - m)
                l = jnp.sum(p, axis=1, keepdims=True)
                acc = _dot(p.astype(BF16), v_ref[0:lk, sl])
                o_ref[r0:lk, sl] = (acc / l).astype(BF16)
                lse_ref[r0:lk, sl] = jnp.broadcast_to(m + jnp.log(l) * LOG2E, (tb_, HEAD_DIM))

    blk = lambda off: pl.BlockSpec((t, 128), lambda b, hp: (b, off + hp))
    return pl.pallas_call(
        body, name="fox_attn_fwd", grid=(bl, 8),
        in_specs=[blk(0), blk(8), blk(16), pl.BlockSpec((None, None, 8, t), lambda b, hp: (b, hp, 0, 0))],
        out_specs=[blk(0), blk(0)],
        out_shape=[jax.ShapeDtypeStruct((n, ATT_WIDTH), BF16), jax.ShapeDtypeStruct((n, ATT_WIDTH), F32)],
        compiler_params=_cparams(("parallel", "parallel")),
    )(qkv, qkv, qkv, negc)


def _attn_bwd(qkv, do, o, lse, negc, after, bl, t):
    n = bl * t
    tb_ = ATT_BLOCK
    nb = t // tb_
    scale = 1.0 / math.sqrt(HEAD_DIM)
    scale2 = LOG2E * scale

    def body(q_ref, k_ref, v_ref, do_ref, o_ref, lse_ref, c_ref, after_ref, dq_ref, dk_ref, dv_ref, dc_ref,
             dq_scr, delta_scr, dr_scr, qt_scr, dot_scr, dkt_scr, dvt_scr):
        row = lax.broadcasted_iota(jnp.int32, (tb_, tb_), 0)
        col = lax.broadcasted_iota(jnp.int32, (tb_, tb_), 1)
        causal = row >= col
        dq_scr[...] = jnp.zeros_like(dq_scr)
        dr_scr[...] = jnp.zeros_like(dr_scr)
        dc_ref[...] = jnp.zeros_like(dc_ref)
        qt_scr[...] = jnp.transpose(q_ref[...].astype(F32)).astype(BF16)
        dot_scr[...] = jnp.transpose(do_ref[...].astype(F32)).astype(BF16)
        prod = do_ref[...].astype(F32) * o_ref[...].astype(F32)
        for j in range(2):
            sl = slice(j * HEAD_DIM, (j + 1) * HEAD_DIM)
            delta_scr[:, sl] = jnp.broadcast_to(jnp.sum(prod[:, sl], axis=1, keepdims=True), (t, HEAD_DIM))
        for kj in range(nb):
            r0, r1 = kj * tb_, (kj + 1) * tb_
            for j in range(2):
                sl = slice(j * HEAD_DIM, (j + 1) * HEAD_DIM)
                one = slice(j * HEAD_DIM, j * HEAD_DIM + 1)
                kb = k_ref[r0:r1, sl]
                qs = q_ref[r0:t, sl]
                dos = do_ref[r0:t, sl]
                s = _dot(qs, kb, NT_DIMS) * scale2 + c_ref[j:j + 1, r0:r1] * LOG2E
                head = jnp.where(causal, s[0:tb_, :], NEG)
                s = head if kj == nb - 1 else jnp.concatenate([head, s[tb_:, :]], axis=0)
                p = jnp.exp2(s - lse_ref[r0:t, one])
                dp = _dot(dos, v_ref[r0:r1, sl], NT_DIMS)
                ds = p * (dp - delta_scr[r0:t, one])
                dsb = ds.astype(BF16)
                dvt_scr[sl, r0:r1] = _dot(dot_scr[sl, r0:t], p.astype(BF16))
                dkt_scr[sl, r0:r1] = _dot(qt_scr[sl, r0:t], dsb)
                dq_scr[r0:t, sl] += _dot(dsb, kb)
                dr_scr[r0:t, sl] += jnp.broadcast_to(jnp.sum(ds, axis=1, keepdims=True), (t - r0, HEAD_DIM))
                dc_ref[j:j + 1, r0:r1] = -jnp.sum(ds, axis=0, keepdims=True)
        dq_ref[...] = (dq_scr[...] * scale).astype(BF16)
        dk_ref[...] = (jnp.transpose(dkt_scr[...]) * scale).astype(BF16)
        dv_ref[...] = jnp.transpose(dvt_scr[...]).astype(BF16)
        dr_t = jnp.transpose(dr_scr[...])
        for j in range(2):
            dc_ref[j:j + 1, :] += dr_t[j * HEAD_DIM:j * HEAD_DIM + 1, :]

    blk = lambda off: pl.BlockSpec((t, 128), lambda b, hp: (b, off + hp))
    cblk = pl.BlockSpec((None, None, 8, t), lambda b, hp: (b, hp, 0, 0))
    return pl.pallas_call(
        body, name="fox_attn_bwd", grid=(bl, 8),
        in_specs=[blk(0), blk(8), blk(16), blk(0), blk(0), blk(0), cblk, ANY],
        out_specs=[blk(0), blk(0), blk(0), cblk],
        out_shape=[jax.ShapeDtypeStruct((n, ATT_WIDTH), BF16)] * 3 + [jax.ShapeDtypeStruct((bl, 8, 8, t), F32)],
        scratch_shapes=[pltpu.VMEM((t, 128), F32), pltpu.VMEM((t, 128), F32), pltpu.VMEM((t, 128), F32),
                        pltpu.VMEM((128, t), BF16), pltpu.VMEM((128, t), BF16),
                        pltpu.VMEM((128, t), F32), pltpu.VMEM((128, t), F32)],
        compiler_params=_cparams(("parallel", "parallel")),
    )(qkv, qkv, qkv, do, o, lse, negc, after)


def _adamw(w, g, m, v, *, name):
    lead = w.ndim == 3
    r, c = w.shape[-2:]
    tr = _pick(r, (256, IN_SHARD // 3, 128, 64, 32, 16, 8))
    bc1 = 1.0 - ADAM_B1 ** ADAM_STEP
    bc2 = 1.0 - ADAM_B2 ** ADAM_STEP

    def body(w_ref, g_ref, m_ref, v_ref, d_ref, nm_ref, nv_ref):
        gv = g_ref[...]
        mn = ADAM_B1 * m_ref[...] + (1.0 - ADAM_B1) * gv
        vn = ADAM_B2 * v_ref[...] + (1.0 - ADAM_B2) * (gv * gv)
        m_hat = mn / bc1
        v_hat = vn / bc2
        d_ref[...] = -ADAM_LR * (m_hat / (jnp.sqrt(v_hat) + ADAM_EPS) + ADAM_WD * w_ref[...])
        nm_ref[...] = mn
        nv_ref[...] = vn

    flat = pl.BlockSpec((tr, c), lambda i: (i, 0))
    blk = pl.BlockSpec((None, tr, c), lambda i: (0, i, 0)) if lead else flat
    return pl.pallas_call(
        body, name=name, grid=(r // tr,), in_specs=[blk, flat, blk, blk], out_specs=[blk] * 3,
        out_shape=[jax.ShapeDtypeStruct(w.shape, F32)] * 3,
        compiler_params=_cparams(("parallel",)),
    )(w, g, m, v)


def _sum_leading(parts, *, name, out_dtype=F32):
    k, r, c = parts.shape
    tr = _pick(r, (512, 256, 128, 96, 64, 32, 16, 8))

    def body(p_ref, o_ref):
        acc = p_ref[0].astype(F32)
        for i in range(1, k):
            acc = acc + p_ref[i].astype(F32)
        o_ref[...] = acc.astype(out_dtype)

    return pl.pallas_call(
        body, name=name, grid=(r // tr,),
        in_specs=[pl.BlockSpec((k, tr, c), lambda i: (0, i, 0))],
        out_specs=pl.BlockSpec((tr, c), lambda i: (i, 0)),
        out_shape=jax.ShapeDtypeStruct((r, c), out_dtype),
        compiler_params=_cparams(("parallel",)),
    )(parts)


def _add_pair(a, b, *, name):
    k, r, c = a.shape
    tr = _pick(r, (512, 256, 128))

    def body(a_ref, b_ref, o_ref):
        o_ref[...] = (a_ref[...].astype(F32) + b_ref[...].astype(F32)).astype(BF16)

    blk = pl.BlockSpec((None, tr, c), lambda j, i: (j, i, 0))
    return pl.pallas_call(
        body, name=name, grid=(k, r // tr), in_specs=[blk, blk], out_specs=blk,
        out_shape=jax.ShapeDtypeStruct((k, r, c), BF16),
        compiler_params=_cparams(("parallel", "parallel")),
    )(a, b)


ANY = pl.BlockSpec(memory_space=pl.ANY)


def _chip_peers(x, y):
    return [(1 - x, y, 2 * (1 - x) + y), (x, 1 - y, 2 * x + 1 - y), (1 - x, 1 - y, 2 * (1 - x) + 1 - y)]


def _gather_weights(blob, *, name):
    rows, cols = blob.shape
    half_rows = rows // 2

    def body(b_ref, o_ref, send_sems, recv_sems):
        x, y, c = lax.axis_index("x"), lax.axis_index("y"), lax.axis_index("c")
        me = 2 * x + y
        sibling = (x, y, 1 - c)
        peers = _chip_peers(x, y)

        def half(chip, hc):
            return o_ref.at[chip, pl.ds(hc * half_rows, half_rows), :]

        def copy(k, src, chip, hc, to):
            return pltpu.make_async_remote_copy(src_ref=src, dst_ref=half(chip, hc), send_sem=send_sems.at[k],
                                                recv_sem=recv_sems.at[k], device_id=to, device_id_type=MESH)

        my_half = b_ref.at[pl.ds(c * half_rows, half_rows), :]
        first = [copy(k, my_half, me, c, (px, py, c)) for k, (px, py, _) in enumerate(peers)]
        for cp in first:
            cp.start()
        passed = [copy(3 + k, half(pc, c), pc, c, sibling) for k, (_, _, pc) in enumerate(peers)]
        for k, (px, py, pc) in enumerate(peers):
            copy(k, my_half, pc, c, (px, py, c)).wait_recv()
            passed[k].start()
        for k, (_, _, pc) in enumerate(peers):
            copy(3 + k, half(pc, 1 - c), pc, 1 - c, sibling).wait_recv()
        for cp in first + passed:
            cp.wait_send()

    return pl.pallas_call(
        body, name=name, in_specs=[ANY], out_specs=ANY,
        out_shape=jax.ShapeDtypeStruct((N_CHIPS, rows, cols), BF16),
        scratch_shapes=[pltpu.SemaphoreType.DMA((6,)), pltpu.SemaphoreType.DMA((6,))],
    )(blob)


def _swap_halves(g, *, name):
    _, rows, cols = g.shape
    half_rows = rows // 2

    def body(g_ref, o_ref, send_sem, recv_sem):
        x, y, c = lax.axis_index("x"), lax.axis_index("y"), lax.axis_index("c")
        cp = pltpu.make_async_remote_copy(
            src_ref=g_ref.at[:, pl.ds((1 - c) * half_rows, half_rows), :], dst_ref=o_ref,
            send_sem=send_sem, recv_sem=recv_sem, device_id=(x, y, 1 - c), device_id_type=MESH)
        cp.start()
        cp.wait()

    return pl.pallas_call(
        body, name=name, in_specs=[ANY], out_specs=ANY,
        out_shape=jax.ShapeDtypeStruct((N_CHIPS, half_rows, cols), BF16),
        scratch_shapes=[pltpu.SemaphoreType.DMA, pltpu.SemaphoreType.DMA],
    )(g)


HBM_SPEC = pl.BlockSpec(memory_space=pltpu.HBM)
SEM_SPEC = pl.BlockSpec(memory_space=pltpu.SEMAPHORE)
SPLIT_EFFECT = pltpu.SideEffectType.DATAFLOW_SIDE_EFFECTING


def _gather_peers_copies(b_ref, land_ref, send_sems, recv_sems, sending):
    x, y, c = lax.axis_index("x"), lax.axis_index("y"), lax.axis_index("c")
    me = 2 * x + y
    half_rows = b_ref.shape[0] // 2
    src = b_ref.at[pl.ds(c * half_rows, half_rows), :]
    return [pltpu.make_async_remote_copy(
        src_ref=src, dst_ref=land_ref.at[me if sending else pc, pl.ds(c * half_rows, half_rows), :],
        send_sem=send_sems.at[k], recv_sem=recv_sems.at[k], device_id=(px, py, c), device_id_type=MESH)
        for k, (px, py, pc) in enumerate(_chip_peers(x, y))]


def _gather_start(blob, after, *, name):
    shape = (N_CHIPS,) + blob.shape

    def body(b_ref, land_ref, after_ref, send_sems, recv_sems, b_thru, land_thru, token):
        for cp in _gather_peers_copies(b_ref, land_ref, send_sems, recv_sems, True):
            cp.start()
        token[...] = jnp.zeros_like(token)

    return pl.pallas_call(
        body, name=name,
        out_shape=(pltpu.SemaphoreType.DMA((3,)), pltpu.SemaphoreType.DMA((3,)), pltpu.HBM(blob.shape, blob.dtype),
                   pltpu.HBM(shape, blob.dtype), jax.ShapeDtypeStruct((8, 128), F32)),
        in_specs=(HBM_SPEC, HBM_SPEC, ANY),
        out_specs=(SEM_SPEC, SEM_SPEC, HBM_SPEC, HBM_SPEC, pl.BlockSpec(memory_space=pltpu.VMEM)),
        input_output_aliases={0: 2, 1: 3},
        compiler_params=pltpu.CompilerParams(has_side_effects=SPLIT_EFFECT),
    )(pltpu.with_memory_space_constraint(blob, pltpu.HBM),
      pltpu.with_memory_space_constraint(lax.empty(shape, blob.dtype), pltpu.HBM), after)


def _gather_wait(send_sems, recv_sems, b_thru, land_thru, after, *, name):
    def body(b_ref, land_ref, send_sems, recv_sems, after_ref, b_dead, got_ref):
        for cp in _gather_peers_copies(b_ref, land_ref, send_sems, recv_sems, False):
            cp.wait_send()
            cp.wait_recv()

    return pl.pallas_call(
        body, name=name,
        out_shape=(pltpu.HBM(b_thru.shape, b_thru.dtype), pltpu.HBM(land_thru.shape, land_thru.dtype)),
        in_specs=(HBM_SPEC, HBM_SPEC, SEM_SPEC, SEM_SPEC, ANY), out_specs=(HBM_SPEC, HBM_SPEC),
        input_output_aliases={0: 0, 1: 1},
        compiler_params=pltpu.CompilerParams(has_side_effects=SPLIT_EFFECT),
    )(b_thru, land_thru, send_sems, recv_sems, after)


def _gather_forward(land, *, name):
    half_rows = land.shape[1] // 2

    def body(l_ref, o_ref, send_sems, recv_sems):
        x, y, c = lax.axis_index("x"), lax.axis_index("y"), lax.axis_index("c")
        cps = []
        for k, (_, _, pc) in enumerate(_chip_peers(x, y)):
            mine = pl.ds(c * half_rows, half_rows)
            cps.append(pltpu.make_async_remote_copy(
                src_ref=l_ref.at[pc, mine, :], dst_ref=o_ref.at[pc, mine, :], send_sem=send_sems.at[k],
                recv_sem=recv_sems.at[k], device_id=(x, y, 1 - c), device_id_type=MESH))
        for cp in cps:
            cp.start()
        for k, (_, _, pc) in enumerate(_chip_peers(x, y)):
            theirs = pl.ds((1 - c) * half_rows, half_rows)
            pltpu.make_async_remote_copy(
                src_ref=l_ref.at[pc, theirs, :], dst_ref=o_ref.at[pc, theirs, :], send_sem=send_sems.at[k],
                recv_sem=recv_sems.at[k], device_id=(x, y, 1 - c), device_id_type=MESH).wait_recv()
        for cp in cps:
            cp.wait_send()

    return pl.pallas_call(
        body, name=name, in_specs=[ANY], out_specs=ANY, input_output_aliases={0: 0},
        out_shape=jax.ShapeDtypeStruct(land.shape, land.dtype),
        scratch_shapes=[pltpu.SemaphoreType.DMA((3,)), pltpu.SemaphoreType.DMA((3,))],
    )(land)


def _exchange_peers_copies(p_ref, land_ref, send_sems, recv_sems, sending):
    x, y, c = lax.axis_index("x"), lax.axis_index("y"), lax.axis_index("c")
    me = 2 * x + y
    return [pltpu.make_async_remote_copy(src_ref=p_ref.at[pc], dst_ref=land_ref.at[me if sending else pc],
                                         send_sem=send_sems.at[k], recv_sem=recv_sems.at[k],
                                         device_id=(px, py, c), device_id_type=MESH)
            for k, (px, py, pc) in enumerate(_chip_peers(x, y))]


def _exchange_start(p, *, name):
    def body(p_ref, land_ref, send_sems, recv_sems, p_thru, land_thru, token):
        for cp in _exchange_peers_copies(p_ref, land_ref, send_sems, recv_sems, True):
            cp.start()
        token[...] = jnp.zeros_like(token)

    return pl.pallas_call(
        body, name=name,
        out_shape=(pltpu.SemaphoreType.DMA((3,)), pltpu.SemaphoreType.DMA((3,)), pltpu.HBM(p.shape, p.dtype),
                   pltpu.HBM(p.shape, p.dtype), jax.ShapeDtypeStruct((8, 128), F32)),
        in_specs=(HBM_SPEC, HBM_SPEC),
        out_specs=(SEM_SPEC, SEM_SPEC, HBM_SPEC, HBM_SPEC, pl.BlockSpec(memory_space=pltpu.VMEM)),
        input_output_aliases={0: 2, 1: 3},
        compiler_params=pltpu.CompilerParams(has_side_effects=SPLIT_EFFECT),
    )(pltpu.with_memory_space_constraint(p, pltpu.HBM),
      pltpu.with_memory_space_constraint(lax.empty(p.shape, p.dtype), pltpu.HBM))


def _exchange_wait(send_sems, recv_sems, p_thru, land_thru, after, *, name):
    def body(p_ref, land_ref, send_sems, recv_sems, after_ref, p_dead, got_ref):
        for cp in _exchange_peers_copies(p_ref, land_ref, send_sems, recv_sems, False):
            cp.wait_send()
            cp.wait_recv()

    return pl.pallas_call(
        body, name=name,
        out_shape=(pltpu.HBM(p_thru.shape, p_thru.dtype), pltpu.HBM(p_thru.shape, p_thru.dtype)),
        in_specs=(HBM_SPEC, HBM_SPEC, SEM_SPEC, SEM_SPEC, ANY), out_specs=(HBM_SPEC, HBM_SPEC),
        input_output_aliases={0: 0, 1: 1},
        compiler_params=pltpu.CompilerParams(has_side_effects=SPLIT_EFFECT),
    )(p_thru, land_thru, send_sems, recv_sems, after)


def _sum_parts(parts, own, *, name):
    k, r, c = parts.shape
    tr = _pick(r, (512, 256, 128))

    def body(p_ref, own_ref, o_ref):
        me = 2 * lax.axis_index("x") + lax.axis_index("y")
        acc = jnp.zeros((tr, c), F32)
        for i in range(k):
            acc = acc + jnp.where(me == i, own_ref[i], p_ref[i]).astype(F32)
        o_ref[...] = acc

    blk = pl.BlockSpec((k, tr, c), lambda i: (0, i, 0))
    return pl.pallas_call(
        body, name=name, grid=(r // tr,), in_specs=[blk, blk],
        out_specs=pl.BlockSpec((tr, c), lambda i: (i, 0)),
        out_shape=jax.ShapeDtypeStruct((r, c), F32),
        compiler_params=_cparams(("parallel",)),
    )(parts, own)


def _join_halves(gh, *, name):
    def body(g_ref, o_ref, send_sem, recv_sem):
        x, y, c = lax.axis_index("x"), lax.axis_index("y"), lax.axis_index("c")
        cp = pltpu.make_async_remote_copy(src_ref=g_ref, dst_ref=o_ref, send_sem=send_sem, recv_sem=recv_sem,
                                          device_id=(x, y, 1 - c), device_id_type=MESH)
        cp.start()
        cp.wait()

    other = pl.pallas_call(
        body, name=name, in_specs=[ANY], out_specs=ANY,
        out_shape=jax.ShapeDtypeStruct(gh.shape, F32),
        scratch_shapes=[pltpu.SemaphoreType.DMA, pltpu.SemaphoreType.DMA],
    )(gh)
    south = lax.axis_index("c") == 0
    return jnp.concatenate([jnp.where(south, gh, other), jnp.where(south, other, gh)], axis=0)


def _gather_small(s, *, name):
    rows = s.shape[0]

    def body(s_ref, o_ref, send_sems, recv_sems, local_sem):
        x, y, c = lax.axis_index("x"), lax.axis_index("y"), lax.axis_index("c")
        me = 4 * x + 2 * y + c
        mine = pltpu.make_async_copy(s_ref, o_ref.at[me], local_sem)
        mine.start()
        peers = []
        for k in range(1, 8):
            peers.append((1 - x if k & 4 else x, 1 - y if k & 2 else y, 1 - c if k & 1 else c))
        cps = [pltpu.make_async_remote_copy(src_ref=s_ref, dst_ref=o_ref.at[me], send_sem=send_sems.at[k],
                                            recv_sem=recv_sems.at[k], device_id=p, device_id_type=MESH)
               for k, p in enumerate(peers)]
        for cp in cps:
            cp.start()
        for k, (px, py, pc) in enumerate(peers):
            pltpu.make_async_remote_copy(src_ref=s_ref, dst_ref=o_ref.at[4 * px + 2 * py + pc],
                                         send_sem=send_sems.at[k], recv_sem=recv_sems.at[k],
                                         device_id=(px, py, pc), device_id_type=MESH).wait_recv()
        for cp in cps:
            cp.wait_send()
        mine.wait()

    return pl.pallas_call(
        body, name=name, in_specs=[ANY], out_specs=ANY,
        out_shape=jax.ShapeDtypeStruct((8, rows, 128), F32),
        scratch_shapes=[pltpu.SemaphoreType.DMA((7,)), pltpu.SemaphoreType.DMA((7,)), pltpu.SemaphoreType.DMA],
    )(s)


IN_SHARD = IN_WIDTH // N_CHIPS
IN_SHARD_PAD = 1536
UP_ROWS, DOWN_ROWS, OUT_ROWS = 1024, 1024, 512
REST_ROWS = UP_ROWS + DOWN_ROWS + OUT_ROWS


def _pack_in(w_in_s):
    return jnp.pad(w_in_s, ((0, 0), (0, IN_SHARD_PAD - IN_SHARD))).astype(BF16)


def _pack_rest(w_out_s, w_up_s, w_down_s):
    return jnp.concatenate([w_up_s, w_down_s, w_out_s], axis=0).astype(BF16)


def _unpack_rest(blob):
    return (blob[UP_ROWS + DOWN_ROWS:], blob[0:UP_ROWS], blob[UP_ROWS:UP_ROWS + DOWN_ROWS])


def _with_own(gathered, own):
    me = 2 * lax.axis_index("x") + lax.axis_index("y")
    return [jnp.where(me == j, own, gathered[j]) for j in range(N_CHIPS)]


def _full_w_in(g_in, own):
    return jnp.concatenate([s[:, :IN_SHARD] for s in _with_own(g_in, own)], axis=1)


def _full_rest(g_rest, own):
    parts = [_unpack_rest(s) for s in _with_own(g_rest, own)]
    w_out = jnp.concatenate([p[0] for p in parts], axis=0)
    w_up = jnp.concatenate([p[1] for p in parts], axis=1)
    w_down = jnp.concatenate([p[2] for p in parts], axis=0)
    return w_out, w_up, w_down


def _split_w_in(w_in):
    z_xbc = w_in[:, 0:2560]
    dt = w_in[:, 2560:2576]
    qkv = w_in[:, 2576:5648]
    f = w_in[:, 5648:5664]
    pad = jnp.zeros((w_in.shape[0], PA_WIDTH - 2592), w_in.dtype)
    return jnp.concatenate([z_xbc, dt, f, pad], axis=1), qkv


def _merge_w_in(d_a, d_qkv):
    return jnp.concatenate([d_a[:, 0:2560], d_a[:, 2560:2576], d_qkv, d_a[:, 2576:2592]], axis=1)


def _local_step(x3, target3, w_in, rest_weights, norm_mix_w, conv_w, conv_b, dt_bias, a_log, d_skip,
                ssd_norm_w, f_bias, norm_mlp_w, norm_final_w, first_after=None, early_grads=None, late_grads=None):
    bl, t, d = x3.shape
    n = bl * t
    x = x3.reshape(n, d)
    target = target3.reshape(n, d)
    w_a, w_qkv = _split_w_in(w_in)
    nfw = norm_final_w.reshape(1, d)
    dskip_e = jnp.repeat(d_skip, HEAD_DIM, axis=1)
    nb = t // ATT_BLOCK

    h0, rstd0 = _rmsnorm_fwd(x, norm_mix_w, name="norm_mix_fwd")
    r1, r2, kt = min(n, 1024), min(n, 512), min(n, 512)
    proj_a = _mm(h0, w_a, name="proj_a", tiles=(r2, PA_WIDTH, D_MODEL), after=first_after)
    qkv = _mm(h0, w_qkv, name="proj_qkv", tiles=(r2, QKV_WIDTH, D_MODEL), out_dtype=BF16)
    bias128 = jnp.concatenate([dt_bias, f_bias, jnp.zeros((1, 96), F32)], axis=1)
    alog128 = jnp.concatenate([a_log, jnp.zeros((1, 112), F32)], axis=1)
    dt, gate_d, acum, ccum = _prep(proj_a, bias128, alog128, bl, t)
    acum_t = acum[:, 0:16].T
    negc = jnp.pad(-ccum.reshape(bl, t, 8, 2).transpose(0, 2, 3, 1), ((0, 0), (0, 0), (0, 6), (0, 0)))
    xc = _conv_fwd(proj_a, conv_w, conv_b, bl, t)
    y_ssd, y_pre, hprev = _ssd_fwd(xc, proj_a, dt, acum, acum_t, dskip_e, ssd_norm_w, bl, t)
    y_att, lse = _attn_fwd(qkv, negc, bl, t)
    w_out, w_up, w_down = rest_weights(y_att)
    wo_s, wo_a = w_out[:SSD_WIDTH], w_out[SSD_WIDTH:]
    t1 = _mm(y_ssd, wo_s, name="out_proj_ssd", tiles=(r1, D_MODEL, SSD_WIDTH), res=x)
    h1, h1n, rstd1 = _mm_norm_fwd(y_att, wo_a, t1, norm_mlp_w, name="out_proj_att_norm_mlp", tm=r2)
    up = _mm(h1n, w_up, name="mlp_up", tiles=(r2, D_FF, D_MODEL), out_dtype=BF16)
    dh2, dh2b, loss, d_nfw = _mm_final(up, w_down, h1, nfw, target, name="mlp_down_final_norm_loss", tm=r2,
                                       a_act="relu2")

    dup = _mm(dh2b, w_down, name="mlp_down_bwd_act", tiles=(r2, D_FF, D_MODEL), tb=True, epi_up=up, out_dtype=BF16)
    rest_shape = (N_CHIPS, REST_ROWS, D_MODEL)
    gb_rest = _mm(up, dh2b, name="mlp_down_bwd_w", tiles=(DOWN_ROWS, D_MODEL, kt), ta=True, a_act="relu2",
                  out_dtype=BF16, into=(rest_shape, (None, DOWN_ROWS, D_MODEL), lambda i, j, k: (i, 1, 0), None))
    dh1, dh1b, d_nmlp = _mm_norm_bwd(dup, w_up, None, h1, rstd1, norm_mlp_w, dh2, name="mlp_up_bwd_act_norm_mlp",
                                     tm=r2)
    gb_rest = _mm(h1n, dup, name="mlp_up_bwd_w", tiles=(D_MODEL, UP_ROWS, kt), ta=True, out_dtype=BF16,
                  into=(rest_shape, (None, D_MODEL, UP_ROWS), lambda i, j, k: (j, 0, 0), gb_rest))
    dys = _mm(dh1b, wo_s, name="out_proj_bwd_ssd", tiles=(r1, SSD_WIDTH, D_MODEL), tb=True)
    do = _mm(dh1b, wo_a, name="out_proj_bwd_att", tiles=(r1, ATT_WIDTH, D_MODEL), tb=True, out_dtype=BF16)
    out_block = (UP_ROWS + DOWN_ROWS) // OUT_ROWS
    for half, (y_half, tag) in enumerate(((y_ssd, "ssd"), (y_att, "att"))):
        gb_rest = _mm(y_half, dh1b, name="out_proj_bwd_w_" + tag, tiles=(2 * OUT_ROWS, D_MODEL, kt), ta=True,
                      out_dtype=BF16, into=(rest_shape, (2, OUT_ROWS, D_MODEL),
                                            functools.partial(lambda i, j, k, h: (h, out_block, 0), h=half),
                                            gb_rest))
    token = jnp.zeros((8, 128), F32) if early_grads is None else early_grads(gb_rest)
    dq, dk, dv, dcb = _attn_bwd(qkv, do, y_att, lse, negc, token, bl, t)
    dc = jnp.pad(dcb[:, :, 0:2, :].transpose(0, 3, 1, 2).reshape(n, 16), ((0, 0), (16, 96)))
    dxc, dpa, ddt_raw, d_snw, d_dsk, d_alog, d_dtb = _ssd_bwd(dys, xc, proj_a, y_pre, hprev, dt, gate_d, acum,
                                                             acum_t, alog128, dskip_e, ssd_norm_w, bl, t)
    dpa, d_conv_w, d_conv_b = _conv_bwd(dxc, proj_a, conv_w, conv_b, dpa, bl, t)
    dproj_a, d_fb = _fpost(dc, gate_d, ddt_raw, dpa, bl, t)
    dqkv = jnp.concatenate([dq, dk, dv], axis=1)
    d_w_a = _mm(h0, dproj_a, name="proj_a_bwd_w", tiles=(1024, 896, kt), ta=True, out_dtype=BF16)
    d_w_qkv = _mm(h0, dqkv, name="proj_qkv_bwd_w", tiles=(1024, 1024, kt), ta=True, out_dtype=BF16)
    d_w_in = _merge_w_in(d_w_a, d_w_qkv)
    late_token = None if late_grads is None else late_grads(d_w_in)
    t2 = _mm(dproj_a, w_a, name="proj_a_bwd_act", tiles=(r1, D_MODEL, PA_WIDTH), tb=True, after=late_token)
    dx, _, d_nmix = _mm_norm_bwd(dqkv, w_qkv, t2, x, rstd0, norm_mix_w, dh1, name="proj_qkv_bwd_act_norm_mix",
                                 tm=r2)

    grads = dict(norm_mix_w=d_nmix, w_in=d_w_in, conv_w=d_conv_w, conv_b=d_conv_b,
                 dt_bias=d_dtb, a_log=d_alog, d_skip=d_dsk, ssd_norm_w=d_snw, f_bias=d_fb, rest=gb_rest,
                 norm_mlp_w=d_nmlp, norm_final_w=d_nfw)
    return dx.reshape(bl, t, d), loss, grads


SMALL_ORDER = ("norm_mix_w", "conv_w", "conv_b", "dt_bias", "a_log", "d_skip", "ssd_norm_w", "f_bias",
               "norm_mlp_w", "norm_final_w")
SMALL_SIZES = (1024, 4 * CONV_CH, CONV_CH, 16, 16, 16, 1024, 16, 1024, 1024)


def _pack_small(vals, rows):
    flat = jnp.concatenate([v.reshape(-1).astype(F32) for v in vals])
    return jnp.pad(flat, (0, rows * 128 - flat.shape[0])).reshape(rows, 128)


def _unpack_small(packed, sizes):
    flat = packed.reshape(-1)
    out, o = [], 0
    for s in sizes:
        out.append(flat[o:o + s])
        o += s
    return out


def kernel(x, norm_mix_w, w_in, conv_w, conv_b, dt_bias, a_log, d_skip, ssd_norm_w, f_bias, w_out, norm_mlp_w, w_up, w_down, norm_final_w, loss_target, m_norm_mix_w, m_w_in, m_conv_w, m_conv_b, m_dt_bias, m_a_log, m_d_skip, m_ssd_norm_w, m_f_bias, m_w_out, m_norm_mlp_w, m_w_up, m_w_down, m_norm_final_w, v_norm_mix_w, v_w_in, v_conv_w, v_conv_b, v_dt_bias, v_a_log, v_d_skip, v_ssd_norm_w, v_f_bias, v_w_out, v_norm_mlp_w, v_w_up, v_w_down, v_norm_final_w):
    chip = 2 * lax.axis_index("x") + lax.axis_index("y")
    cw = CONV_CH // N_CHIPS

    own_in = _pack_in(w_in[0])
    own_rest = _pack_rest(w_out[0], w_up[0], w_down[0])
    g_in = _gather_weights(own_in, name="gather_w_in")
    w_in_f = _full_w_in(g_in, own_in)
    *rest_handles, rest_token = _gather_start(own_rest, g_in, name="gather_start_rest")

    def rest_weights(after):
        _, landed = _gather_wait(*rest_handles, after, name="gather_wait_rest")
        return _full_rest(_gather_forward(landed, name="gather_forward_rest"), own_rest)
    small_all = _gather_small(_pack_small([conv_w[0]], 16), name="gather_conv_w")
    conv_w_f = jnp.concatenate([small_all[2 * j].reshape(-1)[:4 * cw].reshape(4, cw) for j in range(N_CHIPS)], axis=1)

    c = lax.axis_index("c")

    def chip_partial(gb, tag):
        half_rows = gb.shape[1] // 2
        from_sibling = _swap_halves(gb, name="grad_swap_halves_" + tag)
        my_half = lax.dynamic_slice_in_dim(gb, c * half_rows, half_rows, axis=1)
        return _add_pair(my_half, from_sibling, name="grad_add_sibling_" + tag)

    in_flight = {}

    def early_grads(gb_rest):
        part = chip_partial(gb_rest, "rest")
        *handles, token = _exchange_start(part, name="grad_exchange_start_rest")
        in_flight["rest"] = handles
        return token

    def late_grads(d_w_in):
        gb_in = jnp.stack([_pack_in(d_w_in[:, j * IN_SHARD:(j + 1) * IN_SHARD]) for j in range(N_CHIPS)])
        *handles, token = _exchange_start(chip_partial(gb_in, "in"), name="grad_exchange_start_in")
        in_flight["in"] = handles
        return token

    dx, loss_part, g = _local_step(x, loss_target, w_in_f, rest_weights, norm_mix_w, conv_w_f,
                                   conv_b, dt_bias, a_log, d_skip, ssd_norm_w, f_bias, norm_mlp_w, norm_final_w,
                                   first_after=rest_token, early_grads=early_grads, late_grads=late_grads)

    send_sems, recv_sems, part_rest, land_rest = in_flight["rest"]
    part_rest, parts_rest = _exchange_wait(send_sems, recv_sems, part_rest, land_rest, dx,
                                           name="grad_exchange_wait_rest")
    g_rest_half = _sum_parts(parts_rest, part_rest, name="grad_sum_chips_rest")
    g_w_out, g_w_up, g_w_down = _unpack_rest(_join_halves(g_rest_half, name="grad_join_halves_rest"))

    part_in, parts_in = _exchange_wait(*in_flight["in"], dx, name="grad_exchange_wait_in")
    g_in_half = _sum_parts(parts_in, part_in, name="grad_sum_chips_in")
    g_w_in = _join_halves(g_in_half, name="grad_join_halves_in")[:, :IN_SHARD]

    small_vals = [g[k] for k in SMALL_ORDER] + [loss_part[:, 0:1]]
    small_sum = _sum_leading(_gather_small(_pack_small(small_vals, SMALL_ROWS), name="gather_small_grads"), name="small_sum")
    sg = dict(zip(SMALL_ORDER + ("loss",), _unpack_small(small_sum, SMALL_SIZES + (1,))))
    loss = sg["loss"].reshape(())
    g_conv_full = sg["conv_w"].reshape(4, CONV_CH)
    g_conv = lax.dynamic_slice_in_dim(g_conv_full, chip * cw, cw, axis=1)

    grads = dict(norm_mix_w=sg["norm_mix_w"].reshape(1, -1), w_in=g_w_in[None], conv_w=g_conv[None],
                 conv_b=sg["conv_b"].reshape(1, -1), dt_bias=sg["dt_bias"].reshape(1, -1),
                 a_log=sg["a_log"].reshape(1, -1), d_skip=sg["d_skip"].reshape(1, -1),
                 ssd_norm_w=sg["ssd_norm_w"].reshape(1, -1), f_bias=sg["f_bias"].reshape(1, -1), w_out=g_w_out[None],
                 norm_mlp_w=sg["norm_mlp_w"].reshape(1, -1), w_up=g_w_up[None], w_down=g_w_down[None],
                 norm_final_w=sg["norm_final_w"])
    weights = dict(norm_mix_w=norm_mix_w, w_in=w_in, conv_w=conv_w, conv_b=conv_b, dt_bias=dt_bias, a_log=a_log,
                   d_skip=d_skip, ssd_norm_w=ssd_norm_w, f_bias=f_bias, w_out=w_out, norm_mlp_w=norm_mlp_w,
                   w_up=w_up, w_down=w_down, norm_final_w=norm_final_w)
    ms = dict(norm_mix_w=m_norm_mix_w, w_in=m_w_in, conv_w=m_conv_w, conv_b=m_conv_b, dt_bias=m_dt_bias,
              a_log=m_a_log, d_skip=m_d_skip, ssd_norm_w=m_ssd_norm_w, f_bias=m_f_bias, w_out=m_w_out,
              norm_mlp_w=m_norm_mlp_w, w_up=m_w_up, w_down=m_w_down, norm_final_w=m_norm_final_w)
    vs = dict(norm_mix_w=v_norm_mix_w, w_in=v_w_in, conv_w=v_conv_w, conv_b=v_conv_b, dt_bias=v_dt_bias,
              a_log=v_a_log, d_skip=v_d_skip, ssd_norm_w=v_ssd_norm_w, f_bias=v_f_bias, w_out=v_w_out,
              norm_mlp_w=v_norm_mlp_w, w_up=v_w_up, w_down=v_w_down, norm_final_w=v_norm_final_w)
    names = list(weights)
    big = ("w_in", "w_out", "w_up", "w_down")
    delta, new_m, new_v = {}, {}, {}
    for k, g2 in zip(big[1:], (g_w_out, g_w_up, g_w_down)):
        delta[k], new_m[k], new_v[k] = _adamw(weights[k], g2, ms[k], vs[k], name="adamw_" + k)
    g_in_t = g_w_in.T
    outs_t = _adamw(w_in[0].T, g_in_t, m_w_in[0].T, v_w_in[0].T, name="adamw_w_in")
    delta["w_in"], new_m["w_in"], new_v["w_in"] = [o.T[None] for o in outs_t]
    grads["w_in"] = g_in_t.T[None]
    smalls = [k for k in names if k not in big]
    sizes = [math.prod(weights[k].shape) for k in smalls]
    rows = -(-sum(sizes) // 1024) * 8
    packs = [_pack_small([d[k] for k in smalls], rows) for d in (weights, grads, ms, vs)]
    outs = _adamw(*packs, name="adamw_small")
    for o, dst in zip(outs, (delta, new_m, new_v)):
        for k, val in zip(smalls, _unpack_small(o, sizes)):
            dst[k] = val.reshape(weights[k].shape)
    return (loss, dx, *[grads[k] for k in names], *[delta[k] for k in names], *[new_m[k] for k in names],
            *[new_v[k] for k in names])
```

```python
import functools
import math

import jax
import jax.numpy as jnp
from jax import lax
from jax.experimental import pallas as pl
from jax.experimental.pallas import tpu as pltpu

F32 = jnp.float32
BF16 = jnp.bfloat16
HIGHEST = lax.Precision.HIGHEST
MESH = pl.DeviceIdType.MESH

D_MODEL = 1024
SSD_HEADS = 16
HEAD_DIM = 64
SSD_WIDTH = 1024
SSD_STATE = 128
CONV_CH = 1536
CHUNK = 128
ATT_WIDTH = 1024
EPS = 1e-5
IN_WIDTH = 5664
PA_WIDTH = 2688
QKV_WIDTH = 3072
D_FF = 4096
ATT_BLOCK = 256
NEG = -1e30
LOG2E = 1.4426950408889634
VMEM_LIMIT = 48 * 1024 * 1024

ADAM_LR = 0.001
ADAM_B1 = 0.9
ADAM_B2 = 0.999
ADAM_EPS = 1e-08
ADAM_WD = 0.01
ADAM_STEP = 10

N_CHIPS = 4
SMALL_ROWS = 96


def _cparams(sem):
    return pltpu.CompilerParams(dimension_semantics=sem, vmem_limit_bytes=VMEM_LIMIT)


def _pick(n, cands):
    for c in cands:
        if n % c == 0:
            return c
    return n


MM_CHUNK = 512


def _mm(a, b, *, name, tiles, ta=False, tb=False, out_dtype=F32, res=None, a_act=None, epi_up=None, after=None,
        into=None):
    n_unread = (after is not None) + (into is not None and into[3] is not None)
    if ta:
        K, M = a.shape
    else:
        M, K = a.shape
    if tb:
        N, K2 = b.shape
    else:
        K2, N = b.shape
    assert K == K2, (a.shape, b.shape)
    tm, tn, tk = tiles
    assert M % tm == 0 and N % tn == 0 and K % tk == 0, (name, M, N, K, tiles)
    nk = K // tk
    dn = (((0 if ta else 1,), (1 if tb else 0,)), ((), ()))
    has_res = res is not None
    has_up = epi_up is not None
    cn = _pick(tn, (MM_CHUNK, 384, 256, 128))

    def prologue(av):
        if a_act == "relu2":
            r = jnp.maximum(av.astype(F32), 0.0)
            av = r * r
        return av.astype(BF16)

    def epilogue(out, res_v, up_v):
        if has_res:
            out = out + res_v.astype(F32)
        if has_up:
            out = out * (2.0 * jnp.maximum(up_v.astype(F32), 0.0))
        return out.astype(out_dtype)

    def body(*refs):
        a_ref, b_ref = refs[0], refs[1]
        i = 2
        res_ref = up_ref = None
        if has_res:
            res_ref = refs[i]
            i += 1
        if has_up:
            up_ref = refs[i]
            i += 1
        i += n_unread
        o_ref = refs[i]
        if nk == 1:
            av = prologue(a_ref[...])
            for c in range(tn // cn):
                cs = slice(c * cn, (c + 1) * cn)
                bv = (b_ref[cs, :] if tb else b_ref[:, cs]).astype(BF16)
                out = lax.dot_general(av, bv, dn, preferred_element_type=F32)
                o_ref[:, cs] = epilogue(out, res_ref[:, cs] if has_res else None, up_ref[:, cs] if has_up else None)
            return
        acc_ref = refs[i + 1]
        k = pl.program_id(2)

        @pl.when(k == 0)
        def _():
            acc_ref[...] = jnp.zeros_like(acc_ref)

        acc_ref[...] += lax.dot_general(prologue(a_ref[...]), b_ref[...].astype(BF16), dn,
                                        preferred_element_type=F32)

        @pl.when(k == nk - 1)
        def _():
            out = epilogue(acc_ref[...], res_ref[...] if has_res else None, up_ref[...] if has_up else None)
            o_ref[...] = out.reshape(o_ref.shape)

    a_spec = pl.BlockSpec((tk, tm), lambda i, j, k: (k, i)) if ta else pl.BlockSpec((tm, tk), lambda i, j, k: (i, k))
    b_spec = pl.BlockSpec((tn, tk), lambda i, j, k: (j, k)) if tb else pl.BlockSpec((tk, tn), lambda i, j, k: (k, j))
    o_spec = pl.BlockSpec((tm, tn), lambda i, j, k: (i, j))
    ins, specs = [a, b], [a_spec, b_spec]
    if has_res:
        ins.append(res)
        specs.append(o_spec)
    if has_up:
        ins.append(epi_up)
        specs.append(o_spec)
    if after is not None:
        ins.append(after)
        specs.append(pl.BlockSpec(memory_space=pl.ANY))
    out_shape, out_spec, aliases = jax.ShapeDtypeStruct((M, N), out_dtype), o_spec, {}
    if into is not None:
        shape, block, index, buf = into
        out_shape, out_spec = jax.ShapeDtypeStruct(shape, out_dtype), pl.BlockSpec(block, index)
        if buf is not None:
            aliases = {len(ins): 0}
            ins.append(buf)
            specs.append(pl.BlockSpec(memory_space=pl.ANY))
    return pl.pallas_call(
        body, name=name, grid=(M // tm, N // tn, nk),
        in_specs=specs, out_specs=out_spec, out_shape=out_shape, input_output_aliases=aliases,
        scratch_shapes=[] if nk == 1 else [pltpu.VMEM((tm, tn), F32)],
        compiler_params=_cparams(("parallel", "parallel", "arbitrary")),
    )(*ins)


def _rows_product(a_ref, b_ref, tb, a_act):
    av = a_ref[...]
    if a_act == "relu2":
        r = jnp.maximum(av.astype(F32), 0.0)
        av = r * r
    dn = (((1,), (1 if tb else 0,)), ((), ()))
    return lax.dot_general(av.astype(BF16), b_ref[...].astype(BF16), dn, preferred_element_type=F32)


def _norm_mm(x, w, b, after, *, name, tm):
    m, d = x.shape
    n = b.shape[1]
    cn = _pick(n, (MM_CHUNK, 384, 256, 128))

    def body(x_ref, w_ref, b_ref, after_ref, h_ref, r_ref, o_ref):
        xv = x_ref[...]
        rstd = lax.rsqrt(jnp.mean(xv * xv, axis=1, keepdims=True) + EPS)
        hv = (xv * rstd * w_ref[...]).astype(BF16)
        h_ref[...] = hv
        r_ref[...] = rstd
        for c in range(n // cn):
            cs = slice(c * cn, (c + 1) * cn)
            o_ref[:, cs] = jnp.dot(hv, b_ref[:, cs].astype(BF16), preferred_element_type=F32)

    row = pl.BlockSpec((tm, d), lambda i: (i, 0))
    return pl.pallas_call(
        body, name=name, grid=(m // tm,),
        in_specs=[row, pl.BlockSpec((1, d), lambda i: (0, 0)), pl.BlockSpec((d, n), lambda i: (0, 0)),
                  pl.BlockSpec(memory_space=pl.ANY)],
        out_specs=[row, pl.BlockSpec((tm, 1), lambda i: (i, 0)), pl.BlockSpec((tm, n), lambda i: (i, 0))],
        out_shape=[jax.ShapeDtypeStruct((m, d), BF16), jax.ShapeDtypeStruct((m, 1), F32),
                   jax.ShapeDtypeStruct((m, n), F32)],
        compiler_params=_cparams(("parallel",)),
    )(x, w, b, after)


def _mm_norm_fwd(a, b, res, w, *, name, tm):
    m, k = a.shape
    d = b.shape[1]

    def body(a_ref, b_ref, res_ref, w_ref, h_ref, y_ref, r_ref):
        hv = _rows_product(a_ref, b_ref, False, None) + res_ref[...]
        rstd = lax.rsqrt(jnp.mean(hv * hv, axis=1, keepdims=True) + EPS)
        h_ref[...] = hv
        y_ref[...] = (hv * rstd * w_ref[...]).astype(BF16)
        r_ref[...] = rstd

    row = pl.BlockSpec((tm, d), lambda i: (i, 0))
    return pl.pallas_call(
        body, name=name, grid=(m // tm,),
        in_specs=[pl.BlockSpec((tm, k), lambda i: (i, 0)), pl.BlockSpec((k, d), lambda i: (0, 0)), row,
                  pl.BlockSpec((1, d), lambda i: (0, 0))],
        out_specs=[row, row, pl.BlockSpec((tm, 1), lambda i: (i, 0))],
        out_shape=[jax.ShapeDtypeStruct((m, d), F32), jax.ShapeDtypeStruct((m, d), BF16),
                   jax.ShapeDtypeStruct((m, 1), F32)],
        compiler_params=_cparams(("parallel",)),
    )(a, b, res, w)


def _mm_final(a, b, res, w, target, *, name, tm, a_act):
    m, k = a.shape
    d = b.shape[1]

    def body(a_ref, b_ref, res_ref, w_ref, t_ref, dh_ref, dhb_ref, loss_ref, dw_ref):
        @pl.when(pl.program_id(0) == 0)
        def _():
            loss_ref[...] = jnp.zeros_like(loss_ref)
            dw_ref[...] = jnp.zeros_like(dw_ref)

        hv = _rows_product(a_ref, b_ref, False, a_act) + res_ref[...]
        wv = w_ref[...]
        rstd = lax.rsqrt(jnp.mean(hv * hv, axis=1, keepdims=True) + EPS)
        xhat = hv * rstd
        err = xhat * wv - t_ref[...]
        loss_ref[...] += 0.5 * jnp.sum(jnp.mean(err * err, axis=1, keepdims=True), axis=0, keepdims=True)
        dy = err * (1.0 / d)
        gw = dy * wv
        dh = rstd * (gw - xhat * jnp.mean(gw * xhat, axis=1, keepdims=True))
        dh_ref[...] = dh
        dhb_ref[...] = dh.astype(BF16)
        dw_ref[...] += jnp.sum(dy * xhat, axis=0, keepdims=True)

    row = pl.BlockSpec((tm, d), lambda i: (i, 0))
    vec = pl.BlockSpec((1, d), lambda i: (0, 0))
    return pl.pallas_call(
        body, name=name, grid=(m // tm,),
        in_specs=[pl.BlockSpec((tm, k), lambda i: (i, 0)), pl.BlockSpec((k, d), lambda i: (0, 0)), row, vec, row],
        out_specs=[row, row, pl.BlockSpec((1, 128), lambda i: (0, 0)), vec],
        out_shape=[jax.ShapeDtypeStruct((m, d), F32), jax.ShapeDtypeStruct((m, d), BF16),
                   jax.ShapeDtypeStruct((1, 128), F32), jax.ShapeDtypeStruct((1, d), F32)],
        compiler_params=_cparams(("arbitrary",)),
    )(a, b, res, w, target)


def _mm_norm_bwd(a, b, res, x, rstd, w, dres, *, name, tm):
    m, k = a.shape
    d = b.shape[0]
    has_res = res is not None

    def body(*refs):
        a_ref, b_ref = refs[0], refs[1]
        i = 2
        res_ref = None
        if has_res:
            res_ref = refs[i]
            i += 1
        x_ref, r_ref, w_ref, d_ref, dx_ref, dxb_ref, dw_ref = refs[i:i + 7]

        @pl.when(pl.program_id(0) == 0)
        def _():
            dw_ref[...] = jnp.zeros_like(dw_ref)

        g = _rows_product(a_ref, b_ref, True, None)
        if has_res:
            g = g + res_ref[...]
        r = r_ref[...]
        xhat = x_ref[...] * r
        gw = g * w_ref[...]
        dx = d_ref[...] + r * (gw - xhat * jnp.mean(gw * xhat, axis=1, keepdims=True))
        dx_ref[...] = dx
        dxb_ref[...] = dx.astype(BF16)
        dw_ref[...] += jnp.sum(g * xhat, axis=0, keepdims=True)

    row = pl.BlockSpec((tm, d), lambda i: (i, 0))
    vec = pl.BlockSpec((1, d), lambda i: (0, 0))
    ins = [a, b] + ([res] if has_res else []) + [x, rstd, w, dres]
    specs = ([pl.BlockSpec((tm, k), lambda i: (i, 0)), pl.BlockSpec((d, k), lambda i: (0, 0))]
             + ([row] if has_res else []) + [row, pl.BlockSpec((tm, 1), lambda i: (i, 0)), vec, row])
    return pl.pallas_call(
        body, name=name, grid=(m // tm,), in_specs=specs, out_specs=[row, row, vec],
        out_shape=[jax.ShapeDtypeStruct((m, d), F32), jax.ShapeDtypeStruct((m, d), BF16),
                   jax.ShapeDtypeStruct((1, d), F32)],
        compiler_params=_cparams(("arbitrary",)),
    )(*ins)


def _softplus(x):
    return jnp.maximum(x, 0.0) + jnp.log(1.0 + jnp.exp(-jnp.abs(x)))


def _prep(proj_a, bias128, alog128, bl, t):
    n = bl * t
    nch = t // CHUNK
    col0 = (SSD_WIDTH + CONV_CH) // 128

    def body(p_ref, b_ref, al_ref, dt_ref, gd_ref, ac_ref, c_ref):
        row = lax.broadcasted_iota(jnp.int32, (CHUNK, CHUNK), 0)
        col = lax.broadcasted_iota(jnp.int32, (CHUNK, CHUNK), 1)
        tril = (row >= col).astype(F32)
        lane = lax.broadcasted_iota(jnp.int32, (1, 128), 1)
        head_lanes = lane < 16
        a_row = -jnp.exp(al_ref[...])
        carry = jnp.zeros((1, 128), F32)
        for ci in range(nch):
            rows = slice(ci * CHUNK, (ci + 1) * CHUNK)
            xv = p_ref[rows, :] + b_ref[...]
            sp = _softplus(xv)
            acum = jnp.dot(tril, a_row * sp, precision=HIGHEST, preferred_element_type=F32)
            c = jnp.dot(tril, -_softplus(-xv), precision=HIGHEST, preferred_element_type=F32) + carry
            carry = c[CHUNK - 1:CHUNK, :]
            dt_ref[rows, :] = jnp.where(head_lanes, sp, 0.0)
            gd_ref[rows, :] = jnp.where(head_lanes, jax.nn.sigmoid(xv),
                                        jnp.where(lane < 32, jax.nn.sigmoid(-xv), 0.0))
            ac_ref[rows, :] = jnp.where(head_lanes, acum, 0.0)
            c_ref[rows, :] = c[:, 16:32]

    o128 = pl.BlockSpec((t, 128), lambda b: (b, 0))
    v128 = pl.BlockSpec((1, 128), lambda b: (0, 0))
    w128 = jax.ShapeDtypeStruct((n, 128), F32)
    return pl.pallas_call(
        body, name="head_scalars", grid=(bl,),
        in_specs=[pl.BlockSpec((t, 128), lambda b: (b, col0)), v128, v128],
        out_specs=[o128, o128, o128, pl.BlockSpec((t, 16), lambda b: (b, 0))],
        out_shape=[w128, w128, w128, jax.ShapeDtypeStruct((n, 16), F32)],
        compiler_params=_cparams(("parallel",)),
    )(proj_a, bias128, alog128)


def _fpost(dc, gate_d, ddt, dpa, bl, t):
    n = bl * t
    nch = t // CHUNK
    col0 = (SSD_WIDTH + CONV_CH) // 128

    def body(dc_ref, gd_ref, ddt_ref, dpa_in, out_ref, db_ref):
        @pl.when(pl.program_id(0) == 0)
        def _():
            db_ref[...] = jnp.zeros_like(db_ref)

        row = lax.broadcasted_iota(jnp.int32, (CHUNK, CHUNK), 0)
        col = lax.broadcasted_iota(jnp.int32, (CHUNK, CHUNK), 1)
        triu = (row <= col).astype(F32)
        lane = lax.broadcasted_iota(jnp.int32, (1, 128), 1)
        gate_lanes = (lane >= 16) & (lane < 32)
        carry = jnp.zeros((1, 128), F32)
        db = jnp.zeros((1, 128), F32)
        for ci in reversed(range(nch)):
            rows = slice(ci * CHUNK, (ci + 1) * CHUNK)
            dlf = jnp.dot(triu, dc_ref[rows, :], precision=HIGHEST, preferred_element_type=F32) + carry
            carry = dlf[0:1, :]
            df = jnp.where(gate_lanes, dlf * gd_ref[rows, :], 0.0)
            out_ref[rows, :] = (ddt_ref[rows, :] + df).astype(BF16)
            db = db + jnp.sum(df, axis=0, keepdims=True)
        db_ref[...] += db[:, 16:32]

    blk = pl.BlockSpec((t, 128), lambda b: (b, 0))
    return pl.pallas_call(
        body, name="forget_gate_bwd", grid=(bl,),
        in_specs=[blk, blk, blk, ANY],
        out_specs=[pl.BlockSpec((t, 128), lambda b: (b, col0)), pl.BlockSpec((1, 16), lambda b: (0, 0))],
        out_shape=[jax.ShapeDtypeStruct(dpa.shape, dpa.dtype), jax.ShapeDtypeStruct((1, 16), F32)],
        input_output_aliases={3: 0},
        compiler_params=_cparams(("arbitrary",)),
    )(dc, gate_d, ddt, dpa)


CONV_TILE = 256
CONV_ROWS = 256


def _conv_taps(u_ref, i, w, bias):
    r0 = pl.multiple_of(i * CONV_ROWS, CONV_ROWS)
    cur = u_ref[pl.ds(r0, CONV_ROWS), :]
    p0 = pl.multiple_of(jnp.maximum(r0 - 8, 0), 8)
    prev = jnp.where(i > 0, u_ref[pl.ds(p0, 8), :], 0.0)
    cat = jnp.concatenate([prev, cur], axis=0)
    pre = bias + w[3:4, :] * cur
    taps = [cur]
    for s in (1, 2, 3):
        sh = pltpu.roll(cat, s, 0)[8:, :]
        taps.append(sh)
        pre = pre + w[3 - s:4 - s, :] * sh
    return r0, pre, taps


def _conv_fwd(proj_a, conv_w, conv_b, bl, t):
    n = bl * t
    nct = CONV_CH // CONV_TILE
    c0 = SSD_WIDTH // CONV_TILE

    def body(u_ref, w_ref, b_ref, o_ref):
        w = w_ref[...]
        bias = b_ref[...]

        def chunk(i, carry):
            r0, pre, _ = _conv_taps(u_ref, i, w, bias)
            o_ref[pl.ds(r0, CONV_ROWS), :] = pre * jax.nn.sigmoid(pre)
            return carry

        lax.fori_loop(0, t // CONV_ROWS, chunk, 0)

    return pl.pallas_call(
        body, name="conv_silu_fwd", grid=(bl, nct),
        in_specs=[pl.BlockSpec((t, CONV_TILE), lambda b, c: (b, c0 + c)),
                  pl.BlockSpec((4, CONV_TILE), lambda b, c: (0, c)),
                  pl.BlockSpec((1, CONV_TILE), lambda b, c: (0, c))],
        out_specs=pl.BlockSpec((t, CONV_TILE), lambda b, c: (b, c)),
        out_shape=jax.ShapeDtypeStruct((n, CONV_CH), F32),
        compiler_params=_cparams(("parallel", "parallel")),
    )(proj_a, conv_w, conv_b)


def _conv_bwd(dxc, proj_a, conv_w, conv_b, dpa, bl, t):
    nct = CONV_CH // CONV_TILE
    c0 = SSD_WIDTH // CONV_TILE
    nrc = t // CONV_ROWS

    def body(g_ref, u_ref, w_ref, b_ref, dpa_in, du_ref, dw_ref, db_ref, dp_scr):
        @pl.when(pl.program_id(1) == 0)
        def _():
            dw_ref[...] = jnp.zeros_like(dw_ref)
            db_ref[...] = jnp.zeros_like(db_ref)

        w = w_ref[...]
        bias = b_ref[...]
        dp_scr[pl.ds(t, 8), :] = jnp.zeros((8, CONV_TILE), F32)

        def chunk1(i, carry):
            dw0, dw1, dw2, dw3, db = carry
            r0, pre, taps = _conv_taps(u_ref, i, w, bias)
            sg = jax.nn.sigmoid(pre)
            dpre = g_ref[pl.ds(r0, CONV_ROWS), :] * (sg * (1.0 + pre * (1.0 - sg)))
            dp_scr[pl.ds(r0, CONV_ROWS), :] = dpre
            dw3 = dw3 + jnp.sum(dpre * taps[0], axis=0, keepdims=True)
            dw2 = dw2 + jnp.sum(dpre * taps[1], axis=0, keepdims=True)
            dw1 = dw1 + jnp.sum(dpre * taps[2], axis=0, keepdims=True)
            dw0 = dw0 + jnp.sum(dpre * taps[3], axis=0, keepdims=True)
            db = db + jnp.sum(dpre, axis=0, keepdims=True)
            return dw0, dw1, dw2, dw3, db

        z = jnp.zeros((1, CONV_TILE), F32)
        dw0, dw1, dw2, dw3, db = lax.fori_loop(0, nrc, chunk1, (z, z, z, z, z))
        dw_ref[...] += jnp.concatenate([dw0, dw1, dw2, dw3], axis=0)
        db_ref[...] += db

        def chunk2(i, carry):
            r0 = pl.multiple_of(i * CONV_ROWS, CONV_ROWS)
            cat = dp_scr[pl.ds(r0, CONV_ROWS + 8), :]
            du = w[3:4, :] * cat[:CONV_ROWS, :]
            for s in (1, 2, 3):
                du = du + w[3 - s:4 - s, :] * pltpu.roll(cat, CONV_ROWS + 8 - s, 0)[:CONV_ROWS, :]
            du_ref[pl.ds(r0, CONV_ROWS), :] = du.astype(BF16)
            return carry

        lax.fori_loop(0, nrc, chunk2, 0)

    return pl.pallas_call(
        body, name="conv_silu_bwd", grid=(nct, bl),
        in_specs=[pl.BlockSpec((t, CONV_TILE), lambda c, b: (b, c)),
                  pl.BlockSpec((t, CONV_TILE), lambda c, b: (b, c0 + c)),
                  pl.BlockSpec((4, CONV_TILE), lambda c, b: (0, c)),
                  pl.BlockSpec((1, CONV_TILE), lambda c, b: (0, c)), ANY],
        out_specs=[pl.BlockSpec((t, CONV_TILE), lambda c, b: (b, c0 + c)),
                   pl.BlockSpec((4, CONV_TILE), lambda c, b: (0, c)),
                   pl.BlockSpec((1, CONV_TILE), lambda c, b: (0, c))],
        out_shape=[jax.ShapeDtypeStruct(dpa.shape, dpa.dtype), jax.ShapeDtypeStruct((4, CONV_CH), F32),
                   jax.ShapeDtypeStruct((1, CONV_CH), F32)],
        input_output_aliases={4: 0},
        scratch_shapes=[pltpu.VMEM((t + 8, CONV_TILE), F32)],
        compiler_params=_cparams(("parallel", "arbitrary")),
    )(dxc, proj_a, conv_w, conv_b, dpa)


NT_DIMS = (((1,), (1,)), ((), ()))
TN_DIMS = (((0,), (0,)), ((), ()))


def _dot(a, b, dims=None):
    if dims is None:
        return jnp.dot(a, b, preferred_element_type=F32)
    return lax.dot_general(a, b, dims, preferred_element_type=F32)


def _head_expander():
    r = lax.broadcasted_iota(jnp.int32, (128, SSD_WIDTH), 0)
    c = lax.broadcasted_iota(jnp.int32, (128, SSD_WIDTH), 1)
    return ((c // HEAD_DIM == r % 16) & (r < 48)).astype(BF16)


def _spread(v128, expander):
    hi = v128.astype(BF16).astype(F32)
    r1 = v128 - hi
    mid = r1.astype(BF16).astype(F32)
    lo = (r1 - mid).astype(BF16).astype(F32)
    packed = (hi + pltpu.roll(mid, 16, 1) + pltpu.roll(lo, 32, 1)).astype(BF16)
    return jnp.dot(packed, expander, preferred_element_type=F32)


def _head_sums(v1024, expander):
    hi = v1024.astype(BF16)
    lo = (v1024 - hi.astype(F32)).astype(BF16)
    heads = jnp.where(lax.broadcasted_iota(jnp.int32, expander.shape, 0) < 16, expander, jnp.zeros_like(expander))
    return _dot(hi, heads, NT_DIMS) + _dot(lo, heads, NT_DIMS)


def _ssd_fwd(xc, proj_a, dt, acum, acum_t, dskip_e, norm_w, bl, t):
    n = bl * t
    nch = t // CHUNK
    L = CHUNK

    def body(xc_ref, z_ref, dt_ref, ac_ref, act_ref, dsk_ref, nw_ref, ys_ref, yp_ref, hp_ref, h_scr, y_scr, x_scr):
        @pl.when(pl.program_id(1) == 0)
        def _():
            h_scr[...] = jnp.zeros_like(h_scr)

        row = lax.broadcasted_iota(jnp.int32, (L, L), 0)
        col = lax.broadcasted_iota(jnp.int32, (L, L), 1)
        causal = row >= col
        expander = _head_expander()
        ac_all = ac_ref[...]
        act_all = act_ref[...]
        ac_e = _spread(ac_all, expander)
        e_in = jnp.exp(ac_e)
        dec = jnp.exp(ac_e[L - 1:L, :] - ac_e)
        xs_all = xc_ref[:, 0:SSD_WIDTH]
        x_all = xs_all * _spread(dt_ref[...], expander)
        x_scr[...] = x_all.astype(BF16)
        hp_all = h_scr[...]
        hp_ref[...] = hp_all
        for g in range(2):
            gs = slice(g * 512, (g + 1) * 512)
            bg = xc_ref[:, SSD_WIDTH + g * 128:SSD_WIDTH + (g + 1) * 128].astype(BF16)
            cg = xc_ref[:, SSD_WIDTH + 256 + g * 128:SSD_WIDTH + 256 + (g + 1) * 128].astype(BF16)
            gmat = _dot(cg, bg, NT_DIMS)
            y_scr[:, gs] = (_dot(cg, hp_all[gs, :].astype(BF16), NT_DIMS) * e_in[:, gs]
                            + dsk_ref[:, gs] * xs_all[:, gs])
            s_new = _dot((x_all[:, gs] * dec[:, gs]).astype(BF16), bg, TN_DIMS)
            for r in range(8):
                h = g * 8 + r
                sl = slice(h * HEAD_DIM, (h + 1) * HEAD_DIM)
                ldec = jnp.exp(jnp.where(causal, ac_all[:, h:h + 1] - act_all[h:h + 1, :], NEG))
                y_scr[:, sl] += _dot((gmat * ldec).astype(BF16), x_scr[:, sl])
                elast = jnp.exp(ac_all[L - 1:L, h:h + 1])
                h_scr[sl, :] = elast * hp_all[sl, :] + s_new[r * HEAD_DIM:(r + 1) * HEAD_DIM, :]
        y = y_scr[...]
        yp_ref[...] = y
        zv = z_ref[...]
        yg = y * (zv * jax.nn.sigmoid(zv))
        for g in range(2):
            gs = slice(g * 512, (g + 1) * 512)
            grp = yg[:, gs]
            rstd = lax.rsqrt(jnp.mean(grp * grp, axis=1, keepdims=True) + EPS)
            ys_ref[:, gs] = (grp * rstd * nw_ref[:, gs]).astype(BF16)

    rb = lambda b, c: (b * nch + c, 0)
    v1k = pl.BlockSpec((1, SSD_WIDTH), lambda b, c: (0, 0))
    return pl.pallas_call(
        body, name="ssd_fwd", grid=(bl, nch),
        in_specs=[pl.BlockSpec((L, CONV_CH), rb), pl.BlockSpec((L, SSD_WIDTH), rb),
                  pl.BlockSpec((L, 128), rb), pl.BlockSpec((L, 128), rb),
                  pl.BlockSpec((16, L), lambda b, c: (0, b * nch + c)), v1k, v1k],
        out_specs=[pl.BlockSpec((L, SSD_WIDTH), rb), pl.BlockSpec((L, SSD_WIDTH), rb),
                   pl.BlockSpec((None, SSD_WIDTH, SSD_STATE), lambda b, c: (b * nch + c, 0, 0))],
        out_shape=[jax.ShapeDtypeStruct((n, SSD_WIDTH), BF16), jax.ShapeDtypeStruct((n, SSD_WIDTH), F32),
                   jax.ShapeDtypeStruct((bl * nch, SSD_WIDTH, SSD_STATE), F32)],
        scratch_shapes=[pltpu.VMEM((SSD_WIDTH, SSD_STATE), F32), pltpu.VMEM((L, SSD_WIDTH), F32),
                        pltpu.VMEM((L, SSD_WIDTH), BF16)],
        compiler_params=_cparams(("parallel", "arbitrary")),
    )(xc, proj_a, dt, acum, acum_t, dskip_e, norm_w)


def _ssd_bwd(dys, xc, proj_a, ypre, hprev, dt, gate_d, acum, acum_t, alog128, dskip_e, norm_w, bl, t):
    n = bl * t
    nch = t // CHUNK
    L = CHUNK

    def body(dys_ref, xc_ref, z_ref, yp_ref, hp_ref, dt_ref, gd_ref, ac_ref, act_ref, al_ref, dsk_ref, nw_ref,
             dxc_ref, dz_ref, ddt_ref, dnw_ref, dsk16_ref, da16_ref, db16_ref,
             dh_scr, dy_scr, x_scr, dx_scr, red_scr):
        first = (pl.program_id(0) == 0) & (pl.program_id(1) == 0)

        @pl.when(first)
        def _():
            dnw_ref[...] = jnp.zeros_like(dnw_ref)
            dsk16_ref[...] = jnp.zeros_like(dsk16_ref)
            da16_ref[...] = jnp.zeros_like(da16_ref)
            db16_ref[...] = jnp.zeros_like(db16_ref)

        @pl.when(pl.program_id(1) == 0)
        def _():
            dh_scr[...] = jnp.zeros_like(dh_scr)

        y = yp_ref[...]
        zv = z_ref[...]
        sz = jax.nn.sigmoid(zv)
        gate = zv * sz
        yg = y * gate
        dout = dys_ref[...]
        nw = nw_ref[...]
        for g in range(2):
            gs = slice(g * 512, (g + 1) * 512)
            grp = yg[:, gs]
            rstd = lax.rsqrt(jnp.mean(grp * grp, axis=1, keepdims=True) + EPS)
            ghat = grp * rstd
            dnw_ref[:, gs] += jnp.sum(dout[:, gs] * ghat, axis=0, keepdims=True)
            gw = dout[:, gs] * nw[:, gs]
            dyg = rstd * (gw - ghat * jnp.mean(gw * ghat, axis=1, keepdims=True))
            dy_scr[:, gs] = dyg * gate[:, gs]
            dz_ref[:, gs] = (dyg * y[:, gs] * (sz[:, gs] * (1.0 + zv[:, gs] * (1.0 - sz[:, gs])))).astype(BF16)

        row = lax.broadcasted_iota(jnp.int32, (L, L), 0)
        col = lax.broadcasted_iota(jnp.int32, (L, L), 1)
        causal = row >= col
        lane128 = lax.broadcasted_iota(jnp.int32, (1, L), 1)
        rows128 = lax.broadcasted_iota(jnp.int32, (L, 1), 0)
        last_row = rows128 == (L - 1)
        expander = _head_expander()
        ac_all = ac_ref[...]
        act_all = act_ref[...]
        dt_all = dt_ref[...]
        dt_e = _spread(dt_all, expander)
        ac_e = _spread(ac_all, expander)
        e_in = jnp.exp(ac_e)
        dec = jnp.exp(ac_e[L - 1:L, :] - ac_e)
        xs_all = xc_ref[:, 0:SSD_WIDTH]
        x_all = xs_all * dt_e
        x_scr[...] = x_all.astype(BF16)
        dy_all = dy_scr[...]
        hp_all = hp_ref[...]
        ds_all = dh_scr[...]
        dsk_cols = jnp.sum(dy_all * xs_all, axis=0, keepdims=True)
        dac = jnp.zeros((L, L), F32)
        dac_row = jnp.zeros((L, L), F32)
        ddec_cols = []
        for g in range(2):
            gs = slice(g * 512, (g + 1) * 512)
            bsl = slice(SSD_WIDTH + g * 128, SSD_WIDTH + (g + 1) * 128)
            csl = slice(SSD_WIDTH + 256 + g * 128, SSD_WIDTH + 256 + (g + 1) * 128)
            bg = xc_ref[:, bsl].astype(BF16)
            cg = xc_ref[:, csl].astype(BF16)
            gmat = _dot(cg, bg, NT_DIMS)
            hpb = hp_all[gs, :].astype(BF16)
            dsb = ds_all[gs, :].astype(BF16)
            ch = _dot(cg, hpb, NT_DIMS)
            dye = dy_all[:, gs] * e_in[:, gs]
            dyeb = dye.astype(BF16)
            dc_acc = _dot(dyeb, hpb)
            dhp = _dot(dyeb, cg, TN_DIMS)
            dxd = _dot(bg, dsb, NT_DIMS)
            db_acc = _dot((x_all[:, gs] * dec[:, gs]).astype(BF16), dsb)
            ddec = dxd * x_all[:, gs] * dec[:, gs]
            ddec_cols.append(jnp.sum(ddec, axis=0, keepdims=True))
            dx_scr[:, gs] = dxd * dec[:, gs]
            red_scr[:, gs] = dye * ch - ddec
            dg_sum = jnp.zeros((L, L), F32)
            for r in range(8):
                h = g * 8 + r
                sl = slice(h * HEAD_DIM, (h + 1) * HEAD_DIM)
                onehot_w = lane128 == h
                ldec = jnp.exp(jnp.where(causal, ac_all[:, h:h + 1] - act_all[h:h + 1, :], NEG))
                mf = gmat * ldec
                dyb = dy_scr[:, sl].astype(BF16)
                dm = _dot(dyb, x_scr[:, sl], NT_DIMS)
                dx_scr[:, sl] += _dot(mf.astype(BF16), dyb, TN_DIMS)
                dg_sum = dg_sum + dm * ldec
                wmat = dm * mf
                elast = jnp.exp(ac_all[L - 1:L, h:h + 1])
                hp_h = hp_all[sl, :]
                ds_h = ds_all[sl, :]
                extra = elast * jnp.sum(jnp.sum(hp_h * ds_h, axis=1, keepdims=True), axis=0, keepdims=True)
                dac = dac + jnp.where(onehot_w, jnp.sum(wmat, axis=1, keepdims=True) + jnp.where(last_row, extra, 0.0),
                                      0.0)
                dac_row = dac_row + jnp.where(rows128 == h, -jnp.sum(wmat, axis=0, keepdims=True), 0.0)
                dh_scr[sl, :] = elast * ds_h + dhp[r * HEAD_DIM:(r + 1) * HEAD_DIM, :]
            dgb = dg_sum.astype(BF16)
            dxc_ref[:, csl] = dc_acc + _dot(dgb, bg)
            dxc_ref[:, bsl] = db_acc + _dot(dgb, cg, TN_DIMS)
        dx_all = dx_scr[...]
        dxc_ref[:, 0:SSD_WIDTH] = dx_all * dt_e + dsk_ref[...] * dy_all
        red = red_scr[...]
        dac_slab = _head_sums(red, expander)
        ddec_tot = _head_sums(jnp.broadcast_to(jnp.concatenate(ddec_cols, axis=1), (8, SSD_WIDTH)), expander)
        ddt_x = _head_sums(dx_all * xs_all, expander)
        dsk16_ref[...] += _head_sums(jnp.broadcast_to(dsk_cols, (8, SSD_WIDTH)), expander)[0:1, 0:16]
        dac = dac + dac_slab + jnp.transpose(dac_row) + jnp.where(last_row, ddec_tot[0:1, :], 0.0)
        triu = (row <= col).astype(F32)
        da = jnp.dot(triu, dac, precision=HIGHEST, preferred_element_type=F32)
        a_row = -jnp.exp(al_ref[...])
        ddt = jnp.where(lane128 < 16, (ddt_x + da * a_row) * gd_ref[...], 0.0)
        ddt_ref[...] = ddt
        da16_ref[...] += (jnp.sum(da * dt_all, axis=0, keepdims=True) * a_row)[:, 0:16]
        db16_ref[...] += jnp.sum(ddt, axis=0, keepdims=True)[:, 0:16]

    rb = lambda b, c: (b * nch + nch - 1 - c, 0)
    v1k = pl.BlockSpec((1, SSD_WIDTH), lambda b, c: (0, 0))
    v16 = pl.BlockSpec((1, 16), lambda b, c: (0, 0))
    v128 = pl.BlockSpec((1, 128), lambda b, c: (0, 0))
    wide = pl.BlockSpec((L, SSD_WIDTH), rb)
    s128 = pl.BlockSpec((L, 128), rb)
    return pl.pallas_call(
        body, name="ssd_bwd", grid=(bl, nch),
        in_specs=[wide, pl.BlockSpec((L, CONV_CH), rb), wide, wide,
                  pl.BlockSpec((None, SSD_WIDTH, SSD_STATE), lambda b, c: (b * nch + nch - 1 - c, 0, 0)),
                  s128, s128, s128, pl.BlockSpec((16, L), lambda b, c: (0, b * nch + nch - 1 - c)), v128, v1k, v1k],
        out_specs=[pl.BlockSpec((L, CONV_CH), rb), wide, s128, v1k, v16, v16, v16],
        out_shape=[jax.ShapeDtypeStruct((n, CONV_CH), F32), jax.ShapeDtypeStruct((n, PA_WIDTH), BF16),
                   jax.ShapeDtypeStruct((n, 128), F32), jax.ShapeDtypeStruct((1, SSD_WIDTH), F32),
                   jax.ShapeDtypeStruct((1, 16), F32), jax.ShapeDtypeStruct((1, 16), F32),
                   jax.ShapeDtypeStruct((1, 16), F32)],
        scratch_shapes=[pltpu.VMEM((SSD_WIDTH, SSD_STATE), F32), pltpu.VMEM((L, SSD_WIDTH), F32),
                        pltpu.VMEM((L, SSD_WIDTH), BF16), pltpu.VMEM((L, SSD_WIDTH), F32),
                        pltpu.VMEM((L, SSD_WIDTH), F32)],
        compiler_params=_cparams(("arbitrary", "arbitrary")),
    )(dys, xc, proj_a, ypre, hprev, dt, gate_d, acum, acum_t, alog128, dskip_e, norm_w)


def _attn_fwd(qkv, negc, bl, t):
    n = bl * t
    tb_ = ATT_BLOCK
    nb = t // tb_
    scale2 = LOG2E / math.sqrt(HEAD_DIM)

    def body(q_ref, k_ref, v_ref, c_ref, o_ref, lse_ref):
        row = lax.broadcasted_iota(jnp.int32, (tb_, tb_), 0)
        col = lax.broadcasted_iota(jnp.int32, (tb_, tb_), 1)
        causal = row >= col
        for qi in range(nb):
            r0, lk = qi * tb_, (qi + 1) * tb_
            for j in range(2):
                sl = slice(j * HEAD_DIM, (j + 1) * HEAD_DIM)
                s = _dot(q_ref[r0:lk, sl], k_ref[0:lk, sl], NT_DIMS) * scale2 + c_ref[j:j + 1, 0:lk] * LOG2E
                tail = jnp.where(causal, s[:, r0:lk], NEG)
                s = tail if qi == 0 else jnp.concatenate([s[:, 0:r0], tail], axis=1)
                m = jnp.max(s, axis=1, keepdims=True)
                p = jnp.exp2(s - m)
                l = jnp.sum(p, axis=1, keepdims=True)
                acc = _dot(p.astype(BF16), v_ref[0:lk, sl])
                o_ref[r0:lk, sl] = (acc / l).astype(BF16)
                lse_ref[r0:lk, sl] = jnp.broadcast_to(m + jnp.log(l) * LOG2E, (tb_, HEAD_DIM))

    blk = lambda off: pl.BlockSpec((t, 128), lambda b, hp: (b, off + hp))
    return pl.pallas_call(
        body, name="fox_attn_fwd", grid=(bl, 8),
        in_specs=[blk(0), blk(8), blk(16), pl.BlockSpec((None, None, 8, t), lambda b, hp: (b, hp, 0, 0))],
        out_specs=[blk(0), blk(0)],
        out_shape=[jax.ShapeDtypeStruct((n, ATT_WIDTH), BF16), jax.ShapeDtypeStruct((n, ATT_WIDTH), F32)],
        compiler_params=_cparams(("parallel", "parallel")),
    )(qkv, qkv, qkv, negc)


def _attn_bwd(qkv, do, o, lse, negc, after, bl, t):
    n = bl * t
    tb_ = ATT_BLOCK
    nb = t // tb_
    scale = 1.0 / math.sqrt(HEAD_DIM)
    scale2 = LOG2E * scale

    def body(q_ref, k_ref, v_ref, do_ref, o_ref, lse_ref, c_ref, after_ref, dq_ref, dk_ref, dv_ref, dc_ref,
             dq_scr, delta_scr, dr_scr, qt_scr, dot_scr, dkt_scr, dvt_scr):
        row = lax.broadcasted_iota(jnp.int32, (tb_, tb_), 0)
        col = lax.broadcasted_iota(jnp.int32, (tb_, tb_), 1)
        causal = row >= col
        dq_scr[...] = jnp.zeros_like(dq_scr)
        dr_scr[...] = jnp.zeros_like(dr_scr)
        dc_ref[...] = jnp.zeros_like(dc_ref)
        qt_scr[...] = jnp.transpose(q_ref[...].astype(F32)).astype(BF16)
        dot_scr[...] = jnp.transpose(do_ref[...].astype(F32)).astype(BF16)
        prod = do_ref[...].astype(F32) * o_ref[...].astype(F32)
        for j in range(2):
            sl = slice(j * HEAD_DIM, (j + 1) * HEAD_DIM)
            delta_scr[:, sl] = jnp.broadcast_to(jnp.sum(prod[:, sl], axis=1, keepdims=True), (t, HEAD_DIM))
        for kj in range(nb):
            r0, r1 = kj * tb_, (kj + 1) * tb_
            for j in range(2):
                sl = slice(j * HEAD_DIM, (j + 1) * HEAD_DIM)
                one = slice(j * HEAD_DIM, j * HEAD_DIM + 1)
                kb = k_ref[r0:r1, sl]
                qs = q_ref[r0:t, sl]
                dos = do_ref[r0:t, sl]
                s = _dot(qs, kb, NT_DIMS) * scale2 + c_ref[j:j + 1, r0:r1] * LOG2E
                head = jnp.where(causal, s[0:tb_, :], NEG)
                s = head if kj == nb - 1 else jnp.concatenate([head, s[tb_:, :]], axis=0)
                p = jnp.exp2(s - lse_ref[r0:t, one])
                dp = _dot(dos, v_ref[r0:r1, sl], NT_DIMS)
                ds = p * (dp - delta_scr[r0:t, one])
                dsb = ds.astype(BF16)
                dvt_scr[sl, r0:r1] = _dot(dot_scr[sl, r0:t], p.astype(BF16))
                dkt_scr[sl, r0:r1] = _dot(qt_scr[sl, r0:t], dsb)
                dq_scr[r0:t, sl] += _dot(dsb, kb)
                dr_scr[r0:t, sl] += jnp.broadcast_to(jnp.sum(ds, axis=1, keepdims=True), (t - r0, HEAD_DIM))
                dc_ref[j:j + 1, r0:r1] = -jnp.sum(ds, axis=0, keepdims=True)
        dq_ref[...] = (dq_scr[...] * scale).astype(BF16)
        dk_ref[...] = (jnp.transpose(dkt_scr[...]) * scale).astype(BF16)
        dv_ref[...] = jnp.transpose(dvt_scr[...]).astype(BF16)
        dr_t = jnp.transpose(dr_scr[...])
        for j in range(2):
            dc_ref[j:j + 1, :] += dr_t[j * HEAD_DIM:j * HEAD_DIM + 1, :]

    blk = lambda off: pl.BlockSpec((t, 128), lambda b, hp: (b, off + hp))
    cblk = pl.BlockSpec((None, None, 8, t), lambda b, hp: (b, hp, 0, 0))
    return pl.pallas_call(
        body, name="fox_attn_bwd", grid=(bl, 8),
        in_specs=[blk(0), blk(8), blk(16), blk(0), blk(0), blk(0), cblk, ANY],
        out_specs=[blk(0), blk(0), blk(0), cblk],
        out_shape=[jax.ShapeDtypeStruct((n, ATT_WIDTH), BF16)] * 3 + [jax.ShapeDtypeStruct((bl, 8, 8, t), F32)],
        scratch_shapes=[pltpu.VMEM((t, 128), F32), pltpu.VMEM((t, 128), F32), pltpu.VMEM((t, 128), F32),
                        pltpu.VMEM((128, t), BF16), pltpu.VMEM((128, t), BF16),
                        pltpu.VMEM((128, t), F32), pltpu.VMEM((128, t), F32)],
        compiler_params=_cparams(("parallel", "parallel")),
    )(qkv, qkv, qkv, do, o, lse, negc, after)


def _adamw(w, g, m, v, *, name):
    lead = w.ndim == 3
    r, c = w.shape[-2:]
    tr = _pick(r, (256, IN_SHARD // 3, 128, 64, 32, 16, 8))
    bc1 = 1.0 - ADAM_B1 ** ADAM_STEP
    bc2 = 1.0 - ADAM_B2 ** ADAM_STEP

    def body(w_ref, g_ref, m_ref, v_ref, d_ref, nm_ref, nv_ref):
        gv = g_ref[...]
        mn = ADAM_B1 * m_ref[...] + (1.0 - ADAM_B1) * gv
        vn = ADAM_B2 * v_ref[...] + (1.0 - ADAM_B2) * (gv * gv)
        m_hat = mn / bc1
        v_hat = vn / bc2
        d_ref[...] = -ADAM_LR * (m_hat / (jnp.sqrt(v_hat) + ADAM_EPS) + ADAM_WD * w_ref[...])
        nm_ref[...] = mn
        nv_ref[...] = vn

    flat = pl.BlockSpec((tr, c), lambda i: (i, 0))
    blk = pl.BlockSpec((None, tr, c), lambda i: (0, i, 0)) if lead else flat
    return pl.pallas_call(
        body, name=name, grid=(r // tr,), in_specs=[blk, flat, blk, blk], out_specs=[blk] * 3,
        out_shape=[jax.ShapeDtypeStruct(w.shape, F32)] * 3,
        compiler_params=_cparams(("parallel",)),
    )(w, g, m, v)


def _sum_leading(parts, *, name, out_dtype=F32):
    k, r, c = parts.shape
    tr = _pick(r, (512, 256, 128, 96, 64, 32, 16, 8))

    def body(p_ref, o_ref):
        acc = p_ref[0].astype(F32)
        for i in range(1, k):
            acc = acc + p_ref[i].astype(F32)
        o_ref[...] = acc.astype(out_dtype)

    return pl.pallas_call(
        body, name=name, grid=(r // tr,),
        in_specs=[pl.BlockSpec((k, tr, c), lambda i: (0, i, 0))],
        out_specs=pl.BlockSpec((tr, c), lambda i: (i, 0)),
        out_shape=jax.ShapeDtypeStruct((r, c), out_dtype),
        compiler_params=_cparams(("parallel",)),
    )(parts)


def _add_pair(a, b, *, name):
    k, r, c = a.shape
    tr = _pick(r, (512, 256, 128))

    def body(a_ref, b_ref, o_ref):
        o_ref[...] = (a_ref[...].astype(F32) + b_ref[...].astype(F32)).astype(BF16)

    blk = pl.BlockSpec((None, tr, c), lambda j, i: (j, i, 0))
    return pl.pallas_call(
        body, name=name, grid=(k, r // tr), in_specs=[blk, blk], out_specs=blk,
        out_shape=jax.ShapeDtypeStruct((k, r, c), BF16),
        compiler_params=_cparams(("parallel", "parallel")),
    )(a, b)


ANY = pl.BlockSpec(memory_space=pl.ANY)


def _chip_peers(x, y):
    return [(1 - x, y, 2 * (1 - x) + y), (x, 1 - y, 2 * x + 1 - y), (1 - x, 1 - y, 2 * (1 - x) + 1 - y)]


def _gather_weights(blob, *, name):
    rows, cols = blob.shape
    half_rows = rows // 2

    def body(b_ref, o_ref, send_sems, recv_sems):
        x, y, c = lax.axis_index("x"), lax.axis_index("y"), lax.axis_index("c")
        me = 2 * x + y
        sibling = (x, y, 1 - c)
        peers = _chip_peers(x, y)

        def half(chip, hc):
            return o_ref.at[chip, pl.ds(hc * half_rows, half_rows), :]

        def copy(k, src, chip, hc, to):
            return pltpu.make_async_remote_copy(src_ref=src, dst_ref=half(chip, hc), send_sem=send_sems.at[k],
                                                recv_sem=recv_sems.at[k], device_id=to, device_id_type=MESH)

        my_half = b_ref.at[pl.ds(c * half_rows, half_rows), :]
        first = [copy(k, my_half, me, c, (px, py, c)) for k, (px, py, _) in enumerate(peers)]
        for cp in first:
            cp.start()
        passed = [copy(3 + k, half(pc, c), pc, c, sibling) for k, (_, _, pc) in enumerate(peers)]
        for k, (px, py, pc) in enumerate(peers):
            copy(k, my_half, pc, c, (px, py, c)).wait_recv()
            passed[k].start()
        for k, (_, _, pc) in enumerate(peers):
            copy(3 + k, half(pc, 1 - c), pc, 1 - c, sibling).wait_recv()
        for cp in first + passed:
            cp.wait_send()

    return pl.pallas_call(
        body, name=name, in_specs=[ANY], out_specs=ANY,
        out_shape=jax.ShapeDtypeStruct((N_CHIPS, rows, cols), BF16),
        scratch_shapes=[pltpu.SemaphoreType.DMA((6,)), pltpu.SemaphoreType.DMA((6,))],
    )(blob)


def _swap_halves(g, *, name):
    _, rows, cols = g.shape
    half_rows = rows // 2

    def body(g_ref, o_ref, send_sem, recv_sem):
        x, y, c = lax.axis_index("x"), lax.axis_index("y"), lax.axis_index("c")
        cp = pltpu.make_async_remote_copy(
            src_ref=g_ref.at[:, pl.ds((1 - c) * half_rows, half_rows), :], dst_ref=o_ref,
            send_sem=send_sem, recv_sem=recv_sem, device_id=(x, y, 1 - c), device_id_type=MESH)
        cp.start()
        cp.wait()

    return pl.pallas_call(
        body, name=name, in_specs=[ANY], out_specs=ANY,
        out_shape=jax.ShapeDtypeStruct((N_CHIPS, half_rows, cols), BF16),
        scratch_shapes=[pltpu.SemaphoreType.DMA, pltpu.SemaphoreType.DMA],
    )(g)


HBM_SPEC = pl.BlockSpec(memory_space=pltpu.HBM)
SEM_SPEC = pl.BlockSpec(memory_space=pltpu.SEMAPHORE)
SPLIT_EFFECT = pltpu.SideEffectType.DATAFLOW_SIDE_EFFECTING


def _gather_peers_copies(b_ref, land_ref, send_sems, recv_sems, sending):
    x, y, c = lax.axis_index("x"), lax.axis_index("y"), lax.axis_index("c")
    me = 2 * x + y
    half_rows = b_ref.shape[0] // 2
    src = b_ref.at[pl.ds(c * half_rows, half_rows), :]
    return [pltpu.make_async_remote_copy(
        src_ref=src, dst_ref=land_ref.at[me if sending else pc, pl.ds(c * half_rows, half_rows), :],
        send_sem=send_sems.at[k], recv_sem=recv_sems.at[k], device_id=(px, py, c), device_id_type=MESH)
        for k, (px, py, pc) in enumerate(_chip_peers(x, y))]


def _gather_start(blob, after, *, name):
    shape = (N_CHIPS,) + blob.shape

    def body(b_ref, land_ref, after_ref, send_sems, recv_sems, b_thru, land_thru, token):
        for cp in _gather_peers_copies(b_ref, land_ref, send_sems, recv_sems, True):
            cp.start()
        token[...] = jnp.zeros_like(token)

    return pl.pallas_call(
        body, name=name,
        out_shape=(pltpu.SemaphoreType.DMA((3,)), pltpu.SemaphoreType.DMA((3,)), pltpu.HBM(blob.shape, blob.dtype),
                   pltpu.HBM(shape, blob.dtype), jax.ShapeDtypeStruct((8, 128), F32)),
        in_specs=(HBM_SPEC, HBM_SPEC, ANY),
        out_specs=(SEM_SPEC, SEM_SPEC, HBM_SPEC, HBM_SPEC, pl.BlockSpec(memory_space=pltpu.VMEM)),
        input_output_aliases={0: 2, 1: 3},
        compiler_params=pltpu.CompilerParams(has_side_effects=SPLIT_EFFECT),
    )(pltpu.with_memory_space_constraint(blob, pltpu.HBM),
      pltpu.with_memory_space_constraint(lax.empty(shape, blob.dtype), pltpu.HBM), after)


def _gather_wait(send_sems, recv_sems, b_thru, land_thru, after, *, name):
    def body(b_ref, land_ref, send_sems, recv_sems, after_ref, b_dead, got_ref):
        for cp in _gather_peers_copies(b_ref, land_ref, send_sems, recv_sems, False):
            cp.wait_send()
            cp.wait_recv()

    return pl.pallas_call(
        body, name=name,
        out_shape=(pltpu.HBM(b_thru.shape, b_thru.dtype), pltpu.HBM(land_thru.shape, land_thru.dtype)),
        in_specs=(HBM_SPEC, HBM_SPEC, SEM_SPEC, SEM_SPEC, ANY), out_specs=(HBM_SPEC, HBM_SPEC),
        input_output_aliases={0: 0, 1: 1},
        compiler_params=pltpu.CompilerParams(has_side_effects=SPLIT_EFFECT),
    )(b_thru, land_thru, send_sems, recv_sems, after)


def _gather_forward(land, *, name):
    half_rows = land.shape[1] // 2

    def body(l_ref, o_ref, send_sems, recv_sems):
        x, y, c = lax.axis_index("x"), lax.axis_index("y"), lax.axis_index("c")
        cps = []
        for k, (_, _, pc) in enumerate(_chip_peers(x, y)):
            mine = pl.ds(c * half_rows, half_rows)
            cps.append(pltpu.make_async_remote_copy(
                src_ref=l_ref.at[pc, mine, :], dst_ref=o_ref.at[pc, mine, :], send_sem=send_sems.at[k],
                recv_sem=recv_sems.at[k], device_id=(x, y, 1 - c), device_id_type=MESH))
        for cp in cps:
            cp.start()
        for k, (_, _, pc) in enumerate(_chip_peers(x, y)):
            theirs = pl.ds((1 - c) * half_rows, half_rows)
            pltpu.make_async_remote_copy(
                src_ref=l_ref.at[pc, theirs, :], dst_ref=o_ref.at[pc, theirs, :], send_sem=send_sems.at[k],
                recv_sem=recv_sems.at[k], device_id=(x, y, 1 - c), device_id_type=MESH).wait_recv()
        for cp in cps:
            cp.wait_send()

    return pl.pallas_call(
        body, name=name, in_specs=[ANY], out_specs=ANY, input_output_aliases={0: 0},
        out_shape=jax.ShapeDtypeStruct(land.shape, land.dtype),
        scratch_shapes=[pltpu.SemaphoreType.DMA((3,)), pltpu.SemaphoreType.DMA((3,))],
    )(land)


def _exchange_peers_copies(p_ref, land_ref, send_sems, recv_sems, sending):
    x, y, c = lax.axis_index("x"), lax.axis_index("y"), lax.axis_index("c")
    me = 2 * x + y
    return [pltpu.make_async_remote_copy(src_ref=p_ref.at[pc], dst_ref=land_ref.at[me if sending else pc],
                                         send_sem=send_sems.at[k], recv_sem=recv_sems.at[k],
                                         device_id=(px, py, c), device_id_type=MESH)
            for k, (px, py, pc) in enumerate(_chip_peers(x, y))]


def _exchange_start(p, *, name):
    def body(p_ref, land_ref, send_sems, recv_sems, p_thru, land_thru, token):
        for cp in _exchange_peers_copies(p_ref, land_ref, send_sems, recv_sems, True):
            cp.start()
        token[...] = jnp.zeros_like(token)

    return pl.pallas_call(
        body, name=name,
        out_shape=(pltpu.SemaphoreType.DMA((3,)), pltpu.SemaphoreType.DMA((3,)), pltpu.HBM(p.shape, p.dtype),
                   pltpu.HBM(p.shape, p.dtype), jax.ShapeDtypeStruct((8, 128), F32)),
        in_specs=(HBM_SPEC, HBM_SPEC),
        out_specs=(SEM_SPEC, SEM_SPEC, HBM_SPEC, HBM_SPEC, pl.BlockSpec(memory_space=pltpu.VMEM)),
        input_output_aliases={0: 2, 1: 3},
        compiler_params=pltpu.CompilerParams(has_side_effects=SPLIT_EFFECT),
    )(pltpu.with_memory_space_constraint(p, pltpu.HBM),
      pltpu.with_memory_space_constraint(lax.empty(p.shape, p.dtype), pltpu.HBM))


def _exchange_wait(send_sems, recv_sems, p_thru, land_thru, after, *, name):
    def body(p_ref, land_ref, send_sems, recv_sems, after_ref, p_dead, got_ref):
        for cp in _exchange_peers_copies(p_ref, land_ref, send_sems, recv_sems, False):
            cp.wait_send()
            cp.wait_recv()

    return pl.pallas_call(
        body, name=name,
        out_shape=(pltpu.HBM(p_thru.shape, p_thru.dtype), pltpu.HBM(p_thru.shape, p_thru.dtype)),
        in_specs=(HBM_SPEC, HBM_SPEC, SEM_SPEC, SEM_SPEC, ANY), out_specs=(HBM_SPEC, HBM_SPEC),
        input_output_aliases={0: 0, 1: 1},
        compiler_params=pltpu.CompilerParams(has_side_effects=SPLIT_EFFECT),
    )(p_thru, land_thru, send_sems, recv_sems, after)


def _sum_parts(parts, own, *, name):
    k, r, c = parts.shape
    tr = _pick(r, (512, 256, 128))

    def body(p_ref, own_ref, o_ref):
        me = 2 * lax.axis_index("x") + lax.axis_index("y")
        acc = jnp.zeros((tr, c), F32)
        for i in range(k):
            acc = acc + jnp.where(me == i, own_ref[i], p_ref[i]).astype(F32)
        o_ref[...] = acc

    blk = pl.BlockSpec((k, tr, c), lambda i: (0, i, 0))
    return pl.pallas_call(
        body, name=name, grid=(r // tr,), in_specs=[blk, blk],
        out_specs=pl.BlockSpec((tr, c), lambda i: (i, 0)),
        out_shape=jax.ShapeDtypeStruct((r, c), F32),
        compiler_params=_cparams(("parallel",)),
    )(parts, own)


def _join_halves(gh, *, name):
    def body(g_ref, o_ref, send_sem, recv_sem):
        x, y, c = lax.axis_index("x"), lax.axis_index("y"), lax.axis_index("c")
        cp = pltpu.make_async_remote_copy(src_ref=g_ref, dst_ref=o_ref, send_sem=send_sem, recv_sem=recv_sem,
                                          device_id=(x, y, 1 - c), device_id_type=MESH)
        cp.start()
        cp.wait()

    other = pl.pallas_call(
        body, name=name, in_specs=[ANY], out_specs=ANY,
        out_shape=jax.ShapeDtypeStruct(gh.shape, F32),
        scratch_shapes=[pltpu.SemaphoreType.DMA, pltpu.SemaphoreType.DMA],
    )(gh)
    south = lax.axis_index("c") == 0
    return jnp.concatenate([jnp.where(south, gh, other), jnp.where(south, other, gh)], axis=0)


def _gather_small(s, *, name):
    rows = s.shape[0]

    def body(s_ref, o_ref, send_sems, recv_sems, local_sem):
        x, y, c = lax.axis_index("x"), lax.axis_index("y"), lax.axis_index("c")
        me = 4 * x + 2 * y + c
        mine = pltpu.make_async_copy(s_ref, o_ref.at[me], local_sem)
        mine.start()
        peers = []
        for k in range(1, 8):
            peers.append((1 - x if k & 4 else x, 1 - y if k & 2 else y, 1 - c if k & 1 else c))
        cps = [pltpu.make_async_remote_copy(src_ref=s_ref, dst_ref=o_ref.at[me], send_sem=send_sems.at[k],
                                            recv_sem=recv_sems.at[k], device_id=p, device_id_type=MESH)
               for k, p in enumerate(peers)]
        for cp in cps:
            cp.start()
        for k, (px, py, pc) in enumerate(peers):
            pltpu.make_async_remote_copy(src_ref=s_ref, dst_ref=o_ref.at[4 * px + 2 * py + pc],
                                         send_sem=send_sems.at[k], recv_sem=recv_sems.at[k],
                                         device_id=(px, py, pc), device_id_type=MESH).wait_recv()
        for cp in cps:
            cp.wait_send()
        mine.wait()

    return pl.pallas_call(
        body, name=name, in_specs=[ANY], out_specs=ANY,
        out_shape=jax.ShapeDtypeStruct((8, rows, 128), F32),
        scratch_shapes=[pltpu.SemaphoreType.DMA((7,)), pltpu.SemaphoreType.DMA((7,)), pltpu.SemaphoreType.DMA],
    )(s)


IN_SHARD = IN_WIDTH // N_CHIPS
IN_SHARD_PAD = 1536
UP_ROWS, DOWN_ROWS, OUT_ROWS = 1024, 1024, 512
REST_ROWS = UP_ROWS + DOWN_ROWS + OUT_ROWS


def _pack_in(w_in_s):
    return jnp.pad(w_in_s, ((0, 0), (0, IN_SHARD_PAD - IN_SHARD))).astype(BF16)


def _pack_rest(w_out_s, w_up_s, w_down_s):
    return jnp.concatenate([w_up_s, w_down_s, w_out_s], axis=0).astype(BF16)


def _unpack_rest(blob):
    return (blob[UP_ROWS + DOWN_ROWS:], blob[0:UP_ROWS], blob[UP_ROWS:UP_ROWS + DOWN_ROWS])


def _with_own(gathered, own):
    me = 2 * lax.axis_index("x") + lax.axis_index("y")
    return [jnp.where(me == j, own, gathered[j]) for j in range(N_CHIPS)]


def _full_w_in(g_in, own):
    return jnp.concatenate([s[:, :IN_SHARD] for s in _with_own(g_in, own)], axis=1)


def _full_rest(g_rest, own):
    parts = [_unpack_rest(s) for s in _with_own(g_rest, own)]
    w_out = jnp.concatenate([p[0] for p in parts], axis=0)
    w_up = jnp.concatenate([p[1] for p in parts], axis=1)
    w_down = jnp.concatenate([p[2] for p in parts], axis=0)
    return w_out, w_up, w_down


def _split_w_in(w_in):
    z_xbc = w_in[:, 0:2560]
    dt = w_in[:, 2560:2576]
    qkv = w_in[:, 2576:5648]
    f = w_in[:, 5648:5664]
    pad = jnp.zeros((w_in.shape[0], PA_WIDTH - 2592), w_in.dtype)
    return jnp.concatenate([z_xbc, dt, f, pad], axis=1), qkv


def _merge_w_in(d_a, d_qkv):
    return jnp.concatenate([d_a[:, 0:2560], d_a[:, 2560:2576], d_qkv, d_a[:, 2576:2592]], axis=1)


def _local_step(x3, target3, w_in, rest_weights, norm_mix_w, conv_w, conv_b, dt_bias, a_log, d_skip,
                ssd_norm_w, f_bias, norm_mlp_w, norm_final_w, first_after=None, early_grads=None, late_grads=None):
    bl, t, d = x3.shape
    n = bl * t
    x = x3.reshape(n, d)
    target = target3.reshape(n, d)
    w_a, w_qkv = _split_w_in(w_in)
    nfw = norm_final_w.reshape(1, d)
    dskip_e = jnp.repeat(d_skip, HEAD_DIM, axis=1)
    nb = t // ATT_BLOCK

    r1, r2, kt = min(n, 1024), min(n, 512), min(n, 512)
    if first_after is None:
        first_after = jnp.zeros((8, 128), F32)
    h0, rstd0, proj_a = _norm_mm(x, norm_mix_w, w_a, first_after, name="norm_mix_proj_a", tm=r2)
    qkv = _mm(h0, w_qkv, name="proj_qkv", tiles=(r2, QKV_WIDTH, D_MODEL), out_dtype=BF16)
    bias128 = jnp.concatenate([dt_bias, f_bias, jnp.zeros((1, 96), F32)], axis=1)
    alog128 = jnp.concatenate([a_log, jnp.zeros((1, 112), F32)], axis=1)
    dt, gate_d, acum, ccum = _prep(proj_a, bias128, alog128, bl, t)
    acum_t = acum[:, 0:16].T
    negc = jnp.pad(-ccum.reshape(bl, t, 8, 2).transpose(0, 2, 3, 1), ((0, 0), (0, 0), (0, 6), (0, 0)))
    xc = _conv_fwd(proj_a, conv_w, conv_b, bl, t)
    y_ssd, y_pre, hprev = _ssd_fwd(xc, proj_a, dt, acum, acum_t, dskip_e, ssd_norm_w, bl, t)
    y_att, lse = _attn_fwd(qkv, negc, bl, t)
    w_out, w_up, w_down = rest_weights(y_att)
    wo_s, wo_a = w_out[:SSD_WIDTH], w_out[SSD_WIDTH:]
    t1 = _mm(y_ssd, wo_s, name="out_proj_ssd", tiles=(r1, D_MODEL, SSD_WIDTH), res=x)
    h1, h1n, rstd1 = _mm_norm_fwd(y_att, wo_a, t1, norm_mlp_w, name="out_proj_att_norm_mlp", tm=r2)
    up = _mm(h1n, w_up, name="mlp_up", tiles=(r2, D_FF, D_MODEL), out_dtype=BF16)
    dh2, dh2b, loss, d_nfw = _mm_final(up, w_down, h1, nfw, target, name="mlp_down_final_norm_loss", tm=r2,
                                       a_act="relu2")

    dup = _mm(dh2b, w_down, name="mlp_down_bwd_act", tiles=(r2, D_FF, D_MODEL), tb=True, epi_up=up, out_dtype=BF16)
    rest_shape = (N_CHIPS, REST_ROWS, D_MODEL)
    gb_rest = _mm(up, dh2b, name="mlp_down_bwd_w", tiles=(DOWN_ROWS, D_MODEL, kt), ta=True, a_act="relu2",
                  out_dtype=BF16, into=(rest_shape, (None, DOWN_ROWS, D_MODEL), lambda i, j, k: (i, 1, 0), None))
    dh1, dh1b, d_nmlp = _mm_norm_bwd(dup, w_up, None, h1, rstd1, norm_mlp_w, dh2, name="mlp_up_bwd_act_norm_mlp",
                                     tm=r2)
    gb_rest = _mm(h1n, dup, name="mlp_up_bwd_w", tiles=(D_MODEL, UP_ROWS, kt), ta=True, out_dtype=BF16,
                  into=(rest_shape, (None, D_MODEL, UP_ROWS), lambda i, j, k: (j, 0, 0), gb_rest))
    dys = _mm(dh1b, wo_s, name="out_proj_bwd_ssd", tiles=(r1, SSD_WIDTH, D_MODEL), tb=True)
    do = _mm(dh1b, wo_a, name="out_proj_bwd_att", tiles=(r1, ATT_WIDTH, D_MODEL), tb=True, out_dtype=BF16)
    out_block = (UP_ROWS + DOWN_ROWS) // OUT_ROWS
    for half, (y_half, tag) in enumerate(((y_ssd, "ssd"), (y_att, "att"))):
        gb_rest = _mm(y_half, dh1b, name="out_proj_bwd_w_" + tag, tiles=(2 * OUT_ROWS, D_MODEL, kt), ta=True,
                      out_dtype=BF16, into=(rest_shape, (2, OUT_ROWS, D_MODEL),
                                            functools.partial(lambda i, j, k, h: (h, out_block, 0), h=half),
                                            gb_rest))
    token = jnp.zeros((8, 128), F32) if early_grads is None else early_grads(gb_rest)
    dq, dk, dv, dcb = _attn_bwd(qkv, do, y_att, lse, negc, token, bl, t)
    dc = jnp.pad(dcb[:, :, 0:2, :].transpose(0, 3, 1, 2).reshape(n, 16), ((0, 0), (16, 96)))
    dxc, dpa, ddt_raw, d_snw, d_dsk, d_alog, d_dtb = _ssd_bwd(dys, xc, proj_a, y_pre, hprev, dt, gate_d, acum,
                                                             acum_t, alog128, dskip_e, ssd_norm_w, bl, t)
    dpa, d_conv_w, d_conv_b = _conv_bwd(dxc, proj_a, conv_w, conv_b, dpa, bl, t)
    dproj_a, d_fb = _fpost(dc, gate_d, ddt_raw, dpa, bl, t)
    dqkv = jnp.concatenate([dq, dk, dv], axis=1)
    d_w_a = _mm(h0, dproj_a, name="proj_a_bwd_w", tiles=(1024, 896, kt), ta=True, out_dtype=BF16)
    d_w_qkv = _mm(h0, dqkv, name="proj_qkv_bwd_w", tiles=(1024, 1024, kt), ta=True, out_dtype=BF16)
    d_w_in = _merge_w_in(d_w_a, d_w_qkv)
    late_token = None if late_grads is None else late_grads(d_w_in)
    t2 = _mm(dproj_a, w_a, name="proj_a_bwd_act", tiles=(r1, D_MODEL, PA_WIDTH), tb=True, after=late_token)
    dx, _, d_nmix = _mm_norm_bwd(dqkv, w_qkv, t2, x, rstd0, norm_mix_w, dh1, name="proj_qkv_bwd_act_norm_mix",
                                 tm=r2)

    grads = dict(norm_mix_w=d_nmix, w_in=d_w_in, conv_w=d_conv_w, conv_b=d_conv_b,
                 dt_bias=d_dtb, a_log=d_alog, d_skip=d_dsk, ssd_norm_w=d_snw, f_bias=d_fb, rest=gb_rest,
                 norm_mlp_w=d_nmlp, norm_final_w=d_nfw)
    return dx.reshape(bl, t, d), loss, grads


SMALL_ORDER = ("norm_mix_w", "conv_w", "conv_b", "dt_bias", "a_log", "d_skip", "ssd_norm_w", "f_bias",
               "norm_mlp_w", "norm_final_w")
SMALL_SIZES = (1024, 4 * CONV_CH, CONV_CH, 16, 16, 16, 1024, 16, 1024, 1024)


def _pack_small(vals, rows):
    flat = jnp.concatenate([v.reshape(-1).astype(F32) for v in vals])
    return jnp.pad(flat, (0, rows * 128 - flat.shape[0])).reshape(rows, 128)


def _unpack_small(packed, sizes):
    flat = packed.reshape(-1)
    out, o = [], 0
    for s in sizes:
        out.append(flat[o:o + s])
        o += s
    return out


def kernel(x, norm_mix_w, w_in, conv_w, conv_b, dt_bias, a_log, d_skip, ssd_norm_w, f_bias, w_out, norm_mlp_w, w_up, w_down, norm_final_w, loss_target, m_norm_mix_w, m_w_in, m_conv_w, m_conv_b, m_dt_bias, m_a_log, m_d_skip, m_ssd_norm_w, m_f_bias, m_w_out, m_norm_mlp_w, m_w_up, m_w_down, m_norm_final_w, v_norm_mix_w, v_w_in, v_conv_w, v_conv_b, v_dt_bias, v_a_log, v_d_skip, v_ssd_norm_w, v_f_bias, v_w_out, v_norm_mlp_w, v_w_up, v_w_down, v_norm_final_w):
    chip = 2 * lax.axis_index("x") + lax.axis_index("y")
    cw = CONV_CH // N_CHIPS

    own_in = _pack_in(w_in[0])
    own_rest = _pack_rest(w_out[0], w_up[0], w_down[0])
    g_in = _gather_weights(own_in, name="gather_w_in")
    w_in_f = _full_w_in(g_in, own_in)
    *rest_handles, rest_token = _gather_start(own_rest, g_in, name="gather_start_rest")

    def rest_weights(after):
        _, landed = _gather_wait(*rest_handles, after, name="gather_wait_rest")
        return _full_rest(_gather_forward(landed, name="gather_forward_rest"), own_rest)
    small_all = _gather_small(_pack_small([conv_w[0]], 16), name="gather_conv_w")
    conv_w_f = jnp.concatenate([small_all[2 * j].reshape(-1)[:4 * cw].reshape(4, cw) for j in range(N_CHIPS)], axis=1)

    c = lax.axis_index("c")

    def chip_partial(gb, tag):
        half_rows = gb.shape[1] // 2
        from_sibling = _swap_halves(gb, name="grad_swap_halves_" + tag)
        my_half = lax.dynamic_slice_in_dim(gb, c * half_rows, half_rows, axis=1)
        return _add_pair(my_half, from_sibling, name="grad_add_sibling_" + tag)

    in_flight = {}

    def early_grads(gb_rest):
        part = chip_partial(gb_rest, "rest")
        *handles, token = _exchange_start(part, name="grad_exchange_start_rest")
        in_flight["rest"] = handles
        return token

    def late_grads(d_w_in):
        gb_in = jnp.stack([_pack_in(d_w_in[:, j * IN_SHARD:(j + 1) * IN_SHARD]) for j in range(N_CHIPS)])
        *handles, token = _exchange_start(chip_partial(gb_in, "in"), name="grad_exchange_start_in")
        in_flight["in"] = handles
        return token

    dx, loss_part, g = _local_step(x, loss_target, w_in_f, rest_weights, norm_mix_w, conv_w_f,
                                   conv_b, dt_bias, a_log, d_skip, ssd_norm_w, f_bias, norm_mlp_w, norm_final_w,
                                   first_after=rest_token, early_grads=early_grads, late_grads=late_grads)

    send_sems, recv_sems, part_rest, land_rest = in_flight["rest"]
    part_rest, parts_rest = _exchange_wait(send_sems, recv_sems, part_rest, land_rest, dx,
                                           name="grad_exchange_wait_rest")
    g_rest_half = _sum_parts(parts_rest, part_rest, name="grad_sum_chips_rest")
    g_w_out, g_w_up, g_w_down = _unpack_rest(_join_halves(g_rest_half, name="grad_join_halves_rest"))

    part_in, parts_in = _exchange_wait(*in_flight["in"], dx, name="grad_exchange_wait_in")
    g_in_half = _sum_parts(parts_in, part_in, name="grad_sum_chips_in")
    g_w_in = _join_halves(g_in_half, name="grad_join_halves_in")[:, :IN_SHARD]

    small_vals = [g[k] for k in SMALL_ORDER] + [loss_part[:, 0:1]]
    small_sum = _sum_leading(_gather_small(_pack_small(small_vals, SMALL_ROWS), name="gather_small_grads"), name="small_sum")
    sg = dict(zip(SMALL_ORDER + ("loss",), _unpack_small(small_sum, SMALL_SIZES + (1,))))
    loss = sg["loss"].reshape(())
    g_conv_full = sg["conv_w"].reshape(4, CONV_CH)
    g_conv = lax.dynamic_slice_in_dim(g_conv_full, chip * cw, cw, axis=1)

    grads = dict(norm_mix_w=sg["norm_mix_w"].reshape(1, -1), w_in=g_w_in[None], conv_w=g_conv[None],
                 conv_b=sg["conv_b"].reshape(1, -1), dt_bias=sg["dt_bias"].reshape(1, -1),
                 a_log=sg["a_log"].reshape(1, -1), d_skip=sg["d_skip"].reshape(1, -1),
                 ssd_norm_w=sg["ssd_norm_w"].reshape(1, -1), f_bias=sg["f_bias"].reshape(1, -1), w_out=g_w_out[None],
                 norm_mlp_w=sg["norm_mlp_w"].reshape(1, -1), w_up=g_w_up[None], w_down=g_w_down[None],
                 norm_final_w=sg["norm_final_w"])
    weights = dict(norm_mix_w=norm_mix_w, w_in=w_in, conv_w=conv_w, conv_b=conv_b, dt_bias=dt_bias, a_log=a_log,
                   d_skip=d_skip, ssd_norm_w=ssd_norm_w, f_bias=f_bias, w_out=w_out, norm_mlp_w=norm_mlp_w,
                   w_up=w_up, w_down=w_down, norm_final_w=norm_final_w)
    ms = dict(norm_mix_w=m_norm_mix_w, w_in=m_w_in, conv_w=m_conv_w, conv_b=m_conv_b, dt_bias=m_dt_bias,
              a_log=m_a_log, d_skip=m_d_skip, ssd_norm_w=m_ssd_norm_w, f_bias=m_f_bias, w_out=m_w_out,
              norm_mlp_w=m_norm_mlp_w, w_up=m_w_up, w_down=m_w_down, norm_final_w=m_norm_final_w)
    vs = dict(norm_mix_w=v_norm_mix_w, w_in=v_w_in, conv_w=v_conv_w, conv_b=v_conv_b, dt_bias=v_dt_bias,
              a_log=v_a_log, d_skip=v_d_skip, ssd_norm_w=v_ssd_norm_w, f_bias=v_f_bias, w_out=v_w_out,
              norm_mlp_w=v_norm_mlp_w, w_up=v_w_up, w_down=v_w_down, norm_final_w=v_norm_final_w)
    names = list(weights)
    big = ("w_in", "w_out", "w_up", "w_down")
    delta, new_m, new_v = {}, {}, {}
    for k, g2 in zip(big[1:], (g_w_out, g_w_up, g_w_down)):
        delta[k], new_m[k], new_v[k] = _adamw(weights[k], g2, ms[k], vs[k], name="adamw_" + k)
    g_in_t = g_w_in.T
    outs_t = _adamw(w_in[0].T, g_in_t, m_w_in[0].T, v_w_in[0].T, name="adamw_w_in")
    delta["w_in"], new_m["w_in"], new_v["w_in"] = [o.T[None] for o in outs_t]
    grads["w_in"] = g_in_t.T[None]
    smalls = [k for k in names if k not in big]
    sizes = [math.prod(weights[k].shape) for k in smalls]
    rows = -(-sum(sizes) // 1024) * 8
    packs = [_pack_small([d[k] for k in smalls], rows) for d in (weights, grads, ms, vs)]
    outs = _adamw(*packs, name="adamw_small")
    for o, dst in zip(outs, (delta, new_m, new_v)):
        for k, val in zip(smalls, _unpack_small(o, sizes)):
            dst[k] = val.reshape(weights[k].shape)
    return (loss, dx, *[grads[k] for k in names], *[delta[k] for k in names], *[new_m[k] for k in names],
            *[new_v[k] for k in names])
```

```python
import functools
import math

import jax
import jax.numpy as jnp
from jax import lax
from jax.experimental import pallas as pl
from jax.experimental.pallas import tpu as pltpu

F32 = jnp.float32
BF16 = jnp.bfloat16
HIGHEST = lax.Precision.HIGHEST
MESH = pl.DeviceIdType.MESH

D_MODEL = 1024
SSD_HEADS = 16
HEAD_DIM = 64
SSD_WIDTH = 1024
SSD_STATE = 128
CONV_CH = 1536
CHUNK = 128
ATT_WIDTH = 1024
EPS = 1e-5
IN_WIDTH = 5664
PA_WIDTH = 2688
QKV_WIDTH = 3072
D_FF = 4096
ATT_BLOCK = 256
NEG = -1e30
LOG2E = 1.4426950408889634
VMEM_LIMIT = 48 * 1024 * 1024

ADAM_LR = 0.001
ADAM_B1 = 0.9
ADAM_B2 = 0.999
ADAM_EPS = 1e-08
ADAM_WD = 0.01
ADAM_STEP = 10

N_CHIPS = 4
SMALL_ROWS = 96


def _cparams(sem):
    return pltpu.CompilerParams(dimension_semantics=sem, vmem_limit_bytes=VMEM_LIMIT)


def _pick(n, cands):
    for c in cands:
        if n % c == 0:
            return c
    return n


MM_CHUNK = 512


def _mm(a, b, *, name, tiles, ta=False, tb=False, out_dtype=F32, res=None, a_act=None, epi_up=None, after=None,
        into=None):
    n_unread = (after is not None) + (into is not None and into[3] is not None)
    if ta:
        K, M = a.shape
    else:
        M, K = a.shape
    if tb:
        N, K2 = b.shape
    else:
        K2, N = b.shape
    assert K == K2, (a.shape, b.shape)
    tm, tn, tk = tiles
    assert M % tm == 0 and N % tn == 0 and K % tk == 0, (name, M, N, K, tiles)
    nk = K // tk
    dn = (((0 if ta else 1,), (1 if tb else 0,)), ((), ()))
    has_res = res is not None
    has_up = epi_up is not None
    cn = _pick(tn, (MM_CHUNK, 384, 256, 128))

    def prologue(av):
        if a_act == "relu2":
            r = jnp.maximum(av.astype(F32), 0.0)
            av = r * r
        return av.astype(BF16)

    def epilogue(out, res_v, up_v):
        if has_res:
            out = out + res_v.astype(F32)
        if has_up:
            out = out * (2.0 * jnp.maximum(up_v.astype(F32), 0.0))
        return out.astype(out_dtype)

    def body(*refs):
        a_ref, b_ref = refs[0], refs[1]
        i = 2
        res_ref = up_ref = None
        if has_res:
            res_ref = refs[i]
            i += 1
        if has_up:
            up_ref = refs[i]
            i += 1
        i += n_unread
        o_ref = refs[i]
        if nk == 1:
            av = prologue(a_ref[...])
            for c in range(tn // cn):
                cs = slice(c * cn, (c + 1) * cn)
                bv = (b_ref[cs, :] if tb else b_ref[:, cs]).astype(BF16)
                out = lax.dot_general(av, bv, dn, preferred_element_type=F32)
                o_ref[:, cs] = epilogue(out, res_ref[:, cs] if has_res else None, up_ref[:, cs] if has_up else None)
            return
        acc_ref = refs[i + 1]
        k = pl.program_id(2)

        @pl.when(k == 0)
        def _():
            acc_ref[...] = jnp.zeros_like(acc_ref)

        acc_ref[...] += lax.dot_general(prologue(a_ref[...]), b_ref[...].astype(BF16), dn,
                                        preferred_element_type=F32)

        @pl.when(k == nk - 1)
        def _():
            out = epilogue(acc_ref[...], res_ref[...] if has_res else None, up_ref[...] if has_up else None)
            o_ref[...] = out.reshape(o_ref.shape)

    a_spec = pl.BlockSpec((tk, tm), lambda i, j, k: (k, i)) if ta else pl.BlockSpec((tm, tk), lambda i, j, k: (i, k))
    b_spec = pl.BlockSpec((tn, tk), lambda i, j, k: (j, k)) if tb else pl.BlockSpec((tk, tn), lambda i, j, k: (k, j))
    o_spec = pl.BlockSpec((tm, tn), lambda i, j, k: (i, j))
    ins, specs = [a, b], [a_spec, b_spec]
    if has_res:
        ins.append(res)
        specs.append(o_spec)
    if has_up:
        ins.append(epi_up)
        specs.append(o_spec)
    if after is not None:
        ins.append(after)
        specs.append(pl.BlockSpec(memory_space=pl.ANY))
    out_shape, out_spec, aliases = jax.ShapeDtypeStruct((M, N), out_dtype), o_spec, {}
    if into is not None:
        shape, block, index, buf = into
        out_shape, out_spec = jax.ShapeDtypeStruct(shape, out_dtype), pl.BlockSpec(block, index)
        if buf is not None:
            aliases = {len(ins): 0}
            ins.append(buf)
            specs.append(pl.BlockSpec(memory_space=pl.ANY))
    return pl.pallas_call(
        body, name=name, grid=(M // tm, N // tn, nk),
        in_specs=specs, out_specs=out_spec, out_shape=out_shape, input_output_aliases=aliases,
        scratch_shapes=[] if nk == 1 else [pltpu.VMEM((tm, tn), F32)],
        compiler_params=_cparams(("parallel", "parallel", "arbitrary")),
    )(*ins)


def _rows_product(a_ref, b_ref, tb, a_act):
    av = a_ref[...]
    if a_act == "relu2":
        r = jnp.maximum(av.astype(F32), 0.0)
        av = r * r
    dn = (((1,), (1 if tb else 0,)), ((), ()))
    return lax.dot_general(av.astype(BF16), b_ref[...].astype(BF16), dn, preferred_element_type=F32)


def _norm_mm(x, w, b, after, *, name, tm):
    m, d = x.shape
    n = b.shape[1]
    cn = _pick(n, (MM_CHUNK, 384, 256, 128))

    def body(x_ref, w_ref, b_ref, after_ref, h_ref, r_ref, o_ref):
        xv = x_ref[...]
        rstd = lax.rsqrt(jnp.mean(xv * xv, axis=1, keepdims=True) + EPS)
        hv = (xv * rstd * w_ref[...]).astype(BF16)
        h_ref[...] = hv
        r_ref[...] = rstd
        for c in range(n // cn):
            cs = slice(c * cn, (c + 1) * cn)
            o_ref[:, cs] = jnp.dot(hv, b_ref[:, cs].astype(BF16), preferred_element_type=F32)

    row = pl.BlockSpec((tm, d), lambda i: (i, 0))
    return pl.pallas_call(
        body, name=name, grid=(m // tm,),
        in_specs=[row, pl.BlockSpec((1, d), lambda i: (0, 0)), pl.BlockSpec((d, n), lambda i: (0, 0)),
                  pl.BlockSpec(memory_space=pl.ANY)],
        out_specs=[row, pl.BlockSpec((tm, 1), lambda i: (i, 0)), pl.BlockSpec((tm, n), lambda i: (i, 0))],
        out_shape=[jax.ShapeDtypeStruct((m, d), BF16), jax.ShapeDtypeStruct((m, 1), F32),
                   jax.ShapeDtypeStruct((m, n), F32)],
        compiler_params=_cparams(("parallel",)),
    )(x, w, b, after)


def _mm_norm_fwd(a1, b1, a2, b2, res, w, *, name, tm):
    m, k1 = a1.shape
    k2 = a2.shape[1]
    d = b1.shape[1]

    def body(a1_ref, b1_ref, a2_ref, b2_ref, res_ref, w_ref, h_ref, y_ref, r_ref):
        hv = _rows_product(a1_ref, b1_ref, False, None) + _rows_product(a2_ref, b2_ref, False, None) + res_ref[...]
        rstd = lax.rsqrt(jnp.mean(hv * hv, axis=1, keepdims=True) + EPS)
        h_ref[...] = hv
        y_ref[...] = (hv * rstd * w_ref[...]).astype(BF16)
        r_ref[...] = rstd

    row = pl.BlockSpec((tm, d), lambda i: (i, 0))
    return pl.pallas_call(
        body, name=name, grid=(m // tm,),
        in_specs=[pl.BlockSpec((tm, k1), lambda i: (i, 0)), pl.BlockSpec((k1, d), lambda i: (0, 0)),
                  pl.BlockSpec((tm, k2), lambda i: (i, 0)), pl.BlockSpec((k2, d), lambda i: (0, 0)), row,
                  pl.BlockSpec((1, d), lambda i: (0, 0))],
        out_specs=[row, row, pl.BlockSpec((tm, 1), lambda i: (i, 0))],
        out_shape=[jax.ShapeDtypeStruct((m, d), F32), jax.ShapeDtypeStruct((m, d), BF16),
                   jax.ShapeDtypeStruct((m, 1), F32)],
        compiler_params=_cparams(("parallel",)),
    )(a1, b1, a2, b2, res, w)


def _mm_final(a, b, res, w, target, *, name, tm, a_act):
    m, k = a.shape
    d = b.shape[1]

    def body(a_ref, b_ref, res_ref, w_ref, t_ref, dh_ref, dhb_ref, loss_ref, dw_ref):
        @pl.when(pl.program_id(0) == 0)
        def _():
            loss_ref[...] = jnp.zeros_like(loss_ref)
            dw_ref[...] = jnp.zeros_like(dw_ref)

        hv = _rows_product(a_ref, b_ref, False, a_act) + res_ref[...]
        wv = w_ref[...]
        rstd = lax.rsqrt(jnp.mean(hv * hv, axis=1, keepdims=True) + EPS)
        xhat = hv * rstd
        err = xhat * wv - t_ref[...]
        loss_ref[...] += 0.5 * jnp.sum(jnp.mean(err * err, axis=1, keepdims=True), axis=0, keepdims=True)
        dy = err * (1.0 / d)
        gw = dy * wv
        dh = rstd * (gw - xhat * jnp.mean(gw * xhat, axis=1, keepdims=True))
        dh_ref[...] = dh
        dhb_ref[...] = dh.astype(BF16)
        dw_ref[...] += jnp.sum(dy * xhat, axis=0, keepdims=True)

    row = pl.BlockSpec((tm, d), lambda i: (i, 0))
    vec = pl.BlockSpec((1, d), lambda i: (0, 0))
    return pl.pallas_call(
        body, name=name, grid=(m // tm,),
        in_specs=[pl.BlockSpec((tm, k), lambda i: (i, 0)), pl.BlockSpec((k, d), lambda i: (0, 0)), row, vec, row],
        out_specs=[row, row, pl.BlockSpec((1, 128), lambda i: (0, 0)), vec],
        out_shape=[jax.ShapeDtypeStruct((m, d), F32), jax.ShapeDtypeStruct((m, d), BF16),
                   jax.ShapeDtypeStruct((1, 128), F32), jax.ShapeDtypeStruct((1, d), F32)],
        compiler_params=_cparams(("arbitrary",)),
    )(a, b, res, w, target)


def _mm_two_halves(a, b, *, name, tm):
    m, k = a.shape
    d = b.shape[0] // 2

    def body(a_ref, b_ref, lo_ref, hi_ref):
        av = a_ref[...].astype(BF16)
        lo_ref[...] = lax.dot_general(av, b_ref[0:d, :].astype(BF16), NT_DIMS, preferred_element_type=F32)
        hi_ref[...] = lax.dot_general(av, b_ref[d:2 * d, :].astype(BF16), NT_DIMS,
                                      preferred_element_type=F32).astype(BF16)

    row = pl.BlockSpec((tm, d), lambda i: (i, 0))
    return pl.pallas_call(
        body, name=name, grid=(m // tm,),
        in_specs=[pl.BlockSpec((tm, k), lambda i: (i, 0)), pl.BlockSpec((2 * d, k), lambda i: (0, 0))],
        out_specs=[row, row],
        out_shape=[jax.ShapeDtypeStruct((m, d), F32), jax.ShapeDtypeStruct((m, d), BF16)],
        compiler_params=_cparams(("parallel",)),
    )(a, b)


def _mm_norm_bwd(pairs, x, rstd, w, dres, *, name, tm, after=None):
    m = pairs[0][0].shape[0]
    d = pairs[0][1].shape[0]
    n_pairs = len(pairs)

    def body(*refs):
        i = 2 * n_pairs
        x_ref, r_ref, w_ref, d_ref = refs[i:i + 4]
        dx_ref, dxb_ref, dw_ref = refs[-3:]

        @pl.when(pl.program_id(0) == 0)
        def _():
            dw_ref[...] = jnp.zeros_like(dw_ref)

        g = _rows_product(refs[0], refs[1], True, None)
        for p in range(1, n_pairs):
            g = g + _rows_product(refs[2 * p], refs[2 * p + 1], True, None)
        r = r_ref[...]
        xhat = x_ref[...] * r
        gw = g * w_ref[...]
        dx = d_ref[...] + r * (gw - xhat * jnp.mean(gw * xhat, axis=1, keepdims=True))
        dx_ref[...] = dx
        dxb_ref[...] = dx.astype(BF16)
        dw_ref[...] += jnp.sum(g * xhat, axis=0, keepdims=True)

    row = pl.BlockSpec((tm, d), lambda i: (i, 0))
    vec = pl.BlockSpec((1, d), lambda i: (0, 0))
    ins, specs = [], []
    for a, b in pairs:
        k = a.shape[1]
        ins += [a, b]
        specs += [pl.BlockSpec((tm, k), lambda i: (i, 0)), pl.BlockSpec((d, k), lambda i: (0, 0))]
    ins += [x, rstd, w, dres]
    specs += [row, pl.BlockSpec((tm, 1), lambda i: (i, 0)), vec, row]
    if after is not None:
        ins.append(after)
        specs.append(pl.BlockSpec(memory_space=pl.ANY))
    return pl.pallas_call(
        body, name=name, grid=(m // tm,), in_specs=specs, out_specs=[row, row, vec],
        out_shape=[jax.ShapeDtypeStruct((m, d), F32), jax.ShapeDtypeStruct((m, d), BF16),
                   jax.ShapeDtypeStruct((1, d), F32)],
        compiler_params=_cparams(("arbitrary",)),
    )(*ins)


def _softplus(x):
    return jnp.maximum(x, 0.0) + jnp.log(1.0 + jnp.exp(-jnp.abs(x)))


def _prep(proj_a, bias128, alog128, bl, t):
    n = bl * t
    nch = t // CHUNK
    col0 = (SSD_WIDTH + CONV_CH) // 128

    def body(p_ref, b_ref, al_ref, dt_ref, gd_ref, ac_ref, c_ref):
        row = lax.broadcasted_iota(jnp.int32, (CHUNK, CHUNK), 0)
        col = lax.broadcasted_iota(jnp.int32, (CHUNK, CHUNK), 1)
        tril = (row >= col).astype(F32)
        lane = lax.broadcasted_iota(jnp.int32, (1, 128), 1)
        head_lanes = lane < 16
        a_row = -jnp.exp(al_ref[...])
        carry = jnp.zeros((1, 128), F32)
        for ci in range(nch):
            rows = slice(ci * CHUNK, (ci + 1) * CHUNK)
            xv = p_ref[rows, :] + b_ref[...]
            sp = _softplus(xv)
            acum = jnp.dot(tril, a_row * sp, precision=HIGHEST, preferred_element_type=F32)
            c = jnp.dot(tril, -_softplus(-xv), precision=HIGHEST, preferred_element_type=F32) + carry
            carry = c[CHUNK - 1:CHUNK, :]
            dt_ref[rows, :] = jnp.where(head_lanes, sp, 0.0)
            gd_ref[rows, :] = jnp.where(head_lanes, jax.nn.sigmoid(xv),
                                        jnp.where(lane < 32, jax.nn.sigmoid(-xv), 0.0))
            ac_ref[rows, :] = jnp.where(head_lanes, acum, 0.0)
            c_ref[rows, :] = c[:, 16:32]

    o128 = pl.BlockSpec((t, 128), lambda b: (b, 0))
    v128 = pl.BlockSpec((1, 128), lambda b: (0, 0))
    w128 = jax.ShapeDtypeStruct((n, 128), F32)
    return pl.pallas_call(
        body, name="head_scalars", grid=(bl,),
        in_specs=[pl.BlockSpec((t, 128), lambda b: (b, col0)), v128, v128],
        out_specs=[o128, o128, o128, pl.BlockSpec((t, 16), lambda b: (b, 0))],
        out_shape=[w128, w128, w128, jax.ShapeDtypeStruct((n, 16), F32)],
        compiler_params=_cparams(("parallel",)),
    )(proj_a, bias128, alog128)


def _fpost(dc, gate_d, ddt, dpa, bl, t):
    n = bl * t
    nch = t // CHUNK
    col0 = (SSD_WIDTH + CONV_CH) // 128

    def body(dc_ref, gd_ref, ddt_ref, dpa_in, out_ref, db_ref):
        @pl.when(pl.program_id(0) == 0)
        def _():
            db_ref[...] = jnp.zeros_like(db_ref)

        row = lax.broadcasted_iota(jnp.int32, (CHUNK, CHUNK), 0)
        col = lax.broadcasted_iota(jnp.int32, (CHUNK, CHUNK), 1)
        triu = (row <= col).astype(F32)
        lane = lax.broadcasted_iota(jnp.int32, (1, 128), 1)
        gate_lanes = (lane >= 16) & (lane < 32)
        carry = jnp.zeros((1, 128), F32)
        db = jnp.zeros((1, 128), F32)
        for ci in reversed(range(nch)):
            rows = slice(ci * CHUNK, (ci + 1) * CHUNK)
            dlf = jnp.dot(triu, dc_ref[rows, :], precision=HIGHEST, preferred_element_type=F32) + carry
            carry = dlf[0:1, :]
            df = jnp.where(gate_lanes, dlf * gd_ref[rows, :], 0.0)
            out_ref[rows, :] = (ddt_ref[rows, :] + df).astype(BF16)
            db = db + jnp.sum(df, axis=0, keepdims=True)
        db_ref[...] += db[:, 16:32]

    blk = pl.BlockSpec((t, 128), lambda b: (b, 0))
    return pl.pallas_call(
        body, name="forget_gate_bwd", grid=(bl,),
        in_specs=[blk, blk, blk, ANY],
        out_specs=[pl.BlockSpec((t, 128), lambda b: (b, col0)), pl.BlockSpec((1, 16), lambda b: (0, 0))],
        out_shape=[jax.ShapeDtypeStruct(dpa.shape, dpa.dtype), jax.ShapeDtypeStruct((1, 16), F32)],
        input_output_aliases={3: 0},
        compiler_params=_cparams(("arbitrary",)),
    )(dc, gate_d, ddt, dpa)


CONV_TILE = 256
CONV_ROWS = 256


def _conv_taps(u_ref, i, w, bias):
    r0 = pl.multiple_of(i * CONV_ROWS, CONV_ROWS)
    cur = u_ref[pl.ds(r0, CONV_ROWS), :]
    p0 = pl.multiple_of(jnp.maximum(r0 - 8, 0), 8)
    prev = jnp.where(i > 0, u_ref[pl.ds(p0, 8), :], 0.0)
    cat = jnp.concatenate([prev, cur], axis=0)
    pre = bias + w[3:4, :] * cur
    taps = [cur]
    for s in (1, 2, 3):
        sh = pltpu.roll(cat, s, 0)[8:, :]
        taps.append(sh)
        pre = pre + w[3 - s:4 - s, :] * sh
    return r0, pre, taps


def _conv_fwd(proj_a, conv_w, conv_b, bl, t):
    n = bl * t
    nct = CONV_CH // CONV_TILE
    c0 = SSD_WIDTH // CONV_TILE

    def body(u_ref, w_ref, b_ref, o_ref):
        w = w_ref[...]
        bias = b_ref[...]

        def chunk(i, carry):
            r0, pre, _ = _conv_taps(u_ref, i, w, bias)
            o_ref[pl.ds(r0, CONV_ROWS), :] = pre * jax.nn.sigmoid(pre)
            return carry

        lax.fori_loop(0, t // CONV_ROWS, chunk, 0)

    return pl.pallas_call(
        body, name="conv_silu_fwd", grid=(bl, nct),
        in_specs=[pl.BlockSpec((t, CONV_TILE), lambda b, c: (b, c0 + c)),
                  pl.BlockSpec((4, CONV_TILE), lambda b, c: (0, c)),
                  pl.BlockSpec((1, CONV_TILE), lambda b, c: (0, c))],
        out_specs=pl.BlockSpec((t, CONV_TILE), lambda b, c: (b, c)),
        out_shape=jax.ShapeDtypeStruct((n, CONV_CH), F32),
        compiler_params=_cparams(("parallel", "parallel")),
    )(proj_a, conv_w, conv_b)


def _conv_bwd(dxc, proj_a, conv_w, conv_b, dpa, bl, t):
    nct = CONV_CH // CONV_TILE
    c0 = SSD_WIDTH // CONV_TILE
    nrc = t // CONV_ROWS

    def body(g_ref, u_ref, w_ref, b_ref, dpa_in, du_ref, dw_ref, db_ref, dp_scr):
        @pl.when(pl.program_id(1) == 0)
        def _():
            dw_ref[...] = jnp.zeros_like(dw_ref)
            db_ref[...] = jnp.zeros_like(db_ref)

        w = w_ref[...]
        bias = b_ref[...]
        dp_scr[pl.ds(t, 8), :] = jnp.zeros((8, CONV_TILE), F32)

        def chunk1(i, carry):
            dw0, dw1, dw2, dw3, db = carry
            r0, pre, taps = _conv_taps(u_ref, i, w, bias)
            sg = jax.nn.sigmoid(pre)
            dpre = g_ref[pl.ds(r0, CONV_ROWS), :] * (sg * (1.0 + pre * (1.0 - sg)))
            dp_scr[pl.ds(r0, CONV_ROWS), :] = dpre
            dw3 = dw3 + jnp.sum(dpre * taps[0], axis=0, keepdims=True)
            dw2 = dw2 + jnp.sum(dpre * taps[1], axis=0, keepdims=True)
            dw1 = dw1 + jnp.sum(dpre * taps[2], axis=0, keepdims=True)
            dw0 = dw0 + jnp.sum(dpre * taps[3], axis=0, keepdims=True)
            db = db + jnp.sum(dpre, axis=0, keepdims=True)
            return dw0, dw1, dw2, dw3, db

        z = jnp.zeros((1, CONV_TILE), F32)
        dw0, dw1, dw2, dw3, db = lax.fori_loop(0, nrc, chunk1, (z, z, z, z, z))
        dw_ref[...] += jnp.concatenate([dw0, dw1, dw2, dw3], axis=0)
        db_ref[...] += db

        def chunk2(i, carry):
            r0 = pl.multiple_of(i * CONV_ROWS, CONV_ROWS)
            cat = dp_scr[pl.ds(r0, CONV_ROWS + 8), :]
            du = w[3:4, :] * cat[:CONV_ROWS, :]
            for s in (1, 2, 3):
                du = du + w[3 - s:4 - s, :] * pltpu.roll(cat, CONV_ROWS + 8 - s, 0)[:CONV_ROWS, :]
            du_ref[pl.ds(r0, CONV_ROWS), :] = du.astype(BF16)
            return carry

        lax.fori_loop(0, nrc, chunk2, 0)

    return pl.pallas_call(
        body, name="conv_silu_bwd", grid=(nct, bl),
        in_specs=[pl.BlockSpec((t, CONV_TILE), lambda c, b: (b, c)),
                  pl.BlockSpec((t, CONV_TILE), lambda c, b: (b, c0 + c)),
                  pl.BlockSpec((4, CONV_TILE), lambda c, b: (0, c)),
                  pl.BlockSpec((1, CONV_TILE), lambda c, b: (0, c)), ANY],
        out_specs=[pl.BlockSpec((t, CONV_TILE), lambda c, b: (b, c0 + c)),
                   pl.BlockSpec((4, CONV_TILE), lambda c, b: (0, c)),
                   pl.BlockSpec((1, CONV_TILE), lambda c, b: (0, c))],
        out_shape=[jax.ShapeDtypeStruct(dpa.shape, dpa.dtype), jax.ShapeDtypeStruct((4, CONV_CH), F32),
                   jax.ShapeDtypeStruct((1, CONV_CH), F32)],
        input_output_aliases={4: 0},
        scratch_shapes=[pltpu.VMEM((t + 8, CONV_TILE), F32)],
        compiler_params=_cparams(("parallel", "arbitrary")),
    )(dxc, proj_a, conv_w, conv_b, dpa)


NT_DIMS = (((1,), (1,)), ((), ()))
TN_DIMS = (((0,), (0,)), ((), ()))


def _dot(a, b, dims=None):
    if dims is None:
        return jnp.dot(a, b, preferred_element_type=F32)
    return lax.dot_general(a, b, dims, preferred_element_type=F32)


def _head_expander():
    r = lax.broadcasted_iota(jnp.int32, (128, SSD_WIDTH), 0)
    c = lax.broadcasted_iota(jnp.int32, (128, SSD_WIDTH), 1)
    return ((c // HEAD_DIM == r % 16) & (r < 48)).astype(BF16)


def _spread(v128, expander):
    hi = v128.astype(BF16).astype(F32)
    r1 = v128 - hi
    mid = r1.astype(BF16).astype(F32)
    lo = (r1 - mid).astype(BF16).astype(F32)
    packed = (hi + pltpu.roll(mid, 16, 1) + pltpu.roll(lo, 32, 1)).astype(BF16)
    return jnp.dot(packed, expander, preferred_element_type=F32)


def _head_sums(v1024, expander):
    hi = v1024.astype(BF16)
    lo = (v1024 - hi.astype(F32)).astype(BF16)
    heads = jnp.where(lax.broadcasted_iota(jnp.int32, expander.shape, 0) < 16, expander, jnp.zeros_like(expander))
    return _dot(hi, heads, NT_DIMS) + _dot(lo, heads, NT_DIMS)


def _ssd_fwd(xc, proj_a, dt, acum, acum_t, dskip_e, norm_w, bl, t):
    n = bl * t
    nch = t // CHUNK
    L = CHUNK

    def body(xc_ref, z_ref, dt_ref, ac_ref, act_ref, dsk_ref, nw_ref, ys_ref, yp_ref, hp_ref, h_scr, y_scr, x_scr):
        @pl.when(pl.program_id(1) == 0)
        def _():
            h_scr[...] = jnp.zeros_like(h_scr)

        row = lax.broadcasted_iota(jnp.int32, (L, L), 0)
        col = lax.broadcasted_iota(jnp.int32, (L, L), 1)
        causal = row >= col
        expander = _head_expander()
        ac_all = ac_ref[...]
        act_all = act_ref[...]
        ac_e = _spread(ac_all, expander)
        e_in = jnp.exp(ac_e)
        dec = jnp.exp(ac_e[L - 1:L, :] - ac_e)
        xs_all = xc_ref[:, 0:SSD_WIDTH]
        x_all = xs_all * _spread(dt_ref[...], expander)
        x_scr[...] = x_all.astype(BF16)
        hp_all = h_scr[...]
        hp_ref[...] = hp_all
        for g in range(2):
            gs = slice(g * 512, (g + 1) * 512)
            bg = xc_ref[:, SSD_WIDTH + g * 128:SSD_WIDTH + (g + 1) * 128].astype(BF16)
            cg = xc_ref[:, SSD_WIDTH + 256 + g * 128:SSD_WIDTH + 256 + (g + 1) * 128].astype(BF16)
            gmat = _dot(cg, bg, NT_DIMS)
            y_scr[:, gs] = (_dot(cg, hp_all[gs, :].astype(BF16), NT_DIMS) * e_in[:, gs]
                            + dsk_ref[:, gs] * xs_all[:, gs])
            s_new = _dot((x_all[:, gs] * dec[:, gs]).astype(BF16), bg, TN_DIMS)
            for r in range(8):
                h = g * 8 + r
                sl = slice(h * HEAD_DIM, (h + 1) * HEAD_DIM)
                ldec = jnp.exp(jnp.where(causal, ac_all[:, h:h + 1] - act_all[h:h + 1, :], NEG))
                y_scr[:, sl] += _dot((gmat * ldec).astype(BF16), x_scr[:, sl])
                elast = jnp.exp(ac_all[L - 1:L, h:h + 1])
                h_scr[sl, :] = elast * hp_all[sl, :] + s_new[r * HEAD_DIM:(r + 1) * HEAD_DIM, :]
        y = y_scr[...]
        yp_ref[...] = y
        zv = z_ref[...]
        yg = y * (zv * jax.nn.sigmoid(zv))
        for g in range(2):
            gs = slice(g * 512, (g + 1) * 512)
            grp = yg[:, gs]
            rstd = lax.rsqrt(jnp.mean(grp * grp, axis=1, keepdims=True) + EPS)
            ys_ref[:, gs] = (grp * rstd * nw_ref[:, gs]).astype(BF16)

    rb = lambda b, c: (b * nch + c, 0)
    v1k = pl.BlockSpec((1, SSD_WIDTH), lambda b, c: (0, 0))
    return pl.pallas_call(
        body, name="ssd_fwd", grid=(bl, nch),
        in_specs=[pl.BlockSpec((L, CONV_CH), rb), pl.BlockSpec((L, SSD_WIDTH), rb),
                  pl.BlockSpec((L, 128), rb), pl.BlockSpec((L, 128), rb),
                  pl.BlockSpec((16, L), lambda b, c: (0, b * nch + c)), v1k, v1k],
        out_specs=[pl.BlockSpec((L, SSD_WIDTH), rb), pl.BlockSpec((L, SSD_WIDTH), rb),
                   pl.BlockSpec((None, SSD_WIDTH, SSD_STATE), lambda b, c: (b * nch + c, 0, 0))],
        out_shape=[jax.ShapeDtypeStruct((n, SSD_WIDTH), BF16), jax.ShapeDtypeStruct((n, SSD_WIDTH), F32),
                   jax.ShapeDtypeStruct((bl * nch, SSD_WIDTH, SSD_STATE), F32)],
        scratch_shapes=[pltpu.VMEM((SSD_WIDTH, SSD_STATE), F32), pltpu.VMEM((L, SSD_WIDTH), F32),
                        pltpu.VMEM((L, SSD_WIDTH), BF16)],
        compiler_params=_cparams(("parallel", "arbitrary")),
    )(xc, proj_a, dt, acum, acum_t, dskip_e, norm_w)


def _ssd_bwd(dys, xc, proj_a, ypre, hprev, dt, gate_d, acum, acum_t, alog128, dskip_e, norm_w, bl, t):
    n = bl * t
    nch = t // CHUNK
    L = CHUNK

    def body(dys_ref, xc_ref, z_ref, yp_ref, hp_ref, dt_ref, gd_ref, ac_ref, act_ref, al_ref, dsk_ref, nw_ref,
             dxc_ref, dz_ref, ddt_ref, dnw_ref, dsk16_ref, da16_ref, db16_ref,
             dh_scr, dy_scr, x_scr, dx_scr, red_scr):
        first = (pl.program_id(0) == 0) & (pl.program_id(1) == 0)

        @pl.when(first)
        def _():
            dnw_ref[...] = jnp.zeros_like(dnw_ref)
            dsk16_ref[...] = jnp.zeros_like(dsk16_ref)
            da16_ref[...] = jnp.zeros_like(da16_ref)
            db16_ref[...] = jnp.zeros_like(db16_ref)

        @pl.when(pl.program_id(1) == 0)
        def _():
            dh_scr[...] = jnp.zeros_like(dh_scr)

        y = yp_ref[...]
        zv = z_ref[...]
        sz = jax.nn.sigmoid(zv)
        gate = zv * sz
        yg = y * gate
        dout = dys_ref[...]
        nw = nw_ref[...]
        for g in range(2):
            gs = slice(g * 512, (g + 1) * 512)
            grp = yg[:, gs]
            rstd = lax.rsqrt(jnp.mean(grp * grp, axis=1, keepdims=True) + EPS)
            ghat = grp * rstd
            dnw_ref[:, gs] += jnp.sum(dout[:, gs] * ghat, axis=0, keepdims=True)
            gw = dout[:, gs] * nw[:, gs]
            dyg = rstd * (gw - ghat * jnp.mean(gw * ghat, axis=1, keepdims=True))
            dy_scr[:, gs] = dyg * gate[:, gs]
            dz_ref[:, gs] = (dyg * y[:, gs] * (sz[:, gs] * (1.0 + zv[:, gs] * (1.0 - sz[:, gs])))).astype(BF16)

        row = lax.broadcasted_iota(jnp.int32, (L, L), 0)
        col = lax.broadcasted_iota(jnp.int32, (L, L), 1)
        causal = row >= col
        lane128 = lax.broadcasted_iota(jnp.int32, (1, L), 1)
        rows128 = lax.broadcasted_iota(jnp.int32, (L, 1), 0)
        last_row = rows128 == (L - 1)
        expander = _head_expander()
        ac_all = ac_ref[...]
        act_all = act_ref[...]
        dt_all = dt_ref[...]
        dt_e = _spread(dt_all, expander)
        ac_e = _spread(ac_all, expander)
        e_in = jnp.exp(ac_e)
        dec = jnp.exp(ac_e[L - 1:L, :] - ac_e)
        xs_all = xc_ref[:, 0:SSD_WIDTH]
        x_all = xs_all * dt_e
        x_scr[...] = x_all.astype(BF16)
        dy_all = dy_scr[...]
        hp_all = hp_ref[...]
        ds_all = dh_scr[...]
        dsk_cols = jnp.sum(dy_all * xs_all, axis=0, keepdims=True)
        dac = jnp.zeros((L, L), F32)
        dac_row = jnp.zeros((L, L), F32)
        ddec_cols = []
        for g in range(2):
            gs = slice(g * 512, (g + 1) * 512)
            bsl = slice(SSD_WIDTH + g * 128, SSD_WIDTH + (g + 1) * 128)
            csl = slice(SSD_WIDTH + 256 + g * 128, SSD_WIDTH + 256 + (g + 1) * 128)
            bg = xc_ref[:, bsl].astype(BF16)
            cg = xc_ref[:, csl].astype(BF16)
            gmat = _dot(cg, bg, NT_DIMS)
            hpb = hp_all[gs, :].astype(BF16)
            dsb = ds_all[gs, :].astype(BF16)
            ch = _dot(cg, hpb, NT_DIMS)
            dye = dy_all[:, gs] * e_in[:, gs]
            dyeb = dye.astype(BF16)
            dc_acc = _dot(dyeb, hpb)
            dhp = _dot(dyeb, cg, TN_DIMS)
            dxd = _dot(bg, dsb, NT_DIMS)
            db_acc = _dot((x_all[:, gs] * dec[:, gs]).astype(BF16), dsb)
            ddec = dxd * x_all[:, gs] * dec[:, gs]
            ddec_cols.append(jnp.sum(ddec, axis=0, keepdims=True))
            dx_scr[:, gs] = dxd * dec[:, gs]
            red_scr[:, gs] = dye * ch - ddec
            dg_sum = jnp.zeros((L, L), F32)
            for r in range(8):
                h = g * 8 + r
                sl = slice(h * HEAD_DIM, (h + 1) * HEAD_DIM)
                onehot_w = lane128 == h
                ldec = jnp.exp(jnp.where(causal, ac_all[:, h:h + 1] - act_all[h:h + 1, :], NEG))
                mf = gmat * ldec
                dyb = dy_scr[:, sl].astype(BF16)
                dm = _dot(dyb, x_scr[:, sl], NT_DIMS)
                dx_scr[:, sl] += _dot(mf.astype(BF16), dyb, TN_DIMS)
                dg_sum = dg_sum + dm * ldec
                wmat = dm * mf
                elast = jnp.exp(ac_all[L - 1:L, h:h + 1])
                hp_h = hp_all[sl, :]
                ds_h = ds_all[sl, :]
                extra = elast * jnp.sum(jnp.sum(hp_h * ds_h, axis=1, keepdims=True), axis=0, keepdims=True)
                dac = dac + jnp.where(onehot_w, jnp.sum(wmat, axis=1, keepdims=True) + jnp.where(last_row, extra, 0.0),
                                      0.0)
                dac_row = dac_row + jnp.where(rows128 == h, -jnp.sum(wmat, axis=0, keepdims=True), 0.0)
                dh_scr[sl, :] = elast * ds_h + dhp[r * HEAD_DIM:(r + 1) * HEAD_DIM, :]
            dgb = dg_sum.astype(BF16)
            dxc_ref[:, csl] = dc_acc + _dot(dgb, bg)
            dxc_ref[:, bsl] = db_acc + _dot(dgb, cg, TN_DIMS)
        dx_all = dx_scr[...]
        dxc_ref[:, 0:SSD_WIDTH] = dx_all * dt_e + dsk_ref[...] * dy_all
        red = red_scr[...]
        dac_slab = _head_sums(red, expander)
        ddec_tot = _head_sums(jnp.broadcast_to(jnp.concatenate(ddec_cols, axis=1), (8, SSD_WIDTH)), expander)
        ddt_x = _head_sums(dx_all * xs_all, expander)
        dsk16_ref[...] += _head_sums(jnp.broadcast_to(dsk_cols, (8, SSD_WIDTH)), expander)[0:1, 0:16]
        dac = dac + dac_slab + jnp.transpose(dac_row) + jnp.where(last_row, ddec_tot[0:1, :], 0.0)
        triu = (row <= col).astype(F32)
        da = jnp.dot(triu, dac, precision=HIGHEST, preferred_element_type=F32)
        a_row = -jnp.exp(al_ref[...])
        ddt = jnp.where(lane128 < 16, (ddt_x + da * a_row) * gd_ref[...], 0.0)
        ddt_ref[...] = ddt
        da16_ref[...] += (jnp.sum(da * dt_all, axis=0, keepdims=True) * a_row)[:, 0:16]
        db16_ref[...] += jnp.sum(ddt, axis=0, keepdims=True)[:, 0:16]

    rb = lambda b, c: (b * nch + nch - 1 - c, 0)
    v1k = pl.BlockSpec((1, SSD_WIDTH), lambda b, c: (0, 0))
    v16 = pl.BlockSpec((1, 16), lambda b, c: (0, 0))
    v128 = pl.BlockSpec((1, 128), lambda b, c: (0, 0))
    wide = pl.BlockSpec((L, SSD_WIDTH), rb)
    s128 = pl.BlockSpec((L, 128), rb)
    return pl.pallas_call(
        body, name="ssd_bwd", grid=(bl, nch),
        in_specs=[wide, pl.BlockSpec((L, CONV_CH), rb), wide, wide,
                  pl.BlockSpec((None, SSD_WIDTH, SSD_STATE), lambda b, c: (b * nch + nch - 1 - c, 0, 0)),
                  s128, s128, s128, pl.BlockSpec((16, L), lambda b, c: (0, b * nch + nch - 1 - c)), v128, v1k, v1k],
        out_specs=[pl.BlockSpec((L, CONV_CH), rb), wide, s128, v1k, v16, v16, v16],
        out_shape=[jax.ShapeDtypeStruct((n, CONV_CH), F32), jax.ShapeDtypeStruct((n, PA_WIDTH), BF16),
                   jax.ShapeDtypeStruct((n, 128), F32), jax.ShapeDtypeStruct((1, SSD_WIDTH), F32),
                   jax.ShapeDtypeStruct((1, 16), F32), jax.ShapeDtypeStruct((1, 16), F32),
                   jax.ShapeDtypeStruct((1, 16), F32)],
        scratch_shapes=[pltpu.VMEM((SSD_WIDTH, SSD_STATE), F32), pltpu.VMEM((L, SSD_WIDTH), F32),
                        pltpu.VMEM((L, SSD_WIDTH), BF16), pltpu.VMEM((L, SSD_WIDTH), F32),
                        pltpu.VMEM((L, SSD_WIDTH), F32)],
        compiler_params=_cparams(("arbitrary", "arbitrary")),
    )(dys, xc, proj_a, ypre, hprev, dt, gate_d, acum, acum_t, alog128, dskip_e, norm_w)


def _attn_fwd(qkv, negc, bl, t):
    n = bl * t
    tb_ = ATT_BLOCK
    nb = t // tb_
    scale2 = LOG2E / math.sqrt(HEAD_DIM)

    def body(q_ref, k_ref, v_ref, c_ref, o_ref, lse_ref):
        row = lax.broadcasted_iota(jnp.int32, (tb_, tb_), 0)
        col = lax.broadcasted_iota(jnp.int32, (tb_, tb_), 1)
        causal = row >= col
        for qi in range(nb):
            r0, lk = qi * tb_, (qi + 1) * tb_
            for j in range(2):
                sl = slice(j * HEAD_DIM, (j + 1) * HEAD_DIM)
                s = _dot(q_ref[r0:lk, sl], k_ref[0:lk, sl], NT_DIMS) * scale2 + c_ref[j:j + 1, 0:lk] * LOG2E
                tail = jnp.where(causal, s[:, r0:lk], NEG)
                s = tail if qi == 0 else jnp.concatenate([s[:, 0:r0], tail], axis=1)
                m = jnp.max(s, axis=1, keepdims=True)
                p = jnp.exp2(s - m)
                l = jnp.sum(p, axis=1, keepdims=True)
                acc = _dot(p.astype(BF16), v_ref[0:lk, sl])
                o_ref[r0:lk, sl] = (acc / l).astype(BF16)
                lse_ref[r0:lk, sl] = jnp.broadcast_to(m + jnp.log(l) * LOG2E, (tb_, HEAD_DIM))

    blk = lambda off: pl.BlockSpec((t, 128), lambda b, hp: (b, off + hp))
    return pl.pallas_call(
        body, name="fox_attn_fwd", grid=(bl, 8),
        in_specs=[blk(0), blk(8), blk(16), pl.BlockSpec((None, None, 8, t), lambda b, hp: (b, hp, 0, 0))],
        out_specs=[blk(0), blk(0)],
        out_shape=[jax.ShapeDtypeStruct((n, ATT_WIDTH), BF16), jax.ShapeDtypeStruct((n, ATT_WIDTH), F32)],
        compiler_params=_cparams(("parallel", "parallel")),
    )(qkv, qkv, qkv, negc)


def _attn_bwd(qkv, do, o, lse, negc, after, bl, t):
    n = bl * t
    tb_ = ATT_BLOCK
    nb = t // tb_
    scale = 1.0 / math.sqrt(HEAD_DIM)
    scale2 = LOG2E * scale

    def body(q_ref, k_ref, v_ref, do_ref, o_ref, lse_ref, c_ref, after_ref, dq_ref, dk_ref, dv_ref, dc_ref,
             dq_scr, delta_scr, dr_scr, qt_scr, dot_scr, dkt_scr, dvt_scr):
        row = lax.broadcasted_iota(jnp.int32, (tb_, tb_), 0)
        col = lax.broadcasted_iota(jnp.int32, (tb_, tb_), 1)
        causal = row >= col
        dq_scr[...] = jnp.zeros_like(dq_scr)
        dr_scr[...] = jnp.zeros_like(dr_scr)
        dc_ref[...] = jnp.zeros_like(dc_ref)
        qt_scr[...] = jnp.transpose(q_ref[...].astype(F32)).astype(BF16)
        dot_scr[...] = jnp.transpose(do_ref[...].astype(F32)).astype(BF16)
        prod = do_ref[...].astype(F32) * o_ref[...].astype(F32)
        for j in range(2):
            sl = slice(j * HEAD_DIM, (j + 1) * HEAD_DIM)
            delta_scr[:, sl] = jnp.broadcast_to(jnp.sum(prod[:, sl], axis=1, keepdims=True), (t, HEAD_DIM))
        for kj in range(nb):
            r0, r1 = kj * tb_, (kj + 1) * tb_
            for j in range(2):
                sl = slice(j * HEAD_DIM, (j + 1) * HEAD_DIM)
                one = slice(j * HEAD_DIM, j * HEAD_DIM + 1)
                kb = k_ref[r0:r1, sl]
                qs = q_ref[r0:t, sl]
                dos = do_ref[r0:t, sl]
                s = _dot(qs, kb, NT_DIMS) * scale2 + c_ref[j:j + 1, r0:r1] * LOG2E
                head = jnp.where(causal, s[0:tb_, :], NEG)
                s = head if kj == nb - 1 else jnp.concatenate([head, s[tb_:, :]], axis=0)
                p = jnp.exp2(s - lse_ref[r0:t, one])
                dp = _dot(dos, v_ref[r0:r1, sl], NT_DIMS)
                ds = p * (dp - delta_scr[r0:t, one])
                dsb = ds.astype(BF16)
                dvt_scr[sl, r0:r1] = _dot(dot_scr[sl, r0:t], p.astype(BF16))
                dkt_scr[sl, r0:r1] = _dot(qt_scr[sl, r0:t], dsb)
                dq_scr[r0:t, sl] += _dot(dsb, kb)
                dr_scr[r0:t, sl] += jnp.broadcast_to(jnp.sum(ds, axis=1, keepdims=True), (t - r0, HEAD_DIM))
                dc_ref[j:j + 1, r0:r1] = -jnp.sum(ds, axis=0, keepdims=True)
        dq_ref[...] = (dq_scr[...] * scale).astype(BF16)
        dk_ref[...] = (jnp.transpose(dkt_scr[...]) * scale).astype(BF16)
        dv_ref[...] = jnp.transpose(dvt_scr[...]).astype(BF16)
        dr_t = jnp.transpose(dr_scr[...])
        for j in range(2):
            dc_ref[j:j + 1, :] += dr_t[j * HEAD_DIM:j * HEAD_DIM + 1, :]

    blk = lambda off: pl.BlockSpec((t, 128), lambda b, hp: (b, off + hp))
    cblk = pl.BlockSpec((None, None, 8, t), lambda b, hp: (b, hp, 0, 0))
    return pl.pallas_call(
        body, name="fox_attn_bwd", grid=(bl, 8),
        in_specs=[blk(0), blk(8), blk(16), blk(0), blk(0), blk(0), cblk, ANY],
        out_specs=[blk(0), blk(0), blk(0), cblk],
        out_shape=[jax.ShapeDtypeStruct((n, ATT_WIDTH), BF16)] * 3 + [jax.ShapeDtypeStruct((bl, 8, 8, t), F32)],
        scratch_shapes=[pltpu.VMEM((t, 128), F32), pltpu.VMEM((t, 128), F32), pltpu.VMEM((t, 128), F32),
                        pltpu.VMEM((128, t), BF16), pltpu.VMEM((128, t), BF16),
                        pltpu.VMEM((128, t), F32), pltpu.VMEM((128, t), F32)],
        compiler_params=_cparams(("parallel", "parallel")),
    )(qkv, qkv, qkv, do, o, lse, negc, after)


def _adamw(w, g, m, v, *, name):
    lead = w.ndim == 3
    r, c = w.shape[-2:]
    tr = _pick(r, (256, IN_SHARD // 3, 128, 64, 32, 16, 8))
    bc1 = 1.0 - ADAM_B1 ** ADAM_STEP
    bc2 = 1.0 - ADAM_B2 ** ADAM_STEP

    def body(w_ref, g_ref, m_ref, v_ref, d_ref, nm_ref, nv_ref):
        gv = g_ref[...]
        mn = ADAM_B1 * m_ref[...] + (1.0 - ADAM_B1) * gv
        vn = ADAM_B2 * v_ref[...] + (1.0 - ADAM_B2) * (gv * gv)
        m_hat = mn / bc1
        v_hat = vn / bc2
        d_ref[...] = -ADAM_LR * (m_hat / (jnp.sqrt(v_hat) + ADAM_EPS) + ADAM_WD * w_ref[...])
        nm_ref[...] = mn
        nv_ref[...] = vn

    flat = pl.BlockSpec((tr, c), lambda i: (i, 0))
    blk = pl.BlockSpec((None, tr, c), lambda i: (0, i, 0)) if lead else flat
    return pl.pallas_call(
        body, name=name, grid=(r // tr,), in_specs=[blk, flat, blk, blk], out_specs=[blk] * 3,
        out_shape=[jax.ShapeDtypeStruct(w.shape, F32)] * 3,
        compiler_params=_cparams(("parallel",)),
    )(w, g, m, v)


def _sum_leading(parts, *, name, out_dtype=F32):
    k, r, c = parts.shape
    tr = _pick(r, (512, 256, 128, 96, 64, 32, 16, 8))

    def body(p_ref, o_ref):
        acc = p_ref[0].astype(F32)
        for i in range(1, k):
            acc = acc + p_ref[i].astype(F32)
        o_ref[...] = acc.astype(out_dtype)

    return pl.pallas_call(
        body, name=name, grid=(r // tr,),
        in_specs=[pl.BlockSpec((k, tr, c), lambda i: (0, i, 0))],
        out_specs=pl.BlockSpec((tr, c), lambda i: (i, 0)),
        out_shape=jax.ShapeDtypeStruct((r, c), out_dtype),
        compiler_params=_cparams(("parallel",)),
    )(parts)


def _add_pair(a, b, *, name):
    k, r, c = a.shape
    tr = _pick(r, (512, 256, 128))

    def body(a_ref, b_ref, o_ref):
        o_ref[...] = (a_ref[...].astype(F32) + b_ref[...].astype(F32)).astype(BF16)

    blk = pl.BlockSpec((None, tr, c), lambda j, i: (j, i, 0))
    return pl.pallas_call(
        body, name=name, grid=(k, r // tr), in_specs=[blk, blk], out_specs=blk,
        out_shape=jax.ShapeDtypeStruct((k, r, c), BF16),
        compiler_params=_cparams(("parallel", "parallel")),
    )(a, b)


ANY = pl.BlockSpec(memory_space=pl.ANY)


def _chip_peers(x, y):
    return [(1 - x, y, 2 * (1 - x) + y), (x, 1 - y, 2 * x + 1 - y), (1 - x, 1 - y, 2 * (1 - x) + 1 - y)]


def _gather_weights(blob, *, name):
    rows, cols = blob.shape
    half_rows = rows // 2

    def body(b_ref, o_ref, send_sems, recv_sems):
        x, y, c = lax.axis_index("x"), lax.axis_index("y"), lax.axis_index("c")
        me = 2 * x + y
        sibling = (x, y, 1 - c)
        peers = _chip_peers(x, y)

        def half(chip, hc):
            return o_ref.at[chip, pl.ds(hc * half_rows, half_rows), :]

        def copy(k, src, chip, hc, to):
            return pltpu.make_async_remote_copy(src_ref=src, dst_ref=half(chip, hc), send_sem=send_sems.at[k],
                                                recv_sem=recv_sems.at[k], device_id=to, device_id_type=MESH)

        my_half = b_ref.at[pl.ds(c * half_rows, half_rows), :]
        first = [copy(k, my_half, me, c, (px, py, c)) for k, (px, py, _) in enumerate(peers)]
        for cp in first:
            cp.start()
        passed = [copy(3 + k, half(pc, c), pc, c, sibling) for k, (_, _, pc) in enumerate(peers)]
        for k, (px, py, pc) in enumerate(peers):
            copy(k, my_half, pc, c, (px, py, c)).wait_recv()
            passed[k].start()
        for k, (_, _, pc) in enumerate(peers):
            copy(3 + k, half(pc, 1 - c), pc, 1 - c, sibling).wait_recv()
        for cp in first + passed:
            cp.wait_send()

    return pl.pallas_call(
        body, name=name, in_specs=[ANY], out_specs=ANY,
        out_shape=jax.ShapeDtypeStruct((N_CHIPS, rows, cols), BF16),
        scratch_shapes=[pltpu.SemaphoreType.DMA((6,)), pltpu.SemaphoreType.DMA((6,))],
    )(blob)


def _swap_halves(g, *, name):
    _, rows, cols = g.shape
    half_rows = rows // 2

    def body(g_ref, o_ref, send_sem, recv_sem):
        x, y, c = lax.axis_index("x"), lax.axis_index("y"), lax.axis_index("c")
        cp = pltpu.make_async_remote_copy(
            src_ref=g_ref.at[:, pl.ds((1 - c) * half_rows, half_rows), :], dst_ref=o_ref,
            send_sem=send_sem, recv_sem=recv_sem, device_id=(x, y, 1 - c), device_id_type=MESH)
        cp.start()
        cp.wait()

    return pl.pallas_call(
        body, name=name, in_specs=[ANY], out_specs=ANY,
        out_shape=jax.ShapeDtypeStruct((N_CHIPS, half_rows, cols), BF16),
        scratch_shapes=[pltpu.SemaphoreType.DMA, pltpu.SemaphoreType.DMA],
    )(g)


HBM_SPEC = pl.BlockSpec(memory_space=pltpu.HBM)
SEM_SPEC = pl.BlockSpec(memory_space=pltpu.SEMAPHORE)
SPLIT_EFFECT = pltpu.SideEffectType.DATAFLOW_SIDE_EFFECTING


def _gather_peers_copies(b_ref, land_ref, send_sems, recv_sems, sending):
    x, y, c = lax.axis_index("x"), lax.axis_index("y"), lax.axis_index("c")
    me = 2 * x + y
    half_rows = b_ref.shape[0] // 2
    src = b_ref.at[pl.ds(c * half_rows, half_rows), :]
    return [pltpu.make_async_remote_copy(
        src_ref=src, dst_ref=land_ref.at[me if sending else pc, pl.ds(c * half_rows, half_rows), :],
        send_sem=send_sems.at[k], recv_sem=recv_sems.at[k], device_id=(px, py, c), device_id_type=MESH)
        for k, (px, py, pc) in enumerate(_chip_peers(x, y))]


def _gather_start(blob, after, *, name):
    shape = (N_CHIPS,) + blob.shape

    def body(b_ref, land_ref, after_ref, send_sems, recv_sems, b_thru, land_thru, token):
        for cp in _gather_peers_copies(b_ref, land_ref, send_sems, recv_sems, True):
            cp.start()
        token[...] = jnp.zeros_like(token)

    return pl.pallas_call(
        body, name=name,
        out_shape=(pltpu.SemaphoreType.DMA((3,)), pltpu.SemaphoreType.DMA((3,)), pltpu.HBM(blob.shape, blob.dtype),
                   pltpu.HBM(shape, blob.dtype), jax.ShapeDtypeStruct((8, 128), F32)),
        in_specs=(HBM_SPEC, HBM_SPEC, ANY),
        out_specs=(SEM_SPEC, SEM_SPEC, HBM_SPEC, HBM_SPEC, pl.BlockSpec(memory_space=pltpu.VMEM)),
        input_output_aliases={0: 2, 1: 3},
        compiler_params=pltpu.CompilerParams(has_side_effects=SPLIT_EFFECT),
    )(pltpu.with_memory_space_constraint(blob, pltpu.HBM),
      pltpu.with_memory_space_constraint(lax.empty(shape, blob.dtype), pltpu.HBM), after)


def _gather_wait(send_sems, recv_sems, b_thru, land_thru, after, *, name):
    def body(b_ref, land_ref, send_sems, recv_sems, after_ref, b_dead, got_ref):
        for cp in _gather_peers_copies(b_ref, land_ref, send_sems, recv_sems, False):
            cp.wait_send()
            cp.wait_recv()

    return pl.pallas_call(
        body, name=name,
        out_shape=(pltpu.HBM(b_thru.shape, b_thru.dtype), pltpu.HBM(land_thru.shape, land_thru.dtype)),
        in_specs=(HBM_SPEC, HBM_SPEC, SEM_SPEC, SEM_SPEC, ANY), out_specs=(HBM_SPEC, HBM_SPEC),
        input_output_aliases={0: 0, 1: 1},
        compiler_params=pltpu.CompilerParams(has_side_effects=SPLIT_EFFECT),
    )(b_thru, land_thru, send_sems, recv_sems, after)


def _gather_forward(land, *, name):
    half_rows = land.shape[1] // 2

    def body(l_ref, o_ref, send_sems, recv_sems):
        x, y, c = lax.axis_index("x"), lax.axis_index("y"), lax.axis_index("c")
        cps = []
        for k, (_, _, pc) in enumerate(_chip_peers(x, y)):
            mine = pl.ds(c * half_rows, half_rows)
            cps.append(pltpu.make_async_remote_copy(
                src_ref=l_ref.at[pc, mine, :], dst_ref=o_ref.at[pc, mine, :], send_sem=send_sems.at[k],
                recv_sem=recv_sems.at[k], device_id=(x, y, 1 - c), device_id_type=MESH))
        for cp in cps:
            cp.start()
        for k, (_, _, pc) in enumerate(_chip_peers(x, y)):
            theirs = pl.ds((1 - c) * half_rows, half_rows)
            pltpu.make_async_remote_copy(
                src_ref=l_ref.at[pc, theirs, :], dst_ref=o_ref.at[pc, theirs, :], send_sem=send_sems.at[k],
                recv_sem=recv_sems.at[k], device_id=(x, y, 1 - c), device_id_type=MESH).wait_recv()
        for cp in cps:
            cp.wait_send()

    return pl.pallas_call(
        body, name=name, in_specs=[ANY], out_specs=ANY, input_output_aliases={0: 0},
        out_shape=jax.ShapeDtypeStruct(land.shape, land.dtype),
        scratch_shapes=[pltpu.SemaphoreType.DMA((3,)), pltpu.SemaphoreType.DMA((3,))],
    )(land)


def _exchange_peers_copies(p_ref, land_ref, send_sems, recv_sems, sending):
    x, y, c = lax.axis_index("x"), lax.axis_index("y"), lax.axis_index("c")
    me = 2 * x + y
    return [pltpu.make_async_remote_copy(src_ref=p_ref.at[pc], dst_ref=land_ref.at[me if sending else pc],
                                         send_sem=send_sems.at[k], recv_sem=recv_sems.at[k],
                                         device_id=(px, py, c), device_id_type=MESH)
            for k, (px, py, pc) in enumerate(_chip_peers(x, y))]


def _exchange_start(p, *, name):
    def body(p_ref, land_ref, send_sems, recv_sems, p_thru, land_thru, token):
        for cp in _exchange_peers_copies(p_ref, land_ref, send_sems, recv_sems, True):
            cp.start()
        token[...] = jnp.zeros_like(token)

    return pl.pallas_call(
        body, name=name,
        out_shape=(pltpu.SemaphoreType.DMA((3,)), pltpu.SemaphoreType.DMA((3,)), pltpu.HBM(p.shape, p.dtype),
                   pltpu.HBM(p.shape, p.dtype), jax.ShapeDtypeStruct((8, 128), F32)),
        in_specs=(HBM_SPEC, HBM_SPEC),
        out_specs=(SEM_SPEC, SEM_SPEC, HBM_SPEC, HBM_SPEC, pl.BlockSpec(memory_space=pltpu.VMEM)),
        input_output_aliases={0: 2, 1: 3},
        compiler_params=pltpu.CompilerParams(has_side_effects=SPLIT_EFFECT),
    )(pltpu.with_memory_space_constraint(p, pltpu.HBM),
      pltpu.with_memory_space_constraint(lax.empty(p.shape, p.dtype), pltpu.HBM))


def _exchange_wait(send_sems, recv_sems, p_thru, land_thru, after, *, name):
    def body(p_ref, land_ref, send_sems, recv_sems, after_ref, p_dead, got_ref):
        for cp in _exchange_peers_copies(p_ref, land_ref, send_sems, recv_sems, False):
            cp.wait_send()
            cp.wait_recv()

    return pl.pallas_call(
        body, name=name,
        out_shape=(pltpu.HBM(p_thru.shape, p_thru.dtype), pltpu.HBM(p_thru.shape, p_thru.dtype)),
        in_specs=(HBM_SPEC, HBM_SPEC, SEM_SPEC, SEM_SPEC, ANY), out_specs=(HBM_SPEC, HBM_SPEC),
        input_output_aliases={0: 0, 1: 1},
        compiler_params=pltpu.CompilerParams(has_side_effects=SPLIT_EFFECT),
    )(p_thru, land_thru, send_sems, recv_sems, after)


def _sum_parts(parts, own, *, name):
    k, r, c = parts.shape
    tr = _pick(r, (512, 256, 128))

    def body(p_ref, own_ref, o_ref):
        me = 2 * lax.axis_index("x") + lax.axis_index("y")
        acc = jnp.zeros((tr, c), F32)
        for i in range(k):
            acc = acc + jnp.where(me == i, own_ref[i], p_ref[i]).astype(F32)
        o_ref[...] = acc

    blk = pl.BlockSpec((k, tr, c), lambda i: (0, i, 0))
    return pl.pallas_call(
        body, name=name, grid=(r // tr,), in_specs=[blk, blk],
        out_specs=pl.BlockSpec((tr, c), lambda i: (i, 0)),
        out_shape=jax.ShapeDtypeStruct((r, c), F32),
        compiler_params=_cparams(("parallel",)),
    )(parts, own)


def _join_halves(gh, *, name):
    def body(g_ref, o_ref, send_sem, recv_sem):
        x, y, c = lax.axis_index("x"), lax.axis_index("y"), lax.axis_index("c")
        cp = pltpu.make_async_remote_copy(src_ref=g_ref, dst_ref=o_ref, send_sem=send_sem, recv_sem=recv_sem,
                                          device_id=(x, y, 1 - c), device_id_type=MESH)
        cp.start()
        cp.wait()

    other = pl.pallas_call(
        body, name=name, in_specs=[ANY], out_specs=ANY,
        out_shape=jax.ShapeDtypeStruct(gh.shape, F32),
        scratch_shapes=[pltpu.SemaphoreType.DMA, pltpu.SemaphoreType.DMA],
    )(gh)
    south = lax.axis_index("c") == 0
    return jnp.concatenate([jnp.where(south, gh, other), jnp.where(south, other, gh)], axis=0)


def _gather_small(s, *, name):
    rows = s.shape[0]

    def body(s_ref, o_ref, send_sems, recv_sems, local_sem):
        x, y, c = lax.axis_index("x"), lax.axis_index("y"), lax.axis_index("c")
        me = 4 * x + 2 * y + c
        mine = pltpu.make_async_copy(s_ref, o_ref.at[me], local_sem)
        mine.start()
        peers = []
        for k in range(1, 8):
            peers.append((1 - x if k & 4 else x, 1 - y if k & 2 else y, 1 - c if k & 1 else c))
        cps = [pltpu.make_async_remote_copy(src_ref=s_ref, dst_ref=o_ref.at[me], send_sem=send_sems.at[k],
                                            recv_sem=recv_sems.at[k], device_id=p, device_id_type=MESH)
               for k, p in enumerate(peers)]
        for cp in cps:
            cp.start()
        for k, (px, py, pc) in enumerate(peers):
            pltpu.make_async_remote_copy(src_ref=s_ref, dst_ref=o_ref.at[4 * px + 2 * py + pc],
                                         send_sem=send_sems.at[k], recv_sem=recv_sems.at[k],
                                         device_id=(px, py, pc), device_id_type=MESH).wait_recv()
        for cp in cps:
            cp.wait_send()
        mine.wait()

    return pl.pallas_call(
        body, name=name, in_specs=[ANY], out_specs=ANY,
        out_shape=jax.ShapeDtypeStruct((8, rows, 128), F32),
        scratch_shapes=[pltpu.SemaphoreType.DMA((7,)), pltpu.SemaphoreType.DMA((7,)), pltpu.SemaphoreType.DMA],
    )(s)


IN_SHARD = IN_WIDTH // N_CHIPS
IN_SHARD_PAD = 1536
UP_ROWS, DOWN_ROWS, OUT_ROWS = 1024, 1024, 512
REST_ROWS = UP_ROWS + DOWN_ROWS + OUT_ROWS


def _pack_in(w_in_s):
    return jnp.pad(w_in_s, ((0, 0), (0, IN_SHARD_PAD - IN_SHARD))).astype(BF16)


def _pack_rest(w_out_s, w_up_s, w_down_s):
    return jnp.concatenate([w_up_s, w_down_s, w_out_s], axis=0).astype(BF16)


def _unpack_rest(blob):
    return (blob[UP_ROWS + DOWN_ROWS:], blob[0:UP_ROWS], blob[UP_ROWS:UP_ROWS + DOWN_ROWS])


def _with_own(gathered, own):
    me = 2 * lax.axis_index("x") + lax.axis_index("y")
    return [jnp.where(me == j, own, gathered[j]) for j in range(N_CHIPS)]


def _full_w_in(g_in, own):
    return jnp.concatenate([s[:, :IN_SHARD] for s in _with_own(g_in, own)], axis=1)


def _full_rest(g_rest, own):
    parts = [_unpack_rest(s) for s in _with_own(g_rest, own)]
    w_out = jnp.concatenate([p[0] for p in parts], axis=0)
    w_up = jnp.concatenate([p[1] for p in parts], axis=1)
    w_down = jnp.concatenate([p[2] for p in parts], axis=0)
    return w_out, w_up, w_down


def _split_w_in(w_in):
    z_xbc = w_in[:, 0:2560]
    dt = w_in[:, 2560:2576]
    qkv = w_in[:, 2576:5648]
    f = w_in[:, 5648:5664]
    pad = jnp.zeros((w_in.shape[0], PA_WIDTH - 2592), w_in.dtype)
    return jnp.concatenate([z_xbc, dt, f, pad], axis=1), qkv


def _merge_w_in(d_a, d_qkv):
    return jnp.concatenate([d_a[:, 0:2560], d_a[:, 2560:2576], d_qkv, d_a[:, 2576:2592]], axis=1)


def _local_step(x3, target3, w_in, rest_weights, norm_mix_w, conv_w, conv_b, dt_bias, a_log, d_skip,
                ssd_norm_w, f_bias, norm_mlp_w, norm_final_w, first_after=None, early_grads=None, late_grads=None):
    bl, t, d = x3.shape
    n = bl * t
    x = x3.reshape(n, d)
    target = target3.reshape(n, d)
    w_a, w_qkv = _split_w_in(w_in)
    nfw = norm_final_w.reshape(1, d)
    dskip_e = jnp.repeat(d_skip, HEAD_DIM, axis=1)
    nb = t // ATT_BLOCK

    r1, r2, kt = min(n, 1024), min(n, 512), min(n, 512)
    if first_after is None:
        first_after = jnp.zeros((8, 128), F32)
    h0, rstd0, proj_a = _norm_mm(x, norm_mix_w, w_a, first_after, name="norm_mix_proj_a", tm=r2)
    qkv = _mm(h0, w_qkv, name="proj_qkv", tiles=(r2, QKV_WIDTH, D_MODEL), out_dtype=BF16)
    bias128 = jnp.concatenate([dt_bias, f_bias, jnp.zeros((1, 96), F32)], axis=1)
    alog128 = jnp.concatenate([a_log, jnp.zeros((1, 112), F32)], axis=1)
    dt, gate_d, acum, ccum = _prep(proj_a, bias128, alog128, bl, t)
    acum_t = acum[:, 0:16].T
    negc = jnp.pad(-ccum.reshape(bl, t, 8, 2).transpose(0, 2, 3, 1), ((0, 0), (0, 0), (0, 6), (0, 0)))
    xc = _conv_fwd(proj_a, conv_w, conv_b, bl, t)
    y_ssd, y_pre, hprev = _ssd_fwd(xc, proj_a, dt, acum, acum_t, dskip_e, ssd_norm_w, bl, t)
    y_att, lse = _attn_fwd(qkv, negc, bl, t)
    w_out, w_up, w_down = rest_weights(y_att)
    wo_s, wo_a = w_out[:SSD_WIDTH], w_out[SSD_WIDTH:]
    h1, h1n, rstd1 = _mm_norm_fwd(y_ssd, wo_s, y_att, wo_a, x, norm_mlp_w, name="out_proj_norm_mlp", tm=r2)
    up = _mm(h1n, w_up, name="mlp_up", tiles=(r2, D_FF, D_MODEL), out_dtype=BF16)
    dh2, dh2b, loss, d_nfw = _mm_final(up, w_down, h1, nfw, target, name="mlp_down_final_norm_loss", tm=r2,
                                       a_act="relu2")

    dup = _mm(dh2b, w_down, name="mlp_down_bwd_act", tiles=(r2, D_FF, D_MODEL), tb=True, epi_up=up, out_dtype=BF16)
    rest_shape = (N_CHIPS, REST_ROWS, D_MODEL)
    gb_rest = _mm(up, dh2b, name="mlp_down_bwd_w", tiles=(DOWN_ROWS, D_MODEL, kt), ta=True, a_act="relu2",
                  out_dtype=BF16, into=(rest_shape, (None, DOWN_ROWS, D_MODEL), lambda i, j, k: (i, 1, 0), None))
    dh1, dh1b, d_nmlp = _mm_norm_bwd([(dup, w_up)], h1, rstd1, norm_mlp_w, dh2, name="mlp_up_bwd_act_norm_mlp",
                                     tm=r2)
    gb_rest = _mm(h1n, dup, name="mlp_up_bwd_w", tiles=(D_MODEL, UP_ROWS, kt), ta=True, out_dtype=BF16,
                  into=(rest_shape, (None, D_MODEL, UP_ROWS), lambda i, j, k: (j, 0, 0), gb_rest))
    dys, do = _mm_two_halves(dh1b, w_out, name="out_proj_bwd_act", tm=r1)
    out_block = (UP_ROWS + DOWN_ROWS) // OUT_ROWS
    for half, (y_half, tag) in enumerate(((y_ssd, "ssd"), (y_att, "att"))):
        gb_rest = _mm(y_half, dh1b, name="out_proj_bwd_w_" + tag, tiles=(2 * OUT_ROWS, D_MODEL, kt), ta=True,
                      out_dtype=BF16, into=(rest_shape, (2, OUT_ROWS, D_MODEL),
                                            functools.partial(lambda i, j, k, h: (h, out_block, 0), h=half),
                                            gb_rest))
    token = jnp.zeros((8, 128), F32) if early_grads is None else early_grads(gb_rest)
    dq, dk, dv, dcb = _attn_bwd(qkv, do, y_att, lse, negc, token, bl, t)
    dc = jnp.pad(dcb[:, :, 0:2, :].transpose(0, 3, 1, 2).reshape(n, 16), ((0, 0), (16, 96)))
    dxc, dpa, ddt_raw, d_snw, d_dsk, d_alog, d_dtb = _ssd_bwd(dys, xc, proj_a, y_pre, hprev, dt, gate_d, acum,
                                                             acum_t, alog128, dskip_e, ssd_norm_w, bl, t)
    dpa, d_conv_w, d_conv_b = _conv_bwd(dxc, proj_a, conv_w, conv_b, dpa, bl, t)
    dproj_a, d_fb = _fpost(dc, gate_d, ddt_raw, dpa, bl, t)
    dqkv = jnp.concatenate([dq, dk, dv], axis=1)
    d_w_a = _mm(h0, dproj_a, name="proj_a_bwd_w", tiles=(1024, 896, kt), ta=True, out_dtype=BF16)
    d_w_qkv = _mm(h0, dqkv, name="proj_qkv_bwd_w", tiles=(1024, 1024, kt), ta=True, out_dtype=BF16)
    d_w_in = _merge_w_in(d_w_a, d_w_qkv)
    late_token = None if late_grads is None else late_grads(d_w_in)
    dx, _, d_nmix = _mm_norm_bwd([(dproj_a, w_a), (dqkv, w_qkv)], x, rstd0, norm_mix_w, dh1,
                                 name="proj_bwd_act_norm_mix", tm=min(n, 256), after=late_token)

    grads = dict(norm_mix_w=d_nmix, w_in=d_w_in, conv_w=d_conv_w, conv_b=d_conv_b,
                 dt_bias=d_dtb, a_log=d_alog, d_skip=d_dsk, ssd_norm_w=d_snw, f_bias=d_fb, rest=gb_rest,
                 norm_mlp_w=d_nmlp, norm_final_w=d_nfw)
    return dx.reshape(bl, t, d), loss, grads


SMALL_ORDER = ("norm_mix_w", "conv_w", "conv_b", "dt_bias", "a_log", "d_skip", "ssd_norm_w", "f_bias",
               "norm_mlp_w", "norm_final_w")
SMALL_SIZES = (1024, 4 * CONV_CH, CONV_CH, 16, 16, 16, 1024, 16, 1024, 1024)


def _pack_small(vals, rows):
    flat = jnp.concatenate([v.reshape(-1).astype(F32) for v in vals])
    return jnp.pad(flat, (0, rows * 128 - flat.shape[0])).reshape(rows, 128)


def _unpack_small(packed, sizes):
    flat = packed.reshape(-1)
    out, o = [], 0
    for s in sizes:
        out.append(flat[o:o + s])
        o += s
    return out


def kernel(x, norm_mix_w, w_in, conv_w, conv_b, dt_bias, a_log, d_skip, ssd_norm_w, f_bias, w_out, norm_mlp_w, w_up, w_down, norm_final_w, loss_target, m_norm_mix_w, m_w_in, m_conv_w, m_conv_b, m_dt_bias, m_a_log, m_d_skip, m_ssd_norm_w, m_f_bias, m_w_out, m_norm_mlp_w, m_w_up, m_w_down, m_norm_final_w, v_norm_mix_w, v_w_in, v_conv_w, v_conv_b, v_dt_bias, v_a_log, v_d_skip, v_ssd_norm_w, v_f_bias, v_w_out, v_norm_mlp_w, v_w_up, v_w_down, v_norm_final_w):
    chip = 2 * lax.axis_index("x") + lax.axis_index("y")
    cw = CONV_CH // N_CHIPS

    own_in = _pack_in(w_in[0])
    own_rest = _pack_rest(w_out[0], w_up[0], w_down[0])
    g_in = _gather_weights(own_in, name="gather_w_in")
    w_in_f = _full_w_in(g_in, own_in)
    *rest_handles, rest_token = _gather_start(own_rest, g_in, name="gather_start_rest")

    def rest_weights(after):
        _, landed = _gather_wait(*rest_handles, after, name="gather_wait_rest")
        return _full_rest(_gather_forward(landed, name="gather_forward_rest"), own_rest)
    small_all = _gather_small(_pack_small([conv_w[0]], 16), name="gather_conv_w")
    conv_w_f = jnp.concatenate([small_all[2 * j].reshape(-1)[:4 * cw].reshape(4, cw) for j in range(N_CHIPS)], axis=1)

    c = lax.axis_index("c")

    def chip_partial(gb, tag):
        half_rows = gb.shape[1] // 2
        from_sibling = _swap_halves(gb, name="grad_swap_halves_" + tag)
        my_half = lax.dynamic_slice_in_dim(gb, c * half_rows, half_rows, axis=1)
        return _add_pair(my_half, from_sibling, name="grad_add_sibling_" + tag)

    in_flight = {}

    def early_grads(gb_rest):
        part = chip_partial(gb_rest, "rest")
        *handles, token = _exchange_start(part, name="grad_exchange_start_rest")
        in_flight["rest"] = handles
        return token

    def late_grads(d_w_in):
        gb_in = jnp.stack([_pack_in(d_w_in[:, j * IN_SHARD:(j + 1) * IN_SHARD]) for j in range(N_CHIPS)])
        *handles, token = _exchange_start(chip_partial(gb_in, "in"), name="grad_exchange_start_in")
        in_flight["in"] = handles
        return token

    dx, loss_part, g = _local_step(x, loss_target, w_in_f, rest_weights, norm_mix_w, conv_w_f,
                                   conv_b, dt_bias, a_log, d_skip, ssd_norm_w, f_bias, norm_mlp_w, norm_final_w,
                                   first_after=rest_token, early_grads=early_grads, late_grads=late_grads)

    send_sems, recv_sems, part_rest, land_rest = in_flight["rest"]
    part_rest, parts_rest = _exchange_wait(send_sems, recv_sems, part_rest, land_rest, dx,
                                           name="grad_exchange_wait_rest")
    g_rest_half = _sum_parts(parts_rest, part_rest, name="grad_sum_chips_rest")
    g_w_out, g_w_up, g_w_down = _unpack_rest(_join_halves(g_rest_half, name="grad_join_halves_rest"))

    part_in, parts_in = _exchange_wait(*in_flight["in"], dx, name="grad_exchange_wait_in")
    g_in_half = _sum_parts(parts_in, part_in, name="grad_sum_chips_in")
    g_w_in = _join_halves(g_in_half, name="grad_join_halves_in")[:, :IN_SHARD]

    small_vals = [g[k] for k in SMALL_ORDER] + [loss_part[:, 0:1]]
    small_sum = _sum_leading(_gather_small(_pack_small(small_vals, SMALL_ROWS), name="gather_small_grads"), name="small_sum")
    sg = dict(zip(SMALL_ORDER + ("loss",), _unpack_small(small_sum, SMALL_SIZES + (1,))))
    loss = sg["loss"].reshape(())
    g_conv_full = sg["conv_w"].reshape(4, CONV_CH)
    g_conv = lax.dynamic_slice_in_dim(g_conv_full, chip * cw, cw, axis=1)

    grads = dict(norm_mix_w=sg["norm_mix_w"].reshape(1, -1), w_in=g_w_in[None], conv_w=g_conv[None],
                 conv_b=sg["conv_b"].reshape(1, -1), dt_bias=sg["dt_bias"].reshape(1, -1),
                 a_log=sg["a_log"].reshape(1, -1), d_skip=sg["d_skip"].reshape(1, -1),
                 ssd_norm_w=sg["ssd_norm_w"].reshape(1, -1), f_bias=sg["f_bias"].reshape(1, -1), w_out=g_w_out[None],
                 norm_mlp_w=sg["norm_mlp_w"].reshape(1, -1), w_up=g_w_up[None], w_down=g_w_down[None],
                 norm_final_w=sg["norm_final_w"])
    weights = dict(norm_mix_w=norm_mix_w, w_in=w_in, conv_w=conv_w, conv_b=conv_b, dt_bias=dt_bias, a_log=a_log,
                   d_skip=d_skip, ssd_norm_w=ssd_norm_w, f_bias=f_bias, w_out=w_out, norm_mlp_w=norm_mlp_w,
                   w_up=w_up, w_down=w_down, norm_final_w=norm_final_w)
    ms = dict(norm_mix_w=m_norm_mix_w, w_in=m_w_in, conv_w=m_conv_w, conv_b=m_conv_b, dt_bias=m_dt_bias,
              a_log=m_a_log, d_skip=m_d_skip, ssd_norm_w=m_ssd_norm_w, f_bias=m_f_bias, w_out=m_w_out,
              norm_mlp_w=m_norm_mlp_w, w_up=m_w_up, w_down=m_w_down, norm_final_w=m_norm_final_w)
    vs = dict(norm_mix_w=v_norm_mix_w, w_in=v_w_in, conv_w=v_conv_w, conv_b=v_conv_b, dt_bias=v_dt_bias,
              a_log=v_a_log, d_skip=v_d_skip, ssd_norm_w=v_ssd_norm_w, f_bias=v_f_bias, w_out=v_w_out,
              norm_mlp_w=v_norm_mlp_w, w_up=v_w_up, w_down=v_w_down, norm_final_w=v_norm_final_w)
    names = list(weights)
    big = ("w_in", "w_out", "w_up", "w_down")
    delta, new_m, new_v = {}, {}, {}
    for k, g2 in zip(big[1:], (g_w_out, g_w_up, g_w_down)):
        delta[k], new_m[k], new_v[k] = _adamw(weights[k], g2, ms[k], vs[k], name="adamw_" + k)
    g_in_t = g_w_in.T
    outs_t = _adamw(w_in[0].T, g_in_t, m_w_in[0].T, v_w_in[0].T, name="adamw_w_in")
    delta["w_in"], new_m["w_in"], new_v["w_in"] = [o.T[None] for o in outs_t]
    grads["w_in"] = g_in_t.T[None]
    smalls = [k for k in names if k not in big]
    sizes = [math.prod(weights[k].shape) for k in smalls]
    rows = -(-sum(sizes) // 1024) * 8
    packs = [_pack_small([d[k] for k in smalls], rows) for d in (weights, grads, ms, vs)]
    outs = _adamw(*packs, name="adamw_small")
    for o, dst in zip(outs, (delta, new_m, new_v)):
        for k, val in zip(smalls, _unpack_small(o, sizes)):
            dst[k] = val.reshape(weights[k].shape)
    return (loss, dx, *[grads[k] for k in names], *[delta[k] for k in names], *[new_m[k] for k in names],
            *[new_v[k] for k in names])
```

```python
import functools
import math

import jax
import jax.numpy as jnp
from jax import lax
from jax.experimental import pallas as pl
from jax.experimental.pallas import tpu as pltpu

F32 = jnp.float32
BF16 = jnp.bfloat16
HIGHEST = lax.Precision.HIGHEST
MESH = pl.DeviceIdType.MESH

D_MODEL = 1024
SSD_HEADS = 16
HEAD_DIM = 64
SSD_WIDTH = 1024
SSD_STATE = 128
CONV_CH = 1536
CHUNK = 128
ATT_WIDTH = 1024
EPS = 1e-5
IN_WIDTH = 5664
PA_WIDTH = 2688
QKV_WIDTH = 3072
D_FF = 4096
ATT_BLOCK = 256
NEG = -1e30
LOG2E = 1.4426950408889634
VMEM_LIMIT = 48 * 1024 * 1024

ADAM_LR = 0.001
ADAM_B1 = 0.9
ADAM_B2 = 0.999
ADAM_EPS = 1e-08
ADAM_WD = 0.01
ADAM_STEP = 10

N_CHIPS = 4
SMALL_ROWS = 96


def _cparams(sem):
    return pltpu.CompilerParams(dimension_semantics=sem, vmem_limit_bytes=VMEM_LIMIT)


def _pick(n, cands):
    for c in cands:
        if n % c == 0:
            return c
    return n


MM_CHUNK = 512


def _mm(a, b, *, name, tiles, ta=False, tb=False, out_dtype=F32, res=None, a_act=None, epi_up=None, after=None,
        into=None):
    n_unread = (after is not None) + (into is not None and into[3] is not None)
    if ta:
        K, M = a.shape
    else:
        M, K = a.shape
    if tb:
        N, K2 = b.shape
    else:
        K2, N = b.shape
    assert K == K2, (a.shape, b.shape)
    tm, tn, tk = tiles
    assert M % tm == 0 and N % tn == 0 and K % tk == 0, (name, M, N, K, tiles)
    nk = K // tk
    dn = (((0 if ta else 1,), (1 if tb else 0,)), ((), ()))
    has_res = res is not None
    has_up = epi_up is not None
    cn = _pick(tn, (MM_CHUNK, 384, 256, 128))

    def prologue(av):
        if a_act == "relu2":
            r = jnp.maximum(av.astype(F32), 0.0)
            av = r * r
        return av.astype(BF16)

    def epilogue(out, res_v, up_v):
        if has_res:
            out = out + res_v.astype(F32)
        if has_up:
            out = out * (2.0 * jnp.maximum(up_v.astype(F32), 0.0))
        return out.astype(out_dtype)

    def body(*refs):
        a_ref, b_ref = refs[0], refs[1]
        i = 2
        res_ref = up_ref = None
        if has_res:
            res_ref = refs[i]
            i += 1
        if has_up:
            up_ref = refs[i]
            i += 1
        i += n_unread
        o_ref = refs[i]
        if nk == 1:
            av = prologue(a_ref[...])
            for c in range(tn // cn):
                cs = slice(c * cn, (c + 1) * cn)
                bv = (b_ref[cs, :] if tb else b_ref[:, cs]).astype(BF16)
                out = lax.dot_general(av, bv, dn, preferred_element_type=F32)
                o_ref[:, cs] = epilogue(out, res_ref[:, cs] if has_res else None, up_ref[:, cs] if has_up else None)
            return
        acc_ref = refs[i + 1]
        k = pl.program_id(2)

        @pl.when(k == 0)
        def _():
            acc_ref[...] = jnp.zeros_like(acc_ref)

        acc_ref[...] += lax.dot_general(prologue(a_ref[...]), b_ref[...].astype(BF16), dn,
                                        preferred_element_type=F32)

        @pl.when(k == nk - 1)
        def _():
            out = epilogue(acc_ref[...], res_ref[...] if has_res else None, up_ref[...] if has_up else None)
            o_ref[...] = out.reshape(o_ref.shape)

    a_spec = pl.BlockSpec((tk, tm), lambda i, j, k: (k, i)) if ta else pl.BlockSpec((tm, tk), lambda i, j, k: (i, k))
    b_spec = pl.BlockSpec((tn, tk), lambda i, j, k: (j, k)) if tb else pl.BlockSpec((tk, tn), lambda i, j, k: (k, j))
    o_spec = pl.BlockSpec((tm, tn), lambda i, j, k: (i, j))
    ins, specs = [a, b], [a_spec, b_spec]
    if has_res:
        ins.append(res)
        specs.append(o_spec)
    if has_up:
        ins.append(epi_up)
        specs.append(o_spec)
    if after is not None:
        ins.append(after)
        specs.append(pl.BlockSpec(memory_space=pl.ANY))
    out_shape, out_spec, aliases = jax.ShapeDtypeStruct((M, N), out_dtype), o_spec, {}
    if into is not None:
        shape, block, index, buf = into
        out_shape, out_spec = jax.ShapeDtypeStruct(shape, out_dtype), pl.BlockSpec(block, index)
        if buf is not None:
            aliases = {len(ins): 0}
            ins.append(buf)
            specs.append(pl.BlockSpec(memory_space=pl.ANY))
    return pl.pallas_call(
        body, name=name, grid=(M // tm, N // tn, nk),
        in_specs=specs, out_specs=out_spec, out_shape=out_shape, input_output_aliases=aliases,
        scratch_shapes=[] if nk == 1 else [pltpu.VMEM((tm, tn), F32)],
        compiler_params=_cparams(("parallel", "parallel", "arbitrary")),
    )(*ins)


def _rows_product(a_ref, b_ref, tb, a_act):
    av = a_ref[...]
    if a_act == "relu2":
        r = jnp.maximum(av.astype(F32), 0.0)
        av = r * r
    dn = (((1,), (1 if tb else 0,)), ((), ()))
    return lax.dot_general(av.astype(BF16), b_ref[...].astype(BF16), dn, preferred_element_type=F32)


def _norm_mm(x, w, b, after, *, name, tm):
    m, d = x.shape
    n = b.shape[1]
    cn = _pick(n, (MM_CHUNK, 384, 256, 128))

    def body(x_ref, w_ref, b_ref, after_ref, h_ref, r_ref, o_ref):
        xv = x_ref[...]
        rstd = lax.rsqrt(jnp.mean(xv * xv, axis=1, keepdims=True) + EPS)
        hv = (xv * rstd * w_ref[...]).astype(BF16)
        h_ref[...] = hv
        r_ref[...] = rstd
        for c in range(n // cn):
            cs = slice(c * cn, (c + 1) * cn)
            o_ref[:, cs] = jnp.dot(hv, b_ref[:, cs].astype(BF16), preferred_element_type=F32)

    row = pl.BlockSpec((tm, d), lambda i: (i, 0))
    return pl.pallas_call(
        body, name=name, grid=(m // tm,),
        in_specs=[row, pl.BlockSpec((1, d), lambda i: (0, 0)), pl.BlockSpec((d, n), lambda i: (0, 0)),
                  pl.BlockSpec(memory_space=pl.ANY)],
        out_specs=[row, pl.BlockSpec((tm, 1), lambda i: (i, 0)), pl.BlockSpec((tm, n), lambda i: (i, 0))],
        out_shape=[jax.ShapeDtypeStruct((m, d), BF16), jax.ShapeDtypeStruct((m, 1), F32),
                   jax.ShapeDtypeStruct((m, n), F32)],
        compiler_params=_cparams(("parallel",)),
    )(x, w, b, after)


def _mm_norm_fwd(a1, b1, a2, b2, res, w, *, name, tm):
    m, k1 = a1.shape
    k2 = a2.shape[1]
    d = b1.shape[1]

    def body(a1_ref, b1_ref, a2_ref, b2_ref, res_ref, w_ref, h_ref, y_ref, r_ref):
        hv = _rows_product(a1_ref, b1_ref, False, None) + _rows_product(a2_ref, b2_ref, False, None) + res_ref[...]
        rstd = lax.rsqrt(jnp.mean(hv * hv, axis=1, keepdims=True) + EPS)
        h_ref[...] = hv
        y_ref[...] = (hv * rstd * w_ref[...]).astype(BF16)
        r_ref[...] = rstd

    row = pl.BlockSpec((tm, d), lambda i: (i, 0))
    return pl.pallas_call(
        body, name=name, grid=(m // tm,),
        in_specs=[pl.BlockSpec((tm, k1), lambda i: (i, 0)), pl.BlockSpec((k1, d), lambda i: (0, 0)),
                  pl.BlockSpec((tm, k2), lambda i: (i, 0)), pl.BlockSpec((k2, d), lambda i: (0, 0)), row,
                  pl.BlockSpec((1, d), lambda i: (0, 0))],
        out_specs=[row, row, pl.BlockSpec((tm, 1), lambda i: (i, 0))],
        out_shape=[jax.ShapeDtypeStruct((m, d), F32), jax.ShapeDtypeStruct((m, d), BF16),
                   jax.ShapeDtypeStruct((m, 1), F32)],
        compiler_params=_cparams(("parallel",)),
    )(a1, b1, a2, b2, res, w)


def _mm_final(a, b, res, w, target, *, name, tm, a_act):
    m, k = a.shape
    d = b.shape[1]

    def body(a_ref, b_ref, res_ref, w_ref, t_ref, dh_ref, dhb_ref, loss_ref, dw_ref):
        @pl.when(pl.program_id(0) == 0)
        def _():
            loss_ref[...] = jnp.zeros_like(loss_ref)
            dw_ref[...] = jnp.zeros_like(dw_ref)

        hv = _rows_product(a_ref, b_ref, False, a_act) + res_ref[...]
        wv = w_ref[...]
        rstd = lax.rsqrt(jnp.mean(hv * hv, axis=1, keepdims=True) + EPS)
        xhat = hv * rstd
        err = xhat * wv - t_ref[...]
        loss_ref[...] += 0.5 * jnp.sum(jnp.mean(err * err, axis=1, keepdims=True), axis=0, keepdims=True)
        dy = err * (1.0 / d)
        gw = dy * wv
        dh = rstd * (gw - xhat * jnp.mean(gw * xhat, axis=1, keepdims=True))
        dh_ref[...] = dh
        dhb_ref[...] = dh.astype(BF16)
        dw_ref[...] += jnp.sum(dy * xhat, axis=0, keepdims=True)

    row = pl.BlockSpec((tm, d), lambda i: (i, 0))
    vec = pl.BlockSpec((1, d), lambda i: (0, 0))
    return pl.pallas_call(
        body, name=name, grid=(m // tm,),
        in_specs=[pl.BlockSpec((tm, k), lambda i: (i, 0)), pl.BlockSpec((k, d), lambda i: (0, 0)), row, vec, row],
        out_specs=[row, row, pl.BlockSpec((1, 128), lambda i: (0, 0)), vec],
        out_shape=[jax.ShapeDtypeStruct((m, d), F32), jax.ShapeDtypeStruct((m, d), BF16),
                   jax.ShapeDtypeStruct((1, 128), F32), jax.ShapeDtypeStruct((1, d), F32)],
        compiler_params=_cparams(("arbitrary",)),
    )(a, b, res, w, target)


def _mm_two_halves(a, b, *, name, tm):
    m, k = a.shape
    d = b.shape[0] // 2

    def body(a_ref, b_ref, lo_ref, hi_ref):
        av = a_ref[...].astype(BF16)
        lo_ref[...] = lax.dot_general(av, b_ref[0:d, :].astype(BF16), NT_DIMS, preferred_element_type=F32)
        hi_ref[...] = lax.dot_general(av, b_ref[d:2 * d, :].astype(BF16), NT_DIMS,
                                      preferred_element_type=F32).astype(BF16)

    row = pl.BlockSpec((tm, d), lambda i: (i, 0))
    return pl.pallas_call(
        body, name=name, grid=(m // tm,),
        in_specs=[pl.BlockSpec((tm, k), lambda i: (i, 0)), pl.BlockSpec((2 * d, k), lambda i: (0, 0))],
        out_specs=[row, row],
        out_shape=[jax.ShapeDtypeStruct((m, d), F32), jax.ShapeDtypeStruct((m, d), BF16)],
        compiler_params=_cparams(("parallel",)),
    )(a, b)


def _mm_norm_bwd(pairs, x, rstd, w, dres, *, name, tm, after=None):
    m = pairs[0][0].shape[0]
    d = pairs[0][1].shape[0]
    n_pairs = len(pairs)

    def body(*refs):
        i = 2 * n_pairs
        x_ref, r_ref, w_ref, d_ref = refs[i:i + 4]
        dx_ref, dxb_ref, dw_ref = refs[-3:]

        @pl.when(pl.program_id(0) == 0)
        def _():
            dw_ref[...] = jnp.zeros_like(dw_ref)

        g = _rows_product(refs[0], refs[1], True, None)
        for p in range(1, n_pairs):
            g = g + _rows_product(refs[2 * p], refs[2 * p + 1], True, None)
        r = r_ref[...]
        xhat = x_ref[...] * r
        gw = g * w_ref[...]
        dx = d_ref[...] + r * (gw - xhat * jnp.mean(gw * xhat, axis=1, keepdims=True))
        dx_ref[...] = dx
        dxb_ref[...] = dx.astype(BF16)
        dw_ref[...] += jnp.sum(g * xhat, axis=0, keepdims=True)

    row = pl.BlockSpec((tm, d), lambda i: (i, 0))
    vec = pl.BlockSpec((1, d), lambda i: (0, 0))
    ins, specs = [], []
    for a, b in pairs:
        k = a.shape[1]
        ins += [a, b]
        specs += [pl.BlockSpec((tm, k), lambda i: (i, 0)), pl.BlockSpec((d, k), lambda i: (0, 0))]
    ins += [x, rstd, w, dres]
    specs += [row, pl.BlockSpec((tm, 1), lambda i: (i, 0)), vec, row]
    if after is not None:
        ins.append(after)
        specs.append(pl.BlockSpec(memory_space=pl.ANY))
    return pl.pallas_call(
        body, name=name, grid=(m // tm,), in_specs=specs, out_specs=[row, row, vec],
        out_shape=[jax.ShapeDtypeStruct((m, d), F32), jax.ShapeDtypeStruct((m, d), BF16),
                   jax.ShapeDtypeStruct((1, d), F32)],
        compiler_params=_cparams(("arbitrary",)),
    )(*ins)


def _softplus(x):
    return jnp.maximum(x, 0.0) + jnp.log(1.0 + jnp.exp(-jnp.abs(x)))


def _prep(proj_a, bias128, alog128, bl, t):
    n = bl * t
    nch = t // CHUNK
    col0 = (SSD_WIDTH + CONV_CH) // 128

    def body(p_ref, b_ref, al_ref, dt_ref, gd_ref, ac_ref, c_ref):
        row = lax.broadcasted_iota(jnp.int32, (CHUNK, CHUNK), 0)
        col = lax.broadcasted_iota(jnp.int32, (CHUNK, CHUNK), 1)
        tril = (row >= col).astype(F32)
        lane = lax.broadcasted_iota(jnp.int32, (1, 128), 1)
        head_lanes = lane < 16
        a_row = -jnp.exp(al_ref[...])
        carry = jnp.zeros((1, 128), F32)
        for ci in range(nch):
            rows = slice(ci * CHUNK, (ci + 1) * CHUNK)
            xv = p_ref[rows, :] + b_ref[...]
            sp = _softplus(xv)
            acum = jnp.dot(tril, a_row * sp, precision=HIGHEST, preferred_element_type=F32)
            c = jnp.dot(tril, -_softplus(-xv), precision=HIGHEST, preferred_element_type=F32) + carry
            carry = c[CHUNK - 1:CHUNK, :]
            dt_ref[rows, :] = jnp.where(head_lanes, sp, 0.0)
            gd_ref[rows, :] = jnp.where(head_lanes, jax.nn.sigmoid(xv),
                                        jnp.where(lane < 32, jax.nn.sigmoid(-xv), 0.0))
            ac_ref[rows, :] = jnp.where(head_lanes, acum, 0.0)
            c_ref[rows, :] = c[:, 16:32]

    o128 = pl.BlockSpec((t, 128), lambda b: (b, 0))
    v128 = pl.BlockSpec((1, 128), lambda b: (0, 0))
    w128 = jax.ShapeDtypeStruct((n, 128), F32)
    return pl.pallas_call(
        body, name="head_scalars", grid=(bl,),
        in_specs=[pl.BlockSpec((t, 128), lambda b: (b, col0)), v128, v128],
        out_specs=[o128, o128, o128, pl.BlockSpec((t, 16), lambda b: (b, 0))],
        out_shape=[w128, w128, w128, jax.ShapeDtypeStruct((n, 16), F32)],
        compiler_params=_cparams(("parallel",)),
    )(proj_a, bias128, alog128)


def _fpost(dc, gate_d, ddt, dpa, bl, t):
    n = bl * t
    nch = t // CHUNK
    col0 = (SSD_WIDTH + CONV_CH) // 128

    def body(dc_ref, gd_ref, ddt_ref, dpa_in, out_ref, db_ref):
        @pl.when(pl.program_id(0) == 0)
        def _():
            db_ref[...] = jnp.zeros_like(db_ref)

        row = lax.broadcasted_iota(jnp.int32, (CHUNK, CHUNK), 0)
        col = lax.broadcasted_iota(jnp.int32, (CHUNK, CHUNK), 1)
        triu = (row <= col).astype(F32)
        lane = lax.broadcasted_iota(jnp.int32, (1, 128), 1)
        gate_lanes = (lane >= 16) & (lane < 32)
        carry = jnp.zeros((1, 128), F32)
        db = jnp.zeros((1, 128), F32)
        for ci in reversed(range(nch)):
            rows = slice(ci * CHUNK, (ci + 1) * CHUNK)
            dlf = jnp.dot(triu, dc_ref[rows, :], precision=HIGHEST, preferred_element_type=F32) + carry
            carry = dlf[0:1, :]
            df = jnp.where(gate_lanes, dlf * gd_ref[rows, :], 0.0)
            out_ref[rows, :] = (ddt_ref[rows, :] + df).astype(BF16)
            db = db + jnp.sum(df, axis=0, keepdims=True)
        db_ref[...] += db[:, 16:32]

    blk = pl.BlockSpec((t, 128), lambda b: (b, 0))
    return pl.pallas_call(
        body, name="forget_gate_bwd", grid=(bl,),
        in_specs=[blk, blk, blk, ANY],
        out_specs=[pl.BlockSpec((t, 128), lambda b: (b, col0)), pl.BlockSpec((1, 16), lambda b: (0, 0))],
        out_shape=[jax.ShapeDtypeStruct(dpa.shape, dpa.dtype), jax.ShapeDtypeStruct((1, 16), F32)],
        input_output_aliases={3: 0},
        compiler_params=_cparams(("arbitrary",)),
    )(dc, gate_d, ddt, dpa)


CONV_TILE = 256
CONV_ROWS = 256


def _conv_taps(u_ref, i, w, bias):
    r0 = pl.multiple_of(i * CONV_ROWS, CONV_ROWS)
    cur = u_ref[pl.ds(r0, CONV_ROWS), :]
    p0 = pl.multiple_of(jnp.maximum(r0 - 8, 0), 8)
    prev = jnp.where(i > 0, u_ref[pl.ds(p0, 8), :], 0.0)
    cat = jnp.concatenate([prev, cur], axis=0)
    pre = bias + w[3:4, :] * cur
    taps = [cur]
    for s in (1, 2, 3):
        sh = pltpu.roll(cat, s, 0)[8:, :]
        taps.append(sh)
        pre = pre + w[3 - s:4 - s, :] * sh
    return r0, pre, taps


def _conv_fwd(proj_a, conv_w, conv_b, bl, t):
    n = bl * t
    nct = CONV_CH // CONV_TILE
    c0 = SSD_WIDTH // CONV_TILE

    def body(u_ref, w_ref, b_ref, o_ref):
        w = w_ref[...]
        bias = b_ref[...]

        def chunk(i, carry):
            r0, pre, _ = _conv_taps(u_ref, i, w, bias)
            o_ref[pl.ds(r0, CONV_ROWS), :] = pre * jax.nn.sigmoid(pre)
            return carry

        lax.fori_loop(0, t // CONV_ROWS, chunk, 0)

    return pl.pallas_call(
        body, name="conv_silu_fwd", grid=(bl, nct),
        in_specs=[pl.BlockSpec((t, CONV_TILE), lambda b, c: (b, c0 + c)),
                  pl.BlockSpec((4, CONV_TILE), lambda b, c: (0, c)),
                  pl.BlockSpec((1, CONV_TILE), lambda b, c: (0, c))],
        out_specs=pl.BlockSpec((t, CONV_TILE), lambda b, c: (b, c)),
        out_shape=jax.ShapeDtypeStruct((n, CONV_CH), F32),
        compiler_params=_cparams(("parallel", "parallel")),
    )(proj_a, conv_w, conv_b)


def _conv_bwd(dxc, proj_a, conv_w, conv_b, dpa, bl, t):
    nct = CONV_CH // CONV_TILE
    c0 = SSD_WIDTH // CONV_TILE
    nrc = t // CONV_ROWS

    def body(g_ref, u_ref, w_ref, b_ref, dpa_in, du_ref, dw_ref, db_ref, dp_scr):
        @pl.when(pl.program_id(1) == 0)
        def _():
            dw_ref[...] = jnp.zeros_like(dw_ref)
            db_ref[...] = jnp.zeros_like(db_ref)

        w = w_ref[...]
        bias = b_ref[...]
        dp_scr[pl.ds(t, 8), :] = jnp.zeros((8, CONV_TILE), F32)

        def chunk1(i, carry):
            dw0, dw1, dw2, dw3, db = carry
            r0, pre, taps = _conv_taps(u_ref, i, w, bias)
            sg = jax.nn.sigmoid(pre)
            dpre = g_ref[pl.ds(r0, CONV_ROWS), :] * (sg * (1.0 + pre * (1.0 - sg)))
            dp_scr[pl.ds(r0, CONV_ROWS), :] = dpre
            dw3 = dw3 + jnp.sum(dpre * taps[0], axis=0, keepdims=True)
            dw2 = dw2 + jnp.sum(dpre * taps[1], axis=0, keepdims=True)
            dw1 = dw1 + jnp.sum(dpre * taps[2], axis=0, keepdims=True)
            dw0 = dw0 + jnp.sum(dpre * taps[3], axis=0, keepdims=True)
            db = db + jnp.sum(dpre, axis=0, keepdims=True)
            return dw0, dw1, dw2, dw3, db

        z = jnp.zeros((1, CONV_TILE), F32)
        dw0, dw1, dw2, dw3, db = lax.fori_loop(0, nrc, chunk1, (z, z, z, z, z))
        dw_ref[...] += jnp.concatenate([dw0, dw1, dw2, dw3], axis=0)
        db_ref[...] += db

        def chunk2(i, carry):
            r0 = pl.multiple_of(i * CONV_ROWS, CONV_ROWS)
            cat = dp_scr[pl.ds(r0, CONV_ROWS + 8), :]
            du = w[3:4, :] * cat[:CONV_ROWS, :]
            for s in (1, 2, 3):
                du = du + w[3 - s:4 - s, :] * pltpu.roll(cat, CONV_ROWS + 8 - s, 0)[:CONV_ROWS, :]
            du_ref[pl.ds(r0, CONV_ROWS), :] = du.astype(BF16)
            return carry

        lax.fori_loop(0, nrc, chunk2, 0)

    return pl.pallas_call(
        body, name="conv_silu_bwd", grid=(nct, bl),
        in_specs=[pl.BlockSpec((t, CONV_TILE), lambda c, b: (b, c)),
                  pl.BlockSpec((t, CONV_TILE), lambda c, b: (b, c0 + c)),
                  pl.BlockSpec((4, CONV_TILE), lambda c, b: (0, c)),
                  pl.BlockSpec((1, CONV_TILE), lambda c, b: (0, c)), ANY],
        out_specs=[pl.BlockSpec((t, CONV_TILE), lambda c, b: (b, c0 + c)),
                   pl.BlockSpec((4, CONV_TILE), lambda c, b: (0, c)),
                   pl.BlockSpec((1, CONV_TILE), lambda c, b: (0, c))],
        out_shape=[jax.ShapeDtypeStruct(dpa.shape, dpa.dtype), jax.ShapeDtypeStruct((4, CONV_CH), F32),
                   jax.ShapeDtypeStruct((1, CONV_CH), F32)],
        input_output_aliases={4: 0},
        scratch_shapes=[pltpu.VMEM((t + 8, CONV_TILE), F32)],
        compiler_params=_cparams(("parallel", "arbitrary")),
    )(dxc, proj_a, conv_w, conv_b, dpa)


NT_DIMS = (((1,), (1,)), ((), ()))
TN_DIMS = (((0,), (0,)), ((), ()))


def _dot(a, b, dims=None):
    if dims is None:
        return jnp.dot(a, b, preferred_element_type=F32)
    return lax.dot_general(a, b, dims, preferred_element_type=F32)


def _head_expander():
    r = lax.broadcasted_iota(jnp.int32, (128, SSD_WIDTH), 0)
    c = lax.broadcasted_iota(jnp.int32, (128, SSD_WIDTH), 1)
    return ((c // HEAD_DIM == r % 16) & (r < 48)).astype(BF16)


def _spread(v128, expander):
    hi = v128.astype(BF16).astype(F32)
    r1 = v128 - hi
    mid = r1.astype(BF16).astype(F32)
    lo = (r1 - mid).astype(BF16).astype(F32)
    packed = (hi + pltpu.roll(mid, 16, 1) + pltpu.roll(lo, 32, 1)).astype(BF16)
    return jnp.dot(packed, expander, preferred_element_type=F32)


def _head_sums(v1024, expander):
    hi = v1024.astype(BF16)
    lo = (v1024 - hi.astype(F32)).astype(BF16)
    heads = jnp.where(lax.broadcasted_iota(jnp.int32, expander.shape, 0) < 16, expander, jnp.zeros_like(expander))
    return _dot(hi, heads, NT_DIMS) + _dot(lo, heads, NT_DIMS)


def _ssd_fwd(xc, proj_a, dt, acum, acum_t, dskip_e, norm_w, bl, t):
    n = bl * t
    nch = t // CHUNK
    L = CHUNK

    def body(xc_ref, z_ref, dt_ref, ac_ref, act_ref, dsk_ref, nw_ref, ys_ref, yp_ref, hp_ref, h_scr, y_scr, x_scr):
        @pl.when(pl.program_id(1) == 0)
        def _():
            h_scr[...] = jnp.zeros_like(h_scr)

        row = lax.broadcasted_iota(jnp.int32, (L, L), 0)
        col = lax.broadcasted_iota(jnp.int32, (L, L), 1)
        causal = row >= col
        expander = _head_expander()
        ac_all = ac_ref[...]
        act_all = act_ref[...]
        ac_e = _spread(ac_all, expander)
        e_in = jnp.exp(ac_e)
        dec = jnp.exp(ac_e[L - 1:L, :] - ac_e)
        xs_all = xc_ref[:, 0:SSD_WIDTH]
        x_all = xs_all * _spread(dt_ref[...], expander)
        x_scr[...] = x_all.astype(BF16)
        hp_all = h_scr[...]
        hp_ref[...] = hp_all
        for g in range(2):
            gs = slice(g * 512, (g + 1) * 512)
            bg = xc_ref[:, SSD_WIDTH + g * 128:SSD_WIDTH + (g + 1) * 128].astype(BF16)
            cg = xc_ref[:, SSD_WIDTH + 256 + g * 128:SSD_WIDTH + 256 + (g + 1) * 128].astype(BF16)
            gmat = _dot(cg, bg, NT_DIMS)
            y_scr[:, gs] = (_dot(cg, hp_all[gs, :].astype(BF16), NT_DIMS) * e_in[:, gs]
                            + dsk_ref[:, gs] * xs_all[:, gs])
            s_new = _dot((x_all[:, gs] * dec[:, gs]).astype(BF16), bg, TN_DIMS)
            for r in range(8):
                h = g * 8 + r
                sl = slice(h * HEAD_DIM, (h + 1) * HEAD_DIM)
                ldec = jnp.exp(jnp.where(causal, ac_all[:, h:h + 1] - act_all[h:h + 1, :], NEG))
                y_scr[:, sl] += _dot((gmat * ldec).astype(BF16), x_scr[:, sl])
                elast = jnp.exp(ac_all[L - 1:L, h:h + 1])
                h_scr[sl, :] = elast * hp_all[sl, :] + s_new[r * HEAD_DIM:(r + 1) * HEAD_DIM, :]
        y = y_scr[...]
        yp_ref[...] = y
        zv = z_ref[...]
        yg = y * (zv * jax.nn.sigmoid(zv))
        for g in range(2):
            gs = slice(g * 512, (g + 1) * 512)
            grp = yg[:, gs]
            rstd = lax.rsqrt(jnp.mean(grp * grp, axis=1, keepdims=True) + EPS)
            ys_ref[:, gs] = (grp * rstd * nw_ref[:, gs]).astype(BF16)

    rb = lambda b, c: (b * nch + c, 0)
    v1k = pl.BlockSpec((1, SSD_WIDTH), lambda b, c: (0, 0))
    return pl.pallas_call(
        body, name="ssd_fwd", grid=(bl, nch),
        in_specs=[pl.BlockSpec((L, CONV_CH), rb), pl.BlockSpec((L, SSD_WIDTH), rb),
                  pl.BlockSpec((L, 128), rb), pl.BlockSpec((L, 128), rb),
                  pl.BlockSpec((16, L), lambda b, c: (0, b * nch + c)), v1k, v1k],
        out_specs=[pl.BlockSpec((L, SSD_WIDTH), rb), pl.BlockSpec((L, SSD_WIDTH), rb),
                   pl.BlockSpec((None, SSD_WIDTH, SSD_STATE), lambda b, c: (b * nch + c, 0, 0))],
        out_shape=[jax.ShapeDtypeStruct((n, SSD_WIDTH), BF16), jax.ShapeDtypeStruct((n, SSD_WIDTH), F32),
                   jax.ShapeDtypeStruct((bl * nch, SSD_WIDTH, SSD_STATE), F32)],
        scratch_shapes=[pltpu.VMEM((SSD_WIDTH, SSD_STATE), F32), pltpu.VMEM((L, SSD_WIDTH), F32),
                        pltpu.VMEM((L, SSD_WIDTH), BF16)],
        compiler_params=_cparams(("parallel", "arbitrary")),
    )(xc, proj_a, dt, acum, acum_t, dskip_e, norm_w)


def _ssd_bwd(dys, xc, proj_a, ypre, hprev, dt, gate_d, acum, acum_t, alog128, dskip_e, norm_w, bl, t):
    n = bl * t
    nch = t // CHUNK
    L = CHUNK

    def body(dys_ref, xc_ref, z_ref, yp_ref, hp_ref, dt_ref, gd_ref, ac_ref, act_ref, al_ref, dsk_ref, nw_ref,
             dxc_ref, dz_ref, ddt_ref, dnw_ref, dsk16_ref, da16_ref, db16_ref,
             dh_scr, dy_scr, x_scr, dx_scr, red_scr):
        first = (pl.program_id(0) == 0) & (pl.program_id(1) == 0)

        @pl.when(first)
        def _():
            dnw_ref[...] = jnp.zeros_like(dnw_ref)
            dsk16_ref[...] = jnp.zeros_like(dsk16_ref)
            da16_ref[...] = jnp.zeros_like(da16_ref)
            db16_ref[...] = jnp.zeros_like(db16_ref)

        @pl.when(pl.program_id(1) == 0)
        def _():
            dh_scr[...] = jnp.zeros_like(dh_scr)

        y = yp_ref[...]
        zv = z_ref[...]
        sz = jax.nn.sigmoid(zv)
        gate = zv * sz
        yg = y * gate
        dout = dys_ref[...]
        nw = nw_ref[...]
        for g in range(2):
            gs = slice(g * 512, (g + 1) * 512)
            grp = yg[:, gs]
            rstd = lax.rsqrt(jnp.mean(grp * grp, axis=1, keepdims=True) + EPS)
            ghat = grp * rstd
            dnw_ref[:, gs] += jnp.sum(dout[:, gs] * ghat, axis=0, keepdims=True)
            gw = dout[:, gs] * nw[:, gs]
            dyg = rstd * (gw - ghat * jnp.mean(gw * ghat, axis=1, keepdims=True))
            dy_scr[:, gs] = dyg * gate[:, gs]
            dz_ref[:, gs] = (dyg * y[:, gs] * (sz[:, gs] * (1.0 + zv[:, gs] * (1.0 - sz[:, gs])))).astype(BF16)

        row = lax.broadcasted_iota(jnp.int32, (L, L), 0)
        col = lax.broadcasted_iota(jnp.int32, (L, L), 1)
        causal = row >= col
        lane128 = lax.broadcasted_iota(jnp.int32, (1, L), 1)
        rows128 = lax.broadcasted_iota(jnp.int32, (L, 1), 0)
        last_row = rows128 == (L - 1)
        expander = _head_expander()
        ac_all = ac_ref[...]
        act_all = act_ref[...]
        dt_all = dt_ref[...]
        dt_e = _spread(dt_all, expander)
        ac_e = _spread(ac_all, expander)
        e_in = jnp.exp(ac_e)
        dec = jnp.exp(ac_e[L - 1:L, :] - ac_e)
        xs_all = xc_ref[:, 0:SSD_WIDTH]
        x_all = xs_all * dt_e
        x_scr[...] = x_all.astype(BF16)
        dy_all = dy_scr[...]
        hp_all = hp_ref[...]
        ds_all = dh_scr[...]
        dsk_cols = jnp.sum(dy_all * xs_all, axis=0, keepdims=True)
        dac = jnp.zeros((L, L), F32)
        dac_row = jnp.zeros((L, L), F32)
        ddec_cols = []
        for g in range(2):
            gs = slice(g * 512, (g + 1) * 512)
            bsl = slice(SSD_WIDTH + g * 128, SSD_WIDTH + (g + 1) * 128)
            csl = slice(SSD_WIDTH + 256 + g * 128, SSD_WIDTH + 256 + (g + 1) * 128)
            bg = xc_ref[:, bsl].astype(BF16)
            cg = xc_ref[:, csl].astype(BF16)
            gmat = _dot(cg, bg, NT_DIMS)
            hpb = hp_all[gs, :].astype(BF16)
            dsb = ds_all[gs, :].astype(BF16)
            ch = _dot(cg, hpb, NT_DIMS)
            dye = dy_all[:, gs] * e_in[:, gs]
            dyeb = dye.astype(BF16)
            dc_acc = _dot(dyeb, hpb)
            dhp = _dot(dyeb, cg, TN_DIMS)
            dxd = _dot(bg, dsb, NT_DIMS)
            db_acc = _dot((x_all[:, gs] * dec[:, gs]).astype(BF16), dsb)
            ddec = dxd * x_all[:, gs] * dec[:, gs]
            ddec_cols.append(jnp.sum(ddec, axis=0, keepdims=True))
            dx_scr[:, gs] = dxd * dec[:, gs]
            red_scr[:, gs] = dye * ch - ddec
            dg_sum = jnp.zeros((L, L), F32)
            for r in range(8):
                h = g * 8 + r
                sl = slice(h * HEAD_DIM, (h + 1) * HEAD_DIM)
                onehot_w = lane128 == h
                ldec = jnp.exp(jnp.where(causal, ac_all[:, h:h + 1] - act_all[h:h + 1, :], NEG))
                mf = gmat * ldec
                dyb = dy_scr[:, sl].astype(BF16)
                dm = _dot(dyb, x_scr[:, sl], NT_DIMS)
                dx_scr[:, sl] += _dot(mf.astype(BF16), dyb, TN_DIMS)
                dg_sum = dg_sum + dm * ldec
                wmat = dm * mf
                elast = jnp.exp(ac_all[L - 1:L, h:h + 1])
                hp_h = hp_all[sl, :]
                ds_h = ds_all[sl, :]
                extra = elast * jnp.sum(jnp.sum(hp_h * ds_h, axis=1, keepdims=True), axis=0, keepdims=True)
                dac = dac + jnp.where(onehot_w, jnp.sum(wmat, axis=1, keepdims=True) + jnp.where(last_row, extra, 0.0),
                                      0.0)
                dac_row = dac_row + jnp.where(rows128 == h, -jnp.sum(wmat, axis=0, keepdims=True), 0.0)
                dh_scr[sl, :] = elast * ds_h + dhp[r * HEAD_DIM:(r + 1) * HEAD_DIM, :]
            dgb = dg_sum.astype(BF16)
            dxc_ref[:, csl] = dc_acc + _dot(dgb, bg)
            dxc_ref[:, bsl] = db_acc + _dot(dgb, cg, TN_DIMS)
        dx_all = dx_scr[...]
        dxc_ref[:, 0:SSD_WIDTH] = dx_all * dt_e + dsk_ref[...] * dy_all
        red = red_scr[...]
        dac_slab = _head_sums(red, expander)
        ddec_tot = _head_sums(jnp.broadcast_to(jnp.concatenate(ddec_cols, axis=1), (8, SSD_WIDTH)), expander)
        ddt_x = _head_sums(dx_all * xs_all, expander)
        dsk16_ref[...] += _head_sums(jnp.broadcast_to(dsk_cols, (8, SSD_WIDTH)), expander)[0:1, 0:16]
        dac = dac + dac_slab + jnp.transpose(dac_row) + jnp.where(last_row, ddec_tot[0:1, :], 0.0)
        triu = (row <= col).astype(F32)
        da = jnp.dot(triu, dac, precision=HIGHEST, preferred_element_type=F32)
        a_row = -jnp.exp(al_ref[...])
        ddt = jnp.where(lane128 < 16, (ddt_x + da * a_row) * gd_ref[...], 0.0)
        ddt_ref[...] = ddt
        da16_ref[...] += (jnp.sum(da * dt_all, axis=0, keepdims=True) * a_row)[:, 0:16]
        db16_ref[...] += jnp.sum(ddt, axis=0, keepdims=True)[:, 0:16]

    rb = lambda b, c: (b * nch + nch - 1 - c, 0)
    v1k = pl.BlockSpec((1, SSD_WIDTH), lambda b, c: (0, 0))
    v16 = pl.BlockSpec((1, 16), lambda b, c: (0, 0))
    v128 = pl.BlockSpec((1, 128), lambda b, c: (0, 0))
    wide = pl.BlockSpec((L, SSD_WIDTH), rb)
    s128 = pl.BlockSpec((L, 128), rb)
    return pl.pallas_call(
        body, name="ssd_bwd", grid=(bl, nch),
        in_specs=[wide, pl.BlockSpec((L, CONV_CH), rb), wide, wide,
                  pl.BlockSpec((None, SSD_WIDTH, SSD_STATE), lambda b, c: (b * nch + nch - 1 - c, 0, 0)),
                  s128, s128, s128, pl.BlockSpec((16, L), lambda b, c: (0, b * nch + nch - 1 - c)), v128, v1k, v1k],
        out_specs=[pl.BlockSpec((L, CONV_CH), rb), wide, s128, v1k, v16, v16, v16],
        out_shape=[jax.ShapeDtypeStruct((n, CONV_CH), F32), jax.ShapeDtypeStruct((n, PA_WIDTH), BF16),
                   jax.ShapeDtypeStruct((n, 128), F32), jax.ShapeDtypeStruct((1, SSD_WIDTH), F32),
                   jax.ShapeDtypeStruct((1, 16), F32), jax.ShapeDtypeStruct((1, 16), F32),
                   jax.ShapeDtypeStruct((1, 16), F32)],
        scratch_shapes=[pltpu.VMEM((SSD_WIDTH, SSD_STATE), F32), pltpu.VMEM((L, SSD_WIDTH), F32),
                        pltpu.VMEM((L, SSD_WIDTH), BF16), pltpu.VMEM((L, SSD_WIDTH), F32),
                        pltpu.VMEM((L, SSD_WIDTH), F32)],
        compiler_params=_cparams(("arbitrary", "arbitrary")),
    )(dys, xc, proj_a, ypre, hprev, dt, gate_d, acum, acum_t, alog128, dskip_e, norm_w)


def _attn_fwd(qkv, negc, bl, t):
    n = bl * t
    tb_ = ATT_BLOCK
    nb = t // tb_
    scale2 = LOG2E / math.sqrt(HEAD_DIM)

    def body(q_ref, k_ref, v_ref, c_ref, o_ref, lse_ref):
        row = lax.broadcasted_iota(jnp.int32, (tb_, tb_), 0)
        col = lax.broadcasted_iota(jnp.int32, (tb_, tb_), 1)
        causal = row >= col
        for qi in range(nb):
            r0, lk = qi * tb_, (qi + 1) * tb_
            for j in range(2):
                sl = slice(j * HEAD_DIM, (j + 1) * HEAD_DIM)
                s = _dot(q_ref[r0:lk, sl], k_ref[0:lk, sl], NT_DIMS) * scale2 + c_ref[j:j + 1, 0:lk] * LOG2E
                tail = jnp.where(causal, s[:, r0:lk], NEG)
                s = tail if qi == 0 else jnp.concatenate([s[:, 0:r0], tail], axis=1)
                m = jnp.max(s, axis=1, keepdims=True)
                p = jnp.exp2(s - m)
                l = jnp.sum(p, axis=1, keepdims=True)
                acc = _dot(p.astype(BF16), v_ref[0:lk, sl])
                o_ref[r0:lk, sl] = (acc / l).astype(BF16)
                lse_ref[r0:lk, sl] = jnp.broadcast_to(m + jnp.log(l) * LOG2E, (tb_, HEAD_DIM))

    blk = lambda off: pl.BlockSpec((t, 128), lambda b, hp: (b, off + hp))
    return pl.pallas_call(
        body, name="fox_attn_fwd", grid=(bl, 8),
        in_specs=[blk(0), blk(8), blk(16), pl.BlockSpec((None, None, 8, t), lambda b, hp: (b, hp, 0, 0))],
        out_specs=[blk(0), blk(0)],
        out_shape=[jax.ShapeDtypeStruct((n, ATT_WIDTH), BF16), jax.ShapeDtypeStruct((n, ATT_WIDTH), F32)],
        compiler_params=_cparams(("parallel", "parallel")),
    )(qkv, qkv, qkv, negc)


def _attn_bwd(qkv, do, o, lse, negc, after, bl, t):
    n = bl * t
    tb_ = ATT_BLOCK
    nb = t // tb_
    scale = 1.0 / math.sqrt(HEAD_DIM)
    scale2 = LOG2E * scale

    def body(q_ref, k_ref, v_ref, do_ref, o_ref, lse_ref, c_ref, after_ref, dq_ref, dk_ref, dv_ref, dc_ref,
             dq0_scr, delta_scr, dq1_scr, qt0_scr, qt1_scr, dot_scr, dkt0_scr, dkt1_scr, dvt_scr):
        row = lax.broadcasted_iota(jnp.int32, (tb_, tb_), 0)
        col = lax.broadcasted_iota(jnp.int32, (tb_, tb_), 1)
        causal = row >= col
        lane = lax.broadcasted_iota(jnp.int32, (1, 128), 1)
        dq_scrs = (dq0_scr, dq1_scr)
        qt_scrs = (qt0_scr, qt1_scr)
        dkt_scrs = (dkt0_scr, dkt1_scr)
        dq0_scr[...] = jnp.zeros_like(dq0_scr)
        dq1_scr[...] = jnp.zeros_like(dq1_scr)
        dc_ref[...] = jnp.zeros_like(dc_ref)
        q_t = jnp.transpose(q_ref[...].astype(F32))
        ones_row = jnp.where(lax.broadcasted_iota(jnp.int32, (8, t), 0) == 0, 1.0, 0.0)
        for j in range(2):
            qt_scrs[j][...] = jnp.concatenate(
                [q_t[j * HEAD_DIM:(j + 1) * HEAD_DIM, :], ones_row, jnp.zeros((HEAD_DIM - 8, t), F32)],
                axis=0).astype(BF16)
        dot_scr[...] = jnp.transpose(do_ref[...].astype(F32)).astype(BF16)
        prod = do_ref[...].astype(F32) * o_ref[...].astype(F32)
        for j in range(2):
            sl = slice(j * HEAD_DIM, (j + 1) * HEAD_DIM)
            delta_scr[:, sl] = jnp.broadcast_to(jnp.sum(prod[:, sl], axis=1, keepdims=True), (t, HEAD_DIM))
        for kj in range(nb):
            r0, r1 = kj * tb_, (kj + 1) * tb_
            k_pair = k_ref[r0:r1, :].astype(F32)
            for j in range(2):
                sl = slice(j * HEAD_DIM, (j + 1) * HEAD_DIM)
                one = slice(j * HEAD_DIM, j * HEAD_DIM + 1)
                kb = k_ref[r0:r1, sl]
                k_head = k_pair if j == 0 else pltpu.roll(k_pair, HEAD_DIM, 1)
                k_ones = jnp.where(lane < HEAD_DIM, k_head, jnp.where(lane == HEAD_DIM, 1.0, 0.0)).astype(BF16)
                qs = q_ref[r0:t, sl]
                dos = do_ref[r0:t, sl]
                s = _dot(qs, kb, NT_DIMS) * scale2 + c_ref[j:j + 1, r0:r1] * LOG2E
                head = jnp.where(causal, s[0:tb_, :], NEG)
                s = head if kj == nb - 1 else jnp.concatenate([head, s[tb_:, :]], axis=0)
                p = jnp.exp2(s - lse_ref[r0:t, one])
                dp = _dot(dos, v_ref[r0:r1, sl], NT_DIMS)
                ds = p * (dp - delta_scr[r0:t, one])
                dsb = ds.astype(BF16)
                dvt_scr[sl, r0:r1] = _dot(dot_scr[sl, r0:t], p.astype(BF16))
                dkt_scrs[j][:, r0:r1] = _dot(qt_scrs[j][:, r0:t], dsb)
                dq_scrs[j][r0:t, :] += _dot(dsb, k_ones)
        dv_ref[...] = jnp.transpose(dvt_scr[...]).astype(BF16)
        for j in range(2):
            sl = slice(j * HEAD_DIM, (j + 1) * HEAD_DIM)
            acc = dq_scrs[j][...]
            dkt = dkt_scrs[j][...]
            dq_ref[:, sl] = (acc[:, 0:HEAD_DIM] * scale).astype(BF16)
            dk_ref[:, sl] = (jnp.transpose(dkt)[:, 0:HEAD_DIM] * scale).astype(BF16)
            dc_ref[j:j + 1, :] = jnp.transpose(acc)[HEAD_DIM:HEAD_DIM + 1, :] - dkt[HEAD_DIM:HEAD_DIM + 1, :]

    blk = lambda off: pl.BlockSpec((t, 128), lambda b, hp: (b, off + hp))
    cblk = pl.BlockSpec((None, None, 8, t), lambda b, hp: (b, hp, 0, 0))
    return pl.pallas_call(
        body, name="fox_attn_bwd", grid=(bl, 8),
        in_specs=[blk(0), blk(8), blk(16), blk(0), blk(0), blk(0), cblk, ANY],
        out_specs=[blk(0), blk(0), blk(0), cblk],
        out_shape=[jax.ShapeDtypeStruct((n, ATT_WIDTH), BF16)] * 3 + [jax.ShapeDtypeStruct((bl, 8, 8, t), F32)],
        scratch_shapes=[pltpu.VMEM((t, 128), F32), pltpu.VMEM((t, 128), F32), pltpu.VMEM((t, 128), F32),
                        pltpu.VMEM((128, t), BF16), pltpu.VMEM((128, t), BF16), pltpu.VMEM((128, t), BF16),
                        pltpu.VMEM((128, t), F32), pltpu.VMEM((128, t), F32), pltpu.VMEM((128, t), F32)],
        compiler_params=_cparams(("parallel", "parallel")),
    )(qkv, qkv, qkv, do, o, lse, negc, after)


def _adamw(w, g, m, v, *, name):
    lead = w.ndim == 3
    r, c = w.shape[-2:]
    tr = _pick(r, (256, IN_SHARD // 3, 128, 64, 32, 16, 8))
    bc1 = 1.0 - ADAM_B1 ** ADAM_STEP
    bc2 = 1.0 - ADAM_B2 ** ADAM_STEP

    def body(w_ref, g_ref, m_ref, v_ref, d_ref, nm_ref, nv_ref):
        gv = g_ref[...]
        mn = ADAM_B1 * m_ref[...] + (1.0 - ADAM_B1) * gv
        vn = ADAM_B2 * v_ref[...] + (1.0 - ADAM_B2) * (gv * gv)
        m_hat = mn / bc1
        v_hat = vn / bc2
        d_ref[...] = -ADAM_LR * (m_hat / (jnp.sqrt(v_hat) + ADAM_EPS) + ADAM_WD * w_ref[...])
        nm_ref[...] = mn
        nv_ref[...] = vn

    flat = pl.BlockSpec((tr, c), lambda i: (i, 0))
    blk = pl.BlockSpec((None, tr, c), lambda i: (0, i, 0)) if lead else flat
    return pl.pallas_call(
        body, name=name, grid=(r // tr,), in_specs=[blk, flat, blk, blk], out_specs=[blk] * 3,
        out_shape=[jax.ShapeDtypeStruct(w.shape, F32)] * 3,
        compiler_params=_cparams(("parallel",)),
    )(w, g, m, v)


def _sum_leading(parts, *, name, out_dtype=F32):
    k, r, c = parts.shape
    tr = _pick(r, (512, 256, 128, 96, 64, 32, 16, 8))

    def body(p_ref, o_ref):
        acc = p_ref[0].astype(F32)
        for i in range(1, k):
            acc = acc + p_ref[i].astype(F32)
        o_ref[...] = acc.astype(out_dtype)

    return pl.pallas_call(
        body, name=name, grid=(r // tr,),
        in_specs=[pl.BlockSpec((k, tr, c), lambda i: (0, i, 0))],
        out_specs=pl.BlockSpec((tr, c), lambda i: (i, 0)),
        out_shape=jax.ShapeDtypeStruct((r, c), out_dtype),
        compiler_params=_cparams(("parallel",)),
    )(parts)


def _add_pair(a, b, *, name):
    k, r, c = a.shape
    tr = _pick(r, (512, 256, 128))

    def body(a_ref, b_ref, o_ref):
        o_ref[...] = (a_ref[...].astype(F32) + b_ref[...].astype(F32)).astype(BF16)

    blk = pl.BlockSpec((None, tr, c), lambda j, i: (j, i, 0))
    return pl.pallas_call(
        body, name=name, grid=(k, r // tr), in_specs=[blk, blk], out_specs=blk,
        out_shape=jax.ShapeDtypeStruct((k, r, c), BF16),
        compiler_params=_cparams(("parallel", "parallel")),
    )(a, b)


ANY = pl.BlockSpec(memory_space=pl.ANY)


def _chip_peers(x, y):
    return [(1 - x, y, 2 * (1 - x) + y), (x, 1 - y, 2 * x + 1 - y), (1 - x, 1 - y, 2 * (1 - x) + 1 - y)]


def _gather_weights(blob, *, name):
    rows, cols = blob.shape
    half_rows = rows // 2

    def body(b_ref, o_ref, send_sems, recv_sems):
        x, y, c = lax.axis_index("x"), lax.axis_index("y"), lax.axis_index("c")
        me = 2 * x + y
        sibling = (x, y, 1 - c)
        peers = _chip_peers(x, y)

        def half(chip, hc):
            return o_ref.at[chip, pl.ds(hc * half_rows, half_rows), :]

        def copy(k, src, chip, hc, to):
            return pltpu.make_async_remote_copy(src_ref=src, dst_ref=half(chip, hc), send_sem=send_sems.at[k],
                                                recv_sem=recv_sems.at[k], device_id=to, device_id_type=MESH)

        my_half = b_ref.at[pl.ds(c * half_rows, half_rows), :]
        first = [copy(k, my_half, me, c, (px, py, c)) for k, (px, py, _) in enumerate(peers)]
        for cp in first:
            cp.start()
        passed = [copy(3 + k, half(pc, c), pc, c, sibling) for k, (_, _, pc) in enumerate(peers)]
        for k, (px, py, pc) in enumerate(peers):
            copy(k, my_half, pc, c, (px, py, c)).wait_recv()
            passed[k].start()
        for k, (_, _, pc) in enumerate(peers):
            copy(3 + k, half(pc, 1 - c), pc, 1 - c, sibling).wait_recv()
        for cp in first + passed:
            cp.wait_send()

    return pl.pallas_call(
        body, name=name, in_specs=[ANY], out_specs=ANY,
        out_shape=jax.ShapeDtypeStruct((N_CHIPS, rows, cols), BF16),
        scratch_shapes=[pltpu.SemaphoreType.DMA((6,)), pltpu.SemaphoreType.DMA((6,))],
    )(blob)


def _swap_halves(g, *, name):
    _, rows, cols = g.shape
    half_rows = rows // 2

    def body(g_ref, o_ref, send_sem, recv_sem):
        x, y, c = lax.axis_index("x"), lax.axis_index("y"), lax.axis_index("c")
        cp = pltpu.make_async_remote_copy(
            src_ref=g_ref.at[:, pl.ds((1 - c) * half_rows, half_rows), :], dst_ref=o_ref,
            send_sem=send_sem, recv_sem=recv_sem, device_id=(x, y, 1 - c), device_id_type=MESH)
        cp.start()
        cp.wait()

    return pl.pallas_call(
        body, name=name, in_specs=[ANY], out_specs=ANY,
        out_shape=jax.ShapeDtypeStruct((N_CHIPS, half_rows, cols), BF16),
        scratch_shapes=[pltpu.SemaphoreType.DMA, pltpu.SemaphoreType.DMA],
    )(g)


HBM_SPEC = pl.BlockSpec(memory_space=pltpu.HBM)
SEM_SPEC = pl.BlockSpec(memory_space=pltpu.SEMAPHORE)
SPLIT_EFFECT = pltpu.SideEffectType.DATAFLOW_SIDE_EFFECTING


def _gather_peers_copies(b_ref, land_ref, send_sems, recv_sems, sending):
    x, y, c = lax.axis_index("x"), lax.axis_index("y"), lax.axis_index("c")
    me = 2 * x + y
    half_rows = b_ref.shape[0] // 2
    src = b_ref.at[pl.ds(c * half_rows, half_rows), :]
    return [pltpu.make_async_remote_copy(
        src_ref=src, dst_ref=land_ref.at[me if sending else pc, pl.ds(c * half_rows, half_rows), :],
        send_sem=send_sems.at[k], recv_sem=recv_sems.at[k], device_id=(px, py, c), device_id_type=MESH)
        for k, (px, py, pc) in enumerate(_chip_peers(x, y))]


def _gather_start(blob, after, *, name):
    shape = (N_CHIPS,) + blob.shape

    def body(b_ref, land_ref, after_ref, send_sems, recv_sems, b_thru, land_thru, token):
        for cp in _gather_peers_copies(b_ref, land_ref, send_sems, recv_sems, True):
            cp.start()
        token[...] = jnp.zeros_like(token)

    return pl.pallas_call(
        body, name=name,
        out_shape=(pltpu.SemaphoreType.DMA((3,)), pltpu.SemaphoreType.DMA((3,)), pltpu.HBM(blob.shape, blob.dtype),
                   pltpu.HBM(shape, blob.dtype), jax.ShapeDtypeStruct((8, 128), F32)),
        in_specs=(HBM_SPEC, HBM_SPEC, ANY),
        out_specs=(SEM_SPEC, SEM_SPEC, HBM_SPEC, HBM_SPEC, pl.BlockSpec(memory_space=pltpu.VMEM)),
        input_output_aliases={0: 2, 1: 3},
        compiler_params=pltpu.CompilerParams(has_side_effects=SPLIT_EFFECT),
    )(pltpu.with_memory_space_constraint(blob, pltpu.HBM),
      pltpu.with_memory_space_constraint(lax.empty(shape, blob.dtype), pltpu.HBM), after)


def _gather_wait(send_sems, recv_sems, b_thru, land_thru, after, *, name):
    def body(b_ref, land_ref, send_sems, recv_sems, after_ref, b_dead, got_ref):
        for cp in _gather_peers_copies(b_ref, land_ref, send_sems, recv_sems, False):
            cp.wait_send()
            cp.wait_recv()

    return pl.pallas_call(
        body, name=name,
        out_shape=(pltpu.HBM(b_thru.shape, b_thru.dtype), pltpu.HBM(land_thru.shape, land_thru.dtype)),
        in_specs=(HBM_SPEC, HBM_SPEC, SEM_SPEC, SEM_SPEC, ANY), out_specs=(HBM_SPEC, HBM_SPEC),
        input_output_aliases={0: 0, 1: 1},
        compiler_params=pltpu.CompilerParams(has_side_effects=SPLIT_EFFECT),
    )(b_thru, land_thru, send_sems, recv_sems, after)


def _gather_forward(land, *, name):
    half_rows = land.shape[1] // 2

    def body(l_ref, o_ref, send_sems, recv_sems):
        x, y, c = lax.axis_index("x"), lax.axis_index("y"), lax.axis_index("c")
        cps = []
        for k, (_, _, pc) in enumerate(_chip_peers(x, y)):
            mine = pl.ds(c * half_rows, half_rows)
            cps.append(pltpu.make_async_remote_copy(
                src_ref=l_ref.at[pc, mine, :], dst_ref=o_ref.at[pc, mine, :], send_sem=send_sems.at[k],
                recv_sem=recv_sems.at[k], device_id=(x, y, 1 - c), device_id_type=MESH))
        for cp in cps:
            cp.start()
        for k, (_, _, pc) in enumerate(_chip_peers(x, y)):
            theirs = pl.ds((1 - c) * half_rows, half_rows)
            pltpu.make_async_remote_copy(
                src_ref=l_ref.at[pc, theirs, :], dst_ref=o_ref.at[pc, theirs, :], send_sem=send_sems.at[k],
                recv_sem=recv_sems.at[k], device_id=(x, y, 1 - c), device_id_type=MESH).wait_recv()
        for cp in cps:
            cp.wait_send()

    return pl.pallas_call(
        body, name=name, in_specs=[ANY], out_specs=ANY, input_output_aliases={0: 0},
        out_shape=jax.ShapeDtypeStruct(land.shape, land.dtype),
        scratch_shapes=[pltpu.SemaphoreType.DMA((3,)), pltpu.SemaphoreType.DMA((3,))],
    )(land)


def _exchange_peers_copies(p_ref, land_ref, send_sems, recv_sems, sending):
    x, y, c = lax.axis_index("x"), lax.axis_index("y"), lax.axis_index("c")
    me = 2 * x + y
    return [pltpu.make_async_remote_copy(src_ref=p_ref.at[pc], dst_ref=land_ref.at[me if sending else pc],
                                         send_sem=send_sems.at[k], recv_sem=recv_sems.at[k],
                                         device_id=(px, py, c), device_id_type=MESH)
            for k, (px, py, pc) in enumerate(_chip_peers(x, y))]


def _exchange_start(p, *, name):
    def body(p_ref, land_ref, send_sems, recv_sems, p_thru, land_thru, token):
        for cp in _exchange_peers_copies(p_ref, land_ref, send_sems, recv_sems, True):
            cp.start()
        token[...] = jnp.zeros_like(token)

    return pl.pallas_call(
        body, name=name,
        out_shape=(pltpu.SemaphoreType.DMA((3,)), pltpu.SemaphoreType.DMA((3,)), pltpu.HBM(p.shape, p.dtype),
                   pltpu.HBM(p.shape, p.dtype), jax.ShapeDtypeStruct((8, 128), F32)),
        in_specs=(HBM_SPEC, HBM_SPEC),
        out_specs=(SEM_SPEC, SEM_SPEC, HBM_SPEC, HBM_SPEC, pl.BlockSpec(memory_space=pltpu.VMEM)),
        input_output_aliases={0: 2, 1: 3},
        compiler_params=pltpu.CompilerParams(has_side_effects=SPLIT_EFFECT),
    )(pltpu.with_memory_space_constraint(p, pltpu.HBM),
      pltpu.with_memory_space_constraint(lax.empty(p.shape, p.dtype), pltpu.HBM))


def _exchange_wait(send_sems, recv_sems, p_thru, land_thru, after, *, name):
    def body(p_ref, land_ref, send_sems, recv_sems, after_ref, p_dead, got_ref):
        for cp in _exchange_peers_copies(p_ref, land_ref, send_sems, recv_sems, False):
            cp.wait_send()
            cp.wait_recv()

    return pl.pallas_call(
        body, name=name,
        out_shape=(pltpu.HBM(p_thru.shape, p_thru.dtype), pltpu.HBM(p_thru.shape, p_thru.dtype)),
        in_specs=(HBM_SPEC, HBM_SPEC, SEM_SPEC, SEM_SPEC, ANY), out_specs=(HBM_SPEC, HBM_SPEC),
        input_output_aliases={0: 0, 1: 1},
        compiler_params=pltpu.CompilerParams(has_side_effects=SPLIT_EFFECT),
    )(p_thru, land_thru, send_sems, recv_sems, after)


def _sum_parts(parts, own, *, name):
    k, r, c = parts.shape
    tr = _pick(r, (512, 256, 128))

    def body(p_ref, own_ref, o_ref):
        me = 2 * lax.axis_index("x") + lax.axis_index("y")
        acc = jnp.zeros((tr, c), F32)
        for i in range(k):
            acc = acc + jnp.where(me == i, own_ref[i], p_ref[i]).astype(F32)
        o_ref[...] = acc

    blk = pl.BlockSpec((k, tr, c), lambda i: (0, i, 0))
    return pl.pallas_call(
        body, name=name, grid=(r // tr,), in_specs=[blk, blk],
        out_specs=pl.BlockSpec((tr, c), lambda i: (i, 0)),
        out_shape=jax.ShapeDtypeStruct((r, c), F32),
        compiler_params=_cparams(("parallel",)),
    )(parts, own)


def _join_halves(gh, *, name):
    def body(g_ref, o_ref, send_sem, recv_sem):
        x, y, c = lax.axis_index("x"), lax.axis_index("y"), lax.axis_index("c")
        cp = pltpu.make_async_remote_copy(src_ref=g_ref, dst_ref=o_ref, send_sem=send_sem, recv_sem=recv_sem,
                                          device_id=(x, y, 1 - c), device_id_type=MESH)
        cp.start()
        cp.wait()

    other = pl.pallas_call(
        body, name=name, in_specs=[ANY], out_specs=ANY,
        out_shape=jax.ShapeDtypeStruct(gh.shape, F32),
        scratch_shapes=[pltpu.SemaphoreType.DMA, pltpu.SemaphoreType.DMA],
    )(gh)
    south = lax.axis_index("c") == 0
    return jnp.concatenate([jnp.where(south, gh, other), jnp.where(south, other, gh)], axis=0)


def _gather_small(s, *, name):
    rows = s.shape[0]

    def body(s_ref, o_ref, send_sems, recv_sems, local_sem):
        x, y, c = lax.axis_index("x"), lax.axis_index("y"), lax.axis_index("c")
        me = 4 * x + 2 * y + c
        mine = pltpu.make_async_copy(s_ref, o_ref.at[me], local_sem)
        mine.start()
        peers = []
        for k in range(1, 8):
            peers.append((1 - x if k & 4 else x, 1 - y if k & 2 else y, 1 - c if k & 1 else c))
        cps = [pltpu.make_async_remote_copy(src_ref=s_ref, dst_ref=o_ref.at[me], send_sem=send_sems.at[k],
                                            recv_sem=recv_sems.at[k], device_id=p, device_id_type=MESH)
               for k, p in enumerate(peers)]
        for cp in cps:
            cp.start()
        for k, (px, py, pc) in enumerate(peers):
            pltpu.make_async_remote_copy(src_ref=s_ref, dst_ref=o_ref.at[4 * px + 2 * py + pc],
                                         send_sem=send_sems.at[k], recv_sem=recv_sems.at[k],
                                         device_id=(px, py, pc), device_id_type=MESH).wait_recv()
        for cp in cps:
            cp.wait_send()
        mine.wait()

    return pl.pallas_call(
        body, name=name, in_specs=[ANY], out_specs=ANY,
        out_shape=jax.ShapeDtypeStruct((8, rows, 128), F32),
        scratch_shapes=[pltpu.SemaphoreType.DMA((7,)), pltpu.SemaphoreType.DMA((7,)), pltpu.SemaphoreType.DMA],
    )(s)


IN_SHARD = IN_WIDTH // N_CHIPS
IN_SHARD_PAD = 1536
UP_ROWS, DOWN_ROWS, OUT_ROWS = 1024, 1024, 512
REST_ROWS = UP_ROWS + DOWN_ROWS + OUT_ROWS


def _pack_in(w_in_s):
    return jnp.pad(w_in_s, ((0, 0), (0, IN_SHARD_PAD - IN_SHARD))).astype(BF16)


def _pack_rest(w_out_s, w_up_s, w_down_s):
    return jnp.concatenate([w_up_s, w_down_s, w_out_s], axis=0).astype(BF16)


def _unpack_rest(blob):
    return (blob[UP_ROWS + DOWN_ROWS:], blob[0:UP_ROWS], blob[UP_ROWS:UP_ROWS + DOWN_ROWS])


def _with_own(gathered, own):
    me = 2 * lax.axis_index("x") + lax.axis_index("y")
    return [jnp.where(me == j, own, gathered[j]) for j in range(N_CHIPS)]


def _full_w_in(g_in, own):
    return jnp.concatenate([s[:, :IN_SHARD] for s in _with_own(g_in, own)], axis=1)


def _full_rest(g_rest, own):
    parts = [_unpack_rest(s) for s in _with_own(g_rest, own)]
    w_out = jnp.concatenate([p[0] for p in parts], axis=0)
    w_up = jnp.concatenate([p[1] for p in parts], axis=1)
    w_down = jnp.concatenate([p[2] for p in parts], axis=0)
    return w_out, w_up, w_down


def _split_w_in(w_in):
    z_xbc = w_in[:, 0:2560]
    dt = w_in[:, 2560:2576]
    qkv = w_in[:, 2576:5648]
    f = w_in[:, 5648:5664]
    pad = jnp.zeros((w_in.shape[0], PA_WIDTH - 2592), w_in.dtype)
    return jnp.concatenate([z_xbc, dt, f, pad], axis=1), qkv


def _merge_w_in(d_a, d_qkv):
    return jnp.concatenate([d_a[:, 0:2560], d_a[:, 2560:2576], d_qkv, d_a[:, 2576:2592]], axis=1)


def _local_step(x3, target3, w_in, rest_weights, norm_mix_w, conv_w, conv_b, dt_bias, a_log, d_skip,
                ssd_norm_w, f_bias, norm_mlp_w, norm_final_w, first_after=None, early_grads=None, late_grads=None):
    bl, t, d = x3.shape
    n = bl * t
    x = x3.reshape(n, d)
    target = target3.reshape(n, d)
    w_a, w_qkv = _split_w_in(w_in)
    nfw = norm_final_w.reshape(1, d)
    dskip_e = jnp.repeat(d_skip, HEAD_DIM, axis=1)
    nb = t // ATT_BLOCK

    r1, r2, kt = min(n, 1024), min(n, 512), min(n, 512)
    if first_after is None:
        first_after = jnp.zeros((8, 128), F32)
    h0, rstd0, proj_a = _norm_mm(x, norm_mix_w, w_a, first_after, name="norm_mix_proj_a", tm=r2)
    qkv = _mm(h0, w_qkv, name="proj_qkv", tiles=(r2, QKV_WIDTH, D_MODEL), out_dtype=BF16)
    bias128 = jnp.concatenate([dt_bias, f_bias, jnp.zeros((1, 96), F32)], axis=1)
    alog128 = jnp.concatenate([a_log, jnp.zeros((1, 112), F32)], axis=1)
    dt, gate_d, acum, ccum = _prep(proj_a, bias128, alog128, bl, t)
    acum_t = acum[:, 0:16].T
    negc = jnp.pad(-ccum.reshape(bl, t, 8, 2).transpose(0, 2, 3, 1), ((0, 0), (0, 0), (0, 6), (0, 0)))
    xc = _conv_fwd(proj_a, conv_w, conv_b, bl, t)
    y_ssd, y_pre, hprev = _ssd_fwd(xc, proj_a, dt, acum, acum_t, dskip_e, ssd_norm_w, bl, t)
    y_att, lse = _attn_fwd(qkv, negc, bl, t)
    w_out, w_up, w_down = rest_weights(y_att)
    wo_s, wo_a = w_out[:SSD_WIDTH], w_out[SSD_WIDTH:]
    h1, h1n, rstd1 = _mm_norm_fwd(y_ssd, wo_s, y_att, wo_a, x, norm_mlp_w, name="out_proj_norm_mlp", tm=r2)
    up = _mm(h1n, w_up, name="mlp_up", tiles=(r2, D_FF, D_MODEL), out_dtype=BF16)
    dh2, dh2b, loss, d_nfw = _mm_final(up, w_down, h1, nfw, target, name="mlp_down_final_norm_loss", tm=r2,
                                       a_act="relu2")

    dup = _mm(dh2b, w_down, name="mlp_down_bwd_act", tiles=(r2, D_FF, D_MODEL), tb=True, epi_up=up, out_dtype=BF16)
    rest_shape = (N_CHIPS, REST_ROWS, D_MODEL)
    gb_rest = _mm(up, dh2b, name="mlp_down_bwd_w", tiles=(DOWN_ROWS, D_MODEL, kt), ta=True, a_act="relu2",
                  out_dtype=BF16, into=(rest_shape, (None, DOWN_ROWS, D_MODEL), lambda i, j, k: (i, 1, 0), None))
    dh1, dh1b, d_nmlp = _mm_norm_bwd([(dup, w_up)], h1, rstd1, norm_mlp_w, dh2, name="mlp_up_bwd_act_norm_mlp",
                                     tm=r2)
    gb_rest = _mm(h1n, dup, name="mlp_up_bwd_w", tiles=(D_MODEL, UP_ROWS, kt), ta=True, out_dtype=BF16,
                  into=(rest_shape, (None, D_MODEL, UP_ROWS), lambda i, j, k: (j, 0, 0), gb_rest))
    dys, do = _mm_two_halves(dh1b, w_out, name="out_proj_bwd_act", tm=r1)
    out_block = (UP_ROWS + DOWN_ROWS) // OUT_ROWS
    for half, (y_half, tag) in enumerate(((y_ssd, "ssd"), (y_att, "att"))):
        gb_rest = _mm(y_half, dh1b, name="out_proj_bwd_w_" + tag, tiles=(2 * OUT_ROWS, D_MODEL, kt), ta=True,
                      out_dtype=BF16, into=(rest_shape, (2, OUT_ROWS, D_MODEL),
                                            functools.partial(lambda i, j, k, h: (h, out_block, 0), h=half),
                                            gb_rest))
    token = jnp.zeros((8, 128), F32) if early_grads is None else early_grads(gb_rest)
    dq, dk, dv, dcb = _attn_bwd(qkv, do, y_att, lse, negc, token, bl, t)
    dc = jnp.pad(dcb[:, :, 0:2, :].transpose(0, 3, 1, 2).reshape(n, 16), ((0, 0), (16, 96)))
    dxc, dpa, ddt_raw, d_snw, d_dsk, d_alog, d_dtb = _ssd_bwd(dys, xc, proj_a, y_pre, hprev, dt, gate_d, acum,
                                                             acum_t, alog128, dskip_e, ssd_norm_w, bl, t)
    dpa, d_conv_w, d_conv_b = _conv_bwd(dxc, proj_a, conv_w, conv_b, dpa, bl, t)
    dproj_a, d_fb = _fpost(dc, gate_d, ddt_raw, dpa, bl, t)
    dqkv = jnp.concatenate([dq, dk, dv], axis=1)
    d_w_a = _mm(h0, dproj_a, name="proj_a_bwd_w", tiles=(1024, 896, kt), ta=True, out_dtype=BF16)
    d_w_qkv = _mm(h0, dqkv, name="proj_qkv_bwd_w", tiles=(1024, 1024, kt), ta=True, out_dtype=BF16)
    d_w_in = _merge_w_in(d_w_a, d_w_qkv)
    late_token = None if late_grads is None else late_grads(d_w_in)
    dx, _, d_nmix = _mm_norm_bwd([(dproj_a, w_a), (dqkv, w_qkv)], x, rstd0, norm_mix_w, dh1,
                                 name="proj_bwd_act_norm_mix", tm=min(n, 256), after=late_token)

    grads = dict(norm_mix_w=d_nmix, w_in=d_w_in, conv_w=d_conv_w, conv_b=d_conv_b,
                 dt_bias=d_dtb, a_log=d_alog, d_skip=d_dsk, ssd_norm_w=d_snw, f_bias=d_fb, rest=gb_rest,
                 norm_mlp_w=d_nmlp, norm_final_w=d_nfw)
    return dx.reshape(bl, t, d), loss, grads


SMALL_ORDER = ("norm_mix_w", "conv_w", "conv_b", "dt_bias", "a_log", "d_skip", "ssd_norm_w", "f_bias",
               "norm_mlp_w", "norm_final_w")
SMALL_SIZES = (1024, 4 * CONV_CH, CONV_CH, 16, 16, 16, 1024, 16, 1024, 1024)


def _pack_small(vals, rows):
    flat = jnp.concatenate([v.reshape(-1).astype(F32) for v in vals])
    return jnp.pad(flat, (0, rows * 128 - flat.shape[0])).reshape(rows, 128)


def _unpack_small(packed, sizes):
    flat = packed.reshape(-1)
    out, o = [], 0
    for s in sizes:
        out.append(flat[o:o + s])
        o += s
    return out


def kernel(x, norm_mix_w, w_in, conv_w, conv_b, dt_bias, a_log, d_skip, ssd_norm_w, f_bias, w_out, norm_mlp_w, w_up, w_down, norm_final_w, loss_target, m_norm_mix_w, m_w_in, m_conv_w, m_conv_b, m_dt_bias, m_a_log, m_d_skip, m_ssd_norm_w, m_f_bias, m_w_out, m_norm_mlp_w, m_w_up, m_w_down, m_norm_final_w, v_norm_mix_w, v_w_in, v_conv_w, v_conv_b, v_dt_bias, v_a_log, v_d_skip, v_ssd_norm_w, v_f_bias, v_w_out, v_norm_mlp_w, v_w_up, v_w_down, v_norm_final_w):
    chip = 2 * lax.axis_index("x") + lax.axis_index("y")
    cw = CONV_CH // N_CHIPS

    own_in = _pack_in(w_in[0])
    own_rest = _pack_rest(w_out[0], w_up[0], w_down[0])
    g_in = _gather_weights(own_in, name="gather_w_in")
    w_in_f = _full_w_in(g_in, own_in)
    *rest_handles, rest_token = _gather_start(own_rest, g_in, name="gather_start_rest")

    def rest_weights(after):
        _, landed = _gather_wait(*rest_handles, after, name="gather_wait_rest")
        return _full_rest(_gather_forward(landed, name="gather_forward_rest"), own_rest)
    small_all = _gather_small(_pack_small([conv_w[0]], 16), name="gather_conv_w")
    conv_w_f = jnp.concatenate([small_all[2 * j].reshape(-1)[:4 * cw].reshape(4, cw) for j in range(N_CHIPS)], axis=1)

    c = lax.axis_index("c")

    def chip_partial(gb, tag):
        half_rows = gb.shape[1] // 2
        from_sibling = _swap_halves(gb, name="grad_swap_halves_" + tag)
        my_half = lax.dynamic_slice_in_dim(gb, c * half_rows, half_rows, axis=1)
        return _add_pair(my_half, from_sibling, name="grad_add_sibling_" + tag)

    in_flight = {}

    def early_grads(gb_rest):
        part = chip_partial(gb_rest, "rest")
        *handles, token = _exchange_start(part, name="grad_exchange_start_rest")
        in_flight["rest"] = handles
        return token

    def late_grads(d_w_in):
        gb_in = jnp.stack([_pack_in(d_w_in[:, j * IN_SHARD:(j + 1) * IN_SHARD]) for j in range(N_CHIPS)])
        *handles, token = _exchange_start(chip_partial(gb_in, "in"), name="grad_exchange_start_in")
        in_flight["in"] = handles
        return token

    dx, loss_part, g = _local_step(x, loss_target, w_in_f, rest_weights, norm_mix_w, conv_w_f,
                                   conv_b, dt_bias, a_log, d_skip, ssd_norm_w, f_bias, norm_mlp_w, norm_final_w,
                                   first_after=rest_token, early_grads=early_grads, late_grads=late_grads)

    send_sems, recv_sems, part_rest, land_rest = in_flight["rest"]
    part_rest, parts_rest = _exchange_wait(send_sems, recv_sems, part_rest, land_rest, dx,
                                           name="grad_exchange_wait_rest")
    g_rest_half = _sum_parts(parts_rest, part_rest, name="grad_sum_chips_rest")
    g_w_out, g_w_up, g_w_down = _unpack_rest(_join_halves(g_rest_half, name="grad_join_halves_rest"))

    part_in, parts_in = _exchange_wait(*in_flight["in"], dx, name="grad_exchange_wait_in")
    g_in_half = _sum_parts(parts_in, part_in, name="grad_sum_chips_in")
    g_w_in = _join_halves(g_in_half, name="grad_join_halves_in")[:, :IN_SHARD]

    small_vals = [g[k] for k in SMALL_ORDER] + [loss_part[:, 0:1]]
    small_sum = _sum_leading(_gather_small(_pack_small(small_vals, SMALL_ROWS), name="gather_small_grads"), name="small_sum")
    sg = dict(zip(SMALL_ORDER + ("loss",), _unpack_small(small_sum, SMALL_SIZES + (1,))))
    loss = sg["loss"].reshape(())
    g_conv_full = sg["conv_w"].reshape(4, CONV_CH)
    g_conv = lax.dynamic_slice_in_dim(g_conv_full, chip * cw, cw, axis=1)

    grads = dict(norm_mix_w=sg["norm_mix_w"].reshape(1, -1), w_in=g_w_in[None], conv_w=g_conv[None],
                 conv_b=sg["conv_b"].reshape(1, -1), dt_bias=sg["dt_bias"].reshape(1, -1),
                 a_log=sg["a_log"].reshape(1, -1), d_skip=sg["d_skip"].reshape(1, -1),
                 ssd_norm_w=sg["ssd_norm_w"].reshape(1, -1), f_bias=sg["f_bias"].reshape(1, -1), w_out=g_w_out[None],
                 norm_mlp_w=sg["norm_mlp_w"].reshape(1, -1), w_up=g_w_up[None], w_down=g_w_down[None],
                 norm_final_w=sg["norm_final_w"])
    weights = dict(norm_mix_w=norm_mix_w, w_in=w_in, conv_w=conv_w, conv_b=conv_b, dt_bias=dt_bias, a_log=a_log,
                   d_skip=d_skip, ssd_norm_w=ssd_norm_w, f_bias=f_bias, w_out=w_out, norm_mlp_w=norm_mlp_w,
                   w_up=w_up, w_down=w_down, norm_final_w=norm_final_w)
    ms = dict(norm_mix_w=m_norm_mix_w, w_in=m_w_in, conv_w=m_conv_w, conv_b=m_conv_b, dt_bias=m_dt_bias,
              a_log=m_a_log, d_skip=m_d_skip, ssd_norm_w=m_ssd_norm_w, f_bias=m_f_bias, w_out=m_w_out,
              norm_mlp_w=m_norm_mlp_w, w_up=m_w_up, w_down=m_w_down, norm_final_w=m_norm_final_w)
    vs = dict(norm_mix_w=v_norm_mix_w, w_in=v_w_in, conv_w=v_conv_w, conv_b=v_conv_b, dt_bias=v_dt_bias,
              a_log=v_a_log, d_skip=v_d_skip, ssd_norm_w=v_ssd_norm_w, f_bias=v_f_bias, w_out=v_w_out,
              norm_mlp_w=v_norm_mlp_w, w_up=v_w_up, w_down=v_w_down, norm_final_w=v_norm_final_w)
    names = list(weights)
    big = ("w_in", "w_out", "w_up", "w_down")
    delta, new_m, new_v = {}, {}, {}
    for k, g2 in zip(big[1:], (g_w_out, g_w_up, g_w_down)):
        delta[k], new_m[k], new_v[k] = _adamw(weights[k], g2, ms[k], vs[k], name="adamw_" + k)
    g_in_t = g_w_in.T
    outs_t = _adamw(w_in[0].T, g_in_t, m_w_in[0].T, v_w_in[0].T, name="adamw_w_in")
    delta["w_in"], new_m["w_in"], new_v["w_in"] = [o.T[None] for o in outs_t]
    grads["w_in"] = g_in_t.T[None]
    smalls = [k for k in names if k not in big]
    sizes = [math.prod(weights[k].shape) for k in smalls]
    rows = -(-sum(sizes) // 1024) * 8
    packs = [_pack_small([d[k] for k in smalls], rows) for d in (weights, grads, ms, vs)]
    outs = _adamw(*packs, name="adamw_small")
    for o, dst in zip(outs, (delta, new_m, new_v)):
        for k, val in zip(smalls, _unpack_small(o, sizes)):
            dst[k] = val.reshape(weights[k].shape)
    return (loss, dx, *[grads[k] for k in names], *[delta[k] for k in names], *[new_m[k] for k in names],
            *[new_v[k] for k in names])
```

```python
import functools
import math

import jax
import jax.numpy as jnp
from jax import lax
from jax.experimental import pallas as pl
from jax.experimental.pallas import tpu as pltpu

F32 = jnp.float32
BF16 = jnp.bfloat16
HIGHEST = lax.Precision.HIGHEST
MESH = pl.DeviceIdType.MESH

D_MODEL = 1024
SSD_HEADS = 16
HEAD_DIM = 64
SSD_WIDTH = 1024
SSD_STATE = 128
CONV_CH = 1536
CHUNK = 128
ATT_WIDTH = 1024
EPS = 1e-5
IN_WIDTH = 5664
PA_WIDTH = 2688
QKV_WIDTH = 3072
D_FF = 4096
ATT_BLOCK = 256
NEG = -1e30
LOG2E = 1.4426950408889634
VMEM_LIMIT = 48 * 1024 * 1024

ADAM_LR = 0.001
ADAM_B1 = 0.9
ADAM_B2 = 0.999
ADAM_EPS = 1e-08
ADAM_WD = 0.01
ADAM_STEP = 10

N_CHIPS = 4
SMALL_ROWS = 96


def _cparams(sem):
    return pltpu.CompilerParams(dimension_semantics=sem, vmem_limit_bytes=VMEM_LIMIT)


def _pick(n, cands):
    for c in cands:
        if n % c == 0:
            return c
    return n


MM_CHUNK = 512


def _mm(a, b, *, name, tiles, ta=False, tb=False, out_dtype=F32, res=None, a_act=None, epi_up=None, after=None,
        into=None):
    n_unread = (after is not None) + (into is not None and into[3] is not None)
    if ta:
        K, M = a.shape
    else:
        M, K = a.shape
    if tb:
        N, K2 = b.shape
    else:
        K2, N = b.shape
    assert K == K2, (a.shape, b.shape)
    tm, tn, tk = tiles
    assert M % tm == 0 and N % tn == 0 and K % tk == 0, (name, M, N, K, tiles)
    nk = K // tk
    dn = (((0 if ta else 1,), (1 if tb else 0,)), ((), ()))
    has_res = res is not None
    has_up = epi_up is not None
    cn = _pick(tn, (MM_CHUNK, 384, 256, 128))

    def prologue(av):
        if a_act == "relu2":
            r = jnp.maximum(av.astype(F32), 0.0)
            av = r * r
        return av.astype(BF16)

    def epilogue(out, res_v, up_v):
        if has_res:
            out = out + res_v.astype(F32)
        if has_up:
            out = out * (2.0 * jnp.maximum(up_v.astype(F32), 0.0))
        return out.astype(out_dtype)

    def body(*refs):
        a_ref, b_ref = refs[0], refs[1]
        i = 2
        res_ref = up_ref = None
        if has_res:
            res_ref = refs[i]
            i += 1
        if has_up:
            up_ref = refs[i]
            i += 1
        i += n_unread
        o_ref = refs[i]
        if nk == 1:
            av = prologue(a_ref[...])
            for c in range(tn // cn):
                cs = slice(c * cn, (c + 1) * cn)
                bv = (b_ref[cs, :] if tb else b_ref[:, cs]).astype(BF16)
                out = lax.dot_general(av, bv, dn, preferred_element_type=F32)
                o_ref[:, cs] = epilogue(out, res_ref[:, cs] if has_res else None, up_ref[:, cs] if has_up else None)
            return
        acc_ref = refs[i + 1]
        k = pl.program_id(2)

        @pl.when(k == 0)
        def _():
            acc_ref[...] = jnp.zeros_like(acc_ref)

        acc_ref[...] += lax.dot_general(prologue(a_ref[...]), b_ref[...].astype(BF16), dn,
                                        preferred_element_type=F32)

        @pl.when(k == nk - 1)
        def _():
            out = epilogue(acc_ref[...], res_ref[...] if has_res else None, up_ref[...] if has_up else None)
            o_ref[...] = out.reshape(o_ref.shape)

    a_spec = pl.BlockSpec((tk, tm), lambda i, j, k: (k, i)) if ta else pl.BlockSpec((tm, tk), lambda i, j, k: (i, k))
    b_spec = pl.BlockSpec((tn, tk), lambda i, j, k: (j, k)) if tb else pl.BlockSpec((tk, tn), lambda i, j, k: (k, j))
    o_spec = pl.BlockSpec((tm, tn), lambda i, j, k: (i, j))
    ins, specs = [a, b], [a_spec, b_spec]
    if has_res:
        ins.append(res)
        specs.append(o_spec)
    if has_up:
        ins.append(epi_up)
        specs.append(o_spec)
    if after is not None:
        ins.append(after)
        specs.append(pl.BlockSpec(memory_space=pl.ANY))
    out_shape, out_spec, aliases = jax.ShapeDtypeStruct((M, N), out_dtype), o_spec, {}
    if into is not None:
        shape, block, index, buf = into
        out_shape, out_spec = jax.ShapeDtypeStruct(shape, out_dtype), pl.BlockSpec(block, index)
        if buf is not None:
            aliases = {len(ins): 0}
            ins.append(buf)
            specs.append(pl.BlockSpec(memory_space=pl.ANY))
    return pl.pallas_call(
        body, name=name, grid=(M // tm, N // tn, nk),
        in_specs=specs, out_specs=out_spec, out_shape=out_shape, input_output_aliases=aliases,
        scratch_shapes=[] if nk == 1 else [pltpu.VMEM((tm, tn), F32)],
        compiler_params=_cparams(("parallel", "parallel", "arbitrary")),
    )(*ins)


def _rows_product(a_ref, b_ref, tb, a_act):
    av = a_ref[...]
    if a_act == "relu2":
        r = jnp.maximum(av.astype(F32), 0.0)
        av = r * r
    dn = (((1,), (1 if tb else 0,)), ((), ()))
    return lax.dot_general(av.astype(BF16), b_ref[...].astype(BF16), dn, preferred_element_type=F32)


def _norm_mm(x, w, b, after, *, name, tm):
    m, d = x.shape
    n = b.shape[1]
    cn = _pick(n, (MM_CHUNK, 384, 256, 128))

    def body(x_ref, w_ref, b_ref, after_ref, h_ref, r_ref, o_ref):
        xv = x_ref[...]
        rstd = lax.rsqrt(jnp.mean(xv * xv, axis=1, keepdims=True) + EPS)
        hv = (xv * rstd * w_ref[...]).astype(BF16)
        h_ref[...] = hv
        r_ref[...] = rstd
        for c in range(n // cn):
            cs = slice(c * cn, (c + 1) * cn)
            o_ref[:, cs] = jnp.dot(hv, b_ref[:, cs].astype(BF16), preferred_element_type=F32)

    row = pl.BlockSpec((tm, d), lambda i: (i, 0))
    return pl.pallas_call(
        body, name=name, grid=(m // tm,),
        in_specs=[row, pl.BlockSpec((1, d), lambda i: (0, 0)), pl.BlockSpec((d, n), lambda i: (0, 0)),
                  pl.BlockSpec(memory_space=pl.ANY)],
        out_specs=[row, pl.BlockSpec((tm, 1), lambda i: (i, 0)), pl.BlockSpec((tm, n), lambda i: (i, 0))],
        out_shape=[jax.ShapeDtypeStruct((m, d), BF16), jax.ShapeDtypeStruct((m, 1), F32),
                   jax.ShapeDtypeStruct((m, n), F32)],
        compiler_params=_cparams(("parallel",)),
    )(x, w, b, after)


def _mm_norm_fwd(a1, b1, a2, b2, res, w, *, name, tm):
    m, k1 = a1.shape
    k2 = a2.shape[1]
    d = b1.shape[1]

    def body(a1_ref, b1_ref, a2_ref, b2_ref, res_ref, w_ref, h_ref, y_ref, r_ref):
        hv = _rows_product(a1_ref, b1_ref, False, None) + _rows_product(a2_ref, b2_ref, False, None) + res_ref[...]
        rstd = lax.rsqrt(jnp.mean(hv * hv, axis=1, keepdims=True) + EPS)
        h_ref[...] = hv
        y_ref[...] = (hv * rstd * w_ref[...]).astype(BF16)
        r_ref[...] = rstd

    row = pl.BlockSpec((tm, d), lambda i: (i, 0))
    return pl.pallas_call(
        body, name=name, grid=(m // tm,),
        in_specs=[pl.BlockSpec((tm, k1), lambda i: (i, 0)), pl.BlockSpec((k1, d), lambda i: (0, 0)),
                  pl.BlockSpec((tm, k2), lambda i: (i, 0)), pl.BlockSpec((k2, d), lambda i: (0, 0)), row,
                  pl.BlockSpec((1, d), lambda i: (0, 0))],
        out_specs=[row, row, pl.BlockSpec((tm, 1), lambda i: (i, 0))],
        out_shape=[jax.ShapeDtypeStruct((m, d), F32), jax.ShapeDtypeStruct((m, d), BF16),
                   jax.ShapeDtypeStruct((m, 1), F32)],
        compiler_params=_cparams(("parallel",)),
    )(a1, b1, a2, b2, res, w)


def _mm_final(a, b, res, w, target, *, name, tm, a_act):
    m, k = a.shape
    d = b.shape[1]

    def body(a_ref, b_ref, res_ref, w_ref, t_ref, dh_ref, dhb_ref, loss_ref, dw_ref):
        @pl.when(pl.program_id(0) == 0)
        def _():
            loss_ref[...] = jnp.zeros_like(loss_ref)
            dw_ref[...] = jnp.zeros_like(dw_ref)

        hv = _rows_product(a_ref, b_ref, False, a_act) + res_ref[...]
        wv = w_ref[...]
        rstd = lax.rsqrt(jnp.mean(hv * hv, axis=1, keepdims=True) + EPS)
        xhat = hv * rstd
        err = xhat * wv - t_ref[...]
        loss_ref[...] += 0.5 * jnp.sum(jnp.mean(err * err, axis=1, keepdims=True), axis=0, keepdims=True)
        dy = err * (1.0 / d)
        gw = dy * wv
        dh = rstd * (gw - xhat * jnp.mean(gw * xhat, axis=1, keepdims=True))
        dh_ref[...] = dh
        dhb_ref[...] = dh.astype(BF16)
        dw_ref[...] += jnp.sum(dy * xhat, axis=0, keepdims=True)

    row = pl.BlockSpec((tm, d), lambda i: (i, 0))
    vec = pl.BlockSpec((1, d), lambda i: (0, 0))
    return pl.pallas_call(
        body, name=name, grid=(m // tm,),
        in_specs=[pl.BlockSpec((tm, k), lambda i: (i, 0)), pl.BlockSpec((k, d), lambda i: (0, 0)), row, vec, row],
        out_specs=[row, row, pl.BlockSpec((1, 128), lambda i: (0, 0)), vec],
        out_shape=[jax.ShapeDtypeStruct((m, d), F32), jax.ShapeDtypeStruct((m, d), BF16),
                   jax.ShapeDtypeStruct((1, 128), F32), jax.ShapeDtypeStruct((1, d), F32)],
        compiler_params=_cparams(("arbitrary",)),
    )(a, b, res, w, target)


def _mm_two_halves(a, b, *, name, tm):
    m, k = a.shape
    d = b.shape[0] // 2

    def body(a_ref, b_ref, lo_ref, hi_ref):
        av = a_ref[...].astype(BF16)
        lo_ref[...] = lax.dot_general(av, b_ref[0:d, :].astype(BF16), NT_DIMS, preferred_element_type=F32)
        hi_ref[...] = lax.dot_general(av, b_ref[d:2 * d, :].astype(BF16), NT_DIMS,
                                      preferred_element_type=F32).astype(BF16)

    row = pl.BlockSpec((tm, d), lambda i: (i, 0))
    return pl.pallas_call(
        body, name=name, grid=(m // tm,),
        in_specs=[pl.BlockSpec((tm, k), lambda i: (i, 0)), pl.BlockSpec((2 * d, k), lambda i: (0, 0))],
        out_specs=[row, row],
        out_shape=[jax.ShapeDtypeStruct((m, d), F32), jax.ShapeDtypeStruct((m, d), BF16)],
        compiler_params=_cparams(("parallel",)),
    )(a, b)


def _mm_norm_bwd(pairs, x, rstd, w, dres, *, name, tm, after=None):
    m = pairs[0][0].shape[0]
    d = pairs[0][1].shape[0]
    n_pairs = len(pairs)

    def body(*refs):
        i = 2 * n_pairs
        x_ref, r_ref, w_ref, d_ref = refs[i:i + 4]
        dx_ref, dxb_ref, dw_ref = refs[-3:]

        @pl.when(pl.program_id(0) == 0)
        def _():
            dw_ref[...] = jnp.zeros_like(dw_ref)

        g = _rows_product(refs[0], refs[1], True, None)
        for p in range(1, n_pairs):
            g = g + _rows_product(refs[2 * p], refs[2 * p + 1], True, None)
        r = r_ref[...]
        xhat = x_ref[...] * r
        gw = g * w_ref[...]
        dx = d_ref[...] + r * (gw - xhat * jnp.mean(gw * xhat, axis=1, keepdims=True))
        dx_ref[...] = dx
        dxb_ref[...] = dx.astype(BF16)
        dw_ref[...] += jnp.sum(g * xhat, axis=0, keepdims=True)

    row = pl.BlockSpec((tm, d), lambda i: (i, 0))
    vec = pl.BlockSpec((1, d), lambda i: (0, 0))
    ins, specs = [], []
    for a, b in pairs:
        k = a.shape[1]
        ins += [a, b]
        specs += [pl.BlockSpec((tm, k), lambda i: (i, 0)), pl.BlockSpec((d, k), lambda i: (0, 0))]
    ins += [x, rstd, w, dres]
    specs += [row, pl.BlockSpec((tm, 1), lambda i: (i, 0)), vec, row]
    if after is not None:
        ins.append(after)
        specs.append(pl.BlockSpec(memory_space=pl.ANY))
    return pl.pallas_call(
        body, name=name, grid=(m // tm,), in_specs=specs, out_specs=[row, row, vec],
        out_shape=[jax.ShapeDtypeStruct((m, d), F32), jax.ShapeDtypeStruct((m, d), BF16),
                   jax.ShapeDtypeStruct((1, d), F32)],
        compiler_params=_cparams(("arbitrary",)),
    )(*ins)


def _softplus(x):
    return jnp.maximum(x, 0.0) + jnp.log(1.0 + jnp.exp(-jnp.abs(x)))


def _prep(proj_a, bias128, alog128, bl, t):
    n = bl * t
    nch = t // CHUNK
    col0 = (SSD_WIDTH + CONV_CH) // 128

    def body(p_ref, b_ref, al_ref, dt_ref, gd_ref, ac_ref, c_ref):
        row = lax.broadcasted_iota(jnp.int32, (CHUNK, CHUNK), 0)
        col = lax.broadcasted_iota(jnp.int32, (CHUNK, CHUNK), 1)
        tril = (row >= col).astype(F32)
        lane = lax.broadcasted_iota(jnp.int32, (1, 128), 1)
        head_lanes = lane < 16
        a_row = -jnp.exp(al_ref[...])
        carry = jnp.zeros((1, 128), F32)
        for ci in range(nch):
            rows = slice(ci * CHUNK, (ci + 1) * CHUNK)
            xv = p_ref[rows, :] + b_ref[...]
            sp = _softplus(xv)
            acum = jnp.dot(tril, a_row * sp, precision=HIGHEST, preferred_element_type=F32)
            c = jnp.dot(tril, -_softplus(-xv), precision=HIGHEST, preferred_element_type=F32) + carry
            carry = c[CHUNK - 1:CHUNK, :]
            dt_ref[rows, :] = jnp.where(head_lanes, sp, 0.0)
            gd_ref[rows, :] = jnp.where(head_lanes, jax.nn.sigmoid(xv),
                                        jnp.where(lane < 32, jax.nn.sigmoid(-xv), 0.0))
            ac_ref[rows, :] = jnp.where(head_lanes, acum, 0.0)
            c_ref[rows, :] = c[:, 16:32]

    o128 = pl.BlockSpec((t, 128), lambda b: (b, 0))
    v128 = pl.BlockSpec((1, 128), lambda b: (0, 0))
    w128 = jax.ShapeDtypeStruct((n, 128), F32)
    return pl.pallas_call(
        body, name="head_scalars", grid=(bl,),
        in_specs=[pl.BlockSpec((t, 128), lambda b: (b, col0)), v128, v128],
        out_specs=[o128, o128, o128, pl.BlockSpec((t, 16), lambda b: (b, 0))],
        out_shape=[w128, w128, w128, jax.ShapeDtypeStruct((n, 16), F32)],
        compiler_params=_cparams(("parallel",)),
    )(proj_a, bias128, alog128)


def _fpost(dc, gate_d, ddt, dpa, bl, t):
    n = bl * t
    nch = t // CHUNK
    col0 = (SSD_WIDTH + CONV_CH) // 128

    def body(dc_ref, gd_ref, ddt_ref, dpa_in, out_ref, db_ref):
        @pl.when(pl.program_id(0) == 0)
        def _():
            db_ref[...] = jnp.zeros_like(db_ref)

        row = lax.broadcasted_iota(jnp.int32, (CHUNK, CHUNK), 0)
        col = lax.broadcasted_iota(jnp.int32, (CHUNK, CHUNK), 1)
        triu = (row <= col).astype(F32)
        lane = lax.broadcasted_iota(jnp.int32, (1, 128), 1)
        gate_lanes = (lane >= 16) & (lane < 32)
        carry = jnp.zeros((1, 128), F32)
        db = jnp.zeros((1, 128), F32)
        for ci in reversed(range(nch)):
            rows = slice(ci * CHUNK, (ci + 1) * CHUNK)
            dlf = jnp.dot(triu, dc_ref[rows, :], precision=HIGHEST, preferred_element_type=F32) + carry
            carry = dlf[0:1, :]
            df = jnp.where(gate_lanes, dlf * gd_ref[rows, :], 0.0)
            out_ref[rows, :] = (ddt_ref[rows, :] + df).astype(BF16)
            db = db + jnp.sum(df, axis=0, keepdims=True)
        db_ref[...] += db[:, 16:32]

    blk = pl.BlockSpec((t, 128), lambda b: (b, 0))
    return pl.pallas_call(
        body, name="forget_gate_bwd", grid=(bl,),
        in_specs=[blk, blk, blk, ANY],
        out_specs=[pl.BlockSpec((t, 128), lambda b: (b, col0)), pl.BlockSpec((1, 16), lambda b: (0, 0))],
        out_shape=[jax.ShapeDtypeStruct(dpa.shape, dpa.dtype), jax.ShapeDtypeStruct((1, 16), F32)],
        input_output_aliases={3: 0},
        compiler_params=_cparams(("arbitrary",)),
    )(dc, gate_d, ddt, dpa)


CONV_TILE = 256
CONV_ROWS = 256


def _conv_taps(u_ref, i, w, bias):
    r0 = pl.multiple_of(i * CONV_ROWS, CONV_ROWS)
    cur = u_ref[pl.ds(r0, CONV_ROWS), :]
    p0 = pl.multiple_of(jnp.maximum(r0 - 8, 0), 8)
    prev = jnp.where(i > 0, u_ref[pl.ds(p0, 8), :], 0.0)
    cat = jnp.concatenate([prev, cur], axis=0)
    pre = bias + w[3:4, :] * cur
    taps = [cur]
    for s in (1, 2, 3):
        sh = pltpu.roll(cat, s, 0)[8:, :]
        taps.append(sh)
        pre = pre + w[3 - s:4 - s, :] * sh
    return r0, pre, taps


def _conv_fwd(proj_a, conv_w, conv_b, bl, t):
    n = bl * t
    nct = CONV_CH // CONV_TILE
    c0 = SSD_WIDTH // CONV_TILE

    def body(u_ref, w_ref, b_ref, o_ref):
        w = w_ref[...]
        bias = b_ref[...]

        def chunk(i, carry):
            r0, pre, _ = _conv_taps(u_ref, i, w, bias)
            o_ref[pl.ds(r0, CONV_ROWS), :] = pre * jax.nn.sigmoid(pre)
            return carry

        lax.fori_loop(0, t // CONV_ROWS, chunk, 0)

    return pl.pallas_call(
        body, name="conv_silu_fwd", grid=(bl, nct),
        in_specs=[pl.BlockSpec((t, CONV_TILE), lambda b, c: (b, c0 + c)),
                  pl.BlockSpec((4, CONV_TILE), lambda b, c: (0, c)),
                  pl.BlockSpec((1, CONV_TILE), lambda b, c: (0, c))],
        out_specs=pl.BlockSpec((t, CONV_TILE), lambda b, c: (b, c)),
        out_shape=jax.ShapeDtypeStruct((n, CONV_CH), F32),
        compiler_params=_cparams(("parallel", "parallel")),
    )(proj_a, conv_w, conv_b)


def _conv_bwd(dxc, proj_a, conv_w, conv_b, dpa, bl, t):
    nct = CONV_CH // CONV_TILE
    c0 = SSD_WIDTH // CONV_TILE
    nrc = t // CONV_ROWS

    def body(g_ref, u_ref, w_ref, b_ref, dpa_in, du_ref, dw_ref, db_ref, dp_scr):
        @pl.when(pl.program_id(1) == 0)
        def _():
            dw_ref[...] = jnp.zeros_like(dw_ref)
            db_ref[...] = jnp.zeros_like(db_ref)

        w = w_ref[...]
        bias = b_ref[...]
        dp_scr[pl.ds(t, 8), :] = jnp.zeros((8, CONV_TILE), F32)

        def chunk1(i, carry):
            dw0, dw1, dw2, dw3, db = carry
            r0, pre, taps = _conv_taps(u_ref, i, w, bias)
            sg = jax.nn.sigmoid(pre)
            dpre = g_ref[pl.ds(r0, CONV_ROWS), :] * (sg * (1.0 + pre * (1.0 - sg)))
            dp_scr[pl.ds(r0, CONV_ROWS), :] = dpre
            dw3 = dw3 + jnp.sum(dpre * taps[0], axis=0, keepdims=True)
            dw2 = dw2 + jnp.sum(dpre * taps[1], axis=0, keepdims=True)
            dw1 = dw1 + jnp.sum(dpre * taps[2], axis=0, keepdims=True)
            dw0 = dw0 + jnp.sum(dpre * taps[3], axis=0, keepdims=True)
            db = db + jnp.sum(dpre, axis=0, keepdims=True)
            return dw0, dw1, dw2, dw3, db

        z = jnp.zeros((1, CONV_TILE), F32)
        dw0, dw1, dw2, dw3, db = lax.fori_loop(0, nrc, chunk1, (z, z, z, z, z))
        dw_ref[...] += jnp.concatenate([dw0, dw1, dw2, dw3], axis=0)
        db_ref[...] += db

        def chunk2(i, carry):
            r0 = pl.multiple_of(i * CONV_ROWS, CONV_ROWS)
            cat = dp_scr[pl.ds(r0, CONV_ROWS + 8), :]
            du = w[3:4, :] * cat[:CONV_ROWS, :]
            for s in (1, 2, 3):
                du = du + w[3 - s:4 - s, :] * pltpu.roll(cat, CONV_ROWS + 8 - s, 0)[:CONV_ROWS, :]
            du_ref[pl.ds(r0, CONV_ROWS), :] = du.astype(BF16)
            return carry

        lax.fori_loop(0, nrc, chunk2, 0)

    return pl.pallas_call(
        body, name="conv_silu_bwd", grid=(nct, bl),
        in_specs=[pl.BlockSpec((t, CONV_TILE), lambda c, b: (b, c)),
                  pl.BlockSpec((t, CONV_TILE), lambda c, b: (b, c0 + c)),
                  pl.BlockSpec((4, CONV_TILE), lambda c, b: (0, c)),
                  pl.BlockSpec((1, CONV_TILE), lambda c, b: (0, c)), ANY],
        out_specs=[pl.BlockSpec((t, CONV_TILE), lambda c, b: (b, c0 + c)),
                   pl.BlockSpec((4, CONV_TILE), lambda c, b: (0, c)),
                   pl.BlockSpec((1, CONV_TILE), lambda c, b: (0, c))],
        out_shape=[jax.ShapeDtypeStruct(dpa.shape, dpa.dtype), jax.ShapeDtypeStruct((4, CONV_CH), F32),
                   jax.ShapeDtypeStruct((1, CONV_CH), F32)],
        input_output_aliases={4: 0},
        scratch_shapes=[pltpu.VMEM((t + 8, CONV_TILE), F32)],
        compiler_params=_cparams(("parallel", "arbitrary")),
    )(dxc, proj_a, conv_w, conv_b, dpa)


NT_DIMS = (((1,), (1,)), ((), ()))
TN_DIMS = (((0,), (0,)), ((), ()))


def _dot(a, b, dims=None):
    if dims is None:
        return jnp.dot(a, b, preferred_element_type=F32)
    return lax.dot_general(a, b, dims, preferred_element_type=F32)


def _head_expander():
    r = lax.broadcasted_iota(jnp.int32, (128, SSD_WIDTH), 0)
    c = lax.broadcasted_iota(jnp.int32, (128, SSD_WIDTH), 1)
    return ((c // HEAD_DIM == r % 16) & (r < 48)).astype(BF16)


def _spread(v128, expander):
    hi = v128.astype(BF16).astype(F32)
    r1 = v128 - hi
    mid = r1.astype(BF16).astype(F32)
    lo = (r1 - mid).astype(BF16).astype(F32)
    packed = (hi + pltpu.roll(mid, 16, 1) + pltpu.roll(lo, 32, 1)).astype(BF16)
    return jnp.dot(packed, expander, preferred_element_type=F32)


def _head_sums(v1024, expander):
    hi = v1024.astype(BF16)
    lo = (v1024 - hi.astype(F32)).astype(BF16)
    heads = jnp.where(lax.broadcasted_iota(jnp.int32, expander.shape, 0) < 16, expander, jnp.zeros_like(expander))
    return _dot(hi, heads, NT_DIMS) + _dot(lo, heads, NT_DIMS)


def _ssd_fwd(xc, proj_a, dt, acum, acum_t, dskip_e, norm_w, bl, t):
    n = bl * t
    nch = t // CHUNK
    L = CHUNK

    def body(xc_ref, z_ref, dt_ref, ac_ref, act_ref, dsk_ref, nw_ref, ys_ref, yp_ref, hp_ref, h_scr, y_scr, x_scr):
        @pl.when(pl.program_id(1) == 0)
        def _():
            h_scr[...] = jnp.zeros_like(h_scr)

        row = lax.broadcasted_iota(jnp.int32, (L, L), 0)
        col = lax.broadcasted_iota(jnp.int32, (L, L), 1)
        causal = row >= col
        lane128 = lax.broadcasted_iota(jnp.int32, (1, L), 1)
        expander = _head_expander()
        ac_all = ac_ref[...]
        act_all = act_ref[...]
        ac_e = _spread(ac_all, expander)
        e_in = jnp.exp(ac_e)
        dec = jnp.exp(ac_e[L - 1:L, :] - ac_e)
        xs_all = xc_ref[:, 0:SSD_WIDTH]
        x_all = xs_all * _spread(dt_ref[...], expander)
        x_scr[...] = x_all.astype(BF16)
        hp_all = h_scr[...]
        hp_ref[...] = hp_all
        for g in range(2):
            gs = slice(g * 512, (g + 1) * 512)
            bg = xc_ref[:, SSD_WIDTH + g * 128:SSD_WIDTH + (g + 1) * 128].astype(BF16)
            cg = xc_ref[:, SSD_WIDTH + 256 + g * 128:SSD_WIDTH + 256 + (g + 1) * 128].astype(BF16)
            gmat = _dot(cg, bg, NT_DIMS)
            y_off = _dot(cg, hp_all[gs, :].astype(BF16), NT_DIMS) * e_in[:, gs] + dsk_ref[:, gs] * xs_all[:, gs]
            s_new = _dot((x_all[:, gs] * dec[:, gs]).astype(BF16), bg, TN_DIMS)
            for pr in range(4):
                pair = slice((g * 4 + pr) * 128, (g * 4 + pr + 1) * 128)
                x_pair = x_scr[:, pair]
                y_pair = y_off[:, pr * 128:(pr + 1) * 128]
                for j in range(2):
                    h = g * 8 + 2 * pr + j
                    sl = slice(h * HEAD_DIM, (h + 1) * HEAD_DIM)
                    r = 2 * pr + j
                    ldec = jnp.exp(jnp.where(causal, ac_all[:, h:h + 1] - act_all[h:h + 1, :], NEG))
                    x_head = jnp.where((lane128 < HEAD_DIM) == (j == 0), x_pair, jnp.zeros_like(x_pair))
                    y_pair = y_pair + _dot((gmat * ldec).astype(BF16), x_head)
                    elast = jnp.exp(ac_all[L - 1:L, h:h + 1])
                    h_scr[sl, :] = elast * hp_all[sl, :] + s_new[r * HEAD_DIM:(r + 1) * HEAD_DIM, :]
                y_scr[:, pair] = y_pair
        y = y_scr[...]
        yp_ref[...] = y
        zv = z_ref[...]
        yg = y * (zv * jax.nn.sigmoid(zv))
        for g in range(2):
            gs = slice(g * 512, (g + 1) * 512)
            grp = yg[:, gs]
            rstd = lax.rsqrt(jnp.mean(grp * grp, axis=1, keepdims=True) + EPS)
            ys_ref[:, gs] = (grp * rstd * nw_ref[:, gs]).astype(BF16)

    rb = lambda b, c: (b * nch + c, 0)
    v1k = pl.BlockSpec((1, SSD_WIDTH), lambda b, c: (0, 0))
    return pl.pallas_call(
        body, name="ssd_fwd", grid=(bl, nch),
        in_specs=[pl.BlockSpec((L, CONV_CH), rb), pl.BlockSpec((L, SSD_WIDTH), rb),
                  pl.BlockSpec((L, 128), rb), pl.BlockSpec((L, 128), rb),
                  pl.BlockSpec((16, L), lambda b, c: (0, b * nch + c)), v1k, v1k],
        out_specs=[pl.BlockSpec((L, SSD_WIDTH), rb), pl.BlockSpec((L, SSD_WIDTH), rb),
                   pl.BlockSpec((None, SSD_WIDTH, SSD_STATE), lambda b, c: (b * nch + c, 0, 0))],
        out_shape=[jax.ShapeDtypeStruct((n, SSD_WIDTH), BF16), jax.ShapeDtypeStruct((n, SSD_WIDTH), F32),
                   jax.ShapeDtypeStruct((bl * nch, SSD_WIDTH, SSD_STATE), F32)],
        scratch_shapes=[pltpu.VMEM((SSD_WIDTH, SSD_STATE), F32), pltpu.VMEM((L, SSD_WIDTH), F32),
                        pltpu.VMEM((L, SSD_WIDTH), BF16)],
        compiler_params=_cparams(("parallel", "arbitrary")),
    )(xc, proj_a, dt, acum, acum_t, dskip_e, norm_w)


def _ssd_bwd(dys, xc, proj_a, ypre, hprev, dt, gate_d, acum, acum_t, alog128, dskip_e, norm_w, bl, t):
    n = bl * t
    nch = t // CHUNK
    L = CHUNK

    def body(dys_ref, xc_ref, z_ref, yp_ref, hp_ref, dt_ref, gd_ref, ac_ref, act_ref, al_ref, dsk_ref, nw_ref,
             dxc_ref, dz_ref, ddt_ref, dnw_ref, dsk16_ref, da16_ref, db16_ref,
             dh_scr, dy_scr, x_scr, dx_scr, red_scr):
        first = (pl.program_id(0) == 0) & (pl.program_id(1) == 0)

        @pl.when(first)
        def _():
            dnw_ref[...] = jnp.zeros_like(dnw_ref)
            dsk16_ref[...] = jnp.zeros_like(dsk16_ref)
            da16_ref[...] = jnp.zeros_like(da16_ref)
            db16_ref[...] = jnp.zeros_like(db16_ref)

        @pl.when(pl.program_id(1) == 0)
        def _():
            dh_scr[...] = jnp.zeros_like(dh_scr)

        y = yp_ref[...]
        zv = z_ref[...]
        sz = jax.nn.sigmoid(zv)
        gate = zv * sz
        yg = y * gate
        dout = dys_ref[...]
        nw = nw_ref[...]
        for g in range(2):
            gs = slice(g * 512, (g + 1) * 512)
            grp = yg[:, gs]
            rstd = lax.rsqrt(jnp.mean(grp * grp, axis=1, keepdims=True) + EPS)
            ghat = grp * rstd
            dnw_ref[:, gs] += jnp.sum(dout[:, gs] * ghat, axis=0, keepdims=True)
            gw = dout[:, gs] * nw[:, gs]
            dyg = rstd * (gw - ghat * jnp.mean(gw * ghat, axis=1, keepdims=True))
            dy_scr[:, gs] = dyg * gate[:, gs]
            dz_ref[:, gs] = (dyg * y[:, gs] * (sz[:, gs] * (1.0 + zv[:, gs] * (1.0 - sz[:, gs])))).astype(BF16)

        row = lax.broadcasted_iota(jnp.int32, (L, L), 0)
        col = lax.broadcasted_iota(jnp.int32, (L, L), 1)
        causal = row >= col
        lane128 = lax.broadcasted_iota(jnp.int32, (1, L), 1)
        rows128 = lax.broadcasted_iota(jnp.int32, (L, 1), 0)
        last_row = rows128 == (L - 1)
        expander = _head_expander()
        ac_all = ac_ref[...]
        act_all = act_ref[...]
        dt_all = dt_ref[...]
        dt_e = _spread(dt_all, expander)
        ac_e = _spread(ac_all, expander)
        e_in = jnp.exp(ac_e)
        dec = jnp.exp(ac_e[L - 1:L, :] - ac_e)
        xs_all = xc_ref[:, 0:SSD_WIDTH]
        x_all = xs_all * dt_e
        x_scr[...] = x_all.astype(BF16)
        dy_all = dy_scr[...]
        hp_all = hp_ref[...]
        ds_all = dh_scr[...]
        dsk_cols = jnp.sum(dy_all * xs_all, axis=0, keepdims=True)
        dac = jnp.zeros((L, L), F32)
        dac_row = jnp.zeros((L, L), F32)
        ddec_cols = []
        for g in range(2):
            gs = slice(g * 512, (g + 1) * 512)
            bsl = slice(SSD_WIDTH + g * 128, SSD_WIDTH + (g + 1) * 128)
            csl = slice(SSD_WIDTH + 256 + g * 128, SSD_WIDTH + 256 + (g + 1) * 128)
            bg = xc_ref[:, bsl].astype(BF16)
            cg = xc_ref[:, csl].astype(BF16)
            gmat = _dot(cg, bg, NT_DIMS)
            hpb = hp_all[gs, :].astype(BF16)
            dsb = ds_all[gs, :].astype(BF16)
            ch = _dot(cg, hpb, NT_DIMS)
            dye = dy_all[:, gs] * e_in[:, gs]
            dyeb = dye.astype(BF16)
            dc_acc = _dot(dyeb, hpb)
            dhp = _dot(dyeb, cg, TN_DIMS)
            dxd = _dot(bg, dsb, NT_DIMS)
            db_acc = _dot((x_all[:, gs] * dec[:, gs]).astype(BF16), dsb)
            ddec = dxd * x_all[:, gs] * dec[:, gs]
            ddec_cols.append(jnp.sum(ddec, axis=0, keepdims=True))
            dx_inter = dxd * dec[:, gs]
            red_scr[:, gs] = dye * ch - ddec
            dg_sum = jnp.zeros((L, L), F32)
            for pr in range(4):
                pair = slice((g * 4 + pr) * 128, (g * 4 + pr + 1) * 128)
                x_pair = x_scr[:, pair]
                dy_pair = dy_scr[:, pair].astype(BF16)
                dx_pair = dx_inter[:, pr * 128:(pr + 1) * 128]
                for j in range(2):
                    h = g * 8 + 2 * pr + j
                    r = 2 * pr + j
                    sl = slice(h * HEAD_DIM, (h + 1) * HEAD_DIM)
                    onehot_w = lane128 == h
                    ldec = jnp.exp(jnp.where(causal, ac_all[:, h:h + 1] - act_all[h:h + 1, :], NEG))
                    mf = gmat * ldec
                    dyb = jnp.where((lane128 < HEAD_DIM) == (j == 0), dy_pair, jnp.zeros_like(dy_pair))
                    dm = _dot(dyb, x_pair, NT_DIMS)
                    dx_pair = dx_pair + _dot(mf.astype(BF16), dyb, TN_DIMS)
                    dg_sum = dg_sum + dm * ldec
                    wmat = dm * mf
                    elast = jnp.exp(ac_all[L - 1:L, h:h + 1])
                    hp_h = hp_all[sl, :]
                    ds_h = ds_all[sl, :]
                    extra = elast * jnp.sum(jnp.sum(hp_h * ds_h, axis=1, keepdims=True), axis=0, keepdims=True)
                    dac = dac + jnp.where(onehot_w,
                                          jnp.sum(wmat, axis=1, keepdims=True) + jnp.where(last_row, extra, 0.0), 0.0)
                    dac_row = dac_row + jnp.where(rows128 == h, -jnp.sum(wmat, axis=0, keepdims=True), 0.0)
                    dh_scr[sl, :] = elast * ds_h + dhp[r * HEAD_DIM:(r + 1) * HEAD_DIM, :]
                dx_scr[:, pair] = dx_pair
            dgb = dg_sum.astype(BF16)
            dxc_ref[:, csl] = dc_acc + _dot(dgb, bg)
            dxc_ref[:, bsl] = db_acc + _dot(dgb, cg, TN_DIMS)
        dx_all = dx_scr[...]
        dxc_ref[:, 0:SSD_WIDTH] = dx_all * dt_e + dsk_ref[...] * dy_all
        red = red_scr[...]
        dac_slab = _head_sums(red, expander)
        ddec_tot = _head_sums(jnp.broadcast_to(jnp.concatenate(ddec_cols, axis=1), (8, SSD_WIDTH)), expander)
        ddt_x = _head_sums(dx_all * xs_all, expander)
        dsk16_ref[...] += _head_sums(jnp.broadcast_to(dsk_cols, (8, SSD_WIDTH)), expander)[0:1, 0:16]
        dac = dac + dac_slab + jnp.transpose(dac_row) + jnp.where(last_row, ddec_tot[0:1, :], 0.0)
        triu = (row <= col).astype(F32)
        da = jnp.dot(triu, dac, precision=HIGHEST, preferred_element_type=F32)
        a_row = -jnp.exp(al_ref[...])
        ddt = jnp.where(lane128 < 16, (ddt_x + da * a_row) * gd_ref[...], 0.0)
        ddt_ref[...] = ddt
        da16_ref[...] += (jnp.sum(da * dt_all, axis=0, keepdims=True) * a_row)[:, 0:16]
        db16_ref[...] += jnp.sum(ddt, axis=0, keepdims=True)[:, 0:16]

    rb = lambda b, c: (b * nch + nch - 1 - c, 0)
    v1k = pl.BlockSpec((1, SSD_WIDTH), lambda b, c: (0, 0))
    v16 = pl.BlockSpec((1, 16), lambda b, c: (0, 0))
    v128 = pl.BlockSpec((1, 128), lambda b, c: (0, 0))
    wide = pl.BlockSpec((L, SSD_WIDTH), rb)
    s128 = pl.BlockSpec((L, 128), rb)
    return pl.pallas_call(
        body, name="ssd_bwd", grid=(bl, nch),
        in_specs=[wide, pl.BlockSpec((L, CONV_CH), rb), wide, wide,
                  pl.BlockSpec((None, SSD_WIDTH, SSD_STATE), lambda b, c: (b * nch + nch - 1 - c, 0, 0)),
                  s128, s128, s128, pl.BlockSpec((16, L), lambda b, c: (0, b * nch + nch - 1 - c)), v128, v1k, v1k],
        out_specs=[pl.BlockSpec((L, CONV_CH), rb), wide, s128, v1k, v16, v16, v16],
        out_shape=[jax.ShapeDtypeStruct((n, CONV_CH), F32), jax.ShapeDtypeStruct((n, PA_WIDTH), BF16),
                   jax.ShapeDtypeStruct((n, 128), F32), jax.ShapeDtypeStruct((1, SSD_WIDTH), F32),
                   jax.ShapeDtypeStruct((1, 16), F32), jax.ShapeDtypeStruct((1, 16), F32),
                   jax.ShapeDtypeStruct((1, 16), F32)],
        scratch_shapes=[pltpu.VMEM((SSD_WIDTH, SSD_STATE), F32), pltpu.VMEM((L, SSD_WIDTH), F32),
                        pltpu.VMEM((L, SSD_WIDTH), BF16), pltpu.VMEM((L, SSD_WIDTH), F32),
                        pltpu.VMEM((L, SSD_WIDTH), F32)],
        compiler_params=_cparams(("arbitrary", "arbitrary")),
    )(dys, xc, proj_a, ypre, hprev, dt, gate_d, acum, acum_t, alog128, dskip_e, norm_w)


def _attn_fwd(qkv, negc, bl, t):
    n = bl * t
    tb_ = ATT_BLOCK
    nb = t // tb_
    scale2 = LOG2E / math.sqrt(HEAD_DIM)

    def body(q_ref, k_ref, v_ref, c_ref, o_ref, lse_ref, v0_scr, v1_scr):
        row = lax.broadcasted_iota(jnp.int32, (tb_, tb_), 0)
        col = lax.broadcasted_iota(jnp.int32, (tb_, tb_), 1)
        causal = row >= col
        lane = lax.broadcasted_iota(jnp.int32, (1, 128), 1)
        v_pair = v_ref[...].astype(F32)
        v_scrs = (v0_scr, v1_scr)
        for j in range(2):
            v_head = v_pair if j == 0 else pltpu.roll(v_pair, HEAD_DIM, 1)
            v_scrs[j][...] = jnp.where(lane < HEAD_DIM, v_head, jnp.where(lane == HEAD_DIM, 1.0, 0.0)).astype(BF16)
        for qi in range(nb):
            r0, lk = qi * tb_, (qi + 1) * tb_
            for j in range(2):
                sl = slice(j * HEAD_DIM, (j + 1) * HEAD_DIM)
                s = _dot(q_ref[r0:lk, sl], k_ref[0:lk, sl], NT_DIMS) * scale2 + c_ref[j:j + 1, 0:lk] * LOG2E
                tail = jnp.where(causal, s[:, r0:lk], NEG)
                s = tail if qi == 0 else jnp.concatenate([s[:, 0:r0], tail], axis=1)
                m = jnp.max(s, axis=1, keepdims=True)
                p = jnp.exp2(s - m)
                acc = _dot(p.astype(BF16), v_scrs[j][0:lk, :])
                l = acc[:, HEAD_DIM:HEAD_DIM + 1]
                o_ref[r0:lk, sl] = (acc[:, 0:HEAD_DIM] / l).astype(BF16)
                lse_ref[r0:lk, sl] = jnp.broadcast_to(m + jnp.log(l) * LOG2E, (tb_, HEAD_DIM))

    blk = lambda off: pl.BlockSpec((t, 128), lambda b, hp: (b, off + hp))
    return pl.pallas_call(
        body, name="fox_attn_fwd", grid=(bl, 8),
        in_specs=[blk(0), blk(8), blk(16), pl.BlockSpec((None, None, 8, t), lambda b, hp: (b, hp, 0, 0))],
        out_specs=[blk(0), blk(0)],
        out_shape=[jax.ShapeDtypeStruct((n, ATT_WIDTH), BF16), jax.ShapeDtypeStruct((n, ATT_WIDTH), F32)],
        scratch_shapes=[pltpu.VMEM((t, 128), BF16), pltpu.VMEM((t, 128), BF16)],
        compiler_params=_cparams(("parallel", "parallel")),
    )(qkv, qkv, qkv, negc)


def _attn_bwd(qkv, do, o, lse, negc, after, bl, t):
    n = bl * t
    tb_ = ATT_BLOCK
    nb = t // tb_
    scale = 1.0 / math.sqrt(HEAD_DIM)
    scale2 = LOG2E * scale

    def body(q_ref, k_ref, v_ref, do_ref, o_ref, lse_ref, c_ref, after_ref, dq_ref, dk_ref, dv_ref, dc_ref,
             dq0_scr, delta_scr, dq1_scr, qt0_scr, qt1_scr, dot_scr, dkt0_scr, dkt1_scr, dvt_scr):
        row = lax.broadcasted_iota(jnp.int32, (tb_, tb_), 0)
        col = lax.broadcasted_iota(jnp.int32, (tb_, tb_), 1)
        causal = row >= col
        lane = lax.broadcasted_iota(jnp.int32, (1, 128), 1)
        dq_scrs = (dq0_scr, dq1_scr)
        qt_scrs = (qt0_scr, qt1_scr)
        dkt_scrs = (dkt0_scr, dkt1_scr)
        dq0_scr[...] = jnp.zeros_like(dq0_scr)
        dq1_scr[...] = jnp.zeros_like(dq1_scr)
        dc_ref[...] = jnp.zeros_like(dc_ref)
        q_t = jnp.transpose(q_ref[...].astype(F32))
        ones_row = jnp.where(lax.broadcasted_iota(jnp.int32, (8, t), 0) == 0, 1.0, 0.0)
        for j in range(2):
            qt_scrs[j][...] = jnp.concatenate(
                [q_t[j * HEAD_DIM:(j + 1) * HEAD_DIM, :], ones_row, jnp.zeros((HEAD_DIM - 8, t), F32)],
                axis=0).astype(BF16)
        dot_scr[...] = jnp.transpose(do_ref[...].astype(F32)).astype(BF16)
        prod = do_ref[...].astype(F32) * o_ref[...].astype(F32)
        for j in range(2):
            sl = slice(j * HEAD_DIM, (j + 1) * HEAD_DIM)
            delta_scr[:, sl] = jnp.broadcast_to(jnp.sum(prod[:, sl], axis=1, keepdims=True), (t, HEAD_DIM))
        for kj in range(nb):
            r0, r1 = kj * tb_, (kj + 1) * tb_
            k_pair = k_ref[r0:r1, :].astype(F32)
            for j in range(2):
                sl = slice(j * HEAD_DIM, (j + 1) * HEAD_DIM)
                one = slice(j * HEAD_DIM, j * HEAD_DIM + 1)
                kb = k_ref[r0:r1, sl]
                k_head = k_pair if j == 0 else pltpu.roll(k_pair, HEAD_DIM, 1)
                k_ones = jnp.where(lane < HEAD_DIM, k_head, jnp.where(lane == HEAD_DIM, 1.0, 0.0)).astype(BF16)
                qs = q_ref[r0:t, sl]
                dos = do_ref[r0:t, sl]
                s = _dot(qs, kb, NT_DIMS) * scale2 + c_ref[j:j + 1, r0:r1] * LOG2E
                head = jnp.where(causal, s[0:tb_, :], NEG)
                s = head if kj == nb - 1 else jnp.concatenate([head, s[tb_:, :]], axis=0)
                p = jnp.exp2(s - lse_ref[r0:t, one])
                dp = _dot(dos, v_ref[r0:r1, sl], NT_DIMS)
                ds = p * (dp - delta_scr[r0:t, one])
                dsb = ds.astype(BF16)
                dvt_scr[sl, r0:r1] = _dot(dot_scr[sl, r0:t], p.astype(BF16))
                dkt_scrs[j][:, r0:r1] = _dot(qt_scrs[j][:, r0:t], dsb)
                dq_scrs[j][r0:t, :] += _dot(dsb, k_ones)
        dv_ref[...] = jnp.transpose(dvt_scr[...]).astype(BF16)
        for j in range(2):
            sl = slice(j * HEAD_DIM, (j + 1) * HEAD_DIM)
            acc = dq_scrs[j][...]
            dkt = dkt_scrs[j][...]
            dq_ref[:, sl] = (acc[:, 0:HEAD_DIM] * scale).astype(BF16)
            dk_ref[:, sl] = (jnp.transpose(dkt)[:, 0:HEAD_DIM] * scale).astype(BF16)
            dc_ref[j:j + 1, :] = jnp.transpose(acc)[HEAD_DIM:HEAD_DIM + 1, :] - dkt[HEAD_DIM:HEAD_DIM + 1, :]

    blk = lambda off: pl.BlockSpec((t, 128), lambda b, hp: (b, off + hp))
    cblk = pl.BlockSpec((None, None, 8, t), lambda b, hp: (b, hp, 0, 0))
    return pl.pallas_call(
        body, name="fox_attn_bwd", grid=(bl, 8),
        in_specs=[blk(0), blk(8), blk(16), blk(0), blk(0), blk(0), cblk, ANY],
        out_specs=[blk(0), blk(0), blk(0), cblk],
        out_shape=[jax.ShapeDtypeStruct((n, ATT_WIDTH), BF16)] * 3 + [jax.ShapeDtypeStruct((bl, 8, 8, t), F32)],
        scratch_shapes=[pltpu.VMEM((t, 128), F32), pltpu.VMEM((t, 128), F32), pltpu.VMEM((t, 128), F32),
                        pltpu.VMEM((128, t), BF16), pltpu.VMEM((128, t), BF16), pltpu.VMEM((128, t), BF16),
                        pltpu.VMEM((128, t), F32), pltpu.VMEM((128, t), F32), pltpu.VMEM((128, t), F32)],
        compiler_params=_cparams(("parallel", "parallel")),
    )(qkv, qkv, qkv, do, o, lse, negc, after)


def _adamw(w, g, m, v, *, name):
    lead = w.ndim == 3
    r, c = w.shape[-2:]
    tr = _pick(r, (256, IN_SHARD // 3, 128, 64, 32, 16, 8))
    bc1 = 1.0 - ADAM_B1 ** ADAM_STEP
    bc2 = 1.0 - ADAM_B2 ** ADAM_STEP

    def body(w_ref, g_ref, m_ref, v_ref, d_ref, nm_ref, nv_ref):
        gv = g_ref[...]
        mn = ADAM_B1 * m_ref[...] + (1.0 - ADAM_B1) * gv
        vn = ADAM_B2 * v_ref[...] + (1.0 - ADAM_B2) * (gv * gv)
        m_hat = mn / bc1
        v_hat = vn / bc2
        d_ref[...] = -ADAM_LR * (m_hat / (jnp.sqrt(v_hat) + ADAM_EPS) + ADAM_WD * w_ref[...])
        nm_ref[...] = mn
        nv_ref[...] = vn

    flat = pl.BlockSpec((tr, c), lambda i: (i, 0))
    blk = pl.BlockSpec((None, tr, c), lambda i: (0, i, 0)) if lead else flat
    return pl.pallas_call(
        body, name=name, grid=(r // tr,), in_specs=[blk, flat, blk, blk], out_specs=[blk] * 3,
        out_shape=[jax.ShapeDtypeStruct(w.shape, F32)] * 3,
        compiler_params=_cparams(("parallel",)),
    )(w, g, m, v)


def _sum_leading(parts, *, name, out_dtype=F32):
    k, r, c = parts.shape
    tr = _pick(r, (512, 256, 128, 96, 64, 32, 16, 8))

    def body(p_ref, o_ref):
        acc = p_ref[0].astype(F32)
        for i in range(1, k):
            acc = acc + p_ref[i].astype(F32)
        o_ref[...] = acc.astype(out_dtype)

    return pl.pallas_call(
        body, name=name, grid=(r // tr,),
        in_specs=[pl.BlockSpec((k, tr, c), lambda i: (0, i, 0))],
        out_specs=pl.BlockSpec((tr, c), lambda i: (i, 0)),
        out_shape=jax.ShapeDtypeStruct((r, c), out_dtype),
        compiler_params=_cparams(("parallel",)),
    )(parts)


def _add_pair(a, b, *, name):
    k, r, c = a.shape
    tr = _pick(r, (512, 256, 128))

    def body(a_ref, b_ref, o_ref):
        o_ref[...] = (a_ref[...].astype(F32) + b_ref[...].astype(F32)).astype(BF16)

    blk = pl.BlockSpec((None, tr, c), lambda j, i: (j, i, 0))
    return pl.pallas_call(
        body, name=name, grid=(k, r // tr), in_specs=[blk, blk], out_specs=blk,
        out_shape=jax.ShapeDtypeStruct((k, r, c), BF16),
        compiler_params=_cparams(("parallel", "parallel")),
    )(a, b)


ANY = pl.BlockSpec(memory_space=pl.ANY)


def _chip_peers(x, y):
    return [(1 - x, y, 2 * (1 - x) + y), (x, 1 - y, 2 * x + 1 - y), (1 - x, 1 - y, 2 * (1 - x) + 1 - y)]


def _gather_weights(blob, *, name):
    rows, cols = blob.shape
    half_rows = rows // 2

    def body(b_ref, o_ref, send_sems, recv_sems):
        x, y, c = lax.axis_index("x"), lax.axis_index("y"), lax.axis_index("c")
        me = 2 * x + y
        sibling = (x, y, 1 - c)
        peers = _chip_peers(x, y)

        def half(chip, hc):
            return o_ref.at[chip, pl.ds(hc * half_rows, half_rows), :]

        def copy(k, src, chip, hc, to):
            return pltpu.make_async_remote_copy(src_ref=src, dst_ref=half(chip, hc), send_sem=send_sems.at[k],
                                                recv_sem=recv_sems.at[k], device_id=to, device_id_type=MESH)

        my_half = b_ref.at[pl.ds(c * half_rows, half_rows), :]
        first = [copy(k, my_half, me, c, (px, py, c)) for k, (px, py, _) in enumerate(peers)]
        for cp in first:
            cp.start()
        passed = [copy(3 + k, half(pc, c), pc, c, sibling) for k, (_, _, pc) in enumerate(peers)]
        for k, (px, py, pc) in enumerate(peers):
            copy(k, my_half, pc, c, (px, py, c)).wait_recv()
            passed[k].start()
        for k, (_, _, pc) in enumerate(peers):
            copy(3 + k, half(pc, 1 - c), pc, 1 - c, sibling).wait_recv()
        for cp in first + passed:
            cp.wait_send()

    return pl.pallas_call(
        body, name=name, in_specs=[ANY], out_specs=ANY,
        out_shape=jax.ShapeDtypeStruct((N_CHIPS, rows, cols), BF16),
        scratch_shapes=[pltpu.SemaphoreType.DMA((6,)), pltpu.SemaphoreType.DMA((6,))],
    )(blob)


def _swap_halves(g, *, name):
    _, rows, cols = g.shape
    half_rows = rows // 2

    def body(g_ref, o_ref, send_sem, recv_sem):
        x, y, c = lax.axis_index("x"), lax.axis_index("y"), lax.axis_index("c")
        cp = pltpu.make_async_remote_copy(
            src_ref=g_ref.at[:, pl.ds((1 - c) * half_rows, half_rows), :], dst_ref=o_ref,
            send_sem=send_sem, recv_sem=recv_sem, device_id=(x, y, 1 - c), device_id_type=MESH)
        cp.start()
        cp.wait()

    return pl.pallas_call(
        body, name=name, in_specs=[ANY], out_specs=ANY,
        out_shape=jax.ShapeDtypeStruct((N_CHIPS, half_rows, cols), BF16),
        scratch_shapes=[pltpu.SemaphoreType.DMA, pltpu.SemaphoreType.DMA],
    )(g)


HBM_SPEC = pl.BlockSpec(memory_space=pltpu.HBM)
SEM_SPEC = pl.BlockSpec(memory_space=pltpu.SEMAPHORE)
SPLIT_EFFECT = pltpu.SideEffectType.DATAFLOW_SIDE_EFFECTING


def _gather_peers_copies(b_ref, land_ref, send_sems, recv_sems, sending):
    x, y, c = lax.axis_index("x"), lax.axis_index("y"), lax.axis_index("c")
    me = 2 * x + y
    half_rows = b_ref.shape[0] // 2
    src = b_ref.at[pl.ds(c * half_rows, half_rows), :]
    return [pltpu.make_async_remote_copy(
        src_ref=src, dst_ref=land_ref.at[me if sending else pc, pl.ds(c * half_rows, half_rows), :],
        send_sem=send_sems.at[k], recv_sem=recv_sems.at[k], device_id=(px, py, c), device_id_type=MESH)
        for k, (px, py, pc) in enumerate(_chip_peers(x, y))]


def _gather_start(blob, after, *, name):
    shape = (N_CHIPS,) + blob.shape

    def body(b_ref, land_ref, after_ref, send_sems, recv_sems, b_thru, land_thru, token):
        for cp in _gather_peers_copies(b_ref, land_ref, send_sems, recv_sems, True):
            cp.start()
        token[...] = jnp.zeros_like(token)

    return pl.pallas_call(
        body, name=name,
        out_shape=(pltpu.SemaphoreType.DMA((3,)), pltpu.SemaphoreType.DMA((3,)), pltpu.HBM(blob.shape, blob.dtype),
                   pltpu.HBM(shape, blob.dtype), jax.ShapeDtypeStruct((8, 128), F32)),
        in_specs=(HBM_SPEC, HBM_SPEC, ANY),
        out_specs=(SEM_SPEC, SEM_SPEC, HBM_SPEC, HBM_SPEC, pl.BlockSpec(memory_space=pltpu.VMEM)),
        input_output_aliases={0: 2, 1: 3},
        compiler_params=pltpu.CompilerParams(has_side_effects=SPLIT_EFFECT),
    )(pltpu.with_memory_space_constraint(blob, pltpu.HBM),
      pltpu.with_memory_space_constraint(lax.empty(shape, blob.dtype), pltpu.HBM), after)


def _gather_wait(send_sems, recv_sems, b_thru, land_thru, after, *, name):
    def body(b_ref, land_ref, send_sems, recv_sems, after_ref, b_dead, got_ref):
        for cp in _gather_peers_copies(b_ref, land_ref, send_sems, recv_sems, False):
            cp.wait_send()
            cp.wait_recv()

    return pl.pallas_call(
        body, name=name,
        out_shape=(pltpu.HBM(b_thru.shape, b_thru.dtype), pltpu.HBM(land_thru.shape, land_thru.dtype)),
        in_specs=(HBM_SPEC, HBM_SPEC, SEM_SPEC, SEM_SPEC, ANY), out_specs=(HBM_SPEC, HBM_SPEC),
        input_output_aliases={0: 0, 1: 1},
        compiler_params=pltpu.CompilerParams(has_side_effects=SPLIT_EFFECT),
    )(b_thru, land_thru, send_sems, recv_sems, after)


def _gather_forward(land, *, name):
    half_rows = land.shape[1] // 2

    def body(l_ref, o_ref, send_sems, recv_sems):
        x, y, c = lax.axis_index("x"), lax.axis_index("y"), lax.axis_index("c")
        cps = []
        for k, (_, _, pc) in enumerate(_chip_peers(x, y)):
            mine = pl.ds(c * half_rows, half_rows)
            cps.append(pltpu.make_async_remote_copy(
                src_ref=l_ref.at[pc, mine, :], dst_ref=o_ref.at[pc, mine, :], send_sem=send_sems.at[k],
                recv_sem=recv_sems.at[k], device_id=(x, y, 1 - c), device_id_type=MESH))
        for cp in cps:
            cp.start()
        for k, (_, _, pc) in enumerate(_chip_peers(x, y)):
            theirs = pl.ds((1 - c) * half_rows, half_rows)
            pltpu.make_async_remote_copy(
                src_ref=l_ref.at[pc, theirs, :], dst_ref=o_ref.at[pc, theirs, :], send_sem=send_sems.at[k],
                recv_sem=recv_sems.at[k], device_id=(x, y, 1 - c), device_id_type=MESH).wait_recv()
        for cp in cps:
            cp.wait_send()

    return pl.pallas_call(
        body, name=name, in_specs=[ANY], out_specs=ANY, input_output_aliases={0: 0},
        out_shape=jax.ShapeDtypeStruct(land.shape, land.dtype),
        scratch_shapes=[pltpu.SemaphoreType.DMA((3,)), pltpu.SemaphoreType.DMA((3,))],
    )(land)


def _exchange_peers_copies(p_ref, land_ref, send_sems, recv_sems, sending):
    x, y, c = lax.axis_index("x"), lax.axis_index("y"), lax.axis_index("c")
    me = 2 * x + y
    return [pltpu.make_async_remote_copy(src_ref=p_ref.at[pc], dst_ref=land_ref.at[me if sending else pc],
                                         send_sem=send_sems.at[k], recv_sem=recv_sems.at[k],
                                         device_id=(px, py, c), device_id_type=MESH)
            for k, (px, py, pc) in enumerate(_chip_peers(x, y))]


def _exchange_start(p, *, name):
    def body(p_ref, land_ref, send_sems, recv_sems, p_thru, land_thru, token):
        for cp in _exchange_peers_copies(p_ref, land_ref, send_sems, recv_sems, True):
            cp.start()
        token[...] = jnp.zeros_like(token)

    return pl.pallas_call(
        body, name=name,
        out_shape=(pltpu.SemaphoreType.DMA((3,)), pltpu.SemaphoreType.DMA((3,)), pltpu.HBM(p.shape, p.dtype),
                   pltpu.HBM(p.shape, p.dtype), jax.ShapeDtypeStruct((8, 128), F32)),
        in_specs=(HBM_SPEC, HBM_SPEC),
        out_specs=(SEM_SPEC, SEM_SPEC, HBM_SPEC, HBM_SPEC, pl.BlockSpec(memory_space=pltpu.VMEM)),
        input_output_aliases={0: 2, 1: 3},
        compiler_params=pltpu.CompilerParams(has_side_effects=SPLIT_EFFECT),
    )(pltpu.with_memory_space_constraint(p, pltpu.HBM),
      pltpu.with_memory_space_constraint(lax.empty(p.shape, p.dtype), pltpu.HBM))


def _exchange_wait(send_sems, recv_sems, p_thru, land_thru, after, *, name):
    def body(p_ref, land_ref, send_sems, recv_sems, after_ref, p_dead, got_ref):
        for cp in _exchange_peers_copies(p_ref, land_ref, send_sems, recv_sems, False):
            cp.wait_send()
            cp.wait_recv()

    return pl.pallas_call(
        body, name=name,
        out_shape=(pltpu.HBM(p_thru.shape, p_thru.dtype), pltpu.HBM(p_thru.shape, p_thru.dtype)),
        in_specs=(HBM_SPEC, HBM_SPEC, SEM_SPEC, SEM_SPEC, ANY), out_specs=(HBM_SPEC, HBM_SPEC),
        input_output_aliases={0: 0, 1: 1},
        compiler_params=pltpu.CompilerParams(has_side_effects=SPLIT_EFFECT),
    )(p_thru, land_thru, send_sems, recv_sems, after)


def _sum_parts(parts, own, *, name):
    k, r, c = parts.shape
    tr = _pick(r, (512, 256, 128))

    def body(p_ref, own_ref, o_ref):
        me = 2 * lax.axis_index("x") + lax.axis_index("y")
        acc = jnp.zeros((tr, c), F32)
        for i in range(k):
            acc = acc + jnp.where(me == i, own_ref[i], p_ref[i]).astype(F32)
        o_ref[...] = acc

    blk = pl.BlockSpec((k, tr, c), lambda i: (0, i, 0))
    return pl.pallas_call(
        body, name=name, grid=(r // tr,), in_specs=[blk, blk],
        out_specs=pl.BlockSpec((tr, c), lambda i: (i, 0)),
        out_shape=jax.ShapeDtypeStruct((r, c), F32),
        compiler_params=_cparams(("parallel",)),
    )(parts, own)


def _join_halves(gh, *, name):
    def body(g_ref, o_ref, send_sem, recv_sem):
        x, y, c = lax.axis_index("x"), lax.axis_index("y"), lax.axis_index("c")
        cp = pltpu.make_async_remote_copy(src_ref=g_ref, dst_ref=o_ref, send_sem=send_sem, recv_sem=recv_sem,
                                          device_id=(x, y, 1 - c), device_id_type=MESH)
        cp.start()
        cp.wait()

    other = pl.pallas_call(
        body, name=name, in_specs=[ANY], out_specs=ANY,
        out_shape=jax.ShapeDtypeStruct(gh.shape, F32),
        scratch_shapes=[pltpu.SemaphoreType.DMA, pltpu.SemaphoreType.DMA],
    )(gh)
    south = lax.axis_index("c") == 0
    return jnp.concatenate([jnp.where(south, gh, other), jnp.where(south, other, gh)], axis=0)


def _gather_small(s, *, name):
    rows = s.shape[0]

    def body(s_ref, o_ref, send_sems, recv_sems, local_sem):
        x, y, c = lax.axis_index("x"), lax.axis_index("y"), lax.axis_index("c")
        me = 4 * x + 2 * y + c
        mine = pltpu.make_async_copy(s_ref, o_ref.at[me], local_sem)
        mine.start()
        peers = []
        for k in range(1, 8):
            peers.append((1 - x if k & 4 else x, 1 - y if k & 2 else y, 1 - c if k & 1 else c))
        cps = [pltpu.make_async_remote_copy(src_ref=s_ref, dst_ref=o_ref.at[me], send_sem=send_sems.at[k],
                                            recv_sem=recv_sems.at[k], device_id=p, device_id_type=MESH)
               for k, p in enumerate(peers)]
        for cp in cps:
            cp.start()
        for k, (px, py, pc) in enumerate(peers):
            pltpu.make_async_remote_copy(src_ref=s_ref, dst_ref=o_ref.at[4 * px + 2 * py + pc],
                                         send_sem=send_sems.at[k], recv_sem=recv_sems.at[k],
                                         device_id=(px, py, pc), device_id_type=MESH).wait_recv()
        for cp in cps:
            cp.wait_send()
        mine.wait()

    return pl.pallas_call(
        body, name=name, in_specs=[ANY], out_specs=ANY,
        out_shape=jax.ShapeDtypeStruct((8, rows, 128), F32),
        scratch_shapes=[pltpu.SemaphoreType.DMA((7,)), pltpu.SemaphoreType.DMA((7,)), pltpu.SemaphoreType.DMA],
    )(s)


IN_SHARD = IN_WIDTH // N_CHIPS
IN_SHARD_PAD = 1536
UP_ROWS, DOWN_ROWS, OUT_ROWS = 1024, 1024, 512
REST_ROWS = UP_ROWS + DOWN_ROWS + OUT_ROWS


def _pack_in(w_in_s):
    return jnp.pad(w_in_s, ((0, 0), (0, IN_SHARD_PAD - IN_SHARD))).astype(BF16)


def _pack_rest(w_out_s, w_up_s, w_down_s):
    return jnp.concatenate([w_up_s, w_down_s, w_out_s], axis=0).astype(BF16)


def _unpack_rest(blob):
    return (blob[UP_ROWS + DOWN_ROWS:], blob[0:UP_ROWS], blob[UP_ROWS:UP_ROWS + DOWN_ROWS])


def _with_own(gathered, own):
    me = 2 * lax.axis_index("x") + lax.axis_index("y")
    return [jnp.where(me == j, own, gathered[j]) for j in range(N_CHIPS)]


def _full_w_in(g_in, own):
    return jnp.concatenate([s[:, :IN_SHARD] for s in _with_own(g_in, own)], axis=1)


def _full_rest(g_rest, own):
    parts = [_unpack_rest(s) for s in _with_own(g_rest, own)]
    w_out = jnp.concatenate([p[0] for p in parts], axis=0)
    w_up = jnp.concatenate([p[1] for p in parts], axis=1)
    w_down = jnp.concatenate([p[2] for p in parts], axis=0)
    return w_out, w_up, w_down


def _split_w_in(w_in):
    z_xbc = w_in[:, 0:2560]
    dt = w_in[:, 2560:2576]
    qkv = w_in[:, 2576:5648]
    f = w_in[:, 5648:5664]
    pad = jnp.zeros((w_in.shape[0], PA_WIDTH - 2592), w_in.dtype)
    return jnp.concatenate([z_xbc, dt, f, pad], axis=1), qkv


def _merge_w_in(d_a, d_qkv):
    return jnp.concatenate([d_a[:, 0:2560], d_a[:, 2560:2576], d_qkv, d_a[:, 2576:2592]], axis=1)


def _local_step(x3, target3, w_in, rest_weights, norm_mix_w, conv_w, conv_b, dt_bias, a_log, d_skip,
                ssd_norm_w, f_bias, norm_mlp_w, norm_final_w, first_after=None, early_grads=None, late_grads=None):
    bl, t, d = x3.shape
    n = bl * t
    x = x3.reshape(n, d)
    target = target3.reshape(n, d)
    w_a, w_qkv = _split_w_in(w_in)
    nfw = norm_final_w.reshape(1, d)
    dskip_e = jnp.repeat(d_skip, HEAD_DIM, axis=1)
    nb = t // ATT_BLOCK

    r1, r2, kt = min(n, 1024), min(n, 512), min(n, 512)
    if first_after is None:
        first_after = jnp.zeros((8, 128), F32)
    h0, rstd0, proj_a = _norm_mm(x, norm_mix_w, w_a, first_after, name="norm_mix_proj_a", tm=r2)
    qkv = _mm(h0, w_qkv, name="proj_qkv", tiles=(r2, QKV_WIDTH, D_MODEL), out_dtype=BF16)
    bias128 = jnp.concatenate([dt_bias, f_bias, jnp.zeros((1, 96), F32)], axis=1)
    alog128 = jnp.concatenate([a_log, jnp.zeros((1, 112), F32)], axis=1)
    dt, gate_d, acum, ccum = _prep(proj_a, bias128, alog128, bl, t)
    acum_t = acum[:, 0:16].T
    negc = jnp.pad(-ccum.reshape(bl, t, 8, 2).transpose(0, 2, 3, 1), ((0, 0), (0, 0), (0, 6), (0, 0)))
    xc = _conv_fwd(proj_a, conv_w, conv_b, bl, t)
    y_ssd, y_pre, hprev = _ssd_fwd(xc, proj_a, dt, acum, acum_t, dskip_e, ssd_norm_w, bl, t)
    y_att, lse = _attn_fwd(qkv, negc, bl, t)
    w_out, w_up, w_down = rest_weights(y_att)
    wo_s, wo_a = w_out[:SSD_WIDTH], w_out[SSD_WIDTH:]
    h1, h1n, rstd1 = _mm_norm_fwd(y_ssd, wo_s, y_att, wo_a, x, norm_mlp_w, name="out_proj_norm_mlp", tm=r2)
    up = _mm(h1n, w_up, name="mlp_up", tiles=(r2, D_FF, D_MODEL), out_dtype=BF16)
    dh2, dh2b, loss, d_nfw = _mm_final(up, w_down, h1, nfw, target, name="mlp_down_final_norm_loss", tm=r2,
                                       a_act="relu2")

    dup = _mm(dh2b, w_down, name="mlp_down_bwd_act", tiles=(r2, D_FF, D_MODEL), tb=True, epi_up=up, out_dtype=BF16)
    rest_shape = (N_CHIPS, REST_ROWS, D_MODEL)
    gb_rest = _mm(up, dh2b, name="mlp_down_bwd_w", tiles=(DOWN_ROWS, D_MODEL, kt), ta=True, a_act="relu2",
                  out_dtype=BF16, into=(rest_shape, (None, DOWN_ROWS, D_MODEL), lambda i, j, k: (i, 1, 0), None))
    dh1, dh1b, d_nmlp = _mm_norm_bwd([(dup, w_up)], h1, rstd1, norm_mlp_w, dh2, name="mlp_up_bwd_act_norm_mlp",
                                     tm=r2)
    gb_rest = _mm(h1n, dup, name="mlp_up_bwd_w", tiles=(D_MODEL, UP_ROWS, kt), ta=True, out_dtype=BF16,
                  into=(rest_shape, (None, D_MODEL, UP_ROWS), lambda i, j, k: (j, 0, 0), gb_rest))
    dys, do = _mm_two_halves(dh1b, w_out, name="out_proj_bwd_act", tm=r1)
    out_block = (UP_ROWS + DOWN_ROWS) // OUT_ROWS
    for half, (y_half, tag) in enumerate(((y_ssd, "ssd"), (y_att, "att"))):
        gb_rest = _mm(y_half, dh1b, name="out_proj_bwd_w_" + tag, tiles=(2 * OUT_ROWS, D_MODEL, kt), ta=True,
                      out_dtype=BF16, into=(rest_shape, (2, OUT_ROWS, D_MODEL),
                                            functools.partial(lambda i, j, k, h: (h, out_block, 0), h=half),
                                            gb_rest))
    token = jnp.zeros((8, 128), F32) if early_grads is None else early_grads(gb_rest)
    dq, dk, dv, dcb = _attn_bwd(qkv, do, y_att, lse, negc, token, bl, t)
    dc = jnp.pad(dcb[:, :, 0:2, :].transpose(0, 3, 1, 2).reshape(n, 16), ((0, 0), (16, 96)))
    dxc, dpa, ddt_raw, d_snw, d_dsk, d_alog, d_dtb = _ssd_bwd(dys, xc, proj_a, y_pre, hprev, dt, gate_d, acum,
                                                             acum_t, alog128, dskip_e, ssd_norm_w, bl, t)
    dpa, d_conv_w, d_conv_b = _conv_bwd(dxc, proj_a, conv_w, conv_b, dpa, bl, t)
    dproj_a, d_fb = _fpost(dc, gate_d, ddt_raw, dpa, bl, t)
    dqkv = jnp.concatenate([dq, dk, dv], axis=1)
    d_w_a = _mm(h0, dproj_a, name="proj_a_bwd_w", tiles=(1024, 896, kt), ta=True, out_dtype=BF16)
    d_w_qkv = _mm(h0, dqkv, name="proj_qkv_bwd_w", tiles=(1024, 1024, kt), ta=True, out_dtype=BF16)
    d_w_in = _merge_w_in(d_w_a, d_w_qkv)
    late_token = None if late_grads is None else late_grads(d_w_in)
    dx, _, d_nmix = _mm_norm_bwd([(dproj_a, w_a), (dqkv, w_qkv)], x, rstd0, norm_mix_w, dh1,
                                 name="proj_bwd_act_norm_mix", tm=min(n, 256), after=late_token)

    grads = dict(norm_mix_w=d_nmix, w_in=d_w_in, conv_w=d_conv_w, conv_b=d_conv_b,
                 dt_bias=d_dtb, a_log=d_alog, d_skip=d_dsk, ssd_norm_w=d_snw, f_bias=d_fb, rest=gb_rest,
                 norm_mlp_w=d_nmlp, norm_final_w=d_nfw)
    return dx.reshape(bl, t, d), loss, grads


SMALL_ORDER = ("norm_mix_w", "conv_w", "conv_b", "dt_bias", "a_log", "d_skip", "ssd_norm_w", "f_bias",
               "norm_mlp_w", "norm_final_w")
SMALL_SIZES = (1024, 4 * CONV_CH, CONV_CH, 16, 16, 16, 1024, 16, 1024, 1024)


def _pack_small(vals, rows):
    flat = jnp.concatenate([v.reshape(-1).astype(F32) for v in vals])
    return jnp.pad(flat, (0, rows * 128 - flat.shape[0])).reshape(rows, 128)


def _unpack_small(packed, sizes):
    flat = packed.reshape(-1)
    out, o = [], 0
    for s in sizes:
        out.append(flat[o:o + s])
        o += s
    return out


def kernel(x, norm_mix_w, w_in, conv_w, conv_b, dt_bias, a_log, d_skip, ssd_norm_w, f_bias, w_out, norm_mlp_w, w_up, w_down, norm_final_w, loss_target, m_norm_mix_w, m_w_in, m_conv_w, m_conv_b, m_dt_bias, m_a_log, m_d_skip, m_ssd_norm_w, m_f_bias, m_w_out, m_norm_mlp_w, m_w_up, m_w_down, m_norm_final_w, v_norm_mix_w, v_w_in, v_conv_w, v_conv_b, v_dt_bias, v_a_log, v_d_skip, v_ssd_norm_w, v_f_bias, v_w_out, v_norm_mlp_w, v_w_up, v_w_down, v_norm_final_w):
    chip = 2 * lax.axis_index("x") + lax.axis_index("y")
    cw = CONV_CH // N_CHIPS

    own_in = _pack_in(w_in[0])
    own_rest = _pack_rest(w_out[0], w_up[0], w_down[0])
    g_in = _gather_weights(own_in, name="gather_w_in")
    w_in_f = _full_w_in(g_in, own_in)
    *rest_handles, rest_token = _gather_start(own_rest, g_in, name="gather_start_rest")

    def rest_weights(after):
        _, landed = _gather_wait(*rest_handles, after, name="gather_wait_rest")
        return _full_rest(_gather_forward(landed, name="gather_forward_rest"), own_rest)
    small_all = _gather_small(_pack_small([conv_w[0]], 16), name="gather_conv_w")
    conv_w_f = jnp.concatenate([small_all[2 * j].reshape(-1)[:4 * cw].reshape(4, cw) for j in range(N_CHIPS)], axis=1)

    c = lax.axis_index("c")

    def chip_partial(gb, tag):
        half_rows = gb.shape[1] // 2
        from_sibling = _swap_halves(gb, name="grad_swap_halves_" + tag)
        my_half = lax.dynamic_slice_in_dim(gb, c * half_rows, half_rows, axis=1)
        return _add_pair(my_half, from_sibling, name="grad_add_sibling_" + tag)

    in_flight = {}

    def early_grads(gb_rest):
        part = chip_partial(gb_rest, "rest")
        *handles, token = _exchange_start(part, name="grad_exchange_start_rest")
        in_flight["rest"] = handles
        return token

    def late_grads(d_w_in):
        gb_in = jnp.stack([_pack_in(d_w_in[:, j * IN_SHARD:(j + 1) * IN_SHARD]) for j in range(N_CHIPS)])
        *handles, token = _exchange_start(chip_partial(gb_in, "in"), name="grad_exchange_start_in")
        in_flight["in"] = handles
        return token

    dx, loss_part, g = _local_step(x, loss_target, w_in_f, rest_weights, norm_mix_w, conv_w_f,
                                   conv_b, dt_bias, a_log, d_skip, ssd_norm_w, f_bias, norm_mlp_w, norm_final_w,
                                   first_after=rest_token, early_grads=early_grads, late_grads=late_grads)

    send_sems, recv_sems, part_rest, land_rest = in_flight["rest"]
    part_rest, parts_rest = _exchange_wait(send_sems, recv_sems, part_rest, land_rest, dx,
                                           name="grad_exchange_wait_rest")
    g_rest_half = _sum_parts(parts_rest, part_rest, name="grad_sum_chips_rest")
    g_w_out, g_w_up, g_w_down = _unpack_rest(_join_halves(g_rest_half, name="grad_join_halves_rest"))

    part_in, parts_in = _exchange_wait(*in_flight["in"], dx, name="grad_exchange_wait_in")
    g_in_half = _sum_parts(parts_in, part_in, name="grad_sum_chips_in")
    g_w_in = _join_halves(g_in_half, name="grad_join_halves_in")[:, :IN_SHARD]

    small_vals = [g[k] for k in SMALL_ORDER] + [loss_part[:, 0:1]]
    small_sum = _sum_leading(_gather_small(_pack_small(small_vals, SMALL_ROWS), name="gather_small_grads"), name="small_sum")
    sg = dict(zip(SMALL_ORDER + ("loss",), _unpack_small(small_sum, SMALL_SIZES + (1,))))
    loss = sg["loss"].reshape(())
    g_conv_full = sg["conv_w"].reshape(4, CONV_CH)
    g_conv = lax.dynamic_slice_in_dim(g_conv_full, chip * cw, cw, axis=1)

    grads = dict(norm_mix_w=sg["norm_mix_w"].reshape(1, -1), w_in=g_w_in[None], conv_w=g_conv[None],
                 conv_b=sg["conv_b"].reshape(1, -1), dt_bias=sg["dt_bias"].reshape(1, -1),
                 a_log=sg["a_log"].reshape(1, -1), d_skip=sg["d_skip"].reshape(1, -1),
                 ssd_norm_w=sg["ssd_norm_w"].reshape(1, -1), f_bias=sg["f_bias"].reshape(1, -1), w_out=g_w_out[None],
                 norm_mlp_w=sg["norm_mlp_w"].reshape(1, -1), w_up=g_w_up[None], w_down=g_w_down[None],
                 norm_final_w=sg["norm_final_w"])
    weights = dict(norm_mix_w=norm_mix_w, w_in=w_in, conv_w=conv_w, conv_b=conv_b, dt_bias=dt_bias, a_log=a_log,
                   d_skip=d_skip, ssd_norm_w=ssd_norm_w, f_bias=f_bias, w_out=w_out, norm_mlp_w=norm_mlp_w,
                   w_up=w_up, w_down=w_down, norm_final_w=norm_final_w)
    ms = dict(norm_mix_w=m_norm_mix_w, w_in=m_w_in, conv_w=m_conv_w, conv_b=m_conv_b, dt_bias=m_dt_bias,
              a_log=m_a_log, d_skip=m_d_skip, ssd_norm_w=m_ssd_norm_w, f_bias=m_f_bias, w_out=m_w_out,
              norm_mlp_w=m_norm_mlp_w, w_up=m_w_up, w_down=m_w_down, norm_final_w=m_norm_final_w)
    vs = dict(norm_mix_w=v_norm_mix_w, w_in=v_w_in, conv_w=v_conv_w, conv_b=v_conv_b, dt_bias=v_dt_bias,
              a_log=v_a_log, d_skip=v_d_skip, ssd_norm_w=v_ssd_norm_w, f_bias=v_f_bias, w_out=v_w_out,
              norm_mlp_w=v_norm_mlp_w, w_up=v_w_up, w_down=v_w_down, norm_final_w=v_norm_final_w)
    names = list(weights)
    big = ("w_in", "w_out", "w_up", "w_down")
    delta, new_m, new_v = {}, {}, {}
    for k, g2 in zip(big[1:], (g_w_out, g_w_up, g_w_down)):
        delta[k], new_m[k], new_v[k] = _adamw(weights[k], g2, ms[k], vs[k], name="adamw_" + k)
    g_in_t = g_w_in.T
    outs_t = _adamw(w_in[0].T, g_in_t, m_w_in[0].T, v_w_in[0].T, name="adamw_w_in")
    delta["w_in"], new_m["w_in"], new_v["w_in"] = [o.T[None] for o in outs_t]
    grads["w_in"] = g_in_t.T[None]
    smalls = [k for k in names if k not in big]
    sizes = [math.prod(weights[k].shape) for k in smalls]
    rows = -(-sum(sizes) // 1024) * 8
    packs = [_pack_small([d[k] for k in smalls], rows) for d in (weights, grads, ms, vs)]
    outs = _adamw(*packs, name="adamw_small")
    for o, dst in zip(outs, (delta, new_m, new_v)):
        for k, val in zip(smalls, _unpack_small(o, sizes)):
            dst[k] = val.reshape(weights[k].shape)
    return (loss, dx, *[grads[k] for k in names], *[delta[k] for k in names], *[new_m[k] for k in names],
            *[new_v[k] for k in names])
```

```python
import functools
import math

import jax
import jax.numpy as jnp
from jax import lax
from jax.experimental import pallas as pl
from jax.experimental.pallas import tpu as pltpu

F32 = jnp.float32
BF16 = jnp.bfloat16
HIGHEST = lax.Precision.HIGHEST
MESH = pl.DeviceIdType.MESH

D_MODEL = 1024
SSD_HEADS = 16
HEAD_DIM = 64
SSD_WIDTH = 1024
SSD_STATE = 128
CONV_CH = 1536
CHUNK = 128
ATT_WIDTH = 1024
EPS = 1e-5
IN_WIDTH = 5664
PA_WIDTH = 2688
QKV_WIDTH = 3072
D_FF = 4096
ATT_BLOCK = 256
NEG = -1e30
LOG2E = 1.4426950408889634
VMEM_LIMIT = 48 * 1024 * 1024

ADAM_LR = 0.001
ADAM_B1 = 0.9
ADAM_B2 = 0.999
ADAM_EPS = 1e-08
ADAM_WD = 0.01
ADAM_STEP = 10

N_CHIPS = 4
SMALL_ROWS = 96


def _cparams(sem):
    return pltpu.CompilerParams(dimension_semantics=sem, vmem_limit_bytes=VMEM_LIMIT)


def _pick(n, cands):
    for c in cands:
        if n % c == 0:
            return c
    return n


MM_CHUNK = 512


def _mm(a, b, *, name, tiles, ta=False, tb=False, out_dtype=F32, res=None, a_act=None, epi_up=None, after=None,
        into=None):
    n_unread = (after is not None) + (into is not None and into[3] is not None)
    if ta:
        K, M = a.shape
    else:
        M, K = a.shape
    if tb:
        N, K2 = b.shape
    else:
        K2, N = b.shape
    assert K == K2, (a.shape, b.shape)
    tm, tn, tk = tiles
    assert M % tm == 0 and N % tn == 0 and K % tk == 0, (name, M, N, K, tiles)
    nk = K // tk
    dn = (((0 if ta else 1,), (1 if tb else 0,)), ((), ()))
    has_res = res is not None
    has_up = epi_up is not None
    cn = _pick(tn, (MM_CHUNK, 384, 256, 128))

    def prologue(av):
        if a_act == "relu2":
            r = jnp.maximum(av.astype(F32), 0.0)
            av = r * r
        return av.astype(BF16)

    def epilogue(out, res_v, up_v):
        if has_res:
            out = out + res_v.astype(F32)
        if has_up:
            out = out * (2.0 * jnp.maximum(up_v.astype(F32), 0.0))
        return out.astype(out_dtype)

    def body(*refs):
        a_ref, b_ref = refs[0], refs[1]
        i = 2
        res_ref = up_ref = None
        if has_res:
            res_ref = refs[i]
            i += 1
        if has_up:
            up_ref = refs[i]
            i += 1
        i += n_unread
        o_ref = refs[i]
        if nk == 1:
            av = prologue(a_ref[...])
            for c in range(tn // cn):
                cs = slice(c * cn, (c + 1) * cn)
                bv = (b_ref[cs, :] if tb else b_ref[:, cs]).astype(BF16)
                out = lax.dot_general(av, bv, dn, preferred_element_type=F32)
                o_ref[:, cs] = epilogue(out, res_ref[:, cs] if has_res else None, up_ref[:, cs] if has_up else None)
            return
        acc_ref = refs[i + 1]
        k = pl.program_id(2)

        @pl.when(k == 0)
        def _():
            acc_ref[...] = jnp.zeros_like(acc_ref)

        acc_ref[...] += lax.dot_general(prologue(a_ref[...]), b_ref[...].astype(BF16), dn,
                                        preferred_element_type=F32)

        @pl.when(k == nk - 1)
        def _():
            out = epilogue(acc_ref[...], res_ref[...] if has_res else None, up_ref[...] if has_up else None)
            o_ref[...] = out.reshape(o_ref.shape)

    a_spec = pl.BlockSpec((tk, tm), lambda i, j, k: (k, i)) if ta else pl.BlockSpec((tm, tk), lambda i, j, k: (i, k))
    b_spec = pl.BlockSpec((tn, tk), lambda i, j, k: (j, k)) if tb else pl.BlockSpec((tk, tn), lambda i, j, k: (k, j))
    o_spec = pl.BlockSpec((tm, tn), lambda i, j, k: (i, j))
    ins, specs = [a, b], [a_spec, b_spec]
    if has_res:
        ins.append(res)
        specs.append(o_spec)
    if has_up:
        ins.append(epi_up)
        specs.append(o_spec)
    if after is not None:
        ins.append(after)
        specs.append(pl.BlockSpec(memory_space=pl.ANY))
    out_shape, out_spec, aliases = jax.ShapeDtypeStruct((M, N), out_dtype), o_spec, {}
    if into is not None:
        shape, block, index, buf = into
        out_shape, out_spec = jax.ShapeDtypeStruct(shape, out_dtype), pl.BlockSpec(block, index)
        if buf is not None:
            aliases = {len(ins): 0}
            ins.append(buf)
            specs.append(pl.BlockSpec(memory_space=pl.ANY))
    return pl.pallas_call(
        body, name=name, grid=(M // tm, N // tn, nk),
        in_specs=specs, out_specs=out_spec, out_shape=out_shape, input_output_aliases=aliases,
        scratch_shapes=[] if nk == 1 else [pltpu.VMEM((tm, tn), F32)],
        compiler_params=_cparams(("parallel", "parallel", "arbitrary")),
    )(*ins)


def _rows_product(a_ref, b_ref, tb, a_act):
    av = a_ref[...]
    if a_act == "relu2":
        r = jnp.maximum(av.astype(F32), 0.0)
        av = r * r
    dn = (((1,), (1 if tb else 0,)), ((), ()))
    return lax.dot_general(av.astype(BF16), b_ref[...].astype(BF16), dn, preferred_element_type=F32)


def _norm_mm(x, w, b, after, *, name, tm):
    m, d = x.shape
    n = b.shape[1]
    cn = _pick(n, (MM_CHUNK, 384, 256, 128))

    def body(x_ref, w_ref, b_ref, after_ref, h_ref, r_ref, o_ref):
        xv = x_ref[...]
        rstd = lax.rsqrt(jnp.mean(xv * xv, axis=1, keepdims=True) + EPS)
        hv = (xv * rstd * w_ref[...]).astype(BF16)
        h_ref[...] = hv
        r_ref[...] = rstd
        for c in range(n // cn):
            cs = slice(c * cn, (c + 1) * cn)
            o_ref[:, cs] = jnp.dot(hv, b_ref[:, cs].astype(BF16), preferred_element_type=F32)

    row = pl.BlockSpec((tm, d), lambda i: (i, 0))
    return pl.pallas_call(
        body, name=name, grid=(m // tm,),
        in_specs=[row, pl.BlockSpec((1, d), lambda i: (0, 0)), pl.BlockSpec((d, n), lambda i: (0, 0)),
                  pl.BlockSpec(memory_space=pl.ANY)],
        out_specs=[row, pl.BlockSpec((tm, 1), lambda i: (i, 0)), pl.BlockSpec((tm, n), lambda i: (i, 0))],
        out_shape=[jax.ShapeDtypeStruct((m, d), BF16), jax.ShapeDtypeStruct((m, 1), F32),
                   jax.ShapeDtypeStruct((m, n), F32)],
        compiler_params=_cparams(("parallel",)),
    )(x, w, b, after)


def _mm_norm_fwd(a1, b1, a2, b2, res, w, *, name, tm):
    m, k1 = a1.shape
    k2 = a2.shape[1]
    d = b1.shape[1]

    def body(a1_ref, b1_ref, a2_ref, b2_ref, res_ref, w_ref, h_ref, y_ref, r_ref):
        hv = _rows_product(a1_ref, b1_ref, False, None) + _rows_product(a2_ref, b2_ref, False, None) + res_ref[...]
        rstd = lax.rsqrt(jnp.mean(hv * hv, axis=1, keepdims=True) + EPS)
        h_ref[...] = hv
        y_ref[...] = (hv * rstd * w_ref[...]).astype(BF16)
        r_ref[...] = rstd

    row = pl.BlockSpec((tm, d), lambda i: (i, 0))
    return pl.pallas_call(
        body, name=name, grid=(m // tm,),
        in_specs=[pl.BlockSpec((tm, k1), lambda i: (i, 0)), pl.BlockSpec((k1, d), lambda i: (0, 0)),
                  pl.BlockSpec((tm, k2), lambda i: (i, 0)), pl.BlockSpec((k2, d), lambda i: (0, 0)), row,
                  pl.BlockSpec((1, d), lambda i: (0, 0))],
        out_specs=[row, row, pl.BlockSpec((tm, 1), lambda i: (i, 0))],
        out_shape=[jax.ShapeDtypeStruct((m, d), F32), jax.ShapeDtypeStruct((m, d), BF16),
                   jax.ShapeDtypeStruct((m, 1), F32)],
        compiler_params=_cparams(("parallel",)),
    )(a1, b1, a2, b2, res, w)


def _mm_final(a, b, res, w, target, *, name, tm, a_act):
    m, k = a.shape
    d = b.shape[1]

    def body(a_ref, b_ref, res_ref, w_ref, t_ref, dh_ref, dhb_ref, loss_ref, dw_ref):
        @pl.when(pl.program_id(0) == 0)
        def _():
            loss_ref[...] = jnp.zeros_like(loss_ref)
            dw_ref[...] = jnp.zeros_like(dw_ref)

        hv = _rows_product(a_ref, b_ref, False, a_act) + res_ref[...]
        wv = w_ref[...]
        rstd = lax.rsqrt(jnp.mean(hv * hv, axis=1, keepdims=True) + EPS)
        xhat = hv * rstd
        err = xhat * wv - t_ref[...]
        loss_ref[...] += 0.5 * jnp.sum(jnp.mean(err * err, axis=1, keepdims=True), axis=0, keepdims=True)
        dy = err * (1.0 / d)
        gw = dy * wv
        dh = rstd * (gw - xhat * jnp.mean(gw * xhat, axis=1, keepdims=True))
        dh_ref[...] = dh
        dhb_ref[...] = dh.astype(BF16)
        dw_ref[...] += jnp.sum(dy * xhat, axis=0, keepdims=True)

    row = pl.BlockSpec((tm, d), lambda i: (i, 0))
    vec = pl.BlockSpec((1, d), lambda i: (0, 0))
    return pl.pallas_call(
        body, name=name, grid=(m // tm,),
        in_specs=[pl.BlockSpec((tm, k), lambda i: (i, 0)), pl.BlockSpec((k, d), lambda i: (0, 0)), row, vec, row],
        out_specs=[row, row, pl.BlockSpec((1, 128), lambda i: (0, 0)), vec],
        out_shape=[jax.ShapeDtypeStruct((m, d), F32), jax.ShapeDtypeStruct((m, d), BF16),
                   jax.ShapeDtypeStruct((1, 128), F32), jax.ShapeDtypeStruct((1, d), F32)],
        compiler_params=_cparams(("arbitrary",)),
    )(a, b, res, w, target)


def _mm_two_halves(a, b, *, name, tm):
    m, k = a.shape
    d = b.shape[0] // 2

    def body(a_ref, b_ref, lo_ref, hi_ref):
        av = a_ref[...].astype(BF16)
        lo_ref[...] = lax.dot_general(av, b_ref[0:d, :].astype(BF16), NT_DIMS, preferred_element_type=F32)
        hi_ref[...] = lax.dot_general(av, b_ref[d:2 * d, :].astype(BF16), NT_DIMS,
                                      preferred_element_type=F32).astype(BF16)

    row = pl.BlockSpec((tm, d), lambda i: (i, 0))
    return pl.pallas_call(
        body, name=name, grid=(m // tm,),
        in_specs=[pl.BlockSpec((tm, k), lambda i: (i, 0)), pl.BlockSpec((2 * d, k), lambda i: (0, 0))],
        out_specs=[row, row],
        out_shape=[jax.ShapeDtypeStruct((m, d), F32), jax.ShapeDtypeStruct((m, d), BF16)],
        compiler_params=_cparams(("parallel",)),
    )(a, b)


def _mm_norm_bwd(pairs, x, rstd, w, dres, *, name, tm, after=None):
    m = pairs[0][0].shape[0]
    d = pairs[0][1].shape[0]
    n_pairs = len(pairs)

    def body(*refs):
        i = 2 * n_pairs
        x_ref, r_ref, w_ref, d_ref = refs[i:i + 4]
        dx_ref, dxb_ref, dw_ref = refs[-3:]

        @pl.when(pl.program_id(0) == 0)
        def _():
            dw_ref[...] = jnp.zeros_like(dw_ref)

        g = _rows_product(refs[0], refs[1], True, None)
        for p in range(1, n_pairs):
            g = g + _rows_product(refs[2 * p], refs[2 * p + 1], True, None)
        r = r_ref[...]
        xhat = x_ref[...] * r
        gw = g * w_ref[...]
        dx = d_ref[...] + r * (gw - xhat * jnp.mean(gw * xhat, axis=1, keepdims=True))
        dx_ref[...] = dx
        dxb_ref[...] = dx.astype(BF16)
        dw_ref[...] += jnp.sum(g * xhat, axis=0, keepdims=True)

    row = pl.BlockSpec((tm, d), lambda i: (i, 0))
    vec = pl.BlockSpec((1, d), lambda i: (0, 0))
    ins, specs = [], []
    for a, b in pairs:
        k = a.shape[1]
        ins += [a, b]
        specs += [pl.BlockSpec((tm, k), lambda i: (i, 0)), pl.BlockSpec((d, k), lambda i: (0, 0))]
    ins += [x, rstd, w, dres]
    specs += [row, pl.BlockSpec((tm, 1), lambda i: (i, 0)), vec, row]
    if after is not None:
        ins.append(after)
        specs.append(pl.BlockSpec(memory_space=pl.ANY))
    return pl.pallas_call(
        body, name=name, grid=(m // tm,), in_specs=specs, out_specs=[row, row, vec],
        out_shape=[jax.ShapeDtypeStruct((m, d), F32), jax.ShapeDtypeStruct((m, d), BF16),
                   jax.ShapeDtypeStruct((1, d), F32)],
        compiler_params=_cparams(("arbitrary",)),
    )(*ins)


def _softplus(x):
    return jnp.maximum(x, 0.0) + jnp.log(1.0 + jnp.exp(-jnp.abs(x)))


def _prep(proj_a, bias128, alog128, bl, t):
    n = bl * t
    nch = t // CHUNK
    col0 = (SSD_WIDTH + CONV_CH) // 128

    def body(p_ref, b_ref, al_ref, dt_ref, gd_ref, ac_ref, c_ref):
        row = lax.broadcasted_iota(jnp.int32, (CHUNK, CHUNK), 0)
        col = lax.broadcasted_iota(jnp.int32, (CHUNK, CHUNK), 1)
        tril = (row >= col).astype(F32)
        lane = lax.broadcasted_iota(jnp.int32, (1, 128), 1)
        head_lanes = lane < 16
        a_row = -jnp.exp(al_ref[...])
        carry = jnp.zeros((1, 128), F32)
        for ci in range(nch):
            rows = slice(ci * CHUNK, (ci + 1) * CHUNK)
            xv = p_ref[rows, :] + b_ref[...]
            sp = _softplus(xv)
            acum = jnp.dot(tril, a_row * sp, precision=HIGHEST, preferred_element_type=F32)
            c = jnp.dot(tril, -_softplus(-xv), precision=HIGHEST, preferred_element_type=F32) + carry
            carry = c[CHUNK - 1:CHUNK, :]
            dt_ref[rows, :] = jnp.where(head_lanes, sp, 0.0)
            gd_ref[rows, :] = jnp.where(head_lanes, jax.nn.sigmoid(xv),
                                        jnp.where(lane < 32, jax.nn.sigmoid(-xv), 0.0))
            ac_ref[rows, :] = jnp.where(head_lanes, acum, 0.0)
            c_ref[rows, :] = c[:, 16:32]

    o128 = pl.BlockSpec((t, 128), lambda b: (b, 0))
    v128 = pl.BlockSpec((1, 128), lambda b: (0, 0))
    w128 = jax.ShapeDtypeStruct((n, 128), F32)
    return pl.pallas_call(
        body, name="head_scalars", grid=(bl,),
        in_specs=[pl.BlockSpec((t, 128), lambda b: (b, col0)), v128, v128],
        out_specs=[o128, o128, o128, pl.BlockSpec((t, 16), lambda b: (b, 0))],
        out_shape=[w128, w128, w128, jax.ShapeDtypeStruct((n, 16), F32)],
        compiler_params=_cparams(("parallel",)),
    )(proj_a, bias128, alog128)


def _fpost(dc, gate_d, ddt, dpa, bl, t):
    n = bl * t
    nch = t // CHUNK
    col0 = (SSD_WIDTH + CONV_CH) // 128

    def body(dc_ref, gd_ref, ddt_ref, dpa_in, out_ref, db_ref):
        @pl.when(pl.program_id(0) == 0)
        def _():
            db_ref[...] = jnp.zeros_like(db_ref)

        row = lax.broadcasted_iota(jnp.int32, (CHUNK, CHUNK), 0)
        col = lax.broadcasted_iota(jnp.int32, (CHUNK, CHUNK), 1)
        triu = (row <= col).astype(F32)
        lane = lax.broadcasted_iota(jnp.int32, (1, 128), 1)
        gate_lanes = (lane >= 16) & (lane < 32)
        carry = jnp.zeros((1, 128), F32)
        db = jnp.zeros((1, 128), F32)
        for ci in reversed(range(nch)):
            rows = slice(ci * CHUNK, (ci + 1) * CHUNK)
            dlf = jnp.dot(triu, dc_ref[rows, :], precision=HIGHEST, preferred_element_type=F32) + carry
            carry = dlf[0:1, :]
            df = jnp.where(gate_lanes, dlf * gd_ref[rows, :], 0.0)
            out_ref[rows, :] = (ddt_ref[rows, :] + df).astype(BF16)
            db = db + jnp.sum(df, axis=0, keepdims=True)
        db_ref[...] += db[:, 16:32]

    blk = pl.BlockSpec((t, 128), lambda b: (b, 0))
    return pl.pallas_call(
        body, name="forget_gate_bwd", grid=(bl,),
        in_specs=[blk, blk, blk, ANY],
        out_specs=[pl.BlockSpec((t, 128), lambda b: (b, col0)), pl.BlockSpec((1, 16), lambda b: (0, 0))],
        out_shape=[jax.ShapeDtypeStruct(dpa.shape, dpa.dtype), jax.ShapeDtypeStruct((1, 16), F32)],
        input_output_aliases={3: 0},
        compiler_params=_cparams(("arbitrary",)),
    )(dc, gate_d, ddt, dpa)


CONV_TILE = 256
CONV_ROWS = 256


def _conv_taps(u_ref, i, w, bias):
    r0 = pl.multiple_of(i * CONV_ROWS, CONV_ROWS)
    cur = u_ref[pl.ds(r0, CONV_ROWS), :]
    p0 = pl.multiple_of(jnp.maximum(r0 - 8, 0), 8)
    prev = jnp.where(i > 0, u_ref[pl.ds(p0, 8), :], 0.0)
    cat = jnp.concatenate([prev, cur], axis=0)
    pre = bias + w[3:4, :] * cur
    taps = [cur]
    for s in (1, 2, 3):
        sh = pltpu.roll(cat, s, 0)[8:, :]
        taps.append(sh)
        pre = pre + w[3 - s:4 - s, :] * sh
    return r0, pre, taps


def _conv_fwd(proj_a, conv_w, conv_b, bl, t):
    n = bl * t
    nct = CONV_CH // CONV_TILE
    c0 = SSD_WIDTH // CONV_TILE

    def body(u_ref, w_ref, b_ref, o_ref):
        w = w_ref[...]
        bias = b_ref[...]

        def chunk(i, carry):
            r0, pre, _ = _conv_taps(u_ref, i, w, bias)
            o_ref[pl.ds(r0, CONV_ROWS), :] = pre * jax.nn.sigmoid(pre)
            return carry

        lax.fori_loop(0, t // CONV_ROWS, chunk, 0)

    return pl.pallas_call(
        body, name="conv_silu_fwd", grid=(bl, nct),
        in_specs=[pl.BlockSpec((t, CONV_TILE), lambda b, c: (b, c0 + c)),
                  pl.BlockSpec((4, CONV_TILE), lambda b, c: (0, c)),
                  pl.BlockSpec((1, CONV_TILE), lambda b, c: (0, c))],
        out_specs=pl.BlockSpec((t, CONV_TILE), lambda b, c: (b, c)),
        out_shape=jax.ShapeDtypeStruct((n, CONV_CH), F32),
        compiler_params=_cparams(("parallel", "parallel")),
    )(proj_a, conv_w, conv_b)


def _conv_bwd(dxc, proj_a, conv_w, conv_b, dpa, bl, t):
    nct = CONV_CH // CONV_TILE
    c0 = SSD_WIDTH // CONV_TILE
    nrc = t // CONV_ROWS

    def body(g_ref, u_ref, w_ref, b_ref, dpa_in, du_ref, dw_ref, db_ref, dp_scr):
        @pl.when(pl.program_id(1) == 0)
        def _():
            dw_ref[...] = jnp.zeros_like(dw_ref)
            db_ref[...] = jnp.zeros_like(db_ref)

        w = w_ref[...]
        bias = b_ref[...]
        dp_scr[pl.ds(t, 8), :] = jnp.zeros((8, CONV_TILE), F32)

        def chunk1(i, carry):
            dw0, dw1, dw2, dw3, db = carry
            r0, pre, taps = _conv_taps(u_ref, i, w, bias)
            sg = jax.nn.sigmoid(pre)
            dpre = g_ref[pl.ds(r0, CONV_ROWS), :] * (sg * (1.0 + pre * (1.0 - sg)))
            dp_scr[pl.ds(r0, CONV_ROWS), :] = dpre
            dw3 = dw3 + jnp.sum(dpre * taps[0], axis=0, keepdims=True)
            dw2 = dw2 + jnp.sum(dpre * taps[1], axis=0, keepdims=True)
            dw1 = dw1 + jnp.sum(dpre * taps[2], axis=0, keepdims=True)
            dw0 = dw0 + jnp.sum(dpre * taps[3], axis=0, keepdims=True)
            db = db + jnp.sum(dpre, axis=0, keepdims=True)
            return dw0, dw1, dw2, dw3, db

        z = jnp.zeros((1, CONV_TILE), F32)
        dw0, dw1, dw2, dw3, db = lax.fori_loop(0, nrc, chunk1, (z, z, z, z, z))
        dw_ref[...] += jnp.concatenate([dw0, dw1, dw2, dw3], axis=0)
        db_ref[...] += db

        def chunk2(i, carry):
            r0 = pl.multiple_of(i * CONV_ROWS, CONV_ROWS)
            cat = dp_scr[pl.ds(r0, CONV_ROWS + 8), :]
            du = w[3:4, :] * cat[:CONV_ROWS, :]
            for s in (1, 2, 3):
                du = du + w[3 - s:4 - s, :] * pltpu.roll(cat, CONV_ROWS + 8 - s, 0)[:CONV_ROWS, :]
            du_ref[pl.ds(r0, CONV_ROWS), :] = du.astype(BF16)
            return carry

        lax.fori_loop(0, nrc, chunk2, 0)

    return pl.pallas_call(
        body, name="conv_silu_bwd", grid=(nct, bl),
        in_specs=[pl.BlockSpec((t, CONV_TILE), lambda c, b: (b, c)),
                  pl.BlockSpec((t, CONV_TILE), lambda c, b: (b, c0 + c)),
                  pl.BlockSpec((4, CONV_TILE), lambda c, b: (0, c)),
                  pl.BlockSpec((1, CONV_TILE), lambda c, b: (0, c)), ANY],
        out_specs=[pl.BlockSpec((t, CONV_TILE), lambda c, b: (b, c0 + c)),
                   pl.BlockSpec((4, CONV_TILE), lambda c, b: (0, c)),
                   pl.BlockSpec((1, CONV_TILE), lambda c, b: (0, c))],
        out_shape=[jax.ShapeDtypeStruct(dpa.shape, dpa.dtype), jax.ShapeDtypeStruct((4, CONV_CH), F32),
                   jax.ShapeDtypeStruct((1, CONV_CH), F32)],
        input_output_aliases={4: 0},
        scratch_shapes=[pltpu.VMEM((t + 8, CONV_TILE), F32)],
        compiler_params=_cparams(("parallel", "arbitrary")),
    )(dxc, proj_a, conv_w, conv_b, dpa)


NT_DIMS = (((1,), (1,)), ((), ()))
TN_DIMS = (((0,), (0,)), ((), ()))


def _dot(a, b, dims=None):
    if dims is None:
        return jnp.dot(a, b, preferred_element_type=F32)
    return lax.dot_general(a, b, dims, preferred_element_type=F32)


def _head_expander():
    r = lax.broadcasted_iota(jnp.int32, (128, SSD_WIDTH), 0)
    c = lax.broadcasted_iota(jnp.int32, (128, SSD_WIDTH), 1)
    return ((c // HEAD_DIM == r % 16) & (r < 48)).astype(BF16)


def _spread(v128, expander):
    hi = v128.astype(BF16).astype(F32)
    r1 = v128 - hi
    mid = r1.astype(BF16).astype(F32)
    lo = (r1 - mid).astype(BF16).astype(F32)
    packed = (hi + pltpu.roll(mid, 16, 1) + pltpu.roll(lo, 32, 1)).astype(BF16)
    return jnp.dot(packed, expander, preferred_element_type=F32)


def _head_sums(v1024, expander):
    hi = v1024.astype(BF16)
    lo = (v1024 - hi.astype(F32)).astype(BF16)
    heads = jnp.where(lax.broadcasted_iota(jnp.int32, expander.shape, 0) < 16, expander, jnp.zeros_like(expander))
    return _dot(hi, heads, NT_DIMS) + _dot(lo, heads, NT_DIMS)


def _ssd_fwd(xc, proj_a, dt, acum, acum_t, dskip_e, norm_w, bl, t):
    n = bl * t
    nch = t // CHUNK
    L = CHUNK

    def body(xc_ref, z_ref, dt_ref, ac_ref, act_ref, dsk_ref, nw_ref, ys_ref, yp_ref, hp_ref, h_scr, y_scr, x_scr):
        @pl.when(pl.program_id(1) == 0)
        def _():
            h_scr[...] = jnp.zeros_like(h_scr)

        row = lax.broadcasted_iota(jnp.int32, (L, L), 0)
        col = lax.broadcasted_iota(jnp.int32, (L, L), 1)
        causal = row >= col
        lane128 = lax.broadcasted_iota(jnp.int32, (1, L), 1)
        expander = _head_expander()
        ac_all = ac_ref[...]
        act_all = act_ref[...]
        ac_e = _spread(ac_all, expander)
        e_in = jnp.exp(ac_e)
        dec = jnp.exp(ac_e[L - 1:L, :] - ac_e)
        xs_all = xc_ref[:, 0:SSD_WIDTH]
        x_all = xs_all * _spread(dt_ref[...], expander)
        x_scr[...] = x_all.astype(BF16)
        hp_all = h_scr[...]
        hp_ref[...] = hp_all
        for g in range(2):
            gs = slice(g * 512, (g + 1) * 512)
            bg = xc_ref[:, SSD_WIDTH + g * 128:SSD_WIDTH + (g + 1) * 128].astype(BF16)
            cg = xc_ref[:, SSD_WIDTH + 256 + g * 128:SSD_WIDTH + 256 + (g + 1) * 128].astype(BF16)
            gmat = _dot(cg, bg, NT_DIMS)
            y_off = _dot(cg, hp_all[gs, :].astype(BF16), NT_DIMS) * e_in[:, gs] + dsk_ref[:, gs] * xs_all[:, gs]
            s_new = _dot((x_all[:, gs] * dec[:, gs]).astype(BF16), bg, TN_DIMS)
            for pr in range(4):
                pair = slice((g * 4 + pr) * 128, (g * 4 + pr + 1) * 128)
                x_pair = x_scr[:, pair]
                y_pair = y_off[:, pr * 128:(pr + 1) * 128]
                for j in range(2):
                    h = g * 8 + 2 * pr + j
                    sl = slice(h * HEAD_DIM, (h + 1) * HEAD_DIM)
                    r = 2 * pr + j
                    ldec = jnp.exp(jnp.where(causal, ac_all[:, h:h + 1] - act_all[h:h + 1, :], NEG))
                    x_head = jnp.where((lane128 < HEAD_DIM) == (j == 0), x_pair, jnp.zeros_like(x_pair))
                    y_pair = y_pair + _dot((gmat * ldec).astype(BF16), x_head)
                    elast = jnp.exp(ac_all[L - 1:L, h:h + 1])
                    h_scr[sl, :] = elast * hp_all[sl, :] + s_new[r * HEAD_DIM:(r + 1) * HEAD_DIM, :]
                y_scr[:, pair] = y_pair
        y = y_scr[...]
        yp_ref[...] = y
        zv = z_ref[...]
        yg = y * (zv * jax.nn.sigmoid(zv))
        for g in range(2):
            gs = slice(g * 512, (g + 1) * 512)
            grp = yg[:, gs]
            rstd = lax.rsqrt(jnp.mean(grp * grp, axis=1, keepdims=True) + EPS)
            ys_ref[:, gs] = (grp * rstd * nw_ref[:, gs]).astype(BF16)

    rb = lambda b, c: (b * nch + c, 0)
    v1k = pl.BlockSpec((1, SSD_WIDTH), lambda b, c: (0, 0))
    return pl.pallas_call(
        body, name="ssd_fwd", grid=(bl, nch),
        in_specs=[pl.BlockSpec((L, CONV_CH), rb), pl.BlockSpec((L, SSD_WIDTH), rb),
                  pl.BlockSpec((L, 128), rb), pl.BlockSpec((L, 128), rb),
                  pl.BlockSpec((16, L), lambda b, c: (0, b * nch + c)), v1k, v1k],
        out_specs=[pl.BlockSpec((L, SSD_WIDTH), rb), pl.BlockSpec((L, SSD_WIDTH), rb),
                   pl.BlockSpec((None, SSD_WIDTH, SSD_STATE), lambda b, c: (b * nch + c, 0, 0))],
        out_shape=[jax.ShapeDtypeStruct((n, SSD_WIDTH), BF16), jax.ShapeDtypeStruct((n, SSD_WIDTH), F32),
                   jax.ShapeDtypeStruct((bl * nch, SSD_WIDTH, SSD_STATE), F32)],
        scratch_shapes=[pltpu.VMEM((SSD_WIDTH, SSD_STATE), F32), pltpu.VMEM((L, SSD_WIDTH), F32),
                        pltpu.VMEM((L, SSD_WIDTH), BF16)],
        compiler_params=_cparams(("parallel", "arbitrary")),
    )(xc, proj_a, dt, acum, acum_t, dskip_e, norm_w)


def _ssd_bwd(dys, xc, proj_a, ypre, hprev, dt, gate_d, acum, acum_t, alog128, dskip_e, norm_w, bl, t):
    n = bl * t
    nch = t // CHUNK
    L = CHUNK

    def body(dys_ref, xc_ref, z_ref, yp_ref, hp_ref, dt_ref, gd_ref, ac_ref, act_ref, al_ref, dsk_ref, nw_ref,
             dxc_ref, dz_ref, ddt_ref, dnw_ref, dsk16_ref, da16_ref, db16_ref,
             dh_scr, dy_scr, x_scr, dx_scr, red_scr):
        first = (pl.program_id(0) == 0) & (pl.program_id(1) == 0)

        @pl.when(first)
        def _():
            dnw_ref[...] = jnp.zeros_like(dnw_ref)
            dsk16_ref[...] = jnp.zeros_like(dsk16_ref)
            da16_ref[...] = jnp.zeros_like(da16_ref)
            db16_ref[...] = jnp.zeros_like(db16_ref)

        @pl.when(pl.program_id(1) == 0)
        def _():
            dh_scr[...] = jnp.zeros_like(dh_scr)

        y = yp_ref[...]
        zv = z_ref[...]
        sz = jax.nn.sigmoid(zv)
        gate = zv * sz
        yg = y * gate
        dout = dys_ref[...]
        nw = nw_ref[...]
        for g in range(2):
            gs = slice(g * 512, (g + 1) * 512)
            grp = yg[:, gs]
            rstd = lax.rsqrt(jnp.mean(grp * grp, axis=1, keepdims=True) + EPS)
            ghat = grp * rstd
            dnw_ref[:, gs] += jnp.sum(dout[:, gs] * ghat, axis=0, keepdims=True)
            gw = dout[:, gs] * nw[:, gs]
            dyg = rstd * (gw - ghat * jnp.mean(gw * ghat, axis=1, keepdims=True))
            dy_scr[:, gs] = dyg * gate[:, gs]
            dz_ref[:, gs] = (dyg * y[:, gs] * (sz[:, gs] * (1.0 + zv[:, gs] * (1.0 - sz[:, gs])))).astype(BF16)

        row = lax.broadcasted_iota(jnp.int32, (L, L), 0)
        col = lax.broadcasted_iota(jnp.int32, (L, L), 1)
        causal = row >= col
        lane128 = lax.broadcasted_iota(jnp.int32, (1, L), 1)
        rows128 = lax.broadcasted_iota(jnp.int32, (L, 1), 0)
        last_row = rows128 == (L - 1)
        expander = _head_expander()
        ac_all = ac_ref[...]
        act_all = act_ref[...]
        dt_all = dt_ref[...]
        dt_e = _spread(dt_all, expander)
        ac_e = _spread(ac_all, expander)
        e_in = jnp.exp(ac_e)
        dec = jnp.exp(ac_e[L - 1:L, :] - ac_e)
        xs_all = xc_ref[:, 0:SSD_WIDTH]
        x_all = xs_all * dt_e
        x_scr[...] = x_all.astype(BF16)
        dy_all = dy_scr[...]
        hp_all = hp_ref[...]
        ds_all = dh_scr[...]
        dsk_cols = jnp.sum(dy_all * xs_all, axis=0, keepdims=True)
        dac = jnp.zeros((L, L), F32)
        dac_row = jnp.zeros((L, L), F32)
        ddec_cols = []
        for g in range(2):
            gs = slice(g * 512, (g + 1) * 512)
            bsl = slice(SSD_WIDTH + g * 128, SSD_WIDTH + (g + 1) * 128)
            csl = slice(SSD_WIDTH + 256 + g * 128, SSD_WIDTH + 256 + (g + 1) * 128)
            bg = xc_ref[:, bsl].astype(BF16)
            cg = xc_ref[:, csl].astype(BF16)
            gmat = _dot(cg, bg, NT_DIMS)
            hpb = hp_all[gs, :].astype(BF16)
            dsb = ds_all[gs, :].astype(BF16)
            ch = _dot(cg, hpb, NT_DIMS)
            dye = dy_all[:, gs] * e_in[:, gs]
            dyeb = dye.astype(BF16)
            dc_acc = _dot(dyeb, hpb)
            dhp = _dot(dyeb, cg, TN_DIMS)
            dxd = _dot(bg, dsb, NT_DIMS)
            db_acc = _dot((x_all[:, gs] * dec[:, gs]).astype(BF16), dsb)
            ddec = dxd * x_all[:, gs] * dec[:, gs]
            ddec_cols.append(jnp.sum(ddec, axis=0, keepdims=True))
            dx_inter = dxd * dec[:, gs]
            red_scr[:, gs] = dye * ch - ddec
            dg_sum = jnp.zeros((L, L), F32)
            for pr in range(4):
                pair = slice((g * 4 + pr) * 128, (g * 4 + pr + 1) * 128)
                x_pair = x_scr[:, pair]
                dy_pair = dy_scr[:, pair].astype(BF16)
                dx_pair = dx_inter[:, pr * 128:(pr + 1) * 128]
                for j in range(2):
                    h = g * 8 + 2 * pr + j
                    r = 2 * pr + j
                    sl = slice(h * HEAD_DIM, (h + 1) * HEAD_DIM)
                    onehot_w = lane128 == h
                    ldec = jnp.exp(jnp.where(causal, ac_all[:, h:h + 1] - act_all[h:h + 1, :], NEG))
                    mf = gmat * ldec
                    dyb = jnp.where((lane128 < HEAD_DIM) == (j == 0), dy_pair, jnp.zeros_like(dy_pair))
                    dm = _dot(dyb, x_pair, NT_DIMS)
                    dx_pair = dx_pair + _dot(mf.astype(BF16), dyb, TN_DIMS)
                    dg_sum = dg_sum + dm * ldec
                    wmat = dm * mf
                    elast = jnp.exp(ac_all[L - 1:L, h:h + 1])
                    hp_h = hp_all[sl, :]
                    ds_h = ds_all[sl, :]
                    extra = elast * jnp.sum(jnp.sum(hp_h * ds_h, axis=1, keepdims=True), axis=0, keepdims=True)
                    dac = dac + jnp.where(onehot_w,
                                          jnp.sum(wmat, axis=1, keepdims=True) + jnp.where(last_row, extra, 0.0), 0.0)
                    dac_row = dac_row + jnp.where(rows128 == h, -jnp.sum(wmat, axis=0, keepdims=True), 0.0)
                    dh_scr[sl, :] = elast * ds_h + dhp[r * HEAD_DIM:(r + 1) * HEAD_DIM, :]
                dx_scr[:, pair] = dx_pair
            dgb = dg_sum.astype(BF16)
            dxc_ref[:, csl] = dc_acc + _dot(dgb, bg)
            dxc_ref[:, bsl] = db_acc + _dot(dgb, cg, TN_DIMS)
        dx_all = dx_scr[...]
        dxc_ref[:, 0:SSD_WIDTH] = dx_all * dt_e + dsk_ref[...] * dy_all
        red = red_scr[...]
        dac_slab = _head_sums(red, expander)
        ddec_tot = _head_sums(jnp.broadcast_to(jnp.concatenate(ddec_cols, axis=1), (8, SSD_WIDTH)), expander)
        ddt_x = _head_sums(dx_all * xs_all, expander)
        dsk16_ref[...] += _head_sums(jnp.broadcast_to(dsk_cols, (8, SSD_WIDTH)), expander)[0:1, 0:16]
        dac = dac + dac_slab + jnp.transpose(dac_row) + jnp.where(last_row, ddec_tot[0:1, :], 0.0)
        triu = (row <= col).astype(F32)
        da = jnp.dot(triu, dac, precision=HIGHEST, preferred_element_type=F32)
        a_row = -jnp.exp(al_ref[...])
        ddt = jnp.where(lane128 < 16, (ddt_x + da * a_row) * gd_ref[...], 0.0)
        ddt_ref[...] = ddt
        da16_ref[...] += (jnp.sum(da * dt_all, axis=0, keepdims=True) * a_row)[:, 0:16]
        db16_ref[...] += jnp.sum(ddt, axis=0, keepdims=True)[:, 0:16]

    rb = lambda b, c: (b * nch + nch - 1 - c, 0)
    v1k = pl.BlockSpec((1, SSD_WIDTH), lambda b, c: (0, 0))
    v16 = pl.BlockSpec((1, 16), lambda b, c: (0, 0))
    v128 = pl.BlockSpec((1, 128), lambda b, c: (0, 0))
    wide = pl.BlockSpec((L, SSD_WIDTH), rb)
    s128 = pl.BlockSpec((L, 128), rb)
    return pl.pallas_call(
        body, name="ssd_bwd", grid=(bl, nch),
        in_specs=[wide, pl.BlockSpec((L, CONV_CH), rb), wide, wide,
                  pl.BlockSpec((None, SSD_WIDTH, SSD_STATE), lambda b, c: (b * nch + nch - 1 - c, 0, 0)),
                  s128, s128, s128, pl.BlockSpec((16, L), lambda b, c: (0, b * nch + nch - 1 - c)), v128, v1k, v1k],
        out_specs=[pl.BlockSpec((L, CONV_CH), rb), wide, s128, v1k, v16, v16, v16],
        out_shape=[jax.ShapeDtypeStruct((n, CONV_CH), F32), jax.ShapeDtypeStruct((n, PA_WIDTH), BF16),
                   jax.ShapeDtypeStruct((n, 128), F32), jax.ShapeDtypeStruct((1, SSD_WIDTH), F32),
                   jax.ShapeDtypeStruct((1, 16), F32), jax.ShapeDtypeStruct((1, 16), F32),
                   jax.ShapeDtypeStruct((1, 16), F32)],
        scratch_shapes=[pltpu.VMEM((SSD_WIDTH, SSD_STATE), F32), pltpu.VMEM((L, SSD_WIDTH), F32),
                        pltpu.VMEM((L, SSD_WIDTH), BF16), pltpu.VMEM((L, SSD_WIDTH), F32),
                        pltpu.VMEM((L, SSD_WIDTH), F32)],
        compiler_params=_cparams(("arbitrary", "arbitrary")),
    )(dys, xc, proj_a, ypre, hprev, dt, gate_d, acum, acum_t, alog128, dskip_e, norm_w)


def _attn_fwd(qkv, negc, bl, t):
    n = bl * t
    tb_ = ATT_BLOCK
    nb = t // tb_
    scale2 = LOG2E / math.sqrt(HEAD_DIM)

    def body(q_ref, k_ref, v_ref, c_ref, o_ref, lse_ref, v0_scr, v1_scr, k0_scr, k1_scr):
        row = lax.broadcasted_iota(jnp.int32, (tb_, tb_), 0)
        col = lax.broadcasted_iota(jnp.int32, (tb_, tb_), 1)
        causal = row >= col
        lane = lax.broadcasted_iota(jnp.int32, (1, 128), 1)
        v_pair = v_ref[...].astype(F32)
        k_pair = k_ref[...]
        v_scrs = (v0_scr, v1_scr)
        k_scrs = (k0_scr, k1_scr)
        for j in range(2):
            v_head = v_pair if j == 0 else pltpu.roll(v_pair, HEAD_DIM, 1)
            v_scrs[j][...] = jnp.where(lane < HEAD_DIM, v_head, jnp.where(lane == HEAD_DIM, 1.0, 0.0)).astype(BF16)
            k_scrs[j][...] = jnp.where((lane < HEAD_DIM) == (j == 0), k_pair, jnp.zeros_like(k_pair))
        for qi in range(nb):
            r0, lk = qi * tb_, (qi + 1) * tb_
            for j in range(2):
                sl = slice(j * HEAD_DIM, (j + 1) * HEAD_DIM)
                s = _dot(q_ref[r0:lk, :], k_scrs[j][0:lk, :], NT_DIMS) * scale2 + c_ref[j:j + 1, 0:lk] * LOG2E
                tail = jnp.where(causal, s[:, r0:lk], NEG)
                s = tail if qi == 0 else jnp.concatenate([s[:, 0:r0], tail], axis=1)
                m = jnp.max(s, axis=1, keepdims=True)
                p = jnp.exp2(s - m)
                acc = _dot(p.astype(BF16), v_scrs[j][0:lk, :])
                l = acc[:, HEAD_DIM:HEAD_DIM + 1]
                o_ref[r0:lk, sl] = (acc[:, 0:HEAD_DIM] / l).astype(BF16)
                lse_ref[r0:lk, sl] = jnp.broadcast_to(m + jnp.log(l) * LOG2E, (tb_, HEAD_DIM))

    blk = lambda off: pl.BlockSpec((t, 128), lambda b, hp: (b, off + hp))
    return pl.pallas_call(
        body, name="fox_attn_fwd", grid=(bl, 8),
        in_specs=[blk(0), blk(8), blk(16), pl.BlockSpec((None, None, 8, t), lambda b, hp: (b, hp, 0, 0))],
        out_specs=[blk(0), blk(0)],
        out_shape=[jax.ShapeDtypeStruct((n, ATT_WIDTH), BF16), jax.ShapeDtypeStruct((n, ATT_WIDTH), F32)],
        scratch_shapes=[pltpu.VMEM((t, 128), BF16)] * 4,
        compiler_params=_cparams(("parallel", "parallel")),
    )(qkv, qkv, qkv, negc)


def _attn_bwd(qkv, do, o, lse, negc, after, bl, t):
    n = bl * t
    tb_ = ATT_BLOCK
    nb = t // tb_
    scale = 1.0 / math.sqrt(HEAD_DIM)
    scale2 = LOG2E * scale

    def body(q_ref, k_ref, v_ref, do_ref, o_ref, lse_ref, c_ref, after_ref, dq_ref, dk_ref, dv_ref, dc_ref,
             dq0_scr, delta_scr, dq1_scr, qt0_scr, qt1_scr, dot_scr, dkt0_scr, dkt1_scr, dvt_scr):
        row = lax.broadcasted_iota(jnp.int32, (tb_, tb_), 0)
        col = lax.broadcasted_iota(jnp.int32, (tb_, tb_), 1)
        causal = row >= col
        lane = lax.broadcasted_iota(jnp.int32, (1, 128), 1)
        dq_scrs = (dq0_scr, dq1_scr)
        qt_scrs = (qt0_scr, qt1_scr)
        dkt_scrs = (dkt0_scr, dkt1_scr)
        dq0_scr[...] = jnp.zeros_like(dq0_scr)
        dq1_scr[...] = jnp.zeros_like(dq1_scr)
        dc_ref[...] = jnp.zeros_like(dc_ref)
        q_t = jnp.transpose(q_ref[...].astype(F32))
        ones_row = jnp.where(lax.broadcasted_iota(jnp.int32, (8, t), 0) == 0, 1.0, 0.0)
        for j in range(2):
            qt_scrs[j][...] = jnp.concatenate(
                [q_t[j * HEAD_DIM:(j + 1) * HEAD_DIM, :], ones_row, jnp.zeros((HEAD_DIM - 8, t), F32)],
                axis=0).astype(BF16)
        dot_scr[...] = jnp.transpose(do_ref[...].astype(F32)).astype(BF16)
        prod = do_ref[...].astype(F32) * o_ref[...].astype(F32)
        for j in range(2):
            sl = slice(j * HEAD_DIM, (j + 1) * HEAD_DIM)
            delta_scr[:, sl] = jnp.broadcast_to(jnp.sum(prod[:, sl], axis=1, keepdims=True), (t, HEAD_DIM))
        for kj in range(nb):
            r0, r1 = kj * tb_, (kj + 1) * tb_
            k_blk = k_ref[r0:r1, :]
            v_blk = v_ref[r0:r1, :]
            k_pair = k_blk.astype(F32)
            for j in range(2):
                sl = slice(j * HEAD_DIM, (j + 1) * HEAD_DIM)
                one = slice(j * HEAD_DIM, j * HEAD_DIM + 1)
                own = (lane < HEAD_DIM) == (j == 0)
                k_head = k_pair if j == 0 else pltpu.roll(k_pair, HEAD_DIM, 1)
                k_ones = jnp.where(lane < HEAD_DIM, k_head, jnp.where(lane == HEAD_DIM, 1.0, 0.0)).astype(BF16)
                s = (_dot(q_ref[r0:t, :], jnp.where(own, k_blk, jnp.zeros_like(k_blk)), NT_DIMS) * scale2
                     + c_ref[j:j + 1, r0:r1] * LOG2E)
                head = jnp.where(causal, s[0:tb_, :], NEG)
                s = head if kj == nb - 1 else jnp.concatenate([head, s[tb_:, :]], axis=0)
                p = jnp.exp2(s - lse_ref[r0:t, one])
                dp = _dot(do_ref[r0:t, :], jnp.where(own, v_blk, jnp.zeros_like(v_blk)), NT_DIMS)
                ds = p * (dp - delta_scr[r0:t, one])
                dsb = ds.astype(BF16)
                dvt_scr[sl, r0:r1] = _dot(dot_scr[sl, r0:t], p.astype(BF16))
                dkt_scrs[j][:, r0:r1] = _dot(qt_scrs[j][:, r0:t], dsb)
                dq_scrs[j][r0:t, :] += _dot(dsb, k_ones)
        dv_ref[...] = jnp.transpose(dvt_scr[...]).astype(BF16)
        for j in range(2):
            sl = slice(j * HEAD_DIM, (j + 1) * HEAD_DIM)
            acc = dq_scrs[j][...]
            dkt = dkt_scrs[j][...]
            dq_ref[:, sl] = (acc[:, 0:HEAD_DIM] * scale).astype(BF16)
            dk_ref[:, sl] = (jnp.transpose(dkt)[:, 0:HEAD_DIM] * scale).astype(BF16)
            dc_ref[j:j + 1, :] = jnp.transpose(acc)[HEAD_DIM:HEAD_DIM + 1, :] - dkt[HEAD_DIM:HEAD_DIM + 1, :]

    blk = lambda off: pl.BlockSpec((t, 128), lambda b, hp: (b, off + hp))
    cblk = pl.BlockSpec((None, None, 8, t), lambda b, hp: (b, hp, 0, 0))
    return pl.pallas_call(
        body, name="fox_attn_bwd", grid=(bl, 8),
        in_specs=[blk(0), blk(8), blk(16), blk(0), blk(0), blk(0), cblk, ANY],
        out_specs=[blk(0), blk(0), blk(0), cblk],
        out_shape=[jax.ShapeDtypeStruct((n, ATT_WIDTH), BF16)] * 3 + [jax.ShapeDtypeStruct((bl, 8, 8, t), F32)],
        scratch_shapes=[pltpu.VMEM((t, 128), F32), pltpu.VMEM((t, 128), F32), pltpu.VMEM((t, 128), F32),
                        pltpu.VMEM((128, t), BF16), pltpu.VMEM((128, t), BF16), pltpu.VMEM((128, t), BF16),
                        pltpu.VMEM((128, t), F32), pltpu.VMEM((128, t), F32), pltpu.VMEM((128, t), F32)],
        compiler_params=_cparams(("parallel", "parallel")),
    )(qkv, qkv, qkv, do, o, lse, negc, after)


def _adamw(w, g, m, v, *, name):
    lead = w.ndim == 3
    r, c = w.shape[-2:]
    tr = _pick(r, (256, IN_SHARD // 3, 128, 64, 32, 16, 8))
    bc1 = 1.0 - ADAM_B1 ** ADAM_STEP
    bc2 = 1.0 - ADAM_B2 ** ADAM_STEP

    def body(w_ref, g_ref, m_ref, v_ref, d_ref, nm_ref, nv_ref):
        gv = g_ref[...]
        mn = ADAM_B1 * m_ref[...] + (1.0 - ADAM_B1) * gv
        vn = ADAM_B2 * v_ref[...] + (1.0 - ADAM_B2) * (gv * gv)
        m_hat = mn / bc1
        v_hat = vn / bc2
        d_ref[...] = -ADAM_LR * (m_hat / (jnp.sqrt(v_hat) + ADAM_EPS) + ADAM_WD * w_ref[...])
        nm_ref[...] = mn
        nv_ref[...] = vn

    flat = pl.BlockSpec((tr, c), lambda i: (i, 0))
    blk = pl.BlockSpec((None, tr, c), lambda i: (0, i, 0)) if lead else flat
    return pl.pallas_call(
        body, name=name, grid=(r // tr,), in_specs=[blk, flat, blk, blk], out_specs=[blk] * 3,
        out_shape=[jax.ShapeDtypeStruct(w.shape, F32)] * 3,
        compiler_params=_cparams(("parallel",)),
    )(w, g, m, v)


def _sum_leading(parts, *, name, out_dtype=F32):
    k, r, c = parts.shape
    tr = _pick(r, (512, 256, 128, 96, 64, 32, 16, 8))

    def body(p_ref, o_ref):
        acc = p_ref[0].astype(F32)
        for i in range(1, k):
            acc = acc + p_ref[i].astype(F32)
        o_ref[...] = acc.astype(out_dtype)

    return pl.pallas_call(
        body, name=name, grid=(r // tr,),
        in_specs=[pl.BlockSpec((k, tr, c), lambda i: (0, i, 0))],
        out_specs=pl.BlockSpec((tr, c), lambda i: (i, 0)),
        out_shape=jax.ShapeDtypeStruct((r, c), out_dtype),
        compiler_params=_cparams(("parallel",)),
    )(parts)


def _add_pair(a, b, *, name):
    k, r, c = a.shape
    tr = _pick(r, (512, 256, 128))

    def body(a_ref, b_ref, o_ref):
        o_ref[...] = (a_ref[...].astype(F32) + b_ref[...].astype(F32)).astype(BF16)

    blk = pl.BlockSpec((None, tr, c), lambda j, i: (j, i, 0))
    return pl.pallas_call(
        body, name=name, grid=(k, r // tr), in_specs=[blk, blk], out_specs=blk,
        out_shape=jax.ShapeDtypeStruct((k, r, c), BF16),
        compiler_params=_cparams(("parallel", "parallel")),
    )(a, b)


ANY = pl.BlockSpec(memory_space=pl.ANY)


def _chip_peers(x, y):
    return [(1 - x, y, 2 * (1 - x) + y), (x, 1 - y, 2 * x + 1 - y), (1 - x, 1 - y, 2 * (1 - x) + 1 - y)]


def _gather_weights(blob, *, name):
    rows, cols = blob.shape
    half_rows = rows // 2

    def body(b_ref, o_ref, send_sems, recv_sems):
        x, y, c = lax.axis_index("x"), lax.axis_index("y"), lax.axis_index("c")
        me = 2 * x + y
        sibling = (x, y, 1 - c)
        peers = _chip_peers(x, y)

        def half(chip, hc):
            return o_ref.at[chip, pl.ds(hc * half_rows, half_rows), :]

        def copy(k, src, chip, hc, to):
            return pltpu.make_async_remote_copy(src_ref=src, dst_ref=half(chip, hc), send_sem=send_sems.at[k],
                                                recv_sem=recv_sems.at[k], device_id=to, device_id_type=MESH)

        my_half = b_ref.at[pl.ds(c * half_rows, half_rows), :]
        first = [copy(k, my_half, me, c, (px, py, c)) for k, (px, py, _) in enumerate(peers)]
        for cp in first:
            cp.start()
        passed = [copy(3 + k, half(pc, c), pc, c, sibling) for k, (_, _, pc) in enumerate(peers)]
        for k, (px, py, pc) in enumerate(peers):
            copy(k, my_half, pc, c, (px, py, c)).wait_recv()
            passed[k].start()
        for k, (_, _, pc) in enumerate(peers):
            copy(3 + k, half(pc, 1 - c), pc, 1 - c, sibling).wait_recv()
        for cp in first + passed:
            cp.wait_send()

    return pl.pallas_call(
        body, name=name, in_specs=[ANY], out_specs=ANY,
        out_shape=jax.ShapeDtypeStruct((N_CHIPS, rows, cols), BF16),
        scratch_shapes=[pltpu.SemaphoreType.DMA((6,)), pltpu.SemaphoreType.DMA((6,))],
    )(blob)


def _swap_halves(g, *, name):
    _, rows, cols = g.shape
    half_rows = rows // 2

    def body(g_ref, o_ref, send_sem, recv_sem):
        x, y, c = lax.axis_index("x"), lax.axis_index("y"), lax.axis_index("c")
        cp = pltpu.make_async_remote_copy(
            src_ref=g_ref.at[:, pl.ds((1 - c) * half_rows, half_rows), :], dst_ref=o_ref,
            send_sem=send_sem, recv_sem=recv_sem, device_id=(x, y, 1 - c), device_id_type=MESH)
        cp.start()
        cp.wait()

    return pl.pallas_call(
        body, name=name, in_specs=[ANY], out_specs=ANY,
        out_shape=jax.ShapeDtypeStruct((N_CHIPS, half_rows, cols), BF16),
        scratch_shapes=[pltpu.SemaphoreType.DMA, pltpu.SemaphoreType.DMA],
    )(g)


HBM_SPEC = pl.BlockSpec(memory_space=pltpu.HBM)
SEM_SPEC = pl.BlockSpec(memory_space=pltpu.SEMAPHORE)
SPLIT_EFFECT = pltpu.SideEffectType.DATAFLOW_SIDE_EFFECTING


def _gather_peers_copies(b_ref, land_ref, send_sems, recv_sems, sending):
    x, y, c = lax.axis_index("x"), lax.axis_index("y"), lax.axis_index("c")
    me = 2 * x + y
    half_rows = b_ref.shape[0] // 2
    src = b_ref.at[pl.ds(c * half_rows, half_rows), :]
    return [pltpu.make_async_remote_copy(
        src_ref=src, dst_ref=land_ref.at[me if sending else pc, pl.ds(c * half_rows, half_rows), :],
        send_sem=send_sems.at[k], recv_sem=recv_sems.at[k], device_id=(px, py, c), device_id_type=MESH)
        for k, (px, py, pc) in enumerate(_chip_peers(x, y))]


def _gather_start(blob, after, *, name):
    shape = (N_CHIPS,) + blob.shape

    def body(b_ref, land_ref, after_ref, send_sems, recv_sems, b_thru, land_thru, token):
        for cp in _gather_peers_copies(b_ref, land_ref, send_sems, recv_sems, True):
            cp.start()
        token[...] = jnp.zeros_like(token)

    return pl.pallas_call(
        body, name=name,
        out_shape=(pltpu.SemaphoreType.DMA((3,)), pltpu.SemaphoreType.DMA((3,)), pltpu.HBM(blob.shape, blob.dtype),
                   pltpu.HBM(shape, blob.dtype), jax.ShapeDtypeStruct((8, 128), F32)),
        in_specs=(HBM_SPEC, HBM_SPEC, ANY),
        out_specs=(SEM_SPEC, SEM_SPEC, HBM_SPEC, HBM_SPEC, pl.BlockSpec(memory_space=pltpu.VMEM)),
        input_output_aliases={0: 2, 1: 3},
        compiler_params=pltpu.CompilerParams(has_side_effects=SPLIT_EFFECT),
    )(pltpu.with_memory_space_constraint(blob, pltpu.HBM),
      pltpu.with_memory_space_constraint(lax.empty(shape, blob.dtype), pltpu.HBM), after)


def _gather_wait(send_sems, recv_sems, b_thru, land_thru, after, *, name):
    def body(b_ref, land_ref, send_sems, recv_sems, after_ref, b_dead, got_ref):
        for cp in _gather_peers_copies(b_ref, land_ref, send_sems, recv_sems, False):
            cp.wait_send()
            cp.wait_recv()

    return pl.pallas_call(
        body, name=name,
        out_shape=(pltpu.HBM(b_thru.shape, b_thru.dtype), pltpu.HBM(land_thru.shape, land_thru.dtype)),
        in_specs=(HBM_SPEC, HBM_SPEC, SEM_SPEC, SEM_SPEC, ANY), out_specs=(HBM_SPEC, HBM_SPEC),
        input_output_aliases={0: 0, 1: 1},
        compiler_params=pltpu.CompilerParams(has_side_effects=SPLIT_EFFECT),
    )(b_thru, land_thru, send_sems, recv_sems, after)


def _gather_forward(land, *, name):
    half_rows = land.shape[1] // 2

    def body(l_ref, o_ref, send_sems, recv_sems):
        x, y, c = lax.axis_index("x"), lax.axis_index("y"), lax.axis_index("c")
        cps = []
        for k, (_, _, pc) in enumerate(_chip_peers(x, y)):
            mine = pl.ds(c * half_rows, half_rows)
            cps.append(pltpu.make_async_remote_copy(
                src_ref=l_ref.at[pc, mine, :], dst_ref=o_ref.at[pc, mine, :], send_sem=send_sems.at[k],
                recv_sem=recv_sems.at[k], device_id=(x, y, 1 - c), device_id_type=MESH))
        for cp in cps:
            cp.start()
        for k, (_, _, pc) in enumerate(_chip_peers(x, y)):
            theirs = pl.ds((1 - c) * half_rows, half_rows)
            pltpu.make_async_remote_copy(
                src_ref=l_ref.at[pc, theirs, :], dst_ref=o_ref.at[pc, theirs, :], send_sem=send_sems.at[k],
                recv_sem=recv_sems.at[k], device_id=(x, y, 1 - c), device_id_type=MESH).wait_recv()
        for cp in cps:
            cp.wait_send()

    return pl.pallas_call(
        body, name=name, in_specs=[ANY], out_specs=ANY, input_output_aliases={0: 0},
        out_shape=jax.ShapeDtypeStruct(land.shape, land.dtype),
        scratch_shapes=[pltpu.SemaphoreType.DMA((3,)), pltpu.SemaphoreType.DMA((3,))],
    )(land)


def _exchange_peers_copies(p_ref, land_ref, send_sems, recv_sems, sending):
    x, y, c = lax.axis_index("x"), lax.axis_index("y"), lax.axis_index("c")
    me = 2 * x + y
    return [pltpu.make_async_remote_copy(src_ref=p_ref.at[pc], dst_ref=land_ref.at[me if sending else pc],
                                         send_sem=send_sems.at[k], recv_sem=recv_sems.at[k],
                                         device_id=(px, py, c), device_id_type=MESH)
            for k, (px, py, pc) in enumerate(_chip_peers(x, y))]


def _exchange_start(p, *, name):
    def body(p_ref, land_ref, send_sems, recv_sems, p_thru, land_thru, token):
        for cp in _exchange_peers_copies(p_ref, land_ref, send_sems, recv_sems, True):
            cp.start()
        token[...] = jnp.zeros_like(token)

    return pl.pallas_call(
        body, name=name,
        out_shape=(pltpu.SemaphoreType.DMA((3,)), pltpu.SemaphoreType.DMA((3,)), pltpu.HBM(p.shape, p.dtype),
                   pltpu.HBM(p.shape, p.dtype), jax.ShapeDtypeStruct((8, 128), F32)),
        in_specs=(HBM_SPEC, HBM_SPEC),
        out_specs=(SEM_SPEC, SEM_SPEC, HBM_SPEC, HBM_SPEC, pl.BlockSpec(memory_space=pltpu.VMEM)),
        input_output_aliases={0: 2, 1: 3},
        compiler_params=pltpu.CompilerParams(has_side_effects=SPLIT_EFFECT),
    )(pltpu.with_memory_space_constraint(p, pltpu.HBM),
      pltpu.with_memory_space_constraint(lax.empty(p.shape, p.dtype), pltpu.HBM))


def _exchange_wait(send_sems, recv_sems, p_thru, land_thru, after, *, name):
    def body(p_ref, land_ref, send_sems, recv_sems, after_ref, p_dead, got_ref):
        for cp in _exchange_peers_copies(p_ref, land_ref, send_sems, recv_sems, False):
            cp.wait_send()
            cp.wait_recv()

    return pl.pallas_call(
        body, name=name,
        out_shape=(pltpu.HBM(p_thru.shape, p_thru.dtype), pltpu.HBM(p_thru.shape, p_thru.dtype)),
        in_specs=(HBM_SPEC, HBM_SPEC, SEM_SPEC, SEM_SPEC, ANY), out_specs=(HBM_SPEC, HBM_SPEC),
        input_output_aliases={0: 0, 1: 1},
        compiler_params=pltpu.CompilerParams(has_side_effects=SPLIT_EFFECT),
    )(p_thru, land_thru, send_sems, recv_sems, after)


def _sum_parts(parts, own, *, name):
    k, r, c = parts.shape
    tr = _pick(r, (512, 256, 128))

    def body(p_ref, own_ref, o_ref):
        me = 2 * lax.axis_index("x") + lax.axis_index("y")
        acc = jnp.zeros((tr, c), F32)
        for i in range(k):
            acc = acc + jnp.where(me == i, own_ref[i], p_ref[i]).astype(F32)
        o_ref[...] = acc

    blk = pl.BlockSpec((k, tr, c), lambda i: (0, i, 0))
    return pl.pallas_call(
        body, name=name, grid=(r // tr,), in_specs=[blk, blk],
        out_specs=pl.BlockSpec((tr, c), lambda i: (i, 0)),
        out_shape=jax.ShapeDtypeStruct((r, c), F32),
        compiler_params=_cparams(("parallel",)),
    )(parts, own)


def _join_halves(gh, *, name):
    def body(g_ref, o_ref, send_sem, recv_sem):
        x, y, c = lax.axis_index("x"), lax.axis_index("y"), lax.axis_index("c")
        cp = pltpu.make_async_remote_copy(src_ref=g_ref, dst_ref=o_ref, send_sem=send_sem, recv_sem=recv_sem,
                                          device_id=(x, y, 1 - c), device_id_type=MESH)
        cp.start()
        cp.wait()

    other = pl.pallas_call(
        body, name=name, in_specs=[ANY], out_specs=ANY,
        out_shape=jax.ShapeDtypeStruct(gh.shape, F32),
        scratch_shapes=[pltpu.SemaphoreType.DMA, pltpu.SemaphoreType.DMA],
    )(gh)
    south = lax.axis_index("c") == 0
    return jnp.concatenate([jnp.where(south, gh, other), jnp.where(south, other, gh)], axis=0)


def _gather_small(s, *, name):
    rows = s.shape[0]

    def body(s_ref, o_ref, send_sems, recv_sems, local_sem):
        x, y, c = lax.axis_index("x"), lax.axis_index("y"), lax.axis_index("c")
        me = 4 * x + 2 * y + c
        mine = pltpu.make_async_copy(s_ref, o_ref.at[me], local_sem)
        mine.start()
        peers = []
        for k in range(1, 8):
            peers.append((1 - x if k & 4 else x, 1 - y if k & 2 else y, 1 - c if k & 1 else c))
        cps = [pltpu.make_async_remote_copy(src_ref=s_ref, dst_ref=o_ref.at[me], send_sem=send_sems.at[k],
                                            recv_sem=recv_sems.at[k], device_id=p, device_id_type=MESH)
               for k, p in enumerate(peers)]
        for cp in cps:
            cp.start()
        for k, (px, py, pc) in enumerate(peers):
            pltpu.make_async_remote_copy(src_ref=s_ref, dst_ref=o_ref.at[4 * px + 2 * py + pc],
                                         send_sem=send_sems.at[k], recv_sem=recv_sems.at[k],
                                         device_id=(px, py, pc), device_id_type=MESH).wait_recv()
        for cp in cps:
            cp.wait_send()
        mine.wait()

    return pl.pallas_call(
        body, name=name, in_specs=[ANY], out_specs=ANY,
        out_shape=jax.ShapeDtypeStruct((8, rows, 128), F32),
        scratch_shapes=[pltpu.SemaphoreType.DMA((7,)), pltpu.SemaphoreType.DMA((7,)), pltpu.SemaphoreType.DMA],
    )(s)


IN_SHARD = IN_WIDTH // N_CHIPS
IN_SHARD_PAD = 1536
UP_ROWS, DOWN_ROWS, OUT_ROWS = 1024, 1024, 512
REST_ROWS = UP_ROWS + DOWN_ROWS + OUT_ROWS


def _pack_in(w_in_s):
    return jnp.pad(w_in_s, ((0, 0), (0, IN_SHARD_PAD - IN_SHARD))).astype(BF16)


def _pack_rest(w_out_s, w_up_s, w_down_s):
    return jnp.concatenate([w_up_s, w_down_s, w_out_s], axis=0).astype(BF16)


def _unpack_rest(blob):
    return (blob[UP_ROWS + DOWN_ROWS:], blob[0:UP_ROWS], blob[UP_ROWS:UP_ROWS + DOWN_ROWS])


def _with_own(gathered, own):
    me = 2 * lax.axis_index("x") + lax.axis_index("y")
    return [jnp.where(me == j, own, gathered[j]) for j in range(N_CHIPS)]


def _full_w_in(g_in, own):
    return jnp.concatenate([s[:, :IN_SHARD] for s in _with_own(g_in, own)], axis=1)


def _full_rest(g_rest, own):
    parts = [_unpack_rest(s) for s in _with_own(g_rest, own)]
    w_out = jnp.concatenate([p[0] for p in parts], axis=0)
    w_up = jnp.concatenate([p[1] for p in parts], axis=1)
    w_down = jnp.concatenate([p[2] for p in parts], axis=0)
    return w_out, w_up, w_down


def _split_w_in(w_in):
    z_xbc = w_in[:, 0:2560]
    dt = w_in[:, 2560:2576]
    qkv = w_in[:, 2576:5648]
    f = w_in[:, 5648:5664]
    pad = jnp.zeros((w_in.shape[0], PA_WIDTH - 2592), w_in.dtype)
    return jnp.concatenate([z_xbc, dt, f, pad], axis=1), qkv


def _merge_w_in(d_a, d_qkv):
    return jnp.concatenate([d_a[:, 0:2560], d_a[:, 2560:2576], d_qkv, d_a[:, 2576:2592]], axis=1)


def _local_step(x3, target3, w_in, rest_weights, norm_mix_w, conv_w, conv_b, dt_bias, a_log, d_skip,
                ssd_norm_w, f_bias, norm_mlp_w, norm_final_w, first_after=None, early_grads=None, late_grads=None):
    bl, t, d = x3.shape
    n = bl * t
    x = x3.reshape(n, d)
    target = target3.reshape(n, d)
    w_a, w_qkv = _split_w_in(w_in)
    nfw = norm_final_w.reshape(1, d)
    dskip_e = jnp.repeat(d_skip, HEAD_DIM, axis=1)
    nb = t // ATT_BLOCK

    r1, r2, kt = min(n, 1024), min(n, 512), min(n, 512)
    if first_after is None:
        first_after = jnp.zeros((8, 128), F32)
    h0, rstd0, proj_a = _norm_mm(x, norm_mix_w, w_a, first_after, name="norm_mix_proj_a", tm=r2)
    qkv = _mm(h0, w_qkv, name="proj_qkv", tiles=(r2, QKV_WIDTH, D_MODEL), out_dtype=BF16)
    bias128 = jnp.concatenate([dt_bias, f_bias, jnp.zeros((1, 96), F32)], axis=1)
    alog128 = jnp.concatenate([a_log, jnp.zeros((1, 112), F32)], axis=1)
    dt, gate_d, acum, ccum = _prep(proj_a, bias128, alog128, bl, t)
    acum_t = acum[:, 0:16].T
    negc = jnp.pad(-ccum.reshape(bl, t, 8, 2).transpose(0, 2, 3, 1), ((0, 0), (0, 0), (0, 6), (0, 0)))
    xc = _conv_fwd(proj_a, conv_w, conv_b, bl, t)
    y_ssd, y_pre, hprev = _ssd_fwd(xc, proj_a, dt, acum, acum_t, dskip_e, ssd_norm_w, bl, t)
    y_att, lse = _attn_fwd(qkv, negc, bl, t)
    w_out, w_up, w_down = rest_weights(y_att)
    wo_s, wo_a = w_out[:SSD_WIDTH], w_out[SSD_WIDTH:]
    h1, h1n, rstd1 = _mm_norm_fwd(y_ssd, wo_s, y_att, wo_a, x, norm_mlp_w, name="out_proj_norm_mlp", tm=r2)
    up = _mm(h1n, w_up, name="mlp_up", tiles=(r2, D_FF, D_MODEL), out_dtype=BF16)
    dh2, dh2b, loss, d_nfw = _mm_final(up, w_down, h1, nfw, target, name="mlp_down_final_norm_loss", tm=r2,
                                       a_act="relu2")

    dup = _mm(dh2b, w_down, name="mlp_down_bwd_act", tiles=(r2, D_FF, D_MODEL), tb=True, epi_up=up, out_dtype=BF16)
    rest_shape = (N_CHIPS, REST_ROWS, D_MODEL)
    gb_rest = _mm(up, dh2b, name="mlp_down_bwd_w", tiles=(DOWN_ROWS, D_MODEL, kt), ta=True, a_act="relu2",
                  out_dtype=BF16, into=(rest_shape, (None, DOWN_ROWS, D_MODEL), lambda i, j, k: (i, 1, 0), None))
    dh1, dh1b, d_nmlp = _mm_norm_bwd([(dup, w_up)], h1, rstd1, norm_mlp_w, dh2, name="mlp_up_bwd_act_norm_mlp",
                                     tm=r2)
    gb_rest = _mm(h1n, dup, name="mlp_up_bwd_w", tiles=(D_MODEL, UP_ROWS, kt), ta=True, out_dtype=BF16,
                  into=(rest_shape, (None, D_MODEL, UP_ROWS), lambda i, j, k: (j, 0, 0), gb_rest))
    dys, do = _mm_two_halves(dh1b, w_out, name="out_proj_bwd_act", tm=r1)
    out_block = (UP_ROWS + DOWN_ROWS) // OUT_ROWS
    for half, (y_half, tag) in enumerate(((y_ssd, "ssd"), (y_att, "att"))):
        gb_rest = _mm(y_half, dh1b, name="out_proj_bwd_w_" + tag, tiles=(2 * OUT_ROWS, D_MODEL, kt), ta=True,
                      out_dtype=BF16, into=(rest_shape, (2, OUT_ROWS, D_MODEL),
                                            functools.partial(lambda i, j, k, h: (h, out_block, 0), h=half),
                                            gb_rest))
    token = jnp.zeros((8, 128), F32) if early_grads is None else early_grads(gb_rest)
    dq, dk, dv, dcb = _attn_bwd(qkv, do, y_att, lse, negc, token, bl, t)
    dc = jnp.pad(dcb[:, :, 0:2, :].transpose(0, 3, 1, 2).reshape(n, 16), ((0, 0), (16, 96)))
    dxc, dpa, ddt_raw, d_snw, d_dsk, d_alog, d_dtb = _ssd_bwd(dys, xc, proj_a, y_pre, hprev, dt, gate_d, acum,
                                                             acum_t, alog128, dskip_e, ssd_norm_w, bl, t)
    dpa, d_conv_w, d_conv_b = _conv_bwd(dxc, proj_a, conv_w, conv_b, dpa, bl, t)
    dproj_a, d_fb = _fpost(dc, gate_d, ddt_raw, dpa, bl, t)
    dqkv = jnp.concatenate([dq, dk, dv], axis=1)
    d_w_a = _mm(h0, dproj_a, name="proj_a_bwd_w", tiles=(1024, 896, kt), ta=True, out_dtype=BF16)
    d_w_qkv = _mm(h0, dqkv, name="proj_qkv_bwd_w", tiles=(1024, 1024, kt), ta=True, out_dtype=BF16)
    d_w_in = _merge_w_in(d_w_a, d_w_qkv)
    late_token = None if late_grads is None else late_grads(d_w_in)
    dx, _, d_nmix = _mm_norm_bwd([(dproj_a, w_a), (dqkv, w_qkv)], x, rstd0, norm_mix_w, dh1,
                                 name="proj_bwd_act_norm_mix", tm=min(n, 256), after=late_token)

    grads = dict(norm_mix_w=d_nmix, w_in=d_w_in, conv_w=d_conv_w, conv_b=d_conv_b,
                 dt_bias=d_dtb, a_log=d_alog, d_skip=d_dsk, ssd_norm_w=d_snw, f_bias=d_fb, rest=gb_rest,
                 norm_mlp_w=d_nmlp, norm_final_w=d_nfw)
    return dx.reshape(bl, t, d), loss, grads


SMALL_ORDER = ("norm_mix_w", "conv_w", "conv_b", "dt_bias", "a_log", "d_skip", "ssd_norm_w", "f_bias",
               "norm_mlp_w", "norm_final_w")
SMALL_SIZES = (1024, 4 * CONV_CH, CONV_CH, 16, 16, 16, 1024, 16, 1024, 1024)


def _pack_small(vals, rows):
    flat = jnp.concatenate([v.reshape(-1).astype(F32) for v in vals])
    return jnp.pad(flat, (0, rows * 128 - flat.shape[0])).reshape(rows, 128)


def _unpack_small(packed, sizes):
    flat = packed.reshape(-1)
    out, o = [], 0
    for s in sizes:
        out.append(flat[o:o + s])
        o += s
    return out


def kernel(x, norm_mix_w, w_in, conv_w, conv_b, dt_bias, a_log, d_skip, ssd_norm_w, f_bias, w_out, norm_mlp_w, w_up, w_down, norm_final_w, loss_target, m_norm_mix_w, m_w_in, m_conv_w, m_conv_b, m_dt_bias, m_a_log, m_d_skip, m_ssd_norm_w, m_f_bias, m_w_out, m_norm_mlp_w, m_w_up, m_w_down, m_norm_final_w, v_norm_mix_w, v_w_in, v_conv_w, v_conv_b, v_dt_bias, v_a_log, v_d_skip, v_ssd_norm_w, v_f_bias, v_w_out, v_norm_mlp_w, v_w_up, v_w_down, v_norm_final_w):
    chip = 2 * lax.axis_index("x") + lax.axis_index("y")
    cw = CONV_CH // N_CHIPS

    own_in = _pack_in(w_in[0])
    own_rest = _pack_rest(w_out[0], w_up[0], w_down[0])
    g_in = _gather_weights(own_in, name="gather_w_in")
    w_in_f = _full_w_in(g_in, own_in)
    *rest_handles, rest_token = _gather_start(own_rest, g_in, name="gather_start_rest")

    def rest_weights(after):
        _, landed = _gather_wait(*rest_handles, after, name="gather_wait_rest")
        return _full_rest(_gather_forward(landed, name="gather_forward_rest"), own_rest)
    small_all = _gather_small(_pack_small([conv_w[0]], 16), name="gather_conv_w")
    conv_w_f = jnp.concatenate([small_all[2 * j].reshape(-1)[:4 * cw].reshape(4, cw) for j in range(N_CHIPS)], axis=1)

    c = lax.axis_index("c")

    def chip_partial(gb, tag):
        half_rows = gb.shape[1] // 2
        from_sibling = _swap_halves(gb, name="grad_swap_halves_" + tag)
        my_half = lax.dynamic_slice_in_dim(gb, c * half_rows, half_rows, axis=1)
        return _add_pair(my_half, from_sibling, name="grad_add_sibling_" + tag)

    in_flight = {}

    def early_grads(gb_rest):
        part = chip_partial(gb_rest, "rest")
        *handles, token = _exchange_start(part, name="grad_exchange_start_rest")
        in_flight["rest"] = handles
        return token

    def late_grads(d_w_in):
        gb_in = jnp.stack([_pack_in(d_w_in[:, j * IN_SHARD:(j + 1) * IN_SHARD]) for j in range(N_CHIPS)])
        *handles, token = _exchange_start(chip_partial(gb_in, "in"), name="grad_exchange_start_in")
        in_flight["in"] = handles
        return token

    dx, loss_part, g = _local_step(x, loss_target, w_in_f, rest_weights, norm_mix_w, conv_w_f,
                                   conv_b, dt_bias, a_log, d_skip, ssd_norm_w, f_bias, norm_mlp_w, norm_final_w,
                                   first_after=rest_token, early_grads=early_grads, late_grads=late_grads)

    send_sems, recv_sems, part_rest, land_rest = in_flight["rest"]
    part_rest, parts_rest = _exchange_wait(send_sems, recv_sems, part_rest, land_rest, dx,
                                           name="grad_exchange_wait_rest")
    g_rest_half = _sum_parts(parts_rest, part_rest, name="grad_sum_chips_rest")
    g_w_out, g_w_up, g_w_down = _unpack_rest(_join_halves(g_rest_half, name="grad_join_halves_rest"))

    part_in, parts_in = _exchange_wait(*in_flight["in"], dx, name="grad_exchange_wait_in")
    g_in_half = _sum_parts(parts_in, part_in, name="grad_sum_chips_in")
    g_w_in = _join_halves(g_in_half, name="grad_join_halves_in")[:, :IN_SHARD]

    small_vals = [g[k] for k in SMALL_ORDER] + [loss_part[:, 0:1]]
    small_sum = _sum_leading(_gather_small(_pack_small(small_vals, SMALL_ROWS), name="gather_small_grads"), name="small_sum")
    sg = dict(zip(SMALL_ORDER + ("loss",), _unpack_small(small_sum, SMALL_SIZES + (1,))))
    loss = sg["loss"].reshape(())
    g_conv_full = sg["conv_w"].reshape(4, CONV_CH)
    g_conv = lax.dynamic_slice_in_dim(g_conv_full, chip * cw, cw, axis=1)

    grads = dict(norm_mix_w=sg["norm_mix_w"].reshape(1, -1), w_in=g_w_in[None], conv_w=g_conv[None],
                 conv_b=sg["conv_b"].reshape(1, -1), dt_bias=sg["dt_bias"].reshape(1, -1),
                 a_log=sg["a_log"].reshape(1, -1), d_skip=sg["d_skip"].reshape(1, -1),
                 ssd_norm_w=sg["ssd_norm_w"].reshape(1, -1), f_bias=sg["f_bias"].reshape(1, -1), w_out=g_w_out[None],
                 norm_mlp_w=sg["norm_mlp_w"].reshape(1, -1), w_up=g_w_up[None], w_down=g_w_down[None],
                 norm_final_w=sg["norm_final_w"])
    weights = dict(norm_mix_w=norm_mix_w, w_in=w_in, conv_w=conv_w, conv_b=conv_b, dt_bias=dt_bias, a_log=a_log,
                   d_skip=d_skip, ssd_norm_w=ssd_norm_w, f_bias=f_bias, w_out=w_out, norm_mlp_w=norm_mlp_w,
                   w_up=w_up, w_down=w_down, norm_final_w=norm_final_w)
    ms = dict(norm_mix_w=m_norm_mix_w, w_in=m_w_in, conv_w=m_conv_w, conv_b=m_conv_b, dt_bias=m_dt_bias,
              a_log=m_a_log, d_skip=m_d_skip, ssd_norm_w=m_ssd_norm_w, f_bias=m_f_bias, w_out=m_w_out,
              norm_mlp_w=m_norm_mlp_w, w_up=m_w_up, w_down=m_w_down, norm_final_w=m_norm_final_w)
    vs = dict(norm_mix_w=v_norm_mix_w, w_in=v_w_in, conv_w=v_conv_w, conv_b=v_conv_b, dt_bias=v_dt_bias,
              a_log=v_a_log, d_skip=v_d_skip, ssd_norm_w=v_ssd_norm_w, f_bias=v_f_bias, w_out=v_w_out,
              norm_mlp_w=v_norm_mlp_w, w_up=v_w_up, w_down=v_w_down, norm_final_w=v_norm_final_w)
    names = list(weights)
    big = ("w_in", "w_out", "w_up", "w_down")
    delta, new_m, new_v = {}, {}, {}
    for k, g2 in zip(big[1:], (g_w_out, g_w_up, g_w_down)):
        delta[k], new_m[k], new_v[k] = _adamw(weights[k], g2, ms[k], vs[k], name="adamw_" + k)
    g_in_t = g_w_in.T
    outs_t = _adamw(w_in[0].T, g_in_t, m_w_in[0].T, v_w_in[0].T, name="adamw_w_in")
    delta["w_in"], new_m["w_in"], new_v["w_in"] = [o.T[None] for o in outs_t]
    grads["w_in"] = g_in_t.T[None]
    smalls = [k for k in names if k not in big]
    sizes = [math.prod(weights[k].shape) for k in smalls]
    rows = -(-sum(sizes) // 1024) * 8
    packs = [_pack_small([d[k] for k in smalls], rows) for d in (weights, grads, ms, vs)]
    outs = _adamw(*packs, name="adamw_small")
    for o, dst in zip(outs, (delta, new_m, new_v)):
        for k, val in zip(smalls, _unpack_small(o, sizes)):
            dst[k] = val.reshape(weights[k].shape)
    return (loss, dx, *[grads[k] for k in names], *[delta[k] for k in names], *[new_m[k] for k in names],
            *[new_v[k] for k in names])
```

```python
import functools
import math

import jax
import jax.numpy as jnp
from jax import lax
from jax.experimental import pallas as pl
from jax.experimental.pallas import tpu as pltpu

F32 = jnp.float32
BF16 = jnp.bfloat16
HIGHEST = lax.Precision.HIGHEST
MESH = pl.DeviceIdType.MESH

D_MODEL = 1024
SSD_HEADS = 16
HEAD_DIM = 64
SSD_WIDTH = 1024
SSD_STATE = 128
CONV_CH = 1536
CHUNK = 128
ATT_WIDTH = 1024
EPS = 1e-5
IN_WIDTH = 5664
PA_WIDTH = 2688
QKV_WIDTH = 3072
D_FF = 4096
ATT_BLOCK = 256
NEG = -1e30
LOG2E = 1.4426950408889634
VMEM_LIMIT = 48 * 1024 * 1024

ADAM_LR = 0.001
ADAM_B1 = 0.9
ADAM_B2 = 0.999
ADAM_EPS = 1e-08
ADAM_WD = 0.01
ADAM_STEP = 10

N_CHIPS = 4
SMALL_ROWS = 96


def _cparams(sem):
    return pltpu.CompilerParams(dimension_semantics=sem, vmem_limit_bytes=VMEM_LIMIT)


def _pick(n, cands):
    for c in cands:
        if n % c == 0:
            return c
    return n


MM_CHUNK = 512


def _mm(a, b, *, name, tiles, ta=False, tb=False, out_dtype=F32, res=None, a_act=None, epi_up=None, after=None,
        into=None):
    n_unread = (after is not None) + (into is not None and into[3] is not None)
    if ta:
        K, M = a.shape
    else:
        M, K = a.shape
    if tb:
        N, K2 = b.shape
    else:
        K2, N = b.shape
    assert K == K2, (a.shape, b.shape)
    tm, tn, tk = tiles
    assert M % tm == 0 and N % tn == 0 and K % tk == 0, (name, M, N, K, tiles)
    nk = K // tk
    dn = (((0 if ta else 1,), (1 if tb else 0,)), ((), ()))
    has_res = res is not None
    has_up = epi_up is not None
    cn = _pick(tn, (MM_CHUNK, 384, 256, 128))

    def prologue(av):
        if a_act == "relu2":
            r = jnp.maximum(av.astype(F32), 0.0)
            av = r * r
        return av.astype(BF16)

    def epilogue(out, res_v, up_v):
        if has_res:
            out = out + res_v.astype(F32)
        if has_up:
            out = out * (2.0 * jnp.maximum(up_v.astype(F32), 0.0))
        return out.astype(out_dtype)

    def body(*refs):
        a_ref, b_ref = refs[0], refs[1]
        i = 2
        res_ref = up_ref = None
        if has_res:
            res_ref = refs[i]
            i += 1
        if has_up:
            up_ref = refs[i]
            i += 1
        i += n_unread
        o_ref = refs[i]
        if nk == 1:
            av = prologue(a_ref[...])
            for c in range(tn // cn):
                cs = slice(c * cn, (c + 1) * cn)
                bv = (b_ref[cs, :] if tb else b_ref[:, cs]).astype(BF16)
                out = lax.dot_general(av, bv, dn, preferred_element_type=F32)
                o_ref[:, cs] = epilogue(out, res_ref[:, cs] if has_res else None, up_ref[:, cs] if has_up else None)
            return
        acc_ref = refs[i + 1]
        k = pl.program_id(2)

        @pl.when(k == 0)
        def _():
            acc_ref[...] = jnp.zeros_like(acc_ref)

        acc_ref[...] += lax.dot_general(prologue(a_ref[...]), b_ref[...].astype(BF16), dn,
                                        preferred_element_type=F32)

        @pl.when(k == nk - 1)
        def _():
            out = epilogue(acc_ref[...], res_ref[...] if has_res else None, up_ref[...] if has_up else None)
            o_ref[...] = out.reshape(o_ref.shape)

    a_spec = pl.BlockSpec((tk, tm), lambda i, j, k: (k, i)) if ta else pl.BlockSpec((tm, tk), lambda i, j, k: (i, k))
    b_spec = pl.BlockSpec((tn, tk), lambda i, j, k: (j, k)) if tb else pl.BlockSpec((tk, tn), lambda i, j, k: (k, j))
    o_spec = pl.BlockSpec((tm, tn), lambda i, j, k: (i, j))
    ins, specs = [a, b], [a_spec, b_spec]
    if has_res:
        ins.append(res)
        specs.append(o_spec)
    if has_up:
        ins.append(epi_up)
        specs.append(o_spec)
    if after is not None:
        ins.append(after)
        specs.append(pl.BlockSpec(memory_space=pl.ANY))
    out_shape, out_spec, aliases = jax.ShapeDtypeStruct((M, N), out_dtype), o_spec, {}
    if into is not None:
        shape, block, index, buf = into
        out_shape, out_spec = jax.ShapeDtypeStruct(shape, out_dtype), pl.BlockSpec(block, index)
        if buf is not None:
            aliases = {len(ins): 0}
            ins.append(buf)
            specs.append(pl.BlockSpec(memory_space=pl.ANY))
    return pl.pallas_call(
        body, name=name, grid=(M // tm, N // tn, nk),
        in_specs=specs, out_specs=out_spec, out_shape=out_shape, input_output_aliases=aliases,
        scratch_shapes=[] if nk == 1 else [pltpu.VMEM((tm, tn), F32)],
        compiler_params=_cparams(("parallel", "parallel", "arbitrary")),
    )(*ins)


def _rows_product(a_ref, b_ref, tb, a_act):
    av = a_ref[...]
    if a_act == "relu2":
        r = jnp.maximum(av.astype(F32), 0.0)
        av = r * r
    dn = (((1,), (1 if tb else 0,)), ((), ()))
    return lax.dot_general(av.astype(BF16), b_ref[...].astype(BF16), dn, preferred_element_type=F32)


def _norm_mm(x, w, b, after, *, name, tm):
    m, d = x.shape
    n = b.shape[1]
    cn = _pick(n, (MM_CHUNK, 384, 256, 128))

    def body(x_ref, w_ref, b_ref, after_ref, h_ref, r_ref, o_ref):
        xv = x_ref[...]
        rstd = lax.rsqrt(jnp.mean(xv * xv, axis=1, keepdims=True) + EPS)
        hv = (xv * rstd * w_ref[...]).astype(BF16)
        h_ref[...] = hv
        r_ref[...] = rstd
        for c in range(n // cn):
            cs = slice(c * cn, (c + 1) * cn)
            o_ref[:, cs] = jnp.dot(hv, b_ref[:, cs].astype(BF16), preferred_element_type=F32)

    row = pl.BlockSpec((tm, d), lambda i: (i, 0))
    return pl.pallas_call(
        body, name=name, grid=(m // tm,),
        in_specs=[row, pl.BlockSpec((1, d), lambda i: (0, 0)), pl.BlockSpec((d, n), lambda i: (0, 0)),
                  pl.BlockSpec(memory_space=pl.ANY)],
        out_specs=[row, pl.BlockSpec((tm, 1), lambda i: (i, 0)), pl.BlockSpec((tm, n), lambda i: (i, 0))],
        out_shape=[jax.ShapeDtypeStruct((m, d), BF16), jax.ShapeDtypeStruct((m, 1), F32),
                   jax.ShapeDtypeStruct((m, n), F32)],
        compiler_params=_cparams(("parallel",)),
    )(x, w, b, after)


def _mm_norm_fwd(a1, b1, a2, b2, res, w, *, name, tm):
    m, k1 = a1.shape
    k2 = a2.shape[1]
    d = b1.shape[1]

    def body(a1_ref, b1_ref, a2_ref, b2_ref, res_ref, w_ref, h_ref, y_ref, r_ref):
        hv = _rows_product(a1_ref, b1_ref, False, None) + _rows_product(a2_ref, b2_ref, False, None) + res_ref[...]
        rstd = lax.rsqrt(jnp.mean(hv * hv, axis=1, keepdims=True) + EPS)
        h_ref[...] = hv
        y_ref[...] = (hv * rstd * w_ref[...]).astype(BF16)
        r_ref[...] = rstd

    row = pl.BlockSpec((tm, d), lambda i: (i, 0))
    return pl.pallas_call(
        body, name=name, grid=(m // tm,),
        in_specs=[pl.BlockSpec((tm, k1), lambda i: (i, 0)), pl.BlockSpec((k1, d), lambda i: (0, 0)),
                  pl.BlockSpec((tm, k2), lambda i: (i, 0)), pl.BlockSpec((k2, d), lambda i: (0, 0)), row,
                  pl.BlockSpec((1, d), lambda i: (0, 0))],
        out_specs=[row, row, pl.BlockSpec((tm, 1), lambda i: (i, 0))],
        out_shape=[jax.ShapeDtypeStruct((m, d), F32), jax.ShapeDtypeStruct((m, d), BF16),
                   jax.ShapeDtypeStruct((m, 1), F32)],
        compiler_params=_cparams(("parallel",)),
    )(a1, b1, a2, b2, res, w)


def _mm_final(a, b, res, w, target, *, name, tm, a_act):
    m, k = a.shape
    d = b.shape[1]

    def body(a_ref, b_ref, res_ref, w_ref, t_ref, dh_ref, dhb_ref, loss_ref, dw_ref):
        @pl.when(pl.program_id(0) == 0)
        def _():
            loss_ref[...] = jnp.zeros_like(loss_ref)
            dw_ref[...] = jnp.zeros_like(dw_ref)

        hv = _rows_product(a_ref, b_ref, False, a_act) + res_ref[...]
        wv = w_ref[...]
        rstd = lax.rsqrt(jnp.mean(hv * hv, axis=1, keepdims=True) + EPS)
        xhat = hv * rstd
        err = xhat * wv - t_ref[...]
        loss_ref[...] += 0.5 * jnp.sum(jnp.mean(err * err, axis=1, keepdims=True), axis=0, keepdims=True)
        dy = err * (1.0 / d)
        gw = dy * wv
        dh = rstd * (gw - xhat * jnp.mean(gw * xhat, axis=1, keepdims=True))
        dh_ref[...] = dh
        dhb_ref[...] = dh.astype(BF16)
        dw_ref[...] += jnp.sum(dy * xhat, axis=0, keepdims=True)

    row = pl.BlockSpec((tm, d), lambda i: (i, 0))
    vec = pl.BlockSpec((1, d), lambda i: (0, 0))
    return pl.pallas_call(
        body, name=name, grid=(m // tm,),
        in_specs=[pl.BlockSpec((tm, k), lambda i: (i, 0)), pl.BlockSpec((k, d), lambda i: (0, 0)), row, vec, row],
        out_specs=[row, row, pl.BlockSpec((1, 128), lambda i: (0, 0)), vec],
        out_shape=[jax.ShapeDtypeStruct((m, d), F32), jax.ShapeDtypeStruct((m, d), BF16),
                   jax.ShapeDtypeStruct((1, 128), F32), jax.ShapeDtypeStruct((1, d), F32)],
        compiler_params=_cparams(("arbitrary",)),
    )(a, b, res, w, target)


def _mm_two_halves(a, b, *, name, tm):
    m, k = a.shape
    d = b.shape[0] // 2

    def body(a_ref, b_ref, lo_ref, hi_ref):
        av = a_ref[...].astype(BF16)
        lo_ref[...] = lax.dot_general(av, b_ref[0:d, :].astype(BF16), NT_DIMS, preferred_element_type=F32)
        hi_ref[...] = lax.dot_general(av, b_ref[d:2 * d, :].astype(BF16), NT_DIMS,
                                      preferred_element_type=F32).astype(BF16)

    row = pl.BlockSpec((tm, d), lambda i: (i, 0))
    return pl.pallas_call(
        body, name=name, grid=(m // tm,),
        in_specs=[pl.BlockSpec((tm, k), lambda i: (i, 0)), pl.BlockSpec((2 * d, k), lambda i: (0, 0))],
        out_specs=[row, row],
        out_shape=[jax.ShapeDtypeStruct((m, d), F32), jax.ShapeDtypeStruct((m, d), BF16)],
        compiler_params=_cparams(("parallel",)),
    )(a, b)


def _mm_norm_bwd(pairs, x, rstd, w, dres, *, name, tm, after=None):
    m = pairs[0][0].shape[0]
    d = pairs[0][1].shape[0]
    n_pairs = len(pairs)

    def body(*refs):
        i = 2 * n_pairs
        x_ref, r_ref, w_ref, d_ref = refs[i:i + 4]
        dx_ref, dxb_ref, dw_ref = refs[-3:]

        @pl.when(pl.program_id(0) == 0)
        def _():
            dw_ref[...] = jnp.zeros_like(dw_ref)

        g = _rows_product(refs[0], refs[1], True, None)
        for p in range(1, n_pairs):
            g = g + _rows_product(refs[2 * p], refs[2 * p + 1], True, None)
        r = r_ref[...]
        xhat = x_ref[...] * r
        gw = g * w_ref[...]
        dx = d_ref[...] + r * (gw - xhat * jnp.mean(gw * xhat, axis=1, keepdims=True))
        dx_ref[...] = dx
        dxb_ref[...] = dx.astype(BF16)
        dw_ref[...] += jnp.sum(g * xhat, axis=0, keepdims=True)

    row = pl.BlockSpec((tm, d), lambda i: (i, 0))
    vec = pl.BlockSpec((1, d), lambda i: (0, 0))
    ins, specs = [], []
    for a, b in pairs:
        k = a.shape[1]
        ins += [a, b]
        specs += [pl.BlockSpec((tm, k), lambda i: (i, 0)), pl.BlockSpec((d, k), lambda i: (0, 0))]
    ins += [x, rstd, w, dres]
    specs += [row, pl.BlockSpec((tm, 1), lambda i: (i, 0)), vec, row]
    if after is not None:
        ins.append(after)
        specs.append(pl.BlockSpec(memory_space=pl.ANY))
    return pl.pallas_call(
        body, name=name, grid=(m // tm,), in_specs=specs, out_specs=[row, row, vec],
        out_shape=[jax.ShapeDtypeStruct((m, d), F32), jax.ShapeDtypeStruct((m, d), BF16),
                   jax.ShapeDtypeStruct((1, d), F32)],
        compiler_params=_cparams(("arbitrary",)),
    )(*ins)


def _softplus(x):
    return jnp.maximum(x, 0.0) + jnp.log(1.0 + jnp.exp(-jnp.abs(x)))


def _prep(proj_a, bias128, alog128, bl, t):
    n = bl * t
    nch = t // CHUNK
    col0 = (SSD_WIDTH + CONV_CH) // 128

    def body(p_ref, b_ref, al_ref, dt_ref, gd_ref, ac_ref, act_ref, negc_ref):
        negc_ref[...] = jnp.zeros_like(negc_ref)
        row = lax.broadcasted_iota(jnp.int32, (CHUNK, CHUNK), 0)
        col = lax.broadcasted_iota(jnp.int32, (CHUNK, CHUNK), 1)
        tril = (row >= col).astype(F32)
        lane = lax.broadcasted_iota(jnp.int32, (1, 128), 1)
        head_lanes = lane < 16
        a_row = -jnp.exp(al_ref[...])
        carry = jnp.zeros((1, 128), F32)
        for ci in range(nch):
            rows = slice(ci * CHUNK, (ci + 1) * CHUNK)
            xv = p_ref[rows, :] + b_ref[...]
            sp = _softplus(xv)
            acum = jnp.dot(tril, a_row * sp, precision=HIGHEST, preferred_element_type=F32)
            c = jnp.dot(tril, -_softplus(-xv), precision=HIGHEST, preferred_element_type=F32) + carry
            carry = c[CHUNK - 1:CHUNK, :]
            dt_ref[rows, :] = jnp.where(head_lanes, sp, 0.0)
            gd_ref[rows, :] = jnp.where(head_lanes, jax.nn.sigmoid(xv),
                                        jnp.where(lane < 32, jax.nn.sigmoid(-xv), 0.0))
            ac_ref[rows, :] = jnp.where(head_lanes, acum, 0.0)
            act_ref[:, rows] = jnp.transpose(acum)[0:16, :]
            c_t = jnp.transpose(c)
            for hp in range(8):
                negc_ref[hp, 0:2, rows] = -c_t[16 + 2 * hp:18 + 2 * hp, :]

    o128 = pl.BlockSpec((t, 128), lambda b: (b, 0))
    v128 = pl.BlockSpec((1, 128), lambda b: (0, 0))
    w128 = jax.ShapeDtypeStruct((n, 128), F32)
    return pl.pallas_call(
        body, name="head_scalars", grid=(bl,),
        in_specs=[pl.BlockSpec((t, 128), lambda b: (b, col0)), v128, v128],
        out_specs=[o128, o128, o128, pl.BlockSpec((16, t), lambda b: (0, b)),
                   pl.BlockSpec((None, 8, 8, t), lambda b: (b, 0, 0, 0))],
        out_shape=[w128, w128, w128, jax.ShapeDtypeStruct((16, n), F32),
                   jax.ShapeDtypeStruct((bl, 8, 8, t), F32)],
        compiler_params=_cparams(("parallel",)),
    )(proj_a, bias128, alog128)


def _fpost(dc, gate_d, ddt, dpa, bl, t):
    n = bl * t
    nch = t // CHUNK
    col0 = (SSD_WIDTH + CONV_CH) // 128

    def body(dc_ref, gd_ref, ddt_ref, dpa_in, out_ref, db_ref):
        @pl.when(pl.program_id(0) == 0)
        def _():
            db_ref[...] = jnp.zeros_like(db_ref)

        row = lax.broadcasted_iota(jnp.int32, (CHUNK, CHUNK), 0)
        col = lax.broadcasted_iota(jnp.int32, (CHUNK, CHUNK), 1)
        triu = (row <= col).astype(F32)
        lane = lax.broadcasted_iota(jnp.int32, (1, 128), 1)
        gate_lanes = (lane >= 16) & (lane < 32)
        carry = jnp.zeros((1, 128), F32)
        db = jnp.zeros((1, 128), F32)
        for ci in reversed(range(nch)):
            rows = slice(ci * CHUNK, (ci + 1) * CHUNK)
            dlf = jnp.dot(triu, dc_ref[rows, :], precision=HIGHEST, preferred_element_type=F32) + carry
            carry = dlf[0:1, :]
            df = jnp.where(gate_lanes, dlf * gd_ref[rows, :], 0.0)
            out_ref[rows, :] = (ddt_ref[rows, :] + df).astype(BF16)
            db = db + jnp.sum(df, axis=0, keepdims=True)
        db_ref[...] += db[:, 16:32]

    blk = pl.BlockSpec((t, 128), lambda b: (b, 0))
    return pl.pallas_call(
        body, name="forget_gate_bwd", grid=(bl,),
        in_specs=[blk, blk, blk, ANY],
        out_specs=[pl.BlockSpec((t, 128), lambda b: (b, col0)), pl.BlockSpec((1, 16), lambda b: (0, 0))],
        out_shape=[jax.ShapeDtypeStruct(dpa.shape, dpa.dtype), jax.ShapeDtypeStruct((1, 16), F32)],
        input_output_aliases={3: 0},
        compiler_params=_cparams(("arbitrary",)),
    )(dc, gate_d, ddt, dpa)


CONV_TILE = 256
CONV_ROWS = 256


def _conv_taps(u_ref, i):
    r0 = pl.multiple_of(i * CONV_ROWS, CONV_ROWS)
    cur = u_ref[pl.ds(r0, CONV_ROWS), :]
    p0 = pl.multiple_of(jnp.maximum(r0 - 8, 0), 8)
    prev = jnp.where(i > 0, u_ref[pl.ds(p0, 8), :], 0.0)
    cat = jnp.concatenate([prev, cur], axis=0)
    return r0, [cur] + [pltpu.roll(cat, s, 0)[8:, :] for s in (1, 2, 3)]


def _conv_fwd(proj_a, conv_w, conv_b, bl, t):
    n = bl * t
    nct = CONV_CH // CONV_TILE
    c0 = SSD_WIDTH // CONV_TILE

    def body(u_ref, w_ref, b_ref, o_ref, d_ref):
        w = w_ref[...]
        bias = b_ref[...]

        def chunk(i, carry):
            r0, taps = _conv_taps(u_ref, i)
            pre = bias + w[3:4, :] * taps[0]
            for s in (1, 2, 3):
                pre = pre + w[3 - s:4 - s, :] * taps[s]
            sg = jax.nn.sigmoid(pre)
            o_ref[pl.ds(r0, CONV_ROWS), :] = pre * sg
            d_ref[pl.ds(r0, CONV_ROWS), :] = (sg * (1.0 + pre * (1.0 - sg))).astype(BF16)
            return carry

        lax.fori_loop(0, t // CONV_ROWS, chunk, 0)

    out = pl.BlockSpec((t, CONV_TILE), lambda b, c: (b, c))
    return pl.pallas_call(
        body, name="conv_silu_fwd", grid=(bl, nct),
        in_specs=[pl.BlockSpec((t, CONV_TILE), lambda b, c: (b, c0 + c)),
                  pl.BlockSpec((4, CONV_TILE), lambda b, c: (0, c)),
                  pl.BlockSpec((1, CONV_TILE), lambda b, c: (0, c))],
        out_specs=[out, out],
        out_shape=[jax.ShapeDtypeStruct((n, CONV_CH), F32), jax.ShapeDtypeStruct((n, CONV_CH), BF16)],
        compiler_params=_cparams(("parallel", "parallel")),
    )(proj_a, conv_w, conv_b)


def _conv_bwd(dxc, dsilu, proj_a, conv_w, dpa, bl, t):
    nct = CONV_CH // CONV_TILE
    c0 = SSD_WIDTH // CONV_TILE
    nrc = t // CONV_ROWS

    def body(g_ref, s_ref, u_ref, w_ref, dpa_in, du_ref, dw_ref, db_ref, dp_scr):
        @pl.when(pl.program_id(1) == 0)
        def _():
            dw_ref[...] = jnp.zeros_like(dw_ref)
            db_ref[...] = jnp.zeros_like(db_ref)

        w = w_ref[...]
        dp_scr[pl.ds(t, 8), :] = jnp.zeros((8, CONV_TILE), F32)

        def chunk1(i, carry):
            dw0, dw1, dw2, dw3, db = carry
            r0, taps = _conv_taps(u_ref, i)
            dpre = g_ref[pl.ds(r0, CONV_ROWS), :] * s_ref[pl.ds(r0, CONV_ROWS), :].astype(F32)
            dp_scr[pl.ds(r0, CONV_ROWS), :] = dpre
            dw3 = dw3 + jnp.sum(dpre * taps[0], axis=0, keepdims=True)
            dw2 = dw2 + jnp.sum(dpre * taps[1], axis=0, keepdims=True)
            dw1 = dw1 + jnp.sum(dpre * taps[2], axis=0, keepdims=True)
            dw0 = dw0 + jnp.sum(dpre * taps[3], axis=0, keepdims=True)
            db = db + jnp.sum(dpre, axis=0, keepdims=True)
            return dw0, dw1, dw2, dw3, db

        z = jnp.zeros((1, CONV_TILE), F32)
        dw0, dw1, dw2, dw3, db = lax.fori_loop(0, nrc, chunk1, (z, z, z, z, z))
        dw_ref[...] += jnp.concatenate([dw0, dw1, dw2, dw3], axis=0)
        db_ref[...] += db

        def chunk2(i, carry):
            r0 = pl.multiple_of(i * CONV_ROWS, CONV_ROWS)
            cat = dp_scr[pl.ds(r0, CONV_ROWS + 8), :]
            du = w[3:4, :] * cat[:CONV_ROWS, :]
            for s in (1, 2, 3):
                du = du + w[3 - s:4 - s, :] * pltpu.roll(cat, CONV_ROWS + 8 - s, 0)[:CONV_ROWS, :]
            du_ref[pl.ds(r0, CONV_ROWS), :] = du.astype(BF16)
            return carry

        lax.fori_loop(0, nrc, chunk2, 0)

    tile = pl.BlockSpec((t, CONV_TILE), lambda c, b: (b, c))
    return pl.pallas_call(
        body, name="conv_silu_bwd", grid=(nct, bl),
        in_specs=[tile, tile, pl.BlockSpec((t, CONV_TILE), lambda c, b: (b, c0 + c)),
                  pl.BlockSpec((4, CONV_TILE), lambda c, b: (0, c)), ANY],
        out_specs=[pl.BlockSpec((t, CONV_TILE), lambda c, b: (b, c0 + c)),
                   pl.BlockSpec((4, CONV_TILE), lambda c, b: (0, c)),
                   pl.BlockSpec((1, CONV_TILE), lambda c, b: (0, c))],
        out_shape=[jax.ShapeDtypeStruct(dpa.shape, dpa.dtype), jax.ShapeDtypeStruct((4, CONV_CH), F32),
                   jax.ShapeDtypeStruct((1, CONV_CH), F32)],
        input_output_aliases={4: 0},
        scratch_shapes=[pltpu.VMEM((t + 8, CONV_TILE), F32)],
        compiler_params=_cparams(("parallel", "arbitrary")),
    )(dxc, dsilu, proj_a, conv_w, dpa)


NT_DIMS = (((1,), (1,)), ((), ()))
TN_DIMS = (((0,), (0,)), ((), ()))


def _dot(a, b, dims=None):
    if dims is None:
        return jnp.dot(a, b, preferred_element_type=F32)
    return lax.dot_general(a, b, dims, preferred_element_type=F32)


def _head_expander():
    r = lax.broadcasted_iota(jnp.int32, (128, SSD_WIDTH), 0)
    c = lax.broadcasted_iota(jnp.int32, (128, SSD_WIDTH), 1)
    return ((c // HEAD_DIM == r % 16) & (r < 48)).astype(BF16)


def _spread(v128, expander):
    hi = v128.astype(BF16).astype(F32)
    r1 = v128 - hi
    mid = r1.astype(BF16).astype(F32)
    lo = (r1 - mid).astype(BF16).astype(F32)
    packed = (hi + pltpu.roll(mid, 16, 1) + pltpu.roll(lo, 32, 1)).astype(BF16)
    return jnp.dot(packed, expander, preferred_element_type=F32)


def _head_sums(v1024, expander):
    hi = v1024.astype(BF16)
    lo = (v1024 - hi.astype(F32)).astype(BF16)
    heads = jnp.where(lax.broadcasted_iota(jnp.int32, expander.shape, 0) < 16, expander, jnp.zeros_like(expander))
    return _dot(hi, heads, NT_DIMS) + _dot(lo, heads, NT_DIMS)


def _ssd_fwd(xc, proj_a, dt, acum, acum_t, dskip_e, norm_w, bl, t):
    n = bl * t
    nch = t // CHUNK
    L = CHUNK

    def body(xc_ref, z_ref, dt_ref, ac_ref, act_ref, dsk_ref, nw_ref, ys_ref, yp_ref, hp_ref, h_scr, y_scr, x_scr):
        @pl.when(pl.program_id(1) == 0)
        def _():
            h_scr[...] = jnp.zeros_like(h_scr)

        row = lax.broadcasted_iota(jnp.int32, (L, L), 0)
        col = lax.broadcasted_iota(jnp.int32, (L, L), 1)
        causal = row >= col
        lane128 = lax.broadcasted_iota(jnp.int32, (1, L), 1)
        expander = _head_expander()
        ac_all = ac_ref[...]
        act_all = act_ref[...]
        ac_e = _spread(ac_all, expander)
        e_in = jnp.exp(ac_e)
        dec = jnp.exp(ac_e[L - 1:L, :] - ac_e)
        xs_all = xc_ref[:, 0:SSD_WIDTH]
        x_all = xs_all * _spread(dt_ref[...], expander)
        x_scr[...] = x_all.astype(BF16)
        hp_all = h_scr[...]
        hp_ref[...] = hp_all
        for g in range(2):
            gs = slice(g * 512, (g + 1) * 512)
            bg = xc_ref[:, SSD_WIDTH + g * 128:SSD_WIDTH + (g + 1) * 128].astype(BF16)
            cg = xc_ref[:, SSD_WIDTH + 256 + g * 128:SSD_WIDTH + 256 + (g + 1) * 128].astype(BF16)
            gmat = _dot(cg, bg, NT_DIMS)
            y_off = _dot(cg, hp_all[gs, :].astype(BF16), NT_DIMS) * e_in[:, gs] + dsk_ref[:, gs] * xs_all[:, gs]
            s_new = _dot((x_all[:, gs] * dec[:, gs]).astype(BF16), bg, TN_DIMS)
            for pr in range(4):
                pair = slice((g * 4 + pr) * 128, (g * 4 + pr + 1) * 128)
                x_pair = x_scr[:, pair]
                y_pair = y_off[:, pr * 128:(pr + 1) * 128]
                for j in range(2):
                    h = g * 8 + 2 * pr + j
                    sl = slice(h * HEAD_DIM, (h + 1) * HEAD_DIM)
                    r = 2 * pr + j
                    ldec = jnp.exp(jnp.where(causal, ac_all[:, h:h + 1] - act_all[h:h + 1, :], NEG))
                    x_head = jnp.where((lane128 < HEAD_DIM) == (j == 0), x_pair, jnp.zeros_like(x_pair))
                    y_pair = y_pair + _dot((gmat * ldec).astype(BF16), x_head)
                    elast = jnp.exp(ac_all[L - 1:L, h:h + 1])
                    h_scr[sl, :] = elast * hp_all[sl, :] + s_new[r * HEAD_DIM:(r + 1) * HEAD_DIM, :]
                y_scr[:, pair] = y_pair
        y = y_scr[...]
        yp_ref[...] = y
        zv = z_ref[...]
        yg = y * (zv * jax.nn.sigmoid(zv))
        for g in range(2):
            gs = slice(g * 512, (g + 1) * 512)
            grp = yg[:, gs]
            rstd = lax.rsqrt(jnp.mean(grp * grp, axis=1, keepdims=True) + EPS)
            ys_ref[:, gs] = (grp * rstd * nw_ref[:, gs]).astype(BF16)

    rb = lambda b, c: (b * nch + c, 0)
    v1k = pl.BlockSpec((1, SSD_WIDTH), lambda b, c: (0, 0))
    return pl.pallas_call(
        body, name="ssd_fwd", grid=(bl, nch),
        in_specs=[pl.BlockSpec((L, CONV_CH), rb), pl.BlockSpec((L, SSD_WIDTH), rb),
                  pl.BlockSpec((L, 128), rb), pl.BlockSpec((L, 128), rb),
                  pl.BlockSpec((16, L), lambda b, c: (0, b * nch + c)), v1k, v1k],
        out_specs=[pl.BlockSpec((L, SSD_WIDTH), rb), pl.BlockSpec((L, SSD_WIDTH), rb),
                   pl.BlockSpec((None, SSD_WIDTH, SSD_STATE), lambda b, c: (b * nch + c, 0, 0))],
        out_shape=[jax.ShapeDtypeStruct((n, SSD_WIDTH), BF16), jax.ShapeDtypeStruct((n, SSD_WIDTH), F32),
                   jax.ShapeDtypeStruct((bl * nch, SSD_WIDTH, SSD_STATE), F32)],
        scratch_shapes=[pltpu.VMEM((SSD_WIDTH, SSD_STATE), F32), pltpu.VMEM((L, SSD_WIDTH), F32),
                        pltpu.VMEM((L, SSD_WIDTH), BF16)],
        compiler_params=_cparams(("parallel", "arbitrary")),
    )(xc, proj_a, dt, acum, acum_t, dskip_e, norm_w)


def _ssd_bwd(dys, xc, proj_a, ypre, hprev, dt, gate_d, acum, acum_t, alog128, dskip_e, norm_w, bl, t):
    n = bl * t
    nch = t // CHUNK
    L = CHUNK

    def body(dys_ref, xc_ref, z_ref, yp_ref, hp_ref, dt_ref, gd_ref, ac_ref, act_ref, al_ref, dsk_ref, nw_ref,
             dxc_ref, dz_ref, ddt_ref, dnw_ref, dsk16_ref, da16_ref, db16_ref,
             dh_scr, dy_scr, x_scr, dx_scr, red_scr):
        first = (pl.program_id(0) == 0) & (pl.program_id(1) == 0)

        @pl.when(first)
        def _():
            dnw_ref[...] = jnp.zeros_like(dnw_ref)
            dsk16_ref[...] = jnp.zeros_like(dsk16_ref)
            da16_ref[...] = jnp.zeros_like(da16_ref)
            db16_ref[...] = jnp.zeros_like(db16_ref)

        @pl.when(pl.program_id(1) == 0)
        def _():
            dh_scr[...] = jnp.zeros_like(dh_scr)

        y = yp_ref[...]
        zv = z_ref[...]
        sz = jax.nn.sigmoid(zv)
        gate = zv * sz
        yg = y * gate
        dout = dys_ref[...]
        nw = nw_ref[...]
        for g in range(2):
            gs = slice(g * 512, (g + 1) * 512)
            grp = yg[:, gs]
            rstd = lax.rsqrt(jnp.mean(grp * grp, axis=1, keepdims=True) + EPS)
            ghat = grp * rstd
            dnw_ref[:, gs] += jnp.sum(dout[:, gs] * ghat, axis=0, keepdims=True)
            gw = dout[:, gs] * nw[:, gs]
            dyg = rstd * (gw - ghat * jnp.mean(gw * ghat, axis=1, keepdims=True))
            dy_scr[:, gs] = dyg * gate[:, gs]
            dz_ref[:, gs] = (dyg * y[:, gs] * (sz[:, gs] * (1.0 + zv[:, gs] * (1.0 - sz[:, gs])))).astype(BF16)

        row = lax.broadcasted_iota(jnp.int32, (L, L), 0)
        col = lax.broadcasted_iota(jnp.int32, (L, L), 1)
        causal = row >= col
        lane128 = lax.broadcasted_iota(jnp.int32, (1, L), 1)
        rows128 = lax.broadcasted_iota(jnp.int32, (L, 1), 0)
        last_row = rows128 == (L - 1)
        expander = _head_expander()
        ac_all = ac_ref[...]
        act_all = act_ref[...]
        dt_all = dt_ref[...]
        dt_e = _spread(dt_all, expander)
        ac_e = _spread(ac_all, expander)
        e_in = jnp.exp(ac_e)
        dec = jnp.exp(ac_e[L - 1:L, :] - ac_e)
        xs_all = xc_ref[:, 0:SSD_WIDTH]
        x_all = xs_all * dt_e
        x_scr[...] = x_all.astype(BF16)
        dy_all = dy_scr[...]
        hp_all = hp_ref[...]
        ds_all = dh_scr[...]
        dsk_cols = jnp.sum(dy_all * xs_all, axis=0, keepdims=True)
        dac = jnp.zeros((L, L), F32)
        dac_row = jnp.zeros((L, L), F32)
        ddec_cols = []
        for g in range(2):
            gs = slice(g * 512, (g + 1) * 512)
            bsl = slice(SSD_WIDTH + g * 128, SSD_WIDTH + (g + 1) * 128)
            csl = slice(SSD_WIDTH + 256 + g * 128, SSD_WIDTH + 256 + (g + 1) * 128)
            bg = xc_ref[:, bsl].astype(BF16)
            cg = xc_ref[:, csl].astype(BF16)
            gmat = _dot(cg, bg, NT_DIMS)
            hpb = hp_all[gs, :].astype(BF16)
            dsb = ds_all[gs, :].astype(BF16)
            ch = _dot(cg, hpb, NT_DIMS)
            dye = dy_all[:, gs] * e_in[:, gs]
            dyeb = dye.astype(BF16)
            dc_acc = _dot(dyeb, hpb)
            dhp = _dot(dyeb, cg, TN_DIMS)
            dxd = _dot(bg, dsb, NT_DIMS)
            db_acc = _dot((x_all[:, gs] * dec[:, gs]).astype(BF16), dsb)
            ddec = dxd * x_all[:, gs] * dec[:, gs]
            ddec_cols.append(jnp.sum(ddec, axis=0, keepdims=True))
            dx_inter = dxd * dec[:, gs]
            red_scr[:, gs] = dye * ch - ddec
            dg_sum = jnp.zeros((L, L), F32)
            for pr in range(4):
                pair = slice((g * 4 + pr) * 128, (g * 4 + pr + 1) * 128)
                x_pair = x_scr[:, pair]
                dy_pair = dy_scr[:, pair].astype(BF16)
                dx_pair = dx_inter[:, pr * 128:(pr + 1) * 128]
                for j in range(2):
                    h = g * 8 + 2 * pr + j
                    r = 2 * pr + j
                    sl = slice(h * HEAD_DIM, (h + 1) * HEAD_DIM)
                    onehot_w = lane128 == h
                    ldec = jnp.exp(jnp.where(causal, ac_all[:, h:h + 1] - act_all[h:h + 1, :], NEG))
                    mf = gmat * ldec
                    dyb = jnp.where((lane128 < HEAD_DIM) == (j == 0), dy_pair, jnp.zeros_like(dy_pair))
                    dm = _dot(dyb, x_pair, NT_DIMS)
                    dx_pair = dx_pair + _dot(mf.astype(BF16), dyb, TN_DIMS)
                    dg_sum = dg_sum + dm * ldec
                    wmat = dm * mf
                    elast = jnp.exp(ac_all[L - 1:L, h:h + 1])
                    hp_h = hp_all[sl, :]
                    ds_h = ds_all[sl, :]
                    extra = elast * jnp.sum(jnp.sum(hp_h * ds_h, axis=1, keepdims=True), axis=0, keepdims=True)
                    dac = dac + jnp.where(onehot_w,
                                          jnp.sum(wmat, axis=1, keepdims=True) + jnp.where(last_row, extra, 0.0), 0.0)
                    dac_row = dac_row + jnp.where(rows128 == h, -jnp.sum(wmat, axis=0, keepdims=True), 0.0)
                    dh_scr[sl, :] = elast * ds_h + dhp[r * HEAD_DIM:(r + 1) * HEAD_DIM, :]
                dx_scr[:, pair] = dx_pair
            dgb = dg_sum.astype(BF16)
            dxc_ref[:, csl] = dc_acc + _dot(dgb, bg)
            dxc_ref[:, bsl] = db_acc + _dot(dgb, cg, TN_DIMS)
        dx_all = dx_scr[...]
        dxc_ref[:, 0:SSD_WIDTH] = dx_all * dt_e + dsk_ref[...] * dy_all
        red = red_scr[...]
        dac_slab = _head_sums(red, expander)
        ddec_tot = _head_sums(jnp.broadcast_to(jnp.concatenate(ddec_cols, axis=1), (8, SSD_WIDTH)), expander)
        ddt_x = _head_sums(dx_all * xs_all, expander)
        dsk16_ref[...] += _head_sums(jnp.broadcast_to(dsk_cols, (8, SSD_WIDTH)), expander)[0:1, 0:16]
        dac = dac + dac_slab + jnp.transpose(dac_row) + jnp.where(last_row, ddec_tot[0:1, :], 0.0)
        triu = (row <= col).astype(F32)
        da = jnp.dot(triu, dac, precision=HIGHEST, preferred_element_type=F32)
        a_row = -jnp.exp(al_ref[...])
        ddt = jnp.where(lane128 < 16, (ddt_x + da * a_row) * gd_ref[...], 0.0)
        ddt_ref[...] = ddt
        da16_ref[...] += (jnp.sum(da * dt_all, axis=0, keepdims=True) * a_row)[:, 0:16]
        db16_ref[...] += jnp.sum(ddt, axis=0, keepdims=True)[:, 0:16]

    rb = lambda b, c: (b * nch + nch - 1 - c, 0)
    v1k = pl.BlockSpec((1, SSD_WIDTH), lambda b, c: (0, 0))
    v16 = pl.BlockSpec((1, 16), lambda b, c: (0, 0))
    v128 = pl.BlockSpec((1, 128), lambda b, c: (0, 0))
    wide = pl.BlockSpec((L, SSD_WIDTH), rb)
    s128 = pl.BlockSpec((L, 128), rb)
    return pl.pallas_call(
        body, name="ssd_bwd", grid=(bl, nch),
        in_specs=[wide, pl.BlockSpec((L, CONV_CH), rb), wide, wide,
                  pl.BlockSpec((None, SSD_WIDTH, SSD_STATE), lambda b, c: (b * nch + nch - 1 - c, 0, 0)),
                  s128, s128, s128, pl.BlockSpec((16, L), lambda b, c: (0, b * nch + nch - 1 - c)), v128, v1k, v1k],
        out_specs=[pl.BlockSpec((L, CONV_CH), rb), wide, s128, v1k, v16, v16, v16],
        out_shape=[jax.ShapeDtypeStruct((n, CONV_CH), F32), jax.ShapeDtypeStruct((n, PA_WIDTH), BF16),
                   jax.ShapeDtypeStruct((n, 128), F32), jax.ShapeDtypeStruct((1, SSD_WIDTH), F32),
                   jax.ShapeDtypeStruct((1, 16), F32), jax.ShapeDtypeStruct((1, 16), F32),
                   jax.ShapeDtypeStruct((1, 16), F32)],
        scratch_shapes=[pltpu.VMEM((SSD_WIDTH, SSD_STATE), F32), pltpu.VMEM((L, SSD_WIDTH), F32),
                        pltpu.VMEM((L, SSD_WIDTH), BF16), pltpu.VMEM((L, SSD_WIDTH), F32),
                        pltpu.VMEM((L, SSD_WIDTH), F32)],
        compiler_params=_cparams(("arbitrary", "arbitrary")),
    )(dys, xc, proj_a, ypre, hprev, dt, gate_d, acum, acum_t, alog128, dskip_e, norm_w)


def _attn_fwd(qkv, negc, bl, t):
    n = bl * t
    tb_ = ATT_BLOCK
    nb = t // tb_
    scale2 = LOG2E / math.sqrt(HEAD_DIM)

    def body(q_ref, k_ref, v_ref, c_ref, o_ref, lse_ref, v0_scr, v1_scr, k0_scr, k1_scr):
        row = lax.broadcasted_iota(jnp.int32, (tb_, tb_), 0)
        col = lax.broadcasted_iota(jnp.int32, (tb_, tb_), 1)
        causal = row >= col
        lane = lax.broadcasted_iota(jnp.int32, (1, 128), 1)
        v_pair = v_ref[...].astype(F32)
        k_pair = k_ref[...]
        v_scrs = (v0_scr, v1_scr)
        k_scrs = (k0_scr, k1_scr)
        for j in range(2):
            v_head = v_pair if j == 0 else pltpu.roll(v_pair, HEAD_DIM, 1)
            v_scrs[j][...] = jnp.where(lane < HEAD_DIM, v_head, jnp.where(lane == HEAD_DIM, 1.0, 0.0)).astype(BF16)
            k_scrs[j][...] = jnp.where((lane < HEAD_DIM) == (j == 0), k_pair, jnp.zeros_like(k_pair))
        for qi in range(nb):
            r0, lk = qi * tb_, (qi + 1) * tb_
            for j in range(2):
                sl = slice(j * HEAD_DIM, (j + 1) * HEAD_DIM)
                s = _dot(q_ref[r0:lk, :], k_scrs[j][0:lk, :], NT_DIMS) * scale2 + c_ref[j:j + 1, 0:lk] * LOG2E
                tail = jnp.where(causal, s[:, r0:lk], NEG)
                s = tail if qi == 0 else jnp.concatenate([s[:, 0:r0], tail], axis=1)
                m = jnp.max(s, axis=1, keepdims=True)
                p = jnp.exp2(s - m)
                acc = _dot(p.astype(BF16), v_scrs[j][0:lk, :])
                l = acc[:, HEAD_DIM:HEAD_DIM + 1]
                o_ref[r0:lk, sl] = (acc[:, 0:HEAD_DIM] / l).astype(BF16)
                lse_ref[r0:lk, sl] = jnp.broadcast_to(m + jnp.log(l) * LOG2E, (tb_, HEAD_DIM))

    blk = lambda off: pl.BlockSpec((t, 128), lambda b, hp: (b, off + hp))
    return pl.pallas_call(
        body, name="fox_attn_fwd", grid=(bl, 8),
        in_specs=[blk(0), blk(8), blk(16), pl.BlockSpec((None, None, 8, t), lambda b, hp: (b, hp, 0, 0))],
        out_specs=[blk(0), blk(0)],
        out_shape=[jax.ShapeDtypeStruct((n, ATT_WIDTH), BF16), jax.ShapeDtypeStruct((n, ATT_WIDTH), F32)],
        scratch_shapes=[pltpu.VMEM((t, 128), BF16)] * 4,
        compiler_params=_cparams(("parallel", "parallel")),
    )(qkv, qkv, qkv, negc)


def _attn_bwd(qkv, do, o, lse, negc, after, bl, t):
    n = bl * t
    tb_ = ATT_BLOCK
    nb = t // tb_
    scale = 1.0 / math.sqrt(HEAD_DIM)
    scale2 = LOG2E * scale

    def body(q_ref, k_ref, v_ref, do_ref, o_ref, lse_ref, c_ref, after_ref, dq_ref, dk_ref, dv_ref, dc_ref,
             dq0_scr, delta_scr, dq1_scr, qt0_scr, qt1_scr, dot_scr, dkt0_scr, dkt1_scr, dvt_scr):
        row = lax.broadcasted_iota(jnp.int32, (tb_, tb_), 0)
        col = lax.broadcasted_iota(jnp.int32, (tb_, tb_), 1)
        causal = row >= col
        lane = lax.broadcasted_iota(jnp.int32, (1, 128), 1)
        dq_scrs = (dq0_scr, dq1_scr)
        qt_scrs = (qt0_scr, qt1_scr)
        dkt_scrs = (dkt0_scr, dkt1_scr)
        dq0_scr[...] = jnp.zeros_like(dq0_scr)
        dq1_scr[...] = jnp.zeros_like(dq1_scr)
        dc_ref[...] = jnp.zeros_like(dc_ref)
        q_t = jnp.transpose(q_ref[...].astype(F32))
        ones_row = jnp.where(lax.broadcasted_iota(jnp.int32, (8, t), 0) == 0, 1.0, 0.0)
        for j in range(2):
            qt_scrs[j][...] = jnp.concatenate(
                [q_t[j * HEAD_DIM:(j + 1) * HEAD_DIM, :], ones_row, jnp.zeros((HEAD_DIM - 8, t), F32)],
                axis=0).astype(BF16)
        dot_scr[...] = jnp.transpose(do_ref[...].astype(F32)).astype(BF16)
        prod = do_ref[...].astype(F32) * o_ref[...].astype(F32)
        for j in range(2):
            sl = slice(j * HEAD_DIM, (j + 1) * HEAD_DIM)
            delta_scr[:, sl] = jnp.broadcast_to(jnp.sum(prod[:, sl], axis=1, keepdims=True), (t, HEAD_DIM))
        for kj in range(nb):
            r0, r1 = kj * tb_, (kj + 1) * tb_
            k_blk = k_ref[r0:r1, :]
            v_blk = v_ref[r0:r1, :]
            k_pair = k_blk.astype(F32)
            for j in range(2):
                sl = slice(j * HEAD_DIM, (j + 1) * HEAD_DIM)
                one = slice(j * HEAD_DIM, j * HEAD_DIM + 1)
                own = (lane < HEAD_DIM) == (j == 0)
                k_head = k_pair if j == 0 else pltpu.roll(k_pair, HEAD_DIM, 1)
                k_ones = jnp.where(lane < HEAD_DIM, k_head, jnp.where(lane == HEAD_DIM, 1.0, 0.0)).astype(BF16)
                s = (_dot(q_ref[r0:t, :], jnp.where(own, k_blk, jnp.zeros_like(k_blk)), NT_DIMS) * scale2
                     + c_ref[j:j + 1, r0:r1] * LOG2E)
                head = jnp.where(causal, s[0:tb_, :], NEG)
                s = head if kj == nb - 1 else jnp.concatenate([head, s[tb_:, :]], axis=0)
                p = jnp.exp2(s - lse_ref[r0:t, one])
                dp = _dot(do_ref[r0:t, :], jnp.where(own, v_blk, jnp.zeros_like(v_blk)), NT_DIMS)
                ds = p * (dp - delta_scr[r0:t, one])
                dsb = ds.astype(BF16)
                dvt_scr[sl, r0:r1] = _dot(dot_scr[sl, r0:t], p.astype(BF16))
                dkt_scrs[j][:, r0:r1] = _dot(qt_scrs[j][:, r0:t], dsb)
                dq_scrs[j][r0:t, :] += _dot(dsb, k_ones)
        dv_ref[...] = jnp.transpose(dvt_scr[...]).astype(BF16)
        for j in range(2):
            sl = slice(j * HEAD_DIM, (j + 1) * HEAD_DIM)
            acc = dq_scrs[j][...]
            dkt = dkt_scrs[j][...]
            dq_ref[:, sl] = (acc[:, 0:HEAD_DIM] * scale).astype(BF16)
            dk_ref[:, sl] = (jnp.transpose(dkt)[:, 0:HEAD_DIM] * scale).astype(BF16)
            dc_ref[j:j + 1, :] = jnp.transpose(acc)[HEAD_DIM:HEAD_DIM + 1, :] - dkt[HEAD_DIM:HEAD_DIM + 1, :]

    blk = lambda off: pl.BlockSpec((t, 128), lambda b, hp: (b, off + hp))
    cblk = pl.BlockSpec((None, None, 8, t), lambda b, hp: (b, hp, 0, 0))
    return pl.pallas_call(
        body, name="fox_attn_bwd", grid=(bl, 8),
        in_specs=[blk(0), blk(8), blk(16), blk(0), blk(0), blk(0), cblk, ANY],
        out_specs=[blk(0), blk(0), blk(0), cblk],
        out_shape=[jax.ShapeDtypeStruct((n, ATT_WIDTH), BF16)] * 3 + [jax.ShapeDtypeStruct((bl, 8, 8, t), F32)],
        scratch_shapes=[pltpu.VMEM((t, 128), F32), pltpu.VMEM((t, 128), F32), pltpu.VMEM((t, 128), F32),
                        pltpu.VMEM((128, t), BF16), pltpu.VMEM((128, t), BF16), pltpu.VMEM((128, t), BF16),
                        pltpu.VMEM((128, t), F32), pltpu.VMEM((128, t), F32), pltpu.VMEM((128, t), F32)],
        compiler_params=_cparams(("parallel", "parallel")),
    )(qkv, qkv, qkv, do, o, lse, negc, after)


def _adamw(w, g, m, v, *, name):
    lead = w.ndim == 3
    r, c = w.shape[-2:]
    tr = _pick(r, (256, IN_SHARD // 3, 128, 64, 32, 16, 8))
    bc1 = 1.0 - ADAM_B1 ** ADAM_STEP
    bc2 = 1.0 - ADAM_B2 ** ADAM_STEP

    def body(w_ref, g_ref, m_ref, v_ref, d_ref, nm_ref, nv_ref):
        gv = g_ref[...]
        mn = ADAM_B1 * m_ref[...] + (1.0 - ADAM_B1) * gv
        vn = ADAM_B2 * v_ref[...] + (1.0 - ADAM_B2) * (gv * gv)
        m_hat = mn / bc1
        v_hat = vn / bc2
        d_ref[...] = -ADAM_LR * (m_hat / (jnp.sqrt(v_hat) + ADAM_EPS) + ADAM_WD * w_ref[...])
        nm_ref[...] = mn
        nv_ref[...] = vn

    flat = pl.BlockSpec((tr, c), lambda i: (i, 0))
    blk = pl.BlockSpec((None, tr, c), lambda i: (0, i, 0)) if lead else flat
    return pl.pallas_call(
        body, name=name, grid=(r // tr,), in_specs=[blk, flat, blk, blk], out_specs=[blk] * 3,
        out_shape=[jax.ShapeDtypeStruct(w.shape, F32)] * 3,
        compiler_params=_cparams(("parallel",)),
    )(w, g, m, v)


def _sum_leading(parts, *, name, out_dtype=F32):
    k, r, c = parts.shape
    tr = _pick(r, (512, 256, 128, 96, 64, 32, 16, 8))

    def body(p_ref, o_ref):
        acc = p_ref[0].astype(F32)
        for i in range(1, k):
            acc = acc + p_ref[i].astype(F32)
        o_ref[...] = acc.astype(out_dtype)

    return pl.pallas_call(
        body, name=name, grid=(r // tr,),
        in_specs=[pl.BlockSpec((k, tr, c), lambda i: (0, i, 0))],
        out_specs=pl.BlockSpec((tr, c), lambda i: (i, 0)),
        out_shape=jax.ShapeDtypeStruct((r, c), out_dtype),
        compiler_params=_cparams(("parallel",)),
    )(parts)


def _add_pair(a, b, *, name):
    k, r, c = a.shape
    tr = _pick(r, (512, 256, 128))

    def body(a_ref, b_ref, o_ref):
        o_ref[...] = (a_ref[...].astype(F32) + b_ref[...].astype(F32)).astype(BF16)

    blk = pl.BlockSpec((None, tr, c), lambda j, i: (j, i, 0))
    return pl.pallas_call(
        body, name=name, grid=(k, r // tr), in_specs=[blk, blk], out_specs=blk,
        out_shape=jax.ShapeDtypeStruct((k, r, c), BF16),
        compiler_params=_cparams(("parallel", "parallel")),
    )(a, b)


ANY = pl.BlockSpec(memory_space=pl.ANY)


def _chip_peers(x, y):
    return [(1 - x, y, 2 * (1 - x) + y), (x, 1 - y, 2 * x + 1 - y), (1 - x, 1 - y, 2 * (1 - x) + 1 - y)]


def _gather_weights(blob, *, name):
    rows, cols = blob.shape
    half_rows = rows // 2

    def body(b_ref, o_ref, send_sems, recv_sems):
        x, y, c = lax.axis_index("x"), lax.axis_index("y"), lax.axis_index("c")
        me = 2 * x + y
        sibling = (x, y, 1 - c)
        peers = _chip_peers(x, y)

        def half(chip, hc):
            return o_ref.at[chip, pl.ds(hc * half_rows, half_rows), :]

        def copy(k, src, chip, hc, to):
            return pltpu.make_async_remote_copy(src_ref=src, dst_ref=half(chip, hc), send_sem=send_sems.at[k],
                                                recv_sem=recv_sems.at[k], device_id=to, device_id_type=MESH)

        my_half = b_ref.at[pl.ds(c * half_rows, half_rows), :]
        first = [copy(k, my_half, me, c, (px, py, c)) for k, (px, py, _) in enumerate(peers)]
        for cp in first:
            cp.start()
        passed = [copy(3 + k, half(pc, c), pc, c, sibling) for k, (_, _, pc) in enumerate(peers)]
        for k, (px, py, pc) in enumerate(peers):
            copy(k, my_half, pc, c, (px, py, c)).wait_recv()
            passed[k].start()
        for k, (_, _, pc) in enumerate(peers):
            copy(3 + k, half(pc, 1 - c), pc, 1 - c, sibling).wait_recv()
        for cp in first + passed:
            cp.wait_send()

    return pl.pallas_call(
        body, name=name, in_specs=[ANY], out_specs=ANY,
        out_shape=jax.ShapeDtypeStruct((N_CHIPS, rows, cols), BF16),
        scratch_shapes=[pltpu.SemaphoreType.DMA((6,)), pltpu.SemaphoreType.DMA((6,))],
    )(blob)


def _swap_halves(g, *, name):
    _, rows, cols = g.shape
    half_rows = rows // 2

    def body(g_ref, o_ref, send_sem, recv_sem):
        x, y, c = lax.axis_index("x"), lax.axis_index("y"), lax.axis_index("c")
        cp = pltpu.make_async_remote_copy(
            src_ref=g_ref.at[:, pl.ds((1 - c) * half_rows, half_rows), :], dst_ref=o_ref,
            send_sem=send_sem, recv_sem=recv_sem, device_id=(x, y, 1 - c), device_id_type=MESH)
        cp.start()
        cp.wait()

    return pl.pallas_call(
        body, name=name, in_specs=[ANY], out_specs=ANY,
        out_shape=jax.ShapeDtypeStruct((N_CHIPS, half_rows, cols), BF16),
        scratch_shapes=[pltpu.SemaphoreType.DMA, pltpu.SemaphoreType.DMA],
    )(g)


HBM_SPEC = pl.BlockSpec(memory_space=pltpu.HBM)
SEM_SPEC = pl.BlockSpec(memory_space=pltpu.SEMAPHORE)
SPLIT_EFFECT = pltpu.SideEffectType.DATAFLOW_SIDE_EFFECTING


def _gather_peers_copies(b_ref, land_ref, send_sems, recv_sems, sending):
    x, y, c = lax.axis_index("x"), lax.axis_index("y"), lax.axis_index("c")
    me = 2 * x + y
    half_rows = b_ref.shape[0] // 2
    src = b_ref.at[pl.ds(c * half_rows, half_rows), :]
    return [pltpu.make_async_remote_copy(
        src_ref=src, dst_ref=land_ref.at[me if sending else pc, pl.ds(c * half_rows, half_rows), :],
        send_sem=send_sems.at[k], recv_sem=recv_sems.at[k], device_id=(px, py, c), device_id_type=MESH)
        for k, (px, py, pc) in enumerate(_chip_peers(x, y))]


def _gather_start(blob, after, *, name):
    shape = (N_CHIPS,) + blob.shape

    def body(b_ref, land_ref, after_ref, send_sems, recv_sems, b_thru, land_thru, token):
        for cp in _gather_peers_copies(b_ref, land_ref, send_sems, recv_sems, True):
            cp.start()
        token[...] = jnp.zeros_like(token)

    return pl.pallas_call(
        body, name=name,
        out_shape=(pltpu.SemaphoreType.DMA((3,)), pltpu.SemaphoreType.DMA((3,)), pltpu.HBM(blob.shape, blob.dtype),
                   pltpu.HBM(shape, blob.dtype), jax.ShapeDtypeStruct((8, 128), F32)),
        in_specs=(HBM_SPEC, HBM_SPEC, ANY),
        out_specs=(SEM_SPEC, SEM_SPEC, HBM_SPEC, HBM_SPEC, pl.BlockSpec(memory_space=pltpu.VMEM)),
        input_output_aliases={0: 2, 1: 3},
        compiler_params=pltpu.CompilerParams(has_side_effects=SPLIT_EFFECT),
    )(pltpu.with_memory_space_constraint(blob, pltpu.HBM),
      pltpu.with_memory_space_constraint(lax.empty(shape, blob.dtype), pltpu.HBM), after)


def _gather_wait(send_sems, recv_sems, b_thru, land_thru, after, *, name):
    def body(b_ref, land_ref, send_sems, recv_sems, after_ref, b_dead, got_ref):
        for cp in _gather_peers_copies(b_ref, land_ref, send_sems, recv_sems, False):
            cp.wait_send()
            cp.wait_recv()

    return pl.pallas_call(
        body, name=name,
        out_shape=(pltpu.HBM(b_thru.shape, b_thru.dtype), pltpu.HBM(land_thru.shape, land_thru.dtype)),
        in_specs=(HBM_SPEC, HBM_SPEC, SEM_SPEC, SEM_SPEC, ANY), out_specs=(HBM_SPEC, HBM_SPEC),
        input_output_aliases={0: 0, 1: 1},
        compiler_params=pltpu.CompilerParams(has_side_effects=SPLIT_EFFECT),
    )(b_thru, land_thru, send_sems, recv_sems, after)


def _gather_forward(land, *, name):
    half_rows = land.shape[1] // 2

    def body(l_ref, o_ref, send_sems, recv_sems):
        x, y, c = lax.axis_index("x"), lax.axis_index("y"), lax.axis_index("c")
        cps = []
        for k, (_, _, pc) in enumerate(_chip_peers(x, y)):
            mine = pl.ds(c * half_rows, half_rows)
            cps.append(pltpu.make_async_remote_copy(
                src_ref=l_ref.at[pc, mine, :], dst_ref=o_ref.at[pc, mine, :], send_sem=send_sems.at[k],
                recv_sem=recv_sems.at[k], device_id=(x, y, 1 - c), device_id_type=MESH))
        for cp in cps:
            cp.start()
        for k, (_, _, pc) in enumerate(_chip_peers(x, y)):
            theirs = pl.ds((1 - c) * half_rows, half_rows)
            pltpu.make_async_remote_copy(
                src_ref=l_ref.at[pc, theirs, :], dst_ref=o_ref.at[pc, theirs, :], send_sem=send_sems.at[k],
                recv_sem=recv_sems.at[k], device_id=(x, y, 1 - c), device_id_type=MESH).wait_recv()
        for cp in cps:
            cp.wait_send()

    return pl.pallas_call(
        body, name=name, in_specs=[ANY], out_specs=ANY, input_output_aliases={0: 0},
        out_shape=jax.ShapeDtypeStruct(land.shape, land.dtype),
        scratch_shapes=[pltpu.SemaphoreType.DMA((3,)), pltpu.SemaphoreType.DMA((3,))],
    )(land)


def _exchange_peers_copies(p_ref, land_ref, send_sems, recv_sems, sending):
    x, y, c = lax.axis_index("x"), lax.axis_index("y"), lax.axis_index("c")
    me = 2 * x + y
    return [pltpu.make_async_remote_copy(src_ref=p_ref.at[pc], dst_ref=land_ref.at[me if sending else pc],
                                         send_sem=send_sems.at[k], recv_sem=recv_sems.at[k],
                                         device_id=(px, py, c), device_id_type=MESH)
            for k, (px, py, pc) in enumerate(_chip_peers(x, y))]


def _exchange_start(p, *, name):
    def body(p_ref, land_ref, send_sems, recv_sems, p_thru, land_thru, token):
        for cp in _exchange_peers_copies(p_ref, land_ref, send_sems, recv_sems, True):
            cp.start()
        token[...] = jnp.zeros_like(token)

    return pl.pallas_call(
        body, name=name,
        out_shape=(pltpu.SemaphoreType.DMA((3,)), pltpu.SemaphoreType.DMA((3,)), pltpu.HBM(p.shape, p.dtype),
                   pltpu.HBM(p.shape, p.dtype), jax.ShapeDtypeStruct((8, 128), F32)),
        in_specs=(HBM_SPEC, HBM_SPEC),
        out_specs=(SEM_SPEC, SEM_SPEC, HBM_SPEC, HBM_SPEC, pl.BlockSpec(memory_space=pltpu.VMEM)),
        input_output_aliases={0: 2, 1: 3},
        compiler_params=pltpu.CompilerParams(has_side_effects=SPLIT_EFFECT),
    )(pltpu.with_memory_space_constraint(p, pltpu.HBM),
      pltpu.with_memory_space_constraint(lax.empty(p.shape, p.dtype), pltpu.HBM))


def _exchange_wait(send_sems, recv_sems, p_thru, land_thru, after, *, name):
    def body(p_ref, land_ref, send_sems, recv_sems, after_ref, p_dead, got_ref):
        for cp in _exchange_peers_copies(p_ref, land_ref, send_sems, recv_sems, False):
            cp.wait_send()
            cp.wait_recv()

    return pl.pallas_call(
        body, name=name,
        out_shape=(pltpu.HBM(p_thru.shape, p_thru.dtype), pltpu.HBM(p_thru.shape, p_thru.dtype)),
        in_specs=(HBM_SPEC, HBM_SPEC, SEM_SPEC, SEM_SPEC, ANY), out_specs=(HBM_SPEC, HBM_SPEC),
        input_output_aliases={0: 0, 1: 1},
        compiler_params=pltpu.CompilerParams(has_side_effects=SPLIT_EFFECT),
    )(p_thru, land_thru, send_sems, recv_sems, after)


def _sum_parts(parts, own, *, name):
    k, r, c = parts.shape
    tr = _pick(r, (512, 256, 128))

    def body(p_ref, own_ref, o_ref):
        me = 2 * lax.axis_index("x") + lax.axis_index("y")
        acc = jnp.zeros((tr, c), F32)
        for i in range(k):
            acc = acc + jnp.where(me == i, own_ref[i], p_ref[i]).astype(F32)
        o_ref[...] = acc

    blk = pl.BlockSpec((k, tr, c), lambda i: (0, i, 0))
    return pl.pallas_call(
        body, name=name, grid=(r // tr,), in_specs=[blk, blk],
        out_specs=pl.BlockSpec((tr, c), lambda i: (i, 0)),
        out_shape=jax.ShapeDtypeStruct((r, c), F32),
        compiler_params=_cparams(("parallel",)),
    )(parts, own)


def _join_halves(gh, *, name):
    def body(g_ref, o_ref, send_sem, recv_sem):
        x, y, c = lax.axis_index("x"), lax.axis_index("y"), lax.axis_index("c")
        cp = pltpu.make_async_remote_copy(src_ref=g_ref, dst_ref=o_ref, send_sem=send_sem, recv_sem=recv_sem,
                                          device_id=(x, y, 1 - c), device_id_type=MESH)
        cp.start()
        cp.wait()

    other = pl.pallas_call(
        body, name=name, in_specs=[ANY], out_specs=ANY,
        out_shape=jax.ShapeDtypeStruct(gh.shape, F32),
        scratch_shapes=[pltpu.SemaphoreType.DMA, pltpu.SemaphoreType.DMA],
    )(gh)
    south = lax.axis_index("c") == 0
    return jnp.concatenate([jnp.where(south, gh, other), jnp.where(south, other, gh)], axis=0)


def _gather_small(s, *, name):
    rows = s.shape[0]

    def body(s_ref, o_ref, send_sems, recv_sems, local_sem):
        x, y, c = lax.axis_index("x"), lax.axis_index("y"), lax.axis_index("c")
        me = 4 * x + 2 * y + c
        mine = pltpu.make_async_copy(s_ref, o_ref.at[me], local_sem)
        mine.start()
        peers = []
        for k in range(1, 8):
            peers.append((1 - x if k & 4 else x, 1 - y if k & 2 else y, 1 - c if k & 1 else c))
        cps = [pltpu.make_async_remote_copy(src_ref=s_ref, dst_ref=o_ref.at[me], send_sem=send_sems.at[k],
                                            recv_sem=recv_sems.at[k], device_id=p, device_id_type=MESH)
               for k, p in enumerate(peers)]
        for cp in cps:
            cp.start()
        for k, (px, py, pc) in enumerate(peers):
            pltpu.make_async_remote_copy(src_ref=s_ref, dst_ref=o_ref.at[4 * px + 2 * py + pc],
                                         send_sem=send_sems.at[k], recv_sem=recv_sems.at[k],
                                         device_id=(px, py, pc), device_id_type=MESH).wait_recv()
        for cp in cps:
            cp.wait_send()
        mine.wait()

    return pl.pallas_call(
        body, name=name, in_specs=[ANY], out_specs=ANY,
        out_shape=jax.ShapeDtypeStruct((8, rows, 128), F32),
        scratch_shapes=[pltpu.SemaphoreType.DMA((7,)), pltpu.SemaphoreType.DMA((7,)), pltpu.SemaphoreType.DMA],
    )(s)


IN_SHARD = IN_WIDTH // N_CHIPS
IN_SHARD_PAD = 1536
UP_ROWS, DOWN_ROWS, OUT_ROWS = 1024, 1024, 512
REST_ROWS = UP_ROWS + DOWN_ROWS + OUT_ROWS


def _pack_in(w_in_s):
    return jnp.pad(w_in_s, ((0, 0), (0, IN_SHARD_PAD - IN_SHARD))).astype(BF16)


def _pack_rest(w_out_s, w_up_s, w_down_s):
    return jnp.concatenate([w_up_s, w_down_s, w_out_s], axis=0).astype(BF16)


def _unpack_rest(blob):
    return (blob[UP_ROWS + DOWN_ROWS:], blob[0:UP_ROWS], blob[UP_ROWS:UP_ROWS + DOWN_ROWS])


def _with_own(gathered, own):
    me = 2 * lax.axis_index("x") + lax.axis_index("y")
    return [jnp.where(me == j, own, gathered[j]) for j in range(N_CHIPS)]


def _full_w_in(g_in, own):
    return jnp.concatenate([s[:, :IN_SHARD] for s in _with_own(g_in, own)], axis=1)


def _full_rest(g_rest, own):
    parts = [_unpack_rest(s) for s in _with_own(g_rest, own)]
    w_out = jnp.concatenate([p[0] for p in parts], axis=0)
    w_up = jnp.concatenate([p[1] for p in parts], axis=1)
    w_down = jnp.concatenate([p[2] for p in parts], axis=0)
    return w_out, w_up, w_down


def _split_w_in(w_in):
    z_xbc = w_in[:, 0:2560]
    dt = w_in[:, 2560:2576]
    qkv = w_in[:, 2576:5648]
    f = w_in[:, 5648:5664]
    pad = jnp.zeros((w_in.shape[0], PA_WIDTH - 2592), w_in.dtype)
    return jnp.concatenate([z_xbc, dt, f, pad], axis=1), qkv


def _merge_w_in(d_a, d_qkv):
    return jnp.concatenate([d_a[:, 0:2560], d_a[:, 2560:2576], d_qkv, d_a[:, 2576:2592]], axis=1)


def _local_step(x3, target3, w_in, rest_weights, norm_mix_w, conv_w, conv_b, dt_bias, a_log, d_skip,
                ssd_norm_w, f_bias, norm_mlp_w, norm_final_w, first_after=None, early_grads=None, late_grads=None):
    bl, t, d = x3.shape
    n = bl * t
    x = x3.reshape(n, d)
    target = target3.reshape(n, d)
    w_a, w_qkv = _split_w_in(w_in)
    nfw = norm_final_w.reshape(1, d)
    dskip_e = jnp.repeat(d_skip, HEAD_DIM, axis=1)
    nb = t // ATT_BLOCK

    r1, r2, kt = min(n, 1024), min(n, 512), min(n, 512)
    if first_after is None:
        first_after = jnp.zeros((8, 128), F32)
    h0, rstd0, proj_a = _norm_mm(x, norm_mix_w, w_a, first_after, name="norm_mix_proj_a", tm=r2)
    qkv = _mm(h0, w_qkv, name="proj_qkv", tiles=(r2, QKV_WIDTH, D_MODEL), out_dtype=BF16)
    bias128 = jnp.concatenate([dt_bias, f_bias, jnp.zeros((1, 96), F32)], axis=1)
    alog128 = jnp.concatenate([a_log, jnp.zeros((1, 112), F32)], axis=1)
    dt, gate_d, acum, acum_t, negc = _prep(proj_a, bias128, alog128, bl, t)
    xc, dsilu = _conv_fwd(proj_a, conv_w, conv_b, bl, t)
    y_ssd, y_pre, hprev = _ssd_fwd(xc, proj_a, dt, acum, acum_t, dskip_e, ssd_norm_w, bl, t)
    y_att, lse = _attn_fwd(qkv, negc, bl, t)
    w_out, w_up, w_down = rest_weights(y_att)
    wo_s, wo_a = w_out[:SSD_WIDTH], w_out[SSD_WIDTH:]
    h1, h1n, rstd1 = _mm_norm_fwd(y_ssd, wo_s, y_att, wo_a, x, norm_mlp_w, name="out_proj_norm_mlp", tm=r2)
    up = _mm(h1n, w_up, name="mlp_up", tiles=(r2, D_FF, D_MODEL), out_dtype=BF16)
    dh2, dh2b, loss, d_nfw = _mm_final(up, w_down, h1, nfw, target, name="mlp_down_final_norm_loss", tm=r2,
                                       a_act="relu2")

    dup = _mm(dh2b, w_down, name="mlp_down_bwd_act", tiles=(r2, D_FF, D_MODEL), tb=True, epi_up=up, out_dtype=BF16)
    rest_shape = (N_CHIPS, REST_ROWS, D_MODEL)
    gb_rest = _mm(up, dh2b, name="mlp_down_bwd_w", tiles=(DOWN_ROWS, D_MODEL, kt), ta=True, a_act="relu2",
                  out_dtype=BF16, into=(rest_shape, (None, DOWN_ROWS, D_MODEL), lambda i, j, k: (i, 1, 0), None))
    dh1, dh1b, d_nmlp = _mm_norm_bwd([(dup, w_up)], h1, rstd1, norm_mlp_w, dh2, name="mlp_up_bwd_act_norm_mlp",
                                     tm=r2)
    gb_rest = _mm(h1n, dup, name="mlp_up_bwd_w", tiles=(D_MODEL, UP_ROWS, kt), ta=True, out_dtype=BF16,
                  into=(rest_shape, (None, D_MODEL, UP_ROWS), lambda i, j, k: (j, 0, 0), gb_rest))
    dys, do = _mm_two_halves(dh1b, w_out, name="out_proj_bwd_act", tm=r1)
    out_block = (UP_ROWS + DOWN_ROWS) // OUT_ROWS
    for half, (y_half, tag) in enumerate(((y_ssd, "ssd"), (y_att, "att"))):
        gb_rest = _mm(y_half, dh1b, name="out_proj_bwd_w_" + tag, tiles=(2 * OUT_ROWS, D_MODEL, kt), ta=True,
                      out_dtype=BF16, into=(rest_shape, (2, OUT_ROWS, D_MODEL),
                                            functools.partial(lambda i, j, k, h: (h, out_block, 0), h=half),
                                            gb_rest))
    token = jnp.zeros((8, 128), F32) if early_grads is None else early_grads(gb_rest)
    dq, dk, dv, dcb = _attn_bwd(qkv, do, y_att, lse, negc, token, bl, t)
    dc = jnp.pad(dcb[:, :, 0:2, :].transpose(0, 3, 1, 2).reshape(n, 16), ((0, 0), (16, 96)))
    dxc, dpa, ddt_raw, d_snw, d_dsk, d_alog, d_dtb = _ssd_bwd(dys, xc, proj_a, y_pre, hprev, dt, gate_d, acum,
                                                             acum_t, alog128, dskip_e, ssd_norm_w, bl, t)
    dpa, d_conv_w, d_conv_b = _conv_bwd(dxc, dsilu, proj_a, conv_w, dpa, bl, t)
    dproj_a, d_fb = _fpost(dc, gate_d, ddt_raw, dpa, bl, t)
    dqkv = jnp.concatenate([dq, dk, dv], axis=1)
    d_w_a = _mm(h0, dproj_a, name="proj_a_bwd_w", tiles=(1024, 896, kt), ta=True, out_dtype=BF16)
    d_w_qkv = _mm(h0, dqkv, name="proj_qkv_bwd_w", tiles=(1024, 1024, kt), ta=True, out_dtype=BF16)
    d_w_in = _merge_w_in(d_w_a, d_w_qkv)
    late_token = None if late_grads is None else late_grads(d_w_in)
    dx, _, d_nmix = _mm_norm_bwd([(dproj_a, w_a), (dqkv, w_qkv)], x, rstd0, norm_mix_w, dh1,
                                 name="proj_bwd_act_norm_mix", tm=min(n, 256), after=late_token)

    grads = dict(norm_mix_w=d_nmix, w_in=d_w_in, conv_w=d_conv_w, conv_b=d_conv_b,
                 dt_bias=d_dtb, a_log=d_alog, d_skip=d_dsk, ssd_norm_w=d_snw, f_bias=d_fb, rest=gb_rest,
                 norm_mlp_w=d_nmlp, norm_final_w=d_nfw)
    return dx.reshape(bl, t, d), loss, grads


SMALL_ORDER = ("norm_mix_w", "conv_w", "conv_b", "dt_bias", "a_log", "d_skip", "ssd_norm_w", "f_bias",
               "norm_mlp_w", "norm_final_w")
SMALL_SIZES = (1024, 4 * CONV_CH, CONV_CH, 16, 16, 16, 1024, 16, 1024, 1024)


def _pack_small(vals, rows):
    flat = jnp.concatenate([v.reshape(-1).astype(F32) for v in vals])
    return jnp.pad(flat, (0, rows * 128 - flat.shape[0])).reshape(rows, 128)


def _unpack_small(packed, sizes):
    flat = packed.reshape(-1)
    out, o = [], 0
    for s in sizes:
        out.append(flat[o:o + s])
        o += s
    return out


def kernel(x, norm_mix_w, w_in, conv_w, conv_b, dt_bias, a_log, d_skip, ssd_norm_w, f_bias, w_out, norm_mlp_w, w_up, w_down, norm_final_w, loss_target, m_norm_mix_w, m_w_in, m_conv_w, m_conv_b, m_dt_bias, m_a_log, m_d_skip, m_ssd_norm_w, m_f_bias, m_w_out, m_norm_mlp_w, m_w_up, m_w_down, m_norm_final_w, v_norm_mix_w, v_w_in, v_conv_w, v_conv_b, v_dt_bias, v_a_log, v_d_skip, v_ssd_norm_w, v_f_bias, v_w_out, v_norm_mlp_w, v_w_up, v_w_down, v_norm_final_w):
    chip = 2 * lax.axis_index("x") + lax.axis_index("y")
    cw = CONV_CH // N_CHIPS

    own_in = _pack_in(w_in[0])
    own_rest = _pack_rest(w_out[0], w_up[0], w_down[0])
    g_in = _gather_weights(own_in, name="gather_w_in")
    w_in_f = _full_w_in(g_in, own_in)
    *rest_handles, rest_token = _gather_start(own_rest, g_in, name="gather_start_rest")

    def rest_weights(after):
        _, landed = _gather_wait(*rest_handles, after, name="gather_wait_rest")
        return _full_rest(_gather_forward(landed, name="gather_forward_rest"), own_rest)
    small_all = _gather_small(_pack_small([conv_w[0]], 16), name="gather_conv_w")
    conv_w_f = jnp.concatenate([small_all[2 * j].reshape(-1)[:4 * cw].reshape(4, cw) for j in range(N_CHIPS)], axis=1)

    c = lax.axis_index("c")

    def chip_partial(gb, tag):
        half_rows = gb.shape[1] // 2
        from_sibling = _swap_halves(gb, name="grad_swap_halves_" + tag)
        my_half = lax.dynamic_slice_in_dim(gb, c * half_rows, half_rows, axis=1)
        return _add_pair(my_half, from_sibling, name="grad_add_sibling_" + tag)

    in_flight = {}

    def early_grads(gb_rest):
        part = chip_partial(gb_rest, "rest")
        *handles, token = _exchange_start(part, name="grad_exchange_start_rest")
        in_flight["rest"] = handles
        return token

    def late_grads(d_w_in):
        gb_in = jnp.stack([_pack_in(d_w_in[:, j * IN_SHARD:(j + 1) * IN_SHARD]) for j in range(N_CHIPS)])
        *handles, token = _exchange_start(chip_partial(gb_in, "in"), name="grad_exchange_start_in")
        in_flight["in"] = handles
        return token

    dx, loss_part, g = _local_step(x, loss_target, w_in_f, rest_weights, norm_mix_w, conv_w_f,
                                   conv_b, dt_bias, a_log, d_skip, ssd_norm_w, f_bias, norm_mlp_w, norm_final_w,
                                   first_after=rest_token, early_grads=early_grads, late_grads=late_grads)

    send_sems, recv_sems, part_rest, land_rest = in_flight["rest"]
    part_rest, parts_rest = _exchange_wait(send_sems, recv_sems, part_rest, land_rest, dx,
                                           name="grad_exchange_wait_rest")
    g_rest_half = _sum_parts(parts_rest, part_rest, name="grad_sum_chips_rest")
    g_w_out, g_w_up, g_w_down = _unpack_rest(_join_halves(g_rest_half, name="grad_join_halves_rest"))

    part_in, parts_in = _exchange_wait(*in_flight["in"], dx, name="grad_exchange_wait_in")
    g_in_half = _sum_parts(parts_in, part_in, name="grad_sum_chips_in")
    g_w_in = _join_halves(g_in_half, name="grad_join_halves_in")[:, :IN_SHARD]

    small_vals = [g[k] for k in SMALL_ORDER] + [loss_part[:, 0:1]]
    small_sum = _sum_leading(_gather_small(_pack_small(small_vals, SMALL_ROWS), name="gather_small_grads"), name="small_sum")
    sg = dict(zip(SMALL_ORDER + ("loss",), _unpack_small(small_sum, SMALL_SIZES + (1,))))
    loss = sg["loss"].reshape(())
    g_conv_full = sg["conv_w"].reshape(4, CONV_CH)
    g_conv = lax.dynamic_slice_in_dim(g_conv_full, chip * cw, cw, axis=1)

    grads = dict(norm_mix_w=sg["norm_mix_w"].reshape(1, -1), w_in=g_w_in[None], conv_w=g_conv[None],
                 conv_b=sg["conv_b"].reshape(1, -1), dt_bias=sg["dt_bias"].reshape(1, -1),
                 a_log=sg["a_log"].reshape(1, -1), d_skip=sg["d_skip"].reshape(1, -1),
                 ssd_norm_w=sg["ssd_norm_w"].reshape(1, -1), f_bias=sg["f_bias"].reshape(1, -1), w_out=g_w_out[None],
                 norm_mlp_w=sg["norm_mlp_w"].reshape(1, -1), w_up=g_w_up[None], w_down=g_w_down[None],
                 norm_final_w=sg["norm_final_w"])
    weights = dict(norm_mix_w=norm_mix_w, w_in=w_in, conv_w=conv_w, conv_b=conv_b, dt_bias=dt_bias, a_log=a_log,
                   d_skip=d_skip, ssd_norm_w=ssd_norm_w, f_bias=f_bias, w_out=w_out, norm_mlp_w=norm_mlp_w,
                   w_up=w_up, w_down=w_down, norm_final_w=norm_final_w)
    ms = dict(norm_mix_w=m_norm_mix_w, w_in=m_w_in, conv_w=m_conv_w, conv_b=m_conv_b, dt_bias=m_dt_bias,
              a_log=m_a_log, d_skip=m_d_skip, ssd_norm_w=m_ssd_norm_w, f_bias=m_f_bias, w_out=m_w_out,
              norm_mlp_w=m_norm_mlp_w, w_up=m_w_up, w_down=m_w_down, norm_final_w=m_norm_final_w)
    vs = dict(norm_mix_w=v_norm_mix_w, w_in=v_w_in, conv_w=v_conv_w, conv_b=v_conv_b, dt_bias=v_dt_bias,
              a_log=v_a_log, d_skip=v_d_skip, ssd_norm_w=v_ssd_norm_w, f_bias=v_f_bias, w_out=v_w_out,
              norm_mlp_w=v_norm_mlp_w, w_up=v_w_up, w_down=v_w_down, norm_final_w=v_norm_final_w)
    names = list(weights)
    big = ("w_in", "w_out", "w_up", "w_down")
    delta, new_m, new_v = {}, {}, {}
    for k, g2 in zip(big[1:], (g_w_out, g_w_up, g_w_down)):
        delta[k], new_m[k], new_v[k] = _adamw(weights[k], g2, ms[k], vs[k], name="adamw_" + k)
    g_in_t = g_w_in.T
    outs_t = _adamw(w_in[0].T, g_in_t, m_w_in[0].T, v_w_in[0].T, name="adamw_w_in")
    delta["w_in"], new_m["w_in"], new_v["w_in"] = [o.T[None] for o in outs_t]
    grads["w_in"] = g_in_t.T[None]
    smalls = [k for k in names if k not in big]
    sizes = [math.prod(weights[k].shape) for k in smalls]
    rows = -(-sum(sizes) // 1024) * 8
    packs = [_pack_small([d[k] for k in smalls], rows) for d in (weights, grads, ms, vs)]
    outs = _adamw(*packs, name="adamw_small")
    for o, dst in zip(outs, (delta, new_m, new_v)):
        for k, val in zip(smalls, _unpack_small(o, sizes)):
            dst[k] = val.reshape(weights[k].shape)
    return (loss, dx, *[grads[k] for k in names], *[delta[k] for k in names], *[new_m[k] for k in names],
            *[new_v[k] for k in names])
```

```python
import functools
import math

import jax
import jax.numpy as jnp
from jax import lax
from jax.experimental import pallas as pl
from jax.experimental.pallas import tpu as pltpu

F32 = jnp.float32
BF16 = jnp.bfloat16
HIGHEST = lax.Precision.HIGHEST
MESH = pl.DeviceIdType.MESH

D_MODEL = 1024
SSD_HEADS = 16
HEAD_DIM = 64
SSD_WIDTH = 1024
SSD_STATE = 128
CONV_CH = 1536
CHUNK = 128
ATT_WIDTH = 1024
EPS = 1e-5
IN_WIDTH = 5664
PA_WIDTH = 2688
QKV_WIDTH = 3072
D_FF = 4096
ATT_BLOCK = 256
NEG = -1e30
LOG2E = 1.4426950408889634
VMEM_LIMIT = 48 * 1024 * 1024

ADAM_LR = 0.001
ADAM_B1 = 0.9
ADAM_B2 = 0.999
ADAM_EPS = 1e-08
ADAM_WD = 0.01
ADAM_STEP = 10

N_CHIPS = 4
SMALL_ROWS = 96


def _cparams(sem):
    return pltpu.CompilerParams(dimension_semantics=sem, vmem_limit_bytes=VMEM_LIMIT)


def _pick(n, cands):
    for c in cands:
        if n % c == 0:
            return c
    return n


MM_CHUNK = 512


def _mm(a, b, *, name, tiles, ta=False, tb=False, out_dtype=F32, res=None, a_act=None, epi_up=None, after=None,
        into=None):
    n_unread = (after is not None) + (into is not None and into[3] is not None)
    if ta:
        K, M = a.shape
    else:
        M, K = a.shape
    if tb:
        N, K2 = b.shape
    else:
        K2, N = b.shape
    assert K == K2, (a.shape, b.shape)
    tm, tn, tk = tiles
    assert M % tm == 0 and N % tn == 0 and K % tk == 0, (name, M, N, K, tiles)
    nk = K // tk
    dn = (((0 if ta else 1,), (1 if tb else 0,)), ((), ()))
    has_res = res is not None
    has_up = epi_up is not None
    cn = _pick(tn, (MM_CHUNK, 384, 256, 128))

    def prologue(av):
        if a_act == "relu2":
            r = jnp.maximum(av.astype(F32), 0.0)
            av = r * r
        return av.astype(BF16)

    def epilogue(out, res_v, up_v):
        if has_res:
            out = out + res_v.astype(F32)
        if has_up:
            out = out * (2.0 * jnp.maximum(up_v.astype(F32), 0.0))
        return out.astype(out_dtype)

    def body(*refs):
        a_ref, b_ref = refs[0], refs[1]
        i = 2
        res_ref = up_ref = None
        if has_res:
            res_ref = refs[i]
            i += 1
        if has_up:
            up_ref = refs[i]
            i += 1
        i += n_unread
        o_ref = refs[i]
        if nk == 1:
            av = prologue(a_ref[...])
            for c in range(tn // cn):
                cs = slice(c * cn, (c + 1) * cn)
                bv = (b_ref[cs, :] if tb else b_ref[:, cs]).astype(BF16)
                out = lax.dot_general(av, bv, dn, preferred_element_type=F32)
                o_ref[:, cs] = epilogue(out, res_ref[:, cs] if has_res else None, up_ref[:, cs] if has_up else None)
            return
        acc_ref = refs[i + 1]
        k = pl.program_id(2)

        @pl.when(k == 0)
        def _():
            acc_ref[...] = jnp.zeros_like(acc_ref)

        acc_ref[...] += lax.dot_general(prologue(a_ref[...]), b_ref[...].astype(BF16), dn,
                                        preferred_element_type=F32)

        @pl.when(k == nk - 1)
        def _():
            out = epilogue(acc_ref[...], res_ref[...] if has_res else None, up_ref[...] if has_up else None)
            o_ref[...] = out.reshape(o_ref.shape)

    a_spec = pl.BlockSpec((tk, tm), lambda i, j, k: (k, i)) if ta else pl.BlockSpec((tm, tk), lambda i, j, k: (i, k))
    b_spec = pl.BlockSpec((tn, tk), lambda i, j, k: (j, k)) if tb else pl.BlockSpec((tk, tn), lambda i, j, k: (k, j))
    o_spec = pl.BlockSpec((tm, tn), lambda i, j, k: (i, j))
    ins, specs = [a, b], [a_spec, b_spec]
    if has_res:
        ins.append(res)
        specs.append(o_spec)
    if has_up:
        ins.append(epi_up)
        specs.append(o_spec)
    if after is not None:
        ins.append(after)
        specs.append(pl.BlockSpec(memory_space=pl.ANY))
    out_shape, out_spec, aliases = jax.ShapeDtypeStruct((M, N), out_dtype), o_spec, {}
    if into is not None:
        shape, block, index, buf = into
        out_shape, out_spec = jax.ShapeDtypeStruct(shape, out_dtype), pl.BlockSpec(block, index)
        if buf is not None:
            aliases = {len(ins): 0}
            ins.append(buf)
            specs.append(pl.BlockSpec(memory_space=pl.ANY))
    return pl.pallas_call(
        body, name=name, grid=(M // tm, N // tn, nk),
        in_specs=specs, out_specs=out_spec, out_shape=out_shape, input_output_aliases=aliases,
        scratch_shapes=[] if nk == 1 else [pltpu.VMEM((tm, tn), F32)],
        compiler_params=_cparams(("parallel", "parallel", "arbitrary")),
    )(*ins)


def _rows_product(a_ref, b_ref, tb, a_act):
    av = a_ref[...]
    if a_act == "relu2":
        r = jnp.maximum(av.astype(F32), 0.0)
        av = r * r
    dn = (((1,), (1 if tb else 0,)), ((), ()))
    return lax.dot_general(av.astype(BF16), b_ref[...].astype(BF16), dn, preferred_element_type=F32)


def _norm_mm(x, w, b, after, *, name, tm):
    m, d = x.shape
    n = b.shape[1]
    cn = _pick(n, (MM_CHUNK, 384, 256, 128))

    def body(x_ref, w_ref, b_ref, after_ref, h_ref, r_ref, o_ref):
        xv = x_ref[...]
        rstd = lax.rsqrt(jnp.mean(xv * xv, axis=1, keepdims=True) + EPS)
        hv = (xv * rstd * w_ref[...]).astype(BF16)
        h_ref[...] = hv
        r_ref[...] = rstd
        for c in range(n // cn):
            cs = slice(c * cn, (c + 1) * cn)
            o_ref[:, cs] = jnp.dot(hv, b_ref[:, cs].astype(BF16), preferred_element_type=F32)

    row = pl.BlockSpec((tm, d), lambda i: (i, 0))
    return pl.pallas_call(
        body, name=name, grid=(m // tm,),
        in_specs=[row, pl.BlockSpec((1, d), lambda i: (0, 0)), pl.BlockSpec((d, n), lambda i: (0, 0)),
                  pl.BlockSpec(memory_space=pl.ANY)],
        out_specs=[row, pl.BlockSpec((tm, 1), lambda i: (i, 0)), pl.BlockSpec((tm, n), lambda i: (i, 0))],
        out_shape=[jax.ShapeDtypeStruct((m, d), BF16), jax.ShapeDtypeStruct((m, 1), F32),
                   jax.ShapeDtypeStruct((m, n), F32)],
        compiler_params=_cparams(("parallel",)),
    )(x, w, b, after)


def _mm_norm_fwd(a1, b1, a2, b2, res, w, *, name, tm):
    m, k1 = a1.shape
    k2 = a2.shape[1]
    d = b1.shape[1]

    def body(a1_ref, b1_ref, a2_ref, b2_ref, res_ref, w_ref, h_ref, y_ref, r_ref):
        hv = _rows_product(a1_ref, b1_ref, False, None) + _rows_product(a2_ref, b2_ref, False, None) + res_ref[...]
        rstd = lax.rsqrt(jnp.mean(hv * hv, axis=1, keepdims=True) + EPS)
        h_ref[...] = hv
        y_ref[...] = (hv * rstd * w_ref[...]).astype(BF16)
        r_ref[...] = rstd

    row = pl.BlockSpec((tm, d), lambda i: (i, 0))
    return pl.pallas_call(
        body, name=name, grid=(m // tm,),
        in_specs=[pl.BlockSpec((tm, k1), lambda i: (i, 0)), pl.BlockSpec((k1, d), lambda i: (0, 0)),
                  pl.BlockSpec((tm, k2), lambda i: (i, 0)), pl.BlockSpec((k2, d), lambda i: (0, 0)), row,
                  pl.BlockSpec((1, d), lambda i: (0, 0))],
        out_specs=[row, row, pl.BlockSpec((tm, 1), lambda i: (i, 0))],
        out_shape=[jax.ShapeDtypeStruct((m, d), F32), jax.ShapeDtypeStruct((m, d), BF16),
                   jax.ShapeDtypeStruct((m, 1), F32)],
        compiler_params=_cparams(("parallel",)),
    )(a1, b1, a2, b2, res, w)


def _mm_final(a, b, res, w, target, *, name, tm, a_act):
    m, k = a.shape
    d = b.shape[1]

    def body(a_ref, b_ref, res_ref, w_ref, t_ref, dh_ref, dhb_ref, loss_ref, dw_ref):
        @pl.when(pl.program_id(0) == 0)
        def _():
            loss_ref[...] = jnp.zeros_like(loss_ref)
            dw_ref[...] = jnp.zeros_like(dw_ref)

        hv = _rows_product(a_ref, b_ref, False, a_act) + res_ref[...]
        wv = w_ref[...]
        rstd = lax.rsqrt(jnp.mean(hv * hv, axis=1, keepdims=True) + EPS)
        xhat = hv * rstd
        err = xhat * wv - t_ref[...]
        loss_ref[...] += 0.5 * jnp.sum(jnp.mean(err * err, axis=1, keepdims=True), axis=0, keepdims=True)
        dy = err * (1.0 / d)
        gw = dy * wv
        dh = rstd * (gw - xhat * jnp.mean(gw * xhat, axis=1, keepdims=True))
        dh_ref[...] = dh
        dhb_ref[...] = dh.astype(BF16)
        dw_ref[...] += jnp.sum(dy * xhat, axis=0, keepdims=True)

    row = pl.BlockSpec((tm, d), lambda i: (i, 0))
    vec = pl.BlockSpec((1, d), lambda i: (0, 0))
    return pl.pallas_call(
        body, name=name, grid=(m // tm,),
        in_specs=[pl.BlockSpec((tm, k), lambda i: (i, 0)), pl.BlockSpec((k, d), lambda i: (0, 0)), row, vec, row],
        out_specs=[row, row, pl.BlockSpec((1, 128), lambda i: (0, 0)), vec],
        out_shape=[jax.ShapeDtypeStruct((m, d), F32), jax.ShapeDtypeStruct((m, d), BF16),
                   jax.ShapeDtypeStruct((1, 128), F32), jax.ShapeDtypeStruct((1, d), F32)],
        compiler_params=_cparams(("arbitrary",)),
    )(a, b, res, w, target)


def _mm_two_halves(a, b, *, name, tm):
    m, k = a.shape
    d = b.shape[0] // 2

    def body(a_ref, b_ref, lo_ref, hi_ref):
        av = a_ref[...].astype(BF16)
        lo_ref[...] = lax.dot_general(av, b_ref[0:d, :].astype(BF16), NT_DIMS, preferred_element_type=F32)
        hi_ref[...] = lax.dot_general(av, b_ref[d:2 * d, :].astype(BF16), NT_DIMS,
                                      preferred_element_type=F32).astype(BF16)

    row = pl.BlockSpec((tm, d), lambda i: (i, 0))
    return pl.pallas_call(
        body, name=name, grid=(m // tm,),
        in_specs=[pl.BlockSpec((tm, k), lambda i: (i, 0)), pl.BlockSpec((2 * d, k), lambda i: (0, 0))],
        out_specs=[row, row],
        out_shape=[jax.ShapeDtypeStruct((m, d), F32), jax.ShapeDtypeStruct((m, d), BF16)],
        compiler_params=_cparams(("parallel",)),
    )(a, b)


def _mm_norm_bwd(pairs, x, rstd, w, dres, *, name, tm, after=None):
    m = pairs[0][0].shape[0]
    d = pairs[0][1].shape[0]
    n_pairs = len(pairs)

    def body(*refs):
        i = 2 * n_pairs
        x_ref, r_ref, w_ref, d_ref = refs[i:i + 4]
        dx_ref, dxb_ref, dw_ref = refs[-3:]

        @pl.when(pl.program_id(0) == 0)
        def _():
            dw_ref[...] = jnp.zeros_like(dw_ref)

        g = _rows_product(refs[0], refs[1], True, None)
        for p in range(1, n_pairs):
            g = g + _rows_product(refs[2 * p], refs[2 * p + 1], True, None)
        r = r_ref[...]
        xhat = x_ref[...] * r
        gw = g * w_ref[...]
        dx = d_ref[...] + r * (gw - xhat * jnp.mean(gw * xhat, axis=1, keepdims=True))
        dx_ref[...] = dx
        dxb_ref[...] = dx.astype(BF16)
        dw_ref[...] += jnp.sum(g * xhat, axis=0, keepdims=True)

    row = pl.BlockSpec((tm, d), lambda i: (i, 0))
    vec = pl.BlockSpec((1, d), lambda i: (0, 0))
    ins, specs = [], []
    for a, b in pairs:
        k = a.shape[1]
        ins += [a, b]
        specs += [pl.BlockSpec((tm, k), lambda i: (i, 0)), pl.BlockSpec((d, k), lambda i: (0, 0))]
    ins += [x, rstd, w, dres]
    specs += [row, pl.BlockSpec((tm, 1), lambda i: (i, 0)), vec, row]
    if after is not None:
        ins.append(after)
        specs.append(pl.BlockSpec(memory_space=pl.ANY))
    return pl.pallas_call(
        body, name=name, grid=(m // tm,), in_specs=specs, out_specs=[row, row, vec],
        out_shape=[jax.ShapeDtypeStruct((m, d), F32), jax.ShapeDtypeStruct((m, d), BF16),
                   jax.ShapeDtypeStruct((1, d), F32)],
        compiler_params=_cparams(("arbitrary",)),
    )(*ins)


def _softplus(x):
    return jnp.maximum(x, 0.0) + jnp.log(1.0 + jnp.exp(-jnp.abs(x)))


def _prep(proj_a, bias128, alog128, bl, t):
    n = bl * t
    nch = t // CHUNK
    col0 = (SSD_WIDTH + CONV_CH) // 128

    def body(p_ref, b_ref, al_ref, dt_ref, gd_ref, ac_ref, act_ref, negc_ref):
        negc_ref[...] = jnp.zeros_like(negc_ref)
        row = lax.broadcasted_iota(jnp.int32, (CHUNK, CHUNK), 0)
        col = lax.broadcasted_iota(jnp.int32, (CHUNK, CHUNK), 1)
        tril = (row >= col).astype(F32)
        lane = lax.broadcasted_iota(jnp.int32, (1, 128), 1)
        head_lanes = lane < 16
        a_row = -jnp.exp(al_ref[...])
        carry = jnp.zeros((1, 128), F32)
        for ci in range(nch):
            rows = slice(ci * CHUNK, (ci + 1) * CHUNK)
            xv = p_ref[rows, :] + b_ref[...]
            sp = _softplus(xv)
            acum = jnp.dot(tril, a_row * sp, precision=HIGHEST, preferred_element_type=F32)
            c = jnp.dot(tril, -_softplus(-xv), precision=HIGHEST, preferred_element_type=F32) + carry
            carry = c[CHUNK - 1:CHUNK, :]
            dt_ref[rows, :] = jnp.where(head_lanes, sp, 0.0)
            gd_ref[rows, :] = jnp.where(head_lanes, jax.nn.sigmoid(xv),
                                        jnp.where(lane < 32, jax.nn.sigmoid(-xv), 0.0))
            ac_ref[rows, :] = jnp.where(head_lanes, acum, 0.0)
            act_ref[:, rows] = jnp.transpose(acum)[0:16, :]
            c_t = jnp.transpose(c)
            for hp in range(8):
                negc_ref[hp, 0:2, rows] = -c_t[16 + 2 * hp:18 + 2 * hp, :]

    o128 = pl.BlockSpec((t, 128), lambda b: (b, 0))
    v128 = pl.BlockSpec((1, 128), lambda b: (0, 0))
    w128 = jax.ShapeDtypeStruct((n, 128), F32)
    return pl.pallas_call(
        body, name="head_scalars", grid=(bl,),
        in_specs=[pl.BlockSpec((t, 128), lambda b: (b, col0)), v128, v128],
        out_specs=[o128, o128, o128, pl.BlockSpec((16, t), lambda b: (0, b)),
                   pl.BlockSpec((None, 8, 8, t), lambda b: (b, 0, 0, 0))],
        out_shape=[w128, w128, w128, jax.ShapeDtypeStruct((16, n), F32),
                   jax.ShapeDtypeStruct((bl, 8, 8, t), F32)],
        compiler_params=_cparams(("parallel",)),
    )(proj_a, bias128, alog128)


def _fpost(dc, gate_d, ddt, dpa, bl, t):
    n = bl * t
    nch = t // CHUNK
    col0 = (SSD_WIDTH + CONV_CH) // 128

    def body(dc_ref, gd_ref, ddt_ref, dpa_in, out_ref, db_ref):
        @pl.when(pl.program_id(0) == 0)
        def _():
            db_ref[...] = jnp.zeros_like(db_ref)

        row = lax.broadcasted_iota(jnp.int32, (CHUNK, CHUNK), 0)
        col = lax.broadcasted_iota(jnp.int32, (CHUNK, CHUNK), 1)
        triu = (row <= col).astype(F32)
        lane = lax.broadcasted_iota(jnp.int32, (1, 128), 1)
        gate_lanes = (lane >= 16) & (lane < 32)
        carry = jnp.zeros((1, 128), F32)
        db = jnp.zeros((1, 128), F32)
        for ci in reversed(range(nch)):
            rows = slice(ci * CHUNK, (ci + 1) * CHUNK)
            dlf = jnp.dot(triu, dc_ref[rows, :], precision=HIGHEST, preferred_element_type=F32) + carry
            carry = dlf[0:1, :]
            df = jnp.where(gate_lanes, dlf * gd_ref[rows, :], 0.0)
            out_ref[rows, :] = (ddt_ref[rows, :] + df).astype(BF16)
            db = db + jnp.sum(df, axis=0, keepdims=True)
        db_ref[...] += db[:, 16:32]

    blk = pl.BlockSpec((t, 128), lambda b: (b, 0))
    return pl.pallas_call(
        body, name="forget_gate_bwd", grid=(bl,),
        in_specs=[blk, blk, blk, ANY],
        out_specs=[pl.BlockSpec((t, 128), lambda b: (b, col0)), pl.BlockSpec((1, 16), lambda b: (0, 0))],
        out_shape=[jax.ShapeDtypeStruct(dpa.shape, dpa.dtype), jax.ShapeDtypeStruct((1, 16), F32)],
        input_output_aliases={3: 0},
        compiler_params=_cparams(("arbitrary",)),
    )(dc, gate_d, ddt, dpa)


CONV_TILE = 256
CONV_ROWS = 256


def _conv_taps(u_ref, i):
    r0 = pl.multiple_of(i * CONV_ROWS, CONV_ROWS)
    cur = u_ref[pl.ds(r0, CONV_ROWS), :]
    p0 = pl.multiple_of(jnp.maximum(r0 - 8, 0), 8)
    prev = jnp.where(i > 0, u_ref[pl.ds(p0, 8), :], 0.0)
    cat = jnp.concatenate([prev, cur], axis=0)
    return r0, [cur] + [pltpu.roll(cat, s, 0)[8:, :] for s in (1, 2, 3)]


def _conv_fwd(proj_a, conv_w, conv_b, bl, t):
    n = bl * t
    nct = CONV_CH // CONV_TILE
    c0 = SSD_WIDTH // CONV_TILE

    def body(u_ref, w_ref, b_ref, o_ref, d_ref):
        w = w_ref[...]
        bias = b_ref[...]

        def chunk(i, carry):
            r0, taps = _conv_taps(u_ref, i)
            pre = bias + w[3:4, :] * taps[0]
            for s in (1, 2, 3):
                pre = pre + w[3 - s:4 - s, :] * taps[s]
            sg = jax.nn.sigmoid(pre)
            o_ref[pl.ds(r0, CONV_ROWS), :] = pre * sg
            d_ref[pl.ds(r0, CONV_ROWS), :] = (sg * (1.0 + pre * (1.0 - sg))).astype(BF16)
            return carry

        lax.fori_loop(0, t // CONV_ROWS, chunk, 0)

    out = pl.BlockSpec((t, CONV_TILE), lambda b, c: (b, c))
    return pl.pallas_call(
        body, name="conv_silu_fwd", grid=(bl, nct),
        in_specs=[pl.BlockSpec((t, CONV_TILE), lambda b, c: (b, c0 + c)),
                  pl.BlockSpec((4, CONV_TILE), lambda b, c: (0, c)),
                  pl.BlockSpec((1, CONV_TILE), lambda b, c: (0, c))],
        out_specs=[out, out],
        out_shape=[jax.ShapeDtypeStruct((n, CONV_CH), F32), jax.ShapeDtypeStruct((n, CONV_CH), BF16)],
        compiler_params=_cparams(("parallel", "parallel")),
    )(proj_a, conv_w, conv_b)


def _conv_bwd(dxc, dsilu, proj_a, conv_w, dpa, bl, t):
    nct = CONV_CH // CONV_TILE
    c0 = SSD_WIDTH // CONV_TILE
    nrc = t // CONV_ROWS

    def body(g_ref, s_ref, u_ref, w_ref, dpa_in, du_ref, dw_ref, db_ref, dp_scr):
        @pl.when(pl.program_id(1) == 0)
        def _():
            dw_ref[...] = jnp.zeros_like(dw_ref)
            db_ref[...] = jnp.zeros_like(db_ref)

        w = w_ref[...]
        dp_scr[pl.ds(t, 8), :] = jnp.zeros((8, CONV_TILE), F32)

        def chunk1(i, carry):
            dw0, dw1, dw2, dw3, db = carry
            r0, taps = _conv_taps(u_ref, i)
            dpre = g_ref[pl.ds(r0, CONV_ROWS), :] * s_ref[pl.ds(r0, CONV_ROWS), :].astype(F32)
            dp_scr[pl.ds(r0, CONV_ROWS), :] = dpre
            dw3 = dw3 + jnp.sum(dpre * taps[0], axis=0, keepdims=True)
            dw2 = dw2 + jnp.sum(dpre * taps[1], axis=0, keepdims=True)
            dw1 = dw1 + jnp.sum(dpre * taps[2], axis=0, keepdims=True)
            dw0 = dw0 + jnp.sum(dpre * taps[3], axis=0, keepdims=True)
            db = db + jnp.sum(dpre, axis=0, keepdims=True)
            return dw0, dw1, dw2, dw3, db

        z = jnp.zeros((1, CONV_TILE), F32)
        dw0, dw1, dw2, dw3, db = lax.fori_loop(0, nrc, chunk1, (z, z, z, z, z))
        dw_ref[...] += jnp.concatenate([dw0, dw1, dw2, dw3], axis=0)
        db_ref[...] += db

        def chunk2(i, carry):
            r0 = pl.multiple_of(i * CONV_ROWS, CONV_ROWS)
            cat = dp_scr[pl.ds(r0, CONV_ROWS + 8), :]
            du = w[3:4, :] * cat[:CONV_ROWS, :]
            for s in (1, 2, 3):
                du = du + w[3 - s:4 - s, :] * pltpu.roll(cat, CONV_ROWS + 8 - s, 0)[:CONV_ROWS, :]
            du_ref[pl.ds(r0, CONV_ROWS), :] = du.astype(BF16)
            return carry

        lax.fori_loop(0, nrc, chunk2, 0)

    tile = pl.BlockSpec((t, CONV_TILE), lambda c, b: (b, c))
    return pl.pallas_call(
        body, name="conv_silu_bwd", grid=(nct, bl),
        in_specs=[tile, tile, pl.BlockSpec((t, CONV_TILE), lambda c, b: (b, c0 + c)),
                  pl.BlockSpec((4, CONV_TILE), lambda c, b: (0, c)), ANY],
        out_specs=[pl.BlockSpec((t, CONV_TILE), lambda c, b: (b, c0 + c)),
                   pl.BlockSpec((4, CONV_TILE), lambda c, b: (0, c)),
                   pl.BlockSpec((1, CONV_TILE), lambda c, b: (0, c))],
        out_shape=[jax.ShapeDtypeStruct(dpa.shape, dpa.dtype), jax.ShapeDtypeStruct((4, CONV_CH), F32),
                   jax.ShapeDtypeStruct((1, CONV_CH), F32)],
        input_output_aliases={4: 0},
        scratch_shapes=[pltpu.VMEM((t + 8, CONV_TILE), F32)],
        compiler_params=_cparams(("parallel", "arbitrary")),
    )(dxc, dsilu, proj_a, conv_w, dpa)


NT_DIMS = (((1,), (1,)), ((), ()))
TN_DIMS = (((0,), (0,)), ((), ()))


def _dot(a, b, dims=None):
    if dims is None:
        return jnp.dot(a, b, preferred_element_type=F32)
    return lax.dot_general(a, b, dims, preferred_element_type=F32)


def _head_expander():
    r = lax.broadcasted_iota(jnp.int32, (128, SSD_WIDTH), 0)
    c = lax.broadcasted_iota(jnp.int32, (128, SSD_WIDTH), 1)
    return ((c // HEAD_DIM == r % 16) & (r < 48)).astype(BF16)


def _spread(v128, expander):
    hi = v128.astype(BF16).astype(F32)
    r1 = v128 - hi
    mid = r1.astype(BF16).astype(F32)
    lo = (r1 - mid).astype(BF16).astype(F32)
    packed = (hi + pltpu.roll(mid, 16, 1) + pltpu.roll(lo, 32, 1)).astype(BF16)
    return jnp.dot(packed, expander, preferred_element_type=F32)


def _head_sums(v1024, expander):
    hi = v1024.astype(BF16)
    lo = (v1024 - hi.astype(F32)).astype(BF16)
    heads = jnp.where(lax.broadcasted_iota(jnp.int32, expander.shape, 0) < 16, expander, jnp.zeros_like(expander))
    return _dot(hi, heads, NT_DIMS) + _dot(lo, heads, NT_DIMS)


def _ssd_fwd(xc, proj_a, dt, acum, acum_t, dskip_e, norm_w, bl, t):
    n = bl * t
    nch = t // CHUNK
    L = CHUNK

    def body(xc_ref, z_ref, dt_ref, ac_ref, act_ref, dsk_ref, nw_ref, ys_ref, yp_ref, hp_ref, h_scr, y_scr, x_scr):
        @pl.when(pl.program_id(1) == 0)
        def _():
            h_scr[...] = jnp.zeros_like(h_scr)

        row = lax.broadcasted_iota(jnp.int32, (L, L), 0)
        col = lax.broadcasted_iota(jnp.int32, (L, L), 1)
        causal = row >= col
        lane128 = lax.broadcasted_iota(jnp.int32, (1, L), 1)
        expander = _head_expander()
        ac_all = ac_ref[...]
        act_all = act_ref[...]
        ac_e = _spread(ac_all, expander)
        e_in = jnp.exp(ac_e)
        dec = jnp.exp(ac_e[L - 1:L, :] - ac_e)
        xs_all = xc_ref[:, 0:SSD_WIDTH]
        x_all = xs_all * _spread(dt_ref[...], expander)
        x_scr[...] = x_all.astype(BF16)
        hp_all = h_scr[...]
        hp_ref[...] = hp_all
        for g in range(2):
            gs = slice(g * 512, (g + 1) * 512)
            bg = xc_ref[:, SSD_WIDTH + g * 128:SSD_WIDTH + (g + 1) * 128].astype(BF16)
            cg = xc_ref[:, SSD_WIDTH + 256 + g * 128:SSD_WIDTH + 256 + (g + 1) * 128].astype(BF16)
            gmat = _dot(cg, bg, NT_DIMS)
            y_off = _dot(cg, hp_all[gs, :].astype(BF16), NT_DIMS) * e_in[:, gs] + dsk_ref[:, gs] * xs_all[:, gs]
            s_new = _dot((x_all[:, gs] * dec[:, gs]).astype(BF16), bg, TN_DIMS)
            for pr in range(4):
                pair = slice((g * 4 + pr) * 128, (g * 4 + pr + 1) * 128)
                x_pair = x_scr[:, pair]
                y_pair = y_off[:, pr * 128:(pr + 1) * 128]
                for j in range(2):
                    h = g * 8 + 2 * pr + j
                    sl = slice(h * HEAD_DIM, (h + 1) * HEAD_DIM)
                    r = 2 * pr + j
                    ldec = jnp.exp(jnp.where(causal, ac_all[:, h:h + 1] - act_all[h:h + 1, :], NEG))
                    x_head = jnp.where((lane128 < HEAD_DIM) == (j == 0), x_pair, jnp.zeros_like(x_pair))
                    y_pair = y_pair + _dot((gmat * ldec).astype(BF16), x_head)
                    elast = jnp.exp(ac_all[L - 1:L, h:h + 1])
                    h_scr[sl, :] = elast * hp_all[sl, :] + s_new[r * HEAD_DIM:(r + 1) * HEAD_DIM, :]
                y_scr[:, pair] = y_pair
        y = y_scr[...]
        yp_ref[...] = y
        zv = z_ref[...]
        yg = y * (zv * jax.nn.sigmoid(zv))
        for g in range(2):
            gs = slice(g * 512, (g + 1) * 512)
            grp = yg[:, gs]
            rstd = lax.rsqrt(jnp.mean(grp * grp, axis=1, keepdims=True) + EPS)
            ys_ref[:, gs] = (grp * rstd * nw_ref[:, gs]).astype(BF16)

    rb = lambda b, c: (b * nch + c, 0)
    v1k = pl.BlockSpec((1, SSD_WIDTH), lambda b, c: (0, 0))
    return pl.pallas_call(
        body, name="ssd_fwd", grid=(bl, nch),
        in_specs=[pl.BlockSpec((L, CONV_CH), rb), pl.BlockSpec((L, SSD_WIDTH), rb),
                  pl.BlockSpec((L, 128), rb), pl.BlockSpec((L, 128), rb),
                  pl.BlockSpec((16, L), lambda b, c: (0, b * nch + c)), v1k, v1k],
        out_specs=[pl.BlockSpec((L, SSD_WIDTH), rb), pl.BlockSpec((L, SSD_WIDTH), rb),
                   pl.BlockSpec((None, SSD_WIDTH, SSD_STATE), lambda b, c: (b * nch + c, 0, 0))],
        out_shape=[jax.ShapeDtypeStruct((n, SSD_WIDTH), BF16), jax.ShapeDtypeStruct((n, SSD_WIDTH), F32),
                   jax.ShapeDtypeStruct((bl * nch, SSD_WIDTH, SSD_STATE), F32)],
        scratch_shapes=[pltpu.VMEM((SSD_WIDTH, SSD_STATE), F32), pltpu.VMEM((L, SSD_WIDTH), F32),
                        pltpu.VMEM((L, SSD_WIDTH), BF16)],
        compiler_params=_cparams(("parallel", "arbitrary")),
    )(xc, proj_a, dt, acum, acum_t, dskip_e, norm_w)


def _ssd_bwd(dys, xc, proj_a, ypre, hprev, dt, gate_d, acum, acum_t, alog128, dskip_e, norm_w, bl, t):
    n = bl * t
    nch = t // CHUNK
    L = CHUNK

    def body(dys_ref, xc_ref, z_ref, yp_ref, hp_ref, dt_ref, gd_ref, ac_ref, act_ref, al_ref, dsk_ref, nw_ref,
             dxc_ref, dz_ref, ddt_ref, dnw_ref, dsk16_ref, da16_ref, db16_ref,
             dh_scr, dy_scr, x_scr, dx_scr, red_scr):
        first = (pl.program_id(0) == 0) & (pl.program_id(1) == 0)

        @pl.when(first)
        def _():
            dnw_ref[...] = jnp.zeros_like(dnw_ref)
            dsk16_ref[...] = jnp.zeros_like(dsk16_ref)
            da16_ref[...] = jnp.zeros_like(da16_ref)
            db16_ref[...] = jnp.zeros_like(db16_ref)

        @pl.when(pl.program_id(1) == 0)
        def _():
            dh_scr[...] = jnp.zeros_like(dh_scr)

        y = yp_ref[...]
        zv = z_ref[...]
        sz = jax.nn.sigmoid(zv)
        gate = zv * sz
        yg = y * gate
        dout = dys_ref[...]
        nw = nw_ref[...]
        for g in range(2):
            gs = slice(g * 512, (g + 1) * 512)
            grp = yg[:, gs]
            rstd = lax.rsqrt(jnp.mean(grp * grp, axis=1, keepdims=True) + EPS)
            ghat = grp * rstd
            dnw_ref[:, gs] += jnp.sum(dout[:, gs] * ghat, axis=0, keepdims=True)
            gw = dout[:, gs] * nw[:, gs]
            dyg = rstd * (gw - ghat * jnp.mean(gw * ghat, axis=1, keepdims=True))
            dy_scr[:, gs] = dyg * gate[:, gs]
            dz_ref[:, gs] = (dyg * y[:, gs] * (sz[:, gs] * (1.0 + zv[:, gs] * (1.0 - sz[:, gs])))).astype(BF16)

        row = lax.broadcasted_iota(jnp.int32, (L, L), 0)
        col = lax.broadcasted_iota(jnp.int32, (L, L), 1)
        causal = row >= col
        lane128 = lax.broadcasted_iota(jnp.int32, (1, L), 1)
        rows128 = lax.broadcasted_iota(jnp.int32, (L, 1), 0)
        last_row = rows128 == (L - 1)
        expander = _head_expander()
        ac_all = ac_ref[...]
        act_all = act_ref[...]
        dt_all = dt_ref[...]
        dt_e = _spread(dt_all, expander)
        ac_e = _spread(ac_all, expander)
        e_in = jnp.exp(ac_e)
        dec = jnp.exp(ac_e[L - 1:L, :] - ac_e)
        xs_all = xc_ref[:, 0:SSD_WIDTH]
        x_all = xs_all * dt_e
        x_scr[...] = x_all.astype(BF16)
        dy_all = dy_scr[...]
        hp_all = hp_ref[...]
        ds_all = dh_scr[...]
        dsk_cols = jnp.sum(dy_all * xs_all, axis=0, keepdims=True)
        dac = jnp.zeros((L, L), F32)
        dac_row = jnp.zeros((L, L), F32)
        ddec_cols = []
        for g in range(2):
            gs = slice(g * 512, (g + 1) * 512)
            bsl = slice(SSD_WIDTH + g * 128, SSD_WIDTH + (g + 1) * 128)
            csl = slice(SSD_WIDTH + 256 + g * 128, SSD_WIDTH + 256 + (g + 1) * 128)
            bg = xc_ref[:, bsl].astype(BF16)
            cg = xc_ref[:, csl].astype(BF16)
            gmat = _dot(cg, bg, NT_DIMS)
            hpb = hp_all[gs, :].astype(BF16)
            dsb = ds_all[gs, :].astype(BF16)
            ch = _dot(cg, hpb, NT_DIMS)
            dye = dy_all[:, gs] * e_in[:, gs]
            dyeb = dye.astype(BF16)
            dc_acc = _dot(dyeb, hpb)
            dhp = _dot(dyeb, cg, TN_DIMS)
            dxd = _dot(bg, dsb, NT_DIMS)
            db_acc = _dot((x_all[:, gs] * dec[:, gs]).astype(BF16), dsb)
            ddec = dxd * x_all[:, gs] * dec[:, gs]
            ddec_cols.append(jnp.sum(ddec, axis=0, keepdims=True))
            dx_inter = dxd * dec[:, gs]
            red_scr[:, gs] = dye * ch - ddec
            dg_sum = jnp.zeros((L, L), F32)
            for pr in range(4):
                pair = slice((g * 4 + pr) * 128, (g * 4 + pr + 1) * 128)
                x_pair = x_scr[:, pair]
                dy_pair = dy_scr[:, pair].astype(BF16)
                dx_pair = dx_inter[:, pr * 128:(pr + 1) * 128]
                for j in range(2):
                    h = g * 8 + 2 * pr + j
                    r = 2 * pr + j
                    sl = slice(h * HEAD_DIM, (h + 1) * HEAD_DIM)
                    onehot_w = lane128 == h
                    ldec = jnp.exp(jnp.where(causal, ac_all[:, h:h + 1] - act_all[h:h + 1, :], NEG))
                    mf = gmat * ldec
                    dyb = jnp.where((lane128 < HEAD_DIM) == (j == 0), dy_pair, jnp.zeros_like(dy_pair))
                    dm = _dot(dyb, x_pair, NT_DIMS)
                    dx_pair = dx_pair + _dot(mf.astype(BF16), dyb, TN_DIMS)
                    dg_sum = dg_sum + dm * ldec
                    wmat = dm * mf
                    elast = jnp.exp(ac_all[L - 1:L, h:h + 1])
                    hp_h = hp_all[sl, :]
                    ds_h = ds_all[sl, :]
                    extra = elast * jnp.sum(jnp.sum(hp_h * ds_h, axis=1, keepdims=True), axis=0, keepdims=True)
                    dac = dac + jnp.where(onehot_w,
                                          jnp.sum(wmat, axis=1, keepdims=True) + jnp.where(last_row, extra, 0.0), 0.0)
                    dac_row = dac_row + jnp.where(rows128 == h, -jnp.sum(wmat, axis=0, keepdims=True), 0.0)
                    dh_scr[sl, :] = elast * ds_h + dhp[r * HEAD_DIM:(r + 1) * HEAD_DIM, :]
                dx_scr[:, pair] = dx_pair
            dgb = dg_sum.astype(BF16)
            dxc_ref[:, csl] = dc_acc + _dot(dgb, bg)
            dxc_ref[:, bsl] = db_acc + _dot(dgb, cg, TN_DIMS)
        dx_all = dx_scr[...]
        dxc_ref[:, 0:SSD_WIDTH] = dx_all * dt_e + dsk_ref[...] * dy_all
        red = red_scr[...]
        dac_slab = _head_sums(red, expander)
        ddec_tot = _head_sums(jnp.broadcast_to(jnp.concatenate(ddec_cols, axis=1), (8, SSD_WIDTH)), expander)
        ddt_x = _head_sums(dx_all * xs_all, expander)
        dsk16_ref[...] += _head_sums(jnp.broadcast_to(dsk_cols, (8, SSD_WIDTH)), expander)[0:1, 0:16]
        dac = dac + dac_slab + jnp.transpose(dac_row) + jnp.where(last_row, ddec_tot[0:1, :], 0.0)
        triu = (row <= col).astype(F32)
        da = jnp.dot(triu, dac, precision=HIGHEST, preferred_element_type=F32)
        a_row = -jnp.exp(al_ref[...])
        ddt = jnp.where(lane128 < 16, (ddt_x + da * a_row) * gd_ref[...], 0.0)
        ddt_ref[...] = ddt
        da16_ref[...] += (jnp.sum(da * dt_all, axis=0, keepdims=True) * a_row)[:, 0:16]
        db16_ref[...] += jnp.sum(ddt, axis=0, keepdims=True)[:, 0:16]

    rb = lambda b, c: (b * nch + nch - 1 - c, 0)
    v1k = pl.BlockSpec((1, SSD_WIDTH), lambda b, c: (0, 0))
    v16 = pl.BlockSpec((1, 16), lambda b, c: (0, 0))
    v128 = pl.BlockSpec((1, 128), lambda b, c: (0, 0))
    wide = pl.BlockSpec((L, SSD_WIDTH), rb)
    s128 = pl.BlockSpec((L, 128), rb)
    return pl.pallas_call(
        body, name="ssd_bwd", grid=(bl, nch),
        in_specs=[wide, pl.BlockSpec((L, CONV_CH), rb), wide, wide,
                  pl.BlockSpec((None, SSD_WIDTH, SSD_STATE), lambda b, c: (b * nch + nch - 1 - c, 0, 0)),
                  s128, s128, s128, pl.BlockSpec((16, L), lambda b, c: (0, b * nch + nch - 1 - c)), v128, v1k, v1k],
        out_specs=[pl.BlockSpec((L, CONV_CH), rb), wide, s128, v1k, v16, v16, v16],
        out_shape=[jax.ShapeDtypeStruct((n, CONV_CH), F32), jax.ShapeDtypeStruct((n, PA_WIDTH), BF16),
                   jax.ShapeDtypeStruct((n, 128), F32), jax.ShapeDtypeStruct((1, SSD_WIDTH), F32),
                   jax.ShapeDtypeStruct((1, 16), F32), jax.ShapeDtypeStruct((1, 16), F32),
                   jax.ShapeDtypeStruct((1, 16), F32)],
        scratch_shapes=[pltpu.VMEM((SSD_WIDTH, SSD_STATE), F32), pltpu.VMEM((L, SSD_WIDTH), F32),
                        pltpu.VMEM((L, SSD_WIDTH), BF16), pltpu.VMEM((L, SSD_WIDTH), F32),
                        pltpu.VMEM((L, SSD_WIDTH), F32)],
        compiler_params=_cparams(("arbitrary", "arbitrary")),
    )(dys, xc, proj_a, ypre, hprev, dt, gate_d, acum, acum_t, alog128, dskip_e, norm_w)


def _attn_fwd(qkv, negc, bl, t):
    n = bl * t
    tb_ = ATT_BLOCK
    nb = t // tb_
    scale2 = LOG2E / math.sqrt(HEAD_DIM)

    def body(q_ref, k_ref, v_ref, c_ref, o_ref, lse_ref, v0_scr, v1_scr, k0_scr, k1_scr):
        row = lax.broadcasted_iota(jnp.int32, (tb_, tb_), 0)
        col = lax.broadcasted_iota(jnp.int32, (tb_, tb_), 1)
        causal = row >= col
        lane = lax.broadcasted_iota(jnp.int32, (1, 128), 1)
        v_pair = v_ref[...].astype(F32)
        k_pair = k_ref[...]
        v_scrs = (v0_scr, v1_scr)
        k_scrs = (k0_scr, k1_scr)
        for j in range(2):
            v_head = v_pair if j == 0 else pltpu.roll(v_pair, HEAD_DIM, 1)
            v_scrs[j][...] = jnp.where(lane < HEAD_DIM, v_head, jnp.where(lane == HEAD_DIM, 1.0, 0.0)).astype(BF16)
            k_scrs[j][...] = jnp.where((lane < HEAD_DIM) == (j == 0), k_pair, jnp.zeros_like(k_pair))
        for qi in range(nb):
            r0, lk = qi * tb_, (qi + 1) * tb_
            for j in range(2):
                sl = slice(j * HEAD_DIM, (j + 1) * HEAD_DIM)
                s = _dot(q_ref[r0:lk, :], k_scrs[j][0:lk, :], NT_DIMS) * scale2 + c_ref[j:j + 1, 0:lk] * LOG2E
                tail = jnp.where(causal, s[:, r0:lk], NEG)
                s = tail if qi == 0 else jnp.concatenate([s[:, 0:r0], tail], axis=1)
                m = jnp.max(s, axis=1, keepdims=True)
                p = jnp.exp2(s - m)
                acc = _dot(p.astype(BF16), v_scrs[j][0:lk, :])
                l = acc[:, HEAD_DIM:HEAD_DIM + 1]
                o_ref[r0:lk, sl] = (acc[:, 0:HEAD_DIM] / l).astype(BF16)
                lse_ref[r0:lk, sl] = jnp.broadcast_to(m + jnp.log(l) * LOG2E, (tb_, HEAD_DIM))

    blk = lambda off: pl.BlockSpec((t, 128), lambda b, hp: (b, off + hp))
    return pl.pallas_call(
        body, name="fox_attn_fwd", grid=(bl, 8),
        in_specs=[blk(0), blk(8), blk(16), pl.BlockSpec((None, None, 8, t), lambda b, hp: (b, hp, 0, 0))],
        out_specs=[blk(0), blk(0)],
        out_shape=[jax.ShapeDtypeStruct((n, ATT_WIDTH), BF16), jax.ShapeDtypeStruct((n, ATT_WIDTH), F32)],
        scratch_shapes=[pltpu.VMEM((t, 128), BF16)] * 4,
        compiler_params=_cparams(("parallel", "parallel")),
    )(qkv, qkv, qkv, negc)


def _attn_bwd(qkv, do, o, lse, negc, after, bl, t):
    n = bl * t
    tb_ = ATT_BLOCK
    nb = t // tb_
    scale = 1.0 / math.sqrt(HEAD_DIM)
    scale2 = LOG2E * scale

    def body(q_ref, k_ref, v_ref, do_ref, o_ref, lse_ref, c_ref, after_ref, dq_ref, dk_ref, dv_ref, dc_ref,
             dq0_scr, delta_scr, dq1_scr, qt0_scr, qt1_scr, dot_scr, dkt0_scr, dkt1_scr, dvt_scr):
        row = lax.broadcasted_iota(jnp.int32, (tb_, tb_), 0)
        col = lax.broadcasted_iota(jnp.int32, (tb_, tb_), 1)
        causal = row >= col
        lane = lax.broadcasted_iota(jnp.int32, (1, 128), 1)
        dq_scrs = (dq0_scr, dq1_scr)
        qt_scrs = (qt0_scr, qt1_scr)
        dkt_scrs = (dkt0_scr, dkt1_scr)
        dq0_scr[...] = jnp.zeros_like(dq0_scr)
        dq1_scr[...] = jnp.zeros_like(dq1_scr)
        dc_ref[...] = jnp.zeros_like(dc_ref)
        q_t = jnp.transpose(q_ref[...].astype(F32))
        ones_row = jnp.where(lax.broadcasted_iota(jnp.int32, (8, t), 0) == 0, 1.0, 0.0)
        for j in range(2):
            qt_scrs[j][...] = jnp.concatenate(
                [q_t[j * HEAD_DIM:(j + 1) * HEAD_DIM, :], ones_row, jnp.zeros((HEAD_DIM - 8, t), F32)],
                axis=0).astype(BF16)
        dot_scr[...] = jnp.transpose(do_ref[...].astype(F32)).astype(BF16)
        prod = do_ref[...].astype(F32) * o_ref[...].astype(F32)
        for j in range(2):
            sl = slice(j * HEAD_DIM, (j + 1) * HEAD_DIM)
            delta_scr[:, sl] = jnp.broadcast_to(jnp.sum(prod[:, sl], axis=1, keepdims=True), (t, HEAD_DIM))
        for kj in range(nb):
            r0, r1 = kj * tb_, (kj + 1) * tb_
            k_blk = k_ref[r0:r1, :]
            v_blk = v_ref[r0:r1, :]
            k_pair = k_blk.astype(F32)
            for j in range(2):
                sl = slice(j * HEAD_DIM, (j + 1) * HEAD_DIM)
                one = slice(j * HEAD_DIM, j * HEAD_DIM + 1)
                own = (lane < HEAD_DIM) == (j == 0)
                k_head = k_pair if j == 0 else pltpu.roll(k_pair, HEAD_DIM, 1)
                k_ones = jnp.where(lane < HEAD_DIM, k_head, jnp.where(lane == HEAD_DIM, 1.0, 0.0)).astype(BF16)
                s = (_dot(q_ref[r0:t, :], jnp.where(own, k_blk, jnp.zeros_like(k_blk)), NT_DIMS) * scale2
                     + c_ref[j:j + 1, r0:r1] * LOG2E)
                head = jnp.where(causal, s[0:tb_, :], NEG)
                s = head if kj == nb - 1 else jnp.concatenate([head, s[tb_:, :]], axis=0)
                p = jnp.exp2(s - lse_ref[r0:t, one])
                dp = _dot(do_ref[r0:t, :], jnp.where(own, v_blk, jnp.zeros_like(v_blk)), NT_DIMS)
                ds = p * (dp - delta_scr[r0:t, one])
                dsb = ds.astype(BF16)
                dvt_scr[sl, r0:r1] = _dot(dot_scr[sl, r0:t], p.astype(BF16))
                dkt_scrs[j][:, r0:r1] = _dot(qt_scrs[j][:, r0:t], dsb)
                dq_scrs[j][r0:t, :] += _dot(dsb, k_ones)
        dv_ref[...] = jnp.transpose(dvt_scr[...]).astype(BF16)
        for j in range(2):
            sl = slice(j * HEAD_DIM, (j + 1) * HEAD_DIM)
            acc = dq_scrs[j][...]
            dkt = dkt_scrs[j][...]
            dq_ref[:, sl] = (acc[:, 0:HEAD_DIM] * scale).astype(BF16)
            dk_ref[:, sl] = (jnp.transpose(dkt)[:, 0:HEAD_DIM] * scale).astype(BF16)
            dc_ref[j:j + 1, :] = jnp.transpose(acc)[HEAD_DIM:HEAD_DIM + 1, :] - dkt[HEAD_DIM:HEAD_DIM + 1, :]

    blk = lambda off: pl.BlockSpec((t, 128), lambda b, hp: (b, off + hp))
    cblk = pl.BlockSpec((None, None, 8, t), lambda b, hp: (b, hp, 0, 0))
    return pl.pallas_call(
        body, name="fox_attn_bwd", grid=(bl, 8),
        in_specs=[blk(0), blk(8), blk(16), blk(0), blk(0), blk(0), cblk, ANY],
        out_specs=[blk(0), blk(0), blk(0), cblk],
        out_shape=[jax.ShapeDtypeStruct((n, ATT_WIDTH), BF16)] * 3 + [jax.ShapeDtypeStruct((bl, 8, 8, t), F32)],
        scratch_shapes=[pltpu.VMEM((t, 128), F32), pltpu.VMEM((t, 128), F32), pltpu.VMEM((t, 128), F32),
                        pltpu.VMEM((128, t), BF16), pltpu.VMEM((128, t), BF16), pltpu.VMEM((128, t), BF16),
                        pltpu.VMEM((128, t), F32), pltpu.VMEM((128, t), F32), pltpu.VMEM((128, t), F32)],
        compiler_params=_cparams(("parallel", "parallel")),
    )(qkv, qkv, qkv, do, o, lse, negc, after)


def _adamw(w, g, m, v, *, name):
    lead = w.ndim == 3
    r, c = w.shape[-2:]
    tr = _pick(r, (256, IN_SHARD // 3, 128, 64, 32, 16, 8))
    bc1 = 1.0 - ADAM_B1 ** ADAM_STEP
    bc2 = 1.0 - ADAM_B2 ** ADAM_STEP

    def body(w_ref, g_ref, m_ref, v_ref, d_ref, nm_ref, nv_ref):
        gv = g_ref[...]
        mn = ADAM_B1 * m_ref[...] + (1.0 - ADAM_B1) * gv
        vn = ADAM_B2 * v_ref[...] + (1.0 - ADAM_B2) * (gv * gv)
        m_hat = mn / bc1
        v_hat = vn / bc2
        d_ref[...] = -ADAM_LR * (m_hat / (jnp.sqrt(v_hat) + ADAM_EPS) + ADAM_WD * w_ref[...])
        nm_ref[...] = mn
        nv_ref[...] = vn

    flat = pl.BlockSpec((tr, c), lambda i: (i, 0))
    blk = pl.BlockSpec((None, tr, c), lambda i: (0, i, 0)) if lead else flat
    return pl.pallas_call(
        body, name=name, grid=(r // tr,), in_specs=[blk, flat, blk, blk], out_specs=[blk] * 3,
        out_shape=[jax.ShapeDtypeStruct(w.shape, F32)] * 3,
        compiler_params=_cparams(("parallel",)),
    )(w, g, m, v)


def _sum_leading(parts, *, name, out_dtype=F32):
    k, r, c = parts.shape
    tr = _pick(r, (512, 256, 128, 96, 64, 32, 16, 8))

    def body(p_ref, o_ref):
        acc = p_ref[0].astype(F32)
        for i in range(1, k):
            acc = acc + p_ref[i].astype(F32)
        o_ref[...] = acc.astype(out_dtype)

    return pl.pallas_call(
        body, name=name, grid=(r // tr,),
        in_specs=[pl.BlockSpec((k, tr, c), lambda i: (0, i, 0))],
        out_specs=pl.BlockSpec((tr, c), lambda i: (i, 0)),
        out_shape=jax.ShapeDtypeStruct((r, c), out_dtype),
        compiler_params=_cparams(("parallel",)),
    )(parts)


def _add_pair(a, b, *, name):
    k, r, c = a.shape
    tr = _pick(r, (512, 256, 128))

    def body(a_ref, b_ref, o_ref):
        o_ref[...] = (a_ref[...].astype(F32) + b_ref[...].astype(F32)).astype(BF16)

    blk = pl.BlockSpec((None, tr, c), lambda j, i: (j, i, 0))
    return pl.pallas_call(
        body, name=name, grid=(k, r // tr), in_specs=[blk, blk], out_specs=blk,
        out_shape=jax.ShapeDtypeStruct((k, r, c), BF16),
        compiler_params=_cparams(("parallel", "parallel")),
    )(a, b)


ANY = pl.BlockSpec(memory_space=pl.ANY)


def _chip_peers(x, y):
    return [(1 - x, y, 2 * (1 - x) + y), (x, 1 - y, 2 * x + 1 - y), (1 - x, 1 - y, 2 * (1 - x) + 1 - y)]


def _gather_weights(blob, *, name):
    rows, cols = blob.shape
    half_rows = rows // 2

    def body(b_ref, o_ref, send_sems, recv_sems):
        x, y, c = lax.axis_index("x"), lax.axis_index("y"), lax.axis_index("c")
        me = 2 * x + y
        sibling = (x, y, 1 - c)
        peers = _chip_peers(x, y)

        def half(chip, hc):
            return o_ref.at[chip, pl.ds(hc * half_rows, half_rows), :]

        def copy(k, src, chip, hc, to):
            return pltpu.make_async_remote_copy(src_ref=src, dst_ref=half(chip, hc), send_sem=send_sems.at[k],
                                                recv_sem=recv_sems.at[k], device_id=to, device_id_type=MESH)

        my_half = b_ref.at[pl.ds(c * half_rows, half_rows), :]
        first = [copy(k, my_half, me, c, (px, py, c)) for k, (px, py, _) in enumerate(peers)]
        for cp in first:
            cp.start()
        passed = [copy(3 + k, half(pc, c), pc, c, sibling) for k, (_, _, pc) in enumerate(peers)]
        for k, (px, py, pc) in enumerate(peers):
            copy(k, my_half, pc, c, (px, py, c)).wait_recv()
            passed[k].start()
        for k, (_, _, pc) in enumerate(peers):
            copy(3 + k, half(pc, 1 - c), pc, 1 - c, sibling).wait_recv()
        for cp in first + passed:
            cp.wait_send()

    return pl.pallas_call(
        body, name=name, in_specs=[ANY], out_specs=ANY,
        out_shape=jax.ShapeDtypeStruct((N_CHIPS, rows, cols), BF16),
        scratch_shapes=[pltpu.SemaphoreType.DMA((6,)), pltpu.SemaphoreType.DMA((6,))],
    )(blob)


def _swap_halves(g, *, name):
    _, rows, cols = g.shape
    half_rows = rows // 2

    def body(g_ref, o_ref, send_sem, recv_sem):
        x, y, c = lax.axis_index("x"), lax.axis_index("y"), lax.axis_index("c")
        cp = pltpu.make_async_remote_copy(
            src_ref=g_ref.at[:, pl.ds((1 - c) * half_rows, half_rows), :], dst_ref=o_ref,
            send_sem=send_sem, recv_sem=recv_sem, device_id=(x, y, 1 - c), device_id_type=MESH)
        cp.start()
        cp.wait()

    return pl.pallas_call(
        body, name=name, in_specs=[ANY], out_specs=ANY,
        out_shape=jax.ShapeDtypeStruct((N_CHIPS, half_rows, cols), BF16),
        scratch_shapes=[pltpu.SemaphoreType.DMA, pltpu.SemaphoreType.DMA],
    )(g)


HBM_SPEC = pl.BlockSpec(memory_space=pltpu.HBM)
SEM_SPEC = pl.BlockSpec(memory_space=pltpu.SEMAPHORE)
SPLIT_EFFECT = pltpu.SideEffectType.DATAFLOW_SIDE_EFFECTING


def _gather_peers_copies(b_ref, land_ref, send_sems, recv_sems, sending):
    x, y, c = lax.axis_index("x"), lax.axis_index("y"), lax.axis_index("c")
    me = 2 * x + y
    half_rows = b_ref.shape[0] // 2
    src = b_ref.at[pl.ds(c * half_rows, half_rows), :]
    return [pltpu.make_async_remote_copy(
        src_ref=src, dst_ref=land_ref.at[me if sending else pc, pl.ds(c * half_rows, half_rows), :],
        send_sem=send_sems.at[k], recv_sem=recv_sems.at[k], device_id=(px, py, c), device_id_type=MESH)
        for k, (px, py, pc) in enumerate(_chip_peers(x, y))]


def _gather_start(blob, after, *, name):
    shape = (N_CHIPS,) + blob.shape

    def body(b_ref, land_ref, after_ref, send_sems, recv_sems, b_thru, land_thru, token):
        for cp in _gather_peers_copies(b_ref, land_ref, send_sems, recv_sems, True):
            cp.start()
        token[...] = jnp.zeros_like(token)

    return pl.pallas_call(
        body, name=name,
        out_shape=(pltpu.SemaphoreType.DMA((3,)), pltpu.SemaphoreType.DMA((3,)), pltpu.HBM(blob.shape, blob.dtype),
                   pltpu.HBM(shape, blob.dtype), jax.ShapeDtypeStruct((8, 128), F32)),
        in_specs=(HBM_SPEC, HBM_SPEC, ANY),
        out_specs=(SEM_SPEC, SEM_SPEC, HBM_SPEC, HBM_SPEC, pl.BlockSpec(memory_space=pltpu.VMEM)),
        input_output_aliases={0: 2, 1: 3},
        compiler_params=pltpu.CompilerParams(has_side_effects=SPLIT_EFFECT),
    )(pltpu.with_memory_space_constraint(blob, pltpu.HBM),
      pltpu.with_memory_space_constraint(lax.empty(shape, blob.dtype), pltpu.HBM), after)


def _gather_wait(send_sems, recv_sems, b_thru, land_thru, after, *, name):
    def body(b_ref, land_ref, send_sems, recv_sems, after_ref, b_dead, got_ref):
        for cp in _gather_peers_copies(b_ref, land_ref, send_sems, recv_sems, False):
            cp.wait_send()
            cp.wait_recv()

    return pl.pallas_call(
        body, name=name,
        out_shape=(pltpu.HBM(b_thru.shape, b_thru.dtype), pltpu.HBM(land_thru.shape, land_thru.dtype)),
        in_specs=(HBM_SPEC, HBM_SPEC, SEM_SPEC, SEM_SPEC, ANY), out_specs=(HBM_SPEC, HBM_SPEC),
        input_output_aliases={0: 0, 1: 1},
        compiler_params=pltpu.CompilerParams(has_side_effects=SPLIT_EFFECT),
    )(b_thru, land_thru, send_sems, recv_sems, after)


def _gather_forward(land, *, name):
    half_rows = land.shape[1] // 2

    def body(l_ref, o_ref, send_sems, recv_sems):
        x, y, c = lax.axis_index("x"), lax.axis_index("y"), lax.axis_index("c")
        cps = []
        for k, (_, _, pc) in enumerate(_chip_peers(x, y)):
            mine = pl.ds(c * half_rows, half_rows)
            cps.append(pltpu.make_async_remote_copy(
                src_ref=l_ref.at[pc, mine, :], dst_ref=o_ref.at[pc, mine, :], send_sem=send_sems.at[k],
                recv_sem=recv_sems.at[k], device_id=(x, y, 1 - c), device_id_type=MESH))
        for cp in cps:
            cp.start()
        for k, (_, _, pc) in enumerate(_chip_peers(x, y)):
            theirs = pl.ds((1 - c) * half_rows, half_rows)
            pltpu.make_async_remote_copy(
                src_ref=l_ref.at[pc, theirs, :], dst_ref=o_ref.at[pc, theirs, :], send_sem=send_sems.at[k],
                recv_sem=recv_sems.at[k], device_id=(x, y, 1 - c), device_id_type=MESH).wait_recv()
        for cp in cps:
            cp.wait_send()

    return pl.pallas_call(
        body, name=name, in_specs=[ANY], out_specs=ANY, input_output_aliases={0: 0},
        out_shape=jax.ShapeDtypeStruct(land.shape, land.dtype),
        scratch_shapes=[pltpu.SemaphoreType.DMA((3,)), pltpu.SemaphoreType.DMA((3,))],
    )(land)


def _exchange_peers_copies(p_ref, land_ref, send_sems, recv_sems, sending):
    x, y, c = lax.axis_index("x"), lax.axis_index("y"), lax.axis_index("c")
    me = 2 * x + y
    return [pltpu.make_async_remote_copy(src_ref=p_ref.at[pc], dst_ref=land_ref.at[me if sending else pc],
                                         send_sem=send_sems.at[k], recv_sem=recv_sems.at[k],
                                         device_id=(px, py, c), device_id_type=MESH)
            for k, (px, py, pc) in enumerate(_chip_peers(x, y))]


def _exchange_start(p, *, name):
    def body(p_ref, land_ref, send_sems, recv_sems, p_thru, land_thru, token):
        for cp in _exchange_peers_copies(p_ref, land_ref, send_sems, recv_sems, True):
            cp.start()
        token[...] = jnp.zeros_like(token)

    return pl.pallas_call(
        body, name=name,
        out_shape=(pltpu.SemaphoreType.DMA((3,)), pltpu.SemaphoreType.DMA((3,)), pltpu.HBM(p.shape, p.dtype),
                   pltpu.HBM(p.shape, p.dtype), jax.ShapeDtypeStruct((8, 128), F32)),
        in_specs=(HBM_SPEC, HBM_SPEC),
        out_specs=(SEM_SPEC, SEM_SPEC, HBM_SPEC, HBM_SPEC, pl.BlockSpec(memory_space=pltpu.VMEM)),
        input_output_aliases={0: 2, 1: 3},
        compiler_params=pltpu.CompilerParams(has_side_effects=SPLIT_EFFECT),
    )(pltpu.with_memory_space_constraint(p, pltpu.HBM),
      pltpu.with_memory_space_constraint(lax.empty(p.shape, p.dtype), pltpu.HBM))


def _exchange_wait(send_sems, recv_sems, p_thru, land_thru, after, *, name):
    def body(p_ref, land_ref, send_sems, recv_sems, after_ref, p_dead, got_ref):
        for cp in _exchange_peers_copies(p_ref, land_ref, send_sems, recv_sems, False):
            cp.wait_send()
            cp.wait_recv()

    return pl.pallas_call(
        body, name=name,
        out_shape=(pltpu.HBM(p_thru.shape, p_thru.dtype), pltpu.HBM(p_thru.shape, p_thru.dtype)),
        in_specs=(HBM_SPEC, HBM_SPEC, SEM_SPEC, SEM_SPEC, ANY), out_specs=(HBM_SPEC, HBM_SPEC),
        input_output_aliases={0: 0, 1: 1},
        compiler_params=pltpu.CompilerParams(has_side_effects=SPLIT_EFFECT),
    )(p_thru, land_thru, send_sems, recv_sems, after)


def _sum_parts(parts, own, *, name):
    k, r, c = parts.shape
    tr = _pick(r, (512, 256, 128))

    def body(p_ref, own_ref, o_ref):
        me = 2 * lax.axis_index("x") + lax.axis_index("y")
        acc = jnp.zeros((tr, c), F32)
        for i in range(k):
            acc = acc + jnp.where(me == i, own_ref[i], p_ref[i]).astype(F32)
        o_ref[...] = acc

    blk = pl.BlockSpec((k, tr, c), lambda i: (0, i, 0))
    return pl.pallas_call(
        body, name=name, grid=(r // tr,), in_specs=[blk, blk],
        out_specs=pl.BlockSpec((tr, c), lambda i: (i, 0)),
        out_shape=jax.ShapeDtypeStruct((r, c), F32),
        compiler_params=_cparams(("parallel",)),
    )(parts, own)


def _join_halves(gh, *, name):
    def body(g_ref, o_ref, send_sem, recv_sem):
        x, y, c = lax.axis_index("x"), lax.axis_index("y"), lax.axis_index("c")
        cp = pltpu.make_async_remote_copy(src_ref=g_ref, dst_ref=o_ref, send_sem=send_sem, recv_sem=recv_sem,
                                          device_id=(x, y, 1 - c), device_id_type=MESH)
        cp.start()
        cp.wait()

    other = pl.pallas_call(
        body, name=name, in_specs=[ANY], out_specs=ANY,
        out_shape=jax.ShapeDtypeStruct(gh.shape, F32),
        scratch_shapes=[pltpu.SemaphoreType.DMA, pltpu.SemaphoreType.DMA],
    )(gh)
    south = lax.axis_index("c") == 0
    return jnp.concatenate([jnp.where(south, gh, other), jnp.where(south, other, gh)], axis=0)


def _gather_small(s, *, name):
    rows = s.shape[0]

    def body(s_ref, o_ref, send_sems, recv_sems, local_sem):
        x, y, c = lax.axis_index("x"), lax.axis_index("y"), lax.axis_index("c")
        me = 4 * x + 2 * y + c
        mine = pltpu.make_async_copy(s_ref, o_ref.at[me], local_sem)
        mine.start()
        peers = []
        for k in range(1, 8):
            peers.append((1 - x if k & 4 else x, 1 - y if k & 2 else y, 1 - c if k & 1 else c))
        cps = [pltpu.make_async_remote_copy(src_ref=s_ref, dst_ref=o_ref.at[me], send_sem=send_sems.at[k],
                                            recv_sem=recv_sems.at[k], device_id=p, device_id_type=MESH)
               for k, p in enumerate(peers)]
        for cp in cps:
            cp.start()
        for k, (px, py, pc) in enumerate(peers):
            pltpu.make_async_remote_copy(src_ref=s_ref, dst_ref=o_ref.at[4 * px + 2 * py + pc],
                                         send_sem=send_sems.at[k], recv_sem=recv_sems.at[k],
                                         device_id=(px, py, pc), device_id_type=MESH).wait_recv()
        for cp in cps:
            cp.wait_send()
        mine.wait()

    return pl.pallas_call(
        body, name=name, in_specs=[ANY], out_specs=ANY,
        out_shape=jax.ShapeDtypeStruct((8, rows, 128), F32),
        scratch_shapes=[pltpu.SemaphoreType.DMA((7,)), pltpu.SemaphoreType.DMA((7,)), pltpu.SemaphoreType.DMA],
    )(s)


IN_SHARD = IN_WIDTH // N_CHIPS
IN_SHARD_PAD = 1536
UP_ROWS, DOWN_ROWS, OUT_ROWS = 1024, 1024, 512
REST_ROWS = UP_ROWS + DOWN_ROWS + OUT_ROWS


def _pack_in(w_in_s):
    return jnp.pad(w_in_s, ((0, 0), (0, IN_SHARD_PAD - IN_SHARD))).astype(BF16)


def _pack_rest(w_out_s, w_up_s, w_down_s):
    return jnp.concatenate([w_up_s, w_down_s, w_out_s], axis=0).astype(BF16)


def _unpack_rest(blob):
    return (blob[UP_ROWS + DOWN_ROWS:], blob[0:UP_ROWS], blob[UP_ROWS:UP_ROWS + DOWN_ROWS])


def _with_own(gathered, own):
    me = 2 * lax.axis_index("x") + lax.axis_index("y")
    return [jnp.where(me == j, own, gathered[j]) for j in range(N_CHIPS)]


def _full_w_in(g_in, own):
    return jnp.concatenate([s[:, :IN_SHARD] for s in _with_own(g_in, own)], axis=1)


def _full_rest(g_rest, own):
    parts = [_unpack_rest(s) for s in _with_own(g_rest, own)]
    w_out = jnp.concatenate([p[0] for p in parts], axis=0)
    w_up = jnp.concatenate([p[1] for p in parts], axis=1)
    w_down = jnp.concatenate([p[2] for p in parts], axis=0)
    return w_out, w_up, w_down


def _split_w_in(w_in):
    z_xbc = w_in[:, 0:2560]
    dt = w_in[:, 2560:2576]
    qkv = w_in[:, 2576:5648]
    f = w_in[:, 5648:5664]
    pad = jnp.zeros((w_in.shape[0], PA_WIDTH - 2592), w_in.dtype)
    return jnp.concatenate([z_xbc, dt, f, pad], axis=1), qkv


def _merge_w_in(d_a, d_qkv):
    return jnp.concatenate([d_a[:, 0:2560], d_a[:, 2560:2576], d_qkv, d_a[:, 2576:2592]], axis=1)


def _local_step(x3, target3, w_in, rest_weights, norm_mix_w, conv_w, conv_b, dt_bias, a_log, d_skip,
                ssd_norm_w, f_bias, norm_mlp_w, norm_final_w, first_after=None, early_grads=None, late_grads=None):
    bl, t, d = x3.shape
    n = bl * t
    x = x3.reshape(n, d)
    target = target3.reshape(n, d)
    w_a, w_qkv = _split_w_in(w_in)
    nfw = norm_final_w.reshape(1, d)
    dskip_e = jnp.repeat(d_skip, HEAD_DIM, axis=1)
    nb = t // ATT_BLOCK

    r1, r2, kt = min(n, 1024), min(n, 512), min(n, 1024)
    if first_after is None:
        first_after = jnp.zeros((8, 128), F32)
    h0, rstd0, proj_a = _norm_mm(x, norm_mix_w, w_a, first_after, name="norm_mix_proj_a", tm=r2)
    qkv = _mm(h0, w_qkv, name="proj_qkv", tiles=(r2, QKV_WIDTH, D_MODEL), out_dtype=BF16)
    bias128 = jnp.concatenate([dt_bias, f_bias, jnp.zeros((1, 96), F32)], axis=1)
    alog128 = jnp.concatenate([a_log, jnp.zeros((1, 112), F32)], axis=1)
    dt, gate_d, acum, acum_t, negc = _prep(proj_a, bias128, alog128, bl, t)
    xc, dsilu = _conv_fwd(proj_a, conv_w, conv_b, bl, t)
    y_ssd, y_pre, hprev = _ssd_fwd(xc, proj_a, dt, acum, acum_t, dskip_e, ssd_norm_w, bl, t)
    y_att, lse = _attn_fwd(qkv, negc, bl, t)
    w_out, w_up, w_down = rest_weights(y_att)
    wo_s, wo_a = w_out[:SSD_WIDTH], w_out[SSD_WIDTH:]
    h1, h1n, rstd1 = _mm_norm_fwd(y_ssd, wo_s, y_att, wo_a, x, norm_mlp_w, name="out_proj_norm_mlp", tm=r2)
    up = _mm(h1n, w_up, name="mlp_up", tiles=(r2, D_FF, D_MODEL), out_dtype=BF16)
    dh2, dh2b, loss, d_nfw = _mm_final(up, w_down, h1, nfw, target, name="mlp_down_final_norm_loss", tm=r2,
                                       a_act="relu2")

    dup = _mm(dh2b, w_down, name="mlp_down_bwd_act", tiles=(r2, D_FF, D_MODEL), tb=True, epi_up=up, out_dtype=BF16)
    rest_shape = (N_CHIPS, REST_ROWS, D_MODEL)
    gb_rest = _mm(up, dh2b, name="mlp_down_bwd_w", tiles=(DOWN_ROWS, D_MODEL, kt), ta=True, a_act="relu2",
                  out_dtype=BF16, into=(rest_shape, (None, DOWN_ROWS, D_MODEL), lambda i, j, k: (i, 1, 0), None))
    dh1, dh1b, d_nmlp = _mm_norm_bwd([(dup, w_up)], h1, rstd1, norm_mlp_w, dh2, name="mlp_up_bwd_act_norm_mlp",
                                     tm=r2)
    gb_rest = _mm(h1n, dup, name="mlp_up_bwd_w", tiles=(D_MODEL, UP_ROWS, kt), ta=True, out_dtype=BF16,
                  into=(rest_shape, (None, D_MODEL, UP_ROWS), lambda i, j, k: (j, 0, 0), gb_rest))
    dys, do = _mm_two_halves(dh1b, w_out, name="out_proj_bwd_act", tm=r1)
    out_block = (UP_ROWS + DOWN_ROWS) // OUT_ROWS
    for half, (y_half, tag) in enumerate(((y_ssd, "ssd"), (y_att, "att"))):
        gb_rest = _mm(y_half, dh1b, name="out_proj_bwd_w_" + tag, tiles=(2 * OUT_ROWS, D_MODEL, kt), ta=True,
                      out_dtype=BF16, into=(rest_shape, (2, OUT_ROWS, D_MODEL),
                                            functools.partial(lambda i, j, k, h: (h, out_block, 0), h=half),
                                            gb_rest))
    token = jnp.zeros((8, 128), F32) if early_grads is None else early_grads(gb_rest)
    dq, dk, dv, dcb = _attn_bwd(qkv, do, y_att, lse, negc, token, bl, t)
    dc = jnp.pad(dcb[:, :, 0:2, :].transpose(0, 3, 1, 2).reshape(n, 16), ((0, 0), (16, 96)))
    dxc, dpa, ddt_raw, d_snw, d_dsk, d_alog, d_dtb = _ssd_bwd(dys, xc, proj_a, y_pre, hprev, dt, gate_d, acum,
                                                             acum_t, alog128, dskip_e, ssd_norm_w, bl, t)
    dpa, d_conv_w, d_conv_b = _conv_bwd(dxc, dsilu, proj_a, conv_w, dpa, bl, t)
    dproj_a, d_fb = _fpost(dc, gate_d, ddt_raw, dpa, bl, t)
    dqkv = jnp.concatenate([dq, dk, dv], axis=1)
    d_w_a = _mm(h0, dproj_a, name="proj_a_bwd_w", tiles=(1024, 896, kt), ta=True, out_dtype=BF16)
    d_w_qkv = _mm(h0, dqkv, name="proj_qkv_bwd_w", tiles=(1024, 1024, kt), ta=True, out_dtype=BF16)
    d_w_in = _merge_w_in(d_w_a, d_w_qkv)
    late_token = None if late_grads is None else late_grads(d_w_in)
    dx, _, d_nmix = _mm_norm_bwd([(dproj_a, w_a), (dqkv, w_qkv)], x, rstd0, norm_mix_w, dh1,
                                 name="proj_bwd_act_norm_mix", tm=min(n, 256), after=late_token)

    grads = dict(norm_mix_w=d_nmix, w_in=d_w_in, conv_w=d_conv_w, conv_b=d_conv_b,
                 dt_bias=d_dtb, a_log=d_alog, d_skip=d_dsk, ssd_norm_w=d_snw, f_bias=d_fb, rest=gb_rest,
                 norm_mlp_w=d_nmlp, norm_final_w=d_nfw)
    return dx.reshape(bl, t, d), loss, grads


SMALL_ORDER = ("norm_mix_w", "conv_w", "conv_b", "dt_bias", "a_log", "d_skip", "ssd_norm_w", "f_bias",
               "norm_mlp_w", "norm_final_w")
SMALL_SIZES = (1024, 4 * CONV_CH, CONV_CH, 16, 16, 16, 1024, 16, 1024, 1024)


def _pack_small(vals, rows):
    flat = jnp.concatenate([v.reshape(-1).astype(F32) for v in vals])
    return jnp.pad(flat, (0, rows * 128 - flat.shape[0])).reshape(rows, 128)


def _unpack_small(packed, sizes):
    flat = packed.reshape(-1)
    out, o = [], 0
    for s in sizes:
        out.append(flat[o:o + s])
        o += s
    return out


def kernel(x, norm_mix_w, w_in, conv_w, conv_b, dt_bias, a_log, d_skip, ssd_norm_w, f_bias, w_out, norm_mlp_w, w_up, w_down, norm_final_w, loss_target, m_norm_mix_w, m_w_in, m_conv_w, m_conv_b, m_dt_bias, m_a_log, m_d_skip, m_ssd_norm_w, m_f_bias, m_w_out, m_norm_mlp_w, m_w_up, m_w_down, m_norm_final_w, v_norm_mix_w, v_w_in, v_conv_w, v_conv_b, v_dt_bias, v_a_log, v_d_skip, v_ssd_norm_w, v_f_bias, v_w_out, v_norm_mlp_w, v_w_up, v_w_down, v_norm_final_w):
    chip = 2 * lax.axis_index("x") + lax.axis_index("y")
    cw = CONV_CH // N_CHIPS

    own_in = _pack_in(w_in[0])
    own_rest = _pack_rest(w_out[0], w_up[0], w_down[0])
    g_in = _gather_weights(own_in, name="gather_w_in")
    w_in_f = _full_w_in(g_in, own_in)
    *rest_handles, rest_token = _gather_start(own_rest, g_in, name="gather_start_rest")

    def rest_weights(after):
        _, landed = _gather_wait(*rest_handles, after, name="gather_wait_rest")
        return _full_rest(_gather_forward(landed, name="gather_forward_rest"), own_rest)
    small_all = _gather_small(_pack_small([conv_w[0]], 16), name="gather_conv_w")
    conv_w_f = jnp.concatenate([small_all[2 * j].reshape(-1)[:4 * cw].reshape(4, cw) for j in range(N_CHIPS)], axis=1)

    c = lax.axis_index("c")

    def chip_partial(gb, tag):
        half_rows = gb.shape[1] // 2
        from_sibling = _swap_halves(gb, name="grad_swap_halves_" + tag)
        my_half = lax.dynamic_slice_in_dim(gb, c * half_rows, half_rows, axis=1)
        return _add_pair(my_half, from_sibling, name="grad_add_sibling_" + tag)

    in_flight = {}

    def early_grads(gb_rest):
        part = chip_partial(gb_rest, "rest")
        *handles, token = _exchange_start(part, name="grad_exchange_start_rest")
        in_flight["rest"] = handles
        return token

    def late_grads(d_w_in):
        gb_in = jnp.stack([_pack_in(d_w_in[:, j * IN_SHARD:(j + 1) * IN_SHARD]) for j in range(N_CHIPS)])
        *handles, token = _exchange_start(chip_partial(gb_in, "in"), name="grad_exchange_start_in")
        in_flight["in"] = handles
        return token

    dx, loss_part, g = _local_step(x, loss_target, w_in_f, rest_weights, norm_mix_w, conv_w_f,
                                   conv_b, dt_bias, a_log, d_skip, ssd_norm_w, f_bias, norm_mlp_w, norm_final_w,
                                   first_after=rest_token, early_grads=early_grads, late_grads=late_grads)

    send_sems, recv_sems, part_rest, land_rest = in_flight["rest"]
    part_rest, parts_rest = _exchange_wait(send_sems, recv_sems, part_rest, land_rest, dx,
                                           name="grad_exchange_wait_rest")
    g_rest_half = _sum_parts(parts_rest, part_rest, name="grad_sum_chips_rest")
    g_w_out, g_w_up, g_w_down = _unpack_rest(_join_halves(g_rest_half, name="grad_join_halves_rest"))

    part_in, parts_in = _exchange_wait(*in_flight["in"], dx, name="grad_exchange_wait_in")
    g_in_half = _sum_parts(parts_in, part_in, name="grad_sum_chips_in")
    g_w_in = _join_halves(g_in_half, name="grad_join_halves_in")[:, :IN_SHARD]

    small_vals = [g[k] for k in SMALL_ORDER] + [loss_part[:, 0:1]]
    small_sum = _sum_leading(_gather_small(_pack_small(small_vals, SMALL_ROWS), name="gather_small_grads"), name="small_sum")
    sg = dict(zip(SMALL_ORDER + ("loss",), _unpack_small(small_sum, SMALL_SIZES + (1,))))
    loss = sg["loss"].reshape(())
    g_conv_full = sg["conv_w"].reshape(4, CONV_CH)
    g_conv = lax.dynamic_slice_in_dim(g_conv_full, chip * cw, cw, axis=1)

    grads = dict(norm_mix_w=sg["norm_mix_w"].reshape(1, -1), w_in=g_w_in[None], conv_w=g_conv[None],
                 conv_b=sg["conv_b"].reshape(1, -1), dt_bias=sg["dt_bias"].reshape(1, -1),
                 a_log=sg["a_log"].reshape(1, -1), d_skip=sg["d_skip"].reshape(1, -1),
                 ssd_norm_w=sg["ssd_norm_w"].reshape(1, -1), f_bias=sg["f_bias"].reshape(1, -1), w_out=g_w_out[None],
                 norm_mlp_w=sg["norm_mlp_w"].reshape(1, -1), w_up=g_w_up[None], w_down=g_w_down[None],
                 norm_final_w=sg["norm_final_w"])
    weights = dict(norm_mix_w=norm_mix_w, w_in=w_in, conv_w=conv_w, conv_b=conv_b, dt_bias=dt_bias, a_log=a_log,
                   d_skip=d_skip, ssd_norm_w=ssd_norm_w, f_bias=f_bias, w_out=w_out, norm_mlp_w=norm_mlp_w,
                   w_up=w_up, w_down=w_down, norm_final_w=norm_final_w)
    ms = dict(norm_mix_w=m_norm_mix_w, w_in=m_w_in, conv_w=m_conv_w, conv_b=m_conv_b, dt_bias=m_dt_bias,
              a_log=m_a_log, d_skip=m_d_skip, ssd_norm_w=m_ssd_norm_w, f_bias=m_f_bias, w_out=m_w_out,
              norm_mlp_w=m_norm_mlp_w, w_up=m_w_up, w_down=m_w_down, norm_final_w=m_norm_final_w)
    vs = dict(norm_mix_w=v_norm_mix_w, w_in=v_w_in, conv_w=v_conv_w, conv_b=v_conv_b, dt_bias=v_dt_bias,
              a_log=v_a_log, d_skip=v_d_skip, ssd_norm_w=v_ssd_norm_w, f_bias=v_f_bias, w_out=v_w_out,
              norm_mlp_w=v_norm_mlp_w, w_up=v_w_up, w_down=v_w_down, norm_final_w=v_norm_final_w)
    names = list(weights)
    big = ("w_in", "w_out", "w_up", "w_down")
    delta, new_m, new_v = {}, {}, {}
    for k, g2 in zip(big[1:], (g_w_out, g_w_up, g_w_down)):
        delta[k], new_m[k], new_v[k] = _adamw(weights[k], g2, ms[k], vs[k], name="adamw_" + k)
    g_in_t = g_w_in.T
    outs_t = _adamw(w_in[0].T, g_in_t, m_w_in[0].T, v_w_in[0].T, name="adamw_w_in")
    delta["w_in"], new_m["w_in"], new_v["w_in"] = [o.T[None] for o in outs_t]
    grads["w_in"] = g_in_t.T[None]
    smalls = [k for k in names if k not in big]
    sizes = [math.prod(weights[k].shape) for k in smalls]
    rows = -(-sum(sizes) // 1024) * 8
    packs = [_pack_small([d[k] for k in smalls], rows) for d in (weights, grads, ms, vs)]
    outs = _adamw(*packs, name="adamw_small")
    for o, dst in zip(outs, (delta, new_m, new_v)):
        for k, val in zip(smalls, _unpack_small(o, sizes)):
            dst[k] = val.reshape(weights[k].shape)
    return (loss, dx, *[grads[k] for k in names], *[delta[k] for k in names], *[new_m[k] for k in names],
            *[new_v[k] for k in names])
```

```python
import functools
import math

import jax
import jax.numpy as jnp
from jax import lax
from jax.experimental import pallas as pl
from jax.experimental.pallas import tpu as pltpu

F32 = jnp.float32
BF16 = jnp.bfloat16
HIGHEST = lax.Precision.HIGHEST
MESH = pl.DeviceIdType.MESH

D_MODEL = 1024
SSD_HEADS = 16
HEAD_DIM = 64
SSD_WIDTH = 1024
SSD_STATE = 128
CONV_CH = 1536
CHUNK = 128
ATT_WIDTH = 1024
EPS = 1e-5
IN_WIDTH = 5664
PA_WIDTH = 2688
QKV_WIDTH = 3072
D_FF = 4096
ATT_BLOCK = 256
NEG = -1e30
LOG2E = 1.4426950408889634
VMEM_LIMIT = 48 * 1024 * 1024

ADAM_LR = 0.001
ADAM_B1 = 0.9
ADAM_B2 = 0.999
ADAM_EPS = 1e-08
ADAM_WD = 0.01
ADAM_STEP = 10

N_CHIPS = 4
SMALL_ROWS = 96


def _cparams(sem):
    return pltpu.CompilerParams(dimension_semantics=sem, vmem_limit_bytes=VMEM_LIMIT)


def _pick(n, cands):
    for c in cands:
        if n % c == 0:
            return c
    return n


MM_CHUNK = 512


def _mm(a, b, *, name, tiles, ta=False, tb=False, out_dtype=F32, res=None, a_act=None, epi_up=None, after=None,
        into=None):
    n_unread = (after is not None) + (into is not None and into[3] is not None)
    if ta:
        K, M = a.shape
    else:
        M, K = a.shape
    if tb:
        N, K2 = b.shape
    else:
        K2, N = b.shape
    assert K == K2, (a.shape, b.shape)
    tm, tn, tk = tiles
    assert M % tm == 0 and N % tn == 0 and K % tk == 0, (name, M, N, K, tiles)
    nk = K // tk
    dn = (((0 if ta else 1,), (1 if tb else 0,)), ((), ()))
    has_res = res is not None
    has_up = epi_up is not None
    cn = _pick(tn, (MM_CHUNK, 384, 256, 128))

    def prologue(av):
        if a_act == "relu2":
            r = jnp.maximum(av.astype(F32), 0.0)
            av = r * r
        return av.astype(BF16)

    def epilogue(out, res_v, up_v):
        if has_res:
            out = out + res_v.astype(F32)
        if has_up:
            out = out * (2.0 * jnp.maximum(up_v.astype(F32), 0.0))
        return out.astype(out_dtype)

    def body(*refs):
        a_ref, b_ref = refs[0], refs[1]
        i = 2
        res_ref = up_ref = None
        if has_res:
            res_ref = refs[i]
            i += 1
        if has_up:
            up_ref = refs[i]
            i += 1
        i += n_unread
        o_ref = refs[i]
        if nk == 1:
            av = prologue(a_ref[...])
            for c in range(tn // cn):
                cs = slice(c * cn, (c + 1) * cn)
                bv = (b_ref[cs, :] if tb else b_ref[:, cs]).astype(BF16)
                out = lax.dot_general(av, bv, dn, preferred_element_type=F32)
                o_ref[:, cs] = epilogue(out, res_ref[:, cs] if has_res else None, up_ref[:, cs] if has_up else None)
            return
        acc_ref = refs[i + 1]
        k = pl.program_id(2)

        @pl.when(k == 0)
        def _():
            acc_ref[...] = jnp.zeros_like(acc_ref)

        acc_ref[...] += lax.dot_general(prologue(a_ref[...]), b_ref[...].astype(BF16), dn,
                                        preferred_element_type=F32)

        @pl.when(k == nk - 1)
        def _():
            out = epilogue(acc_ref[...], res_ref[...] if has_res else None, up_ref[...] if has_up else None)
            o_ref[...] = out.reshape(o_ref.shape)

    a_spec = pl.BlockSpec((tk, tm), lambda i, j, k: (k, i)) if ta else pl.BlockSpec((tm, tk), lambda i, j, k: (i, k))
    b_spec = pl.BlockSpec((tn, tk), lambda i, j, k: (j, k)) if tb else pl.BlockSpec((tk, tn), lambda i, j, k: (k, j))
    o_spec = pl.BlockSpec((tm, tn), lambda i, j, k: (i, j))
    ins, specs = [a, b], [a_spec, b_spec]
    if has_res:
        ins.append(res)
        specs.append(o_spec)
    if has_up:
        ins.append(epi_up)
        specs.append(o_spec)
    if after is not None:
        ins.append(after)
        specs.append(pl.BlockSpec(memory_space=pl.ANY))
    out_shape, out_spec, aliases = jax.ShapeDtypeStruct((M, N), out_dtype), o_spec, {}
    if into is not None:
        shape, block, index, buf = into
        out_shape, out_spec = jax.ShapeDtypeStruct(shape, out_dtype), pl.BlockSpec(block, index)
        if buf is not None:
            aliases = {len(ins): 0}
            ins.append(buf)
            specs.append(pl.BlockSpec(memory_space=pl.ANY))
    return pl.pallas_call(
        body, name=name, grid=(M // tm, N // tn, nk),
        in_specs=specs, out_specs=out_spec, out_shape=out_shape, input_output_aliases=aliases,
        scratch_shapes=[] if nk == 1 else [pltpu.VMEM((tm, tn), F32)],
        compiler_params=_cparams(("parallel", "parallel", "arbitrary")),
    )(*ins)


def _rows_product(a_ref, b_ref, tb, a_act):
    av = a_ref[...]
    if a_act == "relu2":
        r = jnp.maximum(av.astype(F32), 0.0)
        av = r * r
    dn = (((1,), (1 if tb else 0,)), ((), ()))
    return lax.dot_general(av.astype(BF16), b_ref[...].astype(BF16), dn, preferred_element_type=F32)


def _norm_mm(x, w, b, after, *, name, tm):
    m, d = x.shape
    n = b.shape[1]
    cn = _pick(n, (MM_CHUNK, 384, 256, 128))

    def body(x_ref, w_ref, b_ref, after_ref, h_ref, r_ref, o_ref):
        xv = x_ref[...]
        rstd = lax.rsqrt(jnp.mean(xv * xv, axis=1, keepdims=True) + EPS)
        hv = (xv * rstd * w_ref[...]).astype(BF16)
        h_ref[...] = hv
        r_ref[...] = rstd
        for c in range(n // cn):
            cs = slice(c * cn, (c + 1) * cn)
            o_ref[:, cs] = jnp.dot(hv, b_ref[:, cs].astype(BF16), preferred_element_type=F32)

    row = pl.BlockSpec((tm, d), lambda i: (i, 0))
    return pl.pallas_call(
        body, name=name, grid=(m // tm,),
        in_specs=[row, pl.BlockSpec((1, d), lambda i: (0, 0)), pl.BlockSpec((d, n), lambda i: (0, 0)),
                  pl.BlockSpec(memory_space=pl.ANY)],
        out_specs=[row, pl.BlockSpec((tm, 1), lambda i: (i, 0)), pl.BlockSpec((tm, n), lambda i: (i, 0))],
        out_shape=[jax.ShapeDtypeStruct((m, d), BF16), jax.ShapeDtypeStruct((m, 1), F32),
                   jax.ShapeDtypeStruct((m, n), F32)],
        compiler_params=_cparams(("parallel",)),
    )(x, w, b, after)


def _mm_norm_fwd(a1, b1, a2, b2, res, w, *, name, tm):
    m, k1 = a1.shape
    k2 = a2.shape[1]
    d = b1.shape[1]

    def body(a1_ref, b1_ref, a2_ref, b2_ref, res_ref, w_ref, h_ref, y_ref, r_ref):
        hv = _rows_product(a1_ref, b1_ref, False, None) + _rows_product(a2_ref, b2_ref, False, None) + res_ref[...]
        rstd = lax.rsqrt(jnp.mean(hv * hv, axis=1, keepdims=True) + EPS)
        h_ref[...] = hv
        y_ref[...] = (hv * rstd * w_ref[...]).astype(BF16)
        r_ref[...] = rstd

    row = pl.BlockSpec((tm, d), lambda i: (i, 0))
    return pl.pallas_call(
        body, name=name, grid=(m // tm,),
        in_specs=[pl.BlockSpec((tm, k1), lambda i: (i, 0)), pl.BlockSpec((k1, d), lambda i: (0, 0)),
                  pl.BlockSpec((tm, k2), lambda i: (i, 0)), pl.BlockSpec((k2, d), lambda i: (0, 0)), row,
                  pl.BlockSpec((1, d), lambda i: (0, 0))],
        out_specs=[row, row, pl.BlockSpec((tm, 1), lambda i: (i, 0))],
        out_shape=[jax.ShapeDtypeStruct((m, d), F32), jax.ShapeDtypeStruct((m, d), BF16),
                   jax.ShapeDtypeStruct((m, 1), F32)],
        compiler_params=_cparams(("parallel",)),
    )(a1, b1, a2, b2, res, w)


def _mm_final(a, b, res, w, target, *, name, tm, a_act):
    m, k = a.shape
    d = b.shape[1]

    def body(a_ref, b_ref, res_ref, w_ref, t_ref, dh_ref, dhb_ref, loss_ref, dw_ref):
        @pl.when(pl.program_id(0) == 0)
        def _():
            loss_ref[...] = jnp.zeros_like(loss_ref)
            dw_ref[...] = jnp.zeros_like(dw_ref)

        hv = _rows_product(a_ref, b_ref, False, a_act) + res_ref[...]
        wv = w_ref[...]
        rstd = lax.rsqrt(jnp.mean(hv * hv, axis=1, keepdims=True) + EPS)
        xhat = hv * rstd
        err = xhat * wv - t_ref[...]
        loss_ref[...] += 0.5 * jnp.sum(jnp.mean(err * err, axis=1, keepdims=True), axis=0, keepdims=True)
        dy = err * (1.0 / d)
        gw = dy * wv
        dh = rstd * (gw - xhat * jnp.mean(gw * xhat, axis=1, keepdims=True))
        dh_ref[...] = dh
        dhb_ref[...] = dh.astype(BF16)
        dw_ref[...] += jnp.sum(dy * xhat, axis=0, keepdims=True)

    row = pl.BlockSpec((tm, d), lambda i: (i, 0))
    vec = pl.BlockSpec((1, d), lambda i: (0, 0))
    return pl.pallas_call(
        body, name=name, grid=(m // tm,),
        in_specs=[pl.BlockSpec((tm, k), lambda i: (i, 0)), pl.BlockSpec((k, d), lambda i: (0, 0)), row, vec, row],
        out_specs=[row, row, pl.BlockSpec((1, 128), lambda i: (0, 0)), vec],
        out_shape=[jax.ShapeDtypeStruct((m, d), F32), jax.ShapeDtypeStruct((m, d), BF16),
                   jax.ShapeDtypeStruct((1, 128), F32), jax.ShapeDtypeStruct((1, d), F32)],
        compiler_params=_cparams(("arbitrary",)),
    )(a, b, res, w, target)


def _mm_two_halves(a, b, *, name, tm):
    m, k = a.shape
    d = b.shape[0] // 2

    def body(a_ref, b_ref, lo_ref, hi_ref):
        av = a_ref[...].astype(BF16)
        lo_ref[...] = lax.dot_general(av, b_ref[0:d, :].astype(BF16), NT_DIMS, preferred_element_type=F32)
        hi_ref[...] = lax.dot_general(av, b_ref[d:2 * d, :].astype(BF16), NT_DIMS,
                                      preferred_element_type=F32).astype(BF16)

    row = pl.BlockSpec((tm, d), lambda i: (i, 0))
    return pl.pallas_call(
        body, name=name, grid=(m // tm,),
        in_specs=[pl.BlockSpec((tm, k), lambda i: (i, 0)), pl.BlockSpec((2 * d, k), lambda i: (0, 0))],
        out_specs=[row, row],
        out_shape=[jax.ShapeDtypeStruct((m, d), F32), jax.ShapeDtypeStruct((m, d), BF16)],
        compiler_params=_cparams(("parallel",)),
    )(a, b)


def _mm_norm_bwd(pairs, x, rstd, w, dres, *, name, tm, after=None):
    m = pairs[0][0].shape[0]
    d = pairs[0][1].shape[0]
    n_pairs = len(pairs)

    def body(*refs):
        i = 2 * n_pairs
        x_ref, r_ref, w_ref, d_ref = refs[i:i + 4]
        dx_ref, dxb_ref, dw_ref = refs[-3:]

        @pl.when(pl.program_id(0) == 0)
        def _():
            dw_ref[...] = jnp.zeros_like(dw_ref)

        g = _rows_product(refs[0], refs[1], True, None)
        for p in range(1, n_pairs):
            g = g + _rows_product(refs[2 * p], refs[2 * p + 1], True, None)
        r = r_ref[...]
        xhat = x_ref[...] * r
        gw = g * w_ref[...]
        dx = d_ref[...] + r * (gw - xhat * jnp.mean(gw * xhat, axis=1, keepdims=True))
        dx_ref[...] = dx
        dxb_ref[...] = dx.astype(BF16)
        dw_ref[...] += jnp.sum(g * xhat, axis=0, keepdims=True)

    row = pl.BlockSpec((tm, d), lambda i: (i, 0))
    vec = pl.BlockSpec((1, d), lambda i: (0, 0))
    ins, specs = [], []
    for a, b in pairs:
        k = a.shape[1]
        ins += [a, b]
        specs += [pl.BlockSpec((tm, k), lambda i: (i, 0)), pl.BlockSpec((d, k), lambda i: (0, 0))]
    ins += [x, rstd, w, dres]
    specs += [row, pl.BlockSpec((tm, 1), lambda i: (i, 0)), vec, row]
    if after is not None:
        ins.append(after)
        specs.append(pl.BlockSpec(memory_space=pl.ANY))
    return pl.pallas_call(
        body, name=name, grid=(m // tm,), in_specs=specs, out_specs=[row, row, vec],
        out_shape=[jax.ShapeDtypeStruct((m, d), F32), jax.ShapeDtypeStruct((m, d), BF16),
                   jax.ShapeDtypeStruct((1, d), F32)],
        compiler_params=_cparams(("arbitrary",)),
    )(*ins)


def _softplus(x):
    return jnp.maximum(x, 0.0) + jnp.log(1.0 + jnp.exp(-jnp.abs(x)))


def _prep(proj_a, bias128, alog128, bl, t):
    n = bl * t
    nch = t // CHUNK
    col0 = (SSD_WIDTH + CONV_CH) // 128

    def body(p_ref, b_ref, al_ref, dt_ref, gd_ref, ac_ref, act_ref, negc_ref):
        negc_ref[...] = jnp.zeros_like(negc_ref)
        row = lax.broadcasted_iota(jnp.int32, (CHUNK, CHUNK), 0)
        col = lax.broadcasted_iota(jnp.int32, (CHUNK, CHUNK), 1)
        tril = (row >= col).astype(F32)
        lane = lax.broadcasted_iota(jnp.int32, (1, 128), 1)
        head_lanes = lane < 16
        a_row = -jnp.exp(al_ref[...])
        carry = jnp.zeros((1, 128), F32)
        for ci in range(nch):
            rows = slice(ci * CHUNK, (ci + 1) * CHUNK)
            xv = p_ref[rows, :] + b_ref[...]
            sp = _softplus(xv)
            acum = jnp.dot(tril, a_row * sp, precision=HIGHEST, preferred_element_type=F32)
            c = jnp.dot(tril, -_softplus(-xv), precision=HIGHEST, preferred_element_type=F32) + carry
            carry = c[CHUNK - 1:CHUNK, :]
            dt_ref[rows, :] = jnp.where(head_lanes, sp, 0.0)
            gd_ref[rows, :] = jnp.where(head_lanes, jax.nn.sigmoid(xv),
                                        jnp.where(lane < 32, jax.nn.sigmoid(-xv), 0.0))
            ac_ref[rows, :] = jnp.where(head_lanes, acum, 0.0)
            act_ref[:, rows] = jnp.transpose(acum)[0:16, :]
            c_t = jnp.transpose(c)
            for hp in range(8):
                negc_ref[hp, 0:2, rows] = -c_t[16 + 2 * hp:18 + 2 * hp, :]

    o128 = pl.BlockSpec((t, 128), lambda b: (b, 0))
    v128 = pl.BlockSpec((1, 128), lambda b: (0, 0))
    w128 = jax.ShapeDtypeStruct((n, 128), F32)
    return pl.pallas_call(
        body, name="head_scalars", grid=(bl,),
        in_specs=[pl.BlockSpec((t, 128), lambda b: (b, col0)), v128, v128],
        out_specs=[o128, o128, o128, pl.BlockSpec((16, t), lambda b: (0, b)),
                   pl.BlockSpec((None, 8, 8, t), lambda b: (b, 0, 0, 0))],
        out_shape=[w128, w128, w128, jax.ShapeDtypeStruct((16, n), F32),
                   jax.ShapeDtypeStruct((bl, 8, 8, t), F32)],
        compiler_params=_cparams(("parallel",)),
    )(proj_a, bias128, alog128)


def _fpost(dc, gate_d, ddt, dpa, bl, t):
    n = bl * t
    nch = t // CHUNK
    col0 = (SSD_WIDTH + CONV_CH) // 128

    def body(dc_ref, gd_ref, ddt_ref, dpa_in, out_ref, db_ref):
        @pl.when(pl.program_id(0) == 0)
        def _():
            db_ref[...] = jnp.zeros_like(db_ref)

        row = lax.broadcasted_iota(jnp.int32, (CHUNK, CHUNK), 0)
        col = lax.broadcasted_iota(jnp.int32, (CHUNK, CHUNK), 1)
        triu = (row <= col).astype(F32)
        lane = lax.broadcasted_iota(jnp.int32, (1, 128), 1)
        gate_lanes = (lane >= 16) & (lane < 32)
        carry = jnp.zeros((1, 128), F32)
        db = jnp.zeros((1, 128), F32)
        for ci in reversed(range(nch)):
            rows = slice(ci * CHUNK, (ci + 1) * CHUNK)
            dlf = jnp.dot(triu, dc_ref[rows, :], precision=HIGHEST, preferred_element_type=F32) + carry
            carry = dlf[0:1, :]
            df = jnp.where(gate_lanes, dlf * gd_ref[rows, :], 0.0)
            out_ref[rows, :] = (ddt_ref[rows, :] + df).astype(BF16)
            db = db + jnp.sum(df, axis=0, keepdims=True)
        db_ref[...] += db[:, 16:32]

    blk = pl.BlockSpec((t, 128), lambda b: (b, 0))
    return pl.pallas_call(
        body, name="forget_gate_bwd", grid=(bl,),
        in_specs=[blk, blk, blk, ANY],
        out_specs=[pl.BlockSpec((t, 128), lambda b: (b, col0)), pl.BlockSpec((1, 16), lambda b: (0, 0))],
        out_shape=[jax.ShapeDtypeStruct(dpa.shape, dpa.dtype), jax.ShapeDtypeStruct((1, 16), F32)],
        input_output_aliases={3: 0},
        compiler_params=_cparams(("arbitrary",)),
    )(dc, gate_d, ddt, dpa)


CONV_TILE = 256
CONV_ROWS = 256


def _conv_taps(u_ref, i):
    r0 = pl.multiple_of(i * CONV_ROWS, CONV_ROWS)
    cur = u_ref[pl.ds(r0, CONV_ROWS), :]
    p0 = pl.multiple_of(jnp.maximum(r0 - 8, 0), 8)
    prev = jnp.where(i > 0, u_ref[pl.ds(p0, 8), :], 0.0)
    cat = jnp.concatenate([prev, cur], axis=0)
    return r0, [cur] + [pltpu.roll(cat, s, 0)[8:, :] for s in (1, 2, 3)]


def _conv_fwd(proj_a, conv_w, conv_b, bl, t):
    n = bl * t
    nct = CONV_CH // CONV_TILE
    c0 = SSD_WIDTH // CONV_TILE

    def body(u_ref, w_ref, b_ref, o_ref, d_ref):
        w = w_ref[...]
        bias = b_ref[...]

        def chunk(i, carry):
            r0, taps = _conv_taps(u_ref, i)
            pre = bias + w[3:4, :] * taps[0]
            for s in (1, 2, 3):
                pre = pre + w[3 - s:4 - s, :] * taps[s]
            sg = jax.nn.sigmoid(pre)
            o_ref[pl.ds(r0, CONV_ROWS), :] = pre * sg
            d_ref[pl.ds(r0, CONV_ROWS), :] = (sg * (1.0 + pre * (1.0 - sg))).astype(BF16)
            return carry

        lax.fori_loop(0, t // CONV_ROWS, chunk, 0)

    out = pl.BlockSpec((t, CONV_TILE), lambda b, c: (b, c))
    return pl.pallas_call(
        body, name="conv_silu_fwd", grid=(bl, nct),
        in_specs=[pl.BlockSpec((t, CONV_TILE), lambda b, c: (b, c0 + c)),
                  pl.BlockSpec((4, CONV_TILE), lambda b, c: (0, c)),
                  pl.BlockSpec((1, CONV_TILE), lambda b, c: (0, c))],
        out_specs=[out, out],
        out_shape=[jax.ShapeDtypeStruct((n, CONV_CH), F32), jax.ShapeDtypeStruct((n, CONV_CH), BF16)],
        compiler_params=_cparams(("parallel", "parallel")),
    )(proj_a, conv_w, conv_b)


def _conv_bwd(dxc, dsilu, proj_a, conv_w, dpa, bl, t):
    nct = CONV_CH // CONV_TILE
    c0 = SSD_WIDTH // CONV_TILE
    nrc = t // CONV_ROWS

    def body(g_ref, s_ref, u_ref, w_ref, dpa_in, du_ref, dw_ref, db_ref, dp_scr):
        @pl.when(pl.program_id(1) == 0)
        def _():
            dw_ref[...] = jnp.zeros_like(dw_ref)
            db_ref[...] = jnp.zeros_like(db_ref)

        w = w_ref[...]
        dp_scr[pl.ds(t, 8), :] = jnp.zeros((8, CONV_TILE), F32)

        def chunk1(i, carry):
            dw0, dw1, dw2, dw3, db = carry
            r0, taps = _conv_taps(u_ref, i)
            dpre = g_ref[pl.ds(r0, CONV_ROWS), :] * s_ref[pl.ds(r0, CONV_ROWS), :].astype(F32)
            dp_scr[pl.ds(r0, CONV_ROWS), :] = dpre
            dw3 = dw3 + jnp.sum(dpre * taps[0], axis=0, keepdims=True)
            dw2 = dw2 + jnp.sum(dpre * taps[1], axis=0, keepdims=True)
            dw1 = dw1 + jnp.sum(dpre * taps[2], axis=0, keepdims=True)
            dw0 = dw0 + jnp.sum(dpre * taps[3], axis=0, keepdims=True)
            db = db + jnp.sum(dpre, axis=0, keepdims=True)
            return dw0, dw1, dw2, dw3, db

        z = jnp.zeros((1, CONV_TILE), F32)
        dw0, dw1, dw2, dw3, db = lax.fori_loop(0, nrc, chunk1, (z, z, z, z, z))
        dw_ref[...] += jnp.concatenate([dw0, dw1, dw2, dw3], axis=0)
        db_ref[...] += db

        def chunk2(i, carry):
            r0 = pl.multiple_of(i * CONV_ROWS, CONV_ROWS)
            cat = dp_scr[pl.ds(r0, CONV_ROWS + 8), :]
            du = w[3:4, :] * cat[:CONV_ROWS, :]
            for s in (1, 2, 3):
                du = du + w[3 - s:4 - s, :] * pltpu.roll(cat, CONV_ROWS + 8 - s, 0)[:CONV_ROWS, :]
            du_ref[pl.ds(r0, CONV_ROWS), :] = du.astype(BF16)
            return carry

        lax.fori_loop(0, nrc, chunk2, 0)

    tile = pl.BlockSpec((t, CONV_TILE), lambda c, b: (b, c))
    return pl.pallas_call(
        body, name="conv_silu_bwd", grid=(nct, bl),
        in_specs=[tile, tile, pl.BlockSpec((t, CONV_TILE), lambda c, b: (b, c0 + c)),
                  pl.BlockSpec((4, CONV_TILE), lambda c, b: (0, c)), ANY],
        out_specs=[pl.BlockSpec((t, CONV_TILE), lambda c, b: (b, c0 + c)),
                   pl.BlockSpec((4, CONV_TILE), lambda c, b: (0, c)),
                   pl.BlockSpec((1, CONV_TILE), lambda c, b: (0, c))],
        out_shape=[jax.ShapeDtypeStruct(dpa.shape, dpa.dtype), jax.ShapeDtypeStruct((4, CONV_CH), F32),
                   jax.ShapeDtypeStruct((1, CONV_CH), F32)],
        input_output_aliases={4: 0},
        scratch_shapes=[pltpu.VMEM((t + 8, CONV_TILE), F32)],
        compiler_params=_cparams(("parallel", "arbitrary")),
    )(dxc, dsilu, proj_a, conv_w, dpa)


NT_DIMS = (((1,), (1,)), ((), ()))
TN_DIMS = (((0,), (0,)), ((), ()))


def _dot(a, b, dims=None):
    if dims is None:
        return jnp.dot(a, b, preferred_element_type=F32)
    return lax.dot_general(a, b, dims, preferred_element_type=F32)


def _head_expander():
    r = lax.broadcasted_iota(jnp.int32, (128, SSD_WIDTH), 0)
    c = lax.broadcasted_iota(jnp.int32, (128, SSD_WIDTH), 1)
    return ((c // HEAD_DIM == r % 16) & (r < 48)).astype(BF16)


def _spread(v128, expander):
    hi = v128.astype(BF16).astype(F32)
    r1 = v128 - hi
    mid = r1.astype(BF16).astype(F32)
    lo = (r1 - mid).astype(BF16).astype(F32)
    packed = (hi + pltpu.roll(mid, 16, 1) + pltpu.roll(lo, 32, 1)).astype(BF16)
    return jnp.dot(packed, expander, preferred_element_type=F32)


def _head_sums(v1024, expander):
    hi = v1024.astype(BF16)
    lo = (v1024 - hi.astype(F32)).astype(BF16)
    heads = jnp.where(lax.broadcasted_iota(jnp.int32, expander.shape, 0) < 16, expander, jnp.zeros_like(expander))
    return _dot(hi, heads, NT_DIMS) + _dot(lo, heads, NT_DIMS)


def _ssd_fwd(xc, proj_a, dt, acum, acum_t, dskip_e, norm_w, bl, t):
    n = bl * t
    nch = t // CHUNK
    L = CHUNK

    def body(xc_ref, z_ref, dt_ref, ac_ref, act_ref, dsk_ref, nw_ref, ys_ref, yp_ref, hp_ref, h_scr, y_scr, x_scr):
        @pl.when(pl.program_id(1) == 0)
        def _():
            h_scr[...] = jnp.zeros_like(h_scr)

        row = lax.broadcasted_iota(jnp.int32, (L, L), 0)
        col = lax.broadcasted_iota(jnp.int32, (L, L), 1)
        causal = row >= col
        lane128 = lax.broadcasted_iota(jnp.int32, (1, L), 1)
        expander = _head_expander()
        ac_all = ac_ref[...]
        act_all = act_ref[...]
        ac_e = _spread(ac_all, expander)
        e_in = jnp.exp(ac_e)
        dec = jnp.exp(ac_e[L - 1:L, :] - ac_e)
        xs_all = xc_ref[:, 0:SSD_WIDTH]
        x_all = xs_all * _spread(dt_ref[...], expander)
        x_scr[...] = x_all.astype(BF16)
        hp_all = h_scr[...]
        hp_ref[...] = hp_all
        for g in range(2):
            gs = slice(g * 512, (g + 1) * 512)
            bg = xc_ref[:, SSD_WIDTH + g * 128:SSD_WIDTH + (g + 1) * 128].astype(BF16)
            cg = xc_ref[:, SSD_WIDTH + 256 + g * 128:SSD_WIDTH + 256 + (g + 1) * 128].astype(BF16)
            gmat = _dot(cg, bg, NT_DIMS)
            y_off = _dot(cg, hp_all[gs, :].astype(BF16), NT_DIMS) * e_in[:, gs] + dsk_ref[:, gs] * xs_all[:, gs]
            s_new = _dot((x_all[:, gs] * dec[:, gs]).astype(BF16), bg, TN_DIMS)
            for pr in range(4):
                pair = slice((g * 4 + pr) * 128, (g * 4 + pr + 1) * 128)
                x_pair = x_scr[:, pair]
                y_pair = y_off[:, pr * 128:(pr + 1) * 128]
                for j in range(2):
                    h = g * 8 + 2 * pr + j
                    sl = slice(h * HEAD_DIM, (h + 1) * HEAD_DIM)
                    r = 2 * pr + j
                    ldec = jnp.exp(jnp.where(causal, ac_all[:, h:h + 1] - act_all[h:h + 1, :], NEG))
                    x_head = jnp.where((lane128 < HEAD_DIM) == (j == 0), x_pair, jnp.zeros_like(x_pair))
                    y_pair = y_pair + _dot((gmat * ldec).astype(BF16), x_head)
                    elast = jnp.exp(ac_all[L - 1:L, h:h + 1])
                    h_scr[sl, :] = elast * hp_all[sl, :] + s_new[r * HEAD_DIM:(r + 1) * HEAD_DIM, :]
                y_scr[:, pair] = y_pair
        y = y_scr[...]
        yp_ref[...] = y
        zv = z_ref[...]
        yg = y * (zv * jax.nn.sigmoid(zv))
        for g in range(2):
            gs = slice(g * 512, (g + 1) * 512)
            grp = yg[:, gs]
            rstd = lax.rsqrt(jnp.mean(grp * grp, axis=1, keepdims=True) + EPS)
            ys_ref[:, gs] = (grp * rstd * nw_ref[:, gs]).astype(BF16)

    rb = lambda b, c: (b * nch + c, 0)
    v1k = pl.BlockSpec((1, SSD_WIDTH), lambda b, c: (0, 0))
    return pl.pallas_call(
        body, name="ssd_fwd", grid=(bl, nch),
        in_specs=[pl.BlockSpec((L, CONV_CH), rb), pl.BlockSpec((L, SSD_WIDTH), rb),
                  pl.BlockSpec((L, 128), rb), pl.BlockSpec((L, 128), rb),
                  pl.BlockSpec((16, L), lambda b, c: (0, b * nch + c)), v1k, v1k],
        out_specs=[pl.BlockSpec((L, SSD_WIDTH), rb), pl.BlockSpec((L, SSD_WIDTH), rb),
                   pl.BlockSpec((None, SSD_WIDTH, SSD_STATE), lambda b, c: (b * nch + c, 0, 0))],
        out_shape=[jax.ShapeDtypeStruct((n, SSD_WIDTH), BF16), jax.ShapeDtypeStruct((n, SSD_WIDTH), F32),
                   jax.ShapeDtypeStruct((bl * nch, SSD_WIDTH, SSD_STATE), F32)],
        scratch_shapes=[pltpu.VMEM((SSD_WIDTH, SSD_STATE), F32), pltpu.VMEM((L, SSD_WIDTH), F32),
                        pltpu.VMEM((L, SSD_WIDTH), BF16)],
        compiler_params=_cparams(("parallel", "arbitrary")),
    )(xc, proj_a, dt, acum, acum_t, dskip_e, norm_w)


def _ssd_bwd(dys, xc, proj_a, ypre, hprev, dt, gate_d, acum, acum_t, alog128, dskip_e, norm_w, bl, t):
    n = bl * t
    nch = t // CHUNK
    L = CHUNK

    def body(dys_ref, xc_ref, z_ref, yp_ref, hp_ref, dt_ref, gd_ref, ac_ref, act_ref, al_ref, dsk_ref, nw_ref,
             dxc_ref, dz_ref, ddt_ref, dnw_ref, dsk16_ref, da16_ref, db16_ref,
             dh_scr, dy_scr, x_scr, dx_scr, red_scr):
        first = (pl.program_id(0) == 0) & (pl.program_id(1) == 0)

        @pl.when(first)
        def _():
            dnw_ref[...] = jnp.zeros_like(dnw_ref)
            dsk16_ref[...] = jnp.zeros_like(dsk16_ref)
            da16_ref[...] = jnp.zeros_like(da16_ref)
            db16_ref[...] = jnp.zeros_like(db16_ref)

        @pl.when(pl.program_id(1) == 0)
        def _():
            dh_scr[...] = jnp.zeros_like(dh_scr)

        y = yp_ref[...]
        zv = z_ref[...]
        sz = jax.nn.sigmoid(zv)
        gate = zv * sz
        yg = y * gate
        dout = dys_ref[...]
        nw = nw_ref[...]
        for g in range(2):
            gs = slice(g * 512, (g + 1) * 512)
            grp = yg[:, gs]
            rstd = lax.rsqrt(jnp.mean(grp * grp, axis=1, keepdims=True) + EPS)
            ghat = grp * rstd
            dnw_ref[:, gs] += jnp.sum(dout[:, gs] * ghat, axis=0, keepdims=True)
            gw = dout[:, gs] * nw[:, gs]
            dyg = rstd * (gw - ghat * jnp.mean(gw * ghat, axis=1, keepdims=True))
            dy_scr[:, gs] = dyg * gate[:, gs]
            dz_ref[:, gs] = (dyg * y[:, gs] * (sz[:, gs] * (1.0 + zv[:, gs] * (1.0 - sz[:, gs])))).astype(BF16)

        row = lax.broadcasted_iota(jnp.int32, (L, L), 0)
        col = lax.broadcasted_iota(jnp.int32, (L, L), 1)
        causal = row >= col
        lane128 = lax.broadcasted_iota(jnp.int32, (1, L), 1)
        rows128 = lax.broadcasted_iota(jnp.int32, (L, 1), 0)
        last_row = rows128 == (L - 1)
        expander = _head_expander()
        ac_all = ac_ref[...]
        act_all = act_ref[...]
        dt_all = dt_ref[...]
        dt_e = _spread(dt_all, expander)
        ac_e = _spread(ac_all, expander)
        e_in = jnp.exp(ac_e)
        dec = jnp.exp(ac_e[L - 1:L, :] - ac_e)
        xs_all = xc_ref[:, 0:SSD_WIDTH]
        x_all = xs_all * dt_e
        x_scr[...] = x_all.astype(BF16)
        dy_all = dy_scr[...]
        hp_all = hp_ref[...]
        ds_all = dh_scr[...]
        dsk_cols = jnp.sum(dy_all * xs_all, axis=0, keepdims=True)
        dac = jnp.zeros((L, L), F32)
        dac_row = jnp.zeros((L, L), F32)
        ddec_cols = []
        for g in range(2):
            gs = slice(g * 512, (g + 1) * 512)
            bsl = slice(SSD_WIDTH + g * 128, SSD_WIDTH + (g + 1) * 128)
            csl = slice(SSD_WIDTH + 256 + g * 128, SSD_WIDTH + 256 + (g + 1) * 128)
            bg = xc_ref[:, bsl].astype(BF16)
            cg = xc_ref[:, csl].astype(BF16)
            gmat = _dot(cg, bg, NT_DIMS)
            hpb = hp_all[gs, :].astype(BF16)
            dsb = ds_all[gs, :].astype(BF16)
            ch = _dot(cg, hpb, NT_DIMS)
            dye = dy_all[:, gs] * e_in[:, gs]
            dyeb = dye.astype(BF16)
            dc_acc = _dot(dyeb, hpb)
            dhp = _dot(dyeb, cg, TN_DIMS)
            dxd = _dot(bg, dsb, NT_DIMS)
            db_acc = _dot((x_all[:, gs] * dec[:, gs]).astype(BF16), dsb)
            ddec = dxd * x_all[:, gs] * dec[:, gs]
            ddec_cols.append(jnp.sum(ddec, axis=0, keepdims=True))
            dx_inter = dxd * dec[:, gs]
            red_scr[:, gs] = dye * ch - ddec
            dg_sum = jnp.zeros((L, L), F32)
            for pr in range(4):
                pair = slice((g * 4 + pr) * 128, (g * 4 + pr + 1) * 128)
                x_pair = x_scr[:, pair]
                dy_pair = dy_scr[:, pair].astype(BF16)
                dx_pair = dx_inter[:, pr * 128:(pr + 1) * 128]
                for j in range(2):
                    h = g * 8 + 2 * pr + j
                    r = 2 * pr + j
                    sl = slice(h * HEAD_DIM, (h + 1) * HEAD_DIM)
                    onehot_w = lane128 == h
                    ldec = jnp.exp(jnp.where(causal, ac_all[:, h:h + 1] - act_all[h:h + 1, :], NEG))
                    mf = gmat * ldec
                    dyb = jnp.where((lane128 < HEAD_DIM) == (j == 0), dy_pair, jnp.zeros_like(dy_pair))
                    dm = _dot(dyb, x_pair, NT_DIMS)
                    dx_pair = dx_pair + _dot(mf.astype(BF16), dyb, TN_DIMS)
                    dg_sum = dg_sum + dm * ldec
                    wmat = dm * mf
                    elast = jnp.exp(ac_all[L - 1:L, h:h + 1])
                    hp_h = hp_all[sl, :]
                    ds_h = ds_all[sl, :]
                    extra = elast * jnp.sum(jnp.sum(hp_h * ds_h, axis=1, keepdims=True), axis=0, keepdims=True)
                    dac = dac + jnp.where(onehot_w,
                                          jnp.sum(wmat, axis=1, keepdims=True) + jnp.where(last_row, extra, 0.0), 0.0)
                    dac_row = dac_row + jnp.where(rows128 == h, -jnp.sum(wmat, axis=0, keepdims=True), 0.0)
                    dh_scr[sl, :] = elast * ds_h + dhp[r * HEAD_DIM:(r + 1) * HEAD_DIM, :]
                dx_scr[:, pair] = dx_pair
            dgb = dg_sum.astype(BF16)
            dxc_ref[:, csl] = dc_acc + _dot(dgb, bg)
            dxc_ref[:, bsl] = db_acc + _dot(dgb, cg, TN_DIMS)
        dx_all = dx_scr[...]
        dxc_ref[:, 0:SSD_WIDTH] = dx_all * dt_e + dsk_ref[...] * dy_all
        red = red_scr[...]
        dac_slab = _head_sums(red, expander)
        ddec_tot = _head_sums(jnp.broadcast_to(jnp.concatenate(ddec_cols, axis=1), (8, SSD_WIDTH)), expander)
        ddt_x = _head_sums(dx_all * xs_all, expander)
        dsk16_ref[...] += _head_sums(jnp.broadcast_to(dsk_cols, (8, SSD_WIDTH)), expander)[0:1, 0:16]
        dac = dac + dac_slab + jnp.transpose(dac_row) + jnp.where(last_row, ddec_tot[0:1, :], 0.0)
        triu = (row <= col).astype(F32)
        da = jnp.dot(triu, dac, precision=HIGHEST, preferred_element_type=F32)
        a_row = -jnp.exp(al_ref[...])
        ddt = jnp.where(lane128 < 16, (ddt_x + da * a_row) * gd_ref[...], 0.0)
        ddt_ref[...] = ddt
        da16_ref[...] += (jnp.sum(da * dt_all, axis=0, keepdims=True) * a_row)[:, 0:16]
        db16_ref[...] += jnp.sum(ddt, axis=0, keepdims=True)[:, 0:16]

    rb = lambda b, c: (b * nch + nch - 1 - c, 0)
    v1k = pl.BlockSpec((1, SSD_WIDTH), lambda b, c: (0, 0))
    v16 = pl.BlockSpec((1, 16), lambda b, c: (0, 0))
    v128 = pl.BlockSpec((1, 128), lambda b, c: (0, 0))
    wide = pl.BlockSpec((L, SSD_WIDTH), rb)
    s128 = pl.BlockSpec((L, 128), rb)
    return pl.pallas_call(
        body, name="ssd_bwd", grid=(bl, nch),
        in_specs=[wide, pl.BlockSpec((L, CONV_CH), rb), wide, wide,
                  pl.BlockSpec((None, SSD_WIDTH, SSD_STATE), lambda b, c: (b * nch + nch - 1 - c, 0, 0)),
                  s128, s128, s128, pl.BlockSpec((16, L), lambda b, c: (0, b * nch + nch - 1 - c)), v128, v1k, v1k],
        out_specs=[pl.BlockSpec((L, CONV_CH), rb), wide, s128, v1k, v16, v16, v16],
        out_shape=[jax.ShapeDtypeStruct((n, CONV_CH), F32), jax.ShapeDtypeStruct((n, PA_WIDTH), BF16),
                   jax.ShapeDtypeStruct((n, 128), F32), jax.ShapeDtypeStruct((1, SSD_WIDTH), F32),
                   jax.ShapeDtypeStruct((1, 16), F32), jax.ShapeDtypeStruct((1, 16), F32),
                   jax.ShapeDtypeStruct((1, 16), F32)],
        scratch_shapes=[pltpu.VMEM((SSD_WIDTH, SSD_STATE), F32), pltpu.VMEM((L, SSD_WIDTH), F32),
                        pltpu.VMEM((L, SSD_WIDTH), BF16), pltpu.VMEM((L, SSD_WIDTH), F32),
                        pltpu.VMEM((L, SSD_WIDTH), F32)],
        compiler_params=_cparams(("arbitrary", "arbitrary")),
    )(dys, xc, proj_a, ypre, hprev, dt, gate_d, acum, acum_t, alog128, dskip_e, norm_w)


def _attn_fwd(qkv, negc, bl, t):
    n = bl * t
    tb_ = ATT_BLOCK
    nb = t // tb_
    scale2 = LOG2E / math.sqrt(HEAD_DIM)

    def body(q_ref, k_ref, v_ref, c_ref, o_ref, lse_ref, v0_scr, v1_scr, k0_scr, k1_scr):
        row = lax.broadcasted_iota(jnp.int32, (tb_, tb_), 0)
        col = lax.broadcasted_iota(jnp.int32, (tb_, tb_), 1)
        causal = row >= col
        lane = lax.broadcasted_iota(jnp.int32, (1, 128), 1)
        v_pair = v_ref[...].astype(F32)
        k_pair = k_ref[...]
        v_scrs = (v0_scr, v1_scr)
        k_scrs = (k0_scr, k1_scr)
        for j in range(2):
            v_head = v_pair if j == 0 else pltpu.roll(v_pair, HEAD_DIM, 1)
            v_scrs[j][...] = jnp.where(lane < HEAD_DIM, v_head, jnp.where(lane == HEAD_DIM, 1.0, 0.0)).astype(BF16)
            k_scrs[j][...] = jnp.where((lane < HEAD_DIM) == (j == 0), k_pair, jnp.zeros_like(k_pair))
        for qi in range(nb):
            r0, lk = qi * tb_, (qi + 1) * tb_
            for j in range(2):
                sl = slice(j * HEAD_DIM, (j + 1) * HEAD_DIM)
                s = _dot(q_ref[r0:lk, :], k_scrs[j][0:lk, :], NT_DIMS) * scale2 + c_ref[j:j + 1, 0:lk] * LOG2E
                tail = jnp.where(causal, s[:, r0:lk], NEG)
                s = tail if qi == 0 else jnp.concatenate([s[:, 0:r0], tail], axis=1)
                m = jnp.max(s, axis=1, keepdims=True)
                p = jnp.exp2(s - m)
                acc = _dot(p.astype(BF16), v_scrs[j][0:lk, :])
                l = acc[:, HEAD_DIM:HEAD_DIM + 1]
                o_ref[r0:lk, sl] = (acc[:, 0:HEAD_DIM] / l).astype(BF16)
                lse_ref[r0:lk, sl] = jnp.broadcast_to(m + jnp.log(l) * LOG2E, (tb_, HEAD_DIM))

    blk = lambda off: pl.BlockSpec((t, 128), lambda b, hp: (b, off + hp))
    return pl.pallas_call(
        body, name="fox_attn_fwd", grid=(bl, 8),
        in_specs=[blk(0), blk(8), blk(16), pl.BlockSpec((None, None, 8, t), lambda b, hp: (b, hp, 0, 0))],
        out_specs=[blk(0), blk(0)],
        out_shape=[jax.ShapeDtypeStruct((n, ATT_WIDTH), BF16), jax.ShapeDtypeStruct((n, ATT_WIDTH), F32)],
        scratch_shapes=[pltpu.VMEM((t, 128), BF16)] * 4,
        compiler_params=_cparams(("parallel", "parallel")),
    )(qkv, qkv, qkv, negc)


def _attn_bwd(qkv, do, o, lse, negc, after, bl, t):
    n = bl * t
    tb_ = ATT_BLOCK
    nb = t // tb_
    scale = 1.0 / math.sqrt(HEAD_DIM)
    scale2 = LOG2E * scale

    def body(q_ref, k_ref, v_ref, do_ref, o_ref, lse_ref, c_ref, after_ref, dq_ref, dk_ref, dv_ref, dc_ref,
             dq0_scr, delta_scr, dq1_scr, qt0_scr, qt1_scr, dot_scr, dkt0_scr, dkt1_scr, dvt_scr):
        row = lax.broadcasted_iota(jnp.int32, (tb_, tb_), 0)
        col = lax.broadcasted_iota(jnp.int32, (tb_, tb_), 1)
        causal = row >= col
        lane = lax.broadcasted_iota(jnp.int32, (1, 128), 1)
        dq_scrs = (dq0_scr, dq1_scr)
        qt_scrs = (qt0_scr, qt1_scr)
        dkt_scrs = (dkt0_scr, dkt1_scr)
        dq0_scr[...] = jnp.zeros_like(dq0_scr)
        dq1_scr[...] = jnp.zeros_like(dq1_scr)
        dc_ref[...] = jnp.zeros_like(dc_ref)
        q_t = jnp.transpose(q_ref[...].astype(F32))
        ones_row = jnp.where(lax.broadcasted_iota(jnp.int32, (8, t), 0) == 0, 1.0, 0.0)
        for j in range(2):
            qt_scrs[j][...] = jnp.concatenate(
                [q_t[j * HEAD_DIM:(j + 1) * HEAD_DIM, :], ones_row, jnp.zeros((HEAD_DIM - 8, t), F32)],
                axis=0).astype(BF16)
        dot_scr[...] = jnp.transpose(do_ref[...].astype(F32)).astype(BF16)
        prod = do_ref[...].astype(F32) * o_ref[...].astype(F32)
        for j in range(2):
            sl = slice(j * HEAD_DIM, (j + 1) * HEAD_DIM)
            delta_scr[:, sl] = jnp.broadcast_to(jnp.sum(prod[:, sl], axis=1, keepdims=True), (t, HEAD_DIM))
        for kj in range(nb):
            r0, r1 = kj * tb_, (kj + 1) * tb_
            k_blk = k_ref[r0:r1, :]
            v_blk = v_ref[r0:r1, :]
            k_pair = k_blk.astype(F32)
            for j in range(2):
                sl = slice(j * HEAD_DIM, (j + 1) * HEAD_DIM)
                one = slice(j * HEAD_DIM, j * HEAD_DIM + 1)
                own = (lane < HEAD_DIM) == (j == 0)
                k_head = k_pair if j == 0 else pltpu.roll(k_pair, HEAD_DIM, 1)
                k_ones = jnp.where(lane < HEAD_DIM, k_head, jnp.where(lane == HEAD_DIM, 1.0, 0.0)).astype(BF16)
                s = (_dot(q_ref[r0:t, :], jnp.where(own, k_blk, jnp.zeros_like(k_blk)), NT_DIMS) * scale2
                     + c_ref[j:j + 1, r0:r1] * LOG2E)
                head = jnp.where(causal, s[0:tb_, :], NEG)
                s = head if kj == nb - 1 else jnp.concatenate([head, s[tb_:, :]], axis=0)
                p = jnp.exp2(s - lse_ref[r0:t, one])
                dp = _dot(do_ref[r0:t, :], jnp.where(own, v_blk, jnp.zeros_like(v_blk)), NT_DIMS)
                ds = p * (dp - delta_scr[r0:t, one])
                dsb = ds.astype(BF16)
                dvt_scr[sl, r0:r1] = _dot(dot_scr[sl, r0:t], p.astype(BF16))
                dkt_scrs[j][:, r0:r1] = _dot(qt_scrs[j][:, r0:t], dsb)
                dq_scrs[j][r0:t, :] += _dot(dsb, k_ones)
        dv_ref[...] = jnp.transpose(dvt_scr[...]).astype(BF16)
        for j in range(2):
            sl = slice(j * HEAD_DIM, (j + 1) * HEAD_DIM)
            acc = dq_scrs[j][...]
            dkt = dkt_scrs[j][...]
            dq_ref[:, sl] = (acc[:, 0:HEAD_DIM] * scale).astype(BF16)
            dk_ref[:, sl] = (jnp.transpose(dkt)[:, 0:HEAD_DIM] * scale).astype(BF16)
            dc_ref[j:j + 1, :] = jnp.transpose(acc)[HEAD_DIM:HEAD_DIM + 1, :] - dkt[HEAD_DIM:HEAD_DIM + 1, :]

    blk = lambda off: pl.BlockSpec((t, 128), lambda b, hp: (b, off + hp))
    cblk = pl.BlockSpec((None, None, 8, t), lambda b, hp: (b, hp, 0, 0))
    return pl.pallas_call(
        body, name="fox_attn_bwd", grid=(bl, 8),
        in_specs=[blk(0), blk(8), blk(16), blk(0), blk(0), blk(0), cblk, ANY],
        out_specs=[blk(0), blk(0), blk(0), cblk],
        out_shape=[jax.ShapeDtypeStruct((n, ATT_WIDTH), BF16)] * 3 + [jax.ShapeDtypeStruct((bl, 8, 8, t), F32)],
        scratch_shapes=[pltpu.VMEM((t, 128), F32), pltpu.VMEM((t, 128), F32), pltpu.VMEM((t, 128), F32),
                        pltpu.VMEM((128, t), BF16), pltpu.VMEM((128, t), BF16), pltpu.VMEM((128, t), BF16),
                        pltpu.VMEM((128, t), F32), pltpu.VMEM((128, t), F32), pltpu.VMEM((128, t), F32)],
        compiler_params=_cparams(("parallel", "parallel")),
    )(qkv, qkv, qkv, do, o, lse, negc, after)


def _adamw(w, g, m, v, *, name):
    lead = w.ndim == 3
    r, c = w.shape[-2:]
    tr = _pick(r, (256, IN_SHARD // 3, 128, 64, 32, 16, 8))
    bc1 = 1.0 - ADAM_B1 ** ADAM_STEP
    bc2 = 1.0 - ADAM_B2 ** ADAM_STEP

    def body(w_ref, g_ref, m_ref, v_ref, d_ref, nm_ref, nv_ref):
        gv = g_ref[...]
        mn = ADAM_B1 * m_ref[...] + (1.0 - ADAM_B1) * gv
        vn = ADAM_B2 * v_ref[...] + (1.0 - ADAM_B2) * (gv * gv)
        m_hat = mn / bc1
        v_hat = vn / bc2
        d_ref[...] = -ADAM_LR * (m_hat / (jnp.sqrt(v_hat) + ADAM_EPS) + ADAM_WD * w_ref[...])
        nm_ref[...] = mn
        nv_ref[...] = vn

    flat = pl.BlockSpec((tr, c), lambda i: (i, 0))
    blk = pl.BlockSpec((None, tr, c), lambda i: (0, i, 0)) if lead else flat
    return pl.pallas_call(
        body, name=name, grid=(r // tr,), in_specs=[blk, flat, blk, blk], out_specs=[blk] * 3,
        out_shape=[jax.ShapeDtypeStruct(w.shape, F32)] * 3,
        compiler_params=_cparams(("parallel",)),
    )(w, g, m, v)


def _sum_leading(parts, *, name, out_dtype=F32):
    k, r, c = parts.shape
    tr = _pick(r, (512, 256, 128, 96, 64, 32, 16, 8))

    def body(p_ref, o_ref):
        acc = p_ref[0].astype(F32)
        for i in range(1, k):
            acc = acc + p_ref[i].astype(F32)
        o_ref[...] = acc.astype(out_dtype)

    return pl.pallas_call(
        body, name=name, grid=(r // tr,),
        in_specs=[pl.BlockSpec((k, tr, c), lambda i: (0, i, 0))],
        out_specs=pl.BlockSpec((tr, c), lambda i: (i, 0)),
        out_shape=jax.ShapeDtypeStruct((r, c), out_dtype),
        compiler_params=_cparams(("parallel",)),
    )(parts)


def _add_pair(a, b, *, name):
    k, r, c = a.shape
    tr = _pick(r, (512, 256, 128))

    def body(a_ref, b_ref, o_ref):
        o_ref[...] = (a_ref[...].astype(F32) + b_ref[...].astype(F32)).astype(BF16)

    blk = pl.BlockSpec((None, tr, c), lambda j, i: (j, i, 0))
    return pl.pallas_call(
        body, name=name, grid=(k, r // tr), in_specs=[blk, blk], out_specs=blk,
        out_shape=jax.ShapeDtypeStruct((k, r, c), BF16),
        compiler_params=_cparams(("parallel", "parallel")),
    )(a, b)


ANY = pl.BlockSpec(memory_space=pl.ANY)


def _chip_peers(x, y):
    return [(1 - x, y, 2 * (1 - x) + y), (x, 1 - y, 2 * x + 1 - y), (1 - x, 1 - y, 2 * (1 - x) + 1 - y)]


def _gather_weights(blob, *, name):
    rows, cols = blob.shape
    half_rows = rows // 2

    def body(b_ref, o_ref, send_sems, recv_sems):
        x, y, c = lax.axis_index("x"), lax.axis_index("y"), lax.axis_index("c")
        me = 2 * x + y
        sibling = (x, y, 1 - c)
        peers = _chip_peers(x, y)

        def half(chip, hc):
            return o_ref.at[chip, pl.ds(hc * half_rows, half_rows), :]

        def copy(k, src, chip, hc, to):
            return pltpu.make_async_remote_copy(src_ref=src, dst_ref=half(chip, hc), send_sem=send_sems.at[k],
                                                recv_sem=recv_sems.at[k], device_id=to, device_id_type=MESH)

        my_half = b_ref.at[pl.ds(c * half_rows, half_rows), :]
        first = [copy(k, my_half, me, c, (px, py, c)) for k, (px, py, _) in enumerate(peers)]
        for cp in first:
            cp.start()
        passed = [copy(3 + k, half(pc, c), pc, c, sibling) for k, (_, _, pc) in enumerate(peers)]
        for k, (px, py, pc) in enumerate(peers):
            copy(k, my_half, pc, c, (px, py, c)).wait_recv()
            passed[k].start()
        for k, (_, _, pc) in enumerate(peers):
            copy(3 + k, half(pc, 1 - c), pc, 1 - c, sibling).wait_recv()
        for cp in first + passed:
            cp.wait_send()

    return pl.pallas_call(
        body, name=name, in_specs=[ANY], out_specs=ANY,
        out_shape=jax.ShapeDtypeStruct((N_CHIPS, rows, cols), BF16),
        scratch_shapes=[pltpu.SemaphoreType.DMA((6,)), pltpu.SemaphoreType.DMA((6,))],
    )(blob)


def _swap_halves(g, *, name):
    _, rows, cols = g.shape
    half_rows = rows // 2

    def body(g_ref, o_ref, send_sem, recv_sem):
        x, y, c = lax.axis_index("x"), lax.axis_index("y"), lax.axis_index("c")
        cp = pltpu.make_async_remote_copy(
            src_ref=g_ref.at[:, pl.ds((1 - c) * half_rows, half_rows), :], dst_ref=o_ref,
            send_sem=send_sem, recv_sem=recv_sem, device_id=(x, y, 1 - c), device_id_type=MESH)
        cp.start()
        cp.wait()

    return pl.pallas_call(
        body, name=name, in_specs=[ANY], out_specs=ANY,
        out_shape=jax.ShapeDtypeStruct((N_CHIPS, half_rows, cols), BF16),
        scratch_shapes=[pltpu.SemaphoreType.DMA, pltpu.SemaphoreType.DMA],
    )(g)


HBM_SPEC = pl.BlockSpec(memory_space=pltpu.HBM)
SEM_SPEC = pl.BlockSpec(memory_space=pltpu.SEMAPHORE)
SPLIT_EFFECT = pltpu.SideEffectType.DATAFLOW_SIDE_EFFECTING


def _gather_peers_copies(b_ref, land_ref, send_sems, recv_sems, sending):
    x, y, c = lax.axis_index("x"), lax.axis_index("y"), lax.axis_index("c")
    me = 2 * x + y
    half_rows = b_ref.shape[0] // 2
    src = b_ref.at[pl.ds(c * half_rows, half_rows), :]
    return [pltpu.make_async_remote_copy(
        src_ref=src, dst_ref=land_ref.at[me if sending else pc, pl.ds(c * half_rows, half_rows), :],
        send_sem=send_sems.at[k], recv_sem=recv_sems.at[k], device_id=(px, py, c), device_id_type=MESH)
        for k, (px, py, pc) in enumerate(_chip_peers(x, y))]


def _gather_start(blob, after, *, name):
    shape = (N_CHIPS,) + blob.shape

    def body(b_ref, land_ref, after_ref, send_sems, recv_sems, b_thru, land_thru, token):
        for cp in _gather_peers_copies(b_ref, land_ref, send_sems, recv_sems, True):
            cp.start()
        token[...] = jnp.zeros_like(token)

    return pl.pallas_call(
        body, name=name,
        out_shape=(pltpu.SemaphoreType.DMA((3,)), pltpu.SemaphoreType.DMA((3,)), pltpu.HBM(blob.shape, blob.dtype),
                   pltpu.HBM(shape, blob.dtype), jax.ShapeDtypeStruct((8, 128), F32)),
        in_specs=(HBM_SPEC, HBM_SPEC, ANY),
        out_specs=(SEM_SPEC, SEM_SPEC, HBM_SPEC, HBM_SPEC, pl.BlockSpec(memory_space=pltpu.VMEM)),
        input_output_aliases={0: 2, 1: 3},
        compiler_params=pltpu.CompilerParams(has_side_effects=SPLIT_EFFECT),
    )(pltpu.with_memory_space_constraint(blob, pltpu.HBM),
      pltpu.with_memory_space_constraint(lax.empty(shape, blob.dtype), pltpu.HBM), after)


def _gather_wait(send_sems, recv_sems, b_thru, land_thru, after, *, name):
    def body(b_ref, land_ref, send_sems, recv_sems, after_ref, b_dead, got_ref):
        for cp in _gather_peers_copies(b_ref, land_ref, send_sems, recv_sems, False):
            cp.wait_send()
            cp.wait_recv()

    return pl.pallas_call(
        body, name=name,
        out_shape=(pltpu.HBM(b_thru.shape, b_thru.dtype), pltpu.HBM(land_thru.shape, land_thru.dtype)),
        in_specs=(HBM_SPEC, HBM_SPEC, SEM_SPEC, SEM_SPEC, ANY), out_specs=(HBM_SPEC, HBM_SPEC),
        input_output_aliases={0: 0, 1: 1},
        compiler_params=pltpu.CompilerParams(has_side_effects=SPLIT_EFFECT),
    )(b_thru, land_thru, send_sems, recv_sems, after)


def _gather_forward(land, *, name):
    half_rows = land.shape[1] // 2

    def body(l_ref, o_ref, send_sems, recv_sems):
        x, y, c = lax.axis_index("x"), lax.axis_index("y"), lax.axis_index("c")
        cps = []
        for k, (_, _, pc) in enumerate(_chip_peers(x, y)):
            mine = pl.ds(c * half_rows, half_rows)
            cps.append(pltpu.make_async_remote_copy(
                src_ref=l_ref.at[pc, mine, :], dst_ref=o_ref.at[pc, mine, :], send_sem=send_sems.at[k],
                recv_sem=recv_sems.at[k], device_id=(x, y, 1 - c), device_id_type=MESH))
        for cp in cps:
            cp.start()
        for k, (_, _, pc) in enumerate(_chip_peers(x, y)):
            theirs = pl.ds((1 - c) * half_rows, half_rows)
            pltpu.make_async_remote_copy(
                src_ref=l_ref.at[pc, theirs, :], dst_ref=o_ref.at[pc, theirs, :], send_sem=send_sems.at[k],
                recv_sem=recv_sems.at[k], device_id=(x, y, 1 - c), device_id_type=MESH).wait_recv()
        for cp in cps:
            cp.wait_send()

    return pl.pallas_call(
        body, name=name, in_specs=[ANY], out_specs=ANY, input_output_aliases={0: 0},
        out_shape=jax.ShapeDtypeStruct(land.shape, land.dtype),
        scratch_shapes=[pltpu.SemaphoreType.DMA((3,)), pltpu.SemaphoreType.DMA((3,))],
    )(land)


def _exchange_peers_copies(p_ref, land_ref, send_sems, recv_sems, sending):
    x, y, c = lax.axis_index("x"), lax.axis_index("y"), lax.axis_index("c")
    me = 2 * x + y
    return [pltpu.make_async_remote_copy(src_ref=p_ref.at[pc], dst_ref=land_ref.at[me if sending else pc],
                                         send_sem=send_sems.at[k], recv_sem=recv_sems.at[k],
                                         device_id=(px, py, c), device_id_type=MESH)
            for k, (px, py, pc) in enumerate(_chip_peers(x, y))]


def _exchange_start(p, *, name):
    def body(p_ref, land_ref, send_sems, recv_sems, p_thru, land_thru, token):
        for cp in _exchange_peers_copies(p_ref, land_ref, send_sems, recv_sems, True):
            cp.start()
        token[...] = jnp.zeros_like(token)

    return pl.pallas_call(
        body, name=name,
        out_shape=(pltpu.SemaphoreType.DMA((3,)), pltpu.SemaphoreType.DMA((3,)), pltpu.HBM(p.shape, p.dtype),
                   pltpu.HBM(p.shape, p.dtype), jax.ShapeDtypeStruct((8, 128), F32)),
        in_specs=(HBM_SPEC, HBM_SPEC),
        out_specs=(SEM_SPEC, SEM_SPEC, HBM_SPEC, HBM_SPEC, pl.BlockSpec(memory_space=pltpu.VMEM)),
        input_output_aliases={0: 2, 1: 3},
        compiler_params=pltpu.CompilerParams(has_side_effects=SPLIT_EFFECT),
    )(pltpu.with_memory_space_constraint(p, pltpu.HBM),
      pltpu.with_memory_space_constraint(lax.empty(p.shape, p.dtype), pltpu.HBM))


def _exchange_wait(send_sems, recv_sems, p_thru, land_thru, after, *, name):
    def body(p_ref, land_ref, send_sems, recv_sems, after_ref, p_dead, got_ref):
        for cp in _exchange_peers_copies(p_ref, land_ref, send_sems, recv_sems, False):
            cp.wait_send()
            cp.wait_recv()

    return pl.pallas_call(
        body, name=name,
        out_shape=(pltpu.HBM(p_thru.shape, p_thru.dtype), pltpu.HBM(p_thru.shape, p_thru.dtype)),
        in_specs=(HBM_SPEC, HBM_SPEC, SEM_SPEC, SEM_SPEC, ANY), out_specs=(HBM_SPEC, HBM_SPEC),
        input_output_aliases={0: 0, 1: 1},
        compiler_params=pltpu.CompilerParams(has_side_effects=SPLIT_EFFECT),
    )(p_thru, land_thru, send_sems, recv_sems, after)


def _sum_parts(parts, own, *, name):
    k, r, c = parts.shape
    tr = _pick(r, (512, 256, 128))

    def body(p_ref, own_ref, o_ref):
        me = 2 * lax.axis_index("x") + lax.axis_index("y")
        acc = jnp.zeros((tr, c), F32)
        for i in range(k):
            acc = acc + jnp.where(me == i, own_ref[i], p_ref[i]).astype(F32)
        o_ref[...] = acc

    blk = pl.BlockSpec((k, tr, c), lambda i: (0, i, 0))
    return pl.pallas_call(
        body, name=name, grid=(r // tr,), in_specs=[blk, blk],
        out_specs=pl.BlockSpec((tr, c), lambda i: (i, 0)),
        out_shape=jax.ShapeDtypeStruct((r, c), F32),
        compiler_params=_cparams(("parallel",)),
    )(parts, own)


def _join_halves(gh, *, name):
    def body(g_ref, o_ref, send_sem, recv_sem):
        x, y, c = lax.axis_index("x"), lax.axis_index("y"), lax.axis_index("c")
        cp = pltpu.make_async_remote_copy(src_ref=g_ref, dst_ref=o_ref, send_sem=send_sem, recv_sem=recv_sem,
                                          device_id=(x, y, 1 - c), device_id_type=MESH)
        cp.start()
        cp.wait()

    other = pl.pallas_call(
        body, name=name, in_specs=[ANY], out_specs=ANY,
        out_shape=jax.ShapeDtypeStruct(gh.shape, F32),
        scratch_shapes=[pltpu.SemaphoreType.DMA, pltpu.SemaphoreType.DMA],
    )(gh)
    south = lax.axis_index("c") == 0
    return jnp.concatenate([jnp.where(south, gh, other), jnp.where(south, other, gh)], axis=0)


def _gather_small(s, *, name):
    rows = s.shape[0]

    def body(s_ref, o_ref, send_sems, recv_sems, local_sem):
        x, y, c = lax.axis_index("x"), lax.axis_index("y"), lax.axis_index("c")
        me = 4 * x + 2 * y + c
        mine = pltpu.make_async_copy(s_ref, o_ref.at[me], local_sem)
        mine.start()
        peers = []
        for k in range(1, 8):
            peers.append((1 - x if k & 4 else x, 1 - y if k & 2 else y, 1 - c if k & 1 else c))
        cps = [pltpu.make_async_remote_copy(src_ref=s_ref, dst_ref=o_ref.at[me], send_sem=send_sems.at[k],
                                            recv_sem=recv_sems.at[k], device_id=p, device_id_type=MESH)
               for k, p in enumerate(peers)]
        for cp in cps:
            cp.start()
        for k, (px, py, pc) in enumerate(peers):
            pltpu.make_async_remote_copy(src_ref=s_ref, dst_ref=o_ref.at[4 * px + 2 * py + pc],
                                         send_sem=send_sems.at[k], recv_sem=recv_sems.at[k],
                                         device_id=(px, py, pc), device_id_type=MESH).wait_recv()
        for cp in cps:
            cp.wait_send()
        mine.wait()

    return pl.pallas_call(
        body, name=name, in_specs=[ANY], out_specs=ANY,
        out_shape=jax.ShapeDtypeStruct((8, rows, 128), F32),
        scratch_shapes=[pltpu.SemaphoreType.DMA((7,)), pltpu.SemaphoreType.DMA((7,)), pltpu.SemaphoreType.DMA],
    )(s)


IN_SHARD = IN_WIDTH // N_CHIPS
IN_SHARD_PAD = 1536
UP_ROWS, DOWN_ROWS, OUT_ROWS = 1024, 1024, 512
REST_ROWS = UP_ROWS + DOWN_ROWS + OUT_ROWS


def _pack_in(w_in_s):
    return jnp.pad(w_in_s, ((0, 0), (0, IN_SHARD_PAD - IN_SHARD))).astype(BF16)


def _pack_rest(w_out_s, w_up_s, w_down_s):
    return jnp.concatenate([w_up_s, w_down_s, w_out_s], axis=0).astype(BF16)


def _unpack_rest(blob):
    return (blob[UP_ROWS + DOWN_ROWS:], blob[0:UP_ROWS], blob[UP_ROWS:UP_ROWS + DOWN_ROWS])


def _with_own(gathered, own):
    me = 2 * lax.axis_index("x") + lax.axis_index("y")
    return [jnp.where(me == j, own, gathered[j]) for j in range(N_CHIPS)]


def _full_w_in(g_in, own):
    return jnp.concatenate([s[:, :IN_SHARD] for s in _with_own(g_in, own)], axis=1)


def _full_rest(g_rest, own):
    parts = [_unpack_rest(s) for s in _with_own(g_rest, own)]
    w_out = jnp.concatenate([p[0] for p in parts], axis=0)
    w_up = jnp.concatenate([p[1] for p in parts], axis=1)
    w_down = jnp.concatenate([p[2] for p in parts], axis=0)
    return w_out, w_up, w_down


def _split_w_in(w_in):
    z_xbc = w_in[:, 0:2560]
    dt = w_in[:, 2560:2576]
    qkv = w_in[:, 2576:5648]
    f = w_in[:, 5648:5664]
    pad = jnp.zeros((w_in.shape[0], PA_WIDTH - 2592), w_in.dtype)
    return jnp.concatenate([z_xbc, dt, f, pad], axis=1), qkv


def _merge_w_in(d_a, d_qkv):
    return jnp.concatenate([d_a[:, 0:2560], d_a[:, 2560:2576], d_qkv, d_a[:, 2576:2592]], axis=1)


def _local_step(x3, target3, w_in, rest_weights, norm_mix_w, conv_w, conv_b, dt_bias, a_log, d_skip,
                ssd_norm_w, f_bias, norm_mlp_w, norm_final_w, first_after=None, early_grads=None, late_grads=None):
    bl, t, d = x3.shape
    n = bl * t
    x = x3.reshape(n, d)
    target = target3.reshape(n, d)
    w_a, w_qkv = _split_w_in(w_in)
    nfw = norm_final_w.reshape(1, d)
    dskip_e = jnp.repeat(d_skip, HEAD_DIM, axis=1)
    nb = t // ATT_BLOCK

    r1, r2, kt = min(n, 1024), min(n, 512), min(n, 2048)
    if first_after is None:
        first_after = jnp.zeros((8, 128), F32)
    h0, rstd0, proj_a = _norm_mm(x, norm_mix_w, w_a, first_after, name="norm_mix_proj_a", tm=r2)
    qkv = _mm(h0, w_qkv, name="proj_qkv", tiles=(r2, QKV_WIDTH, D_MODEL), out_dtype=BF16)
    bias128 = jnp.concatenate([dt_bias, f_bias, jnp.zeros((1, 96), F32)], axis=1)
    alog128 = jnp.concatenate([a_log, jnp.zeros((1, 112), F32)], axis=1)
    dt, gate_d, acum, acum_t, negc = _prep(proj_a, bias128, alog128, bl, t)
    xc, dsilu = _conv_fwd(proj_a, conv_w, conv_b, bl, t)
    y_ssd, y_pre, hprev = _ssd_fwd(xc, proj_a, dt, acum, acum_t, dskip_e, ssd_norm_w, bl, t)
    y_att, lse = _attn_fwd(qkv, negc, bl, t)
    w_out, w_up, w_down = rest_weights(y_att)
    wo_s, wo_a = w_out[:SSD_WIDTH], w_out[SSD_WIDTH:]
    h1, h1n, rstd1 = _mm_norm_fwd(y_ssd, wo_s, y_att, wo_a, x, norm_mlp_w, name="out_proj_norm_mlp", tm=r2)
    up = _mm(h1n, w_up, name="mlp_up", tiles=(r1, D_FF, D_MODEL), out_dtype=BF16)
    dh2, dh2b, loss, d_nfw = _mm_final(up, w_down, h1, nfw, target, name="mlp_down_final_norm_loss", tm=r2,
                                       a_act="relu2")

    dup = _mm(dh2b, w_down, name="mlp_down_bwd_act", tiles=(r2, D_FF, D_MODEL), tb=True, epi_up=up, out_dtype=BF16)
    rest_shape = (N_CHIPS, REST_ROWS, D_MODEL)
    gb_rest = _mm(up, dh2b, name="mlp_down_bwd_w", tiles=(DOWN_ROWS, D_MODEL, kt), ta=True, a_act="relu2",
                  out_dtype=BF16, into=(rest_shape, (None, DOWN_ROWS, D_MODEL), lambda i, j, k: (i, 1, 0), None))
    dh1, dh1b, d_nmlp = _mm_norm_bwd([(dup, w_up)], h1, rstd1, norm_mlp_w, dh2, name="mlp_up_bwd_act_norm_mlp",
                                     tm=r2)
    gb_rest = _mm(h1n, dup, name="mlp_up_bwd_w", tiles=(D_MODEL, UP_ROWS, kt), ta=True, out_dtype=BF16,
                  into=(rest_shape, (None, D_MODEL, UP_ROWS), lambda i, j, k: (j, 0, 0), gb_rest))
    dys, do = _mm_two_halves(dh1b, w_out, name="out_proj_bwd_act", tm=r1)
    out_block = (UP_ROWS + DOWN_ROWS) // OUT_ROWS
    for half, (y_half, tag) in enumerate(((y_ssd, "ssd"), (y_att, "att"))):
        gb_rest = _mm(y_half, dh1b, name="out_proj_bwd_w_" + tag, tiles=(2 * OUT_ROWS, D_MODEL, kt), ta=True,
                      out_dtype=BF16, into=(rest_shape, (2, OUT_ROWS, D_MODEL),
                                            functools.partial(lambda i, j, k, h: (h, out_block, 0), h=half),
                                            gb_rest))
    token = jnp.zeros((8, 128), F32) if early_grads is None else early_grads(gb_rest)
    dq, dk, dv, dcb = _attn_bwd(qkv, do, y_att, lse, negc, token, bl, t)
    dc = jnp.pad(dcb[:, :, 0:2, :].transpose(0, 3, 1, 2).reshape(n, 16), ((0, 0), (16, 96)))
    dxc, dpa, ddt_raw, d_snw, d_dsk, d_alog, d_dtb = _ssd_bwd(dys, xc, proj_a, y_pre, hprev, dt, gate_d, acum,
                                                             acum_t, alog128, dskip_e, ssd_norm_w, bl, t)
    dpa, d_conv_w, d_conv_b = _conv_bwd(dxc, dsilu, proj_a, conv_w, dpa, bl, t)
    dproj_a, d_fb = _fpost(dc, gate_d, ddt_raw, dpa, bl, t)
    dqkv = jnp.concatenate([dq, dk, dv], axis=1)
    d_w_a = _mm(h0, dproj_a, name="proj_a_bwd_w", tiles=(1024, 896, kt), ta=True, out_dtype=BF16)
    d_w_qkv = _mm(h0, dqkv, name="proj_qkv_bwd_w", tiles=(1024, 1024, kt), ta=True, out_dtype=BF16)
    d_w_in = _merge_w_in(d_w_a, d_w_qkv)
    late_token = None if late_grads is None else late_grads(d_w_in)
    dx, _, d_nmix = _mm_norm_bwd([(dproj_a, w_a), (dqkv, w_qkv)], x, rstd0, norm_mix_w, dh1,
                                 name="proj_bwd_act_norm_mix", tm=min(n, 256), after=late_token)

    grads = dict(norm_mix_w=d_nmix, w_in=d_w_in, conv_w=d_conv_w, conv_b=d_conv_b,
                 dt_bias=d_dtb, a_log=d_alog, d_skip=d_dsk, ssd_norm_w=d_snw, f_bias=d_fb, rest=gb_rest,
                 norm_mlp_w=d_nmlp, norm_final_w=d_nfw)
    return dx.reshape(bl, t, d), loss, grads


SMALL_ORDER = ("norm_mix_w", "conv_w", "conv_b", "dt_bias", "a_log", "d_skip", "ssd_norm_w", "f_bias",
               "norm_mlp_w", "norm_final_w")
SMALL_SIZES = (1024, 4 * CONV_CH, CONV_CH, 16, 16, 16, 1024, 16, 1024, 1024)


def _pack_small(vals, rows):
    flat = jnp.concatenate([v.reshape(-1).astype(F32) for v in vals])
    return jnp.pad(flat, (0, rows * 128 - flat.shape[0])).reshape(rows, 128)


def _unpack_small(packed, sizes):
    flat = packed.reshape(-1)
    out, o = [], 0
    for s in sizes:
        out.append(flat[o:o + s])
        o += s
    return out


def kernel(x, norm_mix_w, w_in, conv_w, conv_b, dt_bias, a_log, d_skip, ssd_norm_w, f_bias, w_out, norm_mlp_w, w_up, w_down, norm_final_w, loss_target, m_norm_mix_w, m_w_in, m_conv_w, m_conv_b, m_dt_bias, m_a_log, m_d_skip, m_ssd_norm_w, m_f_bias, m_w_out, m_norm_mlp_w, m_w_up, m_w_down, m_norm_final_w, v_norm_mix_w, v_w_in, v_conv_w, v_conv_b, v_dt_bias, v_a_log, v_d_skip, v_ssd_norm_w, v_f_bias, v_w_out, v_norm_mlp_w, v_w_up, v_w_down, v_norm_final_w):
    chip = 2 * lax.axis_index("x") + lax.axis_index("y")
    cw = CONV_CH // N_CHIPS

    own_in = _pack_in(w_in[0])
    own_rest = _pack_rest(w_out[0], w_up[0], w_down[0])
    g_in = _gather_weights(own_in, name="gather_w_in")
    w_in_f = _full_w_in(g_in, own_in)
    *rest_handles, rest_token = _gather_start(own_rest, g_in, name="gather_start_rest")

    def rest_weights(after):
        _, landed = _gather_wait(*rest_handles, after, name="gather_wait_rest")
        return _full_rest(_gather_forward(landed, name="gather_forward_rest"), own_rest)
    small_all = _gather_small(_pack_small([conv_w[0]], 16), name="gather_conv_w")
    conv_w_f = jnp.concatenate([small_all[2 * j].reshape(-1)[:4 * cw].reshape(4, cw) for j in range(N_CHIPS)], axis=1)

    c = lax.axis_index("c")

    def chip_partial(gb, tag):
        half_rows = gb.shape[1] // 2
        from_sibling = _swap_halves(gb, name="grad_swap_halves_" + tag)
        my_half = lax.dynamic_slice_in_dim(gb, c * half_rows, half_rows, axis=1)
        return _add_pair(my_half, from_sibling, name="grad_add_sibling_" + tag)

    in_flight = {}

    def early_grads(gb_rest):
        part = chip_partial(gb_rest, "rest")
        *handles, token = _exchange_start(part, name="grad_exchange_start_rest")
        in_flight["rest"] = handles
        return token

    def late_grads(d_w_in):
        gb_in = jnp.stack([_pack_in(d_w_in[:, j * IN_SHARD:(j + 1) * IN_SHARD]) for j in range(N_CHIPS)])
        *handles, token = _exchange_start(chip_partial(gb_in, "in"), name="grad_exchange_start_in")
        in_flight["in"] = handles
        return token

    dx, loss_part, g = _local_step(x, loss_target, w_in_f, rest_weights, norm_mix_w, conv_w_f,
                                   conv_b, dt_bias, a_log, d_skip, ssd_norm_w, f_bias, norm_mlp_w, norm_final_w,
                                   first_after=rest_token, early_grads=early_grads, late_grads=late_grads)

    send_sems, recv_sems, part_rest, land_rest = in_flight["rest"]
    part_rest, parts_rest = _exchange_wait(send_sems, recv_sems, part_rest, land_rest, dx,
                                           name="grad_exchange_wait_rest")
    g_rest_half = _sum_parts(parts_rest, part_rest, name="grad_sum_chips_rest")
    g_w_out, g_w_up, g_w_down = _unpack_rest(_join_halves(g_rest_half, name="grad_join_halves_rest"))

    part_in, parts_in = _exchange_wait(*in_flight["in"], dx, name="grad_exchange_wait_in")
    g_in_half = _sum_parts(parts_in, part_in, name="grad_sum_chips_in")
    g_w_in = _join_halves(g_in_half, name="grad_join_halves_in")[:, :IN_SHARD]

    small_vals = [g[k] for k in SMALL_ORDER] + [loss_part[:, 0:1]]
    small_sum = _sum_leading(_gather_small(_pack_small(small_vals, SMALL_ROWS), name="gather_small_grads"), name="small_sum")
    sg = dict(zip(SMALL_ORDER + ("loss",), _unpack_small(small_sum, SMALL_SIZES + (1,))))
    loss = sg["loss"].reshape(())
    g_conv_full = sg["conv_w"].reshape(4, CONV_CH)
    g_conv = lax.dynamic_slice_in_dim(g_conv_full, chip * cw, cw, axis=1)

    grads = dict(norm_mix_w=sg["norm_mix_w"].reshape(1, -1), w_in=g_w_in[None], conv_w=g_conv[None],
                 conv_b=sg["conv_b"].reshape(1, -1), dt_bias=sg["dt_bias"].reshape(1, -1),
                 a_log=sg["a_log"].reshape(1, -1), d_skip=sg["d_skip"].reshape(1, -1),
                 ssd_norm_w=sg["ssd_norm_w"].reshape(1, -1), f_bias=sg["f_bias"].reshape(1, -1), w_out=g_w_out[None],
                 norm_mlp_w=sg["norm_mlp_w"].reshape(1, -1), w_up=g_w_up[None], w_down=g_w_down[None],
                 norm_final_w=sg["norm_final_w"])
    weights = dict(norm_mix_w=norm_mix_w, w_in=w_in, conv_w=conv_w, conv_b=conv_b, dt_bias=dt_bias, a_log=a_log,
                   d_skip=d_skip, ssd_norm_w=ssd_norm_w, f_bias=f_bias, w_out=w_out, norm_mlp_w=norm_mlp_w,
                   w_up=w_up, w_down=w_down, norm_final_w=norm_final_w)
    ms = dict(norm_mix_w=m_norm_mix_w, w_in=m_w_in, conv_w=m_conv_w, conv_b=m_conv_b, dt_bias=m_dt_bias,
              a_log=m_a_log, d_skip=m_d_skip, ssd_norm_w=m_ssd_norm_w, f_bias=m_f_bias, w_out=m_w_out,
              norm_mlp_w=m_norm_mlp_w, w_up=m_w_up, w_down=m_w_down, norm_final_w=m_norm_final_w)
    vs = dict(norm_mix_w=v_norm_mix_w, w_in=v_w_in, conv_w=v_conv_w, conv_b=v_conv_b, dt_bias=v_dt_bias,
              a_log=v_a_log, d_skip=v_d_skip, ssd_norm_w=v_ssd_norm_w, f_bias=v_f_bias, w_out=v_w_out,
              norm_mlp_w=v_norm_mlp_w, w_up=v_w_up, w_down=v_w_down, norm_final_w=v_norm_final_w)
    names = list(weights)
    big = ("w_in", "w_out", "w_up", "w_down")
    delta, new_m, new_v = {}, {}, {}
    for k, g2 in zip(big[1:], (g_w_out, g_w_up, g_w_down)):
        delta[k], new_m[k], new_v[k] = _adamw(weights[k], g2, ms[k], vs[k], name="adamw_" + k)
    g_in_t = g_w_in.T
    outs_t = _adamw(w_in[0].T, g_in_t, m_w_in[0].T, v_w_in[0].T, name="adamw_w_in")
    delta["w_in"], new_m["w_in"], new_v["w_in"] = [o.T[None] for o in outs_t]
    grads["w_in"] = g_in_t.T[None]
    smalls = [k for k in names if k not in big]
    sizes = [math.prod(weights[k].shape) for k in smalls]
    rows = -(-sum(sizes) // 1024) * 8
    packs = [_pack_small([d[k] for k in smalls], rows) for d in (weights, grads, ms, vs)]
    outs = _adamw(*packs, name="adamw_small")
    for o, dst in zip(outs, (delta, new_m, new_v)):
        for k, val in zip(smalls, _unpack_small(o, sizes)):
            dst[k] = val.reshape(weights[k].shape)
    return (loss, dx, *[grads[k] for k in names], *[delta[k] for k in names], *[new_m[k] for k in names],
            *[new_v[k] for k in names])
```

```python
import functools
import math

import jax
import jax.numpy as jnp
from jax import lax
from jax.experimental import pallas as pl
from jax.experimental.pallas import tpu as pltpu

F32 = jnp.float32
BF16 = jnp.bfloat16
HIGHEST = lax.Precision.HIGHEST
MESH = pl.DeviceIdType.MESH

D_MODEL = 1024
HEAD_DIM = 64
SSD_WIDTH = 1024
SSD_STATE = 128
CONV_CH = 1536
CHUNK = 128
ATT_WIDTH = 1024
EPS = 1e-5
IN_WIDTH = 5664
PA_WIDTH = 2688
QKV_WIDTH = 3072
D_FF = 4096
ATT_BLOCK = 256
NEG = -1e30
LOG2E = 1.4426950408889634
VMEM_LIMIT = 48 * 1024 * 1024

ADAM_LR = 0.001
ADAM_B1 = 0.9
ADAM_B2 = 0.999
ADAM_EPS = 1e-08
ADAM_WD = 0.01
ADAM_STEP = 10

N_CHIPS = 4
SMALL_ROWS = 96


def _cparams(sem):
    return pltpu.CompilerParams(dimension_semantics=sem, vmem_limit_bytes=VMEM_LIMIT)


def _pick(n, cands):
    for c in cands:
        if n % c == 0:
            return c
    return n


MM_CHUNK = 512


def _mm(a, b, *, name, tiles, ta=False, tb=False, out_dtype=F32, res=None, a_act=None, epi_up=None, after=None,
        into=None):
    n_unread = (after is not None) + (into is not None and into[3] is not None)
    if ta:
        K, M = a.shape
    else:
        M, K = a.shape
    if tb:
        N, K2 = b.shape
    else:
        K2, N = b.shape
    assert K == K2, (a.shape, b.shape)
    tm, tn, tk = tiles
    assert M % tm == 0 and N % tn == 0 and K % tk == 0, (name, M, N, K, tiles)
    nk = K // tk
    dn = (((0 if ta else 1,), (1 if tb else 0,)), ((), ()))
    has_res = res is not None
    has_up = epi_up is not None
    cn = _pick(tn, (MM_CHUNK, 384, 256, 128))

    def prologue(av):
        if a_act == "relu2":
            r = jnp.maximum(av.astype(F32), 0.0)
            av = r * r
        return av.astype(BF16)

    def epilogue(out, res_v, up_v):
        if has_res:
            out = out + res_v.astype(F32)
        if has_up:
            out = out * (2.0 * jnp.maximum(up_v.astype(F32), 0.0))
        return out.astype(out_dtype)

    def body(*refs):
        a_ref, b_ref = refs[0], refs[1]
        i = 2
        res_ref = up_ref = None
        if has_res:
            res_ref = refs[i]
            i += 1
        if has_up:
            up_ref = refs[i]
            i += 1
        i += n_unread
        o_ref = refs[i]
        if nk == 1:
            av = prologue(a_ref[...])
            if len(o_ref.shape) == 3:
                out = lax.dot_general(av, b_ref[...].astype(BF16), dn, preferred_element_type=F32)
                out = epilogue(out, res_ref[...] if has_res else None, up_ref[...] if has_up else None)
                o_ref[...] = out.reshape(o_ref.shape)
                return
            for c in range(tn // cn):
                cs = slice(c * cn, (c + 1) * cn)
                bv = (b_ref[cs, :] if tb else b_ref[:, cs]).astype(BF16)
                out = lax.dot_general(av, bv, dn, preferred_element_type=F32)
                o_ref[:, cs] = epilogue(out, res_ref[:, cs] if has_res else None, up_ref[:, cs] if has_up else None)
            return
        acc_ref = refs[i + 1]
        k = pl.program_id(2)

        @pl.when(k == 0)
        def _():
            acc_ref[...] = jnp.zeros_like(acc_ref)

        acc_ref[...] += lax.dot_general(prologue(a_ref[...]), b_ref[...].astype(BF16), dn,
                                        preferred_element_type=F32)

        @pl.when(k == nk - 1)
        def _():
            out = epilogue(acc_ref[...], res_ref[...] if has_res else None, up_ref[...] if has_up else None)
            o_ref[...] = out.reshape(o_ref.shape)

    a_spec = pl.BlockSpec((tk, tm), lambda i, j, k: (k, i)) if ta else pl.BlockSpec((tm, tk), lambda i, j, k: (i, k))
    b_spec = pl.BlockSpec((tn, tk), lambda i, j, k: (j, k)) if tb else pl.BlockSpec((tk, tn), lambda i, j, k: (k, j))
    o_spec = pl.BlockSpec((tm, tn), lambda i, j, k: (i, j))
    ins, specs = [a, b], [a_spec, b_spec]
    if has_res:
        ins.append(res)
        specs.append(o_spec)
    if has_up:
        ins.append(epi_up)
        specs.append(o_spec)
    if after is not None:
        ins.append(after)
        specs.append(pl.BlockSpec(memory_space=pl.ANY))
    out_shape, out_spec, aliases = jax.ShapeDtypeStruct((M, N), out_dtype), o_spec, {}
    if into is not None:
        shape, block, index, buf = into
        out_shape, out_spec = jax.ShapeDtypeStruct(shape, out_dtype), pl.BlockSpec(block, index)
        if buf is not None:
            aliases = {len(ins): 0}
            ins.append(buf)
            specs.append(pl.BlockSpec(memory_space=pl.ANY))
    return pl.pallas_call(
        body, name=name, grid=(M // tm, N // tn, nk),
        in_specs=specs, out_specs=out_spec, out_shape=out_shape, input_output_aliases=aliases,
        scratch_shapes=[] if nk == 1 else [pltpu.VMEM((tm, tn), F32)],
        compiler_params=_cparams(("parallel", "parallel", "arbitrary")),
    )(*ins)


def _rows_product(a_ref, b_ref, tb, a_act):
    av = a_ref[...]
    if a_act == "relu2":
        r = jnp.maximum(av.astype(F32), 0.0)
        av = r * r
    dn = (((1,), (1 if tb else 0,)), ((), ()))
    return lax.dot_general(av.astype(BF16), b_ref[...].astype(BF16), dn, preferred_element_type=F32)


def _norm_mm(x, w, b, after, *, name, tm):
    m, d = x.shape
    n = b.shape[1]
    cn = _pick(n, (MM_CHUNK, 384, 256, 128))

    def body(x_ref, w_ref, b_ref, after_ref, h_ref, r_ref, o_ref):
        xv = x_ref[...]
        rstd = lax.rsqrt(jnp.mean(xv * xv, axis=1, keepdims=True) + EPS)
        hv = (xv * rstd * w_ref[...]).astype(BF16)
        h_ref[...] = hv
        r_ref[...] = rstd
        for c in range(n // cn):
            cs = slice(c * cn, (c + 1) * cn)
            o_ref[:, cs] = jnp.dot(hv, b_ref[:, cs].astype(BF16), preferred_element_type=F32)

    row = pl.BlockSpec((tm, d), lambda i: (i, 0))
    return pl.pallas_call(
        body, name=name, grid=(m // tm,),
        in_specs=[row, pl.BlockSpec((1, d), lambda i: (0, 0)), pl.BlockSpec((d, n), lambda i: (0, 0)),
                  pl.BlockSpec(memory_space=pl.ANY)],
        out_specs=[row, pl.BlockSpec((tm, 1), lambda i: (i, 0)), pl.BlockSpec((tm, n), lambda i: (i, 0))],
        out_shape=[jax.ShapeDtypeStruct((m, d), BF16), jax.ShapeDtypeStruct((m, 1), F32),
                   jax.ShapeDtypeStruct((m, n), F32)],
        compiler_params=_cparams(("parallel",)),
    )(x, w, b, after)


def _mm_norm_fwd(a1, b1, a2, b2, res, w, *, name, tm):
    m, k1 = a1.shape
    k2 = a2.shape[1]
    d = b1.shape[1]

    def body(a1_ref, b1_ref, a2_ref, b2_ref, res_ref, w_ref, h_ref, y_ref, r_ref):
        hv = _rows_product(a1_ref, b1_ref, False, None) + _rows_product(a2_ref, b2_ref, False, None) + res_ref[...]
        rstd = lax.rsqrt(jnp.mean(hv * hv, axis=1, keepdims=True) + EPS)
        h_ref[...] = hv
        y_ref[...] = (hv * rstd * w_ref[...]).astype(BF16)
        r_ref[...] = rstd

    row = pl.BlockSpec((tm, d), lambda i: (i, 0))
    return pl.pallas_call(
        body, name=name, grid=(m // tm,),
        in_specs=[pl.BlockSpec((tm, k1), lambda i: (i, 0)), pl.BlockSpec((k1, d), lambda i: (0, 0)),
                  pl.BlockSpec((tm, k2), lambda i: (i, 0)), pl.BlockSpec((k2, d), lambda i: (0, 0)), row,
                  pl.BlockSpec((1, d), lambda i: (0, 0))],
        out_specs=[row, row, pl.BlockSpec((tm, 1), lambda i: (i, 0))],
        out_shape=[jax.ShapeDtypeStruct((m, d), F32), jax.ShapeDtypeStruct((m, d), BF16),
                   jax.ShapeDtypeStruct((m, 1), F32)],
        compiler_params=_cparams(("parallel",)),
    )(a1, b1, a2, b2, res, w)


def _mm_final(a, b, res, w, target, *, name, tm, a_act):
    m, k = a.shape
    d = b.shape[1]

    def body(a_ref, b_ref, res_ref, w_ref, t_ref, dh_ref, dhb_ref, loss_ref, dw_ref):
        @pl.when(pl.program_id(0) == 0)
        def _():
            loss_ref[...] = jnp.zeros_like(loss_ref)
            dw_ref[...] = jnp.zeros_like(dw_ref)

        hv = _rows_product(a_ref, b_ref, False, a_act) + res_ref[...]
        wv = w_ref[...]
        rstd = lax.rsqrt(jnp.mean(hv * hv, axis=1, keepdims=True) + EPS)
        xhat = hv * rstd
        err = xhat * wv - t_ref[...]
        loss_ref[...] += 0.5 * jnp.sum(jnp.mean(err * err, axis=1, keepdims=True), axis=0, keepdims=True)
        dy = err * (1.0 / d)
        gw = dy * wv
        dh = rstd * (gw - xhat * jnp.mean(gw * xhat, axis=1, keepdims=True))
        dh_ref[...] = dh
        dhb_ref[...] = dh.astype(BF16)
        dw_ref[...] += jnp.sum(dy * xhat, axis=0, keepdims=True)

    row = pl.BlockSpec((tm, d), lambda i: (i, 0))
    vec = pl.BlockSpec((1, d), lambda i: (0, 0))
    return pl.pallas_call(
        body, name=name, grid=(m // tm,),
        in_specs=[pl.BlockSpec((tm, k), lambda i: (i, 0)), pl.BlockSpec((k, d), lambda i: (0, 0)), row, vec, row],
        out_specs=[row, row, pl.BlockSpec((1, 128), lambda i: (0, 0)), vec],
        out_shape=[jax.ShapeDtypeStruct((m, d), F32), jax.ShapeDtypeStruct((m, d), BF16),
                   jax.ShapeDtypeStruct((1, 128), F32), jax.ShapeDtypeStruct((1, d), F32)],
        compiler_params=_cparams(("arbitrary",)),
    )(a, b, res, w, target)


def _mm_two_halves(a, b, *, name, tm):
    m, k = a.shape
    d = b.shape[0] // 2

    def body(a_ref, b_ref, lo_ref, hi_ref):
        av = a_ref[...].astype(BF16)
        lo_ref[...] = lax.dot_general(av, b_ref[0:d, :].astype(BF16), NT_DIMS, preferred_element_type=F32)
        hi_ref[...] = lax.dot_general(av, b_ref[d:2 * d, :].astype(BF16), NT_DIMS,
                                      preferred_element_type=F32).astype(BF16)

    row = pl.BlockSpec((tm, d), lambda i: (i, 0))
    return pl.pallas_call(
        body, name=name, grid=(m // tm,),
        in_specs=[pl.BlockSpec((tm, k), lambda i: (i, 0)), pl.BlockSpec((2 * d, k), lambda i: (0, 0))],
        out_specs=[row, row],
        out_shape=[jax.ShapeDtypeStruct((m, d), F32), jax.ShapeDtypeStruct((m, d), BF16)],
        compiler_params=_cparams(("parallel",)),
    )(a, b)


def _mm_norm_bwd(pairs, x, rstd, w, dres, *, name, tm, after=None):
    m = pairs[0][0].shape[0]
    d = pairs[0][1].shape[0]
    n_pairs = len(pairs)

    def body(*refs):
        i = 2 * n_pairs
        x_ref, r_ref, w_ref, d_ref = refs[i:i + 4]
        dx_ref, dxb_ref, dw_ref = refs[-3:]

        @pl.when(pl.program_id(0) == 0)
        def _():
            dw_ref[...] = jnp.zeros_like(dw_ref)

        g = _rows_product(refs[0], refs[1], True, None)
        for p in range(1, n_pairs):
            g = g + _rows_product(refs[2 * p], refs[2 * p + 1], True, None)
        r = r_ref[...]
        xhat = x_ref[...] * r
        gw = g * w_ref[...]
        dx = d_ref[...] + r * (gw - xhat * jnp.mean(gw * xhat, axis=1, keepdims=True))
        dx_ref[...] = dx
        dxb_ref[...] = dx.astype(BF16)
        dw_ref[...] += jnp.sum(g * xhat, axis=0, keepdims=True)

    row = pl.BlockSpec((tm, d), lambda i: (i, 0))
    vec = pl.BlockSpec((1, d), lambda i: (0, 0))
    ins, specs = [], []
    for a, b in pairs:
        k = a.shape[1]
        ins += [a, b]
        specs += [pl.BlockSpec((tm, k), lambda i: (i, 0)), pl.BlockSpec((d, k), lambda i: (0, 0))]
    ins += [x, rstd, w, dres]
    specs += [row, pl.BlockSpec((tm, 1), lambda i: (i, 0)), vec, row]
    if after is not None:
        ins.append(after)
        specs.append(pl.BlockSpec(memory_space=pl.ANY))
    return pl.pallas_call(
        body, name=name, grid=(m // tm,), in_specs=specs, out_specs=[row, row, vec],
        out_shape=[jax.ShapeDtypeStruct((m, d), F32), jax.ShapeDtypeStruct((m, d), BF16),
                   jax.ShapeDtypeStruct((1, d), F32)],
        compiler_params=_cparams(("arbitrary",)),
    )(*ins)


def _softplus(x):
    return jnp.maximum(x, 0.0) + jnp.log(1.0 + jnp.exp(-jnp.abs(x)))


def _prep(proj_a, bias128, alog128, bl, t):
    n = bl * t
    nch = t // CHUNK
    col0 = (SSD_WIDTH + CONV_CH) // 128

    def body(p_ref, b_ref, al_ref, dt_ref, gd_ref, ac_ref, act_ref, negc_ref):
        negc_ref[...] = jnp.zeros_like(negc_ref)
        row = lax.broadcasted_iota(jnp.int32, (CHUNK, CHUNK), 0)
        col = lax.broadcasted_iota(jnp.int32, (CHUNK, CHUNK), 1)
        tril = (row >= col).astype(F32)
        lane = lax.broadcasted_iota(jnp.int32, (1, 128), 1)
        head_lanes = lane < 16
        a_row = -jnp.exp(al_ref[...])
        carry = jnp.zeros((1, 128), F32)
        for ci in range(nch):
            rows = slice(ci * CHUNK, (ci + 1) * CHUNK)
            xv = p_ref[rows, :] + b_ref[...]
            sp = _softplus(xv)
            acum = jnp.dot(tril, a_row * sp, precision=HIGHEST, preferred_element_type=F32)
            c = jnp.dot(tril, -_softplus(-xv), precision=HIGHEST, preferred_element_type=F32) + carry
            carry = c[CHUNK - 1:CHUNK, :]
            dt_ref[rows, :] = jnp.where(head_lanes, sp, 0.0)
            gd_ref[rows, :] = jnp.where(head_lanes, jax.nn.sigmoid(xv),
                                        jnp.where(lane < 32, jax.nn.sigmoid(-xv), 0.0))
            ac_ref[rows, :] = jnp.where(head_lanes, acum, 0.0)
            act_ref[:, rows] = jnp.transpose(acum)[0:16, :]
            c_t = jnp.transpose(c)
            for hp in range(8):
                negc_ref[hp, 0:2, rows] = -c_t[16 + 2 * hp:18 + 2 * hp, :]

    o128 = pl.BlockSpec((t, 128), lambda b: (b, 0))
    v128 = pl.BlockSpec((1, 128), lambda b: (0, 0))
    w128 = jax.ShapeDtypeStruct((n, 128), F32)
    return pl.pallas_call(
        body, name="head_scalars", grid=(bl,),
        in_specs=[pl.BlockSpec((t, 128), lambda b: (b, col0)), v128, v128],
        out_specs=[o128, o128, o128, pl.BlockSpec((16, t), lambda b: (0, b)),
                   pl.BlockSpec((None, 8, 8, t), lambda b: (b, 0, 0, 0))],
        out_shape=[w128, w128, w128, jax.ShapeDtypeStruct((16, n), F32),
                   jax.ShapeDtypeStruct((bl, 8, 8, t), F32)],
        compiler_params=_cparams(("parallel",)),
    )(proj_a, bias128, alog128)


def _fpost(dc, gate_d, ddt, dpa, bl, t):
    n = bl * t
    nch = t // CHUNK
    col0 = (SSD_WIDTH + CONV_CH) // 128

    def body(dc_ref, gd_ref, ddt_ref, dpa_in, out_ref, db_ref):
        @pl.when(pl.program_id(0) == 0)
        def _():
            db_ref[...] = jnp.zeros_like(db_ref)

        row = lax.broadcasted_iota(jnp.int32, (CHUNK, CHUNK), 0)
        col = lax.broadcasted_iota(jnp.int32, (CHUNK, CHUNK), 1)
        triu = (row <= col).astype(F32)
        lane = lax.broadcasted_iota(jnp.int32, (1, 128), 1)
        gate_lanes = (lane >= 16) & (lane < 32)
        carry = jnp.zeros((1, 128), F32)
        db = jnp.zeros((1, 128), F32)
        for ci in reversed(range(nch)):
            rows = slice(ci * CHUNK, (ci + 1) * CHUNK)
            dlf = jnp.dot(triu, dc_ref[rows, :], precision=HIGHEST, preferred_element_type=F32) + carry
            carry = dlf[0:1, :]
            df = jnp.where(gate_lanes, dlf * gd_ref[rows, :], 0.0)
            out_ref[rows, :] = (ddt_ref[rows, :] + df).astype(BF16)
            db = db + jnp.sum(df, axis=0, keepdims=True)
        db_ref[...] += db[:, 16:32]

    blk = pl.BlockSpec((t, 128), lambda b: (b, 0))
    return pl.pallas_call(
        body, name="forget_gate_bwd", grid=(bl,),
        in_specs=[blk, blk, blk, ANY],
        out_specs=[pl.BlockSpec((t, 128), lambda b: (b, col0)), pl.BlockSpec((1, 16), lambda b: (0, 0))],
        out_shape=[jax.ShapeDtypeStruct(dpa.shape, dpa.dtype), jax.ShapeDtypeStruct((1, 16), F32)],
        input_output_aliases={3: 0},
        compiler_params=_cparams(("arbitrary",)),
    )(dc, gate_d, ddt, dpa)


CONV_TILE = 256
CONV_ROWS = 256


def _conv_taps(u_ref, i):
    r0 = pl.multiple_of(i * CONV_ROWS, CONV_ROWS)
    cur = u_ref[pl.ds(r0, CONV_ROWS), :]
    p0 = pl.multiple_of(jnp.maximum(r0 - 8, 0), 8)
    prev = jnp.where(i > 0, u_ref[pl.ds(p0, 8), :], 0.0)
    cat = jnp.concatenate([prev, cur], axis=0)
    return r0, [cur] + [pltpu.roll(cat, s, 0)[8:, :] for s in (1, 2, 3)]


def _conv_fwd(proj_a, conv_w, conv_b, bl, t):
    n = bl * t
    nct = CONV_CH // CONV_TILE
    c0 = SSD_WIDTH // CONV_TILE

    def body(u_ref, w_ref, b_ref, o_ref, d_ref):
        w = w_ref[...]
        bias = b_ref[...]

        def chunk(i, carry):
            r0, taps = _conv_taps(u_ref, i)
            pre = bias + w[3:4, :] * taps[0]
            for s in (1, 2, 3):
                pre = pre + w[3 - s:4 - s, :] * taps[s]
            sg = jax.nn.sigmoid(pre)
            o_ref[pl.ds(r0, CONV_ROWS), :] = pre * sg
            d_ref[pl.ds(r0, CONV_ROWS), :] = (sg * (1.0 + pre * (1.0 - sg))).astype(BF16)
            return carry

        lax.fori_loop(0, t // CONV_ROWS, chunk, 0)

    out = pl.BlockSpec((t, CONV_TILE), lambda b, c: (b, c))
    return pl.pallas_call(
        body, name="conv_silu_fwd", grid=(bl, nct),
        in_specs=[pl.BlockSpec((t, CONV_TILE), lambda b, c: (b, c0 + c)),
                  pl.BlockSpec((4, CONV_TILE), lambda b, c: (0, c)),
                  pl.BlockSpec((1, CONV_TILE), lambda b, c: (0, c))],
        out_specs=[out, out],
        out_shape=[jax.ShapeDtypeStruct((n, CONV_CH), F32), jax.ShapeDtypeStruct((n, CONV_CH), BF16)],
        compiler_params=_cparams(("parallel", "parallel")),
    )(proj_a, conv_w, conv_b)


def _conv_bwd(dxc, dsilu, proj_a, conv_w, dpa, bl, t):
    nct = CONV_CH // CONV_TILE
    c0 = SSD_WIDTH // CONV_TILE
    nrc = t // CONV_ROWS

    def body(g_ref, s_ref, u_ref, w_ref, dpa_in, du_ref, dw_ref, db_ref, dp_scr):
        @pl.when(pl.program_id(1) == 0)
        def _():
            dw_ref[...] = jnp.zeros_like(dw_ref)
            db_ref[...] = jnp.zeros_like(db_ref)

        w = w_ref[...]
        dp_scr[pl.ds(t, 8), :] = jnp.zeros((8, CONV_TILE), F32)

        def chunk1(i, carry):
            dw0, dw1, dw2, dw3, db = carry
            r0, taps = _conv_taps(u_ref, i)
            dpre = g_ref[pl.ds(r0, CONV_ROWS), :] * s_ref[pl.ds(r0, CONV_ROWS), :].astype(F32)
            dp_scr[pl.ds(r0, CONV_ROWS), :] = dpre
            dw3 = dw3 + jnp.sum(dpre * taps[0], axis=0, keepdims=True)
            dw2 = dw2 + jnp.sum(dpre * taps[1], axis=0, keepdims=True)
            dw1 = dw1 + jnp.sum(dpre * taps[2], axis=0, keepdims=True)
            dw0 = dw0 + jnp.sum(dpre * taps[3], axis=0, keepdims=True)
            db = db + jnp.sum(dpre, axis=0, keepdims=True)
            return dw0, dw1, dw2, dw3, db

        z = jnp.zeros((1, CONV_TILE), F32)
        dw0, dw1, dw2, dw3, db = lax.fori_loop(0, nrc, chunk1, (z, z, z, z, z))
        dw_ref[...] += jnp.concatenate([dw0, dw1, dw2, dw3], axis=0)
        db_ref[...] += db

        def chunk2(i, carry):
            r0 = pl.multiple_of(i * CONV_ROWS, CONV_ROWS)
            cat = dp_scr[pl.ds(r0, CONV_ROWS + 8), :]
            du = w[3:4, :] * cat[:CONV_ROWS, :]
            for s in (1, 2, 3):
                du = du + w[3 - s:4 - s, :] * pltpu.roll(cat, CONV_ROWS + 8 - s, 0)[:CONV_ROWS, :]
            du_ref[pl.ds(r0, CONV_ROWS), :] = du.astype(BF16)
            return carry

        lax.fori_loop(0, nrc, chunk2, 0)

    tile = pl.BlockSpec((t, CONV_TILE), lambda c, b: (b, c))
    return pl.pallas_call(
        body, name="conv_silu_bwd", grid=(nct, bl),
        in_specs=[tile, tile, pl.BlockSpec((t, CONV_TILE), lambda c, b: (b, c0 + c)),
                  pl.BlockSpec((4, CONV_TILE), lambda c, b: (0, c)), ANY],
        out_specs=[pl.BlockSpec((t, CONV_TILE), lambda c, b: (b, c0 + c)),
                   pl.BlockSpec((4, CONV_TILE), lambda c, b: (0, c)),
                   pl.BlockSpec((1, CONV_TILE), lambda c, b: (0, c))],
        out_shape=[jax.ShapeDtypeStruct(dpa.shape, dpa.dtype), jax.ShapeDtypeStruct((4, CONV_CH), F32),
                   jax.ShapeDtypeStruct((1, CONV_CH), F32)],
        input_output_aliases={4: 0},
        scratch_shapes=[pltpu.VMEM((t + 8, CONV_TILE), F32)],
        compiler_params=_cparams(("parallel", "arbitrary")),
    )(dxc, dsilu, proj_a, conv_w, dpa)


SSD_CHUNKS_PER_STEP = 2
NT_DIMS = (((1,), (1,)), ((), ()))
TN_DIMS = (((0,), (0,)), ((), ()))


def _dot(a, b, dims=None):
    if dims is None:
        return jnp.dot(a, b, preferred_element_type=F32)
    return lax.dot_general(a, b, dims, preferred_element_type=F32)


def _head_expander():
    r = lax.broadcasted_iota(jnp.int32, (128, SSD_WIDTH), 0)
    c = lax.broadcasted_iota(jnp.int32, (128, SSD_WIDTH), 1)
    return ((c // HEAD_DIM == r % 16) & (r < 48)).astype(BF16)


def _spread(v128, expander):
    hi = v128.astype(BF16).astype(F32)
    r1 = v128 - hi
    mid = r1.astype(BF16).astype(F32)
    lo = (r1 - mid).astype(BF16).astype(F32)
    packed = (hi + pltpu.roll(mid, 16, 1) + pltpu.roll(lo, 32, 1)).astype(BF16)
    return jnp.dot(packed, expander, preferred_element_type=F32)


def _head_sums(v1024, expander):
    hi = v1024.astype(BF16)
    lo = (v1024 - hi.astype(F32)).astype(BF16)
    heads = jnp.where(lax.broadcasted_iota(jnp.int32, expander.shape, 0) < 16, expander, jnp.zeros_like(expander))
    return _dot(hi, heads, NT_DIMS) + _dot(lo, heads, NT_DIMS)


def _ssd_fwd(xc, proj_a, dt, acum, acum_t, dskip_e, norm_w, bl, t):
    n = bl * t
    nch = t // CHUNK
    L = CHUNK

    def body(xc_blk, z_blk, dt_blk, ac_blk, act_blk, dsk_ref, nw_ref, ys_blk, yp_blk, hp_blk, h_scr, y_scr, x_scr):
        @pl.when(pl.program_id(1) == 0)
        def _():
            h_scr[...] = jnp.zeros_like(h_scr)

        for sub in range(SSD_CHUNKS_PER_STEP):
            rows = pl.ds(sub * L, L)
            chunk(xc_blk.at[rows, :], z_blk.at[rows, :], dt_blk.at[rows, :], ac_blk.at[rows, :], act_blk.at[:, rows],
                  dsk_ref, nw_ref, ys_blk.at[rows, :], yp_blk.at[rows, :], hp_blk.at[sub], h_scr, y_scr, x_scr)

    def chunk(xc_ref, z_ref, dt_ref, ac_ref, act_ref, dsk_ref, nw_ref, ys_ref, yp_ref, hp_ref, h_scr, y_scr, x_scr):
        row = lax.broadcasted_iota(jnp.int32, (L, L), 0)
        col = lax.broadcasted_iota(jnp.int32, (L, L), 1)
        causal = row >= col
        lane128 = lax.broadcasted_iota(jnp.int32, (1, L), 1)
        expander = _head_expander()
        ac_all = ac_ref[...]
        act_all = act_ref[...]
        ac_e = _spread(ac_all, expander)
        e_in = jnp.exp(ac_e)
        dec = jnp.exp(ac_e[L - 1:L, :] - ac_e)
        xs_all = xc_ref[:, 0:SSD_WIDTH]
        x_all = xs_all * _spread(dt_ref[...], expander)
        x_scr[...] = x_all.astype(BF16)
        hp_all = h_scr[...]
        hp_ref[...] = hp_all
        for g in range(2):
            gs = slice(g * 512, (g + 1) * 512)
            bg = xc_ref[:, SSD_WIDTH + g * 128:SSD_WIDTH + (g + 1) * 128].astype(BF16)
            cg = xc_ref[:, SSD_WIDTH + 256 + g * 128:SSD_WIDTH + 256 + (g + 1) * 128].astype(BF16)
            gmat = _dot(cg, bg, NT_DIMS)
            y_off = _dot(cg, hp_all[gs, :].astype(BF16), NT_DIMS) * e_in[:, gs] + dsk_ref[:, gs] * xs_all[:, gs]
            s_new = _dot((x_all[:, gs] * dec[:, gs]).astype(BF16), bg, TN_DIMS)
            for pr in range(4):
                pair = slice((g * 4 + pr) * 128, (g * 4 + pr + 1) * 128)
                x_pair = x_scr[:, pair]
                y_pair = y_off[:, pr * 128:(pr + 1) * 128]
                for j in range(2):
                    h = g * 8 + 2 * pr + j
                    sl = slice(h * HEAD_DIM, (h + 1) * HEAD_DIM)
                    r = 2 * pr + j
                    ldec = jnp.exp(jnp.where(causal, ac_all[:, h:h + 1] - act_all[h:h + 1, :], NEG))
                    x_head = jnp.where((lane128 < HEAD_DIM) == (j == 0), x_pair, jnp.zeros_like(x_pair))
                    y_pair = y_pair + _dot((gmat * ldec).astype(BF16), x_head)
                    elast = jnp.exp(ac_all[L - 1:L, h:h + 1])
                    h_scr[sl, :] = elast * hp_all[sl, :] + s_new[r * HEAD_DIM:(r + 1) * HEAD_DIM, :]
                y_scr[:, pair] = y_pair
        y = y_scr[...]
        yp_ref[...] = y
        zv = z_ref[...]
        yg = y * (zv * jax.nn.sigmoid(zv))
        for g in range(2):
            gs = slice(g * 512, (g + 1) * 512)
            grp = yg[:, gs]
            rstd = lax.rsqrt(jnp.mean(grp * grp, axis=1, keepdims=True) + EPS)
            ys_ref[:, gs] = (grp * rstd * nw_ref[:, gs]).astype(BF16)

    cps = SSD_CHUNKS_PER_STEP
    steps = nch // cps
    rb = lambda b, c: (b * steps + c, 0)
    v1k = pl.BlockSpec((1, SSD_WIDTH), lambda b, c: (0, 0))
    return pl.pallas_call(
        body, name="ssd_fwd", grid=(bl, steps),
        in_specs=[pl.BlockSpec((cps * L, CONV_CH), rb), pl.BlockSpec((cps * L, SSD_WIDTH), rb),
                  pl.BlockSpec((cps * L, 128), rb), pl.BlockSpec((cps * L, 128), rb),
                  pl.BlockSpec((16, cps * L), lambda b, c: (0, b * steps + c)), v1k, v1k],
        out_specs=[pl.BlockSpec((cps * L, SSD_WIDTH), rb), pl.BlockSpec((cps * L, SSD_WIDTH), rb),
                   pl.BlockSpec((cps, SSD_WIDTH, SSD_STATE), lambda b, c: (b * steps + c, 0, 0))],
        out_shape=[jax.ShapeDtypeStruct((n, SSD_WIDTH), BF16), jax.ShapeDtypeStruct((n, SSD_WIDTH), F32),
                   jax.ShapeDtypeStruct((bl * nch, SSD_WIDTH, SSD_STATE), F32)],
        scratch_shapes=[pltpu.VMEM((SSD_WIDTH, SSD_STATE), F32), pltpu.VMEM((L, SSD_WIDTH), F32),
                        pltpu.VMEM((L, SSD_WIDTH), BF16)],
        compiler_params=_cparams(("parallel", "arbitrary")),
    )(xc, proj_a, dt, acum, acum_t, dskip_e, norm_w)


def _ssd_bwd(dys, xc, proj_a, ypre, hprev, dt, gate_d, acum, acum_t, alog128, dskip_e, norm_w, bl, t):
    n = bl * t
    nch = t // CHUNK
    L = CHUNK

    def body(dys_blk, xc_blk, z_blk, yp_blk, hp_blk, dt_blk, gd_blk, ac_blk, act_blk, al_ref, dsk_ref, nw_ref,
             dxc_blk, dz_blk, ddt_blk, dnw_ref, dsk16_ref, da16_ref, db16_ref,
             dh_scr, dy_scr, x_scr, dx_scr, red_scr):
        first = (pl.program_id(0) == 0) & (pl.program_id(1) == 0)

        @pl.when(first)
        def _():
            dnw_ref[...] = jnp.zeros_like(dnw_ref)
            dsk16_ref[...] = jnp.zeros_like(dsk16_ref)
            da16_ref[...] = jnp.zeros_like(da16_ref)
            db16_ref[...] = jnp.zeros_like(db16_ref)

        @pl.when(pl.program_id(1) == 0)
        def _():
            dh_scr[...] = jnp.zeros_like(dh_scr)

        for sub in reversed(range(SSD_CHUNKS_PER_STEP)):
            rows = pl.ds(sub * L, L)
            chunk(dys_blk.at[rows, :], xc_blk.at[rows, :], z_blk.at[rows, :], yp_blk.at[rows, :], hp_blk.at[sub],
                  dt_blk.at[rows, :], gd_blk.at[rows, :], ac_blk.at[rows, :], act_blk.at[:, rows], al_ref, dsk_ref,
                  nw_ref, dxc_blk.at[rows, :], dz_blk.at[rows, :], ddt_blk.at[rows, :], dnw_ref, dsk16_ref, da16_ref,
                  db16_ref, dh_scr, dy_scr, x_scr, dx_scr, red_scr)

    def chunk(dys_ref, xc_ref, z_ref, yp_ref, hp_ref, dt_ref, gd_ref, ac_ref, act_ref, al_ref, dsk_ref, nw_ref,
              dxc_ref, dz_ref, ddt_ref, dnw_ref, dsk16_ref, da16_ref, db16_ref,
              dh_scr, dy_scr, x_scr, dx_scr, red_scr):
        y = yp_ref[...]
        zv = z_ref[...]
        sz = jax.nn.sigmoid(zv)
        gate = zv * sz
        yg = y * gate
        dout = dys_ref[...]
        nw = nw_ref[...]
        for g in range(2):
            gs = slice(g * 512, (g + 1) * 512)
            grp = yg[:, gs]
            rstd = lax.rsqrt(jnp.mean(grp * grp, axis=1, keepdims=True) + EPS)
            ghat = grp * rstd
            dnw_ref[:, gs] += jnp.sum(dout[:, gs] * ghat, axis=0, keepdims=True)
            gw = dout[:, gs] * nw[:, gs]
            dyg = rstd * (gw - ghat * jnp.mean(gw * ghat, axis=1, keepdims=True))
            dy_scr[:, gs] = dyg * gate[:, gs]
            dz_ref[:, gs] = (dyg * y[:, gs] * (sz[:, gs] * (1.0 + zv[:, gs] * (1.0 - sz[:, gs])))).astype(BF16)

        row = lax.broadcasted_iota(jnp.int32, (L, L), 0)
        col = lax.broadcasted_iota(jnp.int32, (L, L), 1)
        causal = row >= col
        lane128 = lax.broadcasted_iota(jnp.int32, (1, L), 1)
        rows128 = lax.broadcasted_iota(jnp.int32, (L, 1), 0)
        last_row = rows128 == (L - 1)
        expander = _head_expander()
        ac_all = ac_ref[...]
        act_all = act_ref[...]
        dt_all = dt_ref[...]
        dt_e = _spread(dt_all, expander)
        ac_e = _spread(ac_all, expander)
        e_in = jnp.exp(ac_e)
        dec = jnp.exp(ac_e[L - 1:L, :] - ac_e)
        xs_all = xc_ref[:, 0:SSD_WIDTH]
        x_all = xs_all * dt_e
        x_scr[...] = x_all.astype(BF16)
        dy_all = dy_scr[...]
        hp_all = hp_ref[...]
        ds_all = dh_scr[...]
        dsk_cols = jnp.sum(dy_all * xs_all, axis=0, keepdims=True)
        dac = jnp.zeros((L, L), F32)
        dac_row = jnp.zeros((L, L), F32)
        ddec_cols = []
        for g in range(2):
            gs = slice(g * 512, (g + 1) * 512)
            bsl = slice(SSD_WIDTH + g * 128, SSD_WIDTH + (g + 1) * 128)
            csl = slice(SSD_WIDTH + 256 + g * 128, SSD_WIDTH + 256 + (g + 1) * 128)
            bg = xc_ref[:, bsl].astype(BF16)
            cg = xc_ref[:, csl].astype(BF16)
            gmat = _dot(cg, bg, NT_DIMS)
            hpb = hp_all[gs, :].astype(BF16)
            dsb = ds_all[gs, :].astype(BF16)
            ch = _dot(cg, hpb, NT_DIMS)
            dye = dy_all[:, gs] * e_in[:, gs]
            dyeb = dye.astype(BF16)
            dc_acc = _dot(dyeb, hpb)
            dhp = _dot(dyeb, cg, TN_DIMS)
            dxd = _dot(bg, dsb, NT_DIMS)
            db_acc = _dot((x_all[:, gs] * dec[:, gs]).astype(BF16), dsb)
            ddec = dxd * x_all[:, gs] * dec[:, gs]
            ddec_cols.append(jnp.sum(ddec, axis=0, keepdims=True))
            dx_inter = dxd * dec[:, gs]
            red_scr[:, gs] = dye * ch - ddec
            dg_sum = jnp.zeros((L, L), F32)
            for pr in range(4):
                pair = slice((g * 4 + pr) * 128, (g * 4 + pr + 1) * 128)
                x_pair = x_scr[:, pair]
                dy_pair = dy_scr[:, pair].astype(BF16)
                dx_pair = dx_inter[:, pr * 128:(pr + 1) * 128]
                for j in range(2):
                    h = g * 8 + 2 * pr + j
                    r = 2 * pr + j
                    sl = slice(h * HEAD_DIM, (h + 1) * HEAD_DIM)
                    onehot_w = lane128 == h
                    ldec = jnp.exp(jnp.where(causal, ac_all[:, h:h + 1] - act_all[h:h + 1, :], NEG))
                    mf = gmat * ldec
                    dyb = jnp.where((lane128 < HEAD_DIM) == (j == 0), dy_pair, jnp.zeros_like(dy_pair))
                    dm = _dot(dyb, x_pair, NT_DIMS)
                    dx_pair = dx_pair + _dot(mf.astype(BF16), dyb, TN_DIMS)
                    dg_sum = dg_sum + dm * ldec
                    wmat = dm * mf
                    elast = jnp.exp(ac_all[L - 1:L, h:h + 1])
                    hp_h = hp_all[sl, :]
                    ds_h = ds_all[sl, :]
                    extra = elast * jnp.sum(jnp.sum(hp_h * ds_h, axis=1, keepdims=True), axis=0, keepdims=True)
                    dac = dac + jnp.where(onehot_w,
                                          jnp.sum(wmat, axis=1, keepdims=True) + jnp.where(last_row, extra, 0.0), 0.0)
                    dac_row = dac_row + jnp.where(rows128 == h, -jnp.sum(wmat, axis=0, keepdims=True), 0.0)
                    dh_scr[sl, :] = elast * ds_h + dhp[r * HEAD_DIM:(r + 1) * HEAD_DIM, :]
                dx_scr[:, pair] = dx_pair
            dgb = dg_sum.astype(BF16)
            dxc_ref[:, csl] = dc_acc + _dot(dgb, bg)
            dxc_ref[:, bsl] = db_acc + _dot(dgb, cg, TN_DIMS)
        dx_all = dx_scr[...]
        dxc_ref[:, 0:SSD_WIDTH] = dx_all * dt_e + dsk_ref[...] * dy_all
        red = red_scr[...]
        dac_slab = _head_sums(red, expander)
        ddec_tot = _head_sums(jnp.broadcast_to(jnp.concatenate(ddec_cols, axis=1), (8, SSD_WIDTH)), expander)
        ddt_x = _head_sums(dx_all * xs_all, expander)
        dsk16_ref[...] += _head_sums(jnp.broadcast_to(dsk_cols, (8, SSD_WIDTH)), expander)[0:1, 0:16]
        dac = dac + dac_slab + jnp.transpose(dac_row) + jnp.where(last_row, ddec_tot[0:1, :], 0.0)
        triu = (row <= col).astype(F32)
        da = jnp.dot(triu, dac, precision=HIGHEST, preferred_element_type=F32)
        a_row = -jnp.exp(al_ref[...])
        ddt = jnp.where(lane128 < 16, (ddt_x + da * a_row) * gd_ref[...], 0.0)
        ddt_ref[...] = ddt
        da16_ref[...] += (jnp.sum(da * dt_all, axis=0, keepdims=True) * a_row)[:, 0:16]
        db16_ref[...] += jnp.sum(ddt, axis=0, keepdims=True)[:, 0:16]

    cps = SSD_CHUNKS_PER_STEP
    steps = nch // cps
    rb = lambda b, c: (b * steps + steps - 1 - c, 0)
    v1k = pl.BlockSpec((1, SSD_WIDTH), lambda b, c: (0, 0))
    v16 = pl.BlockSpec((1, 16), lambda b, c: (0, 0))
    v128 = pl.BlockSpec((1, 128), lambda b, c: (0, 0))
    wide = pl.BlockSpec((cps * L, SSD_WIDTH), rb)
    s128 = pl.BlockSpec((cps * L, 128), rb)
    return pl.pallas_call(
        body, name="ssd_bwd", grid=(bl, steps),
        in_specs=[wide, pl.BlockSpec((cps * L, CONV_CH), rb), wide, wide,
                  pl.BlockSpec((cps, SSD_WIDTH, SSD_STATE), lambda b, c: (b * steps + steps - 1 - c, 0, 0)),
                  s128, s128, s128, pl.BlockSpec((16, cps * L), lambda b, c: (0, b * steps + steps - 1 - c)),
                  v128, v1k, v1k],
        out_specs=[pl.BlockSpec((cps * L, CONV_CH), rb), wide, s128, v1k, v16, v16, v16],
        out_shape=[jax.ShapeDtypeStruct((n, CONV_CH), F32), jax.ShapeDtypeStruct((n, PA_WIDTH), BF16),
                   jax.ShapeDtypeStruct((n, 128), F32), jax.ShapeDtypeStruct((1, SSD_WIDTH), F32),
                   jax.ShapeDtypeStruct((1, 16), F32), jax.ShapeDtypeStruct((1, 16), F32),
                   jax.ShapeDtypeStruct((1, 16), F32)],
        scratch_shapes=[pltpu.VMEM((SSD_WIDTH, SSD_STATE), F32), pltpu.VMEM((L, SSD_WIDTH), F32),
                        pltpu.VMEM((L, SSD_WIDTH), BF16), pltpu.VMEM((L, SSD_WIDTH), F32),
                        pltpu.VMEM((L, SSD_WIDTH), F32)],
        compiler_params=_cparams(("arbitrary", "arbitrary")),
    )(dys, xc, proj_a, ypre, hprev, dt, gate_d, acum, acum_t, alog128, dskip_e, norm_w)


def _attn_fwd(qkv, negc, bl, t):
    n = bl * t
    tb_ = ATT_BLOCK
    nb = t // tb_
    scale2 = LOG2E / math.sqrt(HEAD_DIM)

    def body(q_ref, k_ref, v_ref, c_ref, o_ref, lse_ref, v0_scr, v1_scr, k0_scr, k1_scr):
        row = lax.broadcasted_iota(jnp.int32, (tb_, tb_), 0)
        col = lax.broadcasted_iota(jnp.int32, (tb_, tb_), 1)
        causal = row >= col
        lane = lax.broadcasted_iota(jnp.int32, (1, 128), 1)
        v_pair = v_ref[...].astype(F32)
        k_pair = k_ref[...]
        v_scrs = (v0_scr, v1_scr)
        k_scrs = (k0_scr, k1_scr)
        for j in range(2):
            v_head = v_pair if j == 0 else pltpu.roll(v_pair, HEAD_DIM, 1)
            v_scrs[j][...] = jnp.where(lane < HEAD_DIM, v_head, jnp.where(lane == HEAD_DIM, 1.0, 0.0)).astype(BF16)
            k_scrs[j][...] = jnp.where((lane < HEAD_DIM) == (j == 0), k_pair, jnp.zeros_like(k_pair))
        for qi in range(nb):
            r0, lk = qi * tb_, (qi + 1) * tb_
            for j in range(2):
                sl = slice(j * HEAD_DIM, (j + 1) * HEAD_DIM)
                s = _dot(q_ref[r0:lk, :], k_scrs[j][0:lk, :], NT_DIMS) * scale2 + c_ref[j:j + 1, 0:lk] * LOG2E
                tail = jnp.where(causal, s[:, r0:lk], NEG)
                s = tail if qi == 0 else jnp.concatenate([s[:, 0:r0], tail], axis=1)
                m = jnp.max(s, axis=1, keepdims=True)
                p = jnp.exp2(s - m)
                acc = _dot(p.astype(BF16), v_scrs[j][0:lk, :])
                l = acc[:, HEAD_DIM:HEAD_DIM + 1]
                o_ref[r0:lk, sl] = (acc[:, 0:HEAD_DIM] / l).astype(BF16)
                lse_ref[r0:lk, sl] = jnp.broadcast_to(m + jnp.log(l) * LOG2E, (tb_, HEAD_DIM))

    blk = lambda off: pl.BlockSpec((t, 128), lambda b, hp: (b, off + hp))
    return pl.pallas_call(
        body, name="fox_attn_fwd", grid=(bl, 8),
        in_specs=[blk(0), blk(8), blk(16), pl.BlockSpec((None, None, 8, t), lambda b, hp: (b, hp, 0, 0))],
        out_specs=[blk(0), blk(0)],
        out_shape=[jax.ShapeDtypeStruct((n, ATT_WIDTH), BF16), jax.ShapeDtypeStruct((n, ATT_WIDTH), F32)],
        scratch_shapes=[pltpu.VMEM((t, 128), BF16)] * 4,
        compiler_params=_cparams(("parallel", "parallel")),
    )(qkv, qkv, qkv, negc)


def _attn_bwd(qkv, do, o, lse, negc, after, bl, t):
    n = bl * t
    tb_ = ATT_BLOCK
    nb = t // tb_
    scale = 1.0 / math.sqrt(HEAD_DIM)
    scale2 = LOG2E * scale

    def body(q_ref, k_ref, v_ref, do_ref, o_ref, lse_ref, c_ref, after_ref, dq_ref, dk_ref, dv_ref, dc_ref,
             dq0_scr, delta_scr, dq1_scr, qt0_scr, qt1_scr, dot_scr, dkt0_scr, dkt1_scr, dvt_scr):
        row = lax.broadcasted_iota(jnp.int32, (tb_, tb_), 0)
        col = lax.broadcasted_iota(jnp.int32, (tb_, tb_), 1)
        causal = row >= col
        lane = lax.broadcasted_iota(jnp.int32, (1, 128), 1)
        dq_scrs = (dq0_scr, dq1_scr)
        qt_scrs = (qt0_scr, qt1_scr)
        dkt_scrs = (dkt0_scr, dkt1_scr)
        dq0_scr[...] = jnp.zeros_like(dq0_scr)
        dq1_scr[...] = jnp.zeros_like(dq1_scr)
        dc_ref[...] = jnp.zeros_like(dc_ref)
        q_t = jnp.transpose(q_ref[...].astype(F32))
        ones_row = jnp.where(lax.broadcasted_iota(jnp.int32, (8, t), 0) == 0, 1.0, 0.0)
        for j in range(2):
            qt_scrs[j][...] = jnp.concatenate(
                [q_t[j * HEAD_DIM:(j + 1) * HEAD_DIM, :], ones_row, jnp.zeros((HEAD_DIM - 8, t), F32)],
                axis=0).astype(BF16)
        dot_scr[...] = jnp.transpose(do_ref[...].astype(F32)).astype(BF16)
        prod = do_ref[...].astype(F32) * o_ref[...].astype(F32)
        for j in range(2):
            sl = slice(j * HEAD_DIM, (j + 1) * HEAD_DIM)
            delta_scr[:, sl] = jnp.broadcast_to(jnp.sum(prod[:, sl], axis=1, keepdims=True), (t, HEAD_DIM))
        for kj in range(nb):
            r0, r1 = kj * tb_, (kj + 1) * tb_
            k_blk = k_ref[r0:r1, :]
            v_blk = v_ref[r0:r1, :]
            k_pair = k_blk.astype(F32)
            for j in range(2):
                sl = slice(j * HEAD_DIM, (j + 1) * HEAD_DIM)
                one = slice(j * HEAD_DIM, j * HEAD_DIM + 1)
                own = (lane < HEAD_DIM) == (j == 0)
                k_head = k_pair if j == 0 else pltpu.roll(k_pair, HEAD_DIM, 1)
                k_ones = jnp.where(lane < HEAD_DIM, k_head, jnp.where(lane == HEAD_DIM, 1.0, 0.0)).astype(BF16)
                s = (_dot(q_ref[r0:t, :], jnp.where(own, k_blk, jnp.zeros_like(k_blk)), NT_DIMS) * scale2
                     + c_ref[j:j + 1, r0:r1] * LOG2E)
                head = jnp.where(causal, s[0:tb_, :], NEG)
                s = head if kj == nb - 1 else jnp.concatenate([head, s[tb_:, :]], axis=0)
                p = jnp.exp2(s - lse_ref[r0:t, one])
                dp = _dot(do_ref[r0:t, :], jnp.where(own, v_blk, jnp.zeros_like(v_blk)), NT_DIMS)
                ds = p * (dp - delta_scr[r0:t, one])
                dsb = ds.astype(BF16)
                dvt_scr[sl, r0:r1] = _dot(dot_scr[sl, r0:t], p.astype(BF16))
                dkt_scrs[j][:, r0:r1] = _dot(qt_scrs[j][:, r0:t], dsb)
                dq_scrs[j][r0:t, :] += _dot(dsb, k_ones)
        dv_ref[...] = jnp.transpose(dvt_scr[...]).astype(BF16)
        for j in range(2):
            sl = slice(j * HEAD_DIM, (j + 1) * HEAD_DIM)
            acc = dq_scrs[j][...]
            dkt = dkt_scrs[j][...]
            dq_ref[:, sl] = (acc[:, 0:HEAD_DIM] * scale).astype(BF16)
            dk_ref[:, sl] = (jnp.transpose(dkt)[:, 0:HEAD_DIM] * scale).astype(BF16)
            dc_ref[j:j + 1, :] = jnp.transpose(acc)[HEAD_DIM:HEAD_DIM + 1, :] - dkt[HEAD_DIM:HEAD_DIM + 1, :]

    blk = lambda off: pl.BlockSpec((t, 128), lambda b, hp: (b, off + hp))
    cblk = pl.BlockSpec((None, None, 8, t), lambda b, hp: (b, hp, 0, 0))
    return pl.pallas_call(
        body, name="fox_attn_bwd", grid=(bl, 8),
        in_specs=[blk(0), blk(8), blk(16), blk(0), blk(0), blk(0), cblk, ANY],
        out_specs=[blk(0), blk(0), blk(0), cblk],
        out_shape=[jax.ShapeDtypeStruct((n, ATT_WIDTH), BF16)] * 3 + [jax.ShapeDtypeStruct((bl, 8, 8, t), F32)],
        scratch_shapes=[pltpu.VMEM((t, 128), F32), pltpu.VMEM((t, 128), F32), pltpu.VMEM((t, 128), F32),
                        pltpu.VMEM((128, t), BF16), pltpu.VMEM((128, t), BF16), pltpu.VMEM((128, t), BF16),
                        pltpu.VMEM((128, t), F32), pltpu.VMEM((128, t), F32), pltpu.VMEM((128, t), F32)],
        compiler_params=_cparams(("parallel", "parallel")),
    )(qkv, qkv, qkv, do, o, lse, negc, after)


def _adamw(w, g, m, v, *, name):
    lead = w.ndim == 3
    r, c = w.shape[-2:]
    tr = _pick(r, (256, IN_SHARD // 3, 128, 64, 32, 16, 8))
    bc1 = 1.0 - ADAM_B1 ** ADAM_STEP
    bc2 = 1.0 - ADAM_B2 ** ADAM_STEP

    def body(w_ref, g_ref, m_ref, v_ref, d_ref, nm_ref, nv_ref):
        gv = g_ref[...]
        mn = ADAM_B1 * m_ref[...] + (1.0 - ADAM_B1) * gv
        vn = ADAM_B2 * v_ref[...] + (1.0 - ADAM_B2) * (gv * gv)
        m_hat = mn / bc1
        v_hat = vn / bc2
        d_ref[...] = -ADAM_LR * (m_hat / (jnp.sqrt(v_hat) + ADAM_EPS) + ADAM_WD * w_ref[...])
        nm_ref[...] = mn
        nv_ref[...] = vn

    flat = pl.BlockSpec((tr, c), lambda i: (i, 0))
    blk = pl.BlockSpec((None, tr, c), lambda i: (0, i, 0)) if lead else flat
    return pl.pallas_call(
        body, name=name, grid=(r // tr,), in_specs=[blk, flat, blk, blk], out_specs=[blk] * 3,
        out_shape=[jax.ShapeDtypeStruct(w.shape, F32)] * 3,
        compiler_params=_cparams(("parallel",)),
    )(w, g, m, v)


def _sum_leading(parts, *, name, out_dtype=F32):
    k, r, c = parts.shape
    tr = _pick(r, (512, 256, 128, 96, 64, 32, 16, 8))

    def body(p_ref, o_ref):
        acc = p_ref[0].astype(F32)
        for i in range(1, k):
            acc = acc + p_ref[i].astype(F32)
        o_ref[...] = acc.astype(out_dtype)

    return pl.pallas_call(
        body, name=name, grid=(r // tr,),
        in_specs=[pl.BlockSpec((k, tr, c), lambda i: (0, i, 0))],
        out_specs=pl.BlockSpec((tr, c), lambda i: (i, 0)),
        out_shape=jax.ShapeDtypeStruct((r, c), out_dtype),
        compiler_params=_cparams(("parallel",)),
    )(parts)


def _add_pair(a, b, *, name):
    k, r, c = a.shape
    tr = _pick(r, (512, 256, 128))

    def body(a_ref, b_ref, o_ref):
        o_ref[...] = (a_ref[...].astype(F32) + b_ref[...].astype(F32)).astype(BF16)

    blk = pl.BlockSpec((None, tr, c), lambda j, i: (j, i, 0))
    return pl.pallas_call(
        body, name=name, grid=(k, r // tr), in_specs=[blk, blk], out_specs=blk,
        out_shape=jax.ShapeDtypeStruct((k, r, c), BF16),
        compiler_params=_cparams(("parallel", "parallel")),
    )(a, b)


ANY = pl.BlockSpec(memory_space=pl.ANY)


def _chip_peers(x, y):
    return [(1 - x, y, 2 * (1 - x) + y), (x, 1 - y, 2 * x + 1 - y), (1 - x, 1 - y, 2 * (1 - x) + 1 - y)]


def _gather_weights(blob, *, name):
    rows, cols = blob.shape
    half_rows = rows // 2

    def body(b_ref, o_ref, send_sems, recv_sems):
        x, y, c = lax.axis_index("x"), lax.axis_index("y"), lax.axis_index("c")
        me = 2 * x + y
        sibling = (x, y, 1 - c)
        peers = _chip_peers(x, y)

        def half(chip, hc):
            return o_ref.at[chip, pl.ds(hc * half_rows, half_rows), :]

        def copy(k, src, chip, hc, to):
            return pltpu.make_async_remote_copy(src_ref=src, dst_ref=half(chip, hc), send_sem=send_sems.at[k],
                                                recv_sem=recv_sems.at[k], device_id=to, device_id_type=MESH)

        my_half = b_ref.at[pl.ds(c * half_rows, half_rows), :]
        first = [copy(k, my_half, me, c, (px, py, c)) for k, (px, py, _) in enumerate(peers)]
        for cp in first:
            cp.start()
        passed = [copy(3 + k, half(pc, c), pc, c, sibling) for k, (_, _, pc) in enumerate(peers)]
        for k, (px, py, pc) in enumerate(peers):
            copy(k, my_half, pc, c, (px, py, c)).wait_recv()
            passed[k].start()
        for k, (_, _, pc) in enumerate(peers):
            copy(3 + k, half(pc, 1 - c), pc, 1 - c, sibling).wait_recv()
        for cp in first + passed:
            cp.wait_send()

    return pl.pallas_call(
        body, name=name, in_specs=[ANY], out_specs=ANY,
        out_shape=jax.ShapeDtypeStruct((N_CHIPS, rows, cols), BF16),
        scratch_shapes=[pltpu.SemaphoreType.DMA((6,)), pltpu.SemaphoreType.DMA((6,))],
    )(blob)


def _swap_halves(g, *, name):
    _, rows, cols = g.shape
    half_rows = rows // 2

    def body(g_ref, o_ref, send_sem, recv_sem):
        x, y, c = lax.axis_index("x"), lax.axis_index("y"), lax.axis_index("c")
        cp = pltpu.make_async_remote_copy(
            src_ref=g_ref.at[:, pl.ds((1 - c) * half_rows, half_rows), :], dst_ref=o_ref,
            send_sem=send_sem, recv_sem=recv_sem, device_id=(x, y, 1 - c), device_id_type=MESH)
        cp.start()
        cp.wait()

    return pl.pallas_call(
        body, name=name, in_specs=[ANY], out_specs=ANY,
        out_shape=jax.ShapeDtypeStruct((N_CHIPS, half_rows, cols), BF16),
        scratch_shapes=[pltpu.SemaphoreType.DMA, pltpu.SemaphoreType.DMA],
    )(g)


HBM_SPEC = pl.BlockSpec(memory_space=pltpu.HBM)
SEM_SPEC = pl.BlockSpec(memory_space=pltpu.SEMAPHORE)
SPLIT_EFFECT = pltpu.SideEffectType.DATAFLOW_SIDE_EFFECTING


def _gather_peers_copies(b_ref, land_ref, send_sems, recv_sems, sending):
    x, y, c = lax.axis_index("x"), lax.axis_index("y"), lax.axis_index("c")
    me = 2 * x + y
    half_rows = b_ref.shape[0] // 2
    src = b_ref.at[pl.ds(c * half_rows, half_rows), :]
    return [pltpu.make_async_remote_copy(
        src_ref=src, dst_ref=land_ref.at[me if sending else pc, pl.ds(c * half_rows, half_rows), :],
        send_sem=send_sems.at[k], recv_sem=recv_sems.at[k], device_id=(px, py, c), device_id_type=MESH)
        for k, (px, py, pc) in enumerate(_chip_peers(x, y))]


def _gather_start(blob, after, *, name):
    shape = (N_CHIPS,) + blob.shape

    def body(b_ref, land_ref, after_ref, send_sems, recv_sems, b_thru, land_thru, token):
        for cp in _gather_peers_copies(b_ref, land_ref, send_sems, recv_sems, True):
            cp.start()
        token[...] = jnp.zeros_like(token)

    return pl.pallas_call(
        body, name=name,
        out_shape=(pltpu.SemaphoreType.DMA((3,)), pltpu.SemaphoreType.DMA((3,)), pltpu.HBM(blob.shape, blob.dtype),
                   pltpu.HBM(shape, blob.dtype), jax.ShapeDtypeStruct((8, 128), F32)),
        in_specs=(HBM_SPEC, HBM_SPEC, ANY),
        out_specs=(SEM_SPEC, SEM_SPEC, HBM_SPEC, HBM_SPEC, pl.BlockSpec(memory_space=pltpu.VMEM)),
        input_output_aliases={0: 2, 1: 3},
        compiler_params=pltpu.CompilerParams(has_side_effects=SPLIT_EFFECT),
    )(pltpu.with_memory_space_constraint(blob, pltpu.HBM),
      pltpu.with_memory_space_constraint(lax.empty(shape, blob.dtype), pltpu.HBM), after)


def _gather_wait(send_sems, recv_sems, b_thru, land_thru, after, *, name):
    def body(b_ref, land_ref, send_sems, recv_sems, after_ref, b_dead, got_ref):
        for cp in _gather_peers_copies(b_ref, land_ref, send_sems, recv_sems, False):
            cp.wait_send()
            cp.wait_recv()

    return pl.pallas_call(
        body, name=name,
        out_shape=(pltpu.HBM(b_thru.shape, b_thru.dtype), pltpu.HBM(land_thru.shape, land_thru.dtype)),
        in_specs=(HBM_SPEC, HBM_SPEC, SEM_SPEC, SEM_SPEC, ANY), out_specs=(HBM_SPEC, HBM_SPEC),
        input_output_aliases={0: 0, 1: 1},
        compiler_params=pltpu.CompilerParams(has_side_effects=SPLIT_EFFECT),
    )(b_thru, land_thru, send_sems, recv_sems, after)


def _gather_forward(land, *, name):
    half_rows = land.shape[1] // 2

    def body(l_ref, o_ref, send_sems, recv_sems):
        x, y, c = lax.axis_index("x"), lax.axis_index("y"), lax.axis_index("c")
        cps = []
        for k, (_, _, pc) in enumerate(_chip_peers(x, y)):
            mine = pl.ds(c * half_rows, half_rows)
            cps.append(pltpu.make_async_remote_copy(
                src_ref=l_ref.at[pc, mine, :], dst_ref=o_ref.at[pc, mine, :], send_sem=send_sems.at[k],
                recv_sem=recv_sems.at[k], device_id=(x, y, 1 - c), device_id_type=MESH))
        for cp in cps:
            cp.start()
        for k, (_, _, pc) in enumerate(_chip_peers(x, y)):
            theirs = pl.ds((1 - c) * half_rows, half_rows)
            pltpu.make_async_remote_copy(
                src_ref=l_ref.at[pc, theirs, :], dst_ref=o_ref.at[pc, theirs, :], send_sem=send_sems.at[k],
                recv_sem=recv_sems.at[k], device_id=(x, y, 1 - c), device_id_type=MESH).wait_recv()
        for cp in cps:
            cp.wait_send()

    return pl.pallas_call(
        body, name=name, in_specs=[ANY], out_specs=ANY, input_output_aliases={0: 0},
        out_shape=jax.ShapeDtypeStruct(land.shape, land.dtype),
        scratch_shapes=[pltpu.SemaphoreType.DMA((3,)), pltpu.SemaphoreType.DMA((3,))],
    )(land)


def _exchange_peers_copies(p_ref, land_ref, send_sems, recv_sems, sending):
    x, y, c = lax.axis_index("x"), lax.axis_index("y"), lax.axis_index("c")
    me = 2 * x + y
    return [pltpu.make_async_remote_copy(src_ref=p_ref.at[pc], dst_ref=land_ref.at[me if sending else pc],
                                         send_sem=send_sems.at[k], recv_sem=recv_sems.at[k],
                                         device_id=(px, py, c), device_id_type=MESH)
            for k, (px, py, pc) in enumerate(_chip_peers(x, y))]


def _exchange_start(p, *, name):
    def body(p_ref, land_ref, send_sems, recv_sems, p_thru, land_thru, token):
        for cp in _exchange_peers_copies(p_ref, land_ref, send_sems, recv_sems, True):
            cp.start()
        token[...] = jnp.zeros_like(token)

    return pl.pallas_call(
        body, name=name,
        out_shape=(pltpu.SemaphoreType.DMA((3,)), pltpu.SemaphoreType.DMA((3,)), pltpu.HBM(p.shape, p.dtype),
                   pltpu.HBM(p.shape, p.dtype), jax.ShapeDtypeStruct((8, 128), F32)),
        in_specs=(HBM_SPEC, HBM_SPEC),
        out_specs=(SEM_SPEC, SEM_SPEC, HBM_SPEC, HBM_SPEC, pl.BlockSpec(memory_space=pltpu.VMEM)),
        input_output_aliases={0: 2, 1: 3},
        compiler_params=pltpu.CompilerParams(has_side_effects=SPLIT_EFFECT),
    )(pltpu.with_memory_space_constraint(p, pltpu.HBM),
      pltpu.with_memory_space_constraint(lax.empty(p.shape, p.dtype), pltpu.HBM))


def _exchange_wait(send_sems, recv_sems, p_thru, land_thru, after, *, name):
    def body(p_ref, land_ref, send_sems, recv_sems, after_ref, p_dead, got_ref):
        for cp in _exchange_peers_copies(p_ref, land_ref, send_sems, recv_sems, False):
            cp.wait_send()
            cp.wait_recv()

    return pl.pallas_call(
        body, name=name,
        out_shape=(pltpu.HBM(p_thru.shape, p_thru.dtype), pltpu.HBM(p_thru.shape, p_thru.dtype)),
        in_specs=(HBM_SPEC, HBM_SPEC, SEM_SPEC, SEM_SPEC, ANY), out_specs=(HBM_SPEC, HBM_SPEC),
        input_output_aliases={0: 0, 1: 1},
        compiler_params=pltpu.CompilerParams(has_side_effects=SPLIT_EFFECT),
    )(p_thru, land_thru, send_sems, recv_sems, after)


def _sum_parts(parts, own, *, name):
    k, r, c = parts.shape
    tr = _pick(r, (512, 256, 128))

    def body(p_ref, own_ref, o_ref):
        me = 2 * lax.axis_index("x") + lax.axis_index("y")
        acc = jnp.zeros((tr, c), F32)
        for i in range(k):
            acc = acc + jnp.where(me == i, own_ref[i], p_ref[i]).astype(F32)
        o_ref[...] = acc

    blk = pl.BlockSpec((k, tr, c), lambda i: (0, i, 0))
    return pl.pallas_call(
        body, name=name, grid=(r // tr,), in_specs=[blk, blk],
        out_specs=pl.BlockSpec((tr, c), lambda i: (i, 0)),
        out_shape=jax.ShapeDtypeStruct((r, c), F32),
        compiler_params=_cparams(("parallel",)),
    )(parts, own)


def _join_halves(gh, *, name):
    def body(g_ref, o_ref, send_sem, recv_sem):
        x, y, c = lax.axis_index("x"), lax.axis_index("y"), lax.axis_index("c")
        cp = pltpu.make_async_remote_copy(src_ref=g_ref, dst_ref=o_ref, send_sem=send_sem, recv_sem=recv_sem,
                                          device_id=(x, y, 1 - c), device_id_type=MESH)
        cp.start()
        cp.wait()

    other = pl.pallas_call(
        body, name=name, in_specs=[ANY], out_specs=ANY,
        out_shape=jax.ShapeDtypeStruct(gh.shape, F32),
        scratch_shapes=[pltpu.SemaphoreType.DMA, pltpu.SemaphoreType.DMA],
    )(gh)
    south = lax.axis_index("c") == 0
    return jnp.concatenate([jnp.where(south, gh, other), jnp.where(south, other, gh)], axis=0)


def _gather_small(s, *, name):
    rows = s.shape[0]

    def body(s_ref, o_ref, send_sems, recv_sems, local_sem):
        x, y, c = lax.axis_index("x"), lax.axis_index("y"), lax.axis_index("c")
        me = 4 * x + 2 * y + c
        mine = pltpu.make_async_copy(s_ref, o_ref.at[me], local_sem)
        mine.start()
        peers = []
        for k in range(1, 8):
            peers.append((1 - x if k & 4 else x, 1 - y if k & 2 else y, 1 - c if k & 1 else c))
        cps = [pltpu.make_async_remote_copy(src_ref=s_ref, dst_ref=o_ref.at[me], send_sem=send_sems.at[k],
                                            recv_sem=recv_sems.at[k], device_id=p, device_id_type=MESH)
               for k, p in enumerate(peers)]
        for cp in cps:
            cp.start()
        for k, (px, py, pc) in enumerate(peers):
            pltpu.make_async_remote_copy(src_ref=s_ref, dst_ref=o_ref.at[4 * px + 2 * py + pc],
                                         send_sem=send_sems.at[k], recv_sem=recv_sems.at[k],
                                         device_id=(px, py, pc), device_id_type=MESH).wait_recv()
        for cp in cps:
            cp.wait_send()
        mine.wait()

    return pl.pallas_call(
        body, name=name, in_specs=[ANY], out_specs=ANY,
        out_shape=jax.ShapeDtypeStruct((8, rows, 128), F32),
        scratch_shapes=[pltpu.SemaphoreType.DMA((7,)), pltpu.SemaphoreType.DMA((7,)), pltpu.SemaphoreType.DMA],
    )(s)


IN_SHARD = IN_WIDTH // N_CHIPS
IN_SHARD_PAD = 1536
UP_ROWS, DOWN_ROWS, OUT_ROWS = 1024, 1024, 512
REST_ROWS = UP_ROWS + DOWN_ROWS + OUT_ROWS


def _pack_in(w_in_s):
    return jnp.pad(w_in_s, ((0, 0), (0, IN_SHARD_PAD - IN_SHARD))).astype(BF16)


def _pack_rest(w_out_s, w_up_s, w_down_s):
    return jnp.concatenate([w_up_s, w_down_s, w_out_s], axis=0).astype(BF16)


def _unpack_rest(blob):
    return (blob[UP_ROWS + DOWN_ROWS:], blob[0:UP_ROWS], blob[UP_ROWS:UP_ROWS + DOWN_ROWS])


def _with_own(gathered, own):
    me = 2 * lax.axis_index("x") + lax.axis_index("y")
    return [jnp.where(me == j, own, gathered[j]) for j in range(N_CHIPS)]


def _full_w_in(g_in, own):
    return jnp.concatenate([s[:, :IN_SHARD] for s in _with_own(g_in, own)], axis=1)


def _full_rest(g_rest, own):
    parts = [_unpack_rest(s) for s in _with_own(g_rest, own)]
    w_out = jnp.concatenate([p[0] for p in parts], axis=0)
    w_up = jnp.concatenate([p[1] for p in parts], axis=1)
    w_down = jnp.concatenate([p[2] for p in parts], axis=0)
    return w_out, w_up, w_down


def _split_w_in(w_in):
    z_xbc = w_in[:, 0:2560]
    dt = w_in[:, 2560:2576]
    qkv = w_in[:, 2576:5648]
    f = w_in[:, 5648:5664]
    pad = jnp.zeros((w_in.shape[0], PA_WIDTH - 2592), w_in.dtype)
    return jnp.concatenate([z_xbc, dt, f, pad], axis=1), qkv


def _merge_w_in(d_a, d_qkv):
    return jnp.concatenate([d_a[:, 0:2560], d_a[:, 2560:2576], d_qkv, d_a[:, 2576:2592]], axis=1)


def _local_step(x3, target3, w_in, rest_weights, norm_mix_w, conv_w, conv_b, dt_bias, a_log, d_skip,
                ssd_norm_w, f_bias, norm_mlp_w, norm_final_w, first_after=None, early_grads=None, late_grads=None):
    bl, t, d = x3.shape
    n = bl * t
    x = x3.reshape(n, d)
    target = target3.reshape(n, d)
    w_a, w_qkv = _split_w_in(w_in)
    nfw = norm_final_w.reshape(1, d)
    dskip_e = jnp.repeat(d_skip, HEAD_DIM, axis=1)
    nb = t // ATT_BLOCK

    r1, r2, kt = min(n, 1024), min(n, 512), min(n, 2048)
    if first_after is None:
        first_after = jnp.zeros((8, 128), F32)
    h0, rstd0, proj_a = _norm_mm(x, norm_mix_w, w_a, first_after, name="norm_mix_proj_a", tm=r2)
    qkv = _mm(h0, w_qkv, name="proj_qkv", tiles=(r2, QKV_WIDTH, D_MODEL), out_dtype=BF16)
    bias128 = jnp.concatenate([dt_bias, f_bias, jnp.zeros((1, 96), F32)], axis=1)
    alog128 = jnp.concatenate([a_log, jnp.zeros((1, 112), F32)], axis=1)
    dt, gate_d, acum, acum_t, negc = _prep(proj_a, bias128, alog128, bl, t)
    xc, dsilu = _conv_fwd(proj_a, conv_w, conv_b, bl, t)
    y_ssd, y_pre, hprev = _ssd_fwd(xc, proj_a, dt, acum, acum_t, dskip_e, ssd_norm_w, bl, t)
    y_att, lse = _attn_fwd(qkv, negc, bl, t)
    w_out, w_up, w_down = rest_weights(y_att)
    wo_s, wo_a = w_out[:SSD_WIDTH], w_out[SSD_WIDTH:]
    h1, h1n, rstd1 = _mm_norm_fwd(y_ssd, wo_s, y_att, wo_a, x, norm_mlp_w, name="out_proj_norm_mlp", tm=r2)
    up = _mm(h1n, w_up, name="mlp_up", tiles=(r1, D_FF, D_MODEL), out_dtype=BF16)
    dh2, dh2b, loss, d_nfw = _mm_final(up, w_down, h1, nfw, target, name="mlp_down_final_norm_loss", tm=r2,
                                       a_act="relu2")

    dup = _mm(dh2b, w_down, name="mlp_down_bwd_act", tiles=(r2, D_FF, D_MODEL), tb=True, epi_up=up, out_dtype=BF16)
    rest_shape = (N_CHIPS, REST_ROWS, D_MODEL)
    gb_rest = _mm(up, dh2b, name="mlp_down_bwd_w", tiles=(DOWN_ROWS, D_MODEL, kt), ta=True, a_act="relu2",
                  out_dtype=BF16, into=(rest_shape, (None, DOWN_ROWS, D_MODEL), lambda i, j, k: (i, 1, 0), None))
    dh1, dh1b, d_nmlp = _mm_norm_bwd([(dup, w_up)], h1, rstd1, norm_mlp_w, dh2, name="mlp_up_bwd_act_norm_mlp",
                                     tm=r2)
    gb_rest = _mm(h1n, dup, name="mlp_up_bwd_w", tiles=(D_MODEL, UP_ROWS, kt), ta=True, out_dtype=BF16,
                  into=(rest_shape, (None, D_MODEL, UP_ROWS), lambda i, j, k: (j, 0, 0), gb_rest))
    dys, do = _mm_two_halves(dh1b, w_out, name="out_proj_bwd_act", tm=r1)
    out_block = (UP_ROWS + DOWN_ROWS) // OUT_ROWS
    for half, (y_half, tag) in enumerate(((y_ssd, "ssd"), (y_att, "att"))):
        gb_rest = _mm(y_half, dh1b, name="out_proj_bwd_w_" + tag, tiles=(2 * OUT_ROWS, D_MODEL, kt), ta=True,
                      out_dtype=BF16, into=(rest_shape, (2, OUT_ROWS, D_MODEL),
                                            functools.partial(lambda i, j, k, h: (h, out_block, 0), h=half),
                                            gb_rest))
    token = jnp.zeros((8, 128), F32) if early_grads is None else early_grads(gb_rest)
    dq, dk, dv, dcb = _attn_bwd(qkv, do, y_att, lse, negc, token, bl, t)
    dc = jnp.pad(dcb[:, :, 0:2, :].transpose(0, 3, 1, 2).reshape(n, 16), ((0, 0), (16, 96)))
    dxc, dpa, ddt_raw, d_snw, d_dsk, d_alog, d_dtb = _ssd_bwd(dys, xc, proj_a, y_pre, hprev, dt, gate_d, acum,
                                                             acum_t, alog128, dskip_e, ssd_norm_w, bl, t)
    dpa, d_conv_w, d_conv_b = _conv_bwd(dxc, dsilu, proj_a, conv_w, dpa, bl, t)
    dproj_a, d_fb = _fpost(dc, gate_d, ddt_raw, dpa, bl, t)
    dqkv = jnp.concatenate([dq, dk, dv], axis=1)
    d_w_a = _mm(h0, dproj_a, name="proj_a_bwd_w", tiles=(1024, 896, kt), ta=True, out_dtype=BF16)
    d_w_qkv = _mm(h0, dqkv, name="proj_qkv_bwd_w", tiles=(1024, 1024, kt), ta=True, out_dtype=BF16)
    d_w_in = _merge_w_in(d_w_a, d_w_qkv)
    late_token = None if late_grads is None else late_grads(d_w_in)
    dx, _, d_nmix = _mm_norm_bwd([(dproj_a, w_a), (dqkv, w_qkv)], x, rstd0, norm_mix_w, dh1,
                                 name="proj_bwd_act_norm_mix", tm=min(n, 256), after=late_token)

    grads = dict(norm_mix_w=d_nmix, w_in=d_w_in, conv_w=d_conv_w, conv_b=d_conv_b,
                 dt_bias=d_dtb, a_log=d_alog, d_skip=d_dsk, ssd_norm_w=d_snw, f_bias=d_fb, rest=gb_rest,
                 norm_mlp_w=d_nmlp, norm_final_w=d_nfw)
    return dx.reshape(bl, t, d), loss, grads


SMALL_ORDER = ("norm_mix_w", "conv_w", "conv_b", "dt_bias", "a_log", "d_skip", "ssd_norm_w", "f_bias",
               "norm_mlp_w", "norm_final_w")
SMALL_SIZES = (1024, 4 * CONV_CH, CONV_CH, 16, 16, 16, 1024, 16, 1024, 1024)


def _pack_small(vals, rows):
    flat = jnp.concatenate([v.reshape(-1).astype(F32) for v in vals])
    return jnp.pad(flat, (0, rows * 128 - flat.shape[0])).reshape(rows, 128)


def _unpack_small(packed, sizes):
    flat = packed.reshape(-1)
    out, o = [], 0
    for s in sizes:
        out.append(flat[o:o + s])
        o += s
    return out


def kernel(x, norm_mix_w, w_in, conv_w, conv_b, dt_bias, a_log, d_skip, ssd_norm_w, f_bias, w_out, norm_mlp_w, w_up, w_down, norm_final_w, loss_target, m_norm_mix_w, m_w_in, m_conv_w, m_conv_b, m_dt_bias, m_a_log, m_d_skip, m_ssd_norm_w, m_f_bias, m_w_out, m_norm_mlp_w, m_w_up, m_w_down, m_norm_final_w, v_norm_mix_w, v_w_in, v_conv_w, v_conv_b, v_dt_bias, v_a_log, v_d_skip, v_ssd_norm_w, v_f_bias, v_w_out, v_norm_mlp_w, v_w_up, v_w_down, v_norm_final_w):
    chip = 2 * lax.axis_index("x") + lax.axis_index("y")
    cw = CONV_CH // N_CHIPS

    own_in = _pack_in(w_in[0])
    own_rest = _pack_rest(w_out[0], w_up[0], w_down[0])
    g_in = _gather_weights(own_in, name="gather_w_in")
    w_in_f = _full_w_in(g_in, own_in)
    *rest_handles, rest_token = _gather_start(own_rest, g_in, name="gather_start_rest")

    def rest_weights(after):
        _, landed = _gather_wait(*rest_handles, after, name="gather_wait_rest")
        return _full_rest(_gather_forward(landed, name="gather_forward_rest"), own_rest)
    small_all = _gather_small(_pack_small([conv_w[0]], 16), name="gather_conv_w")
    conv_w_f = jnp.concatenate([small_all[2 * j].reshape(-1)[:4 * cw].reshape(4, cw) for j in range(N_CHIPS)], axis=1)

    c = lax.axis_index("c")

    def chip_partial(gb, tag):
        half_rows = gb.shape[1] // 2
        from_sibling = _swap_halves(gb, name="grad_swap_halves_" + tag)
        my_half = lax.dynamic_slice_in_dim(gb, c * half_rows, half_rows, axis=1)
        return _add_pair(my_half, from_sibling, name="grad_add_sibling_" + tag)

    in_flight = {}

    def early_grads(gb_rest):
        part = chip_partial(gb_rest, "rest")
        *handles, token = _exchange_start(part, name="grad_exchange_start_rest")
        in_flight["rest"] = handles
        return token

    def late_grads(d_w_in):
        gb_in = jnp.stack([_pack_in(d_w_in[:, j * IN_SHARD:(j + 1) * IN_SHARD]) for j in range(N_CHIPS)])
        *handles, token = _exchange_start(chip_partial(gb_in, "in"), name="grad_exchange_start_in")
        in_flight["in"] = handles
        return token

    dx, loss_part, g = _local_step(x, loss_target, w_in_f, rest_weights, norm_mix_w, conv_w_f,
                                   conv_b, dt_bias, a_log, d_skip, ssd_norm_w, f_bias, norm_mlp_w, norm_final_w,
                                   first_after=rest_token, early_grads=early_grads, late_grads=late_grads)

    send_sems, recv_sems, part_rest, land_rest = in_flight["rest"]
    part_rest, parts_rest = _exchange_wait(send_sems, recv_sems, part_rest, land_rest, dx,
                                           name="grad_exchange_wait_rest")
    g_rest_half = _sum_parts(parts_rest, part_rest, name="grad_sum_chips_rest")
    g_w_out, g_w_up, g_w_down = _unpack_rest(_join_halves(g_rest_half, name="grad_join_halves_rest"))

    part_in, parts_in = _exchange_wait(*in_flight["in"], dx, name="grad_exchange_wait_in")
    g_in_half = _sum_parts(parts_in, part_in, name="grad_sum_chips_in")
    g_w_in = _join_halves(g_in_half, name="grad_join_halves_in")[:, :IN_SHARD]

    small_vals = [g[k] for k in SMALL_ORDER] + [loss_part[:, 0:1]]
    small_sum = _sum_leading(_gather_small(_pack_small(small_vals, SMALL_ROWS), name="gather_small_grads"), name="small_sum")
    sg = dict(zip(SMALL_ORDER + ("loss",), _unpack_small(small_sum, SMALL_SIZES + (1,))))
    loss = sg["loss"].reshape(())
    g_conv_full = sg["conv_w"].reshape(4, CONV_CH)
    g_conv = lax.dynamic_slice_in_dim(g_conv_full, chip * cw, cw, axis=1)

    grads = dict(norm_mix_w=sg["norm_mix_w"].reshape(1, -1), w_in=g_w_in[None], conv_w=g_conv[None],
                 conv_b=sg["conv_b"].reshape(1, -1), dt_bias=sg["dt_bias"].reshape(1, -1),
                 a_log=sg["a_log"].reshape(1, -1), d_skip=sg["d_skip"].reshape(1, -1),
                 ssd_norm_w=sg["ssd_norm_w"].reshape(1, -1), f_bias=sg["f_bias"].reshape(1, -1), w_out=g_w_out[None],
                 norm_mlp_w=sg["norm_mlp_w"].reshape(1, -1), w_up=g_w_up[None], w_down=g_w_down[None],
                 norm_final_w=sg["norm_final_w"])
    weights = dict(norm_mix_w=norm_mix_w, w_in=w_in, conv_w=conv_w, conv_b=conv_b, dt_bias=dt_bias, a_log=a_log,
                   d_skip=d_skip, ssd_norm_w=ssd_norm_w, f_bias=f_bias, w_out=w_out, norm_mlp_w=norm_mlp_w,
                   w_up=w_up, w_down=w_down, norm_final_w=norm_final_w)
    ms = dict(norm_mix_w=m_norm_mix_w, w_in=m_w_in, conv_w=m_conv_w, conv_b=m_conv_b, dt_bias=m_dt_bias,
              a_log=m_a_log, d_skip=m_d_skip, ssd_norm_w=m_ssd_norm_w, f_bias=m_f_bias, w_out=m_w_out,
              norm_mlp_w=m_norm_mlp_w, w_up=m_w_up, w_down=m_w_down, norm_final_w=m_norm_final_w)
    vs = dict(norm_mix_w=v_norm_mix_w, w_in=v_w_in, conv_w=v_conv_w, conv_b=v_conv_b, dt_bias=v_dt_bias,
              a_log=v_a_log, d_skip=v_d_skip, ssd_norm_w=v_ssd_norm_w, f_bias=v_f_bias, w_out=v_w_out,
              norm_mlp_w=v_norm_mlp_w, w_up=v_w_up, w_down=v_w_down, norm_final_w=v_norm_final_w)
    names = list(weights)
    big = ("w_in", "w_out", "w_up", "w_down")
    delta, new_m, new_v = {}, {}, {}
    for k, g2 in zip(big[1:], (g_w_out, g_w_up, g_w_down)):
        delta[k], new_m[k], new_v[k] = _adamw(weights[k], g2, ms[k], vs[k], name="adamw_" + k)
    g_in_t = g_w_in.T
    outs_t = _adamw(w_in[0].T, g_in_t, m_w_in[0].T, v_w_in[0].T, name="adamw_w_in")
    delta["w_in"], new_m["w_in"], new_v["w_in"] = [o.T[None] for o in outs_t]
    grads["w_in"] = g_in_t.T[None]
    smalls = [k for k in names if k not in big]
    sizes = [math.prod(weights[k].shape) for k in smalls]
    rows = -(-sum(sizes) // 1024) * 8
    packs = [_pack_small([d[k] for k in smalls], rows) for d in (weights, grads, ms, vs)]
    outs = _adamw(*packs, name="adamw_small")
    for o, dst in zip(outs, (delta, new_m, new_v)):
        for k, val in zip(smalls, _unpack_small(o, sizes)):
            dst[k] = val.reshape(weights[k].shape)
    return (loss, dx, *[grads[k] for k in names], *[delta[k] for k in names], *[new_m[k] for k in names],
            *[new_v[k] for k in names])
```

```python
import functools
import math

import jax
import jax.numpy as jnp
from jax import lax
from jax.experimental import pallas as pl
from jax.experimental.pallas import tpu as pltpu

F32 = jnp.float32
BF16 = jnp.bfloat16
HIGHEST = lax.Precision.HIGHEST
MESH = pl.DeviceIdType.MESH

D_MODEL = 1024
HEAD_DIM = 64
SSD_WIDTH = 1024
SSD_STATE = 128
CONV_CH = 1536
CHUNK = 128
ATT_WIDTH = 1024
EPS = 1e-5
IN_WIDTH = 5664
PA_WIDTH = 2688
QKV_WIDTH = 3072
D_FF = 4096
ATT_BLOCK = 256
NEG = -1e30
LOG2E = 1.4426950408889634
VMEM_LIMIT = 48 * 1024 * 1024

ADAM_LR = 0.001
ADAM_B1 = 0.9
ADAM_B2 = 0.999
ADAM_EPS = 1e-08
ADAM_WD = 0.01
ADAM_STEP = 10

N_CHIPS = 4
SMALL_ROWS = 96


def _cparams(sem):
    return pltpu.CompilerParams(dimension_semantics=sem, vmem_limit_bytes=VMEM_LIMIT)


def _pick(n, cands):
    for c in cands:
        if n % c == 0:
            return c
    return n


MM_CHUNK = 512


def _mm(a, b, *, name, tiles, ta=False, tb=False, out_dtype=F32, res=None, a_act=None, epi_up=None, after=None,
        into=None):
    n_unread = (after is not None) + (into is not None and into[3] is not None)
    if ta:
        K, M = a.shape
    else:
        M, K = a.shape
    if tb:
        N, K2 = b.shape
    else:
        K2, N = b.shape
    assert K == K2, (a.shape, b.shape)
    tm, tn, tk = tiles
    assert M % tm == 0 and N % tn == 0 and K % tk == 0, (name, M, N, K, tiles)
    nk = K // tk
    dn = (((0 if ta else 1,), (1 if tb else 0,)), ((), ()))
    has_res = res is not None
    has_up = epi_up is not None
    cn = _pick(tn, (MM_CHUNK, 384, 256, 128))

    def prologue(av):
        if a_act == "relu2":
            r = jnp.maximum(av.astype(F32), 0.0)
            av = r * r
        return av.astype(BF16)

    def epilogue(out, res_v, up_v):
        if has_res:
            out = out + res_v.astype(F32)
        if has_up:
            out = out * (2.0 * jnp.maximum(up_v.astype(F32), 0.0))
        return out.astype(out_dtype)

    def body(*refs):
        a_ref, b_ref = refs[0], refs[1]
        i = 2
        res_ref = up_ref = None
        if has_res:
            res_ref = refs[i]
            i += 1
        if has_up:
            up_ref = refs[i]
            i += 1
        i += n_unread
        o_ref = refs[i]
        if nk == 1:
            av = prologue(a_ref[...])
            if len(o_ref.shape) == 3:
                out = lax.dot_general(av, b_ref[...].astype(BF16), dn, preferred_element_type=F32)
                out = epilogue(out, res_ref[...] if has_res else None, up_ref[...] if has_up else None)
                o_ref[...] = out.reshape(o_ref.shape)
                return
            for c in range(tn // cn):
                cs = slice(c * cn, (c + 1) * cn)
                bv = (b_ref[cs, :] if tb else b_ref[:, cs]).astype(BF16)
                out = lax.dot_general(av, bv, dn, preferred_element_type=F32)
                o_ref[:, cs] = epilogue(out, res_ref[:, cs] if has_res else None, up_ref[:, cs] if has_up else None)
            return
        acc_ref = refs[i + 1]
        k = pl.program_id(2)

        @pl.when(k == 0)
        def _():
            acc_ref[...] = jnp.zeros_like(acc_ref)

        acc_ref[...] += lax.dot_general(prologue(a_ref[...]), b_ref[...].astype(BF16), dn,
                                        preferred_element_type=F32)

        @pl.when(k == nk - 1)
        def _():
            out = epilogue(acc_ref[...], res_ref[...] if has_res else None, up_ref[...] if has_up else None)
            o_ref[...] = out.reshape(o_ref.shape)

    a_spec = pl.BlockSpec((tk, tm), lambda i, j, k: (k, i)) if ta else pl.BlockSpec((tm, tk), lambda i, j, k: (i, k))
    b_spec = pl.BlockSpec((tn, tk), lambda i, j, k: (j, k)) if tb else pl.BlockSpec((tk, tn), lambda i, j, k: (k, j))
    o_spec = pl.BlockSpec((tm, tn), lambda i, j, k: (i, j))
    ins, specs = [a, b], [a_spec, b_spec]
    if has_res:
        ins.append(res)
        specs.append(o_spec)
    if has_up:
        ins.append(epi_up)
        specs.append(o_spec)
    if after is not None:
        ins.append(after)
        specs.append(pl.BlockSpec(memory_space=pl.ANY))
    out_shape, out_spec, aliases = jax.ShapeDtypeStruct((M, N), out_dtype), o_spec, {}
    if into is not None:
        shape, block, index, buf = into
        out_shape, out_spec = jax.ShapeDtypeStruct(shape, out_dtype), pl.BlockSpec(block, index)
        if buf is not None:
            aliases = {len(ins): 0}
            ins.append(buf)
            specs.append(pl.BlockSpec(memory_space=pl.ANY))
    return pl.pallas_call(
        body, name=name, grid=(M // tm, N // tn, nk),
        in_specs=specs, out_specs=out_spec, out_shape=out_shape, input_output_aliases=aliases,
        scratch_shapes=[] if nk == 1 else [pltpu.VMEM((tm, tn), F32)],
        compiler_params=_cparams(("parallel", "parallel", "arbitrary")),
    )(*ins)


def _rows_product(a_ref, b_ref, tb, a_act):
    av = a_ref[...]
    if a_act == "relu2":
        r = jnp.maximum(av.astype(F32), 0.0)
        av = r * r
    dn = (((1,), (1 if tb else 0,)), ((), ()))
    return lax.dot_general(av.astype(BF16), b_ref[...].astype(BF16), dn, preferred_element_type=F32)


def _norm_mm(x, w, b, after, *, name, tm):
    m, d = x.shape
    n = b.shape[1]
    cn = _pick(n, (MM_CHUNK, 384, 256, 128))

    def body(x_ref, w_ref, b_ref, after_ref, h_ref, r_ref, o_ref):
        xv = x_ref[...]
        rstd = lax.rsqrt(jnp.mean(xv * xv, axis=1, keepdims=True) + EPS)
        hv = (xv * rstd * w_ref[...]).astype(BF16)
        h_ref[...] = hv
        r_ref[...] = rstd
        for c in range(n // cn):
            cs = slice(c * cn, (c + 1) * cn)
            o_ref[:, cs] = jnp.dot(hv, b_ref[:, cs].astype(BF16), preferred_element_type=F32)

    row = pl.BlockSpec((tm, d), lambda i: (i, 0))
    return pl.pallas_call(
        body, name=name, grid=(m // tm,),
        in_specs=[row, pl.BlockSpec((1, d), lambda i: (0, 0)), pl.BlockSpec((d, n), lambda i: (0, 0)),
                  pl.BlockSpec(memory_space=pl.ANY)],
        out_specs=[row, pl.BlockSpec((tm, 1), lambda i: (i, 0)), pl.BlockSpec((tm, n), lambda i: (i, 0))],
        out_shape=[jax.ShapeDtypeStruct((m, d), BF16), jax.ShapeDtypeStruct((m, 1), F32),
                   jax.ShapeDtypeStruct((m, n), F32)],
        compiler_params=_cparams(("parallel",)),
    )(x, w, b, after)


def _mm_norm_fwd(a1, b1, a2, b2, res, w, *, name, tm):
    m, k1 = a1.shape
    k2 = a2.shape[1]
    d = b1.shape[1]

    def body(a1_ref, b1_ref, a2_ref, b2_ref, res_ref, w_ref, h_ref, y_ref, r_ref):
        hv = _rows_product(a1_ref, b1_ref, False, None) + _rows_product(a2_ref, b2_ref, False, None) + res_ref[...]
        rstd = lax.rsqrt(jnp.mean(hv * hv, axis=1, keepdims=True) + EPS)
        h_ref[...] = hv
        y_ref[...] = (hv * rstd * w_ref[...]).astype(BF16)
        r_ref[...] = rstd

    row = pl.BlockSpec((tm, d), lambda i: (i, 0))
    return pl.pallas_call(
        body, name=name, grid=(m // tm,),
        in_specs=[pl.BlockSpec((tm, k1), lambda i: (i, 0)), pl.BlockSpec((k1, d), lambda i: (0, 0)),
                  pl.BlockSpec((tm, k2), lambda i: (i, 0)), pl.BlockSpec((k2, d), lambda i: (0, 0)), row,
                  pl.BlockSpec((1, d), lambda i: (0, 0))],
        out_specs=[row, row, pl.BlockSpec((tm, 1), lambda i: (i, 0))],
        out_shape=[jax.ShapeDtypeStruct((m, d), F32), jax.ShapeDtypeStruct((m, d), BF16),
                   jax.ShapeDtypeStruct((m, 1), F32)],
        compiler_params=_cparams(("parallel",)),
    )(a1, b1, a2, b2, res, w)


def _mm_final(a, b, res, w, target, *, name, tm, a_act):
    m, k = a.shape
    d = b.shape[1]

    def body(a_ref, b_ref, res_ref, w_ref, t_ref, dh_ref, dhb_ref, loss_ref, dw_ref):
        @pl.when(pl.program_id(0) == 0)
        def _():
            loss_ref[...] = jnp.zeros_like(loss_ref)
            dw_ref[...] = jnp.zeros_like(dw_ref)

        hv = _rows_product(a_ref, b_ref, False, a_act) + res_ref[...]
        wv = w_ref[...]
        rstd = lax.rsqrt(jnp.mean(hv * hv, axis=1, keepdims=True) + EPS)
        xhat = hv * rstd
        err = xhat * wv - t_ref[...]
        loss_ref[...] += 0.5 * jnp.sum(jnp.mean(err * err, axis=1, keepdims=True), axis=0, keepdims=True)
        dy = err * (1.0 / d)
        gw = dy * wv
        dh = rstd * (gw - xhat * jnp.mean(gw * xhat, axis=1, keepdims=True))
        dh_ref[...] = dh
        dhb_ref[...] = dh.astype(BF16)
        dw_ref[...] += jnp.sum(dy * xhat, axis=0, keepdims=True)

    row = pl.BlockSpec((tm, d), lambda i: (i, 0))
    vec = pl.BlockSpec((1, d), lambda i: (0, 0))
    return pl.pallas_call(
        body, name=name, grid=(m // tm,),
        in_specs=[pl.BlockSpec((tm, k), lambda i: (i, 0)), pl.BlockSpec((k, d), lambda i: (0, 0)), row, vec, row],
        out_specs=[row, row, pl.BlockSpec((1, 128), lambda i: (0, 0)), vec],
        out_shape=[jax.ShapeDtypeStruct((m, d), F32), jax.ShapeDtypeStruct((m, d), BF16),
                   jax.ShapeDtypeStruct((1, 128), F32), jax.ShapeDtypeStruct((1, d), F32)],
        compiler_params=_cparams(("arbitrary",)),
    )(a, b, res, w, target)


def _mm_two_halves(a, b, *, name, tm):
    m, k = a.shape
    d = b.shape[0] // 2

    def body(a_ref, b_ref, lo_ref, hi_ref):
        av = a_ref[...].astype(BF16)
        lo_ref[...] = lax.dot_general(av, b_ref[0:d, :].astype(BF16), NT_DIMS, preferred_element_type=F32)
        hi_ref[...] = lax.dot_general(av, b_ref[d:2 * d, :].astype(BF16), NT_DIMS,
                                      preferred_element_type=F32).astype(BF16)

    row = pl.BlockSpec((tm, d), lambda i: (i, 0))
    return pl.pallas_call(
        body, name=name, grid=(m // tm,),
        in_specs=[pl.BlockSpec((tm, k), lambda i: (i, 0)), pl.BlockSpec((2 * d, k), lambda i: (0, 0))],
        out_specs=[row, row],
        out_shape=[jax.ShapeDtypeStruct((m, d), F32), jax.ShapeDtypeStruct((m, d), BF16)],
        compiler_params=_cparams(("parallel",)),
    )(a, b)


def _mm_norm_bwd(pairs, x, rstd, w, dres, *, name, tm, after=None):
    m = pairs[0][0].shape[0]
    d = pairs[0][1].shape[0]
    n_pairs = len(pairs)

    def body(*refs):
        i = 2 * n_pairs
        x_ref, r_ref, w_ref, d_ref = refs[i:i + 4]
        dx_ref, dxb_ref, dw_ref = refs[-3:]

        @pl.when(pl.program_id(0) == 0)
        def _():
            dw_ref[...] = jnp.zeros_like(dw_ref)

        g = _rows_product(refs[0], refs[1], True, None)
        for p in range(1, n_pairs):
            g = g + _rows_product(refs[2 * p], refs[2 * p + 1], True, None)
        r = r_ref[...]
        xhat = x_ref[...] * r
        gw = g * w_ref[...]
        dx = d_ref[...] + r * (gw - xhat * jnp.mean(gw * xhat, axis=1, keepdims=True))
        dx_ref[...] = dx
        dxb_ref[...] = dx.astype(BF16)
        dw_ref[...] += jnp.sum(g * xhat, axis=0, keepdims=True)

    row = pl.BlockSpec((tm, d), lambda i: (i, 0))
    vec = pl.BlockSpec((1, d), lambda i: (0, 0))
    ins, specs = [], []
    for a, b in pairs:
        k = a.shape[1]
        ins += [a, b]
        specs += [pl.BlockSpec((tm, k), lambda i: (i, 0)), pl.BlockSpec((d, k), lambda i: (0, 0))]
    ins += [x, rstd, w, dres]
    specs += [row, pl.BlockSpec((tm, 1), lambda i: (i, 0)), vec, row]
    if after is not None:
        ins.append(after)
        specs.append(pl.BlockSpec(memory_space=pl.ANY))
    return pl.pallas_call(
        body, name=name, grid=(m // tm,), in_specs=specs, out_specs=[row, row, vec],
        out_shape=[jax.ShapeDtypeStruct((m, d), F32), jax.ShapeDtypeStruct((m, d), BF16),
                   jax.ShapeDtypeStruct((1, d), F32)],
        compiler_params=_cparams(("arbitrary",)),
    )(*ins)


def _softplus(x):
    return jnp.maximum(x, 0.0) + jnp.log(1.0 + jnp.exp(-jnp.abs(x)))


def _prep(proj_a, bias128, alog128, bl, t):
    n = bl * t
    nch = t // CHUNK
    col0 = (SSD_WIDTH + CONV_CH) // 128

    def body(p_ref, b_ref, al_ref, dt_ref, gd_ref, ac_ref, act_ref, negc_ref):
        negc_ref[...] = jnp.zeros_like(negc_ref)
        row = lax.broadcasted_iota(jnp.int32, (CHUNK, CHUNK), 0)
        col = lax.broadcasted_iota(jnp.int32, (CHUNK, CHUNK), 1)
        tril = (row >= col).astype(F32)
        lane = lax.broadcasted_iota(jnp.int32, (1, 128), 1)
        head_lanes = lane < 16
        a_row = -jnp.exp(al_ref[...])
        carry = jnp.zeros((1, 128), F32)
        for ci in range(nch):
            rows = slice(ci * CHUNK, (ci + 1) * CHUNK)
            xv = p_ref[rows, :] + b_ref[...]
            sp = _softplus(xv)
            acum = jnp.dot(tril, a_row * sp, precision=HIGHEST, preferred_element_type=F32)
            c = jnp.dot(tril, -_softplus(-xv), precision=HIGHEST, preferred_element_type=F32) + carry
            carry = c[CHUNK - 1:CHUNK, :]
            dt_ref[rows, :] = jnp.where(head_lanes, sp, 0.0)
            gd_ref[rows, :] = jnp.where(head_lanes, jax.nn.sigmoid(xv),
                                        jnp.where(lane < 32, jax.nn.sigmoid(-xv), 0.0))
            ac_ref[rows, :] = jnp.where(head_lanes, acum, 0.0)
            act_ref[:, rows] = jnp.transpose(acum)[0:16, :]
            c_t = jnp.transpose(c)
            for hp in range(8):
                negc_ref[hp, 0:2, rows] = -c_t[16 + 2 * hp:18 + 2 * hp, :]

    o128 = pl.BlockSpec((t, 128), lambda b: (b, 0))
    v128 = pl.BlockSpec((1, 128), lambda b: (0, 0))
    w128 = jax.ShapeDtypeStruct((n, 128), F32)
    return pl.pallas_call(
        body, name="head_scalars", grid=(bl,),
        in_specs=[pl.BlockSpec((t, 128), lambda b: (b, col0)), v128, v128],
        out_specs=[o128, o128, o128, pl.BlockSpec((16, t), lambda b: (0, b)),
                   pl.BlockSpec((None, 8, 8, t), lambda b: (b, 0, 0, 0))],
        out_shape=[w128, w128, w128, jax.ShapeDtypeStruct((16, n), F32),
                   jax.ShapeDtypeStruct((bl, 8, 8, t), F32)],
        compiler_params=_cparams(("parallel",)),
    )(proj_a, bias128, alog128)


def _fpost(dc, gate_d, ddt, dpa, bl, t):
    n = bl * t
    nch = t // CHUNK
    col0 = (SSD_WIDTH + CONV_CH) // 128

    def body(dc_ref, gd_ref, ddt_ref, dpa_in, out_ref, db_ref):
        @pl.when(pl.program_id(0) == 0)
        def _():
            db_ref[...] = jnp.zeros_like(db_ref)

        row = lax.broadcasted_iota(jnp.int32, (CHUNK, CHUNK), 0)
        col = lax.broadcasted_iota(jnp.int32, (CHUNK, CHUNK), 1)
        triu = (row <= col).astype(F32)
        lane = lax.broadcasted_iota(jnp.int32, (1, 128), 1)
        gate_lanes = (lane >= 16) & (lane < 32)
        carry = jnp.zeros((1, 128), F32)
        db = jnp.zeros((1, 128), F32)
        for ci in reversed(range(nch)):
            rows = slice(ci * CHUNK, (ci + 1) * CHUNK)
            dlf = jnp.dot(triu, dc_ref[rows, :], precision=HIGHEST, preferred_element_type=F32) + carry
            carry = dlf[0:1, :]
            df = jnp.where(gate_lanes, dlf * gd_ref[rows, :], 0.0)
            out_ref[rows, :] = (ddt_ref[rows, :] + df).astype(BF16)
            db = db + jnp.sum(df, axis=0, keepdims=True)
        db_ref[...] += db[:, 16:32]

    blk = pl.BlockSpec((t, 128), lambda b: (b, 0))
    return pl.pallas_call(
        body, name="forget_gate_bwd", grid=(bl,),
        in_specs=[blk, blk, blk, ANY],
        out_specs=[pl.BlockSpec((t, 128), lambda b: (b, col0)), pl.BlockSpec((1, 16), lambda b: (0, 0))],
        out_shape=[jax.ShapeDtypeStruct(dpa.shape, dpa.dtype), jax.ShapeDtypeStruct((1, 16), F32)],
        input_output_aliases={3: 0},
        compiler_params=_cparams(("arbitrary",)),
    )(dc, gate_d, ddt, dpa)


CONV_TILE = 256
CONV_ROWS = 256


def _conv_taps(u_ref, i):
    r0 = pl.multiple_of(i * CONV_ROWS, CONV_ROWS)
    cur = u_ref[pl.ds(r0, CONV_ROWS), :]
    p0 = pl.multiple_of(jnp.maximum(r0 - 8, 0), 8)
    prev = jnp.where(i > 0, u_ref[pl.ds(p0, 8), :], 0.0)
    cat = jnp.concatenate([prev, cur], axis=0)
    return r0, [cur] + [pltpu.roll(cat, s, 0)[8:, :] for s in (1, 2, 3)]


def _conv_fwd(proj_a, conv_w, conv_b, bl, t):
    n = bl * t
    nct = CONV_CH // CONV_TILE
    c0 = SSD_WIDTH // CONV_TILE

    def body(u_ref, w_ref, b_ref, o_ref, d_ref):
        w = w_ref[...]
        bias = b_ref[...]

        def chunk(i, carry):
            r0, taps = _conv_taps(u_ref, i)
            pre = bias + w[3:4, :] * taps[0]
            for s in (1, 2, 3):
                pre = pre + w[3 - s:4 - s, :] * taps[s]
            sg = jax.nn.sigmoid(pre)
            o_ref[pl.ds(r0, CONV_ROWS), :] = pre * sg
            d_ref[pl.ds(r0, CONV_ROWS), :] = (sg * (1.0 + pre * (1.0 - sg))).astype(BF16)
            return carry

        lax.fori_loop(0, t // CONV_ROWS, chunk, 0)

    out = pl.BlockSpec((t, CONV_TILE), lambda b, c: (b, c))
    return pl.pallas_call(
        body, name="conv_silu_fwd", grid=(bl, nct),
        in_specs=[pl.BlockSpec((t, CONV_TILE), lambda b, c: (b, c0 + c)),
                  pl.BlockSpec((4, CONV_TILE), lambda b, c: (0, c)),
                  pl.BlockSpec((1, CONV_TILE), lambda b, c: (0, c))],
        out_specs=[out, out],
        out_shape=[jax.ShapeDtypeStruct((n, CONV_CH), F32), jax.ShapeDtypeStruct((n, CONV_CH), BF16)],
        compiler_params=_cparams(("parallel", "parallel")),
    )(proj_a, conv_w, conv_b)


def _conv_bwd(dxc, dsilu, proj_a, conv_w, dpa, bl, t):
    nct = CONV_CH // CONV_TILE
    c0 = SSD_WIDTH // CONV_TILE
    nrc = t // CONV_ROWS

    def body(g_ref, s_ref, u_ref, w_ref, dpa_in, du_ref, dw_ref, db_ref, dp_scr):
        @pl.when(pl.program_id(1) == 0)
        def _():
            dw_ref[...] = jnp.zeros_like(dw_ref)
            db_ref[...] = jnp.zeros_like(db_ref)

        w = w_ref[...]
        dp_scr[pl.ds(t, 8), :] = jnp.zeros((8, CONV_TILE), F32)

        def chunk1(i, carry):
            dw0, dw1, dw2, dw3, db = carry
            r0, taps = _conv_taps(u_ref, i)
            dpre = g_ref[pl.ds(r0, CONV_ROWS), :] * s_ref[pl.ds(r0, CONV_ROWS), :].astype(F32)
            dp_scr[pl.ds(r0, CONV_ROWS), :] = dpre
            dw3 = dw3 + jnp.sum(dpre * taps[0], axis=0, keepdims=True)
            dw2 = dw2 + jnp.sum(dpre * taps[1], axis=0, keepdims=True)
            dw1 = dw1 + jnp.sum(dpre * taps[2], axis=0, keepdims=True)
            dw0 = dw0 + jnp.sum(dpre * taps[3], axis=0, keepdims=True)
            db = db + jnp.sum(dpre, axis=0, keepdims=True)
            return dw0, dw1, dw2, dw3, db

        z = jnp.zeros((1, CONV_TILE), F32)
        dw0, dw1, dw2, dw3, db = lax.fori_loop(0, nrc, chunk1, (z, z, z, z, z))
        dw_ref[...] += jnp.concatenate([dw0, dw1, dw2, dw3], axis=0)
        db_ref[...] += db

        def chunk2(i, carry):
            r0 = pl.multiple_of(i * CONV_ROWS, CONV_ROWS)
            cat = dp_scr[pl.ds(r0, CONV_ROWS + 8), :]
            du = w[3:4, :] * cat[:CONV_ROWS, :]
            for s in (1, 2, 3):
                du = du + w[3 - s:4 - s, :] * pltpu.roll(cat, CONV_ROWS + 8 - s, 0)[:CONV_ROWS, :]
            du_ref[pl.ds(r0, CONV_ROWS), :] = du.astype(BF16)
            return carry

        lax.fori_loop(0, nrc, chunk2, 0)

    tile = pl.BlockSpec((t, CONV_TILE), lambda c, b: (b, c))
    return pl.pallas_call(
        body, name="conv_silu_bwd", grid=(nct, bl),
        in_specs=[tile, tile, pl.BlockSpec((t, CONV_TILE), lambda c, b: (b, c0 + c)),
                  pl.BlockSpec((4, CONV_TILE), lambda c, b: (0, c)), ANY],
        out_specs=[pl.BlockSpec((t, CONV_TILE), lambda c, b: (b, c0 + c)),
                   pl.BlockSpec((4, CONV_TILE), lambda c, b: (0, c)),
                   pl.BlockSpec((1, CONV_TILE), lambda c, b: (0, c))],
        out_shape=[jax.ShapeDtypeStruct(dpa.shape, dpa.dtype), jax.ShapeDtypeStruct((4, CONV_CH), F32),
                   jax.ShapeDtypeStruct((1, CONV_CH), F32)],
        input_output_aliases={4: 0},
        scratch_shapes=[pltpu.VMEM((t + 8, CONV_TILE), F32)],
        compiler_params=_cparams(("parallel", "arbitrary")),
    )(dxc, dsilu, proj_a, conv_w, dpa)


SSD_FWD_CHUNKS = 4
SSD_BWD_CHUNKS = 1
NT_DIMS = (((1,), (1,)), ((), ()))
TN_DIMS = (((0,), (0,)), ((), ()))


def _dot(a, b, dims=None):
    if dims is None:
        return jnp.dot(a, b, preferred_element_type=F32)
    return lax.dot_general(a, b, dims, preferred_element_type=F32)


def _head_expander():
    r = lax.broadcasted_iota(jnp.int32, (128, SSD_WIDTH), 0)
    c = lax.broadcasted_iota(jnp.int32, (128, SSD_WIDTH), 1)
    return ((c // HEAD_DIM == r % 16) & (r < 48)).astype(BF16)


def _spread(v128, expander):
    hi = v128.astype(BF16).astype(F32)
    r1 = v128 - hi
    mid = r1.astype(BF16).astype(F32)
    lo = (r1 - mid).astype(BF16).astype(F32)
    packed = (hi + pltpu.roll(mid, 16, 1) + pltpu.roll(lo, 32, 1)).astype(BF16)
    return jnp.dot(packed, expander, preferred_element_type=F32)


def _head_sums(v1024, expander):
    hi = v1024.astype(BF16)
    lo = (v1024 - hi.astype(F32)).astype(BF16)
    heads = jnp.where(lax.broadcasted_iota(jnp.int32, expander.shape, 0) < 16, expander, jnp.zeros_like(expander))
    return _dot(hi, heads, NT_DIMS) + _dot(lo, heads, NT_DIMS)


def _ssd_fwd(xc, proj_a, dt, acum, acum_t, dskip_e, norm_w, bl, t):
    n = bl * t
    nch = t // CHUNK
    L = CHUNK

    def body(xc_blk, z_blk, dt_blk, ac_blk, act_blk, dsk_ref, nw_ref, ys_blk, yp_blk, hp_blk, h_scr, y_scr, x_scr):
        @pl.when(pl.program_id(1) == 0)
        def _():
            h_scr[...] = jnp.zeros_like(h_scr)

        for sub in range(SSD_FWD_CHUNKS):
            rows = pl.ds(sub * L, L)
            chunk(xc_blk.at[rows, :], z_blk.at[rows, :], dt_blk.at[rows, :], ac_blk.at[rows, :], act_blk.at[:, rows],
                  dsk_ref, nw_ref, ys_blk.at[rows, :], yp_blk.at[rows, :], hp_blk.at[sub], h_scr, y_scr, x_scr)

    def chunk(xc_ref, z_ref, dt_ref, ac_ref, act_ref, dsk_ref, nw_ref, ys_ref, yp_ref, hp_ref, h_scr, y_scr, x_scr):
        row = lax.broadcasted_iota(jnp.int32, (L, L), 0)
        col = lax.broadcasted_iota(jnp.int32, (L, L), 1)
        causal = row >= col
        lane128 = lax.broadcasted_iota(jnp.int32, (1, L), 1)
        expander = _head_expander()
        ac_all = ac_ref[...]
        act_all = act_ref[...]
        ac_e = _spread(ac_all, expander)
        e_in = jnp.exp(ac_e)
        dec = jnp.exp(ac_e[L - 1:L, :] - ac_e)
        xs_all = xc_ref[:, 0:SSD_WIDTH]
        x_all = xs_all * _spread(dt_ref[...], expander)
        x_scr[...] = x_all.astype(BF16)
        hp_all = h_scr[...]
        hp_ref[...] = hp_all
        for g in range(2):
            gs = slice(g * 512, (g + 1) * 512)
            bg = xc_ref[:, SSD_WIDTH + g * 128:SSD_WIDTH + (g + 1) * 128].astype(BF16)
            cg = xc_ref[:, SSD_WIDTH + 256 + g * 128:SSD_WIDTH + 256 + (g + 1) * 128].astype(BF16)
            gmat = _dot(cg, bg, NT_DIMS)
            y_off = _dot(cg, hp_all[gs, :].astype(BF16), NT_DIMS) * e_in[:, gs] + dsk_ref[:, gs] * xs_all[:, gs]
            s_new = _dot((x_all[:, gs] * dec[:, gs]).astype(BF16), bg, TN_DIMS)
            for pr in range(4):
                pair = slice((g * 4 + pr) * 128, (g * 4 + pr + 1) * 128)
                x_pair = x_scr[:, pair]
                y_pair = y_off[:, pr * 128:(pr + 1) * 128]
                for j in range(2):
                    h = g * 8 + 2 * pr + j
                    sl = slice(h * HEAD_DIM, (h + 1) * HEAD_DIM)
                    r = 2 * pr + j
                    ldec = jnp.exp(jnp.where(causal, ac_all[:, h:h + 1] - act_all[h:h + 1, :], NEG))
                    x_head = jnp.where((lane128 < HEAD_DIM) == (j == 0), x_pair, jnp.zeros_like(x_pair))
                    y_pair = y_pair + _dot((gmat * ldec).astype(BF16), x_head)
                    elast = jnp.exp(ac_all[L - 1:L, h:h + 1])
                    h_scr[sl, :] = elast * hp_all[sl, :] + s_new[r * HEAD_DIM:(r + 1) * HEAD_DIM, :]
                y_scr[:, pair] = y_pair
        y = y_scr[...]
        yp_ref[...] = y
        zv = z_ref[...]
        yg = y * (zv * jax.nn.sigmoid(zv))
        for g in range(2):
            gs = slice(g * 512, (g + 1) * 512)
            grp = yg[:, gs]
            rstd = lax.rsqrt(jnp.mean(grp * grp, axis=1, keepdims=True) + EPS)
            ys_ref[:, gs] = (grp * rstd * nw_ref[:, gs]).astype(BF16)

    cps = SSD_FWD_CHUNKS
    steps = nch // cps
    rb = lambda b, c: (b * steps + c, 0)
    v1k = pl.BlockSpec((1, SSD_WIDTH), lambda b, c: (0, 0))
    return pl.pallas_call(
        body, name="ssd_fwd", grid=(bl, steps),
        in_specs=[pl.BlockSpec((cps * L, CONV_CH), rb), pl.BlockSpec((cps * L, SSD_WIDTH), rb),
                  pl.BlockSpec((cps * L, 128), rb), pl.BlockSpec((cps * L, 128), rb),
                  pl.BlockSpec((16, cps * L), lambda b, c: (0, b * steps + c)), v1k, v1k],
        out_specs=[pl.BlockSpec((cps * L, SSD_WIDTH), rb), pl.BlockSpec((cps * L, SSD_WIDTH), rb),
                   pl.BlockSpec((cps, SSD_WIDTH, SSD_STATE), lambda b, c: (b * steps + c, 0, 0))],
        out_shape=[jax.ShapeDtypeStruct((n, SSD_WIDTH), BF16), jax.ShapeDtypeStruct((n, SSD_WIDTH), F32),
                   jax.ShapeDtypeStruct((bl * nch, SSD_WIDTH, SSD_STATE), F32)],
        scratch_shapes=[pltpu.VMEM((SSD_WIDTH, SSD_STATE), F32), pltpu.VMEM((L, SSD_WIDTH), F32),
                        pltpu.VMEM((L, SSD_WIDTH), BF16)],
        compiler_params=_cparams(("parallel", "arbitrary")),
    )(xc, proj_a, dt, acum, acum_t, dskip_e, norm_w)


def _ssd_bwd(dys, xc, proj_a, ypre, hprev, dt, gate_d, acum, acum_t, alog128, dskip_e, norm_w, bl, t):
    n = bl * t
    nch = t // CHUNK
    L = CHUNK

    def body(dys_blk, xc_blk, z_blk, yp_blk, hp_blk, dt_blk, gd_blk, ac_blk, act_blk, al_ref, dsk_ref, nw_ref,
             dxc_blk, dz_blk, ddt_blk, dnw_ref, dsk16_ref, da16_ref, db16_ref,
             dh_scr, dy_scr, x_scr, dx_scr, red_scr):
        first = (pl.program_id(0) == 0) & (pl.program_id(1) == 0)

        @pl.when(first)
        def _():
            dnw_ref[...] = jnp.zeros_like(dnw_ref)
            dsk16_ref[...] = jnp.zeros_like(dsk16_ref)
            da16_ref[...] = jnp.zeros_like(da16_ref)
            db16_ref[...] = jnp.zeros_like(db16_ref)

        @pl.when(pl.program_id(1) == 0)
        def _():
            dh_scr[...] = jnp.zeros_like(dh_scr)

        for sub in reversed(range(SSD_BWD_CHUNKS)):
            rows = pl.ds(sub * L, L)
            chunk(dys_blk.at[rows, :], xc_blk.at[rows, :], z_blk.at[rows, :], yp_blk.at[rows, :], hp_blk.at[sub],
                  dt_blk.at[rows, :], gd_blk.at[rows, :], ac_blk.at[rows, :], act_blk.at[:, rows], al_ref, dsk_ref,
                  nw_ref, dxc_blk.at[rows, :], dz_blk.at[rows, :], ddt_blk.at[rows, :], dnw_ref, dsk16_ref, da16_ref,
                  db16_ref, dh_scr, dy_scr, x_scr, dx_scr, red_scr)

    def chunk(dys_ref, xc_ref, z_ref, yp_ref, hp_ref, dt_ref, gd_ref, ac_ref, act_ref, al_ref, dsk_ref, nw_ref,
              dxc_ref, dz_ref, ddt_ref, dnw_ref, dsk16_ref, da16_ref, db16_ref,
              dh_scr, dy_scr, x_scr, dx_scr, red_scr):
        y = yp_ref[...]
        zv = z_ref[...]
        sz = jax.nn.sigmoid(zv)
        gate = zv * sz
        yg = y * gate
        dout = dys_ref[...]
        nw = nw_ref[...]
        for g in range(2):
            gs = slice(g * 512, (g + 1) * 512)
            grp = yg[:, gs]
            rstd = lax.rsqrt(jnp.mean(grp * grp, axis=1, keepdims=True) + EPS)
            ghat = grp * rstd
            dnw_ref[:, gs] += jnp.sum(dout[:, gs] * ghat, axis=0, keepdims=True)
            gw = dout[:, gs] * nw[:, gs]
            dyg = rstd * (gw - ghat * jnp.mean(gw * ghat, axis=1, keepdims=True))
            dy_scr[:, gs] = dyg * gate[:, gs]
            dz_ref[:, gs] = (dyg * y[:, gs] * (sz[:, gs] * (1.0 + zv[:, gs] * (1.0 - sz[:, gs])))).astype(BF16)

        row = lax.broadcasted_iota(jnp.int32, (L, L), 0)
        col = lax.broadcasted_iota(jnp.int32, (L, L), 1)
        causal = row >= col
        lane128 = lax.broadcasted_iota(jnp.int32, (1, L), 1)
        rows128 = lax.broadcasted_iota(jnp.int32, (L, 1), 0)
        last_row = rows128 == (L - 1)
        expander = _head_expander()
        ac_all = ac_ref[...]
        act_all = act_ref[...]
        dt_all = dt_ref[...]
        dt_e = _spread(dt_all, expander)
        ac_e = _spread(ac_all, expander)
        e_in = jnp.exp(ac_e)
        dec = jnp.exp(ac_e[L - 1:L, :] - ac_e)
        xs_all = xc_ref[:, 0:SSD_WIDTH]
        x_all = xs_all * dt_e
        x_scr[...] = x_all.astype(BF16)
        dy_all = dy_scr[...]
        hp_all = hp_ref[...]
        ds_all = dh_scr[...]
        dsk_cols = jnp.sum(dy_all * xs_all, axis=0, keepdims=True)
        dac = jnp.zeros((L, L), F32)
        dac_row = jnp.zeros((L, L), F32)
        ddec_cols = []
        for g in range(2):
            gs = slice(g * 512, (g + 1) * 512)
            bsl = slice(SSD_WIDTH + g * 128, SSD_WIDTH + (g + 1) * 128)
            csl = slice(SSD_WIDTH + 256 + g * 128, SSD_WIDTH + 256 + (g + 1) * 128)
            bg = xc_ref[:, bsl].astype(BF16)
            cg = xc_ref[:, csl].astype(BF16)
            gmat = _dot(cg, bg, NT_DIMS)
            hpb = hp_all[gs, :].astype(BF16)
            dsb = ds_all[gs, :].astype(BF16)
            ch = _dot(cg, hpb, NT_DIMS)
            dye = dy_all[:, gs] * e_in[:, gs]
            dyeb = dye.astype(BF16)
            dc_acc = _dot(dyeb, hpb)
            dhp = _dot(dyeb, cg, TN_DIMS)
            dxd = _dot(bg, dsb, NT_DIMS)
            db_acc = _dot((x_all[:, gs] * dec[:, gs]).astype(BF16), dsb)
            ddec = dxd * x_all[:, gs] * dec[:, gs]
            ddec_cols.append(jnp.sum(ddec, axis=0, keepdims=True))
            dx_inter = dxd * dec[:, gs]
            red_scr[:, gs] = dye * ch - ddec
            dg_sum = jnp.zeros((L, L), F32)
            for pr in range(4):
                pair = slice((g * 4 + pr) * 128, (g * 4 + pr + 1) * 128)
                x_pair = x_scr[:, pair]
                dy_pair = dy_scr[:, pair].astype(BF16)
                dx_pair = dx_inter[:, pr * 128:(pr + 1) * 128]
                for j in range(2):
                    h = g * 8 + 2 * pr + j
                    r = 2 * pr + j
                    sl = slice(h * HEAD_DIM, (h + 1) * HEAD_DIM)
                    onehot_w = lane128 == h
                    ldec = jnp.exp(jnp.where(causal, ac_all[:, h:h + 1] - act_all[h:h + 1, :], NEG))
                    mf = gmat * ldec
                    dyb = jnp.where((lane128 < HEAD_DIM) == (j == 0), dy_pair, jnp.zeros_like(dy_pair))
                    dm = _dot(dyb, x_pair, NT_DIMS)
                    dx_pair = dx_pair + _dot(mf.astype(BF16), dyb, TN_DIMS)
                    dg_sum = dg_sum + dm * ldec
                    wmat = dm * mf
                    elast = jnp.exp(ac_all[L - 1:L, h:h + 1])
                    hp_h = hp_all[sl, :]
                    ds_h = ds_all[sl, :]
                    extra = elast * jnp.sum(jnp.sum(hp_h * ds_h, axis=1, keepdims=True), axis=0, keepdims=True)
                    dac = dac + jnp.where(onehot_w,
                                          jnp.sum(wmat, axis=1, keepdims=True) + jnp.where(last_row, extra, 0.0), 0.0)
                    dac_row = dac_row + jnp.where(rows128 == h, -jnp.sum(wmat, axis=0, keepdims=True), 0.0)
                    dh_scr[sl, :] = elast * ds_h + dhp[r * HEAD_DIM:(r + 1) * HEAD_DIM, :]
                dx_scr[:, pair] = dx_pair
            dgb = dg_sum.astype(BF16)
            dxc_ref[:, csl] = dc_acc + _dot(dgb, bg)
            dxc_ref[:, bsl] = db_acc + _dot(dgb, cg, TN_DIMS)
        dx_all = dx_scr[...]
        dxc_ref[:, 0:SSD_WIDTH] = dx_all * dt_e + dsk_ref[...] * dy_all
        red = red_scr[...]
        dac_slab = _head_sums(red, expander)
        ddec_tot = _head_sums(jnp.broadcast_to(jnp.concatenate(ddec_cols, axis=1), (8, SSD_WIDTH)), expander)
        ddt_x = _head_sums(dx_all * xs_all, expander)
        dsk16_ref[...] += _head_sums(jnp.broadcast_to(dsk_cols, (8, SSD_WIDTH)), expander)[0:1, 0:16]
        dac = dac + dac_slab + jnp.transpose(dac_row) + jnp.where(last_row, ddec_tot[0:1, :], 0.0)
        triu = (row <= col).astype(F32)
        da = jnp.dot(triu, dac, precision=HIGHEST, preferred_element_type=F32)
        a_row = -jnp.exp(al_ref[...])
        ddt = jnp.where(lane128 < 16, (ddt_x + da * a_row) * gd_ref[...], 0.0)
        ddt_ref[...] = ddt
        da16_ref[...] += (jnp.sum(da * dt_all, axis=0, keepdims=True) * a_row)[:, 0:16]
        db16_ref[...] += jnp.sum(ddt, axis=0, keepdims=True)[:, 0:16]

    cps = SSD_BWD_CHUNKS
    steps = nch // cps
    rb = lambda b, c: (b * steps + steps - 1 - c, 0)
    v1k = pl.BlockSpec((1, SSD_WIDTH), lambda b, c: (0, 0))
    v16 = pl.BlockSpec((1, 16), lambda b, c: (0, 0))
    v128 = pl.BlockSpec((1, 128), lambda b, c: (0, 0))
    wide = pl.BlockSpec((cps * L, SSD_WIDTH), rb)
    s128 = pl.BlockSpec((cps * L, 128), rb)
    return pl.pallas_call(
        body, name="ssd_bwd", grid=(bl, steps),
        in_specs=[wide, pl.BlockSpec((cps * L, CONV_CH), rb), wide, wide,
                  pl.BlockSpec((cps, SSD_WIDTH, SSD_STATE), lambda b, c: (b * steps + steps - 1 - c, 0, 0)),
                  s128, s128, s128, pl.BlockSpec((16, cps * L), lambda b, c: (0, b * steps + steps - 1 - c)),
                  v128, v1k, v1k],
        out_specs=[pl.BlockSpec((cps * L, CONV_CH), rb), wide, s128, v1k, v16, v16, v16],
        out_shape=[jax.ShapeDtypeStruct((n, CONV_CH), F32), jax.ShapeDtypeStruct((n, PA_WIDTH), BF16),
                   jax.ShapeDtypeStruct((n, 128), F32), jax.ShapeDtypeStruct((1, SSD_WIDTH), F32),
                   jax.ShapeDtypeStruct((1, 16), F32), jax.ShapeDtypeStruct((1, 16), F32),
                   jax.ShapeDtypeStruct((1, 16), F32)],
        scratch_shapes=[pltpu.VMEM((SSD_WIDTH, SSD_STATE), F32), pltpu.VMEM((L, SSD_WIDTH), F32),
                        pltpu.VMEM((L, SSD_WIDTH), BF16), pltpu.VMEM((L, SSD_WIDTH), F32),
                        pltpu.VMEM((L, SSD_WIDTH), F32)],
        compiler_params=_cparams(("arbitrary", "arbitrary")),
    )(dys, xc, proj_a, ypre, hprev, dt, gate_d, acum, acum_t, alog128, dskip_e, norm_w)


def _attn_fwd(qkv, negc, bl, t):
    n = bl * t
    tb_ = ATT_BLOCK
    nb = t // tb_
    scale2 = LOG2E / math.sqrt(HEAD_DIM)

    def body(q_ref, k_ref, v_ref, c_ref, o_ref, lse_ref, v0_scr, v1_scr, k0_scr, k1_scr):
        row = lax.broadcasted_iota(jnp.int32, (tb_, tb_), 0)
        col = lax.broadcasted_iota(jnp.int32, (tb_, tb_), 1)
        causal = row >= col
        lane = lax.broadcasted_iota(jnp.int32, (1, 128), 1)
        v_pair = v_ref[...].astype(F32)
        k_pair = k_ref[...]
        v_scrs = (v0_scr, v1_scr)
        k_scrs = (k0_scr, k1_scr)
        for j in range(2):
            v_head = v_pair if j == 0 else pltpu.roll(v_pair, HEAD_DIM, 1)
            v_scrs[j][...] = jnp.where(lane < HEAD_DIM, v_head, jnp.where(lane == HEAD_DIM, 1.0, 0.0)).astype(BF16)
            k_scrs[j][...] = jnp.where((lane < HEAD_DIM) == (j == 0), k_pair, jnp.zeros_like(k_pair))
        for qi in range(nb):
            r0, lk = qi * tb_, (qi + 1) * tb_
            for j in range(2):
                sl = slice(j * HEAD_DIM, (j + 1) * HEAD_DIM)
                s = _dot(q_ref[r0:lk, :], k_scrs[j][0:lk, :], NT_DIMS) * scale2 + c_ref[j:j + 1, 0:lk] * LOG2E
                tail = jnp.where(causal, s[:, r0:lk], NEG)
                s = tail if qi == 0 else jnp.concatenate([s[:, 0:r0], tail], axis=1)
                m = jnp.max(s, axis=1, keepdims=True)
                p = jnp.exp2(s - m)
                acc = _dot(p.astype(BF16), v_scrs[j][0:lk, :])
                l = acc[:, HEAD_DIM:HEAD_DIM + 1]
                o_ref[r0:lk, sl] = (acc[:, 0:HEAD_DIM] / l).astype(BF16)
                lse_ref[r0:lk, sl] = jnp.broadcast_to(m + jnp.log(l) * LOG2E, (tb_, HEAD_DIM))

    blk = lambda off: pl.BlockSpec((t, 128), lambda b, hp: (b, off + hp))
    return pl.pallas_call(
        body, name="fox_attn_fwd", grid=(bl, 8),
        in_specs=[blk(0), blk(8), blk(16), pl.BlockSpec((None, None, 8, t), lambda b, hp: (b, hp, 0, 0))],
        out_specs=[blk(0), blk(0)],
        out_shape=[jax.ShapeDtypeStruct((n, ATT_WIDTH), BF16), jax.ShapeDtypeStruct((n, ATT_WIDTH), F32)],
        scratch_shapes=[pltpu.VMEM((t, 128), BF16)] * 4,
        compiler_params=_cparams(("parallel", "parallel")),
    )(qkv, qkv, qkv, negc)


def _attn_bwd(qkv, do, o, lse, negc, after, bl, t):
    n = bl * t
    tb_ = ATT_BLOCK
    nb = t // tb_
    scale = 1.0 / math.sqrt(HEAD_DIM)
    scale2 = LOG2E * scale

    def body(q_ref, k_ref, v_ref, do_ref, o_ref, lse_ref, c_ref, after_ref, dq_ref, dk_ref, dv_ref, dc_ref,
             dq0_scr, delta_scr, dq1_scr, qt0_scr, qt1_scr, dot_scr, dkt0_scr, dkt1_scr, dvt_scr):
        row = lax.broadcasted_iota(jnp.int32, (tb_, tb_), 0)
        col = lax.broadcasted_iota(jnp.int32, (tb_, tb_), 1)
        causal = row >= col
        lane = lax.broadcasted_iota(jnp.int32, (1, 128), 1)
        dq_scrs = (dq0_scr, dq1_scr)
        qt_scrs = (qt0_scr, qt1_scr)
        dkt_scrs = (dkt0_scr, dkt1_scr)
        dq0_scr[...] = jnp.zeros_like(dq0_scr)
        dq1_scr[...] = jnp.zeros_like(dq1_scr)
        dc_ref[...] = jnp.zeros_like(dc_ref)
        q_t = jnp.transpose(q_ref[...].astype(F32))
        ones_row = jnp.where(lax.broadcasted_iota(jnp.int32, (8, t), 0) == 0, 1.0, 0.0)
        for j in range(2):
            qt_scrs[j][...] = jnp.concatenate(
                [q_t[j * HEAD_DIM:(j + 1) * HEAD_DIM, :], ones_row, jnp.zeros((HEAD_DIM - 8, t), F32)],
                axis=0).astype(BF16)
        dot_scr[...] = jnp.transpose(do_ref[...].astype(F32)).astype(BF16)
        prod = do_ref[...].astype(F32) * o_ref[...].astype(F32)
        for j in range(2):
            sl = slice(j * HEAD_DIM, (j + 1) * HEAD_DIM)
            delta_scr[:, sl] = jnp.broadcast_to(jnp.sum(prod[:, sl], axis=1, keepdims=True), (t, HEAD_DIM))
        for kj in range(nb):
            r0, r1 = kj * tb_, (kj + 1) * tb_
            k_blk = k_ref[r0:r1, :]
            v_blk = v_ref[r0:r1, :]
            k_pair = k_blk.astype(F32)
            for j in range(2):
                sl = slice(j * HEAD_DIM, (j + 1) * HEAD_DIM)
                one = slice(j * HEAD_DIM, j * HEAD_DIM + 1)
                own = (lane < HEAD_DIM) == (j == 0)
                k_head = k_pair if j == 0 else pltpu.roll(k_pair, HEAD_DIM, 1)
                k_ones = jnp.where(lane < HEAD_DIM, k_head, jnp.where(lane == HEAD_DIM, 1.0, 0.0)).astype(BF16)
                s = (_dot(q_ref[r0:t, :], jnp.where(own, k_blk, jnp.zeros_like(k_blk)), NT_DIMS) * scale2
                     + c_ref[j:j + 1, r0:r1] * LOG2E)
                head = jnp.where(causal, s[0:tb_, :], NEG)
                s = head if kj == nb - 1 else jnp.concatenate([head, s[tb_:, :]], axis=0)
                p = jnp.exp2(s - lse_ref[r0:t, one])
                dp = _dot(do_ref[r0:t, :], jnp.where(own, v_blk, jnp.zeros_like(v_blk)), NT_DIMS)
                ds = p * (dp - delta_scr[r0:t, one])
                dsb = ds.astype(BF16)
                dvt_scr[sl, r0:r1] = _dot(dot_scr[sl, r0:t], p.astype(BF16))
                dkt_scrs[j][:, r0:r1] = _dot(qt_scrs[j][:, r0:t], dsb)
                dq_scrs[j][r0:t, :] += _dot(dsb, k_ones)
        dv_ref[...] = jnp.transpose(dvt_scr[...]).astype(BF16)
        for j in range(2):
            sl = slice(j * HEAD_DIM, (j + 1) * HEAD_DIM)
            acc = dq_scrs[j][...]
            dkt = dkt_scrs[j][...]
            dq_ref[:, sl] = (acc[:, 0:HEAD_DIM] * scale).astype(BF16)
            dk_ref[:, sl] = (jnp.transpose(dkt)[:, 0:HEAD_DIM] * scale).astype(BF16)
            dc_ref[j:j + 1, :] = jnp.transpose(acc)[HEAD_DIM:HEAD_DIM + 1, :] - dkt[HEAD_DIM:HEAD_DIM + 1, :]

    blk = lambda off: pl.BlockSpec((t, 128), lambda b, hp: (b, off + hp))
    cblk = pl.BlockSpec((None, None, 8, t), lambda b, hp: (b, hp, 0, 0))
    return pl.pallas_call(
        body, name="fox_attn_bwd", grid=(bl, 8),
        in_specs=[blk(0), blk(8), blk(16), blk(0), blk(0), blk(0), cblk, ANY],
        out_specs=[blk(0), blk(0), blk(0), cblk],
        out_shape=[jax.ShapeDtypeStruct((n, ATT_WIDTH), BF16)] * 3 + [jax.ShapeDtypeStruct((bl, 8, 8, t), F32)],
        scratch_shapes=[pltpu.VMEM((t, 128), F32), pltpu.VMEM((t, 128), F32), pltpu.VMEM((t, 128), F32),
                        pltpu.VMEM((128, t), BF16), pltpu.VMEM((128, t), BF16), pltpu.VMEM((128, t), BF16),
                        pltpu.VMEM((128, t), F32), pltpu.VMEM((128, t), F32), pltpu.VMEM((128, t), F32)],
        compiler_params=_cparams(("parallel", "parallel")),
    )(qkv, qkv, qkv, do, o, lse, negc, after)


def _adamw(w, g, m, v, *, name):
    lead = w.ndim == 3
    r, c = w.shape[-2:]
    tr = _pick(r, (256, IN_SHARD // 3, 128, 64, 32, 16, 8))
    bc1 = 1.0 - ADAM_B1 ** ADAM_STEP
    bc2 = 1.0 - ADAM_B2 ** ADAM_STEP

    def body(w_ref, g_ref, m_ref, v_ref, d_ref, nm_ref, nv_ref):
        gv = g_ref[...]
        mn = ADAM_B1 * m_ref[...] + (1.0 - ADAM_B1) * gv
        vn = ADAM_B2 * v_ref[...] + (1.0 - ADAM_B2) * (gv * gv)
        m_hat = mn / bc1
        v_hat = vn / bc2
        d_ref[...] = -ADAM_LR * (m_hat / (jnp.sqrt(v_hat) + ADAM_EPS) + ADAM_WD * w_ref[...])
        nm_ref[...] = mn
        nv_ref[...] = vn

    flat = pl.BlockSpec((tr, c), lambda i: (i, 0))
    blk = pl.BlockSpec((None, tr, c), lambda i: (0, i, 0)) if lead else flat
    return pl.pallas_call(
        body, name=name, grid=(r // tr,), in_specs=[blk, flat, blk, blk], out_specs=[blk] * 3,
        out_shape=[jax.ShapeDtypeStruct(w.shape, F32)] * 3,
        compiler_params=_cparams(("parallel",)),
    )(w, g, m, v)


def _sum_leading(parts, *, name, out_dtype=F32):
    k, r, c = parts.shape
    tr = _pick(r, (512, 256, 128, 96, 64, 32, 16, 8))

    def body(p_ref, o_ref):
        acc = p_ref[0].astype(F32)
        for i in range(1, k):
            acc = acc + p_ref[i].astype(F32)
        o_ref[...] = acc.astype(out_dtype)

    return pl.pallas_call(
        body, name=name, grid=(r // tr,),
        in_specs=[pl.BlockSpec((k, tr, c), lambda i: (0, i, 0))],
        out_specs=pl.BlockSpec((tr, c), lambda i: (i, 0)),
        out_shape=jax.ShapeDtypeStruct((r, c), out_dtype),
        compiler_params=_cparams(("parallel",)),
    )(parts)


def _add_pair(a, b, *, name):
    k, r, c = a.shape
    tr = _pick(r, (512, 256, 128))

    def body(a_ref, b_ref, o_ref):
        o_ref[...] = (a_ref[...].astype(F32) + b_ref[...].astype(F32)).astype(BF16)

    blk = pl.BlockSpec((None, tr, c), lambda j, i: (j, i, 0))
    return pl.pallas_call(
        body, name=name, grid=(k, r // tr), in_specs=[blk, blk], out_specs=blk,
        out_shape=jax.ShapeDtypeStruct((k, r, c), BF16),
        compiler_params=_cparams(("parallel", "parallel")),
    )(a, b)


ANY = pl.BlockSpec(memory_space=pl.ANY)


def _chip_peers(x, y):
    return [(1 - x, y, 2 * (1 - x) + y), (x, 1 - y, 2 * x + 1 - y), (1 - x, 1 - y, 2 * (1 - x) + 1 - y)]


def _gather_weights(blob, *, name):
    rows, cols = blob.shape
    half_rows = rows // 2

    def body(b_ref, o_ref, send_sems, recv_sems):
        x, y, c = lax.axis_index("x"), lax.axis_index("y"), lax.axis_index("c")
        me = 2 * x + y
        sibling = (x, y, 1 - c)
        peers = _chip_peers(x, y)

        def half(chip, hc):
            return o_ref.at[chip, pl.ds(hc * half_rows, half_rows), :]

        def copy(k, src, chip, hc, to):
            return pltpu.make_async_remote_copy(src_ref=src, dst_ref=half(chip, hc), send_sem=send_sems.at[k],
                                                recv_sem=recv_sems.at[k], device_id=to, device_id_type=MESH)

        my_half = b_ref.at[pl.ds(c * half_rows, half_rows), :]
        first = [copy(k, my_half, me, c, (px, py, c)) for k, (px, py, _) in enumerate(peers)]
        for cp in first:
            cp.start()
        passed = [copy(3 + k, half(pc, c), pc, c, sibling) for k, (_, _, pc) in enumerate(peers)]
        for k, (px, py, pc) in enumerate(peers):
            copy(k, my_half, pc, c, (px, py, c)).wait_recv()
            passed[k].start()
        for k, (_, _, pc) in enumerate(peers):
            copy(3 + k, half(pc, 1 - c), pc, 1 - c, sibling).wait_recv()
        for cp in first + passed:
            cp.wait_send()

    return pl.pallas_call(
        body, name=name, in_specs=[ANY], out_specs=ANY,
        out_shape=jax.ShapeDtypeStruct((N_CHIPS, rows, cols), BF16),
        scratch_shapes=[pltpu.SemaphoreType.DMA((6,)), pltpu.SemaphoreType.DMA((6,))],
    )(blob)


def _swap_halves(g, *, name):
    _, rows, cols = g.shape
    half_rows = rows // 2

    def body(g_ref, o_ref, send_sem, recv_sem):
        x, y, c = lax.axis_index("x"), lax.axis_index("y"), lax.axis_index("c")
        cp = pltpu.make_async_remote_copy(
            src_ref=g_ref.at[:, pl.ds((1 - c) * half_rows, half_rows), :], dst_ref=o_ref,
            send_sem=send_sem, recv_sem=recv_sem, device_id=(x, y, 1 - c), device_id_type=MESH)
        cp.start()
        cp.wait()

    return pl.pallas_call(
        body, name=name, in_specs=[ANY], out_specs=ANY,
        out_shape=jax.ShapeDtypeStruct((N_CHIPS, half_rows, cols), BF16),
        scratch_shapes=[pltpu.SemaphoreType.DMA, pltpu.SemaphoreType.DMA],
    )(g)


HBM_SPEC = pl.BlockSpec(memory_space=pltpu.HBM)
SEM_SPEC = pl.BlockSpec(memory_space=pltpu.SEMAPHORE)
SPLIT_EFFECT = pltpu.SideEffectType.DATAFLOW_SIDE_EFFECTING


def _gather_peers_copies(b_ref, land_ref, send_sems, recv_sems, sending):
    x, y, c = lax.axis_index("x"), lax.axis_index("y"), lax.axis_index("c")
    me = 2 * x + y
    half_rows = b_ref.shape[0] // 2
    src = b_ref.at[pl.ds(c * half_rows, half_rows), :]
    return [pltpu.make_async_remote_copy(
        src_ref=src, dst_ref=land_ref.at[me if sending else pc, pl.ds(c * half_rows, half_rows), :],
        send_sem=send_sems.at[k], recv_sem=recv_sems.at[k], device_id=(px, py, c), device_id_type=MESH)
        for k, (px, py, pc) in enumerate(_chip_peers(x, y))]


def _gather_start(blob, after, *, name):
    shape = (N_CHIPS,) + blob.shape

    def body(b_ref, land_ref, after_ref, send_sems, recv_sems, b_thru, land_thru, token):
        for cp in _gather_peers_copies(b_ref, land_ref, send_sems, recv_sems, True):
            cp.start()
        token[...] = jnp.zeros_like(token)

    return pl.pallas_call(
        body, name=name,
        out_shape=(pltpu.SemaphoreType.DMA((3,)), pltpu.SemaphoreType.DMA((3,)), pltpu.HBM(blob.shape, blob.dtype),
                   pltpu.HBM(shape, blob.dtype), jax.ShapeDtypeStruct((8, 128), F32)),
        in_specs=(HBM_SPEC, HBM_SPEC, ANY),
        out_specs=(SEM_SPEC, SEM_SPEC, HBM_SPEC, HBM_SPEC, pl.BlockSpec(memory_space=pltpu.VMEM)),
        input_output_aliases={0: 2, 1: 3},
        compiler_params=pltpu.CompilerParams(has_side_effects=SPLIT_EFFECT),
    )(pltpu.with_memory_space_constraint(blob, pltpu.HBM),
      pltpu.with_memory_space_constraint(lax.empty(shape, blob.dtype), pltpu.HBM), after)


def _gather_wait(send_sems, recv_sems, b_thru, land_thru, after, *, name):
    def body(b_ref, land_ref, send_sems, recv_sems, after_ref, b_dead, got_ref):
        for cp in _gather_peers_copies(b_ref, land_ref, send_sems, recv_sems, False):
            cp.wait_send()
            cp.wait_recv()

    return pl.pallas_call(
        body, name=name,
        out_shape=(pltpu.HBM(b_thru.shape, b_thru.dtype), pltpu.HBM(land_thru.shape, land_thru.dtype)),
        in_specs=(HBM_SPEC, HBM_SPEC, SEM_SPEC, SEM_SPEC, ANY), out_specs=(HBM_SPEC, HBM_SPEC),
        input_output_aliases={0: 0, 1: 1},
        compiler_params=pltpu.CompilerParams(has_side_effects=SPLIT_EFFECT),
    )(b_thru, land_thru, send_sems, recv_sems, after)


def _gather_forward(land, *, name):
    half_rows = land.shape[1] // 2

    def body(l_ref, o_ref, send_sems, recv_sems):
        x, y, c = lax.axis_index("x"), lax.axis_index("y"), lax.axis_index("c")
        cps = []
        for k, (_, _, pc) in enumerate(_chip_peers(x, y)):
            mine = pl.ds(c * half_rows, half_rows)
            cps.append(pltpu.make_async_remote_copy(
                src_ref=l_ref.at[pc, mine, :], dst_ref=o_ref.at[pc, mine, :], send_sem=send_sems.at[k],
                recv_sem=recv_sems.at[k], device_id=(x, y, 1 - c), device_id_type=MESH))
        for cp in cps:
            cp.start()
        for k, (_, _, pc) in enumerate(_chip_peers(x, y)):
            theirs = pl.ds((1 - c) * half_rows, half_rows)
            pltpu.make_async_remote_copy(
                src_ref=l_ref.at[pc, theirs, :], dst_ref=o_ref.at[pc, theirs, :], send_sem=send_sems.at[k],
                recv_sem=recv_sems.at[k], device_id=(x, y, 1 - c), device_id_type=MESH).wait_recv()
        for cp in cps:
            cp.wait_send()

    return pl.pallas_call(
        body, name=name, in_specs=[ANY], out_specs=ANY, input_output_aliases={0: 0},
        out_shape=jax.ShapeDtypeStruct(land.shape, land.dtype),
        scratch_shapes=[pltpu.SemaphoreType.DMA((3,)), pltpu.SemaphoreType.DMA((3,))],
    )(land)


def _exchange_peers_copies(p_ref, land_ref, send_sems, recv_sems, sending):
    x, y, c = lax.axis_index("x"), lax.axis_index("y"), lax.axis_index("c")
    me = 2 * x + y
    return [pltpu.make_async_remote_copy(src_ref=p_ref.at[pc], dst_ref=land_ref.at[me if sending else pc],
                                         send_sem=send_sems.at[k], recv_sem=recv_sems.at[k],
                                         device_id=(px, py, c), device_id_type=MESH)
            for k, (px, py, pc) in enumerate(_chip_peers(x, y))]


def _exchange_start(p, *, name):
    def body(p_ref, land_ref, send_sems, recv_sems, p_thru, land_thru, token):
        for cp in _exchange_peers_copies(p_ref, land_ref, send_sems, recv_sems, True):
            cp.start()
        token[...] = jnp.zeros_like(token)

    return pl.pallas_call(
        body, name=name,
        out_shape=(pltpu.SemaphoreType.DMA((3,)), pltpu.SemaphoreType.DMA((3,)), pltpu.HBM(p.shape, p.dtype),
                   pltpu.HBM(p.shape, p.dtype), jax.ShapeDtypeStruct((8, 128), F32)),
        in_specs=(HBM_SPEC, HBM_SPEC),
        out_specs=(SEM_SPEC, SEM_SPEC, HBM_SPEC, HBM_SPEC, pl.BlockSpec(memory_space=pltpu.VMEM)),
        input_output_aliases={0: 2, 1: 3},
        compiler_params=pltpu.CompilerParams(has_side_effects=SPLIT_EFFECT),
    )(pltpu.with_memory_space_constraint(p, pltpu.HBM),
      pltpu.with_memory_space_constraint(lax.empty(p.shape, p.dtype), pltpu.HBM))


def _exchange_wait(send_sems, recv_sems, p_thru, land_thru, after, *, name):
    def body(p_ref, land_ref, send_sems, recv_sems, after_ref, p_dead, got_ref):
        for cp in _exchange_peers_copies(p_ref, land_ref, send_sems, recv_sems, False):
            cp.wait_send()
            cp.wait_recv()

    return pl.pallas_call(
        body, name=name,
        out_shape=(pltpu.HBM(p_thru.shape, p_thru.dtype), pltpu.HBM(p_thru.shape, p_thru.dtype)),
        in_specs=(HBM_SPEC, HBM_SPEC, SEM_SPEC, SEM_SPEC, ANY), out_specs=(HBM_SPEC, HBM_SPEC),
        input_output_aliases={0: 0, 1: 1},
        compiler_params=pltpu.CompilerParams(has_side_effects=SPLIT_EFFECT),
    )(p_thru, land_thru, send_sems, recv_sems, after)


def _sum_parts(parts, own, *, name):
    k, r, c = parts.shape
    tr = _pick(r, (512, 256, 128))

    def body(p_ref, own_ref, o_ref):
        me = 2 * lax.axis_index("x") + lax.axis_index("y")
        acc = jnp.zeros((tr, c), F32)
        for i in range(k):
            acc = acc + jnp.where(me == i, own_ref[i], p_ref[i]).astype(F32)
        o_ref[...] = acc

    blk = pl.BlockSpec((k, tr, c), lambda i: (0, i, 0))
    return pl.pallas_call(
        body, name=name, grid=(r // tr,), in_specs=[blk, blk],
        out_specs=pl.BlockSpec((tr, c), lambda i: (i, 0)),
        out_shape=jax.ShapeDtypeStruct((r, c), F32),
        compiler_params=_cparams(("parallel",)),
    )(parts, own)


def _join_halves(gh, *, name):
    def body(g_ref, o_ref, send_sem, recv_sem):
        x, y, c = lax.axis_index("x"), lax.axis_index("y"), lax.axis_index("c")
        cp = pltpu.make_async_remote_copy(src_ref=g_ref, dst_ref=o_ref, send_sem=send_sem, recv_sem=recv_sem,
                                          device_id=(x, y, 1 - c), device_id_type=MESH)
        cp.start()
        cp.wait()

    other = pl.pallas_call(
        body, name=name, in_specs=[ANY], out_specs=ANY,
        out_shape=jax.ShapeDtypeStruct(gh.shape, F32),
        scratch_shapes=[pltpu.SemaphoreType.DMA, pltpu.SemaphoreType.DMA],
    )(gh)
    south = lax.axis_index("c") == 0
    return jnp.concatenate([jnp.where(south, gh, other), jnp.where(south, other, gh)], axis=0)


def _gather_small(s, *, name):
    rows = s.shape[0]

    def body(s_ref, o_ref, send_sems, recv_sems, local_sem):
        x, y, c = lax.axis_index("x"), lax.axis_index("y"), lax.axis_index("c")
        me = 4 * x + 2 * y + c
        mine = pltpu.make_async_copy(s_ref, o_ref.at[me], local_sem)
        mine.start()
        peers = []
        for k in range(1, 8):
            peers.append((1 - x if k & 4 else x, 1 - y if k & 2 else y, 1 - c if k & 1 else c))
        cps = [pltpu.make_async_remote_copy(src_ref=s_ref, dst_ref=o_ref.at[me], send_sem=send_sems.at[k],
                                            recv_sem=recv_sems.at[k], device_id=p, device_id_type=MESH)
               for k, p in enumerate(peers)]
        for cp in cps:
            cp.start()
        for k, (px, py, pc) in enumerate(peers):
            pltpu.make_async_remote_copy(src_ref=s_ref, dst_ref=o_ref.at[4 * px + 2 * py + pc],
                                         send_sem=send_sems.at[k], recv_sem=recv_sems.at[k],
                                         device_id=(px, py, pc), device_id_type=MESH).wait_recv()
        for cp in cps:
            cp.wait_send()
        mine.wait()

    return pl.pallas_call(
        body, name=name, in_specs=[ANY], out_specs=ANY,
        out_shape=jax.ShapeDtypeStruct((8, rows, 128), F32),
        scratch_shapes=[pltpu.SemaphoreType.DMA((7,)), pltpu.SemaphoreType.DMA((7,)), pltpu.SemaphoreType.DMA],
    )(s)


IN_SHARD = IN_WIDTH // N_CHIPS
IN_SHARD_PAD = 1536
UP_ROWS, DOWN_ROWS, OUT_ROWS = 1024, 1024, 512
REST_ROWS = UP_ROWS + DOWN_ROWS + OUT_ROWS


def _pack_in(w_in_s):
    return jnp.pad(w_in_s, ((0, 0), (0, IN_SHARD_PAD - IN_SHARD))).astype(BF16)


def _pack_rest(w_out_s, w_up_s, w_down_s):
    return jnp.concatenate([w_up_s, w_down_s, w_out_s], axis=0).astype(BF16)


def _unpack_rest(blob):
    return (blob[UP_ROWS + DOWN_ROWS:], blob[0:UP_ROWS], blob[UP_ROWS:UP_ROWS + DOWN_ROWS])


def _with_own(gathered, own):
    me = 2 * lax.axis_index("x") + lax.axis_index("y")
    return [jnp.where(me == j, own, gathered[j]) for j in range(N_CHIPS)]


def _full_w_in(g_in, own):
    return jnp.concatenate([s[:, :IN_SHARD] for s in _with_own(g_in, own)], axis=1)


def _full_rest(g_rest, own):
    parts = [_unpack_rest(s) for s in _with_own(g_rest, own)]
    w_out = jnp.concatenate([p[0] for p in parts], axis=0)
    w_up = jnp.concatenate([p[1] for p in parts], axis=1)
    w_down = jnp.concatenate([p[2] for p in parts], axis=0)
    return w_out, w_up, w_down


def _split_w_in(w_in):
    z_xbc = w_in[:, 0:2560]
    dt = w_in[:, 2560:2576]
    qkv = w_in[:, 2576:5648]
    f = w_in[:, 5648:5664]
    pad = jnp.zeros((w_in.shape[0], PA_WIDTH - 2592), w_in.dtype)
    return jnp.concatenate([z_xbc, dt, f, pad], axis=1), qkv


def _merge_w_in(d_a, d_qkv):
    return jnp.concatenate([d_a[:, 0:2560], d_a[:, 2560:2576], d_qkv, d_a[:, 2576:2592]], axis=1)


def _local_step(x3, target3, w_in, rest_weights, norm_mix_w, conv_w, conv_b, dt_bias, a_log, d_skip,
                ssd_norm_w, f_bias, norm_mlp_w, norm_final_w, first_after=None, early_grads=None, late_grads=None):
    bl, t, d = x3.shape
    n = bl * t
    x = x3.reshape(n, d)
    target = target3.reshape(n, d)
    w_a, w_qkv = _split_w_in(w_in)
    nfw = norm_final_w.reshape(1, d)
    dskip_e = jnp.repeat(d_skip, HEAD_DIM, axis=1)
    nb = t // ATT_BLOCK

    r1, r2, kt = min(n, 1024), min(n, 512), min(n, 2048)
    if first_after is None:
        first_after = jnp.zeros((8, 128), F32)
    h0, rstd0, proj_a = _norm_mm(x, norm_mix_w, w_a, first_after, name="norm_mix_proj_a", tm=r2)
    qkv = _mm(h0, w_qkv, name="proj_qkv", tiles=(r2, QKV_WIDTH, D_MODEL), out_dtype=BF16)
    bias128 = jnp.concatenate([dt_bias, f_bias, jnp.zeros((1, 96), F32)], axis=1)
    alog128 = jnp.concatenate([a_log, jnp.zeros((1, 112), F32)], axis=1)
    dt, gate_d, acum, acum_t, negc = _prep(proj_a, bias128, alog128, bl, t)
    xc, dsilu = _conv_fwd(proj_a, conv_w, conv_b, bl, t)
    y_ssd, y_pre, hprev = _ssd_fwd(xc, proj_a, dt, acum, acum_t, dskip_e, ssd_norm_w, bl, t)
    y_att, lse = _attn_fwd(qkv, negc, bl, t)
    w_out, w_up, w_down = rest_weights(y_att)
    wo_s, wo_a = w_out[:SSD_WIDTH], w_out[SSD_WIDTH:]
    h1, h1n, rstd1 = _mm_norm_fwd(y_ssd, wo_s, y_att, wo_a, x, norm_mlp_w, name="out_proj_norm_mlp", tm=r2)
    up = _mm(h1n, w_up, name="mlp_up", tiles=(r1, D_FF, D_MODEL), out_dtype=BF16)
    dh2, dh2b, loss, d_nfw = _mm_final(up, w_down, h1, nfw, target, name="mlp_down_final_norm_loss", tm=r2,
                                       a_act="relu2")

    dup = _mm(dh2b, w_down, name="mlp_down_bwd_act", tiles=(r2, D_FF, D_MODEL), tb=True, epi_up=up, out_dtype=BF16)
    rest_shape = (N_CHIPS, REST_ROWS, D_MODEL)
    gb_rest = _mm(up, dh2b, name="mlp_down_bwd_w", tiles=(DOWN_ROWS, D_MODEL, kt), ta=True, a_act="relu2",
                  out_dtype=BF16, into=(rest_shape, (None, DOWN_ROWS, D_MODEL), lambda i, j, k: (i, 1, 0), None))
    dh1, dh1b, d_nmlp = _mm_norm_bwd([(dup, w_up)], h1, rstd1, norm_mlp_w, dh2, name="mlp_up_bwd_act_norm_mlp",
                                     tm=r2)
    gb_rest = _mm(h1n, dup, name="mlp_up_bwd_w", tiles=(D_MODEL, UP_ROWS, kt), ta=True, out_dtype=BF16,
                  into=(rest_shape, (None, D_MODEL, UP_ROWS), lambda i, j, k: (j, 0, 0), gb_rest))
    dys, do = _mm_two_halves(dh1b, w_out, name="out_proj_bwd_act", tm=r1)
    out_block = (UP_ROWS + DOWN_ROWS) // OUT_ROWS
    for half, (y_half, tag) in enumerate(((y_ssd, "ssd"), (y_att, "att"))):
        gb_rest = _mm(y_half, dh1b, name="out_proj_bwd_w_" + tag, tiles=(2 * OUT_ROWS, D_MODEL, kt), ta=True,
                      out_dtype=BF16, into=(rest_shape, (2, OUT_ROWS, D_MODEL),
                                            functools.partial(lambda i, j, k, h: (h, out_block, 0), h=half),
                                            gb_rest))
    token = jnp.zeros((8, 128), F32) if early_grads is None else early_grads(gb_rest)
    dq, dk, dv, dcb = _attn_bwd(qkv, do, y_att, lse, negc, token, bl, t)
    dc = jnp.pad(dcb[:, :, 0:2, :].transpose(0, 3, 1, 2).reshape(n, 16), ((0, 0), (16, 96)))
    dxc, dpa, ddt_raw, d_snw, d_dsk, d_alog, d_dtb = _ssd_bwd(dys, xc, proj_a, y_pre, hprev, dt, gate_d, acum,
                                                             acum_t, alog128, dskip_e, ssd_norm_w, bl, t)
    dpa, d_conv_w, d_conv_b = _conv_bwd(dxc, dsilu, proj_a, conv_w, dpa, bl, t)
    dproj_a, d_fb = _fpost(dc, gate_d, ddt_raw, dpa, bl, t)
    dqkv = jnp.concatenate([dq, dk, dv], axis=1)
    d_w_a = _mm(h0, dproj_a, name="proj_a_bwd_w", tiles=(1024, 896, kt), ta=True, out_dtype=BF16)
    d_w_qkv = _mm(h0, dqkv, name="proj_qkv_bwd_w", tiles=(1024, 1024, kt), ta=True, out_dtype=BF16)
    d_w_in = _merge_w_in(d_w_a, d_w_qkv)
    late_token = None if late_grads is None else late_grads(d_w_in)
    dx, _, d_nmix = _mm_norm_bwd([(dproj_a, w_a), (dqkv, w_qkv)], x, rstd0, norm_mix_w, dh1,
                                 name="proj_bwd_act_norm_mix", tm=min(n, 256), after=late_token)

    grads = dict(norm_mix_w=d_nmix, w_in=d_w_in, conv_w=d_conv_w, conv_b=d_conv_b,
                 dt_bias=d_dtb, a_log=d_alog, d_skip=d_dsk, ssd_norm_w=d_snw, f_bias=d_fb, rest=gb_rest,
                 norm_mlp_w=d_nmlp, norm_final_w=d_nfw)
    return dx.reshape(bl, t, d), loss, grads


SMALL_ORDER = ("norm_mix_w", "conv_w", "conv_b", "dt_bias", "a_log", "d_skip", "ssd_norm_w", "f_bias",
               "norm_mlp_w", "norm_final_w")
SMALL_SIZES = (1024, 4 * CONV_CH, CONV_CH, 16, 16, 16, 1024, 16, 1024, 1024)


def _pack_small(vals, rows):
    flat = jnp.concatenate([v.reshape(-1).astype(F32) for v in vals])
    return jnp.pad(flat, (0, rows * 128 - flat.shape[0])).reshape(rows, 128)


def _unpack_small(packed, sizes):
    flat = packed.reshape(-1)
    out, o = [], 0
    for s in sizes:
        out.append(flat[o:o + s])
        o += s
    return out


def kernel(x, norm_mix_w, w_in, conv_w, conv_b, dt_bias, a_log, d_skip, ssd_norm_w, f_bias, w_out, norm_mlp_w, w_up, w_down, norm_final_w, loss_target, m_norm_mix_w, m_w_in, m_conv_w, m_conv_b, m_dt_bias, m_a_log, m_d_skip, m_ssd_norm_w, m_f_bias, m_w_out, m_norm_mlp_w, m_w_up, m_w_down, m_norm_final_w, v_norm_mix_w, v_w_in, v_conv_w, v_conv_b, v_dt_bias, v_a_log, v_d_skip, v_ssd_norm_w, v_f_bias, v_w_out, v_norm_mlp_w, v_w_up, v_w_down, v_norm_final_w):
    chip = 2 * lax.axis_index("x") + lax.axis_index("y")
    cw = CONV_CH // N_CHIPS

    own_in = _pack_in(w_in[0])
    own_rest = _pack_rest(w_out[0], w_up[0], w_down[0])
    g_in = _gather_weights(own_in, name="gather_w_in")
    w_in_f = _full_w_in(g_in, own_in)
    *rest_handles, rest_token = _gather_start(own_rest, g_in, name="gather_start_rest")

    def rest_weights(after):
        _, landed = _gather_wait(*rest_handles, after, name="gather_wait_rest")
        return _full_rest(_gather_forward(landed, name="gather_forward_rest"), own_rest)
    small_all = _gather_small(_pack_small([conv_w[0]], 16), name="gather_conv_w")
    conv_w_f = jnp.concatenate([small_all[2 * j].reshape(-1)[:4 * cw].reshape(4, cw) for j in range(N_CHIPS)], axis=1)

    c = lax.axis_index("c")

    def chip_partial(gb, tag):
        half_rows = gb.shape[1] // 2
        from_sibling = _swap_halves(gb, name="grad_swap_halves_" + tag)
        my_half = lax.dynamic_slice_in_dim(gb, c * half_rows, half_rows, axis=1)
        return _add_pair(my_half, from_sibling, name="grad_add_sibling_" + tag)

    in_flight = {}

    def early_grads(gb_rest):
        part = chip_partial(gb_rest, "rest")
        *handles, token = _exchange_start(part, name="grad_exchange_start_rest")
        in_flight["rest"] = handles
        return token

    def late_grads(d_w_in):
        gb_in = jnp.stack([_pack_in(d_w_in[:, j * IN_SHARD:(j + 1) * IN_SHARD]) for j in range(N_CHIPS)])
        *handles, token = _exchange_start(chip_partial(gb_in, "in"), name="grad_exchange_start_in")
        in_flight["in"] = handles
        return token

    dx, loss_part, g = _local_step(x, loss_target, w_in_f, rest_weights, norm_mix_w, conv_w_f,
                                   conv_b, dt_bias, a_log, d_skip, ssd_norm_w, f_bias, norm_mlp_w, norm_final_w,
                                   first_after=rest_token, early_grads=early_grads, late_grads=late_grads)

    send_sems, recv_sems, part_rest, land_rest = in_flight["rest"]
    part_rest, parts_rest = _exchange_wait(send_sems, recv_sems, part_rest, land_rest, dx,
                                           name="grad_exchange_wait_rest")
    g_rest_half = _sum_parts(parts_rest, part_rest, name="grad_sum_chips_rest")
    g_w_out, g_w_up, g_w_down = _unpack_rest(_join_halves(g_rest_half, name="grad_join_halves_rest"))

    part_in, parts_in = _exchange_wait(*in_flight["in"], dx, name="grad_exchange_wait_in")
    g_in_half = _sum_parts(parts_in, part_in, name="grad_sum_chips_in")
    g_w_in = _join_halves(g_in_half, name="grad_join_halves_in")[:, :IN_SHARD]

    small_vals = [g[k] for k in SMALL_ORDER] + [loss_part[:, 0:1]]
    small_sum = _sum_leading(_gather_small(_pack_small(small_vals, SMALL_ROWS), name="gather_small_grads"), name="small_sum")
    sg = dict(zip(SMALL_ORDER + ("loss",), _unpack_small(small_sum, SMALL_SIZES + (1,))))
    loss = sg["loss"].reshape(())
    g_conv_full = sg["conv_w"].reshape(4, CONV_CH)
    g_conv = lax.dynamic_slice_in_dim(g_conv_full, chip * cw, cw, axis=1)

    grads = dict(norm_mix_w=sg["norm_mix_w"].reshape(1, -1), w_in=g_w_in[None], conv_w=g_conv[None],
                 conv_b=sg["conv_b"].reshape(1, -1), dt_bias=sg["dt_bias"].reshape(1, -1),
                 a_log=sg["a_log"].reshape(1, -1), d_skip=sg["d_skip"].reshape(1, -1),
                 ssd_norm_w=sg["ssd_norm_w"].reshape(1, -1), f_bias=sg["f_bias"].reshape(1, -1), w_out=g_w_out[None],
                 norm_mlp_w=sg["norm_mlp_w"].reshape(1, -1), w_up=g_w_up[None], w_down=g_w_down[None],
                 norm_final_w=sg["norm_final_w"])
    weights = dict(norm_mix_w=norm_mix_w, w_in=w_in, conv_w=conv_w, conv_b=conv_b, dt_bias=dt_bias, a_log=a_log,
                   d_skip=d_skip, ssd_norm_w=ssd_norm_w, f_bias=f_bias, w_out=w_out, norm_mlp_w=norm_mlp_w,
                   w_up=w_up, w_down=w_down, norm_final_w=norm_final_w)
    ms = dict(norm_mix_w=m_norm_mix_w, w_in=m_w_in, conv_w=m_conv_w, conv_b=m_conv_b, dt_bias=m_dt_bias,
              a_log=m_a_log, d_skip=m_d_skip, ssd_norm_w=m_ssd_norm_w, f_bias=m_f_bias, w_out=m_w_out,
              norm_mlp_w=m_norm_mlp_w, w_up=m_w_up, w_down=m_w_down, norm_final_w=m_norm_final_w)
    vs = dict(norm_mix_w=v_norm_mix_w, w_in=v_w_in, conv_w=v_conv_w, conv_b=v_conv_b, dt_bias=v_dt_bias,
              a_log=v_a_log, d_skip=v_d_skip, ssd_norm_w=v_ssd_norm_w, f_bias=v_f_bias, w_out=v_w_out,
              norm_mlp_w=v_norm_mlp_w, w_up=v_w_up, w_down=v_w_down, norm_final_w=v_norm_final_w)
    names = list(weights)
    big = ("w_in", "w_out", "w_up", "w_down")
    delta, new_m, new_v = {}, {}, {}
    for k, g2 in zip(big[1:], (g_w_out, g_w_up, g_w_down)):
        delta[k], new_m[k], new_v[k] = _adamw(weights[k], g2, ms[k], vs[k], name="adamw_" + k)
    g_in_t = g_w_in.T
    outs_t = _adamw(w_in[0].T, g_in_t, m_w_in[0].T, v_w_in[0].T, name="adamw_w_in")
    delta["w_in"], new_m["w_in"], new_v["w_in"] = [o.T[None] for o in outs_t]
    grads["w_in"] = g_in_t.T[None]
    smalls = [k for k in names if k not in big]
    sizes = [math.prod(weights[k].shape) for k in smalls]
    rows = -(-sum(sizes) // 1024) * 8
    packs = [_pack_small([d[k] for k in smalls], rows) for d in (weights, grads, ms, vs)]
    outs = _adamw(*packs, name="adamw_small")
    for o, dst in zip(outs, (delta, new_m, new_v)):
        for k, val in zip(smalls, _unpack_small(o, sizes)):
            dst[k] = val.reshape(weights[k].shape)
    return (loss, dx, *[grads[k] for k in names], *[delta[k] for k in names], *[new_m[k] for k in names],
            *[new_v[k] for k in names])
```

```python
import functools
import math

import jax
import jax.numpy as jnp
from jax import lax
from jax.experimental import pallas as pl
from jax.experimental.pallas import tpu as pltpu

F32 = jnp.float32
BF16 = jnp.bfloat16
HIGHEST = lax.Precision.HIGHEST
MESH = pl.DeviceIdType.MESH

D_MODEL = 1024
HEAD_DIM = 64
SSD_WIDTH = 1024
SSD_STATE = 128
CONV_CH = 1536
CHUNK = 128
ATT_WIDTH = 1024
EPS = 1e-5
IN_WIDTH = 5664
PA_WIDTH = 2688
QKV_WIDTH = 3072
D_FF = 4096
ATT_BLOCK = 256
NEG = -1e30
LOG2E = 1.4426950408889634
VMEM_LIMIT = 48 * 1024 * 1024

ADAM_LR = 0.001
ADAM_B1 = 0.9
ADAM_B2 = 0.999
ADAM_EPS = 1e-08
ADAM_WD = 0.01
ADAM_STEP = 10

N_CHIPS = 4
SMALL_ROWS = 96


def _cparams(sem):
    return pltpu.CompilerParams(dimension_semantics=sem, vmem_limit_bytes=VMEM_LIMIT)


def _pick(n, cands):
    for c in cands:
        if n % c == 0:
            return c
    return n


MM_CHUNK = 512


def _mm(a, b, *, name, tiles, ta=False, tb=False, out_dtype=F32, res=None, a_act=None, epi_up=None, after=None,
        into=None):
    n_unread = (after is not None) + (into is not None and into[3] is not None)
    if ta:
        K, M = a.shape
    else:
        M, K = a.shape
    if tb:
        N, K2 = b.shape
    else:
        K2, N = b.shape
    assert K == K2, (a.shape, b.shape)
    tm, tn, tk = tiles
    assert M % tm == 0 and N % tn == 0 and K % tk == 0, (name, M, N, K, tiles)
    nk = K // tk
    dn = (((0 if ta else 1,), (1 if tb else 0,)), ((), ()))
    has_res = res is not None
    has_up = epi_up is not None
    cn = _pick(tn, (MM_CHUNK, 384, 256, 128))

    def prologue(av):
        if a_act == "relu2":
            r = jnp.maximum(av.astype(F32), 0.0)
            av = r * r
        return av.astype(BF16)

    def epilogue(out, res_v, up_v):
        if has_res:
            out = out + res_v.astype(F32)
        if has_up:
            out = out * (2.0 * jnp.maximum(up_v.astype(F32), 0.0))
        return out.astype(out_dtype)

    def body(*refs):
        a_ref, b_ref = refs[0], refs[1]
        i = 2
        res_ref = up_ref = None
        if has_res:
            res_ref = refs[i]
            i += 1
        if has_up:
            up_ref = refs[i]
            i += 1
        i += n_unread
        o_ref = refs[i]
        if nk == 1:
            av = prologue(a_ref[...])
            if len(o_ref.shape) == 3:
                out = lax.dot_general(av, b_ref[...].astype(BF16), dn, preferred_element_type=F32)
                out = epilogue(out, res_ref[...] if has_res else None, up_ref[...] if has_up else None)
                o_ref[...] = out.reshape(o_ref.shape)
                return
            for c in range(tn // cn):
                cs = slice(c * cn, (c + 1) * cn)
                bv = (b_ref[cs, :] if tb else b_ref[:, cs]).astype(BF16)
                out = lax.dot_general(av, bv, dn, preferred_element_type=F32)
                o_ref[:, cs] = epilogue(out, res_ref[:, cs] if has_res else None, up_ref[:, cs] if has_up else None)
            return
        acc_ref = refs[i + 1]
        k = pl.program_id(2)

        @pl.when(k == 0)
        def _():
            acc_ref[...] = jnp.zeros_like(acc_ref)

        acc_ref[...] += lax.dot_general(prologue(a_ref[...]), b_ref[...].astype(BF16), dn,
                                        preferred_element_type=F32)

        @pl.when(k == nk - 1)
        def _():
            out = epilogue(acc_ref[...], res_ref[...] if has_res else None, up_ref[...] if has_up else None)
            o_ref[...] = out.reshape(o_ref.shape)

    a_spec = pl.BlockSpec((tk, tm), lambda i, j, k: (k, i)) if ta else pl.BlockSpec((tm, tk), lambda i, j, k: (i, k))
    b_spec = pl.BlockSpec((tn, tk), lambda i, j, k: (j, k)) if tb else pl.BlockSpec((tk, tn), lambda i, j, k: (k, j))
    o_spec = pl.BlockSpec((tm, tn), lambda i, j, k: (i, j))
    ins, specs = [a, b], [a_spec, b_spec]
    if has_res:
        ins.append(res)
        specs.append(o_spec)
    if has_up:
        ins.append(epi_up)
        specs.append(o_spec)
    if after is not None:
        ins.append(after)
        specs.append(pl.BlockSpec(memory_space=pl.ANY))
    out_shape, out_spec, aliases = jax.ShapeDtypeStruct((M, N), out_dtype), o_spec, {}
    if into is not None:
        shape, block, index, buf = into
        out_shape, out_spec = jax.ShapeDtypeStruct(shape, out_dtype), pl.BlockSpec(block, index)
        if buf is not None:
            aliases = {len(ins): 0}
            ins.append(buf)
            specs.append(pl.BlockSpec(memory_space=pl.ANY))
    return pl.pallas_call(
        body, name=name, grid=(M // tm, N // tn, nk),
        in_specs=specs, out_specs=out_spec, out_shape=out_shape, input_output_aliases=aliases,
        scratch_shapes=[] if nk == 1 else [pltpu.VMEM((tm, tn), F32)],
        compiler_params=_cparams(("parallel", "parallel", "arbitrary")),
    )(*ins)


def _rows_product(a_ref, b_ref, tb, a_act):
    av = a_ref[...]
    if a_act == "relu2":
        r = jnp.maximum(av.astype(F32), 0.0)
        av = r * r
    dn = (((1,), (1 if tb else 0,)), ((), ()))
    return lax.dot_general(av.astype(BF16), b_ref[...].astype(BF16), dn, preferred_element_type=F32)


def _norm_mm(x, w, b, after, *, name, tm):
    m, d = x.shape
    n = b.shape[1]
    cn = _pick(n, (MM_CHUNK, 384, 256, 128))

    def body(x_ref, w_ref, b_ref, after_ref, h_ref, r_ref, o_ref):
        xv = x_ref[...]
        rstd = lax.rsqrt(jnp.mean(xv * xv, axis=1, keepdims=True) + EPS)
        hv = (xv * rstd * w_ref[...]).astype(BF16)
        h_ref[...] = hv
        r_ref[...] = rstd
        for c in range(n // cn):
            cs = slice(c * cn, (c + 1) * cn)
            o_ref[:, cs] = jnp.dot(hv, b_ref[:, cs].astype(BF16), preferred_element_type=F32)

    row = pl.BlockSpec((tm, d), lambda i: (i, 0))
    return pl.pallas_call(
        body, name=name, grid=(m // tm,),
        in_specs=[row, pl.BlockSpec((1, d), lambda i: (0, 0)), pl.BlockSpec((d, n), lambda i: (0, 0)),
                  pl.BlockSpec(memory_space=pl.ANY)],
        out_specs=[row, pl.BlockSpec((tm, 1), lambda i: (i, 0)), pl.BlockSpec((tm, n), lambda i: (i, 0))],
        out_shape=[jax.ShapeDtypeStruct((m, d), BF16), jax.ShapeDtypeStruct((m, 1), F32),
                   jax.ShapeDtypeStruct((m, n), F32)],
        compiler_params=_cparams(("parallel",)),
    )(x, w, b, after)


def _mm_norm_fwd(a1, b1, a2, b2, res, w, *, name, tm):
    m, k1 = a1.shape
    k2 = a2.shape[1]
    d = b1.shape[1]

    def body(a1_ref, b1_ref, a2_ref, b2_ref, res_ref, w_ref, h_ref, y_ref, r_ref):
        hv = _rows_product(a1_ref, b1_ref, False, None) + _rows_product(a2_ref, b2_ref, False, None) + res_ref[...]
        rstd = lax.rsqrt(jnp.mean(hv * hv, axis=1, keepdims=True) + EPS)
        h_ref[...] = hv
        y_ref[...] = (hv * rstd * w_ref[...]).astype(BF16)
        r_ref[...] = rstd

    row = pl.BlockSpec((tm, d), lambda i: (i, 0))
    return pl.pallas_call(
        body, name=name, grid=(m // tm,),
        in_specs=[pl.BlockSpec((tm, k1), lambda i: (i, 0)), pl.BlockSpec((k1, d), lambda i: (0, 0)),
                  pl.BlockSpec((tm, k2), lambda i: (i, 0)), pl.BlockSpec((k2, d), lambda i: (0, 0)), row,
                  pl.BlockSpec((1, d), lambda i: (0, 0))],
        out_specs=[row, row, pl.BlockSpec((tm, 1), lambda i: (i, 0))],
        out_shape=[jax.ShapeDtypeStruct((m, d), F32), jax.ShapeDtypeStruct((m, d), BF16),
                   jax.ShapeDtypeStruct((m, 1), F32)],
        compiler_params=_cparams(("parallel",)),
    )(a1, b1, a2, b2, res, w)


def _mm_final(a, b, res, w, target, *, name, tm, a_act):
    m, k = a.shape
    d = b.shape[1]

    def body(a_ref, b_ref, res_ref, w_ref, t_ref, dh_ref, dhb_ref, loss_ref, dw_ref):
        @pl.when(pl.program_id(0) == 0)
        def _():
            loss_ref[...] = jnp.zeros_like(loss_ref)
            dw_ref[...] = jnp.zeros_like(dw_ref)

        hv = _rows_product(a_ref, b_ref, False, a_act) + res_ref[...]
        wv = w_ref[...]
        rstd = lax.rsqrt(jnp.mean(hv * hv, axis=1, keepdims=True) + EPS)
        xhat = hv * rstd
        err = xhat * wv - t_ref[...]
        loss_ref[...] += 0.5 * jnp.sum(jnp.mean(err * err, axis=1, keepdims=True), axis=0, keepdims=True)
        dy = err * (1.0 / d)
        gw = dy * wv
        dh = rstd * (gw - xhat * jnp.mean(gw * xhat, axis=1, keepdims=True))
        dh_ref[...] = dh
        dhb_ref[...] = dh.astype(BF16)
        dw_ref[...] += jnp.sum(dy * xhat, axis=0, keepdims=True)

    row = pl.BlockSpec((tm, d), lambda i: (i, 0))
    vec = pl.BlockSpec((1, d), lambda i: (0, 0))
    return pl.pallas_call(
        body, name=name, grid=(m // tm,),
        in_specs=[pl.BlockSpec((tm, k), lambda i: (i, 0)), pl.BlockSpec((k, d), lambda i: (0, 0)), row, vec, row],
        out_specs=[row, row, pl.BlockSpec((1, 128), lambda i: (0, 0)), vec],
        out_shape=[jax.ShapeDtypeStruct((m, d), F32), jax.ShapeDtypeStruct((m, d), BF16),
                   jax.ShapeDtypeStruct((1, 128), F32), jax.ShapeDtypeStruct((1, d), F32)],
        compiler_params=_cparams(("arbitrary",)),
    )(a, b, res, w, target)


def _mm_two_halves(a, b, *, name, tm):
    m, k = a.shape
    d = b.shape[0] // 2

    def body(a_ref, b_ref, lo_ref, hi_ref):
        av = a_ref[...].astype(BF16)
        lo_ref[...] = lax.dot_general(av, b_ref[0:d, :].astype(BF16), NT_DIMS, preferred_element_type=F32)
        hi_ref[...] = lax.dot_general(av, b_ref[d:2 * d, :].astype(BF16), NT_DIMS,
                                      preferred_element_type=F32).astype(BF16)

    row = pl.BlockSpec((tm, d), lambda i: (i, 0))
    return pl.pallas_call(
        body, name=name, grid=(m // tm,),
        in_specs=[pl.BlockSpec((tm, k), lambda i: (i, 0)), pl.BlockSpec((2 * d, k), lambda i: (0, 0))],
        out_specs=[row, row],
        out_shape=[jax.ShapeDtypeStruct((m, d), F32), jax.ShapeDtypeStruct((m, d), BF16)],
        compiler_params=_cparams(("parallel",)),
    )(a, b)


def _mm_norm_bwd(pairs, x, rstd, w, dres, *, name, tm, after=None):
    m = pairs[0][0].shape[0]
    d = pairs[0][1].shape[0]
    n_pairs = len(pairs)

    def body(*refs):
        i = 2 * n_pairs
        x_ref, r_ref, w_ref, d_ref = refs[i:i + 4]
        dx_ref, dxb_ref, dw_ref = refs[-3:]

        @pl.when(pl.program_id(0) == 0)
        def _():
            dw_ref[...] = jnp.zeros_like(dw_ref)

        g = _rows_product(refs[0], refs[1], True, None)
        for p in range(1, n_pairs):
            g = g + _rows_product(refs[2 * p], refs[2 * p + 1], True, None)
        r = r_ref[...]
        xhat = x_ref[...] * r
        gw = g * w_ref[...]
        dx = d_ref[...] + r * (gw - xhat * jnp.mean(gw * xhat, axis=1, keepdims=True))
        dx_ref[...] = dx
        dxb_ref[...] = dx.astype(BF16)
        dw_ref[...] += jnp.sum(g * xhat, axis=0, keepdims=True)

    row = pl.BlockSpec((tm, d), lambda i: (i, 0))
    vec = pl.BlockSpec((1, d), lambda i: (0, 0))
    ins, specs = [], []
    for a, b in pairs:
        k = a.shape[1]
        ins += [a, b]
        specs += [pl.BlockSpec((tm, k), lambda i: (i, 0)), pl.BlockSpec((d, k), lambda i: (0, 0))]
    ins += [x, rstd, w, dres]
    specs += [row, pl.BlockSpec((tm, 1), lambda i: (i, 0)), vec, row]
    if after is not None:
        ins.append(after)
        specs.append(pl.BlockSpec(memory_space=pl.ANY))
    return pl.pallas_call(
        body, name=name, grid=(m // tm,), in_specs=specs, out_specs=[row, row, vec],
        out_shape=[jax.ShapeDtypeStruct((m, d), F32), jax.ShapeDtypeStruct((m, d), BF16),
                   jax.ShapeDtypeStruct((1, d), F32)],
        compiler_params=_cparams(("arbitrary",)),
    )(*ins)


def _softplus(x):
    return jnp.maximum(x, 0.0) + jnp.log(1.0 + jnp.exp(-jnp.abs(x)))


def _prep(proj_a, bias128, alog128, bl, t):
    n = bl * t
    nch = t // CHUNK
    col0 = (SSD_WIDTH + CONV_CH) // 128

    def body(p_ref, b_ref, al_ref, dt_ref, gd_ref, ac_ref, act_ref, negc_ref):
        negc_ref[...] = jnp.zeros_like(negc_ref)
        row = lax.broadcasted_iota(jnp.int32, (CHUNK, CHUNK), 0)
        col = lax.broadcasted_iota(jnp.int32, (CHUNK, CHUNK), 1)
        tril = (row >= col).astype(F32)
        lane = lax.broadcasted_iota(jnp.int32, (1, 128), 1)
        head_lanes = lane < 16
        a_row = -jnp.exp(al_ref[...])
        carry = jnp.zeros((1, 128), F32)
        for ci in range(nch):
            rows = slice(ci * CHUNK, (ci + 1) * CHUNK)
            xv = p_ref[rows, :] + b_ref[...]
            sp = _softplus(xv)
            acum = jnp.dot(tril, a_row * sp, precision=HIGHEST, preferred_element_type=F32)
            c = jnp.dot(tril, -_softplus(-xv), precision=HIGHEST, preferred_element_type=F32) + carry
            carry = c[CHUNK - 1:CHUNK, :]
            dt_ref[rows, :] = jnp.where(head_lanes, sp, 0.0)
            gd_ref[rows, :] = jnp.where(head_lanes, jax.nn.sigmoid(xv),
                                        jnp.where(lane < 32, jax.nn.sigmoid(-xv), 0.0))
            ac_ref[rows, :] = jnp.where(head_lanes, acum, 0.0)
            act_ref[:, rows] = jnp.transpose(acum)[0:16, :]
            c_t = jnp.transpose(c)
            for hp in range(8):
                negc_ref[hp, 0:2, rows] = -c_t[16 + 2 * hp:18 + 2 * hp, :]

    o128 = pl.BlockSpec((t, 128), lambda b: (b, 0))
    v128 = pl.BlockSpec((1, 128), lambda b: (0, 0))
    w128 = jax.ShapeDtypeStruct((n, 128), F32)
    return pl.pallas_call(
        body, name="head_scalars", grid=(bl,),
        in_specs=[pl.BlockSpec((t, 128), lambda b: (b, col0)), v128, v128],
        out_specs=[o128, o128, o128, pl.BlockSpec((16, t), lambda b: (0, b)),
                   pl.BlockSpec((None, 8, 8, t), lambda b: (b, 0, 0, 0))],
        out_shape=[w128, w128, w128, jax.ShapeDtypeStruct((16, n), F32),
                   jax.ShapeDtypeStruct((bl, 8, 8, t), F32)],
        compiler_params=_cparams(("parallel",)),
    )(proj_a, bias128, alog128)


def _fpost(dc, gate_d, ddt, dpa, bl, t):
    n = bl * t
    nch = t // CHUNK
    col0 = (SSD_WIDTH + CONV_CH) // 128

    def body(dc_ref, gd_ref, ddt_ref, dpa_in, out_ref, db_ref):
        @pl.when(pl.program_id(0) == 0)
        def _():
            db_ref[...] = jnp.zeros_like(db_ref)

        row = lax.broadcasted_iota(jnp.int32, (CHUNK, CHUNK), 0)
        col = lax.broadcasted_iota(jnp.int32, (CHUNK, CHUNK), 1)
        triu = (row <= col).astype(F32)
        lane = lax.broadcasted_iota(jnp.int32, (1, 128), 1)
        gate_lanes = (lane >= 16) & (lane < 32)
        carry = jnp.zeros((1, 128), F32)
        db = jnp.zeros((1, 128), F32)
        for ci in reversed(range(nch)):
            rows = slice(ci * CHUNK, (ci + 1) * CHUNK)
            dlf = jnp.dot(triu, dc_ref[rows, :], precision=HIGHEST, preferred_element_type=F32) + carry
            carry = dlf[0:1, :]
            df = jnp.where(gate_lanes, dlf * gd_ref[rows, :], 0.0)
            out_ref[rows, :] = (ddt_ref[rows, :] + df).astype(BF16)
            db = db + jnp.sum(df, axis=0, keepdims=True)
        db_ref[...] += db[:, 16:32]

    blk = pl.BlockSpec((t, 128), lambda b: (b, 0))
    return pl.pallas_call(
        body, name="forget_gate_bwd", grid=(bl,),
        in_specs=[blk, blk, blk, ANY],
        out_specs=[pl.BlockSpec((t, 128), lambda b: (b, col0)), pl.BlockSpec((1, 16), lambda b: (0, 0))],
        out_shape=[jax.ShapeDtypeStruct(dpa.shape, dpa.dtype), jax.ShapeDtypeStruct((1, 16), F32)],
        input_output_aliases={3: 0},
        compiler_params=_cparams(("arbitrary",)),
    )(dc, gate_d, ddt, dpa)


CONV_TILE = 256
CONV_ROWS = 256


def _conv_taps(u_ref, i):
    r0 = pl.multiple_of(i * CONV_ROWS, CONV_ROWS)
    cur = u_ref[pl.ds(r0, CONV_ROWS), :]
    p0 = pl.multiple_of(jnp.maximum(r0 - 8, 0), 8)
    prev = jnp.where(i > 0, u_ref[pl.ds(p0, 8), :], 0.0)
    cat = jnp.concatenate([prev, cur], axis=0)
    return r0, [cur] + [pltpu.roll(cat, s, 0)[8:, :] for s in (1, 2, 3)]


def _conv_fwd(proj_a, conv_w, conv_b, bl, t):
    n = bl * t
    nct = CONV_CH // CONV_TILE
    c0 = SSD_WIDTH // CONV_TILE

    def body(u_ref, w_ref, b_ref, o_ref, d_ref):
        w = w_ref[...]
        bias = b_ref[...]

        def chunk(i, carry):
            r0, taps = _conv_taps(u_ref, i)
            pre = bias + w[3:4, :] * taps[0]
            for s in (1, 2, 3):
                pre = pre + w[3 - s:4 - s, :] * taps[s]
            sg = jax.nn.sigmoid(pre)
            o_ref[pl.ds(r0, CONV_ROWS), :] = pre * sg
            d_ref[pl.ds(r0, CONV_ROWS), :] = (sg * (1.0 + pre * (1.0 - sg))).astype(BF16)
            return carry

        lax.fori_loop(0, t // CONV_ROWS, chunk, 0)

    out = pl.BlockSpec((t, CONV_TILE), lambda b, c: (b, c))
    return pl.pallas_call(
        body, name="conv_silu_fwd", grid=(bl, nct),
        in_specs=[pl.BlockSpec((t, CONV_TILE), lambda b, c: (b, c0 + c)),
                  pl.BlockSpec((4, CONV_TILE), lambda b, c: (0, c)),
                  pl.BlockSpec((1, CONV_TILE), lambda b, c: (0, c))],
        out_specs=[out, out],
        out_shape=[jax.ShapeDtypeStruct((n, CONV_CH), F32), jax.ShapeDtypeStruct((n, CONV_CH), BF16)],
        compiler_params=_cparams(("parallel", "parallel")),
    )(proj_a, conv_w, conv_b)


def _conv_bwd(dxc, dsilu, proj_a, conv_w, dpa, bl, t):
    nct = CONV_CH // CONV_TILE
    c0 = SSD_WIDTH // CONV_TILE
    nrc = t // CONV_ROWS

    def body(g_ref, s_ref, u_ref, w_ref, dpa_in, du_ref, dw_ref, db_ref, dp_scr):
        @pl.when(pl.program_id(1) == 0)
        def _():
            dw_ref[...] = jnp.zeros_like(dw_ref)
            db_ref[...] = jnp.zeros_like(db_ref)

        w = w_ref[...]
        dp_scr[pl.ds(t, 8), :] = jnp.zeros((8, CONV_TILE), F32)

        def chunk1(i, carry):
            dw0, dw1, dw2, dw3, db = carry
            r0, taps = _conv_taps(u_ref, i)
            dpre = g_ref[pl.ds(r0, CONV_ROWS), :] * s_ref[pl.ds(r0, CONV_ROWS), :].astype(F32)
            dp_scr[pl.ds(r0, CONV_ROWS), :] = dpre
            dw3 = dw3 + jnp.sum(dpre * taps[0], axis=0, keepdims=True)
            dw2 = dw2 + jnp.sum(dpre * taps[1], axis=0, keepdims=True)
            dw1 = dw1 + jnp.sum(dpre * taps[2], axis=0, keepdims=True)
            dw0 = dw0 + jnp.sum(dpre * taps[3], axis=0, keepdims=True)
            db = db + jnp.sum(dpre, axis=0, keepdims=True)
            return dw0, dw1, dw2, dw3, db

        z = jnp.zeros((1, CONV_TILE), F32)
        dw0, dw1, dw2, dw3, db = lax.fori_loop(0, nrc, chunk1, (z, z, z, z, z))
        dw_ref[...] += jnp.concatenate([dw0, dw1, dw2, dw3], axis=0)
        db_ref[...] += db

        def chunk2(i, carry):
            r0 = pl.multiple_of(i * CONV_ROWS, CONV_ROWS)
            cat = dp_scr[pl.ds(r0, CONV_ROWS + 8), :]
            du = w[3:4, :] * cat[:CONV_ROWS, :]
            for s in (1, 2, 3):
                du = du + w[3 - s:4 - s, :] * pltpu.roll(cat, CONV_ROWS + 8 - s, 0)[:CONV_ROWS, :]
            du_ref[pl.ds(r0, CONV_ROWS), :] = du.astype(BF16)
            return carry

        lax.fori_loop(0, nrc, chunk2, 0)

    tile = pl.BlockSpec((t, CONV_TILE), lambda c, b: (b, c))
    return pl.pallas_call(
        body, name="conv_silu_bwd", grid=(nct, bl),
        in_specs=[tile, tile, pl.BlockSpec((t, CONV_TILE), lambda c, b: (b, c0 + c)),
                  pl.BlockSpec((4, CONV_TILE), lambda c, b: (0, c)), ANY],
        out_specs=[pl.BlockSpec((t, CONV_TILE), lambda c, b: (b, c0 + c)),
                   pl.BlockSpec((4, CONV_TILE), lambda c, b: (0, c)),
                   pl.BlockSpec((1, CONV_TILE), lambda c, b: (0, c))],
        out_shape=[jax.ShapeDtypeStruct(dpa.shape, dpa.dtype), jax.ShapeDtypeStruct((4, CONV_CH), F32),
                   jax.ShapeDtypeStruct((1, CONV_CH), F32)],
        input_output_aliases={4: 0},
        scratch_shapes=[pltpu.VMEM((t + 8, CONV_TILE), F32)],
        compiler_params=_cparams(("parallel", "arbitrary")),
    )(dxc, dsilu, proj_a, conv_w, dpa)


SSD_FWD_CHUNKS = 4
SSD_BWD_CHUNKS = 1
NT_DIMS = (((1,), (1,)), ((), ()))
TN_DIMS = (((0,), (0,)), ((), ()))


def _dot(a, b, dims=None):
    if dims is None:
        return jnp.dot(a, b, preferred_element_type=F32)
    return lax.dot_general(a, b, dims, preferred_element_type=F32)


def _head_expander():
    r = lax.broadcasted_iota(jnp.int32, (128, SSD_WIDTH), 0)
    c = lax.broadcasted_iota(jnp.int32, (128, SSD_WIDTH), 1)
    return ((c // HEAD_DIM == r % 16) & (r < 48)).astype(BF16)


def _spread(v128, expander):
    hi = v128.astype(BF16).astype(F32)
    r1 = v128 - hi
    mid = r1.astype(BF16).astype(F32)
    lo = (r1 - mid).astype(BF16).astype(F32)
    packed = (hi + pltpu.roll(mid, 16, 1) + pltpu.roll(lo, 32, 1)).astype(BF16)
    return jnp.dot(packed, expander, preferred_element_type=F32)


def _head_sums(v1024, expander):
    hi = v1024.astype(BF16)
    lo = (v1024 - hi.astype(F32)).astype(BF16)
    heads = jnp.where(lax.broadcasted_iota(jnp.int32, expander.shape, 0) < 16, expander, jnp.zeros_like(expander))
    return _dot(hi, heads, NT_DIMS) + _dot(lo, heads, NT_DIMS)


def _ssd_fwd(xc, proj_a, dt, acum, acum_t, dskip_e, norm_w, bl, t):
    n = bl * t
    nch = t // CHUNK
    L = CHUNK

    def body(xc_blk, z_blk, dt_blk, ac_blk, act_blk, dsk_ref, nw_ref, ys_blk, yp_blk, hp_blk, h_scr, y_scr, x_scr):
        @pl.when(pl.program_id(1) == 0)
        def _():
            h_scr[...] = jnp.zeros_like(h_scr)

        for sub in range(SSD_FWD_CHUNKS):
            rows = pl.ds(sub * L, L)
            chunk(xc_blk.at[rows, :], z_blk.at[rows, :], dt_blk.at[rows, :], ac_blk.at[rows, :], act_blk.at[:, rows],
                  dsk_ref, nw_ref, ys_blk.at[rows, :], yp_blk.at[rows, :], hp_blk.at[sub], h_scr, y_scr, x_scr)

    def chunk(xc_ref, z_ref, dt_ref, ac_ref, act_ref, dsk_ref, nw_ref, ys_ref, yp_ref, hp_ref, h_scr, y_scr, x_scr):
        row = lax.broadcasted_iota(jnp.int32, (L, L), 0)
        col = lax.broadcasted_iota(jnp.int32, (L, L), 1)
        causal = row >= col
        lane128 = lax.broadcasted_iota(jnp.int32, (1, L), 1)
        expander = _head_expander()
        ac_all = ac_ref[...]
        act_all = act_ref[...]
        ac_e = _spread(ac_all, expander)
        e_in = jnp.exp(ac_e)
        dec = jnp.exp(ac_e[L - 1:L, :] - ac_e)
        xs_all = xc_ref[:, 0:SSD_WIDTH]
        x_all = xs_all * _spread(dt_ref[...], expander)
        x_scr[...] = x_all.astype(BF16)
        hp_all = h_scr[...]
        hp_ref[...] = hp_all
        for g in range(2):
            gs = slice(g * 512, (g + 1) * 512)
            bg = xc_ref[:, SSD_WIDTH + g * 128:SSD_WIDTH + (g + 1) * 128].astype(BF16)
            cg = xc_ref[:, SSD_WIDTH + 256 + g * 128:SSD_WIDTH + 256 + (g + 1) * 128].astype(BF16)
            gmat = _dot(cg, bg, NT_DIMS)
            y_off = _dot(cg, hp_all[gs, :].astype(BF16), NT_DIMS) * e_in[:, gs] + dsk_ref[:, gs] * xs_all[:, gs]
            s_new = _dot((x_all[:, gs] * dec[:, gs]).astype(BF16), bg, TN_DIMS)
            for pr in range(4):
                pair = slice((g * 4 + pr) * 128, (g * 4 + pr + 1) * 128)
                x_pair = x_scr[:, pair]
                y_pair = y_off[:, pr * 128:(pr + 1) * 128]
                for j in range(2):
                    h = g * 8 + 2 * pr + j
                    sl = slice(h * HEAD_DIM, (h + 1) * HEAD_DIM)
                    r = 2 * pr + j
                    ldec = jnp.exp(jnp.where(causal, ac_all[:, h:h + 1] - act_all[h:h + 1, :], NEG))
                    x_head = jnp.where((lane128 < HEAD_DIM) == (j == 0), x_pair, jnp.zeros_like(x_pair))
                    y_pair = y_pair + _dot((gmat * ldec).astype(BF16), x_head)
                    elast = jnp.exp(ac_all[L - 1:L, h:h + 1])
                    h_scr[sl, :] = elast * hp_all[sl, :] + s_new[r * HEAD_DIM:(r + 1) * HEAD_DIM, :]
                y_scr[:, pair] = y_pair
        y = y_scr[...]
        yp_ref[...] = y
        zv = z_ref[...]
        yg = y * (zv * jax.nn.sigmoid(zv))
        for g in range(2):
            gs = slice(g * 512, (g + 1) * 512)
            grp = yg[:, gs]
            rstd = lax.rsqrt(jnp.mean(grp * grp, axis=1, keepdims=True) + EPS)
            ys_ref[:, gs] = (grp * rstd * nw_ref[:, gs]).astype(BF16)

    cps = SSD_FWD_CHUNKS
    steps = nch // cps
    rb = lambda b, c: (b * steps + c, 0)
    v1k = pl.BlockSpec((1, SSD_WIDTH), lambda b, c: (0, 0))
    return pl.pallas_call(
        body, name="ssd_fwd", grid=(bl, steps),
        in_specs=[pl.BlockSpec((cps * L, CONV_CH), rb), pl.BlockSpec((cps * L, SSD_WIDTH), rb),
                  pl.BlockSpec((cps * L, 128), rb), pl.BlockSpec((cps * L, 128), rb),
                  pl.BlockSpec((16, cps * L), lambda b, c: (0, b * steps + c)), v1k, v1k],
        out_specs=[pl.BlockSpec((cps * L, SSD_WIDTH), rb), pl.BlockSpec((cps * L, SSD_WIDTH), rb),
                   pl.BlockSpec((cps, SSD_WIDTH, SSD_STATE), lambda b, c: (b * steps + c, 0, 0))],
        out_shape=[jax.ShapeDtypeStruct((n, SSD_WIDTH), BF16), jax.ShapeDtypeStruct((n, SSD_WIDTH), F32),
                   jax.ShapeDtypeStruct((bl * nch, SSD_WIDTH, SSD_STATE), F32)],
        scratch_shapes=[pltpu.VMEM((SSD_WIDTH, SSD_STATE), F32), pltpu.VMEM((L, SSD_WIDTH), F32),
                        pltpu.VMEM((L, SSD_WIDTH), BF16)],
        compiler_params=_cparams(("parallel", "arbitrary")),
    )(xc, proj_a, dt, acum, acum_t, dskip_e, norm_w)


def _ssd_bwd(dys, xc, proj_a, ypre, hprev, dt, gate_d, acum, acum_t, alog128, dskip_e, norm_w, bl, t):
    n = bl * t
    nch = t // CHUNK
    L = CHUNK

    def body(dys_blk, xc_blk, z_blk, yp_blk, hp_blk, dt_blk, gd_blk, ac_blk, act_blk, al_ref, dsk_ref, nw_ref,
             dxc_blk, dz_blk, ddt_blk, dnw_ref, dsk16_ref, da16_ref, db16_ref,
             dh_scr, dy_scr, x_scr, dx_scr, red_scr):
        first = (pl.program_id(0) == 0) & (pl.program_id(1) == 0)

        @pl.when(first)
        def _():
            dnw_ref[...] = jnp.zeros_like(dnw_ref)
            dsk16_ref[...] = jnp.zeros_like(dsk16_ref)
            da16_ref[...] = jnp.zeros_like(da16_ref)
            db16_ref[...] = jnp.zeros_like(db16_ref)

        @pl.when(pl.program_id(1) == 0)
        def _():
            dh_scr[...] = jnp.zeros_like(dh_scr)

        for sub in reversed(range(SSD_BWD_CHUNKS)):
            rows = pl.ds(sub * L, L)
            chunk(dys_blk.at[rows, :], xc_blk.at[rows, :], z_blk.at[rows, :], yp_blk.at[rows, :], hp_blk.at[sub],
                  dt_blk.at[rows, :], gd_blk.at[rows, :], ac_blk.at[rows, :], act_blk.at[:, rows], al_ref, dsk_ref,
                  nw_ref, dxc_blk.at[rows, :], dz_blk.at[rows, :], ddt_blk.at[rows, :], dnw_ref, dsk16_ref, da16_ref,
                  db16_ref, dh_scr, dy_scr, x_scr, dx_scr, red_scr)

    def chunk(dys_ref, xc_ref, z_ref, yp_ref, hp_ref, dt_ref, gd_ref, ac_ref, act_ref, al_ref, dsk_ref, nw_ref,
              dxc_ref, dz_ref, ddt_ref, dnw_ref, dsk16_ref, da16_ref, db16_ref,
              dh_scr, dy_scr, x_scr, dx_scr, red_scr):
        y = yp_ref[...]
        zv = z_ref[...]
        sz = jax.nn.sigmoid(zv)
        gate = zv * sz
        yg = y * gate
        dout = dys_ref[...]
        nw = nw_ref[...]
        for g in range(2):
            gs = slice(g * 512, (g + 1) * 512)
            grp = yg[:, gs]
            rstd = lax.rsqrt(jnp.mean(grp * grp, axis=1, keepdims=True) + EPS)
            ghat = grp * rstd
            dnw_ref[:, gs] += jnp.sum(dout[:, gs] * ghat, axis=0, keepdims=True)
            gw = dout[:, gs] * nw[:, gs]
            dyg = rstd * (gw - ghat * jnp.mean(gw * ghat, axis=1, keepdims=True))
            dy_scr[:, gs] = dyg * gate[:, gs]
            dz_ref[:, gs] = (dyg * y[:, gs] * (sz[:, gs] * (1.0 + zv[:, gs] * (1.0 - sz[:, gs])))).astype(BF16)

        row = lax.broadcasted_iota(jnp.int32, (L, L), 0)
        col = lax.broadcasted_iota(jnp.int32, (L, L), 1)
        causal = row >= col
        lane128 = lax.broadcasted_iota(jnp.int32, (1, L), 1)
        rows128 = lax.broadcasted_iota(jnp.int32, (L, 1), 0)
        last_row = rows128 == (L - 1)
        expander = _head_expander()
        ac_all = ac_ref[...]
        act_all = act_ref[...]
        dt_all = dt_ref[...]
        dt_e = _spread(dt_all, expander)
        ac_e = _spread(ac_all, expander)
        e_in = jnp.exp(ac_e)
        dec = jnp.exp(ac_e[L - 1:L, :] - ac_e)
        xs_all = xc_ref[:, 0:SSD_WIDTH]
        x_all = xs_all * dt_e
        x_scr[...] = x_all.astype(BF16)
        dy_all = dy_scr[...]
        hp_all = hp_ref[...]
        ds_all = dh_scr[...]
        dsk_cols = jnp.sum(dy_all * xs_all, axis=0, keepdims=True)
        dac = jnp.zeros((L, L), F32)
        dac_row = jnp.zeros((L, L), F32)
        ddec_cols = []
        for g in range(2):
            gs = slice(g * 512, (g + 1) * 512)
            bsl = slice(SSD_WIDTH + g * 128, SSD_WIDTH + (g + 1) * 128)
            csl = slice(SSD_WIDTH + 256 + g * 128, SSD_WIDTH + 256 + (g + 1) * 128)
            bg = xc_ref[:, bsl].astype(BF16)
            cg = xc_ref[:, csl].astype(BF16)
            gmat = _dot(cg, bg, NT_DIMS)
            hpb = hp_all[gs, :].astype(BF16)
            dsb = ds_all[gs, :].astype(BF16)
            ch = _dot(cg, hpb, NT_DIMS)
            dye = dy_all[:, gs] * e_in[:, gs]
            dyeb = dye.astype(BF16)
            dc_acc = _dot(dyeb, hpb)
            dhp = _dot(dyeb, cg, TN_DIMS)
            dxd = _dot(bg, dsb, NT_DIMS)
            db_acc = _dot((x_all[:, gs] * dec[:, gs]).astype(BF16), dsb)
            ddec = dxd * x_all[:, gs] * dec[:, gs]
            ddec_cols.append(jnp.sum(ddec, axis=0, keepdims=True))
            dx_inter = dxd * dec[:, gs]
            red_scr[:, gs] = dye * ch - ddec
            dg_sum = jnp.zeros((L, L), F32)
            for pr in range(4):
                pair = slice((g * 4 + pr) * 128, (g * 4 + pr + 1) * 128)
                x_pair = x_scr[:, pair]
                dy_pair = dy_scr[:, pair].astype(BF16)
                dx_pair = dx_inter[:, pr * 128:(pr + 1) * 128]
                for j in range(2):
                    h = g * 8 + 2 * pr + j
                    r = 2 * pr + j
                    sl = slice(h * HEAD_DIM, (h + 1) * HEAD_DIM)
                    onehot_w = lane128 == h
                    ldec = jnp.exp(jnp.where(causal, ac_all[:, h:h + 1] - act_all[h:h + 1, :], NEG))
                    mf = gmat * ldec
                    dyb = jnp.where((lane128 < HEAD_DIM) == (j == 0), dy_pair, jnp.zeros_like(dy_pair))
                    dm = _dot(dyb, x_pair, NT_DIMS)
                    dx_pair = dx_pair + _dot(mf.astype(BF16), dyb, TN_DIMS)
                    dg_sum = dg_sum + dm * ldec
                    wmat = dm * mf
                    elast = jnp.exp(ac_all[L - 1:L, h:h + 1])
                    hp_h = hp_all[sl, :]
                    ds_h = ds_all[sl, :]
                    extra = elast * jnp.sum(jnp.sum(hp_h * ds_h, axis=1, keepdims=True), axis=0, keepdims=True)
                    dac = dac + jnp.where(onehot_w,
                                          jnp.sum(wmat, axis=1, keepdims=True) + jnp.where(last_row, extra, 0.0), 0.0)
                    dac_row = dac_row + jnp.where(rows128 == h, -jnp.sum(wmat, axis=0, keepdims=True), 0.0)
                    dh_scr[sl, :] = elast * ds_h + dhp[r * HEAD_DIM:(r + 1) * HEAD_DIM, :]
                dx_scr[:, pair] = dx_pair
            dgb = dg_sum.astype(BF16)
            dxc_ref[:, csl] = dc_acc + _dot(dgb, bg)
            dxc_ref[:, bsl] = db_acc + _dot(dgb, cg, TN_DIMS)
        dx_all = dx_scr[...]
        dxc_ref[:, 0:SSD_WIDTH] = dx_all * dt_e + dsk_ref[...] * dy_all
        red = red_scr[...]
        dac_slab = _head_sums(red, expander)
        ddec_tot = _head_sums(jnp.broadcast_to(jnp.concatenate(ddec_cols, axis=1), (8, SSD_WIDTH)), expander)
        ddt_x = _head_sums(dx_all * xs_all, expander)
        dsk16_ref[...] += _head_sums(jnp.broadcast_to(dsk_cols, (8, SSD_WIDTH)), expander)[0:1, 0:16]
        dac = dac + dac_slab + jnp.transpose(dac_row) + jnp.where(last_row, ddec_tot[0:1, :], 0.0)
        triu = (row <= col).astype(F32)
        da = jnp.dot(triu, dac, precision=HIGHEST, preferred_element_type=F32)
        a_row = -jnp.exp(al_ref[...])
        ddt = jnp.where(lane128 < 16, (ddt_x + da * a_row) * gd_ref[...], 0.0)
        ddt_ref[...] = ddt
        da16_ref[...] += (jnp.sum(da * dt_all, axis=0, keepdims=True) * a_row)[:, 0:16]
        db16_ref[...] += jnp.sum(ddt, axis=0, keepdims=True)[:, 0:16]

    cps = SSD_BWD_CHUNKS
    steps = nch // cps
    rb = lambda b, c: (b * steps + steps - 1 - c, 0)
    v1k = pl.BlockSpec((1, SSD_WIDTH), lambda b, c: (0, 0))
    v16 = pl.BlockSpec((1, 16), lambda b, c: (0, 0))
    v128 = pl.BlockSpec((1, 128), lambda b, c: (0, 0))
    wide = pl.BlockSpec((cps * L, SSD_WIDTH), rb)
    s128 = pl.BlockSpec((cps * L, 128), rb)
    return pl.pallas_call(
        body, name="ssd_bwd", grid=(bl, steps),
        in_specs=[wide, pl.BlockSpec((cps * L, CONV_CH), rb), wide, wide,
                  pl.BlockSpec((cps, SSD_WIDTH, SSD_STATE), lambda b, c: (b * steps + steps - 1 - c, 0, 0)),
                  s128, s128, s128, pl.BlockSpec((16, cps * L), lambda b, c: (0, b * steps + steps - 1 - c)),
                  v128, v1k, v1k],
        out_specs=[pl.BlockSpec((cps * L, CONV_CH), rb), wide, s128, v1k, v16, v16, v16],
        out_shape=[jax.ShapeDtypeStruct((n, CONV_CH), F32), jax.ShapeDtypeStruct((n, PA_WIDTH), BF16),
                   jax.ShapeDtypeStruct((n, 128), F32), jax.ShapeDtypeStruct((1, SSD_WIDTH), F32),
                   jax.ShapeDtypeStruct((1, 16), F32), jax.ShapeDtypeStruct((1, 16), F32),
                   jax.ShapeDtypeStruct((1, 16), F32)],
        scratch_shapes=[pltpu.VMEM((SSD_WIDTH, SSD_STATE), F32), pltpu.VMEM((L, SSD_WIDTH), F32),
                        pltpu.VMEM((L, SSD_WIDTH), BF16), pltpu.VMEM((L, SSD_WIDTH), F32),
                        pltpu.VMEM((L, SSD_WIDTH), F32)],
        compiler_params=_cparams(("arbitrary", "arbitrary")),
    )(dys, xc, proj_a, ypre, hprev, dt, gate_d, acum, acum_t, alog128, dskip_e, norm_w)


def _attn_fwd(qkv, negc, bl, t):
    n = bl * t
    tb_ = ATT_BLOCK
    nb = t // tb_
    scale2 = LOG2E / math.sqrt(HEAD_DIM)

    def body(q_ref, k_ref, v_ref, c_ref, o_ref, lse_ref, v0_scr, v1_scr, k0_scr, k1_scr):
        row = lax.broadcasted_iota(jnp.int32, (tb_, tb_), 0)
        col = lax.broadcasted_iota(jnp.int32, (tb_, tb_), 1)
        causal = row >= col
        lane = lax.broadcasted_iota(jnp.int32, (1, 128), 1)
        v_pair = v_ref[...].astype(F32)
        k_pair = k_ref[...]
        v_scrs = (v0_scr, v1_scr)
        k_scrs = (k0_scr, k1_scr)
        for j in range(2):
            v_head = v_pair if j == 0 else pltpu.roll(v_pair, HEAD_DIM, 1)
            v_scrs[j][...] = jnp.where(lane < HEAD_DIM, v_head, jnp.where(lane == HEAD_DIM, 1.0, 0.0)).astype(BF16)
            k_scrs[j][...] = jnp.where((lane < HEAD_DIM) == (j == 0), k_pair, jnp.zeros_like(k_pair))
        for qi in range(nb):
            r0, lk = qi * tb_, (qi + 1) * tb_
            for j in range(2):
                sl = slice(j * HEAD_DIM, (j + 1) * HEAD_DIM)
                s = _dot(q_ref[r0:lk, :], k_scrs[j][0:lk, :], NT_DIMS) * scale2 + c_ref[j:j + 1, 0:lk] * LOG2E
                tail = jnp.where(causal, s[:, r0:lk], NEG)
                s = tail if qi == 0 else jnp.concatenate([s[:, 0:r0], tail], axis=1)
                m = jnp.max(s, axis=1, keepdims=True)
                p = jnp.exp2(s - m)
                acc = _dot(p.astype(BF16), v_scrs[j][0:lk, :])
                l = acc[:, HEAD_DIM:HEAD_DIM + 1]
                o_ref[r0:lk, sl] = (acc[:, 0:HEAD_DIM] / l).astype(BF16)
                lse_ref[r0:lk, sl] = jnp.broadcast_to(m + jnp.log(l) * LOG2E, (tb_, HEAD_DIM))

    blk = lambda off: pl.BlockSpec((t, 128), lambda b, hp: (b, off + hp))
    return pl.pallas_call(
        body, name="fox_attn_fwd", grid=(bl, 8),
        in_specs=[blk(0), blk(8), blk(16), pl.BlockSpec((None, None, 8, t), lambda b, hp: (b, hp, 0, 0))],
        out_specs=[blk(0), blk(0)],
        out_shape=[jax.ShapeDtypeStruct((n, ATT_WIDTH), BF16), jax.ShapeDtypeStruct((n, ATT_WIDTH), F32)],
        scratch_shapes=[pltpu.VMEM((t, 128), BF16)] * 4,
        compiler_params=_cparams(("parallel", "parallel")),
    )(qkv, qkv, qkv, negc)


def _attn_bwd(qkv, do, o, lse, negc, after, bl, t):
    n = bl * t
    tb_ = ATT_BLOCK
    nb = t // tb_
    scale = 1.0 / math.sqrt(HEAD_DIM)
    scale2 = LOG2E * scale

    def body(q_ref, k_ref, v_ref, do_ref, o_ref, lse_ref, c_ref, after_ref, dq_ref, dk_ref, dv_ref, dc_ref,
             dq0_scr, delta_scr, dq1_scr, qt0_scr, qt1_scr, dot_scr, dkt0_scr, dkt1_scr, dvt_scr):
        row = lax.broadcasted_iota(jnp.int32, (tb_, tb_), 0)
        col = lax.broadcasted_iota(jnp.int32, (tb_, tb_), 1)
        causal = row >= col
        lane = lax.broadcasted_iota(jnp.int32, (1, 128), 1)
        dq_scrs = (dq0_scr, dq1_scr)
        qt_scrs = (qt0_scr, qt1_scr)
        dkt_scrs = (dkt0_scr, dkt1_scr)
        dq0_scr[...] = jnp.zeros_like(dq0_scr)
        dq1_scr[...] = jnp.zeros_like(dq1_scr)
        dc_ref[...] = jnp.zeros_like(dc_ref)
        q_t = jnp.transpose(q_ref[...].astype(F32))
        ones_row = jnp.where(lax.broadcasted_iota(jnp.int32, (8, t), 0) == 0, 1.0, 0.0)
        for j in range(2):
            qt_scrs[j][...] = jnp.concatenate(
                [q_t[j * HEAD_DIM:(j + 1) * HEAD_DIM, :], ones_row, jnp.zeros((HEAD_DIM - 8, t), F32)],
                axis=0).astype(BF16)
        dot_scr[...] = jnp.transpose(do_ref[...].astype(F32)).astype(BF16)
        prod = do_ref[...].astype(F32) * o_ref[...].astype(F32)
        for j in range(2):
            sl = slice(j * HEAD_DIM, (j + 1) * HEAD_DIM)
            delta_scr[:, sl] = jnp.broadcast_to(jnp.sum(prod[:, sl], axis=1, keepdims=True), (t, HEAD_DIM))
        for kj in range(nb):
            r0, r1 = kj * tb_, (kj + 1) * tb_
            k_blk = k_ref[r0:r1, :]
            v_blk = v_ref[r0:r1, :]
            k_pair = k_blk.astype(F32)
            for j in range(2):
                sl = slice(j * HEAD_DIM, (j + 1) * HEAD_DIM)
                one = slice(j * HEAD_DIM, j * HEAD_DIM + 1)
                own = (lane < HEAD_DIM) == (j == 0)
                k_head = k_pair if j == 0 else pltpu.roll(k_pair, HEAD_DIM, 1)
                k_ones = jnp.where(lane < HEAD_DIM, k_head, jnp.where(lane == HEAD_DIM, 1.0, 0.0)).astype(BF16)
                s = (_dot(q_ref[r0:t, :], jnp.where(own, k_blk, jnp.zeros_like(k_blk)), NT_DIMS) * scale2
                     + c_ref[j:j + 1, r0:r1] * LOG2E)
                head = jnp.where(causal, s[0:tb_, :], NEG)
                s = head if kj == nb - 1 else jnp.concatenate([head, s[tb_:, :]], axis=0)
                p = jnp.exp2(s - lse_ref[r0:t, one])
                dp = _dot(do_ref[r0:t, :], jnp.where(own, v_blk, jnp.zeros_like(v_blk)), NT_DIMS)
                ds = p * (dp - delta_scr[r0:t, one])
                dsb = ds.astype(BF16)
                dvt_scr[sl, r0:r1] = _dot(dot_scr[sl, r0:t], p.astype(BF16))
                dkt_scrs[j][:, r0:r1] = _dot(qt_scrs[j][:, r0:t], dsb)
                dq_scrs[j][r0:t, :] += _dot(dsb, k_ones)
        dv_ref[...] = jnp.transpose(dvt_scr[...]).astype(BF16)
        for j in range(2):
            sl = slice(j * HEAD_DIM, (j + 1) * HEAD_DIM)
            acc = dq_scrs[j][...]
            dkt = dkt_scrs[j][...]
            dq_ref[:, sl] = (acc[:, 0:HEAD_DIM] * scale).astype(BF16)
            dk_ref[:, sl] = (jnp.transpose(dkt)[:, 0:HEAD_DIM] * scale).astype(BF16)
            dc_ref[j:j + 1, :] = jnp.transpose(acc)[HEAD_DIM:HEAD_DIM + 1, :] - dkt[HEAD_DIM:HEAD_DIM + 1, :]

    blk = lambda off: pl.BlockSpec((t, 128), lambda b, hp: (b, off + hp))
    cblk = pl.BlockSpec((None, None, 8, t), lambda b, hp: (b, hp, 0, 0))
    return pl.pallas_call(
        body, name="fox_attn_bwd", grid=(bl, 8),
        in_specs=[blk(0), blk(8), blk(16), blk(0), blk(0), blk(0), cblk, ANY],
        out_specs=[blk(0), blk(0), blk(0), cblk],
        out_shape=[jax.ShapeDtypeStruct((n, ATT_WIDTH), BF16)] * 3 + [jax.ShapeDtypeStruct((bl, 8, 8, t), F32)],
        scratch_shapes=[pltpu.VMEM((t, 128), F32), pltpu.VMEM((t, 128), F32), pltpu.VMEM((t, 128), F32),
                        pltpu.VMEM((128, t), BF16), pltpu.VMEM((128, t), BF16), pltpu.VMEM((128, t), BF16),
                        pltpu.VMEM((128, t), F32), pltpu.VMEM((128, t), F32), pltpu.VMEM((128, t), F32)],
        compiler_params=_cparams(("parallel", "parallel")),
    )(qkv, qkv, qkv, do, o, lse, negc, after)


def _adamw(w, g, m, v, *, name):
    lead = w.ndim == 3
    r, c = w.shape[-2:]
    tr = _pick(r, (256, IN_SHARD // 3, 128, 64, 32, 16, 8))
    bc1 = 1.0 - ADAM_B1 ** ADAM_STEP
    bc2 = 1.0 - ADAM_B2 ** ADAM_STEP

    def body(w_ref, g_ref, m_ref, v_ref, d_ref, nm_ref, nv_ref):
        gv = g_ref[...]
        mn = ADAM_B1 * m_ref[...] + (1.0 - ADAM_B1) * gv
        vn = ADAM_B2 * v_ref[...] + (1.0 - ADAM_B2) * (gv * gv)
        m_hat = mn / bc1
        v_hat = vn / bc2
        d_ref[...] = -ADAM_LR * (m_hat / (jnp.sqrt(v_hat) + ADAM_EPS) + ADAM_WD * w_ref[...])
        nm_ref[...] = mn
        nv_ref[...] = vn

    flat = pl.BlockSpec((tr, c), lambda i: (i, 0))
    blk = pl.BlockSpec((None, tr, c), lambda i: (0, i, 0)) if lead else flat
    return pl.pallas_call(
        body, name=name, grid=(r // tr,), in_specs=[blk, flat, blk, blk], out_specs=[blk] * 3,
        out_shape=[jax.ShapeDtypeStruct(w.shape, F32)] * 3,
        compiler_params=_cparams(("parallel",)),
    )(w, g, m, v)


def _sum_leading(parts, *, name, out_dtype=F32):
    k, r, c = parts.shape
    tr = _pick(r, (512, 256, 128, 96, 64, 32, 16, 8))

    def body(p_ref, o_ref):
        acc = p_ref[0].astype(F32)
        for i in range(1, k):
            acc = acc + p_ref[i].astype(F32)
        o_ref[...] = acc.astype(out_dtype)

    return pl.pallas_call(
        body, name=name, grid=(r // tr,),
        in_specs=[pl.BlockSpec((k, tr, c), lambda i: (0, i, 0))],
        out_specs=pl.BlockSpec((tr, c), lambda i: (i, 0)),
        out_shape=jax.ShapeDtypeStruct((r, c), out_dtype),
        compiler_params=_cparams(("parallel",)),
    )(parts)


def _add_my_half(g, b, *, name):
    k, r, c = b.shape
    tr = _pick(r, (512, 256, 128))
    nrt = r // tr

    def body(lo_ref, hi_ref, b_ref, o_ref):
        mine = jnp.where(lax.axis_index("c") == 0, lo_ref[...], hi_ref[...])
        o_ref[...] = (mine.astype(F32) + b_ref[...].astype(F32)).astype(BF16)

    blk = pl.BlockSpec((None, tr, c), lambda j, i: (j, i, 0))
    return pl.pallas_call(
        body, name=name, grid=(k, nrt),
        in_specs=[blk, pl.BlockSpec((None, tr, c), lambda j, i: (j, i + nrt, 0)), blk], out_specs=blk,
        out_shape=jax.ShapeDtypeStruct((k, r, c), BF16),
        compiler_params=_cparams(("parallel", "parallel")),
    )(g, g, b)


ANY = pl.BlockSpec(memory_space=pl.ANY)


def _chip_peers(x, y):
    return [(1 - x, y, 2 * (1 - x) + y), (x, 1 - y, 2 * x + 1 - y), (1 - x, 1 - y, 2 * (1 - x) + 1 - y)]


def _gather_weights(blob, *, name):
    rows, cols = blob.shape
    half_rows = rows // 2

    def body(b_ref, o_ref, send_sems, recv_sems):
        x, y, c = lax.axis_index("x"), lax.axis_index("y"), lax.axis_index("c")
        me = 2 * x + y
        sibling = (x, y, 1 - c)
        peers = _chip_peers(x, y)

        def half(chip, hc):
            return o_ref.at[chip, pl.ds(hc * half_rows, half_rows), :]

        def copy(k, src, chip, hc, to):
            return pltpu.make_async_remote_copy(src_ref=src, dst_ref=half(chip, hc), send_sem=send_sems.at[k],
                                                recv_sem=recv_sems.at[k], device_id=to, device_id_type=MESH)

        my_half = b_ref.at[pl.ds(c * half_rows, half_rows), :]
        first = [copy(k, my_half, me, c, (px, py, c)) for k, (px, py, _) in enumerate(peers)]
        own = pltpu.make_async_remote_copy(src_ref=b_ref, dst_ref=o_ref.at[me], send_sem=send_sems.at[6],
                                           recv_sem=recv_sems.at[6], device_id=sibling, device_id_type=MESH)
        for cp in first + [own]:
            cp.start()
        passed = [copy(3 + k, half(pc, c), pc, c, sibling) for k, (_, _, pc) in enumerate(peers)]
        for k, (px, py, pc) in enumerate(peers):
            copy(k, my_half, pc, c, (px, py, c)).wait_recv()
            passed[k].start()
        for k, (_, _, pc) in enumerate(peers):
            copy(3 + k, half(pc, 1 - c), pc, 1 - c, sibling).wait_recv()
        own.wait_recv()
        for cp in first + passed + [own]:
            cp.wait_send()

    return pl.pallas_call(
        body, name=name, in_specs=[ANY], out_specs=ANY,
        out_shape=jax.ShapeDtypeStruct((N_CHIPS, rows, cols), BF16),
        scratch_shapes=[pltpu.SemaphoreType.DMA((7,)), pltpu.SemaphoreType.DMA((7,))],
    )(blob)


def _swap_halves(g, *, name):
    _, rows, cols = g.shape
    half_rows = rows // 2

    def body(g_ref, o_ref, send_sem, recv_sem):
        x, y, c = lax.axis_index("x"), lax.axis_index("y"), lax.axis_index("c")
        cp = pltpu.make_async_remote_copy(
            src_ref=g_ref.at[:, pl.ds((1 - c) * half_rows, half_rows), :], dst_ref=o_ref,
            send_sem=send_sem, recv_sem=recv_sem, device_id=(x, y, 1 - c), device_id_type=MESH)
        cp.start()
        cp.wait()

    return pl.pallas_call(
        body, name=name, in_specs=[ANY], out_specs=ANY,
        out_shape=jax.ShapeDtypeStruct((N_CHIPS, half_rows, cols), BF16),
        scratch_shapes=[pltpu.SemaphoreType.DMA, pltpu.SemaphoreType.DMA],
    )(g)


HBM_SPEC = pl.BlockSpec(memory_space=pltpu.HBM)
SEM_SPEC = pl.BlockSpec(memory_space=pltpu.SEMAPHORE)
SPLIT_EFFECT = pltpu.SideEffectType.DATAFLOW_SIDE_EFFECTING


def _gather_peers_copies(b_ref, land_ref, send_sems, recv_sems, sending):
    x, y, c = lax.axis_index("x"), lax.axis_index("y"), lax.axis_index("c")
    me = 2 * x + y
    half_rows = b_ref.shape[0] // 2
    src = b_ref.at[pl.ds(c * half_rows, half_rows), :]
    return [pltpu.make_async_remote_copy(
        src_ref=src, dst_ref=land_ref.at[me if sending else pc, pl.ds(c * half_rows, half_rows), :],
        send_sem=send_sems.at[k], recv_sem=recv_sems.at[k], device_id=(px, py, c), device_id_type=MESH)
        for k, (px, py, pc) in enumerate(_chip_peers(x, y))]


def _gather_start(blob, after, *, name):
    shape = (N_CHIPS,) + blob.shape

    def body(b_ref, land_ref, after_ref, send_sems, recv_sems, b_thru, land_thru, token):
        for cp in _gather_peers_copies(b_ref, land_ref, send_sems, recv_sems, True):
            cp.start()
        token[...] = jnp.zeros_like(token)

    return pl.pallas_call(
        body, name=name,
        out_shape=(pltpu.SemaphoreType.DMA((3,)), pltpu.SemaphoreType.DMA((3,)), pltpu.HBM(blob.shape, blob.dtype),
                   pltpu.HBM(shape, blob.dtype), jax.ShapeDtypeStruct((8, 128), F32)),
        in_specs=(HBM_SPEC, HBM_SPEC, ANY),
        out_specs=(SEM_SPEC, SEM_SPEC, HBM_SPEC, HBM_SPEC, pl.BlockSpec(memory_space=pltpu.VMEM)),
        input_output_aliases={0: 2, 1: 3},
        compiler_params=pltpu.CompilerParams(has_side_effects=SPLIT_EFFECT),
    )(pltpu.with_memory_space_constraint(blob, pltpu.HBM),
      pltpu.with_memory_space_constraint(lax.empty(shape, blob.dtype), pltpu.HBM), after)


def _gather_wait(send_sems, recv_sems, b_thru, land_thru, after, *, name):
    def body(b_ref, land_ref, send_sems, recv_sems, after_ref, b_dead, got_ref):
        for cp in _gather_peers_copies(b_ref, land_ref, send_sems, recv_sems, False):
            cp.wait_send()
            cp.wait_recv()

    return pl.pallas_call(
        body, name=name,
        out_shape=(pltpu.HBM(b_thru.shape, b_thru.dtype), pltpu.HBM(land_thru.shape, land_thru.dtype)),
        in_specs=(HBM_SPEC, HBM_SPEC, SEM_SPEC, SEM_SPEC, ANY), out_specs=(HBM_SPEC, HBM_SPEC),
        input_output_aliases={0: 0, 1: 1},
        compiler_params=pltpu.CompilerParams(has_side_effects=SPLIT_EFFECT),
    )(b_thru, land_thru, send_sems, recv_sems, after)


def _gather_forward(land, blob, *, name):
    half_rows = land.shape[1] // 2

    def body(l_ref, b_ref, o_ref, send_sems, recv_sems):
        x, y, c = lax.axis_index("x"), lax.axis_index("y"), lax.axis_index("c")
        me = 2 * x + y
        sibling = (x, y, 1 - c)
        cps = []
        for k, (_, _, pc) in enumerate(_chip_peers(x, y)):
            mine = pl.ds(c * half_rows, half_rows)
            cps.append(pltpu.make_async_remote_copy(
                src_ref=l_ref.at[pc, mine, :], dst_ref=o_ref.at[pc, mine, :], send_sem=send_sems.at[k],
                recv_sem=recv_sems.at[k], device_id=sibling, device_id_type=MESH))
        cps.append(pltpu.make_async_remote_copy(src_ref=b_ref, dst_ref=o_ref.at[me], send_sem=send_sems.at[3],
                                                recv_sem=recv_sems.at[3], device_id=sibling, device_id_type=MESH))
        for cp in cps:
            cp.start()
        for k, (_, _, pc) in enumerate(_chip_peers(x, y)):
            theirs = pl.ds((1 - c) * half_rows, half_rows)
            pltpu.make_async_remote_copy(
                src_ref=l_ref.at[pc, theirs, :], dst_ref=o_ref.at[pc, theirs, :], send_sem=send_sems.at[k],
                recv_sem=recv_sems.at[k], device_id=sibling, device_id_type=MESH).wait_recv()
        cps[3].wait_recv()
        for cp in cps:
            cp.wait_send()

    return pl.pallas_call(
        body, name=name, in_specs=[ANY, ANY], out_specs=ANY, input_output_aliases={0: 0},
        out_shape=jax.ShapeDtypeStruct(land.shape, land.dtype),
        scratch_shapes=[pltpu.SemaphoreType.DMA((4,)), pltpu.SemaphoreType.DMA((4,))],
    )(land, blob)


def _exchange_peers_copies(p_ref, land_ref, send_sems, recv_sems, sending):
    x, y, c = lax.axis_index("x"), lax.axis_index("y"), lax.axis_index("c")
    me = 2 * x + y
    return [pltpu.make_async_remote_copy(src_ref=p_ref.at[pc], dst_ref=land_ref.at[me if sending else pc],
                                         send_sem=send_sems.at[k], recv_sem=recv_sems.at[k],
                                         device_id=(px, py, c), device_id_type=MESH)
            for k, (px, py, pc) in enumerate(_chip_peers(x, y))]


def _exchange_start(p, *, name):
    def body(p_ref, land_ref, send_sems, recv_sems, p_thru, land_thru, token):
        for cp in _exchange_peers_copies(p_ref, land_ref, send_sems, recv_sems, True):
            cp.start()
        token[...] = jnp.zeros_like(token)

    return pl.pallas_call(
        body, name=name,
        out_shape=(pltpu.SemaphoreType.DMA((3,)), pltpu.SemaphoreType.DMA((3,)), pltpu.HBM(p.shape, p.dtype),
                   pltpu.HBM(p.shape, p.dtype), jax.ShapeDtypeStruct((8, 128), F32)),
        in_specs=(HBM_SPEC, HBM_SPEC),
        out_specs=(SEM_SPEC, SEM_SPEC, HBM_SPEC, HBM_SPEC, pl.BlockSpec(memory_space=pltpu.VMEM)),
        input_output_aliases={0: 2, 1: 3},
        compiler_params=pltpu.CompilerParams(has_side_effects=SPLIT_EFFECT),
    )(pltpu.with_memory_space_constraint(p, pltpu.HBM),
      pltpu.with_memory_space_constraint(lax.empty(p.shape, p.dtype), pltpu.HBM))


def _exchange_wait(send_sems, recv_sems, p_thru, land_thru, after, *, name):
    def body(p_ref, land_ref, send_sems, recv_sems, after_ref, p_dead, got_ref):
        for cp in _exchange_peers_copies(p_ref, land_ref, send_sems, recv_sems, False):
            cp.wait_send()
            cp.wait_recv()

    return pl.pallas_call(
        body, name=name,
        out_shape=(pltpu.HBM(p_thru.shape, p_thru.dtype), pltpu.HBM(p_thru.shape, p_thru.dtype)),
        in_specs=(HBM_SPEC, HBM_SPEC, SEM_SPEC, SEM_SPEC, ANY), out_specs=(HBM_SPEC, HBM_SPEC),
        input_output_aliases={0: 0, 1: 1},
        compiler_params=pltpu.CompilerParams(has_side_effects=SPLIT_EFFECT),
    )(p_thru, land_thru, send_sems, recv_sems, after)


def _sum_parts(parts, own, *, name):
    k, r, c = parts.shape
    tr = _pick(r, (512, 256, 128))

    def body(p_ref, own_ref, o_ref):
        me = 2 * lax.axis_index("x") + lax.axis_index("y")
        acc = jnp.zeros((tr, c), F32)
        for i in range(k):
            acc = acc + jnp.where(me == i, own_ref[i], p_ref[i]).astype(F32)
        o_ref[...] = acc

    blk = pl.BlockSpec((k, tr, c), lambda i: (0, i, 0))
    return pl.pallas_call(
        body, name=name, grid=(r // tr,), in_specs=[blk, blk],
        out_specs=pl.BlockSpec((tr, c), lambda i: (i, 0)),
        out_shape=jax.ShapeDtypeStruct((r, c), F32),
        compiler_params=_cparams(("parallel",)),
    )(parts, own)


def _join_halves(gh, *, name):
    def body(g_ref, o_ref, send_sem, recv_sem):
        x, y, c = lax.axis_index("x"), lax.axis_index("y"), lax.axis_index("c")
        cp = pltpu.make_async_remote_copy(src_ref=g_ref, dst_ref=o_ref, send_sem=send_sem, recv_sem=recv_sem,
                                          device_id=(x, y, 1 - c), device_id_type=MESH)
        cp.start()
        cp.wait()

    other = pl.pallas_call(
        body, name=name, in_specs=[ANY], out_specs=ANY,
        out_shape=jax.ShapeDtypeStruct(gh.shape, F32),
        scratch_shapes=[pltpu.SemaphoreType.DMA, pltpu.SemaphoreType.DMA],
    )(gh)
    south = lax.axis_index("c") == 0
    return jnp.concatenate([jnp.where(south, gh, other), jnp.where(south, other, gh)], axis=0)


def _gather_small(s, *, name):
    rows = s.shape[0]

    def body(s_ref, o_ref, send_sems, recv_sems, local_sem):
        x, y, c = lax.axis_index("x"), lax.axis_index("y"), lax.axis_index("c")
        me = 4 * x + 2 * y + c
        mine = pltpu.make_async_copy(s_ref, o_ref.at[me], local_sem)
        mine.start()
        peers = []
        for k in range(1, 8):
            peers.append((1 - x if k & 4 else x, 1 - y if k & 2 else y, 1 - c if k & 1 else c))
        cps = [pltpu.make_async_remote_copy(src_ref=s_ref, dst_ref=o_ref.at[me], send_sem=send_sems.at[k],
                                            recv_sem=recv_sems.at[k], device_id=p, device_id_type=MESH)
               for k, p in enumerate(peers)]
        for cp in cps:
            cp.start()
        for k, (px, py, pc) in enumerate(peers):
            pltpu.make_async_remote_copy(src_ref=s_ref, dst_ref=o_ref.at[4 * px + 2 * py + pc],
                                         send_sem=send_sems.at[k], recv_sem=recv_sems.at[k],
                                         device_id=(px, py, pc), device_id_type=MESH).wait_recv()
        for cp in cps:
            cp.wait_send()
        mine.wait()

    return pl.pallas_call(
        body, name=name, in_specs=[ANY], out_specs=ANY,
        out_shape=jax.ShapeDtypeStruct((8, rows, 128), F32),
        scratch_shapes=[pltpu.SemaphoreType.DMA((7,)), pltpu.SemaphoreType.DMA((7,)), pltpu.SemaphoreType.DMA],
    )(s)


IN_SHARD = IN_WIDTH // N_CHIPS
IN_SHARD_PAD = 1536
UP_ROWS, DOWN_ROWS, OUT_ROWS = 1024, 1024, 512
REST_ROWS = UP_ROWS + DOWN_ROWS + OUT_ROWS


def _pack_in(w_in_s):
    return jnp.pad(w_in_s, ((0, 0), (0, IN_SHARD_PAD - IN_SHARD))).astype(BF16)


def _pack_rest(w_out_s, w_up_s, w_down_s):
    return jnp.concatenate([w_up_s, w_down_s, w_out_s], axis=0).astype(BF16)


def _unpack_rest(blob):
    return (blob[UP_ROWS + DOWN_ROWS:], blob[0:UP_ROWS], blob[UP_ROWS:UP_ROWS + DOWN_ROWS])


def _full_w_in(g_in):
    return jnp.concatenate([g_in[j, :, :IN_SHARD] for j in range(N_CHIPS)], axis=1)


def _full_rest(g_rest):
    parts = [_unpack_rest(g_rest[j]) for j in range(N_CHIPS)]
    w_out = jnp.concatenate([p[0] for p in parts], axis=0)
    w_up = jnp.concatenate([p[1] for p in parts], axis=1)
    w_down = jnp.concatenate([p[2] for p in parts], axis=0)
    return w_out, w_up, w_down


def _split_w_in(w_in):
    z_xbc = w_in[:, 0:2560]
    dt = w_in[:, 2560:2576]
    qkv = w_in[:, 2576:5648]
    f = w_in[:, 5648:5664]
    pad = jnp.zeros((w_in.shape[0], PA_WIDTH - 2592), w_in.dtype)
    return jnp.concatenate([z_xbc, dt, f, pad], axis=1), qkv


def _merge_w_in(d_a, d_qkv):
    return jnp.concatenate([d_a[:, 0:2560], d_a[:, 2560:2576], d_qkv, d_a[:, 2576:2592]], axis=1)


def _local_step(x3, target3, w_in, rest_weights, norm_mix_w, conv_w, conv_b, dt_bias, a_log, d_skip,
                ssd_norm_w, f_bias, norm_mlp_w, norm_final_w, first_after=None, early_grads=None, late_grads=None):
    bl, t, d = x3.shape
    n = bl * t
    x = x3.reshape(n, d)
    target = target3.reshape(n, d)
    w_a, w_qkv = _split_w_in(w_in)
    nfw = norm_final_w.reshape(1, d)
    dskip_e = jnp.repeat(d_skip, HEAD_DIM, axis=1)
    nb = t // ATT_BLOCK

    r1, r2, kt = min(n, 1024), min(n, 512), min(n, 2048)
    if first_after is None:
        first_after = jnp.zeros((8, 128), F32)
    h0, rstd0, proj_a = _norm_mm(x, norm_mix_w, w_a, first_after, name="norm_mix_proj_a", tm=r2)
    qkv = _mm(h0, w_qkv, name="proj_qkv", tiles=(r2, QKV_WIDTH, D_MODEL), out_dtype=BF16)
    bias128 = jnp.concatenate([dt_bias, f_bias, jnp.zeros((1, 96), F32)], axis=1)
    alog128 = jnp.concatenate([a_log, jnp.zeros((1, 112), F32)], axis=1)
    dt, gate_d, acum, acum_t, negc = _prep(proj_a, bias128, alog128, bl, t)
    xc, dsilu = _conv_fwd(proj_a, conv_w, conv_b, bl, t)
    y_ssd, y_pre, hprev = _ssd_fwd(xc, proj_a, dt, acum, acum_t, dskip_e, ssd_norm_w, bl, t)
    y_att, lse = _attn_fwd(qkv, negc, bl, t)
    w_out, w_up, w_down = rest_weights(y_att)
    wo_s, wo_a = w_out[:SSD_WIDTH], w_out[SSD_WIDTH:]
    h1, h1n, rstd1 = _mm_norm_fwd(y_ssd, wo_s, y_att, wo_a, x, norm_mlp_w, name="out_proj_norm_mlp", tm=r2)
    up = _mm(h1n, w_up, name="mlp_up", tiles=(r1, D_FF, D_MODEL), out_dtype=BF16)
    dh2, dh2b, loss, d_nfw = _mm_final(up, w_down, h1, nfw, target, name="mlp_down_final_norm_loss", tm=r2,
                                       a_act="relu2")

    dup = _mm(dh2b, w_down, name="mlp_down_bwd_act", tiles=(r2, D_FF, D_MODEL), tb=True, epi_up=up, out_dtype=BF16)
    rest_shape = (N_CHIPS, REST_ROWS, D_MODEL)
    gb_rest = _mm(up, dh2b, name="mlp_down_bwd_w", tiles=(DOWN_ROWS, D_MODEL, kt), ta=True, a_act="relu2",
                  out_dtype=BF16, into=(rest_shape, (None, DOWN_ROWS, D_MODEL), lambda i, j, k: (i, 1, 0), None))
    dh1, dh1b, d_nmlp = _mm_norm_bwd([(dup, w_up)], h1, rstd1, norm_mlp_w, dh2, name="mlp_up_bwd_act_norm_mlp",
                                     tm=r2)
    gb_rest = _mm(h1n, dup, name="mlp_up_bwd_w", tiles=(D_MODEL, UP_ROWS, kt), ta=True, out_dtype=BF16,
                  into=(rest_shape, (None, D_MODEL, UP_ROWS), lambda i, j, k: (j, 0, 0), gb_rest))
    dys, do = _mm_two_halves(dh1b, w_out, name="out_proj_bwd_act", tm=r1)
    out_block = (UP_ROWS + DOWN_ROWS) // OUT_ROWS
    for half, (y_half, tag) in enumerate(((y_ssd, "ssd"), (y_att, "att"))):
        gb_rest = _mm(y_half, dh1b, name="out_proj_bwd_w_" + tag, tiles=(2 * OUT_ROWS, D_MODEL, kt), ta=True,
                      out_dtype=BF16, into=(rest_shape, (2, OUT_ROWS, D_MODEL),
                                            functools.partial(lambda i, j, k, h: (h, out_block, 0), h=half),
                                            gb_rest))
    token = jnp.zeros((8, 128), F32) if early_grads is None else early_grads(gb_rest)
    dq, dk, dv, dcb = _attn_bwd(qkv, do, y_att, lse, negc, token, bl, t)
    dc = jnp.pad(dcb[:, :, 0:2, :].transpose(0, 3, 1, 2).reshape(n, 16), ((0, 0), (16, 96)))
    dxc, dpa, ddt_raw, d_snw, d_dsk, d_alog, d_dtb = _ssd_bwd(dys, xc, proj_a, y_pre, hprev, dt, gate_d, acum,
                                                             acum_t, alog128, dskip_e, ssd_norm_w, bl, t)
    dpa, d_conv_w, d_conv_b = _conv_bwd(dxc, dsilu, proj_a, conv_w, dpa, bl, t)
    dproj_a, d_fb = _fpost(dc, gate_d, ddt_raw, dpa, bl, t)
    dqkv = jnp.concatenate([dq, dk, dv], axis=1)
    d_w_a = _mm(h0, dproj_a, name="proj_a_bwd_w", tiles=(1024, 896, kt), ta=True, out_dtype=BF16)
    d_w_qkv = _mm(h0, dqkv, name="proj_qkv_bwd_w", tiles=(1024, 1024, kt), ta=True, out_dtype=BF16)
    d_w_in = _merge_w_in(d_w_a, d_w_qkv)
    late_token = None if late_grads is None else late_grads(d_w_in)
    dx, _, d_nmix = _mm_norm_bwd([(dproj_a, w_a), (dqkv, w_qkv)], x, rstd0, norm_mix_w, dh1,
                                 name="proj_bwd_act_norm_mix", tm=min(n, 256), after=late_token)

    grads = dict(norm_mix_w=d_nmix, w_in=d_w_in, conv_w=d_conv_w, conv_b=d_conv_b,
                 dt_bias=d_dtb, a_log=d_alog, d_skip=d_dsk, ssd_norm_w=d_snw, f_bias=d_fb, rest=gb_rest,
                 norm_mlp_w=d_nmlp, norm_final_w=d_nfw)
    return dx.reshape(bl, t, d), loss, grads


SMALL_ORDER = ("norm_mix_w", "conv_w", "conv_b", "dt_bias", "a_log", "d_skip", "ssd_norm_w", "f_bias",
               "norm_mlp_w", "norm_final_w")
SMALL_SIZES = (1024, 4 * CONV_CH, CONV_CH, 16, 16, 16, 1024, 16, 1024, 1024)


def _pack_small(vals, rows):
    flat = jnp.concatenate([v.reshape(-1).astype(F32) for v in vals])
    return jnp.pad(flat, (0, rows * 128 - flat.shape[0])).reshape(rows, 128)


def _unpack_small(packed, sizes):
    flat = packed.reshape(-1)
    out, o = [], 0
    for s in sizes:
        out.append(flat[o:o + s])
        o += s
    return out


def kernel(x, norm_mix_w, w_in, conv_w, conv_b, dt_bias, a_log, d_skip, ssd_norm_w, f_bias, w_out, norm_mlp_w, w_up, w_down, norm_final_w, loss_target, m_norm_mix_w, m_w_in, m_conv_w, m_conv_b, m_dt_bias, m_a_log, m_d_skip, m_ssd_norm_w, m_f_bias, m_w_out, m_norm_mlp_w, m_w_up, m_w_down, m_norm_final_w, v_norm_mix_w, v_w_in, v_conv_w, v_conv_b, v_dt_bias, v_a_log, v_d_skip, v_ssd_norm_w, v_f_bias, v_w_out, v_norm_mlp_w, v_w_up, v_w_down, v_norm_final_w):
    chip = 2 * lax.axis_index("x") + lax.axis_index("y")
    cw = CONV_CH // N_CHIPS

    own_in = _pack_in(w_in[0])
    own_rest = _pack_rest(w_out[0], w_up[0], w_down[0])
    g_in = _gather_weights(own_in, name="gather_w_in")
    w_in_f = _full_w_in(g_in)
    *rest_handles, rest_token = _gather_start(own_rest, g_in, name="gather_start_rest")

    def rest_weights(after):
        _, landed = _gather_wait(*rest_handles, after, name="gather_wait_rest")
        return _full_rest(_gather_forward(landed, own_rest, name="gather_forward_rest"))
    small_all = _gather_small(_pack_small([conv_w[0]], 16), name="gather_conv_w")
    conv_w_f = jnp.concatenate([small_all[2 * j].reshape(-1)[:4 * cw].reshape(4, cw) for j in range(N_CHIPS)], axis=1)

    def chip_partial(gb, tag):
        from_sibling = _swap_halves(gb, name="grad_swap_halves_" + tag)
        return _add_my_half(gb, from_sibling, name="grad_add_sibling_" + tag)

    in_flight = {}

    def early_grads(gb_rest):
        part = chip_partial(gb_rest, "rest")
        *handles, token = _exchange_start(part, name="grad_exchange_start_rest")
        in_flight["rest"] = handles
        return token

    def late_grads(d_w_in):
        gb_in = jnp.stack([_pack_in(d_w_in[:, j * IN_SHARD:(j + 1) * IN_SHARD]) for j in range(N_CHIPS)])
        *handles, token = _exchange_start(chip_partial(gb_in, "in"), name="grad_exchange_start_in")
        in_flight["in"] = handles
        return token

    dx, loss_part, g = _local_step(x, loss_target, w_in_f, rest_weights, norm_mix_w, conv_w_f,
                                   conv_b, dt_bias, a_log, d_skip, ssd_norm_w, f_bias, norm_mlp_w, norm_final_w,
                                   first_after=rest_token, early_grads=early_grads, late_grads=late_grads)

    send_sems, recv_sems, part_rest, land_rest = in_flight["rest"]
    part_rest, parts_rest = _exchange_wait(send_sems, recv_sems, part_rest, land_rest, dx,
                                           name="grad_exchange_wait_rest")
    g_rest_half = _sum_parts(parts_rest, part_rest, name="grad_sum_chips_rest")
    g_w_out, g_w_up, g_w_down = _unpack_rest(_join_halves(g_rest_half, name="grad_join_halves_rest"))

    part_in, parts_in = _exchange_wait(*in_flight["in"], dx, name="grad_exchange_wait_in")
    g_in_half = _sum_parts(parts_in, part_in, name="grad_sum_chips_in")
    g_w_in = _join_halves(g_in_half, name="grad_join_halves_in")[:, :IN_SHARD]

    small_vals = [g[k] for k in SMALL_ORDER] + [loss_part[:, 0:1]]
    small_sum = _sum_leading(_gather_small(_pack_small(small_vals, SMALL_ROWS), name="gather_small_grads"), name="small_sum")
    sg = dict(zip(SMALL_ORDER + ("loss",), _unpack_small(small_sum, SMALL_SIZES + (1,))))
    loss = sg["loss"].reshape(())
    g_conv_full = sg["conv_w"].reshape(4, CONV_CH)
    g_conv = lax.dynamic_slice_in_dim(g_conv_full, chip * cw, cw, axis=1)

    grads = dict(norm_mix_w=sg["norm_mix_w"].reshape(1, -1), w_in=g_w_in[None], conv_w=g_conv[None],
                 conv_b=sg["conv_b"].reshape(1, -1), dt_bias=sg["dt_bias"].reshape(1, -1),
                 a_log=sg["a_log"].reshape(1, -1), d_skip=sg["d_skip"].reshape(1, -1),
                 ssd_norm_w=sg["ssd_norm_w"].reshape(1, -1), f_bias=sg["f_bias"].reshape(1, -1), w_out=g_w_out[None],
                 norm_mlp_w=sg["norm_mlp_w"].reshape(1, -1), w_up=g_w_up[None], w_down=g_w_down[None],
                 norm_final_w=sg["norm_final_w"])
    weights = dict(norm_mix_w=norm_mix_w, w_in=w_in, conv_w=conv_w, conv_b=conv_b, dt_bias=dt_bias, a_log=a_log,
                   d_skip=d_skip, ssd_norm_w=ssd_norm_w, f_bias=f_bias, w_out=w_out, norm_mlp_w=norm_mlp_w,
                   w_up=w_up, w_down=w_down, norm_final_w=norm_final_w)
    ms = dict(norm_mix_w=m_norm_mix_w, w_in=m_w_in, conv_w=m_conv_w, conv_b=m_conv_b, dt_bias=m_dt_bias,
              a_log=m_a_log, d_skip=m_d_skip, ssd_norm_w=m_ssd_norm_w, f_bias=m_f_bias, w_out=m_w_out,
              norm_mlp_w=m_norm_mlp_w, w_up=m_w_up, w_down=m_w_down, norm_final_w=m_norm_final_w)
    vs = dict(norm_mix_w=v_norm_mix_w, w_in=v_w_in, conv_w=v_conv_w, conv_b=v_conv_b, dt_bias=v_dt_bias,
              a_log=v_a_log, d_skip=v_d_skip, ssd_norm_w=v_ssd_norm_w, f_bias=v_f_bias, w_out=v_w_out,
              norm_mlp_w=v_norm_mlp_w, w_up=v_w_up, w_down=v_w_down, norm_final_w=v_norm_final_w)
    names = list(weights)
    big = ("w_in", "w_out", "w_up", "w_down")
    delta, new_m, new_v = {}, {}, {}
    for k, g2 in zip(big[1:], (g_w_out, g_w_up, g_w_down)):
        delta[k], new_m[k], new_v[k] = _adamw(weights[k], g2, ms[k], vs[k], name="adamw_" + k)
    g_in_t = g_w_in.T
    outs_t = _adamw(w_in[0].T, g_in_t, m_w_in[0].T, v_w_in[0].T, name="adamw_w_in")
    delta["w_in"], new_m["w_in"], new_v["w_in"] = [o.T[None] for o in outs_t]
    grads["w_in"] = g_in_t.T[None]
    smalls = [k for k in names if k not in big]
    sizes = [math.prod(weights[k].shape) for k in smalls]
    rows = -(-sum(sizes) // 1024) * 8
    packs = [_pack_small([d[k] for k in smalls], rows) for d in (weights, grads, ms, vs)]
    outs = _adamw(*packs, name="adamw_small")
    for o, dst in zip(outs, (delta, new_m, new_v)):
        for k, val in zip(smalls, _unpack_small(o, sizes)):
            dst[k] = val.reshape(weights[k].shape)
    return (loss, dx, *[grads[k] for k in names], *[delta[k] for k in names], *[new_m[k] for k in names],
            *[new_v[k] for k in names])
```

```python
import functools
import math

import jax
import jax.numpy as jnp
from jax import lax
from jax.experimental import pallas as pl
from jax.experimental.pallas import tpu as pltpu

F32 = jnp.float32
BF16 = jnp.bfloat16
HIGHEST = lax.Precision.HIGHEST
MESH = pl.DeviceIdType.MESH

D_MODEL = 1024
HEAD_DIM = 64
SSD_WIDTH = 1024
SSD_STATE = 128
CONV_CH = 1536
CHUNK = 128
ATT_WIDTH = 1024
EPS = 1e-5
IN_WIDTH = 5664
PA_WIDTH = 2688
QKV_WIDTH = 3072
D_FF = 4096
ATT_BLOCK = 256
NEG = -1e30
LOG2E = 1.4426950408889634
VMEM_LIMIT = 48 * 1024 * 1024

ADAM_LR = 0.001
ADAM_B1 = 0.9
ADAM_B2 = 0.999
ADAM_EPS = 1e-08
ADAM_WD = 0.01
ADAM_STEP = 10

N_CHIPS = 4
SMALL_ROWS = 96


def _cparams(sem):
    return pltpu.CompilerParams(dimension_semantics=sem, vmem_limit_bytes=VMEM_LIMIT)


def _pick(n, cands):
    for c in cands:
        if n % c == 0:
            return c
    return n


MM_CHUNK = 512


def _mm(a, b, *, name, tiles, ta=False, tb=False, out_dtype=F32, res=None, a_act=None, epi_up=None, after=None,
        into=None):
    n_unread = (after is not None) + (into is not None and into[3] is not None)
    if ta:
        K, M = a.shape
    else:
        M, K = a.shape
    if tb:
        N, K2 = b.shape
    else:
        K2, N = b.shape
    assert K == K2, (a.shape, b.shape)
    tm, tn, tk = tiles
    assert M % tm == 0 and N % tn == 0 and K % tk == 0, (name, M, N, K, tiles)
    nk = K // tk
    dn = (((0 if ta else 1,), (1 if tb else 0,)), ((), ()))
    has_res = res is not None
    has_up = epi_up is not None
    cn = _pick(tn, (MM_CHUNK, 384, 256, 128))

    def prologue(av):
        if a_act == "relu2":
            r = jnp.maximum(av.astype(F32), 0.0)
            av = r * r
        return av.astype(BF16)

    def epilogue(out, res_v, up_v):
        if has_res:
            out = out + res_v.astype(F32)
        if has_up:
            out = out * (2.0 * jnp.maximum(up_v.astype(F32), 0.0))
        return out.astype(out_dtype)

    def body(*refs):
        a_ref, b_ref = refs[0], refs[1]
        i = 2
        res_ref = up_ref = None
        if has_res:
            res_ref = refs[i]
            i += 1
        if has_up:
            up_ref = refs[i]
            i += 1
        i += n_unread
        o_ref = refs[i]
        if nk == 1:
            av = prologue(a_ref[...])
            if len(o_ref.shape) == 3:
                out = lax.dot_general(av, b_ref[...].astype(BF16), dn, preferred_element_type=F32)
                out = epilogue(out, res_ref[...] if has_res else None, up_ref[...] if has_up else None)
                o_ref[...] = out.reshape(o_ref.shape)
                return
            for c in range(tn // cn):
                cs = slice(c * cn, (c + 1) * cn)
                bv = (b_ref[cs, :] if tb else b_ref[:, cs]).astype(BF16)
                out = lax.dot_general(av, bv, dn, preferred_element_type=F32)
                o_ref[:, cs] = epilogue(out, res_ref[:, cs] if has_res else None, up_ref[:, cs] if has_up else None)
            return
        acc_ref = refs[i + 1]
        k = pl.program_id(2)

        @pl.when(k == 0)
        def _():
            acc_ref[...] = jnp.zeros_like(acc_ref)

        acc_ref[...] += lax.dot_general(prologue(a_ref[...]), b_ref[...].astype(BF16), dn,
                                        preferred_element_type=F32)

        @pl.when(k == nk - 1)
        def _():
            out = epilogue(acc_ref[...], res_ref[...] if has_res else None, up_ref[...] if has_up else None)
            o_ref[...] = out.reshape(o_ref.shape)

    a_spec = pl.BlockSpec((tk, tm), lambda i, j, k: (k, i)) if ta else pl.BlockSpec((tm, tk), lambda i, j, k: (i, k))
    b_spec = pl.BlockSpec((tn, tk), lambda i, j, k: (j, k)) if tb else pl.BlockSpec((tk, tn), lambda i, j, k: (k, j))
    o_spec = pl.BlockSpec((tm, tn), lambda i, j, k: (i, j))
    ins, specs = [a, b], [a_spec, b_spec]
    if has_res:
        ins.append(res)
        specs.append(o_spec)
    if has_up:
        ins.append(epi_up)
        specs.append(o_spec)
    if after is not None:
        ins.append(after)
        specs.append(pl.BlockSpec(memory_space=pl.ANY))
    out_shape, out_spec, aliases = jax.ShapeDtypeStruct((M, N), out_dtype), o_spec, {}
    if into is not None:
        shape, block, index, buf = into
        out_shape, out_spec = jax.ShapeDtypeStruct(shape, out_dtype), pl.BlockSpec(block, index)
        if buf is not None:
            aliases = {len(ins): 0}
            ins.append(buf)
            specs.append(pl.BlockSpec(memory_space=pl.ANY))
    return pl.pallas_call(
        body, name=name, grid=(M // tm, N // tn, nk),
        in_specs=specs, out_specs=out_spec, out_shape=out_shape, input_output_aliases=aliases,
        scratch_shapes=[] if nk == 1 else [pltpu.VMEM((tm, tn), F32)],
        compiler_params=_cparams(("parallel", "parallel", "arbitrary")),
    )(*ins)


def _rows_product(a_ref, b_ref, tb, a_act):
    av = a_ref[...]
    if a_act == "relu2":
        r = jnp.maximum(av.astype(F32), 0.0)
        av = r * r
    dn = (((1,), (1 if tb else 0,)), ((), ()))
    return lax.dot_general(av.astype(BF16), b_ref[...].astype(BF16), dn, preferred_element_type=F32)


def _norm_mm(x, w, b, after, *, name, tm):
    m, d = x.shape
    n = b.shape[1]
    cn = _pick(n, (MM_CHUNK, 384, 256, 128))

    def body(x_ref, w_ref, b_ref, after_ref, h_ref, r_ref, o_ref):
        xv = x_ref[...]
        rstd = lax.rsqrt(jnp.mean(xv * xv, axis=1, keepdims=True) + EPS)
        hv = (xv * rstd * w_ref[...]).astype(BF16)
        h_ref[...] = hv
        r_ref[...] = rstd
        for c in range(n // cn):
            cs = slice(c * cn, (c + 1) * cn)
            o_ref[:, cs] = jnp.dot(hv, b_ref[:, cs].astype(BF16), preferred_element_type=F32)

    row = pl.BlockSpec((tm, d), lambda i: (i, 0))
    return pl.pallas_call(
        body, name=name, grid=(m // tm,),
        in_specs=[row, pl.BlockSpec((1, d), lambda i: (0, 0)), pl.BlockSpec((d, n), lambda i: (0, 0)),
                  pl.BlockSpec(memory_space=pl.ANY)],
        out_specs=[row, pl.BlockSpec((tm, 1), lambda i: (i, 0)), pl.BlockSpec((tm, n), lambda i: (i, 0))],
        out_shape=[jax.ShapeDtypeStruct((m, d), BF16), jax.ShapeDtypeStruct((m, 1), F32),
                   jax.ShapeDtypeStruct((m, n), F32)],
        compiler_params=_cparams(("parallel",)),
    )(x, w, b, after)


def _mm_norm_fwd(a1, b1, a2, b2, res, w, *, name, tm):
    m, k1 = a1.shape
    k2 = a2.shape[1]
    d = b1.shape[1]

    def body(a1_ref, b1_ref, a2_ref, b2_ref, res_ref, w_ref, h_ref, y_ref, r_ref):
        hv = _rows_product(a1_ref, b1_ref, False, None) + _rows_product(a2_ref, b2_ref, False, None) + res_ref[...]
        rstd = lax.rsqrt(jnp.mean(hv * hv, axis=1, keepdims=True) + EPS)
        h_ref[...] = hv
        y_ref[...] = (hv * rstd * w_ref[...]).astype(BF16)
        r_ref[...] = rstd

    row = pl.BlockSpec((tm, d), lambda i: (i, 0))
    return pl.pallas_call(
        body, name=name, grid=(m // tm,),
        in_specs=[pl.BlockSpec((tm, k1), lambda i: (i, 0)), pl.BlockSpec((k1, d), lambda i: (0, 0)),
                  pl.BlockSpec((tm, k2), lambda i: (i, 0)), pl.BlockSpec((k2, d), lambda i: (0, 0)), row,
                  pl.BlockSpec((1, d), lambda i: (0, 0))],
        out_specs=[row, row, pl.BlockSpec((tm, 1), lambda i: (i, 0))],
        out_shape=[jax.ShapeDtypeStruct((m, d), F32), jax.ShapeDtypeStruct((m, d), BF16),
                   jax.ShapeDtypeStruct((m, 1), F32)],
        compiler_params=_cparams(("parallel",)),
    )(a1, b1, a2, b2, res, w)


def _mm_final(a, b, res, w, target, *, name, tm, a_act):
    m, k = a.shape
    d = b.shape[1]

    def body(a_ref, b_ref, res_ref, w_ref, t_ref, dh_ref, dhb_ref, loss_ref, dw_ref):
        @pl.when(pl.program_id(0) == 0)
        def _():
            loss_ref[...] = jnp.zeros_like(loss_ref)
            dw_ref[...] = jnp.zeros_like(dw_ref)

        hv = _rows_product(a_ref, b_ref, False, a_act) + res_ref[...]
        wv = w_ref[...]
        rstd = lax.rsqrt(jnp.mean(hv * hv, axis=1, keepdims=True) + EPS)
        xhat = hv * rstd
        err = xhat * wv - t_ref[...]
        loss_ref[...] += 0.5 * jnp.sum(jnp.mean(err * err, axis=1, keepdims=True), axis=0, keepdims=True)
        dy = err * (1.0 / d)
        gw = dy * wv
        dh = rstd * (gw - xhat * jnp.mean(gw * xhat, axis=1, keepdims=True))
        dh_ref[...] = dh
        dhb_ref[...] = dh.astype(BF16)
        dw_ref[...] += jnp.sum(dy * xhat, axis=0, keepdims=True)

    row = pl.BlockSpec((tm, d), lambda i: (i, 0))
    vec = pl.BlockSpec((1, d), lambda i: (0, 0))
    return pl.pallas_call(
        body, name=name, grid=(m // tm,),
        in_specs=[pl.BlockSpec((tm, k), lambda i: (i, 0)), pl.BlockSpec((k, d), lambda i: (0, 0)), row, vec, row],
        out_specs=[row, row, pl.BlockSpec((1, 128), lambda i: (0, 0)), vec],
        out_shape=[jax.ShapeDtypeStruct((m, d), F32), jax.ShapeDtypeStruct((m, d), BF16),
                   jax.ShapeDtypeStruct((1, 128), F32), jax.ShapeDtypeStruct((1, d), F32)],
        compiler_params=_cparams(("arbitrary",)),
    )(a, b, res, w, target)


def _mm_two_halves(a, b, *, name, tm):
    m, k = a.shape
    d = b.shape[0] // 2

    def body(a_ref, b_ref, lo_ref, hi_ref):
        av = a_ref[...].astype(BF16)
        lo_ref[...] = lax.dot_general(av, b_ref[0:d, :].astype(BF16), NT_DIMS, preferred_element_type=F32)
        hi_ref[...] = lax.dot_general(av, b_ref[d:2 * d, :].astype(BF16), NT_DIMS,
                                      preferred_element_type=F32).astype(BF16)

    row = pl.BlockSpec((tm, d), lambda i: (i, 0))
    return pl.pallas_call(
        body, name=name, grid=(m // tm,),
        in_specs=[pl.BlockSpec((tm, k), lambda i: (i, 0)), pl.BlockSpec((2 * d, k), lambda i: (0, 0))],
        out_specs=[row, row],
        out_shape=[jax.ShapeDtypeStruct((m, d), F32), jax.ShapeDtypeStruct((m, d), BF16)],
        compiler_params=_cparams(("parallel",)),
    )(a, b)


def _mm_norm_bwd(pairs, x, rstd, w, dres, *, name, tm, after=None):
    m = pairs[0][0].shape[0]
    d = pairs[0][1].shape[0]
    n_pairs = len(pairs)

    def body(*refs):
        i = 2 * n_pairs
        x_ref, r_ref, w_ref, d_ref = refs[i:i + 4]
        dx_ref, dxb_ref, dw_ref = refs[-3:]

        @pl.when(pl.program_id(0) == 0)
        def _():
            dw_ref[...] = jnp.zeros_like(dw_ref)

        g = _rows_product(refs[0], refs[1], True, None)
        for p in range(1, n_pairs):
            g = g + _rows_product(refs[2 * p], refs[2 * p + 1], True, None)
        r = r_ref[...]
        xhat = x_ref[...] * r
        gw = g * w_ref[...]
        dx = d_ref[...] + r * (gw - xhat * jnp.mean(gw * xhat, axis=1, keepdims=True))
        dx_ref[...] = dx
        dxb_ref[...] = dx.astype(BF16)
        dw_ref[...] += jnp.sum(g * xhat, axis=0, keepdims=True)

    row = pl.BlockSpec((tm, d), lambda i: (i, 0))
    vec = pl.BlockSpec((1, d), lambda i: (0, 0))
    ins, specs = [], []
    for a, b in pairs:
        k = a.shape[1]
        ins += [a, b]
        specs += [pl.BlockSpec((tm, k), lambda i: (i, 0)), pl.BlockSpec((d, k), lambda i: (0, 0))]
    ins += [x, rstd, w, dres]
    specs += [row, pl.BlockSpec((tm, 1), lambda i: (i, 0)), vec, row]
    if after is not None:
        ins.append(after)
        specs.append(pl.BlockSpec(memory_space=pl.ANY))
    return pl.pallas_call(
        body, name=name, grid=(m // tm,), in_specs=specs, out_specs=[row, row, vec],
        out_shape=[jax.ShapeDtypeStruct((m, d), F32), jax.ShapeDtypeStruct((m, d), BF16),
                   jax.ShapeDtypeStruct((1, d), F32)],
        compiler_params=_cparams(("arbitrary",)),
    )(*ins)


def _softplus(x):
    return jnp.maximum(x, 0.0) + jnp.log(1.0 + jnp.exp(-jnp.abs(x)))


def _prep(proj_a, bias128, alog128, bl, t):
    n = bl * t
    nch = t // CHUNK
    col0 = (SSD_WIDTH + CONV_CH) // 128

    def body(p_ref, b_ref, al_ref, dt_ref, gd_ref, ac_ref, act_ref, negc_ref):
        negc_ref[...] = jnp.zeros_like(negc_ref)
        row = lax.broadcasted_iota(jnp.int32, (CHUNK, CHUNK), 0)
        col = lax.broadcasted_iota(jnp.int32, (CHUNK, CHUNK), 1)
        tril = (row >= col).astype(F32)
        lane = lax.broadcasted_iota(jnp.int32, (1, 128), 1)
        head_lanes = lane < 16
        a_row = -jnp.exp(al_ref[...])
        carry = jnp.zeros((1, 128), F32)
        for ci in range(nch):
            rows = slice(ci * CHUNK, (ci + 1) * CHUNK)
            xv = p_ref[rows, :] + b_ref[...]
            sp = _softplus(xv)
            acum = jnp.dot(tril, a_row * sp, precision=HIGHEST, preferred_element_type=F32)
            c = jnp.dot(tril, -_softplus(-xv), precision=HIGHEST, preferred_element_type=F32) + carry
            carry = c[CHUNK - 1:CHUNK, :]
            dt_ref[rows, :] = jnp.where(head_lanes, sp, 0.0)
            gd_ref[rows, :] = jnp.where(head_lanes, jax.nn.sigmoid(xv),
                                        jnp.where(lane < 32, jax.nn.sigmoid(-xv), 0.0))
            ac_ref[rows, :] = jnp.where(head_lanes, acum, 0.0)
            act_ref[:, rows] = jnp.transpose(acum)[0:16, :]
            c_t = jnp.transpose(c)
            for hp in range(8):
                negc_ref[hp, 0:2, rows] = -c_t[16 + 2 * hp:18 + 2 * hp, :]

    o128 = pl.BlockSpec((t, 128), lambda b: (b, 0))
    v128 = pl.BlockSpec((1, 128), lambda b: (0, 0))
    w128 = jax.ShapeDtypeStruct((n, 128), F32)
    return pl.pallas_call(
        body, name="head_scalars", grid=(bl,),
        in_specs=[pl.BlockSpec((t, 128), lambda b: (b, col0)), v128, v128],
        out_specs=[o128, o128, o128, pl.BlockSpec((16, t), lambda b: (0, b)),
                   pl.BlockSpec((None, 8, 8, t), lambda b: (b, 0, 0, 0))],
        out_shape=[w128, w128, w128, jax.ShapeDtypeStruct((16, n), F32),
                   jax.ShapeDtypeStruct((bl, 8, 8, t), F32)],
        compiler_params=_cparams(("parallel",)),
    )(proj_a, bias128, alog128)


def _fpost(dc, gate_d, ddt, dpa, bl, t):
    n = bl * t
    nch = t // CHUNK
    col0 = (SSD_WIDTH + CONV_CH) // 128

    def body(dc_ref, gd_ref, ddt_ref, dpa_in, out_ref, db_ref):
        @pl.when(pl.program_id(0) == 0)
        def _():
            db_ref[...] = jnp.zeros_like(db_ref)

        row = lax.broadcasted_iota(jnp.int32, (CHUNK, CHUNK), 0)
        col = lax.broadcasted_iota(jnp.int32, (CHUNK, CHUNK), 1)
        triu = (row <= col).astype(F32)
        lane = lax.broadcasted_iota(jnp.int32, (1, 128), 1)
        gate_lanes = (lane >= 16) & (lane < 32)
        carry = jnp.zeros((1, 128), F32)
        db = jnp.zeros((1, 128), F32)
        for ci in reversed(range(nch)):
            rows = slice(ci * CHUNK, (ci + 1) * CHUNK)
            dlf = jnp.dot(triu, dc_ref[rows, :], precision=HIGHEST, preferred_element_type=F32) + carry
            carry = dlf[0:1, :]
            df = jnp.where(gate_lanes, dlf * gd_ref[rows, :], 0.0)
            out_ref[rows, :] = (ddt_ref[rows, :] + df).astype(BF16)
            db = db + jnp.sum(df, axis=0, keepdims=True)
        db_ref[...] += db[:, 16:32]

    blk = pl.BlockSpec((t, 128), lambda b: (b, 0))
    return pl.pallas_call(
        body, name="forget_gate_bwd", grid=(bl,),
        in_specs=[blk, blk, blk, ANY],
        out_specs=[pl.BlockSpec((t, 128), lambda b: (b, col0)), pl.BlockSpec((1, 16), lambda b: (0, 0))],
        out_shape=[jax.ShapeDtypeStruct(dpa.shape, dpa.dtype), jax.ShapeDtypeStruct((1, 16), F32)],
        input_output_aliases={3: 0},
        compiler_params=_cparams(("arbitrary",)),
    )(dc, gate_d, ddt, dpa)


CONV_TILE = 256
CONV_ROWS = 256


def _conv_taps(u_ref, i):
    r0 = pl.multiple_of(i * CONV_ROWS, CONV_ROWS)
    cur = u_ref[pl.ds(r0, CONV_ROWS), :]
    p0 = pl.multiple_of(jnp.maximum(r0 - 8, 0), 8)
    prev = jnp.where(i > 0, u_ref[pl.ds(p0, 8), :], 0.0)
    cat = jnp.concatenate([prev, cur], axis=0)
    return r0, [cur] + [pltpu.roll(cat, s, 0)[8:, :] for s in (1, 2, 3)]


def _conv_fwd(proj_a, conv_w, conv_b, bl, t):
    n = bl * t
    nct = CONV_CH // CONV_TILE
    c0 = SSD_WIDTH // CONV_TILE

    def body(u_ref, w_ref, b_ref, o_ref, d_ref):
        w = w_ref[...]
        bias = b_ref[...]

        def chunk(i, carry):
            r0, taps = _conv_taps(u_ref, i)
            pre = bias + w[3:4, :] * taps[0]
            for s in (1, 2, 3):
                pre = pre + w[3 - s:4 - s, :] * taps[s]
            sg = jax.nn.sigmoid(pre)
            o_ref[pl.ds(r0, CONV_ROWS), :] = pre * sg
            d_ref[pl.ds(r0, CONV_ROWS), :] = (sg * (1.0 + pre * (1.0 - sg))).astype(BF16)
            return carry

        lax.fori_loop(0, t // CONV_ROWS, chunk, 0)

    out = pl.BlockSpec((t, CONV_TILE), lambda b, c: (b, c))
    return pl.pallas_call(
        body, name="conv_silu_fwd", grid=(bl, nct),
        in_specs=[pl.BlockSpec((t, CONV_TILE), lambda b, c: (b, c0 + c)),
                  pl.BlockSpec((4, CONV_TILE), lambda b, c: (0, c)),
                  pl.BlockSpec((1, CONV_TILE), lambda b, c: (0, c))],
        out_specs=[out, out],
        out_shape=[jax.ShapeDtypeStruct((n, CONV_CH), F32), jax.ShapeDtypeStruct((n, CONV_CH), BF16)],
        compiler_params=_cparams(("parallel", "parallel")),
    )(proj_a, conv_w, conv_b)


def _conv_bwd(dxc, dsilu, proj_a, conv_w, dpa, bl, t):
    nct = CONV_CH // CONV_TILE
    c0 = SSD_WIDTH // CONV_TILE
    nrc = t // CONV_ROWS

    def body(g_ref, s_ref, u_ref, w_ref, dpa_in, du_ref, dw_ref, db_ref, dp_scr):
        @pl.when(pl.program_id(1) == 0)
        def _():
            dw_ref[...] = jnp.zeros_like(dw_ref)
            db_ref[...] = jnp.zeros_like(db_ref)

        w = w_ref[...]
        dp_scr[pl.ds(t, 8), :] = jnp.zeros((8, CONV_TILE), F32)

        def chunk1(i, carry):
            dw0, dw1, dw2, dw3, db = carry
            r0, taps = _conv_taps(u_ref, i)
            dpre = g_ref[pl.ds(r0, CONV_ROWS), :] * s_ref[pl.ds(r0, CONV_ROWS), :].astype(F32)
            dp_scr[pl.ds(r0, CONV_ROWS), :] = dpre
            dw3 = dw3 + jnp.sum(dpre * taps[0], axis=0, keepdims=True)
            dw2 = dw2 + jnp.sum(dpre * taps[1], axis=0, keepdims=True)
            dw1 = dw1 + jnp.sum(dpre * taps[2], axis=0, keepdims=True)
            dw0 = dw0 + jnp.sum(dpre * taps[3], axis=0, keepdims=True)
            db = db + jnp.sum(dpre, axis=0, keepdims=True)
            return dw0, dw1, dw2, dw3, db

        z = jnp.zeros((1, CONV_TILE), F32)
        dw0, dw1, dw2, dw3, db = lax.fori_loop(0, nrc, chunk1, (z, z, z, z, z))
        dw_ref[...] += jnp.concatenate([dw0, dw1, dw2, dw3], axis=0)
        db_ref[...] += db

        def chunk2(i, carry):
            r0 = pl.multiple_of(i * CONV_ROWS, CONV_ROWS)
            cat = dp_scr[pl.ds(r0, CONV_ROWS + 8), :]
            du = w[3:4, :] * cat[:CONV_ROWS, :]
            for s in (1, 2, 3):
                du = du + w[3 - s:4 - s, :] * pltpu.roll(cat, CONV_ROWS + 8 - s, 0)[:CONV_ROWS, :]
            du_ref[pl.ds(r0, CONV_ROWS), :] = du.astype(BF16)
            return carry

        lax.fori_loop(0, nrc, chunk2, 0)

    tile = pl.BlockSpec((t, CONV_TILE), lambda c, b: (b, c))
    return pl.pallas_call(
        body, name="conv_silu_bwd", grid=(nct, bl),
        in_specs=[tile, tile, pl.BlockSpec((t, CONV_TILE), lambda c, b: (b, c0 + c)),
                  pl.BlockSpec((4, CONV_TILE), lambda c, b: (0, c)), ANY],
        out_specs=[pl.BlockSpec((t, CONV_TILE), lambda c, b: (b, c0 + c)),
                   pl.BlockSpec((4, CONV_TILE), lambda c, b: (0, c)),
                   pl.BlockSpec((1, CONV_TILE), lambda c, b: (0, c))],
        out_shape=[jax.ShapeDtypeStruct(dpa.shape, dpa.dtype), jax.ShapeDtypeStruct((4, CONV_CH), F32),
                   jax.ShapeDtypeStruct((1, CONV_CH), F32)],
        input_output_aliases={4: 0},
        scratch_shapes=[pltpu.VMEM((t + 8, CONV_TILE), F32)],
        compiler_params=_cparams(("parallel", "arbitrary")),
    )(dxc, dsilu, proj_a, conv_w, dpa)


SSD_FWD_CHUNKS = 4
SSD_BWD_CHUNKS = 1
NT_DIMS = (((1,), (1,)), ((), ()))
TN_DIMS = (((0,), (0,)), ((), ()))


def _dot(a, b, dims=None):
    if dims is None:
        return jnp.dot(a, b, preferred_element_type=F32)
    return lax.dot_general(a, b, dims, preferred_element_type=F32)


def _head_expander():
    r = lax.broadcasted_iota(jnp.int32, (128, SSD_WIDTH), 0)
    c = lax.broadcasted_iota(jnp.int32, (128, SSD_WIDTH), 1)
    return ((c // HEAD_DIM == r % 16) & (r < 48)).astype(BF16)


def _spread(v128, expander):
    hi = v128.astype(BF16).astype(F32)
    r1 = v128 - hi
    mid = r1.astype(BF16).astype(F32)
    lo = (r1 - mid).astype(BF16).astype(F32)
    packed = (hi + pltpu.roll(mid, 16, 1) + pltpu.roll(lo, 32, 1)).astype(BF16)
    return jnp.dot(packed, expander, preferred_element_type=F32)


def _head_sums(v1024, expander):
    hi = v1024.astype(BF16)
    lo = (v1024 - hi.astype(F32)).astype(BF16)
    heads = jnp.where(lax.broadcasted_iota(jnp.int32, expander.shape, 0) < 16, expander, jnp.zeros_like(expander))
    return _dot(hi, heads, NT_DIMS) + _dot(lo, heads, NT_DIMS)


def _ssd_fwd(xc, proj_a, dt, acum, acum_t, dskip_e, norm_w, bl, t):
    n = bl * t
    nch = t // CHUNK
    L = CHUNK

    def body(xc_blk, z_blk, dt_blk, ac_blk, act_blk, dsk_ref, nw_ref, ys_blk, yp_blk, hp_blk, h_scr, y_scr, x_scr):
        @pl.when(pl.program_id(1) == 0)
        def _():
            h_scr[...] = jnp.zeros_like(h_scr)

        for sub in range(SSD_FWD_CHUNKS):
            rows = pl.ds(sub * L, L)
            chunk(xc_blk.at[rows, :], z_blk.at[rows, :], dt_blk.at[rows, :], ac_blk.at[rows, :], act_blk.at[:, rows],
                  dsk_ref, nw_ref, ys_blk.at[rows, :], yp_blk.at[rows, :], hp_blk.at[sub], h_scr, y_scr, x_scr)

    def chunk(xc_ref, z_ref, dt_ref, ac_ref, act_ref, dsk_ref, nw_ref, ys_ref, yp_ref, hp_ref, h_scr, y_scr, x_scr):
        row = lax.broadcasted_iota(jnp.int32, (L, L), 0)
        col = lax.broadcasted_iota(jnp.int32, (L, L), 1)
        causal = row >= col
        lane128 = lax.broadcasted_iota(jnp.int32, (1, L), 1)
        expander = _head_expander()
        ac_all = ac_ref[...]
        act_all = act_ref[...]
        ac_e = _spread(ac_all, expander)
        e_in = jnp.exp(ac_e)
        dec = jnp.exp(ac_e[L - 1:L, :] - ac_e)
        xs_all = xc_ref[:, 0:SSD_WIDTH]
        x_all = xs_all * _spread(dt_ref[...], expander)
        x_scr[...] = x_all.astype(BF16)
        hp_all = h_scr[...]
        hp_ref[...] = hp_all
        for g in range(2):
            gs = slice(g * 512, (g + 1) * 512)
            bg = xc_ref[:, SSD_WIDTH + g * 128:SSD_WIDTH + (g + 1) * 128].astype(BF16)
            cg = xc_ref[:, SSD_WIDTH + 256 + g * 128:SSD_WIDTH + 256 + (g + 1) * 128].astype(BF16)
            gmat = _dot(cg, bg, NT_DIMS)
            y_off = _dot(cg, hp_all[gs, :].astype(BF16), NT_DIMS) * e_in[:, gs] + dsk_ref[:, gs] * xs_all[:, gs]
            s_new = _dot((x_all[:, gs] * dec[:, gs]).astype(BF16), bg, TN_DIMS)
            for pr in range(4):
                pair = slice((g * 4 + pr) * 128, (g * 4 + pr + 1) * 128)
                x_pair = x_scr[:, pair]
                y_pair = y_off[:, pr * 128:(pr + 1) * 128]
                for j in range(2):
                    h = g * 8 + 2 * pr + j
                    sl = slice(h * HEAD_DIM, (h + 1) * HEAD_DIM)
                    r = 2 * pr + j
                    ldec = jnp.exp(jnp.where(causal, ac_all[:, h:h + 1] - act_all[h:h + 1, :], NEG))
                    x_head = jnp.where((lane128 < HEAD_DIM) == (j == 0), x_pair, jnp.zeros_like(x_pair))
                    y_pair = y_pair + _dot((gmat * ldec).astype(BF16), x_head)
                    elast = jnp.exp(ac_all[L - 1:L, h:h + 1])
                    h_scr[sl, :] = elast * hp_all[sl, :] + s_new[r * HEAD_DIM:(r + 1) * HEAD_DIM, :]
                y_scr[:, pair] = y_pair
        y = y_scr[...]
        yp_ref[...] = y
        zv = z_ref[...]
        yg = y * (zv * jax.nn.sigmoid(zv))
        for g in range(2):
            gs = slice(g * 512, (g + 1) * 512)
            grp = yg[:, gs]
            rstd = lax.rsqrt(jnp.mean(grp * grp, axis=1, keepdims=True) + EPS)
            ys_ref[:, gs] = (grp * rstd * nw_ref[:, gs]).astype(BF16)

    cps = SSD_FWD_CHUNKS
    steps = nch // cps
    rb = lambda b, c: (b * steps + c, 0)
    v1k = pl.BlockSpec((1, SSD_WIDTH), lambda b, c: (0, 0))
    return pl.pallas_call(
        body, name="ssd_fwd", grid=(bl, steps),
        in_specs=[pl.BlockSpec((cps * L, CONV_CH), rb), pl.BlockSpec((cps * L, SSD_WIDTH), rb),
                  pl.BlockSpec((cps * L, 128), rb), pl.BlockSpec((cps * L, 128), rb),
                  pl.BlockSpec((16, cps * L), lambda b, c: (0, b * steps + c)), v1k, v1k],
        out_specs=[pl.BlockSpec((cps * L, SSD_WIDTH), rb), pl.BlockSpec((cps * L, SSD_WIDTH), rb),
                   pl.BlockSpec((cps, SSD_WIDTH, SSD_STATE), lambda b, c: (b * steps + c, 0, 0))],
        out_shape=[jax.ShapeDtypeStruct((n, SSD_WIDTH), BF16), jax.ShapeDtypeStruct((n, SSD_WIDTH), F32),
                   jax.ShapeDtypeStruct((bl * nch, SSD_WIDTH, SSD_STATE), F32)],
        scratch_shapes=[pltpu.VMEM((SSD_WIDTH, SSD_STATE), F32), pltpu.VMEM((L, SSD_WIDTH), F32),
                        pltpu.VMEM((L, SSD_WIDTH), BF16)],
        compiler_params=_cparams(("parallel", "arbitrary")),
    )(xc, proj_a, dt, acum, acum_t, dskip_e, norm_w)


def _ssd_bwd(dys, xc, proj_a, ypre, hprev, dt, gate_d, acum, acum_t, alog128, dskip_e, norm_w, bl, t):
    n = bl * t
    nch = t // CHUNK
    L = CHUNK

    def body(dys_blk, xc_blk, z_blk, yp_blk, hp_blk, dt_blk, gd_blk, ac_blk, act_blk, al_ref, dsk_ref, nw_ref,
             dxc_blk, dz_blk, ddt_blk, dnw_ref, dsk16_ref, da16_ref, db16_ref,
             dh_scr, dy_scr, x_scr, dx_scr, red_scr):
        first = (pl.program_id(0) == 0) & (pl.program_id(1) == 0)

        @pl.when(first)
        def _():
            dnw_ref[...] = jnp.zeros_like(dnw_ref)
            dsk16_ref[...] = jnp.zeros_like(dsk16_ref)
            da16_ref[...] = jnp.zeros_like(da16_ref)
            db16_ref[...] = jnp.zeros_like(db16_ref)

        @pl.when(pl.program_id(1) == 0)
        def _():
            dh_scr[...] = jnp.zeros_like(dh_scr)

        for sub in reversed(range(SSD_BWD_CHUNKS)):
            rows = pl.ds(sub * L, L)
            chunk(dys_blk.at[rows, :], xc_blk.at[rows, :], z_blk.at[rows, :], yp_blk.at[rows, :], hp_blk.at[sub],
                  dt_blk.at[rows, :], gd_blk.at[rows, :], ac_blk.at[rows, :], act_blk.at[:, rows], al_ref, dsk_ref,
                  nw_ref, dxc_blk.at[rows, :], dz_blk.at[rows, :], ddt_blk.at[rows, :], dnw_ref, dsk16_ref, da16_ref,
                  db16_ref, dh_scr, dy_scr, x_scr, dx_scr, red_scr)

    def chunk(dys_ref, xc_ref, z_ref, yp_ref, hp_ref, dt_ref, gd_ref, ac_ref, act_ref, al_ref, dsk_ref, nw_ref,
              dxc_ref, dz_ref, ddt_ref, dnw_ref, dsk16_ref, da16_ref, db16_ref,
              dh_scr, dy_scr, x_scr, dx_scr, red_scr):
        y = yp_ref[...]
        zv = z_ref[...]
        sz = jax.nn.sigmoid(zv)
        gate = zv * sz
        yg = y * gate
        dout = dys_ref[...]
        nw = nw_ref[...]
        for g in range(2):
            gs = slice(g * 512, (g + 1) * 512)
            grp = yg[:, gs]
            rstd = lax.rsqrt(jnp.mean(grp * grp, axis=1, keepdims=True) + EPS)
            ghat = grp * rstd
            dnw_ref[:, gs] += jnp.sum(dout[:, gs] * ghat, axis=0, keepdims=True)
            gw = dout[:, gs] * nw[:, gs]
            dyg = rstd * (gw - ghat * jnp.mean(gw * ghat, axis=1, keepdims=True))
            dy_scr[:, gs] = dyg * gate[:, gs]
            dz_ref[:, gs] = (dyg * y[:, gs] * (sz[:, gs] * (1.0 + zv[:, gs] * (1.0 - sz[:, gs])))).astype(BF16)

        row = lax.broadcasted_iota(jnp.int32, (L, L), 0)
        col = lax.broadcasted_iota(jnp.int32, (L, L), 1)
        causal = row >= col
        lane128 = lax.broadcasted_iota(jnp.int32, (1, L), 1)
        rows128 = lax.broadcasted_iota(jnp.int32, (L, 1), 0)
        last_row = rows128 == (L - 1)
        expander = _head_expander()
        ac_all = ac_ref[...]
        act_all = act_ref[...]
        dt_all = dt_ref[...]
        dt_e = _spread(dt_all, expander)
        ac_e = _spread(ac_all, expander)
        e_in = jnp.exp(ac_e)
        dec = jnp.exp(ac_e[L - 1:L, :] - ac_e)
        xs_all = xc_ref[:, 0:SSD_WIDTH]
        x_all = xs_all * dt_e
        x_scr[...] = x_all.astype(BF16)
        dy_all = dy_scr[...]
        hp_all = hp_ref[...]
        ds_all = dh_scr[...]
        dsk_cols = jnp.sum(dy_all * xs_all, axis=0, keepdims=True)
        dac = jnp.zeros((L, L), F32)
        dac_row = jnp.zeros((L, L), F32)
        ddec_cols = []
        for g in range(2):
            gs = slice(g * 512, (g + 1) * 512)
            bsl = slice(SSD_WIDTH + g * 128, SSD_WIDTH + (g + 1) * 128)
            csl = slice(SSD_WIDTH + 256 + g * 128, SSD_WIDTH + 256 + (g + 1) * 128)
            bg = xc_ref[:, bsl].astype(BF16)
            cg = xc_ref[:, csl].astype(BF16)
            gmat = _dot(cg, bg, NT_DIMS)
            hpb = hp_all[gs, :].astype(BF16)
            dsb = ds_all[gs, :].astype(BF16)
            ch = _dot(cg, hpb, NT_DIMS)
            dye = dy_all[:, gs] * e_in[:, gs]
            dyeb = dye.astype(BF16)
            dc_acc = _dot(dyeb, hpb)
            dhp = _dot(dyeb, cg, TN_DIMS)
            dxd = _dot(bg, dsb, NT_DIMS)
            db_acc = _dot((x_all[:, gs] * dec[:, gs]).astype(BF16), dsb)
            ddec = dxd * x_all[:, gs] * dec[:, gs]
            ddec_cols.append(jnp.sum(ddec, axis=0, keepdims=True))
            dx_inter = dxd * dec[:, gs]
            red_scr[:, gs] = dye * ch - ddec
            dg_sum = jnp.zeros((L, L), F32)
            for pr in range(4):
                pair = slice((g * 4 + pr) * 128, (g * 4 + pr + 1) * 128)
                x_pair = x_scr[:, pair]
                dy_pair = dy_scr[:, pair].astype(BF16)
                dx_pair = dx_inter[:, pr * 128:(pr + 1) * 128]
                for j in range(2):
                    h = g * 8 + 2 * pr + j
                    r = 2 * pr + j
                    sl = slice(h * HEAD_DIM, (h + 1) * HEAD_DIM)
                    onehot_w = lane128 == h
                    ldec = jnp.exp(jnp.where(causal, ac_all[:, h:h + 1] - act_all[h:h + 1, :], NEG))
                    mf = gmat * ldec
                    dyb = jnp.where((lane128 < HEAD_DIM) == (j == 0), dy_pair, jnp.zeros_like(dy_pair))
                    dm = _dot(dyb, x_pair, NT_DIMS)
                    dx_pair = dx_pair + _dot(mf.astype(BF16), dyb, TN_DIMS)
                    dg_sum = dg_sum + dm * ldec
                    wmat = dm * mf
                    elast = jnp.exp(ac_all[L - 1:L, h:h + 1])
                    hp_h = hp_all[sl, :]
                    ds_h = ds_all[sl, :]
                    extra = elast * jnp.sum(jnp.sum(hp_h * ds_h, axis=1, keepdims=True), axis=0, keepdims=True)
                    dac = dac + jnp.where(onehot_w,
                                          jnp.sum(wmat, axis=1, keepdims=True) + jnp.where(last_row, extra, 0.0), 0.0)
                    dac_row = dac_row + jnp.where(rows128 == h, -jnp.sum(wmat, axis=0, keepdims=True), 0.0)
                    dh_scr[sl, :] = elast * ds_h + dhp[r * HEAD_DIM:(r + 1) * HEAD_DIM, :]
                dx_scr[:, pair] = dx_pair
            dgb = dg_sum.astype(BF16)
            dxc_ref[:, csl] = dc_acc + _dot(dgb, bg)
            dxc_ref[:, bsl] = db_acc + _dot(dgb, cg, TN_DIMS)
        dx_all = dx_scr[...]
        dxc_ref[:, 0:SSD_WIDTH] = dx_all * dt_e + dsk_ref[...] * dy_all
        red = red_scr[...]
        dac_slab = _head_sums(red, expander)
        ddec_tot = _head_sums(jnp.broadcast_to(jnp.concatenate(ddec_cols, axis=1), (8, SSD_WIDTH)), expander)
        ddt_x = _head_sums(dx_all * xs_all, expander)
        dsk16_ref[...] += _head_sums(jnp.broadcast_to(dsk_cols, (8, SSD_WIDTH)), expander)[0:1, 0:16]
        dac = dac + dac_slab + jnp.transpose(dac_row) + jnp.where(last_row, ddec_tot[0:1, :], 0.0)
        triu = (row <= col).astype(F32)
        da = jnp.dot(triu, dac, precision=HIGHEST, preferred_element_type=F32)
        a_row = -jnp.exp(al_ref[...])
        ddt = jnp.where(lane128 < 16, (ddt_x + da * a_row) * gd_ref[...], 0.0)
        ddt_ref[...] = ddt
        da16_ref[...] += (jnp.sum(da * dt_all, axis=0, keepdims=True) * a_row)[:, 0:16]
        db16_ref[...] += jnp.sum(ddt, axis=0, keepdims=True)[:, 0:16]

    cps = SSD_BWD_CHUNKS
    steps = nch // cps
    rb = lambda b, c: (b * steps + steps - 1 - c, 0)
    v1k = pl.BlockSpec((1, SSD_WIDTH), lambda b, c: (0, 0))
    v16 = pl.BlockSpec((1, 16), lambda b, c: (0, 0))
    v128 = pl.BlockSpec((1, 128), lambda b, c: (0, 0))
    wide = pl.BlockSpec((cps * L, SSD_WIDTH), rb)
    s128 = pl.BlockSpec((cps * L, 128), rb)
    return pl.pallas_call(
        body, name="ssd_bwd", grid=(bl, steps),
        in_specs=[wide, pl.BlockSpec((cps * L, CONV_CH), rb), wide, wide,
                  pl.BlockSpec((cps, SSD_WIDTH, SSD_STATE), lambda b, c: (b * steps + steps - 1 - c, 0, 0)),
                  s128, s128, s128, pl.BlockSpec((16, cps * L), lambda b, c: (0, b * steps + steps - 1 - c)),
                  v128, v1k, v1k],
        out_specs=[pl.BlockSpec((cps * L, CONV_CH), rb), wide, s128, v1k, v16, v16, v16],
        out_shape=[jax.ShapeDtypeStruct((n, CONV_CH), F32), jax.ShapeDtypeStruct((n, PA_WIDTH), BF16),
                   jax.ShapeDtypeStruct((n, 128), F32), jax.ShapeDtypeStruct((1, SSD_WIDTH), F32),
                   jax.ShapeDtypeStruct((1, 16), F32), jax.ShapeDtypeStruct((1, 16), F32),
                   jax.ShapeDtypeStruct((1, 16), F32)],
        scratch_shapes=[pltpu.VMEM((SSD_WIDTH, SSD_STATE), F32), pltpu.VMEM((L, SSD_WIDTH), F32),
                        pltpu.VMEM((L, SSD_WIDTH), BF16), pltpu.VMEM((L, SSD_WIDTH), F32),
                        pltpu.VMEM((L, SSD_WIDTH), F32)],
        compiler_params=_cparams(("arbitrary", "arbitrary")),
    )(dys, xc, proj_a, ypre, hprev, dt, gate_d, acum, acum_t, alog128, dskip_e, norm_w)


def _attn_fwd(qkv, negc, bl, t):
    n = bl * t
    tb_ = ATT_BLOCK
    nb = t // tb_
    scale2 = LOG2E / math.sqrt(HEAD_DIM)

    def body(q_ref, k_ref, v_ref, c_ref, o_ref, lse_ref, v0_scr, v1_scr, k0_scr, k1_scr):
        row = lax.broadcasted_iota(jnp.int32, (tb_, tb_), 0)
        col = lax.broadcasted_iota(jnp.int32, (tb_, tb_), 1)
        causal = row >= col
        lane = lax.broadcasted_iota(jnp.int32, (1, 128), 1)
        v_pair = v_ref[...].astype(F32)
        k_pair = k_ref[...]
        v_scrs = (v0_scr, v1_scr)
        k_scrs = (k0_scr, k1_scr)
        for j in range(2):
            v_head = v_pair if j == 0 else pltpu.roll(v_pair, HEAD_DIM, 1)
            v_scrs[j][...] = jnp.where(lane < HEAD_DIM, v_head, jnp.where(lane == HEAD_DIM, 1.0, 0.0)).astype(BF16)
            k_scrs[j][...] = jnp.where((lane < HEAD_DIM) == (j == 0), k_pair, jnp.zeros_like(k_pair))
        for qi in range(nb):
            r0, lk = qi * tb_, (qi + 1) * tb_
            for j in range(2):
                sl = slice(j * HEAD_DIM, (j + 1) * HEAD_DIM)
                s = _dot(q_ref[r0:lk, :], k_scrs[j][0:lk, :], NT_DIMS) * scale2 + c_ref[j:j + 1, 0:lk] * LOG2E
                tail = jnp.where(causal, s[:, r0:lk], NEG)
                s = tail if qi == 0 else jnp.concatenate([s[:, 0:r0], tail], axis=1)
                m = jnp.max(s, axis=1, keepdims=True)
                p = jnp.exp2(s - m)
                acc = _dot(p.astype(BF16), v_scrs[j][0:lk, :])
                l = acc[:, HEAD_DIM:HEAD_DIM + 1]
                o_ref[r0:lk, sl] = (acc[:, 0:HEAD_DIM] / l).astype(BF16)
                lse_ref[r0:lk, sl] = jnp.broadcast_to(m + jnp.log(l) * LOG2E, (tb_, HEAD_DIM))

    blk = lambda off: pl.BlockSpec((t, 128), lambda b, hp: (b, off + hp))
    return pl.pallas_call(
        body, name="fox_attn_fwd", grid=(bl, 8),
        in_specs=[blk(0), blk(8), blk(16), pl.BlockSpec((None, None, 8, t), lambda b, hp: (b, hp, 0, 0))],
        out_specs=[blk(0), blk(0)],
        out_shape=[jax.ShapeDtypeStruct((n, ATT_WIDTH), BF16), jax.ShapeDtypeStruct((n, ATT_WIDTH), F32)],
        scratch_shapes=[pltpu.VMEM((t, 128), BF16)] * 4,
        compiler_params=_cparams(("parallel", "parallel")),
    )(qkv, qkv, qkv, negc)


def _attn_bwd(qkv, do, o, lse, negc, after, bl, t):
    n = bl * t
    tb_ = ATT_BLOCK
    nb = t // tb_
    scale = 1.0 / math.sqrt(HEAD_DIM)
    scale2 = LOG2E * scale

    def body(q_ref, k_ref, v_ref, do_ref, o_ref, lse_ref, c_ref, after_ref, dq_ref, dk_ref, dv_ref, dc_ref,
             dq0_scr, delta_scr, dq1_scr, qt0_scr, qt1_scr, dot_scr, dkt0_scr, dkt1_scr, dvt_scr):
        row = lax.broadcasted_iota(jnp.int32, (tb_, tb_), 0)
        col = lax.broadcasted_iota(jnp.int32, (tb_, tb_), 1)
        causal = row >= col
        lane = lax.broadcasted_iota(jnp.int32, (1, 128), 1)
        dq_scrs = (dq0_scr, dq1_scr)
        qt_scrs = (qt0_scr, qt1_scr)
        dkt_scrs = (dkt0_scr, dkt1_scr)
        dq0_scr[...] = jnp.zeros_like(dq0_scr)
        dq1_scr[...] = jnp.zeros_like(dq1_scr)
        dc_ref[...] = jnp.zeros_like(dc_ref)
        q_t = jnp.transpose(q_ref[...].astype(F32))
        ones_row = jnp.where(lax.broadcasted_iota(jnp.int32, (8, t), 0) == 0, 1.0, 0.0)
        for j in range(2):
            qt_scrs[j][...] = jnp.concatenate(
                [q_t[j * HEAD_DIM:(j + 1) * HEAD_DIM, :], ones_row, jnp.zeros((HEAD_DIM - 8, t), F32)],
                axis=0).astype(BF16)
        dot_scr[...] = jnp.transpose(do_ref[...].astype(F32)).astype(BF16)
        prod = do_ref[...].astype(F32) * o_ref[...].astype(F32)
        for j in range(2):
            sl = slice(j * HEAD_DIM, (j + 1) * HEAD_DIM)
            delta_scr[:, sl] = jnp.broadcast_to(jnp.sum(prod[:, sl], axis=1, keepdims=True), (t, HEAD_DIM))
        for kj in range(nb):
            r0, r1 = kj * tb_, (kj + 1) * tb_
            k_blk = k_ref[r0:r1, :]
            v_blk = v_ref[r0:r1, :]
            k_pair = k_blk.astype(F32)
            for j in range(2):
                sl = slice(j * HEAD_DIM, (j + 1) * HEAD_DIM)
                one = slice(j * HEAD_DIM, j * HEAD_DIM + 1)
                own = (lane < HEAD_DIM) == (j == 0)
                k_head = k_pair if j == 0 else pltpu.roll(k_pair, HEAD_DIM, 1)
                k_ones = jnp.where(lane < HEAD_DIM, k_head, jnp.where(lane == HEAD_DIM, 1.0, 0.0)).astype(BF16)
                s = (_dot(q_ref[r0:t, :], jnp.where(own, k_blk, jnp.zeros_like(k_blk)), NT_DIMS) * scale2
                     + c_ref[j:j + 1, r0:r1] * LOG2E)
                head = jnp.where(causal, s[0:tb_, :], NEG)
                s = head if kj == nb - 1 else jnp.concatenate([head, s[tb_:, :]], axis=0)
                p = jnp.exp2(s - lse_ref[r0:t, one])
                dp = _dot(do_ref[r0:t, :], jnp.where(own, v_blk, jnp.zeros_like(v_blk)), NT_DIMS)
                ds = p * (dp - delta_scr[r0:t, one])
                dsb = ds.astype(BF16)
                dvt_scr[sl, r0:r1] = _dot(dot_scr[sl, r0:t], p.astype(BF16))
                dkt_scrs[j][:, r0:r1] = _dot(qt_scrs[j][:, r0:t], dsb)
                dq_scrs[j][r0:t, :] += _dot(dsb, k_ones)
        dv_ref[...] = jnp.transpose(dvt_scr[...]).astype(BF16)
        for j in range(2):
            sl = slice(j * HEAD_DIM, (j + 1) * HEAD_DIM)
            acc = dq_scrs[j][...]
            dkt = dkt_scrs[j][...]
            dq_ref[:, sl] = (acc[:, 0:HEAD_DIM] * scale).astype(BF16)
            dk_ref[:, sl] = (jnp.transpose(dkt)[:, 0:HEAD_DIM] * scale).astype(BF16)
            dc_ref[j:j + 1, :] = jnp.transpose(acc)[HEAD_DIM:HEAD_DIM + 1, :] - dkt[HEAD_DIM:HEAD_DIM + 1, :]

    blk = lambda off: pl.BlockSpec((t, 128), lambda b, hp: (b, off + hp))
    cblk = pl.BlockSpec((None, None, 8, t), lambda b, hp: (b, hp, 0, 0))
    return pl.pallas_call(
        body, name="fox_attn_bwd", grid=(bl, 8),
        in_specs=[blk(0), blk(8), blk(16), blk(0), blk(0), blk(0), cblk, ANY],
        out_specs=[blk(0), blk(0), blk(0), cblk],
        out_shape=[jax.ShapeDtypeStruct((n, ATT_WIDTH), BF16)] * 3 + [jax.ShapeDtypeStruct((bl, 8, 8, t), F32)],
        scratch_shapes=[pltpu.VMEM((t, 128), F32), pltpu.VMEM((t, 128), F32), pltpu.VMEM((t, 128), F32),
                        pltpu.VMEM((128, t), BF16), pltpu.VMEM((128, t), BF16), pltpu.VMEM((128, t), BF16),
                        pltpu.VMEM((128, t), F32), pltpu.VMEM((128, t), F32), pltpu.VMEM((128, t), F32)],
        compiler_params=_cparams(("parallel", "parallel")),
    )(qkv, qkv, qkv, do, o, lse, negc, after)


def _adamw(w, g, m, v, *, name):
    lead = w.ndim == 3
    r, c = w.shape[-2:]
    tr = _pick(r, (256, IN_SHARD // 3, 128, 64, 32, 16, 8))
    bc1 = 1.0 - ADAM_B1 ** ADAM_STEP
    bc2 = 1.0 - ADAM_B2 ** ADAM_STEP

    def body(w_ref, g_ref, m_ref, v_ref, d_ref, nm_ref, nv_ref):
        gv = g_ref[...]
        mn = ADAM_B1 * m_ref[...] + (1.0 - ADAM_B1) * gv
        vn = ADAM_B2 * v_ref[...] + (1.0 - ADAM_B2) * (gv * gv)
        m_hat = mn / bc1
        v_hat = vn / bc2
        d_ref[...] = -ADAM_LR * (m_hat / (jnp.sqrt(v_hat) + ADAM_EPS) + ADAM_WD * w_ref[...])
        nm_ref[...] = mn
        nv_ref[...] = vn

    flat = pl.BlockSpec((tr, c), lambda i: (i, 0))
    blk = pl.BlockSpec((None, tr, c), lambda i: (0, i, 0)) if lead else flat
    return pl.pallas_call(
        body, name=name, grid=(r // tr,), in_specs=[blk, flat, blk, blk], out_specs=[blk] * 3,
        out_shape=[jax.ShapeDtypeStruct(w.shape, F32)] * 3,
        compiler_params=_cparams(("parallel",)),
    )(w, g, m, v)


def _sum_leading(parts, *, name, out_dtype=F32):
    k, r, c = parts.shape
    tr = _pick(r, (512, 256, 128, 96, 64, 32, 16, 8))

    def body(p_ref, o_ref):
        acc = p_ref[0].astype(F32)
        for i in range(1, k):
            acc = acc + p_ref[i].astype(F32)
        o_ref[...] = acc.astype(out_dtype)

    return pl.pallas_call(
        body, name=name, grid=(r // tr,),
        in_specs=[pl.BlockSpec((k, tr, c), lambda i: (0, i, 0))],
        out_specs=pl.BlockSpec((tr, c), lambda i: (i, 0)),
        out_shape=jax.ShapeDtypeStruct((r, c), out_dtype),
        compiler_params=_cparams(("parallel",)),
    )(parts)


def _add_my_half(g, b, *, name):
    k, r, c = b.shape
    tr = _pick(r, (512, 256, 128))
    nrt = r // tr

    def body(lo_ref, hi_ref, b_ref, o_ref):
        mine = jnp.where(lax.axis_index("c") == 0, lo_ref[...], hi_ref[...])
        o_ref[...] = (mine.astype(F32) + b_ref[...].astype(F32)).astype(BF16)

    blk = pl.BlockSpec((None, tr, c), lambda j, i: (j, i, 0))
    return pl.pallas_call(
        body, name=name, grid=(k, nrt),
        in_specs=[blk, pl.BlockSpec((None, tr, c), lambda j, i: (j, i + nrt, 0)), blk], out_specs=blk,
        out_shape=jax.ShapeDtypeStruct((k, r, c), BF16),
        compiler_params=_cparams(("parallel", "parallel")),
    )(g, g, b)


ANY = pl.BlockSpec(memory_space=pl.ANY)


def _chip_peers(x, y):
    return [(1 - x, y, 2 * (1 - x) + y), (x, 1 - y, 2 * x + 1 - y), (1 - x, 1 - y, 2 * (1 - x) + 1 - y)]


def _gather_weights(blob, *, name):
    rows, cols = blob.shape
    half_rows = rows // 2

    def body(b_ref, o_ref, send_sems, recv_sems):
        x, y, c = lax.axis_index("x"), lax.axis_index("y"), lax.axis_index("c")
        me = 2 * x + y
        sibling = (x, y, 1 - c)
        peers = _chip_peers(x, y)

        def half(chip, hc):
            return o_ref.at[chip, pl.ds(hc * half_rows, half_rows), :]

        def copy(k, src, chip, hc, to):
            return pltpu.make_async_remote_copy(src_ref=src, dst_ref=half(chip, hc), send_sem=send_sems.at[k],
                                                recv_sem=recv_sems.at[k], device_id=to, device_id_type=MESH)

        my_half = b_ref.at[pl.ds(c * half_rows, half_rows), :]
        first = [copy(k, my_half, me, c, (px, py, c)) for k, (px, py, _) in enumerate(peers)]
        own = pltpu.make_async_remote_copy(src_ref=b_ref, dst_ref=o_ref.at[me], send_sem=send_sems.at[6],
                                           recv_sem=recv_sems.at[6], device_id=sibling, device_id_type=MESH)
        for cp in first + [own]:
            cp.start()
        passed = [copy(3 + k, half(pc, c), pc, c, sibling) for k, (_, _, pc) in enumerate(peers)]
        for k, (px, py, pc) in enumerate(peers):
            copy(k, my_half, pc, c, (px, py, c)).wait_recv()
            passed[k].start()
        for k, (_, _, pc) in enumerate(peers):
            copy(3 + k, half(pc, 1 - c), pc, 1 - c, sibling).wait_recv()
        own.wait_recv()
        for cp in first + passed + [own]:
            cp.wait_send()

    return pl.pallas_call(
        body, name=name, in_specs=[ANY], out_specs=ANY,
        out_shape=jax.ShapeDtypeStruct((N_CHIPS, rows, cols), BF16),
        scratch_shapes=[pltpu.SemaphoreType.DMA((7,)), pltpu.SemaphoreType.DMA((7,))],
    )(blob)


def _swap_halves(g, *, name):
    _, rows, cols = g.shape
    half_rows = rows // 2

    def body(g_ref, o_ref, send_sem, recv_sem):
        x, y, c = lax.axis_index("x"), lax.axis_index("y"), lax.axis_index("c")
        cp = pltpu.make_async_remote_copy(
            src_ref=g_ref.at[:, pl.ds((1 - c) * half_rows, half_rows), :], dst_ref=o_ref,
            send_sem=send_sem, recv_sem=recv_sem, device_id=(x, y, 1 - c), device_id_type=MESH)
        cp.start()
        cp.wait()

    return pl.pallas_call(
        body, name=name, in_specs=[ANY], out_specs=ANY,
        out_shape=jax.ShapeDtypeStruct((N_CHIPS, half_rows, cols), BF16),
        scratch_shapes=[pltpu.SemaphoreType.DMA, pltpu.SemaphoreType.DMA],
    )(g)


HBM_SPEC = pl.BlockSpec(memory_space=pltpu.HBM)
SEM_SPEC = pl.BlockSpec(memory_space=pltpu.SEMAPHORE)
SPLIT_EFFECT = pltpu.SideEffectType.DATAFLOW_SIDE_EFFECTING


def _gather_peers_copies(b_ref, land_ref, send_sems, recv_sems, sending):
    x, y, c = lax.axis_index("x"), lax.axis_index("y"), lax.axis_index("c")
    me = 2 * x + y
    half_rows = b_ref.shape[0] // 2
    src = b_ref.at[pl.ds(c * half_rows, half_rows), :]
    return [pltpu.make_async_remote_copy(
        src_ref=src, dst_ref=land_ref.at[me if sending else pc, pl.ds(c * half_rows, half_rows), :],
        send_sem=send_sems.at[k], recv_sem=recv_sems.at[k], device_id=(px, py, c), device_id_type=MESH)
        for k, (px, py, pc) in enumerate(_chip_peers(x, y))]


def _gather_start(blob, after, *, name):
    shape = (N_CHIPS,) + blob.shape

    def body(b_ref, land_ref, after_ref, send_sems, recv_sems, b_thru, land_thru, token):
        for cp in _gather_peers_copies(b_ref, land_ref, send_sems, recv_sems, True):
            cp.start()
        token[...] = jnp.zeros_like(token)

    return pl.pallas_call(
        body, name=name,
        out_shape=(pltpu.SemaphoreType.DMA((3,)), pltpu.SemaphoreType.DMA((3,)), pltpu.HBM(blob.shape, blob.dtype),
                   pltpu.HBM(shape, blob.dtype), jax.ShapeDtypeStruct((8, 128), F32)),
        in_specs=(HBM_SPEC, HBM_SPEC, ANY),
        out_specs=(SEM_SPEC, SEM_SPEC, HBM_SPEC, HBM_SPEC, pl.BlockSpec(memory_space=pltpu.VMEM)),
        input_output_aliases={0: 2, 1: 3},
        compiler_params=pltpu.CompilerParams(has_side_effects=SPLIT_EFFECT),
    )(pltpu.with_memory_space_constraint(blob, pltpu.HBM),
      pltpu.with_memory_space_constraint(lax.empty(shape, blob.dtype), pltpu.HBM), after)


def _gather_wait(send_sems, recv_sems, b_thru, land_thru, after, *, name):
    def body(b_ref, land_ref, send_sems, recv_sems, after_ref, b_dead, got_ref):
        for cp in _gather_peers_copies(b_ref, land_ref, send_sems, recv_sems, False):
            cp.wait_send()
            cp.wait_recv()

    return pl.pallas_call(
        body, name=name,
        out_shape=(pltpu.HBM(b_thru.shape, b_thru.dtype), pltpu.HBM(land_thru.shape, land_thru.dtype)),
        in_specs=(HBM_SPEC, HBM_SPEC, SEM_SPEC, SEM_SPEC, ANY), out_specs=(HBM_SPEC, HBM_SPEC),
        input_output_aliases={0: 0, 1: 1},
        compiler_params=pltpu.CompilerParams(has_side_effects=SPLIT_EFFECT),
    )(b_thru, land_thru, send_sems, recv_sems, after)


def _gather_forward(land, blob, *, name):
    half_rows = land.shape[1] // 2

    def body(l_ref, b_ref, o_ref, send_sems, recv_sems):
        x, y, c = lax.axis_index("x"), lax.axis_index("y"), lax.axis_index("c")
        me = 2 * x + y
        sibling = (x, y, 1 - c)
        cps = []
        for k, (_, _, pc) in enumerate(_chip_peers(x, y)):
            mine = pl.ds(c * half_rows, half_rows)
            cps.append(pltpu.make_async_remote_copy(
                src_ref=l_ref.at[pc, mine, :], dst_ref=o_ref.at[pc, mine, :], send_sem=send_sems.at[k],
                recv_sem=recv_sems.at[k], device_id=sibling, device_id_type=MESH))
        cps.append(pltpu.make_async_remote_copy(src_ref=b_ref, dst_ref=o_ref.at[me], send_sem=send_sems.at[3],
                                                recv_sem=recv_sems.at[3], device_id=sibling, device_id_type=MESH))
        for cp in cps:
            cp.start()
        for k, (_, _, pc) in enumerate(_chip_peers(x, y)):
            theirs = pl.ds((1 - c) * half_rows, half_rows)
            pltpu.make_async_remote_copy(
                src_ref=l_ref.at[pc, theirs, :], dst_ref=o_ref.at[pc, theirs, :], send_sem=send_sems.at[k],
                recv_sem=recv_sems.at[k], device_id=sibling, device_id_type=MESH).wait_recv()
        cps[3].wait_recv()
        for cp in cps:
            cp.wait_send()

    return pl.pallas_call(
        body, name=name, in_specs=[ANY, ANY], out_specs=ANY, input_output_aliases={0: 0},
        out_shape=jax.ShapeDtypeStruct(land.shape, land.dtype),
        scratch_shapes=[pltpu.SemaphoreType.DMA((4,)), pltpu.SemaphoreType.DMA((4,))],
    )(land, blob)


def _exchange_peers_copies(p_ref, land_ref, send_sems, recv_sems, sending):
    x, y, c = lax.axis_index("x"), lax.axis_index("y"), lax.axis_index("c")
    me = 2 * x + y
    return [pltpu.make_async_remote_copy(src_ref=p_ref.at[pc], dst_ref=land_ref.at[me if sending else pc],
                                         send_sem=send_sems.at[k], recv_sem=recv_sems.at[k],
                                         device_id=(px, py, c), device_id_type=MESH)
            for k, (px, py, pc) in enumerate(_chip_peers(x, y))]


def _exchange_start(p, *, name):
    def body(p_ref, land_ref, send_sems, recv_sems, p_thru, land_thru, token):
        for cp in _exchange_peers_copies(p_ref, land_ref, send_sems, recv_sems, True):
            cp.start()
        token[...] = jnp.zeros_like(token)

    return pl.pallas_call(
        body, name=name,
        out_shape=(pltpu.SemaphoreType.DMA((3,)), pltpu.SemaphoreType.DMA((3,)), pltpu.HBM(p.shape, p.dtype),
                   pltpu.HBM(p.shape, p.dtype), jax.ShapeDtypeStruct((8, 128), F32)),
        in_specs=(HBM_SPEC, HBM_SPEC),
        out_specs=(SEM_SPEC, SEM_SPEC, HBM_SPEC, HBM_SPEC, pl.BlockSpec(memory_space=pltpu.VMEM)),
        input_output_aliases={0: 2, 1: 3},
        compiler_params=pltpu.CompilerParams(has_side_effects=SPLIT_EFFECT),
    )(pltpu.with_memory_space_constraint(p, pltpu.HBM),
      pltpu.with_memory_space_constraint(lax.empty(p.shape, p.dtype), pltpu.HBM))


def _exchange_wait(send_sems, recv_sems, p_thru, land_thru, after, *, name):
    def body(p_ref, land_ref, send_sems, recv_sems, after_ref, p_dead, got_ref):
        for cp in _exchange_peers_copies(p_ref, land_ref, send_sems, recv_sems, False):
            cp.wait_send()
            cp.wait_recv()

    return pl.pallas_call(
        body, name=name,
        out_shape=(pltpu.HBM(p_thru.shape, p_thru.dtype), pltpu.HBM(p_thru.shape, p_thru.dtype)),
        in_specs=(HBM_SPEC, HBM_SPEC, SEM_SPEC, SEM_SPEC, ANY), out_specs=(HBM_SPEC, HBM_SPEC),
        input_output_aliases={0: 0, 1: 1},
        compiler_params=pltpu.CompilerParams(has_side_effects=SPLIT_EFFECT),
    )(p_thru, land_thru, send_sems, recv_sems, after)


def _sum_parts(parts, own, *, name):
    k, r, c = parts.shape
    tr = _pick(r, (512, 256, 128))

    def body(p_ref, own_ref, o_ref):
        me = 2 * lax.axis_index("x") + lax.axis_index("y")
        acc = jnp.zeros((tr, c), F32)
        for i in range(k):
            acc = acc + jnp.where(me == i, own_ref[i], p_ref[i]).astype(F32)
        o_ref[...] = acc

    blk = pl.BlockSpec((k, tr, c), lambda i: (0, i, 0))
    return pl.pallas_call(
        body, name=name, grid=(r // tr,), in_specs=[blk, blk],
        out_specs=pl.BlockSpec((tr, c), lambda i: (i, 0)),
        out_shape=jax.ShapeDtypeStruct((r, c), F32),
        compiler_params=_cparams(("parallel",)),
    )(parts, own)


def _join_halves(gh, *, name):
    def body(g_ref, o_ref, send_sem, recv_sem):
        x, y, c = lax.axis_index("x"), lax.axis_index("y"), lax.axis_index("c")
        cp = pltpu.make_async_remote_copy(src_ref=g_ref, dst_ref=o_ref, send_sem=send_sem, recv_sem=recv_sem,
                                          device_id=(x, y, 1 - c), device_id_type=MESH)
        cp.start()
        cp.wait()

    other = pl.pallas_call(
        body, name=name, in_specs=[ANY], out_specs=ANY,
        out_shape=jax.ShapeDtypeStruct(gh.shape, F32),
        scratch_shapes=[pltpu.SemaphoreType.DMA, pltpu.SemaphoreType.DMA],
    )(gh)
    south = lax.axis_index("c") == 0
    return jnp.concatenate([jnp.where(south, gh, other), jnp.where(south, other, gh)], axis=0)


def _gather_small(s, *, name):
    rows = s.shape[0]

    def body(s_ref, o_ref, send_sems, recv_sems, local_sem):
        x, y, c = lax.axis_index("x"), lax.axis_index("y"), lax.axis_index("c")
        me = 4 * x + 2 * y + c
        mine = pltpu.make_async_copy(s_ref, o_ref.at[me], local_sem)
        mine.start()
        peers = []
        for k in range(1, 8):
            peers.append((1 - x if k & 4 else x, 1 - y if k & 2 else y, 1 - c if k & 1 else c))
        cps = [pltpu.make_async_remote_copy(src_ref=s_ref, dst_ref=o_ref.at[me], send_sem=send_sems.at[k],
                                            recv_sem=recv_sems.at[k], device_id=p, device_id_type=MESH)
               for k, p in enumerate(peers)]
        for cp in cps:
            cp.start()
        for k, (px, py, pc) in enumerate(peers):
            pltpu.make_async_remote_copy(src_ref=s_ref, dst_ref=o_ref.at[4 * px + 2 * py + pc],
                                         send_sem=send_sems.at[k], recv_sem=recv_sems.at[k],
                                         device_id=(px, py, pc), device_id_type=MESH).wait_recv()
        for cp in cps:
            cp.wait_send()
        mine.wait()

    return pl.pallas_call(
        body, name=name, in_specs=[ANY], out_specs=ANY,
        out_shape=jax.ShapeDtypeStruct((8, rows, 128), F32),
        scratch_shapes=[pltpu.SemaphoreType.DMA((7,)), pltpu.SemaphoreType.DMA((7,)), pltpu.SemaphoreType.DMA],
    )(s)


IN_SHARD = IN_WIDTH // N_CHIPS
IN_SHARD_PAD = 1536
UP_ROWS, DOWN_ROWS, OUT_ROWS = 1024, 1024, 512
REST_ROWS = UP_ROWS + DOWN_ROWS + OUT_ROWS


def _pack_in(w_in_s):
    return jnp.pad(w_in_s, ((0, 0), (0, IN_SHARD_PAD - IN_SHARD))).astype(BF16)


def _pack_rest(w_out_s, w_up_s, w_down_s):
    return jnp.concatenate([w_up_s, w_down_s, w_out_s], axis=0).astype(BF16)


def _unpack_rest(blob):
    return (blob[UP_ROWS + DOWN_ROWS:], blob[0:UP_ROWS], blob[UP_ROWS:UP_ROWS + DOWN_ROWS])


def _full_w_in(g_in):
    return jnp.concatenate([g_in[j, :, :IN_SHARD] for j in range(N_CHIPS)], axis=1)


def _full_rest(g_rest):
    parts = [_unpack_rest(g_rest[j]) for j in range(N_CHIPS)]
    w_out = jnp.concatenate([p[0] for p in parts], axis=0)
    w_up = jnp.concatenate([p[1] for p in parts], axis=1)
    w_down = jnp.concatenate([p[2] for p in parts], axis=0)
    return w_out, w_up, w_down


def _split_w_in(w_in):
    z_xbc = w_in[:, 0:2560]
    dt = w_in[:, 2560:2576]
    qkv = w_in[:, 2576:5648]
    f = w_in[:, 5648:5664]
    pad = jnp.zeros((w_in.shape[0], PA_WIDTH - 2592), w_in.dtype)
    return jnp.concatenate([z_xbc, dt, f, pad], axis=1), qkv


def _merge_w_in(d_a, d_qkv):
    return jnp.concatenate([d_a[:, 0:2560], d_a[:, 2560:2576], d_qkv, d_a[:, 2576:2592]], axis=1)


def _local_step(x3, target3, w_in, rest_weights, norm_mix_w, conv_w, conv_b, dt_bias, a_log, d_skip,
                ssd_norm_w, f_bias, norm_mlp_w, norm_final_w, first_after=None, early_grads=None, late_grads=None):
    bl, t, d = x3.shape
    n = bl * t
    x = x3.reshape(n, d)
    target = target3.reshape(n, d)
    w_a, w_qkv = _split_w_in(w_in)
    nfw = norm_final_w.reshape(1, d)
    dskip_e = jnp.repeat(d_skip, HEAD_DIM, axis=1)
    nb = t // ATT_BLOCK

    r1, r2, kt = min(n, 1024), min(n, 512), min(n, 2048)
    if first_after is None:
        first_after = jnp.zeros((8, 128), F32)
    h0, rstd0, proj_a = _norm_mm(x, norm_mix_w, w_a, first_after, name="norm_mix_proj_a", tm=r2)
    qkv = _mm(h0, w_qkv, name="proj_qkv", tiles=(r2, QKV_WIDTH, D_MODEL), out_dtype=BF16)
    bias128 = jnp.concatenate([dt_bias, f_bias, jnp.zeros((1, 96), F32)], axis=1)
    alog128 = jnp.concatenate([a_log, jnp.zeros((1, 112), F32)], axis=1)
    dt, gate_d, acum, acum_t, negc = _prep(proj_a, bias128, alog128, bl, t)
    xc, dsilu = _conv_fwd(proj_a, conv_w, conv_b, bl, t)
    y_ssd, y_pre, hprev = _ssd_fwd(xc, proj_a, dt, acum, acum_t, dskip_e, ssd_norm_w, bl, t)
    y_att, lse = _attn_fwd(qkv, negc, bl, t)
    w_out, w_up, w_down = rest_weights(y_att)
    wo_s, wo_a = w_out[:SSD_WIDTH], w_out[SSD_WIDTH:]
    h1, h1n, rstd1 = _mm_norm_fwd(y_ssd, wo_s, y_att, wo_a, x, norm_mlp_w, name="out_proj_norm_mlp", tm=r2)
    up = _mm(h1n, w_up, name="mlp_up", tiles=(r1, D_FF, D_MODEL), out_dtype=BF16)
    dh2, dh2b, loss, d_nfw = _mm_final(up, w_down, h1, nfw, target, name="mlp_down_final_norm_loss", tm=r2,
                                       a_act="relu2")

    dup = _mm(dh2b, w_down, name="mlp_down_bwd_act", tiles=(r2, D_FF, D_MODEL), tb=True, epi_up=up, out_dtype=BF16)
    rest_shape = (N_CHIPS, REST_ROWS, D_MODEL)
    gb_rest = _mm(up, dh2b, name="mlp_down_bwd_w", tiles=(DOWN_ROWS, D_MODEL, kt), ta=True, a_act="relu2",
                  out_dtype=BF16, into=(rest_shape, (None, DOWN_ROWS, D_MODEL), lambda i, j, k: (i, 1, 0), None))
    dh1, dh1b, d_nmlp = _mm_norm_bwd([(dup, w_up)], h1, rstd1, norm_mlp_w, dh2, name="mlp_up_bwd_act_norm_mlp",
                                     tm=r2)
    gb_rest = _mm(h1n, dup, name="mlp_up_bwd_w", tiles=(D_MODEL, UP_ROWS, kt), ta=True, out_dtype=BF16,
                  into=(rest_shape, (None, D_MODEL, UP_ROWS), lambda i, j, k: (j, 0, 0), gb_rest))
    dys, do = _mm_two_halves(dh1b, w_out, name="out_proj_bwd_act", tm=r1)
    out_block = (UP_ROWS + DOWN_ROWS) // OUT_ROWS
    for half, (y_half, tag) in enumerate(((y_ssd, "ssd"), (y_att, "att"))):
        gb_rest = _mm(y_half, dh1b, name="out_proj_bwd_w_" + tag, tiles=(2 * OUT_ROWS, D_MODEL, kt), ta=True,
                      out_dtype=BF16, into=(rest_shape, (2, OUT_ROWS, D_MODEL),
                                            functools.partial(lambda i, j, k, h: (h, out_block, 0), h=half),
                                            gb_rest))
    token = jnp.zeros((8, 128), F32) if early_grads is None else early_grads(gb_rest)
    dq, dk, dv, dcb = _attn_bwd(qkv, do, y_att, lse, negc, token, bl, t)
    dc = jnp.pad(dcb[:, :, 0:2, :].transpose(0, 3, 1, 2).reshape(n, 16), ((0, 0), (16, 96)))
    dxc, dpa, ddt_raw, d_snw, d_dsk, d_alog, d_dtb = _ssd_bwd(dys, xc, proj_a, y_pre, hprev, dt, gate_d, acum,
                                                             acum_t, alog128, dskip_e, ssd_norm_w, bl, t)
    dpa, d_conv_w, d_conv_b = _conv_bwd(dxc, dsilu, proj_a, conv_w, dpa, bl, t)
    dproj_a, d_fb = _fpost(dc, gate_d, ddt_raw, dpa, bl, t)
    dqkv = jnp.concatenate([dq, dk, dv], axis=1)
    d_w_a = _mm(h0, dproj_a, name="proj_a_bwd_w", tiles=(1024, 896, kt), ta=True, out_dtype=BF16)
    d_w_qkv = _mm(h0, dqkv, name="proj_qkv_bwd_w", tiles=(1024, 1024, kt), ta=True, out_dtype=BF16)
    d_w_in = _merge_w_in(d_w_a, d_w_qkv)
    late_token = None if late_grads is None else late_grads(d_w_in)
    dx, _, d_nmix = _mm_norm_bwd([(dproj_a, w_a), (dqkv, w_qkv)], x, rstd0, norm_mix_w, dh1,
                                 name="proj_bwd_act_norm_mix", tm=min(n, 256), after=late_token)

    grads = dict(norm_mix_w=d_nmix, w_in=d_w_in, conv_w=d_conv_w, conv_b=d_conv_b,
                 dt_bias=d_dtb, a_log=d_alog, d_skip=d_dsk, ssd_norm_w=d_snw, f_bias=d_fb, rest=gb_rest,
                 norm_mlp_w=d_nmlp, norm_final_w=d_nfw)
    return dx.reshape(bl, t, d), loss, grads


SMALL_ORDER = ("norm_mix_w", "conv_w", "conv_b", "dt_bias", "a_log", "d_skip", "ssd_norm_w", "f_bias",
               "norm_mlp_w", "norm_final_w")
SMALL_SIZES = (1024, 4 * CONV_CH, CONV_CH, 16, 16, 16, 1024, 16, 1024, 1024)


def _pack_small(vals, rows):
    flat = jnp.concatenate([v.reshape(-1).astype(F32) for v in vals])
    return jnp.pad(flat, (0, rows * 128 - flat.shape[0])).reshape(rows, 128)


def _unpack_small(packed, sizes):
    flat = packed.reshape(-1)
    out, o = [], 0
    for s in sizes:
        out.append(flat[o:o + s])
        o += s
    return out


def kernel(x, norm_mix_w, w_in, conv_w, conv_b, dt_bias, a_log, d_skip, ssd_norm_w, f_bias, w_out, norm_mlp_w, w_up, w_down, norm_final_w, loss_target, m_norm_mix_w, m_w_in, m_conv_w, m_conv_b, m_dt_bias, m_a_log, m_d_skip, m_ssd_norm_w, m_f_bias, m_w_out, m_norm_mlp_w, m_w_up, m_w_down, m_norm_final_w, v_norm_mix_w, v_w_in, v_conv_w, v_conv_b, v_dt_bias, v_a_log, v_d_skip, v_ssd_norm_w, v_f_bias, v_w_out, v_norm_mlp_w, v_w_up, v_w_down, v_norm_final_w):
    chip = 2 * lax.axis_index("x") + lax.axis_index("y")
    cw = CONV_CH // N_CHIPS

    own_in = _pack_in(w_in[0])
    own_rest = _pack_rest(w_out[0], w_up[0], w_down[0])
    g_in = _gather_weights(own_in, name="gather_w_in")
    w_in_f = _full_w_in(g_in)
    *rest_handles, rest_token = _gather_start(own_rest, g_in, name="gather_start_rest")

    def rest_weights(after):
        _, landed = _gather_wait(*rest_handles, after, name="gather_wait_rest")
        return _full_rest(_gather_forward(landed, own_rest, name="gather_forward_rest"))
    small_all = _gather_small(_pack_small([conv_w[0]], 16), name="gather_conv_w")
    conv_w_f = jnp.concatenate([small_all[2 * j].reshape(-1)[:4 * cw].reshape(4, cw) for j in range(N_CHIPS)], axis=1)

    def chip_partial(gb, tag):
        from_sibling = _swap_halves(gb, name="grad_swap_halves_" + tag)
        return _add_my_half(gb, from_sibling, name="grad_add_sibling_" + tag)

    in_flight = {}

    def early_grads(gb_rest):
        part = chip_partial(gb_rest, "rest")
        *handles, token = _exchange_start(part, name="grad_exchange_start_rest")
        in_flight["rest"] = handles
        return token

    def late_grads(d_w_in):
        by_chip = d_w_in.reshape(D_MODEL, N_CHIPS, IN_SHARD)
        gb_in = jnp.pad(by_chip, ((0, 0), (0, 0), (0, IN_SHARD_PAD - IN_SHARD))).transpose(1, 0, 2).astype(BF16)
        *handles, token = _exchange_start(chip_partial(gb_in, "in"), name="grad_exchange_start_in")
        in_flight["in"] = handles
        return token

    dx, loss_part, g = _local_step(x, loss_target, w_in_f, rest_weights, norm_mix_w, conv_w_f,
                                   conv_b, dt_bias, a_log, d_skip, ssd_norm_w, f_bias, norm_mlp_w, norm_final_w,
                                   first_after=rest_token, early_grads=early_grads, late_grads=late_grads)

    send_sems, recv_sems, part_rest, land_rest = in_flight["rest"]
    part_rest, parts_rest = _exchange_wait(send_sems, recv_sems, part_rest, land_rest, dx,
                                           name="grad_exchange_wait_rest")
    g_rest_half = _sum_parts(parts_rest, part_rest, name="grad_sum_chips_rest")
    g_w_out, g_w_up, g_w_down = _unpack_rest(_join_halves(g_rest_half, name="grad_join_halves_rest"))

    part_in, parts_in = _exchange_wait(*in_flight["in"], dx, name="grad_exchange_wait_in")
    g_in_half = _sum_parts(parts_in, part_in, name="grad_sum_chips_in")
    g_w_in = _join_halves(g_in_half, name="grad_join_halves_in")[:, :IN_SHARD]

    small_vals = [g[k] for k in SMALL_ORDER] + [loss_part[:, 0:1]]
    small_sum = _sum_leading(_gather_small(_pack_small(small_vals, SMALL_ROWS), name="gather_small_grads"), name="small_sum")
    sg = dict(zip(SMALL_ORDER + ("loss",), _unpack_small(small_sum, SMALL_SIZES + (1,))))
    loss = sg["loss"].reshape(())
    g_conv_full = sg["conv_w"].reshape(4, CONV_CH)
    g_conv = lax.dynamic_slice_in_dim(g_conv_full, chip * cw, cw, axis=1)

    grads = dict(norm_mix_w=sg["norm_mix_w"].reshape(1, -1), w_in=g_w_in[None], conv_w=g_conv[None],
                 conv_b=sg["conv_b"].reshape(1, -1), dt_bias=sg["dt_bias"].reshape(1, -1),
                 a_log=sg["a_log"].reshape(1, -1), d_skip=sg["d_skip"].reshape(1, -1),
                 ssd_norm_w=sg["ssd_norm_w"].reshape(1, -1), f_bias=sg["f_bias"].reshape(1, -1), w_out=g_w_out[None],
                 norm_mlp_w=sg["norm_mlp_w"].reshape(1, -1), w_up=g_w_up[None], w_down=g_w_down[None],
                 norm_final_w=sg["norm_final_w"])
    weights = dict(norm_mix_w=norm_mix_w, w_in=w_in, conv_w=conv_w, conv_b=conv_b, dt_bias=dt_bias, a_log=a_log,
                   d_skip=d_skip, ssd_norm_w=ssd_norm_w, f_bias=f_bias, w_out=w_out, norm_mlp_w=norm_mlp_w,
                   w_up=w_up, w_down=w_down, norm_final_w=norm_final_w)
    ms = dict(norm_mix_w=m_norm_mix_w, w_in=m_w_in, conv_w=m_conv_w, conv_b=m_conv_b, dt_bias=m_dt_bias,
              a_log=m_a_log, d_skip=m_d_skip, ssd_norm_w=m_ssd_norm_w, f_bias=m_f_bias, w_out=m_w_out,
              norm_mlp_w=m_norm_mlp_w, w_up=m_w_up, w_down=m_w_down, norm_final_w=m_norm_final_w)
    vs = dict(norm_mix_w=v_norm_mix_w, w_in=v_w_in, conv_w=v_conv_w, conv_b=v_conv_b, dt_bias=v_dt_bias,
              a_log=v_a_log, d_skip=v_d_skip, ssd_norm_w=v_ssd_norm_w, f_bias=v_f_bias, w_out=v_w_out,
              norm_mlp_w=v_norm_mlp_w, w_up=v_w_up, w_down=v_w_down, norm_final_w=v_norm_final_w)
    names = list(weights)
    big = ("w_in", "w_out", "w_up", "w_down")
    delta, new_m, new_v = {}, {}, {}
    for k, g2 in zip(big[1:], (g_w_out, g_w_up, g_w_down)):
        delta[k], new_m[k], new_v[k] = _adamw(weights[k], g2, ms[k], vs[k], name="adamw_" + k)
    g_in_t = g_w_in.T
    outs_t = _adamw(w_in[0].T, g_in_t, m_w_in[0].T, v_w_in[0].T, name="adamw_w_in")
    delta["w_in"], new_m["w_in"], new_v["w_in"] = [o.T[None] for o in outs_t]
    grads["w_in"] = g_in_t.T[None]
    smalls = [k for k in names if k not in big]
    sizes = [math.prod(weights[k].shape) for k in smalls]
    rows = -(-sum(sizes) // 1024) * 8
    packs = [_pack_small([d[k] for k in smalls], rows) for d in (weights, grads, ms, vs)]
    outs = _adamw(*packs, name="adamw_small")
    for o, dst in zip(outs, (delta, new_m, new_v)):
        for k, val in zip(smalls, _unpack_small(o, sizes)):
            dst[k] = val.reshape(weights[k].shape)
    return (loss, dx, *[grads[k] for k in names], *[delta[k] for k in names], *[new_m[k] for k in names],
            *[new_v[k] for k in names])
```

```python
import functools
import math

import jax
import jax.numpy as jnp
from jax import lax
from jax.experimental import pallas as pl
from jax.experimental.pallas import tpu as pltpu

F32 = jnp.float32
BF16 = jnp.bfloat16
HIGHEST = lax.Precision.HIGHEST
MESH = pl.DeviceIdType.MESH

D_MODEL = 1024
HEAD_DIM = 64
SSD_WIDTH = 1024
SSD_STATE = 128
CONV_CH = 1536
CHUNK = 128
ATT_WIDTH = 1024
EPS = 1e-5
IN_WIDTH = 5664
PA_WIDTH = 2688
QKV_WIDTH = 3072
D_FF = 4096
ATT_BLOCK = 256
NEG = -1e30
LOG2E = 1.4426950408889634
VMEM_LIMIT = 48 * 1024 * 1024

ADAM_LR = 0.001
ADAM_B1 = 0.9
ADAM_B2 = 0.999
ADAM_EPS = 1e-08
ADAM_WD = 0.01
ADAM_STEP = 10

N_CHIPS = 4
SMALL_ROWS = 96


def _cparams(sem):
    return pltpu.CompilerParams(dimension_semantics=sem, vmem_limit_bytes=VMEM_LIMIT)


def _pick(n, cands):
    for c in cands:
        if n % c == 0:
            return c
    return n


MM_CHUNK = 512


def _mm(a, b, *, name, tiles, ta=False, tb=False, out_dtype=F32, res=None, a_act=None, epi_up=None, after=None,
        into=None):
    n_unread = (after is not None) + (into is not None and into[3] is not None)
    if ta:
        K, M = a.shape
    else:
        M, K = a.shape
    if tb:
        N, K2 = b.shape
    else:
        K2, N = b.shape
    assert K == K2, (a.shape, b.shape)
    tm, tn, tk = tiles
    assert M % tm == 0 and N % tn == 0 and K % tk == 0, (name, M, N, K, tiles)
    nk = K // tk
    dn = (((0 if ta else 1,), (1 if tb else 0,)), ((), ()))
    has_res = res is not None
    has_up = epi_up is not None
    cn = _pick(tn, (MM_CHUNK, 384, 256, 128))

    def prologue(av):
        if a_act == "relu2":
            r = jnp.maximum(av.astype(F32), 0.0)
            av = r * r
        return av.astype(BF16)

    def epilogue(out, res_v, up_v):
        if has_res:
            out = out + res_v.astype(F32)
        if has_up:
            out = out * (2.0 * jnp.maximum(up_v.astype(F32), 0.0))
        return out.astype(out_dtype)

    def body(*refs):
        a_ref, b_ref = refs[0], refs[1]
        i = 2
        res_ref = up_ref = None
        if has_res:
            res_ref = refs[i]
            i += 1
        if has_up:
            up_ref = refs[i]
            i += 1
        i += n_unread
        o_ref = refs[i]
        if nk == 1:
            av = prologue(a_ref[...])
            if len(o_ref.shape) == 3:
                out = lax.dot_general(av, b_ref[...].astype(BF16), dn, preferred_element_type=F32)
                out = epilogue(out, res_ref[...] if has_res else None, up_ref[...] if has_up else None)
                o_ref[...] = out.reshape(o_ref.shape)
                return
            for c in range(tn // cn):
                cs = slice(c * cn, (c + 1) * cn)
                bv = (b_ref[cs, :] if tb else b_ref[:, cs]).astype(BF16)
                out = lax.dot_general(av, bv, dn, preferred_element_type=F32)
                o_ref[:, cs] = epilogue(out, res_ref[:, cs] if has_res else None, up_ref[:, cs] if has_up else None)
            return
        acc_ref = refs[i + 1]
        k = pl.program_id(2)

        @pl.when(k == 0)
        def _():
            acc_ref[...] = jnp.zeros_like(acc_ref)

        acc_ref[...] += lax.dot_general(prologue(a_ref[...]), b_ref[...].astype(BF16), dn,
                                        preferred_element_type=F32)

        @pl.when(k == nk - 1)
        def _():
            out = epilogue(acc_ref[...], res_ref[...] if has_res else None, up_ref[...] if has_up else None)
            o_ref[...] = out.reshape(o_ref.shape)

    a_spec = pl.BlockSpec((tk, tm), lambda i, j, k: (k, i)) if ta else pl.BlockSpec((tm, tk), lambda i, j, k: (i, k))
    b_spec = pl.BlockSpec((tn, tk), lambda i, j, k: (j, k)) if tb else pl.BlockSpec((tk, tn), lambda i, j, k: (k, j))
    o_spec = pl.BlockSpec((tm, tn), lambda i, j, k: (i, j))
    ins, specs = [a, b], [a_spec, b_spec]
    if has_res:
        ins.append(res)
        specs.append(o_spec)
    if has_up:
        ins.append(epi_up)
        specs.append(o_spec)
    if after is not None:
        ins.append(after)
        specs.append(pl.BlockSpec(memory_space=pl.ANY))
    out_shape, out_spec, aliases = jax.ShapeDtypeStruct((M, N), out_dtype), o_spec, {}
    if into is not None:
        shape, block, index, buf = into
        out_shape, out_spec = jax.ShapeDtypeStruct(shape, out_dtype), pl.BlockSpec(block, index)
        if buf is not None:
            aliases = {len(ins): 0}
            ins.append(buf)
            specs.append(pl.BlockSpec(memory_space=pl.ANY))
    return pl.pallas_call(
        body, name=name, grid=(M // tm, N // tn, nk),
        in_specs=specs, out_specs=out_spec, out_shape=out_shape, input_output_aliases=aliases,
        scratch_shapes=[] if nk == 1 else [pltpu.VMEM((tm, tn), F32)],
        compiler_params=_cparams(("parallel", "parallel", "arbitrary")),
    )(*ins)


ROW_PARTS = 2


def _row_parts(tm):
    n = ROW_PARTS if tm % (16 * ROW_PARTS) == 0 else 1
    return [slice(p * tm // n, (p + 1) * tm // n) for p in range(n)]


def _rows_product(a_ref, b_ref, tb, a_act, rows=slice(None)):
    av = a_ref[rows, :]
    if a_act == "relu2":
        r = jnp.maximum(av.astype(F32), 0.0)
        av = r * r
    dn = (((1,), (1 if tb else 0,)), ((), ()))
    return lax.dot_general(av.astype(BF16), b_ref[...].astype(BF16), dn, preferred_element_type=F32)


def _norm_mm(x, w, b, after, *, name, tm):
    m, d = x.shape
    n = b.shape[1]
    cn = _pick(n, (MM_CHUNK, 384, 256, 128))

    def body(x_ref, w_ref, b_ref, after_ref, h_ref, r_ref, o_ref):
        xv = x_ref[...]
        rstd = lax.rsqrt(jnp.mean(xv * xv, axis=1, keepdims=True) + EPS)
        hv = (xv * rstd * w_ref[...]).astype(BF16)
        h_ref[...] = hv
        r_ref[...] = rstd
        for c in range(n // cn):
            cs = slice(c * cn, (c + 1) * cn)
            o_ref[:, cs] = jnp.dot(hv, b_ref[:, cs].astype(BF16), preferred_element_type=F32)

    row = pl.BlockSpec((tm, d), lambda i: (i, 0))
    return pl.pallas_call(
        body, name=name, grid=(m // tm,),
        in_specs=[row, pl.BlockSpec((1, d), lambda i: (0, 0)), pl.BlockSpec((d, n), lambda i: (0, 0)),
                  pl.BlockSpec(memory_space=pl.ANY)],
        out_specs=[row, pl.BlockSpec((tm, 1), lambda i: (i, 0)), pl.BlockSpec((tm, n), lambda i: (i, 0))],
        out_shape=[jax.ShapeDtypeStruct((m, d), BF16), jax.ShapeDtypeStruct((m, 1), F32),
                   jax.ShapeDtypeStruct((m, n), F32)],
        compiler_params=_cparams(("parallel",)),
    )(x, w, b, after)


def _mm_norm_fwd(a1, b1, a2, b2, res, w, *, name, tm):
    m, k1 = a1.shape
    k2 = a2.shape[1]
    d = b1.shape[1]

    def body(a1_ref, b1_ref, a2_ref, b2_ref, res_ref, w_ref, h_ref, y_ref, r_ref):
        hv = _rows_product(a1_ref, b1_ref, False, None) + _rows_product(a2_ref, b2_ref, False, None) + res_ref[...]
        rstd = lax.rsqrt(jnp.mean(hv * hv, axis=1, keepdims=True) + EPS)
        h_ref[...] = hv
        y_ref[...] = (hv * rstd * w_ref[...]).astype(BF16)
        r_ref[...] = rstd

    row = pl.BlockSpec((tm, d), lambda i: (i, 0))
    return pl.pallas_call(
        body, name=name, grid=(m // tm,),
        in_specs=[pl.BlockSpec((tm, k1), lambda i: (i, 0)), pl.BlockSpec((k1, d), lambda i: (0, 0)),
                  pl.BlockSpec((tm, k2), lambda i: (i, 0)), pl.BlockSpec((k2, d), lambda i: (0, 0)), row,
                  pl.BlockSpec((1, d), lambda i: (0, 0))],
        out_specs=[row, row, pl.BlockSpec((tm, 1), lambda i: (i, 0))],
        out_shape=[jax.ShapeDtypeStruct((m, d), F32), jax.ShapeDtypeStruct((m, d), BF16),
                   jax.ShapeDtypeStruct((m, 1), F32)],
        compiler_params=_cparams(("parallel",)),
    )(a1, b1, a2, b2, res, w)


def _mm_final(a, b, res, w, target, *, name, tm, a_act):
    m, k = a.shape
    d = b.shape[1]

    def body(a_ref, b_ref, res_ref, w_ref, t_ref, dh_ref, dhb_ref, loss_ref, dw_ref):
        @pl.when(pl.program_id(0) == 0)
        def _():
            loss_ref[...] = jnp.zeros_like(loss_ref)
            dw_ref[...] = jnp.zeros_like(dw_ref)

        wv = w_ref[...]
        for rows in _row_parts(tm):
            hv = _rows_product(a_ref, b_ref, False, a_act, rows) + res_ref[rows, :]
            rstd = lax.rsqrt(jnp.mean(hv * hv, axis=1, keepdims=True) + EPS)
            xhat = hv * rstd
            err = xhat * wv - t_ref[rows, :]
            loss_ref[...] += 0.5 * jnp.sum(jnp.mean(err * err, axis=1, keepdims=True), axis=0, keepdims=True)
            dy = err * (1.0 / d)
            gw = dy * wv
            dh = rstd * (gw - xhat * jnp.mean(gw * xhat, axis=1, keepdims=True))
            dh_ref[rows, :] = dh
            dhb_ref[rows, :] = dh.astype(BF16)
            dw_ref[...] += jnp.sum(dy * xhat, axis=0, keepdims=True)

    row = pl.BlockSpec((tm, d), lambda i: (i, 0))
    vec = pl.BlockSpec((1, d), lambda i: (0, 0))
    return pl.pallas_call(
        body, name=name, grid=(m // tm,),
        in_specs=[pl.BlockSpec((tm, k), lambda i: (i, 0)), pl.BlockSpec((k, d), lambda i: (0, 0)), row, vec, row],
        out_specs=[row, row, pl.BlockSpec((1, 128), lambda i: (0, 0)), vec],
        out_shape=[jax.ShapeDtypeStruct((m, d), F32), jax.ShapeDtypeStruct((m, d), BF16),
                   jax.ShapeDtypeStruct((1, 128), F32), jax.ShapeDtypeStruct((1, d), F32)],
        compiler_params=_cparams(("arbitrary",)),
    )(a, b, res, w, target)


def _mm_two_halves(a, b, *, name, tm):
    m, k = a.shape
    d = b.shape[0] // 2

    def body(a_ref, b_ref, lo_ref, hi_ref):
        av = a_ref[...].astype(BF16)
        lo_ref[...] = lax.dot_general(av, b_ref[0:d, :].astype(BF16), NT_DIMS, preferred_element_type=F32)
        hi_ref[...] = lax.dot_general(av, b_ref[d:2 * d, :].astype(BF16), NT_DIMS,
                                      preferred_element_type=F32).astype(BF16)

    row = pl.BlockSpec((tm, d), lambda i: (i, 0))
    return pl.pallas_call(
        body, name=name, grid=(m // tm,),
        in_specs=[pl.BlockSpec((tm, k), lambda i: (i, 0)), pl.BlockSpec((2 * d, k), lambda i: (0, 0))],
        out_specs=[row, row],
        out_shape=[jax.ShapeDtypeStruct((m, d), F32), jax.ShapeDtypeStruct((m, d), BF16)],
        compiler_params=_cparams(("parallel",)),
    )(a, b)


def _mm_norm_bwd(pairs, x, rstd, w, dres, *, name, tm, after=None):
    m = pairs[0][0].shape[0]
    d = pairs[0][1].shape[0]
    n_pairs = len(pairs)

    def body(*refs):
        i = 2 * n_pairs
        x_ref, r_ref, w_ref, d_ref = refs[i:i + 4]
        dx_ref, dxb_ref, dw_ref = refs[-3:]

        @pl.when(pl.program_id(0) == 0)
        def _():
            dw_ref[...] = jnp.zeros_like(dw_ref)

        for rows in _row_parts(tm):
            g = _rows_product(refs[0], refs[1], True, None, rows)
            for p in range(1, n_pairs):
                g = g + _rows_product(refs[2 * p], refs[2 * p + 1], True, None, rows)
            r = r_ref[rows, :]
            xhat = x_ref[rows, :] * r
            gw = g * w_ref[...]
            dx = d_ref[rows, :] + r * (gw - xhat * jnp.mean(gw * xhat, axis=1, keepdims=True))
            dx_ref[rows, :] = dx
            dxb_ref[rows, :] = dx.astype(BF16)
            dw_ref[...] += jnp.sum(g * xhat, axis=0, keepdims=True)

    row = pl.BlockSpec((tm, d), lambda i: (i, 0))
    vec = pl.BlockSpec((1, d), lambda i: (0, 0))
    ins, specs = [], []
    for a, b in pairs:
        k = a.shape[1]
        ins += [a, b]
        specs += [pl.BlockSpec((tm, k), lambda i: (i, 0)), pl.BlockSpec((d, k), lambda i: (0, 0))]
    ins += [x, rstd, w, dres]
    specs += [row, pl.BlockSpec((tm, 1), lambda i: (i, 0)), vec, row]
    if after is not None:
        ins.append(after)
        specs.append(pl.BlockSpec(memory_space=pl.ANY))
    return pl.pallas_call(
        body, name=name, grid=(m // tm,), in_specs=specs, out_specs=[row, row, vec],
        out_shape=[jax.ShapeDtypeStruct((m, d), F32), jax.ShapeDtypeStruct((m, d), BF16),
                   jax.ShapeDtypeStruct((1, d), F32)],
        compiler_params=_cparams(("arbitrary",)),
    )(*ins)


def _softplus(x):
    return jnp.maximum(x, 0.0) + jnp.log(1.0 + jnp.exp(-jnp.abs(x)))


def _prep(proj_a, bias128, alog128, bl, t):
    n = bl * t
    nch = t // CHUNK
    col0 = (SSD_WIDTH + CONV_CH) // 128

    def body(p_ref, b_ref, al_ref, dt_ref, gd_ref, ac_ref, act_ref, negc_ref):
        negc_ref[...] = jnp.zeros_like(negc_ref)
        row = lax.broadcasted_iota(jnp.int32, (CHUNK, CHUNK), 0)
        col = lax.broadcasted_iota(jnp.int32, (CHUNK, CHUNK), 1)
        tril = (row >= col).astype(F32)
        lane = lax.broadcasted_iota(jnp.int32, (1, 128), 1)
        head_lanes = lane < 16
        a_row = -jnp.exp(al_ref[...])
        carry = jnp.zeros((1, 128), F32)
        for ci in range(nch):
            rows = slice(ci * CHUNK, (ci + 1) * CHUNK)
            xv = p_ref[rows, :] + b_ref[...]
            sp = _softplus(xv)
            acum = jnp.dot(tril, a_row * sp, precision=HIGHEST, preferred_element_type=F32)
            c = jnp.dot(tril, -_softplus(-xv), precision=HIGHEST, preferred_element_type=F32) + carry
            carry = c[CHUNK - 1:CHUNK, :]
            dt_ref[rows, :] = jnp.where(head_lanes, sp, 0.0)
            gd_ref[rows, :] = jnp.where(head_lanes, jax.nn.sigmoid(xv),
                                        jnp.where(lane < 32, jax.nn.sigmoid(-xv), 0.0))
            ac_ref[rows, :] = jnp.where(head_lanes, acum, 0.0)
            act_ref[:, rows] = jnp.transpose(acum)[0:16, :]
            c_t = jnp.transpose(c)
            for hp in range(8):
                negc_ref[hp, 0:2, rows] = -c_t[16 + 2 * hp:18 + 2 * hp, :]

    o128 = pl.BlockSpec((t, 128), lambda b: (b, 0))
    v128 = pl.BlockSpec((1, 128), lambda b: (0, 0))
    w128 = jax.ShapeDtypeStruct((n, 128), F32)
    return pl.pallas_call(
        body, name="head_scalars", grid=(bl,),
        in_specs=[pl.BlockSpec((t, 128), lambda b: (b, col0)), v128, v128],
        out_specs=[o128, o128, o128, pl.BlockSpec((16, t), lambda b: (0, b)),
                   pl.BlockSpec((None, 8, 8, t), lambda b: (b, 0, 0, 0))],
        out_shape=[w128, w128, w128, jax.ShapeDtypeStruct((16, n), F32),
                   jax.ShapeDtypeStruct((bl, 8, 8, t), F32)],
        compiler_params=_cparams(("parallel",)),
    )(proj_a, bias128, alog128)


def _fpost(dc, gate_d, ddt, dpa, bl, t):
    n = bl * t
    nch = t // CHUNK
    col0 = (SSD_WIDTH + CONV_CH) // 128

    def body(dc_ref, gd_ref, ddt_ref, dpa_in, out_ref, db_ref):
        @pl.when(pl.program_id(0) == 0)
        def _():
            db_ref[...] = jnp.zeros_like(db_ref)

        row = lax.broadcasted_iota(jnp.int32, (CHUNK, CHUNK), 0)
        col = lax.broadcasted_iota(jnp.int32, (CHUNK, CHUNK), 1)
        triu = (row <= col).astype(F32)
        lane = lax.broadcasted_iota(jnp.int32, (1, 128), 1)
        gate_lanes = (lane >= 16) & (lane < 32)
        carry = jnp.zeros((1, 128), F32)
        db = jnp.zeros((1, 128), F32)
        for ci in reversed(range(nch)):
            rows = slice(ci * CHUNK, (ci + 1) * CHUNK)
            dlf = jnp.dot(triu, dc_ref[rows, :], precision=HIGHEST, preferred_element_type=F32) + carry
            carry = dlf[0:1, :]
            df = jnp.where(gate_lanes, dlf * gd_ref[rows, :], 0.0)
            out_ref[rows, :] = (ddt_ref[rows, :] + df).astype(BF16)
            db = db + jnp.sum(df, axis=0, keepdims=True)
        db_ref[...] += db[:, 16:32]

    blk = pl.BlockSpec((t, 128), lambda b: (b, 0))
    return pl.pallas_call(
        body, name="forget_gate_bwd", grid=(bl,),
        in_specs=[blk, blk, blk, ANY],
        out_specs=[pl.BlockSpec((t, 128), lambda b: (b, col0)), pl.BlockSpec((1, 16), lambda b: (0, 0))],
        out_shape=[jax.ShapeDtypeStruct(dpa.shape, dpa.dtype), jax.ShapeDtypeStruct((1, 16), F32)],
        input_output_aliases={3: 0},
        compiler_params=_cparams(("arbitrary",)),
    )(dc, gate_d, ddt, dpa)


CONV_TILE = 256
CONV_ROWS = 256


def _conv_taps(u_ref, i):
    r0 = pl.multiple_of(i * CONV_ROWS, CONV_ROWS)
    cur = u_ref[pl.ds(r0, CONV_ROWS), :]
    p0 = pl.multiple_of(jnp.maximum(r0 - 8, 0), 8)
    prev = jnp.where(i > 0, u_ref[pl.ds(p0, 8), :], 0.0)
    cat = jnp.concatenate([prev, cur], axis=0)
    return r0, [cur] + [pltpu.roll(cat, s, 0)[8:, :] for s in (1, 2, 3)]


def _conv_fwd(proj_a, conv_w, conv_b, bl, t):
    n = bl * t
    nct = CONV_CH // CONV_TILE
    c0 = SSD_WIDTH // CONV_TILE

    def body(u_ref, w_ref, b_ref, o_ref, d_ref):
        w = w_ref[...]
        bias = b_ref[...]

        def chunk(i, carry):
            r0, taps = _conv_taps(u_ref, i)
            pre = bias + w[3:4, :] * taps[0]
            for s in (1, 2, 3):
                pre = pre + w[3 - s:4 - s, :] * taps[s]
            sg = jax.nn.sigmoid(pre)
            o_ref[pl.ds(r0, CONV_ROWS), :] = pre * sg
            d_ref[pl.ds(r0, CONV_ROWS), :] = (sg * (1.0 + pre * (1.0 - sg))).astype(BF16)
            return carry

        lax.fori_loop(0, t // CONV_ROWS, chunk, 0)

    out = pl.BlockSpec((t, CONV_TILE), lambda b, c: (b, c))
    return pl.pallas_call(
        body, name="conv_silu_fwd", grid=(bl, nct),
        in_specs=[pl.BlockSpec((t, CONV_TILE), lambda b, c: (b, c0 + c)),
                  pl.BlockSpec((4, CONV_TILE), lambda b, c: (0, c)),
                  pl.BlockSpec((1, CONV_TILE), lambda b, c: (0, c))],
        out_specs=[out, out],
        out_shape=[jax.ShapeDtypeStruct((n, CONV_CH), F32), jax.ShapeDtypeStruct((n, CONV_CH), BF16)],
        compiler_params=_cparams(("parallel", "parallel")),
    )(proj_a, conv_w, conv_b)


def _conv_bwd(dxc, dsilu, proj_a, conv_w, dpa, bl, t):
    nct = CONV_CH // CONV_TILE
    c0 = SSD_WIDTH // CONV_TILE
    nrc = t // CONV_ROWS

    def body(g_ref, s_ref, u_ref, w_ref, dpa_in, du_ref, dw_ref, db_ref, dp_scr):
        @pl.when(pl.program_id(1) == 0)
        def _():
            dw_ref[...] = jnp.zeros_like(dw_ref)
            db_ref[...] = jnp.zeros_like(db_ref)

        w = w_ref[...]
        dp_scr[pl.ds(t, 8), :] = jnp.zeros((8, CONV_TILE), F32)

        def chunk1(i, carry):
            dw0, dw1, dw2, dw3, db = carry
            r0, taps = _conv_taps(u_ref, i)
            dpre = g_ref[pl.ds(r0, CONV_ROWS), :] * s_ref[pl.ds(r0, CONV_ROWS), :].astype(F32)
            dp_scr[pl.ds(r0, CONV_ROWS), :] = dpre
            dw3 = dw3 + jnp.sum(dpre * taps[0], axis=0, keepdims=True)
            dw2 = dw2 + jnp.sum(dpre * taps[1], axis=0, keepdims=True)
            dw1 = dw1 + jnp.sum(dpre * taps[2], axis=0, keepdims=True)
            dw0 = dw0 + jnp.sum(dpre * taps[3], axis=0, keepdims=True)
            db = db + jnp.sum(dpre, axis=0, keepdims=True)
            return dw0, dw1, dw2, dw3, db

        z = jnp.zeros((1, CONV_TILE), F32)
        dw0, dw1, dw2, dw3, db = lax.fori_loop(0, nrc, chunk1, (z, z, z, z, z))
        dw_ref[...] += jnp.concatenate([dw0, dw1, dw2, dw3], axis=0)
        db_ref[...] += db

        def chunk2(i, carry):
            r0 = pl.multiple_of(i * CONV_ROWS, CONV_ROWS)
            cat = dp_scr[pl.ds(r0, CONV_ROWS + 8), :]
            du = w[3:4, :] * cat[:CONV_ROWS, :]
            for s in (1, 2, 3):
                du = du + w[3 - s:4 - s, :] * pltpu.roll(cat, CONV_ROWS + 8 - s, 0)[:CONV_ROWS, :]
            du_ref[pl.ds(r0, CONV_ROWS), :] = du.astype(BF16)
            return carry

        lax.fori_loop(0, nrc, chunk2, 0)

    tile = pl.BlockSpec((t, CONV_TILE), lambda c, b: (b, c))
    return pl.pallas_call(
        body, name="conv_silu_bwd", grid=(nct, bl),
        in_specs=[tile, tile, pl.BlockSpec((t, CONV_TILE), lambda c, b: (b, c0 + c)),
                  pl.BlockSpec((4, CONV_TILE), lambda c, b: (0, c)), ANY],
        out_specs=[pl.BlockSpec((t, CONV_TILE), lambda c, b: (b, c0 + c)),
                   pl.BlockSpec((4, CONV_TILE), lambda c, b: (0, c)),
                   pl.BlockSpec((1, CONV_TILE), lambda c, b: (0, c))],
        out_shape=[jax.ShapeDtypeStruct(dpa.shape, dpa.dtype), jax.ShapeDtypeStruct((4, CONV_CH), F32),
                   jax.ShapeDtypeStruct((1, CONV_CH), F32)],
        input_output_aliases={4: 0},
        scratch_shapes=[pltpu.VMEM((t + 8, CONV_TILE), F32)],
        compiler_params=_cparams(("parallel", "arbitrary")),
    )(dxc, dsilu, proj_a, conv_w, dpa)


SSD_FWD_CHUNKS = 4
SSD_BWD_CHUNKS = 1
NT_DIMS = (((1,), (1,)), ((), ()))
TN_DIMS = (((0,), (0,)), ((), ()))


def _dot(a, b, dims=None):
    if dims is None:
        return jnp.dot(a, b, preferred_element_type=F32)
    return lax.dot_general(a, b, dims, preferred_element_type=F32)


def _head_expander():
    r = lax.broadcasted_iota(jnp.int32, (128, SSD_WIDTH), 0)
    c = lax.broadcasted_iota(jnp.int32, (128, SSD_WIDTH), 1)
    return ((c // HEAD_DIM == r % 16) & (r < 48)).astype(BF16)


def _spread(v128, expander):
    hi = v128.astype(BF16).astype(F32)
    r1 = v128 - hi
    mid = r1.astype(BF16).astype(F32)
    lo = (r1 - mid).astype(BF16).astype(F32)
    packed = (hi + pltpu.roll(mid, 16, 1) + pltpu.roll(lo, 32, 1)).astype(BF16)
    return jnp.dot(packed, expander, preferred_element_type=F32)


def _head_sums(v1024, expander):
    hi = v1024.astype(BF16)
    lo = (v1024 - hi.astype(F32)).astype(BF16)
    heads = jnp.where(lax.broadcasted_iota(jnp.int32, expander.shape, 0) < 16, expander, jnp.zeros_like(expander))
    return _dot(hi, heads, NT_DIMS) + _dot(lo, heads, NT_DIMS)


def _ssd_fwd(xc, proj_a, dt, acum, acum_t, dskip_e, norm_w, bl, t):
    n = bl * t
    nch = t // CHUNK
    L = CHUNK

    def body(xc_blk, z_blk, dt_blk, ac_blk, act_blk, dsk_ref, nw_ref, ys_blk, yp_blk, hp_blk, h_scr, y_scr, x_scr):
        @pl.when(pl.program_id(1) == 0)
        def _():
            h_scr[...] = jnp.zeros_like(h_scr)

        for sub in range(SSD_FWD_CHUNKS):
            rows = pl.ds(sub * L, L)
            chunk(xc_blk.at[rows, :], z_blk.at[rows, :], dt_blk.at[rows, :], ac_blk.at[rows, :], act_blk.at[:, rows],
                  dsk_ref, nw_ref, ys_blk.at[rows, :], yp_blk.at[rows, :], hp_blk.at[sub], h_scr, y_scr, x_scr)

    def chunk(xc_ref, z_ref, dt_ref, ac_ref, act_ref, dsk_ref, nw_ref, ys_ref, yp_ref, hp_ref, h_scr, y_scr, x_scr):
        row = lax.broadcasted_iota(jnp.int32, (L, L), 0)
        col = lax.broadcasted_iota(jnp.int32, (L, L), 1)
        causal = row >= col
        lane128 = lax.broadcasted_iota(jnp.int32, (1, L), 1)
        expander = _head_expander()
        ac_all = ac_ref[...]
        act_all = act_ref[...]
        ac_e = _spread(ac_all, expander)
        e_in = jnp.exp(ac_e)
        dec = jnp.exp(ac_e[L - 1:L, :] - ac_e)
        xs_all = xc_ref[:, 0:SSD_WIDTH]
        x_all = xs_all * _spread(dt_ref[...], expander)
        x_scr[...] = x_all.astype(BF16)
        hp_all = h_scr[...]
        hp_ref[...] = hp_all
        for g in range(2):
            gs = slice(g * 512, (g + 1) * 512)
            bg = xc_ref[:, SSD_WIDTH + g * 128:SSD_WIDTH + (g + 1) * 128].astype(BF16)
            cg = xc_ref[:, SSD_WIDTH + 256 + g * 128:SSD_WIDTH + 256 + (g + 1) * 128].astype(BF16)
            gmat = _dot(cg, bg, NT_DIMS)
            y_off = _dot(cg, hp_all[gs, :].astype(BF16), NT_DIMS) * e_in[:, gs] + dsk_ref[:, gs] * xs_all[:, gs]
            s_new = _dot((x_all[:, gs] * dec[:, gs]).astype(BF16), bg, TN_DIMS)
            for pr in range(4):
                pair = slice((g * 4 + pr) * 128, (g * 4 + pr + 1) * 128)
                x_pair = x_scr[:, pair]
                y_pair = y_off[:, pr * 128:(pr + 1) * 128]
                for j in range(2):
                    h = g * 8 + 2 * pr + j
                    sl = slice(h * HEAD_DIM, (h + 1) * HEAD_DIM)
                    r = 2 * pr + j
                    ldec = jnp.exp(jnp.where(causal, ac_all[:, h:h + 1] - act_all[h:h + 1, :], NEG))
                    x_head = jnp.where((lane128 < HEAD_DIM) == (j == 0), x_pair, jnp.zeros_like(x_pair))
                    y_pair = y_pair + _dot((gmat * ldec).astype(BF16), x_head)
                    elast = jnp.exp(ac_all[L - 1:L, h:h + 1])
                    h_scr[sl, :] = elast * hp_all[sl, :] + s_new[r * HEAD_DIM:(r + 1) * HEAD_DIM, :]
                y_scr[:, pair] = y_pair
        y = y_scr[...]
        yp_ref[...] = y
        zv = z_ref[...]
        yg = y * (zv * jax.nn.sigmoid(zv))
        for g in range(2):
            gs = slice(g * 512, (g + 1) * 512)
            grp = yg[:, gs]
            rstd = lax.rsqrt(jnp.mean(grp * grp, axis=1, keepdims=True) + EPS)
            ys_ref[:, gs] = (grp * rstd * nw_ref[:, gs]).astype(BF16)

    cps = SSD_FWD_CHUNKS
    steps = nch // cps
    rb = lambda b, c: (b * steps + c, 0)
    v1k = pl.BlockSpec((1, SSD_WIDTH), lambda b, c: (0, 0))
    return pl.pallas_call(
        body, name="ssd_fwd", grid=(bl, steps),
        in_specs=[pl.BlockSpec((cps * L, CONV_CH), rb), pl.BlockSpec((cps * L, SSD_WIDTH), rb),
                  pl.BlockSpec((cps * L, 128), rb), pl.BlockSpec((cps * L, 128), rb),
                  pl.BlockSpec((16, cps * L), lambda b, c: (0, b * steps + c)), v1k, v1k],
        out_specs=[pl.BlockSpec((cps * L, SSD_WIDTH), rb), pl.BlockSpec((cps * L, SSD_WIDTH), rb),
                   pl.BlockSpec((cps, SSD_WIDTH, SSD_STATE), lambda b, c: (b * steps + c, 0, 0))],
        out_shape=[jax.ShapeDtypeStruct((n, SSD_WIDTH), BF16), jax.ShapeDtypeStruct((n, SSD_WIDTH), F32),
                   jax.ShapeDtypeStruct((bl * nch, SSD_WIDTH, SSD_STATE), F32)],
        scratch_shapes=[pltpu.VMEM((SSD_WIDTH, SSD_STATE), F32), pltpu.VMEM((L, SSD_WIDTH), F32),
                        pltpu.VMEM((L, SSD_WIDTH), BF16)],
        compiler_params=_cparams(("parallel", "arbitrary")),
    )(xc, proj_a, dt, acum, acum_t, dskip_e, norm_w)


def _ssd_bwd(dys, xc, proj_a, ypre, hprev, dt, gate_d, acum, acum_t, alog128, dskip_e, norm_w, bl, t):
    n = bl * t
    nch = t // CHUNK
    L = CHUNK

    def body(dys_blk, xc_blk, z_blk, yp_blk, hp_blk, dt_blk, gd_blk, ac_blk, act_blk, al_ref, dsk_ref, nw_ref,
             dxc_blk, dz_blk, ddt_blk, dnw_ref, dsk16_ref, da16_ref, db16_ref,
             dh_scr, dy_scr, x_scr, dx_scr, red_scr):
        first = (pl.program_id(0) == 0) & (pl.program_id(1) == 0)

        @pl.when(first)
        def _():
            dnw_ref[...] = jnp.zeros_like(dnw_ref)
            dsk16_ref[...] = jnp.zeros_like(dsk16_ref)
            da16_ref[...] = jnp.zeros_like(da16_ref)
            db16_ref[...] = jnp.zeros_like(db16_ref)

        @pl.when(pl.program_id(1) == 0)
        def _():
            dh_scr[...] = jnp.zeros_like(dh_scr)

        for sub in reversed(range(SSD_BWD_CHUNKS)):
            rows = pl.ds(sub * L, L)
            chunk(dys_blk.at[rows, :], xc_blk.at[rows, :], z_blk.at[rows, :], yp_blk.at[rows, :], hp_blk.at[sub],
                  dt_blk.at[rows, :], gd_blk.at[rows, :], ac_blk.at[rows, :], act_blk.at[:, rows], al_ref, dsk_ref,
                  nw_ref, dxc_blk.at[rows, :], dz_blk.at[rows, :], ddt_blk.at[rows, :], dnw_ref, dsk16_ref, da16_ref,
                  db16_ref, dh_scr, dy_scr, x_scr, dx_scr, red_scr)

    def chunk(dys_ref, xc_ref, z_ref, yp_ref, hp_ref, dt_ref, gd_ref, ac_ref, act_ref, al_ref, dsk_ref, nw_ref,
              dxc_ref, dz_ref, ddt_ref, dnw_ref, dsk16_ref, da16_ref, db16_ref,
              dh_scr, dy_scr, x_scr, dx_scr, red_scr):
        y = yp_ref[...]
        zv = z_ref[...]
        sz = jax.nn.sigmoid(zv)
        gate = zv * sz
        yg = y * gate
        dout = dys_ref[...]
        nw = nw_ref[...]
        for g in range(2):
            gs = slice(g * 512, (g + 1) * 512)
            grp = yg[:, gs]
            rstd = lax.rsqrt(jnp.mean(grp * grp, axis=1, keepdims=True) + EPS)
            ghat = grp * rstd
            dnw_ref[:, gs] += jnp.sum(dout[:, gs] * ghat, axis=0, keepdims=True)
            gw = dout[:, gs] * nw[:, gs]
            dyg = rstd * (gw - ghat * jnp.mean(gw * ghat, axis=1, keepdims=True))
            dy_scr[:, gs] = dyg * gate[:, gs]
            dz_ref[:, gs] = (dyg * y[:, gs] * (sz[:, gs] * (1.0 + zv[:, gs] * (1.0 - sz[:, gs])))).astype(BF16)

        row = lax.broadcasted_iota(jnp.int32, (L, L), 0)
        col = lax.broadcasted_iota(jnp.int32, (L, L), 1)
        causal = row >= col
        lane128 = lax.broadcasted_iota(jnp.int32, (1, L), 1)
        rows128 = lax.broadcasted_iota(jnp.int32, (L, 1), 0)
        last_row = rows128 == (L - 1)
        expander = _head_expander()
        ac_all = ac_ref[...]
        act_all = act_ref[...]
        dt_all = dt_ref[...]
        dt_e = _spread(dt_all, expander)
        ac_e = _spread(ac_all, expander)
        e_in = jnp.exp(ac_e)
        dec = jnp.exp(ac_e[L - 1:L, :] - ac_e)
        xs_all = xc_ref[:, 0:SSD_WIDTH]
        x_all = xs_all * dt_e
        x_scr[...] = x_all.astype(BF16)
        dy_all = dy_scr[...]
        hp_all = hp_ref[...]
        ds_all = dh_scr[...]
        dsk_cols = jnp.sum(dy_all * xs_all, axis=0, keepdims=True)
        dac = jnp.zeros((L, L), F32)
        dac_row = jnp.zeros((L, L), F32)
        ddec_cols = []
        for g in range(2):
            gs = slice(g * 512, (g + 1) * 512)
            bsl = slice(SSD_WIDTH + g * 128, SSD_WIDTH + (g + 1) * 128)
            csl = slice(SSD_WIDTH + 256 + g * 128, SSD_WIDTH + 256 + (g + 1) * 128)
            bg = xc_ref[:, bsl].astype(BF16)
            cg = xc_ref[:, csl].astype(BF16)
            gmat = _dot(cg, bg, NT_DIMS)
            hpb = hp_all[gs, :].astype(BF16)
            dsb = ds_all[gs, :].astype(BF16)
            ch = _dot(cg, hpb, NT_DIMS)
            dye = dy_all[:, gs] * e_in[:, gs]
            dyeb = dye.astype(BF16)
            dc_acc = _dot(dyeb, hpb)
            dhp = _dot(dyeb, cg, TN_DIMS)
            dxd = _dot(bg, dsb, NT_DIMS)
            db_acc = _dot((x_all[:, gs] * dec[:, gs]).astype(BF16), dsb)
            ddec = dxd * x_all[:, gs] * dec[:, gs]
            ddec_cols.append(jnp.sum(ddec, axis=0, keepdims=True))
            dx_inter = dxd * dec[:, gs]
            red_scr[:, gs] = dye * ch - ddec
            dg_sum = jnp.zeros((L, L), F32)
            for pr in range(4):
                pair = slice((g * 4 + pr) * 128, (g * 4 + pr + 1) * 128)
                x_pair = x_scr[:, pair]
                dy_pair = dy_scr[:, pair].astype(BF16)
                dx_pair = dx_inter[:, pr * 128:(pr + 1) * 128]
                for j in range(2):
                    h = g * 8 + 2 * pr + j
                    r = 2 * pr + j
                    sl = slice(h * HEAD_DIM, (h + 1) * HEAD_DIM)
                    onehot_w = lane128 == h
                    ldec = jnp.exp(jnp.where(causal, ac_all[:, h:h + 1] - act_all[h:h + 1, :], NEG))
                    mf = gmat * ldec
                    dyb = jnp.where((lane128 < HEAD_DIM) == (j == 0), dy_pair, jnp.zeros_like(dy_pair))
                    dm = _dot(dyb, x_pair, NT_DIMS)
                    dx_pair = dx_pair + _dot(mf.astype(BF16), dyb, TN_DIMS)
                    dg_sum = dg_sum + dm * ldec
                    wmat = dm * mf
                    elast = jnp.exp(ac_all[L - 1:L, h:h + 1])
                    hp_h = hp_all[sl, :]
                    ds_h = ds_all[sl, :]
                    extra = elast * jnp.sum(jnp.sum(hp_h * ds_h, axis=1, keepdims=True), axis=0, keepdims=True)
                    dac = dac + jnp.where(onehot_w,
                                          jnp.sum(wmat, axis=1, keepdims=True) + jnp.where(last_row, extra, 0.0), 0.0)
                    dac_row = dac_row + jnp.where(rows128 == h, -jnp.sum(wmat, axis=0, keepdims=True), 0.0)
                    dh_scr[sl, :] = elast * ds_h + dhp[r * HEAD_DIM:(r + 1) * HEAD_DIM, :]
                dx_scr[:, pair] = dx_pair
            dgb = dg_sum.astype(BF16)
            dxc_ref[:, csl] = dc_acc + _dot(dgb, bg)
            dxc_ref[:, bsl] = db_acc + _dot(dgb, cg, TN_DIMS)
        dx_all = dx_scr[...]
        dxc_ref[:, 0:SSD_WIDTH] = dx_all * dt_e + dsk_ref[...] * dy_all
        red = red_scr[...]
        dac_slab = _head_sums(red, expander)
        ddec_tot = _head_sums(jnp.broadcast_to(jnp.concatenate(ddec_cols, axis=1), (8, SSD_WIDTH)), expander)
        ddt_x = _head_sums(dx_all * xs_all, expander)
        dsk16_ref[...] += _head_sums(jnp.broadcast_to(dsk_cols, (8, SSD_WIDTH)), expander)[0:1, 0:16]
        dac = dac + dac_slab + jnp.transpose(dac_row) + jnp.where(last_row, ddec_tot[0:1, :], 0.0)
        triu = (row <= col).astype(F32)
        da = jnp.dot(triu, dac, precision=HIGHEST, preferred_element_type=F32)
        a_row = -jnp.exp(al_ref[...])
        ddt = jnp.where(lane128 < 16, (ddt_x + da * a_row) * gd_ref[...], 0.0)
        ddt_ref[...] = ddt
        da16_ref[...] += (jnp.sum(da * dt_all, axis=0, keepdims=True) * a_row)[:, 0:16]
        db16_ref[...] += jnp.sum(ddt, axis=0, keepdims=True)[:, 0:16]

    cps = SSD_BWD_CHUNKS
    steps = nch // cps
    rb = lambda b, c: (b * steps + steps - 1 - c, 0)
    v1k = pl.BlockSpec((1, SSD_WIDTH), lambda b, c: (0, 0))
    v16 = pl.BlockSpec((1, 16), lambda b, c: (0, 0))
    v128 = pl.BlockSpec((1, 128), lambda b, c: (0, 0))
    wide = pl.BlockSpec((cps * L, SSD_WIDTH), rb)
    s128 = pl.BlockSpec((cps * L, 128), rb)
    return pl.pallas_call(
        body, name="ssd_bwd", grid=(bl, steps),
        in_specs=[wide, pl.BlockSpec((cps * L, CONV_CH), rb), wide, wide,
                  pl.BlockSpec((cps, SSD_WIDTH, SSD_STATE), lambda b, c: (b * steps + steps - 1 - c, 0, 0)),
                  s128, s128, s128, pl.BlockSpec((16, cps * L), lambda b, c: (0, b * steps + steps - 1 - c)),
                  v128, v1k, v1k],
        out_specs=[pl.BlockSpec((cps * L, CONV_CH), rb), wide, s128, v1k, v16, v16, v16],
        out_shape=[jax.ShapeDtypeStruct((n, CONV_CH), F32), jax.ShapeDtypeStruct((n, PA_WIDTH), BF16),
                   jax.ShapeDtypeStruct((n, 128), F32), jax.ShapeDtypeStruct((1, SSD_WIDTH), F32),
                   jax.ShapeDtypeStruct((1, 16), F32), jax.ShapeDtypeStruct((1, 16), F32),
                   jax.ShapeDtypeStruct((1, 16), F32)],
        scratch_shapes=[pltpu.VMEM((SSD_WIDTH, SSD_STATE), F32), pltpu.VMEM((L, SSD_WIDTH), F32),
                        pltpu.VMEM((L, SSD_WIDTH), BF16), pltpu.VMEM((L, SSD_WIDTH), F32),
                        pltpu.VMEM((L, SSD_WIDTH), F32)],
        compiler_params=_cparams(("arbitrary", "arbitrary")),
    )(dys, xc, proj_a, ypre, hprev, dt, gate_d, acum, acum_t, alog128, dskip_e, norm_w)


def _attn_fwd(qkv, negc, bl, t):
    n = bl * t
    tb_ = ATT_BLOCK
    nb = t // tb_
    scale2 = LOG2E / math.sqrt(HEAD_DIM)

    def body(q_ref, k_ref, v_ref, c_ref, o_ref, lse_ref, v0_scr, v1_scr, k0_scr, k1_scr):
        row = lax.broadcasted_iota(jnp.int32, (tb_, tb_), 0)
        col = lax.broadcasted_iota(jnp.int32, (tb_, tb_), 1)
        causal = row >= col
        lane = lax.broadcasted_iota(jnp.int32, (1, 128), 1)
        v_pair = v_ref[...].astype(F32)
        k_pair = k_ref[...]
        v_scrs = (v0_scr, v1_scr)
        k_scrs = (k0_scr, k1_scr)
        for j in range(2):
            v_head = v_pair if j == 0 else pltpu.roll(v_pair, HEAD_DIM, 1)
            v_scrs[j][...] = jnp.where(lane < HEAD_DIM, v_head, jnp.where(lane == HEAD_DIM, 1.0, 0.0)).astype(BF16)
            k_scrs[j][...] = jnp.where((lane < HEAD_DIM) == (j == 0), k_pair, jnp.zeros_like(k_pair))
        for qi in range(nb):
            r0, lk = qi * tb_, (qi + 1) * tb_
            for j in range(2):
                sl = slice(j * HEAD_DIM, (j + 1) * HEAD_DIM)
                s = _dot(q_ref[r0:lk, :], k_scrs[j][0:lk, :], NT_DIMS) * scale2 + c_ref[j:j + 1, 0:lk] * LOG2E
                tail = jnp.where(causal, s[:, r0:lk], NEG)
                s = tail if qi == 0 else jnp.concatenate([s[:, 0:r0], tail], axis=1)
                m = jnp.max(s, axis=1, keepdims=True)
                p = jnp.exp2(s - m)
                acc = _dot(p.astype(BF16), v_scrs[j][0:lk, :])
                l = acc[:, HEAD_DIM:HEAD_DIM + 1]
                o_ref[r0:lk, sl] = (acc[:, 0:HEAD_DIM] / l).astype(BF16)
                lse_ref[r0:lk, sl] = jnp.broadcast_to(m + jnp.log(l) * LOG2E, (tb_, HEAD_DIM))

    blk = lambda off: pl.BlockSpec((t, 128), lambda b, hp: (b, off + hp))
    return pl.pallas_call(
        body, name="fox_attn_fwd", grid=(bl, 8),
        in_specs=[blk(0), blk(8), blk(16), pl.BlockSpec((None, None, 8, t), lambda b, hp: (b, hp, 0, 0))],
        out_specs=[blk(0), blk(0)],
        out_shape=[jax.ShapeDtypeStruct((n, ATT_WIDTH), BF16), jax.ShapeDtypeStruct((n, ATT_WIDTH), F32)],
        scratch_shapes=[pltpu.VMEM((t, 128), BF16)] * 4,
        compiler_params=_cparams(("parallel", "parallel")),
    )(qkv, qkv, qkv, negc)


def _attn_bwd(qkv, do, o, lse, negc, after, bl, t):
    n = bl * t
    tb_ = ATT_BLOCK
    nb = t // tb_
    scale = 1.0 / math.sqrt(HEAD_DIM)
    scale2 = LOG2E * scale

    def body(q_ref, k_ref, v_ref, do_ref, o_ref, lse_ref, c_ref, after_ref, dq_ref, dk_ref, dv_ref, dc_ref,
             dq0_scr, delta_scr, dq1_scr, qt0_scr, qt1_scr, dot_scr, dkt0_scr, dkt1_scr, dvt_scr):
        row = lax.broadcasted_iota(jnp.int32, (tb_, tb_), 0)
        col = lax.broadcasted_iota(jnp.int32, (tb_, tb_), 1)
        causal = row >= col
        lane = lax.broadcasted_iota(jnp.int32, (1, 128), 1)
        dq_scrs = (dq0_scr, dq1_scr)
        qt_scrs = (qt0_scr, qt1_scr)
        dkt_scrs = (dkt0_scr, dkt1_scr)
        dq0_scr[...] = jnp.zeros_like(dq0_scr)
        dq1_scr[...] = jnp.zeros_like(dq1_scr)
        dc_ref[...] = jnp.zeros_like(dc_ref)
        q_t = jnp.transpose(q_ref[...].astype(F32))
        ones_row = jnp.where(lax.broadcasted_iota(jnp.int32, (8, t), 0) == 0, 1.0, 0.0)
        for j in range(2):
            qt_scrs[j][...] = jnp.concatenate(
                [q_t[j * HEAD_DIM:(j + 1) * HEAD_DIM, :], ones_row, jnp.zeros((HEAD_DIM - 8, t), F32)],
                axis=0).astype(BF16)
        dot_scr[...] = jnp.transpose(do_ref[...].astype(F32)).astype(BF16)
        prod = do_ref[...].astype(F32) * o_ref[...].astype(F32)
        for j in range(2):
            sl = slice(j * HEAD_DIM, (j + 1) * HEAD_DIM)
            delta_scr[:, sl] = jnp.broadcast_to(jnp.sum(prod[:, sl], axis=1, keepdims=True), (t, HEAD_DIM))
        for kj in range(nb):
            r0, r1 = kj * tb_, (kj + 1) * tb_
            k_blk = k_ref[r0:r1, :]
            v_blk = v_ref[r0:r1, :]
            k_pair = k_blk.astype(F32)
            for j in range(2):
                sl = slice(j * HEAD_DIM, (j + 1) * HEAD_DIM)
                one = slice(j * HEAD_DIM, j * HEAD_DIM + 1)
                own = (lane < HEAD_DIM) == (j == 0)
                k_head = k_pair if j == 0 else pltpu.roll(k_pair, HEAD_DIM, 1)
                k_ones = jnp.where(lane < HEAD_DIM, k_head, jnp.where(lane == HEAD_DIM, 1.0, 0.0)).astype(BF16)
                s = (_dot(q_ref[r0:t, :], jnp.where(own, k_blk, jnp.zeros_like(k_blk)), NT_DIMS) * scale2
                     + c_ref[j:j + 1, r0:r1] * LOG2E)
                head = jnp.where(causal, s[0:tb_, :], NEG)
                s = head if kj == nb - 1 else jnp.concatenate([head, s[tb_:, :]], axis=0)
                p = jnp.exp2(s - lse_ref[r0:t, one])
                dp = _dot(do_ref[r0:t, :], jnp.where(own, v_blk, jnp.zeros_like(v_blk)), NT_DIMS)
                ds = p * (dp - delta_scr[r0:t, one])
                dsb = ds.astype(BF16)
                dvt_scr[sl, r0:r1] = _dot(dot_scr[sl, r0:t], p.astype(BF16))
                dkt_scrs[j][:, r0:r1] = _dot(qt_scrs[j][:, r0:t], dsb)
                dq_scrs[j][r0:t, :] += _dot(dsb, k_ones)
        dv_ref[...] = jnp.transpose(dvt_scr[...]).astype(BF16)
        for j in range(2):
            sl = slice(j * HEAD_DIM, (j + 1) * HEAD_DIM)
            acc = dq_scrs[j][...]
            dkt = dkt_scrs[j][...]
            dq_ref[:, sl] = (acc[:, 0:HEAD_DIM] * scale).astype(BF16)
            dk_ref[:, sl] = (jnp.transpose(dkt)[:, 0:HEAD_DIM] * scale).astype(BF16)
            dc_ref[j:j + 1, :] = jnp.transpose(acc)[HEAD_DIM:HEAD_DIM + 1, :] - dkt[HEAD_DIM:HEAD_DIM + 1, :]

    blk = lambda off: pl.BlockSpec((t, 128), lambda b, hp: (b, off + hp))
    cblk = pl.BlockSpec((None, None, 8, t), lambda b, hp: (b, hp, 0, 0))
    return pl.pallas_call(
        body, name="fox_attn_bwd", grid=(bl, 8),
        in_specs=[blk(0), blk(8), blk(16), blk(0), blk(0), blk(0), cblk, ANY],
        out_specs=[blk(0), blk(0), blk(0), cblk],
        out_shape=[jax.ShapeDtypeStruct((n, ATT_WIDTH), BF16)] * 3 + [jax.ShapeDtypeStruct((bl, 8, 8, t), F32)],
        scratch_shapes=[pltpu.VMEM((t, 128), F32), pltpu.VMEM((t, 128), F32), pltpu.VMEM((t, 128), F32),
                        pltpu.VMEM((128, t), BF16), pltpu.VMEM((128, t), BF16), pltpu.VMEM((128, t), BF16),
                        pltpu.VMEM((128, t), F32), pltpu.VMEM((128, t), F32), pltpu.VMEM((128, t), F32)],
        compiler_params=_cparams(("parallel", "parallel")),
    )(qkv, qkv, qkv, do, o, lse, negc, after)


def _adamw(w, g, m, v, *, name):
    lead = w.ndim == 3
    r, c = w.shape[-2:]
    tr = _pick(r, (256, IN_SHARD // 3, 128, 64, 32, 16, 8))
    bc1 = 1.0 - ADAM_B1 ** ADAM_STEP
    bc2 = 1.0 - ADAM_B2 ** ADAM_STEP

    def body(w_ref, g_ref, m_ref, v_ref, d_ref, nm_ref, nv_ref):
        gv = g_ref[...]
        mn = ADAM_B1 * m_ref[...] + (1.0 - ADAM_B1) * gv
        vn = ADAM_B2 * v_ref[...] + (1.0 - ADAM_B2) * (gv * gv)
        m_hat = mn / bc1
        v_hat = vn / bc2
        d_ref[...] = -ADAM_LR * (m_hat / (jnp.sqrt(v_hat) + ADAM_EPS) + ADAM_WD * w_ref[...])
        nm_ref[...] = mn
        nv_ref[...] = vn

    flat = pl.BlockSpec((tr, c), lambda i: (i, 0))
    blk = pl.BlockSpec((None, tr, c), lambda i: (0, i, 0)) if lead else flat
    return pl.pallas_call(
        body, name=name, grid=(r // tr,), in_specs=[blk, flat, blk, blk], out_specs=[blk] * 3,
        out_shape=[jax.ShapeDtypeStruct(w.shape, F32)] * 3,
        compiler_params=_cparams(("parallel",)),
    )(w, g, m, v)


def _sum_leading(parts, *, name, out_dtype=F32):
    k, r, c = parts.shape
    tr = _pick(r, (512, 256, 128, 96, 64, 32, 16, 8))

    def body(p_ref, o_ref):
        acc = p_ref[0].astype(F32)
        for i in range(1, k):
            acc = acc + p_ref[i].astype(F32)
        o_ref[...] = acc.astype(out_dtype)

    return pl.pallas_call(
        body, name=name, grid=(r // tr,),
        in_specs=[pl.BlockSpec((k, tr, c), lambda i: (0, i, 0))],
        out_specs=pl.BlockSpec((tr, c), lambda i: (i, 0)),
        out_shape=jax.ShapeDtypeStruct((r, c), out_dtype),
        compiler_params=_cparams(("parallel",)),
    )(parts)


def _add_my_half(g, b, *, name):
    k, r, c = b.shape
    tr = _pick(r, (512, 256, 128))
    nrt = r // tr

    def body(lo_ref, hi_ref, b_ref, o_ref):
        mine = jnp.where(lax.axis_index("c") == 0, lo_ref[...], hi_ref[...])
        o_ref[...] = (mine.astype(F32) + b_ref[...].astype(F32)).astype(BF16)

    blk = pl.BlockSpec((None, tr, c), lambda j, i: (j, i, 0))
    return pl.pallas_call(
        body, name=name, grid=(k, nrt),
        in_specs=[blk, pl.BlockSpec((None, tr, c), lambda j, i: (j, i + nrt, 0)), blk], out_specs=blk,
        out_shape=jax.ShapeDtypeStruct((k, r, c), BF16),
        compiler_params=_cparams(("parallel", "parallel")),
    )(g, g, b)


ANY = pl.BlockSpec(memory_space=pl.ANY)


def _chip_peers(x, y):
    return [(1 - x, y, 2 * (1 - x) + y), (x, 1 - y, 2 * x + 1 - y), (1 - x, 1 - y, 2 * (1 - x) + 1 - y)]


def _gather_weights(blob, *, name):
    rows, cols = blob.shape
    half_rows = rows // 2

    def body(b_ref, o_ref, send_sems, recv_sems):
        x, y, c = lax.axis_index("x"), lax.axis_index("y"), lax.axis_index("c")
        me = 2 * x + y
        sibling = (x, y, 1 - c)
        peers = _chip_peers(x, y)

        def half(chip, hc):
            return o_ref.at[chip, pl.ds(hc * half_rows, half_rows), :]

        def copy(k, src, chip, hc, to):
            return pltpu.make_async_remote_copy(src_ref=src, dst_ref=half(chip, hc), send_sem=send_sems.at[k],
                                                recv_sem=recv_sems.at[k], device_id=to, device_id_type=MESH)

        my_half = b_ref.at[pl.ds(c * half_rows, half_rows), :]
        first = [copy(k, my_half, me, c, (px, py, c)) for k, (px, py, _) in enumerate(peers)]
        own = pltpu.make_async_remote_copy(src_ref=b_ref, dst_ref=o_ref.at[me], send_sem=send_sems.at[6],
                                           recv_sem=recv_sems.at[6], device_id=sibling, device_id_type=MESH)
        for cp in first + [own]:
            cp.start()
        passed = [copy(3 + k, half(pc, c), pc, c, sibling) for k, (_, _, pc) in enumerate(peers)]
        for k, (px, py, pc) in enumerate(peers):
            copy(k, my_half, pc, c, (px, py, c)).wait_recv()
            passed[k].start()
        for k, (_, _, pc) in enumerate(peers):
            copy(3 + k, half(pc, 1 - c), pc, 1 - c, sibling).wait_recv()
        own.wait_recv()
        for cp in first + passed + [own]:
            cp.wait_send()

    return pl.pallas_call(
        body, name=name, in_specs=[ANY], out_specs=ANY,
        out_shape=jax.ShapeDtypeStruct((N_CHIPS, rows, cols), BF16),
        scratch_shapes=[pltpu.SemaphoreType.DMA((7,)), pltpu.SemaphoreType.DMA((7,))],
    )(blob)


def _swap_halves(g, *, name):
    _, rows, cols = g.shape
    half_rows = rows // 2

    def body(g_ref, o_ref, send_sem, recv_sem):
        x, y, c = lax.axis_index("x"), lax.axis_index("y"), lax.axis_index("c")
        cp = pltpu.make_async_remote_copy(
            src_ref=g_ref.at[:, pl.ds((1 - c) * half_rows, half_rows), :], dst_ref=o_ref,
            send_sem=send_sem, recv_sem=recv_sem, device_id=(x, y, 1 - c), device_id_type=MESH)
        cp.start()
        cp.wait()

    return pl.pallas_call(
        body, name=name, in_specs=[ANY], out_specs=ANY,
        out_shape=jax.ShapeDtypeStruct((N_CHIPS, half_rows, cols), BF16),
        scratch_shapes=[pltpu.SemaphoreType.DMA, pltpu.SemaphoreType.DMA],
    )(g)


HBM_SPEC = pl.BlockSpec(memory_space=pltpu.HBM)
SEM_SPEC = pl.BlockSpec(memory_space=pltpu.SEMAPHORE)
SPLIT_EFFECT = pltpu.SideEffectType.DATAFLOW_SIDE_EFFECTING


def _gather_peers_copies(b_ref, land_ref, send_sems, recv_sems, sending):
    x, y, c = lax.axis_index("x"), lax.axis_index("y"), lax.axis_index("c")
    me = 2 * x + y
    half_rows = b_ref.shape[0] // 2
    src = b_ref.at[pl.ds(c * half_rows, half_rows), :]
    return [pltpu.make_async_remote_copy(
        src_ref=src, dst_ref=land_ref.at[me if sending else pc, pl.ds(c * half_rows, half_rows), :],
        send_sem=send_sems.at[k], recv_sem=recv_sems.at[k], device_id=(px, py, c), device_id_type=MESH)
        for k, (px, py, pc) in enumerate(_chip_peers(x, y))]


def _gather_start(blob, after, *, name):
    shape = (N_CHIPS,) + blob.shape

    def body(b_ref, land_ref, after_ref, send_sems, recv_sems, b_thru, land_thru, token):
        for cp in _gather_peers_copies(b_ref, land_ref, send_sems, recv_sems, True):
            cp.start()
        token[...] = jnp.zeros_like(token)

    return pl.pallas_call(
        body, name=name,
        out_shape=(pltpu.SemaphoreType.DMA((3,)), pltpu.SemaphoreType.DMA((3,)), pltpu.HBM(blob.shape, blob.dtype),
                   pltpu.HBM(shape, blob.dtype), jax.ShapeDtypeStruct((8, 128), F32)),
        in_specs=(HBM_SPEC, HBM_SPEC, ANY),
        out_specs=(SEM_SPEC, SEM_SPEC, HBM_SPEC, HBM_SPEC, pl.BlockSpec(memory_space=pltpu.VMEM)),
        input_output_aliases={0: 2, 1: 3},
        compiler_params=pltpu.CompilerParams(has_side_effects=SPLIT_EFFECT),
    )(pltpu.with_memory_space_constraint(blob, pltpu.HBM),
      pltpu.with_memory_space_constraint(lax.empty(shape, blob.dtype), pltpu.HBM), after)


def _gather_wait(send_sems, recv_sems, b_thru, land_thru, after, *, name):
    def body(b_ref, land_ref, send_sems, recv_sems, after_ref, b_dead, got_ref):
        for cp in _gather_peers_copies(b_ref, land_ref, send_sems, recv_sems, False):
            cp.wait_send()
            cp.wait_recv()

    return pl.pallas_call(
        body, name=name,
        out_shape=(pltpu.HBM(b_thru.shape, b_thru.dtype), pltpu.HBM(land_thru.shape, land_thru.dtype)),
        in_specs=(HBM_SPEC, HBM_SPEC, SEM_SPEC, SEM_SPEC, ANY), out_specs=(HBM_SPEC, HBM_SPEC),
        input_output_aliases={0: 0, 1: 1},
        compiler_params=pltpu.CompilerParams(has_side_effects=SPLIT_EFFECT),
    )(b_thru, land_thru, send_sems, recv_sems, after)


def _gather_forward(land, blob, *, name):
    half_rows = land.shape[1] // 2

    def body(l_ref, b_ref, o_ref, send_sems, recv_sems):
        x, y, c = lax.axis_index("x"), lax.axis_index("y"), lax.axis_index("c")
        me = 2 * x + y
        sibling = (x, y, 1 - c)
        cps = []
        for k, (_, _, pc) in enumerate(_chip_peers(x, y)):
            mine = pl.ds(c * half_rows, half_rows)
            cps.append(pltpu.make_async_remote_copy(
                src_ref=l_ref.at[pc, mine, :], dst_ref=o_ref.at[pc, mine, :], send_sem=send_sems.at[k],
                recv_sem=recv_sems.at[k], device_id=sibling, device_id_type=MESH))
        cps.append(pltpu.make_async_remote_copy(src_ref=b_ref, dst_ref=o_ref.at[me], send_sem=send_sems.at[3],
                                                recv_sem=recv_sems.at[3], device_id=sibling, device_id_type=MESH))
        for cp in cps:
            cp.start()
        for k, (_, _, pc) in enumerate(_chip_peers(x, y)):
            theirs = pl.ds((1 - c) * half_rows, half_rows)
            pltpu.make_async_remote_copy(
                src_ref=l_ref.at[pc, theirs, :], dst_ref=o_ref.at[pc, theirs, :], send_sem=send_sems.at[k],
                recv_sem=recv_sems.at[k], device_id=sibling, device_id_type=MESH).wait_recv()
        cps[3].wait_recv()
        for cp in cps:
            cp.wait_send()

    return pl.pallas_call(
        body, name=name, in_specs=[ANY, ANY], out_specs=ANY, input_output_aliases={0: 0},
        out_shape=jax.ShapeDtypeStruct(land.shape, land.dtype),
        scratch_shapes=[pltpu.SemaphoreType.DMA((4,)), pltpu.SemaphoreType.DMA((4,))],
    )(land, blob)


def _exchange_peers_copies(p_ref, land_ref, send_sems, recv_sems, sending):
    x, y, c = lax.axis_index("x"), lax.axis_index("y"), lax.axis_index("c")
    me = 2 * x + y
    return [pltpu.make_async_remote_copy(src_ref=p_ref.at[pc], dst_ref=land_ref.at[me if sending else pc],
                                         send_sem=send_sems.at[k], recv_sem=recv_sems.at[k],
                                         device_id=(px, py, c), device_id_type=MESH)
            for k, (px, py, pc) in enumerate(_chip_peers(x, y))]


def _exchange_start(p, *, name):
    def body(p_ref, land_ref, send_sems, recv_sems, p_thru, land_thru, token):
        for cp in _exchange_peers_copies(p_ref, land_ref, send_sems, recv_sems, True):
            cp.start()
        token[...] = jnp.zeros_like(token)

    return pl.pallas_call(
        body, name=name,
        out_shape=(pltpu.SemaphoreType.DMA((3,)), pltpu.SemaphoreType.DMA((3,)), pltpu.HBM(p.shape, p.dtype),
                   pltpu.HBM(p.shape, p.dtype), jax.ShapeDtypeStruct((8, 128), F32)),
        in_specs=(HBM_SPEC, HBM_SPEC),
        out_specs=(SEM_SPEC, SEM_SPEC, HBM_SPEC, HBM_SPEC, pl.BlockSpec(memory_space=pltpu.VMEM)),
        input_output_aliases={0: 2, 1: 3},
        compiler_params=pltpu.CompilerParams(has_side_effects=SPLIT_EFFECT),
    )(pltpu.with_memory_space_constraint(p, pltpu.HBM),
      pltpu.with_memory_space_constraint(lax.empty(p.shape, p.dtype), pltpu.HBM))


def _exchange_wait(send_sems, recv_sems, p_thru, land_thru, after, *, name):
    def body(p_ref, land_ref, send_sems, recv_sems, after_ref, p_dead, got_ref):
        for cp in _exchange_peers_copies(p_ref, land_ref, send_sems, recv_sems, False):
            cp.wait_send()
            cp.wait_recv()

    return pl.pallas_call(
        body, name=name,
        out_shape=(pltpu.HBM(p_thru.shape, p_thru.dtype), pltpu.HBM(p_thru.shape, p_thru.dtype)),
        in_specs=(HBM_SPEC, HBM_SPEC, SEM_SPEC, SEM_SPEC, ANY), out_specs=(HBM_SPEC, HBM_SPEC),
        input_output_aliases={0: 0, 1: 1},
        compiler_params=pltpu.CompilerParams(has_side_effects=SPLIT_EFFECT),
    )(p_thru, land_thru, send_sems, recv_sems, after)


def _sum_parts(parts, own, *, name):
    k, r, c = parts.shape
    tr = _pick(r, (512, 256, 128))

    def body(p_ref, own_ref, o_ref):
        me = 2 * lax.axis_index("x") + lax.axis_index("y")
        acc = jnp.zeros((tr, c), F32)
        for i in range(k):
            acc = acc + jnp.where(me == i, own_ref[i], p_ref[i]).astype(F32)
        o_ref[...] = acc

    blk = pl.BlockSpec((k, tr, c), lambda i: (0, i, 0))
    return pl.pallas_call(
        body, name=name, grid=(r // tr,), in_specs=[blk, blk],
        out_specs=pl.BlockSpec((tr, c), lambda i: (i, 0)),
        out_shape=jax.ShapeDtypeStruct((r, c), F32),
        compiler_params=_cparams(("parallel",)),
    )(parts, own)


def _join_halves(gh, *, name):
    def body(g_ref, o_ref, send_sem, recv_sem):
        x, y, c = lax.axis_index("x"), lax.axis_index("y"), lax.axis_index("c")
        cp = pltpu.make_async_remote_copy(src_ref=g_ref, dst_ref=o_ref, send_sem=send_sem, recv_sem=recv_sem,
                                          device_id=(x, y, 1 - c), device_id_type=MESH)
        cp.start()
        cp.wait()

    other = pl.pallas_call(
        body, name=name, in_specs=[ANY], out_specs=ANY,
        out_shape=jax.ShapeDtypeStruct(gh.shape, F32),
        scratch_shapes=[pltpu.SemaphoreType.DMA, pltpu.SemaphoreType.DMA],
    )(gh)
    south = lax.axis_index("c") == 0
    return jnp.concatenate([jnp.where(south, gh, other), jnp.where(south, other, gh)], axis=0)


def _gather_small(s, *, name):
    rows = s.shape[0]

    def body(s_ref, o_ref, send_sems, recv_sems, local_sem):
        x, y, c = lax.axis_index("x"), lax.axis_index("y"), lax.axis_index("c")
        me = 4 * x + 2 * y + c
        mine = pltpu.make_async_copy(s_ref, o_ref.at[me], local_sem)
        mine.start()
        peers = []
        for k in range(1, 8):
            peers.append((1 - x if k & 4 else x, 1 - y if k & 2 else y, 1 - c if k & 1 else c))
        cps = [pltpu.make_async_remote_copy(src_ref=s_ref, dst_ref=o_ref.at[me], send_sem=send_sems.at[k],
                                            recv_sem=recv_sems.at[k], device_id=p, device_id_type=MESH)
               for k, p in enumerate(peers)]
        for cp in cps:
            cp.start()
        for k, (px, py, pc) in enumerate(peers):
            pltpu.make_async_remote_copy(src_ref=s_ref, dst_ref=o_ref.at[4 * px + 2 * py + pc],
                                         send_sem=send_sems.at[k], recv_sem=recv_sems.at[k],
                                         device_id=(px, py, pc), device_id_type=MESH).wait_recv()
        for cp in cps:
            cp.wait_send()
        mine.wait()

    return pl.pallas_call(
        body, name=name, in_specs=[ANY], out_specs=ANY,
        out_shape=jax.ShapeDtypeStruct((8, rows, 128), F32),
        scratch_shapes=[pltpu.SemaphoreType.DMA((7,)), pltpu.SemaphoreType.DMA((7,)), pltpu.SemaphoreType.DMA],
    )(s)


IN_SHARD = IN_WIDTH // N_CHIPS
IN_SHARD_PAD = 1536
UP_ROWS, DOWN_ROWS, OUT_ROWS = 1024, 1024, 512
REST_ROWS = UP_ROWS + DOWN_ROWS + OUT_ROWS


def _pack_in(w_in_s):
    return jnp.pad(w_in_s, ((0, 0), (0, IN_SHARD_PAD - IN_SHARD))).astype(BF16)


def _pack_rest(w_out_s, w_up_s, w_down_s):
    return jnp.concatenate([w_up_s, w_down_s, w_out_s], axis=0).astype(BF16)


def _unpack_rest(blob):
    return (blob[UP_ROWS + DOWN_ROWS:], blob[0:UP_ROWS], blob[UP_ROWS:UP_ROWS + DOWN_ROWS])


def _full_w_in(g_in):
    return jnp.concatenate([g_in[j, :, :IN_SHARD] for j in range(N_CHIPS)], axis=1)


def _full_rest(g_rest):
    parts = [_unpack_rest(g_rest[j]) for j in range(N_CHIPS)]
    w_out = jnp.concatenate([p[0] for p in parts], axis=0)
    w_up = jnp.concatenate([p[1] for p in parts], axis=1)
    w_down = jnp.concatenate([p[2] for p in parts], axis=0)
    return w_out, w_up, w_down


def _split_w_in(w_in):
    z_xbc = w_in[:, 0:2560]
    dt = w_in[:, 2560:2576]
    qkv = w_in[:, 2576:5648]
    f = w_in[:, 5648:5664]
    pad = jnp.zeros((w_in.shape[0], PA_WIDTH - 2592), w_in.dtype)
    return jnp.concatenate([z_xbc, dt, f, pad], axis=1), qkv


def _merge_w_in(d_a, d_qkv):
    return jnp.concatenate([d_a[:, 0:2560], d_a[:, 2560:2576], d_qkv, d_a[:, 2576:2592]], axis=1)


def _local_step(x3, target3, w_in, rest_weights, norm_mix_w, conv_w, conv_b, dt_bias, a_log, d_skip,
                ssd_norm_w, f_bias, norm_mlp_w, norm_final_w, first_after=None, early_grads=None, late_grads=None):
    bl, t, d = x3.shape
    n = bl * t
    x = x3.reshape(n, d)
    target = target3.reshape(n, d)
    w_a, w_qkv = _split_w_in(w_in)
    nfw = norm_final_w.reshape(1, d)
    dskip_e = jnp.repeat(d_skip, HEAD_DIM, axis=1)
    nb = t // ATT_BLOCK

    r1, r2, kt = min(n, 1024), min(n, 512), min(n, 2048)
    if first_after is None:
        first_after = jnp.zeros((8, 128), F32)
    h0, rstd0, proj_a = _norm_mm(x, norm_mix_w, w_a, first_after, name="norm_mix_proj_a", tm=r2)
    qkv = _mm(h0, w_qkv, name="proj_qkv", tiles=(r2, QKV_WIDTH, D_MODEL), out_dtype=BF16)
    bias128 = jnp.concatenate([dt_bias, f_bias, jnp.zeros((1, 96), F32)], axis=1)
    alog128 = jnp.concatenate([a_log, jnp.zeros((1, 112), F32)], axis=1)
    dt, gate_d, acum, acum_t, negc = _prep(proj_a, bias128, alog128, bl, t)
    xc, dsilu = _conv_fwd(proj_a, conv_w, conv_b, bl, t)
    y_ssd, y_pre, hprev = _ssd_fwd(xc, proj_a, dt, acum, acum_t, dskip_e, ssd_norm_w, bl, t)
    y_att, lse = _attn_fwd(qkv, negc, bl, t)
    w_out, w_up, w_down = rest_weights(y_att)
    wo_s, wo_a = w_out[:SSD_WIDTH], w_out[SSD_WIDTH:]
    h1, h1n, rstd1 = _mm_norm_fwd(y_ssd, wo_s, y_att, wo_a, x, norm_mlp_w, name="out_proj_norm_mlp", tm=r2)
    up = _mm(h1n, w_up, name="mlp_up", tiles=(r1, D_FF, D_MODEL), out_dtype=BF16)
    dh2, dh2b, loss, d_nfw = _mm_final(up, w_down, h1, nfw, target, name="mlp_down_final_norm_loss", tm=r2,
                                       a_act="relu2")

    dup = _mm(dh2b, w_down, name="mlp_down_bwd_act", tiles=(r2, D_FF, D_MODEL), tb=True, epi_up=up, out_dtype=BF16)
    rest_shape = (N_CHIPS, REST_ROWS, D_MODEL)
    gb_rest = _mm(up, dh2b, name="mlp_down_bwd_w", tiles=(DOWN_ROWS, D_MODEL, kt), ta=True, a_act="relu2",
                  out_dtype=BF16, into=(rest_shape, (None, DOWN_ROWS, D_MODEL), lambda i, j, k: (i, 1, 0), None))
    dh1, dh1b, d_nmlp = _mm_norm_bwd([(dup, w_up)], h1, rstd1, norm_mlp_w, dh2, name="mlp_up_bwd_act_norm_mlp",
                                     tm=r2)
    gb_rest = _mm(h1n, dup, name="mlp_up_bwd_w", tiles=(D_MODEL, UP_ROWS, kt), ta=True, out_dtype=BF16,
                  into=(rest_shape, (None, D_MODEL, UP_ROWS), lambda i, j, k: (j, 0, 0), gb_rest))
    dys, do = _mm_two_halves(dh1b, w_out, name="out_proj_bwd_act", tm=r1)
    out_block = (UP_ROWS + DOWN_ROWS) // OUT_ROWS
    for half, (y_half, tag) in enumerate(((y_ssd, "ssd"), (y_att, "att"))):
        gb_rest = _mm(y_half, dh1b, name="out_proj_bwd_w_" + tag, tiles=(2 * OUT_ROWS, D_MODEL, kt), ta=True,
                      out_dtype=BF16, into=(rest_shape, (2, OUT_ROWS, D_MODEL),
                                            functools.partial(lambda i, j, k, h: (h, out_block, 0), h=half),
                                            gb_rest))
    token = jnp.zeros((8, 128), F32) if early_grads is None else early_grads(gb_rest)
    dq, dk, dv, dcb = _attn_bwd(qkv, do, y_att, lse, negc, token, bl, t)
    dc = jnp.pad(dcb[:, :, 0:2, :].transpose(0, 3, 1, 2).reshape(n, 16), ((0, 0), (16, 96)))
    dxc, dpa, ddt_raw, d_snw, d_dsk, d_alog, d_dtb = _ssd_bwd(dys, xc, proj_a, y_pre, hprev, dt, gate_d, acum,
                                                             acum_t, alog128, dskip_e, ssd_norm_w, bl, t)
    dpa, d_conv_w, d_conv_b = _conv_bwd(dxc, dsilu, proj_a, conv_w, dpa, bl, t)
    dproj_a, d_fb = _fpost(dc, gate_d, ddt_raw, dpa, bl, t)
    dqkv = jnp.concatenate([dq, dk, dv], axis=1)
    d_w_a = _mm(h0, dproj_a, name="proj_a_bwd_w", tiles=(1024, 896, kt), ta=True, out_dtype=BF16)
    d_w_qkv = _mm(h0, dqkv, name="proj_qkv_bwd_w", tiles=(1024, 1024, kt), ta=True, out_dtype=BF16)
    d_w_in = _merge_w_in(d_w_a, d_w_qkv)
    late_token = None if late_grads is None else late_grads(d_w_in)
    dx, _, d_nmix = _mm_norm_bwd([(dproj_a, w_a), (dqkv, w_qkv)], x, rstd0, norm_mix_w, dh1,
                                 name="proj_bwd_act_norm_mix", tm=min(n, 256), after=late_token)

    grads = dict(norm_mix_w=d_nmix, w_in=d_w_in, conv_w=d_conv_w, conv_b=d_conv_b,
                 dt_bias=d_dtb, a_log=d_alog, d_skip=d_dsk, ssd_norm_w=d_snw, f_bias=d_fb, rest=gb_rest,
                 norm_mlp_w=d_nmlp, norm_final_w=d_nfw)
    return dx.reshape(bl, t, d), loss, grads


SMALL_ORDER = ("norm_mix_w", "conv_w", "conv_b", "dt_bias", "a_log", "d_skip", "ssd_norm_w", "f_bias",
               "norm_mlp_w", "norm_final_w")
SMALL_SIZES = (1024, 4 * CONV_CH, CONV_CH, 16, 16, 16, 1024, 16, 1024, 1024)


def _pack_small(vals, rows):
    flat = jnp.concatenate([v.reshape(-1).astype(F32) for v in vals])
    return jnp.pad(flat, (0, rows * 128 - flat.shape[0])).reshape(rows, 128)


def _unpack_small(packed, sizes):
    flat = packed.reshape(-1)
    out, o = [], 0
    for s in sizes:
        out.append(flat[o:o + s])
        o += s
    return out


def kernel(x, norm_mix_w, w_in, conv_w, conv_b, dt_bias, a_log, d_skip, ssd_norm_w, f_bias, w_out, norm_mlp_w, w_up, w_down, norm_final_w, loss_target, m_norm_mix_w, m_w_in, m_conv_w, m_conv_b, m_dt_bias, m_a_log, m_d_skip, m_ssd_norm_w, m_f_bias, m_w_out, m_norm_mlp_w, m_w_up, m_w_down, m_norm_final_w, v_norm_mix_w, v_w_in, v_conv_w, v_conv_b, v_dt_bias, v_a_log, v_d_skip, v_ssd_norm_w, v_f_bias, v_w_out, v_norm_mlp_w, v_w_up, v_w_down, v_norm_final_w):
    chip = 2 * lax.axis_index("x") + lax.axis_index("y")
    cw = CONV_CH // N_CHIPS

    own_in = _pack_in(w_in[0])
    own_rest = _pack_rest(w_out[0], w_up[0], w_down[0])
    g_in = _gather_weights(own_in, name="gather_w_in")
    w_in_f = _full_w_in(g_in)
    *rest_handles, rest_token = _gather_start(own_rest, g_in, name="gather_start_rest")

    def rest_weights(after):
        _, landed = _gather_wait(*rest_handles, after, name="gather_wait_rest")
        return _full_rest(_gather_forward(landed, own_rest, name="gather_forward_rest"))
    small_all = _gather_small(_pack_small([conv_w[0]], 16), name="gather_conv_w")
    conv_w_f = jnp.concatenate([small_all[2 * j].reshape(-1)[:4 * cw].reshape(4, cw) for j in range(N_CHIPS)], axis=1)

    def chip_partial(gb, tag):
        from_sibling = _swap_halves(gb, name="grad_swap_halves_" + tag)
        return _add_my_half(gb, from_sibling, name="grad_add_sibling_" + tag)

    in_flight = {}

    def early_grads(gb_rest):
        part = chip_partial(gb_rest, "rest")
        *handles, token = _exchange_start(part, name="grad_exchange_start_rest")
        in_flight["rest"] = handles
        return token

    def late_grads(d_w_in):
        gb_in = jnp.stack([_pack_in(d_w_in[:, j * IN_SHARD:(j + 1) * IN_SHARD]) for j in range(N_CHIPS)])
        *handles, token = _exchange_start(chip_partial(gb_in, "in"), name="grad_exchange_start_in")
        in_flight["in"] = handles
        return token

    dx, loss_part, g = _local_step(x, loss_target, w_in_f, rest_weights, norm_mix_w, conv_w_f,
                                   conv_b, dt_bias, a_log, d_skip, ssd_norm_w, f_bias, norm_mlp_w, norm_final_w,
                                   first_after=rest_token, early_grads=early_grads, late_grads=late_grads)

    send_sems, recv_sems, part_rest, land_rest = in_flight["rest"]
    part_rest, parts_rest = _exchange_wait(send_sems, recv_sems, part_rest, land_rest, dx,
                                           name="grad_exchange_wait_rest")
    g_rest_half = _sum_parts(parts_rest, part_rest, name="grad_sum_chips_rest")
    g_w_out, g_w_up, g_w_down = _unpack_rest(_join_halves(g_rest_half, name="grad_join_halves_rest"))

    part_in, parts_in = _exchange_wait(*in_flight["in"], dx, name="grad_exchange_wait_in")
    g_in_half = _sum_parts(parts_in, part_in, name="grad_sum_chips_in")
    g_w_in = _join_halves(g_in_half, name="grad_join_halves_in")[:, :IN_SHARD]

    small_vals = [g[k] for k in SMALL_ORDER] + [loss_part[:, 0:1]]
    small_sum = _sum_leading(_gather_small(_pack_small(small_vals, SMALL_ROWS), name="gather_small_grads"), name="small_sum")
    sg = dict(zip(SMALL_ORDER + ("loss",), _unpack_small(small_sum, SMALL_SIZES + (1,))))
    loss = sg["loss"].reshape(())
    g_conv_full = sg["conv_w"].reshape(4, CONV_CH)
    g_conv = lax.dynamic_slice_in_dim(g_conv_full, chip * cw, cw, axis=1)

    grads = dict(norm_mix_w=sg["norm_mix_w"].reshape(1, -1), w_in=g_w_in[None], conv_w=g_conv[None],
                 conv_b=sg["conv_b"].reshape(1, -1), dt_bias=sg["dt_bias"].reshape(1, -1),
                 a_log=sg["a_log"].reshape(1, -1), d_skip=sg["d_skip"].reshape(1, -1),
                 ssd_norm_w=sg["ssd_norm_w"].reshape(1, -1), f_bias=sg["f_bias"].reshape(1, -1), w_out=g_w_out[None],
                 norm_mlp_w=sg["norm_mlp_w"].reshape(1, -1), w_up=g_w_up[None], w_down=g_w_down[None],
                 norm_final_w=sg["norm_final_w"])
    weights = dict(norm_mix_w=norm_mix_w, w_in=w_in, conv_w=conv_w, conv_b=conv_b, dt_bias=dt_bias, a_log=a_log,
                   d_skip=d_skip, ssd_norm_w=ssd_norm_w, f_bias=f_bias, w_out=w_out, norm_mlp_w=norm_mlp_w,
                   w_up=w_up, w_down=w_down, norm_final_w=norm_final_w)
    ms = dict(norm_mix_w=m_norm_mix_w, w_in=m_w_in, conv_w=m_conv_w, conv_b=m_conv_b, dt_bias=m_dt_bias,
              a_log=m_a_log, d_skip=m_d_skip, ssd_norm_w=m_ssd_norm_w, f_bias=m_f_bias, w_out=m_w_out,
              norm_mlp_w=m_norm_mlp_w, w_up=m_w_up, w_down=m_w_down, norm_final_w=m_norm_final_w)
    vs = dict(norm_mix_w=v_norm_mix_w, w_in=v_w_in, conv_w=v_conv_w, conv_b=v_conv_b, dt_bias=v_dt_bias,
              a_log=v_a_log, d_skip=v_d_skip, ssd_norm_w=v_ssd_norm_w, f_bias=v_f_bias, w_out=v_w_out,
              norm_mlp_w=v_norm_mlp_w, w_up=v_w_up, w_down=v_w_down, norm_final_w=v_norm_final_w)
    names = list(weights)
    big = ("w_in", "w_out", "w_up", "w_down")
    delta, new_m, new_v = {}, {}, {}
    for k, g2 in zip(big[1:], (g_w_out, g_w_up, g_w_down)):
        delta[k], new_m[k], new_v[k] = _adamw(weights[k], g2, ms[k], vs[k], name="adamw_" + k)
    g_in_t = g_w_in.T
    outs_t = _adamw(w_in[0].T, g_in_t, m_w_in[0].T, v_w_in[0].T, name="adamw_w_in")
    delta["w_in"], new_m["w_in"], new_v["w_in"] = [o.T[None] for o in outs_t]
    grads["w_in"] = g_in_t.T[None]
    smalls = [k for k in names if k not in big]
    sizes = [math.prod(weights[k].shape) for k in smalls]
    rows = -(-sum(sizes) // 1024) * 8
    packs = [_pack_small([d[k] for k in smalls], rows) for d in (weights, grads, ms, vs)]
    outs = _adamw(*packs, name="adamw_small")
    for o, dst in zip(outs, (delta, new_m, new_v)):
        for k, val in zip(smalls, _unpack_small(o, sizes)):
            dst[k] = val.reshape(weights[k].shape)
    return (loss, dx, *[grads[k] for k in names], *[delta[k] for k in names], *[new_m[k] for k in names],
            *[new_v[k] for k in names])
```

```python
import functools
import math

import jax
import jax.numpy as jnp
from jax import lax
from jax.experimental import pallas as pl
from jax.experimental.pallas import tpu as pltpu

F32 = jnp.float32
BF16 = jnp.bfloat16
HIGHEST = lax.Precision.HIGHEST
MESH = pl.DeviceIdType.MESH

D_MODEL = 1024
HEAD_DIM = 64
SSD_WIDTH = 1024
SSD_STATE = 128
CONV_CH = 1536
CHUNK = 128
ATT_WIDTH = 1024
EPS = 1e-5
IN_WIDTH = 5664
PA_WIDTH = 2688
QKV_WIDTH = 3072
D_FF = 4096
ATT_FWD_BLOCK = 512
ATT_BWD_BLOCK = 256
NEG = -1e30
LOG2E = 1.4426950408889634
VMEM_LIMIT = 48 * 1024 * 1024

ADAM_LR = 0.001
ADAM_B1 = 0.9
ADAM_B2 = 0.999
ADAM_EPS = 1e-08
ADAM_WD = 0.01
ADAM_STEP = 10

N_CHIPS = 4
SMALL_ROWS = 96


def _cparams(sem):
    return pltpu.CompilerParams(dimension_semantics=sem, vmem_limit_bytes=VMEM_LIMIT)


def _pick(n, cands):
    for c in cands:
        if n % c == 0:
            return c
    return n


MM_CHUNK = 512


def _mm(a, b, *, name, tiles, ta=False, tb=False, out_dtype=F32, res=None, a_act=None, epi_up=None, after=None,
        into=None):
    n_unread = (after is not None) + (into is not None and into[3] is not None)
    if ta:
        K, M = a.shape
    else:
        M, K = a.shape
    if tb:
        N, K2 = b.shape
    else:
        K2, N = b.shape
    assert K == K2, (a.shape, b.shape)
    tm, tn, tk = tiles
    assert M % tm == 0 and N % tn == 0 and K % tk == 0, (name, M, N, K, tiles)
    nk = K // tk
    dn = (((0 if ta else 1,), (1 if tb else 0,)), ((), ()))
    has_res = res is not None
    has_up = epi_up is not None
    cn = _pick(tn, (MM_CHUNK, 384, 256, 128))

    def prologue(av):
        if a_act == "relu2":
            r = jnp.maximum(av.astype(F32), 0.0)
            av = r * r
        return av.astype(BF16)

    def epilogue(out, res_v, up_v):
        if has_res:
            out = out + res_v.astype(F32)
        if has_up:
            out = out * (2.0 * jnp.maximum(up_v.astype(F32), 0.0))
        return out.astype(out_dtype)

    def body(*refs):
        a_ref, b_ref = refs[0], refs[1]
        i = 2
        res_ref = up_ref = None
        if has_res:
            res_ref = refs[i]
            i += 1
        if has_up:
            up_ref = refs[i]
            i += 1
        i += n_unread
        o_ref = refs[i]
        if nk == 1:
            av = prologue(a_ref[...])
            if len(o_ref.shape) == 3:
                out = lax.dot_general(av, b_ref[...].astype(BF16), dn, preferred_element_type=F32)
                out = epilogue(out, res_ref[...] if has_res else None, up_ref[...] if has_up else None)
                o_ref[...] = out.reshape(o_ref.shape)
                return
            for c in range(tn // cn):
                cs = slice(c * cn, (c + 1) * cn)
                bv = (b_ref[cs, :] if tb else b_ref[:, cs]).astype(BF16)
                out = lax.dot_general(av, bv, dn, preferred_element_type=F32)
                o_ref[:, cs] = epilogue(out, res_ref[:, cs] if has_res else None, up_ref[:, cs] if has_up else None)
            return
        acc_ref = refs[i + 1]
        k = pl.program_id(2)

        @pl.when(k == 0)
        def _():
            acc_ref[...] = jnp.zeros_like(acc_ref)

        acc_ref[...] += lax.dot_general(prologue(a_ref[...]), b_ref[...].astype(BF16), dn,
                                        preferred_element_type=F32)

        @pl.when(k == nk - 1)
        def _():
            out = epilogue(acc_ref[...], res_ref[...] if has_res else None, up_ref[...] if has_up else None)
            o_ref[...] = out.reshape(o_ref.shape)

    a_spec = pl.BlockSpec((tk, tm), lambda i, j, k: (k, i)) if ta else pl.BlockSpec((tm, tk), lambda i, j, k: (i, k))
    b_spec = pl.BlockSpec((tn, tk), lambda i, j, k: (j, k)) if tb else pl.BlockSpec((tk, tn), lambda i, j, k: (k, j))
    o_spec = pl.BlockSpec((tm, tn), lambda i, j, k: (i, j))
    ins, specs = [a, b], [a_spec, b_spec]
    if has_res:
        ins.append(res)
        specs.append(o_spec)
    if has_up:
        ins.append(epi_up)
        specs.append(o_spec)
    if after is not None:
        ins.append(after)
        specs.append(pl.BlockSpec(memory_space=pl.ANY))
    out_shape, out_spec, aliases = jax.ShapeDtypeStruct((M, N), out_dtype), o_spec, {}
    if into is not None:
        shape, block, index, buf = into
        out_shape, out_spec = jax.ShapeDtypeStruct(shape, out_dtype), pl.BlockSpec(block, index)
        if buf is not None:
            aliases = {len(ins): 0}
            ins.append(buf)
            specs.append(pl.BlockSpec(memory_space=pl.ANY))
    return pl.pallas_call(
        body, name=name, grid=(M // tm, N // tn, nk),
        in_specs=specs, out_specs=out_spec, out_shape=out_shape, input_output_aliases=aliases,
        scratch_shapes=[] if nk == 1 else [pltpu.VMEM((tm, tn), F32)],
        compiler_params=_cparams(("parallel", "parallel", "arbitrary")),
    )(*ins)


def _rows_product(a_ref, b_ref, tb, a_act):
    av = a_ref[...]
    if a_act == "relu2":
        r = jnp.maximum(av.astype(F32), 0.0)
        av = r * r
    dn = (((1,), (1 if tb else 0,)), ((), ()))
    return lax.dot_general(av.astype(BF16), b_ref[...].astype(BF16), dn, preferred_element_type=F32)


def _norm_mm(x, w, b, after, *, name, tm):
    m, d = x.shape
    n = b.shape[1]
    cn = _pick(n, (MM_CHUNK, 384, 256, 128))

    def body(x_ref, w_ref, b_ref, after_ref, h_ref, r_ref, o_ref):
        xv = x_ref[...]
        rstd = lax.rsqrt(jnp.mean(xv * xv, axis=1, keepdims=True) + EPS)
        hv = (xv * rstd * w_ref[...]).astype(BF16)
        h_ref[...] = hv
        r_ref[...] = rstd
        for c in range(n // cn):
            cs = slice(c * cn, (c + 1) * cn)
            o_ref[:, cs] = jnp.dot(hv, b_ref[:, cs].astype(BF16), preferred_element_type=F32)

    row = pl.BlockSpec((tm, d), lambda i: (i, 0))
    return pl.pallas_call(
        body, name=name, grid=(m // tm,),
        in_specs=[row, pl.BlockSpec((1, d), lambda i: (0, 0)), pl.BlockSpec((d, n), lambda i: (0, 0)),
                  pl.BlockSpec(memory_space=pl.ANY)],
        out_specs=[row, pl.BlockSpec((tm, 1), lambda i: (i, 0)), pl.BlockSpec((tm, n), lambda i: (i, 0))],
        out_shape=[jax.ShapeDtypeStruct((m, d), BF16), jax.ShapeDtypeStruct((m, 1), F32),
                   jax.ShapeDtypeStruct((m, n), F32)],
        compiler_params=_cparams(("parallel",)),
    )(x, w, b, after)


def _mm_norm_fwd(a1, b1, a2, b2, res, w, *, name, tm):
    m, k1 = a1.shape
    k2 = a2.shape[1]
    d = b1.shape[1]

    def body(a1_ref, b1_ref, a2_ref, b2_ref, res_ref, w_ref, h_ref, y_ref, r_ref):
        hv = _rows_product(a1_ref, b1_ref, False, None) + _rows_product(a2_ref, b2_ref, False, None) + res_ref[...]
        rstd = lax.rsqrt(jnp.mean(hv * hv, axis=1, keepdims=True) + EPS)
        h_ref[...] = hv
        y_ref[...] = (hv * rstd * w_ref[...]).astype(BF16)
        r_ref[...] = rstd

    row = pl.BlockSpec((tm, d), lambda i: (i, 0))
    return pl.pallas_call(
        body, name=name, grid=(m // tm,),
        in_specs=[pl.BlockSpec((tm, k1), lambda i: (i, 0)), pl.BlockSpec((k1, d), lambda i: (0, 0)),
                  pl.BlockSpec((tm, k2), lambda i: (i, 0)), pl.BlockSpec((k2, d), lambda i: (0, 0)), row,
                  pl.BlockSpec((1, d), lambda i: (0, 0))],
        out_specs=[row, row, pl.BlockSpec((tm, 1), lambda i: (i, 0))],
        out_shape=[jax.ShapeDtypeStruct((m, d), F32), jax.ShapeDtypeStruct((m, d), BF16),
                   jax.ShapeDtypeStruct((m, 1), F32)],
        compiler_params=_cparams(("parallel",)),
    )(a1, b1, a2, b2, res, w)


def _mm_final(a, b, res, w, target, *, name, tm, a_act):
    m, k = a.shape
    d = b.shape[1]

    def body(a_ref, b_ref, res_ref, w_ref, t_ref, dh_ref, dhb_ref, loss_ref, dw_ref):
        @pl.when(pl.program_id(0) == 0)
        def _():
            loss_ref[...] = jnp.zeros_like(loss_ref)
            dw_ref[...] = jnp.zeros_like(dw_ref)

        hv = _rows_product(a_ref, b_ref, False, a_act) + res_ref[...]
        wv = w_ref[...]
        rstd = lax.rsqrt(jnp.mean(hv * hv, axis=1, keepdims=True) + EPS)
        xhat = hv * rstd
        err = xhat * wv - t_ref[...]
        loss_ref[...] += 0.5 * jnp.sum(jnp.mean(err * err, axis=1, keepdims=True), axis=0, keepdims=True)
        dy = err * (1.0 / d)
        gw = dy * wv
        dh = rstd * (gw - xhat * jnp.mean(gw * xhat, axis=1, keepdims=True))
        dh_ref[...] = dh
        dhb_ref[...] = dh.astype(BF16)
        dw_ref[...] += jnp.sum(dy * xhat, axis=0, keepdims=True)

    row = pl.BlockSpec((tm, d), lambda i: (i, 0))
    vec = pl.BlockSpec((1, d), lambda i: (0, 0))
    return pl.pallas_call(
        body, name=name, grid=(m // tm,),
        in_specs=[pl.BlockSpec((tm, k), lambda i: (i, 0)), pl.BlockSpec((k, d), lambda i: (0, 0)), row, vec, row],
        out_specs=[row, row, pl.BlockSpec((1, 128), lambda i: (0, 0)), vec],
        out_shape=[jax.ShapeDtypeStruct((m, d), F32), jax.ShapeDtypeStruct((m, d), BF16),
                   jax.ShapeDtypeStruct((1, 128), F32), jax.ShapeDtypeStruct((1, d), F32)],
        compiler_params=_cparams(("arbitrary",)),
    )(a, b, res, w, target)


def _mm_two_halves(a, b, *, name, tm):
    m, k = a.shape
    d = b.shape[0] // 2

    def body(a_ref, b_ref, lo_ref, hi_ref):
        av = a_ref[...].astype(BF16)
        lo_ref[...] = lax.dot_general(av, b_ref[0:d, :].astype(BF16), NT_DIMS, preferred_element_type=F32)
        hi_ref[...] = lax.dot_general(av, b_ref[d:2 * d, :].astype(BF16), NT_DIMS,
                                      preferred_element_type=F32).astype(BF16)

    row = pl.BlockSpec((tm, d), lambda i: (i, 0))
    return pl.pallas_call(
        body, name=name, grid=(m // tm,),
        in_specs=[pl.BlockSpec((tm, k), lambda i: (i, 0)), pl.BlockSpec((2 * d, k), lambda i: (0, 0))],
        out_specs=[row, row],
        out_shape=[jax.ShapeDtypeStruct((m, d), F32), jax.ShapeDtypeStruct((m, d), BF16)],
        compiler_params=_cparams(("parallel",)),
    )(a, b)


def _mm_norm_bwd(pairs, x, rstd, w, dres, *, name, tm, after=None):
    m = pairs[0][0].shape[0]
    d = pairs[0][1].shape[0]
    n_pairs = len(pairs)

    def body(*refs):
        i = 2 * n_pairs
        x_ref, r_ref, w_ref, d_ref = refs[i:i + 4]
        dx_ref, dxb_ref, dw_ref = refs[-3:]

        @pl.when(pl.program_id(0) == 0)
        def _():
            dw_ref[...] = jnp.zeros_like(dw_ref)

        g = _rows_product(refs[0], refs[1], True, None)
        for p in range(1, n_pairs):
            g = g + _rows_product(refs[2 * p], refs[2 * p + 1], True, None)
        r = r_ref[...]
        xhat = x_ref[...] * r
        gw = g * w_ref[...]
        dx = d_ref[...] + r * (gw - xhat * jnp.mean(gw * xhat, axis=1, keepdims=True))
        dx_ref[...] = dx
        dxb_ref[...] = dx.astype(BF16)
        dw_ref[...] += jnp.sum(g * xhat, axis=0, keepdims=True)

    row = pl.BlockSpec((tm, d), lambda i: (i, 0))
    vec = pl.BlockSpec((1, d), lambda i: (0, 0))
    ins, specs = [], []
    for a, b in pairs:
        k = a.shape[1]
        ins += [a, b]
        specs += [pl.BlockSpec((tm, k), lambda i: (i, 0)), pl.BlockSpec((d, k), lambda i: (0, 0))]
    ins += [x, rstd, w, dres]
    specs += [row, pl.BlockSpec((tm, 1), lambda i: (i, 0)), vec, row]
    if after is not None:
        ins.append(after)
        specs.append(pl.BlockSpec(memory_space=pl.ANY))
    return pl.pallas_call(
        body, name=name, grid=(m // tm,), in_specs=specs, out_specs=[row, row, vec],
        out_shape=[jax.ShapeDtypeStruct((m, d), F32), jax.ShapeDtypeStruct((m, d), BF16),
                   jax.ShapeDtypeStruct((1, d), F32)],
        compiler_params=_cparams(("arbitrary",)),
    )(*ins)


def _softplus(x):
    return jnp.maximum(x, 0.0) + jnp.log(1.0 + jnp.exp(-jnp.abs(x)))


def _prep(proj_a, bias128, alog128, bl, t):
    n = bl * t
    nch = t // CHUNK
    col0 = (SSD_WIDTH + CONV_CH) // 128

    def body(p_ref, b_ref, al_ref, dt_ref, gd_ref, ac_ref, act_ref, negc_ref):
        negc_ref[...] = jnp.zeros_like(negc_ref)
        row = lax.broadcasted_iota(jnp.int32, (CHUNK, CHUNK), 0)
        col = lax.broadcasted_iota(jnp.int32, (CHUNK, CHUNK), 1)
        tril = (row >= col).astype(F32)
        lane = lax.broadcasted_iota(jnp.int32, (1, 128), 1)
        head_lanes = lane < 16
        a_row = -jnp.exp(al_ref[...])
        carry = jnp.zeros((1, 128), F32)
        for ci in range(nch):
            rows = slice(ci * CHUNK, (ci + 1) * CHUNK)
            xv = p_ref[rows, :] + b_ref[...]
            sp = _softplus(xv)
            acum = jnp.dot(tril, a_row * sp, precision=HIGHEST, preferred_element_type=F32)
            c = jnp.dot(tril, -_softplus(-xv), precision=HIGHEST, preferred_element_type=F32) + carry
            carry = c[CHUNK - 1:CHUNK, :]
            dt_ref[rows, :] = jnp.where(head_lanes, sp, 0.0)
            gd_ref[rows, :] = jnp.where(head_lanes, jax.nn.sigmoid(xv),
                                        jnp.where(lane < 32, jax.nn.sigmoid(-xv), 0.0))
            ac_ref[rows, :] = jnp.where(head_lanes, acum, 0.0)
            act_ref[:, rows] = jnp.transpose(acum)[0:16, :]
            c_t = jnp.transpose(c)
            for hp in range(8):
                negc_ref[hp, 0:2, rows] = -c_t[16 + 2 * hp:18 + 2 * hp, :]

    o128 = pl.BlockSpec((t, 128), lambda b: (b, 0))
    v128 = pl.BlockSpec((1, 128), lambda b: (0, 0))
    w128 = jax.ShapeDtypeStruct((n, 128), F32)
    return pl.pallas_call(
        body, name="head_scalars", grid=(bl,),
        in_specs=[pl.BlockSpec((t, 128), lambda b: (b, col0)), v128, v128],
        out_specs=[o128, o128, o128, pl.BlockSpec((16, t), lambda b: (0, b)),
                   pl.BlockSpec((None, 8, 8, t), lambda b: (b, 0, 0, 0))],
        out_shape=[w128, w128, w128, jax.ShapeDtypeStruct((16, n), F32),
                   jax.ShapeDtypeStruct((bl, 8, 8, t), F32)],
        compiler_params=_cparams(("parallel",)),
    )(proj_a, bias128, alog128)


def _fpost(dc, gate_d, ddt, dpa, bl, t):
    n = bl * t
    nch = t // CHUNK
    col0 = (SSD_WIDTH + CONV_CH) // 128

    def body(dc_ref, gd_ref, ddt_ref, dpa_in, out_ref, db_ref):
        @pl.when(pl.program_id(0) == 0)
        def _():
            db_ref[...] = jnp.zeros_like(db_ref)

        row = lax.broadcasted_iota(jnp.int32, (CHUNK, CHUNK), 0)
        col = lax.broadcasted_iota(jnp.int32, (CHUNK, CHUNK), 1)
        triu = (row <= col).astype(F32)
        lane = lax.broadcasted_iota(jnp.int32, (1, 128), 1)
        gate_lanes = (lane >= 16) & (lane < 32)
        carry = jnp.zeros((1, 128), F32)
        db = jnp.zeros((1, 128), F32)
        for ci in reversed(range(nch)):
            rows = slice(ci * CHUNK, (ci + 1) * CHUNK)
            dlf = jnp.dot(triu, dc_ref[rows, :], precision=HIGHEST, preferred_element_type=F32) + carry
            carry = dlf[0:1, :]
            df = jnp.where(gate_lanes, dlf * gd_ref[rows, :], 0.0)
            out_ref[rows, :] = (ddt_ref[rows, :] + df).astype(BF16)
            db = db + jnp.sum(df, axis=0, keepdims=True)
        db_ref[...] += db[:, 16:32]

    blk = pl.BlockSpec((t, 128), lambda b: (b, 0))
    return pl.pallas_call(
        body, name="forget_gate_bwd", grid=(bl,),
        in_specs=[blk, blk, blk, ANY],
        out_specs=[pl.BlockSpec((t, 128), lambda b: (b, col0)), pl.BlockSpec((1, 16), lambda b: (0, 0))],
        out_shape=[jax.ShapeDtypeStruct(dpa.shape, dpa.dtype), jax.ShapeDtypeStruct((1, 16), F32)],
        input_output_aliases={3: 0},
        compiler_params=_cparams(("arbitrary",)),
    )(dc, gate_d, ddt, dpa)


CONV_TILE = 256
CONV_ROWS = 256


def _conv_taps(u_ref, i):
    r0 = pl.multiple_of(i * CONV_ROWS, CONV_ROWS)
    cur = u_ref[pl.ds(r0, CONV_ROWS), :]
    p0 = pl.multiple_of(jnp.maximum(r0 - 8, 0), 8)
    prev = jnp.where(i > 0, u_ref[pl.ds(p0, 8), :], 0.0)
    cat = jnp.concatenate([prev, cur], axis=0)
    return r0, [cur] + [pltpu.roll(cat, s, 0)[8:, :] for s in (1, 2, 3)]


def _conv_fwd(proj_a, conv_w, conv_b, bl, t):
    n = bl * t
    nct = CONV_CH // CONV_TILE
    c0 = SSD_WIDTH // CONV_TILE

    def body(u_ref, w_ref, b_ref, o_ref, d_ref):
        w = w_ref[...]
        bias = b_ref[...]

        def chunk(i, carry):
            r0, taps = _conv_taps(u_ref, i)
            pre = bias + w[3:4, :] * taps[0]
            for s in (1, 2, 3):
                pre = pre + w[3 - s:4 - s, :] * taps[s]
            sg = jax.nn.sigmoid(pre)
            o_ref[pl.ds(r0, CONV_ROWS), :] = pre * sg
            d_ref[pl.ds(r0, CONV_ROWS), :] = (sg * (1.0 + pre * (1.0 - sg))).astype(BF16)
            return carry

        lax.fori_loop(0, t // CONV_ROWS, chunk, 0)

    out = pl.BlockSpec((t, CONV_TILE), lambda b, c: (b, c))
    return pl.pallas_call(
        body, name="conv_silu_fwd", grid=(bl, nct),
        in_specs=[pl.BlockSpec((t, CONV_TILE), lambda b, c: (b, c0 + c)),
                  pl.BlockSpec((4, CONV_TILE), lambda b, c: (0, c)),
                  pl.BlockSpec((1, CONV_TILE), lambda b, c: (0, c))],
        out_specs=[out, out],
        out_shape=[jax.ShapeDtypeStruct((n, CONV_CH), F32), jax.ShapeDtypeStruct((n, CONV_CH), BF16)],
        compiler_params=_cparams(("parallel", "parallel")),
    )(proj_a, conv_w, conv_b)


def _conv_bwd(dxc, dsilu, proj_a, conv_w, dpa, bl, t):
    nct = CONV_CH // CONV_TILE
    c0 = SSD_WIDTH // CONV_TILE
    nrc = t // CONV_ROWS

    def body(g_ref, s_ref, u_ref, w_ref, dpa_in, du_ref, dw_ref, db_ref, dp_scr):
        @pl.when(pl.program_id(1) == 0)
        def _():
            dw_ref[...] = jnp.zeros_like(dw_ref)
            db_ref[...] = jnp.zeros_like(db_ref)

        w = w_ref[...]
        dp_scr[pl.ds(t, 8), :] = jnp.zeros((8, CONV_TILE), F32)

        def chunk1(i, carry):
            dw0, dw1, dw2, dw3, db = carry
            r0, taps = _conv_taps(u_ref, i)
            dpre = g_ref[pl.ds(r0, CONV_ROWS), :] * s_ref[pl.ds(r0, CONV_ROWS), :].astype(F32)
            dp_scr[pl.ds(r0, CONV_ROWS), :] = dpre
            dw3 = dw3 + jnp.sum(dpre * taps[0], axis=0, keepdims=True)
            dw2 = dw2 + jnp.sum(dpre * taps[1], axis=0, keepdims=True)
            dw1 = dw1 + jnp.sum(dpre * taps[2], axis=0, keepdims=True)
            dw0 = dw0 + jnp.sum(dpre * taps[3], axis=0, keepdims=True)
            db = db + jnp.sum(dpre, axis=0, keepdims=True)
            return dw0, dw1, dw2, dw3, db

        z = jnp.zeros((1, CONV_TILE), F32)
        dw0, dw1, dw2, dw3, db = lax.fori_loop(0, nrc, chunk1, (z, z, z, z, z))
        dw_ref[...] += jnp.concatenate([dw0, dw1, dw2, dw3], axis=0)
        db_ref[...] += db

        def chunk2(i, carry):
            r0 = pl.multiple_of(i * CONV_ROWS, CONV_ROWS)
            cat = dp_scr[pl.ds(r0, CONV_ROWS + 8), :]
            du = w[3:4, :] * cat[:CONV_ROWS, :]
            for s in (1, 2, 3):
                du = du + w[3 - s:4 - s, :] * pltpu.roll(cat, CONV_ROWS + 8 - s, 0)[:CONV_ROWS, :]
            du_ref[pl.ds(r0, CONV_ROWS), :] = du.astype(BF16)
            return carry

        lax.fori_loop(0, nrc, chunk2, 0)

    tile = pl.BlockSpec((t, CONV_TILE), lambda c, b: (b, c))
    return pl.pallas_call(
        body, name="conv_silu_bwd", grid=(nct, bl),
        in_specs=[tile, tile, pl.BlockSpec((t, CONV_TILE), lambda c, b: (b, c0 + c)),
                  pl.BlockSpec((4, CONV_TILE), lambda c, b: (0, c)), ANY],
        out_specs=[pl.BlockSpec((t, CONV_TILE), lambda c, b: (b, c0 + c)),
                   pl.BlockSpec((4, CONV_TILE), lambda c, b: (0, c)),
                   pl.BlockSpec((1, CONV_TILE), lambda c, b: (0, c))],
        out_shape=[jax.ShapeDtypeStruct(dpa.shape, dpa.dtype), jax.ShapeDtypeStruct((4, CONV_CH), F32),
                   jax.ShapeDtypeStruct((1, CONV_CH), F32)],
        input_output_aliases={4: 0},
        scratch_shapes=[pltpu.VMEM((t + 8, CONV_TILE), F32)],
        compiler_params=_cparams(("parallel", "arbitrary")),
    )(dxc, dsilu, proj_a, conv_w, dpa)


SSD_FWD_CHUNKS = 4
SSD_BWD_CHUNKS = 1
NT_DIMS = (((1,), (1,)), ((), ()))
TN_DIMS = (((0,), (0,)), ((), ()))


def _dot(a, b, dims=None):
    if dims is None:
        return jnp.dot(a, b, preferred_element_type=F32)
    return lax.dot_general(a, b, dims, preferred_element_type=F32)


def _head_expander():
    r = lax.broadcasted_iota(jnp.int32, (128, SSD_WIDTH), 0)
    c = lax.broadcasted_iota(jnp.int32, (128, SSD_WIDTH), 1)
    return ((c // HEAD_DIM == r % 16) & (r < 48)).astype(BF16)


def _spread(v128, expander):
    hi = v128.astype(BF16).astype(F32)
    r1 = v128 - hi
    mid = r1.astype(BF16).astype(F32)
    lo = (r1 - mid).astype(BF16).astype(F32)
    packed = (hi + pltpu.roll(mid, 16, 1) + pltpu.roll(lo, 32, 1)).astype(BF16)
    return jnp.dot(packed, expander, preferred_element_type=F32)


def _head_sums(v1024, expander):
    hi = v1024.astype(BF16)
    lo = (v1024 - hi.astype(F32)).astype(BF16)
    heads = jnp.where(lax.broadcasted_iota(jnp.int32, expander.shape, 0) < 16, expander, jnp.zeros_like(expander))
    return _dot(hi, heads, NT_DIMS) + _dot(lo, heads, NT_DIMS)


def _ssd_fwd(xc, proj_a, dt, acum, acum_t, dskip_e, norm_w, bl, t):
    n = bl * t
    nch = t // CHUNK
    L = CHUNK

    def body(xc_blk, z_blk, dt_blk, ac_blk, act_blk, dsk_ref, nw_ref, ys_blk, yp_blk, hp_blk, h_scr, y_scr, x_scr):
        @pl.when(pl.program_id(1) == 0)
        def _():
            h_scr[...] = jnp.zeros_like(h_scr)

        for sub in range(SSD_FWD_CHUNKS):
            rows = pl.ds(sub * L, L)
            chunk(xc_blk.at[rows, :], z_blk.at[rows, :], dt_blk.at[rows, :], ac_blk.at[rows, :], act_blk.at[:, rows],
                  dsk_ref, nw_ref, ys_blk.at[rows, :], yp_blk.at[rows, :], hp_blk.at[sub], h_scr, y_scr, x_scr)

    def chunk(xc_ref, z_ref, dt_ref, ac_ref, act_ref, dsk_ref, nw_ref, ys_ref, yp_ref, hp_ref, h_scr, y_scr, x_scr):
        row = lax.broadcasted_iota(jnp.int32, (L, L), 0)
        col = lax.broadcasted_iota(jnp.int32, (L, L), 1)
        causal = row >= col
        lane128 = lax.broadcasted_iota(jnp.int32, (1, L), 1)
        expander = _head_expander()
        ac_all = ac_ref[...]
        act_all = act_ref[...]
        ac_e = _spread(ac_all, expander)
        e_in = jnp.exp(ac_e)
        dec = jnp.exp(ac_e[L - 1:L, :] - ac_e)
        xs_all = xc_ref[:, 0:SSD_WIDTH]
        x_all = xs_all * _spread(dt_ref[...], expander)
        x_scr[...] = x_all.astype(BF16)
        hp_all = h_scr[...]
        hp_ref[...] = hp_all
        for g in range(2):
            gs = slice(g * 512, (g + 1) * 512)
            bg = xc_ref[:, SSD_WIDTH + g * 128:SSD_WIDTH + (g + 1) * 128].astype(BF16)
            cg = xc_ref[:, SSD_WIDTH + 256 + g * 128:SSD_WIDTH + 256 + (g + 1) * 128].astype(BF16)
            gmat = _dot(cg, bg, NT_DIMS)
            y_off = _dot(cg, hp_all[gs, :].astype(BF16), NT_DIMS) * e_in[:, gs] + dsk_ref[:, gs] * xs_all[:, gs]
            s_new = _dot((x_all[:, gs] * dec[:, gs]).astype(BF16), bg, TN_DIMS)
            for pr in range(4):
                pair = slice((g * 4 + pr) * 128, (g * 4 + pr + 1) * 128)
                x_pair = x_scr[:, pair]
                y_pair = y_off[:, pr * 128:(pr + 1) * 128]
                for j in range(2):
                    h = g * 8 + 2 * pr + j
                    sl = slice(h * HEAD_DIM, (h + 1) * HEAD_DIM)
                    r = 2 * pr + j
                    ldec = jnp.exp(jnp.where(causal, ac_all[:, h:h + 1] - act_all[h:h + 1, :], NEG))
                    x_head = jnp.where((lane128 < HEAD_DIM) == (j == 0), x_pair, jnp.zeros_like(x_pair))
                    y_pair = y_pair + _dot((gmat * ldec).astype(BF16), x_head)
                    elast = jnp.exp(ac_all[L - 1:L, h:h + 1])
                    h_scr[sl, :] = elast * hp_all[sl, :] + s_new[r * HEAD_DIM:(r + 1) * HEAD_DIM, :]
                y_scr[:, pair] = y_pair
        y = y_scr[...]
        yp_ref[...] = y
        zv = z_ref[...]
        yg = y * (zv * jax.nn.sigmoid(zv))
        for g in range(2):
            gs = slice(g * 512, (g + 1) * 512)
            grp = yg[:, gs]
            rstd = lax.rsqrt(jnp.mean(grp * grp, axis=1, keepdims=True) + EPS)
            ys_ref[:, gs] = (grp * rstd * nw_ref[:, gs]).astype(BF16)

    cps = SSD_FWD_CHUNKS
    steps = nch // cps
    rb = lambda b, c: (b * steps + c, 0)
    v1k = pl.BlockSpec((1, SSD_WIDTH), lambda b, c: (0, 0))
    return pl.pallas_call(
        body, name="ssd_fwd", grid=(bl, steps),
        in_specs=[pl.BlockSpec((cps * L, CONV_CH), rb), pl.BlockSpec((cps * L, SSD_WIDTH), rb),
                  pl.BlockSpec((cps * L, 128), rb), pl.BlockSpec((cps * L, 128), rb),
                  pl.BlockSpec((16, cps * L), lambda b, c: (0, b * steps + c)), v1k, v1k],
        out_specs=[pl.BlockSpec((cps * L, SSD_WIDTH), rb), pl.BlockSpec((cps * L, SSD_WIDTH), rb),
                   pl.BlockSpec((cps, SSD_WIDTH, SSD_STATE), lambda b, c: (b * steps + c, 0, 0))],
        out_shape=[jax.ShapeDtypeStruct((n, SSD_WIDTH), BF16), jax.ShapeDtypeStruct((n, SSD_WIDTH), F32),
                   jax.ShapeDtypeStruct((bl * nch, SSD_WIDTH, SSD_STATE), F32)],
        scratch_shapes=[pltpu.VMEM((SSD_WIDTH, SSD_STATE), F32), pltpu.VMEM((L, SSD_WIDTH), F32),
                        pltpu.VMEM((L, SSD_WIDTH), BF16)],
        compiler_params=_cparams(("parallel", "arbitrary")),
    )(xc, proj_a, dt, acum, acum_t, dskip_e, norm_w)


def _ssd_bwd(dys, xc, proj_a, ypre, hprev, dt, gate_d, acum, acum_t, alog128, dskip_e, norm_w, bl, t):
    n = bl * t
    nch = t // CHUNK
    L = CHUNK

    def body(dys_blk, xc_blk, z_blk, yp_blk, hp_blk, dt_blk, gd_blk, ac_blk, act_blk, al_ref, dsk_ref, nw_ref,
             dxc_blk, dz_blk, ddt_blk, dnw_ref, dsk16_ref, da16_ref, db16_ref,
             dh_scr, dy_scr, x_scr, dx_scr, red_scr):
        first = (pl.program_id(0) == 0) & (pl.program_id(1) == 0)

        @pl.when(first)
        def _():
            dnw_ref[...] = jnp.zeros_like(dnw_ref)
            dsk16_ref[...] = jnp.zeros_like(dsk16_ref)
            da16_ref[...] = jnp.zeros_like(da16_ref)
            db16_ref[...] = jnp.zeros_like(db16_ref)

        @pl.when(pl.program_id(1) == 0)
        def _():
            dh_scr[...] = jnp.zeros_like(dh_scr)

        for sub in reversed(range(SSD_BWD_CHUNKS)):
            rows = pl.ds(sub * L, L)
            chunk(dys_blk.at[rows, :], xc_blk.at[rows, :], z_blk.at[rows, :], yp_blk.at[rows, :], hp_blk.at[sub],
                  dt_blk.at[rows, :], gd_blk.at[rows, :], ac_blk.at[rows, :], act_blk.at[:, rows], al_ref, dsk_ref,
                  nw_ref, dxc_blk.at[rows, :], dz_blk.at[rows, :], ddt_blk.at[rows, :], dnw_ref, dsk16_ref, da16_ref,
                  db16_ref, dh_scr, dy_scr, x_scr, dx_scr, red_scr)

    def chunk(dys_ref, xc_ref, z_ref, yp_ref, hp_ref, dt_ref, gd_ref, ac_ref, act_ref, al_ref, dsk_ref, nw_ref,
              dxc_ref, dz_ref, ddt_ref, dnw_ref, dsk16_ref, da16_ref, db16_ref,
              dh_scr, dy_scr, x_scr, dx_scr, red_scr):
        y = yp_ref[...]
        zv = z_ref[...]
        sz = jax.nn.sigmoid(zv)
        gate = zv * sz
        yg = y * gate
        dout = dys_ref[...]
        nw = nw_ref[...]
        for g in range(2):
            gs = slice(g * 512, (g + 1) * 512)
            grp = yg[:, gs]
            rstd = lax.rsqrt(jnp.mean(grp * grp, axis=1, keepdims=True) + EPS)
            ghat = grp * rstd
            dnw_ref[:, gs] += jnp.sum(dout[:, gs] * ghat, axis=0, keepdims=True)
            gw = dout[:, gs] * nw[:, gs]
            dyg = rstd * (gw - ghat * jnp.mean(gw * ghat, axis=1, keepdims=True))
            dy_scr[:, gs] = dyg * gate[:, gs]
            dz_ref[:, gs] = (dyg * y[:, gs] * (sz[:, gs] * (1.0 + zv[:, gs] * (1.0 - sz[:, gs])))).astype(BF16)

        row = lax.broadcasted_iota(jnp.int32, (L, L), 0)
        col = lax.broadcasted_iota(jnp.int32, (L, L), 1)
        causal = row >= col
        lane128 = lax.broadcasted_iota(jnp.int32, (1, L), 1)
        rows128 = lax.broadcasted_iota(jnp.int32, (L, 1), 0)
        last_row = rows128 == (L - 1)
        expander = _head_expander()
        ac_all = ac_ref[...]
        act_all = act_ref[...]
        dt_all = dt_ref[...]
        dt_e = _spread(dt_all, expander)
        ac_e = _spread(ac_all, expander)
        e_in = jnp.exp(ac_e)
        dec = jnp.exp(ac_e[L - 1:L, :] - ac_e)
        xs_all = xc_ref[:, 0:SSD_WIDTH]
        x_all = xs_all * dt_e
        x_scr[...] = x_all.astype(BF16)
        dy_all = dy_scr[...]
        hp_all = hp_ref[...]
        ds_all = dh_scr[...]
        dsk_cols = jnp.sum(dy_all * xs_all, axis=0, keepdims=True)
        dac = jnp.zeros((L, L), F32)
        dac_row = jnp.zeros((L, L), F32)
        ddec_cols = []
        for g in range(2):
            gs = slice(g * 512, (g + 1) * 512)
            bsl = slice(SSD_WIDTH + g * 128, SSD_WIDTH + (g + 1) * 128)
            csl = slice(SSD_WIDTH + 256 + g * 128, SSD_WIDTH + 256 + (g + 1) * 128)
            bg = xc_ref[:, bsl].astype(BF16)
            cg = xc_ref[:, csl].astype(BF16)
            gmat = _dot(cg, bg, NT_DIMS)
            hpb = hp_all[gs, :].astype(BF16)
            dsb = ds_all[gs, :].astype(BF16)
            ch = _dot(cg, hpb, NT_DIMS)
            dye = dy_all[:, gs] * e_in[:, gs]
            dyeb = dye.astype(BF16)
            dc_acc = _dot(dyeb, hpb)
            dhp = _dot(dyeb, cg, TN_DIMS)
            dxd = _dot(bg, dsb, NT_DIMS)
            db_acc = _dot((x_all[:, gs] * dec[:, gs]).astype(BF16), dsb)
            ddec = dxd * x_all[:, gs] * dec[:, gs]
            ddec_cols.append(jnp.sum(ddec, axis=0, keepdims=True))
            dx_inter = dxd * dec[:, gs]
            red_scr[:, gs] = dye * ch - ddec
            dg_sum = jnp.zeros((L, L), F32)
            for pr in range(4):
                pair = slice((g * 4 + pr) * 128, (g * 4 + pr + 1) * 128)
                x_pair = x_scr[:, pair]
                dy_pair = dy_scr[:, pair].astype(BF16)
                dx_pair = dx_inter[:, pr * 128:(pr + 1) * 128]
                for j in range(2):
                    h = g * 8 + 2 * pr + j
                    r = 2 * pr + j
                    sl = slice(h * HEAD_DIM, (h + 1) * HEAD_DIM)
                    onehot_w = lane128 == h
                    ldec = jnp.exp(jnp.where(causal, ac_all[:, h:h + 1] - act_all[h:h + 1, :], NEG))
                    mf = gmat * ldec
                    dyb = jnp.where((lane128 < HEAD_DIM) == (j == 0), dy_pair, jnp.zeros_like(dy_pair))
                    dm = _dot(dyb, x_pair, NT_DIMS)
                    dx_pair = dx_pair + _dot(mf.astype(BF16), dyb, TN_DIMS)
                    dg_sum = dg_sum + dm * ldec
                    wmat = dm * mf
                    elast = jnp.exp(ac_all[L - 1:L, h:h + 1])
                    hp_h = hp_all[sl, :]
                    ds_h = ds_all[sl, :]
                    extra = elast * jnp.sum(jnp.sum(hp_h * ds_h, axis=1, keepdims=True), axis=0, keepdims=True)
                    dac = dac + jnp.where(onehot_w,
                                          jnp.sum(wmat, axis=1, keepdims=True) + jnp.where(last_row, extra, 0.0), 0.0)
                    dac_row = dac_row + jnp.where(rows128 == h, -jnp.sum(wmat, axis=0, keepdims=True), 0.0)
                    dh_scr[sl, :] = elast * ds_h + dhp[r * HEAD_DIM:(r + 1) * HEAD_DIM, :]
                dx_scr[:, pair] = dx_pair
            dgb = dg_sum.astype(BF16)
            dxc_ref[:, csl] = dc_acc + _dot(dgb, bg)
            dxc_ref[:, bsl] = db_acc + _dot(dgb, cg, TN_DIMS)
        dx_all = dx_scr[...]
        dxc_ref[:, 0:SSD_WIDTH] = dx_all * dt_e + dsk_ref[...] * dy_all
        red = red_scr[...]
        dac_slab = _head_sums(red, expander)
        ddec_tot = _head_sums(jnp.broadcast_to(jnp.concatenate(ddec_cols, axis=1), (8, SSD_WIDTH)), expander)
        ddt_x = _head_sums(dx_all * xs_all, expander)
        dsk16_ref[...] += _head_sums(jnp.broadcast_to(dsk_cols, (8, SSD_WIDTH)), expander)[0:1, 0:16]
        dac = dac + dac_slab + jnp.transpose(dac_row) + jnp.where(last_row, ddec_tot[0:1, :], 0.0)
        triu = (row <= col).astype(F32)
        da = jnp.dot(triu, dac, precision=HIGHEST, preferred_element_type=F32)
        a_row = -jnp.exp(al_ref[...])
        ddt = jnp.where(lane128 < 16, (ddt_x + da * a_row) * gd_ref[...], 0.0)
        ddt_ref[...] = ddt
        da16_ref[...] += (jnp.sum(da * dt_all, axis=0, keepdims=True) * a_row)[:, 0:16]
        db16_ref[...] += jnp.sum(ddt, axis=0, keepdims=True)[:, 0:16]

    cps = SSD_BWD_CHUNKS
    steps = nch // cps
    rb = lambda b, c: (b * steps + steps - 1 - c, 0)
    v1k = pl.BlockSpec((1, SSD_WIDTH), lambda b, c: (0, 0))
    v16 = pl.BlockSpec((1, 16), lambda b, c: (0, 0))
    v128 = pl.BlockSpec((1, 128), lambda b, c: (0, 0))
    wide = pl.BlockSpec((cps * L, SSD_WIDTH), rb)
    s128 = pl.BlockSpec((cps * L, 128), rb)
    return pl.pallas_call(
        body, name="ssd_bwd", grid=(bl, steps),
        in_specs=[wide, pl.BlockSpec((cps * L, CONV_CH), rb), wide, wide,
                  pl.BlockSpec((cps, SSD_WIDTH, SSD_STATE), lambda b, c: (b * steps + steps - 1 - c, 0, 0)),
                  s128, s128, s128, pl.BlockSpec((16, cps * L), lambda b, c: (0, b * steps + steps - 1 - c)),
                  v128, v1k, v1k],
        out_specs=[pl.BlockSpec((cps * L, CONV_CH), rb), wide, s128, v1k, v16, v16, v16],
        out_shape=[jax.ShapeDtypeStruct((n, CONV_CH), F32), jax.ShapeDtypeStruct((n, PA_WIDTH), BF16),
                   jax.ShapeDtypeStruct((n, 128), F32), jax.ShapeDtypeStruct((1, SSD_WIDTH), F32),
                   jax.ShapeDtypeStruct((1, 16), F32), jax.ShapeDtypeStruct((1, 16), F32),
                   jax.ShapeDtypeStruct((1, 16), F32)],
        scratch_shapes=[pltpu.VMEM((SSD_WIDTH, SSD_STATE), F32), pltpu.VMEM((L, SSD_WIDTH), F32),
                        pltpu.VMEM((L, SSD_WIDTH), BF16), pltpu.VMEM((L, SSD_WIDTH), F32),
                        pltpu.VMEM((L, SSD_WIDTH), F32)],
        compiler_params=_cparams(("arbitrary", "arbitrary")),
    )(dys, xc, proj_a, ypre, hprev, dt, gate_d, acum, acum_t, alog128, dskip_e, norm_w)


def _attn_fwd(qkv, negc, bl, t):
    n = bl * t
    tb_ = min(t, ATT_FWD_BLOCK)
    nb = t // tb_
    scale2 = LOG2E / math.sqrt(HEAD_DIM)

    def body(q_ref, k_ref, v_ref, c_ref, o_ref, lse_ref, v0_scr, v1_scr, k0_scr, k1_scr):
        row = lax.broadcasted_iota(jnp.int32, (tb_, tb_), 0)
        col = lax.broadcasted_iota(jnp.int32, (tb_, tb_), 1)
        causal = row >= col
        lane = lax.broadcasted_iota(jnp.int32, (1, 128), 1)
        v_pair = v_ref[...].astype(F32)
        k_pair = k_ref[...]
        v_scrs = (v0_scr, v1_scr)
        k_scrs = (k0_scr, k1_scr)
        for j in range(2):
            v_head = v_pair if j == 0 else pltpu.roll(v_pair, HEAD_DIM, 1)
            v_scrs[j][...] = jnp.where(lane < HEAD_DIM, v_head, jnp.where(lane == HEAD_DIM, 1.0, 0.0)).astype(BF16)
            k_scrs[j][...] = jnp.where((lane < HEAD_DIM) == (j == 0), k_pair, jnp.zeros_like(k_pair))
        for qi in range(nb):
            r0, lk = qi * tb_, (qi + 1) * tb_
            for j in range(2):
                sl = slice(j * HEAD_DIM, (j + 1) * HEAD_DIM)
                s = _dot(q_ref[r0:lk, :], k_scrs[j][0:lk, :], NT_DIMS) * scale2 + c_ref[j:j + 1, 0:lk] * LOG2E
                tail = jnp.where(causal, s[:, r0:lk], NEG)
                s = tail if qi == 0 else jnp.concatenate([s[:, 0:r0], tail], axis=1)
                m = jnp.max(s, axis=1, keepdims=True)
                p = jnp.exp2(s - m)
                acc = _dot(p.astype(BF16), v_scrs[j][0:lk, :])
                l = acc[:, HEAD_DIM:HEAD_DIM + 1]
                o_ref[r0:lk, sl] = (acc[:, 0:HEAD_DIM] / l).astype(BF16)
                lse_ref[r0:lk, sl] = jnp.broadcast_to(m + jnp.log(l) * LOG2E, (tb_, HEAD_DIM))

    blk = lambda off: pl.BlockSpec((t, 128), lambda b, hp: (b, off + hp))
    return pl.pallas_call(
        body, name="fox_attn_fwd", grid=(bl, 8),
        in_specs=[blk(0), blk(8), blk(16), pl.BlockSpec((None, None, 8, t), lambda b, hp: (b, hp, 0, 0))],
        out_specs=[blk(0), blk(0)],
        out_shape=[jax.ShapeDtypeStruct((n, ATT_WIDTH), BF16), jax.ShapeDtypeStruct((n, ATT_WIDTH), F32)],
        scratch_shapes=[pltpu.VMEM((t, 128), BF16)] * 4,
        compiler_params=_cparams(("parallel", "parallel")),
    )(qkv, qkv, qkv, negc)


def _attn_bwd(qkv, do, o, lse, negc, after, bl, t):
    n = bl * t
    tb_ = min(t, ATT_BWD_BLOCK)
    nb = t // tb_
    scale = 1.0 / math.sqrt(HEAD_DIM)
    scale2 = LOG2E * scale

    def body(q_ref, k_ref, v_ref, do_ref, o_ref, lse_ref, c_ref, after_ref, dq_ref, dk_ref, dv_ref, dc_ref,
             dq0_scr, delta_scr, dq1_scr, qt0_scr, qt1_scr, dot_scr, dkt0_scr, dkt1_scr, dvt_scr):
        row = lax.broadcasted_iota(jnp.int32, (tb_, tb_), 0)
        col = lax.broadcasted_iota(jnp.int32, (tb_, tb_), 1)
        causal = row >= col
        lane = lax.broadcasted_iota(jnp.int32, (1, 128), 1)
        dq_scrs = (dq0_scr, dq1_scr)
        qt_scrs = (qt0_scr, qt1_scr)
        dkt_scrs = (dkt0_scr, dkt1_scr)
        dq0_scr[...] = jnp.zeros_like(dq0_scr)
        dq1_scr[...] = jnp.zeros_like(dq1_scr)
        dc_ref[...] = jnp.zeros_like(dc_ref)
        q_t = jnp.transpose(q_ref[...].astype(F32))
        ones_row = jnp.where(lax.broadcasted_iota(jnp.int32, (8, t), 0) == 0, 1.0, 0.0)
        for j in range(2):
            qt_scrs[j][...] = jnp.concatenate(
                [q_t[j * HEAD_DIM:(j + 1) * HEAD_DIM, :], ones_row, jnp.zeros((HEAD_DIM - 8, t), F32)],
                axis=0).astype(BF16)
        dot_scr[...] = jnp.transpose(do_ref[...].astype(F32)).astype(BF16)
        prod = do_ref[...].astype(F32) * o_ref[...].astype(F32)
        for j in range(2):
            sl = slice(j * HEAD_DIM, (j + 1) * HEAD_DIM)
            delta_scr[:, sl] = jnp.broadcast_to(jnp.sum(prod[:, sl], axis=1, keepdims=True), (t, HEAD_DIM))
        for kj in range(nb):
            r0, r1 = kj * tb_, (kj + 1) * tb_
            k_blk = k_ref[r0:r1, :]
            v_blk = v_ref[r0:r1, :]
            k_pair = k_blk.astype(F32)
            for j in range(2):
                sl = slice(j * HEAD_DIM, (j + 1) * HEAD_DIM)
                one = slice(j * HEAD_DIM, j * HEAD_DIM + 1)
                own = (lane < HEAD_DIM) == (j == 0)
                k_head = k_pair if j == 0 else pltpu.roll(k_pair, HEAD_DIM, 1)
                k_ones = jnp.where(lane < HEAD_DIM, k_head, jnp.where(lane == HEAD_DIM, 1.0, 0.0)).astype(BF16)
                s = (_dot(q_ref[r0:t, :], jnp.where(own, k_blk, jnp.zeros_like(k_blk)), NT_DIMS) * scale2
                     + c_ref[j:j + 1, r0:r1] * LOG2E)
                head = jnp.where(causal, s[0:tb_, :], NEG)
                s = head if kj == nb - 1 else jnp.concatenate([head, s[tb_:, :]], axis=0)
                p = jnp.exp2(s - lse_ref[r0:t, one])
                dp = _dot(do_ref[r0:t, :], jnp.where(own, v_blk, jnp.zeros_like(v_blk)), NT_DIMS)
                ds = p * (dp - delta_scr[r0:t, one])
                dsb = ds.astype(BF16)
                dvt_scr[sl, r0:r1] = _dot(dot_scr[sl, r0:t], p.astype(BF16))
                dkt_scrs[j][:, r0:r1] = _dot(qt_scrs[j][:, r0:t], dsb)
                dq_scrs[j][r0:t, :] += _dot(dsb, k_ones)
        dv_ref[...] = jnp.transpose(dvt_scr[...]).astype(BF16)
        for j in range(2):
            sl = slice(j * HEAD_DIM, (j + 1) * HEAD_DIM)
            acc = dq_scrs[j][...]
            dkt = dkt_scrs[j][...]
            dq_ref[:, sl] = (acc[:, 0:HEAD_DIM] * scale).astype(BF16)
            dk_ref[:, sl] = (jnp.transpose(dkt)[:, 0:HEAD_DIM] * scale).astype(BF16)
            dc_ref[j:j + 1, :] = jnp.transpose(acc)[HEAD_DIM:HEAD_DIM + 1, :] - dkt[HEAD_DIM:HEAD_DIM + 1, :]

    blk = lambda off: pl.BlockSpec((t, 128), lambda b, hp: (b, off + hp))
    cblk = pl.BlockSpec((None, None, 8, t), lambda b, hp: (b, hp, 0, 0))
    return pl.pallas_call(
        body, name="fox_attn_bwd", grid=(bl, 8),
        in_specs=[blk(0), blk(8), blk(16), blk(0), blk(0), blk(0), cblk, ANY],
        out_specs=[blk(0), blk(0), blk(0), cblk],
        out_shape=[jax.ShapeDtypeStruct((n, ATT_WIDTH), BF16)] * 3 + [jax.ShapeDtypeStruct((bl, 8, 8, t), F32)],
        scratch_shapes=[pltpu.VMEM((t, 128), F32), pltpu.VMEM((t, 128), F32), pltpu.VMEM((t, 128), F32),
                        pltpu.VMEM((128, t), BF16), pltpu.VMEM((128, t), BF16), pltpu.VMEM((128, t), BF16),
                        pltpu.VMEM((128, t), F32), pltpu.VMEM((128, t), F32), pltpu.VMEM((128, t), F32)],
        compiler_params=_cparams(("parallel", "parallel")),
    )(qkv, qkv, qkv, do, o, lse, negc, after)


def _adamw(w, g, m, v, *, name):
    lead = w.ndim == 3
    r, c = w.shape[-2:]
    tr = _pick(r, (256, IN_SHARD // 3, 128, 64, 32, 16, 8))
    bc1 = 1.0 - ADAM_B1 ** ADAM_STEP
    bc2 = 1.0 - ADAM_B2 ** ADAM_STEP

    def body(w_ref, g_ref, m_ref, v_ref, d_ref, nm_ref, nv_ref):
        gv = g_ref[...]
        mn = ADAM_B1 * m_ref[...] + (1.0 - ADAM_B1) * gv
        vn = ADAM_B2 * v_ref[...] + (1.0 - ADAM_B2) * (gv * gv)
        m_hat = mn / bc1
        v_hat = vn / bc2
        d_ref[...] = -ADAM_LR * (m_hat / (jnp.sqrt(v_hat) + ADAM_EPS) + ADAM_WD * w_ref[...])
        nm_ref[...] = mn
        nv_ref[...] = vn

    flat = pl.BlockSpec((tr, c), lambda i: (i, 0))
    blk = pl.BlockSpec((None, tr, c), lambda i: (0, i, 0)) if lead else flat
    return pl.pallas_call(
        body, name=name, grid=(r // tr,), in_specs=[blk, flat, blk, blk], out_specs=[blk] * 3,
        out_shape=[jax.ShapeDtypeStruct(w.shape, F32)] * 3,
        compiler_params=_cparams(("parallel",)),
    )(w, g, m, v)


def _sum_leading(parts, *, name, out_dtype=F32):
    k, r, c = parts.shape
    tr = _pick(r, (512, 256, 128, 96, 64, 32, 16, 8))

    def body(p_ref, o_ref):
        acc = p_ref[0].astype(F32)
        for i in range(1, k):
            acc = acc + p_ref[i].astype(F32)
        o_ref[...] = acc.astype(out_dtype)

    return pl.pallas_call(
        body, name=name, grid=(r // tr,),
        in_specs=[pl.BlockSpec((k, tr, c), lambda i: (0, i, 0))],
        out_specs=pl.BlockSpec((tr, c), lambda i: (i, 0)),
        out_shape=jax.ShapeDtypeStruct((r, c), out_dtype),
        compiler_params=_cparams(("parallel",)),
    )(parts)


def _add_my_half(g, b, *, name):
    k, r, c = b.shape
    tr = _pick(r, (512, 256, 128))
    nrt = r // tr

    def body(lo_ref, hi_ref, b_ref, o_ref):
        mine = jnp.where(lax.axis_index("c") == 0, lo_ref[...], hi_ref[...])
        o_ref[...] = (mine.astype(F32) + b_ref[...].astype(F32)).astype(BF16)

    blk = pl.BlockSpec((None, tr, c), lambda j, i: (j, i, 0))
    return pl.pallas_call(
        body, name=name, grid=(k, nrt),
        in_specs=[blk, pl.BlockSpec((None, tr, c), lambda j, i: (j, i + nrt, 0)), blk], out_specs=blk,
        out_shape=jax.ShapeDtypeStruct((k, r, c), BF16),
        compiler_params=_cparams(("parallel", "parallel")),
    )(g, g, b)


ANY = pl.BlockSpec(memory_space=pl.ANY)


def _chip_peers(x, y):
    return [(1 - x, y, 2 * (1 - x) + y), (x, 1 - y, 2 * x + 1 - y), (1 - x, 1 - y, 2 * (1 - x) + 1 - y)]


def _gather_weights(blob, *, name):
    rows, cols = blob.shape
    half_rows = rows // 2

    def body(b_ref, o_ref, send_sems, recv_sems):
        x, y, c = lax.axis_index("x"), lax.axis_index("y"), lax.axis_index("c")
        me = 2 * x + y
        sibling = (x, y, 1 - c)
        peers = _chip_peers(x, y)

        def half(chip, hc):
            return o_ref.at[chip, pl.ds(hc * half_rows, half_rows), :]

        def copy(k, src, chip, hc, to):
            return pltpu.make_async_remote_copy(src_ref=src, dst_ref=half(chip, hc), send_sem=send_sems.at[k],
                                                recv_sem=recv_sems.at[k], device_id=to, device_id_type=MESH)

        my_half = b_ref.at[pl.ds(c * half_rows, half_rows), :]
        first = [copy(k, my_half, me, c, (px, py, c)) for k, (px, py, _) in enumerate(peers)]
        own = pltpu.make_async_remote_copy(src_ref=b_ref, dst_ref=o_ref.at[me], send_sem=send_sems.at[6],
                                           recv_sem=recv_sems.at[6], device_id=sibling, device_id_type=MESH)
        for cp in first + [own]:
            cp.start()
        passed = [copy(3 + k, half(pc, c), pc, c, sibling) for k, (_, _, pc) in enumerate(peers)]
        for k, (px, py, pc) in enumerate(peers):
            copy(k, my_half, pc, c, (px, py, c)).wait_recv()
            passed[k].start()
        for k, (_, _, pc) in enumerate(peers):
            copy(3 + k, half(pc, 1 - c), pc, 1 - c, sibling).wait_recv()
        own.wait_recv()
        for cp in first + passed + [own]:
            cp.wait_send()

    return pl.pallas_call(
        body, name=name, in_specs=[ANY], out_specs=ANY,
        out_shape=jax.ShapeDtypeStruct((N_CHIPS, rows, cols), BF16),
        scratch_shapes=[pltpu.SemaphoreType.DMA((7,)), pltpu.SemaphoreType.DMA((7,))],
    )(blob)


def _swap_halves(g, *, name):
    _, rows, cols = g.shape
    half_rows = rows // 2

    def body(g_ref, o_ref, send_sem, recv_sem):
        x, y, c = lax.axis_index("x"), lax.axis_index("y"), lax.axis_index("c")
        cp = pltpu.make_async_remote_copy(
            src_ref=g_ref.at[:, pl.ds((1 - c) * half_rows, half_rows), :], dst_ref=o_ref,
            send_sem=send_sem, recv_sem=recv_sem, device_id=(x, y, 1 - c), device_id_type=MESH)
        cp.start()
        cp.wait()

    return pl.pallas_call(
        body, name=name, in_specs=[ANY], out_specs=ANY,
        out_shape=jax.ShapeDtypeStruct((N_CHIPS, half_rows, cols), BF16),
        scratch_shapes=[pltpu.SemaphoreType.DMA, pltpu.SemaphoreType.DMA],
    )(g)


HBM_SPEC = pl.BlockSpec(memory_space=pltpu.HBM)
SEM_SPEC = pl.BlockSpec(memory_space=pltpu.SEMAPHORE)
SPLIT_EFFECT = pltpu.SideEffectType.DATAFLOW_SIDE_EFFECTING


def _gather_peers_copies(b_ref, land_ref, send_sems, recv_sems, sending):
    x, y, c = lax.axis_index("x"), lax.axis_index("y"), lax.axis_index("c")
    me = 2 * x + y
    half_rows = b_ref.shape[0] // 2
    src = b_ref.at[pl.ds(c * half_rows, half_rows), :]
    return [pltpu.make_async_remote_copy(
        src_ref=src, dst_ref=land_ref.at[me if sending else pc, pl.ds(c * half_rows, half_rows), :],
        send_sem=send_sems.at[k], recv_sem=recv_sems.at[k], device_id=(px, py, c), device_id_type=MESH)
        for k, (px, py, pc) in enumerate(_chip_peers(x, y))]


def _gather_start(blob, after, *, name):
    shape = (N_CHIPS,) + blob.shape

    def body(b_ref, land_ref, after_ref, send_sems, recv_sems, b_thru, land_thru, token):
        for cp in _gather_peers_copies(b_ref, land_ref, send_sems, recv_sems, True):
            cp.start()
        token[...] = jnp.zeros_like(token)

    return pl.pallas_call(
        body, name=name,
        out_shape=(pltpu.SemaphoreType.DMA((3,)), pltpu.SemaphoreType.DMA((3,)), pltpu.HBM(blob.shape, blob.dtype),
                   pltpu.HBM(shape, blob.dtype), jax.ShapeDtypeStruct((8, 128), F32)),
        in_specs=(HBM_SPEC, HBM_SPEC, ANY),
        out_specs=(SEM_SPEC, SEM_SPEC, HBM_SPEC, HBM_SPEC, pl.BlockSpec(memory_space=pltpu.VMEM)),
        input_output_aliases={0: 2, 1: 3},
        compiler_params=pltpu.CompilerParams(has_side_effects=SPLIT_EFFECT),
    )(pltpu.with_memory_space_constraint(blob, pltpu.HBM),
      pltpu.with_memory_space_constraint(lax.empty(shape, blob.dtype), pltpu.HBM), after)


def _gather_wait(send_sems, recv_sems, b_thru, land_thru, after, *, name):
    def body(b_ref, land_ref, send_sems, recv_sems, after_ref, b_dead, got_ref):
        for cp in _gather_peers_copies(b_ref, land_ref, send_sems, recv_sems, False):
            cp.wait_send()
            cp.wait_recv()

    return pl.pallas_call(
        body, name=name,
        out_shape=(pltpu.HBM(b_thru.shape, b_thru.dtype), pltpu.HBM(land_thru.shape, land_thru.dtype)),
        in_specs=(HBM_SPEC, HBM_SPEC, SEM_SPEC, SEM_SPEC, ANY), out_specs=(HBM_SPEC, HBM_SPEC),
        input_output_aliases={0: 0, 1: 1},
        compiler_params=pltpu.CompilerParams(has_side_effects=SPLIT_EFFECT),
    )(b_thru, land_thru, send_sems, recv_sems, after)


def _gather_forward(land, blob, *, name):
    half_rows = land.shape[1] // 2

    def body(l_ref, b_ref, o_ref, send_sems, recv_sems):
        x, y, c = lax.axis_index("x"), lax.axis_index("y"), lax.axis_index("c")
        me = 2 * x + y
        sibling = (x, y, 1 - c)
        cps = []
        for k, (_, _, pc) in enumerate(_chip_peers(x, y)):
            mine = pl.ds(c * half_rows, half_rows)
            cps.append(pltpu.make_async_remote_copy(
                src_ref=l_ref.at[pc, mine, :], dst_ref=o_ref.at[pc, mine, :], send_sem=send_sems.at[k],
                recv_sem=recv_sems.at[k], device_id=sibling, device_id_type=MESH))
        cps.append(pltpu.make_async_remote_copy(src_ref=b_ref, dst_ref=o_ref.at[me], send_sem=send_sems.at[3],
                                                recv_sem=recv_sems.at[3], device_id=sibling, device_id_type=MESH))
        for cp in cps:
            cp.start()
        for k, (_, _, pc) in enumerate(_chip_peers(x, y)):
            theirs = pl.ds((1 - c) * half_rows, half_rows)
            pltpu.make_async_remote_copy(
                src_ref=l_ref.at[pc, theirs, :], dst_ref=o_ref.at[pc, theirs, :], send_sem=send_sems.at[k],
                recv_sem=recv_sems.at[k], device_id=sibling, device_id_type=MESH).wait_recv()
        cps[3].wait_recv()
        for cp in cps:
            cp.wait_send()

    return pl.pallas_call(
        body, name=name, in_specs=[ANY, ANY], out_specs=ANY, input_output_aliases={0: 0},
        out_shape=jax.ShapeDtypeStruct(land.shape, land.dtype),
        scratch_shapes=[pltpu.SemaphoreType.DMA((4,)), pltpu.SemaphoreType.DMA((4,))],
    )(land, blob)


def _exchange_peers_copies(p_ref, land_ref, send_sems, recv_sems, sending):
    x, y, c = lax.axis_index("x"), lax.axis_index("y"), lax.axis_index("c")
    me = 2 * x + y
    return [pltpu.make_async_remote_copy(src_ref=p_ref.at[pc], dst_ref=land_ref.at[me if sending else pc],
                                         send_sem=send_sems.at[k], recv_sem=recv_sems.at[k],
                                         device_id=(px, py, c), device_id_type=MESH)
            for k, (px, py, pc) in enumerate(_chip_peers(x, y))]


def _exchange_start(p, *, name):
    def body(p_ref, land_ref, send_sems, recv_sems, p_thru, land_thru, token):
        for cp in _exchange_peers_copies(p_ref, land_ref, send_sems, recv_sems, True):
            cp.start()
        token[...] = jnp.zeros_like(token)

    return pl.pallas_call(
        body, name=name,
        out_shape=(pltpu.SemaphoreType.DMA((3,)), pltpu.SemaphoreType.DMA((3,)), pltpu.HBM(p.shape, p.dtype),
                   pltpu.HBM(p.shape, p.dtype), jax.ShapeDtypeStruct((8, 128), F32)),
        in_specs=(HBM_SPEC, HBM_SPEC),
        out_specs=(SEM_SPEC, SEM_SPEC, HBM_SPEC, HBM_SPEC, pl.BlockSpec(memory_space=pltpu.VMEM)),
        input_output_aliases={0: 2, 1: 3},
        compiler_params=pltpu.CompilerParams(has_side_effects=SPLIT_EFFECT),
    )(pltpu.with_memory_space_constraint(p, pltpu.HBM),
      pltpu.with_memory_space_constraint(lax.empty(p.shape, p.dtype), pltpu.HBM))


def _exchange_wait(send_sems, recv_sems, p_thru, land_thru, after, *, name):
    def body(p_ref, land_ref, send_sems, recv_sems, after_ref, p_dead, got_ref):
        for cp in _exchange_peers_copies(p_ref, land_ref, send_sems, recv_sems, False):
            cp.wait_send()
            cp.wait_recv()

    return pl.pallas_call(
        body, name=name,
        out_shape=(pltpu.HBM(p_thru.shape, p_thru.dtype), pltpu.HBM(p_thru.shape, p_thru.dtype)),
        in_specs=(HBM_SPEC, HBM_SPEC, SEM_SPEC, SEM_SPEC, ANY), out_specs=(HBM_SPEC, HBM_SPEC),
        input_output_aliases={0: 0, 1: 1},
        compiler_params=pltpu.CompilerParams(has_side_effects=SPLIT_EFFECT),
    )(p_thru, land_thru, send_sems, recv_sems, after)


def _sum_parts(parts, own, *, name):
    k, r, c = parts.shape
    tr = _pick(r, (512, 256, 128))

    def body(p_ref, own_ref, o_ref):
        me = 2 * lax.axis_index("x") + lax.axis_index("y")
        acc = jnp.zeros((tr, c), F32)
        for i in range(k):
            acc = acc + jnp.where(me == i, own_ref[i], p_ref[i]).astype(F32)
        o_ref[...] = acc

    blk = pl.BlockSpec((k, tr, c), lambda i: (0, i, 0))
    return pl.pallas_call(
        body, name=name, grid=(r // tr,), in_specs=[blk, blk],
        out_specs=pl.BlockSpec((tr, c), lambda i: (i, 0)),
        out_shape=jax.ShapeDtypeStruct((r, c), F32),
        compiler_params=_cparams(("parallel",)),
    )(parts, own)


def _join_halves(gh, *, name):
    def body(g_ref, o_ref, send_sem, recv_sem):
        x, y, c = lax.axis_index("x"), lax.axis_index("y"), lax.axis_index("c")
        cp = pltpu.make_async_remote_copy(src_ref=g_ref, dst_ref=o_ref, send_sem=send_sem, recv_sem=recv_sem,
                                          device_id=(x, y, 1 - c), device_id_type=MESH)
        cp.start()
        cp.wait()

    other = pl.pallas_call(
        body, name=name, in_specs=[ANY], out_specs=ANY,
        out_shape=jax.ShapeDtypeStruct(gh.shape, F32),
        scratch_shapes=[pltpu.SemaphoreType.DMA, pltpu.SemaphoreType.DMA],
    )(gh)
    south = lax.axis_index("c") == 0
    return jnp.concatenate([jnp.where(south, gh, other), jnp.where(south, other, gh)], axis=0)


def _gather_small(s, *, name):
    rows = s.shape[0]

    def body(s_ref, o_ref, send_sems, recv_sems, local_sem):
        x, y, c = lax.axis_index("x"), lax.axis_index("y"), lax.axis_index("c")
        me = 4 * x + 2 * y + c
        mine = pltpu.make_async_copy(s_ref, o_ref.at[me], local_sem)
        mine.start()
        peers = []
        for k in range(1, 8):
            peers.append((1 - x if k & 4 else x, 1 - y if k & 2 else y, 1 - c if k & 1 else c))
        cps = [pltpu.make_async_remote_copy(src_ref=s_ref, dst_ref=o_ref.at[me], send_sem=send_sems.at[k],
                                            recv_sem=recv_sems.at[k], device_id=p, device_id_type=MESH)
               for k, p in enumerate(peers)]
        for cp in cps:
            cp.start()
        for k, (px, py, pc) in enumerate(peers):
            pltpu.make_async_remote_copy(src_ref=s_ref, dst_ref=o_ref.at[4 * px + 2 * py + pc],
                                         send_sem=send_sems.at[k], recv_sem=recv_sems.at[k],
                                         device_id=(px, py, pc), device_id_type=MESH).wait_recv()
        for cp in cps:
            cp.wait_send()
        mine.wait()

    return pl.pallas_call(
        body, name=name, in_specs=[ANY], out_specs=ANY,
        out_shape=jax.ShapeDtypeStruct((8, rows, 128), F32),
        scratch_shapes=[pltpu.SemaphoreType.DMA((7,)), pltpu.SemaphoreType.DMA((7,)), pltpu.SemaphoreType.DMA],
    )(s)


IN_SHARD = IN_WIDTH // N_CHIPS
IN_SHARD_PAD = 1536
UP_ROWS, DOWN_ROWS, OUT_ROWS = 1024, 1024, 512
REST_ROWS = UP_ROWS + DOWN_ROWS + OUT_ROWS


def _pack_in(w_in_s):
    return jnp.pad(w_in_s, ((0, 0), (0, IN_SHARD_PAD - IN_SHARD))).astype(BF16)


def _pack_rest(w_out_s, w_up_s, w_down_s):
    return jnp.concatenate([w_up_s, w_down_s, w_out_s], axis=0).astype(BF16)


def _unpack_rest(blob):
    return (blob[UP_ROWS + DOWN_ROWS:], blob[0:UP_ROWS], blob[UP_ROWS:UP_ROWS + DOWN_ROWS])


def _full_w_in(g_in):
    return jnp.concatenate([g_in[j, :, :IN_SHARD] for j in range(N_CHIPS)], axis=1)


def _full_rest(g_rest):
    parts = [_unpack_rest(g_rest[j]) for j in range(N_CHIPS)]
    w_out = jnp.concatenate([p[0] for p in parts], axis=0)
    w_up = jnp.concatenate([p[1] for p in parts], axis=1)
    w_down = jnp.concatenate([p[2] for p in parts], axis=0)
    return w_out, w_up, w_down


def _split_w_in(w_in):
    z_xbc = w_in[:, 0:2560]
    dt = w_in[:, 2560:2576]
    qkv = w_in[:, 2576:5648]
    f = w_in[:, 5648:5664]
    pad = jnp.zeros((w_in.shape[0], PA_WIDTH - 2592), w_in.dtype)
    return jnp.concatenate([z_xbc, dt, f, pad], axis=1), qkv


def _merge_w_in(d_a, d_qkv):
    return jnp.concatenate([d_a[:, 0:2560], d_a[:, 2560:2576], d_qkv, d_a[:, 2576:2592]], axis=1)


def _local_step(x3, target3, w_in, rest_weights, norm_mix_w, conv_w, conv_b, dt_bias, a_log, d_skip,
                ssd_norm_w, f_bias, norm_mlp_w, norm_final_w, first_after=None, early_grads=None, late_grads=None):
    bl, t, d = x3.shape
    n = bl * t
    x = x3.reshape(n, d)
    target = target3.reshape(n, d)
    w_a, w_qkv = _split_w_in(w_in)
    nfw = norm_final_w.reshape(1, d)
    dskip_e = jnp.repeat(d_skip, HEAD_DIM, axis=1)

    r1, r2, kt = min(n, 1024), min(n, 512), min(n, 2048)
    if first_after is None:
        first_after = jnp.zeros((8, 128), F32)
    h0, rstd0, proj_a = _norm_mm(x, norm_mix_w, w_a, first_after, name="norm_mix_proj_a", tm=r2)
    qkv = _mm(h0, w_qkv, name="proj_qkv", tiles=(r2, QKV_WIDTH, D_MODEL), out_dtype=BF16)
    bias128 = jnp.concatenate([dt_bias, f_bias, jnp.zeros((1, 96), F32)], axis=1)
    alog128 = jnp.concatenate([a_log, jnp.zeros((1, 112), F32)], axis=1)
    dt, gate_d, acum, acum_t, negc = _prep(proj_a, bias128, alog128, bl, t)
    xc, dsilu = _conv_fwd(proj_a, conv_w, conv_b, bl, t)
    y_ssd, y_pre, hprev = _ssd_fwd(xc, proj_a, dt, acum, acum_t, dskip_e, ssd_norm_w, bl, t)
    y_att, lse = _attn_fwd(qkv, negc, bl, t)
    w_out, w_up, w_down = rest_weights(y_att)
    wo_s, wo_a = w_out[:SSD_WIDTH], w_out[SSD_WIDTH:]
    h1, h1n, rstd1 = _mm_norm_fwd(y_ssd, wo_s, y_att, wo_a, x, norm_mlp_w, name="out_proj_norm_mlp", tm=r2)
    up = _mm(h1n, w_up, name="mlp_up", tiles=(r1, D_FF, D_MODEL), out_dtype=BF16)
    dh2, dh2b, loss, d_nfw = _mm_final(up, w_down, h1, nfw, target, name="mlp_down_final_norm_loss", tm=r2,
                                       a_act="relu2")

    dup = _mm(dh2b, w_down, name="mlp_down_bwd_act", tiles=(r2, D_FF, D_MODEL), tb=True, epi_up=up, out_dtype=BF16)
    rest_shape = (N_CHIPS, REST_ROWS, D_MODEL)
    gb_rest = _mm(up, dh2b, name="mlp_down_bwd_w", tiles=(DOWN_ROWS, D_MODEL, kt), ta=True, a_act="relu2",
                  out_dtype=BF16, into=(rest_shape, (None, DOWN_ROWS, D_MODEL), lambda i, j, k: (i, 1, 0), None))
    dh1, dh1b, d_nmlp = _mm_norm_bwd([(dup, w_up)], h1, rstd1, norm_mlp_w, dh2, name="mlp_up_bwd_act_norm_mlp",
                                     tm=r2)
    gb_rest = _mm(h1n, dup, name="mlp_up_bwd_w", tiles=(D_MODEL, UP_ROWS, kt), ta=True, out_dtype=BF16,
                  into=(rest_shape, (None, D_MODEL, UP_ROWS), lambda i, j, k: (j, 0, 0), gb_rest))
    dys, do = _mm_two_halves(dh1b, w_out, name="out_proj_bwd_act", tm=r1)
    out_block = (UP_ROWS + DOWN_ROWS) // OUT_ROWS
    for half, (y_half, tag) in enumerate(((y_ssd, "ssd"), (y_att, "att"))):
        gb_rest = _mm(y_half, dh1b, name="out_proj_bwd_w_" + tag, tiles=(2 * OUT_ROWS, D_MODEL, kt), ta=True,
                      out_dtype=BF16, into=(rest_shape, (2, OUT_ROWS, D_MODEL),
                                            functools.partial(lambda i, j, k, h: (h, out_block, 0), h=half),
                                            gb_rest))
    token = jnp.zeros((8, 128), F32) if early_grads is None else early_grads(gb_rest)
    dq, dk, dv, dcb = _attn_bwd(qkv, do, y_att, lse, negc, token, bl, t)
    dc = jnp.pad(dcb[:, :, 0:2, :].transpose(0, 3, 1, 2).reshape(n, 16), ((0, 0), (16, 96)))
    dxc, dpa, ddt_raw, d_snw, d_dsk, d_alog, d_dtb = _ssd_bwd(dys, xc, proj_a, y_pre, hprev, dt, gate_d, acum,
                                                             acum_t, alog128, dskip_e, ssd_norm_w, bl, t)
    dpa, d_conv_w, d_conv_b = _conv_bwd(dxc, dsilu, proj_a, conv_w, dpa, bl, t)
    dproj_a, d_fb = _fpost(dc, gate_d, ddt_raw, dpa, bl, t)
    dqkv = jnp.concatenate([dq, dk, dv], axis=1)
    d_w_a = _mm(h0, dproj_a, name="proj_a_bwd_w", tiles=(1024, 896, kt), ta=True, out_dtype=BF16)
    d_w_qkv = _mm(h0, dqkv, name="proj_qkv_bwd_w", tiles=(1024, 1024, kt), ta=True, out_dtype=BF16)
    d_w_in = _merge_w_in(d_w_a, d_w_qkv)
    late_token = None if late_grads is None else late_grads(d_w_in)
    dx, _, d_nmix = _mm_norm_bwd([(dproj_a, w_a), (dqkv, w_qkv)], x, rstd0, norm_mix_w, dh1,
                                 name="proj_bwd_act_norm_mix", tm=min(n, 256), after=late_token)

    grads = dict(norm_mix_w=d_nmix, w_in=d_w_in, conv_w=d_conv_w, conv_b=d_conv_b,
                 dt_bias=d_dtb, a_log=d_alog, d_skip=d_dsk, ssd_norm_w=d_snw, f_bias=d_fb, rest=gb_rest,
                 norm_mlp_w=d_nmlp, norm_final_w=d_nfw)
    return dx.reshape(bl, t, d), loss, grads


SMALL_ORDER = ("norm_mix_w", "conv_w", "conv_b", "dt_bias", "a_log", "d_skip", "ssd_norm_w", "f_bias",
               "norm_mlp_w", "norm_final_w")
SMALL_SIZES = (1024, 4 * CONV_CH, CONV_CH, 16, 16, 16, 1024, 16, 1024, 1024)


def _pack_small(vals, rows):
    flat = jnp.concatenate([v.reshape(-1).astype(F32) for v in vals])
    return jnp.pad(flat, (0, rows * 128 - flat.shape[0])).reshape(rows, 128)


def _unpack_small(packed, sizes):
    flat = packed.reshape(-1)
    out, o = [], 0
    for s in sizes:
        out.append(flat[o:o + s])
        o += s
    return out


def kernel(x, norm_mix_w, w_in, conv_w, conv_b, dt_bias, a_log, d_skip, ssd_norm_w, f_bias, w_out, norm_mlp_w, w_up, w_down, norm_final_w, loss_target, m_norm_mix_w, m_w_in, m_conv_w, m_conv_b, m_dt_bias, m_a_log, m_d_skip, m_ssd_norm_w, m_f_bias, m_w_out, m_norm_mlp_w, m_w_up, m_w_down, m_norm_final_w, v_norm_mix_w, v_w_in, v_conv_w, v_conv_b, v_dt_bias, v_a_log, v_d_skip, v_ssd_norm_w, v_f_bias, v_w_out, v_norm_mlp_w, v_w_up, v_w_down, v_norm_final_w):
    chip = 2 * lax.axis_index("x") + lax.axis_index("y")
    cw = CONV_CH // N_CHIPS

    own_in = _pack_in(w_in[0])
    own_rest = _pack_rest(w_out[0], w_up[0], w_down[0])
    g_in = _gather_weights(own_in, name="gather_w_in")
    w_in_f = _full_w_in(g_in)
    *rest_handles, rest_token = _gather_start(own_rest, g_in, name="gather_start_rest")

    def rest_weights(after):
        _, landed = _gather_wait(*rest_handles, after, name="gather_wait_rest")
        return _full_rest(_gather_forward(landed, own_rest, name="gather_forward_rest"))
    small_all = _gather_small(_pack_small([conv_w[0]], 16), name="gather_conv_w")
    conv_w_f = jnp.concatenate([small_all[2 * j].reshape(-1)[:4 * cw].reshape(4, cw) for j in range(N_CHIPS)], axis=1)

    def chip_partial(gb, tag):
        from_sibling = _swap_halves(gb, name="grad_swap_halves_" + tag)
        return _add_my_half(gb, from_sibling, name="grad_add_sibling_" + tag)

    in_flight = {}

    def early_grads(gb_rest):
        part = chip_partial(gb_rest, "rest")
        *handles, token = _exchange_start(part, name="grad_exchange_start_rest")
        in_flight["rest"] = handles
        return token

    def late_grads(d_w_in):
        gb_in = jnp.stack([_pack_in(d_w_in[:, j * IN_SHARD:(j + 1) * IN_SHARD]) for j in range(N_CHIPS)])
        *handles, token = _exchange_start(chip_partial(gb_in, "in"), name="grad_exchange_start_in")
        in_flight["in"] = handles
        return token

    dx, loss_part, g = _local_step(x, loss_target, w_in_f, rest_weights, norm_mix_w, conv_w_f,
                                   conv_b, dt_bias, a_log, d_skip, ssd_norm_w, f_bias, norm_mlp_w, norm_final_w,
                                   first_after=rest_token, early_grads=early_grads, late_grads=late_grads)

    send_sems, recv_sems, part_rest, land_rest = in_flight["rest"]
    part_rest, parts_rest = _exchange_wait(send_sems, recv_sems, part_rest, land_rest, dx,
                                           name="grad_exchange_wait_rest")
    g_rest_half = _sum_parts(parts_rest, part_rest, name="grad_sum_chips_rest")
    g_w_out, g_w_up, g_w_down = _unpack_rest(_join_halves(g_rest_half, name="grad_join_halves_rest"))

    part_in, parts_in = _exchange_wait(*in_flight["in"], dx, name="grad_exchange_wait_in")
    g_in_half = _sum_parts(parts_in, part_in, name="grad_sum_chips_in")
    g_w_in = _join_halves(g_in_half, name="grad_join_halves_in")[:, :IN_SHARD]

    small_vals = [g[k] for k in SMALL_ORDER] + [loss_part[:, 0:1]]
    small_sum = _sum_leading(_gather_small(_pack_small(small_vals, SMALL_ROWS), name="gather_small_grads"), name="small_sum")
    sg = dict(zip(SMALL_ORDER + ("loss",), _unpack_small(small_sum, SMALL_SIZES + (1,))))
    loss = sg["loss"].reshape(())
    g_conv_full = sg["conv_w"].reshape(4, CONV_CH)
    g_conv = lax.dynamic_slice_in_dim(g_conv_full, chip * cw, cw, axis=1)

    grads = dict(norm_mix_w=sg["norm_mix_w"].reshape(1, -1), w_in=g_w_in[None], conv_w=g_conv[None],
                 conv_b=sg["conv_b"].reshape(1, -1), dt_bias=sg["dt_bias"].reshape(1, -1),
                 a_log=sg["a_log"].reshape(1, -1), d_skip=sg["d_skip"].reshape(1, -1),
                 ssd_norm_w=sg["ssd_norm_w"].reshape(1, -1), f_bias=sg["f_bias"].reshape(1, -1), w_out=g_w_out[None],
                 norm_mlp_w=sg["norm_mlp_w"].reshape(1, -1), w_up=g_w_up[None], w_down=g_w_down[None],
                 norm_final_w=sg["norm_final_w"])
    weights = dict(norm_mix_w=norm_mix_w, w_in=w_in, conv_w=conv_w, conv_b=conv_b, dt_bias=dt_bias, a_log=a_log,
                   d_skip=d_skip, ssd_norm_w=ssd_norm_w, f_bias=f_bias, w_out=w_out, norm_mlp_w=norm_mlp_w,
                   w_up=w_up, w_down=w_down, norm_final_w=norm_final_w)
    ms = dict(norm_mix_w=m_norm_mix_w, w_in=m_w_in, conv_w=m_conv_w, conv_b=m_conv_b, dt_bias=m_dt_bias,
              a_log=m_a_log, d_skip=m_d_skip, ssd_norm_w=m_ssd_norm_w, f_bias=m_f_bias, w_out=m_w_out,
              norm_mlp_w=m_norm_mlp_w, w_up=m_w_up, w_down=m_w_down, norm_final_w=m_norm_final_w)
    vs = dict(norm_mix_w=v_norm_mix_w, w_in=v_w_in, conv_w=v_conv_w, conv_b=v_conv_b, dt_bias=v_dt_bias,
              a_log=v_a_log, d_skip=v_d_skip, ssd_norm_w=v_ssd_norm_w, f_bias=v_f_bias, w_out=v_w_out,
              norm_mlp_w=v_norm_mlp_w, w_up=v_w_up, w_down=v_w_down, norm_final_w=v_norm_final_w)
    names = list(weights)
    big = ("w_in", "w_out", "w_up", "w_down")
    delta, new_m, new_v = {}, {}, {}
    for k, g2 in zip(big[1:], (g_w_out, g_w_up, g_w_down)):
        delta[k], new_m[k], new_v[k] = _adamw(weights[k], g2, ms[k], vs[k], name="adamw_" + k)
    g_in_t = g_w_in.T
    outs_t = _adamw(w_in[0].T, g_in_t, m_w_in[0].T, v_w_in[0].T, name="adamw_w_in")
    delta["w_in"], new_m["w_in"], new_v["w_in"] = [o.T[None] for o in outs_t]
    grads["w_in"] = g_in_t.T[None]
    smalls = [k for k in names if k not in big]
    sizes = [math.prod(weights[k].shape) for k in smalls]
    rows = -(-sum(sizes) // 1024) * 8
    packs = [_pack_small([d[k] for k in smalls], rows) for d in (weights, grads, ms, vs)]
    outs = _adamw(*packs, name="adamw_small")
    for o, dst in zip(outs, (delta, new_m, new_v)):
        for k, val in zip(smalls, _unpack_small(o, sizes)):
            dst[k] = val.reshape(weights[k].shape)
    return (loss, dx, *[grads[k] for k in names], *[delta[k] for k in names], *[new_m[k] for k in names],
            *[new_v[k] for k in names])
```

```python
import functools
import math

import jax
import jax.numpy as jnp
from jax import lax
from jax.experimental import pallas as pl
from jax.experimental.pallas import tpu as pltpu

F32 = jnp.float32
BF16 = jnp.bfloat16
HIGHEST = lax.Precision.HIGHEST
MESH = pl.DeviceIdType.MESH

D_MODEL = 1024
HEAD_DIM = 64
SSD_WIDTH = 1024
SSD_STATE = 128
CONV_CH = 1536
CHUNK = 128
ATT_WIDTH = 1024
EPS = 1e-5
IN_WIDTH = 5664
PA_WIDTH = 2688
QKV_WIDTH = 3072
D_FF = 4096
ATT_FWD_BLOCK = 512
ATT_BWD_BLOCK = 256
NEG = -1e30
LOG2E = 1.4426950408889634
VMEM_LIMIT = 48 * 1024 * 1024

ADAM_LR = 0.001
ADAM_B1 = 0.9
ADAM_B2 = 0.999
ADAM_EPS = 1e-08
ADAM_WD = 0.01
ADAM_STEP = 10

N_CHIPS = 4
SMALL_ROWS = 96


def _cparams(sem):
    return pltpu.CompilerParams(dimension_semantics=sem, vmem_limit_bytes=VMEM_LIMIT)


def _pick(n, cands):
    for c in cands:
        if n % c == 0:
            return c
    return n


MM_CHUNK = 512


def _mm(a, b, *, name, tiles, ta=False, tb=False, out_dtype=F32, res=None, a_act=None, epi_up=None, after=None,
        into=None):
    n_unread = (after is not None) + (into is not None and into[3] is not None)
    if ta:
        K, M = a.shape
    else:
        M, K = a.shape
    if tb:
        N, K2 = b.shape
    else:
        K2, N = b.shape
    assert K == K2, (a.shape, b.shape)
    tm, tn, tk = tiles
    assert M % tm == 0 and N % tn == 0 and K % tk == 0, (name, M, N, K, tiles)
    nk = K // tk
    dn = (((0 if ta else 1,), (1 if tb else 0,)), ((), ()))
    has_res = res is not None
    has_up = epi_up is not None
    cn = _pick(tn, (MM_CHUNK, 384, 256, 128))

    def prologue(av):
        if a_act == "relu2":
            r = jnp.maximum(av.astype(F32), 0.0)
            av = r * r
        return av.astype(BF16)

    def epilogue(out, res_v, up_v):
        if has_res:
            out = out + res_v.astype(F32)
        if has_up:
            out = out * (2.0 * jnp.maximum(up_v.astype(F32), 0.0))
        return out.astype(out_dtype)

    def body(*refs):
        a_ref, b_ref = refs[0], refs[1]
        i = 2
        res_ref = up_ref = None
        if has_res:
            res_ref = refs[i]
            i += 1
        if has_up:
            up_ref = refs[i]
            i += 1
        i += n_unread
        o_ref = refs[i]
        if nk == 1:
            av = prologue(a_ref[...])
            if len(o_ref.shape) == 3:
                out = lax.dot_general(av, b_ref[...].astype(BF16), dn, preferred_element_type=F32)
                out = epilogue(out, res_ref[...] if has_res else None, up_ref[...] if has_up else None)
                o_ref[...] = out.reshape(o_ref.shape)
                return
            for c in range(tn // cn):
                cs = slice(c * cn, (c + 1) * cn)
                bv = (b_ref[cs, :] if tb else b_ref[:, cs]).astype(BF16)
                out = lax.dot_general(av, bv, dn, preferred_element_type=F32)
                o_ref[:, cs] = epilogue(out, res_ref[:, cs] if has_res else None, up_ref[:, cs] if has_up else None)
            return
        acc_ref = refs[i + 1]
        k = pl.program_id(2)

        @pl.when(k == 0)
        def _():
            acc_ref[...] = jnp.zeros_like(acc_ref)

        acc_ref[...] += lax.dot_general(prologue(a_ref[...]), b_ref[...].astype(BF16), dn,
                                        preferred_element_type=F32)

        @pl.when(k == nk - 1)
        def _():
            out = epilogue(acc_ref[...], res_ref[...] if has_res else None, up_ref[...] if has_up else None)
            o_ref[...] = out.reshape(o_ref.shape)

    a_spec = pl.BlockSpec((tk, tm), lambda i, j, k: (k, i)) if ta else pl.BlockSpec((tm, tk), lambda i, j, k: (i, k))
    b_spec = pl.BlockSpec((tn, tk), lambda i, j, k: (j, k)) if tb else pl.BlockSpec((tk, tn), lambda i, j, k: (k, j))
    o_spec = pl.BlockSpec((tm, tn), lambda i, j, k: (i, j))
    ins, specs = [a, b], [a_spec, b_spec]
    if has_res:
        ins.append(res)
        specs.append(o_spec)
    if has_up:
        ins.append(epi_up)
        specs.append(o_spec)
    if after is not None:
        ins.append(after)
        specs.append(pl.BlockSpec(memory_space=pl.ANY))
    out_shape, out_spec, aliases = jax.ShapeDtypeStruct((M, N), out_dtype), o_spec, {}
    if into is not None:
        shape, block, index, buf = into
        out_shape, out_spec = jax.ShapeDtypeStruct(shape, out_dtype), pl.BlockSpec(block, index)
        if buf is not None:
            aliases = {len(ins): 0}
            ins.append(buf)
            specs.append(pl.BlockSpec(memory_space=pl.ANY))
    return pl.pallas_call(
        body, name=name, grid=(M // tm, N // tn, nk),
        in_specs=specs, out_specs=out_spec, out_shape=out_shape, input_output_aliases=aliases,
        scratch_shapes=[] if nk == 1 else [pltpu.VMEM((tm, tn), F32)],
        compiler_params=_cparams(("parallel", "parallel", "arbitrary")),
    )(*ins)


def _rows_product(a_ref, b_ref, tb, a_act):
    av = a_ref[...]
    if a_act == "relu2":
        r = jnp.maximum(av.astype(F32), 0.0)
        av = r * r
    dn = (((1,), (1 if tb else 0,)), ((), ()))
    return lax.dot_general(av.astype(BF16), b_ref[...].astype(BF16), dn, preferred_element_type=F32)


def _norm_mm(x, w, b, after, *, name, tm):
    m, d = x.shape
    n = b.shape[1]
    cn = _pick(n, (MM_CHUNK, 384, 256, 128))

    def body(x_ref, w_ref, b_ref, after_ref, h_ref, r_ref, o_ref):
        xv = x_ref[...]
        rstd = lax.rsqrt(jnp.mean(xv * xv, axis=1, keepdims=True) + EPS)
        hv = (xv * rstd * w_ref[...]).astype(BF16)
        h_ref[...] = hv
        r_ref[...] = rstd
        for c in range(n // cn):
            cs = slice(c * cn, (c + 1) * cn)
            o_ref[:, cs] = jnp.dot(hv, b_ref[:, cs].astype(BF16), preferred_element_type=F32)

    row = pl.BlockSpec((tm, d), lambda i: (i, 0))
    return pl.pallas_call(
        body, name=name, grid=(m // tm,),
        in_specs=[row, pl.BlockSpec((1, d), lambda i: (0, 0)), pl.BlockSpec((d, n), lambda i: (0, 0)),
                  pl.BlockSpec(memory_space=pl.ANY)],
        out_specs=[row, pl.BlockSpec((tm, 1), lambda i: (i, 0)), pl.BlockSpec((tm, n), lambda i: (i, 0))],
        out_shape=[jax.ShapeDtypeStruct((m, d), BF16), jax.ShapeDtypeStruct((m, 1), F32),
                   jax.ShapeDtypeStruct((m, n), F32)],
        compiler_params=_cparams(("parallel",)),
    )(x, w, b, after)


def _mm_norm_fwd(a1, b1, a2, b2, res, w, *, name, tm):
    m, k1 = a1.shape
    k2 = a2.shape[1]
    d = b1.shape[1]

    def body(a1_ref, b1_ref, a2_ref, b2_ref, res_ref, w_ref, h_ref, y_ref, r_ref):
        hv = _rows_product(a1_ref, b1_ref, False, None) + _rows_product(a2_ref, b2_ref, False, None) + res_ref[...]
        rstd = lax.rsqrt(jnp.mean(hv * hv, axis=1, keepdims=True) + EPS)
        h_ref[...] = hv
        y_ref[...] = (hv * rstd * w_ref[...]).astype(BF16)
        r_ref[...] = rstd

    row = pl.BlockSpec((tm, d), lambda i: (i, 0))
    return pl.pallas_call(
        body, name=name, grid=(m // tm,),
        in_specs=[pl.BlockSpec((tm, k1), lambda i: (i, 0)), pl.BlockSpec((k1, d), lambda i: (0, 0)),
                  pl.BlockSpec((tm, k2), lambda i: (i, 0)), pl.BlockSpec((k2, d), lambda i: (0, 0)), row,
                  pl.BlockSpec((1, d), lambda i: (0, 0))],
        out_specs=[row, row, pl.BlockSpec((tm, 1), lambda i: (i, 0))],
        out_shape=[jax.ShapeDtypeStruct((m, d), F32), jax.ShapeDtypeStruct((m, d), BF16),
                   jax.ShapeDtypeStruct((m, 1), F32)],
        compiler_params=_cparams(("parallel",)),
    )(a1, b1, a2, b2, res, w)


def _mm_final(a, b, res, w, target, *, name, tm, a_act):
    m, k = a.shape
    d = b.shape[1]

    def body(a_ref, b_ref, res_ref, w_ref, t_ref, dh_ref, dhb_ref, loss_ref, dw_ref):
        @pl.when(pl.program_id(0) == 0)
        def _():
            loss_ref[...] = jnp.zeros_like(loss_ref)
            dw_ref[...] = jnp.zeros_like(dw_ref)

        hv = _rows_product(a_ref, b_ref, False, a_act) + res_ref[...]
        wv = w_ref[...]
        rstd = lax.rsqrt(jnp.mean(hv * hv, axis=1, keepdims=True) + EPS)
        xhat = hv * rstd
        err = xhat * wv - t_ref[...]
        loss_ref[...] += 0.5 * jnp.sum(jnp.mean(err * err, axis=1, keepdims=True), axis=0, keepdims=True)
        dy = err * (1.0 / d)
        gw = dy * wv
        dh = rstd * (gw - xhat * jnp.mean(gw * xhat, axis=1, keepdims=True))
        dh_ref[...] = dh
        dhb_ref[...] = dh.astype(BF16)
        dw_ref[...] += jnp.sum(dy * xhat, axis=0, keepdims=True)

    row = pl.BlockSpec((tm, d), lambda i: (i, 0))
    vec = pl.BlockSpec((1, d), lambda i: (0, 0))
    return pl.pallas_call(
        body, name=name, grid=(m // tm,),
        in_specs=[pl.BlockSpec((tm, k), lambda i: (i, 0)), pl.BlockSpec((k, d), lambda i: (0, 0)), row, vec, row],
        out_specs=[row, row, pl.BlockSpec((1, 128), lambda i: (0, 0)), vec],
        out_shape=[jax.ShapeDtypeStruct((m, d), F32), jax.ShapeDtypeStruct((m, d), BF16),
                   jax.ShapeDtypeStruct((1, 128), F32), jax.ShapeDtypeStruct((1, d), F32)],
        compiler_params=_cparams(("arbitrary",)),
    )(a, b, res, w, target)


def _mm_two_halves(a, b, *, name, tm):
    m, k = a.shape
    d = b.shape[0] // 2

    def body(a_ref, b_ref, lo_ref, hi_ref):
        av = a_ref[...].astype(BF16)
        lo_ref[...] = lax.dot_general(av, b_ref[0:d, :].astype(BF16), NT_DIMS, preferred_element_type=F32)
        hi_ref[...] = lax.dot_general(av, b_ref[d:2 * d, :].astype(BF16), NT_DIMS,
                                      preferred_element_type=F32).astype(BF16)

    row = pl.BlockSpec((tm, d), lambda i: (i, 0))
    return pl.pallas_call(
        body, name=name, grid=(m // tm,),
        in_specs=[pl.BlockSpec((tm, k), lambda i: (i, 0)), pl.BlockSpec((2 * d, k), lambda i: (0, 0))],
        out_specs=[row, row],
        out_shape=[jax.ShapeDtypeStruct((m, d), F32), jax.ShapeDtypeStruct((m, d), BF16)],
        compiler_params=_cparams(("parallel",)),
    )(a, b)


def _mm_norm_bwd(pairs, x, rstd, w, dres, *, name, tm, after=None):
    m = pairs[0][0].shape[0]
    d = pairs[0][1].shape[0]
    n_pairs = len(pairs)

    def body(*refs):
        i = 2 * n_pairs
        x_ref, r_ref, w_ref, d_ref = refs[i:i + 4]
        dx_ref, dxb_ref, dw_ref = refs[-3:]

        @pl.when(pl.program_id(0) == 0)
        def _():
            dw_ref[...] = jnp.zeros_like(dw_ref)

        g = _rows_product(refs[0], refs[1], True, None)
        for p in range(1, n_pairs):
            g = g + _rows_product(refs[2 * p], refs[2 * p + 1], True, None)
        r = r_ref[...]
        xhat = x_ref[...] * r
        gw = g * w_ref[...]
        dx = d_ref[...] + r * (gw - xhat * jnp.mean(gw * xhat, axis=1, keepdims=True))
        dx_ref[...] = dx
        dxb_ref[...] = dx.astype(BF16)
        dw_ref[...] += jnp.sum(g * xhat, axis=0, keepdims=True)

    row = pl.BlockSpec((tm, d), lambda i: (i, 0))
    vec = pl.BlockSpec((1, d), lambda i: (0, 0))
    ins, specs = [], []
    for a, b in pairs:
        k = a.shape[1]
        ins += [a, b]
        specs += [pl.BlockSpec((tm, k), lambda i: (i, 0)), pl.BlockSpec((d, k), lambda i: (0, 0))]
    ins += [x, rstd, w, dres]
    specs += [row, pl.BlockSpec((tm, 1), lambda i: (i, 0)), vec, row]
    if after is not None:
        ins.append(after)
        specs.append(pl.BlockSpec(memory_space=pl.ANY))
    return pl.pallas_call(
        body, name=name, grid=(m // tm,), in_specs=specs, out_specs=[row, row, vec],
        out_shape=[jax.ShapeDtypeStruct((m, d), F32), jax.ShapeDtypeStruct((m, d), BF16),
                   jax.ShapeDtypeStruct((1, d), F32)],
        compiler_params=_cparams(("arbitrary",)),
    )(*ins)


def _softplus(x):
    return jnp.maximum(x, 0.0) + jnp.log(1.0 + jnp.exp(-jnp.abs(x)))


def _prep(proj_a, bias128, alog128, bl, t):
    n = bl * t
    nch = t // CHUNK
    col0 = (SSD_WIDTH + CONV_CH) // 128

    def body(p_ref, b_ref, al_ref, dt_ref, gd_ref, ac_ref, act_ref, negc_ref):
        negc_ref[...] = jnp.zeros_like(negc_ref)
        row = lax.broadcasted_iota(jnp.int32, (CHUNK, CHUNK), 0)
        col = lax.broadcasted_iota(jnp.int32, (CHUNK, CHUNK), 1)
        tril = (row >= col).astype(F32)
        lane = lax.broadcasted_iota(jnp.int32, (1, 128), 1)
        head_lanes = lane < 16
        a_row = -jnp.exp(al_ref[...])
        carry = jnp.zeros((1, 128), F32)
        for ci in range(nch):
            rows = slice(ci * CHUNK, (ci + 1) * CHUNK)
            xv = p_ref[rows, :] + b_ref[...]
            sp = _softplus(xv)
            acum = jnp.dot(tril, a_row * sp, precision=HIGHEST, preferred_element_type=F32)
            c = jnp.dot(tril, -_softplus(-xv), precision=HIGHEST, preferred_element_type=F32) + carry
            carry = c[CHUNK - 1:CHUNK, :]
            dt_ref[rows, :] = jnp.where(head_lanes, sp, 0.0)
            gd_ref[rows, :] = jnp.where(head_lanes, jax.nn.sigmoid(xv),
                                        jnp.where(lane < 32, jax.nn.sigmoid(-xv), 0.0))
            ac_ref[rows, :] = jnp.where(head_lanes, acum, 0.0)
            act_ref[:, rows] = jnp.transpose(acum)[0:16, :]
            c_t = jnp.transpose(c)
            for hp in range(8):
                negc_ref[hp, 0:2, rows] = -c_t[16 + 2 * hp:18 + 2 * hp, :]

    o128 = pl.BlockSpec((t, 128), lambda b: (b, 0))
    v128 = pl.BlockSpec((1, 128), lambda b: (0, 0))
    w128 = jax.ShapeDtypeStruct((n, 128), F32)
    return pl.pallas_call(
        body, name="head_scalars", grid=(bl,),
        in_specs=[pl.BlockSpec((t, 128), lambda b: (b, col0)), v128, v128],
        out_specs=[o128, o128, o128, pl.BlockSpec((16, t), lambda b: (0, b)),
                   pl.BlockSpec((None, 8, 8, t), lambda b: (b, 0, 0, 0))],
        out_shape=[w128, w128, w128, jax.ShapeDtypeStruct((16, n), F32),
                   jax.ShapeDtypeStruct((bl, 8, 8, t), F32)],
        compiler_params=_cparams(("parallel",)),
    )(proj_a, bias128, alog128)


def _fpost(dc, gate_d, ddt, dpa, bl, t):
    n = bl * t
    nch = t // CHUNK
    col0 = (SSD_WIDTH + CONV_CH) // 128

    def body(dc_ref, gd_ref, ddt_ref, dpa_in, out_ref, db_ref):
        @pl.when(pl.program_id(0) == 0)
        def _():
            db_ref[...] = jnp.zeros_like(db_ref)

        row = lax.broadcasted_iota(jnp.int32, (CHUNK, CHUNK), 0)
        col = lax.broadcasted_iota(jnp.int32, (CHUNK, CHUNK), 1)
        triu = (row <= col).astype(F32)
        lane = lax.broadcasted_iota(jnp.int32, (1, 128), 1)
        gate_lanes = (lane >= 16) & (lane < 32)
        carry = jnp.zeros((1, 128), F32)
        db = jnp.zeros((1, 128), F32)
        for ci in reversed(range(nch)):
            rows = slice(ci * CHUNK, (ci + 1) * CHUNK)
            dlf = jnp.dot(triu, dc_ref[rows, :], precision=HIGHEST, preferred_element_type=F32) + carry
            carry = dlf[0:1, :]
            df = jnp.where(gate_lanes, dlf * gd_ref[rows, :], 0.0)
            out_ref[rows, :] = (ddt_ref[rows, :] + df).astype(BF16)
            db = db + jnp.sum(df, axis=0, keepdims=True)
        db_ref[...] += db[:, 16:32]

    blk = pl.BlockSpec((t, 128), lambda b: (b, 0))
    return pl.pallas_call(
        body, name="forget_gate_bwd", grid=(bl,),
        in_specs=[blk, blk, blk, ANY],
        out_specs=[pl.BlockSpec((t, 128), lambda b: (b, col0)), pl.BlockSpec((1, 16), lambda b: (0, 0))],
        out_shape=[jax.ShapeDtypeStruct(dpa.shape, dpa.dtype), jax.ShapeDtypeStruct((1, 16), F32)],
        input_output_aliases={3: 0},
        compiler_params=_cparams(("arbitrary",)),
    )(dc, gate_d, ddt, dpa)


CONV_TILE = 256
CONV_ROWS = 256


def _conv_taps(u_ref, i):
    r0 = pl.multiple_of(i * CONV_ROWS, CONV_ROWS)
    cur = u_ref[pl.ds(r0, CONV_ROWS), :]
    p0 = pl.multiple_of(jnp.maximum(r0 - 8, 0), 8)
    prev = jnp.where(i > 0, u_ref[pl.ds(p0, 8), :], 0.0)
    cat = jnp.concatenate([prev, cur], axis=0)
    return r0, [cur] + [pltpu.roll(cat, s, 0)[8:, :] for s in (1, 2, 3)]


def _conv_fwd(proj_a, conv_w, conv_b, bl, t):
    n = bl * t
    nct = CONV_CH // CONV_TILE
    c0 = SSD_WIDTH // CONV_TILE

    def body(u_ref, w_ref, b_ref, o_ref, d_ref):
        w = w_ref[...]
        bias = b_ref[...]

        def chunk(i, carry):
            r0, taps = _conv_taps(u_ref, i)
            pre = bias + w[3:4, :] * taps[0]
            for s in (1, 2, 3):
                pre = pre + w[3 - s:4 - s, :] * taps[s]
            sg = jax.nn.sigmoid(pre)
            o_ref[pl.ds(r0, CONV_ROWS), :] = pre * sg
            d_ref[pl.ds(r0, CONV_ROWS), :] = (sg * (1.0 + pre * (1.0 - sg))).astype(BF16)
            return carry

        lax.fori_loop(0, t // CONV_ROWS, chunk, 0)

    out = pl.BlockSpec((t, CONV_TILE), lambda b, c: (b, c))
    return pl.pallas_call(
        body, name="conv_silu_fwd", grid=(bl, nct),
        in_specs=[pl.BlockSpec((t, CONV_TILE), lambda b, c: (b, c0 + c)),
                  pl.BlockSpec((4, CONV_TILE), lambda b, c: (0, c)),
                  pl.BlockSpec((1, CONV_TILE), lambda b, c: (0, c))],
        out_specs=[out, out],
        out_shape=[jax.ShapeDtypeStruct((n, CONV_CH), F32), jax.ShapeDtypeStruct((n, CONV_CH), BF16)],
        compiler_params=_cparams(("parallel", "parallel")),
    )(proj_a, conv_w, conv_b)


def _conv_bwd(dxc, dsilu, proj_a, conv_w, dpa, bl, t):
    nct = CONV_CH // CONV_TILE
    c0 = SSD_WIDTH // CONV_TILE
    nrc = t // CONV_ROWS

    def body(g_ref, s_ref, u_ref, w_ref, dpa_in, du_ref, dw_ref, db_ref, dp_scr):
        @pl.when(pl.program_id(1) == 0)
        def _():
            dw_ref[...] = jnp.zeros_like(dw_ref)
            db_ref[...] = jnp.zeros_like(db_ref)

        w = w_ref[...]
        dp_scr[pl.ds(t, 8), :] = jnp.zeros((8, CONV_TILE), F32)

        def chunk1(i, carry):
            dw0, dw1, dw2, dw3, db = carry
            r0, taps = _conv_taps(u_ref, i)
            dpre = g_ref[pl.ds(r0, CONV_ROWS), :] * s_ref[pl.ds(r0, CONV_ROWS), :].astype(F32)
            dp_scr[pl.ds(r0, CONV_ROWS), :] = dpre
            dw3 = dw3 + jnp.sum(dpre * taps[0], axis=0, keepdims=True)
            dw2 = dw2 + jnp.sum(dpre * taps[1], axis=0, keepdims=True)
            dw1 = dw1 + jnp.sum(dpre * taps[2], axis=0, keepdims=True)
            dw0 = dw0 + jnp.sum(dpre * taps[3], axis=0, keepdims=True)
            db = db + jnp.sum(dpre, axis=0, keepdims=True)
            return dw0, dw1, dw2, dw3, db

        z = jnp.zeros((1, CONV_TILE), F32)
        dw0, dw1, dw2, dw3, db = lax.fori_loop(0, nrc, chunk1, (z, z, z, z, z))
        dw_ref[...] += jnp.concatenate([dw0, dw1, dw2, dw3], axis=0)
        db_ref[...] += db

        def chunk2(i, carry):
            r0 = pl.multiple_of(i * CONV_ROWS, CONV_ROWS)
            cat = dp_scr[pl.ds(r0, CONV_ROWS + 8), :]
            du = w[3:4, :] * cat[:CONV_ROWS, :]
            for s in (1, 2, 3):
                du = du + w[3 - s:4 - s, :] * pltpu.roll(cat, CONV_ROWS + 8 - s, 0)[:CONV_ROWS, :]
            du_ref[pl.ds(r0, CONV_ROWS), :] = du.astype(BF16)
            return carry

        lax.fori_loop(0, nrc, chunk2, 0)

    tile = pl.BlockSpec((t, CONV_TILE), lambda c, b: (b, c))
    return pl.pallas_call(
        body, name="conv_silu_bwd", grid=(nct, bl),
        in_specs=[tile, tile, pl.BlockSpec((t, CONV_TILE), lambda c, b: (b, c0 + c)),
                  pl.BlockSpec((4, CONV_TILE), lambda c, b: (0, c)), ANY],
        out_specs=[pl.BlockSpec((t, CONV_TILE), lambda c, b: (b, c0 + c)),
                   pl.BlockSpec((4, CONV_TILE), lambda c, b: (0, c)),
                   pl.BlockSpec((1, CONV_TILE), lambda c, b: (0, c))],
        out_shape=[jax.ShapeDtypeStruct(dpa.shape, dpa.dtype), jax.ShapeDtypeStruct((4, CONV_CH), F32),
                   jax.ShapeDtypeStruct((1, CONV_CH), F32)],
        input_output_aliases={4: 0},
        scratch_shapes=[pltpu.VMEM((t + 8, CONV_TILE), F32)],
        compiler_params=_cparams(("parallel", "arbitrary")),
    )(dxc, dsilu, proj_a, conv_w, dpa)


SSD_FWD_CHUNKS = 4
SSD_BWD_CHUNKS = 1
NT_DIMS = (((1,), (1,)), ((), ()))
TN_DIMS = (((0,), (0,)), ((), ()))


def _dot(a, b, dims=None):
    if dims is None:
        return jnp.dot(a, b, preferred_element_type=F32)
    return lax.dot_general(a, b, dims, preferred_element_type=F32)


def _head_expander():
    r = lax.broadcasted_iota(jnp.int32, (128, SSD_WIDTH), 0)
    c = lax.broadcasted_iota(jnp.int32, (128, SSD_WIDTH), 1)
    return ((c // HEAD_DIM == r % 16) & (r < 48)).astype(BF16)


def _spread(v128, expander):
    hi = v128.astype(BF16).astype(F32)
    r1 = v128 - hi
    mid = r1.astype(BF16).astype(F32)
    lo = (r1 - mid).astype(BF16).astype(F32)
    packed = (hi + pltpu.roll(mid, 16, 1) + pltpu.roll(lo, 32, 1)).astype(BF16)
    return jnp.dot(packed, expander, preferred_element_type=F32)


def _head_sums(v1024, expander):
    hi = v1024.astype(BF16)
    lo = (v1024 - hi.astype(F32)).astype(BF16)
    heads = jnp.where(lax.broadcasted_iota(jnp.int32, expander.shape, 0) < 16, expander, jnp.zeros_like(expander))
    return _dot(hi, heads, NT_DIMS) + _dot(lo, heads, NT_DIMS)


def _ssd_fwd(xc, proj_a, dt, acum, acum_t, dskip_e, norm_w, bl, t):
    n = bl * t
    nch = t // CHUNK
    L = CHUNK

    def body(xc_blk, z_blk, dt_blk, ac_blk, act_blk, dsk_ref, nw_ref, ys_blk, yp_blk, hp_blk, h_scr, y_scr, x_scr):
        @pl.when(pl.program_id(1) == 0)
        def _():
            h_scr[...] = jnp.zeros_like(h_scr)

        for sub in range(SSD_FWD_CHUNKS):
            rows = pl.ds(sub * L, L)
            chunk(xc_blk.at[rows, :], z_blk.at[rows, :], dt_blk.at[rows, :], ac_blk.at[rows, :], act_blk.at[:, rows],
                  dsk_ref, nw_ref, ys_blk.at[rows, :], yp_blk.at[rows, :], hp_blk.at[sub], h_scr, y_scr, x_scr)

    def chunk(xc_ref, z_ref, dt_ref, ac_ref, act_ref, dsk_ref, nw_ref, ys_ref, yp_ref, hp_ref, h_scr, y_scr, x_scr):
        row = lax.broadcasted_iota(jnp.int32, (L, L), 0)
        col = lax.broadcasted_iota(jnp.int32, (L, L), 1)
        causal = row >= col
        lane128 = lax.broadcasted_iota(jnp.int32, (1, L), 1)
        expander = _head_expander()
        ac_all = ac_ref[...]
        act_all = act_ref[...]
        ac_e = _spread(ac_all, expander)
        e_in = jnp.exp(ac_e)
        dec = jnp.exp(ac_e[L - 1:L, :] - ac_e)
        xs_all = xc_ref[:, 0:SSD_WIDTH]
        x_all = xs_all * _spread(dt_ref[...], expander)
        x_scr[...] = x_all.astype(BF16)
        hp_all = h_scr[...]
        hp_ref[...] = hp_all
        for g in range(2):
            gs = slice(g * 512, (g + 1) * 512)
            bg = xc_ref[:, SSD_WIDTH + g * 128:SSD_WIDTH + (g + 1) * 128].astype(BF16)
            cg = xc_ref[:, SSD_WIDTH + 256 + g * 128:SSD_WIDTH + 256 + (g + 1) * 128].astype(BF16)
            gmat = _dot(cg, bg, NT_DIMS)
            y_off = _dot(cg, hp_all[gs, :].astype(BF16), NT_DIMS) * e_in[:, gs] + dsk_ref[:, gs] * xs_all[:, gs]
            s_new = _dot((x_all[:, gs] * dec[:, gs]).astype(BF16), bg, TN_DIMS)
            for pr in range(4):
                pair = slice((g * 4 + pr) * 128, (g * 4 + pr + 1) * 128)
                x_pair = x_scr[:, pair]
                y_pair = y_off[:, pr * 128:(pr + 1) * 128]
                for j in range(2):
                    h = g * 8 + 2 * pr + j
                    sl = slice(h * HEAD_DIM, (h + 1) * HEAD_DIM)
                    r = 2 * pr + j
                    ldec = jnp.exp(jnp.where(causal, ac_all[:, h:h + 1] - act_all[h:h + 1, :], NEG))
                    x_head = jnp.where((lane128 < HEAD_DIM) == (j == 0), x_pair, jnp.zeros_like(x_pair))
                    y_pair = y_pair + _dot((gmat * ldec).astype(BF16), x_head)
                    elast = jnp.exp(ac_all[L - 1:L, h:h + 1])
                    h_scr[sl, :] = elast * hp_all[sl, :] + s_new[r * HEAD_DIM:(r + 1) * HEAD_DIM, :]
                y_scr[:, pair] = y_pair
        y = y_scr[...]
        yp_ref[...] = y
        zv = z_ref[...]
        yg = y * (zv * jax.nn.sigmoid(zv))
        for g in range(2):
            gs = slice(g * 512, (g + 1) * 512)
            grp = yg[:, gs]
            rstd = lax.rsqrt(jnp.mean(grp * grp, axis=1, keepdims=True) + EPS)
            ys_ref[:, gs] = (grp * rstd * nw_ref[:, gs]).astype(BF16)

    cps = SSD_FWD_CHUNKS
    steps = nch // cps
    rb = lambda b, c: (b * steps + c, 0)
    v1k = pl.BlockSpec((1, SSD_WIDTH), lambda b, c: (0, 0))
    return pl.pallas_call(
        body, name="ssd_fwd", grid=(bl, steps),
        in_specs=[pl.BlockSpec((cps * L, CONV_CH), rb), pl.BlockSpec((cps * L, SSD_WIDTH), rb),
                  pl.BlockSpec((cps * L, 128), rb), pl.BlockSpec((cps * L, 128), rb),
                  pl.BlockSpec((16, cps * L), lambda b, c: (0, b * steps + c)), v1k, v1k],
        out_specs=[pl.BlockSpec((cps * L, SSD_WIDTH), rb), pl.BlockSpec((cps * L, SSD_WIDTH), rb),
                   pl.BlockSpec((cps, SSD_WIDTH, SSD_STATE), lambda b, c: (b * steps + c, 0, 0))],
        out_shape=[jax.ShapeDtypeStruct((n, SSD_WIDTH), BF16), jax.ShapeDtypeStruct((n, SSD_WIDTH), F32),
                   jax.ShapeDtypeStruct((bl * nch, SSD_WIDTH, SSD_STATE), F32)],
        scratch_shapes=[pltpu.VMEM((SSD_WIDTH, SSD_STATE), F32), pltpu.VMEM((L, SSD_WIDTH), F32),
                        pltpu.VMEM((L, SSD_WIDTH), BF16)],
        compiler_params=_cparams(("parallel", "arbitrary")),
    )(xc, proj_a, dt, acum, acum_t, dskip_e, norm_w)


def _ssd_bwd(dys, xc, proj_a, ypre, hprev, dt, gate_d, acum, acum_t, alog128, dskip_e, norm_w, bl, t):
    n = bl * t
    nch = t // CHUNK
    L = CHUNK

    def body(dys_blk, xc_blk, z_blk, yp_blk, hp_blk, dt_blk, gd_blk, ac_blk, act_blk, al_ref, dsk_ref, nw_ref,
             dxc_blk, dz_blk, ddt_blk, dnw_ref, dsk16_ref, da16_ref, db16_ref,
             dh_scr, dy_scr, x_scr, dx_scr, red_scr):
        first = (pl.program_id(0) == 0) & (pl.program_id(1) == 0)

        @pl.when(first)
        def _():
            dnw_ref[...] = jnp.zeros_like(dnw_ref)
            dsk16_ref[...] = jnp.zeros_like(dsk16_ref)
            da16_ref[...] = jnp.zeros_like(da16_ref)
            db16_ref[...] = jnp.zeros_like(db16_ref)

        @pl.when(pl.program_id(1) == 0)
        def _():
            dh_scr[...] = jnp.zeros_like(dh_scr)

        for sub in reversed(range(SSD_BWD_CHUNKS)):
            rows = pl.ds(sub * L, L)
            chunk(dys_blk.at[rows, :], xc_blk.at[rows, :], z_blk.at[rows, :], yp_blk.at[rows, :], hp_blk.at[sub],
                  dt_blk.at[rows, :], gd_blk.at[rows, :], ac_blk.at[rows, :], act_blk.at[:, rows], al_ref, dsk_ref,
                  nw_ref, dxc_blk.at[rows, :], dz_blk.at[rows, :], ddt_blk.at[rows, :], dnw_ref, dsk16_ref, da16_ref,
                  db16_ref, dh_scr, dy_scr, x_scr, dx_scr, red_scr)

    def chunk(dys_ref, xc_ref, z_ref, yp_ref, hp_ref, dt_ref, gd_ref, ac_ref, act_ref, al_ref, dsk_ref, nw_ref,
              dxc_ref, dz_ref, ddt_ref, dnw_ref, dsk16_ref, da16_ref, db16_ref,
              dh_scr, dy_scr, x_scr, dx_scr, red_scr):
        y = yp_ref[...]
        zv = z_ref[...]
        sz = jax.nn.sigmoid(zv)
        gate = zv * sz
        yg = y * gate
        dout = dys_ref[...]
        nw = nw_ref[...]
        for g in range(2):
            gs = slice(g * 512, (g + 1) * 512)
            grp = yg[:, gs]
            rstd = lax.rsqrt(jnp.mean(grp * grp, axis=1, keepdims=True) + EPS)
            ghat = grp * rstd
            dnw_ref[:, gs] += jnp.sum(dout[:, gs] * ghat, axis=0, keepdims=True)
            gw = dout[:, gs] * nw[:, gs]
            dyg = rstd * (gw - ghat * jnp.mean(gw * ghat, axis=1, keepdims=True))
            dy_scr[:, gs] = dyg * gate[:, gs]
            dz_ref[:, gs] = (dyg * y[:, gs] * (sz[:, gs] * (1.0 + zv[:, gs] * (1.0 - sz[:, gs])))).astype(BF16)

        row = lax.broadcasted_iota(jnp.int32, (L, L), 0)
        col = lax.broadcasted_iota(jnp.int32, (L, L), 1)
        causal = row >= col
        lane128 = lax.broadcasted_iota(jnp.int32, (1, L), 1)
        rows128 = lax.broadcasted_iota(jnp.int32, (L, 1), 0)
        last_row = rows128 == (L - 1)
        expander = _head_expander()
        ac_all = ac_ref[...]
        act_all = act_ref[...]
        dt_all = dt_ref[...]
        dt_e = _spread(dt_all, expander)
        ac_e = _spread(ac_all, expander)
        e_in = jnp.exp(ac_e)
        dec = jnp.exp(ac_e[L - 1:L, :] - ac_e)
        xs_all = xc_ref[:, 0:SSD_WIDTH]
        x_all = xs_all * dt_e
        x_scr[...] = x_all.astype(BF16)
        dy_all = dy_scr[...]
        hp_all = hp_ref[...]
        ds_all = dh_scr[...]
        dsk_cols = jnp.sum(dy_all * xs_all, axis=0, keepdims=True)
        dac = jnp.zeros((L, L), F32)
        dac_row = jnp.zeros((L, L), F32)
        ddec_cols = []
        for g in range(2):
            gs = slice(g * 512, (g + 1) * 512)
            bsl = slice(SSD_WIDTH + g * 128, SSD_WIDTH + (g + 1) * 128)
            csl = slice(SSD_WIDTH + 256 + g * 128, SSD_WIDTH + 256 + (g + 1) * 128)
            bg = xc_ref[:, bsl].astype(BF16)
            cg = xc_ref[:, csl].astype(BF16)
            gmat = _dot(cg, bg, NT_DIMS)
            hpb = hp_all[gs, :].astype(BF16)
            dsb = ds_all[gs, :].astype(BF16)
            ch = _dot(cg, hpb, NT_DIMS)
            dye = dy_all[:, gs] * e_in[:, gs]
            dyeb = dye.astype(BF16)
            dc_acc = _dot(dyeb, hpb)
            dhp = _dot(dyeb, cg, TN_DIMS)
            dxd = _dot(bg, dsb, NT_DIMS)
            db_acc = _dot((x_all[:, gs] * dec[:, gs]).astype(BF16), dsb)
            ddec = dxd * x_all[:, gs] * dec[:, gs]
            ddec_cols.append(jnp.sum(ddec, axis=0, keepdims=True))
            dx_inter = dxd * dec[:, gs]
            red_scr[:, gs] = dye * ch - ddec
            dg_sum = jnp.zeros((L, L), F32)
            for pr in range(4):
                pair = slice((g * 4 + pr) * 128, (g * 4 + pr + 1) * 128)
                x_pair = x_scr[:, pair]
                dy_pair = dy_scr[:, pair].astype(BF16)
                dx_pair = dx_inter[:, pr * 128:(pr + 1) * 128]
                for j in range(2):
                    h = g * 8 + 2 * pr + j
                    r = 2 * pr + j
                    sl = slice(h * HEAD_DIM, (h + 1) * HEAD_DIM)
                    onehot_w = lane128 == h
                    ldec = jnp.exp(jnp.where(causal, ac_all[:, h:h + 1] - act_all[h:h + 1, :], NEG))
                    mf = gmat * ldec
                    dyb = jnp.where((lane128 < HEAD_DIM) == (j == 0), dy_pair, jnp.zeros_like(dy_pair))
                    dm = _dot(dyb, x_pair, NT_DIMS)
                    dx_pair = dx_pair + _dot(mf.astype(BF16), dyb, TN_DIMS)
                    dg_sum = dg_sum + dm * ldec
                    wmat = dm * mf
                    elast = jnp.exp(ac_all[L - 1:L, h:h + 1])
                    hp_h = hp_all[sl, :]
                    ds_h = ds_all[sl, :]
                    extra = elast * jnp.sum(jnp.sum(hp_h * ds_h, axis=1, keepdims=True), axis=0, keepdims=True)
                    dac = dac + jnp.where(onehot_w,
                                          jnp.sum(wmat, axis=1, keepdims=True) + jnp.where(last_row, extra, 0.0), 0.0)
                    dac_row = dac_row + jnp.where(rows128 == h, -jnp.sum(wmat, axis=0, keepdims=True), 0.0)
                    dh_scr[sl, :] = elast * ds_h + dhp[r * HEAD_DIM:(r + 1) * HEAD_DIM, :]
                dx_scr[:, pair] = dx_pair
            dgb = dg_sum.astype(BF16)
            dxc_ref[:, csl] = dc_acc + _dot(dgb, bg)
            dxc_ref[:, bsl] = db_acc + _dot(dgb, cg, TN_DIMS)
        dx_all = dx_scr[...]
        dxc_ref[:, 0:SSD_WIDTH] = dx_all * dt_e + dsk_ref[...] * dy_all
        red = red_scr[...]
        dac_slab = _head_sums(red, expander)
        ddec_tot = _head_sums(jnp.broadcast_to(jnp.concatenate(ddec_cols, axis=1), (8, SSD_WIDTH)), expander)
        ddt_x = _head_sums(dx_all * xs_all, expander)
        dsk16_ref[...] += _head_sums(jnp.broadcast_to(dsk_cols, (8, SSD_WIDTH)), expander)[0:1, 0:16]
        dac = dac + dac_slab + jnp.transpose(dac_row) + jnp.where(last_row, ddec_tot[0:1, :], 0.0)
        triu = (row <= col).astype(F32)
        da = jnp.dot(triu, dac, precision=HIGHEST, preferred_element_type=F32)
        a_row = -jnp.exp(al_ref[...])
        ddt = jnp.where(lane128 < 16, (ddt_x + da * a_row) * gd_ref[...], 0.0)
        ddt_ref[...] = ddt
        da16_ref[...] += (jnp.sum(da * dt_all, axis=0, keepdims=True) * a_row)[:, 0:16]
        db16_ref[...] += jnp.sum(ddt, axis=0, keepdims=True)[:, 0:16]

    cps = SSD_BWD_CHUNKS
    steps = nch // cps
    rb = lambda b, c: (b * steps + steps - 1 - c, 0)
    v1k = pl.BlockSpec((1, SSD_WIDTH), lambda b, c: (0, 0))
    v16 = pl.BlockSpec((1, 16), lambda b, c: (0, 0))
    v128 = pl.BlockSpec((1, 128), lambda b, c: (0, 0))
    wide = pl.BlockSpec((cps * L, SSD_WIDTH), rb)
    s128 = pl.BlockSpec((cps * L, 128), rb)
    return pl.pallas_call(
        body, name="ssd_bwd", grid=(bl, steps),
        in_specs=[wide, pl.BlockSpec((cps * L, CONV_CH), rb), wide, wide,
                  pl.BlockSpec((cps, SSD_WIDTH, SSD_STATE), lambda b, c: (b * steps + steps - 1 - c, 0, 0)),
                  s128, s128, s128, pl.BlockSpec((16, cps * L), lambda b, c: (0, b * steps + steps - 1 - c)),
                  v128, v1k, v1k],
        out_specs=[pl.BlockSpec((cps * L, CONV_CH), rb), wide, s128, v1k, v16, v16, v16],
        out_shape=[jax.ShapeDtypeStruct((n, CONV_CH), F32), jax.ShapeDtypeStruct((n, PA_WIDTH), BF16),
                   jax.ShapeDtypeStruct((n, 128), F32), jax.ShapeDtypeStruct((1, SSD_WIDTH), F32),
                   jax.ShapeDtypeStruct((1, 16), F32), jax.ShapeDtypeStruct((1, 16), F32),
                   jax.ShapeDtypeStruct((1, 16), F32)],
        scratch_shapes=[pltpu.VMEM((SSD_WIDTH, SSD_STATE), F32), pltpu.VMEM((L, SSD_WIDTH), F32),
                        pltpu.VMEM((L, SSD_WIDTH), BF16), pltpu.VMEM((L, SSD_WIDTH), F32),
                        pltpu.VMEM((L, SSD_WIDTH), F32)],
        compiler_params=_cparams(("arbitrary", "arbitrary")),
    )(dys, xc, proj_a, ypre, hprev, dt, gate_d, acum, acum_t, alog128, dskip_e, norm_w)


def _attn_fwd(qkv, negc, bl, t):
    n = bl * t
    tb_ = min(t, ATT_FWD_BLOCK)
    nb = t // tb_
    scale2 = LOG2E / math.sqrt(HEAD_DIM)

    def body(q_ref, k_ref, v_ref, c_ref, o_ref, lse_ref, v0_scr, v1_scr, k0_scr, k1_scr):
        row = lax.broadcasted_iota(jnp.int32, (tb_, tb_), 0)
        col = lax.broadcasted_iota(jnp.int32, (tb_, tb_), 1)
        causal = row >= col
        lane = lax.broadcasted_iota(jnp.int32, (1, 128), 1)
        v_pair = v_ref[...].astype(F32)
        k_pair = k_ref[...]
        v_scrs = (v0_scr, v1_scr)
        k_scrs = (k0_scr, k1_scr)
        for j in range(2):
            v_head = v_pair if j == 0 else pltpu.roll(v_pair, HEAD_DIM, 1)
            v_scrs[j][...] = jnp.where(lane < HEAD_DIM, v_head, jnp.where(lane == HEAD_DIM, 1.0, 0.0)).astype(BF16)
            k_scrs[j][...] = jnp.where((lane < HEAD_DIM) == (j == 0), k_pair, jnp.zeros_like(k_pair))
        for qi in range(nb):
            r0, lk = qi * tb_, (qi + 1) * tb_
            for j in range(2):
                sl = slice(j * HEAD_DIM, (j + 1) * HEAD_DIM)
                s = _dot(q_ref[r0:lk, :], k_scrs[j][0:lk, :], NT_DIMS) * scale2 + c_ref[j:j + 1, 0:lk] * LOG2E
                tail = jnp.where(causal, s[:, r0:lk], NEG)
                s = tail if qi == 0 else jnp.concatenate([s[:, 0:r0], tail], axis=1)
                m = jnp.max(s, axis=1, keepdims=True)
                p = jnp.exp2(s - m)
                acc = _dot(p.astype(BF16), v_scrs[j][0:lk, :])
                l = acc[:, HEAD_DIM:HEAD_DIM + 1]
                o_ref[r0:lk, sl] = (acc[:, 0:HEAD_DIM] / l).astype(BF16)
                lse_ref[r0:lk, sl] = jnp.broadcast_to(m + jnp.log(l) * LOG2E, (tb_, HEAD_DIM))

    blk = lambda off: pl.BlockSpec((t, 128), lambda b, hp: (b, off + hp))
    return pl.pallas_call(
        body, name="fox_attn_fwd", grid=(bl, 8),
        in_specs=[blk(0), blk(8), blk(16), pl.BlockSpec((None, None, 8, t), lambda b, hp: (b, hp, 0, 0))],
        out_specs=[blk(0), blk(0)],
        out_shape=[jax.ShapeDtypeStruct((n, ATT_WIDTH), BF16), jax.ShapeDtypeStruct((n, ATT_WIDTH), F32)],
        scratch_shapes=[pltpu.VMEM((t, 128), BF16)] * 4,
        compiler_params=_cparams(("parallel", "parallel")),
    )(qkv, qkv, qkv, negc)


def _attn_bwd(qkv, do, o, lse, negc, after, bl, t):
    n = bl * t
    tb_ = min(t, ATT_BWD_BLOCK)
    nb = t // tb_
    scale = 1.0 / math.sqrt(HEAD_DIM)
    scale2 = LOG2E * scale

    def body(q_ref, k_ref, v_ref, do_ref, o_ref, lse_ref, c_ref, after_ref, dq_ref, dk_ref, dv_ref, dc_ref,
             dq0_scr, delta_scr, dq1_scr, qt0_scr, qt1_scr, dot_scr, dkt0_scr, dkt1_scr, dvt_scr):
        row = lax.broadcasted_iota(jnp.int32, (tb_, tb_), 0)
        col = lax.broadcasted_iota(jnp.int32, (tb_, tb_), 1)
        causal = row >= col
        lane = lax.broadcasted_iota(jnp.int32, (1, 128), 1)
        dq_scrs = (dq0_scr, dq1_scr)
        qt_scrs = (qt0_scr, qt1_scr)
        dkt_scrs = (dkt0_scr, dkt1_scr)
        dq0_scr[...] = jnp.zeros_like(dq0_scr)
        dq1_scr[...] = jnp.zeros_like(dq1_scr)
        dc_ref[...] = jnp.zeros_like(dc_ref)
        q_t = jnp.transpose(q_ref[...].astype(F32))
        ones_row = jnp.where(lax.broadcasted_iota(jnp.int32, (8, t), 0) == 0, 1.0, 0.0)
        for j in range(2):
            qt_scrs[j][...] = jnp.concatenate(
                [q_t[j * HEAD_DIM:(j + 1) * HEAD_DIM, :], ones_row, jnp.zeros((HEAD_DIM - 8, t), F32)],
                axis=0).astype(BF16)
        dot_scr[...] = jnp.transpose(do_ref[...].astype(F32)).astype(BF16)
        prod = do_ref[...].astype(F32) * o_ref[...].astype(F32)
        for j in range(2):
            sl = slice(j * HEAD_DIM, (j + 1) * HEAD_DIM)
            delta_scr[:, sl] = jnp.broadcast_to(jnp.sum(prod[:, sl], axis=1, keepdims=True), (t, HEAD_DIM))
        for kj in range(nb):
            r0, r1 = kj * tb_, (kj + 1) * tb_
            k_blk = k_ref[r0:r1, :]
            v_blk = v_ref[r0:r1, :]
            k_pair = k_blk.astype(F32)
            for j in range(2):
                sl = slice(j * HEAD_DIM, (j + 1) * HEAD_DIM)
                one = slice(j * HEAD_DIM, j * HEAD_DIM + 1)
                own = (lane < HEAD_DIM) == (j == 0)
                k_head = k_pair if j == 0 else pltpu.roll(k_pair, HEAD_DIM, 1)
                k_ones = jnp.where(lane < HEAD_DIM, k_head, jnp.where(lane == HEAD_DIM, 1.0, 0.0)).astype(BF16)
                s = (_dot(q_ref[r0:t, :], jnp.where(own, k_blk, jnp.zeros_like(k_blk)), NT_DIMS) * scale2
                     + c_ref[j:j + 1, r0:r1] * LOG2E)
                head = jnp.where(causal, s[0:tb_, :], NEG)
                s = head if kj == nb - 1 else jnp.concatenate([head, s[tb_:, :]], axis=0)
                p = jnp.exp2(s - lse_ref[r0:t, one])
                dp = _dot(do_ref[r0:t, :], jnp.where(own, v_blk, jnp.zeros_like(v_blk)), NT_DIMS)
                ds = p * (dp - delta_scr[r0:t, one])
                dsb = ds.astype(BF16)
                dvt_scr[sl, r0:r1] = _dot(dot_scr[sl, r0:t], p.astype(BF16))
                dkt_scrs[j][:, r0:r1] = _dot(qt_scrs[j][:, r0:t], dsb)
                dq_scrs[j][r0:t, :] += _dot(dsb, k_ones)
        dv_ref[...] = jnp.transpose(dvt_scr[...]).astype(BF16)
        for j in range(2):
            sl = slice(j * HEAD_DIM, (j + 1) * HEAD_DIM)
            acc = dq_scrs[j][...]
            dkt = dkt_scrs[j][...]
            dq_ref[:, sl] = (acc[:, 0:HEAD_DIM] * scale).astype(BF16)
            dk_ref[:, sl] = (jnp.transpose(dkt)[:, 0:HEAD_DIM] * scale).astype(BF16)
            dc_ref[j:j + 1, :] = jnp.transpose(acc)[HEAD_DIM:HEAD_DIM + 1, :] - dkt[HEAD_DIM:HEAD_DIM + 1, :]

    blk = lambda off: pl.BlockSpec((t, 128), lambda b, hp: (b, off + hp))
    cblk = pl.BlockSpec((None, None, 8, t), lambda b, hp: (b, hp, 0, 0))
    return pl.pallas_call(
        body, name="fox_attn_bwd", grid=(bl, 8),
        in_specs=[blk(0), blk(8), blk(16), blk(0), blk(0), blk(0), cblk, ANY],
        out_specs=[blk(0), blk(0), blk(0), cblk],
        out_shape=[jax.ShapeDtypeStruct((n, ATT_WIDTH), BF16)] * 3 + [jax.ShapeDtypeStruct((bl, 8, 8, t), F32)],
        scratch_shapes=[pltpu.VMEM((t, 128), F32), pltpu.VMEM((t, 128), F32), pltpu.VMEM((t, 128), F32),
                        pltpu.VMEM((128, t), BF16), pltpu.VMEM((128, t), BF16), pltpu.VMEM((128, t), BF16),
                        pltpu.VMEM((128, t), F32), pltpu.VMEM((128, t), F32), pltpu.VMEM((128, t), F32)],
        compiler_params=_cparams(("parallel", "parallel")),
    )(qkv, qkv, qkv, do, o, lse, negc, after)


def _adamw(w, g, m, v, *, name):
    lead = w.ndim == 3
    r, c = w.shape[-2:]
    tr = _pick(r, (256, IN_SHARD // 3, 128, 64, 32, 16, 8))
    bc1 = 1.0 - ADAM_B1 ** ADAM_STEP
    bc2 = 1.0 - ADAM_B2 ** ADAM_STEP

    def body(w_ref, g_ref, m_ref, v_ref, d_ref, nm_ref, nv_ref):
        gv = g_ref[...]
        mn = ADAM_B1 * m_ref[...] + (1.0 - ADAM_B1) * gv
        vn = ADAM_B2 * v_ref[...] + (1.0 - ADAM_B2) * (gv * gv)
        m_hat = mn / bc1
        v_hat = vn / bc2
        d_ref[...] = -ADAM_LR * (m_hat / (jnp.sqrt(v_hat) + ADAM_EPS) + ADAM_WD * w_ref[...])
        nm_ref[...] = mn
        nv_ref[...] = vn

    flat = pl.BlockSpec((tr, c), lambda i: (i, 0))
    blk = pl.BlockSpec((None, tr, c), lambda i: (0, i, 0)) if lead else flat
    return pl.pallas_call(
        body, name=name, grid=(r // tr,), in_specs=[blk, flat, blk, blk], out_specs=[blk] * 3,
        out_shape=[jax.ShapeDtypeStruct(w.shape, F32)] * 3,
        compiler_params=_cparams(("parallel",)),
    )(w, g, m, v)


def _sum_leading(parts, *, name, out_dtype=F32):
    k, r, c = parts.shape
    tr = _pick(r, (512, 256, 128, 96, 64, 32, 16, 8))

    def body(p_ref, o_ref):
        acc = p_ref[0].astype(F32)
        for i in range(1, k):
            acc = acc + p_ref[i].astype(F32)
        o_ref[...] = acc.astype(out_dtype)

    return pl.pallas_call(
        body, name=name, grid=(r // tr,),
        in_specs=[pl.BlockSpec((k, tr, c), lambda i: (0, i, 0))],
        out_specs=pl.BlockSpec((tr, c), lambda i: (i, 0)),
        out_shape=jax.ShapeDtypeStruct((r, c), out_dtype),
        compiler_params=_cparams(("parallel",)),
    )(parts)


def _add_my_half(g, b, *, name):
    k, r, c = b.shape
    tr = _pick(r, (512, 256, 128))
    nrt = r // tr

    def body(lo_ref, hi_ref, b_ref, o_ref):
        mine = jnp.where(lax.axis_index("c") == 0, lo_ref[...], hi_ref[...])
        o_ref[...] = (mine.astype(F32) + b_ref[...].astype(F32)).astype(BF16)

    blk = pl.BlockSpec((None, tr, c), lambda j, i: (j, i, 0))
    return pl.pallas_call(
        body, name=name, grid=(k, nrt),
        in_specs=[blk, pl.BlockSpec((None, tr, c), lambda j, i: (j, i + nrt, 0)), blk], out_specs=blk,
        out_shape=jax.ShapeDtypeStruct((k, r, c), BF16),
        compiler_params=_cparams(("parallel", "parallel")),
    )(g, g, b)


ANY = pl.BlockSpec(memory_space=pl.ANY)


def _chip_peers(x, y):
    return [(1 - x, y, 2 * (1 - x) + y), (x, 1 - y, 2 * x + 1 - y), (1 - x, 1 - y, 2 * (1 - x) + 1 - y)]


def _gather_weights(blob, *, name):
    rows, cols = blob.shape
    half_rows = rows // 2

    def body(b_ref, o_ref, send_sems, recv_sems):
        x, y, c = lax.axis_index("x"), lax.axis_index("y"), lax.axis_index("c")
        me = 2 * x + y
        sibling = (x, y, 1 - c)
        peers = _chip_peers(x, y)

        def half(chip, hc):
            return o_ref.at[chip, pl.ds(hc * half_rows, half_rows), :]

        def copy(k, src, chip, hc, to):
            return pltpu.make_async_remote_copy(src_ref=src, dst_ref=half(chip, hc), send_sem=send_sems.at[k],
                                                recv_sem=recv_sems.at[k], device_id=to, device_id_type=MESH)

        my_half = b_ref.at[pl.ds(c * half_rows, half_rows), :]
        first = [copy(k, my_half, me, c, (px, py, c)) for k, (px, py, _) in enumerate(peers)]
        own = pltpu.make_async_remote_copy(src_ref=b_ref, dst_ref=o_ref.at[me], send_sem=send_sems.at[6],
                                           recv_sem=recv_sems.at[6], device_id=sibling, device_id_type=MESH)
        for cp in first + [own]:
            cp.start()
        passed = [copy(3 + k, half(pc, c), pc, c, sibling) for k, (_, _, pc) in enumerate(peers)]
        for k, (px, py, pc) in enumerate(peers):
            copy(k, my_half, pc, c, (px, py, c)).wait_recv()
            passed[k].start()
        for k, (_, _, pc) in enumerate(peers):
            copy(3 + k, half(pc, 1 - c), pc, 1 - c, sibling).wait_recv()
        own.wait_recv()
        for cp in first + passed + [own]:
            cp.wait_send()

    return pl.pallas_call(
        body, name=name, in_specs=[ANY], out_specs=ANY,
        out_shape=jax.ShapeDtypeStruct((N_CHIPS, rows, cols), BF16),
        scratch_shapes=[pltpu.SemaphoreType.DMA((7,)), pltpu.SemaphoreType.DMA((7,))],
    )(blob)


def _swap_halves(g, *, name):
    _, rows, cols = g.shape
    half_rows = rows // 2

    def body(g_ref, o_ref, send_sem, recv_sem):
        x, y, c = lax.axis_index("x"), lax.axis_index("y"), lax.axis_index("c")
        cp = pltpu.make_async_remote_copy(
            src_ref=g_ref.at[:, pl.ds((1 - c) * half_rows, half_rows), :], dst_ref=o_ref,
            send_sem=send_sem, recv_sem=recv_sem, device_id=(x, y, 1 - c), device_id_type=MESH)
        cp.start()
        cp.wait()

    return pl.pallas_call(
        body, name=name, in_specs=[ANY], out_specs=ANY,
        out_shape=jax.ShapeDtypeStruct((N_CHIPS, half_rows, cols), BF16),
        scratch_shapes=[pltpu.SemaphoreType.DMA, pltpu.SemaphoreType.DMA],
    )(g)


HBM_SPEC = pl.BlockSpec(memory_space=pltpu.HBM)
SEM_SPEC = pl.BlockSpec(memory_space=pltpu.SEMAPHORE)
SPLIT_EFFECT = pltpu.SideEffectType.DATAFLOW_SIDE_EFFECTING


def _gather_peers_copies(b_ref, land_ref, send_sems, recv_sems, sending):
    x, y, c = lax.axis_index("x"), lax.axis_index("y"), lax.axis_index("c")
    me = 2 * x + y
    half_rows = b_ref.shape[0] // 2
    src = b_ref.at[pl.ds(c * half_rows, half_rows), :]
    return [pltpu.make_async_remote_copy(
        src_ref=src, dst_ref=land_ref.at[me if sending else pc, pl.ds(c * half_rows, half_rows), :],
        send_sem=send_sems.at[k], recv_sem=recv_sems.at[k], device_id=(px, py, c), device_id_type=MESH)
        for k, (px, py, pc) in enumerate(_chip_peers(x, y))]


def _gather_start(blob, after, *, name):
    shape = (N_CHIPS,) + blob.shape

    def body(b_ref, land_ref, after_ref, send_sems, recv_sems, b_thru, land_thru, token):
        for cp in _gather_peers_copies(b_ref, land_ref, send_sems, recv_sems, True):
            cp.start()
        token[...] = jnp.zeros_like(token)

    return pl.pallas_call(
        body, name=name,
        out_shape=(pltpu.SemaphoreType.DMA((3,)), pltpu.SemaphoreType.DMA((3,)), pltpu.HBM(blob.shape, blob.dtype),
                   pltpu.HBM(shape, blob.dtype), jax.ShapeDtypeStruct((8, 128), F32)),
        in_specs=(HBM_SPEC, HBM_SPEC, ANY),
        out_specs=(SEM_SPEC, SEM_SPEC, HBM_SPEC, HBM_SPEC, pl.BlockSpec(memory_space=pltpu.VMEM)),
        input_output_aliases={0: 2, 1: 3},
        compiler_params=pltpu.CompilerParams(has_side_effects=SPLIT_EFFECT),
    )(pltpu.with_memory_space_constraint(blob, pltpu.HBM),
      pltpu.with_memory_space_constraint(lax.empty(shape, blob.dtype), pltpu.HBM), after)


def _gather_wait(send_sems, recv_sems, b_thru, land_thru, after, *, name):
    def body(b_ref, land_ref, send_sems, recv_sems, after_ref, b_dead, got_ref):
        for cp in _gather_peers_copies(b_ref, land_ref, send_sems, recv_sems, False):
            cp.wait_send()
            cp.wait_recv()

    return pl.pallas_call(
        body, name=name,
        out_shape=(pltpu.HBM(b_thru.shape, b_thru.dtype), pltpu.HBM(land_thru.shape, land_thru.dtype)),
        in_specs=(HBM_SPEC, HBM_SPEC, SEM_SPEC, SEM_SPEC, ANY), out_specs=(HBM_SPEC, HBM_SPEC),
        input_output_aliases={0: 0, 1: 1},
        compiler_params=pltpu.CompilerParams(has_side_effects=SPLIT_EFFECT),
    )(b_thru, land_thru, send_sems, recv_sems, after)


def _gather_forward(land, blob, *, name):
    half_rows = land.shape[1] // 2

    def body(l_ref, b_ref, o_ref, send_sems, recv_sems):
        x, y, c = lax.axis_index("x"), lax.axis_index("y"), lax.axis_index("c")
        me = 2 * x + y
        sibling = (x, y, 1 - c)
        cps = []
        for k, (_, _, pc) in enumerate(_chip_peers(x, y)):
            mine = pl.ds(c * half_rows, half_rows)
            cps.append(pltpu.make_async_remote_copy(
                src_ref=l_ref.at[pc, mine, :], dst_ref=o_ref.at[pc, mine, :], send_sem=send_sems.at[k],
                recv_sem=recv_sems.at[k], device_id=sibling, device_id_type=MESH))
        cps.append(pltpu.make_async_remote_copy(src_ref=b_ref, dst_ref=o_ref.at[me], send_sem=send_sems.at[3],
                                                recv_sem=recv_sems.at[3], device_id=sibling, device_id_type=MESH))
        for cp in cps:
            cp.start()
        for k, (_, _, pc) in enumerate(_chip_peers(x, y)):
            theirs = pl.ds((1 - c) * half_rows, half_rows)
            pltpu.make_async_remote_copy(
                src_ref=l_ref.at[pc, theirs, :], dst_ref=o_ref.at[pc, theirs, :], send_sem=send_sems.at[k],
                recv_sem=recv_sems.at[k], device_id=sibling, device_id_type=MESH).wait_recv()
        cps[3].wait_recv()
        for cp in cps:
            cp.wait_send()

    return pl.pallas_call(
        body, name=name, in_specs=[ANY, ANY], out_specs=ANY, input_output_aliases={0: 0},
        out_shape=jax.ShapeDtypeStruct(land.shape, land.dtype),
        scratch_shapes=[pltpu.SemaphoreType.DMA((4,)), pltpu.SemaphoreType.DMA((4,))],
    )(land, blob)


def _exchange_peers_copies(p_ref, land_ref, send_sems, recv_sems, sending):
    x, y, c = lax.axis_index("x"), lax.axis_index("y"), lax.axis_index("c")
    me = 2 * x + y
    return [pltpu.make_async_remote_copy(src_ref=p_ref.at[pc], dst_ref=land_ref.at[me if sending else pc],
                                         send_sem=send_sems.at[k], recv_sem=recv_sems.at[k],
                                         device_id=(px, py, c), device_id_type=MESH)
            for k, (px, py, pc) in enumerate(_chip_peers(x, y))]


def _exchange_start(p, *, name):
    def body(p_ref, land_ref, send_sems, recv_sems, p_thru, land_thru, token):
        for cp in _exchange_peers_copies(p_ref, land_ref, send_sems, recv_sems, True):
            cp.start()
        token[...] = jnp.zeros_like(token)

    return pl.pallas_call(
        body, name=name,
        out_shape=(pltpu.SemaphoreType.DMA((3,)), pltpu.SemaphoreType.DMA((3,)), pltpu.HBM(p.shape, p.dtype),
                   pltpu.HBM(p.shape, p.dtype), jax.ShapeDtypeStruct((8, 128), F32)),
        in_specs=(HBM_SPEC, HBM_SPEC),
        out_specs=(SEM_SPEC, SEM_SPEC, HBM_SPEC, HBM_SPEC, pl.BlockSpec(memory_space=pltpu.VMEM)),
        input_output_aliases={0: 2, 1: 3},
        compiler_params=pltpu.CompilerParams(has_side_effects=SPLIT_EFFECT),
    )(pltpu.with_memory_space_constraint(p, pltpu.HBM),
      pltpu.with_memory_space_constraint(lax.empty(p.shape, p.dtype), pltpu.HBM))


def _exchange_wait(send_sems, recv_sems, p_thru, land_thru, after, *, name):
    def body(p_ref, land_ref, send_sems, recv_sems, after_ref, p_dead, got_ref):
        for cp in _exchange_peers_copies(p_ref, land_ref, send_sems, recv_sems, False):
            cp.wait_send()
            cp.wait_recv()

    return pl.pallas_call(
        body, name=name,
        out_shape=(pltpu.HBM(p_thru.shape, p_thru.dtype), pltpu.HBM(p_thru.shape, p_thru.dtype)),
        in_specs=(HBM_SPEC, HBM_SPEC, SEM_SPEC, SEM_SPEC, ANY), out_specs=(HBM_SPEC, HBM_SPEC),
        input_output_aliases={0: 0, 1: 1},
        compiler_params=pltpu.CompilerParams(has_side_effects=SPLIT_EFFECT),
    )(p_thru, land_thru, send_sems, recv_sems, after)


def _sum_parts(parts, own, *, name):
    k, r, c = parts.shape
    tr = _pick(r, (512, 256, 128))

    def body(p_ref, own_ref, o_ref):
        me = 2 * lax.axis_index("x") + lax.axis_index("y")
        acc = jnp.zeros((tr, c), F32)
        for i in range(k):
            acc = acc + jnp.where(me == i, own_ref[i], p_ref[i]).astype(F32)
        o_ref[...] = acc

    blk = pl.BlockSpec((k, tr, c), lambda i: (0, i, 0))
    return pl.pallas_call(
        body, name=name, grid=(r // tr,), in_specs=[blk, blk],
        out_specs=pl.BlockSpec((tr, c), lambda i: (i, 0)),
        out_shape=jax.ShapeDtypeStruct((r, c), F32),
        compiler_params=_cparams(("parallel",)),
    )(parts, own)


def _from_sibling(a, *, name):
    def body(a_ref, o_ref, send_sem, recv_sem):
        x, y, c = lax.axis_index("x"), lax.axis_index("y"), lax.axis_index("c")
        cp = pltpu.make_async_remote_copy(src_ref=a_ref, dst_ref=o_ref, send_sem=send_sem, recv_sem=recv_sem,
                                          device_id=(x, y, 1 - c), device_id_type=MESH)
        cp.start()
        cp.wait()

    return pl.pallas_call(
        body, name=name, in_specs=[ANY], out_specs=ANY,
        out_shape=jax.ShapeDtypeStruct(a.shape, F32),
        scratch_shapes=[pltpu.SemaphoreType.DMA, pltpu.SemaphoreType.DMA],
    )(a)


def _join_halves(gh, *, name):
    other = _from_sibling(gh, name=name)
    south = lax.axis_index("c") == 0
    return jnp.concatenate([jnp.where(south, gh, other), jnp.where(south, other, gh)], axis=0)


def _add_f32(a, b, *, name):
    r, c = a.shape
    tr = _pick(r, (512, 256, 128))

    def body(a_ref, b_ref, o_ref):
        o_ref[...] = a_ref[...] + b_ref[...]

    blk = pl.BlockSpec((tr, c), lambda i: (i, 0))
    return pl.pallas_call(
        body, name=name, grid=(r // tr,), in_specs=[blk, blk], out_specs=blk,
        out_shape=jax.ShapeDtypeStruct((r, c), F32),
        compiler_params=_cparams(("parallel",)),
    )(a, b)


def _gather_small(s, *, name):
    rows = s.shape[0]

    def body(s_ref, o_ref, send_sems, recv_sems, local_sem):
        x, y, c = lax.axis_index("x"), lax.axis_index("y"), lax.axis_index("c")
        me = 4 * x + 2 * y + c
        mine = pltpu.make_async_copy(s_ref, o_ref.at[me], local_sem)
        mine.start()
        peers = []
        for k in range(1, 8):
            peers.append((1 - x if k & 4 else x, 1 - y if k & 2 else y, 1 - c if k & 1 else c))
        cps = [pltpu.make_async_remote_copy(src_ref=s_ref, dst_ref=o_ref.at[me], send_sem=send_sems.at[k],
                                            recv_sem=recv_sems.at[k], device_id=p, device_id_type=MESH)
               for k, p in enumerate(peers)]
        for cp in cps:
            cp.start()
        for k, (px, py, pc) in enumerate(peers):
            pltpu.make_async_remote_copy(src_ref=s_ref, dst_ref=o_ref.at[4 * px + 2 * py + pc],
                                         send_sem=send_sems.at[k], recv_sem=recv_sems.at[k],
                                         device_id=(px, py, pc), device_id_type=MESH).wait_recv()
        for cp in cps:
            cp.wait_send()
        mine.wait()

    return pl.pallas_call(
        body, name=name, in_specs=[ANY], out_specs=ANY,
        out_shape=jax.ShapeDtypeStruct((8, rows, 128), F32),
        scratch_shapes=[pltpu.SemaphoreType.DMA((7,)), pltpu.SemaphoreType.DMA((7,)), pltpu.SemaphoreType.DMA],
    )(s)


IN_SHARD = IN_WIDTH // N_CHIPS
IN_SHARD_PAD = 1536
UP_ROWS, DOWN_ROWS, OUT_ROWS = 1024, 1024, 512
REST_ROWS = UP_ROWS + DOWN_ROWS + OUT_ROWS


def _pack_in(w_in_s):
    return jnp.pad(w_in_s, ((0, 0), (0, IN_SHARD_PAD - IN_SHARD))).astype(BF16)


def _pack_rest(w_out_s, w_up_s, w_down_s):
    return jnp.concatenate([w_up_s, w_down_s, w_out_s], axis=0).astype(BF16)


def _unpack_rest(blob):
    return (blob[UP_ROWS + DOWN_ROWS:], blob[0:UP_ROWS], blob[UP_ROWS:UP_ROWS + DOWN_ROWS])


def _full_w_in(g_in):
    return jnp.concatenate([g_in[j, :, :IN_SHARD] for j in range(N_CHIPS)], axis=1)


def _full_rest(g_rest):
    parts = [_unpack_rest(g_rest[j]) for j in range(N_CHIPS)]
    w_out = jnp.concatenate([p[0] for p in parts], axis=0)
    w_up = jnp.concatenate([p[1] for p in parts], axis=1)
    w_down = jnp.concatenate([p[2] for p in parts], axis=0)
    return w_out, w_up, w_down


def _split_w_in(w_in):
    z_xbc = w_in[:, 0:2560]
    dt = w_in[:, 2560:2576]
    qkv = w_in[:, 2576:5648]
    f = w_in[:, 5648:5664]
    pad = jnp.zeros((w_in.shape[0], PA_WIDTH - 2592), w_in.dtype)
    return jnp.concatenate([z_xbc, dt, f, pad], axis=1), qkv


def _merge_w_in(d_a, d_qkv):
    return jnp.concatenate([d_a[:, 0:2560], d_a[:, 2560:2576], d_qkv, d_a[:, 2576:2592]], axis=1)


def _local_step(x3, target3, w_in, rest_weights, norm_mix_w, conv_w, conv_b, dt_bias, a_log, d_skip,
                ssd_norm_w, f_bias, norm_mlp_w, norm_final_w, first_after=None, early_grads=None, late_grads=None):
    bl, t, d = x3.shape
    n = bl * t
    x = x3.reshape(n, d)
    target = target3.reshape(n, d)
    w_a, w_qkv = _split_w_in(w_in)
    nfw = norm_final_w.reshape(1, d)
    dskip_e = jnp.repeat(d_skip, HEAD_DIM, axis=1)

    r1, r2, kt = min(n, 1024), min(n, 512), min(n, 2048)
    if first_after is None:
        first_after = jnp.zeros((8, 128), F32)
    h0, rstd0, proj_a = _norm_mm(x, norm_mix_w, w_a, first_after, name="norm_mix_proj_a", tm=r2)
    qkv = _mm(h0, w_qkv, name="proj_qkv", tiles=(r2, QKV_WIDTH, D_MODEL), out_dtype=BF16)
    bias128 = jnp.concatenate([dt_bias, f_bias, jnp.zeros((1, 96), F32)], axis=1)
    alog128 = jnp.concatenate([a_log, jnp.zeros((1, 112), F32)], axis=1)
    dt, gate_d, acum, acum_t, negc = _prep(proj_a, bias128, alog128, bl, t)
    xc, dsilu = _conv_fwd(proj_a, conv_w, conv_b, bl, t)
    y_ssd, y_pre, hprev = _ssd_fwd(xc, proj_a, dt, acum, acum_t, dskip_e, ssd_norm_w, bl, t)
    y_att, lse = _attn_fwd(qkv, negc, bl, t)
    w_out, w_up, w_down = rest_weights(y_att)
    wo_s, wo_a = w_out[:SSD_WIDTH], w_out[SSD_WIDTH:]
    h1, h1n, rstd1 = _mm_norm_fwd(y_ssd, wo_s, y_att, wo_a, x, norm_mlp_w, name="out_proj_norm_mlp", tm=r2)
    up = _mm(h1n, w_up, name="mlp_up", tiles=(r1, D_FF, D_MODEL), out_dtype=BF16)
    dh2, dh2b, loss, d_nfw = _mm_final(up, w_down, h1, nfw, target, name="mlp_down_final_norm_loss", tm=r2,
                                       a_act="relu2")

    dup = _mm(dh2b, w_down, name="mlp_down_bwd_act", tiles=(r2, D_FF, D_MODEL), tb=True, epi_up=up, out_dtype=BF16)
    rest_shape = (N_CHIPS, REST_ROWS, D_MODEL)
    gb_rest = _mm(up, dh2b, name="mlp_down_bwd_w", tiles=(DOWN_ROWS, D_MODEL, kt), ta=True, a_act="relu2",
                  out_dtype=BF16, into=(rest_shape, (None, DOWN_ROWS, D_MODEL), lambda i, j, k: (i, 1, 0), None))
    dh1, dh1b, d_nmlp = _mm_norm_bwd([(dup, w_up)], h1, rstd1, norm_mlp_w, dh2, name="mlp_up_bwd_act_norm_mlp",
                                     tm=r2)
    gb_rest = _mm(h1n, dup, name="mlp_up_bwd_w", tiles=(D_MODEL, UP_ROWS, kt), ta=True, out_dtype=BF16,
                  into=(rest_shape, (None, D_MODEL, UP_ROWS), lambda i, j, k: (j, 0, 0), gb_rest))
    dys, do = _mm_two_halves(dh1b, w_out, name="out_proj_bwd_act", tm=r1)
    out_block = (UP_ROWS + DOWN_ROWS) // OUT_ROWS
    for half, (y_half, tag) in enumerate(((y_ssd, "ssd"), (y_att, "att"))):
        gb_rest = _mm(y_half, dh1b, name="out_proj_bwd_w_" + tag, tiles=(2 * OUT_ROWS, D_MODEL, kt), ta=True,
                      out_dtype=BF16, into=(rest_shape, (2, OUT_ROWS, D_MODEL),
                                            functools.partial(lambda i, j, k, h: (h, out_block, 0), h=half),
                                            gb_rest))
    token = jnp.zeros((8, 128), F32) if early_grads is None else early_grads(gb_rest)
    dq, dk, dv, dcb = _attn_bwd(qkv, do, y_att, lse, negc, token, bl, t)
    dc = jnp.pad(dcb[:, :, 0:2, :].transpose(0, 3, 1, 2).reshape(n, 16), ((0, 0), (16, 96)))
    dxc, dpa, ddt_raw, d_snw, d_dsk, d_alog, d_dtb = _ssd_bwd(dys, xc, proj_a, y_pre, hprev, dt, gate_d, acum,
                                                             acum_t, alog128, dskip_e, ssd_norm_w, bl, t)
    dpa, d_conv_w, d_conv_b = _conv_bwd(dxc, dsilu, proj_a, conv_w, dpa, bl, t)
    dproj_a, d_fb = _fpost(dc, gate_d, ddt_raw, dpa, bl, t)
    dqkv = jnp.concatenate([dq, dk, dv], axis=1)
    d_w_a = _mm(h0, dproj_a, name="proj_a_bwd_w", tiles=(1024, 896, kt), ta=True, out_dtype=BF16)
    d_w_qkv = _mm(h0, dqkv, name="proj_qkv_bwd_w", tiles=(1024, 1024, kt), ta=True, out_dtype=BF16)
    d_w_in = _merge_w_in(d_w_a, d_w_qkv)
    late_token = None if late_grads is None else late_grads(d_w_in)
    dx, _, d_nmix = _mm_norm_bwd([(dproj_a, w_a), (dqkv, w_qkv)], x, rstd0, norm_mix_w, dh1,
                                 name="proj_bwd_act_norm_mix", tm=min(n, 256), after=late_token)

    grads = dict(norm_mix_w=d_nmix, w_in=d_w_in, conv_w=d_conv_w, conv_b=d_conv_b,
                 dt_bias=d_dtb, a_log=d_alog, d_skip=d_dsk, ssd_norm_w=d_snw, f_bias=d_fb, rest=gb_rest,
                 norm_mlp_w=d_nmlp, norm_final_w=d_nfw)
    return dx.reshape(bl, t, d), loss, grads


SMALL_ORDER = ("norm_mix_w", "conv_w", "conv_b", "dt_bias", "a_log", "d_skip", "ssd_norm_w", "f_bias",
               "norm_mlp_w", "norm_final_w")
SMALL_SIZES = (1024, 4 * CONV_CH, CONV_CH, 16, 16, 16, 1024, 16, 1024, 1024)


def _pack_small(vals, rows):
    flat = jnp.concatenate([v.reshape(-1).astype(F32) for v in vals])
    return jnp.pad(flat, (0, rows * 128 - flat.shape[0])).reshape(rows, 128)


def _unpack_small(packed, sizes):
    flat = packed.reshape(-1)
    out, o = [], 0
    for s in sizes:
        out.append(flat[o:o + s])
        o += s
    return out


def kernel(x, norm_mix_w, w_in, conv_w, conv_b, dt_bias, a_log, d_skip, ssd_norm_w, f_bias, w_out, norm_mlp_w, w_up, w_down, norm_final_w, loss_target, m_norm_mix_w, m_w_in, m_conv_w, m_conv_b, m_dt_bias, m_a_log, m_d_skip, m_ssd_norm_w, m_f_bias, m_w_out, m_norm_mlp_w, m_w_up, m_w_down, m_norm_final_w, v_norm_mix_w, v_w_in, v_conv_w, v_conv_b, v_dt_bias, v_a_log, v_d_skip, v_ssd_norm_w, v_f_bias, v_w_out, v_norm_mlp_w, v_w_up, v_w_down, v_norm_final_w):
    chip = 2 * lax.axis_index("x") + lax.axis_index("y")
    cw = CONV_CH // N_CHIPS

    own_in = _pack_in(w_in[0])
    own_rest = _pack_rest(w_out[0], w_up[0], w_down[0])
    g_in = _gather_weights(own_in, name="gather_w_in")
    w_in_f = _full_w_in(g_in)
    *rest_handles, rest_token = _gather_start(own_rest, g_in, name="gather_start_rest")

    def rest_weights(after):
        _, landed = _gather_wait(*rest_handles, after, name="gather_wait_rest")
        return _full_rest(_gather_forward(landed, own_rest, name="gather_forward_rest"))
    small_all = _gather_small(_pack_small([conv_w[0]], 16), name="gather_conv_w")
    conv_w_f = jnp.concatenate([small_all[2 * j].reshape(-1)[:4 * cw].reshape(4, cw) for j in range(N_CHIPS)], axis=1)

    def chip_partial(gb, tag):
        from_sibling = _swap_halves(gb, name="grad_swap_halves_" + tag)
        return _add_my_half(gb, from_sibling, name="grad_add_sibling_" + tag)

    in_flight = {}

    def early_grads(gb_rest):
        *handles, token = _exchange_start(gb_rest, name="grad_exchange_start_rest")
        in_flight["rest"] = handles
        return token

    def late_grads(d_w_in):
        gb_in = jnp.stack([_pack_in(d_w_in[:, j * IN_SHARD:(j + 1) * IN_SHARD]) for j in range(N_CHIPS)])
        *handles, token = _exchange_start(chip_partial(gb_in, "in"), name="grad_exchange_start_in")
        in_flight["in"] = handles
        return token

    dx, loss_part, g = _local_step(x, loss_target, w_in_f, rest_weights, norm_mix_w, conv_w_f,
                                   conv_b, dt_bias, a_log, d_skip, ssd_norm_w, f_bias, norm_mlp_w, norm_final_w,
                                   first_after=rest_token, early_grads=early_grads, late_grads=late_grads)

    send_sems, recv_sems, part_rest, land_rest = in_flight["rest"]
    part_rest, parts_rest = _exchange_wait(send_sems, recv_sems, part_rest, land_rest, dx,
                                           name="grad_exchange_wait_rest")
    g_rest_core = _sum_parts(parts_rest, part_rest, name="grad_sum_chips_rest")
    g_rest_sibling = _from_sibling(g_rest_core, name="grad_swap_sums_rest")
    g_w_out, g_w_up, g_w_down = _unpack_rest(_add_f32(g_rest_core, g_rest_sibling, name="grad_add_cores_rest"))

    part_in, parts_in = _exchange_wait(*in_flight["in"], dx, name="grad_exchange_wait_in")
    g_in_half = _sum_parts(parts_in, part_in, name="grad_sum_chips_in")
    g_w_in = _join_halves(g_in_half, name="grad_join_halves_in")[:, :IN_SHARD]

    small_vals = [g[k] for k in SMALL_ORDER] + [loss_part[:, 0:1]]
    small_sum = _sum_leading(_gather_small(_pack_small(small_vals, SMALL_ROWS), name="gather_small_grads"), name="small_sum")
    sg = dict(zip(SMALL_ORDER + ("loss",), _unpack_small(small_sum, SMALL_SIZES + (1,))))
    loss = sg["loss"].reshape(())
    g_conv_full = sg["conv_w"].reshape(4, CONV_CH)
    g_conv = lax.dynamic_slice_in_dim(g_conv_full, chip * cw, cw, axis=1)

    grads = dict(norm_mix_w=sg["norm_mix_w"].reshape(1, -1), w_in=g_w_in[None], conv_w=g_conv[None],
                 conv_b=sg["conv_b"].reshape(1, -1), dt_bias=sg["dt_bias"].reshape(1, -1),
                 a_log=sg["a_log"].reshape(1, -1), d_skip=sg["d_skip"].reshape(1, -1),
                 ssd_norm_w=sg["ssd_norm_w"].reshape(1, -1), f_bias=sg["f_bias"].reshape(1, -1), w_out=g_w_out[None],
                 norm_mlp_w=sg["norm_mlp_w"].reshape(1, -1), w_up=g_w_up[None], w_down=g_w_down[None],
                 norm_final_w=sg["norm_final_w"])
    weights = dict(norm_mix_w=norm_mix_w, w_in=w_in, conv_w=conv_w, conv_b=conv_b, dt_bias=dt_bias, a_log=a_log,
                   d_skip=d_skip, ssd_norm_w=ssd_norm_w, f_bias=f_bias, w_out=w_out, norm_mlp_w=norm_mlp_w,
                   w_up=w_up, w_down=w_down, norm_final_w=norm_final_w)
    ms = dict(norm_mix_w=m_norm_mix_w, w_in=m_w_in, conv_w=m_conv_w, conv_b=m_conv_b, dt_bias=m_dt_bias,
              a_log=m_a_log, d_skip=m_d_skip, ssd_norm_w=m_ssd_norm_w, f_bias=m_f_bias, w_out=m_w_out,
              norm_mlp_w=m_norm_mlp_w, w_up=m_w_up, w_down=m_w_down, norm_final_w=m_norm_final_w)
    vs = dict(norm_mix_w=v_norm_mix_w, w_in=v_w_in, conv_w=v_conv_w, conv_b=v_conv_b, dt_bias=v_dt_bias,
              a_log=v_a_log, d_skip=v_d_skip, ssd_norm_w=v_ssd_norm_w, f_bias=v_f_bias, w_out=v_w_out,
              norm_mlp_w=v_norm_mlp_w, w_up=v_w_up, w_down=v_w_down, norm_final_w=v_norm_final_w)
    names = list(weights)
    big = ("w_in", "w_out", "w_up", "w_down")
    delta, new_m, new_v = {}, {}, {}
    for k, g2 in zip(big[1:], (g_w_out, g_w_up, g_w_down)):
        delta[k], new_m[k], new_v[k] = _adamw(weights[k], g2, ms[k], vs[k], name="adamw_" + k)
    g_in_t = g_w_in.T
    outs_t = _adamw(w_in[0].T, g_in_t, m_w_in[0].T, v_w_in[0].T, name="adamw_w_in")
    delta["w_in"], new_m["w_in"], new_v["w_in"] = [o.T[None] for o in outs_t]
    grads["w_in"] = g_in_t.T[None]
    smalls = [k for k in names if k not in big]
    sizes = [math.prod(weights[k].shape) for k in smalls]
    rows = -(-sum(sizes) // 1024) * 8
    packs = [_pack_small([d[k] for k in smalls], rows) for d in (weights, grads, ms, vs)]
    outs = _adamw(*packs, name="adamw_small")
    for o, dst in zip(outs, (delta, new_m, new_v)):
        for k, val in zip(smalls, _unpack_small(o, sizes)):
            dst[k] = val.reshape(weights[k].shape)
    return (loss, dx, *[grads[k] for k in names], *[delta[k] for k in names], *[new_m[k] for k in names],
            *[new_v[k] for k in names])
```

```python
import functools
import math

import jax
import jax.numpy as jnp
from jax import lax
from jax.experimental import pallas as pl
from jax.experimental.pallas import tpu as pltpu

F32 = jnp.float32
BF16 = jnp.bfloat16
HIGHEST = lax.Precision.HIGHEST
MESH = pl.DeviceIdType.MESH

D_MODEL = 1024
HEAD_DIM = 64
SSD_WIDTH = 1024
SSD_STATE = 128
CONV_CH = 1536
CHUNK = 128
ATT_WIDTH = 1024
EPS = 1e-5
IN_WIDTH = 5664
PA_WIDTH = 2688
QKV_WIDTH = 3072
D_FF = 4096
ATT_FWD_BLOCK = 512
ATT_BWD_BLOCK = 256
NEG = -1e30
LOG2E = 1.4426950408889634
VMEM_LIMIT = 48 * 1024 * 1024

ADAM_LR = 0.001
ADAM_B1 = 0.9
ADAM_B2 = 0.999
ADAM_EPS = 1e-08
ADAM_WD = 0.01
ADAM_STEP = 10

N_CHIPS = 4
SMALL_ROWS = 96


def _cparams(sem):
    return pltpu.CompilerParams(dimension_semantics=sem, vmem_limit_bytes=VMEM_LIMIT)


def _pick(n, cands):
    for c in cands:
        if n % c == 0:
            return c
    return n


MM_CHUNK = 512


def _mm(a, b, *, name, tiles, ta=False, tb=False, out_dtype=F32, res=None, a_act=None, epi_up=None, after=None,
        into=None):
    n_unread = (after is not None) + (into is not None and into[3] is not None)
    if ta:
        K, M = a.shape
    else:
        M, K = a.shape
    if tb:
        N, K2 = b.shape
    else:
        K2, N = b.shape
    assert K == K2, (a.shape, b.shape)
    tm, tn, tk = tiles
    assert M % tm == 0 and N % tn == 0 and K % tk == 0, (name, M, N, K, tiles)
    nk = K // tk
    dn = (((0 if ta else 1,), (1 if tb else 0,)), ((), ()))
    has_res = res is not None
    has_up = epi_up is not None
    cn = _pick(tn, (MM_CHUNK, 384, 256, 128))

    def prologue(av):
        if a_act == "relu2":
            r = jnp.maximum(av, 0)
            av = r * r
        return av.astype(BF16)

    def epilogue(out, res_v, up_v):
        if has_res:
            out = out + res_v.astype(F32)
        if has_up:
            out = out * (2.0 * jnp.maximum(up_v.astype(F32), 0.0))
        return out.astype(out_dtype)

    def body(*refs):
        a_ref, b_ref = refs[0], refs[1]
        i = 2
        res_ref = up_ref = None
        if has_res:
            res_ref = refs[i]
            i += 1
        if has_up:
            up_ref = refs[i]
            i += 1
        i += n_unread
        o_ref = refs[i]
        if nk == 1:
            av = prologue(a_ref[...])
            if len(o_ref.shape) == 3:
                out = lax.dot_general(av, b_ref[...].astype(BF16), dn, preferred_element_type=F32)
                out = epilogue(out, res_ref[...] if has_res else None, up_ref[...] if has_up else None)
                o_ref[...] = out.reshape(o_ref.shape)
                return
            for c in range(tn // cn):
                cs = slice(c * cn, (c + 1) * cn)
                bv = (b_ref[cs, :] if tb else b_ref[:, cs]).astype(BF16)
                out = lax.dot_general(av, bv, dn, preferred_element_type=F32)
                o_ref[:, cs] = epilogue(out, res_ref[:, cs] if has_res else None, up_ref[:, cs] if has_up else None)
            return
        acc_ref = refs[i + 1]
        k = pl.program_id(2)

        @pl.when(k == 0)
        def _():
            acc_ref[...] = jnp.zeros_like(acc_ref)

        acc_ref[...] += lax.dot_general(prologue(a_ref[...]), b_ref[...].astype(BF16), dn,
                                        preferred_element_type=F32)

        @pl.when(k == nk - 1)
        def _():
            out = epilogue(acc_ref[...], res_ref[...] if has_res else None, up_ref[...] if has_up else None)
            o_ref[...] = out.reshape(o_ref.shape)

    a_spec = pl.BlockSpec((tk, tm), lambda i, j, k: (k, i)) if ta else pl.BlockSpec((tm, tk), lambda i, j, k: (i, k))
    b_spec = pl.BlockSpec((tn, tk), lambda i, j, k: (j, k)) if tb else pl.BlockSpec((tk, tn), lambda i, j, k: (k, j))
    o_spec = pl.BlockSpec((tm, tn), lambda i, j, k: (i, j))
    ins, specs = [a, b], [a_spec, b_spec]
    if has_res:
        ins.append(res)
        specs.append(o_spec)
    if has_up:
        ins.append(epi_up)
        specs.append(o_spec)
    if after is not None:
        ins.append(after)
        specs.append(pl.BlockSpec(memory_space=pl.ANY))
    out_shape, out_spec, aliases = jax.ShapeDtypeStruct((M, N), out_dtype), o_spec, {}
    if into is not None:
        shape, block, index, buf = into
        out_shape, out_spec = jax.ShapeDtypeStruct(shape, out_dtype), pl.BlockSpec(block, index)
        if buf is not None:
            aliases = {len(ins): 0}
            ins.append(buf)
            specs.append(pl.BlockSpec(memory_space=pl.ANY))
    return pl.pallas_call(
        body, name=name, grid=(M // tm, N // tn, nk),
        in_specs=specs, out_specs=out_spec, out_shape=out_shape, input_output_aliases=aliases,
        scratch_shapes=[] if nk == 1 else [pltpu.VMEM((tm, tn), F32)],
        compiler_params=_cparams(("parallel", "parallel", "arbitrary")),
    )(*ins)


def _rows_product(a_ref, b_ref, tb, a_act):
    av = a_ref[...]
    if a_act == "relu2":
        r = jnp.maximum(av, 0)
        av = r * r
    dn = (((1,), (1 if tb else 0,)), ((), ()))
    return lax.dot_general(av.astype(BF16), b_ref[...].astype(BF16), dn, preferred_element_type=F32)


def _norm_mm(x, w, b, after, *, name, tm):
    m, d = x.shape
    n = b.shape[1]
    cn = _pick(n, (MM_CHUNK, 384, 256, 128))

    def body(x_ref, w_ref, b_ref, after_ref, h_ref, r_ref, o_ref):
        xv = x_ref[...]
        rstd = lax.rsqrt(jnp.mean(xv * xv, axis=1, keepdims=True) + EPS)
        hv = (xv * rstd * w_ref[...]).astype(BF16)
        h_ref[...] = hv
        r_ref[...] = rstd
        for c in range(n // cn):
            cs = slice(c * cn, (c + 1) * cn)
            o_ref[:, cs] = jnp.dot(hv, b_ref[:, cs].astype(BF16), preferred_element_type=F32)

    row = pl.BlockSpec((tm, d), lambda i: (i, 0))
    return pl.pallas_call(
        body, name=name, grid=(m // tm,),
        in_specs=[row, pl.BlockSpec((1, d), lambda i: (0, 0)), pl.BlockSpec((d, n), lambda i: (0, 0)),
                  pl.BlockSpec(memory_space=pl.ANY)],
        out_specs=[row, pl.BlockSpec((tm, 1), lambda i: (i, 0)), pl.BlockSpec((tm, n), lambda i: (i, 0))],
        out_shape=[jax.ShapeDtypeStruct((m, d), BF16), jax.ShapeDtypeStruct((m, 1), F32),
                   jax.ShapeDtypeStruct((m, n), F32)],
        compiler_params=_cparams(("parallel",)),
    )(x, w, b, after)


def _mm_norm_fwd(a1, b1, a2, b2, res, w, *, name, tm):
    m, k1 = a1.shape
    k2 = a2.shape[1]
    d = b1.shape[1]

    def body(a1_ref, b1_ref, a2_ref, b2_ref, res_ref, w_ref, h_ref, y_ref, r_ref):
        hv = _rows_product(a1_ref, b1_ref, False, None) + _rows_product(a2_ref, b2_ref, False, None) + res_ref[...]
        rstd = lax.rsqrt(jnp.mean(hv * hv, axis=1, keepdims=True) + EPS)
        h_ref[...] = hv
        y_ref[...] = (hv * rstd * w_ref[...]).astype(BF16)
        r_ref[...] = rstd

    row = pl.BlockSpec((tm, d), lambda i: (i, 0))
    return pl.pallas_call(
        body, name=name, grid=(m // tm,),
        in_specs=[pl.BlockSpec((tm, k1), lambda i: (i, 0)), pl.BlockSpec((k1, d), lambda i: (0, 0)),
                  pl.BlockSpec((tm, k2), lambda i: (i, 0)), pl.BlockSpec((k2, d), lambda i: (0, 0)), row,
                  pl.BlockSpec((1, d), lambda i: (0, 0))],
        out_specs=[row, row, pl.BlockSpec((tm, 1), lambda i: (i, 0))],
        out_shape=[jax.ShapeDtypeStruct((m, d), F32), jax.ShapeDtypeStruct((m, d), BF16),
                   jax.ShapeDtypeStruct((m, 1), F32)],
        compiler_params=_cparams(("parallel",)),
    )(a1, b1, a2, b2, res, w)


def _mm_final(a, b, res, w, target, *, name, tm, a_act):
    m, k = a.shape
    d = b.shape[1]

    def body(a_ref, b_ref, res_ref, w_ref, t_ref, dh_ref, dhb_ref, loss_ref, dw_ref):
        @pl.when(pl.program_id(0) == 0)
        def _():
            loss_ref[...] = jnp.zeros_like(loss_ref)
            dw_ref[...] = jnp.zeros_like(dw_ref)

        hv = _rows_product(a_ref, b_ref, False, a_act) + res_ref[...]
        wv = w_ref[...]
        rstd = lax.rsqrt(jnp.mean(hv * hv, axis=1, keepdims=True) + EPS)
        xhat = hv * rstd
        err = xhat * wv - t_ref[...]
        loss_ref[...] += 0.5 * jnp.sum(jnp.mean(err * err, axis=1, keepdims=True), axis=0, keepdims=True)
        dy = err * (1.0 / d)
        gw = dy * wv
        dh = rstd * (gw - xhat * jnp.mean(gw * xhat, axis=1, keepdims=True))
        dh_ref[...] = dh
        dhb_ref[...] = dh.astype(BF16)
        dw_ref[...] += jnp.sum(dy * xhat, axis=0, keepdims=True)

    row = pl.BlockSpec((tm, d), lambda i: (i, 0))
    vec = pl.BlockSpec((1, d), lambda i: (0, 0))
    return pl.pallas_call(
        body, name=name, grid=(m // tm,),
        in_specs=[pl.BlockSpec((tm, k), lambda i: (i, 0)), pl.BlockSpec((k, d), lambda i: (0, 0)), row, vec, row],
        out_specs=[row, row, pl.BlockSpec((1, 128), lambda i: (0, 0)), vec],
        out_shape=[jax.ShapeDtypeStruct((m, d), F32), jax.ShapeDtypeStruct((m, d), BF16),
                   jax.ShapeDtypeStruct((1, 128), F32), jax.ShapeDtypeStruct((1, d), F32)],
        compiler_params=_cparams(("arbitrary",)),
    )(a, b, res, w, target)


def _mm_two_halves(a, b, *, name, tm):
    m, k = a.shape
    d = b.shape[0] // 2

    def body(a_ref, b_ref, lo_ref, hi_ref):
        av = a_ref[...].astype(BF16)
        lo_ref[...] = lax.dot_general(av, b_ref[0:d, :].astype(BF16), NT_DIMS, preferred_element_type=F32)
        hi_ref[...] = lax.dot_general(av, b_ref[d:2 * d, :].astype(BF16), NT_DIMS,
                                      preferred_element_type=F32).astype(BF16)

    row = pl.BlockSpec((tm, d), lambda i: (i, 0))
    return pl.pallas_call(
        body, name=name, grid=(m // tm,),
        in_specs=[pl.BlockSpec((tm, k), lambda i: (i, 0)), pl.BlockSpec((2 * d, k), lambda i: (0, 0))],
        out_specs=[row, row],
        out_shape=[jax.ShapeDtypeStruct((m, d), F32), jax.ShapeDtypeStruct((m, d), BF16)],
        compiler_params=_cparams(("parallel",)),
    )(a, b)


def _mm_norm_bwd(pairs, x, rstd, w, dres, *, name, tm, after=None):
    m = pairs[0][0].shape[0]
    d = pairs[0][1].shape[0]
    n_pairs = len(pairs)

    def body(*refs):
        i = 2 * n_pairs
        x_ref, r_ref, w_ref, d_ref = refs[i:i + 4]
        dx_ref, dxb_ref, dw_ref = refs[-3:]

        @pl.when(pl.program_id(0) == 0)
        def _():
            dw_ref[...] = jnp.zeros_like(dw_ref)

        g = _rows_product(refs[0], refs[1], True, None)
        for p in range(1, n_pairs):
            g = g + _rows_product(refs[2 * p], refs[2 * p + 1], True, None)
        r = r_ref[...]
        xhat = x_ref[...] * r
        gw = g * w_ref[...]
        dx = d_ref[...] + r * (gw - xhat * jnp.mean(gw * xhat, axis=1, keepdims=True))
        dx_ref[...] = dx
        dxb_ref[...] = dx.astype(BF16)
        dw_ref[...] += jnp.sum(g * xhat, axis=0, keepdims=True)

    row = pl.BlockSpec((tm, d), lambda i: (i, 0))
    vec = pl.BlockSpec((1, d), lambda i: (0, 0))
    ins, specs = [], []
    for a, b in pairs:
        k = a.shape[1]
        ins += [a, b]
        specs += [pl.BlockSpec((tm, k), lambda i: (i, 0)), pl.BlockSpec((d, k), lambda i: (0, 0))]
    ins += [x, rstd, w, dres]
    specs += [row, pl.BlockSpec((tm, 1), lambda i: (i, 0)), vec, row]
    if after is not None:
        ins.append(after)
        specs.append(pl.BlockSpec(memory_space=pl.ANY))
    return pl.pallas_call(
        body, name=name, grid=(m // tm,), in_specs=specs, out_specs=[row, row, vec],
        out_shape=[jax.ShapeDtypeStruct((m, d), F32), jax.ShapeDtypeStruct((m, d), BF16),
                   jax.ShapeDtypeStruct((1, d), F32)],
        compiler_params=_cparams(("arbitrary",)),
    )(*ins)


def _softplus(x):
    return jnp.maximum(x, 0.0) + jnp.log(1.0 + jnp.exp(-jnp.abs(x)))


def _prep(proj_a, bias128, alog128, bl, t):
    n = bl * t
    nch = t // CHUNK
    col0 = (SSD_WIDTH + CONV_CH) // 128

    def body(p_ref, b_ref, al_ref, dt_ref, gd_ref, ac_ref, act_ref, negc_ref):
        negc_ref[...] = jnp.zeros_like(negc_ref)
        row = lax.broadcasted_iota(jnp.int32, (CHUNK, CHUNK), 0)
        col = lax.broadcasted_iota(jnp.int32, (CHUNK, CHUNK), 1)
        tril = (row >= col).astype(F32)
        lane = lax.broadcasted_iota(jnp.int32, (1, 128), 1)
        head_lanes = lane < 16
        a_row = -jnp.exp(al_ref[...])
        carry = jnp.zeros((1, 128), F32)
        for ci in range(nch):
            rows = slice(ci * CHUNK, (ci + 1) * CHUNK)
            xv = p_ref[rows, :] + b_ref[...]
            sp = _softplus(xv)
            acum = jnp.dot(tril, a_row * sp, precision=HIGHEST, preferred_element_type=F32)
            c = jnp.dot(tril, -_softplus(-xv), precision=HIGHEST, preferred_element_type=F32) + carry
            carry = c[CHUNK - 1:CHUNK, :]
            dt_ref[rows, :] = jnp.where(head_lanes, sp, 0.0)
            gd_ref[rows, :] = jnp.where(head_lanes, jax.nn.sigmoid(xv),
                                        jnp.where(lane < 32, jax.nn.sigmoid(-xv), 0.0))
            ac_ref[rows, :] = jnp.where(head_lanes, acum, 0.0)
            act_ref[:, rows] = jnp.transpose(acum)[0:16, :]
            c_t = jnp.transpose(c)
            for hp in range(8):
                negc_ref[hp, 0:2, rows] = -c_t[16 + 2 * hp:18 + 2 * hp, :]

    o128 = pl.BlockSpec((t, 128), lambda b: (b, 0))
    v128 = pl.BlockSpec((1, 128), lambda b: (0, 0))
    w128 = jax.ShapeDtypeStruct((n, 128), F32)
    return pl.pallas_call(
        body, name="head_scalars", grid=(bl,),
        in_specs=[pl.BlockSpec((t, 128), lambda b: (b, col0)), v128, v128],
        out_specs=[o128, o128, o128, pl.BlockSpec((16, t), lambda b: (0, b)),
                   pl.BlockSpec((None, 8, 8, t), lambda b: (b, 0, 0, 0))],
        out_shape=[w128, w128, w128, jax.ShapeDtypeStruct((16, n), F32),
                   jax.ShapeDtypeStruct((bl, 8, 8, t), F32)],
        compiler_params=_cparams(("parallel",)),
    )(proj_a, bias128, alog128)


def _fpost(dc, gate_d, ddt, dpa, bl, t):
    n = bl * t
    nch = t // CHUNK
    col0 = (SSD_WIDTH + CONV_CH) // 128

    def body(dc_ref, gd_ref, ddt_ref, dpa_in, out_ref, db_ref):
        @pl.when(pl.program_id(0) == 0)
        def _():
            db_ref[...] = jnp.zeros_like(db_ref)

        row = lax.broadcasted_iota(jnp.int32, (CHUNK, CHUNK), 0)
        col = lax.broadcasted_iota(jnp.int32, (CHUNK, CHUNK), 1)
        triu = (row <= col).astype(F32)
        lane = lax.broadcasted_iota(jnp.int32, (1, 128), 1)
        gate_lanes = (lane >= 16) & (lane < 32)
        carry = jnp.zeros((1, 128), F32)
        db = jnp.zeros((1, 128), F32)
        for ci in reversed(range(nch)):
            rows = slice(ci * CHUNK, (ci + 1) * CHUNK)
            dlf = jnp.dot(triu, dc_ref[rows, :], precision=HIGHEST, preferred_element_type=F32) + carry
            carry = dlf[0:1, :]
            df = jnp.where(gate_lanes, dlf * gd_ref[rows, :], 0.0)
            out_ref[rows, :] = (ddt_ref[rows, :] + df).astype(BF16)
            db = db + jnp.sum(df, axis=0, keepdims=True)
        db_ref[...] += db[:, 16:32]

    blk = pl.BlockSpec((t, 128), lambda b: (b, 0))
    return pl.pallas_call(
        body, name="forget_gate_bwd", grid=(bl,),
        in_specs=[blk, blk, blk, ANY],
        out_specs=[pl.BlockSpec((t, 128), lambda b: (b, col0)), pl.BlockSpec((1, 16), lambda b: (0, 0))],
        out_shape=[jax.ShapeDtypeStruct(dpa.shape, dpa.dtype), jax.ShapeDtypeStruct((1, 16), F32)],
        input_output_aliases={3: 0},
        compiler_params=_cparams(("arbitrary",)),
    )(dc, gate_d, ddt, dpa)


CONV_TILE = 256
CONV_ROWS = 256


def _conv_taps(u_ref, i):
    r0 = pl.multiple_of(i * CONV_ROWS, CONV_ROWS)
    cur = u_ref[pl.ds(r0, CONV_ROWS), :]
    p0 = pl.multiple_of(jnp.maximum(r0 - 8, 0), 8)
    prev = jnp.where(i > 0, u_ref[pl.ds(p0, 8), :], 0.0)
    cat = jnp.concatenate([prev, cur], axis=0)
    return r0, [cur] + [pltpu.roll(cat, s, 0)[8:, :] for s in (1, 2, 3)]


def _conv_fwd(proj_a, conv_w, conv_b, bl, t):
    n = bl * t
    nct = CONV_CH // CONV_TILE
    c0 = SSD_WIDTH // CONV_TILE

    def body(u_ref, w_ref, b_ref, o_ref, d_ref):
        w = w_ref[...]
        bias = b_ref[...]

        def chunk(i, carry):
            r0, taps = _conv_taps(u_ref, i)
            pre = bias + w[3:4, :] * taps[0]
            for s in (1, 2, 3):
                pre = pre + w[3 - s:4 - s, :] * taps[s]
            sg = jax.nn.sigmoid(pre)
            o_ref[pl.ds(r0, CONV_ROWS), :] = pre * sg
            d_ref[pl.ds(r0, CONV_ROWS), :] = (sg * (1.0 + pre * (1.0 - sg))).astype(BF16)
            return carry

        lax.fori_loop(0, t // CONV_ROWS, chunk, 0)

    out = pl.BlockSpec((t, CONV_TILE), lambda b, c: (b, c))
    return pl.pallas_call(
        body, name="conv_silu_fwd", grid=(bl, nct),
        in_specs=[pl.BlockSpec((t, CONV_TILE), lambda b, c: (b, c0 + c)),
                  pl.BlockSpec((4, CONV_TILE), lambda b, c: (0, c)),
                  pl.BlockSpec((1, CONV_TILE), lambda b, c: (0, c))],
        out_specs=[out, out],
        out_shape=[jax.ShapeDtypeStruct((n, CONV_CH), F32), jax.ShapeDtypeStruct((n, CONV_CH), BF16)],
        compiler_params=_cparams(("parallel", "parallel")),
    )(proj_a, conv_w, conv_b)


def _conv_bwd(dxc, dsilu, proj_a, conv_w, dpa, bl, t):
    nct = CONV_CH // CONV_TILE
    c0 = SSD_WIDTH // CONV_TILE
    nrc = t // CONV_ROWS

    def body(g_ref, s_ref, u_ref, w_ref, dpa_in, du_ref, dw_ref, db_ref, dp_scr):
        @pl.when(pl.program_id(1) == 0)
        def _():
            dw_ref[...] = jnp.zeros_like(dw_ref)
            db_ref[...] = jnp.zeros_like(db_ref)

        w = w_ref[...]
        dp_scr[pl.ds(t, 8), :] = jnp.zeros((8, CONV_TILE), F32)

        def chunk1(i, carry):
            dw0, dw1, dw2, dw3, db = carry
            r0, taps = _conv_taps(u_ref, i)
            dpre = g_ref[pl.ds(r0, CONV_ROWS), :] * s_ref[pl.ds(r0, CONV_ROWS), :].astype(F32)
            dp_scr[pl.ds(r0, CONV_ROWS), :] = dpre
            dw3 = dw3 + jnp.sum(dpre * taps[0], axis=0, keepdims=True)
            dw2 = dw2 + jnp.sum(dpre * taps[1], axis=0, keepdims=True)
            dw1 = dw1 + jnp.sum(dpre * taps[2], axis=0, keepdims=True)
            dw0 = dw0 + jnp.sum(dpre * taps[3], axis=0, keepdims=True)
            db = db + jnp.sum(dpre, axis=0, keepdims=True)
            return dw0, dw1, dw2, dw3, db

        z = jnp.zeros((1, CONV_TILE), F32)
        dw0, dw1, dw2, dw3, db = lax.fori_loop(0, nrc, chunk1, (z, z, z, z, z))
        dw_ref[...] += jnp.concatenate([dw0, dw1, dw2, dw3], axis=0)
        db_ref[...] += db

        def chunk2(i, carry):
            r0 = pl.multiple_of(i * CONV_ROWS, CONV_ROWS)
            cat = dp_scr[pl.ds(r0, CONV_ROWS + 8), :]
            du = w[3:4, :] * cat[:CONV_ROWS, :]
            for s in (1, 2, 3):
                du = du + w[3 - s:4 - s, :] * pltpu.roll(cat, CONV_ROWS + 8 - s, 0)[:CONV_ROWS, :]
            du_ref[pl.ds(r0, CONV_ROWS), :] = du.astype(BF16)
            return carry

        lax.fori_loop(0, nrc, chunk2, 0)

    tile = pl.BlockSpec((t, CONV_TILE), lambda c, b: (b, c))
    return pl.pallas_call(
        body, name="conv_silu_bwd", grid=(nct, bl),
        in_specs=[tile, tile, pl.BlockSpec((t, CONV_TILE), lambda c, b: (b, c0 + c)),
                  pl.BlockSpec((4, CONV_TILE), lambda c, b: (0, c)), ANY],
        out_specs=[pl.BlockSpec((t, CONV_TILE), lambda c, b: (b, c0 + c)),
                   pl.BlockSpec((4, CONV_TILE), lambda c, b: (0, c)),
                   pl.BlockSpec((1, CONV_TILE), lambda c, b: (0, c))],
        out_shape=[jax.ShapeDtypeStruct(dpa.shape, dpa.dtype), jax.ShapeDtypeStruct((4, CONV_CH), F32),
                   jax.ShapeDtypeStruct((1, CONV_CH), F32)],
        input_output_aliases={4: 0},
        scratch_shapes=[pltpu.VMEM((t + 8, CONV_TILE), F32)],
        compiler_params=_cparams(("parallel", "arbitrary")),
    )(dxc, dsilu, proj_a, conv_w, dpa)


SSD_FWD_CHUNKS = 4
SSD_BWD_CHUNKS = 1
NT_DIMS = (((1,), (1,)), ((), ()))
TN_DIMS = (((0,), (0,)), ((), ()))


def _dot(a, b, dims=None):
    if dims is None:
        return jnp.dot(a, b, preferred_element_type=F32)
    return lax.dot_general(a, b, dims, preferred_element_type=F32)


def _head_expander():
    r = lax.broadcasted_iota(jnp.int32, (128, SSD_WIDTH), 0)
    c = lax.broadcasted_iota(jnp.int32, (128, SSD_WIDTH), 1)
    return ((c // HEAD_DIM == r % 16) & (r < 48)).astype(BF16)


def _spread(v128, expander):
    hi = v128.astype(BF16).astype(F32)
    r1 = v128 - hi
    mid = r1.astype(BF16).astype(F32)
    lo = (r1 - mid).astype(BF16).astype(F32)
    packed = (hi + pltpu.roll(mid, 16, 1) + pltpu.roll(lo, 32, 1)).astype(BF16)
    return jnp.dot(packed, expander, preferred_element_type=F32)


def _head_sums(v1024, expander):
    hi = v1024.astype(BF16)
    lo = (v1024 - hi.astype(F32)).astype(BF16)
    heads = jnp.where(lax.broadcasted_iota(jnp.int32, expander.shape, 0) < 16, expander, jnp.zeros_like(expander))
    return _dot(hi, heads, NT_DIMS) + _dot(lo, heads, NT_DIMS)


def _ssd_fwd(xc, proj_a, dt, acum, acum_t, dskip_e, norm_w, bl, t):
    n = bl * t
    nch = t // CHUNK
    L = CHUNK

    def body(xc_blk, z_blk, dt_blk, ac_blk, act_blk, dsk_ref, nw_ref, ys_blk, yp_blk, hp_blk, h_scr, y_scr, x_scr):
        @pl.when(pl.program_id(1) == 0)
        def _():
            h_scr[...] = jnp.zeros_like(h_scr)

        for sub in range(SSD_FWD_CHUNKS):
            rows = pl.ds(sub * L, L)
            chunk(xc_blk.at[rows, :], z_blk.at[rows, :], dt_blk.at[rows, :], ac_blk.at[rows, :], act_blk.at[:, rows],
                  dsk_ref, nw_ref, ys_blk.at[rows, :], yp_blk.at[rows, :], hp_blk.at[sub], h_scr, y_scr, x_scr)

    def chunk(xc_ref, z_ref, dt_ref, ac_ref, act_ref, dsk_ref, nw_ref, ys_ref, yp_ref, hp_ref, h_scr, y_scr, x_scr):
        row = lax.broadcasted_iota(jnp.int32, (L, L), 0)
        col = lax.broadcasted_iota(jnp.int32, (L, L), 1)
        causal = row >= col
        lane128 = lax.broadcasted_iota(jnp.int32, (1, L), 1)
        expander = _head_expander()
        ac_all = ac_ref[...]
        act_all = act_ref[...]
        ac_e = _spread(ac_all, expander)
        e_in = jnp.exp(ac_e)
        dec = jnp.exp(ac_e[L - 1:L, :] - ac_e)
        xs_all = xc_ref[:, 0:SSD_WIDTH]
        x_all = xs_all * _spread(dt_ref[...], expander)
        x_scr[...] = x_all.astype(BF16)
        hp_all = h_scr[...]
        hp_ref[...] = hp_all
        for g in range(2):
            gs = slice(g * 512, (g + 1) * 512)
            bg = xc_ref[:, SSD_WIDTH + g * 128:SSD_WIDTH + (g + 1) * 128].astype(BF16)
            cg = xc_ref[:, SSD_WIDTH + 256 + g * 128:SSD_WIDTH + 256 + (g + 1) * 128].astype(BF16)
            gmat = _dot(cg, bg, NT_DIMS)
            y_off = _dot(cg, hp_all[gs, :].astype(BF16), NT_DIMS) * e_in[:, gs] + dsk_ref[:, gs] * xs_all[:, gs]
            s_new = _dot((x_all[:, gs] * dec[:, gs]).astype(BF16), bg, TN_DIMS)
            for pr in range(4):
                pair = slice((g * 4 + pr) * 128, (g * 4 + pr + 1) * 128)
                x_pair = x_scr[:, pair]
                y_pair = y_off[:, pr * 128:(pr + 1) * 128]
                for j in range(2):
                    h = g * 8 + 2 * pr + j
                    sl = slice(h * HEAD_DIM, (h + 1) * HEAD_DIM)
                    r = 2 * pr + j
                    ldec = jnp.exp(jnp.where(causal, ac_all[:, h:h + 1] - act_all[h:h + 1, :], NEG))
                    x_head = jnp.where((lane128 < HEAD_DIM) == (j == 0), x_pair, jnp.zeros_like(x_pair))
                    y_pair = y_pair + _dot((gmat * ldec).astype(BF16), x_head)
                    elast = jnp.exp(ac_all[L - 1:L, h:h + 1])
                    h_scr[sl, :] = elast * hp_all[sl, :] + s_new[r * HEAD_DIM:(r + 1) * HEAD_DIM, :]
                y_scr[:, pair] = y_pair
        y = y_scr[...]
        yp_ref[...] = y
        zv = z_ref[...]
        yg = y * (zv * jax.nn.sigmoid(zv))
        for g in range(2):
            gs = slice(g * 512, (g + 1) * 512)
            grp = yg[:, gs]
            rstd = lax.rsqrt(jnp.mean(grp * grp, axis=1, keepdims=True) + EPS)
            ys_ref[:, gs] = (grp * rstd * nw_ref[:, gs]).astype(BF16)

    cps = SSD_FWD_CHUNKS
    steps = nch // cps
    rb = lambda b, c: (b * steps + c, 0)
    v1k = pl.BlockSpec((1, SSD_WIDTH), lambda b, c: (0, 0))
    return pl.pallas_call(
        body, name="ssd_fwd", grid=(bl, steps),
        in_specs=[pl.BlockSpec((cps * L, CONV_CH), rb), pl.BlockSpec((cps * L, SSD_WIDTH), rb),
                  pl.BlockSpec((cps * L, 128), rb), pl.BlockSpec((cps * L, 128), rb),
                  pl.BlockSpec((16, cps * L), lambda b, c: (0, b * steps + c)), v1k, v1k],
        out_specs=[pl.BlockSpec((cps * L, SSD_WIDTH), rb), pl.BlockSpec((cps * L, SSD_WIDTH), rb),
                   pl.BlockSpec((cps, SSD_WIDTH, SSD_STATE), lambda b, c: (b * steps + c, 0, 0))],
        out_shape=[jax.ShapeDtypeStruct((n, SSD_WIDTH), BF16), jax.ShapeDtypeStruct((n, SSD_WIDTH), F32),
                   jax.ShapeDtypeStruct((bl * nch, SSD_WIDTH, SSD_STATE), F32)],
        scratch_shapes=[pltpu.VMEM((SSD_WIDTH, SSD_STATE), F32), pltpu.VMEM((L, SSD_WIDTH), F32),
                        pltpu.VMEM((L, SSD_WIDTH), BF16)],
        compiler_params=_cparams(("parallel", "arbitrary")),
    )(xc, proj_a, dt, acum, acum_t, dskip_e, norm_w)


def _ssd_bwd(dys, xc, proj_a, ypre, hprev, dt, gate_d, acum, acum_t, alog128, dskip_e, norm_w, bl, t):
    n = bl * t
    nch = t // CHUNK
    L = CHUNK

    def body(dys_blk, xc_blk, z_blk, yp_blk, hp_blk, dt_blk, gd_blk, ac_blk, act_blk, al_ref, dsk_ref, nw_ref,
             dxc_blk, dz_blk, ddt_blk, dnw_ref, dsk16_ref, da16_ref, db16_ref,
             dh_scr, dy_scr, x_scr, dx_scr, red_scr):
        first = (pl.program_id(0) == 0) & (pl.program_id(1) == 0)

        @pl.when(first)
        def _():
            dnw_ref[...] = jnp.zeros_like(dnw_ref)
            dsk16_ref[...] = jnp.zeros_like(dsk16_ref)
            da16_ref[...] = jnp.zeros_like(da16_ref)
            db16_ref[...] = jnp.zeros_like(db16_ref)

        @pl.when(pl.program_id(1) == 0)
        def _():
            dh_scr[...] = jnp.zeros_like(dh_scr)

        for sub in reversed(range(SSD_BWD_CHUNKS)):
            rows = pl.ds(sub * L, L)
            chunk(dys_blk.at[rows, :], xc_blk.at[rows, :], z_blk.at[rows, :], yp_blk.at[rows, :], hp_blk.at[sub],
                  dt_blk.at[rows, :], gd_blk.at[rows, :], ac_blk.at[rows, :], act_blk.at[:, rows], al_ref, dsk_ref,
                  nw_ref, dxc_blk.at[rows, :], dz_blk.at[rows, :], ddt_blk.at[rows, :], dnw_ref, dsk16_ref, da16_ref,
                  db16_ref, dh_scr, dy_scr, x_scr, dx_scr, red_scr)

    def chunk(dys_ref, xc_ref, z_ref, yp_ref, hp_ref, dt_ref, gd_ref, ac_ref, act_ref, al_ref, dsk_ref, nw_ref,
              dxc_ref, dz_ref, ddt_ref, dnw_ref, dsk16_ref, da16_ref, db16_ref,
              dh_scr, dy_scr, x_scr, dx_scr, red_scr):
        y = yp_ref[...]
        zv = z_ref[...]
        sz = jax.nn.sigmoid(zv)
        gate = zv * sz
        yg = y * gate
        dout = dys_ref[...]
        nw = nw_ref[...]
        for g in range(2):
            gs = slice(g * 512, (g + 1) * 512)
            grp = yg[:, gs]
            rstd = lax.rsqrt(jnp.mean(grp * grp, axis=1, keepdims=True) + EPS)
            ghat = grp * rstd
            dnw_ref[:, gs] += jnp.sum(dout[:, gs] * ghat, axis=0, keepdims=True)
            gw = dout[:, gs] * nw[:, gs]
            dyg = rstd * (gw - ghat * jnp.mean(gw * ghat, axis=1, keepdims=True))
            dy_scr[:, gs] = dyg * gate[:, gs]
            dz_ref[:, gs] = (dyg * y[:, gs] * (sz[:, gs] * (1.0 + zv[:, gs] * (1.0 - sz[:, gs])))).astype(BF16)

        row = lax.broadcasted_iota(jnp.int32, (L, L), 0)
        col = lax.broadcasted_iota(jnp.int32, (L, L), 1)
        causal = row >= col
        lane128 = lax.broadcasted_iota(jnp.int32, (1, L), 1)
        rows128 = lax.broadcasted_iota(jnp.int32, (L, 1), 0)
        last_row = rows128 == (L - 1)
        expander = _head_expander()
        ac_all = ac_ref[...]
        act_all = act_ref[...]
        dt_all = dt_ref[...]
        dt_e = _spread(dt_all, expander)
        ac_e = _spread(ac_all, expander)
        e_in = jnp.exp(ac_e)
        dec = jnp.exp(ac_e[L - 1:L, :] - ac_e)
        xs_all = xc_ref[:, 0:SSD_WIDTH]
        x_all = xs_all * dt_e
        x_scr[...] = x_all.astype(BF16)
        dy_all = dy_scr[...]
        hp_all = hp_ref[...]
        ds_all = dh_scr[...]
        dsk_cols = jnp.sum(dy_all * xs_all, axis=0, keepdims=True)
        dac = jnp.zeros((L, L), F32)
        dac_row = jnp.zeros((L, L), F32)
        ddec_cols = []
        for g in range(2):
            gs = slice(g * 512, (g + 1) * 512)
            bsl = slice(SSD_WIDTH + g * 128, SSD_WIDTH + (g + 1) * 128)
            csl = slice(SSD_WIDTH + 256 + g * 128, SSD_WIDTH + 256 + (g + 1) * 128)
            bg = xc_ref[:, bsl].astype(BF16)
            cg = xc_ref[:, csl].astype(BF16)
            gmat = _dot(cg, bg, NT_DIMS)
            hpb = hp_all[gs, :].astype(BF16)
            dsb = ds_all[gs, :].astype(BF16)
            ch = _dot(cg, hpb, NT_DIMS)
            dye = dy_all[:, gs] * e_in[:, gs]
            dyeb = dye.astype(BF16)
            dc_acc = _dot(dyeb, hpb)
            dhp = _dot(dyeb, cg, TN_DIMS)
            dxd = _dot(bg, dsb, NT_DIMS)
            db_acc = _dot((x_all[:, gs] * dec[:, gs]).astype(BF16), dsb)
            ddec = dxd * x_all[:, gs] * dec[:, gs]
            ddec_cols.append(jnp.sum(ddec, axis=0, keepdims=True))
            dx_inter = dxd * dec[:, gs]
            red_scr[:, gs] = dye * ch - ddec
            dg_sum = jnp.zeros((L, L), F32)
            for pr in range(4):
                pair = slice((g * 4 + pr) * 128, (g * 4 + pr + 1) * 128)
                x_pair = x_scr[:, pair]
                dy_pair = dy_scr[:, pair].astype(BF16)
                dx_pair = dx_inter[:, pr * 128:(pr + 1) * 128]
                for j in range(2):
                    h = g * 8 + 2 * pr + j
                    r = 2 * pr + j
                    sl = slice(h * HEAD_DIM, (h + 1) * HEAD_DIM)
                    onehot_w = lane128 == h
                    ldec = jnp.exp(jnp.where(causal, ac_all[:, h:h + 1] - act_all[h:h + 1, :], NEG))
                    mf = gmat * ldec
                    dyb = jnp.where((lane128 < HEAD_DIM) == (j == 0), dy_pair, jnp.zeros_like(dy_pair))
                    dm = _dot(dyb, x_pair, NT_DIMS)
                    dx_pair = dx_pair + _dot(mf.astype(BF16), dyb, TN_DIMS)
                    dg_sum = dg_sum + dm * ldec
                    wmat = dm * mf
                    elast = jnp.exp(ac_all[L - 1:L, h:h + 1])
                    hp_h = hp_all[sl, :]
                    ds_h = ds_all[sl, :]
                    extra = elast * jnp.sum(jnp.sum(hp_h * ds_h, axis=1, keepdims=True), axis=0, keepdims=True)
                    dac = dac + jnp.where(onehot_w,
                                          jnp.sum(wmat, axis=1, keepdims=True) + jnp.where(last_row, extra, 0.0), 0.0)
                    dac_row = dac_row + jnp.where(rows128 == h, -jnp.sum(wmat, axis=0, keepdims=True), 0.0)
                    dh_scr[sl, :] = elast * ds_h + dhp[r * HEAD_DIM:(r + 1) * HEAD_DIM, :]
                dx_scr[:, pair] = dx_pair
            dgb = dg_sum.astype(BF16)
            dxc_ref[:, csl] = dc_acc + _dot(dgb, bg)
            dxc_ref[:, bsl] = db_acc + _dot(dgb, cg, TN_DIMS)
        dx_all = dx_scr[...]
        dxc_ref[:, 0:SSD_WIDTH] = dx_all * dt_e + dsk_ref[...] * dy_all
        red = red_scr[...]
        dac_slab = _head_sums(red, expander)
        ddec_tot = _head_sums(jnp.broadcast_to(jnp.concatenate(ddec_cols, axis=1), (8, SSD_WIDTH)), expander)
        ddt_x = _head_sums(dx_all * xs_all, expander)
        dsk16_ref[...] += _head_sums(jnp.broadcast_to(dsk_cols, (8, SSD_WIDTH)), expander)[0:1, 0:16]
        dac = dac + dac_slab + jnp.transpose(dac_row) + jnp.where(last_row, ddec_tot[0:1, :], 0.0)
        triu = (row <= col).astype(F32)
        da = jnp.dot(triu, dac, precision=HIGHEST, preferred_element_type=F32)
        a_row = -jnp.exp(al_ref[...])
        ddt = jnp.where(lane128 < 16, (ddt_x + da * a_row) * gd_ref[...], 0.0)
        ddt_ref[...] = ddt
        da16_ref[...] += (jnp.sum(da * dt_all, axis=0, keepdims=True) * a_row)[:, 0:16]
        db16_ref[...] += jnp.sum(ddt, axis=0, keepdims=True)[:, 0:16]

    cps = SSD_BWD_CHUNKS
    steps = nch // cps
    rb = lambda b, c: (b * steps + steps - 1 - c, 0)
    v1k = pl.BlockSpec((1, SSD_WIDTH), lambda b, c: (0, 0))
    v16 = pl.BlockSpec((1, 16), lambda b, c: (0, 0))
    v128 = pl.BlockSpec((1, 128), lambda b, c: (0, 0))
    wide = pl.BlockSpec((cps * L, SSD_WIDTH), rb)
    s128 = pl.BlockSpec((cps * L, 128), rb)
    return pl.pallas_call(
        body, name="ssd_bwd", grid=(bl, steps),
        in_specs=[wide, pl.BlockSpec((cps * L, CONV_CH), rb), wide, wide,
                  pl.BlockSpec((cps, SSD_WIDTH, SSD_STATE), lambda b, c: (b * steps + steps - 1 - c, 0, 0)),
                  s128, s128, s128, pl.BlockSpec((16, cps * L), lambda b, c: (0, b * steps + steps - 1 - c)),
                  v128, v1k, v1k],
        out_specs=[pl.BlockSpec((cps * L, CONV_CH), rb), wide, s128, v1k, v16, v16, v16],
        out_shape=[jax.ShapeDtypeStruct((n, CONV_CH), F32), jax.ShapeDtypeStruct((n, PA_WIDTH), BF16),
                   jax.ShapeDtypeStruct((n, 128), F32), jax.ShapeDtypeStruct((1, SSD_WIDTH), F32),
                   jax.ShapeDtypeStruct((1, 16), F32), jax.ShapeDtypeStruct((1, 16), F32),
                   jax.ShapeDtypeStruct((1, 16), F32)],
        scratch_shapes=[pltpu.VMEM((SSD_WIDTH, SSD_STATE), F32), pltpu.VMEM((L, SSD_WIDTH), F32),
                        pltpu.VMEM((L, SSD_WIDTH), BF16), pltpu.VMEM((L, SSD_WIDTH), F32),
                        pltpu.VMEM((L, SSD_WIDTH), F32)],
        compiler_params=_cparams(("arbitrary", "arbitrary")),
    )(dys, xc, proj_a, ypre, hprev, dt, gate_d, acum, acum_t, alog128, dskip_e, norm_w)


def _attn_fwd(qkv, negc, bl, t):
    n = bl * t
    tb_ = min(t, ATT_FWD_BLOCK)
    nb = t // tb_
    scale2 = LOG2E / math.sqrt(HEAD_DIM)

    def body(q_ref, k_ref, v_ref, c_ref, o_ref, lse_ref, v0_scr, v1_scr, k0_scr, k1_scr):
        row = lax.broadcasted_iota(jnp.int32, (tb_, tb_), 0)
        col = lax.broadcasted_iota(jnp.int32, (tb_, tb_), 1)
        causal = row >= col
        lane = lax.broadcasted_iota(jnp.int32, (1, 128), 1)
        v_pair = v_ref[...].astype(F32)
        k_pair = k_ref[...]
        v_scrs = (v0_scr, v1_scr)
        k_scrs = (k0_scr, k1_scr)
        for j in range(2):
            v_head = v_pair if j == 0 else pltpu.roll(v_pair, HEAD_DIM, 1)
            v_scrs[j][...] = jnp.where(lane < HEAD_DIM, v_head, jnp.where(lane == HEAD_DIM, 1.0, 0.0)).astype(BF16)
            k_scrs[j][...] = jnp.where((lane < HEAD_DIM) == (j == 0), k_pair, jnp.zeros_like(k_pair))
        for qi in range(nb):
            r0, lk = qi * tb_, (qi + 1) * tb_
            for j in range(2):
                sl = slice(j * HEAD_DIM, (j + 1) * HEAD_DIM)
                s = _dot(q_ref[r0:lk, :], k_scrs[j][0:lk, :], NT_DIMS) * scale2 + c_ref[j:j + 1, 0:lk] * LOG2E
                tail = jnp.where(causal, s[:, r0:lk], NEG)
                s = tail if qi == 0 else jnp.concatenate([s[:, 0:r0], tail], axis=1)
                m = jnp.max(s, axis=1, keepdims=True)
                p = jnp.exp2(s - m)
                acc = _dot(p.astype(BF16), v_scrs[j][0:lk, :])
                l = acc[:, HEAD_DIM:HEAD_DIM + 1]
                o_ref[r0:lk, sl] = (acc[:, 0:HEAD_DIM] / l).astype(BF16)
                lse_ref[r0:lk, sl] = jnp.broadcast_to(m + jnp.log(l) * LOG2E, (tb_, HEAD_DIM))

    blk = lambda off: pl.BlockSpec((t, 128), lambda b, hp: (b, off + hp))
    return pl.pallas_call(
        body, name="fox_attn_fwd", grid=(bl, 8),
        in_specs=[blk(0), blk(8), blk(16), pl.BlockSpec((None, None, 8, t), lambda b, hp: (b, hp, 0, 0))],
        out_specs=[blk(0), blk(0)],
        out_shape=[jax.ShapeDtypeStruct((n, ATT_WIDTH), BF16), jax.ShapeDtypeStruct((n, ATT_WIDTH), F32)],
        scratch_shapes=[pltpu.VMEM((t, 128), BF16)] * 4,
        compiler_params=_cparams(("parallel", "parallel")),
    )(qkv, qkv, qkv, negc)


def _attn_bwd(qkv, do, o, lse, negc, after, bl, t):
    n = bl * t
    tb_ = min(t, ATT_BWD_BLOCK)
    nb = t // tb_
    scale = 1.0 / math.sqrt(HEAD_DIM)
    scale2 = LOG2E * scale

    def body(q_ref, k_ref, v_ref, do_ref, o_ref, lse_ref, c_ref, after_ref, dq_ref, dk_ref, dv_ref, dc_ref,
             dq0_scr, delta_scr, dq1_scr, qt0_scr, qt1_scr, dot_scr, dkt0_scr, dkt1_scr, dvt_scr):
        row = lax.broadcasted_iota(jnp.int32, (tb_, tb_), 0)
        col = lax.broadcasted_iota(jnp.int32, (tb_, tb_), 1)
        causal = row >= col
        lane = lax.broadcasted_iota(jnp.int32, (1, 128), 1)
        dq_scrs = (dq0_scr, dq1_scr)
        qt_scrs = (qt0_scr, qt1_scr)
        dkt_scrs = (dkt0_scr, dkt1_scr)
        dq0_scr[...] = jnp.zeros_like(dq0_scr)
        dq1_scr[...] = jnp.zeros_like(dq1_scr)
        dc_ref[...] = jnp.zeros_like(dc_ref)
        q_t = jnp.transpose(q_ref[...].astype(F32))
        ones_row = jnp.where(lax.broadcasted_iota(jnp.int32, (8, t), 0) == 0, 1.0, 0.0)
        for j in range(2):
            qt_scrs[j][...] = jnp.concatenate(
                [q_t[j * HEAD_DIM:(j + 1) * HEAD_DIM, :], ones_row, jnp.zeros((HEAD_DIM - 8, t), F32)],
                axis=0).astype(BF16)
        dot_scr[...] = jnp.transpose(do_ref[...].astype(F32)).astype(BF16)
        prod = do_ref[...].astype(F32) * o_ref[...].astype(F32)
        for j in range(2):
            sl = slice(j * HEAD_DIM, (j + 1) * HEAD_DIM)
            delta_scr[:, sl] = jnp.broadcast_to(jnp.sum(prod[:, sl], axis=1, keepdims=True), (t, HEAD_DIM))
        for kj in range(nb):
            r0, r1 = kj * tb_, (kj + 1) * tb_
            k_blk = k_ref[r0:r1, :]
            v_blk = v_ref[r0:r1, :]
            k_pair = k_blk.astype(F32)
            for j in range(2):
                sl = slice(j * HEAD_DIM, (j + 1) * HEAD_DIM)
                one = slice(j * HEAD_DIM, j * HEAD_DIM + 1)
                own = (lane < HEAD_DIM) == (j == 0)
                k_head = k_pair if j == 0 else pltpu.roll(k_pair, HEAD_DIM, 1)
                k_ones = jnp.where(lane < HEAD_DIM, k_head, jnp.where(lane == HEAD_DIM, 1.0, 0.0)).astype(BF16)
                s = (_dot(q_ref[r0:t, :], jnp.where(own, k_blk, jnp.zeros_like(k_blk)), NT_DIMS) * scale2
                     + c_ref[j:j + 1, r0:r1] * LOG2E)
                head = jnp.where(causal, s[0:tb_, :], NEG)
                s = head if kj == nb - 1 else jnp.concatenate([head, s[tb_:, :]], axis=0)
                p = jnp.exp2(s - lse_ref[r0:t, one])
                dp = _dot(do_ref[r0:t, :], jnp.where(own, v_blk, jnp.zeros_like(v_blk)), NT_DIMS)
                ds = p * (dp - delta_scr[r0:t, one])
                dsb = ds.astype(BF16)
                dvt_scr[sl, r0:r1] = _dot(dot_scr[sl, r0:t], p.astype(BF16))
                dkt_scrs[j][:, r0:r1] = _dot(qt_scrs[j][:, r0:t], dsb)
                dq_scrs[j][r0:t, :] += _dot(dsb, k_ones)
        dv_ref[...] = jnp.transpose(dvt_scr[...]).astype(BF16)
        for j in range(2):
            sl = slice(j * HEAD_DIM, (j + 1) * HEAD_DIM)
            acc = dq_scrs[j][...]
            dkt = dkt_scrs[j][...]
            dq_ref[:, sl] = (acc[:, 0:HEAD_DIM] * scale).astype(BF16)
            dk_ref[:, sl] = (jnp.transpose(dkt)[:, 0:HEAD_DIM] * scale).astype(BF16)
            dc_ref[j:j + 1, :] = jnp.transpose(acc)[HEAD_DIM:HEAD_DIM + 1, :] - dkt[HEAD_DIM:HEAD_DIM + 1, :]

    blk = lambda off: pl.BlockSpec((t, 128), lambda b, hp: (b, off + hp))
    cblk = pl.BlockSpec((None, None, 8, t), lambda b, hp: (b, hp, 0, 0))
    return pl.pallas_call(
        body, name="fox_attn_bwd", grid=(bl, 8),
        in_specs=[blk(0), blk(8), blk(16), blk(0), blk(0), blk(0), cblk, ANY],
        out_specs=[blk(0), blk(0), blk(0), cblk],
        out_shape=[jax.ShapeDtypeStruct((n, ATT_WIDTH), BF16)] * 3 + [jax.ShapeDtypeStruct((bl, 8, 8, t), F32)],
        scratch_shapes=[pltpu.VMEM((t, 128), F32), pltpu.VMEM((t, 128), F32), pltpu.VMEM((t, 128), F32),
                        pltpu.VMEM((128, t), BF16), pltpu.VMEM((128, t), BF16), pltpu.VMEM((128, t), BF16),
                        pltpu.VMEM((128, t), F32), pltpu.VMEM((128, t), F32), pltpu.VMEM((128, t), F32)],
        compiler_params=_cparams(("parallel", "parallel")),
    )(qkv, qkv, qkv, do, o, lse, negc, after)


def _adamw(w, g, m, v, *, name):
    lead = w.ndim == 3
    r, c = w.shape[-2:]
    tr = _pick(r, (256, IN_SHARD // 3, 128, 64, 32, 16, 8))
    bc1 = 1.0 - ADAM_B1 ** ADAM_STEP
    bc2 = 1.0 - ADAM_B2 ** ADAM_STEP

    def body(w_ref, g_ref, m_ref, v_ref, d_ref, nm_ref, nv_ref):
        gv = g_ref[...]
        mn = ADAM_B1 * m_ref[...] + (1.0 - ADAM_B1) * gv
        vn = ADAM_B2 * v_ref[...] + (1.0 - ADAM_B2) * (gv * gv)
        m_hat = mn / bc1
        v_hat = vn / bc2
        d_ref[...] = -ADAM_LR * (m_hat / (jnp.sqrt(v_hat) + ADAM_EPS) + ADAM_WD * w_ref[...])
        nm_ref[...] = mn
        nv_ref[...] = vn

    flat = pl.BlockSpec((tr, c), lambda i: (i, 0))
    blk = pl.BlockSpec((None, tr, c), lambda i: (0, i, 0)) if lead else flat
    return pl.pallas_call(
        body, name=name, grid=(r // tr,), in_specs=[blk, flat, blk, blk], out_specs=[blk] * 3,
        out_shape=[jax.ShapeDtypeStruct(w.shape, F32)] * 3,
        compiler_params=_cparams(("parallel",)),
    )(w, g, m, v)


def _sum_leading(parts, *, name, out_dtype=F32):
    k, r, c = parts.shape
    tr = _pick(r, (512, 256, 128, 96, 64, 32, 16, 8))

    def body(p_ref, o_ref):
        acc = p_ref[0].astype(F32)
        for i in range(1, k):
            acc = acc + p_ref[i].astype(F32)
        o_ref[...] = acc.astype(out_dtype)

    return pl.pallas_call(
        body, name=name, grid=(r // tr,),
        in_specs=[pl.BlockSpec((k, tr, c), lambda i: (0, i, 0))],
        out_specs=pl.BlockSpec((tr, c), lambda i: (i, 0)),
        out_shape=jax.ShapeDtypeStruct((r, c), out_dtype),
        compiler_params=_cparams(("parallel",)),
    )(parts)


def _add_my_half(g, b, *, name):
    k, r, c = b.shape
    tr = _pick(r, (512, 256, 128))
    nrt = r // tr

    def body(lo_ref, hi_ref, b_ref, o_ref):
        mine = jnp.where(lax.axis_index("c") == 0, lo_ref[...], hi_ref[...])
        o_ref[...] = (mine.astype(F32) + b_ref[...].astype(F32)).astype(BF16)

    blk = pl.BlockSpec((None, tr, c), lambda j, i: (j, i, 0))
    return pl.pallas_call(
        body, name=name, grid=(k, nrt),
        in_specs=[blk, pl.BlockSpec((None, tr, c), lambda j, i: (j, i + nrt, 0)), blk], out_specs=blk,
        out_shape=jax.ShapeDtypeStruct((k, r, c), BF16),
        compiler_params=_cparams(("parallel", "parallel")),
    )(g, g, b)


ANY = pl.BlockSpec(memory_space=pl.ANY)


def _chip_peers(x, y):
    return [(1 - x, y, 2 * (1 - x) + y), (x, 1 - y, 2 * x + 1 - y), (1 - x, 1 - y, 2 * (1 - x) + 1 - y)]


def _gather_weights(blob, *, name):
    rows, cols = blob.shape
    half_rows = rows // 2

    def body(b_ref, o_ref, send_sems, recv_sems):
        x, y, c = lax.axis_index("x"), lax.axis_index("y"), lax.axis_index("c")
        me = 2 * x + y
        sibling = (x, y, 1 - c)
        peers = _chip_peers(x, y)

        def half(chip, hc):
            return o_ref.at[chip, pl.ds(hc * half_rows, half_rows), :]

        def copy(k, src, chip, hc, to):
            return pltpu.make_async_remote_copy(src_ref=src, dst_ref=half(chip, hc), send_sem=send_sems.at[k],
                                                recv_sem=recv_sems.at[k], device_id=to, device_id_type=MESH)

        my_half = b_ref.at[pl.ds(c * half_rows, half_rows), :]
        first = [copy(k, my_half, me, c, (px, py, c)) for k, (px, py, _) in enumerate(peers)]
        own = pltpu.make_async_remote_copy(src_ref=b_ref, dst_ref=o_ref.at[me], send_sem=send_sems.at[6],
                                           recv_sem=recv_sems.at[6], device_id=sibling, device_id_type=MESH)
        for cp in first + [own]:
            cp.start()
        passed = [copy(3 + k, half(pc, c), pc, c, sibling) for k, (_, _, pc) in enumerate(peers)]
        for k, (px, py, pc) in enumerate(peers):
            copy(k, my_half, pc, c, (px, py, c)).wait_recv()
            passed[k].start()
        for k, (_, _, pc) in enumerate(peers):
            copy(3 + k, half(pc, 1 - c), pc, 1 - c, sibling).wait_recv()
        own.wait_recv()
        for cp in first + passed + [own]:
            cp.wait_send()

    return pl.pallas_call(
        body, name=name, in_specs=[ANY], out_specs=ANY,
        out_shape=jax.ShapeDtypeStruct((N_CHIPS, rows, cols), BF16),
        scratch_shapes=[pltpu.SemaphoreType.DMA((7,)), pltpu.SemaphoreType.DMA((7,))],
    )(blob)


def _swap_halves(g, *, name):
    _, rows, cols = g.shape
    half_rows = rows // 2

    def body(g_ref, o_ref, send_sem, recv_sem):
        x, y, c = lax.axis_index("x"), lax.axis_index("y"), lax.axis_index("c")
        cp = pltpu.make_async_remote_copy(
            src_ref=g_ref.at[:, pl.ds((1 - c) * half_rows, half_rows), :], dst_ref=o_ref,
            send_sem=send_sem, recv_sem=recv_sem, device_id=(x, y, 1 - c), device_id_type=MESH)
        cp.start()
        cp.wait()

    return pl.pallas_call(
        body, name=name, in_specs=[ANY], out_specs=ANY,
        out_shape=jax.ShapeDtypeStruct((N_CHIPS, half_rows, cols), BF16),
        scratch_shapes=[pltpu.SemaphoreType.DMA, pltpu.SemaphoreType.DMA],
    )(g)


HBM_SPEC = pl.BlockSpec(memory_space=pltpu.HBM)
SEM_SPEC = pl.BlockSpec(memory_space=pltpu.SEMAPHORE)
SPLIT_EFFECT = pltpu.SideEffectType.DATAFLOW_SIDE_EFFECTING


def _gather_peers_copies(b_ref, land_ref, send_sems, recv_sems, sending):
    x, y, c = lax.axis_index("x"), lax.axis_index("y"), lax.axis_index("c")
    me = 2 * x + y
    half_rows = b_ref.shape[0] // 2
    src = b_ref.at[pl.ds(c * half_rows, half_rows), :]
    return [pltpu.make_async_remote_copy(
        src_ref=src, dst_ref=land_ref.at[me if sending else pc, pl.ds(c * half_rows, half_rows), :],
        send_sem=send_sems.at[k], recv_sem=recv_sems.at[k], device_id=(px, py, c), device_id_type=MESH)
        for k, (px, py, pc) in enumerate(_chip_peers(x, y))]


def _gather_start(blob, after, *, name):
    shape = (N_CHIPS,) + blob.shape

    def body(b_ref, land_ref, after_ref, send_sems, recv_sems, b_thru, land_thru, token):
        for cp in _gather_peers_copies(b_ref, land_ref, send_sems, recv_sems, True):
            cp.start()
        token[...] = jnp.zeros_like(token)

    return pl.pallas_call(
        body, name=name,
        out_shape=(pltpu.SemaphoreType.DMA((3,)), pltpu.SemaphoreType.DMA((3,)), pltpu.HBM(blob.shape, blob.dtype),
                   pltpu.HBM(shape, blob.dtype), jax.ShapeDtypeStruct((8, 128), F32)),
        in_specs=(HBM_SPEC, HBM_SPEC, ANY),
        out_specs=(SEM_SPEC, SEM_SPEC, HBM_SPEC, HBM_SPEC, pl.BlockSpec(memory_space=pltpu.VMEM)),
        input_output_aliases={0: 2, 1: 3},
        compiler_params=pltpu.CompilerParams(has_side_effects=SPLIT_EFFECT),
    )(pltpu.with_memory_space_constraint(blob, pltpu.HBM),
      pltpu.with_memory_space_constraint(lax.empty(shape, blob.dtype), pltpu.HBM), after)


def _gather_wait(send_sems, recv_sems, b_thru, land_thru, after, *, name):
    def body(b_ref, land_ref, send_sems, recv_sems, after_ref, b_dead, got_ref):
        for cp in _gather_peers_copies(b_ref, land_ref, send_sems, recv_sems, False):
            cp.wait_send()
            cp.wait_recv()

    return pl.pallas_call(
        body, name=name,
        out_shape=(pltpu.HBM(b_thru.shape, b_thru.dtype), pltpu.HBM(land_thru.shape, land_thru.dtype)),
        in_specs=(HBM_SPEC, HBM_SPEC, SEM_SPEC, SEM_SPEC, ANY), out_specs=(HBM_SPEC, HBM_SPEC),
        input_output_aliases={0: 0, 1: 1},
        compiler_params=pltpu.CompilerParams(has_side_effects=SPLIT_EFFECT),
    )(b_thru, land_thru, send_sems, recv_sems, after)


def _gather_forward(land, blob, *, name):
    half_rows = land.shape[1] // 2

    def body(l_ref, b_ref, o_ref, send_sems, recv_sems):
        x, y, c = lax.axis_index("x"), lax.axis_index("y"), lax.axis_index("c")
        me = 2 * x + y
        sibling = (x, y, 1 - c)
        cps = []
        for k, (_, _, pc) in enumerate(_chip_peers(x, y)):
            mine = pl.ds(c * half_rows, half_rows)
            cps.append(pltpu.make_async_remote_copy(
                src_ref=l_ref.at[pc, mine, :], dst_ref=o_ref.at[pc, mine, :], send_sem=send_sems.at[k],
                recv_sem=recv_sems.at[k], device_id=sibling, device_id_type=MESH))
        cps.append(pltpu.make_async_remote_copy(src_ref=b_ref, dst_ref=o_ref.at[me], send_sem=send_sems.at[3],
                                                recv_sem=recv_sems.at[3], device_id=sibling, device_id_type=MESH))
        for cp in cps:
            cp.start()
        for k, (_, _, pc) in enumerate(_chip_peers(x, y)):
            theirs = pl.ds((1 - c) * half_rows, half_rows)
            pltpu.make_async_remote_copy(
                src_ref=l_ref.at[pc, theirs, :], dst_ref=o_ref.at[pc, theirs, :], send_sem=send_sems.at[k],
                recv_sem=recv_sems.at[k], device_id=sibling, device_id_type=MESH).wait_recv()
        cps[3].wait_recv()
        for cp in cps:
            cp.wait_send()

    return pl.pallas_call(
        body, name=name, in_specs=[ANY, ANY], out_specs=ANY, input_output_aliases={0: 0},
        out_shape=jax.ShapeDtypeStruct(land.shape, land.dtype),
        scratch_shapes=[pltpu.SemaphoreType.DMA((4,)), pltpu.SemaphoreType.DMA((4,))],
    )(land, blob)


def _exchange_peers_copies(p_ref, land_ref, send_sems, recv_sems, sending):
    x, y, c = lax.axis_index("x"), lax.axis_index("y"), lax.axis_index("c")
    me = 2 * x + y
    return [pltpu.make_async_remote_copy(src_ref=p_ref.at[pc], dst_ref=land_ref.at[me if sending else pc],
                                         send_sem=send_sems.at[k], recv_sem=recv_sems.at[k],
                                         device_id=(px, py, c), device_id_type=MESH)
            for k, (px, py, pc) in enumerate(_chip_peers(x, y))]


def _exchange_start(p, *, name):
    def body(p_ref, land_ref, send_sems, recv_sems, p_thru, land_thru, token):
        for cp in _exchange_peers_copies(p_ref, land_ref, send_sems, recv_sems, True):
            cp.start()
        token[...] = jnp.zeros_like(token)

    return pl.pallas_call(
        body, name=name,
        out_shape=(pltpu.SemaphoreType.DMA((3,)), pltpu.SemaphoreType.DMA((3,)), pltpu.HBM(p.shape, p.dtype),
                   pltpu.HBM(p.shape, p.dtype), jax.ShapeDtypeStruct((8, 128), F32)),
        in_specs=(HBM_SPEC, HBM_SPEC),
        out_specs=(SEM_SPEC, SEM_SPEC, HBM_SPEC, HBM_SPEC, pl.BlockSpec(memory_space=pltpu.VMEM)),
        input_output_aliases={0: 2, 1: 3},
        compiler_params=pltpu.CompilerParams(has_side_effects=SPLIT_EFFECT),
    )(pltpu.with_memory_space_constraint(p, pltpu.HBM),
      pltpu.with_memory_space_constraint(lax.empty(p.shape, p.dtype), pltpu.HBM))


def _exchange_wait(send_sems, recv_sems, p_thru, land_thru, after, *, name):
    def body(p_ref, land_ref, send_sems, recv_sems, after_ref, p_dead, got_ref):
        for cp in _exchange_peers_copies(p_ref, land_ref, send_sems, recv_sems, False):
            cp.wait_send()
            cp.wait_recv()

    return pl.pallas_call(
        body, name=name,
        out_shape=(pltpu.HBM(p_thru.shape, p_thru.dtype), pltpu.HBM(p_thru.shape, p_thru.dtype)),
        in_specs=(HBM_SPEC, HBM_SPEC, SEM_SPEC, SEM_SPEC, ANY), out_specs=(HBM_SPEC, HBM_SPEC),
        input_output_aliases={0: 0, 1: 1},
        compiler_params=pltpu.CompilerParams(has_side_effects=SPLIT_EFFECT),
    )(p_thru, land_thru, send_sems, recv_sems, after)


def _sum_parts(parts, own, *, name):
    k, r, c = parts.shape
    tr = _pick(r, (512, 256, 128))

    def body(p_ref, own_ref, o_ref):
        me = 2 * lax.axis_index("x") + lax.axis_index("y")
        acc = jnp.zeros((tr, c), F32)
        for i in range(k):
            acc = acc + jnp.where(me == i, own_ref[i], p_ref[i]).astype(F32)
        o_ref[...] = acc

    blk = pl.BlockSpec((k, tr, c), lambda i: (0, i, 0))
    return pl.pallas_call(
        body, name=name, grid=(r // tr,), in_specs=[blk, blk],
        out_specs=pl.BlockSpec((tr, c), lambda i: (i, 0)),
        out_shape=jax.ShapeDtypeStruct((r, c), F32),
        compiler_params=_cparams(("parallel",)),
    )(parts, own)


def _from_sibling(a, *, name):
    def body(a_ref, o_ref, send_sem, recv_sem):
        x, y, c = lax.axis_index("x"), lax.axis_index("y"), lax.axis_index("c")
        cp = pltpu.make_async_remote_copy(src_ref=a_ref, dst_ref=o_ref, send_sem=send_sem, recv_sem=recv_sem,
                                          device_id=(x, y, 1 - c), device_id_type=MESH)
        cp.start()
        cp.wait()

    return pl.pallas_call(
        body, name=name, in_specs=[ANY], out_specs=ANY,
        out_shape=jax.ShapeDtypeStruct(a.shape, F32),
        scratch_shapes=[pltpu.SemaphoreType.DMA, pltpu.SemaphoreType.DMA],
    )(a)


def _join_halves(gh, *, name):
    other = _from_sibling(gh, name=name)
    south = lax.axis_index("c") == 0
    return jnp.concatenate([jnp.where(south, gh, other), jnp.where(south, other, gh)], axis=0)


def _add_f32(a, b, *, name):
    r, c = a.shape
    tr = _pick(r, (512, 256, 128))

    def body(a_ref, b_ref, o_ref):
        o_ref[...] = a_ref[...] + b_ref[...]

    blk = pl.BlockSpec((tr, c), lambda i: (i, 0))
    return pl.pallas_call(
        body, name=name, grid=(r // tr,), in_specs=[blk, blk], out_specs=blk,
        out_shape=jax.ShapeDtypeStruct((r, c), F32),
        compiler_params=_cparams(("parallel",)),
    )(a, b)


def _gather_small(s, *, name):
    rows = s.shape[0]

    def body(s_ref, o_ref, send_sems, recv_sems, local_sem):
        x, y, c = lax.axis_index("x"), lax.axis_index("y"), lax.axis_index("c")
        me = 4 * x + 2 * y + c
        mine = pltpu.make_async_copy(s_ref, o_ref.at[me], local_sem)
        mine.start()
        peers = []
        for k in range(1, 8):
            peers.append((1 - x if k & 4 else x, 1 - y if k & 2 else y, 1 - c if k & 1 else c))
        cps = [pltpu.make_async_remote_copy(src_ref=s_ref, dst_ref=o_ref.at[me], send_sem=send_sems.at[k],
                                            recv_sem=recv_sems.at[k], device_id=p, device_id_type=MESH)
               for k, p in enumerate(peers)]
        for cp in cps:
            cp.start()
        for k, (px, py, pc) in enumerate(peers):
            pltpu.make_async_remote_copy(src_ref=s_ref, dst_ref=o_ref.at[4 * px + 2 * py + pc],
                                         send_sem=send_sems.at[k], recv_sem=recv_sems.at[k],
                                         device_id=(px, py, pc), device_id_type=MESH).wait_recv()
        for cp in cps:
            cp.wait_send()
        mine.wait()

    return pl.pallas_call(
        body, name=name, in_specs=[ANY], out_specs=ANY,
        out_shape=jax.ShapeDtypeStruct((8, rows, 128), F32),
        scratch_shapes=[pltpu.SemaphoreType.DMA((7,)), pltpu.SemaphoreType.DMA((7,)), pltpu.SemaphoreType.DMA],
    )(s)


IN_SHARD = IN_WIDTH // N_CHIPS
IN_SHARD_PAD = 1536
UP_ROWS, DOWN_ROWS, OUT_ROWS = 1024, 1024, 512
REST_ROWS = UP_ROWS + DOWN_ROWS + OUT_ROWS


def _pack_in(w_in_s):
    return jnp.pad(w_in_s, ((0, 0), (0, IN_SHARD_PAD - IN_SHARD))).astype(BF16)


def _pack_rest(w_out_s, w_up_s, w_down_s):
    return jnp.concatenate([w_up_s, w_down_s, w_out_s], axis=0).astype(BF16)


def _unpack_rest(blob):
    return (blob[UP_ROWS + DOWN_ROWS:], blob[0:UP_ROWS], blob[UP_ROWS:UP_ROWS + DOWN_ROWS])


def _full_w_in(g_in):
    return jnp.concatenate([g_in[j, :, :IN_SHARD] for j in range(N_CHIPS)], axis=1)


def _full_rest(g_rest):
    parts = [_unpack_rest(g_rest[j]) for j in range(N_CHIPS)]
    w_out = jnp.concatenate([p[0] for p in parts], axis=0)
    w_up = jnp.concatenate([p[1] for p in parts], axis=1)
    w_down = jnp.concatenate([p[2] for p in parts], axis=0)
    return w_out, w_up, w_down


def _split_w_in(w_in):
    z_xbc = w_in[:, 0:2560]
    dt = w_in[:, 2560:2576]
    qkv = w_in[:, 2576:5648]
    f = w_in[:, 5648:5664]
    pad = jnp.zeros((w_in.shape[0], PA_WIDTH - 2592), w_in.dtype)
    return jnp.concatenate([z_xbc, dt, f, pad], axis=1), qkv


def _merge_w_in(d_a, d_qkv):
    return jnp.concatenate([d_a[:, 0:2560], d_a[:, 2560:2576], d_qkv, d_a[:, 2576:2592]], axis=1)


def _local_step(x3, target3, w_in, rest_weights, norm_mix_w, conv_w, conv_b, dt_bias, a_log, d_skip,
                ssd_norm_w, f_bias, norm_mlp_w, norm_final_w, first_after=None, early_grads=None, late_grads=None):
    bl, t, d = x3.shape
    n = bl * t
    x = x3.reshape(n, d)
    target = target3.reshape(n, d)
    w_a, w_qkv = _split_w_in(w_in)
    nfw = norm_final_w.reshape(1, d)
    dskip_e = jnp.repeat(d_skip, HEAD_DIM, axis=1)

    r1, r2, kt = min(n, 1024), min(n, 512), min(n, 2048)
    if first_after is None:
        first_after = jnp.zeros((8, 128), F32)
    h0, rstd0, proj_a = _norm_mm(x, norm_mix_w, w_a, first_after, name="norm_mix_proj_a", tm=r2)
    qkv = _mm(h0, w_qkv, name="proj_qkv", tiles=(r2, QKV_WIDTH, D_MODEL), out_dtype=BF16)
    bias128 = jnp.concatenate([dt_bias, f_bias, jnp.zeros((1, 96), F32)], axis=1)
    alog128 = jnp.concatenate([a_log, jnp.zeros((1, 112), F32)], axis=1)
    dt, gate_d, acum, acum_t, negc = _prep(proj_a, bias128, alog128, bl, t)
    xc, dsilu = _conv_fwd(proj_a, conv_w, conv_b, bl, t)
    y_ssd, y_pre, hprev = _ssd_fwd(xc, proj_a, dt, acum, acum_t, dskip_e, ssd_norm_w, bl, t)
    y_att, lse = _attn_fwd(qkv, negc, bl, t)
    w_out, w_up, w_down = rest_weights(y_att)
    wo_s, wo_a = w_out[:SSD_WIDTH], w_out[SSD_WIDTH:]
    h1, h1n, rstd1 = _mm_norm_fwd(y_ssd, wo_s, y_att, wo_a, x, norm_mlp_w, name="out_proj_norm_mlp", tm=r2)
    up = _mm(h1n, w_up, name="mlp_up", tiles=(r1, D_FF, D_MODEL), out_dtype=BF16)
    dh2, dh2b, loss, d_nfw = _mm_final(up, w_down, h1, nfw, target, name="mlp_down_final_norm_loss", tm=r2,
                                       a_act="relu2")

    dup = _mm(dh2b, w_down, name="mlp_down_bwd_act", tiles=(r2, D_FF, D_MODEL), tb=True, epi_up=up, out_dtype=BF16)
    rest_shape = (N_CHIPS, REST_ROWS, D_MODEL)
    gb_rest = _mm(up, dh2b, name="mlp_down_bwd_w", tiles=(DOWN_ROWS, D_MODEL, kt), ta=True, a_act="relu2",
                  out_dtype=BF16, into=(rest_shape, (None, DOWN_ROWS, D_MODEL), lambda i, j, k: (i, 1, 0), None))
    dh1, dh1b, d_nmlp = _mm_norm_bwd([(dup, w_up)], h1, rstd1, norm_mlp_w, dh2, name="mlp_up_bwd_act_norm_mlp",
                                     tm=r2)
    gb_rest = _mm(h1n, dup, name="mlp_up_bwd_w", tiles=(D_MODEL, UP_ROWS, kt), ta=True, out_dtype=BF16,
                  into=(rest_shape, (None, D_MODEL, UP_ROWS), lambda i, j, k: (j, 0, 0), gb_rest))
    dys, do = _mm_two_halves(dh1b, w_out, name="out_proj_bwd_act", tm=r1)
    out_block = (UP_ROWS + DOWN_ROWS) // OUT_ROWS
    for half, (y_half, tag) in enumerate(((y_ssd, "ssd"), (y_att, "att"))):
        gb_rest = _mm(y_half, dh1b, name="out_proj_bwd_w_" + tag, tiles=(2 * OUT_ROWS, D_MODEL, kt), ta=True,
                      out_dtype=BF16, into=(rest_shape, (2, OUT_ROWS, D_MODEL),
                                            functools.partial(lambda i, j, k, h: (h, out_block, 0), h=half),
                                            gb_rest))
    token = jnp.zeros((8, 128), F32) if early_grads is None else early_grads(gb_rest)
    dq, dk, dv, dcb = _attn_bwd(qkv, do, y_att, lse, negc, token, bl, t)
    dc = jnp.pad(dcb[:, :, 0:2, :].transpose(0, 3, 1, 2).reshape(n, 16), ((0, 0), (16, 96)))
    dxc, dpa, ddt_raw, d_snw, d_dsk, d_alog, d_dtb = _ssd_bwd(dys, xc, proj_a, y_pre, hprev, dt, gate_d, acum,
                                                             acum_t, alog128, dskip_e, ssd_norm_w, bl, t)
    dpa, d_conv_w, d_conv_b = _conv_bwd(dxc, dsilu, proj_a, conv_w, dpa, bl, t)
    dproj_a, d_fb = _fpost(dc, gate_d, ddt_raw, dpa, bl, t)
    dqkv = jnp.concatenate([dq, dk, dv], axis=1)
    d_w_a = _mm(h0, dproj_a, name="proj_a_bwd_w", tiles=(1024, 896, kt), ta=True, out_dtype=BF16)
    d_w_qkv = _mm(h0, dqkv, name="proj_qkv_bwd_w", tiles=(1024, 1024, kt), ta=True, out_dtype=BF16)
    d_w_in = _merge_w_in(d_w_a, d_w_qkv)
    late_token = None if late_grads is None else late_grads(d_w_in)
    dx, _, d_nmix = _mm_norm_bwd([(dproj_a, w_a), (dqkv, w_qkv)], x, rstd0, norm_mix_w, dh1,
                                 name="proj_bwd_act_norm_mix", tm=min(n, 256), after=late_token)

    grads = dict(norm_mix_w=d_nmix, w_in=d_w_in, conv_w=d_conv_w, conv_b=d_conv_b,
                 dt_bias=d_dtb, a_log=d_alog, d_skip=d_dsk, ssd_norm_w=d_snw, f_bias=d_fb, rest=gb_rest,
                 norm_mlp_w=d_nmlp, norm_final_w=d_nfw)
    return dx.reshape(bl, t, d), loss, grads


SMALL_ORDER = ("norm_mix_w", "conv_w", "conv_b", "dt_bias", "a_log", "d_skip", "ssd_norm_w", "f_bias",
               "norm_mlp_w", "norm_final_w")
SMALL_SIZES = (1024, 4 * CONV_CH, CONV_CH, 16, 16, 16, 1024, 16, 1024, 1024)


def _pack_small(vals, rows):
    flat = jnp.concatenate([v.reshape(-1).astype(F32) for v in vals])
    return jnp.pad(flat, (0, rows * 128 - flat.shape[0])).reshape(rows, 128)


def _unpack_small(packed, sizes):
    flat = packed.reshape(-1)
    out, o = [], 0
    for s in sizes:
        out.append(flat[o:o + s])
        o += s
    return out


def kernel(x, norm_mix_w, w_in, conv_w, conv_b, dt_bias, a_log, d_skip, ssd_norm_w, f_bias, w_out, norm_mlp_w, w_up, w_down, norm_final_w, loss_target, m_norm_mix_w, m_w_in, m_conv_w, m_conv_b, m_dt_bias, m_a_log, m_d_skip, m_ssd_norm_w, m_f_bias, m_w_out, m_norm_mlp_w, m_w_up, m_w_down, m_norm_final_w, v_norm_mix_w, v_w_in, v_conv_w, v_conv_b, v_dt_bias, v_a_log, v_d_skip, v_ssd_norm_w, v_f_bias, v_w_out, v_norm_mlp_w, v_w_up, v_w_down, v_norm_final_w):
    chip = 2 * lax.axis_index("x") + lax.axis_index("y")
    cw = CONV_CH // N_CHIPS

    own_in = _pack_in(w_in[0])
    own_rest = _pack_rest(w_out[0], w_up[0], w_down[0])
    g_in = _gather_weights(own_in, name="gather_w_in")
    w_in_f = _full_w_in(g_in)
    *rest_handles, rest_token = _gather_start(own_rest, g_in, name="gather_start_rest")

    def rest_weights(after):
        _, landed = _gather_wait(*rest_handles, after, name="gather_wait_rest")
        return _full_rest(_gather_forward(landed, own_rest, name="gather_forward_rest"))
    small_all = _gather_small(_pack_small([conv_w[0]], 16), name="gather_conv_w")
    conv_w_f = jnp.concatenate([small_all[2 * j].reshape(-1)[:4 * cw].reshape(4, cw) for j in range(N_CHIPS)], axis=1)

    def chip_partial(gb, tag):
        from_sibling = _swap_halves(gb, name="grad_swap_halves_" + tag)
        return _add_my_half(gb, from_sibling, name="grad_add_sibling_" + tag)

    in_flight = {}

    def early_grads(gb_rest):
        *handles, token = _exchange_start(gb_rest, name="grad_exchange_start_rest")
        in_flight["rest"] = handles
        return token

    def late_grads(d_w_in):
        gb_in = jnp.stack([_pack_in(d_w_in[:, j * IN_SHARD:(j + 1) * IN_SHARD]) for j in range(N_CHIPS)])
        *handles, token = _exchange_start(chip_partial(gb_in, "in"), name="grad_exchange_start_in")
        in_flight["in"] = handles
        return token

    dx, loss_part, g = _local_step(x, loss_target, w_in_f, rest_weights, norm_mix_w, conv_w_f,
                                   conv_b, dt_bias, a_log, d_skip, ssd_norm_w, f_bias, norm_mlp_w, norm_final_w,
                                   first_after=rest_token, early_grads=early_grads, late_grads=late_grads)

    send_sems, recv_sems, part_rest, land_rest = in_flight["rest"]
    part_rest, parts_rest = _exchange_wait(send_sems, recv_sems, part_rest, land_rest, dx,
                                           name="grad_exchange_wait_rest")
    g_rest_core = _sum_parts(parts_rest, part_rest, name="grad_sum_chips_rest")
    g_rest_sibling = _from_sibling(g_rest_core, name="grad_swap_sums_rest")
    g_w_out, g_w_up, g_w_down = _unpack_rest(_add_f32(g_rest_core, g_rest_sibling, name="grad_add_cores_rest"))

    part_in, parts_in = _exchange_wait(*in_flight["in"], dx, name="grad_exchange_wait_in")
    g_in_half = _sum_parts(parts_in, part_in, name="grad_sum_chips_in")
    g_w_in = _join_halves(g_in_half, name="grad_join_halves_in")[:, :IN_SHARD]

    small_vals = [g[k] for k in SMALL_ORDER] + [loss_part[:, 0:1]]
    small_sum = _sum_leading(_gather_small(_pack_small(small_vals, SMALL_ROWS), name="gather_small_grads"), name="small_sum")
    sg = dict(zip(SMALL_ORDER + ("loss",), _unpack_small(small_sum, SMALL_SIZES + (1,))))
    loss = sg["loss"].reshape(())
    g_conv_full = sg["conv_w"].reshape(4, CONV_CH)
    g_conv = lax.dynamic_slice_in_dim(g_conv_full, chip * cw, cw, axis=1)

    grads = dict(norm_mix_w=sg["norm_mix_w"].reshape(1, -1), w_in=g_w_in[None], conv_w=g_conv[None],
                 conv_b=sg["conv_b"].reshape(1, -1), dt_bias=sg["dt_bias"].reshape(1, -1),
                 a_log=sg["a_log"].reshape(1, -1), d_skip=sg["d_skip"].reshape(1, -1),
                 ssd_norm_w=sg["ssd_norm_w"].reshape(1, -1), f_bias=sg["f_bias"].reshape(1, -1), w_out=g_w_out[None],
                 norm_mlp_w=sg["norm_mlp_w"].reshape(1, -1), w_up=g_w_up[None], w_down=g_w_down[None],
                 norm_final_w=sg["norm_final_w"])
    weights = dict(norm_mix_w=norm_mix_w, w_in=w_in, conv_w=conv_w, conv_b=conv_b, dt_bias=dt_bias, a_log=a_log,
                   d_skip=d_skip, ssd_norm_w=ssd_norm_w, f_bias=f_bias, w_out=w_out, norm_mlp_w=norm_mlp_w,
                   w_up=w_up, w_down=w_down, norm_final_w=norm_final_w)
    ms = dict(norm_mix_w=m_norm_mix_w, w_in=m_w_in, conv_w=m_conv_w, conv_b=m_conv_b, dt_bias=m_dt_bias,
              a_log=m_a_log, d_skip=m_d_skip, ssd_norm_w=m_ssd_norm_w, f_bias=m_f_bias, w_out=m_w_out,
              norm_mlp_w=m_norm_mlp_w, w_up=m_w_up, w_down=m_w_down, norm_final_w=m_norm_final_w)
    vs = dict(norm_mix_w=v_norm_mix_w, w_in=v_w_in, conv_w=v_conv_w, conv_b=v_conv_b, dt_bias=v_dt_bias,
              a_log=v_a_log, d_skip=v_d_skip, ssd_norm_w=v_ssd_norm_w, f_bias=v_f_bias, w_out=v_w_out,
              norm_mlp_w=v_norm_mlp_w, w_up=v_w_up, w_down=v_w_down, norm_final_w=v_norm_final_w)
    names = list(weights)
    big = ("w_in", "w_out", "w_up", "w_down")
    delta, new_m, new_v = {}, {}, {}
    for k, g2 in zip(big[1:], (g_w_out, g_w_up, g_w_down)):
        delta[k], new_m[k], new_v[k] = _adamw(weights[k], g2, ms[k], vs[k], name="adamw_" + k)
    g_in_t = g_w_in.T
    outs_t = _adamw(w_in[0].T, g_in_t, m_w_in[0].T, v_w_in[0].T, name="adamw_w_in")
    delta["w_in"], new_m["w_in"], new_v["w_in"] = [o.T[None] for o in outs_t]
    grads["w_in"] = g_in_t.T[None]
    smalls = [k for k in names if k not in big]
    sizes = [math.prod(weights[k].shape) for k in smalls]
    rows = -(-sum(sizes) // 1024) * 8
    packs = [_pack_small([d[k] for k in smalls], rows) for d in (weights, grads, ms, vs)]
    outs = _adamw(*packs, name="adamw_small")
    for o, dst in zip(outs, (delta, new_m, new_v)):
        for k, val in zip(smalls, _unpack_small(o, sizes)):
            dst[k] = val.reshape(weights[k].shape)
    return (loss, dx, *[grads[k] for k in names], *[delta[k] for k in names], *[new_m[k] for k in names],
            *[new_v[k] for k in names])
```

```python
import functools
import math

import jax
import jax.numpy as jnp
from jax import lax
from jax.experimental import pallas as pl
from jax.experimental.pallas import tpu as pltpu

F32 = jnp.float32
BF16 = jnp.bfloat16
HIGHEST = lax.Precision.HIGHEST
MESH = pl.DeviceIdType.MESH

D_MODEL = 1024
HEAD_DIM = 64
SSD_WIDTH = 1024
SSD_STATE = 128
CONV_CH = 1536
CHUNK = 128
ATT_WIDTH = 1024
EPS = 1e-5
IN_WIDTH = 5664
PA_WIDTH = 2688
QKV_WIDTH = 3072
D_FF = 4096
ATT_FWD_BLOCK = 512
ATT_BWD_BLOCK = 256
NEG = -1e30
LOG2E = 1.4426950408889634
VMEM_LIMIT = 48 * 1024 * 1024

ADAM_LR = 0.001
ADAM_B1 = 0.9
ADAM_B2 = 0.999
ADAM_EPS = 1e-08
ADAM_WD = 0.01
ADAM_STEP = 10

N_CHIPS = 4
SMALL_ROWS = 96


def _cparams(sem):
    return pltpu.CompilerParams(dimension_semantics=sem, vmem_limit_bytes=VMEM_LIMIT)


def _pick(n, cands):
    for c in cands:
        if n % c == 0:
            return c
    return n


MM_CHUNK = 512


def _mm(a, b, *, name, tiles, ta=False, tb=False, out_dtype=F32, res=None, a_act=None, epi_up=None, after=None,
        into=None):
    n_unread = (after is not None) + (into is not None and into[3] is not None)
    if ta:
        K, M = a.shape
    else:
        M, K = a.shape
    if tb:
        N, K2 = b.shape
    else:
        K2, N = b.shape
    assert K == K2, (a.shape, b.shape)
    tm, tn, tk = tiles
    assert M % tm == 0 and N % tn == 0 and K % tk == 0, (name, M, N, K, tiles)
    nk = K // tk
    dn = (((0 if ta else 1,), (1 if tb else 0,)), ((), ()))
    has_res = res is not None
    has_up = epi_up is not None
    cn = _pick(tn, (MM_CHUNK, 384, 256, 128))

    def prologue(av):
        if a_act == "relu2":
            r = jnp.maximum(av.astype(F32), 0.0)
            av = r * r
        return av.astype(BF16)

    def epilogue(out, res_v, up_v):
        if has_res:
            out = out + res_v.astype(F32)
        if has_up:
            out = out * (2.0 * jnp.maximum(up_v.astype(F32), 0.0))
        return out.astype(out_dtype)

    def body(*refs):
        a_ref, b_ref = refs[0], refs[1]
        i = 2
        res_ref = up_ref = None
        if has_res:
            res_ref = refs[i]
            i += 1
        if has_up:
            up_ref = refs[i]
            i += 1
        i += n_unread
        o_ref = refs[i]
        if nk == 1:
            av = prologue(a_ref[...])
            if len(o_ref.shape) == 3:
                out = lax.dot_general(av, b_ref[...].astype(BF16), dn, preferred_element_type=F32)
                out = epilogue(out, res_ref[...] if has_res else None, up_ref[...] if has_up else None)
                o_ref[...] = out.reshape(o_ref.shape)
                return
            for c in range(tn // cn):
                cs = slice(c * cn, (c + 1) * cn)
                bv = (b_ref[cs, :] if tb else b_ref[:, cs]).astype(BF16)
                out = lax.dot_general(av, bv, dn, preferred_element_type=F32)
                o_ref[:, cs] = epilogue(out, res_ref[:, cs] if has_res else None, up_ref[:, cs] if has_up else None)
            return
        acc_ref = refs[i + 1]
        k = pl.program_id(2)

        @pl.when(k == 0)
        def _():
            acc_ref[...] = jnp.zeros_like(acc_ref)

        acc_ref[...] += lax.dot_general(prologue(a_ref[...]), b_ref[...].astype(BF16), dn,
                                        preferred_element_type=F32)

        @pl.when(k == nk - 1)
        def _():
            out = epilogue(acc_ref[...], res_ref[...] if has_res else None, up_ref[...] if has_up else None)
            o_ref[...] = out.reshape(o_ref.shape)

    a_spec = pl.BlockSpec((tk, tm), lambda i, j, k: (k, i)) if ta else pl.BlockSpec((tm, tk), lambda i, j, k: (i, k))
    b_spec = pl.BlockSpec((tn, tk), lambda i, j, k: (j, k)) if tb else pl.BlockSpec((tk, tn), lambda i, j, k: (k, j))
    o_spec = pl.BlockSpec((tm, tn), lambda i, j, k: (i, j))
    ins, specs = [a, b], [a_spec, b_spec]
    if has_res:
        ins.append(res)
        specs.append(o_spec)
    if has_up:
        ins.append(epi_up)
        specs.append(o_spec)
    if after is not None:
        ins.append(after)
        specs.append(pl.BlockSpec(memory_space=pl.ANY))
    out_shape, out_spec, aliases = jax.ShapeDtypeStruct((M, N), out_dtype), o_spec, {}
    if into is not None:
        shape, block, index, buf = into
        out_shape, out_spec = jax.ShapeDtypeStruct(shape, out_dtype), pl.BlockSpec(block, index)
        if buf is not None:
            aliases = {len(ins): 0}
            ins.append(buf)
            specs.append(pl.BlockSpec(memory_space=pl.ANY))
    return pl.pallas_call(
        body, name=name, grid=(M // tm, N // tn, nk),
        in_specs=specs, out_specs=out_spec, out_shape=out_shape, input_output_aliases=aliases,
        scratch_shapes=[] if nk == 1 else [pltpu.VMEM((tm, tn), F32)],
        compiler_params=_cparams(("parallel", "parallel", "arbitrary")),
    )(*ins)


def _rows_product(a_ref, b_ref, tb, a_act):
    av = a_ref[...]
    if a_act == "relu2":
        r = jnp.maximum(av.astype(F32), 0.0)
        av = r * r
    dn = (((1,), (1 if tb else 0,)), ((), ()))
    return lax.dot_general(av.astype(BF16), b_ref[...].astype(BF16), dn, preferred_element_type=F32)


def _norm_mm(x, w, b, after, *, name, tm):
    m, d = x.shape
    n = b.shape[1]
    cn = _pick(n, (MM_CHUNK, 384, 256, 128))

    def body(x_ref, w_ref, b_ref, after_ref, h_ref, r_ref, o_ref):
        xv = x_ref[...]
        rstd = lax.rsqrt(jnp.mean(xv * xv, axis=1, keepdims=True) + EPS)
        hv = (xv * rstd * w_ref[...]).astype(BF16)
        h_ref[...] = hv
        r_ref[...] = rstd
        for c in range(n // cn):
            cs = slice(c * cn, (c + 1) * cn)
            o_ref[:, cs] = jnp.dot(hv, b_ref[:, cs].astype(BF16), preferred_element_type=F32)

    row = pl.BlockSpec((tm, d), lambda i: (i, 0))
    return pl.pallas_call(
        body, name=name, grid=(m // tm,),
        in_specs=[row, pl.BlockSpec((1, d), lambda i: (0, 0)), pl.BlockSpec((d, n), lambda i: (0, 0)),
                  pl.BlockSpec(memory_space=pl.ANY)],
        out_specs=[row, pl.BlockSpec((tm, 1), lambda i: (i, 0)), pl.BlockSpec((tm, n), lambda i: (i, 0))],
        out_shape=[jax.ShapeDtypeStruct((m, d), BF16), jax.ShapeDtypeStruct((m, 1), F32),
                   jax.ShapeDtypeStruct((m, n), F32)],
        compiler_params=_cparams(("parallel",)),
    )(x, w, b, after)


def _mm_norm_fwd(a1, b1, a2, b2, res, w, *, name, tm):
    m, k1 = a1.shape
    k2 = a2.shape[1]
    d = b1.shape[1]

    def body(a1_ref, b1_ref, a2_ref, b2_ref, res_ref, w_ref, h_ref, y_ref, r_ref):
        hv = _rows_product(a1_ref, b1_ref, False, None) + _rows_product(a2_ref, b2_ref, False, None) + res_ref[...]
        rstd = lax.rsqrt(jnp.mean(hv * hv, axis=1, keepdims=True) + EPS)
        h_ref[...] = hv
        y_ref[...] = (hv * rstd * w_ref[...]).astype(BF16)
        r_ref[...] = rstd

    row = pl.BlockSpec((tm, d), lambda i: (i, 0))
    return pl.pallas_call(
        body, name=name, grid=(m // tm,),
        in_specs=[pl.BlockSpec((tm, k1), lambda i: (i, 0)), pl.BlockSpec((k1, d), lambda i: (0, 0)),
                  pl.BlockSpec((tm, k2), lambda i: (i, 0)), pl.BlockSpec((k2, d), lambda i: (0, 0)), row,
                  pl.BlockSpec((1, d), lambda i: (0, 0))],
        out_specs=[row, row, pl.BlockSpec((tm, 1), lambda i: (i, 0))],
        out_shape=[jax.ShapeDtypeStruct((m, d), F32), jax.ShapeDtypeStruct((m, d), BF16),
                   jax.ShapeDtypeStruct((m, 1), F32)],
        compiler_params=_cparams(("parallel",)),
    )(a1, b1, a2, b2, res, w)


def _mm_final(a, b, res, w, target, *, name, tm, a_act):
    m, k = a.shape
    d = b.shape[1]

    def body(a_ref, b_ref, res_ref, w_ref, t_ref, dh_ref, dhb_ref, loss_ref, dw_ref):
        @pl.when(pl.program_id(0) == 0)
        def _():
            loss_ref[...] = jnp.zeros_like(loss_ref)
            dw_ref[...] = jnp.zeros_like(dw_ref)

        hv = _rows_product(a_ref, b_ref, False, a_act) + res_ref[...]
        wv = w_ref[...]
        rstd = lax.rsqrt(jnp.mean(hv * hv, axis=1, keepdims=True) + EPS)
        xhat = hv * rstd
        err = xhat * wv - t_ref[...]
        loss_ref[...] += 0.5 * jnp.sum(jnp.mean(err * err, axis=1, keepdims=True), axis=0, keepdims=True)
        dy = err * (1.0 / d)
        gw = dy * wv
        dh = rstd * (gw - xhat * jnp.mean(gw * xhat, axis=1, keepdims=True))
        dh_ref[...] = dh
        dhb_ref[...] = dh.astype(BF16)
        dw_ref[...] += jnp.sum(dy * xhat, axis=0, keepdims=True)

    row = pl.BlockSpec((tm, d), lambda i: (i, 0))
    vec = pl.BlockSpec((1, d), lambda i: (0, 0))
    return pl.pallas_call(
        body, name=name, grid=(m // tm,),
        in_specs=[pl.BlockSpec((tm, k), lambda i: (i, 0)), pl.BlockSpec((k, d), lambda i: (0, 0)), row, vec, row],
        out_specs=[row, row, pl.BlockSpec((1, 128), lambda i: (0, 0)), vec],
        out_shape=[jax.ShapeDtypeStruct((m, d), F32), jax.ShapeDtypeStruct((m, d), BF16),
                   jax.ShapeDtypeStruct((1, 128), F32), jax.ShapeDtypeStruct((1, d), F32)],
        compiler_params=_cparams(("arbitrary",)),
    )(a, b, res, w, target)


def _mm_two_halves(a, b, *, name, tm):
    m, k = a.shape
    d = b.shape[0] // 2

    def body(a_ref, b_ref, lo_ref, hi_ref):
        av = a_ref[...].astype(BF16)
        lo_ref[...] = lax.dot_general(av, b_ref[0:d, :].astype(BF16), NT_DIMS, preferred_element_type=F32)
        hi_ref[...] = lax.dot_general(av, b_ref[d:2 * d, :].astype(BF16), NT_DIMS,
                                      preferred_element_type=F32).astype(BF16)

    row = pl.BlockSpec((tm, d), lambda i: (i, 0))
    return pl.pallas_call(
        body, name=name, grid=(m // tm,),
        in_specs=[pl.BlockSpec((tm, k), lambda i: (i, 0)), pl.BlockSpec((2 * d, k), lambda i: (0, 0))],
        out_specs=[row, row],
        out_shape=[jax.ShapeDtypeStruct((m, d), F32), jax.ShapeDtypeStruct((m, d), BF16)],
        compiler_params=_cparams(("parallel",)),
    )(a, b)


def _mm_norm_bwd(pairs, x, rstd, w, dres, *, name, tm, after=None):
    m = pairs[0][0].shape[0]
    d = pairs[0][1].shape[0]
    n_pairs = len(pairs)

    def body(*refs):
        i = 2 * n_pairs
        x_ref, r_ref, w_ref, d_ref = refs[i:i + 4]
        dx_ref, dxb_ref, dw_ref = refs[-3:]

        @pl.when(pl.program_id(0) == 0)
        def _():
            dw_ref[...] = jnp.zeros_like(dw_ref)

        g = _rows_product(refs[0], refs[1], True, None)
        for p in range(1, n_pairs):
            g = g + _rows_product(refs[2 * p], refs[2 * p + 1], True, None)
        r = r_ref[...]
        xhat = x_ref[...] * r
        gw = g * w_ref[...]
        dx = d_ref[...] + r * (gw - xhat * jnp.mean(gw * xhat, axis=1, keepdims=True))
        dx_ref[...] = dx
        dxb_ref[...] = dx.astype(BF16)
        dw_ref[...] += jnp.sum(g * xhat, axis=0, keepdims=True)

    row = pl.BlockSpec((tm, d), lambda i: (i, 0))
    vec = pl.BlockSpec((1, d), lambda i: (0, 0))
    ins, specs = [], []
    for a, b in pairs:
        k = a.shape[1]
        ins += [a, b]
        specs += [pl.BlockSpec((tm, k), lambda i: (i, 0)), pl.BlockSpec((d, k), lambda i: (0, 0))]
    ins += [x, rstd, w, dres]
    specs += [row, pl.BlockSpec((tm, 1), lambda i: (i, 0)), vec, row]
    if after is not None:
        ins.append(after)
        specs.append(pl.BlockSpec(memory_space=pl.ANY))
    return pl.pallas_call(
        body, name=name, grid=(m // tm,), in_specs=specs, out_specs=[row, row, vec],
        out_shape=[jax.ShapeDtypeStruct((m, d), F32), jax.ShapeDtypeStruct((m, d), BF16),
                   jax.ShapeDtypeStruct((1, d), F32)],
        compiler_params=_cparams(("arbitrary",)),
    )(*ins)


def _softplus(x):
    return jnp.maximum(x, 0.0) + jnp.log(1.0 + jnp.exp(-jnp.abs(x)))


def _prep(proj_a, bias128, alog128, bl, t):
    n = bl * t
    nch = t // CHUNK
    col0 = (SSD_WIDTH + CONV_CH) // 128

    def body(p_ref, b_ref, al_ref, dt_ref, gd_ref, ac_ref, act_ref, negc_ref):
        negc_ref[...] = jnp.zeros_like(negc_ref)
        row = lax.broadcasted_iota(jnp.int32, (CHUNK, CHUNK), 0)
        col = lax.broadcasted_iota(jnp.int32, (CHUNK, CHUNK), 1)
        tril = (row >= col).astype(F32)
        lane = lax.broadcasted_iota(jnp.int32, (1, 128), 1)
        head_lanes = lane < 16
        a_row = -jnp.exp(al_ref[...])
        carry = jnp.zeros((1, 128), F32)
        for ci in range(nch):
            rows = slice(ci * CHUNK, (ci + 1) * CHUNK)
            xv = p_ref[rows, :] + b_ref[...]
            sp = _softplus(xv)
            acum = jnp.dot(tril, a_row * sp, precision=HIGHEST, preferred_element_type=F32)
            c = jnp.dot(tril, -_softplus(-xv), precision=HIGHEST, preferred_element_type=F32) + carry
            carry = c[CHUNK - 1:CHUNK, :]
            dt_ref[rows, :] = jnp.where(head_lanes, sp, 0.0)
            gd_ref[rows, :] = jnp.where(head_lanes, jax.nn.sigmoid(xv),
                                        jnp.where(lane < 32, jax.nn.sigmoid(-xv), 0.0))
            ac_ref[rows, :] = jnp.where(head_lanes, acum, 0.0)
            act_ref[:, rows] = jnp.transpose(acum)[0:16, :]
            c_t = jnp.transpose(c)
            for hp in range(8):
                negc_ref[hp, 0:2, rows] = -c_t[16 + 2 * hp:18 + 2 * hp, :]

    o128 = pl.BlockSpec((t, 128), lambda b: (b, 0))
    v128 = pl.BlockSpec((1, 128), lambda b: (0, 0))
    w128 = jax.ShapeDtypeStruct((n, 128), F32)
    return pl.pallas_call(
        body, name="head_scalars", grid=(bl,),
        in_specs=[pl.BlockSpec((t, 128), lambda b: (b, col0)), v128, v128],
        out_specs=[o128, o128, o128, pl.BlockSpec((16, t), lambda b: (0, b)),
                   pl.BlockSpec((None, 8, 8, t), lambda b: (b, 0, 0, 0))],
        out_shape=[w128, w128, w128, jax.ShapeDtypeStruct((16, n), F32),
                   jax.ShapeDtypeStruct((bl, 8, 8, t), F32)],
        compiler_params=_cparams(("parallel",)),
    )(proj_a, bias128, alog128)


def _fpost(dc, gate_d, ddt, dpa, bl, t):
    n = bl * t
    nch = t // CHUNK
    col0 = (SSD_WIDTH + CONV_CH) // 128

    def body(dc_ref, gd_ref, ddt_ref, dpa_in, out_ref, db_ref):
        @pl.when(pl.program_id(0) == 0)
        def _():
            db_ref[...] = jnp.zeros_like(db_ref)

        row = lax.broadcasted_iota(jnp.int32, (CHUNK, CHUNK), 0)
        col = lax.broadcasted_iota(jnp.int32, (CHUNK, CHUNK), 1)
        triu = (row <= col).astype(F32)
        lane = lax.broadcasted_iota(jnp.int32, (1, 128), 1)
        gate_lanes = (lane >= 16) & (lane < 32)
        carry = jnp.zeros((1, 128), F32)
        db = jnp.zeros((1, 128), F32)
        for ci in reversed(range(nch)):
            rows = slice(ci * CHUNK, (ci + 1) * CHUNK)
            dlf = jnp.dot(triu, dc_ref[rows, :], precision=HIGHEST, preferred_element_type=F32) + carry
            carry = dlf[0:1, :]
            df = jnp.where(gate_lanes, dlf * gd_ref[rows, :], 0.0)
            out_ref[rows, :] = (ddt_ref[rows, :] + df).astype(BF16)
            db = db + jnp.sum(df, axis=0, keepdims=True)
        db_ref[...] += db[:, 16:32]

    blk = pl.BlockSpec((t, 128), lambda b: (b, 0))
    return pl.pallas_call(
        body, name="forget_gate_bwd", grid=(bl,),
        in_specs=[blk, blk, blk, ANY],
        out_specs=[pl.BlockSpec((t, 128), lambda b: (b, col0)), pl.BlockSpec((1, 16), lambda b: (0, 0))],
        out_shape=[jax.ShapeDtypeStruct(dpa.shape, dpa.dtype), jax.ShapeDtypeStruct((1, 16), F32)],
        input_output_aliases={3: 0},
        compiler_params=_cparams(("arbitrary",)),
    )(dc, gate_d, ddt, dpa)


CONV_TILE = 256
CONV_ROWS = 256


def _conv_taps(u_ref, i):
    r0 = pl.multiple_of(i * CONV_ROWS, CONV_ROWS)
    cur = u_ref[pl.ds(r0, CONV_ROWS), :]
    p0 = pl.multiple_of(jnp.maximum(r0 - 8, 0), 8)
    prev = jnp.where(i > 0, u_ref[pl.ds(p0, 8), :], 0.0)
    cat = jnp.concatenate([prev, cur], axis=0)
    return r0, [cur] + [pltpu.roll(cat, s, 0)[8:, :] for s in (1, 2, 3)]


def _conv_fwd(proj_a, conv_w, conv_b, bl, t):
    n = bl * t
    nct = CONV_CH // CONV_TILE
    c0 = SSD_WIDTH // CONV_TILE

    def body(u_ref, w_ref, b_ref, o_ref, d_ref):
        w = w_ref[...]
        bias = b_ref[...]

        def chunk(i, carry):
            r0, taps = _conv_taps(u_ref, i)
            pre = bias + w[3:4, :] * taps[0]
            for s in (1, 2, 3):
                pre = pre + w[3 - s:4 - s, :] * taps[s]
            sg = jax.nn.sigmoid(pre)
            o_ref[pl.ds(r0, CONV_ROWS), :] = pre * sg
            d_ref[pl.ds(r0, CONV_ROWS), :] = (sg * (1.0 + pre * (1.0 - sg))).astype(BF16)
            return carry

        lax.fori_loop(0, t // CONV_ROWS, chunk, 0)

    out = pl.BlockSpec((t, CONV_TILE), lambda b, c: (b, c))
    return pl.pallas_call(
        body, name="conv_silu_fwd", grid=(bl, nct),
        in_specs=[pl.BlockSpec((t, CONV_TILE), lambda b, c: (b, c0 + c)),
                  pl.BlockSpec((4, CONV_TILE), lambda b, c: (0, c)),
                  pl.BlockSpec((1, CONV_TILE), lambda b, c: (0, c))],
        out_specs=[out, out],
        out_shape=[jax.ShapeDtypeStruct((n, CONV_CH), F32), jax.ShapeDtypeStruct((n, CONV_CH), BF16)],
        compiler_params=_cparams(("parallel", "parallel")),
    )(proj_a, conv_w, conv_b)


def _conv_bwd(dxc, dsilu, proj_a, conv_w, dpa, bl, t):
    nct = CONV_CH // CONV_TILE
    c0 = SSD_WIDTH // CONV_TILE
    nrc = t // CONV_ROWS

    def body(g_ref, s_ref, u_ref, w_ref, dpa_in, du_ref, dw_ref, db_ref, dp_scr):
        @pl.when(pl.program_id(1) == 0)
        def _():
            dw_ref[...] = jnp.zeros_like(dw_ref)
            db_ref[...] = jnp.zeros_like(db_ref)

        w = w_ref[...]
        dp_scr[pl.ds(t, 8), :] = jnp.zeros((8, CONV_TILE), F32)

        def chunk1(i, carry):
            dw0, dw1, dw2, dw3, db = carry
            r0, taps = _conv_taps(u_ref, i)
            dpre = g_ref[pl.ds(r0, CONV_ROWS), :] * s_ref[pl.ds(r0, CONV_ROWS), :].astype(F32)
            dp_scr[pl.ds(r0, CONV_ROWS), :] = dpre
            dw3 = dw3 + jnp.sum(dpre * taps[0], axis=0, keepdims=True)
            dw2 = dw2 + jnp.sum(dpre * taps[1], axis=0, keepdims=True)
            dw1 = dw1 + jnp.sum(dpre * taps[2], axis=0, keepdims=True)
            dw0 = dw0 + jnp.sum(dpre * taps[3], axis=0, keepdims=True)
            db = db + jnp.sum(dpre, axis=0, keepdims=True)
            return dw0, dw1, dw2, dw3, db

        z = jnp.zeros((1, CONV_TILE), F32)
        dw0, dw1, dw2, dw3, db = lax.fori_loop(0, nrc, chunk1, (z, z, z, z, z))
        dw_ref[...] += jnp.concatenate([dw0, dw1, dw2, dw3], axis=0)
        db_ref[...] += db

        def chunk2(i, carry):
            r0 = pl.multiple_of(i * CONV_ROWS, CONV_ROWS)
            cat = dp_scr[pl.ds(r0, CONV_ROWS + 8), :]
            du = w[3:4, :] * cat[:CONV_ROWS, :]
            for s in (1, 2, 3):
                du = du + w[3 - s:4 - s, :] * pltpu.roll(cat, CONV_ROWS + 8 - s, 0)[:CONV_ROWS, :]
            du_ref[pl.ds(r0, CONV_ROWS), :] = du.astype(BF16)
            return carry

        lax.fori_loop(0, nrc, chunk2, 0)

    tile = pl.BlockSpec((t, CONV_TILE), lambda c, b: (b, c))
    return pl.pallas_call(
        body, name="conv_silu_bwd", grid=(nct, bl),
        in_specs=[tile, tile, pl.BlockSpec((t, CONV_TILE), lambda c, b: (b, c0 + c)),
                  pl.BlockSpec((4, CONV_TILE), lambda c, b: (0, c)), ANY],
        out_specs=[pl.BlockSpec((t, CONV_TILE), lambda c, b: (b, c0 + c)),
                   pl.BlockSpec((4, CONV_TILE), lambda c, b: (0, c)),
                   pl.BlockSpec((1, CONV_TILE), lambda c, b: (0, c))],
        out_shape=[jax.ShapeDtypeStruct(dpa.shape, dpa.dtype), jax.ShapeDtypeStruct((4, CONV_CH), F32),
                   jax.ShapeDtypeStruct((1, CONV_CH), F32)],
        input_output_aliases={4: 0},
        scratch_shapes=[pltpu.VMEM((t + 8, CONV_TILE), F32)],
        compiler_params=_cparams(("parallel", "arbitrary")),
    )(dxc, dsilu, proj_a, conv_w, dpa)


SSD_FWD_CHUNKS = 4
SSD_BWD_CHUNKS = 1
NT_DIMS = (((1,), (1,)), ((), ()))
TN_DIMS = (((0,), (0,)), ((), ()))


def _dot(a, b, dims=None):
    if dims is None:
        return jnp.dot(a, b, preferred_element_type=F32)
    return lax.dot_general(a, b, dims, preferred_element_type=F32)


def _head_expander():
    r = lax.broadcasted_iota(jnp.int32, (128, SSD_WIDTH), 0)
    c = lax.broadcasted_iota(jnp.int32, (128, SSD_WIDTH), 1)
    return ((c // HEAD_DIM == r % 16) & (r < 48)).astype(BF16)


def _spread(v128, expander):
    hi = v128.astype(BF16).astype(F32)
    r1 = v128 - hi
    mid = r1.astype(BF16).astype(F32)
    lo = (r1 - mid).astype(BF16).astype(F32)
    packed = (hi + pltpu.roll(mid, 16, 1) + pltpu.roll(lo, 32, 1)).astype(BF16)
    return jnp.dot(packed, expander, preferred_element_type=F32)


def _head_sums(v1024, expander):
    hi = v1024.astype(BF16)
    lo = (v1024 - hi.astype(F32)).astype(BF16)
    heads = jnp.where(lax.broadcasted_iota(jnp.int32, expander.shape, 0) < 16, expander, jnp.zeros_like(expander))
    return _dot(hi, heads, NT_DIMS) + _dot(lo, heads, NT_DIMS)


def _ssd_fwd(xc, proj_a, dt, acum, acum_t, dskip_e, norm_w, bl, t):
    n = bl * t
    nch = t // CHUNK
    L = CHUNK

    def body(xc_blk, z_blk, dt_blk, ac_blk, act_blk, dsk_ref, nw_ref, ys_blk, yp_blk, hp_blk, h_scr, y_scr, x_scr):
        @pl.when(pl.program_id(1) == 0)
        def _():
            h_scr[...] = jnp.zeros_like(h_scr)

        for sub in range(SSD_FWD_CHUNKS):
            rows = pl.ds(sub * L, L)
            chunk(xc_blk.at[rows, :], z_blk.at[rows, :], dt_blk.at[rows, :], ac_blk.at[rows, :], act_blk.at[:, rows],
                  dsk_ref, nw_ref, ys_blk.at[rows, :], yp_blk.at[rows, :], hp_blk.at[sub], h_scr, y_scr, x_scr)

    def chunk(xc_ref, z_ref, dt_ref, ac_ref, act_ref, dsk_ref, nw_ref, ys_ref, yp_ref, hp_ref, h_scr, y_scr, x_scr):
        row = lax.broadcasted_iota(jnp.int32, (L, L), 0)
        col = lax.broadcasted_iota(jnp.int32, (L, L), 1)
        causal = row >= col
        lane128 = lax.broadcasted_iota(jnp.int32, (1, L), 1)
        expander = _head_expander()
        ac_all = ac_ref[...]
        act_all = act_ref[...]
        ac_e = _spread(ac_all, expander)
        e_in = jnp.exp(ac_e)
        dec = jnp.exp(ac_e[L - 1:L, :] - ac_e)
        xs_all = xc_ref[:, 0:SSD_WIDTH]
        x_all = xs_all * _spread(dt_ref[...], expander)
        x_scr[...] = x_all.astype(BF16)
        hp_all = h_scr[...]
        hp_ref[...] = hp_all
        for g in range(2):
            gs = slice(g * 512, (g + 1) * 512)
            bg = xc_ref[:, SSD_WIDTH + g * 128:SSD_WIDTH + (g + 1) * 128].astype(BF16)
            cg = xc_ref[:, SSD_WIDTH + 256 + g * 128:SSD_WIDTH + 256 + (g + 1) * 128].astype(BF16)
            gmat = _dot(cg, bg, NT_DIMS)
            y_off = _dot(cg, hp_all[gs, :].astype(BF16), NT_DIMS) * e_in[:, gs] + dsk_ref[:, gs] * xs_all[:, gs]
            s_new = _dot((x_all[:, gs] * dec[:, gs]).astype(BF16), bg, TN_DIMS)
            for pr in range(4):
                pair = slice((g * 4 + pr) * 128, (g * 4 + pr + 1) * 128)
                x_pair = x_scr[:, pair]
                y_pair = y_off[:, pr * 128:(pr + 1) * 128]
                for j in range(2):
                    h = g * 8 + 2 * pr + j
                    sl = slice(h * HEAD_DIM, (h + 1) * HEAD_DIM)
                    r = 2 * pr + j
                    ldec = jnp.exp(jnp.where(causal, ac_all[:, h:h + 1] - act_all[h:h + 1, :], NEG))
                    x_head = jnp.where((lane128 < HEAD_DIM) == (j == 0), x_pair, jnp.zeros_like(x_pair))
                    y_pair = y_pair + _dot((gmat * ldec).astype(BF16), x_head)
                    elast = jnp.exp(ac_all[L - 1:L, h:h + 1])
                    h_scr[sl, :] = elast * hp_all[sl, :] + s_new[r * HEAD_DIM:(r + 1) * HEAD_DIM, :]
                y_scr[:, pair] = y_pair
        y = y_scr[...]
        yp_ref[...] = y
        zv = z_ref[...]
        yg = y * (zv * jax.nn.sigmoid(zv))
        for g in range(2):
            gs = slice(g * 512, (g + 1) * 512)
            grp = yg[:, gs]
            rstd = lax.rsqrt(jnp.mean(grp * grp, axis=1, keepdims=True) + EPS)
            ys_ref[:, gs] = (grp * rstd * nw_ref[:, gs]).astype(BF16)

    cps = SSD_FWD_CHUNKS
    steps = nch // cps
    rb = lambda b, c: (b * steps + c, 0)
    v1k = pl.BlockSpec((1, SSD_WIDTH), lambda b, c: (0, 0))
    return pl.pallas_call(
        body, name="ssd_fwd", grid=(bl, steps),
        in_specs=[pl.BlockSpec((cps * L, CONV_CH), rb), pl.BlockSpec((cps * L, SSD_WIDTH), rb),
                  pl.BlockSpec((cps * L, 128), rb), pl.BlockSpec((cps * L, 128), rb),
                  pl.BlockSpec((16, cps * L), lambda b, c: (0, b * steps + c)), v1k, v1k],
        out_specs=[pl.BlockSpec((cps * L, SSD_WIDTH), rb), pl.BlockSpec((cps * L, SSD_WIDTH), rb),
                   pl.BlockSpec((cps, SSD_WIDTH, SSD_STATE), lambda b, c: (b * steps + c, 0, 0))],
        out_shape=[jax.ShapeDtypeStruct((n, SSD_WIDTH), BF16), jax.ShapeDtypeStruct((n, SSD_WIDTH), F32),
                   jax.ShapeDtypeStruct((bl * nch, SSD_WIDTH, SSD_STATE), F32)],
        scratch_shapes=[pltpu.VMEM((SSD_WIDTH, SSD_STATE), F32), pltpu.VMEM((L, SSD_WIDTH), F32),
                        pltpu.VMEM((L, SSD_WIDTH), BF16)],
        compiler_params=_cparams(("parallel", "arbitrary")),
    )(xc, proj_a, dt, acum, acum_t, dskip_e, norm_w)


def _ssd_bwd(dys, xc, proj_a, ypre, hprev, dt, gate_d, acum, acum_t, alog128, dskip_e, norm_w, bl, t):
    n = bl * t
    nch = t // CHUNK
    L = CHUNK

    def body(dys_blk, xc_blk, z_blk, yp_blk, hp_blk, dt_blk, gd_blk, ac_blk, act_blk, al_ref, dsk_ref, nw_ref,
             dxc_blk, dz_blk, ddt_blk, dnw_ref, dsk16_ref, da16_ref, db16_ref,
             dh_scr, dy_scr, x_scr, dx_scr, red_scr):
        first = (pl.program_id(0) == 0) & (pl.program_id(1) == 0)

        @pl.when(first)
        def _():
            dnw_ref[...] = jnp.zeros_like(dnw_ref)
            dsk16_ref[...] = jnp.zeros_like(dsk16_ref)
            da16_ref[...] = jnp.zeros_like(da16_ref)
            db16_ref[...] = jnp.zeros_like(db16_ref)

        @pl.when(pl.program_id(1) == 0)
        def _():
            dh_scr[...] = jnp.zeros_like(dh_scr)

        for sub in reversed(range(SSD_BWD_CHUNKS)):
            rows = pl.ds(sub * L, L)
            chunk(dys_blk.at[rows, :], xc_blk.at[rows, :], z_blk.at[rows, :], yp_blk.at[rows, :], hp_blk.at[sub],
                  dt_blk.at[rows, :], gd_blk.at[rows, :], ac_blk.at[rows, :], act_blk.at[:, rows], al_ref, dsk_ref,
                  nw_ref, dxc_blk.at[rows, :], dz_blk.at[rows, :], ddt_blk.at[rows, :], dnw_ref, dsk16_ref, da16_ref,
                  db16_ref, dh_scr, dy_scr, x_scr, dx_scr, red_scr)

    def chunk(dys_ref, xc_ref, z_ref, yp_ref, hp_ref, dt_ref, gd_ref, ac_ref, act_ref, al_ref, dsk_ref, nw_ref,
              dxc_ref, dz_ref, ddt_ref, dnw_ref, dsk16_ref, da16_ref, db16_ref,
              dh_scr, dy_scr, x_scr, dx_scr, red_scr):
        y = yp_ref[...]
        zv = z_ref[...]
        sz = jax.nn.sigmoid(zv)
        gate = zv * sz
        yg = y * gate
        dout = dys_ref[...]
        nw = nw_ref[...]
        for g in range(2):
            gs = slice(g * 512, (g + 1) * 512)
            grp = yg[:, gs]
            rstd = lax.rsqrt(jnp.mean(grp * grp, axis=1, keepdims=True) + EPS)
            ghat = grp * rstd
            dnw_ref[:, gs] += jnp.sum(dout[:, gs] * ghat, axis=0, keepdims=True)
            gw = dout[:, gs] * nw[:, gs]
            dyg = rstd * (gw - ghat * jnp.mean(gw * ghat, axis=1, keepdims=True))
            dy_scr[:, gs] = dyg * gate[:, gs]
            dz_ref[:, gs] = (dyg * y[:, gs] * (sz[:, gs] * (1.0 + zv[:, gs] * (1.0 - sz[:, gs])))).astype(BF16)

        row = lax.broadcasted_iota(jnp.int32, (L, L), 0)
        col = lax.broadcasted_iota(jnp.int32, (L, L), 1)
        causal = row >= col
        lane128 = lax.broadcasted_iota(jnp.int32, (1, L), 1)
        rows128 = lax.broadcasted_iota(jnp.int32, (L, 1), 0)
        last_row = rows128 == (L - 1)
        expander = _head_expander()
        ac_all = ac_ref[...]
        act_all = act_ref[...]
        dt_all = dt_ref[...]
        dt_e = _spread(dt_all, expander)
        ac_e = _spread(ac_all, expander)
        e_in = jnp.exp(ac_e)
        dec = jnp.exp(ac_e[L - 1:L, :] - ac_e)
        xs_all = xc_ref[:, 0:SSD_WIDTH]
        x_all = xs_all * dt_e
        x_scr[...] = x_all.astype(BF16)
        dy_all = dy_scr[...]
        hp_all = hp_ref[...]
        ds_all = dh_scr[...]
        dsk_cols = jnp.sum(dy_all * xs_all, axis=0, keepdims=True)
        dac = jnp.zeros((L, L), F32)
        dac_row = jnp.zeros((L, L), F32)
        ddec_cols = []
        for g in range(2):
            gs = slice(g * 512, (g + 1) * 512)
            bsl = slice(SSD_WIDTH + g * 128, SSD_WIDTH + (g + 1) * 128)
            csl = slice(SSD_WIDTH + 256 + g * 128, SSD_WIDTH + 256 + (g + 1) * 128)
            bg = xc_ref[:, bsl].astype(BF16)
            cg = xc_ref[:, csl].astype(BF16)
            gmat = _dot(cg, bg, NT_DIMS)
            hpb = hp_all[gs, :].astype(BF16)
            dsb = ds_all[gs, :].astype(BF16)
            ch = _dot(cg, hpb, NT_DIMS)
            dye = dy_all[:, gs] * e_in[:, gs]
            dyeb = dye.astype(BF16)
            dc_acc = _dot(dyeb, hpb)
            dhp = _dot(dyeb, cg, TN_DIMS)
            dxd = _dot(bg, dsb, NT_DIMS)
            db_acc = _dot((x_all[:, gs] * dec[:, gs]).astype(BF16), dsb)
            ddec = dxd * x_all[:, gs] * dec[:, gs]
            ddec_cols.append(jnp.sum(ddec, axis=0, keepdims=True))
            dx_inter = dxd * dec[:, gs]
            red_scr[:, gs] = dye * ch - ddec
            dg_sum = jnp.zeros((L, L), F32)
            for pr in range(4):
                pair = slice((g * 4 + pr) * 128, (g * 4 + pr + 1) * 128)
                x_pair = x_scr[:, pair]
                dy_pair = dy_scr[:, pair].astype(BF16)
                dx_pair = dx_inter[:, pr * 128:(pr + 1) * 128]
                for j in range(2):
                    h = g * 8 + 2 * pr + j
                    r = 2 * pr + j
                    sl = slice(h * HEAD_DIM, (h + 1) * HEAD_DIM)
                    onehot_w = lane128 == h
                    ldec = jnp.exp(jnp.where(causal, ac_all[:, h:h + 1] - act_all[h:h + 1, :], NEG))
                    mf = gmat * ldec
                    dyb = jnp.where((lane128 < HEAD_DIM) == (j == 0), dy_pair, jnp.zeros_like(dy_pair))
                    dm = _dot(dyb, x_pair, NT_DIMS)
                    dx_pair = dx_pair + _dot(mf.astype(BF16), dyb, TN_DIMS)
                    dg_sum = dg_sum + dm * ldec
                    wmat = dm * mf
                    elast = jnp.exp(ac_all[L - 1:L, h:h + 1])
                    hp_h = hp_all[sl, :]
                    ds_h = ds_all[sl, :]
                    extra = elast * jnp.sum(jnp.sum(hp_h * ds_h, axis=1, keepdims=True), axis=0, keepdims=True)
                    dac = dac + jnp.where(onehot_w,
                                          jnp.sum(wmat, axis=1, keepdims=True) + jnp.where(last_row, extra, 0.0), 0.0)
                    dac_row = dac_row + jnp.where(rows128 == h, -jnp.sum(wmat, axis=0, keepdims=True), 0.0)
                    dh_scr[sl, :] = elast * ds_h + dhp[r * HEAD_DIM:(r + 1) * HEAD_DIM, :]
                dx_scr[:, pair] = dx_pair
            dgb = dg_sum.astype(BF16)
            dxc_ref[:, csl] = dc_acc + _dot(dgb, bg)
            dxc_ref[:, bsl] = db_acc + _dot(dgb, cg, TN_DIMS)
        dx_all = dx_scr[...]
        dxc_ref[:, 0:SSD_WIDTH] = dx_all * dt_e + dsk_ref[...] * dy_all
        red = red_scr[...]
        dac_slab = _head_sums(red, expander)
        ddec_tot = _head_sums(jnp.broadcast_to(jnp.concatenate(ddec_cols, axis=1), (8, SSD_WIDTH)), expander)
        ddt_x = _head_sums(dx_all * xs_all, expander)
        dsk16_ref[...] += _head_sums(jnp.broadcast_to(dsk_cols, (8, SSD_WIDTH)), expander)[0:1, 0:16]
        dac = dac + dac_slab + jnp.transpose(dac_row) + jnp.where(last_row, ddec_tot[0:1, :], 0.0)
        triu = (row <= col).astype(F32)
        da = jnp.dot(triu, dac, precision=HIGHEST, preferred_element_type=F32)
        a_row = -jnp.exp(al_ref[...])
        ddt = jnp.where(lane128 < 16, (ddt_x + da * a_row) * gd_ref[...], 0.0)
        ddt_ref[...] = ddt
        da16_ref[...] += (jnp.sum(da * dt_all, axis=0, keepdims=True) * a_row)[:, 0:16]
        db16_ref[...] += jnp.sum(ddt, axis=0, keepdims=True)[:, 0:16]

    cps = SSD_BWD_CHUNKS
    steps = nch // cps
    rb = lambda b, c: (b * steps + steps - 1 - c, 0)
    v1k = pl.BlockSpec((1, SSD_WIDTH), lambda b, c: (0, 0))
    v16 = pl.BlockSpec((1, 16), lambda b, c: (0, 0))
    v128 = pl.BlockSpec((1, 128), lambda b, c: (0, 0))
    wide = pl.BlockSpec((cps * L, SSD_WIDTH), rb)
    s128 = pl.BlockSpec((cps * L, 128), rb)
    return pl.pallas_call(
        body, name="ssd_bwd", grid=(bl, steps),
        in_specs=[wide, pl.BlockSpec((cps * L, CONV_CH), rb), wide, wide,
                  pl.BlockSpec((cps, SSD_WIDTH, SSD_STATE), lambda b, c: (b * steps + steps - 1 - c, 0, 0)),
                  s128, s128, s128, pl.BlockSpec((16, cps * L), lambda b, c: (0, b * steps + steps - 1 - c)),
                  v128, v1k, v1k],
        out_specs=[pl.BlockSpec((cps * L, CONV_CH), rb), wide, s128, v1k, v16, v16, v16],
        out_shape=[jax.ShapeDtypeStruct((n, CONV_CH), F32), jax.ShapeDtypeStruct((n, PA_WIDTH), BF16),
                   jax.ShapeDtypeStruct((n, 128), F32), jax.ShapeDtypeStruct((1, SSD_WIDTH), F32),
                   jax.ShapeDtypeStruct((1, 16), F32), jax.ShapeDtypeStruct((1, 16), F32),
                   jax.ShapeDtypeStruct((1, 16), F32)],
        scratch_shapes=[pltpu.VMEM((SSD_WIDTH, SSD_STATE), F32), pltpu.VMEM((L, SSD_WIDTH), F32),
                        pltpu.VMEM((L, SSD_WIDTH), BF16), pltpu.VMEM((L, SSD_WIDTH), F32),
                        pltpu.VMEM((L, SSD_WIDTH), F32)],
        compiler_params=_cparams(("arbitrary", "arbitrary")),
    )(dys, xc, proj_a, ypre, hprev, dt, gate_d, acum, acum_t, alog128, dskip_e, norm_w)


def _attn_fwd(qkv, negc, bl, t):
    n = bl * t
    tb_ = min(t, ATT_FWD_BLOCK)
    nb = t // tb_
    scale2 = LOG2E / math.sqrt(HEAD_DIM)

    def body(q_ref, k_ref, v_ref, c_ref, o_ref, lse_ref, v0_scr, v1_scr, k0_scr, k1_scr):
        row = lax.broadcasted_iota(jnp.int32, (tb_, tb_), 0)
        col = lax.broadcasted_iota(jnp.int32, (tb_, tb_), 1)
        causal = row >= col
        lane = lax.broadcasted_iota(jnp.int32, (1, 128), 1)
        v_pair = v_ref[...].astype(F32)
        k_pair = k_ref[...]
        v_scrs = (v0_scr, v1_scr)
        k_scrs = (k0_scr, k1_scr)
        for j in range(2):
            v_head = v_pair if j == 0 else pltpu.roll(v_pair, HEAD_DIM, 1)
            v_scrs[j][...] = jnp.where(lane < HEAD_DIM, v_head, jnp.where(lane == HEAD_DIM, 1.0, 0.0)).astype(BF16)
            k_scrs[j][...] = jnp.where((lane < HEAD_DIM) == (j == 0), k_pair, jnp.zeros_like(k_pair))
        for qi in range(nb):
            r0, lk = qi * tb_, (qi + 1) * tb_
            for j in range(2):
                sl = slice(j * HEAD_DIM, (j + 1) * HEAD_DIM)
                s = _dot(q_ref[r0:lk, :], k_scrs[j][0:lk, :], NT_DIMS) * scale2 + c_ref[j:j + 1, 0:lk] * LOG2E
                tail = jnp.where(causal, s[:, r0:lk], NEG)
                s = tail if qi == 0 else jnp.concatenate([s[:, 0:r0], tail], axis=1)
                m = jnp.max(s, axis=1, keepdims=True)
                p = jnp.exp2(s - m)
                acc = _dot(p.astype(BF16), v_scrs[j][0:lk, :])
                l = acc[:, HEAD_DIM:HEAD_DIM + 1]
                o_ref[r0:lk, sl] = (acc[:, 0:HEAD_DIM] / l).astype(BF16)
                lse_ref[r0:lk, sl] = jnp.broadcast_to(m + jnp.log(l) * LOG2E, (tb_, HEAD_DIM))

    blk = lambda off: pl.BlockSpec((t, 128), lambda b, hp: (b, off + hp))
    return pl.pallas_call(
        body, name="fox_attn_fwd", grid=(bl, 8),
        in_specs=[blk(0), blk(8), blk(16), pl.BlockSpec((None, None, 8, t), lambda b, hp: (b, hp, 0, 0))],
        out_specs=[blk(0), blk(0)],
        out_shape=[jax.ShapeDtypeStruct((n, ATT_WIDTH), BF16), jax.ShapeDtypeStruct((n, ATT_WIDTH), F32)],
        scratch_shapes=[pltpu.VMEM((t, 128), BF16)] * 4,
        compiler_params=_cparams(("parallel", "parallel")),
    )(qkv, qkv, qkv, negc)


def _attn_bwd(qkv, do, o, lse, negc, after, bl, t):
    n = bl * t
    tb_ = min(t, ATT_BWD_BLOCK)
    nb = t // tb_
    scale = 1.0 / math.sqrt(HEAD_DIM)
    scale2 = LOG2E * scale

    def body(q_ref, k_ref, v_ref, do_ref, o_ref, lse_ref, c_ref, after_ref, dq_ref, dk_ref, dv_ref, dc_ref,
             dq0_scr, delta_scr, dq1_scr, qt0_scr, qt1_scr, dot_scr, dkt0_scr, dkt1_scr, dvt_scr):
        row = lax.broadcasted_iota(jnp.int32, (tb_, tb_), 0)
        col = lax.broadcasted_iota(jnp.int32, (tb_, tb_), 1)
        causal = row >= col
        lane = lax.broadcasted_iota(jnp.int32, (1, 128), 1)
        dq_scrs = (dq0_scr, dq1_scr)
        qt_scrs = (qt0_scr, qt1_scr)
        dkt_scrs = (dkt0_scr, dkt1_scr)
        dq0_scr[...] = jnp.zeros_like(dq0_scr)
        dq1_scr[...] = jnp.zeros_like(dq1_scr)
        dc_ref[...] = jnp.zeros_like(dc_ref)
        q_t = jnp.transpose(q_ref[...].astype(F32))
        ones_row = jnp.where(lax.broadcasted_iota(jnp.int32, (8, t), 0) == 0, 1.0, 0.0)
        for j in range(2):
            qt_scrs[j][...] = jnp.concatenate(
                [q_t[j * HEAD_DIM:(j + 1) * HEAD_DIM, :], ones_row, jnp.zeros((HEAD_DIM - 8, t), F32)],
                axis=0).astype(BF16)
        dot_scr[...] = jnp.transpose(do_ref[...].astype(F32)).astype(BF16)
        prod = do_ref[...].astype(F32) * o_ref[...].astype(F32)
        for j in range(2):
            sl = slice(j * HEAD_DIM, (j + 1) * HEAD_DIM)
            delta_scr[:, sl] = jnp.broadcast_to(jnp.sum(prod[:, sl], axis=1, keepdims=True), (t, HEAD_DIM))
        for kj in range(nb):
            r0, r1 = kj * tb_, (kj + 1) * tb_
            k_blk = k_ref[r0:r1, :]
            v_blk = v_ref[r0:r1, :]
            k_pair = k_blk.astype(F32)
            for j in range(2):
                sl = slice(j * HEAD_DIM, (j + 1) * HEAD_DIM)
                one = slice(j * HEAD_DIM, j * HEAD_DIM + 1)
                own = (lane < HEAD_DIM) == (j == 0)
                k_head = k_pair if j == 0 else pltpu.roll(k_pair, HEAD_DIM, 1)
                k_ones = jnp.where(lane < HEAD_DIM, k_head, jnp.where(lane == HEAD_DIM, 1.0, 0.0)).astype(BF16)
                s = (_dot(q_ref[r0:t, :], jnp.where(own, k_blk, jnp.zeros_like(k_blk)), NT_DIMS) * scale2
                     + c_ref[j:j + 1, r0:r1] * LOG2E)
                head = jnp.where(causal, s[0:tb_, :], NEG)
                s = head if kj == nb - 1 else jnp.concatenate([head, s[tb_:, :]], axis=0)
                p = jnp.exp2(s - lse_ref[r0:t, one])
                dp = _dot(do_ref[r0:t, :], jnp.where(own, v_blk, jnp.zeros_like(v_blk)), NT_DIMS)
                ds = p * (dp - delta_scr[r0:t, one])
                dsb = ds.astype(BF16)
                dvt_scr[sl, r0:r1] = _dot(dot_scr[sl, r0:t], p.astype(BF16))
                dkt_scrs[j][:, r0:r1] = _dot(qt_scrs[j][:, r0:t], dsb)
                dq_scrs[j][r0:t, :] += _dot(dsb, k_ones)
        dv_ref[...] = jnp.transpose(dvt_scr[...]).astype(BF16)
        for j in range(2):
            sl = slice(j * HEAD_DIM, (j + 1) * HEAD_DIM)
            acc = dq_scrs[j][...]
            dkt = dkt_scrs[j][...]
            dq_ref[:, sl] = (acc[:, 0:HEAD_DIM] * scale).astype(BF16)
            dk_ref[:, sl] = (jnp.transpose(dkt)[:, 0:HEAD_DIM] * scale).astype(BF16)
            dc_ref[j:j + 1, :] = jnp.transpose(acc)[HEAD_DIM:HEAD_DIM + 1, :] - dkt[HEAD_DIM:HEAD_DIM + 1, :]

    blk = lambda off: pl.BlockSpec((t, 128), lambda b, hp: (b, off + hp))
    cblk = pl.BlockSpec((None, None, 8, t), lambda b, hp: (b, hp, 0, 0))
    return pl.pallas_call(
        body, name="fox_attn_bwd", grid=(bl, 8),
        in_specs=[blk(0), blk(8), blk(16), blk(0), blk(0), blk(0), cblk, ANY],
        out_specs=[blk(0), blk(0), blk(0), cblk],
        out_shape=[jax.ShapeDtypeStruct((n, ATT_WIDTH), BF16)] * 3 + [jax.ShapeDtypeStruct((bl, 8, 8, t), F32)],
        scratch_shapes=[pltpu.VMEM((t, 128), F32), pltpu.VMEM((t, 128), F32), pltpu.VMEM((t, 128), F32),
                        pltpu.VMEM((128, t), BF16), pltpu.VMEM((128, t), BF16), pltpu.VMEM((128, t), BF16),
                        pltpu.VMEM((128, t), F32), pltpu.VMEM((128, t), F32), pltpu.VMEM((128, t), F32)],
        compiler_params=_cparams(("parallel", "parallel")),
    )(qkv, qkv, qkv, do, o, lse, negc, after)


def _adamw(w, g, m, v, *, name):
    lead = w.ndim == 3
    r, c = w.shape[-2:]
    tr = _pick(r, (256, IN_SHARD // 3, 128, 64, 32, 16, 8))
    bc1 = 1.0 - ADAM_B1 ** ADAM_STEP
    bc2 = 1.0 - ADAM_B2 ** ADAM_STEP

    def body(w_ref, g_ref, m_ref, v_ref, d_ref, nm_ref, nv_ref):
        gv = g_ref[...]
        mn = ADAM_B1 * m_ref[...] + (1.0 - ADAM_B1) * gv
        vn = ADAM_B2 * v_ref[...] + (1.0 - ADAM_B2) * (gv * gv)
        m_hat = mn / bc1
        v_hat = vn / bc2
        d_ref[...] = -ADAM_LR * (m_hat / (jnp.sqrt(v_hat) + ADAM_EPS) + ADAM_WD * w_ref[...])
        nm_ref[...] = mn
        nv_ref[...] = vn

    flat = pl.BlockSpec((tr, c), lambda i: (i, 0))
    blk = pl.BlockSpec((None, tr, c), lambda i: (0, i, 0)) if lead else flat
    return pl.pallas_call(
        body, name=name, grid=(r // tr,), in_specs=[blk, flat, blk, blk], out_specs=[blk] * 3,
        out_shape=[jax.ShapeDtypeStruct(w.shape, F32)] * 3,
        compiler_params=_cparams(("parallel",)),
    )(w, g, m, v)


def _sum_leading(parts, *, name, out_dtype=F32):
    k, r, c = parts.shape
    tr = _pick(r, (512, 256, 128, 96, 64, 32, 16, 8))

    def body(p_ref, o_ref):
        acc = p_ref[0].astype(F32)
        for i in range(1, k):
            acc = acc + p_ref[i].astype(F32)
        o_ref[...] = acc.astype(out_dtype)

    return pl.pallas_call(
        body, name=name, grid=(r // tr,),
        in_specs=[pl.BlockSpec((k, tr, c), lambda i: (0, i, 0))],
        out_specs=pl.BlockSpec((tr, c), lambda i: (i, 0)),
        out_shape=jax.ShapeDtypeStruct((r, c), out_dtype),
        compiler_params=_cparams(("parallel",)),
    )(parts)


def _add_my_half(g, b, *, name):
    k, r, c = b.shape
    tr = _pick(r, (512, 256, 128))
    nrt = r // tr

    def body(lo_ref, hi_ref, b_ref, o_ref):
        mine = jnp.where(lax.axis_index("c") == 0, lo_ref[...], hi_ref[...])
        o_ref[...] = (mine.astype(F32) + b_ref[...].astype(F32)).astype(BF16)

    blk = pl.BlockSpec((None, tr, c), lambda j, i: (j, i, 0))
    return pl.pallas_call(
        body, name=name, grid=(k, nrt),
        in_specs=[blk, pl.BlockSpec((None, tr, c), lambda j, i: (j, i + nrt, 0)), blk], out_specs=blk,
        out_shape=jax.ShapeDtypeStruct((k, r, c), BF16),
        compiler_params=_cparams(("parallel", "parallel")),
    )(g, g, b)


ANY = pl.BlockSpec(memory_space=pl.ANY)


def _chip_peers(x, y):
    return [(1 - x, y, 2 * (1 - x) + y), (x, 1 - y, 2 * x + 1 - y), (1 - x, 1 - y, 2 * (1 - x) + 1 - y)]


def _gather_weights(blob, *, name):
    rows, cols = blob.shape
    half_rows = rows // 2

    def body(b_ref, o_ref, send_sems, recv_sems):
        x, y, c = lax.axis_index("x"), lax.axis_index("y"), lax.axis_index("c")
        me = 2 * x + y
        sibling = (x, y, 1 - c)
        peers = _chip_peers(x, y)

        def half(chip, hc):
            return o_ref.at[chip, pl.ds(hc * half_rows, half_rows), :]

        def copy(k, src, chip, hc, to):
            return pltpu.make_async_remote_copy(src_ref=src, dst_ref=half(chip, hc), send_sem=send_sems.at[k],
                                                recv_sem=recv_sems.at[k], device_id=to, device_id_type=MESH)

        my_half = b_ref.at[pl.ds(c * half_rows, half_rows), :]
        first = [copy(k, my_half, me, c, (px, py, c)) for k, (px, py, _) in enumerate(peers)]
        own = pltpu.make_async_remote_copy(src_ref=b_ref, dst_ref=o_ref.at[me], send_sem=send_sems.at[6],
                                           recv_sem=recv_sems.at[6], device_id=sibling, device_id_type=MESH)
        for cp in first + [own]:
            cp.start()
        passed = [copy(3 + k, half(pc, c), pc, c, sibling) for k, (_, _, pc) in enumerate(peers)]
        for k, (px, py, pc) in enumerate(peers):
            copy(k, my_half, pc, c, (px, py, c)).wait_recv()
            passed[k].start()
        for k, (_, _, pc) in enumerate(peers):
            copy(3 + k, half(pc, 1 - c), pc, 1 - c, sibling).wait_recv()
        own.wait_recv()
        for cp in first + passed + [own]:
            cp.wait_send()

    return pl.pallas_call(
        body, name=name, in_specs=[ANY], out_specs=ANY,
        out_shape=jax.ShapeDtypeStruct((N_CHIPS, rows, cols), BF16),
        scratch_shapes=[pltpu.SemaphoreType.DMA((7,)), pltpu.SemaphoreType.DMA((7,))],
    )(blob)


def _swap_halves(g, *, name):
    _, rows, cols = g.shape
    half_rows = rows // 2

    def body(g_ref, o_ref, send_sem, recv_sem):
        x, y, c = lax.axis_index("x"), lax.axis_index("y"), lax.axis_index("c")
        cp = pltpu.make_async_remote_copy(
            src_ref=g_ref.at[:, pl.ds((1 - c) * half_rows, half_rows), :], dst_ref=o_ref,
            send_sem=send_sem, recv_sem=recv_sem, device_id=(x, y, 1 - c), device_id_type=MESH)
        cp.start()
        cp.wait()

    return pl.pallas_call(
        body, name=name, in_specs=[ANY], out_specs=ANY,
        out_shape=jax.ShapeDtypeStruct((N_CHIPS, half_rows, cols), BF16),
        scratch_shapes=[pltpu.SemaphoreType.DMA, pltpu.SemaphoreType.DMA],
    )(g)


HBM_SPEC = pl.BlockSpec(memory_space=pltpu.HBM)
SEM_SPEC = pl.BlockSpec(memory_space=pltpu.SEMAPHORE)
SPLIT_EFFECT = pltpu.SideEffectType.DATAFLOW_SIDE_EFFECTING


def _gather_peers_copies(b_ref, land_ref, send_sems, recv_sems, sending):
    x, y, c = lax.axis_index("x"), lax.axis_index("y"), lax.axis_index("c")
    me = 2 * x + y
    half_rows = b_ref.shape[0] // 2
    src = b_ref.at[pl.ds(c * half_rows, half_rows), :]
    return [pltpu.make_async_remote_copy(
        src_ref=src, dst_ref=land_ref.at[me if sending else pc, pl.ds(c * half_rows, half_rows), :],
        send_sem=send_sems.at[k], recv_sem=recv_sems.at[k], device_id=(px, py, c), device_id_type=MESH)
        for k, (px, py, pc) in enumerate(_chip_peers(x, y))]


def _gather_start(blob, after, *, name):
    shape = (N_CHIPS,) + blob.shape

    def body(b_ref, land_ref, after_ref, send_sems, recv_sems, b_thru, land_thru, token):
        for cp in _gather_peers_copies(b_ref, land_ref, send_sems, recv_sems, True):
            cp.start()
        token[...] = jnp.zeros_like(token)

    return pl.pallas_call(
        body, name=name,
        out_shape=(pltpu.SemaphoreType.DMA((3,)), pltpu.SemaphoreType.DMA((3,)), pltpu.HBM(blob.shape, blob.dtype),
                   pltpu.HBM(shape, blob.dtype), jax.ShapeDtypeStruct((8, 128), F32)),
        in_specs=(HBM_SPEC, HBM_SPEC, ANY),
        out_specs=(SEM_SPEC, SEM_SPEC, HBM_SPEC, HBM_SPEC, pl.BlockSpec(memory_space=pltpu.VMEM)),
        input_output_aliases={0: 2, 1: 3},
        compiler_params=pltpu.CompilerParams(has_side_effects=SPLIT_EFFECT),
    )(pltpu.with_memory_space_constraint(blob, pltpu.HBM),
      pltpu.with_memory_space_constraint(lax.empty(shape, blob.dtype), pltpu.HBM), after)


def _gather_wait(send_sems, recv_sems, b_thru, land_thru, after, *, name):
    def body(b_ref, land_ref, send_sems, recv_sems, after_ref, b_dead, got_ref):
        for cp in _gather_peers_copies(b_ref, land_ref, send_sems, recv_sems, False):
            cp.wait_send()
            cp.wait_recv()

    return pl.pallas_call(
        body, name=name,
        out_shape=(pltpu.HBM(b_thru.shape, b_thru.dtype), pltpu.HBM(land_thru.shape, land_thru.dtype)),
        in_specs=(HBM_SPEC, HBM_SPEC, SEM_SPEC, SEM_SPEC, ANY), out_specs=(HBM_SPEC, HBM_SPEC),
        input_output_aliases={0: 0, 1: 1},
        compiler_params=pltpu.CompilerParams(has_side_effects=SPLIT_EFFECT),
    )(b_thru, land_thru, send_sems, recv_sems, after)


def _gather_forward(land, blob, *, name):
    half_rows = land.shape[1] // 2

    def body(l_ref, b_ref, o_ref, send_sems, recv_sems):
        x, y, c = lax.axis_index("x"), lax.axis_index("y"), lax.axis_index("c")
        me = 2 * x + y
        sibling = (x, y, 1 - c)
        cps = []
        for k, (_, _, pc) in enumerate(_chip_peers(x, y)):
            mine = pl.ds(c * half_rows, half_rows)
            cps.append(pltpu.make_async_remote_copy(
                src_ref=l_ref.at[pc, mine, :], dst_ref=o_ref.at[pc, mine, :], send_sem=send_sems.at[k],
                recv_sem=recv_sems.at[k], device_id=sibling, device_id_type=MESH))
        cps.append(pltpu.make_async_remote_copy(src_ref=b_ref, dst_ref=o_ref.at[me], send_sem=send_sems.at[3],
                                                recv_sem=recv_sems.at[3], device_id=sibling, device_id_type=MESH))
        for cp in cps:
            cp.start()
        for k, (_, _, pc) in enumerate(_chip_peers(x, y)):
            theirs = pl.ds((1 - c) * half_rows, half_rows)
            pltpu.make_async_remote_copy(
                src_ref=l_ref.at[pc, theirs, :], dst_ref=o_ref.at[pc, theirs, :], send_sem=send_sems.at[k],
                recv_sem=recv_sems.at[k], device_id=sibling, device_id_type=MESH).wait_recv()
        cps[3].wait_recv()
        for cp in cps:
            cp.wait_send()

    return pl.pallas_call(
        body, name=name, in_specs=[ANY, ANY], out_specs=ANY, input_output_aliases={0: 0},
        out_shape=jax.ShapeDtypeStruct(land.shape, land.dtype),
        scratch_shapes=[pltpu.SemaphoreType.DMA((4,)), pltpu.SemaphoreType.DMA((4,))],
    )(land, blob)


def _exchange_peers_copies(p_ref, land_ref, send_sems, recv_sems, sending):
    x, y, c = lax.axis_index("x"), lax.axis_index("y"), lax.axis_index("c")
    me = 2 * x + y
    return [pltpu.make_async_remote_copy(src_ref=p_ref.at[pc], dst_ref=land_ref.at[me if sending else pc],
                                         send_sem=send_sems.at[k], recv_sem=recv_sems.at[k],
                                         device_id=(px, py, c), device_id_type=MESH)
            for k, (px, py, pc) in enumerate(_chip_peers(x, y))]


def _exchange_start(p, *, name):
    def body(p_ref, land_ref, send_sems, recv_sems, p_thru, land_thru, token):
        for cp in _exchange_peers_copies(p_ref, land_ref, send_sems, recv_sems, True):
            cp.start()
        token[...] = jnp.zeros_like(token)

    return pl.pallas_call(
        body, name=name,
        out_shape=(pltpu.SemaphoreType.DMA((3,)), pltpu.SemaphoreType.DMA((3,)), pltpu.HBM(p.shape, p.dtype),
                   pltpu.HBM(p.shape, p.dtype), jax.ShapeDtypeStruct((8, 128), F32)),
        in_specs=(HBM_SPEC, HBM_SPEC),
        out_specs=(SEM_SPEC, SEM_SPEC, HBM_SPEC, HBM_SPEC, pl.BlockSpec(memory_space=pltpu.VMEM)),
        input_output_aliases={0: 2, 1: 3},
        compiler_params=pltpu.CompilerParams(has_side_effects=SPLIT_EFFECT),
    )(pltpu.with_memory_space_constraint(p, pltpu.HBM),
      pltpu.with_memory_space_constraint(lax.empty(p.shape, p.dtype), pltpu.HBM))


def _exchange_wait(send_sems, recv_sems, p_thru, land_thru, after, *, name):
    def body(p_ref, land_ref, send_sems, recv_sems, after_ref, p_dead, got_ref):
        for cp in _exchange_peers_copies(p_ref, land_ref, send_sems, recv_sems, False):
            cp.wait_send()
            cp.wait_recv()

    return pl.pallas_call(
        body, name=name,
        out_shape=(pltpu.HBM(p_thru.shape, p_thru.dtype), pltpu.HBM(p_thru.shape, p_thru.dtype)),
        in_specs=(HBM_SPEC, HBM_SPEC, SEM_SPEC, SEM_SPEC, ANY), out_specs=(HBM_SPEC, HBM_SPEC),
        input_output_aliases={0: 0, 1: 1},
        compiler_params=pltpu.CompilerParams(has_side_effects=SPLIT_EFFECT),
    )(p_thru, land_thru, send_sems, recv_sems, after)


def _sum_parts(parts, own, *, name):
    k, r, c = parts.shape
    tr = _pick(r, (512, 256, 128))

    def body(p_ref, own_ref, o_ref):
        me = 2 * lax.axis_index("x") + lax.axis_index("y")
        acc = jnp.zeros((tr, c), F32)
        for i in range(k):
            acc = acc + jnp.where(me == i, own_ref[i], p_ref[i]).astype(F32)
        o_ref[...] = acc

    blk = pl.BlockSpec((k, tr, c), lambda i: (0, i, 0))
    return pl.pallas_call(
        body, name=name, grid=(r // tr,), in_specs=[blk, blk],
        out_specs=pl.BlockSpec((tr, c), lambda i: (i, 0)),
        out_shape=jax.ShapeDtypeStruct((r, c), F32),
        compiler_params=_cparams(("parallel",)),
    )(parts, own)


def _from_sibling(a, *, name):
    def body(a_ref, o_ref, send_sem, recv_sem):
        x, y, c = lax.axis_index("x"), lax.axis_index("y"), lax.axis_index("c")
        cp = pltpu.make_async_remote_copy(src_ref=a_ref, dst_ref=o_ref, send_sem=send_sem, recv_sem=recv_sem,
                                          device_id=(x, y, 1 - c), device_id_type=MESH)
        cp.start()
        cp.wait()

    return pl.pallas_call(
        body, name=name, in_specs=[ANY], out_specs=ANY,
        out_shape=jax.ShapeDtypeStruct(a.shape, F32),
        scratch_shapes=[pltpu.SemaphoreType.DMA, pltpu.SemaphoreType.DMA],
    )(a)


def _join_halves(gh, *, name):
    other = _from_sibling(gh, name=name)
    south = lax.axis_index("c") == 0
    return jnp.concatenate([jnp.where(south, gh, other), jnp.where(south, other, gh)], axis=0)


def _add_f32(a, b, *, name):
    r, c = a.shape
    tr = _pick(r, (512, 256, 128))

    def body(a_ref, b_ref, o_ref):
        o_ref[...] = a_ref[...] + b_ref[...]

    blk = pl.BlockSpec((tr, c), lambda i: (i, 0))
    return pl.pallas_call(
        body, name=name, grid=(r // tr,), in_specs=[blk, blk], out_specs=blk,
        out_shape=jax.ShapeDtypeStruct((r, c), F32),
        compiler_params=_cparams(("parallel",)),
    )(a, b)


def _gather_small(s, *, name):
    rows = s.shape[0]

    def body(s_ref, o_ref, send_sems, recv_sems, local_sem):
        x, y, c = lax.axis_index("x"), lax.axis_index("y"), lax.axis_index("c")
        me = 4 * x + 2 * y + c
        mine = pltpu.make_async_copy(s_ref, o_ref.at[me], local_sem)
        mine.start()
        peers = []
        for k in range(1, 8):
            peers.append((1 - x if k & 4 else x, 1 - y if k & 2 else y, 1 - c if k & 1 else c))
        cps = [pltpu.make_async_remote_copy(src_ref=s_ref, dst_ref=o_ref.at[me], send_sem=send_sems.at[k],
                                            recv_sem=recv_sems.at[k], device_id=p, device_id_type=MESH)
               for k, p in enumerate(peers)]
        for cp in cps:
            cp.start()
        for k, (px, py, pc) in enumerate(peers):
            pltpu.make_async_remote_copy(src_ref=s_ref, dst_ref=o_ref.at[4 * px + 2 * py + pc],
                                         send_sem=send_sems.at[k], recv_sem=recv_sems.at[k],
                                         device_id=(px, py, pc), device_id_type=MESH).wait_recv()
        for cp in cps:
            cp.wait_send()
        mine.wait()

    return pl.pallas_call(
        body, name=name, in_specs=[ANY], out_specs=ANY,
        out_shape=jax.ShapeDtypeStruct((8, rows, 128), F32),
        scratch_shapes=[pltpu.SemaphoreType.DMA((7,)), pltpu.SemaphoreType.DMA((7,)), pltpu.SemaphoreType.DMA],
    )(s)


IN_SHARD = IN_WIDTH // N_CHIPS
IN_SHARD_PAD = 1536
UP_ROWS, DOWN_ROWS, OUT_ROWS = 1024, 1024, 512
REST_ROWS = UP_ROWS + DOWN_ROWS + OUT_ROWS


def _pack_in(w_in_s):
    return jnp.pad(w_in_s, ((0, 0), (0, IN_SHARD_PAD - IN_SHARD))).astype(BF16)


def _pack_rest(w_out_s, w_up_s, w_down_s):
    return jnp.concatenate([w_up_s, w_down_s, w_out_s], axis=0).astype(BF16)


def _unpack_rest(blob):
    return (blob[UP_ROWS + DOWN_ROWS:], blob[0:UP_ROWS], blob[UP_ROWS:UP_ROWS + DOWN_ROWS])


def _full_w_in(g_in):
    return jnp.concatenate([g_in[j, :, :IN_SHARD] for j in range(N_CHIPS)], axis=1)


def _full_rest(g_rest):
    parts = [_unpack_rest(g_rest[j]) for j in range(N_CHIPS)]
    w_out = jnp.concatenate([p[0] for p in parts], axis=0)
    w_up = jnp.concatenate([p[1] for p in parts], axis=1)
    w_down = jnp.concatenate([p[2] for p in parts], axis=0)
    return w_out, w_up, w_down


def _split_w_in(w_in):
    z_xbc = w_in[:, 0:2560]
    dt = w_in[:, 2560:2576]
    qkv = w_in[:, 2576:5648]
    f = w_in[:, 5648:5664]
    pad = jnp.zeros((w_in.shape[0], PA_WIDTH - 2592), w_in.dtype)
    return jnp.concatenate([z_xbc, dt, f, pad], axis=1), qkv


def _merge_w_in(d_a, d_qkv):
    return jnp.concatenate([d_a[:, 0:2560], d_a[:, 2560:2576], d_qkv, d_a[:, 2576:2592]], axis=1)


def _local_step(x3, target3, w_in, rest_weights, norm_mix_w, conv_w, conv_b, dt_bias, a_log, d_skip,
                ssd_norm_w, f_bias, norm_mlp_w, norm_final_w, first_after=None, early_grads=None, late_grads=None):
    bl, t, d = x3.shape
    n = bl * t
    x = x3.reshape(n, d)
    target = target3.reshape(n, d)
    w_a, w_qkv = _split_w_in(w_in)
    nfw = norm_final_w.reshape(1, d)
    dskip_e = jnp.repeat(d_skip, HEAD_DIM, axis=1)

    r1, r2, kt = min(n, 1024), min(n, 512), min(n, 2048)
    if first_after is None:
        first_after = jnp.zeros((8, 128), F32)
    h0, rstd0, proj_a = _norm_mm(x, norm_mix_w, w_a, first_after, name="norm_mix_proj_a", tm=r2)
    qkv = _mm(h0, w_qkv, name="proj_qkv", tiles=(r2, QKV_WIDTH, D_MODEL), out_dtype=BF16)
    bias128 = jnp.concatenate([dt_bias, f_bias, jnp.zeros((1, 96), F32)], axis=1)
    alog128 = jnp.concatenate([a_log, jnp.zeros((1, 112), F32)], axis=1)
    dt, gate_d, acum, acum_t, negc = _prep(proj_a, bias128, alog128, bl, t)
    xc, dsilu = _conv_fwd(proj_a, conv_w, conv_b, bl, t)
    y_ssd, y_pre, hprev = _ssd_fwd(xc, proj_a, dt, acum, acum_t, dskip_e, ssd_norm_w, bl, t)
    y_att, lse = _attn_fwd(qkv, negc, bl, t)
    w_out, w_up, w_down = rest_weights(y_att)
    wo_s, wo_a = w_out[:SSD_WIDTH], w_out[SSD_WIDTH:]
    h1, h1n, rstd1 = _mm_norm_fwd(y_ssd, wo_s, y_att, wo_a, x, norm_mlp_w, name="out_proj_norm_mlp", tm=r2)
    up = _mm(h1n, w_up, name="mlp_up", tiles=(r1, D_FF, D_MODEL), out_dtype=BF16)
    dh2, dh2b, loss, d_nfw = _mm_final(up, w_down, h1, nfw, target, name="mlp_down_final_norm_loss", tm=r2,
                                       a_act="relu2")

    dup = _mm(dh2b, w_down, name="mlp_down_bwd_act", tiles=(r2, D_FF, D_MODEL), tb=True, epi_up=up, out_dtype=BF16)
    rest_shape = (N_CHIPS, REST_ROWS, D_MODEL)
    gb_rest = _mm(up, dh2b, name="mlp_down_bwd_w", tiles=(DOWN_ROWS, D_MODEL, kt), ta=True, a_act="relu2",
                  out_dtype=BF16, into=(rest_shape, (None, DOWN_ROWS, D_MODEL), lambda i, j, k: (i, 1, 0), None))
    dh1, dh1b, d_nmlp = _mm_norm_bwd([(dup, w_up)], h1, rstd1, norm_mlp_w, dh2, name="mlp_up_bwd_act_norm_mlp",
                                     tm=r2)
    gb_rest = _mm(h1n, dup, name="mlp_up_bwd_w", tiles=(D_MODEL, UP_ROWS, kt), ta=True, out_dtype=BF16,
                  into=(rest_shape, (None, D_MODEL, UP_ROWS), lambda i, j, k: (j, 0, 0), gb_rest))
    dys, do = _mm_two_halves(dh1b, w_out, name="out_proj_bwd_act", tm=r1)
    out_block = (UP_ROWS + DOWN_ROWS) // OUT_ROWS
    for half, (y_half, tag) in enumerate(((y_ssd, "ssd"), (y_att, "att"))):
        gb_rest = _mm(y_half, dh1b, name="out_proj_bwd_w_" + tag, tiles=(2 * OUT_ROWS, D_MODEL, kt), ta=True,
                      out_dtype=BF16, into=(rest_shape, (2, OUT_ROWS, D_MODEL),
                                            functools.partial(lambda i, j, k, h: (h, out_block, 0), h=half),
                                            gb_rest))
    token = jnp.zeros((8, 128), F32) if early_grads is None else early_grads(gb_rest)
    dq, dk, dv, dcb = _attn_bwd(qkv, do, y_att, lse, negc, token, bl, t)
    dc = jnp.pad(dcb[:, :, 0:2, :].transpose(0, 3, 1, 2).reshape(n, 16), ((0, 0), (16, 96)))
    dxc, dpa, ddt_raw, d_snw, d_dsk, d_alog, d_dtb = _ssd_bwd(dys, xc, proj_a, y_pre, hprev, dt, gate_d, acum,
                                                             acum_t, alog128, dskip_e, ssd_norm_w, bl, t)
    dpa, d_conv_w, d_conv_b = _conv_bwd(dxc, dsilu, proj_a, conv_w, dpa, bl, t)
    dproj_a, d_fb = _fpost(dc, gate_d, ddt_raw, dpa, bl, t)
    d_w_a = _mm(h0, dproj_a, name="proj_a_bwd_w", tiles=(1024, 896, kt), ta=True, out_dtype=BF16)
    d_heads = (("q", dq), ("k", dk), ("v", dv))
    d_w_qkv = jnp.concatenate([_mm(h0, g, name=f"proj_{tag}_bwd_w", tiles=(1024, 1024, kt), ta=True, out_dtype=BF16)
                               for tag, g in d_heads], axis=1)
    d_w_in = _merge_w_in(d_w_a, d_w_qkv)
    late_token = None if late_grads is None else late_grads(d_w_in)
    head_pairs = [(g, w_qkv[:, j * D_MODEL:(j + 1) * D_MODEL]) for j, (_, g) in enumerate(d_heads)]
    dx, _, d_nmix = _mm_norm_bwd([(dproj_a, w_a)] + head_pairs, x, rstd0, norm_mix_w, dh1,
                                 name="proj_bwd_act_norm_mix", tm=min(n, 256), after=late_token)

    grads = dict(norm_mix_w=d_nmix, w_in=d_w_in, conv_w=d_conv_w, conv_b=d_conv_b,
                 dt_bias=d_dtb, a_log=d_alog, d_skip=d_dsk, ssd_norm_w=d_snw, f_bias=d_fb, rest=gb_rest,
                 norm_mlp_w=d_nmlp, norm_final_w=d_nfw)
    return dx.reshape(bl, t, d), loss, grads


SMALL_ORDER = ("norm_mix_w", "conv_w", "conv_b", "dt_bias", "a_log", "d_skip", "ssd_norm_w", "f_bias",
               "norm_mlp_w", "norm_final_w")
SMALL_SIZES = (1024, 4 * CONV_CH, CONV_CH, 16, 16, 16, 1024, 16, 1024, 1024)


def _pack_small(vals, rows):
    flat = jnp.concatenate([v.reshape(-1).astype(F32) for v in vals])
    return jnp.pad(flat, (0, rows * 128 - flat.shape[0])).reshape(rows, 128)


def _unpack_small(packed, sizes):
    flat = packed.reshape(-1)
    out, o = [], 0
    for s in sizes:
        out.append(flat[o:o + s])
        o += s
    return out


def kernel(x, norm_mix_w, w_in, conv_w, conv_b, dt_bias, a_log, d_skip, ssd_norm_w, f_bias, w_out, norm_mlp_w, w_up, w_down, norm_final_w, loss_target, m_norm_mix_w, m_w_in, m_conv_w, m_conv_b, m_dt_bias, m_a_log, m_d_skip, m_ssd_norm_w, m_f_bias, m_w_out, m_norm_mlp_w, m_w_up, m_w_down, m_norm_final_w, v_norm_mix_w, v_w_in, v_conv_w, v_conv_b, v_dt_bias, v_a_log, v_d_skip, v_ssd_norm_w, v_f_bias, v_w_out, v_norm_mlp_w, v_w_up, v_w_down, v_norm_final_w):
    chip = 2 * lax.axis_index("x") + lax.axis_index("y")
    cw = CONV_CH // N_CHIPS

    own_in = _pack_in(w_in[0])
    own_rest = _pack_rest(w_out[0], w_up[0], w_down[0])
    g_in = _gather_weights(own_in, name="gather_w_in")
    w_in_f = _full_w_in(g_in)
    *rest_handles, rest_token = _gather_start(own_rest, g_in, name="gather_start_rest")

    def rest_weights(after):
        _, landed = _gather_wait(*rest_handles, after, name="gather_wait_rest")
        return _full_rest(_gather_forward(landed, own_rest, name="gather_forward_rest"))
    small_all = _gather_small(_pack_small([conv_w[0]], 16), name="gather_conv_w")
    conv_w_f = jnp.concatenate([small_all[2 * j].reshape(-1)[:4 * cw].reshape(4, cw) for j in range(N_CHIPS)], axis=1)

    def chip_partial(gb, tag):
        from_sibling = _swap_halves(gb, name="grad_swap_halves_" + tag)
        return _add_my_half(gb, from_sibling, name="grad_add_sibling_" + tag)

    in_flight = {}

    def early_grads(gb_rest):
        *handles, token = _exchange_start(gb_rest, name="grad_exchange_start_rest")
        in_flight["rest"] = handles
        return token

    def late_grads(d_w_in):
        gb_in = jnp.stack([_pack_in(d_w_in[:, j * IN_SHARD:(j + 1) * IN_SHARD]) for j in range(N_CHIPS)])
        *handles, token = _exchange_start(chip_partial(gb_in, "in"), name="grad_exchange_start_in")
        in_flight["in"] = handles
        return token

    dx, loss_part, g = _local_step(x, loss_target, w_in_f, rest_weights, norm_mix_w, conv_w_f,
                                   conv_b, dt_bias, a_log, d_skip, ssd_norm_w, f_bias, norm_mlp_w, norm_final_w,
                                   first_after=rest_token, early_grads=early_grads, late_grads=late_grads)

    send_sems, recv_sems, part_rest, land_rest = in_flight["rest"]
    part_rest, parts_rest = _exchange_wait(send_sems, recv_sems, part_rest, land_rest, dx,
                                           name="grad_exchange_wait_rest")
    g_rest_core = _sum_parts(parts_rest, part_rest, name="grad_sum_chips_rest")
    g_rest_sibling = _from_sibling(g_rest_core, name="grad_swap_sums_rest")
    g_w_out, g_w_up, g_w_down = _unpack_rest(_add_f32(g_rest_core, g_rest_sibling, name="grad_add_cores_rest"))

    part_in, parts_in = _exchange_wait(*in_flight["in"], dx, name="grad_exchange_wait_in")
    g_in_half = _sum_parts(parts_in, part_in, name="grad_sum_chips_in")
    g_w_in = _join_halves(g_in_half, name="grad_join_halves_in")[:, :IN_SHARD]

    small_vals = [g[k] for k in SMALL_ORDER] + [loss_part[:, 0:1]]
    small_sum = _sum_leading(_gather_small(_pack_small(small_vals, SMALL_ROWS), name="gather_small_grads"), name="small_sum")
    sg = dict(zip(SMALL_ORDER + ("loss",), _unpack_small(small_sum, SMALL_SIZES + (1,))))
    loss = sg["loss"].reshape(())
    g_conv_full = sg["conv_w"].reshape(4, CONV_CH)
    g_conv = lax.dynamic_slice_in_dim(g_conv_full, chip * cw, cw, axis=1)

    grads = dict(norm_mix_w=sg["norm_mix_w"].reshape(1, -1), w_in=g_w_in[None], conv_w=g_conv[None],
                 conv_b=sg["conv_b"].reshape(1, -1), dt_bias=sg["dt_bias"].reshape(1, -1),
                 a_log=sg["a_log"].reshape(1, -1), d_skip=sg["d_skip"].reshape(1, -1),
                 ssd_norm_w=sg["ssd_norm_w"].reshape(1, -1), f_bias=sg["f_bias"].reshape(1, -1), w_out=g_w_out[None],
                 norm_mlp_w=sg["norm_mlp_w"].reshape(1, -1), w_up=g_w_up[None], w_down=g_w_down[None],
                 norm_final_w=sg["norm_final_w"])
    weights = dict(norm_mix_w=norm_mix_w, w_in=w_in, conv_w=conv_w, conv_b=conv_b, dt_bias=dt_bias, a_log=a_log,
                   d_skip=d_skip, ssd_norm_w=ssd_norm_w, f_bias=f_bias, w_out=w_out, norm_mlp_w=norm_mlp_w,
                   w_up=w_up, w_down=w_down, norm_final_w=norm_final_w)
    ms = dict(norm_mix_w=m_norm_mix_w, w_in=m_w_in, conv_w=m_conv_w, conv_b=m_conv_b, dt_bias=m_dt_bias,
              a_log=m_a_log, d_skip=m_d_skip, ssd_norm_w=m_ssd_norm_w, f_bias=m_f_bias, w_out=m_w_out,
              norm_mlp_w=m_norm_mlp_w, w_up=m_w_up, w_down=m_w_down, norm_final_w=m_norm_final_w)
    vs = dict(norm_mix_w=v_norm_mix_w, w_in=v_w_in, conv_w=v_conv_w, conv_b=v_conv_b, dt_bias=v_dt_bias,
              a_log=v_a_log, d_skip=v_d_skip, ssd_norm_w=v_ssd_norm_w, f_bias=v_f_bias, w_out=v_w_out,
              norm_mlp_w=v_norm_mlp_w, w_up=v_w_up, w_down=v_w_down, norm_final_w=v_norm_final_w)
    names = list(weights)
    big = ("w_in", "w_out", "w_up", "w_down")
    delta, new_m, new_v = {}, {}, {}
    for k, g2 in zip(big[1:], (g_w_out, g_w_up, g_w_down)):
        delta[k], new_m[k], new_v[k] = _adamw(weights[k], g2, ms[k], vs[k], name="adamw_" + k)
    g_in_t = g_w_in.T
    outs_t = _adamw(w_in[0].T, g_in_t, m_w_in[0].T, v_w_in[0].T, name="adamw_w_in")
    delta["w_in"], new_m["w_in"], new_v["w_in"] = [o.T[None] for o in outs_t]
    grads["w_in"] = g_in_t.T[None]
    smalls = [k for k in names if k not in big]
    sizes = [math.prod(weights[k].shape) for k in smalls]
    rows = -(-sum(sizes) // 1024) * 8
    packs = [_pack_small([d[k] for k in smalls], rows) for d in (weights, grads, ms, vs)]
    outs = _adamw(*packs, name="adamw_small")
    for o, dst in zip(outs, (delta, new_m, new_v)):
        for k, val in zip(smalls, _unpack_small(o, sizes)):
            dst[k] = val.reshape(weights[k].shape)
    return (loss, dx, *[grads[k] for k in names], *[delta[k] for k in names], *[new_m[k] for k in names],
            *[new_v[k] for k in names])
```

```python
import functools
import math

import jax
import jax.numpy as jnp
from jax import lax
from jax.experimental import pallas as pl
from jax.experimental.pallas import tpu as pltpu

F32 = jnp.float32
BF16 = jnp.bfloat16
HIGHEST = lax.Precision.HIGHEST
MESH = pl.DeviceIdType.MESH

D_MODEL = 1024
HEAD_DIM = 64
SSD_WIDTH = 1024
SSD_STATE = 128
CONV_CH = 1536
CHUNK = 128
ATT_WIDTH = 1024
EPS = 1e-5
IN_WIDTH = 5664
PA_WIDTH = 2688
QKV_WIDTH = 3072
D_FF = 4096
ATT_FWD_BLOCK = 512
ATT_BWD_BLOCK = 256
NEG = -1e30
LOG2E = 1.4426950408889634
VMEM_LIMIT = 48 * 1024 * 1024

ADAM_LR = 0.001
ADAM_B1 = 0.9
ADAM_B2 = 0.999
ADAM_EPS = 1e-08
ADAM_WD = 0.01
ADAM_STEP = 10

N_CHIPS = 4
SMALL_ROWS = 96


def _cparams(sem):
    return pltpu.CompilerParams(dimension_semantics=sem, vmem_limit_bytes=VMEM_LIMIT)


def _pick(n, cands):
    for c in cands:
        if n % c == 0:
            return c
    return n


MM_CHUNK = 512


def _mm(a, b, *, name, tiles, ta=False, tb=False, out_dtype=F32, res=None, a_act=None, epi_up=None, after=None,
        into=None):
    n_unread = (after is not None) + (into is not None and into[3] is not None)
    if ta:
        K, M = a.shape
    else:
        M, K = a.shape
    if tb:
        N, K2 = b.shape
    else:
        K2, N = b.shape
    assert K == K2, (a.shape, b.shape)
    tm, tn, tk = tiles
    assert M % tm == 0 and N % tn == 0 and K % tk == 0, (name, M, N, K, tiles)
    nk = K // tk
    dn = (((0 if ta else 1,), (1 if tb else 0,)), ((), ()))
    has_res = res is not None
    has_up = epi_up is not None
    cn = _pick(tn, (MM_CHUNK, 384, 256, 128))

    def prologue(av):
        if a_act == "relu2":
            r = jnp.maximum(av.astype(F32), 0.0)
            av = r * r
        return av.astype(BF16)

    def epilogue(out, res_v, up_v):
        if has_res:
            out = out + res_v.astype(F32)
        if has_up:
            out = out * (2.0 * jnp.maximum(up_v.astype(F32), 0.0))
        return out.astype(out_dtype)

    def body(*refs):
        a_ref, b_ref = refs[0], refs[1]
        i = 2
        res_ref = up_ref = None
        if has_res:
            res_ref = refs[i]
            i += 1
        if has_up:
            up_ref = refs[i]
            i += 1
        i += n_unread
        o_ref = refs[i]
        if nk == 1:
            av = prologue(a_ref[...])
            if len(o_ref.shape) == 3:
                out = lax.dot_general(av, b_ref[...].astype(BF16), dn, preferred_element_type=F32)
                out = epilogue(out, res_ref[...] if has_res else None, up_ref[...] if has_up else None)
                o_ref[...] = out.reshape(o_ref.shape)
                return
            for c in range(tn // cn):
                cs = slice(c * cn, (c + 1) * cn)
                bv = (b_ref[cs, :] if tb else b_ref[:, cs]).astype(BF16)
                out = lax.dot_general(av, bv, dn, preferred_element_type=F32)
                o_ref[:, cs] = epilogue(out, res_ref[:, cs] if has_res else None, up_ref[:, cs] if has_up else None)
            return
        acc_ref = refs[i + 1]
        k = pl.program_id(2)

        @pl.when(k == 0)
        def _():
            acc_ref[...] = jnp.zeros_like(acc_ref)

        acc_ref[...] += lax.dot_general(prologue(a_ref[...]), b_ref[...].astype(BF16), dn,
                                        preferred_element_type=F32)

        @pl.when(k == nk - 1)
        def _():
            out = epilogue(acc_ref[...], res_ref[...] if has_res else None, up_ref[...] if has_up else None)
            o_ref[...] = out.reshape(o_ref.shape)

    a_spec = pl.BlockSpec((tk, tm), lambda i, j, k: (k, i)) if ta else pl.BlockSpec((tm, tk), lambda i, j, k: (i, k))
    b_spec = pl.BlockSpec((tn, tk), lambda i, j, k: (j, k)) if tb else pl.BlockSpec((tk, tn), lambda i, j, k: (k, j))
    o_spec = pl.BlockSpec((tm, tn), lambda i, j, k: (i, j))
    ins, specs = [a, b], [a_spec, b_spec]
    if has_res:
        ins.append(res)
        specs.append(o_spec)
    if has_up:
        ins.append(epi_up)
        specs.append(o_spec)
    if after is not None:
        ins.append(after)
        specs.append(pl.BlockSpec(memory_space=pl.ANY))
    out_shape, out_spec, aliases = jax.ShapeDtypeStruct((M, N), out_dtype), o_spec, {}
    if into is not None:
        shape, block, index, buf = into
        out_shape, out_spec = jax.ShapeDtypeStruct(shape, out_dtype), pl.BlockSpec(block, index)
        if buf is not None:
            aliases = {len(ins): 0}
            ins.append(buf)
            specs.append(pl.BlockSpec(memory_space=pl.ANY))
    return pl.pallas_call(
        body, name=name, grid=(M // tm, N // tn, nk),
        in_specs=specs, out_specs=out_spec, out_shape=out_shape, input_output_aliases=aliases,
        scratch_shapes=[] if nk == 1 else [pltpu.VMEM((tm, tn), F32)],
        compiler_params=_cparams(("parallel", "parallel", "arbitrary")),
    )(*ins)


def _rows_product(a_ref, b_ref, tb, a_act):
    av = a_ref[...]
    if a_act == "relu2":
        r = jnp.maximum(av.astype(F32), 0.0)
        av = r * r
    dn = (((1,), (1 if tb else 0,)), ((), ()))
    return lax.dot_general(av.astype(BF16), b_ref[...].astype(BF16), dn, preferred_element_type=F32)


def _norm_mm(x, w, b, after, *, name, tm):
    m, d = x.shape
    n = b.shape[1]
    cn = _pick(n, (MM_CHUNK, 384, 256, 128))

    def body(x_ref, w_ref, b_ref, after_ref, h_ref, r_ref, o_ref):
        xv = x_ref[...]
        rstd = lax.rsqrt(jnp.mean(xv * xv, axis=1, keepdims=True) + EPS)
        hv = (xv * rstd * w_ref[...]).astype(BF16)
        h_ref[...] = hv
        r_ref[...] = rstd
        for c in range(n // cn):
            cs = slice(c * cn, (c + 1) * cn)
            o_ref[:, cs] = jnp.dot(hv, b_ref[:, cs].astype(BF16), preferred_element_type=F32)

    row = pl.BlockSpec((tm, d), lambda i: (i, 0))
    return pl.pallas_call(
        body, name=name, grid=(m // tm,),
        in_specs=[row, pl.BlockSpec((1, d), lambda i: (0, 0)), pl.BlockSpec((d, n), lambda i: (0, 0)),
                  pl.BlockSpec(memory_space=pl.ANY)],
        out_specs=[row, pl.BlockSpec((tm, 1), lambda i: (i, 0)), pl.BlockSpec((tm, n), lambda i: (i, 0))],
        out_shape=[jax.ShapeDtypeStruct((m, d), BF16), jax.ShapeDtypeStruct((m, 1), F32),
                   jax.ShapeDtypeStruct((m, n), F32)],
        compiler_params=_cparams(("parallel",)),
    )(x, w, b, after)


def _mm_norm_fwd(a1, b1, a2, b2, res, w, *, name, tm):
    m, k1 = a1.shape
    k2 = a2.shape[1]
    d = b1.shape[1]

    def body(a1_ref, b1_ref, a2_ref, b2_ref, res_ref, w_ref, h_ref, y_ref, r_ref):
        hv = _rows_product(a1_ref, b1_ref, False, None) + _rows_product(a2_ref, b2_ref, False, None) + res_ref[...]
        rstd = lax.rsqrt(jnp.mean(hv * hv, axis=1, keepdims=True) + EPS)
        h_ref[...] = hv
        y_ref[...] = (hv * rstd * w_ref[...]).astype(BF16)
        r_ref[...] = rstd

    row = pl.BlockSpec((tm, d), lambda i: (i, 0))
    return pl.pallas_call(
        body, name=name, grid=(m // tm,),
        in_specs=[pl.BlockSpec((tm, k1), lambda i: (i, 0)), pl.BlockSpec((k1, d), lambda i: (0, 0)),
                  pl.BlockSpec((tm, k2), lambda i: (i, 0)), pl.BlockSpec((k2, d), lambda i: (0, 0)), row,
                  pl.BlockSpec((1, d), lambda i: (0, 0))],
        out_specs=[row, row, pl.BlockSpec((tm, 1), lambda i: (i, 0))],
        out_shape=[jax.ShapeDtypeStruct((m, d), F32), jax.ShapeDtypeStruct((m, d), BF16),
                   jax.ShapeDtypeStruct((m, 1), F32)],
        compiler_params=_cparams(("parallel",)),
    )(a1, b1, a2, b2, res, w)


def _mm_final(a, b, res, w, target, *, name, tm, a_act):
    m, k = a.shape
    d = b.shape[1]

    def body(a_ref, b_ref, res_ref, w_ref, t_ref, dh_ref, dhb_ref, loss_ref, dw_ref):
        @pl.when(pl.program_id(0) == 0)
        def _():
            loss_ref[...] = jnp.zeros_like(loss_ref)
            dw_ref[...] = jnp.zeros_like(dw_ref)

        hv = _rows_product(a_ref, b_ref, False, a_act) + res_ref[...]
        wv = w_ref[...]
        rstd = lax.rsqrt(jnp.mean(hv * hv, axis=1, keepdims=True) + EPS)
        xhat = hv * rstd
        err = xhat * wv - t_ref[...]
        loss_ref[...] += 0.5 * jnp.sum(jnp.mean(err * err, axis=1, keepdims=True), axis=0, keepdims=True)
        dy = err * (1.0 / d)
        gw = dy * wv
        dh = rstd * (gw - xhat * jnp.mean(gw * xhat, axis=1, keepdims=True))
        dh_ref[...] = dh
        dhb_ref[...] = dh.astype(BF16)
        dw_ref[...] += jnp.sum(dy * xhat, axis=0, keepdims=True)

    row = pl.BlockSpec((tm, d), lambda i: (i, 0))
    vec = pl.BlockSpec((1, d), lambda i: (0, 0))
    return pl.pallas_call(
        body, name=name, grid=(m // tm,),
        in_specs=[pl.BlockSpec((tm, k), lambda i: (i, 0)), pl.BlockSpec((k, d), lambda i: (0, 0)), row, vec, row],
        out_specs=[row, row, pl.BlockSpec((1, 128), lambda i: (0, 0)), vec],
        out_shape=[jax.ShapeDtypeStruct((m, d), F32), jax.ShapeDtypeStruct((m, d), BF16),
                   jax.ShapeDtypeStruct((1, 128), F32), jax.ShapeDtypeStruct((1, d), F32)],
        compiler_params=_cparams(("arbitrary",)),
    )(a, b, res, w, target)


def _mm_two_halves(a, b, *, name, tm):
    m, k = a.shape
    d = b.shape[0] // 2

    def body(a_ref, b_ref, lo_ref, hi_ref):
        av = a_ref[...].astype(BF16)
        lo_ref[...] = lax.dot_general(av, b_ref[0:d, :].astype(BF16), NT_DIMS, preferred_element_type=F32)
        hi_ref[...] = lax.dot_general(av, b_ref[d:2 * d, :].astype(BF16), NT_DIMS,
                                      preferred_element_type=F32).astype(BF16)

    row = pl.BlockSpec((tm, d), lambda i: (i, 0))
    return pl.pallas_call(
        body, name=name, grid=(m // tm,),
        in_specs=[pl.BlockSpec((tm, k), lambda i: (i, 0)), pl.BlockSpec((2 * d, k), lambda i: (0, 0))],
        out_specs=[row, row],
        out_shape=[jax.ShapeDtypeStruct((m, d), F32), jax.ShapeDtypeStruct((m, d), BF16)],
        compiler_params=_cparams(("parallel",)),
    )(a, b)


def _mm_norm_bwd(pairs, x, rstd, w, dres, *, name, tm, after=None):
    m = pairs[0][0].shape[0]
    d = pairs[0][1].shape[0]
    n_pairs = len(pairs)

    def body(*refs):
        i = 2 * n_pairs
        x_ref, r_ref, w_ref, d_ref = refs[i:i + 4]
        dx_ref, dxb_ref, dw_ref = refs[-3:]

        @pl.when(pl.program_id(0) == 0)
        def _():
            dw_ref[...] = jnp.zeros_like(dw_ref)

        g = _rows_product(refs[0], refs[1], True, None)
        for p in range(1, n_pairs):
            g = g + _rows_product(refs[2 * p], refs[2 * p + 1], True, None)
        r = r_ref[...]
        xhat = x_ref[...] * r
        gw = g * w_ref[...]
        dx = d_ref[...] + r * (gw - xhat * jnp.mean(gw * xhat, axis=1, keepdims=True))
        dx_ref[...] = dx
        dxb_ref[...] = dx.astype(BF16)
        dw_ref[...] += jnp.sum(g * xhat, axis=0, keepdims=True)

    row = pl.BlockSpec((tm, d), lambda i: (i, 0))
    vec = pl.BlockSpec((1, d), lambda i: (0, 0))
    ins, specs = [], []
    for a, b, *col_block in pairs:
        k = a.shape[1]
        ins += [a, b]
        specs += [pl.BlockSpec((tm, k), lambda i: (i, 0)),
                  pl.BlockSpec((d, k), functools.partial(lambda i, cb: (0, cb), cb=col_block[0] if col_block else 0))]
    ins += [x, rstd, w, dres]
    specs += [row, pl.BlockSpec((tm, 1), lambda i: (i, 0)), vec, row]
    if after is not None:
        ins.append(after)
        specs.append(pl.BlockSpec(memory_space=pl.ANY))
    return pl.pallas_call(
        body, name=name, grid=(m // tm,), in_specs=specs, out_specs=[row, row, vec],
        out_shape=[jax.ShapeDtypeStruct((m, d), F32), jax.ShapeDtypeStruct((m, d), BF16),
                   jax.ShapeDtypeStruct((1, d), F32)],
        compiler_params=_cparams(("arbitrary",)),
    )(*ins)


def _softplus(x):
    return jnp.maximum(x, 0.0) + jnp.log(1.0 + jnp.exp(-jnp.abs(x)))


def _prep(proj_a, bias128, alog128, bl, t):
    n = bl * t
    nch = t // CHUNK
    col0 = (SSD_WIDTH + CONV_CH) // 128

    def body(p_ref, b_ref, al_ref, dt_ref, gd_ref, ac_ref, act_ref, negc_ref):
        negc_ref[...] = jnp.zeros_like(negc_ref)
        row = lax.broadcasted_iota(jnp.int32, (CHUNK, CHUNK), 0)
        col = lax.broadcasted_iota(jnp.int32, (CHUNK, CHUNK), 1)
        tril = (row >= col).astype(F32)
        lane = lax.broadcasted_iota(jnp.int32, (1, 128), 1)
        head_lanes = lane < 16
        a_row = -jnp.exp(al_ref[...])
        carry = jnp.zeros((1, 128), F32)
        for ci in range(nch):
            rows = slice(ci * CHUNK, (ci + 1) * CHUNK)
            xv = p_ref[rows, :] + b_ref[...]
            sp = _softplus(xv)
            acum = jnp.dot(tril, a_row * sp, precision=HIGHEST, preferred_element_type=F32)
            c = jnp.dot(tril, -_softplus(-xv), precision=HIGHEST, preferred_element_type=F32) + carry
            carry = c[CHUNK - 1:CHUNK, :]
            dt_ref[rows, :] = jnp.where(head_lanes, sp, 0.0)
            gd_ref[rows, :] = jnp.where(head_lanes, jax.nn.sigmoid(xv),
                                        jnp.where(lane < 32, jax.nn.sigmoid(-xv), 0.0))
            ac_ref[rows, :] = jnp.where(head_lanes, acum, 0.0)
            act_ref[:, rows] = jnp.transpose(acum)[0:16, :]
            c_t = jnp.transpose(c)
            for hp in range(8):
                negc_ref[hp, 0:2, rows] = -c_t[16 + 2 * hp:18 + 2 * hp, :]

    o128 = pl.BlockSpec((t, 128), lambda b: (b, 0))
    v128 = pl.BlockSpec((1, 128), lambda b: (0, 0))
    w128 = jax.ShapeDtypeStruct((n, 128), F32)
    return pl.pallas_call(
        body, name="head_scalars", grid=(bl,),
        in_specs=[pl.BlockSpec((t, 128), lambda b: (b, col0)), v128, v128],
        out_specs=[o128, o128, o128, pl.BlockSpec((16, t), lambda b: (0, b)),
                   pl.BlockSpec((None, 8, 8, t), lambda b: (b, 0, 0, 0))],
        out_shape=[w128, w128, w128, jax.ShapeDtypeStruct((16, n), F32),
                   jax.ShapeDtypeStruct((bl, 8, 8, t), F32)],
        compiler_params=_cparams(("parallel",)),
    )(proj_a, bias128, alog128)


def _fpost(dc, gate_d, ddt, dpa, bl, t):
    n = bl * t
    nch = t // CHUNK
    col0 = (SSD_WIDTH + CONV_CH) // 128

    def body(dc_ref, gd_ref, ddt_ref, dpa_in, out_ref, db_ref):
        @pl.when(pl.program_id(0) == 0)
        def _():
            db_ref[...] = jnp.zeros_like(db_ref)

        row = lax.broadcasted_iota(jnp.int32, (CHUNK, CHUNK), 0)
        col = lax.broadcasted_iota(jnp.int32, (CHUNK, CHUNK), 1)
        triu = (row <= col).astype(F32)
        lane = lax.broadcasted_iota(jnp.int32, (1, 128), 1)
        gate_lanes = (lane >= 16) & (lane < 32)
        carry = jnp.zeros((1, 128), F32)
        db = jnp.zeros((1, 128), F32)
        for ci in reversed(range(nch)):
            rows = slice(ci * CHUNK, (ci + 1) * CHUNK)
            dlf = jnp.dot(triu, dc_ref[rows, :], precision=HIGHEST, preferred_element_type=F32) + carry
            carry = dlf[0:1, :]
            df = jnp.where(gate_lanes, dlf * gd_ref[rows, :], 0.0)
            out_ref[rows, :] = (ddt_ref[rows, :] + df).astype(BF16)
            db = db + jnp.sum(df, axis=0, keepdims=True)
        db_ref[...] += db[:, 16:32]

    blk = pl.BlockSpec((t, 128), lambda b: (b, 0))
    return pl.pallas_call(
        body, name="forget_gate_bwd", grid=(bl,),
        in_specs=[blk, blk, blk, ANY],
        out_specs=[pl.BlockSpec((t, 128), lambda b: (b, col0)), pl.BlockSpec((1, 16), lambda b: (0, 0))],
        out_shape=[jax.ShapeDtypeStruct(dpa.shape, dpa.dtype), jax.ShapeDtypeStruct((1, 16), F32)],
        input_output_aliases={3: 0},
        compiler_params=_cparams(("arbitrary",)),
    )(dc, gate_d, ddt, dpa)


CONV_TILE = 256
CONV_ROWS = 256


def _conv_taps(u_ref, i):
    r0 = pl.multiple_of(i * CONV_ROWS, CONV_ROWS)
    cur = u_ref[pl.ds(r0, CONV_ROWS), :]
    p0 = pl.multiple_of(jnp.maximum(r0 - 8, 0), 8)
    prev = jnp.where(i > 0, u_ref[pl.ds(p0, 8), :], 0.0)
    cat = jnp.concatenate([prev, cur], axis=0)
    return r0, [cur] + [pltpu.roll(cat, s, 0)[8:, :] for s in (1, 2, 3)]


def _conv_fwd(proj_a, conv_w, conv_b, bl, t):
    n = bl * t
    nct = CONV_CH // CONV_TILE
    c0 = SSD_WIDTH // CONV_TILE

    def body(u_ref, w_ref, b_ref, o_ref, d_ref):
        w = w_ref[...]
        bias = b_ref[...]

        def chunk(i, carry):
            r0, taps = _conv_taps(u_ref, i)
            pre = bias + w[3:4, :] * taps[0]
            for s in (1, 2, 3):
                pre = pre + w[3 - s:4 - s, :] * taps[s]
            sg = jax.nn.sigmoid(pre)
            o_ref[pl.ds(r0, CONV_ROWS), :] = pre * sg
            d_ref[pl.ds(r0, CONV_ROWS), :] = (sg * (1.0 + pre * (1.0 - sg))).astype(BF16)
            return carry

        lax.fori_loop(0, t // CONV_ROWS, chunk, 0)

    out = pl.BlockSpec((t, CONV_TILE), lambda b, c: (b, c))
    return pl.pallas_call(
        body, name="conv_silu_fwd", grid=(bl, nct),
        in_specs=[pl.BlockSpec((t, CONV_TILE), lambda b, c: (b, c0 + c)),
                  pl.BlockSpec((4, CONV_TILE), lambda b, c: (0, c)),
                  pl.BlockSpec((1, CONV_TILE), lambda b, c: (0, c))],
        out_specs=[out, out],
        out_shape=[jax.ShapeDtypeStruct((n, CONV_CH), F32), jax.ShapeDtypeStruct((n, CONV_CH), BF16)],
        compiler_params=_cparams(("parallel", "parallel")),
    )(proj_a, conv_w, conv_b)


def _conv_bwd(dxc, dsilu, proj_a, conv_w, dpa, bl, t):
    nct = CONV_CH // CONV_TILE
    c0 = SSD_WIDTH // CONV_TILE
    nrc = t // CONV_ROWS

    def body(g_ref, s_ref, u_ref, w_ref, dpa_in, du_ref, dw_ref, db_ref, dp_scr):
        @pl.when(pl.program_id(1) == 0)
        def _():
            dw_ref[...] = jnp.zeros_like(dw_ref)
            db_ref[...] = jnp.zeros_like(db_ref)

        w = w_ref[...]
        dp_scr[pl.ds(t, 8), :] = jnp.zeros((8, CONV_TILE), F32)

        def chunk1(i, carry):
            dw0, dw1, dw2, dw3, db = carry
            r0, taps = _conv_taps(u_ref, i)
            dpre = g_ref[pl.ds(r0, CONV_ROWS), :] * s_ref[pl.ds(r0, CONV_ROWS), :].astype(F32)
            dp_scr[pl.ds(r0, CONV_ROWS), :] = dpre
            dw3 = dw3 + jnp.sum(dpre * taps[0], axis=0, keepdims=True)
            dw2 = dw2 + jnp.sum(dpre * taps[1], axis=0, keepdims=True)
            dw1 = dw1 + jnp.sum(dpre * taps[2], axis=0, keepdims=True)
            dw0 = dw0 + jnp.sum(dpre * taps[3], axis=0, keepdims=True)
            db = db + jnp.sum(dpre, axis=0, keepdims=True)
            return dw0, dw1, dw2, dw3, db

        z = jnp.zeros((1, CONV_TILE), F32)
        dw0, dw1, dw2, dw3, db = lax.fori_loop(0, nrc, chunk1, (z, z, z, z, z))
        dw_ref[...] += jnp.concatenate([dw0, dw1, dw2, dw3], axis=0)
        db_ref[...] += db

        def chunk2(i, carry):
            r0 = pl.multiple_of(i * CONV_ROWS, CONV_ROWS)
            cat = dp_scr[pl.ds(r0, CONV_ROWS + 8), :]
            du = w[3:4, :] * cat[:CONV_ROWS, :]
            for s in (1, 2, 3):
                du = du + w[3 - s:4 - s, :] * pltpu.roll(cat, CONV_ROWS + 8 - s, 0)[:CONV_ROWS, :]
            du_ref[pl.ds(r0, CONV_ROWS), :] = du.astype(BF16)
            return carry

        lax.fori_loop(0, nrc, chunk2, 0)

    tile = pl.BlockSpec((t, CONV_TILE), lambda c, b: (b, c))
    return pl.pallas_call(
        body, name="conv_silu_bwd", grid=(nct, bl),
        in_specs=[tile, tile, pl.BlockSpec((t, CONV_TILE), lambda c, b: (b, c0 + c)),
                  pl.BlockSpec((4, CONV_TILE), lambda c, b: (0, c)), ANY],
        out_specs=[pl.BlockSpec((t, CONV_TILE), lambda c, b: (b, c0 + c)),
                   pl.BlockSpec((4, CONV_TILE), lambda c, b: (0, c)),
                   pl.BlockSpec((1, CONV_TILE), lambda c, b: (0, c))],
        out_shape=[jax.ShapeDtypeStruct(dpa.shape, dpa.dtype), jax.ShapeDtypeStruct((4, CONV_CH), F32),
                   jax.ShapeDtypeStruct((1, CONV_CH), F32)],
        input_output_aliases={4: 0},
        scratch_shapes=[pltpu.VMEM((t + 8, CONV_TILE), F32)],
        compiler_params=_cparams(("parallel", "arbitrary")),
    )(dxc, dsilu, proj_a, conv_w, dpa)


SSD_FWD_CHUNKS = 4
SSD_BWD_CHUNKS = 1
NT_DIMS = (((1,), (1,)), ((), ()))
TN_DIMS = (((0,), (0,)), ((), ()))


def _dot(a, b, dims=None):
    if dims is None:
        return jnp.dot(a, b, preferred_element_type=F32)
    return lax.dot_general(a, b, dims, preferred_element_type=F32)


def _head_expander():
    r = lax.broadcasted_iota(jnp.int32, (128, SSD_WIDTH), 0)
    c = lax.broadcasted_iota(jnp.int32, (128, SSD_WIDTH), 1)
    return ((c // HEAD_DIM == r % 16) & (r < 48)).astype(BF16)


def _spread(v128, expander):
    hi = v128.astype(BF16).astype(F32)
    r1 = v128 - hi
    mid = r1.astype(BF16).astype(F32)
    lo = (r1 - mid).astype(BF16).astype(F32)
    packed = (hi + pltpu.roll(mid, 16, 1) + pltpu.roll(lo, 32, 1)).astype(BF16)
    return jnp.dot(packed, expander, preferred_element_type=F32)


def _head_sums(v1024, expander):
    hi = v1024.astype(BF16)
    lo = (v1024 - hi.astype(F32)).astype(BF16)
    heads = jnp.where(lax.broadcasted_iota(jnp.int32, expander.shape, 0) < 16, expander, jnp.zeros_like(expander))
    return _dot(hi, heads, NT_DIMS) + _dot(lo, heads, NT_DIMS)


def _ssd_fwd(xc, proj_a, dt, acum, acum_t, dskip_e, norm_w, bl, t):
    n = bl * t
    nch = t // CHUNK
    L = CHUNK

    def body(xc_blk, z_blk, dt_blk, ac_blk, act_blk, dsk_ref, nw_ref, ys_blk, yp_blk, hp_blk, h_scr, y_scr, x_scr):
        @pl.when(pl.program_id(1) == 0)
        def _():
            h_scr[...] = jnp.zeros_like(h_scr)

        for sub in range(SSD_FWD_CHUNKS):
            rows = pl.ds(sub * L, L)
            chunk(xc_blk.at[rows, :], z_blk.at[rows, :], dt_blk.at[rows, :], ac_blk.at[rows, :], act_blk.at[:, rows],
                  dsk_ref, nw_ref, ys_blk.at[rows, :], yp_blk.at[rows, :], hp_blk.at[sub], h_scr, y_scr, x_scr)

    def chunk(xc_ref, z_ref, dt_ref, ac_ref, act_ref, dsk_ref, nw_ref, ys_ref, yp_ref, hp_ref, h_scr, y_scr, x_scr):
        row = lax.broadcasted_iota(jnp.int32, (L, L), 0)
        col = lax.broadcasted_iota(jnp.int32, (L, L), 1)
        causal = row >= col
        lane128 = lax.broadcasted_iota(jnp.int32, (1, L), 1)
        expander = _head_expander()
        ac_all = ac_ref[...]
        act_all = act_ref[...]
        ac_e = _spread(ac_all, expander)
        e_in = jnp.exp(ac_e)
        dec = jnp.exp(ac_e[L - 1:L, :] - ac_e)
        xs_all = xc_ref[:, 0:SSD_WIDTH]
        x_all = xs_all * _spread(dt_ref[...], expander)
        x_scr[...] = x_all.astype(BF16)
        hp_all = h_scr[...]
        hp_ref[...] = hp_all
        for g in range(2):
            gs = slice(g * 512, (g + 1) * 512)
            bg = xc_ref[:, SSD_WIDTH + g * 128:SSD_WIDTH + (g + 1) * 128].astype(BF16)
            cg = xc_ref[:, SSD_WIDTH + 256 + g * 128:SSD_WIDTH + 256 + (g + 1) * 128].astype(BF16)
            gmat = _dot(cg, bg, NT_DIMS)
            y_off = _dot(cg, hp_all[gs, :].astype(BF16), NT_DIMS) * e_in[:, gs] + dsk_ref[:, gs] * xs_all[:, gs]
            s_new = _dot((x_all[:, gs] * dec[:, gs]).astype(BF16), bg, TN_DIMS)
            for pr in range(4):
                pair = slice((g * 4 + pr) * 128, (g * 4 + pr + 1) * 128)
                x_pair = x_scr[:, pair]
                y_pair = y_off[:, pr * 128:(pr + 1) * 128]
                for j in range(2):
                    h = g * 8 + 2 * pr + j
                    sl = slice(h * HEAD_DIM, (h + 1) * HEAD_DIM)
                    r = 2 * pr + j
                    ldec = jnp.exp(jnp.where(causal, ac_all[:, h:h + 1] - act_all[h:h + 1, :], NEG))
                    x_head = jnp.where((lane128 < HEAD_DIM) == (j == 0), x_pair, jnp.zeros_like(x_pair))
                    y_pair = y_pair + _dot((gmat * ldec).astype(BF16), x_head)
                    elast = jnp.exp(ac_all[L - 1:L, h:h + 1])
                    h_scr[sl, :] = elast * hp_all[sl, :] + s_new[r * HEAD_DIM:(r + 1) * HEAD_DIM, :]
                y_scr[:, pair] = y_pair
        y = y_scr[...]
        yp_ref[...] = y
        zv = z_ref[...]
        yg = y * (zv * jax.nn.sigmoid(zv))
        for g in range(2):
            gs = slice(g * 512, (g + 1) * 512)
            grp = yg[:, gs]
            rstd = lax.rsqrt(jnp.mean(grp * grp, axis=1, keepdims=True) + EPS)
            ys_ref[:, gs] = (grp * rstd * nw_ref[:, gs]).astype(BF16)

    cps = SSD_FWD_CHUNKS
    steps = nch // cps
    rb = lambda b, c: (b * steps + c, 0)
    v1k = pl.BlockSpec((1, SSD_WIDTH), lambda b, c: (0, 0))
    return pl.pallas_call(
        body, name="ssd_fwd", grid=(bl, steps),
        in_specs=[pl.BlockSpec((cps * L, CONV_CH), rb), pl.BlockSpec((cps * L, SSD_WIDTH), rb),
                  pl.BlockSpec((cps * L, 128), rb), pl.BlockSpec((cps * L, 128), rb),
                  pl.BlockSpec((16, cps * L), lambda b, c: (0, b * steps + c)), v1k, v1k],
        out_specs=[pl.BlockSpec((cps * L, SSD_WIDTH), rb), pl.BlockSpec((cps * L, SSD_WIDTH), rb),
                   pl.BlockSpec((cps, SSD_WIDTH, SSD_STATE), lambda b, c: (b * steps + c, 0, 0))],
        out_shape=[jax.ShapeDtypeStruct((n, SSD_WIDTH), BF16), jax.ShapeDtypeStruct((n, SSD_WIDTH), F32),
                   jax.ShapeDtypeStruct((bl * nch, SSD_WIDTH, SSD_STATE), F32)],
        scratch_shapes=[pltpu.VMEM((SSD_WIDTH, SSD_STATE), F32), pltpu.VMEM((L, SSD_WIDTH), F32),
                        pltpu.VMEM((L, SSD_WIDTH), BF16)],
        compiler_params=_cparams(("parallel", "arbitrary")),
    )(xc, proj_a, dt, acum, acum_t, dskip_e, norm_w)


def _ssd_bwd(dys, xc, proj_a, ypre, hprev, dt, gate_d, acum, acum_t, alog128, dskip_e, norm_w, bl, t):
    n = bl * t
    nch = t // CHUNK
    L = CHUNK

    def body(dys_blk, xc_blk, z_blk, yp_blk, hp_blk, dt_blk, gd_blk, ac_blk, act_blk, al_ref, dsk_ref, nw_ref,
             dxc_blk, dz_blk, ddt_blk, dnw_ref, dsk16_ref, da16_ref, db16_ref,
             dh_scr, dy_scr, x_scr, dx_scr, red_scr):
        first = (pl.program_id(0) == 0) & (pl.program_id(1) == 0)

        @pl.when(first)
        def _():
            dnw_ref[...] = jnp.zeros_like(dnw_ref)
            dsk16_ref[...] = jnp.zeros_like(dsk16_ref)
            da16_ref[...] = jnp.zeros_like(da16_ref)
            db16_ref[...] = jnp.zeros_like(db16_ref)

        @pl.when(pl.program_id(1) == 0)
        def _():
            dh_scr[...] = jnp.zeros_like(dh_scr)

        for sub in reversed(range(SSD_BWD_CHUNKS)):
            rows = pl.ds(sub * L, L)
            chunk(dys_blk.at[rows, :], xc_blk.at[rows, :], z_blk.at[rows, :], yp_blk.at[rows, :], hp_blk.at[sub],
                  dt_blk.at[rows, :], gd_blk.at[rows, :], ac_blk.at[rows, :], act_blk.at[:, rows], al_ref, dsk_ref,
                  nw_ref, dxc_blk.at[rows, :], dz_blk.at[rows, :], ddt_blk.at[rows, :], dnw_ref, dsk16_ref, da16_ref,
                  db16_ref, dh_scr, dy_scr, x_scr, dx_scr, red_scr)

    def chunk(dys_ref, xc_ref, z_ref, yp_ref, hp_ref, dt_ref, gd_ref, ac_ref, act_ref, al_ref, dsk_ref, nw_ref,
              dxc_ref, dz_ref, ddt_ref, dnw_ref, dsk16_ref, da16_ref, db16_ref,
              dh_scr, dy_scr, x_scr, dx_scr, red_scr):
        y = yp_ref[...]
        zv = z_ref[...]
        sz = jax.nn.sigmoid(zv)
        gate = zv * sz
        yg = y * gate
        dout = dys_ref[...]
        nw = nw_ref[...]
        for g in range(2):
            gs = slice(g * 512, (g + 1) * 512)
            grp = yg[:, gs]
            rstd = lax.rsqrt(jnp.mean(grp * grp, axis=1, keepdims=True) + EPS)
            ghat = grp * rstd
            dnw_ref[:, gs] += jnp.sum(dout[:, gs] * ghat, axis=0, keepdims=True)
            gw = dout[:, gs] * nw[:, gs]
            dyg = rstd * (gw - ghat * jnp.mean(gw * ghat, axis=1, keepdims=True))
            dy_scr[:, gs] = dyg * gate[:, gs]
            dz_ref[:, gs] = (dyg * y[:, gs] * (sz[:, gs] * (1.0 + zv[:, gs] * (1.0 - sz[:, gs])))).astype(BF16)

        row = lax.broadcasted_iota(jnp.int32, (L, L), 0)
        col = lax.broadcasted_iota(jnp.int32, (L, L), 1)
        causal = row >= col
        lane128 = lax.broadcasted_iota(jnp.int32, (1, L), 1)
        rows128 = lax.broadcasted_iota(jnp.int32, (L, 1), 0)
        last_row = rows128 == (L - 1)
        expander = _head_expander()
        ac_all = ac_ref[...]
        act_all = act_ref[...]
        dt_all = dt_ref[...]
        dt_e = _spread(dt_all, expander)
        ac_e = _spread(ac_all, expander)
        e_in = jnp.exp(ac_e)
        dec = jnp.exp(ac_e[L - 1:L, :] - ac_e)
        xs_all = xc_ref[:, 0:SSD_WIDTH]
        x_all = xs_all * dt_e
        x_scr[...] = x_all.astype(BF16)
        dy_all = dy_scr[...]
        hp_all = hp_ref[...]
        ds_all = dh_scr[...]
        dsk_cols = jnp.sum(dy_all * xs_all, axis=0, keepdims=True)
        dac = jnp.zeros((L, L), F32)
        dac_row = jnp.zeros((L, L), F32)
        ddec_cols = []
        for g in range(2):
            gs = slice(g * 512, (g + 1) * 512)
            bsl = slice(SSD_WIDTH + g * 128, SSD_WIDTH + (g + 1) * 128)
            csl = slice(SSD_WIDTH + 256 + g * 128, SSD_WIDTH + 256 + (g + 1) * 128)
            bg = xc_ref[:, bsl].astype(BF16)
            cg = xc_ref[:, csl].astype(BF16)
            gmat = _dot(cg, bg, NT_DIMS)
            hpb = hp_all[gs, :].astype(BF16)
            dsb = ds_all[gs, :].astype(BF16)
            ch = _dot(cg, hpb, NT_DIMS)
            dye = dy_all[:, gs] * e_in[:, gs]
            dyeb = dye.astype(BF16)
            dc_acc = _dot(dyeb, hpb)
            dhp = _dot(dyeb, cg, TN_DIMS)
            dxd = _dot(bg, dsb, NT_DIMS)
            db_acc = _dot((x_all[:, gs] * dec[:, gs]).astype(BF16), dsb)
            ddec = dxd * x_all[:, gs] * dec[:, gs]
            ddec_cols.append(jnp.sum(ddec, axis=0, keepdims=True))
            dx_inter = dxd * dec[:, gs]
            red_scr[:, gs] = dye * ch - ddec
            dg_sum = jnp.zeros((L, L), F32)
            for pr in range(4):
                pair = slice((g * 4 + pr) * 128, (g * 4 + pr + 1) * 128)
                x_pair = x_scr[:, pair]
                dy_pair = dy_scr[:, pair].astype(BF16)
                dx_pair = dx_inter[:, pr * 128:(pr + 1) * 128]
                for j in range(2):
                    h = g * 8 + 2 * pr + j
                    r = 2 * pr + j
                    sl = slice(h * HEAD_DIM, (h + 1) * HEAD_DIM)
                    onehot_w = lane128 == h
                    ldec = jnp.exp(jnp.where(causal, ac_all[:, h:h + 1] - act_all[h:h + 1, :], NEG))
                    mf = gmat * ldec
                    dyb = jnp.where((lane128 < HEAD_DIM) == (j == 0), dy_pair, jnp.zeros_like(dy_pair))
                    dm = _dot(dyb, x_pair, NT_DIMS)
                    dx_pair = dx_pair + _dot(mf.astype(BF16), dyb, TN_DIMS)
                    dg_sum = dg_sum + dm * ldec
                    wmat = dm * mf
                    elast = jnp.exp(ac_all[L - 1:L, h:h + 1])
                    hp_h = hp_all[sl, :]
                    ds_h = ds_all[sl, :]
                    extra = elast * jnp.sum(jnp.sum(hp_h * ds_h, axis=1, keepdims=True), axis=0, keepdims=True)
                    dac = dac + jnp.where(onehot_w,
                                          jnp.sum(wmat, axis=1, keepdims=True) + jnp.where(last_row, extra, 0.0), 0.0)
                    dac_row = dac_row + jnp.where(rows128 == h, -jnp.sum(wmat, axis=0, keepdims=True), 0.0)
                    dh_scr[sl, :] = elast * ds_h + dhp[r * HEAD_DIM:(r + 1) * HEAD_DIM, :]
                dx_scr[:, pair] = dx_pair
            dgb = dg_sum.astype(BF16)
            dxc_ref[:, csl] = dc_acc + _dot(dgb, bg)
            dxc_ref[:, bsl] = db_acc + _dot(dgb, cg, TN_DIMS)
        dx_all = dx_scr[...]
        dxc_ref[:, 0:SSD_WIDTH] = dx_all * dt_e + dsk_ref[...] * dy_all
        red = red_scr[...]
        dac_slab = _head_sums(red, expander)
        ddec_tot = _head_sums(jnp.broadcast_to(jnp.concatenate(ddec_cols, axis=1), (8, SSD_WIDTH)), expander)
        ddt_x = _head_sums(dx_all * xs_all, expander)
        dsk16_ref[...] += _head_sums(jnp.broadcast_to(dsk_cols, (8, SSD_WIDTH)), expander)[0:1, 0:16]
        dac = dac + dac_slab + jnp.transpose(dac_row) + jnp.where(last_row, ddec_tot[0:1, :], 0.0)
        triu = (row <= col).astype(F32)
        da = jnp.dot(triu, dac, precision=HIGHEST, preferred_element_type=F32)
        a_row = -jnp.exp(al_ref[...])
        ddt = jnp.where(lane128 < 16, (ddt_x + da * a_row) * gd_ref[...], 0.0)
        ddt_ref[...] = ddt
        da16_ref[...] += (jnp.sum(da * dt_all, axis=0, keepdims=True) * a_row)[:, 0:16]
        db16_ref[...] += jnp.sum(ddt, axis=0, keepdims=True)[:, 0:16]

    cps = SSD_BWD_CHUNKS
    steps = nch // cps
    rb = lambda b, c: (b * steps + steps - 1 - c, 0)
    v1k = pl.BlockSpec((1, SSD_WIDTH), lambda b, c: (0, 0))
    v16 = pl.BlockSpec((1, 16), lambda b, c: (0, 0))
    v128 = pl.BlockSpec((1, 128), lambda b, c: (0, 0))
    wide = pl.BlockSpec((cps * L, SSD_WIDTH), rb)
    s128 = pl.BlockSpec((cps * L, 128), rb)
    return pl.pallas_call(
        body, name="ssd_bwd", grid=(bl, steps),
        in_specs=[wide, pl.BlockSpec((cps * L, CONV_CH), rb), wide, wide,
                  pl.BlockSpec((cps, SSD_WIDTH, SSD_STATE), lambda b, c: (b * steps + steps - 1 - c, 0, 0)),
                  s128, s128, s128, pl.BlockSpec((16, cps * L), lambda b, c: (0, b * steps + steps - 1 - c)),
                  v128, v1k, v1k],
        out_specs=[pl.BlockSpec((cps * L, CONV_CH), rb), wide, s128, v1k, v16, v16, v16],
        out_shape=[jax.ShapeDtypeStruct((n, CONV_CH), F32), jax.ShapeDtypeStruct((n, PA_WIDTH), BF16),
                   jax.ShapeDtypeStruct((n, 128), F32), jax.ShapeDtypeStruct((1, SSD_WIDTH), F32),
                   jax.ShapeDtypeStruct((1, 16), F32), jax.ShapeDtypeStruct((1, 16), F32),
                   jax.ShapeDtypeStruct((1, 16), F32)],
        scratch_shapes=[pltpu.VMEM((SSD_WIDTH, SSD_STATE), F32), pltpu.VMEM((L, SSD_WIDTH), F32),
                        pltpu.VMEM((L, SSD_WIDTH), BF16), pltpu.VMEM((L, SSD_WIDTH), F32),
                        pltpu.VMEM((L, SSD_WIDTH), F32)],
        compiler_params=_cparams(("arbitrary", "arbitrary")),
    )(dys, xc, proj_a, ypre, hprev, dt, gate_d, acum, acum_t, alog128, dskip_e, norm_w)


def _attn_fwd(qkv, negc, bl, t):
    n = bl * t
    tb_ = min(t, ATT_FWD_BLOCK)
    nb = t // tb_
    scale2 = LOG2E / math.sqrt(HEAD_DIM)

    def body(q_ref, k_ref, v_ref, c_ref, o_ref, lse_ref, v0_scr, v1_scr, k0_scr, k1_scr):
        row = lax.broadcasted_iota(jnp.int32, (tb_, tb_), 0)
        col = lax.broadcasted_iota(jnp.int32, (tb_, tb_), 1)
        causal = row >= col
        lane = lax.broadcasted_iota(jnp.int32, (1, 128), 1)
        v_pair = v_ref[...].astype(F32)
        k_pair = k_ref[...]
        v_scrs = (v0_scr, v1_scr)
        k_scrs = (k0_scr, k1_scr)
        for j in range(2):
            v_head = v_pair if j == 0 else pltpu.roll(v_pair, HEAD_DIM, 1)
            v_scrs[j][...] = jnp.where(lane < HEAD_DIM, v_head, jnp.where(lane == HEAD_DIM, 1.0, 0.0)).astype(BF16)
            k_scrs[j][...] = jnp.where((lane < HEAD_DIM) == (j == 0), k_pair, jnp.zeros_like(k_pair))
        for qi in range(nb):
            r0, lk = qi * tb_, (qi + 1) * tb_
            for j in range(2):
                sl = slice(j * HEAD_DIM, (j + 1) * HEAD_DIM)
                s = _dot(q_ref[r0:lk, :], k_scrs[j][0:lk, :], NT_DIMS) * scale2 + c_ref[j:j + 1, 0:lk] * LOG2E
                tail = jnp.where(causal, s[:, r0:lk], NEG)
                s = tail if qi == 0 else jnp.concatenate([s[:, 0:r0], tail], axis=1)
                m = jnp.max(s, axis=1, keepdims=True)
                p = jnp.exp2(s - m)
                acc = _dot(p.astype(BF16), v_scrs[j][0:lk, :])
                l = acc[:, HEAD_DIM:HEAD_DIM + 1]
                o_ref[r0:lk, sl] = (acc[:, 0:HEAD_DIM] / l).astype(BF16)
                lse_ref[r0:lk, sl] = jnp.broadcast_to(m + jnp.log(l) * LOG2E, (tb_, HEAD_DIM))

    blk = lambda off: pl.BlockSpec((t, 128), lambda b, hp: (b, off + hp))
    return pl.pallas_call(
        body, name="fox_attn_fwd", grid=(bl, 8),
        in_specs=[blk(0), blk(8), blk(16), pl.BlockSpec((None, None, 8, t), lambda b, hp: (b, hp, 0, 0))],
        out_specs=[blk(0), blk(0)],
        out_shape=[jax.ShapeDtypeStruct((n, ATT_WIDTH), BF16), jax.ShapeDtypeStruct((n, ATT_WIDTH), F32)],
        scratch_shapes=[pltpu.VMEM((t, 128), BF16)] * 4,
        compiler_params=_cparams(("parallel", "parallel")),
    )(qkv, qkv, qkv, negc)


def _attn_bwd(qkv, do, o, lse, negc, after, bl, t):
    n = bl * t
    tb_ = min(t, ATT_BWD_BLOCK)
    nb = t // tb_
    scale = 1.0 / math.sqrt(HEAD_DIM)
    scale2 = LOG2E * scale

    def body(q_ref, k_ref, v_ref, do_ref, o_ref, lse_ref, c_ref, after_ref, dq_ref, dk_ref, dv_ref, dc_ref,
             dq0_scr, delta_scr, dq1_scr, qt0_scr, qt1_scr, dot_scr, dkt0_scr, dkt1_scr, dvt_scr):
        row = lax.broadcasted_iota(jnp.int32, (tb_, tb_), 0)
        col = lax.broadcasted_iota(jnp.int32, (tb_, tb_), 1)
        causal = row >= col
        lane = lax.broadcasted_iota(jnp.int32, (1, 128), 1)
        dq_scrs = (dq0_scr, dq1_scr)
        qt_scrs = (qt0_scr, qt1_scr)
        dkt_scrs = (dkt0_scr, dkt1_scr)
        dq0_scr[...] = jnp.zeros_like(dq0_scr)
        dq1_scr[...] = jnp.zeros_like(dq1_scr)
        dc_ref[...] = jnp.zeros_like(dc_ref)
        q_t = jnp.transpose(q_ref[...].astype(F32))
        ones_row = jnp.where(lax.broadcasted_iota(jnp.int32, (8, t), 0) == 0, 1.0, 0.0)
        for j in range(2):
            qt_scrs[j][...] = jnp.concatenate(
                [q_t[j * HEAD_DIM:(j + 1) * HEAD_DIM, :], ones_row, jnp.zeros((HEAD_DIM - 8, t), F32)],
                axis=0).astype(BF16)
        dot_scr[...] = jnp.transpose(do_ref[...].astype(F32)).astype(BF16)
        prod = do_ref[...].astype(F32) * o_ref[...].astype(F32)
        for j in range(2):
            sl = slice(j * HEAD_DIM, (j + 1) * HEAD_DIM)
            delta_scr[:, sl] = jnp.broadcast_to(jnp.sum(prod[:, sl], axis=1, keepdims=True), (t, HEAD_DIM))
        for kj in range(nb):
            r0, r1 = kj * tb_, (kj + 1) * tb_
            k_blk = k_ref[r0:r1, :]
            v_blk = v_ref[r0:r1, :]
            k_pair = k_blk.astype(F32)
            for j in range(2):
                sl = slice(j * HEAD_DIM, (j + 1) * HEAD_DIM)
                one = slice(j * HEAD_DIM, j * HEAD_DIM + 1)
                own = (lane < HEAD_DIM) == (j == 0)
                k_head = k_pair if j == 0 else pltpu.roll(k_pair, HEAD_DIM, 1)
                k_ones = jnp.where(lane < HEAD_DIM, k_head, jnp.where(lane == HEAD_DIM, 1.0, 0.0)).astype(BF16)
                s = (_dot(q_ref[r0:t, :], jnp.where(own, k_blk, jnp.zeros_like(k_blk)), NT_DIMS) * scale2
                     + c_ref[j:j + 1, r0:r1] * LOG2E)
                head = jnp.where(causal, s[0:tb_, :], NEG)
                s = head if kj == nb - 1 else jnp.concatenate([head, s[tb_:, :]], axis=0)
                p = jnp.exp2(s - lse_ref[r0:t, one])
                dp = _dot(do_ref[r0:t, :], jnp.where(own, v_blk, jnp.zeros_like(v_blk)), NT_DIMS)
                ds = p * (dp - delta_scr[r0:t, one])
                dsb = ds.astype(BF16)
                dvt_scr[sl, r0:r1] = _dot(dot_scr[sl, r0:t], p.astype(BF16))
                dkt_scrs[j][:, r0:r1] = _dot(qt_scrs[j][:, r0:t], dsb)
                dq_scrs[j][r0:t, :] += _dot(dsb, k_ones)
        dv_ref[...] = jnp.transpose(dvt_scr[...]).astype(BF16)
        for j in range(2):
            sl = slice(j * HEAD_DIM, (j + 1) * HEAD_DIM)
            acc = dq_scrs[j][...]
            dkt = dkt_scrs[j][...]
            dq_ref[:, sl] = (acc[:, 0:HEAD_DIM] * scale).astype(BF16)
            dk_ref[:, sl] = (jnp.transpose(dkt)[:, 0:HEAD_DIM] * scale).astype(BF16)
            dc_ref[j:j + 1, :] = jnp.transpose(acc)[HEAD_DIM:HEAD_DIM + 1, :] - dkt[HEAD_DIM:HEAD_DIM + 1, :]

    blk = lambda off: pl.BlockSpec((t, 128), lambda b, hp: (b, off + hp))
    cblk = pl.BlockSpec((None, None, 8, t), lambda b, hp: (b, hp, 0, 0))
    return pl.pallas_call(
        body, name="fox_attn_bwd", grid=(bl, 8),
        in_specs=[blk(0), blk(8), blk(16), blk(0), blk(0), blk(0), cblk, ANY],
        out_specs=[blk(0), blk(0), blk(0), cblk],
        out_shape=[jax.ShapeDtypeStruct((n, ATT_WIDTH), BF16)] * 3 + [jax.ShapeDtypeStruct((bl, 8, 8, t), F32)],
        scratch_shapes=[pltpu.VMEM((t, 128), F32), pltpu.VMEM((t, 128), F32), pltpu.VMEM((t, 128), F32),
                        pltpu.VMEM((128, t), BF16), pltpu.VMEM((128, t), BF16), pltpu.VMEM((128, t), BF16),
                        pltpu.VMEM((128, t), F32), pltpu.VMEM((128, t), F32), pltpu.VMEM((128, t), F32)],
        compiler_params=_cparams(("parallel", "parallel")),
    )(qkv, qkv, qkv, do, o, lse, negc, after)


def _adamw(w, g, m, v, *, name):
    lead = w.ndim == 3
    r, c = w.shape[-2:]
    tr = _pick(r, (256, IN_SHARD // 3, 128, 64, 32, 16, 8))
    bc1 = 1.0 - ADAM_B1 ** ADAM_STEP
    bc2 = 1.0 - ADAM_B2 ** ADAM_STEP

    def body(w_ref, g_ref, m_ref, v_ref, d_ref, nm_ref, nv_ref):
        gv = g_ref[...]
        mn = ADAM_B1 * m_ref[...] + (1.0 - ADAM_B1) * gv
        vn = ADAM_B2 * v_ref[...] + (1.0 - ADAM_B2) * (gv * gv)
        m_hat = mn / bc1
        v_hat = vn / bc2
        d_ref[...] = -ADAM_LR * (m_hat / (jnp.sqrt(v_hat) + ADAM_EPS) + ADAM_WD * w_ref[...])
        nm_ref[...] = mn
        nv_ref[...] = vn

    flat = pl.BlockSpec((tr, c), lambda i: (i, 0))
    blk = pl.BlockSpec((None, tr, c), lambda i: (0, i, 0)) if lead else flat
    return pl.pallas_call(
        body, name=name, grid=(r // tr,), in_specs=[blk, flat, blk, blk], out_specs=[blk] * 3,
        out_shape=[jax.ShapeDtypeStruct(w.shape, F32)] * 3,
        compiler_params=_cparams(("parallel",)),
    )(w, g, m, v)


def _sum_leading(parts, *, name, out_dtype=F32):
    k, r, c = parts.shape
    tr = _pick(r, (512, 256, 128, 96, 64, 32, 16, 8))

    def body(p_ref, o_ref):
        acc = p_ref[0].astype(F32)
        for i in range(1, k):
            acc = acc + p_ref[i].astype(F32)
        o_ref[...] = acc.astype(out_dtype)

    return pl.pallas_call(
        body, name=name, grid=(r // tr,),
        in_specs=[pl.BlockSpec((k, tr, c), lambda i: (0, i, 0))],
        out_specs=pl.BlockSpec((tr, c), lambda i: (i, 0)),
        out_shape=jax.ShapeDtypeStruct((r, c), out_dtype),
        compiler_params=_cparams(("parallel",)),
    )(parts)


def _add_my_half(g, b, *, name):
    k, r, c = b.shape
    tr = _pick(r, (512, 256, 128))
    nrt = r // tr

    def body(lo_ref, hi_ref, b_ref, o_ref):
        mine = jnp.where(lax.axis_index("c") == 0, lo_ref[...], hi_ref[...])
        o_ref[...] = (mine.astype(F32) + b_ref[...].astype(F32)).astype(BF16)

    blk = pl.BlockSpec((None, tr, c), lambda j, i: (j, i, 0))
    return pl.pallas_call(
        body, name=name, grid=(k, nrt),
        in_specs=[blk, pl.BlockSpec((None, tr, c), lambda j, i: (j, i + nrt, 0)), blk], out_specs=blk,
        out_shape=jax.ShapeDtypeStruct((k, r, c), BF16),
        compiler_params=_cparams(("parallel", "parallel")),
    )(g, g, b)


ANY = pl.BlockSpec(memory_space=pl.ANY)


def _chip_peers(x, y):
    return [(1 - x, y, 2 * (1 - x) + y), (x, 1 - y, 2 * x + 1 - y), (1 - x, 1 - y, 2 * (1 - x) + 1 - y)]


def _gather_weights(blob, *, name):
    rows, cols = blob.shape
    half_rows = rows // 2

    def body(b_ref, o_ref, send_sems, recv_sems):
        x, y, c = lax.axis_index("x"), lax.axis_index("y"), lax.axis_index("c")
        me = 2 * x + y
        sibling = (x, y, 1 - c)
        peers = _chip_peers(x, y)

        def half(chip, hc):
            return o_ref.at[chip, pl.ds(hc * half_rows, half_rows), :]

        def copy(k, src, chip, hc, to):
            return pltpu.make_async_remote_copy(src_ref=src, dst_ref=half(chip, hc), send_sem=send_sems.at[k],
                                                recv_sem=recv_sems.at[k], device_id=to, device_id_type=MESH)

        my_half = b_ref.at[pl.ds(c * half_rows, half_rows), :]
        first = [copy(k, my_half, me, c, (px, py, c)) for k, (px, py, _) in enumerate(peers)]
        own = pltpu.make_async_remote_copy(src_ref=b_ref, dst_ref=o_ref.at[me], send_sem=send_sems.at[6],
                                           recv_sem=recv_sems.at[6], device_id=sibling, device_id_type=MESH)
        for cp in first + [own]:
            cp.start()
        passed = [copy(3 + k, half(pc, c), pc, c, sibling) for k, (_, _, pc) in enumerate(peers)]
        for k, (px, py, pc) in enumerate(peers):
            copy(k, my_half, pc, c, (px, py, c)).wait_recv()
            passed[k].start()
        for k, (_, _, pc) in enumerate(peers):
            copy(3 + k, half(pc, 1 - c), pc, 1 - c, sibling).wait_recv()
        own.wait_recv()
        for cp in first + passed + [own]:
            cp.wait_send()

    return pl.pallas_call(
        body, name=name, in_specs=[ANY], out_specs=ANY,
        out_shape=jax.ShapeDtypeStruct((N_CHIPS, rows, cols), BF16),
        scratch_shapes=[pltpu.SemaphoreType.DMA((7,)), pltpu.SemaphoreType.DMA((7,))],
    )(blob)


def _swap_halves(g, *, name):
    _, rows, cols = g.shape
    half_rows = rows // 2

    def body(g_ref, o_ref, send_sem, recv_sem):
        x, y, c = lax.axis_index("x"), lax.axis_index("y"), lax.axis_index("c")
        cp = pltpu.make_async_remote_copy(
            src_ref=g_ref.at[:, pl.ds((1 - c) * half_rows, half_rows), :], dst_ref=o_ref,
            send_sem=send_sem, recv_sem=recv_sem, device_id=(x, y, 1 - c), device_id_type=MESH)
        cp.start()
        cp.wait()

    return pl.pallas_call(
        body, name=name, in_specs=[ANY], out_specs=ANY,
        out_shape=jax.ShapeDtypeStruct((N_CHIPS, half_rows, cols), BF16),
        scratch_shapes=[pltpu.SemaphoreType.DMA, pltpu.SemaphoreType.DMA],
    )(g)


HBM_SPEC = pl.BlockSpec(memory_space=pltpu.HBM)
SEM_SPEC = pl.BlockSpec(memory_space=pltpu.SEMAPHORE)
SPLIT_EFFECT = pltpu.SideEffectType.DATAFLOW_SIDE_EFFECTING


def _gather_peers_copies(b_ref, land_ref, send_sems, recv_sems, sending):
    x, y, c = lax.axis_index("x"), lax.axis_index("y"), lax.axis_index("c")
    me = 2 * x + y
    half_rows = b_ref.shape[0] // 2
    src = b_ref.at[pl.ds(c * half_rows, half_rows), :]
    return [pltpu.make_async_remote_copy(
        src_ref=src, dst_ref=land_ref.at[me if sending else pc, pl.ds(c * half_rows, half_rows), :],
        send_sem=send_sems.at[k], recv_sem=recv_sems.at[k], device_id=(px, py, c), device_id_type=MESH)
        for k, (px, py, pc) in enumerate(_chip_peers(x, y))]


def _gather_start(blob, after, *, name):
    shape = (N_CHIPS,) + blob.shape

    def body(b_ref, land_ref, after_ref, send_sems, recv_sems, b_thru, land_thru, token):
        for cp in _gather_peers_copies(b_ref, land_ref, send_sems, recv_sems, True):
            cp.start()
        token[...] = jnp.zeros_like(token)

    return pl.pallas_call(
        body, name=name,
        out_shape=(pltpu.SemaphoreType.DMA((3,)), pltpu.SemaphoreType.DMA((3,)), pltpu.HBM(blob.shape, blob.dtype),
                   pltpu.HBM(shape, blob.dtype), jax.ShapeDtypeStruct((8, 128), F32)),
        in_specs=(HBM_SPEC, HBM_SPEC, ANY),
        out_specs=(SEM_SPEC, SEM_SPEC, HBM_SPEC, HBM_SPEC, pl.BlockSpec(memory_space=pltpu.VMEM)),
        input_output_aliases={0: 2, 1: 3},
        compiler_params=pltpu.CompilerParams(has_side_effects=SPLIT_EFFECT),
    )(pltpu.with_memory_space_constraint(blob, pltpu.HBM),
      pltpu.with_memory_space_constraint(lax.empty(shape, blob.dtype), pltpu.HBM), after)


def _gather_wait(send_sems, recv_sems, b_thru, land_thru, after, *, name):
    def body(b_ref, land_ref, send_sems, recv_sems, after_ref, b_dead, got_ref):
        for cp in _gather_peers_copies(b_ref, land_ref, send_sems, recv_sems, False):
            cp.wait_send()
            cp.wait_recv()

    return pl.pallas_call(
        body, name=name,
        out_shape=(pltpu.HBM(b_thru.shape, b_thru.dtype), pltpu.HBM(land_thru.shape, land_thru.dtype)),
        in_specs=(HBM_SPEC, HBM_SPEC, SEM_SPEC, SEM_SPEC, ANY), out_specs=(HBM_SPEC, HBM_SPEC),
        input_output_aliases={0: 0, 1: 1},
        compiler_params=pltpu.CompilerParams(has_side_effects=SPLIT_EFFECT),
    )(b_thru, land_thru, send_sems, recv_sems, after)


def _gather_forward(land, blob, *, name):
    half_rows = land.shape[1] // 2

    def body(l_ref, b_ref, o_ref, send_sems, recv_sems):
        x, y, c = lax.axis_index("x"), lax.axis_index("y"), lax.axis_index("c")
        me = 2 * x + y
        sibling = (x, y, 1 - c)
        cps = []
        for k, (_, _, pc) in enumerate(_chip_peers(x, y)):
            mine = pl.ds(c * half_rows, half_rows)
            cps.append(pltpu.make_async_remote_copy(
                src_ref=l_ref.at[pc, mine, :], dst_ref=o_ref.at[pc, mine, :], send_sem=send_sems.at[k],
                recv_sem=recv_sems.at[k], device_id=sibling, device_id_type=MESH))
        cps.append(pltpu.make_async_remote_copy(src_ref=b_ref, dst_ref=o_ref.at[me], send_sem=send_sems.at[3],
                                                recv_sem=recv_sems.at[3], device_id=sibling, device_id_type=MESH))
        for cp in cps:
            cp.start()
        for k, (_, _, pc) in enumerate(_chip_peers(x, y)):
            theirs = pl.ds((1 - c) * half_rows, half_rows)
            pltpu.make_async_remote_copy(
                src_ref=l_ref.at[pc, theirs, :], dst_ref=o_ref.at[pc, theirs, :], send_sem=send_sems.at[k],
                recv_sem=recv_sems.at[k], device_id=sibling, device_id_type=MESH).wait_recv()
        cps[3].wait_recv()
        for cp in cps:
            cp.wait_send()

    return pl.pallas_call(
        body, name=name, in_specs=[ANY, ANY], out_specs=ANY, input_output_aliases={0: 0},
        out_shape=jax.ShapeDtypeStruct(land.shape, land.dtype),
        scratch_shapes=[pltpu.SemaphoreType.DMA((4,)), pltpu.SemaphoreType.DMA((4,))],
    )(land, blob)


def _exchange_peers_copies(p_ref, land_ref, send_sems, recv_sems, sending):
    x, y, c = lax.axis_index("x"), lax.axis_index("y"), lax.axis_index("c")
    me = 2 * x + y
    return [pltpu.make_async_remote_copy(src_ref=p_ref.at[pc], dst_ref=land_ref.at[me if sending else pc],
                                         send_sem=send_sems.at[k], recv_sem=recv_sems.at[k],
                                         device_id=(px, py, c), device_id_type=MESH)
            for k, (px, py, pc) in enumerate(_chip_peers(x, y))]


def _exchange_start(p, *, name):
    def body(p_ref, land_ref, send_sems, recv_sems, p_thru, land_thru, token):
        for cp in _exchange_peers_copies(p_ref, land_ref, send_sems, recv_sems, True):
            cp.start()
        token[...] = jnp.zeros_like(token)

    return pl.pallas_call(
        body, name=name,
        out_shape=(pltpu.SemaphoreType.DMA((3,)), pltpu.SemaphoreType.DMA((3,)), pltpu.HBM(p.shape, p.dtype),
                   pltpu.HBM(p.shape, p.dtype), jax.ShapeDtypeStruct((8, 128), F32)),
        in_specs=(HBM_SPEC, HBM_SPEC),
        out_specs=(SEM_SPEC, SEM_SPEC, HBM_SPEC, HBM_SPEC, pl.BlockSpec(memory_space=pltpu.VMEM)),
        input_output_aliases={0: 2, 1: 3},
        compiler_params=pltpu.CompilerParams(has_side_effects=SPLIT_EFFECT),
    )(pltpu.with_memory_space_constraint(p, pltpu.HBM),
      pltpu.with_memory_space_constraint(lax.empty(p.shape, p.dtype), pltpu.HBM))


def _exchange_wait(send_sems, recv_sems, p_thru, land_thru, after, *, name):
    def body(p_ref, land_ref, send_sems, recv_sems, after_ref, p_dead, got_ref):
        for cp in _exchange_peers_copies(p_ref, land_ref, send_sems, recv_sems, False):
            cp.wait_send()
            cp.wait_recv()

    return pl.pallas_call(
        body, name=name,
        out_shape=(pltpu.HBM(p_thru.shape, p_thru.dtype), pltpu.HBM(p_thru.shape, p_thru.dtype)),
        in_specs=(HBM_SPEC, HBM_SPEC, SEM_SPEC, SEM_SPEC, ANY), out_specs=(HBM_SPEC, HBM_SPEC),
        input_output_aliases={0: 0, 1: 1},
        compiler_params=pltpu.CompilerParams(has_side_effects=SPLIT_EFFECT),
    )(p_thru, land_thru, send_sems, recv_sems, after)


def _sum_parts(parts, own, *, name):
    k, r, c = parts.shape
    tr = _pick(r, (512, 256, 128))

    def body(p_ref, own_ref, o_ref):
        me = 2 * lax.axis_index("x") + lax.axis_index("y")
        acc = jnp.zeros((tr, c), F32)
        for i in range(k):
            acc = acc + jnp.where(me == i, own_ref[i], p_ref[i]).astype(F32)
        o_ref[...] = acc

    blk = pl.BlockSpec((k, tr, c), lambda i: (0, i, 0))
    return pl.pallas_call(
        body, name=name, grid=(r // tr,), in_specs=[blk, blk],
        out_specs=pl.BlockSpec((tr, c), lambda i: (i, 0)),
        out_shape=jax.ShapeDtypeStruct((r, c), F32),
        compiler_params=_cparams(("parallel",)),
    )(parts, own)


def _from_sibling(a, *, name):
    def body(a_ref, o_ref, send_sem, recv_sem):
        x, y, c = lax.axis_index("x"), lax.axis_index("y"), lax.axis_index("c")
        cp = pltpu.make_async_remote_copy(src_ref=a_ref, dst_ref=o_ref, send_sem=send_sem, recv_sem=recv_sem,
                                          device_id=(x, y, 1 - c), device_id_type=MESH)
        cp.start()
        cp.wait()

    return pl.pallas_call(
        body, name=name, in_specs=[ANY], out_specs=ANY,
        out_shape=jax.ShapeDtypeStruct(a.shape, F32),
        scratch_shapes=[pltpu.SemaphoreType.DMA, pltpu.SemaphoreType.DMA],
    )(a)


def _join_halves(gh, *, name):
    other = _from_sibling(gh, name=name)
    south = lax.axis_index("c") == 0
    return jnp.concatenate([jnp.where(south, gh, other), jnp.where(south, other, gh)], axis=0)


def _add_f32(a, b, *, name):
    r, c = a.shape
    tr = _pick(r, (512, 256, 128))

    def body(a_ref, b_ref, o_ref):
        o_ref[...] = a_ref[...] + b_ref[...]

    blk = pl.BlockSpec((tr, c), lambda i: (i, 0))
    return pl.pallas_call(
        body, name=name, grid=(r // tr,), in_specs=[blk, blk], out_specs=blk,
        out_shape=jax.ShapeDtypeStruct((r, c), F32),
        compiler_params=_cparams(("parallel",)),
    )(a, b)


def _gather_small(s, *, name):
    rows = s.shape[0]

    def body(s_ref, o_ref, send_sems, recv_sems, local_sem):
        x, y, c = lax.axis_index("x"), lax.axis_index("y"), lax.axis_index("c")
        me = 4 * x + 2 * y + c
        mine = pltpu.make_async_copy(s_ref, o_ref.at[me], local_sem)
        mine.start()
        peers = []
        for k in range(1, 8):
            peers.append((1 - x if k & 4 else x, 1 - y if k & 2 else y, 1 - c if k & 1 else c))
        cps = [pltpu.make_async_remote_copy(src_ref=s_ref, dst_ref=o_ref.at[me], send_sem=send_sems.at[k],
                                            recv_sem=recv_sems.at[k], device_id=p, device_id_type=MESH)
               for k, p in enumerate(peers)]
        for cp in cps:
            cp.start()
        for k, (px, py, pc) in enumerate(peers):
            pltpu.make_async_remote_copy(src_ref=s_ref, dst_ref=o_ref.at[4 * px + 2 * py + pc],
                                         send_sem=send_sems.at[k], recv_sem=recv_sems.at[k],
                                         device_id=(px, py, pc), device_id_type=MESH).wait_recv()
        for cp in cps:
            cp.wait_send()
        mine.wait()

    return pl.pallas_call(
        body, name=name, in_specs=[ANY], out_specs=ANY,
        out_shape=jax.ShapeDtypeStruct((8, rows, 128), F32),
        scratch_shapes=[pltpu.SemaphoreType.DMA((7,)), pltpu.SemaphoreType.DMA((7,)), pltpu.SemaphoreType.DMA],
    )(s)


IN_SHARD = IN_WIDTH // N_CHIPS
IN_SHARD_PAD = 1536
UP_ROWS, DOWN_ROWS, OUT_ROWS = 1024, 1024, 512
REST_ROWS = UP_ROWS + DOWN_ROWS + OUT_ROWS


def _pack_in(w_in_s):
    return jnp.pad(w_in_s, ((0, 0), (0, IN_SHARD_PAD - IN_SHARD))).astype(BF16)


def _pack_rest(w_out_s, w_up_s, w_down_s):
    return jnp.concatenate([w_up_s, w_down_s, w_out_s], axis=0).astype(BF16)


def _unpack_rest(blob):
    return (blob[UP_ROWS + DOWN_ROWS:], blob[0:UP_ROWS], blob[UP_ROWS:UP_ROWS + DOWN_ROWS])


def _full_w_in(g_in):
    return jnp.concatenate([g_in[j, :, :IN_SHARD] for j in range(N_CHIPS)], axis=1)


def _full_rest(g_rest):
    parts = [_unpack_rest(g_rest[j]) for j in range(N_CHIPS)]
    w_out = jnp.concatenate([p[0] for p in parts], axis=0)
    w_up = jnp.concatenate([p[1] for p in parts], axis=1)
    w_down = jnp.concatenate([p[2] for p in parts], axis=0)
    return w_out, w_up, w_down


def _split_w_in(w_in):
    z_xbc = w_in[:, 0:2560]
    dt = w_in[:, 2560:2576]
    qkv = w_in[:, 2576:5648]
    f = w_in[:, 5648:5664]
    pad = jnp.zeros((w_in.shape[0], PA_WIDTH - 2592), w_in.dtype)
    return jnp.concatenate([z_xbc, dt, f, pad], axis=1), qkv


def _merge_w_in(d_a, d_qkv):
    return jnp.concatenate([d_a[:, 0:2560], d_a[:, 2560:2576], *d_qkv, d_a[:, 2576:2592]], axis=1)


def _local_step(x3, target3, w_in, rest_weights, norm_mix_w, conv_w, conv_b, dt_bias, a_log, d_skip,
                ssd_norm_w, f_bias, norm_mlp_w, norm_final_w, first_after=None, early_grads=None, late_grads=None):
    bl, t, d = x3.shape
    n = bl * t
    x = x3.reshape(n, d)
    target = target3.reshape(n, d)
    w_a, w_qkv = _split_w_in(w_in)
    nfw = norm_final_w.reshape(1, d)
    dskip_e = jnp.repeat(d_skip, HEAD_DIM, axis=1)

    r1, r2, kt = min(n, 1024), min(n, 512), min(n, 2048)
    if first_after is None:
        first_after = jnp.zeros((8, 128), F32)
    h0, rstd0, proj_a = _norm_mm(x, norm_mix_w, w_a, first_after, name="norm_mix_proj_a", tm=r2)
    qkv = _mm(h0, w_qkv, name="proj_qkv", tiles=(r2, QKV_WIDTH, D_MODEL), out_dtype=BF16)
    bias128 = jnp.concatenate([dt_bias, f_bias, jnp.zeros((1, 96), F32)], axis=1)
    alog128 = jnp.concatenate([a_log, jnp.zeros((1, 112), F32)], axis=1)
    dt, gate_d, acum, acum_t, negc = _prep(proj_a, bias128, alog128, bl, t)
    xc, dsilu = _conv_fwd(proj_a, conv_w, conv_b, bl, t)
    y_ssd, y_pre, hprev = _ssd_fwd(xc, proj_a, dt, acum, acum_t, dskip_e, ssd_norm_w, bl, t)
    y_att, lse = _attn_fwd(qkv, negc, bl, t)
    w_out, w_up, w_down = rest_weights(y_att)
    wo_s, wo_a = w_out[:SSD_WIDTH], w_out[SSD_WIDTH:]
    h1, h1n, rstd1 = _mm_norm_fwd(y_ssd, wo_s, y_att, wo_a, x, norm_mlp_w, name="out_proj_norm_mlp", tm=r2)
    up = _mm(h1n, w_up, name="mlp_up", tiles=(r1, D_FF, D_MODEL), out_dtype=BF16)
    dh2, dh2b, loss, d_nfw = _mm_final(up, w_down, h1, nfw, target, name="mlp_down_final_norm_loss", tm=r2,
                                       a_act="relu2")

    dup = _mm(dh2b, w_down, name="mlp_down_bwd_act", tiles=(r2, D_FF, D_MODEL), tb=True, epi_up=up, out_dtype=BF16)
    rest_shape = (N_CHIPS, REST_ROWS, D_MODEL)
    gb_rest = _mm(up, dh2b, name="mlp_down_bwd_w", tiles=(DOWN_ROWS, D_MODEL, kt), ta=True, a_act="relu2",
                  out_dtype=BF16, into=(rest_shape, (None, DOWN_ROWS, D_MODEL), lambda i, j, k: (i, 1, 0), None))
    dh1, dh1b, d_nmlp = _mm_norm_bwd([(dup, w_up)], h1, rstd1, norm_mlp_w, dh2, name="mlp_up_bwd_act_norm_mlp",
                                     tm=r2)
    gb_rest = _mm(h1n, dup, name="mlp_up_bwd_w", tiles=(D_MODEL, UP_ROWS, kt), ta=True, out_dtype=BF16,
                  into=(rest_shape, (None, D_MODEL, UP_ROWS), lambda i, j, k: (j, 0, 0), gb_rest))
    dys, do = _mm_two_halves(dh1b, w_out, name="out_proj_bwd_act", tm=r1)
    out_block = (UP_ROWS + DOWN_ROWS) // OUT_ROWS
    for half, (y_half, tag) in enumerate(((y_ssd, "ssd"), (y_att, "att"))):
        gb_rest = _mm(y_half, dh1b, name="out_proj_bwd_w_" + tag, tiles=(2 * OUT_ROWS, D_MODEL, kt), ta=True,
                      out_dtype=BF16, into=(rest_shape, (2, OUT_ROWS, D_MODEL),
                                            functools.partial(lambda i, j, k, h: (h, out_block, 0), h=half),
                                            gb_rest))
    token = jnp.zeros((8, 128), F32) if early_grads is None else early_grads(gb_rest)
    dq, dk, dv, dcb = _attn_bwd(qkv, do, y_att, lse, negc, token, bl, t)
    dc = jnp.pad(dcb[:, :, 0:2, :].transpose(0, 3, 1, 2).reshape(n, 16), ((0, 0), (16, 96)))
    dxc, dpa, ddt_raw, d_snw, d_dsk, d_alog, d_dtb = _ssd_bwd(dys, xc, proj_a, y_pre, hprev, dt, gate_d, acum,
                                                             acum_t, alog128, dskip_e, ssd_norm_w, bl, t)
    dpa, d_conv_w, d_conv_b = _conv_bwd(dxc, dsilu, proj_a, conv_w, dpa, bl, t)
    dproj_a, d_fb = _fpost(dc, gate_d, ddt_raw, dpa, bl, t)
    d_w_a = _mm(h0, dproj_a, name="proj_a_bwd_w", tiles=(1024, 896, kt), ta=True, out_dtype=BF16)
    d_heads = (("q", dq), ("k", dk), ("v", dv))
    d_w_qkv = [_mm(h0, g, name=f"proj_{tag}_bwd_w", tiles=(1024, 1024, kt), ta=True, out_dtype=BF16)
               for tag, g in d_heads]
    d_w_in = _merge_w_in(d_w_a, d_w_qkv)
    late_token = None if late_grads is None else late_grads(d_w_in)
    head_pairs = [(g, w_qkv, j) for j, (_, g) in enumerate(d_heads)]
    dx, _, d_nmix = _mm_norm_bwd([(dproj_a, w_a)] + head_pairs, x, rstd0, norm_mix_w, dh1,
                                 name="proj_bwd_act_norm_mix", tm=min(n, 256), after=late_token)

    grads = dict(norm_mix_w=d_nmix, w_in=d_w_in, conv_w=d_conv_w, conv_b=d_conv_b,
                 dt_bias=d_dtb, a_log=d_alog, d_skip=d_dsk, ssd_norm_w=d_snw, f_bias=d_fb, rest=gb_rest,
                 norm_mlp_w=d_nmlp, norm_final_w=d_nfw)
    return dx.reshape(bl, t, d), loss, grads


SMALL_ORDER = ("norm_mix_w", "conv_w", "conv_b", "dt_bias", "a_log", "d_skip", "ssd_norm_w", "f_bias",
               "norm_mlp_w", "norm_final_w")
SMALL_SIZES = (1024, 4 * CONV_CH, CONV_CH, 16, 16, 16, 1024, 16, 1024, 1024)


def _pack_small(vals, rows):
    flat = jnp.concatenate([v.reshape(-1).astype(F32) for v in vals])
    return jnp.pad(flat, (0, rows * 128 - flat.shape[0])).reshape(rows, 128)


def _unpack_small(packed, sizes):
    flat = packed.reshape(-1)
    out, o = [], 0
    for s in sizes:
        out.append(flat[o:o + s])
        o += s
    return out


def kernel(x, norm_mix_w, w_in, conv_w, conv_b, dt_bias, a_log, d_skip, ssd_norm_w, f_bias, w_out, norm_mlp_w, w_up, w_down, norm_final_w, loss_target, m_norm_mix_w, m_w_in, m_conv_w, m_conv_b, m_dt_bias, m_a_log, m_d_skip, m_ssd_norm_w, m_f_bias, m_w_out, m_norm_mlp_w, m_w_up, m_w_down, m_norm_final_w, v_norm_mix_w, v_w_in, v_conv_w, v_conv_b, v_dt_bias, v_a_log, v_d_skip, v_ssd_norm_w, v_f_bias, v_w_out, v_norm_mlp_w, v_w_up, v_w_down, v_norm_final_w):
    chip = 2 * lax.axis_index("x") + lax.axis_index("y")
    cw = CONV_CH // N_CHIPS

    own_in = _pack_in(w_in[0])
    own_rest = _pack_rest(w_out[0], w_up[0], w_down[0])
    g_in = _gather_weights(own_in, name="gather_w_in")
    w_in_f = _full_w_in(g_in)
    *rest_handles, rest_token = _gather_start(own_rest, g_in, name="gather_start_rest")

    def rest_weights(after):
        _, landed = _gather_wait(*rest_handles, after, name="gather_wait_rest")
        return _full_rest(_gather_forward(landed, own_rest, name="gather_forward_rest"))
    small_all = _gather_small(_pack_small([conv_w[0]], 16), name="gather_conv_w")
    conv_w_f = jnp.concatenate([small_all[2 * j].reshape(-1)[:4 * cw].reshape(4, cw) for j in range(N_CHIPS)], axis=1)

    def chip_partial(gb, tag):
        from_sibling = _swap_halves(gb, name="grad_swap_halves_" + tag)
        return _add_my_half(gb, from_sibling, name="grad_add_sibling_" + tag)

    in_flight = {}

    def early_grads(gb_rest):
        *handles, token = _exchange_start(gb_rest, name="grad_exchange_start_rest")
        in_flight["rest"] = handles
        return token

    def late_grads(d_w_in):
        gb_in = jnp.stack([_pack_in(d_w_in[:, j * IN_SHARD:(j + 1) * IN_SHARD]) for j in range(N_CHIPS)])
        *handles, token = _exchange_start(chip_partial(gb_in, "in"), name="grad_exchange_start_in")
        in_flight["in"] = handles
        return token

    dx, loss_part, g = _local_step(x, loss_target, w_in_f, rest_weights, norm_mix_w, conv_w_f,
                                   conv_b, dt_bias, a_log, d_skip, ssd_norm_w, f_bias, norm_mlp_w, norm_final_w,
                                   first_after=rest_token, early_grads=early_grads, late_grads=late_grads)

    send_sems, recv_sems, part_rest, land_rest = in_flight["rest"]
    part_rest, parts_rest = _exchange_wait(send_sems, recv_sems, part_rest, land_rest, dx,
                                           name="grad_exchange_wait_rest")
    g_rest_core = _sum_parts(parts_rest, part_rest, name="grad_sum_chips_rest")
    g_rest_sibling = _from_sibling(g_rest_core, name="grad_swap_sums_rest")
    g_w_out, g_w_up, g_w_down = _unpack_rest(_add_f32(g_rest_core, g_rest_sibling, name="grad_add_cores_rest"))

    part_in, parts_in = _exchange_wait(*in_flight["in"], dx, name="grad_exchange_wait_in")
    g_in_half = _sum_parts(parts_in, part_in, name="grad_sum_chips_in")
    g_w_in = _join_halves(g_in_half, name="grad_join_halves_in")[:, :IN_SHARD]

    small_vals = [g[k] for k in SMALL_ORDER] + [loss_part[:, 0:1]]
    small_sum = _sum_leading(_gather_small(_pack_small(small_vals, SMALL_ROWS), name="gather_small_grads"), name="small_sum")
    sg = dict(zip(SMALL_ORDER + ("loss",), _unpack_small(small_sum, SMALL_SIZES + (1,))))
    loss = sg["loss"].reshape(())
    g_conv_full = sg["conv_w"].reshape(4, CONV_CH)
    g_conv = lax.dynamic_slice_in_dim(g_conv_full, chip * cw, cw, axis=1)

    grads = dict(norm_mix_w=sg["norm_mix_w"].reshape(1, -1), w_in=g_w_in[None], conv_w=g_conv[None],
                 conv_b=sg["conv_b"].reshape(1, -1), dt_bias=sg["dt_bias"].reshape(1, -1),
                 a_log=sg["a_log"].reshape(1, -1), d_skip=sg["d_skip"].reshape(1, -1),
                 ssd_norm_w=sg["ssd_norm_w"].reshape(1, -1), f_bias=sg["f_bias"].reshape(1, -1), w_out=g_w_out[None],
                 norm_mlp_w=sg["norm_mlp_w"].reshape(1, -1), w_up=g_w_up[None], w_down=g_w_down[None],
                 norm_final_w=sg["norm_final_w"])
    weights = dict(norm_mix_w=norm_mix_w, w_in=w_in, conv_w=conv_w, conv_b=conv_b, dt_bias=dt_bias, a_log=a_log,
                   d_skip=d_skip, ssd_norm_w=ssd_norm_w, f_bias=f_bias, w_out=w_out, norm_mlp_w=norm_mlp_w,
                   w_up=w_up, w_down=w_down, norm_final_w=norm_final_w)
    ms = dict(norm_mix_w=m_norm_mix_w, w_in=m_w_in, conv_w=m_conv_w, conv_b=m_conv_b, dt_bias=m_dt_bias,
              a_log=m_a_log, d_skip=m_d_skip, ssd_norm_w=m_ssd_norm_w, f_bias=m_f_bias, w_out=m_w_out,
              norm_mlp_w=m_norm_mlp_w, w_up=m_w_up, w_down=m_w_down, norm_final_w=m_norm_final_w)
    vs = dict(norm_mix_w=v_norm_mix_w, w_in=v_w_in, conv_w=v_conv_w, conv_b=v_conv_b, dt_bias=v_dt_bias,
              a_log=v_a_log, d_skip=v_d_skip, ssd_norm_w=v_ssd_norm_w, f_bias=v_f_bias, w_out=v_w_out,
              norm_mlp_w=v_norm_mlp_w, w_up=v_w_up, w_down=v_w_down, norm_final_w=v_norm_final_w)
    names = list(weights)
    big = ("w_in", "w_out", "w_up", "w_down")
    delta, new_m, new_v = {}, {}, {}
    for k, g2 in zip(big[1:], (g_w_out, g_w_up, g_w_down)):
        delta[k], new_m[k], new_v[k] = _adamw(weights[k], g2, ms[k], vs[k], name="adamw_" + k)
    g_in_t = g_w_in.T
    outs_t = _adamw(w_in[0].T, g_in_t, m_w_in[0].T, v_w_in[0].T, name="adamw_w_in")
    delta["w_in"], new_m["w_in"], new_v["w_in"] = [o.T[None] for o in outs_t]
    grads["w_in"] = g_in_t.T[None]
    smalls = [k for k in names if k not in big]
    sizes = [math.prod(weights[k].shape) for k in smalls]
    rows = -(-sum(sizes) // 1024) * 8
    packs = [_pack_small([d[k] for k in smalls], rows) for d in (weights, grads, ms, vs)]
    outs = _adamw(*packs, name="adamw_small")
    for o, dst in zip(outs, (delta, new_m, new_v)):
        for k, val in zip(smalls, _unpack_small(o, sizes)):
            dst[k] = val.reshape(weights[k].shape)
    return (loss, dx, *[grads[k] for k in names], *[delta[k] for k in names], *[new_m[k] for k in names],
            *[new_v[k] for k in names])
```

```python
import functools
import math

import jax
import jax.numpy as jnp
from jax import lax
from jax.experimental import pallas as pl
from jax.experimental.pallas import tpu as pltpu

F32 = jnp.float32
BF16 = jnp.bfloat16
HIGHEST = lax.Precision.HIGHEST
MESH = pl.DeviceIdType.MESH

D_MODEL = 1024
HEAD_DIM = 64
SSD_WIDTH = 1024
SSD_STATE = 128
CONV_CH = 1536
CHUNK = 128
ATT_WIDTH = 1024
EPS = 1e-5
IN_WIDTH = 5664
PA_WIDTH = 2688
QKV_WIDTH = 3072
D_FF = 4096
ATT_FWD_BLOCK = 512
ATT_BWD_BLOCK = 256
NEG = -1e30
LOG2E = 1.4426950408889634
VMEM_LIMIT = 48 * 1024 * 1024

ADAM_LR = 0.001
ADAM_B1 = 0.9
ADAM_B2 = 0.999
ADAM_EPS = 1e-08
ADAM_WD = 0.01
ADAM_STEP = 10

N_CHIPS = 4
SMALL_ROWS = 96


def _cparams(sem):
    return pltpu.CompilerParams(dimension_semantics=sem, vmem_limit_bytes=VMEM_LIMIT)


def _pick(n, cands):
    for c in cands:
        if n % c == 0:
            return c
    return n


MM_CHUNK = 512


def _mm(a, b, *, name, tiles, ta=False, tb=False, out_dtype=F32, res=None, a_act=None, epi_up=None, after=None,
        into=None):
    n_unread = (after is not None) + (into is not None and into[3] is not None)
    if ta:
        K, M = a.shape
    else:
        M, K = a.shape
    if tb:
        N, K2 = b.shape
    else:
        K2, N = b.shape
    assert K == K2, (a.shape, b.shape)
    tm, tn, tk = tiles
    assert M % tm == 0 and N % tn == 0 and K % tk == 0, (name, M, N, K, tiles)
    nk = K // tk
    dn = (((0 if ta else 1,), (1 if tb else 0,)), ((), ()))
    has_res = res is not None
    has_up = epi_up is not None
    cn = _pick(tn, (MM_CHUNK, 384, 256, 128))

    def prologue(av):
        if a_act == "relu2":
            r = jnp.maximum(av.astype(F32), 0.0)
            av = r * r
        return av.astype(BF16)

    def epilogue(out, res_v, up_v):
        if has_res:
            out = out + res_v.astype(F32)
        if has_up:
            out = out * (2.0 * jnp.maximum(up_v.astype(F32), 0.0))
        return out.astype(out_dtype)

    def body(*refs):
        a_ref, b_ref = refs[0], refs[1]
        i = 2
        res_ref = up_ref = None
        if has_res:
            res_ref = refs[i]
            i += 1
        if has_up:
            up_ref = refs[i]
            i += 1
        i += n_unread
        o_ref = refs[i]
        if nk == 1:
            av = prologue(a_ref[...])
            if len(o_ref.shape) == 3:
                out = lax.dot_general(av, b_ref[...].astype(BF16), dn, preferred_element_type=F32)
                out = epilogue(out, res_ref[...] if has_res else None, up_ref[...] if has_up else None)
                o_ref[...] = out.reshape(o_ref.shape)
                return
            for c in range(tn // cn):
                cs = slice(c * cn, (c + 1) * cn)
                bv = (b_ref[cs, :] if tb else b_ref[:, cs]).astype(BF16)
                out = lax.dot_general(av, bv, dn, preferred_element_type=F32)
                o_ref[:, cs] = epilogue(out, res_ref[:, cs] if has_res else None, up_ref[:, cs] if has_up else None)
            return
        acc_ref = refs[i + 1]
        k = pl.program_id(2)

        @pl.when(k == 0)
        def _():
            acc_ref[...] = jnp.zeros_like(acc_ref)

        acc_ref[...] += lax.dot_general(prologue(a_ref[...]), b_ref[...].astype(BF16), dn,
                                        preferred_element_type=F32)

        @pl.when(k == nk - 1)
        def _():
            out = epilogue(acc_ref[...], res_ref[...] if has_res else None, up_ref[...] if has_up else None)
            o_ref[...] = out.reshape(o_ref.shape)

    a_spec = pl.BlockSpec((tk, tm), lambda i, j, k: (k, i)) if ta else pl.BlockSpec((tm, tk), lambda i, j, k: (i, k))
    b_spec = pl.BlockSpec((tn, tk), lambda i, j, k: (j, k)) if tb else pl.BlockSpec((tk, tn), lambda i, j, k: (k, j))
    o_spec = pl.BlockSpec((tm, tn), lambda i, j, k: (i, j))
    ins, specs = [a, b], [a_spec, b_spec]
    if has_res:
        ins.append(res)
        specs.append(o_spec)
    if has_up:
        ins.append(epi_up)
        specs.append(o_spec)
    if after is not None:
        ins.append(after)
        specs.append(pl.BlockSpec(memory_space=pl.ANY))
    out_shape, out_spec, aliases = jax.ShapeDtypeStruct((M, N), out_dtype), o_spec, {}
    if into is not None:
        shape, block, index, buf = into
        out_shape, out_spec = jax.ShapeDtypeStruct(shape, out_dtype), pl.BlockSpec(block, index)
        if buf is not None:
            aliases = {len(ins): 0}
            ins.append(buf)
            specs.append(pl.BlockSpec(memory_space=pl.ANY))
    return pl.pallas_call(
        body, name=name, grid=(M // tm, N // tn, nk),
        in_specs=specs, out_specs=out_spec, out_shape=out_shape, input_output_aliases=aliases,
        scratch_shapes=[] if nk == 1 else [pltpu.VMEM((tm, tn), F32)],
        compiler_params=_cparams(("parallel", "parallel", "arbitrary")),
    )(*ins)


def _rows_product(a_ref, b_ref, tb, a_act):
    av = a_ref[...]
    if a_act == "relu2":
        r = jnp.maximum(av.astype(F32), 0.0)
        av = r * r
    dn = (((1,), (1 if tb else 0,)), ((), ()))
    return lax.dot_general(av.astype(BF16), b_ref[...].astype(BF16), dn, preferred_element_type=F32)


def _norm_mm(x, w, b, after, *, name, tm):
    m, d = x.shape
    n = b.shape[1]
    cn = _pick(n, (MM_CHUNK, 384, 256, 128))

    def body(x_ref, w_ref, b_ref, after_ref, h_ref, r_ref, o_ref):
        xv = x_ref[...]
        rstd = lax.rsqrt(jnp.mean(xv * xv, axis=1, keepdims=True) + EPS)
        hv = (xv * rstd * w_ref[...]).astype(BF16)
        h_ref[...] = hv
        r_ref[...] = rstd
        for c in range(n // cn):
            cs = slice(c * cn, (c + 1) * cn)
            o_ref[:, cs] = jnp.dot(hv, b_ref[:, cs].astype(BF16), preferred_element_type=F32)

    row = pl.BlockSpec((tm, d), lambda i: (i, 0))
    return pl.pallas_call(
        body, name=name, grid=(m // tm,),
        in_specs=[row, pl.BlockSpec((1, d), lambda i: (0, 0)), pl.BlockSpec((d, n), lambda i: (0, 0)),
                  pl.BlockSpec(memory_space=pl.ANY)],
        out_specs=[row, pl.BlockSpec((tm, 1), lambda i: (i, 0)), pl.BlockSpec((tm, n), lambda i: (i, 0))],
        out_shape=[jax.ShapeDtypeStruct((m, d), BF16), jax.ShapeDtypeStruct((m, 1), F32),
                   jax.ShapeDtypeStruct((m, n), F32)],
        compiler_params=_cparams(("parallel",)),
    )(x, w, b, after)


def _mm_norm_fwd(a1, b1, a2, b2, res, w, *, name, tm):
    m, k1 = a1.shape
    k2 = a2.shape[1]
    d = b1.shape[1]

    def body(a1_ref, b1_ref, a2_ref, b2_ref, res_ref, w_ref, h_ref, y_ref, r_ref):
        hv = _rows_product(a1_ref, b1_ref, False, None) + _rows_product(a2_ref, b2_ref, False, None) + res_ref[...]
        rstd = lax.rsqrt(jnp.mean(hv * hv, axis=1, keepdims=True) + EPS)
        h_ref[...] = hv
        y_ref[...] = (hv * rstd * w_ref[...]).astype(BF16)
        r_ref[...] = rstd

    row = pl.BlockSpec((tm, d), lambda i: (i, 0))
    return pl.pallas_call(
        body, name=name, grid=(m // tm,),
        in_specs=[pl.BlockSpec((tm, k1), lambda i: (i, 0)), pl.BlockSpec((k1, d), lambda i: (0, 0)),
                  pl.BlockSpec((tm, k2), lambda i: (i, 0)), pl.BlockSpec((k2, d), lambda i: (0, 0)), row,
                  pl.BlockSpec((1, d), lambda i: (0, 0))],
        out_specs=[row, row, pl.BlockSpec((tm, 1), lambda i: (i, 0))],
        out_shape=[jax.ShapeDtypeStruct((m, d), F32), jax.ShapeDtypeStruct((m, d), BF16),
                   jax.ShapeDtypeStruct((m, 1), F32)],
        compiler_params=_cparams(("parallel",)),
    )(a1, b1, a2, b2, res, w)


def _mm_final(a, b, res, w, target, *, name, tm, a_act):
    m, k = a.shape
    d = b.shape[1]

    def body(a_ref, b_ref, res_ref, w_ref, t_ref, dh_ref, dhb_ref, loss_ref, dw_ref):
        @pl.when(pl.program_id(0) == 0)
        def _():
            loss_ref[...] = jnp.zeros_like(loss_ref)
            dw_ref[...] = jnp.zeros_like(dw_ref)

        hv = _rows_product(a_ref, b_ref, False, a_act) + res_ref[...]
        wv = w_ref[...]
        rstd = lax.rsqrt(jnp.mean(hv * hv, axis=1, keepdims=True) + EPS)
        xhat = hv * rstd
        err = xhat * wv - t_ref[...]
        loss_ref[...] += 0.5 * jnp.sum(jnp.mean(err * err, axis=1, keepdims=True), axis=0, keepdims=True)
        dy = err * (1.0 / d)
        gw = dy * wv
        dh = rstd * (gw - xhat * jnp.mean(gw * xhat, axis=1, keepdims=True))
        dh_ref[...] = dh
        dhb_ref[...] = dh.astype(BF16)
        dw_ref[...] += jnp.sum(dy * xhat, axis=0, keepdims=True)

    row = pl.BlockSpec((tm, d), lambda i: (i, 0))
    vec = pl.BlockSpec((1, d), lambda i: (0, 0))
    return pl.pallas_call(
        body, name=name, grid=(m // tm,),
        in_specs=[pl.BlockSpec((tm, k), lambda i: (i, 0)), pl.BlockSpec((k, d), lambda i: (0, 0)), row, vec, row],
        out_specs=[row, row, pl.BlockSpec((1, 128), lambda i: (0, 0)), vec],
        out_shape=[jax.ShapeDtypeStruct((m, d), F32), jax.ShapeDtypeStruct((m, d), BF16),
                   jax.ShapeDtypeStruct((1, 128), F32), jax.ShapeDtypeStruct((1, d), F32)],
        compiler_params=_cparams(("arbitrary",)),
    )(a, b, res, w, target)


def _mm_two_halves(a, b, *, name, tm):
    m, k = a.shape
    d = b.shape[0] // 2

    def body(a_ref, b_ref, lo_ref, hi_ref):
        av = a_ref[...].astype(BF16)
        lo_ref[...] = lax.dot_general(av, b_ref[0:d, :].astype(BF16), NT_DIMS, preferred_element_type=F32)
        hi_ref[...] = lax.dot_general(av, b_ref[d:2 * d, :].astype(BF16), NT_DIMS,
                                      preferred_element_type=F32).astype(BF16)

    row = pl.BlockSpec((tm, d), lambda i: (i, 0))
    return pl.pallas_call(
        body, name=name, grid=(m // tm,),
        in_specs=[pl.BlockSpec((tm, k), lambda i: (i, 0)), pl.BlockSpec((2 * d, k), lambda i: (0, 0))],
        out_specs=[row, row],
        out_shape=[jax.ShapeDtypeStruct((m, d), F32), jax.ShapeDtypeStruct((m, d), BF16)],
        compiler_params=_cparams(("parallel",)),
    )(a, b)


def _mm_norm_bwd(pairs, x, rstd, w, dres, *, name, tm, after=None):
    m = pairs[0][0].shape[0]
    d = pairs[0][1].shape[0]
    n_pairs = len(pairs)

    def body(*refs):
        i = 2 * n_pairs
        x_ref, r_ref, w_ref, d_ref = refs[i:i + 4]
        dx_ref, dxb_ref, dw_ref = refs[-3:]

        @pl.when(pl.program_id(0) == 0)
        def _():
            dw_ref[...] = jnp.zeros_like(dw_ref)

        g = _rows_product(refs[0], refs[1], True, None)
        for p in range(1, n_pairs):
            g = g + _rows_product(refs[2 * p], refs[2 * p + 1], True, None)
        r = r_ref[...]
        xhat = x_ref[...] * r
        gw = g * w_ref[...]
        dx = d_ref[...] + r * (gw - xhat * jnp.mean(gw * xhat, axis=1, keepdims=True))
        dx_ref[...] = dx
        dxb_ref[...] = dx.astype(BF16)
        dw_ref[...] += jnp.sum(g * xhat, axis=0, keepdims=True)

    row = pl.BlockSpec((tm, d), lambda i: (i, 0))
    vec = pl.BlockSpec((1, d), lambda i: (0, 0))
    ins, specs = [], []
    for a, b, *col_block in pairs:
        k = a.shape[1]
        ins += [a, b]
        specs += [pl.BlockSpec((tm, k), lambda i: (i, 0)),
                  pl.BlockSpec((d, k), functools.partial(lambda i, cb: (0, cb), cb=col_block[0] if col_block else 0),
                               pipeline_mode=pl.Buffered(1))]
    ins += [x, rstd, w, dres]
    specs += [row, pl.BlockSpec((tm, 1), lambda i: (i, 0)), vec, row]
    if after is not None:
        ins.append(after)
        specs.append(pl.BlockSpec(memory_space=pl.ANY))
    return pl.pallas_call(
        body, name=name, grid=(m // tm,), in_specs=specs, out_specs=[row, row, vec],
        out_shape=[jax.ShapeDtypeStruct((m, d), F32), jax.ShapeDtypeStruct((m, d), BF16),
                   jax.ShapeDtypeStruct((1, d), F32)],
        compiler_params=_cparams(("arbitrary",)),
    )(*ins)


def _softplus(x):
    return jnp.maximum(x, 0.0) + jnp.log(1.0 + jnp.exp(-jnp.abs(x)))


def _prep(proj_a, bias128, alog128, bl, t):
    n = bl * t
    nch = t // CHUNK
    col0 = (SSD_WIDTH + CONV_CH) // 128

    def body(p_ref, b_ref, al_ref, dt_ref, gd_ref, ac_ref, act_ref, negc_ref):
        negc_ref[...] = jnp.zeros_like(negc_ref)
        row = lax.broadcasted_iota(jnp.int32, (CHUNK, CHUNK), 0)
        col = lax.broadcasted_iota(jnp.int32, (CHUNK, CHUNK), 1)
        tril = (row >= col).astype(F32)
        lane = lax.broadcasted_iota(jnp.int32, (1, 128), 1)
        head_lanes = lane < 16
        a_row = -jnp.exp(al_ref[...])
        carry = jnp.zeros((1, 128), F32)
        for ci in range(nch):
            rows = slice(ci * CHUNK, (ci + 1) * CHUNK)
            xv = p_ref[rows, :] + b_ref[...]
            sp = _softplus(xv)
            acum = jnp.dot(tril, a_row * sp, precision=HIGHEST, preferred_element_type=F32)
            c = jnp.dot(tril, -_softplus(-xv), precision=HIGHEST, preferred_element_type=F32) + carry
            carry = c[CHUNK - 1:CHUNK, :]
            dt_ref[rows, :] = jnp.where(head_lanes, sp, 0.0)
            gd_ref[rows, :] = jnp.where(head_lanes, jax.nn.sigmoid(xv),
                                        jnp.where(lane < 32, jax.nn.sigmoid(-xv), 0.0))
            ac_ref[rows, :] = jnp.where(head_lanes, acum, 0.0)
            act_ref[:, rows] = jnp.transpose(acum)[0:16, :]
            c_t = jnp.transpose(c)
            for hp in range(8):
                negc_ref[hp, 0:2, rows] = -c_t[16 + 2 * hp:18 + 2 * hp, :]

    o128 = pl.BlockSpec((t, 128), lambda b: (b, 0))
    v128 = pl.BlockSpec((1, 128), lambda b: (0, 0))
    w128 = jax.ShapeDtypeStruct((n, 128), F32)
    return pl.pallas_call(
        body, name="head_scalars", grid=(bl,),
        in_specs=[pl.BlockSpec((t, 128), lambda b: (b, col0)), v128, v128],
        out_specs=[o128, o128, o128, pl.BlockSpec((16, t), lambda b: (0, b)),
                   pl.BlockSpec((None, 8, 8, t), lambda b: (b, 0, 0, 0))],
        out_shape=[w128, w128, w128, jax.ShapeDtypeStruct((16, n), F32),
                   jax.ShapeDtypeStruct((bl, 8, 8, t), F32)],
        compiler_params=_cparams(("parallel",)),
    )(proj_a, bias128, alog128)


def _fpost(dc, gate_d, ddt, dpa, bl, t):
    n = bl * t
    nch = t // CHUNK
    col0 = (SSD_WIDTH + CONV_CH) // 128

    def body(dc_ref, gd_ref, ddt_ref, dpa_in, out_ref, db_ref):
        @pl.when(pl.program_id(0) == 0)
        def _():
            db_ref[...] = jnp.zeros_like(db_ref)

        row = lax.broadcasted_iota(jnp.int32, (CHUNK, CHUNK), 0)
        col = lax.broadcasted_iota(jnp.int32, (CHUNK, CHUNK), 1)
        triu = (row <= col).astype(F32)
        lane = lax.broadcasted_iota(jnp.int32, (1, 128), 1)
        gate_lanes = (lane >= 16) & (lane < 32)
        carry = jnp.zeros((1, 128), F32)
        db = jnp.zeros((1, 128), F32)
        for ci in reversed(range(nch)):
            rows = slice(ci * CHUNK, (ci + 1) * CHUNK)
            dlf = jnp.dot(triu, dc_ref[rows, :], precision=HIGHEST, preferred_element_type=F32) + carry
            carry = dlf[0:1, :]
            df = jnp.where(gate_lanes, dlf * gd_ref[rows, :], 0.0)
            out_ref[rows, :] = (ddt_ref[rows, :] + df).astype(BF16)
            db = db + jnp.sum(df, axis=0, keepdims=True)
        db_ref[...] += db[:, 16:32]

    blk = pl.BlockSpec((t, 128), lambda b: (b, 0))
    return pl.pallas_call(
        body, name="forget_gate_bwd", grid=(bl,),
        in_specs=[blk, blk, blk, ANY],
        out_specs=[pl.BlockSpec((t, 128), lambda b: (b, col0)), pl.BlockSpec((1, 16), lambda b: (0, 0))],
        out_shape=[jax.ShapeDtypeStruct(dpa.shape, dpa.dtype), jax.ShapeDtypeStruct((1, 16), F32)],
        input_output_aliases={3: 0},
        compiler_params=_cparams(("arbitrary",)),
    )(dc, gate_d, ddt, dpa)


CONV_TILE = 256
CONV_ROWS = 256


def _conv_taps(u_ref, i):
    r0 = pl.multiple_of(i * CONV_ROWS, CONV_ROWS)
    cur = u_ref[pl.ds(r0, CONV_ROWS), :]
    p0 = pl.multiple_of(jnp.maximum(r0 - 8, 0), 8)
    prev = jnp.where(i > 0, u_ref[pl.ds(p0, 8), :], 0.0)
    cat = jnp.concatenate([prev, cur], axis=0)
    return r0, [cur] + [pltpu.roll(cat, s, 0)[8:, :] for s in (1, 2, 3)]


def _conv_fwd(proj_a, conv_w, conv_b, bl, t):
    n = bl * t
    nct = CONV_CH // CONV_TILE
    c0 = SSD_WIDTH // CONV_TILE

    def body(u_ref, w_ref, b_ref, o_ref, d_ref):
        w = w_ref[...]
        bias = b_ref[...]

        def chunk(i, carry):
            r0, taps = _conv_taps(u_ref, i)
            pre = bias + w[3:4, :] * taps[0]
            for s in (1, 2, 3):
                pre = pre + w[3 - s:4 - s, :] * taps[s]
            sg = jax.nn.sigmoid(pre)
            o_ref[pl.ds(r0, CONV_ROWS), :] = pre * sg
            d_ref[pl.ds(r0, CONV_ROWS), :] = (sg * (1.0 + pre * (1.0 - sg))).astype(BF16)
            return carry

        lax.fori_loop(0, t // CONV_ROWS, chunk, 0)

    out = pl.BlockSpec((t, CONV_TILE), lambda b, c: (b, c))
    return pl.pallas_call(
        body, name="conv_silu_fwd", grid=(bl, nct),
        in_specs=[pl.BlockSpec((t, CONV_TILE), lambda b, c: (b, c0 + c)),
                  pl.BlockSpec((4, CONV_TILE), lambda b, c: (0, c)),
                  pl.BlockSpec((1, CONV_TILE), lambda b, c: (0, c))],
        out_specs=[out, out],
        out_shape=[jax.ShapeDtypeStruct((n, CONV_CH), F32), jax.ShapeDtypeStruct((n, CONV_CH), BF16)],
        compiler_params=_cparams(("parallel", "parallel")),
    )(proj_a, conv_w, conv_b)


def _conv_bwd(dxc, dsilu, proj_a, conv_w, dpa, bl, t):
    nct = CONV_CH // CONV_TILE
    c0 = SSD_WIDTH // CONV_TILE
    nrc = t // CONV_ROWS

    def body(g_ref, s_ref, u_ref, w_ref, dpa_in, du_ref, dw_ref, db_ref, dp_scr):
        @pl.when(pl.program_id(1) == 0)
        def _():
            dw_ref[...] = jnp.zeros_like(dw_ref)
            db_ref[...] = jnp.zeros_like(db_ref)

        w = w_ref[...]
        dp_scr[pl.ds(t, 8), :] = jnp.zeros((8, CONV_TILE), F32)

        def chunk1(i, carry):
            dw0, dw1, dw2, dw3, db = carry
            r0, taps = _conv_taps(u_ref, i)
            dpre = g_ref[pl.ds(r0, CONV_ROWS), :] * s_ref[pl.ds(r0, CONV_ROWS), :].astype(F32)
            dp_scr[pl.ds(r0, CONV_ROWS), :] = dpre
            dw3 = dw3 + jnp.sum(dpre * taps[0], axis=0, keepdims=True)
            dw2 = dw2 + jnp.sum(dpre * taps[1], axis=0, keepdims=True)
            dw1 = dw1 + jnp.sum(dpre * taps[2], axis=0, keepdims=True)
            dw0 = dw0 + jnp.sum(dpre * taps[3], axis=0, keepdims=True)
            db = db + jnp.sum(dpre, axis=0, keepdims=True)
            return dw0, dw1, dw2, dw3, db

        z = jnp.zeros((1, CONV_TILE), F32)
        dw0, dw1, dw2, dw3, db = lax.fori_loop(0, nrc, chunk1, (z, z, z, z, z))
        dw_ref[...] += jnp.concatenate([dw0, dw1, dw2, dw3], axis=0)
        db_ref[...] += db

        def chunk2(i, carry):
            r0 = pl.multiple_of(i * CONV_ROWS, CONV_ROWS)
            cat = dp_scr[pl.ds(r0, CONV_ROWS + 8), :]
            du = w[3:4, :] * cat[:CONV_ROWS, :]
            for s in (1, 2, 3):
                du = du + w[3 - s:4 - s, :] * pltpu.roll(cat, CONV_ROWS + 8 - s, 0)[:CONV_ROWS, :]
            du_ref[pl.ds(r0, CONV_ROWS), :] = du.astype(BF16)
            return carry

        lax.fori_loop(0, nrc, chunk2, 0)

    tile = pl.BlockSpec((t, CONV_TILE), lambda c, b: (b, c))
    return pl.pallas_call(
        body, name="conv_silu_bwd", grid=(nct, bl),
        in_specs=[tile, tile, pl.BlockSpec((t, CONV_TILE), lambda c, b: (b, c0 + c)),
                  pl.BlockSpec((4, CONV_TILE), lambda c, b: (0, c)), ANY],
        out_specs=[pl.BlockSpec((t, CONV_TILE), lambda c, b: (b, c0 + c)),
                   pl.BlockSpec((4, CONV_TILE), lambda c, b: (0, c)),
                   pl.BlockSpec((1, CONV_TILE), lambda c, b: (0, c))],
        out_shape=[jax.ShapeDtypeStruct(dpa.shape, dpa.dtype), jax.ShapeDtypeStruct((4, CONV_CH), F32),
                   jax.ShapeDtypeStruct((1, CONV_CH), F32)],
        input_output_aliases={4: 0},
        scratch_shapes=[pltpu.VMEM((t + 8, CONV_TILE), F32)],
        compiler_params=_cparams(("parallel", "arbitrary")),
    )(dxc, dsilu, proj_a, conv_w, dpa)


SSD_FWD_CHUNKS = 4
SSD_BWD_CHUNKS = 1
NT_DIMS = (((1,), (1,)), ((), ()))
TN_DIMS = (((0,), (0,)), ((), ()))


def _dot(a, b, dims=None):
    if dims is None:
        return jnp.dot(a, b, preferred_element_type=F32)
    return lax.dot_general(a, b, dims, preferred_element_type=F32)


def _head_expander():
    r = lax.broadcasted_iota(jnp.int32, (128, SSD_WIDTH), 0)
    c = lax.broadcasted_iota(jnp.int32, (128, SSD_WIDTH), 1)
    return ((c // HEAD_DIM == r % 16) & (r < 48)).astype(BF16)


def _spread(v128, expander):
    hi = v128.astype(BF16).astype(F32)
    r1 = v128 - hi
    mid = r1.astype(BF16).astype(F32)
    lo = (r1 - mid).astype(BF16).astype(F32)
    packed = (hi + pltpu.roll(mid, 16, 1) + pltpu.roll(lo, 32, 1)).astype(BF16)
    return jnp.dot(packed, expander, preferred_element_type=F32)


def _head_sums(v1024, expander):
    hi = v1024.astype(BF16)
    lo = (v1024 - hi.astype(F32)).astype(BF16)
    heads = jnp.where(lax.broadcasted_iota(jnp.int32, expander.shape, 0) < 16, expander, jnp.zeros_like(expander))
    return _dot(hi, heads, NT_DIMS) + _dot(lo, heads, NT_DIMS)


def _ssd_fwd(xc, proj_a, dt, acum, acum_t, dskip_e, norm_w, bl, t):
    n = bl * t
    nch = t // CHUNK
    L = CHUNK

    def body(xc_blk, z_blk, dt_blk, ac_blk, act_blk, dsk_ref, nw_ref, ys_blk, yp_blk, hp_blk, h_scr, y_scr, x_scr):
        @pl.when(pl.program_id(1) == 0)
        def _():
            h_scr[...] = jnp.zeros_like(h_scr)

        for sub in range(SSD_FWD_CHUNKS):
            rows = pl.ds(sub * L, L)
            chunk(xc_blk.at[rows, :], z_blk.at[rows, :], dt_blk.at[rows, :], ac_blk.at[rows, :], act_blk.at[:, rows],
                  dsk_ref, nw_ref, ys_blk.at[rows, :], yp_blk.at[rows, :], hp_blk.at[sub], h_scr, y_scr, x_scr)

    def chunk(xc_ref, z_ref, dt_ref, ac_ref, act_ref, dsk_ref, nw_ref, ys_ref, yp_ref, hp_ref, h_scr, y_scr, x_scr):
        row = lax.broadcasted_iota(jnp.int32, (L, L), 0)
        col = lax.broadcasted_iota(jnp.int32, (L, L), 1)
        causal = row >= col
        lane128 = lax.broadcasted_iota(jnp.int32, (1, L), 1)
        expander = _head_expander()
        ac_all = ac_ref[...]
        act_all = act_ref[...]
        ac_e = _spread(ac_all, expander)
        e_in = jnp.exp(ac_e)
        dec = jnp.exp(ac_e[L - 1:L, :] - ac_e)
        xs_all = xc_ref[:, 0:SSD_WIDTH]
        x_all = xs_all * _spread(dt_ref[...], expander)
        x_scr[...] = x_all.astype(BF16)
        hp_all = h_scr[...]
        hp_ref[...] = hp_all
        for g in range(2):
            gs = slice(g * 512, (g + 1) * 512)
            bg = xc_ref[:, SSD_WIDTH + g * 128:SSD_WIDTH + (g + 1) * 128].astype(BF16)
            cg = xc_ref[:, SSD_WIDTH + 256 + g * 128:SSD_WIDTH + 256 + (g + 1) * 128].astype(BF16)
            gmat = _dot(cg, bg, NT_DIMS)
            y_off = _dot(cg, hp_all[gs, :].astype(BF16), NT_DIMS) * e_in[:, gs] + dsk_ref[:, gs] * xs_all[:, gs]
            s_new = _dot((x_all[:, gs] * dec[:, gs]).astype(BF16), bg, TN_DIMS)
            for pr in range(4):
                pair = slice((g * 4 + pr) * 128, (g * 4 + pr + 1) * 128)
                x_pair = x_scr[:, pair]
                y_pair = y_off[:, pr * 128:(pr + 1) * 128]
                for j in range(2):
                    h = g * 8 + 2 * pr + j
                    sl = slice(h * HEAD_DIM, (h + 1) * HEAD_DIM)
                    r = 2 * pr + j
                    ldec = jnp.exp(jnp.where(causal, ac_all[:, h:h + 1] - act_all[h:h + 1, :], NEG))
                    x_head = jnp.where((lane128 < HEAD_DIM) == (j == 0), x_pair, jnp.zeros_like(x_pair))
                    y_pair = y_pair + _dot((gmat * ldec).astype(BF16), x_head)
                    elast = jnp.exp(ac_all[L - 1:L, h:h + 1])
                    h_scr[sl, :] = elast * hp_all[sl, :] + s_new[r * HEAD_DIM:(r + 1) * HEAD_DIM, :]
                y_scr[:, pair] = y_pair
        y = y_scr[...]
        yp_ref[...] = y
        zv = z_ref[...]
        yg = y * (zv * jax.nn.sigmoid(zv))
        for g in range(2):
            gs = slice(g * 512, (g + 1) * 512)
            grp = yg[:, gs]
            rstd = lax.rsqrt(jnp.mean(grp * grp, axis=1, keepdims=True) + EPS)
            ys_ref[:, gs] = (grp * rstd * nw_ref[:, gs]).astype(BF16)

    cps = SSD_FWD_CHUNKS
    steps = nch // cps
    rb = lambda b, c: (b * steps + c, 0)
    v1k = pl.BlockSpec((1, SSD_WIDTH), lambda b, c: (0, 0))
    return pl.pallas_call(
        body, name="ssd_fwd", grid=(bl, steps),
        in_specs=[pl.BlockSpec((cps * L, CONV_CH), rb), pl.BlockSpec((cps * L, SSD_WIDTH), rb),
                  pl.BlockSpec((cps * L, 128), rb), pl.BlockSpec((cps * L, 128), rb),
                  pl.BlockSpec((16, cps * L), lambda b, c: (0, b * steps + c)), v1k, v1k],
        out_specs=[pl.BlockSpec((cps * L, SSD_WIDTH), rb), pl.BlockSpec((cps * L, SSD_WIDTH), rb),
                   pl.BlockSpec((cps, SSD_WIDTH, SSD_STATE), lambda b, c: (b * steps + c, 0, 0))],
        out_shape=[jax.ShapeDtypeStruct((n, SSD_WIDTH), BF16), jax.ShapeDtypeStruct((n, SSD_WIDTH), F32),
                   jax.ShapeDtypeStruct((bl * nch, SSD_WIDTH, SSD_STATE), F32)],
        scratch_shapes=[pltpu.VMEM((SSD_WIDTH, SSD_STATE), F32), pltpu.VMEM((L, SSD_WIDTH), F32),
                        pltpu.VMEM((L, SSD_WIDTH), BF16)],
        compiler_params=_cparams(("parallel", "arbitrary")),
    )(xc, proj_a, dt, acum, acum_t, dskip_e, norm_w)


def _ssd_bwd(dys, xc, proj_a, ypre, hprev, dt, gate_d, acum, acum_t, alog128, dskip_e, norm_w, bl, t):
    n = bl * t
    nch = t // CHUNK
    L = CHUNK

    def body(dys_blk, xc_blk, z_blk, yp_blk, hp_blk, dt_blk, gd_blk, ac_blk, act_blk, al_ref, dsk_ref, nw_ref,
             dxc_blk, dz_blk, ddt_blk, dnw_ref, dsk16_ref, da16_ref, db16_ref,
             dh_scr, dy_scr, x_scr, dx_scr, red_scr):
        first = (pl.program_id(0) == 0) & (pl.program_id(1) == 0)

        @pl.when(first)
        def _():
            dnw_ref[...] = jnp.zeros_like(dnw_ref)
            dsk16_ref[...] = jnp.zeros_like(dsk16_ref)
            da16_ref[...] = jnp.zeros_like(da16_ref)
            db16_ref[...] = jnp.zeros_like(db16_ref)

        @pl.when(pl.program_id(1) == 0)
        def _():
            dh_scr[...] = jnp.zeros_like(dh_scr)

        for sub in reversed(range(SSD_BWD_CHUNKS)):
            rows = pl.ds(sub * L, L)
            chunk(dys_blk.at[rows, :], xc_blk.at[rows, :], z_blk.at[rows, :], yp_blk.at[rows, :], hp_blk.at[sub],
                  dt_blk.at[rows, :], gd_blk.at[rows, :], ac_blk.at[rows, :], act_blk.at[:, rows], al_ref, dsk_ref,
                  nw_ref, dxc_blk.at[rows, :], dz_blk.at[rows, :], ddt_blk.at[rows, :], dnw_ref, dsk16_ref, da16_ref,
                  db16_ref, dh_scr, dy_scr, x_scr, dx_scr, red_scr)

    def chunk(dys_ref, xc_ref, z_ref, yp_ref, hp_ref, dt_ref, gd_ref, ac_ref, act_ref, al_ref, dsk_ref, nw_ref,
              dxc_ref, dz_ref, ddt_ref, dnw_ref, dsk16_ref, da16_ref, db16_ref,
              dh_scr, dy_scr, x_scr, dx_scr, red_scr):
        y = yp_ref[...]
        zv = z_ref[...]
        sz = jax.nn.sigmoid(zv)
        gate = zv * sz
        yg = y * gate
        dout = dys_ref[...]
        nw = nw_ref[...]
        for g in range(2):
            gs = slice(g * 512, (g + 1) * 512)
            grp = yg[:, gs]
            rstd = lax.rsqrt(jnp.mean(grp * grp, axis=1, keepdims=True) + EPS)
            ghat = grp * rstd
            dnw_ref[:, gs] += jnp.sum(dout[:, gs] * ghat, axis=0, keepdims=True)
            gw = dout[:, gs] * nw[:, gs]
            dyg = rstd * (gw - ghat * jnp.mean(gw * ghat, axis=1, keepdims=True))
            dy_scr[:, gs] = dyg * gate[:, gs]
            dz_ref[:, gs] = (dyg * y[:, gs] * (sz[:, gs] * (1.0 + zv[:, gs] * (1.0 - sz[:, gs])))).astype(BF16)

        row = lax.broadcasted_iota(jnp.int32, (L, L), 0)
        col = lax.broadcasted_iota(jnp.int32, (L, L), 1)
        causal = row >= col
        lane128 = lax.broadcasted_iota(jnp.int32, (1, L), 1)
        rows128 = lax.broadcasted_iota(jnp.int32, (L, 1), 0)
        last_row = rows128 == (L - 1)
        expander = _head_expander()
        ac_all = ac_ref[...]
        act_all = act_ref[...]
        dt_all = dt_ref[...]
        dt_e = _spread(dt_all, expander)
        ac_e = _spread(ac_all, expander)
        e_in = jnp.exp(ac_e)
        dec = jnp.exp(ac_e[L - 1:L, :] - ac_e)
        xs_all = xc_ref[:, 0:SSD_WIDTH]
        x_all = xs_all * dt_e
        x_scr[...] = x_all.astype(BF16)
        dy_all = dy_scr[...]
        hp_all = hp_ref[...]
        ds_all = dh_scr[...]
        dsk_cols = jnp.sum(dy_all * xs_all, axis=0, keepdims=True)
        dac = jnp.zeros((L, L), F32)
        dac_row = jnp.zeros((L, L), F32)
        ddec_cols = []
        for g in range(2):
            gs = slice(g * 512, (g + 1) * 512)
            bsl = slice(SSD_WIDTH + g * 128, SSD_WIDTH + (g + 1) * 128)
            csl = slice(SSD_WIDTH + 256 + g * 128, SSD_WIDTH + 256 + (g + 1) * 128)
            bg = xc_ref[:, bsl].astype(BF16)
            cg = xc_ref[:, csl].astype(BF16)
            gmat = _dot(cg, bg, NT_DIMS)
            hpb = hp_all[gs, :].astype(BF16)
            dsb = ds_all[gs, :].astype(BF16)
            ch = _dot(cg, hpb, NT_DIMS)
            dye = dy_all[:, gs] * e_in[:, gs]
            dyeb = dye.astype(BF16)
            dc_acc = _dot(dyeb, hpb)
            dhp = _dot(dyeb, cg, TN_DIMS)
            dxd = _dot(bg, dsb, NT_DIMS)
            db_acc = _dot((x_all[:, gs] * dec[:, gs]).astype(BF16), dsb)
            ddec = dxd * x_all[:, gs] * dec[:, gs]
            ddec_cols.append(jnp.sum(ddec, axis=0, keepdims=True))
            dx_inter = dxd * dec[:, gs]
            red_scr[:, gs] = dye * ch - ddec
            dg_sum = jnp.zeros((L, L), F32)
            for pr in range(4):
                pair = slice((g * 4 + pr) * 128, (g * 4 + pr + 1) * 128)
                x_pair = x_scr[:, pair]
                dy_pair = dy_scr[:, pair].astype(BF16)
                dx_pair = dx_inter[:, pr * 128:(pr + 1) * 128]
                for j in range(2):
                    h = g * 8 + 2 * pr + j
                    r = 2 * pr + j
                    sl = slice(h * HEAD_DIM, (h + 1) * HEAD_DIM)
                    onehot_w = lane128 == h
                    ldec = jnp.exp(jnp.where(causal, ac_all[:, h:h + 1] - act_all[h:h + 1, :], NEG))
                    mf = gmat * ldec
                    dyb = jnp.where((lane128 < HEAD_DIM) == (j == 0), dy_pair, jnp.zeros_like(dy_pair))
                    dm = _dot(dyb, x_pair, NT_DIMS)
                    dx_pair = dx_pair + _dot(mf.astype(BF16), dyb, TN_DIMS)
                    dg_sum = dg_sum + dm * ldec
                    wmat = dm * mf
                    elast = jnp.exp(ac_all[L - 1:L, h:h + 1])
                    hp_h = hp_all[sl, :]
                    ds_h = ds_all[sl, :]
                    extra = elast * jnp.sum(jnp.sum(hp_h * ds_h, axis=1, keepdims=True), axis=0, keepdims=True)
                    dac = dac + jnp.where(onehot_w,
                                          jnp.sum(wmat, axis=1, keepdims=True) + jnp.where(last_row, extra, 0.0), 0.0)
                    dac_row = dac_row + jnp.where(rows128 == h, -jnp.sum(wmat, axis=0, keepdims=True), 0.0)
                    dh_scr[sl, :] = elast * ds_h + dhp[r * HEAD_DIM:(r + 1) * HEAD_DIM, :]
                dx_scr[:, pair] = dx_pair
            dgb = dg_sum.astype(BF16)
            dxc_ref[:, csl] = dc_acc + _dot(dgb, bg)
            dxc_ref[:, bsl] = db_acc + _dot(dgb, cg, TN_DIMS)
        dx_all = dx_scr[...]
        dxc_ref[:, 0:SSD_WIDTH] = dx_all * dt_e + dsk_ref[...] * dy_all
        red = red_scr[...]
        dac_slab = _head_sums(red, expander)
        ddec_tot = _head_sums(jnp.broadcast_to(jnp.concatenate(ddec_cols, axis=1), (8, SSD_WIDTH)), expander)
        ddt_x = _head_sums(dx_all * xs_all, expander)
        dsk16_ref[...] += _head_sums(jnp.broadcast_to(dsk_cols, (8, SSD_WIDTH)), expander)[0:1, 0:16]
        dac = dac + dac_slab + jnp.transpose(dac_row) + jnp.where(last_row, ddec_tot[0:1, :], 0.0)
        triu = (row <= col).astype(F32)
        da = jnp.dot(triu, dac, precision=HIGHEST, preferred_element_type=F32)
        a_row = -jnp.exp(al_ref[...])
        ddt = jnp.where(lane128 < 16, (ddt_x + da * a_row) * gd_ref[...], 0.0)
        ddt_ref[...] = ddt
        da16_ref[...] += (jnp.sum(da * dt_all, axis=0, keepdims=True) * a_row)[:, 0:16]
        db16_ref[...] += jnp.sum(ddt, axis=0, keepdims=True)[:, 0:16]

    cps = SSD_BWD_CHUNKS
    steps = nch // cps
    rb = lambda b, c: (b * steps + steps - 1 - c, 0)
    v1k = pl.BlockSpec((1, SSD_WIDTH), lambda b, c: (0, 0))
    v16 = pl.BlockSpec((1, 16), lambda b, c: (0, 0))
    v128 = pl.BlockSpec((1, 128), lambda b, c: (0, 0))
    wide = pl.BlockSpec((cps * L, SSD_WIDTH), rb)
    s128 = pl.BlockSpec((cps * L, 128), rb)
    return pl.pallas_call(
        body, name="ssd_bwd", grid=(bl, steps),
        in_specs=[wide, pl.BlockSpec((cps * L, CONV_CH), rb), wide, wide,
                  pl.BlockSpec((cps, SSD_WIDTH, SSD_STATE), lambda b, c: (b * steps + steps - 1 - c, 0, 0)),
                  s128, s128, s128, pl.BlockSpec((16, cps * L), lambda b, c: (0, b * steps + steps - 1 - c)),
                  v128, v1k, v1k],
        out_specs=[pl.BlockSpec((cps * L, CONV_CH), rb), wide, s128, v1k, v16, v16, v16],
        out_shape=[jax.ShapeDtypeStruct((n, CONV_CH), F32), jax.ShapeDtypeStruct((n, PA_WIDTH), BF16),
                   jax.ShapeDtypeStruct((n, 128), F32), jax.ShapeDtypeStruct((1, SSD_WIDTH), F32),
                   jax.ShapeDtypeStruct((1, 16), F32), jax.ShapeDtypeStruct((1, 16), F32),
                   jax.ShapeDtypeStruct((1, 16), F32)],
        scratch_shapes=[pltpu.VMEM((SSD_WIDTH, SSD_STATE), F32), pltpu.VMEM((L, SSD_WIDTH), F32),
                        pltpu.VMEM((L, SSD_WIDTH), BF16), pltpu.VMEM((L, SSD_WIDTH), F32),
                        pltpu.VMEM((L, SSD_WIDTH), F32)],
        compiler_params=_cparams(("arbitrary", "arbitrary")),
    )(dys, xc, proj_a, ypre, hprev, dt, gate_d, acum, acum_t, alog128, dskip_e, norm_w)


def _attn_fwd(qkv, negc, bl, t):
    n = bl * t
    tb_ = min(t, ATT_FWD_BLOCK)
    nb = t // tb_
    scale2 = LOG2E / math.sqrt(HEAD_DIM)

    def body(q_ref, k_ref, v_ref, c_ref, o_ref, lse_ref, v0_scr, v1_scr, k0_scr, k1_scr):
        row = lax.broadcasted_iota(jnp.int32, (tb_, tb_), 0)
        col = lax.broadcasted_iota(jnp.int32, (tb_, tb_), 1)
        causal = row >= col
        lane = lax.broadcasted_iota(jnp.int32, (1, 128), 1)
        v_pair = v_ref[...].astype(F32)
        k_pair = k_ref[...]
        v_scrs = (v0_scr, v1_scr)
        k_scrs = (k0_scr, k1_scr)
        for j in range(2):
            v_head = v_pair if j == 0 else pltpu.roll(v_pair, HEAD_DIM, 1)
            v_scrs[j][...] = jnp.where(lane < HEAD_DIM, v_head, jnp.where(lane == HEAD_DIM, 1.0, 0.0)).astype(BF16)
            k_scrs[j][...] = jnp.where((lane < HEAD_DIM) == (j == 0), k_pair, jnp.zeros_like(k_pair))
        for qi in range(nb):
            r0, lk = qi * tb_, (qi + 1) * tb_
            for j in range(2):
                sl = slice(j * HEAD_DIM, (j + 1) * HEAD_DIM)
                s = _dot(q_ref[r0:lk, :], k_scrs[j][0:lk, :], NT_DIMS) * scale2 + c_ref[j:j + 1, 0:lk] * LOG2E
                tail = jnp.where(causal, s[:, r0:lk], NEG)
                s = tail if qi == 0 else jnp.concatenate([s[:, 0:r0], tail], axis=1)
                m = jnp.max(s, axis=1, keepdims=True)
                p = jnp.exp2(s - m)
                acc = _dot(p.astype(BF16), v_scrs[j][0:lk, :])
                l = acc[:, HEAD_DIM:HEAD_DIM + 1]
                o_ref[r0:lk, sl] = (acc[:, 0:HEAD_DIM] / l).astype(BF16)
                lse_ref[r0:lk, sl] = jnp.broadcast_to(m + jnp.log(l) * LOG2E, (tb_, HEAD_DIM))

    blk = lambda off: pl.BlockSpec((t, 128), lambda b, hp: (b, off + hp))
    return pl.pallas_call(
        body, name="fox_attn_fwd", grid=(bl, 8),
        in_specs=[blk(0), blk(8), blk(16), pl.BlockSpec((None, None, 8, t), lambda b, hp: (b, hp, 0, 0))],
        out_specs=[blk(0), blk(0)],
        out_shape=[jax.ShapeDtypeStruct((n, ATT_WIDTH), BF16), jax.ShapeDtypeStruct((n, ATT_WIDTH), F32)],
        scratch_shapes=[pltpu.VMEM((t, 128), BF16)] * 4,
        compiler_params=_cparams(("parallel", "parallel")),
    )(qkv, qkv, qkv, negc)


def _attn_bwd(qkv, do, o, lse, negc, after, bl, t):
    n = bl * t
    tb_ = min(t, ATT_BWD_BLOCK)
    nb = t // tb_
    scale = 1.0 / math.sqrt(HEAD_DIM)
    scale2 = LOG2E * scale

    def body(q_ref, k_ref, v_ref, do_ref, o_ref, lse_ref, c_ref, after_ref, dq_ref, dk_ref, dv_ref, dc_ref,
             dq0_scr, delta_scr, dq1_scr, qt0_scr, qt1_scr, dot_scr, dkt0_scr, dkt1_scr, dvt_scr):
        row = lax.broadcasted_iota(jnp.int32, (tb_, tb_), 0)
        col = lax.broadcasted_iota(jnp.int32, (tb_, tb_), 1)
        causal = row >= col
        lane = lax.broadcasted_iota(jnp.int32, (1, 128), 1)
        dq_scrs = (dq0_scr, dq1_scr)
        qt_scrs = (qt0_scr, qt1_scr)
        dkt_scrs = (dkt0_scr, dkt1_scr)
        dq0_scr[...] = jnp.zeros_like(dq0_scr)
        dq1_scr[...] = jnp.zeros_like(dq1_scr)
        dc_ref[...] = jnp.zeros_like(dc_ref)
        q_t = jnp.transpose(q_ref[...].astype(F32))
        ones_row = jnp.where(lax.broadcasted_iota(jnp.int32, (8, t), 0) == 0, 1.0, 0.0)
        for j in range(2):
            qt_scrs[j][...] = jnp.concatenate(
                [q_t[j * HEAD_DIM:(j + 1) * HEAD_DIM, :], ones_row, jnp.zeros((HEAD_DIM - 8, t), F32)],
                axis=0).astype(BF16)
        dot_scr[...] = jnp.transpose(do_ref[...].astype(F32)).astype(BF16)
        prod = do_ref[...].astype(F32) * o_ref[...].astype(F32)
        for j in range(2):
            sl = slice(j * HEAD_DIM, (j + 1) * HEAD_DIM)
            delta_scr[:, sl] = jnp.broadcast_to(jnp.sum(prod[:, sl], axis=1, keepdims=True), (t, HEAD_DIM))
        for kj in range(nb):
            r0, r1 = kj * tb_, (kj + 1) * tb_
            k_blk = k_ref[r0:r1, :]
            v_blk = v_ref[r0:r1, :]
            k_pair = k_blk.astype(F32)
            for j in range(2):
                sl = slice(j * HEAD_DIM, (j + 1) * HEAD_DIM)
                one = slice(j * HEAD_DIM, j * HEAD_DIM + 1)
                own = (lane < HEAD_DIM) == (j == 0)
                k_head = k_pair if j == 0 else pltpu.roll(k_pair, HEAD_DIM, 1)
                k_ones = jnp.where(lane < HEAD_DIM, k_head, jnp.where(lane == HEAD_DIM, 1.0, 0.0)).astype(BF16)
                s = (_dot(q_ref[r0:t, :], jnp.where(own, k_blk, jnp.zeros_like(k_blk)), NT_DIMS) * scale2
                     + c_ref[j:j + 1, r0:r1] * LOG2E)
                head = jnp.where(causal, s[0:tb_, :], NEG)
                s = head if kj == nb - 1 else jnp.concatenate([head, s[tb_:, :]], axis=0)
                p = jnp.exp2(s - lse_ref[r0:t, one])
                dp = _dot(do_ref[r0:t, :], jnp.where(own, v_blk, jnp.zeros_like(v_blk)), NT_DIMS)
                ds = p * (dp - delta_scr[r0:t, one])
                dsb = ds.astype(BF16)
                dvt_scr[sl, r0:r1] = _dot(dot_scr[sl, r0:t], p.astype(BF16))
                dkt_scrs[j][:, r0:r1] = _dot(qt_scrs[j][:, r0:t], dsb)
                dq_scrs[j][r0:t, :] += _dot(dsb, k_ones)
        dv_ref[...] = jnp.transpose(dvt_scr[...]).astype(BF16)
        for j in range(2):
            sl = slice(j * HEAD_DIM, (j + 1) * HEAD_DIM)
            acc = dq_scrs[j][...]
            dkt = dkt_scrs[j][...]
            dq_ref[:, sl] = (acc[:, 0:HEAD_DIM] * scale).astype(BF16)
            dk_ref[:, sl] = (jnp.transpose(dkt)[:, 0:HEAD_DIM] * scale).astype(BF16)
            dc_ref[j:j + 1, :] = jnp.transpose(acc)[HEAD_DIM:HEAD_DIM + 1, :] - dkt[HEAD_DIM:HEAD_DIM + 1, :]

    blk = lambda off: pl.BlockSpec((t, 128), lambda b, hp: (b, off + hp))
    cblk = pl.BlockSpec((None, None, 8, t), lambda b, hp: (b, hp, 0, 0))
    return pl.pallas_call(
        body, name="fox_attn_bwd", grid=(bl, 8),
        in_specs=[blk(0), blk(8), blk(16), blk(0), blk(0), blk(0), cblk, ANY],
        out_specs=[blk(0), blk(0), blk(0), cblk],
        out_shape=[jax.ShapeDtypeStruct((n, ATT_WIDTH), BF16)] * 3 + [jax.ShapeDtypeStruct((bl, 8, 8, t), F32)],
        scratch_shapes=[pltpu.VMEM((t, 128), F32), pltpu.VMEM((t, 128), F32), pltpu.VMEM((t, 128), F32),
                        pltpu.VMEM((128, t), BF16), pltpu.VMEM((128, t), BF16), pltpu.VMEM((128, t), BF16),
                        pltpu.VMEM((128, t), F32), pltpu.VMEM((128, t), F32), pltpu.VMEM((128, t), F32)],
        compiler_params=_cparams(("parallel", "parallel")),
    )(qkv, qkv, qkv, do, o, lse, negc, after)


def _adamw(w, g, m, v, *, name):
    lead = w.ndim == 3
    r, c = w.shape[-2:]
    tr = _pick(r, (256, IN_SHARD // 3, 128, 64, 32, 16, 8))
    bc1 = 1.0 - ADAM_B1 ** ADAM_STEP
    bc2 = 1.0 - ADAM_B2 ** ADAM_STEP

    def body(w_ref, g_ref, m_ref, v_ref, d_ref, nm_ref, nv_ref):
        gv = g_ref[...]
        mn = ADAM_B1 * m_ref[...] + (1.0 - ADAM_B1) * gv
        vn = ADAM_B2 * v_ref[...] + (1.0 - ADAM_B2) * (gv * gv)
        m_hat = mn / bc1
        v_hat = vn / bc2
        d_ref[...] = -ADAM_LR * (m_hat / (jnp.sqrt(v_hat) + ADAM_EPS) + ADAM_WD * w_ref[...])
        nm_ref[...] = mn
        nv_ref[...] = vn

    flat = pl.BlockSpec((tr, c), lambda i: (i, 0))
    blk = pl.BlockSpec((None, tr, c), lambda i: (0, i, 0)) if lead else flat
    return pl.pallas_call(
        body, name=name, grid=(r // tr,), in_specs=[blk, flat, blk, blk], out_specs=[blk] * 3,
        out_shape=[jax.ShapeDtypeStruct(w.shape, F32)] * 3,
        compiler_params=_cparams(("parallel",)),
    )(w, g, m, v)


def _sum_leading(parts, *, name, out_dtype=F32):
    k, r, c = parts.shape
    tr = _pick(r, (512, 256, 128, 96, 64, 32, 16, 8))

    def body(p_ref, o_ref):
        acc = p_ref[0].astype(F32)
        for i in range(1, k):
            acc = acc + p_ref[i].astype(F32)
        o_ref[...] = acc.astype(out_dtype)

    return pl.pallas_call(
        body, name=name, grid=(r // tr,),
        in_specs=[pl.BlockSpec((k, tr, c), lambda i: (0, i, 0))],
        out_specs=pl.BlockSpec((tr, c), lambda i: (i, 0)),
        out_shape=jax.ShapeDtypeStruct((r, c), out_dtype),
        compiler_params=_cparams(("parallel",)),
    )(parts)


def _add_my_half(g, b, *, name):
    k, r, c = b.shape
    tr = _pick(r, (512, 256, 128))
    nrt = r // tr

    def body(lo_ref, hi_ref, b_ref, o_ref):
        mine = jnp.where(lax.axis_index("c") == 0, lo_ref[...], hi_ref[...])
        o_ref[...] = (mine.astype(F32) + b_ref[...].astype(F32)).astype(BF16)

    blk = pl.BlockSpec((None, tr, c), lambda j, i: (j, i, 0))
    return pl.pallas_call(
        body, name=name, grid=(k, nrt),
        in_specs=[blk, pl.BlockSpec((None, tr, c), lambda j, i: (j, i + nrt, 0)), blk], out_specs=blk,
        out_shape=jax.ShapeDtypeStruct((k, r, c), BF16),
        compiler_params=_cparams(("parallel", "parallel")),
    )(g, g, b)


ANY = pl.BlockSpec(memory_space=pl.ANY)


def _chip_peers(x, y):
    return [(1 - x, y, 2 * (1 - x) + y), (x, 1 - y, 2 * x + 1 - y), (1 - x, 1 - y, 2 * (1 - x) + 1 - y)]


def _gather_weights(blob, *, name):
    rows, cols = blob.shape
    half_rows = rows // 2

    def body(b_ref, o_ref, send_sems, recv_sems):
        x, y, c = lax.axis_index("x"), lax.axis_index("y"), lax.axis_index("c")
        me = 2 * x + y
        sibling = (x, y, 1 - c)
        peers = _chip_peers(x, y)

        def half(chip, hc):
            return o_ref.at[chip, pl.ds(hc * half_rows, half_rows), :]

        def copy(k, src, chip, hc, to):
            return pltpu.make_async_remote_copy(src_ref=src, dst_ref=half(chip, hc), send_sem=send_sems.at[k],
                                                recv_sem=recv_sems.at[k], device_id=to, device_id_type=MESH)

        my_half = b_ref.at[pl.ds(c * half_rows, half_rows), :]
        first = [copy(k, my_half, me, c, (px, py, c)) for k, (px, py, _) in enumerate(peers)]
        own = pltpu.make_async_remote_copy(src_ref=b_ref, dst_ref=o_ref.at[me], send_sem=send_sems.at[6],
                                           recv_sem=recv_sems.at[6], device_id=sibling, device_id_type=MESH)
        for cp in first + [own]:
            cp.start()
        passed = [copy(3 + k, half(pc, c), pc, c, sibling) for k, (_, _, pc) in enumerate(peers)]
        for k, (px, py, pc) in enumerate(peers):
            copy(k, my_half, pc, c, (px, py, c)).wait_recv()
            passed[k].start()
        for k, (_, _, pc) in enumerate(peers):
            copy(3 + k, half(pc, 1 - c), pc, 1 - c, sibling).wait_recv()
        own.wait_recv()
        for cp in first + passed + [own]:
            cp.wait_send()

    return pl.pallas_call(
        body, name=name, in_specs=[ANY], out_specs=ANY,
        out_shape=jax.ShapeDtypeStruct((N_CHIPS, rows, cols), BF16),
        scratch_shapes=[pltpu.SemaphoreType.DMA((7,)), pltpu.SemaphoreType.DMA((7,))],
    )(blob)


def _swap_halves(g, *, name):
    _, rows, cols = g.shape
    half_rows = rows // 2

    def body(g_ref, o_ref, send_sem, recv_sem):
        x, y, c = lax.axis_index("x"), lax.axis_index("y"), lax.axis_index("c")
        cp = pltpu.make_async_remote_copy(
            src_ref=g_ref.at[:, pl.ds((1 - c) * half_rows, half_rows), :], dst_ref=o_ref,
            send_sem=send_sem, recv_sem=recv_sem, device_id=(x, y, 1 - c), device_id_type=MESH)
        cp.start()
        cp.wait()

    return pl.pallas_call(
        body, name=name, in_specs=[ANY], out_specs=ANY,
        out_shape=jax.ShapeDtypeStruct((N_CHIPS, half_rows, cols), BF16),
        scratch_shapes=[pltpu.SemaphoreType.DMA, pltpu.SemaphoreType.DMA],
    )(g)


HBM_SPEC = pl.BlockSpec(memory_space=pltpu.HBM)
SEM_SPEC = pl.BlockSpec(memory_space=pltpu.SEMAPHORE)
SPLIT_EFFECT = pltpu.SideEffectType.DATAFLOW_SIDE_EFFECTING


def _gather_peers_copies(b_ref, land_ref, send_sems, recv_sems, sending):
    x, y, c = lax.axis_index("x"), lax.axis_index("y"), lax.axis_index("c")
    me = 2 * x + y
    half_rows = b_ref.shape[0] // 2
    src = b_ref.at[pl.ds(c * half_rows, half_rows), :]
    return [pltpu.make_async_remote_copy(
        src_ref=src, dst_ref=land_ref.at[me if sending else pc, pl.ds(c * half_rows, half_rows), :],
        send_sem=send_sems.at[k], recv_sem=recv_sems.at[k], device_id=(px, py, c), device_id_type=MESH)
        for k, (px, py, pc) in enumerate(_chip_peers(x, y))]


def _gather_start(blob, after, *, name):
    shape = (N_CHIPS,) + blob.shape

    def body(b_ref, land_ref, after_ref, send_sems, recv_sems, b_thru, land_thru, token):
        for cp in _gather_peers_copies(b_ref, land_ref, send_sems, recv_sems, True):
            cp.start()
        token[...] = jnp.zeros_like(token)

    return pl.pallas_call(
        body, name=name,
        out_shape=(pltpu.SemaphoreType.DMA((3,)), pltpu.SemaphoreType.DMA((3,)), pltpu.HBM(blob.shape, blob.dtype),
                   pltpu.HBM(shape, blob.dtype), jax.ShapeDtypeStruct((8, 128), F32)),
        in_specs=(HBM_SPEC, HBM_SPEC, ANY),
        out_specs=(SEM_SPEC, SEM_SPEC, HBM_SPEC, HBM_SPEC, pl.BlockSpec(memory_space=pltpu.VMEM)),
        input_output_aliases={0: 2, 1: 3},
        compiler_params=pltpu.CompilerParams(has_side_effects=SPLIT_EFFECT),
    )(pltpu.with_memory_space_constraint(blob, pltpu.HBM),
      pltpu.with_memory_space_constraint(lax.empty(shape, blob.dtype), pltpu.HBM), after)


def _gather_wait(send_sems, recv_sems, b_thru, land_thru, after, *, name):
    def body(b_ref, land_ref, send_sems, recv_sems, after_ref, b_dead, got_ref):
        for cp in _gather_peers_copies(b_ref, land_ref, send_sems, recv_sems, False):
            cp.wait_send()
            cp.wait_recv()

    return pl.pallas_call(
        body, name=name,
        out_shape=(pltpu.HBM(b_thru.shape, b_thru.dtype), pltpu.HBM(land_thru.shape, land_thru.dtype)),
        in_specs=(HBM_SPEC, HBM_SPEC, SEM_SPEC, SEM_SPEC, ANY), out_specs=(HBM_SPEC, HBM_SPEC),
        input_output_aliases={0: 0, 1: 1},
        compiler_params=pltpu.CompilerParams(has_side_effects=SPLIT_EFFECT),
    )(b_thru, land_thru, send_sems, recv_sems, after)


def _gather_forward(land, blob, *, name):
    half_rows = land.shape[1] // 2

    def body(l_ref, b_ref, o_ref, send_sems, recv_sems):
        x, y, c = lax.axis_index("x"), lax.axis_index("y"), lax.axis_index("c")
        me = 2 * x + y
        sibling = (x, y, 1 - c)
        cps = []
        for k, (_, _, pc) in enumerate(_chip_peers(x, y)):
            mine = pl.ds(c * half_rows, half_rows)
            cps.append(pltpu.make_async_remote_copy(
                src_ref=l_ref.at[pc, mine, :], dst_ref=o_ref.at[pc, mine, :], send_sem=send_sems.at[k],
                recv_sem=recv_sems.at[k], device_id=sibling, device_id_type=MESH))
        cps.append(pltpu.make_async_remote_copy(src_ref=b_ref, dst_ref=o_ref.at[me], send_sem=send_sems.at[3],
                                                recv_sem=recv_sems.at[3], device_id=sibling, device_id_type=MESH))
        for cp in cps:
            cp.start()
        for k, (_, _, pc) in enumerate(_chip_peers(x, y)):
            theirs = pl.ds((1 - c) * half_rows, half_rows)
            pltpu.make_async_remote_copy(
                src_ref=l_ref.at[pc, theirs, :], dst_ref=o_ref.at[pc, theirs, :], send_sem=send_sems.at[k],
                recv_sem=recv_sems.at[k], device_id=sibling, device_id_type=MESH).wait_recv()
        cps[3].wait_recv()
        for cp in cps:
            cp.wait_send()

    return pl.pallas_call(
        body, name=name, in_specs=[ANY, ANY], out_specs=ANY, input_output_aliases={0: 0},
        out_shape=jax.ShapeDtypeStruct(land.shape, land.dtype),
        scratch_shapes=[pltpu.SemaphoreType.DMA((4,)), pltpu.SemaphoreType.DMA((4,))],
    )(land, blob)


def _exchange_peers_copies(p_ref, land_ref, send_sems, recv_sems, sending):
    x, y, c = lax.axis_index("x"), lax.axis_index("y"), lax.axis_index("c")
    me = 2 * x + y
    return [pltpu.make_async_remote_copy(src_ref=p_ref.at[pc], dst_ref=land_ref.at[me if sending else pc],
                                         send_sem=send_sems.at[k], recv_sem=recv_sems.at[k],
                                         device_id=(px, py, c), device_id_type=MESH)
            for k, (px, py, pc) in enumerate(_chip_peers(x, y))]


def _exchange_start(p, *, name):
    def body(p_ref, land_ref, send_sems, recv_sems, p_thru, land_thru, token):
        for cp in _exchange_peers_copies(p_ref, land_ref, send_sems, recv_sems, True):
            cp.start()
        token[...] = jnp.zeros_like(token)

    return pl.pallas_call(
        body, name=name,
        out_shape=(pltpu.SemaphoreType.DMA((3,)), pltpu.SemaphoreType.DMA((3,)), pltpu.HBM(p.shape, p.dtype),
                   pltpu.HBM(p.shape, p.dtype), jax.ShapeDtypeStruct((8, 128), F32)),
        in_specs=(HBM_SPEC, HBM_SPEC),
        out_specs=(SEM_SPEC, SEM_SPEC, HBM_SPEC, HBM_SPEC, pl.BlockSpec(memory_space=pltpu.VMEM)),
        input_output_aliases={0: 2, 1: 3},
        compiler_params=pltpu.CompilerParams(has_side_effects=SPLIT_EFFECT),
    )(pltpu.with_memory_space_constraint(p, pltpu.HBM),
      pltpu.with_memory_space_constraint(lax.empty(p.shape, p.dtype), pltpu.HBM))


def _exchange_wait(send_sems, recv_sems, p_thru, land_thru, after, *, name):
    def body(p_ref, land_ref, send_sems, recv_sems, after_ref, p_dead, got_ref):
        for cp in _exchange_peers_copies(p_ref, land_ref, send_sems, recv_sems, False):
            cp.wait_send()
            cp.wait_recv()

    return pl.pallas_call(
        body, name=name,
        out_shape=(pltpu.HBM(p_thru.shape, p_thru.dtype), pltpu.HBM(p_thru.shape, p_thru.dtype)),
        in_specs=(HBM_SPEC, HBM_SPEC, SEM_SPEC, SEM_SPEC, ANY), out_specs=(HBM_SPEC, HBM_SPEC),
        input_output_aliases={0: 0, 1: 1},
        compiler_params=pltpu.CompilerParams(has_side_effects=SPLIT_EFFECT),
    )(p_thru, land_thru, send_sems, recv_sems, after)


def _sum_parts(parts, own, *, name):
    k, r, c = parts.shape
    tr = _pick(r, (512, 256, 128))

    def body(p_ref, own_ref, o_ref):
        me = 2 * lax.axis_index("x") + lax.axis_index("y")
        acc = jnp.zeros((tr, c), F32)
        for i in range(k):
            acc = acc + jnp.where(me == i, own_ref[i], p_ref[i]).astype(F32)
        o_ref[...] = acc

    blk = pl.BlockSpec((k, tr, c), lambda i: (0, i, 0))
    return pl.pallas_call(
        body, name=name, grid=(r // tr,), in_specs=[blk, blk],
        out_specs=pl.BlockSpec((tr, c), lambda i: (i, 0)),
        out_shape=jax.ShapeDtypeStruct((r, c), F32),
        compiler_params=_cparams(("parallel",)),
    )(parts, own)


def _from_sibling(a, *, name):
    def body(a_ref, o_ref, send_sem, recv_sem):
        x, y, c = lax.axis_index("x"), lax.axis_index("y"), lax.axis_index("c")
        cp = pltpu.make_async_remote_copy(src_ref=a_ref, dst_ref=o_ref, send_sem=send_sem, recv_sem=recv_sem,
                                          device_id=(x, y, 1 - c), device_id_type=MESH)
        cp.start()
        cp.wait()

    return pl.pallas_call(
        body, name=name, in_specs=[ANY], out_specs=ANY,
        out_shape=jax.ShapeDtypeStruct(a.shape, F32),
        scratch_shapes=[pltpu.SemaphoreType.DMA, pltpu.SemaphoreType.DMA],
    )(a)


def _join_halves(gh, *, name):
    other = _from_sibling(gh, name=name)
    south = lax.axis_index("c") == 0
    return jnp.concatenate([jnp.where(south, gh, other), jnp.where(south, other, gh)], axis=0)


def _add_f32(a, b, *, name):
    r, c = a.shape
    tr = _pick(r, (512, 256, 128))

    def body(a_ref, b_ref, o_ref):
        o_ref[...] = a_ref[...] + b_ref[...]

    blk = pl.BlockSpec((tr, c), lambda i: (i, 0))
    return pl.pallas_call(
        body, name=name, grid=(r // tr,), in_specs=[blk, blk], out_specs=blk,
        out_shape=jax.ShapeDtypeStruct((r, c), F32),
        compiler_params=_cparams(("parallel",)),
    )(a, b)


def _gather_small(s, *, name):
    rows = s.shape[0]

    def body(s_ref, o_ref, send_sems, recv_sems, local_sem):
        x, y, c = lax.axis_index("x"), lax.axis_index("y"), lax.axis_index("c")
        me = 4 * x + 2 * y + c
        mine = pltpu.make_async_copy(s_ref, o_ref.at[me], local_sem)
        mine.start()
        peers = []
        for k in range(1, 8):
            peers.append((1 - x if k & 4 else x, 1 - y if k & 2 else y, 1 - c if k & 1 else c))
        cps = [pltpu.make_async_remote_copy(src_ref=s_ref, dst_ref=o_ref.at[me], send_sem=send_sems.at[k],
                                            recv_sem=recv_sems.at[k], device_id=p, device_id_type=MESH)
               for k, p in enumerate(peers)]
        for cp in cps:
            cp.start()
        for k, (px, py, pc) in enumerate(peers):
            pltpu.make_async_remote_copy(src_ref=s_ref, dst_ref=o_ref.at[4 * px + 2 * py + pc],
                                         send_sem=send_sems.at[k], recv_sem=recv_sems.at[k],
                                         device_id=(px, py, pc), device_id_type=MESH).wait_recv()
        for cp in cps:
            cp.wait_send()
        mine.wait()

    return pl.pallas_call(
        body, name=name, in_specs=[ANY], out_specs=ANY,
        out_shape=jax.ShapeDtypeStruct((8, rows, 128), F32),
        scratch_shapes=[pltpu.SemaphoreType.DMA((7,)), pltpu.SemaphoreType.DMA((7,)), pltpu.SemaphoreType.DMA],
    )(s)


IN_SHARD = IN_WIDTH // N_CHIPS
IN_SHARD_PAD = 1536
UP_ROWS, DOWN_ROWS, OUT_ROWS = 1024, 1024, 512
REST_ROWS = UP_ROWS + DOWN_ROWS + OUT_ROWS


def _pack_in(w_in_s):
    return jnp.pad(w_in_s, ((0, 0), (0, IN_SHARD_PAD - IN_SHARD))).astype(BF16)


def _pack_rest(w_out_s, w_up_s, w_down_s):
    return jnp.concatenate([w_up_s, w_down_s, w_out_s], axis=0).astype(BF16)


def _unpack_rest(blob):
    return (blob[UP_ROWS + DOWN_ROWS:], blob[0:UP_ROWS], blob[UP_ROWS:UP_ROWS + DOWN_ROWS])


def _full_w_in(g_in):
    return jnp.concatenate([g_in[j, :, :IN_SHARD] for j in range(N_CHIPS)], axis=1)


def _full_rest(g_rest):
    parts = [_unpack_rest(g_rest[j]) for j in range(N_CHIPS)]
    w_out = jnp.concatenate([p[0] for p in parts], axis=0)
    w_up = jnp.concatenate([p[1] for p in parts], axis=1)
    w_down = jnp.concatenate([p[2] for p in parts], axis=0)
    return w_out, w_up, w_down


def _split_w_in(w_in):
    z_xbc = w_in[:, 0:2560]
    dt = w_in[:, 2560:2576]
    qkv = w_in[:, 2576:5648]
    f = w_in[:, 5648:5664]
    pad = jnp.zeros((w_in.shape[0], PA_WIDTH - 2592), w_in.dtype)
    return jnp.concatenate([z_xbc, dt, f, pad], axis=1), qkv


def _merge_w_in(d_a, d_qkv):
    return jnp.concatenate([d_a[:, 0:2560], d_a[:, 2560:2576], *d_qkv, d_a[:, 2576:2592]], axis=1)


def _local_step(x3, target3, w_in, rest_weights, norm_mix_w, conv_w, conv_b, dt_bias, a_log, d_skip,
                ssd_norm_w, f_bias, norm_mlp_w, norm_final_w, first_after=None, early_grads=None, late_grads=None):
    bl, t, d = x3.shape
    n = bl * t
    x = x3.reshape(n, d)
    target = target3.reshape(n, d)
    w_a, w_qkv = _split_w_in(w_in)
    nfw = norm_final_w.reshape(1, d)
    dskip_e = jnp.repeat(d_skip, HEAD_DIM, axis=1)

    r1, r2, kt = min(n, 1024), min(n, 512), min(n, 2048)
    if first_after is None:
        first_after = jnp.zeros((8, 128), F32)
    h0, rstd0, proj_a = _norm_mm(x, norm_mix_w, w_a, first_after, name="norm_mix_proj_a", tm=r2)
    qkv = _mm(h0, w_qkv, name="proj_qkv", tiles=(r2, QKV_WIDTH, D_MODEL), out_dtype=BF16)
    bias128 = jnp.concatenate([dt_bias, f_bias, jnp.zeros((1, 96), F32)], axis=1)
    alog128 = jnp.concatenate([a_log, jnp.zeros((1, 112), F32)], axis=1)
    dt, gate_d, acum, acum_t, negc = _prep(proj_a, bias128, alog128, bl, t)
    xc, dsilu = _conv_fwd(proj_a, conv_w, conv_b, bl, t)
    y_ssd, y_pre, hprev = _ssd_fwd(xc, proj_a, dt, acum, acum_t, dskip_e, ssd_norm_w, bl, t)
    y_att, lse = _attn_fwd(qkv, negc, bl, t)
    w_out, w_up, w_down = rest_weights(y_att)
    wo_s, wo_a = w_out[:SSD_WIDTH], w_out[SSD_WIDTH:]
    h1, h1n, rstd1 = _mm_norm_fwd(y_ssd, wo_s, y_att, wo_a, x, norm_mlp_w, name="out_proj_norm_mlp", tm=r2)
    up = _mm(h1n, w_up, name="mlp_up", tiles=(r1, D_FF, D_MODEL), out_dtype=BF16)
    dh2, dh2b, loss, d_nfw = _mm_final(up, w_down, h1, nfw, target, name="mlp_down_final_norm_loss", tm=r2,
                                       a_act="relu2")

    dup = _mm(dh2b, w_down, name="mlp_down_bwd_act", tiles=(r2, D_FF, D_MODEL), tb=True, epi_up=up, out_dtype=BF16)
    rest_shape = (N_CHIPS, REST_ROWS, D_MODEL)
    gb_rest = _mm(up, dh2b, name="mlp_down_bwd_w", tiles=(DOWN_ROWS, D_MODEL, kt), ta=True, a_act="relu2",
                  out_dtype=BF16, into=(rest_shape, (None, DOWN_ROWS, D_MODEL), lambda i, j, k: (i, 1, 0), None))
    dh1, dh1b, d_nmlp = _mm_norm_bwd([(dup, w_up)], h1, rstd1, norm_mlp_w, dh2, name="mlp_up_bwd_act_norm_mlp",
                                     tm=r2)
    gb_rest = _mm(h1n, dup, name="mlp_up_bwd_w", tiles=(D_MODEL, UP_ROWS, kt), ta=True, out_dtype=BF16,
                  into=(rest_shape, (None, D_MODEL, UP_ROWS), lambda i, j, k: (j, 0, 0), gb_rest))
    dys, do = _mm_two_halves(dh1b, w_out, name="out_proj_bwd_act", tm=r1)
    out_block = (UP_ROWS + DOWN_ROWS) // OUT_ROWS
    for half, (y_half, tag) in enumerate(((y_ssd, "ssd"), (y_att, "att"))):
        gb_rest = _mm(y_half, dh1b, name="out_proj_bwd_w_" + tag, tiles=(2 * OUT_ROWS, D_MODEL, kt), ta=True,
                      out_dtype=BF16, into=(rest_shape, (2, OUT_ROWS, D_MODEL),
                                            functools.partial(lambda i, j, k, h: (h, out_block, 0), h=half),
                                            gb_rest))
    token = jnp.zeros((8, 128), F32) if early_grads is None else early_grads(gb_rest)
    dq, dk, dv, dcb = _attn_bwd(qkv, do, y_att, lse, negc, token, bl, t)
    dc = jnp.pad(dcb[:, :, 0:2, :].transpose(0, 3, 1, 2).reshape(n, 16), ((0, 0), (16, 96)))
    dxc, dpa, ddt_raw, d_snw, d_dsk, d_alog, d_dtb = _ssd_bwd(dys, xc, proj_a, y_pre, hprev, dt, gate_d, acum,
                                                             acum_t, alog128, dskip_e, ssd_norm_w, bl, t)
    dpa, d_conv_w, d_conv_b = _conv_bwd(dxc, dsilu, proj_a, conv_w, dpa, bl, t)
    dproj_a, d_fb = _fpost(dc, gate_d, ddt_raw, dpa, bl, t)
    d_w_a = _mm(h0, dproj_a, name="proj_a_bwd_w", tiles=(1024, 896, kt), ta=True, out_dtype=BF16)
    d_heads = (("q", dq), ("k", dk), ("v", dv))
    d_w_qkv = [_mm(h0, g, name=f"proj_{tag}_bwd_w", tiles=(1024, 1024, kt), ta=True, out_dtype=BF16)
               for tag, g in d_heads]
    d_w_in = _merge_w_in(d_w_a, d_w_qkv)
    late_token = None if late_grads is None else late_grads(d_w_in)
    head_pairs = [(g, w_qkv, j) for j, (_, g) in enumerate(d_heads)]
    dx, _, d_nmix = _mm_norm_bwd([(dproj_a, w_a)] + head_pairs, x, rstd0, norm_mix_w, dh1,
                                 name="proj_bwd_act_norm_mix", tm=min(n, 512), after=late_token)

    grads = dict(norm_mix_w=d_nmix, w_in=d_w_in, conv_w=d_conv_w, conv_b=d_conv_b,
                 dt_bias=d_dtb, a_log=d_alog, d_skip=d_dsk, ssd_norm_w=d_snw, f_bias=d_fb, rest=gb_rest,
                 norm_mlp_w=d_nmlp, norm_final_w=d_nfw)
    return dx.reshape(bl, t, d), loss, grads


SMALL_ORDER = ("norm_mix_w", "conv_w", "conv_b", "dt_bias", "a_log", "d_skip", "ssd_norm_w", "f_bias",
               "norm_mlp_w", "norm_final_w")
SMALL_SIZES = (1024, 4 * CONV_CH, CONV_CH, 16, 16, 16, 1024, 16, 1024, 1024)


def _pack_small(vals, rows):
    flat = jnp.concatenate([v.reshape(-1).astype(F32) for v in vals])
    return jnp.pad(flat, (0, rows * 128 - flat.shape[0])).reshape(rows, 128)


def _unpack_small(packed, sizes):
    flat = packed.reshape(-1)
    out, o = [], 0
    for s in sizes:
        out.append(flat[o:o + s])
        o += s
    return out


def kernel(x, norm_mix_w, w_in, conv_w, conv_b, dt_bias, a_log, d_skip, ssd_norm_w, f_bias, w_out, norm_mlp_w, w_up, w_down, norm_final_w, loss_target, m_norm_mix_w, m_w_in, m_conv_w, m_conv_b, m_dt_bias, m_a_log, m_d_skip, m_ssd_norm_w, m_f_bias, m_w_out, m_norm_mlp_w, m_w_up, m_w_down, m_norm_final_w, v_norm_mix_w, v_w_in, v_conv_w, v_conv_b, v_dt_bias, v_a_log, v_d_skip, v_ssd_norm_w, v_f_bias, v_w_out, v_norm_mlp_w, v_w_up, v_w_down, v_norm_final_w):
    chip = 2 * lax.axis_index("x") + lax.axis_index("y")
    cw = CONV_CH // N_CHIPS

    own_in = _pack_in(w_in[0])
    own_rest = _pack_rest(w_out[0], w_up[0], w_down[0])
    g_in = _gather_weights(own_in, name="gather_w_in")
    w_in_f = _full_w_in(g_in)
    *rest_handles, rest_token = _gather_start(own_rest, g_in, name="gather_start_rest")

    def rest_weights(after):
        _, landed = _gather_wait(*rest_handles, after, name="gather_wait_rest")
        return _full_rest(_gather_forward(landed, own_rest, name="gather_forward_rest"))
    small_all = _gather_small(_pack_small([conv_w[0]], 16), name="gather_conv_w")
    conv_w_f = jnp.concatenate([small_all[2 * j].reshape(-1)[:4 * cw].reshape(4, cw) for j in range(N_CHIPS)], axis=1)

    def chip_partial(gb, tag):
        from_sibling = _swap_halves(gb, name="grad_swap_halves_" + tag)
        return _add_my_half(gb, from_sibling, name="grad_add_sibling_" + tag)

    in_flight = {}

    def early_grads(gb_rest):
        *handles, token = _exchange_start(gb_rest, name="grad_exchange_start_rest")
        in_flight["rest"] = handles
        return token

    def late_grads(d_w_in):
        gb_in = jnp.stack([_pack_in(d_w_in[:, j * IN_SHARD:(j + 1) * IN_SHARD]) for j in range(N_CHIPS)])
        *handles, token = _exchange_start(chip_partial(gb_in, "in"), name="grad_exchange_start_in")
        in_flight["in"] = handles
        return token

    dx, loss_part, g = _local_step(x, loss_target, w_in_f, rest_weights, norm_mix_w, conv_w_f,
                                   conv_b, dt_bias, a_log, d_skip, ssd_norm_w, f_bias, norm_mlp_w, norm_final_w,
                                   first_after=rest_token, early_grads=early_grads, late_grads=late_grads)

    send_sems, recv_sems, part_rest, land_rest = in_flight["rest"]
    part_rest, parts_rest = _exchange_wait(send_sems, recv_sems, part_rest, land_rest, dx,
                                           name="grad_exchange_wait_rest")
    g_rest_core = _sum_parts(parts_rest, part_rest, name="grad_sum_chips_rest")
    g_rest_sibling = _from_sibling(g_rest_core, name="grad_swap_sums_rest")
    g_w_out, g_w_up, g_w_down = _unpack_rest(_add_f32(g_rest_core, g_rest_sibling, name="grad_add_cores_rest"))

    part_in, parts_in = _exchange_wait(*in_flight["in"], dx, name="grad_exchange_wait_in")
    g_in_half = _sum_parts(parts_in, part_in, name="grad_sum_chips_in")
    g_w_in = _join_halves(g_in_half, name="grad_join_halves_in")[:, :IN_SHARD]

    small_vals = [g[k] for k in SMALL_ORDER] + [loss_part[:, 0:1]]
    small_sum = _sum_leading(_gather_small(_pack_small(small_vals, SMALL_ROWS), name="gather_small_grads"), name="small_sum")
    sg = dict(zip(SMALL_ORDER + ("loss",), _unpack_small(small_sum, SMALL_SIZES + (1,))))
    loss = sg["loss"].reshape(())
    g_conv_full = sg["conv_w"].reshape(4, CONV_CH)
    g_conv = lax.dynamic_slice_in_dim(g_conv_full, chip * cw, cw, axis=1)

    grads = dict(norm_mix_w=sg["norm_mix_w"].reshape(1, -1), w_in=g_w_in[None], conv_w=g_conv[None],
                 conv_b=sg["conv_b"].reshape(1, -1), dt_bias=sg["dt_bias"].reshape(1, -1),
                 a_log=sg["a_log"].reshape(1, -1), d_skip=sg["d_skip"].reshape(1, -1),
                 ssd_norm_w=sg["ssd_norm_w"].reshape(1, -1), f_bias=sg["f_bias"].reshape(1, -1), w_out=g_w_out[None],
                 norm_mlp_w=sg["norm_mlp_w"].reshape(1, -1), w_up=g_w_up[None], w_down=g_w_down[None],
                 norm_final_w=sg["norm_final_w"])
    weights = dict(norm_mix_w=norm_mix_w, w_in=w_in, conv_w=conv_w, conv_b=conv_b, dt_bias=dt_bias, a_log=a_log,
                   d_skip=d_skip, ssd_norm_w=ssd_norm_w, f_bias=f_bias, w_out=w_out, norm_mlp_w=norm_mlp_w,
                   w_up=w_up, w_down=w_down, norm_final_w=norm_final_w)
    ms = dict(norm_mix_w=m_norm_mix_w, w_in=m_w_in, conv_w=m_conv_w, conv_b=m_conv_b, dt_bias=m_dt_bias,
              a_log=m_a_log, d_skip=m_d_skip, ssd_norm_w=m_ssd_norm_w, f_bias=m_f_bias, w_out=m_w_out,
              norm_mlp_w=m_norm_mlp_w, w_up=m_w_up, w_down=m_w_down, norm_final_w=m_norm_final_w)
    vs = dict(norm_mix_w=v_norm_mix_w, w_in=v_w_in, conv_w=v_conv_w, conv_b=v_conv_b, dt_bias=v_dt_bias,
              a_log=v_a_log, d_skip=v_d_skip, ssd_norm_w=v_ssd_norm_w, f_bias=v_f_bias, w_out=v_w_out,
              norm_mlp_w=v_norm_mlp_w, w_up=v_w_up, w_down=v_w_down, norm_final_w=v_norm_final_w)
    names = list(weights)
    big = ("w_in", "w_out", "w_up", "w_down")
    delta, new_m, new_v = {}, {}, {}
    for k, g2 in zip(big[1:], (g_w_out, g_w_up, g_w_down)):
        delta[k], new_m[k], new_v[k] = _adamw(weights[k], g2, ms[k], vs[k], name="adamw_" + k)
    g_in_t = g_w_in.T
    outs_t = _adamw(w_in[0].T, g_in_t, m_w_in[0].T, v_w_in[0].T, name="adamw_w_in")
    delta["w_in"], new_m["w_in"], new_v["w_in"] = [o.T[None] for o in outs_t]
    grads["w_in"] = g_in_t.T[None]
    smalls = [k for k in names if k not in big]
    sizes = [math.prod(weights[k].shape) for k in smalls]
    rows = -(-sum(sizes) // 1024) * 8
    packs = [_pack_small([d[k] for k in smalls], rows) for d in (weights, grads, ms, vs)]
    outs = _adamw(*packs, name="adamw_small")
    for o, dst in zip(outs, (delta, new_m, new_v)):
        for k, val in zip(smalls, _unpack_small(o, sizes)):
            dst[k] = val.reshape(weights[k].shape)
    return (loss, dx, *[grads[k] for k in names], *[delta[k] for k in names], *[new_m[k] for k in names],
            *[new_v[k] for k in names])
```
